```python
import math
import jax, jax.numpy as jnp
from jax import lax
import numpy as np

D_MODEL = 1024
BATCH = 16
SEQ = 256
DEPTH = 2
DEC_BATCH = 2
DEC_SEQ = 1024
PAST_LEN = 512

N_MIXERS = 2
N_HGRN_LAYERS = (DEPTH + N_MIXERS - 1) // N_MIXERS
N_ATTN_LAYERS = DEPTH // N_MIXERS
GRID_W = 64
HG_DK = 128
HG_HEADS = D_MODEL // HG_DK
HG_DV = D_MODEL // HG_HEADS
D_REC = HG_HEADS * HG_DK
CHUNK = 16
ATT_HEAD_DIM = 64
ATT_Q_HEADS = D_MODEL // ATT_HEAD_DIM
ATT_KV_HEADS = ATT_Q_HEADS // 4
ATT_GROUP = ATT_Q_HEADS // ATT_KV_HEADS
WINDOW = 128
BLOCK = 128
ROPE_HALF = ATT_HEAD_DIM // 2
ROPE_BASE = 10000.0
N_GROUPS = 4
EXPERTS_PER_GROUP = 8
N_EXPERTS = N_GROUPS * EXPERTS_PER_GROUP
TOP_K = 2
D_EXPERT = D_MODEL // 4
EPS = 1e-6

kernel_name = 'hybrid_hgrn2_swa_hmoe_diffusion_step'


def _rmsnorm(x, w):
    xf = x.astype(jnp.float32)
    y = xf * lax.rsqrt(jnp.mean(xf * xf, axis=-1, keepdims=True) + EPS)
    return (y * w.astype(jnp.float32)).astype(x.dtype)


def _modulation(cond, w, b):
    m = (jax.nn.silu(cond) @ w + b)[:, None, :]
    return jnp.split(m, 6, axis=-1)


def _modulate(h, shift, scale):
    return h * (1 + scale) + shift


def _chunk_gla(q, k, v, logf, s0):
    B, T, H, _ = q.shape
    dv = v.shape[-1]
    n = T // CHUNK

    def blk(a):
        return a.astype(jnp.float32).reshape(B, n, CHUNK, H, a.shape[-1]).transpose(1, 0, 3, 2, 4)

    qc, kc, vc, gc = blk(q), blk(k), blk(v), blk(logf)
    b = jnp.cumsum(gc, axis=-2)
    b_last = b[..., -1:, :]
    lower = jnp.tril(jnp.ones((CHUNK, CHUNK), bool))
    diff = b[..., :, None, :] - b[..., None, :, :]
    decay = jnp.exp(jnp.where(lower[..., None], diff, -jnp.inf))
    scores = jnp.einsum('nbhtd,nbhtsd,nbhsd->nbhts', qc, decay, kc)
    intra = jnp.einsum('nbhts,nbhsv->nbhtv', scores, vc)
    q_dec = qc * jnp.exp(b)
    k_dec = kc * jnp.exp(b_last - b)
    chunk_decay = jnp.exp(b_last[..., 0, :])

    def step(S, xs):
        qd, kd, vv, cd = xs
        inter = jnp.einsum('bhtd,bhdv->bhtv', qd, S)
        S = cd[..., None] * S + jnp.einsum('bhsd,bhsv->bhdv', kd, vv)
        return S, inter

    S_T, inter = lax.scan(step, s0.astype(jnp.float32), (q_dec, k_dec, vc, chunk_decay))
    o = (intra + inter).transpose(1, 0, 3, 2, 4).reshape(B, T, H, dv)
    return o, S_T


def _hgrn2(h, w_in, onorm, w_out, lb, s0_f, s0_b):
    B, T, _ = h.shape
    q, inp, zf, zb, g = jnp.split(h @ w_in, 5, axis=-1)

    def heads(a):
        return a.reshape(B, T, HG_HEADS, -1)

    def gates(z, lb_dir):
        f = lb_dir + (1 - lb_dir) * jax.nn.sigmoid(z.astype(jnp.float32))
        return heads(jnp.log(f)), heads(1 - f)

    logf_f, k_f = gates(zf, lb[0])
    logf_b, k_b = gates(zb, lb[1])
    q, inp = heads(q), heads(inp)
    o_f, S_f = _chunk_gla(q, k_f, inp, logf_f, s0_f)
    o_b, S_b = _chunk_gla(q[:, ::-1], k_b[:, ::-1], inp[:, ::-1], logf_b[:, ::-1], s0_b)
    o = o_f + o_b[:, ::-1]
    o = _rmsnorm(o, onorm) * jax.nn.silu(heads(g).astype(jnp.float32))
    out = o.reshape(B, T, D_REC).astype(h.dtype) @ w_out
    return out, S_f.astype(h.dtype), S_b.astype(h.dtype)


def _attn_qkv(h, w_in):
    B, T, _ = h.shape
    proj = h @ w_in
    nq = ATT_Q_HEADS * ATT_HEAD_DIM
    nk = ATT_KV_HEADS * ATT_HEAD_DIM
    q = proj[..., :nq].reshape(B, T, ATT_KV_HEADS, ATT_GROUP, ATT_HEAD_DIM)
    k = proj[..., nq:nq + nk].reshape(B, T, ATT_KV_HEADS, ATT_HEAD_DIM)
    v = proj[..., nq + nk:].reshape(B, T, ATT_KV_HEADS, ATT_HEAD_DIM)
    return q, k, v


def _rope_half(x, ang):
    x1, x2 = jnp.split(x, 2, axis=-1)
    cos = jnp.cos(ang).astype(x.dtype)
    sin = jnp.sin(ang).astype(x.dtype)
    return jnp.concatenate([x1 * cos - x2 * sin, x1 * sin + x2 * cos], axis=-1)


def _axial_rope(x):
    T = x.shape[1]
    rows = T // GRID_W
    t_row = jnp.repeat(jnp.arange(rows), GRID_W).astype(jnp.float32)
    t_col = jnp.tile(jnp.arange(GRID_W), rows).astype(jnp.float32)
    inv = ROPE_BASE ** (-jnp.arange(0, ROPE_HALF, 2, dtype=jnp.float32) / ROPE_HALF)
    shape = (T,) + (1,) * (x.ndim - 3) + (ROPE_HALF // 2,)
    a_row = (t_row[:, None] * inv).reshape(shape)
    a_col = (t_col[:, None] * inv).reshape(shape)
    return jnp.concatenate([_rope_half(x[..., :ROPE_HALF], a_row),
                            _rope_half(x[..., ROPE_HALF:], a_col)], axis=-1)


def _sink_attend(qj, kvs, sink):
    scale = ATT_HEAD_DIM ** -0.5
    logits = []
    for k, v, m in kvs:
        s = jnp.einsum('bqhgd,bkhd->bhgqk', qj, k).astype(jnp.float32) * scale
        if m is not None:
            s = jnp.where(m, s, -jnp.inf)
        logits.append(s)
    B, Q = qj.shape[:2]
    logits.append(jnp.broadcast_to(sink.astype(jnp.float32)[None, :, :, None, None],
                                   (B, ATT_KV_HEADS, ATT_GROUP, Q, 1)))
    p = jax.nn.softmax(jnp.concatenate(logits, axis=-1), axis=-1)
    out = None
    start = 0
    for k, v, _ in kvs:
        K = k.shape[1]
        o = jnp.einsum('bhgqk,bkhd->bqhgd', p[..., start:start + K].astype(v.dtype), v)
        out = o if out is None else out + o
        start += K
    return out


def _context_attention(q, k, v, sink):
    B, L = q.shape[:2]
    nb = L // BLOCK
    qb = q.reshape(B, nb, BLOCK, ATT_KV_HEADS, ATT_GROUP, ATT_HEAD_DIM).swapaxes(0, 1)
    ob = lax.map(lambda qj: _sink_attend(qj, [(k, v, None)], sink), qb)
    return ob.swapaxes(0, 1).reshape(B, L, ATT_Q_HEADS * ATT_HEAD_DIM)


def _latent_attention(q, k, v, ctx_k, ctx_v, sink):
    B, T = q.shape[:2]
    nb = T // BLOCK
    pad = ((0, 0), (BLOCK, BLOCK), (0, 0), (0, 0))
    kp = jnp.pad(k, pad)
    vp = jnp.pad(v, pad)

    def block(jb):
        start = jb * BLOCK
        qj = lax.dynamic_slice_in_dim(q, start, BLOCK, axis=1)
        kj = lax.dynamic_slice_in_dim(kp, start, 3 * BLOCK, axis=1)
        vj = lax.dynamic_slice_in_dim(vp, start, 3 * BLOCK, axis=1)
        qpos = start + jnp.arange(BLOCK)
        kpos = start - BLOCK + jnp.arange(3 * BLOCK)
        valid = ((jnp.abs(qpos[:, None] - kpos[None, :]) <= WINDOW)
                 & (kpos >= 0)[None, :] & (kpos < T)[None, :])
        return _sink_attend(qj, [(kj, vj, valid), (ctx_k, ctx_v, None)], sink)

    ob = lax.map(block, jnp.arange(nb))
    return ob.swapaxes(0, 1).reshape(B, T, ATT_Q_HEADS * ATT_HEAD_DIM)


def _hier_moe(h, w_group, b_group, w_expert, b_expert, w1, w3, w2):
    B, T, D = h.shape
    x = h.reshape(-1, D)
    N = x.shape[0]
    pg = jax.nn.softmax((x @ w_group + b_group).astype(jnp.float32), axis=-1)
    p_g, g_sel = lax.top_k(pg, 1)
    le = (x @ w_expert + b_expert).astype(jnp.float32).reshape(N, N_GROUPS, EXPERTS_PER_GROUP)
    le_sel = jnp.take_along_axis(le, g_sel[:, :, None], axis=1)[:, 0]
    pe = jax.nn.softmax(le_sel, axis=-1)
    w_top, i_top = lax.top_k(pe, TOP_K)
    w_top = w_top / jnp.sum(w_top, axis=-1, keepdims=True) * p_g
    eid = g_sel * EXPERTS_PER_GROUP + i_top
    gate = jnp.sum(jax.nn.one_hot(eid, N_EXPERTS, dtype=jnp.float32) * w_top[..., None], axis=1)
    a = jnp.einsum('nd,edf->nef', x, w1)
    b = jnp.einsum('nd,edf->nef', x, w3)
    hid = jax.nn.silu(a) * b * gate[..., None].astype(x.dtype)
    y = jnp.einsum('nef,efd->nd', hid, w2)
    return y.reshape(B, T, D)


def setup_inputs(seed: int = 0) -> dict:
    key = jax.random.key(seed)
    ks = jax.random.split(key, 32)
    f32 = jnp.float32
    D = D_MODEL

    def nrm(k, shape, s):
        return jax.random.normal(k, shape, f32) * s

    return {
        'x_prompt': nrm(ks[0], (BATCH, SEQ, D), 1.0),
        'x_sample': nrm(ks[1], (DEC_BATCH, DEC_SEQ, D), 1.0),
        'state_hgrn': nrm(ks[2], (DEC_BATCH, N_HGRN_LAYERS, 2, HG_HEADS, HG_DK, HG_DV), 0.5),
        'cache_k': nrm(ks[3], (DEC_BATCH, N_ATTN_LAYERS, PAST_LEN, ATT_KV_HEADS, ATT_HEAD_DIM), 1.0),
        'cache_v': nrm(ks[4], (DEC_BATCH, N_ATTN_LAYERS, PAST_LEN, ATT_KV_HEADS, ATT_HEAD_DIM), 1.0),
        'c': nrm(ks[5], (DEC_BATCH, D), 1.0),
        'c_ctx': nrm(ks[6], (D,), 1.0),
        'ada_w': nrm(ks[7], (DEPTH, D, 6 * D), 0.5 * D ** -0.5),
        'ada_b': nrm(ks[8], (DEPTH, 6 * D), 0.01),
        'norm_w': 1.0 + nrm(ks[9], (DEPTH, 2, D), 0.02),
        'hg_w_in': nrm(ks[10], (N_HGRN_LAYERS, D, 5 * D_REC), D ** -0.5),
        'hg_lb_logits': nrm(ks[11], (DEPTH + 1, 2, D_REC), 0.5),
        'hg_onorm': 1.0 + nrm(ks[12], (N_HGRN_LAYERS, HG_DV), 0.02),
        'hg_w_out': nrm(ks[13], (N_HGRN_LAYERS, D_REC, D), D_REC ** -0.5),
        'at_w_in': nrm(ks[14], (N_ATTN_LAYERS, D, (ATT_Q_HEADS + 2 * ATT_KV_HEADS) * ATT_HEAD_DIM), D ** -0.5),
        'at_sink': nrm(ks[15], (N_ATTN_LAYERS, ATT_Q_HEADS), 1.0),
        'at_w_out': nrm(ks[16], (N_ATTN_LAYERS, ATT_Q_HEADS * ATT_HEAD_DIM, D), (ATT_Q_HEADS * ATT_HEAD_DIM) ** -0.5),
        'moe_w_group': nrm(ks[17], (DEPTH, D, N_GROUPS), D ** -0.5),
        'moe_b_group': nrm(ks[18], (DEPTH, N_GROUPS), 0.01),
        'moe_w_expert': nrm(ks[19], (DEPTH, D, N_EXPERTS), D ** -0.5),
        'moe_b_expert': nrm(ks[20], (DEPTH, N_EXPERTS), 0.01),
        'moe_w1': nrm(ks[21], (DEPTH, N_EXPERTS, D, D_EXPERT), D ** -0.5),
        'moe_w3': nrm(ks[22], (DEPTH, N_EXPERTS, D, D_EXPERT), D ** -0.5),
        'moe_w2': nrm(ks[23], (DEPTH, N_EXPERTS, D_EXPERT, D), D_EXPERT ** -0.5),
        'final_norm': 1.0 + nrm(ks[24], (D,), 0.02),
    }


def reference(x_prompt, x_sample, state_hgrn, cache_k, cache_v, c, c_ctx,
              ada_w, ada_b, norm_w, hg_w_in, hg_lb_logits, hg_onorm, hg_w_out,
              at_w_in, at_sink, at_w_out, moe_w_group, moe_b_group, moe_w_expert,
              moe_b_expert, moe_w1, moe_w3, moe_w2, final_norm):
    xp, xs = x_prompt, x_sample
    lb_all = jnp.cumsum(jax.nn.softmax(hg_lb_logits.astype(jnp.float32), axis=0), axis=0)
    new_s, new_k, new_v = [], [], []
    for i in range(DEPTH):
        j = i // N_MIXERS
        mp = _modulation(c_ctx[None, :], ada_w[i], ada_b[i])
        ms = _modulation(c, ada_w[i], ada_b[i])
        hp = _modulate(_rmsnorm(xp, norm_w[i, 0]), mp[0], mp[1])
        hs = _modulate(_rmsnorm(xs, norm_w[i, 0]), ms[0], ms[1])
        if i % N_MIXERS == 0:
            zeros = jnp.zeros((xp.shape[0], HG_HEADS, HG_DK, HG_DV), xp.dtype)
            op, sf, sb = _hgrn2(hp, hg_w_in[j], hg_onorm[j], hg_w_out[j], lb_all[i], zeros, zeros)
            os_, _, _ = _hgrn2(hs, hg_w_in[j], hg_onorm[j], hg_w_out[j], lb_all[i],
                               state_hgrn[:, j, 0], state_hgrn[:, j, 1])
            new_s.append(jnp.stack([sf, sb], axis=1))
        else:
            sink = at_sink[j].reshape(ATT_KV_HEADS, ATT_GROUP)
            qp, kp, vp = _attn_qkv(hp, at_w_in[j])
            op = _context_attention(qp, kp, vp, sink) @ at_w_out[j]
            qs, kl, vl = _attn_qkv(hs, at_w_in[j])
            os_ = _latent_attention(_axial_rope(qs), _axial_rope(kl), vl,
                                    cache_k[:, j], cache_v[:, j], sink) @ at_w_out[j]
            new_k.append(kp)
            new_v.append(vp)
        xp = xp + mp[2] * op
        xs = xs + ms[2] * os_
        hp = _modulate(_rmsnorm(xp, norm_w[i, 1]), mp[3], mp[4])
        hs = _modulate(_rmsnorm(xs, norm_w[i, 1]), ms[3], ms[4])
        xp = xp + mp[5] * _hier_moe(hp, moe_w_group[i], moe_b_group[i], moe_w_expert[i],
                                    moe_b_expert[i], moe_w1[i], moe_w3[i], moe_w2[i])
        xs = xs + ms[5] * _hier_moe(hs, moe_w_group[i], moe_b_group[i], moe_w_expert[i],
                                    moe_b_expert[i], moe_w1[i], moe_w3[i], moe_w2[i])
    y_prompt = _rmsnorm(xp, final_norm)
    y_sample = _rmsnorm(xs, final_norm)
    state_hgrn_new = jnp.stack(new_s, axis=1)
    cache_k_new = jnp.stack(new_k, axis=1)
    cache_v_new = jnp.stack(new_v, axis=1)
    return (y_prompt, y_sample, state_hgrn_new, cache_k_new, cache_v_new)
```

```python
import functools

import jax
import jax.numpy as jnp
from jax import lax
from jax.experimental import pallas as pl
from jax.experimental.pallas import tpu as pltpu

F32 = jnp.float32
BF16 = jnp.bfloat16

D_MODEL = 1024
BATCH = 16
SEQ = 256
DEC_BATCH = 2
DEC_SEQ = 1024
PAST_LEN = 512
GRID_W = 64
HG_HEADS = 8
HG_DK = 128
CHUNK = 16
ATT_HEAD_DIM = 64
ATT_Q_HEADS = 16
ATT_KV_HEADS = 4
ATT_GROUP = 4
WINDOW = 128
BLOCK = 128
ROPE_HALF = 32
ROPE_BASE = 10000.0
N_GROUPS = 4
EXPERTS_PER_GROUP = 8
N_EXPERTS = 32
D_EXPERT = 256
EPS = 1e-6

N_PROMPT = BATCH * SEQ
N_TOK = N_PROMPT + DEC_BATCH * DEC_SEQ
TM = 256
N_TILES = N_TOK // TM
PROMPT_TILES = N_PROMPT // TM
TILES_PER_DEC_SEQ = DEC_SEQ // TM
LANES = 128
N_COND = 8
ROUTER_LANES = 128
DECAY_CLAMP = 60.0
MOE_TM = 2048
MOE_SUB = 512
MIB = 1024 * 1024


def _params(vmem_mib, *semantics):
    return pltpu.CompilerParams(dimension_semantics=semantics, vmem_limit_bytes=vmem_mib * MIB)


def _tile_cond(i):
    return jnp.where(i < PROMPT_TILES, 0, 1 + (i - PROMPT_TILES) // TILES_PER_DEC_SEQ)


def _mod_row(mod_ref, cond, which):
    return mod_ref[pl.ds(cond, 1), which * D_MODEL:(which + 1) * D_MODEL]


def _norm_mod(x, nw, shift, scale):
    y = x * lax.rsqrt(jnp.mean(x * x, axis=-1, keepdims=True) + EPS)
    return (y * nw) * (1.0 + scale) + shift


def _silu(x):
    return x * jax.nn.sigmoid(x)


def _ada_kernel(c_ref, w_ref, b_ref, o_ref):
    s = _silu(c_ref[...])
    o_ref[0] = jnp.dot(s, w_ref[0], precision=lax.Precision.HIGHEST, preferred_element_type=F32) + b_ref[0]


def _ada(cond, ada_w, ada_b):
    depth, _, n = ada_w.shape
    tn = 1536
    return pl.pallas_call(
        _ada_kernel,
        out_shape=jax.ShapeDtypeStruct((depth, N_COND, n), F32),
        grid=(depth, n // tn),
        in_specs=[
            pl.BlockSpec((N_COND, D_MODEL), lambda l, j: (0, 0)),
            pl.BlockSpec((1, D_MODEL, tn), lambda l, j: (l, 0, j)),
            pl.BlockSpec((1, 1, tn), lambda l, j: (l, 0, j)),
        ],
        out_specs=pl.BlockSpec((1, N_COND, tn), lambda l, j: (l, 0, j)),
        compiler_params=_params(40, "arbitrary", "arbitrary"),
        name="ada_modulation",
    )(cond, ada_w, ada_b.reshape(depth, 1, n))


def _hg_in_kernel(x_ref, mod_ref, nw_ref, w_ref, lbl_ref, q_ref, v_ref, g_ref, lff_ref, kf_ref, lfb_ref, kb_ref):
    cond = _tile_cond(pl.program_id(0))
    h = _norm_mod(x_ref[...], nw_ref[...], _mod_row(mod_ref, cond, 0), _mod_row(mod_ref, cond, 1)).astype(BF16)

    def proj(c):
        return jnp.dot(h, w_ref[:, c * D_MODEL:(c + 1) * D_MODEL], preferred_element_type=F32)

    l0, l1, l2 = lbl_ref[0], lbl_ref[1], lbl_ref[2]
    m = jnp.maximum(jnp.maximum(l0, l1), l2)
    e0, e1, e2 = jnp.exp(l0 - m), jnp.exp(l1 - m), jnp.exp(l2 - m)
    lb = e0 / (e0 + e1 + e2)

    q_ref[...] = proj(0).astype(BF16)
    v_ref[...] = proj(1).astype(BF16)
    for d, (lf_ref, k_ref) in enumerate(((lff_ref, kf_ref), (lfb_ref, kb_ref))):
        lbd = lb[d:d + 1, :]
        f = lbd + (1.0 - lbd) * jax.nn.sigmoid(proj(2 + d))
        lf_ref[...] = jnp.log(f)
        k_ref[...] = (1.0 - f).astype(BF16)
    g_ref[...] = proj(4).astype(BF16)


def _hg_in(x, mod, nw, w_in, lb_logits):
    tile = lambda i: (i, 0)
    fixed2 = lambda i: (0, 0)
    bf = jax.ShapeDtypeStruct((N_TOK, D_MODEL), BF16)
    ff = jax.ShapeDtypeStruct((N_TOK, D_MODEL), F32)
    blk = pl.BlockSpec((TM, D_MODEL), tile)
    return pl.pallas_call(
        _hg_in_kernel,
        out_shape=(bf, bf, bf, ff, bf, ff, bf),
        grid=(N_TILES,),
        in_specs=[
            blk,
            pl.BlockSpec(mod.shape, fixed2),
            pl.BlockSpec((1, D_MODEL), fixed2),
            pl.BlockSpec(w_in.shape, fixed2),
            pl.BlockSpec(lb_logits.shape, lambda i: (0, 0, 0)),
        ],
        out_specs=(blk,) * 7,
        compiler_params=_params(56, "arbitrary"),
        name="hgrn_in_proj",
    )(x, mod, nw, w_in, lb_logits)


def _gla_direction(reverse, q_ref, v_ref, lf_ref, k_ref, st_ref, o_ref, qd_scr, kd_scr, ki_scr, cd_scr):
    lf = lf_ref[...]
    r = lax.broadcasted_iota(jnp.int32, (TM, TM), 0)
    c = lax.broadcasted_iota(jnp.int32, (TM, TM), 1)
    same = (r // CHUNK) == (c // CHUNK)
    tri = (same & ((c >= r) if reverse else (c <= r))).astype(BF16)
    hi = lf.astype(BF16)
    lo = (lf - hi.astype(F32)).astype(BF16)
    b = jnp.dot(tri, hi, preferred_element_type=F32) + jnp.dot(tri, lo, preferred_element_type=F32)
    b3 = b.reshape(TM // CHUNK, CHUNK, D_MODEL)
    edge = 0 if reverse else CHUNK - 1
    tot = jnp.broadcast_to(b3[:, edge:edge + 1, :], b3.shape).reshape(TM, D_MODEL)
    kf = k_ref[...].astype(F32)
    qd_scr[...] = (q_ref[...].astype(F32) * jnp.exp(b)).astype(BF16)
    kd_scr[...] = (kf * jnp.exp(tot - b)).astype(BF16)
    ki_scr[...] = (kf * jnp.exp(jnp.minimum(-b, DECAY_CLAMP))).astype(BF16)
    cd_scr[...] = jnp.exp(tot)

    tr = lax.broadcasted_iota(jnp.int32, (CHUNK, CHUNK), 0)
    tc = lax.broadcasted_iota(jnp.int32, (CHUNK, CHUNK), 1)
    keep = (tc >= tr) if reverse else (tc <= tr)
    nt = (((1,), (1,)), ((), ()))
    tn = (((0,), (0,)), ((), ()))

    def chunk(it, carry):
        ci = (TM // CHUNK - 1 - it) if reverse else it
        rows = pl.ds(pl.multiple_of(ci * CHUNK, CHUNK), CHUNK)
        for h in range(HG_HEADS):
            cols = slice(h * HG_DK, (h + 1) * HG_DK)
            qh = qd_scr[rows, cols]
            vh = v_ref[rows, cols]
            a = lax.dot_general(qh, ki_scr[rows, cols], nt, preferred_element_type=F32)
            a = jnp.where(keep, a, 0.0).astype(BF16)
            st = st_ref[h]
            o = jnp.dot(a, vh, preferred_element_type=F32)
            o = o + lax.dot_general(qh, st.astype(BF16), nt, preferred_element_type=F32)
            o_ref[rows, cols] = o
            ut = lax.dot_general(vh, kd_scr[rows, cols], tn, preferred_element_type=F32)
            st_ref[h] = st * cd_scr[pl.ds(pl.multiple_of(ci * CHUNK, CHUNK), 1), cols] + ut
        return carry

    lax.fori_loop(0, TM // CHUNK, chunk, 0)


def _gla_kernel(qf_ref, vf_ref, lff_ref, kf_ref, qb_ref, vb_ref, lfb_ref, kb_ref, s0_ref,
                of_ref, ob_ref, sout_ref, stf_scr, stb_scr, qd_scr, kd_scr, ki_scr, cd_scr):
    i = pl.program_id(0)
    is_prompt = i < PROMPT_TILES
    first = jnp.logical_or(is_prompt, (i - PROMPT_TILES) % TILES_PER_DEC_SEQ == 0)

    @pl.when(first)
    def _():
        keep0 = jnp.where(is_prompt, 0.0, 1.0)
        for h in range(HG_HEADS):
            stf_scr[h] = s0_ref[0, 0, 0, h].T * keep0
            stb_scr[h] = s0_ref[0, 0, 1, h].T * keep0

    _gla_direction(False, qf_ref, vf_ref, lff_ref, kf_ref, stf_scr, of_ref, qd_scr, kd_scr, ki_scr, cd_scr)
    _gla_direction(True, qb_ref, vb_ref, lfb_ref, kb_ref, stb_scr, ob_ref, qd_scr, kd_scr, ki_scr, cd_scr)

    @pl.when(is_prompt)
    def _():
        for h in range(HG_HEADS):
            sout_ref[0, 0, 0, h] = stf_scr[h].T
            sout_ref[0, 0, 1, h] = stb_scr[h].T


def _gla(q, v, lff, kf, lfb, kb, state_hgrn):
    def fwd_tile(i):
        return (i, 0)

    def bwd_tile(i):
        j = (i - PROMPT_TILES) % TILES_PER_DEC_SEQ
        return (jnp.where(i < PROMPT_TILES, i, i - j + (TILES_PER_DEC_SEQ - 1 - j)), 0)

    def s0_idx(i):
        return (jnp.maximum(i - PROMPT_TILES, 0) // TILES_PER_DEC_SEQ, 0, 0, 0, 0, 0)

    def sout_idx(i):
        return (jnp.minimum(i, PROMPT_TILES - 1), 0, 0, 0, 0, 0)

    f_blk = pl.BlockSpec((TM, D_MODEL), fwd_tile)
    b_blk = pl.BlockSpec((TM, D_MODEL), bwd_tile)
    st_blk = (1, 1, 2, HG_HEADS, HG_DK, HG_DK)
    return pl.pallas_call(
        _gla_kernel,
        out_shape=(
            jax.ShapeDtypeStruct((N_TOK, D_MODEL), F32),
            jax.ShapeDtypeStruct((N_TOK, D_MODEL), F32),
            jax.ShapeDtypeStruct((BATCH,) + st_blk[1:], F32),
        ),
        grid=(N_TILES,),
        in_specs=[f_blk, f_blk, f_blk, f_blk, b_blk, b_blk, b_blk, b_blk, pl.BlockSpec(st_blk, s0_idx)],
        out_specs=(f_blk, b_blk, pl.BlockSpec(st_blk, sout_idx)),
        scratch_shapes=[
            pltpu.VMEM((HG_HEADS, HG_DK, HG_DK), F32),
            pltpu.VMEM((HG_HEADS, HG_DK, HG_DK), F32),
            pltpu.VMEM((TM, D_MODEL), BF16),
            pltpu.VMEM((TM, D_MODEL), BF16),
            pltpu.VMEM((TM, D_MODEL), BF16),
            pltpu.VMEM((TM, D_MODEL), F32),
        ],
        compiler_params=_params(48, "arbitrary"),
        name="hgrn_recurrence",
    )(q, v, lff, kf, q, v, lfb, kb, state_hgrn)


def _route(logits):
    lane = lax.broadcasted_iota(jnp.int32, logits.shape, 1)
    big = jnp.int32(ROUTER_LANES)
    neg = jnp.float32(-jnp.inf)
    is_group = (lane >= N_EXPERTS) & (lane < N_EXPERTS + N_GROUPS)
    gl = jnp.where(is_group, logits, neg)
    gmax = jnp.max(gl, axis=-1, keepdims=True)
    gsum = jnp.sum(jnp.exp(gl - gmax), axis=-1, keepdims=True)
    p_g = 1.0 / gsum
    g_sel = jnp.min(jnp.where(is_group & (gl == gmax), lane - N_EXPERTS, big), axis=-1, keepdims=True)
    in_sel = (lane < N_EXPERTS) & ((lane // EXPERTS_PER_GROUP) == g_sel)
    el = jnp.where(in_sel, logits, neg)
    m1 = jnp.max(el, axis=-1, keepdims=True)
    i1 = jnp.min(jnp.where(in_sel & (el == m1), lane, big), axis=-1, keepdims=True)
    el2 = jnp.where(lane == i1, neg, el)
    m2 = jnp.max(el2, axis=-1, keepdims=True)
    i2 = jnp.min(jnp.where(in_sel & (lane != i1) & (el2 == m2), lane, big), axis=-1, keepdims=True)
    e2 = jnp.exp(m2 - m1)
    w1 = p_g / (1.0 + e2)
    w2 = p_g * e2 / (1.0 + e2)
    return jnp.where(lane == i1, w1, 0.0) + jnp.where(lane == i2, w2, 0.0)


def _mixer_tail(mix_bf16, x_ref, mod_ref, nw2_ref, wo_ref, wr_ref, br_ref, x1_ref, h2_ref, gate_ref):
    cond = _tile_cond(pl.program_id(0))
    out = jnp.dot(mix_bf16, wo_ref[...], preferred_element_type=F32)
    x1 = x_ref[...] + _mod_row(mod_ref, cond, 2) * out
    x1_ref[...] = x1
    h2 = _norm_mod(x1, nw2_ref[...], _mod_row(mod_ref, cond, 3), _mod_row(mod_ref, cond, 4))
    h2_ref[...] = h2.astype(BF16)
    logits = jnp.dot(h2, wr_ref[...], precision=lax.Precision.HIGHEST, preferred_element_type=F32) + br_ref[...]
    gate_ref[...] = _route(logits)


def _hg_out_kernel(of_ref, ob_ref, g_ref, on_ref, x_ref, mod_ref, nw2_ref, wo_ref, wr_ref, br_ref,
                   x1_ref, h2_ref, gate_ref):
    o = of_ref[...] + ob_ref[...]
    parts = []
    for h in range(HG_HEADS):
        oh = o[:, h * HG_DK:(h + 1) * HG_DK]
        parts.append(oh * lax.rsqrt(jnp.mean(oh * oh, axis=-1, keepdims=True) + EPS) * on_ref[...])
    y = jnp.concatenate(parts, axis=1) * _silu(g_ref[...].astype(F32))
    _mixer_tail(y.astype(BF16), x_ref, mod_ref, nw2_ref, wo_ref, wr_ref, br_ref, x1_ref, h2_ref, gate_ref)


def _at_out_kernel(a_ref, x_ref, mod_ref, nw2_ref, wo_ref, wr_ref, br_ref, x1_ref, h2_ref, gate_ref):
    _mixer_tail(a_ref[...], x_ref, mod_ref, nw2_ref, wo_ref, wr_ref, br_ref, x1_ref, h2_ref, gate_ref)


def _mixer_out(kernel_fn, name, mix_inputs, mix_specs, x, mod, nw2, w_out, w_router, b_router):
    tile = lambda i: (i, 0)
    fixed2 = lambda i: (0, 0)
    blk = pl.BlockSpec((TM, D_MODEL), tile)
    return pl.pallas_call(
        kernel_fn,
        out_shape=(
            jax.ShapeDtypeStruct((N_TOK, D_MODEL), F32),
            jax.ShapeDtypeStruct((N_TOK, D_MODEL), BF16),
            jax.ShapeDtypeStruct((N_TOK, ROUTER_LANES), F32),
        ),
        grid=(N_TILES,),
        in_specs=list(mix_specs) + [
            blk,
            pl.BlockSpec(mod.shape, fixed2),
            pl.BlockSpec((1, D_MODEL), fixed2),
            pl.BlockSpec(w_out.shape, fixed2),
            pl.BlockSpec(w_router.shape, fixed2),
            pl.BlockSpec((1, ROUTER_LANES), fixed2),
        ],
        out_specs=(blk, blk, pl.BlockSpec((TM, ROUTER_LANES), tile)),
        compiler_params=_params(40, "arbitrary"),
        name=name,
    )(*mix_inputs, x, mod, nw2, w_out, w_router, b_router)


def _moe_kernel(h_ref, gate_ref, w1_ref, w3_ref, w2_ref, o_ref):
    e = pl.program_id(1)
    w1 = w1_ref[0].astype(BF16)
    w3 = w3_ref[0].astype(BF16)
    w2 = w2_ref[0].astype(BF16)

    def sub(s, carry):
        rows = pl.ds(pl.multiple_of(s * MOE_SUB, MOE_SUB), MOE_SUB)
        h = h_ref[rows, :]
        gate = gate_ref[rows, :]
        lane = lax.broadcasted_iota(jnp.int32, gate.shape, 1)
        ge = jnp.sum(jnp.where(lane == e, gate, 0.0), axis=-1, keepdims=True)
        a = jnp.dot(h, w1, preferred_element_type=F32)
        b = jnp.dot(h, w3, preferred_element_type=F32)
        hid = (_silu(a) * b * ge).astype(BF16)
        y = jnp.dot(hid, w2, preferred_element_type=F32)

        @pl.when(e == 0)
        def _():
            o_ref[rows, :] = y

        @pl.when(e != 0)
        def _():
            o_ref[rows, :] += y

        return carry

    lax.fori_loop(0, MOE_TM // MOE_SUB, sub, 0)


def _moe(h2, gate, w1, w3, w2):
    return pl.pallas_call(
        _moe_kernel,
        out_shape=jax.ShapeDtypeStruct((N_TOK, D_MODEL), F32),
        grid=(N_TOK // MOE_TM, N_EXPERTS),
        in_specs=[
            pl.BlockSpec((MOE_TM, D_MODEL), lambda m, e: (m, 0)),
            pl.BlockSpec((MOE_TM, ROUTER_LANES), lambda m, e: (m, 0)),
            pl.BlockSpec((1, D_MODEL, D_EXPERT), lambda m, e: (e, 0, 0)),
            pl.BlockSpec((1, D_MODEL, D_EXPERT), lambda m, e: (e, 0, 0)),
            pl.BlockSpec((1, D_EXPERT, D_MODEL), lambda m, e: (e, 0, 0)),
        ],
        out_specs=pl.BlockSpec((MOE_TM, D_MODEL), lambda m, e: (m, 0)),
        compiler_params=_params(56, "arbitrary", "arbitrary"),
        name="moe_experts",
    )(h2, gate, w1, w3, w2)


def _moe_res_kernel(x_ref, y_ref, mod_ref, o_ref):
    cond = _tile_cond(pl.program_id(0))
    o_ref[...] = x_ref[...] + _mod_row(mod_ref, cond, 5) * y_ref[...]


def _moe_res_final_kernel(x_ref, y_ref, mod_ref, fn_ref, o_ref):
    cond = _tile_cond(pl.program_id(0))
    x = x_ref[...] + _mod_row(mod_ref, cond, 5) * y_ref[...]
    o_ref[...] = x * lax.rsqrt(jnp.mean(x * x, axis=-1, keepdims=True) + EPS) * fn_ref[...]


def _moe_res(x1, y, mod, final_norm=None):
    tile = lambda i: (i, 0)
    fixed2 = lambda i: (0, 0)
    blk = pl.BlockSpec((TM, D_MODEL), tile)
    specs = [blk, blk, pl.BlockSpec(mod.shape, fixed2)]
    args = [x1, y, mod]
    if final_norm is not None:
        specs.append(pl.BlockSpec((1, D_MODEL), fixed2))
        args.append(final_norm)
    return pl.pallas_call(
        _moe_res_kernel if final_norm is None else _moe_res_final_kernel,
        out_shape=jax.ShapeDtypeStruct((N_TOK, D_MODEL), F32),
        grid=(N_TILES,),
        in_specs=specs,
        out_specs=blk,
        compiler_params=_params(32, "arbitrary"),
        name="moe_residual",
    )(*args)


def _swap_rotary_halves(x):
    n = x.shape[-1]
    lane = lax.broadcasted_iota(jnp.int32, x.shape, 1)
    quarter = ROPE_HALF // 2
    return jnp.where((lane % ROPE_HALF) < quarter, pltpu.roll(x, n - quarter, 1), pltpu.roll(x, quarter, 1))


def _at_in_kernel(x_ref, mod_ref, nw_ref, w_ref, cos_ref, sin_ref, q_ref, k_ref, v_ref):
    cond = _tile_cond(pl.program_id(0))
    h = _norm_mod(x_ref[...], nw_ref[...], _mod_row(mod_ref, cond, 0), _mod_row(mod_ref, cond, 1)).astype(BF16)
    nq = ATT_Q_HEADS * ATT_HEAD_DIM
    nk = ATT_KV_HEADS * ATT_HEAD_DIM
    cos = cos_ref[...]
    sin = sin_ref[...]

    def rope(x):
        reps = x.shape[-1] // LANES
        return x * jnp.concatenate([cos] * reps, axis=1) + _swap_rotary_halves(x) * jnp.concatenate([sin] * reps, axis=1)

    q_ref[...] = rope(jnp.dot(h, w_ref[:, :nq], preferred_element_type=F32)).astype(BF16)
    k_ref[...] = rope(jnp.dot(h, w_ref[:, nq:nq + nk], preferred_element_type=F32))
    v_ref[...] = jnp.dot(h, w_ref[:, nq + nk:], preferred_element_type=F32)


def _rope_tables():
    pos = jnp.arange(DEC_SEQ)
    t_row = (pos // GRID_W).astype(F32)
    t_col = (pos % GRID_W).astype(F32)
    inv = ROPE_BASE ** (-jnp.arange(0, ROPE_HALF, 2, dtype=F32) / ROPE_HALF)
    lane = jnp.arange(LANES)
    j = lane % ATT_HEAD_DIM
    freq = inv[(j % ROPE_HALF) % (ROPE_HALF // 2)]
    t = jnp.where((j < ROPE_HALF)[None, :], t_row[:, None], t_col[:, None])
    ang = t * freq[None, :]
    sign = jnp.where((j % ROPE_HALF) < ROPE_HALF // 2, -1.0, 1.0).astype(F32)
    cos = jnp.concatenate([jnp.ones((TM, LANES), F32), jnp.cos(ang)], axis=0)
    sin = jnp.concatenate([jnp.zeros((TM, LANES), F32), jnp.sin(ang) * sign[None, :]], axis=0)
    return cos, sin


def _at_in(x, mod, nw, w_in, cos, sin):
    tile = lambda i: (i, 0)
    fixed2 = lambda i: (0, 0)
    rope_tile = lambda i: (jnp.where(i < PROMPT_TILES, 0, 1 + (i - PROMPT_TILES) % TILES_PER_DEC_SEQ), 0)
    nk = ATT_KV_HEADS * ATT_HEAD_DIM
    return pl.pallas_call(
        _at_in_kernel,
        out_shape=(
            jax.ShapeDtypeStruct((N_TOK, D_MODEL), BF16),
            jax.ShapeDtypeStruct((N_TOK, nk), F32),
            jax.ShapeDtypeStruct((N_TOK, nk), F32),
        ),
        grid=(N_TILES,),
        in_specs=[
            pl.BlockSpec((TM, D_MODEL), tile),
            pl.BlockSpec(mod.shape, fixed2),
            pl.BlockSpec((1, D_MODEL), fixed2),
            pl.BlockSpec(w_in.shape, fixed2),
            pl.BlockSpec((TM, LANES), rope_tile),
            pl.BlockSpec((TM, LANES), rope_tile),
        ],
        out_specs=(pl.BlockSpec((TM, D_MODEL), tile), pl.BlockSpec((TM, nk), tile), pl.BlockSpec((TM, nk), tile)),
        compiler_params=_params(40, "arbitrary"),
        name="attn_in_proj",
    )(x, mod, nw, w_in, cos, sin)


def _attend(q, k_all, v_all, mask, sink_ref, o_ref):
    group_lanes = ATT_GROUP * ATT_HEAD_DIM
    scale = ATT_HEAD_DIM ** -0.5
    nt = (((1,), (1,)), ((), ()))
    lane = lax.broadcasted_iota(jnp.int32, (q.shape[0], group_lanes), 1)
    for hk in range(ATT_KV_HEADS):
        kh = k_all[:, hk * ATT_HEAD_DIM:(hk + 1) * ATT_HEAD_DIM]
        vh = v_all[:, hk * ATT_HEAD_DIM:(hk + 1) * ATT_HEAD_DIM]
        kt = jnp.concatenate([kh] * ATT_GROUP, axis=1)
        vt = jnp.concatenate([vh] * ATT_GROUP, axis=1)
        qg = q[:, hk * group_lanes:(hk + 1) * group_lanes]
        acc = jnp.zeros((q.shape[0], group_lanes), F32)
        for g in range(ATT_GROUP):
            mine = (lane // ATT_HEAD_DIM) == g
            qm = jnp.where(mine, qg, jnp.zeros_like(qg))
            s = lax.dot_general(qm, kt, nt, preferred_element_type=F32) * scale
            if mask is not None:
                s = jnp.where(mask, s, -jnp.inf)
            sink = sink_ref[hk * ATT_GROUP + g]
            m = jnp.maximum(jnp.max(s, axis=-1, keepdims=True), sink)
            p = jnp.exp(s - m)
            denom = jnp.sum(p, axis=-1, keepdims=True) + jnp.exp(sink - m)
            o = jnp.dot(p.astype(BF16), vt, preferred_element_type=F32) / denom
            acc = acc + jnp.where(mine, o, 0.0)
        o_ref[:, hk * group_lanes:(hk + 1) * group_lanes] = acc.astype(BF16)


def _ctx_attn_kernel(sink_ref, q_ref, k_ref, v_ref, o_ref):
    _attend(q_ref[...], k_ref[...].astype(BF16), v_ref[...].astype(BF16), None, sink_ref, o_ref)


def _lat_attn_kernel(sink_ref, q_ref, kp_ref, kc_ref, kn_ref, vp_ref, vc_ref, vn_ref, ck_ref, cv_ref, o_ref):
    jb = pl.program_id(1)
    k_all = jnp.concatenate([kp_ref[...], kc_ref[...], kn_ref[...], ck_ref[0]], axis=0).astype(BF16)
    v_all = jnp.concatenate([vp_ref[...], vc_ref[...], vn_ref[...], cv_ref[0]], axis=0).astype(BF16)
    nkeys = 3 * BLOCK + PAST_LEN
    qi = lax.broadcasted_iota(jnp.int32, (BLOCK, nkeys), 0)
    kj = lax.broadcasted_iota(jnp.int32, (BLOCK, nkeys), 1)
    qpos = jb * BLOCK + qi
    kpos = (jb - 1) * BLOCK + kj
    local_ok = (jnp.abs(qpos - kpos) <= WINDOW) & (kpos >= 0) & (kpos < DEC_SEQ)
    mask = (kj >= 3 * BLOCK) | local_ok
    _attend(q_ref[...], k_all, v_all, mask, sink_ref, o_ref)


def _attention(q, k, v, cache_k, cache_v, sink):
    nk = ATT_KV_HEADS * ATT_HEAD_DIM
    smem = pl.BlockSpec(memory_space=pltpu.SMEM)
    ctx = pl.pallas_call(
        _ctx_attn_kernel,
        out_shape=jax.ShapeDtypeStruct((N_PROMPT, D_MODEL), BF16),
        grid=(BATCH,),
        in_specs=[
            smem,
            pl.BlockSpec((SEQ, D_MODEL), lambda b: (b, 0)),
            pl.BlockSpec((SEQ, nk), lambda b: (b, 0)),
            pl.BlockSpec((SEQ, nk), lambda b: (b, 0)),
        ],
        out_specs=pl.BlockSpec((SEQ, D_MODEL), lambda b: (b, 0)),
        compiler_params=_params(40, "arbitrary"),
        name="context_attention",
    )(sink, q, k, v)

    nb = DEC_SEQ // BLOCK
    base = N_PROMPT // BLOCK
    cur = lambda b, j: (base + b * nb + j, 0)
    prev = lambda b, j: (base + b * nb + jnp.maximum(j - 1, 0), 0)
    nxt = lambda b, j: (base + b * nb + jnp.minimum(j + 1, nb - 1), 0)
    kv_blk = lambda f: pl.BlockSpec((BLOCK, nk), f)
    cache_blk = pl.BlockSpec((1, PAST_LEN, nk), lambda b, j: (b, 0, 0))
    lat = pl.pallas_call(
        _lat_attn_kernel,
        out_shape=jax.ShapeDtypeStruct((DEC_BATCH * DEC_SEQ, D_MODEL), BF16),
        grid=(DEC_BATCH, nb),
        in_specs=[
            smem,
            pl.BlockSpec((BLOCK, D_MODEL), cur),
            kv_blk(prev), kv_blk(cur), kv_blk(nxt),
            kv_blk(prev), kv_blk(cur), kv_blk(nxt),
            cache_blk, cache_blk,
        ],
        out_specs=pl.BlockSpec((BLOCK, D_MODEL), lambda b, j: (b * nb + j, 0)),
        compiler_params=_params(40, "arbitrary", "arbitrary"),
        name="latent_attention",
    )(sink, q, k, k, k, v, v, v, cache_k, cache_v)
    return jnp.concatenate([ctx, lat], axis=0)


def kernel(x_prompt, x_sample, state_hgrn, cache_k, cache_v, c, c_ctx, ada_w, ada_b, norm_w, hg_w_in,
           hg_lb_logits, hg_onorm, hg_w_out, at_w_in, at_sink, at_w_out, moe_w_group, moe_b_group,
           moe_w_expert, moe_b_expert, moe_w1, moe_w3, moe_w2, final_norm):
    x = jnp.concatenate([x_prompt.reshape(N_PROMPT, D_MODEL), x_sample.reshape(-1, D_MODEL)], axis=0)
    cond = jnp.concatenate([c_ctx[None, :], c, jnp.zeros((N_COND - 1 - DEC_BATCH, D_MODEL), F32)], axis=0)
    mod = _ada(cond, ada_w, ada_b)
    nk = ATT_KV_HEADS * ATT_HEAD_DIM

    def router_params(i):
        pad = jnp.zeros((D_MODEL, ROUTER_LANES - N_EXPERTS - N_GROUPS), F32)
        w = jnp.concatenate([moe_w_expert[i], moe_w_group[i], pad], axis=1)
        b = jnp.concatenate([moe_b_expert[i], moe_b_group[i], pad[0]])[None, :]
        return w, b

    tile = lambda i: (i, 0)
    blk = pl.BlockSpec((TM, D_MODEL), tile)

    q, v, g, lff, kf, lfb, kb = _hg_in(x, mod[0], norm_w[0, 0][None, :], hg_w_in[0].astype(BF16), hg_lb_logits)
    o_f, o_b, state_new = _gla(q, v, lff, kf, lfb, kb, state_hgrn)
    wr, br = router_params(0)
    x1, h2, gate = _mixer_out(
        _hg_out_kernel, "hgrn_out_route", (o_f, o_b, g, hg_onorm[0][None, :]),
        (blk, blk, blk, pl.BlockSpec((1, HG_DK), lambda i: (0, 0))),
        x, mod[0], norm_w[0, 1][None, :], hg_w_out[0].astype(BF16), wr, br)
    x = _moe_res(x1, _moe(h2, gate, moe_w1[0], moe_w3[0], moe_w2[0]), mod[0])

    cos, sin = _rope_tables()
    qa, ka, va = _at_in(x, mod[1], norm_w[1, 0][None, :], at_w_in[0].astype(BF16), cos, sin)
    attn = _attention(qa, ka, va, cache_k[:, 0].reshape(DEC_BATCH, PAST_LEN, nk),
                      cache_v[:, 0].reshape(DEC_BATCH, PAST_LEN, nk), at_sink[0])
    wr, br = router_params(1)
    x1, h2, gate = _mixer_out(
        _at_out_kernel, "attn_out_route", (attn,), (blk,),
        x, mod[1], norm_w[1, 1][None, :], at_w_out[0].astype(BF16), wr, br)
    y = _moe_res(x1, _moe(h2, gate, moe_w1[1], moe_w3[1], moe_w2[1]), mod[1], final_norm[None, :])

    y_prompt = y[:N_PROMPT].reshape(BATCH, SEQ, D_MODEL)
    y_sample = y[N_PROMPT:].reshape(DEC_BATCH, DEC_SEQ, D_MODEL)
    cache_k_new = ka[:N_PROMPT].reshape(BATCH, 1, SEQ, ATT_KV_HEADS, ATT_HEAD_DIM)
    cache_v_new = va[:N_PROMPT].reshape(BATCH, 1, SEQ, ATT_KV_HEADS, ATT_HEAD_DIM)
    return (y_prompt, y_sample, state_new, cache_k_new, cache_v_new)
```

```python
import functools

import jax
import jax.numpy as jnp
from jax import lax
from jax.experimental import pallas as pl
from jax.experimental.pallas import tpu as pltpu

F32 = jnp.float32
BF16 = jnp.bfloat16

D_MODEL = 1024
BATCH = 16
SEQ = 256
DEC_BATCH = 2
DEC_SEQ = 1024
PAST_LEN = 512
GRID_W = 64
HG_HEADS = 8
HG_DK = 128
CHUNK = 16
ATT_HEAD_DIM = 64
ATT_Q_HEADS = 16
ATT_KV_HEADS = 4
ATT_GROUP = 4
WINDOW = 128
BLOCK = 128
ROPE_HALF = 32
ROPE_BASE = 10000.0
N_GROUPS = 4
EXPERTS_PER_GROUP = 8
N_EXPERTS = 32
D_EXPERT = 256
EPS = 1e-6

N_PROMPT = BATCH * SEQ
N_TOK = N_PROMPT + DEC_BATCH * DEC_SEQ
TM = 256
N_TILES = N_TOK // TM
PROMPT_TILES = N_PROMPT // TM
TILES_PER_DEC_SEQ = DEC_SEQ // TM
LANES = 128
N_COND = 8
ROUTER_LANES = 128
GLA_UNROLL = 2
DECAY_CLAMP = 60.0
MOE_TM = 2048
MOE_SUB = 512
MIB = 1024 * 1024


def _params(vmem_mib, *semantics):
    return pltpu.CompilerParams(dimension_semantics=semantics, vmem_limit_bytes=vmem_mib * MIB)


def _tile_cond(i):
    return jnp.where(i < PROMPT_TILES, 0, 1 + (i - PROMPT_TILES) // TILES_PER_DEC_SEQ)


def _mod_row(mod_ref, cond, which):
    return mod_ref[pl.ds(cond, 1), which * D_MODEL:(which + 1) * D_MODEL]


def _norm_mod(x, nw, shift, scale):
    y = x * lax.rsqrt(jnp.mean(x * x, axis=-1, keepdims=True) + EPS)
    return (y * nw) * (1.0 + scale) + shift


def _silu(x):
    return x * jax.nn.sigmoid(x)


def _ada_kernel(c_ref, w_ref, b_ref, o_ref):
    s = _silu(c_ref[...])
    o_ref[0] = jnp.dot(s, w_ref[0], precision=lax.Precision.HIGHEST, preferred_element_type=F32) + b_ref[0]


def _ada(cond, ada_w, ada_b):
    depth, _, n = ada_w.shape
    tn = 1536
    return pl.pallas_call(
        _ada_kernel,
        out_shape=jax.ShapeDtypeStruct((depth, N_COND, n), F32),
        grid=(depth, n // tn),
        in_specs=[
            pl.BlockSpec((N_COND, D_MODEL), lambda l, j: (0, 0)),
            pl.BlockSpec((1, D_MODEL, tn), lambda l, j: (l, 0, j)),
            pl.BlockSpec((1, 1, tn), lambda l, j: (l, 0, j)),
        ],
        out_specs=pl.BlockSpec((1, N_COND, tn), lambda l, j: (l, 0, j)),
        compiler_params=_params(40, "arbitrary", "arbitrary"),
        name="ada_modulation",
    )(cond, ada_w, ada_b.reshape(depth, 1, n))


def _hg_in_kernel(x_ref, mod_ref, nw_ref, w_ref, lbl_ref, q_ref, v_ref, g_ref, lff_ref, kf_ref, lfb_ref, kb_ref):
    cond = _tile_cond(pl.program_id(0))
    h = _norm_mod(x_ref[...], nw_ref[...], _mod_row(mod_ref, cond, 0), _mod_row(mod_ref, cond, 1)).astype(BF16)

    def proj(c):
        return jnp.dot(h, w_ref[:, c * D_MODEL:(c + 1) * D_MODEL], preferred_element_type=F32)

    l0, l1, l2 = lbl_ref[0], lbl_ref[1], lbl_ref[2]
    m = jnp.maximum(jnp.maximum(l0, l1), l2)
    e0, e1, e2 = jnp.exp(l0 - m), jnp.exp(l1 - m), jnp.exp(l2 - m)
    lb = e0 / (e0 + e1 + e2)

    q_ref[...] = proj(0).astype(BF16)
    v_ref[...] = proj(1).astype(BF16)
    for d, (lf_ref, k_ref) in enumerate(((lff_ref, kf_ref), (lfb_ref, kb_ref))):
        lbd = lb[d:d + 1, :]
        f = lbd + (1.0 - lbd) * jax.nn.sigmoid(proj(2 + d))
        lf_ref[...] = jnp.log(f)
        k_ref[...] = (1.0 - f).astype(BF16)
    g_ref[...] = proj(4).astype(BF16)


def _hg_in(x, mod, nw, w_in, lb_logits):
    tile = lambda i: (i, 0)
    fixed2 = lambda i: (0, 0)
    bf = jax.ShapeDtypeStruct((N_TOK, D_MODEL), BF16)
    ff = jax.ShapeDtypeStruct((N_TOK, D_MODEL), F32)
    blk = pl.BlockSpec((TM, D_MODEL), tile)
    return pl.pallas_call(
        _hg_in_kernel,
        out_shape=(bf, bf, bf, ff, bf, ff, bf),
        grid=(N_TILES,),
        in_specs=[
            blk,
            pl.BlockSpec(mod.shape, fixed2),
            pl.BlockSpec((1, D_MODEL), fixed2),
            pl.BlockSpec(w_in.shape, fixed2),
            pl.BlockSpec(lb_logits.shape, lambda i: (0, 0, 0)),
        ],
        out_specs=(blk,) * 7,
        compiler_params=_params(56, "arbitrary"),
        name="hgrn_in_proj",
    )(x, mod, nw, w_in, lb_logits)


def _gla_direction(reverse, q_ref, v_ref, lf_ref, k_ref, st_ref, sw_ref, o_ref, qd_scr, kd_scr, ki_scr, cd_scr):
    lf = lf_ref[...]
    r = lax.broadcasted_iota(jnp.int32, (TM, TM), 0)
    c = lax.broadcasted_iota(jnp.int32, (TM, TM), 1)
    same = (r // CHUNK) == (c // CHUNK)
    tri = (same & ((c >= r) if reverse else (c <= r))).astype(BF16)
    hi = lf.astype(BF16)
    lo = (lf - hi.astype(F32)).astype(BF16)
    b = jnp.dot(tri, hi, preferred_element_type=F32) + jnp.dot(tri, lo, preferred_element_type=F32)
    b3 = b.reshape(TM // CHUNK, CHUNK, D_MODEL)
    edge = 0 if reverse else CHUNK - 1
    tot = jnp.broadcast_to(b3[:, edge:edge + 1, :], b3.shape).reshape(TM, D_MODEL)
    kf = k_ref[...].astype(F32)
    qd_scr[...] = (q_ref[...].astype(F32) * jnp.exp(b)).astype(BF16)
    kd_scr[...] = (kf * jnp.exp(tot - b)).astype(BF16)
    ki_scr[...] = (kf * jnp.exp(jnp.minimum(-b, DECAY_CLAMP))).astype(BF16)
    cd_scr[...] = jnp.exp(tot)

    tr = lax.broadcasted_iota(jnp.int32, (CHUNK, CHUNK), 0)
    tc = lax.broadcasted_iota(jnp.int32, (CHUNK, CHUNK), 1)
    keep = (tc >= tr) if reverse else (tc <= tr)
    nt = (((1,), (1,)), ((), ()))
    tn = (((0,), (0,)), ((), ()))

    heads = [slice(h * HG_DK, (h + 1) * HG_DK) for h in range(HG_HEADS)]
    for h in range(HG_HEADS):
        sw_ref[0, h] = st_ref[0, h].T.astype(BF16)

    def chunk(ci, src, dst):
        row0 = pl.multiple_of(ci * CHUNK, CHUNK)
        rows = pl.ds(row0, CHUNK)
        for h, cols in enumerate(heads):
            ut = lax.dot_general(v_ref[rows, cols], kd_scr[rows, cols], tn, preferred_element_type=F32)
            new = st_ref[src, h] * cd_scr[pl.ds(row0, 1), cols] + ut
            st_ref[dst, h] = new
            sw_ref[dst, h] = new.T.astype(BF16)
        a = [lax.dot_general(qd_scr[rows, cols], ki_scr[rows, cols], nt, preferred_element_type=F32) for cols in heads]
        inter = [jnp.dot(qd_scr[rows, cols], sw_ref[src, h], preferred_element_type=F32)
                 for h, cols in enumerate(heads)]
        for h, cols in enumerate(heads):
            am = jnp.where(keep, a[h], 0.0).astype(BF16)
            o_ref[rows, cols] = jnp.dot(am, v_ref[rows, cols], preferred_element_type=F32) + inter[h]

    n_chunks = TM // CHUNK

    def chunk_group(it, carry):
        for p in range(GLA_UNROLL):
            step = GLA_UNROLL * it + p
            chunk((n_chunks - 1 - step) if reverse else step, p % 2, 1 - p % 2)
        return carry

    lax.fori_loop(0, n_chunks // GLA_UNROLL, chunk_group, 0)


def _gla_kernel(qf_ref, vf_ref, lff_ref, kf_ref, qb_ref, vb_ref, lfb_ref, kb_ref, s0_ref,
                of_ref, ob_ref, sout_ref, stf_scr, stb_scr, sw_scr, qd_scr, kd_scr, ki_scr, cd_scr):
    i = pl.program_id(0)
    is_prompt = i < PROMPT_TILES
    first = jnp.logical_or(is_prompt, (i - PROMPT_TILES) % TILES_PER_DEC_SEQ == 0)

    @pl.when(first)
    def _():
        keep0 = jnp.where(is_prompt, 0.0, 1.0)
        for h in range(HG_HEADS):
            stf_scr[0, h] = s0_ref[0, 0, 0, h].T * keep0
            stb_scr[0, h] = s0_ref[0, 0, 1, h].T * keep0

    _gla_direction(False, qf_ref, vf_ref, lff_ref, kf_ref, stf_scr, sw_scr, of_ref, qd_scr, kd_scr, ki_scr, cd_scr)
    _gla_direction(True, qb_ref, vb_ref, lfb_ref, kb_ref, stb_scr, sw_scr, ob_ref, qd_scr, kd_scr, ki_scr, cd_scr)

    @pl.when(is_prompt)
    def _():
        for h in range(HG_HEADS):
            sout_ref[0, 0, 0, h] = stf_scr[0, h].T
            sout_ref[0, 0, 1, h] = stb_scr[0, h].T


def _gla(q, v, lff, kf, lfb, kb, state_hgrn):
    def fwd_tile(i):
        return (i, 0)

    def bwd_tile(i):
        j = (i - PROMPT_TILES) % TILES_PER_DEC_SEQ
        return (jnp.where(i < PROMPT_TILES, i, i - j + (TILES_PER_DEC_SEQ - 1 - j)), 0)

    def s0_idx(i):
        return (jnp.maximum(i - PROMPT_TILES, 0) // TILES_PER_DEC_SEQ, 0, 0, 0, 0, 0)

    def sout_idx(i):
        return (jnp.minimum(i, PROMPT_TILES - 1), 0, 0, 0, 0, 0)

    f_blk = pl.BlockSpec((TM, D_MODEL), fwd_tile)
    b_blk = pl.BlockSpec((TM, D_MODEL), bwd_tile)
    st_blk = (1, 1, 2, HG_HEADS, HG_DK, HG_DK)
    return pl.pallas_call(
        _gla_kernel,
        out_shape=(
            jax.ShapeDtypeStruct((N_TOK, D_MODEL), F32),
            jax.ShapeDtypeStruct((N_TOK, D_MODEL), F32),
            jax.ShapeDtypeStruct((BATCH,) + st_blk[1:], F32),
        ),
        grid=(N_TILES,),
        in_specs=[f_blk, f_blk, f_blk, f_blk, b_blk, b_blk, b_blk, b_blk, pl.BlockSpec(st_blk, s0_idx)],
        out_specs=(f_blk, b_blk, pl.BlockSpec(st_blk, sout_idx)),
        scratch_shapes=[
            pltpu.VMEM((2, HG_HEADS, HG_DK, HG_DK), F32),
            pltpu.VMEM((2, HG_HEADS, HG_DK, HG_DK), F32),
            pltpu.VMEM((2, HG_HEADS, HG_DK, HG_DK), BF16),
            pltpu.VMEM((TM, D_MODEL), BF16),
            pltpu.VMEM((TM, D_MODEL), BF16),
            pltpu.VMEM((TM, D_MODEL), BF16),
            pltpu.VMEM((TM, D_MODEL), F32),
        ],
        compiler_params=_params(48, "arbitrary"),
        name="hgrn_recurrence",
    )(q, v, lff, kf, q, v, lfb, kb, state_hgrn)


def _route(logits):
    lane = lax.broadcasted_iota(jnp.int32, logits.shape, 1)
    big = jnp.int32(ROUTER_LANES)
    neg = jnp.float32(-jnp.inf)
    is_group = (lane >= N_EXPERTS) & (lane < N_EXPERTS + N_GROUPS)
    gl = jnp.where(is_group, logits, neg)
    gmax = jnp.max(gl, axis=-1, keepdims=True)
    gsum = jnp.sum(jnp.exp(gl - gmax), axis=-1, keepdims=True)
    p_g = 1.0 / gsum
    g_sel = jnp.min(jnp.where(is_group & (gl == gmax), lane - N_EXPERTS, big), axis=-1, keepdims=True)
    in_sel = (lane < N_EXPERTS) & ((lane // EXPERTS_PER_GROUP) == g_sel)
    el = jnp.where(in_sel, logits, neg)
    m1 = jnp.max(el, axis=-1, keepdims=True)
    i1 = jnp.min(jnp.where(in_sel & (el == m1), lane, big), axis=-1, keepdims=True)
    el2 = jnp.where(lane == i1, neg, el)
    m2 = jnp.max(el2, axis=-1, keepdims=True)
    i2 = jnp.min(jnp.where(in_sel & (lane != i1) & (el2 == m2), lane, big), axis=-1, keepdims=True)
    e2 = jnp.exp(m2 - m1)
    w1 = p_g / (1.0 + e2)
    w2 = p_g * e2 / (1.0 + e2)
    return jnp.where(lane == i1, w1, 0.0) + jnp.where(lane == i2, w2, 0.0)


def _mixer_tail(mix_bf16, x_ref, mod_ref, nw2_ref, wo_ref, wr_ref, br_ref, x1_ref, h2_ref, gate_ref):
    cond = _tile_cond(pl.program_id(0))
    out = jnp.dot(mix_bf16, wo_ref[...], preferred_element_type=F32)
    x1 = x_ref[...] + _mod_row(mod_ref, cond, 2) * out
    x1_ref[...] = x1
    h2 = _norm_mod(x1, nw2_ref[...], _mod_row(mod_ref, cond, 3), _mod_row(mod_ref, cond, 4))
    h2_ref[...] = h2.astype(BF16)
    logits = jnp.dot(h2, wr_ref[...], precision=lax.Precision.HIGHEST, preferred_element_type=F32) + br_ref[...]
    gate_ref[...] = _route(logits)


def _hg_out_kernel(of_ref, ob_ref, g_ref, on_ref, x_ref, mod_ref, nw2_ref, wo_ref, wr_ref, br_ref,
                   x1_ref, h2_ref, gate_ref):
    o = of_ref[...] + ob_ref[...]
    parts = []
    for h in range(HG_HEADS):
        oh = o[:, h * HG_DK:(h + 1) * HG_DK]
        parts.append(oh * lax.rsqrt(jnp.mean(oh * oh, axis=-1, keepdims=True) + EPS) * on_ref[...])
    y = jnp.concatenate(parts, axis=1) * _silu(g_ref[...].astype(F32))
    _mixer_tail(y.astype(BF16), x_ref, mod_ref, nw2_ref, wo_ref, wr_ref, br_ref, x1_ref, h2_ref, gate_ref)


def _at_out_kernel(a_ref, x_ref, mod_ref, nw2_ref, wo_ref, wr_ref, br_ref, x1_ref, h2_ref, gate_ref):
    _mixer_tail(a_ref[...], x_ref, mod_ref, nw2_ref, wo_ref, wr_ref, br_ref, x1_ref, h2_ref, gate_ref)


def _mixer_out(kernel_fn, name, mix_inputs, mix_specs, x, mod, nw2, w_out, w_router, b_router):
    tile = lambda i: (i, 0)
    fixed2 = lambda i: (0, 0)
    blk = pl.BlockSpec((TM, D_MODEL), tile)
    return pl.pallas_call(
        kernel_fn,
        out_shape=(
            jax.ShapeDtypeStruct((N_TOK, D_MODEL), F32),
            jax.ShapeDtypeStruct((N_TOK, D_MODEL), BF16),
            jax.ShapeDtypeStruct((N_TOK, ROUTER_LANES), F32),
        ),
        grid=(N_TILES,),
        in_specs=list(mix_specs) + [
            blk,
            pl.BlockSpec(mod.shape, fixed2),
            pl.BlockSpec((1, D_MODEL), fixed2),
            pl.BlockSpec(w_out.shape, fixed2),
            pl.BlockSpec(w_router.shape, fixed2),
            pl.BlockSpec((1, ROUTER_LANES), fixed2),
        ],
        out_specs=(blk, blk, pl.BlockSpec((TM, ROUTER_LANES), tile)),
        compiler_params=_params(40, "arbitrary"),
        name=name,
    )(*mix_inputs, x, mod, nw2, w_out, w_router, b_router)


def _moe_kernel(h_ref, gate_ref, w1_ref, w3_ref, w2_ref, o_ref):
    e = pl.program_id(1)
    w1 = w1_ref[0].astype(BF16)
    w3 = w3_ref[0].astype(BF16)
    w2 = w2_ref[0].astype(BF16)

    def sub(s, carry):
        rows = pl.ds(pl.multiple_of(s * MOE_SUB, MOE_SUB), MOE_SUB)
        h = h_ref[rows, :]
        gate = gate_ref[rows, :]
        lane = lax.broadcasted_iota(jnp.int32, gate.shape, 1)
        ge = jnp.sum(jnp.where(lane == e, gate, 0.0), axis=-1, keepdims=True)
        a = jnp.dot(h, w1, preferred_element_type=F32)
        b = jnp.dot(h, w3, preferred_element_type=F32)
        hid = (_silu(a) * b * ge).astype(BF16)
        y = jnp.dot(hid, w2, preferred_element_type=F32)

        @pl.when(e == 0)
        def _():
            o_ref[rows, :] = y

        @pl.when(e != 0)
        def _():
            o_ref[rows, :] += y

        return carry

    lax.fori_loop(0, MOE_TM // MOE_SUB, sub, 0)


def _moe(h2, gate, w1, w3, w2):
    return pl.pallas_call(
        _moe_kernel,
        out_shape=jax.ShapeDtypeStruct((N_TOK, D_MODEL), F32),
        grid=(N_TOK // MOE_TM, N_EXPERTS),
        in_specs=[
            pl.BlockSpec((MOE_TM, D_MODEL), lambda m, e: (m, 0)),
            pl.BlockSpec((MOE_TM, ROUTER_LANES), lambda m, e: (m, 0)),
            pl.BlockSpec((1, D_MODEL, D_EXPERT), lambda m, e: (e, 0, 0)),
            pl.BlockSpec((1, D_MODEL, D_EXPERT), lambda m, e: (e, 0, 0)),
            pl.BlockSpec((1, D_EXPERT, D_MODEL), lambda m, e: (e, 0, 0)),
        ],
        out_specs=pl.BlockSpec((MOE_TM, D_MODEL), lambda m, e: (m, 0)),
        compiler_params=_params(56, "arbitrary", "arbitrary"),
        name="moe_experts",
    )(h2, gate, w1, w3, w2)


def _moe_res_kernel(x_ref, y_ref, mod_ref, o_ref):
    cond = _tile_cond(pl.program_id(0))
    o_ref[...] = x_ref[...] + _mod_row(mod_ref, cond, 5) * y_ref[...]


def _moe_res_final_kernel(x_ref, y_ref, mod_ref, fn_ref, o_ref):
    cond = _tile_cond(pl.program_id(0))
    x = x_ref[...] + _mod_row(mod_ref, cond, 5) * y_ref[...]
    o_ref[...] = x * lax.rsqrt(jnp.mean(x * x, axis=-1, keepdims=True) + EPS) * fn_ref[...]


def _moe_res(x1, y, mod, final_norm=None):
    tile = lambda i: (i, 0)
    fixed2 = lambda i: (0, 0)
    blk = pl.BlockSpec((TM, D_MODEL), tile)
    specs = [blk, blk, pl.BlockSpec(mod.shape, fixed2)]
    args = [x1, y, mod]
    if final_norm is not None:
        specs.append(pl.BlockSpec((1, D_MODEL), fixed2))
        args.append(final_norm)
    return pl.pallas_call(
        _moe_res_kernel if final_norm is None else _moe_res_final_kernel,
        out_shape=jax.ShapeDtypeStruct((N_TOK, D_MODEL), F32),
        grid=(N_TILES,),
        in_specs=specs,
        out_specs=blk,
        compiler_params=_params(32, "arbitrary"),
        name="moe_residual",
    )(*args)


def _swap_rotary_halves(x):
    n = x.shape[-1]
    lane = lax.broadcasted_iota(jnp.int32, x.shape, 1)
    quarter = ROPE_HALF // 2
    return jnp.where((lane % ROPE_HALF) < quarter, pltpu.roll(x, n - quarter, 1), pltpu.roll(x, quarter, 1))


def _at_in_kernel(x_ref, mod_ref, nw_ref, w_ref, cos_ref, sin_ref, q_ref, k_ref, v_ref):
    cond = _tile_cond(pl.program_id(0))
    h = _norm_mod(x_ref[...], nw_ref[...], _mod_row(mod_ref, cond, 0), _mod_row(mod_ref, cond, 1)).astype(BF16)
    nq = ATT_Q_HEADS * ATT_HEAD_DIM
    nk = ATT_KV_HEADS * ATT_HEAD_DIM
    cos = cos_ref[...]
    sin = sin_ref[...]

    def rope(x):
        reps = x.shape[-1] // LANES
        return x * jnp.concatenate([cos] * reps, axis=1) + _swap_rotary_halves(x) * jnp.concatenate([sin] * reps, axis=1)

    q_ref[...] = rope(jnp.dot(h, w_ref[:, :nq], preferred_element_type=F32)).astype(BF16)
    k_ref[...] = rope(jnp.dot(h, w_ref[:, nq:nq + nk], preferred_element_type=F32))
    v_ref[...] = jnp.dot(h, w_ref[:, nq + nk:], preferred_element_type=F32)


def _rope_tables():
    pos = jnp.arange(DEC_SEQ)
    t_row = (pos // GRID_W).astype(F32)
    t_col = (pos % GRID_W).astype(F32)
    inv = ROPE_BASE ** (-jnp.arange(0, ROPE_HALF, 2, dtype=F32) / ROPE_HALF)
    lane = jnp.arange(LANES)
    j = lane % ATT_HEAD_DIM
    freq = inv[(j % ROPE_HALF) % (ROPE_HALF // 2)]
    t = jnp.where((j < ROPE_HALF)[None, :], t_row[:, None], t_col[:, None])
    ang = t * freq[None, :]
    sign = jnp.where((j % ROPE_HALF) < ROPE_HALF // 2, -1.0, 1.0).astype(F32)
    cos = jnp.concatenate([jnp.ones((TM, LANES), F32), jnp.cos(ang)], axis=0)
    sin = jnp.concatenate([jnp.zeros((TM, LANES), F32), jnp.sin(ang) * sign[None, :]], axis=0)
    return cos, sin


def _at_in(x, mod, nw, w_in, cos, sin):
    tile = lambda i: (i, 0)
    fixed2 = lambda i: (0, 0)
    rope_tile = lambda i: (jnp.where(i < PROMPT_TILES, 0, 1 + (i - PROMPT_TILES) % TILES_PER_DEC_SEQ), 0)
    nk = ATT_KV_HEADS * ATT_HEAD_DIM
    return pl.pallas_call(
        _at_in_kernel,
        out_shape=(
            jax.ShapeDtypeStruct((N_TOK, D_MODEL), BF16),
            jax.ShapeDtypeStruct((N_TOK, nk), F32),
            jax.ShapeDtypeStruct((N_TOK, nk), F32),
        ),
        grid=(N_TILES,),
        in_specs=[
            pl.BlockSpec((TM, D_MODEL), tile),
            pl.BlockSpec(mod.shape, fixed2),
            pl.BlockSpec((1, D_MODEL), fixed2),
            pl.BlockSpec(w_in.shape, fixed2),
            pl.BlockSpec((TM, LANES), rope_tile),
            pl.BlockSpec((TM, LANES), rope_tile),
        ],
        out_specs=(pl.BlockSpec((TM, D_MODEL), tile), pl.BlockSpec((TM, nk), tile), pl.BlockSpec((TM, nk), tile)),
        compiler_params=_params(40, "arbitrary"),
        name="attn_in_proj",
    )(x, mod, nw, w_in, cos, sin)


def _attend(q, k_all, v_all, mask, sink_ref, o_ref):
    group_lanes = ATT_GROUP * ATT_HEAD_DIM
    scale = ATT_HEAD_DIM ** -0.5
    nt = (((1,), (1,)), ((), ()))
    lane = lax.broadcasted_iota(jnp.int32, (q.shape[0], group_lanes), 1)
    for hk in range(ATT_KV_HEADS):
        kh = k_all[:, hk * ATT_HEAD_DIM:(hk + 1) * ATT_HEAD_DIM]
        vh = v_all[:, hk * ATT_HEAD_DIM:(hk + 1) * ATT_HEAD_DIM]
        kt = jnp.concatenate([kh] * ATT_GROUP, axis=1)
        vt = jnp.concatenate([vh] * ATT_GROUP, axis=1)
        qg = q[:, hk * group_lanes:(hk + 1) * group_lanes]
        acc = jnp.zeros((q.shape[0], group_lanes), F32)
        for g in range(ATT_GROUP):
            mine = (lane // ATT_HEAD_DIM) == g
            qm = jnp.where(mine, qg, jnp.zeros_like(qg))
            s = lax.dot_general(qm, kt, nt, preferred_element_type=F32) * scale
            if mask is not None:
                s = jnp.where(mask, s, -jnp.inf)
            sink = sink_ref[hk * ATT_GROUP + g]
            m = jnp.maximum(jnp.max(s, axis=-1, keepdims=True), sink)
            p = jnp.exp(s - m)
            denom = jnp.sum(p, axis=-1, keepdims=True) + jnp.exp(sink - m)
            o = jnp.dot(p.astype(BF16), vt, preferred_element_type=F32) / denom
            acc = acc + jnp.where(mine, o, 0.0)
        o_ref[:, hk * group_lanes:(hk + 1) * group_lanes] = acc.astype(BF16)


def _ctx_attn_kernel(sink_ref, q_ref, k_ref, v_ref, o_ref):
    _attend(q_ref[...], k_ref[...].astype(BF16), v_ref[...].astype(BF16), None, sink_ref, o_ref)


def _lat_attn_kernel(sink_ref, q_ref, kp_ref, kc_ref, kn_ref, vp_ref, vc_ref, vn_ref, ck_ref, cv_ref, o_ref):
    jb = pl.program_id(1)
    k_all = jnp.concatenate([kp_ref[...], kc_ref[...], kn_ref[...], ck_ref[0]], axis=0).astype(BF16)
    v_all = jnp.concatenate([vp_ref[...], vc_ref[...], vn_ref[...], cv_ref[0]], axis=0).astype(BF16)
    nkeys = 3 * BLOCK + PAST_LEN
    qi = lax.broadcasted_iota(jnp.int32, (BLOCK, nkeys), 0)
    kj = lax.broadcasted_iota(jnp.int32, (BLOCK, nkeys), 1)
    qpos = jb * BLOCK + qi
    kpos = (jb - 1) * BLOCK + kj
    local_ok = (jnp.abs(qpos - kpos) <= WINDOW) & (kpos >= 0) & (kpos < DEC_SEQ)
    mask = (kj >= 3 * BLOCK) | local_ok
    _attend(q_ref[...], k_all, v_all, mask, sink_ref, o_ref)


def _attention(q, k, v, cache_k, cache_v, sink):
    nk = ATT_KV_HEADS * ATT_HEAD_DIM
    smem = pl.BlockSpec(memory_space=pltpu.SMEM)
    ctx = pl.pallas_call(
        _ctx_attn_kernel,
        out_shape=jax.ShapeDtypeStruct((N_PROMPT, D_MODEL), BF16),
        grid=(BATCH,),
        in_specs=[
            smem,
            pl.BlockSpec((SEQ, D_MODEL), lambda b: (b, 0)),
            pl.BlockSpec((SEQ, nk), lambda b: (b, 0)),
            pl.BlockSpec((SEQ, nk), lambda b: (b, 0)),
        ],
        out_specs=pl.BlockSpec((SEQ, D_MODEL), lambda b: (b, 0)),
        compiler_params=_params(40, "arbitrary"),
        name="context_attention",
    )(sink, q, k, v)

    nb = DEC_SEQ // BLOCK
    base = N_PROMPT // BLOCK
    cur = lambda b, j: (base + b * nb + j, 0)
    prev = lambda b, j: (base + b * nb + jnp.maximum(j - 1, 0), 0)
    nxt = lambda b, j: (base + b * nb + jnp.minimum(j + 1, nb - 1), 0)
    kv_blk = lambda f: pl.BlockSpec((BLOCK, nk), f)
    cache_blk = pl.BlockSpec((1, PAST_LEN, nk), lambda b, j: (b, 0, 0))
    lat = pl.pallas_call(
        _lat_attn_kernel,
        out_shape=jax.ShapeDtypeStruct((DEC_BATCH * DEC_SEQ, D_MODEL), BF16),
        grid=(DEC_BATCH, nb),
        in_specs=[
            smem,
            pl.BlockSpec((BLOCK, D_MODEL), cur),
            kv_blk(prev), kv_blk(cur), kv_blk(nxt),
            kv_blk(prev), kv_blk(cur), kv_blk(nxt),
            cache_blk, cache_blk,
        ],
        out_specs=pl.BlockSpec((BLOCK, D_MODEL), lambda b, j: (b * nb + j, 0)),
        compiler_params=_params(40, "arbitrary", "arbitrary"),
        name="latent_attention",
    )(sink, q, k, k, k, v, v, v, cache_k, cache_v)
    return jnp.concatenate([ctx, lat], axis=0)


def kernel(x_prompt, x_sample, state_hgrn, cache_k, cache_v, c, c_ctx, ada_w, ada_b, norm_w, hg_w_in,
           hg_lb_logits, hg_onorm, hg_w_out, at_w_in, at_sink, at_w_out, moe_w_group, moe_b_group,
           moe_w_expert, moe_b_expert, moe_w1, moe_w3, moe_w2, final_norm):
    x = jnp.concatenate([x_prompt.reshape(N_PROMPT, D_MODEL), x_sample.reshape(-1, D_MODEL)], axis=0)
    cond = jnp.concatenate([c_ctx[None, :], c, jnp.zeros((N_COND - 1 - DEC_BATCH, D_MODEL), F32)], axis=0)
    mod = _ada(cond, ada_w, ada_b)
    nk = ATT_KV_HEADS * ATT_HEAD_DIM

    def router_params(i):
        pad = jnp.zeros((D_MODEL, ROUTER_LANES - N_EXPERTS - N_GROUPS), F32)
        w = jnp.concatenate([moe_w_expert[i], moe_w_group[i], pad], axis=1)
        b = jnp.concatenate([moe_b_expert[i], moe_b_group[i], pad[0]])[None, :]
        return w, b

    tile = lambda i: (i, 0)
    blk = pl.BlockSpec((TM, D_MODEL), tile)

    q, v, g, lff, kf, lfb, kb = _hg_in(x, mod[0], norm_w[0, 0][None, :], hg_w_in[0].astype(BF16), hg_lb_logits)
    o_f, o_b, state_new = _gla(q, v, lff, kf, lfb, kb, state_hgrn)
    wr, br = router_params(0)
    x1, h2, gate = _mixer_out(
        _hg_out_kernel, "hgrn_out_route", (o_f, o_b, g, hg_onorm[0][None, :]),
        (blk, blk, blk, pl.BlockSpec((1, HG_DK), lambda i: (0, 0))),
        x, mod[0], norm_w[0, 1][None, :], hg_w_out[0].astype(BF16), wr, br)
    x = _moe_res(x1, _moe(h2, gate, moe_w1[0], moe_w3[0], moe_w2[0]), mod[0])

    cos, sin = _rope_tables()
    qa, ka, va = _at_in(x, mod[1], norm_w[1, 0][None, :], at_w_in[0].astype(BF16), cos, sin)
    attn = _attention(qa, ka, va, cache_k[:, 0].reshape(DEC_BATCH, PAST_LEN, nk),
                      cache_v[:, 0].reshape(DEC_BATCH, PAST_LEN, nk), at_sink[0])
    wr, br = router_params(1)
    x1, h2, gate = _mixer_out(
        _at_out_kernel, "attn_out_route", (attn,), (blk,),
        x, mod[1], norm_w[1, 1][None, :], at_w_out[0].astype(BF16), wr, br)
    y = _moe_res(x1, _moe(h2, gate, moe_w1[1], moe_w3[1], moe_w2[1]), mod[1], final_norm[None, :])

    y_prompt = y[:N_PROMPT].reshape(BATCH, SEQ, D_MODEL)
    y_sample = y[N_PROMPT:].reshape(DEC_BATCH, DEC_SEQ, D_MODEL)
    cache_k_new = ka[:N_PROMPT].reshape(BATCH, 1, SEQ, ATT_KV_HEADS, ATT_HEAD_DIM)
    cache_v_new = va[:N_PROMPT].reshape(BATCH, 1, SEQ, ATT_KV_HEADS, ATT_HEAD_DIM)
    return (y_prompt, y_sample, state_new, cache_k_new, cache_v_new)
```

```python
import functools

import jax
import jax.numpy as jnp
from jax import lax
from jax.experimental import pallas as pl
from jax.experimental.pallas import tpu as pltpu
from jax.experimental.pallas import tpu_sc as plsc

F32 = jnp.float32
BF16 = jnp.bfloat16

D_MODEL = 1024
BATCH = 16
SEQ = 256
DEC_BATCH = 2
DEC_SEQ = 1024
PAST_LEN = 512
GRID_W = 64
HG_HEADS = 8
HG_DK = 128
CHUNK = 16
ATT_HEAD_DIM = 64
ATT_Q_HEADS = 16
ATT_KV_HEADS = 4
ATT_GROUP = 4
WINDOW = 128
BLOCK = 128
ROPE_HALF = 32
ROPE_BASE = 10000.0
N_GROUPS = 4
EXPERTS_PER_GROUP = 8
N_EXPERTS = 32
D_EXPERT = 256
EPS = 1e-6

N_PROMPT = BATCH * SEQ
N_TOK = N_PROMPT + DEC_BATCH * DEC_SEQ
TM = 256
N_TILES = N_TOK // TM
PROMPT_TILES = N_PROMPT // TM
TILES_PER_DEC_SEQ = DEC_SEQ // TM
LANES = 128
N_COND = 8
ROUTER_LANES = 128
GLA_UNROLL = 2
DECAY_CLAMP = 60.0
MOE_TILE = 128
MOE_ROWS = 2 * N_TOK + N_EXPERTS * MOE_TILE
MIB = 1024 * 1024


def _params(vmem_mib, *semantics):
    return pltpu.CompilerParams(dimension_semantics=semantics, vmem_limit_bytes=vmem_mib * MIB)


def _tile_cond(i):
    return jnp.where(i < PROMPT_TILES, 0, 1 + (i - PROMPT_TILES) // TILES_PER_DEC_SEQ)


def _mod_row(mod_ref, cond, which):
    return mod_ref[pl.ds(cond, 1), which * D_MODEL:(which + 1) * D_MODEL]


def _norm_mod(x, nw, shift, scale):
    y = x * lax.rsqrt(jnp.mean(x * x, axis=-1, keepdims=True) + EPS)
    return (y * nw) * (1.0 + scale) + shift


def _silu(x):
    return x * jax.nn.sigmoid(x)


def _ada_kernel(c_ref, w_ref, b_ref, o_ref):
    s = _silu(c_ref[...])
    o_ref[0] = jnp.dot(s, w_ref[0], precision=lax.Precision.HIGHEST, preferred_element_type=F32) + b_ref[0]


def _ada(cond, ada_w, ada_b):
    depth, _, n = ada_w.shape
    tn = 1536
    return pl.pallas_call(
        _ada_kernel,
        out_shape=jax.ShapeDtypeStruct((depth, N_COND, n), F32),
        grid=(depth, n // tn),
        in_specs=[
            pl.BlockSpec((N_COND, D_MODEL), lambda l, j: (0, 0)),
            pl.BlockSpec((1, D_MODEL, tn), lambda l, j: (l, 0, j)),
            pl.BlockSpec((1, 1, tn), lambda l, j: (l, 0, j)),
        ],
        out_specs=pl.BlockSpec((1, N_COND, tn), lambda l, j: (l, 0, j)),
        compiler_params=_params(40, "arbitrary", "arbitrary"),
        name="ada_modulation",
    )(cond, ada_w, ada_b.reshape(depth, 1, n))


def _hg_in_kernel(x_ref, mod_ref, nw_ref, w_ref, lbl_ref, q_ref, v_ref, g_ref, lff_ref, kf_ref, lfb_ref, kb_ref):
    cond = _tile_cond(pl.program_id(0))
    h = _norm_mod(x_ref[...], nw_ref[...], _mod_row(mod_ref, cond, 0), _mod_row(mod_ref, cond, 1)).astype(BF16)

    def proj(c):
        return jnp.dot(h, w_ref[:, c * D_MODEL:(c + 1) * D_MODEL], preferred_element_type=F32)

    l0, l1, l2 = lbl_ref[0], lbl_ref[1], lbl_ref[2]
    m = jnp.maximum(jnp.maximum(l0, l1), l2)
    e0, e1, e2 = jnp.exp(l0 - m), jnp.exp(l1 - m), jnp.exp(l2 - m)
    lb = e0 / (e0 + e1 + e2)

    q_ref[...] = proj(0).astype(BF16)
    v_ref[...] = proj(1).astype(BF16)
    for d, (lf_ref, k_ref) in enumerate(((lff_ref, kf_ref), (lfb_ref, kb_ref))):
        lbd = lb[d:d + 1, :]
        f = lbd + (1.0 - lbd) * jax.nn.sigmoid(proj(2 + d))
        lf_ref[...] = jnp.log(f)
        k_ref[...] = (1.0 - f).astype(BF16)
    g_ref[...] = proj(4).astype(BF16)


def _hg_in(x, mod, nw, w_in, lb_logits):
    tile = lambda i: (i, 0)
    fixed2 = lambda i: (0, 0)
    bf = jax.ShapeDtypeStruct((N_TOK, D_MODEL), BF16)
    ff = jax.ShapeDtypeStruct((N_TOK, D_MODEL), F32)
    blk = pl.BlockSpec((TM, D_MODEL), tile)
    return pl.pallas_call(
        _hg_in_kernel,
        out_shape=(bf, bf, bf, ff, bf, ff, bf),
        grid=(N_TILES,),
        in_specs=[
            blk,
            pl.BlockSpec(mod.shape, fixed2),
            pl.BlockSpec((1, D_MODEL), fixed2),
            pl.BlockSpec(w_in.shape, fixed2),
            pl.BlockSpec(lb_logits.shape, lambda i: (0, 0, 0)),
        ],
        out_specs=(blk,) * 7,
        compiler_params=_params(56, "arbitrary"),
        name="hgrn_in_proj",
    )(x, mod, nw, w_in, lb_logits)


def _gla_direction(reverse, q_ref, v_ref, lf_ref, k_ref, st_ref, sw_ref, o_ref, qd_scr, kd_scr, ki_scr, cd_scr):
    lf = lf_ref[...]
    r = lax.broadcasted_iota(jnp.int32, (TM, TM), 0)
    c = lax.broadcasted_iota(jnp.int32, (TM, TM), 1)
    same = (r // CHUNK) == (c // CHUNK)
    tri = (same & ((c >= r) if reverse else (c <= r))).astype(BF16)
    hi = lf.astype(BF16)
    lo = (lf - hi.astype(F32)).astype(BF16)
    b = jnp.dot(tri, hi, preferred_element_type=F32) + jnp.dot(tri, lo, preferred_element_type=F32)
    b3 = b.reshape(TM // CHUNK, CHUNK, D_MODEL)
    edge = 0 if reverse else CHUNK - 1
    tot = jnp.broadcast_to(b3[:, edge:edge + 1, :], b3.shape).reshape(TM, D_MODEL)
    kf = k_ref[...].astype(F32)
    qd_scr[...] = (q_ref[...].astype(F32) * jnp.exp(b)).astype(BF16)
    kd_scr[...] = (kf * jnp.exp(tot - b)).astype(BF16)
    ki_scr[...] = (kf * jnp.exp(jnp.minimum(-b, DECAY_CLAMP))).astype(BF16)
    cd_scr[...] = jnp.exp(tot)

    tr = lax.broadcasted_iota(jnp.int32, (CHUNK, CHUNK), 0)
    tc = lax.broadcasted_iota(jnp.int32, (CHUNK, CHUNK), 1)
    keep = (tc >= tr) if reverse else (tc <= tr)
    nt = (((1,), (1,)), ((), ()))
    tn = (((0,), (0,)), ((), ()))

    heads = [slice(h * HG_DK, (h + 1) * HG_DK) for h in range(HG_HEADS)]
    for h in range(HG_HEADS):
        sw_ref[0, h] = st_ref[0, h].T.astype(BF16)

    def chunk(ci, src, dst):
        row0 = pl.multiple_of(ci * CHUNK, CHUNK)
        rows = pl.ds(row0, CHUNK)
        for h, cols in enumerate(heads):
            ut = lax.dot_general(v_ref[rows, cols], kd_scr[rows, cols], tn, preferred_element_type=F32)
            new = st_ref[src, h] * cd_scr[pl.ds(row0, 1), cols] + ut
            st_ref[dst, h] = new
            sw_ref[dst, h] = new.T.astype(BF16)
        a = [lax.dot_general(qd_scr[rows, cols], ki_scr[rows, cols], nt, preferred_element_type=F32) for cols in heads]
        inter = [jnp.dot(qd_scr[rows, cols], sw_ref[src, h], preferred_element_type=F32)
                 for h, cols in enumerate(heads)]
        for h, cols in enumerate(heads):
            am = jnp.where(keep, a[h], 0.0).astype(BF16)
            o_ref[rows, cols] = jnp.dot(am, v_ref[rows, cols], preferred_element_type=F32) + inter[h]

    n_chunks = TM // CHUNK

    def chunk_group(it, carry):
        for p in range(GLA_UNROLL):
            step = GLA_UNROLL * it + p
            chunk((n_chunks - 1 - step) if reverse else step, p % 2, 1 - p % 2)
        return carry

    lax.fori_loop(0, n_chunks // GLA_UNROLL, chunk_group, 0)


def _gla_kernel(qf_ref, vf_ref, lff_ref, kf_ref, qb_ref, vb_ref, lfb_ref, kb_ref, s0_ref,
                of_ref, ob_ref, sout_ref, stf_scr, stb_scr, sw_scr, qd_scr, kd_scr, ki_scr, cd_scr):
    i = pl.program_id(0)
    is_prompt = i < PROMPT_TILES
    first = jnp.logical_or(is_prompt, (i - PROMPT_TILES) % TILES_PER_DEC_SEQ == 0)

    @pl.when(first)
    def _():
        keep0 = jnp.where(is_prompt, 0.0, 1.0)
        for h in range(HG_HEADS):
            stf_scr[0, h] = s0_ref[0, 0, 0, h].T * keep0
            stb_scr[0, h] = s0_ref[0, 0, 1, h].T * keep0

    _gla_direction(False, qf_ref, vf_ref, lff_ref, kf_ref, stf_scr, sw_scr, of_ref, qd_scr, kd_scr, ki_scr, cd_scr)
    _gla_direction(True, qb_ref, vb_ref, lfb_ref, kb_ref, stb_scr, sw_scr, ob_ref, qd_scr, kd_scr, ki_scr, cd_scr)

    @pl.when(is_prompt)
    def _():
        for h in range(HG_HEADS):
            sout_ref[0, 0, 0, h] = stf_scr[0, h].T
            sout_ref[0, 0, 1, h] = stb_scr[0, h].T


def _gla(q, v, lff, kf, lfb, kb, state_hgrn):
    def fwd_tile(i):
        return (i, 0)

    def bwd_tile(i):
        j = (i - PROMPT_TILES) % TILES_PER_DEC_SEQ
        return (jnp.where(i < PROMPT_TILES, i, i - j + (TILES_PER_DEC_SEQ - 1 - j)), 0)

    def s0_idx(i):
        return (jnp.maximum(i - PROMPT_TILES, 0) // TILES_PER_DEC_SEQ, 0, 0, 0, 0, 0)

    def sout_idx(i):
        return (jnp.minimum(i, PROMPT_TILES - 1), 0, 0, 0, 0, 0)

    f_blk = pl.BlockSpec((TM, D_MODEL), fwd_tile)
    b_blk = pl.BlockSpec((TM, D_MODEL), bwd_tile)
    st_blk = (1, 1, 2, HG_HEADS, HG_DK, HG_DK)
    return pl.pallas_call(
        _gla_kernel,
        out_shape=(
            jax.ShapeDtypeStruct((N_TOK, D_MODEL), F32),
            jax.ShapeDtypeStruct((N_TOK, D_MODEL), F32),
            jax.ShapeDtypeStruct((BATCH,) + st_blk[1:], F32),
        ),
        grid=(N_TILES,),
        in_specs=[f_blk, f_blk, f_blk, f_blk, b_blk, b_blk, b_blk, b_blk, pl.BlockSpec(st_blk, s0_idx)],
        out_specs=(f_blk, b_blk, pl.BlockSpec(st_blk, sout_idx)),
        scratch_shapes=[
            pltpu.VMEM((2, HG_HEADS, HG_DK, HG_DK), F32),
            pltpu.VMEM((2, HG_HEADS, HG_DK, HG_DK), F32),
            pltpu.VMEM((2, HG_HEADS, HG_DK, HG_DK), BF16),
            pltpu.VMEM((TM, D_MODEL), BF16),
            pltpu.VMEM((TM, D_MODEL), BF16),
            pltpu.VMEM((TM, D_MODEL), BF16),
            pltpu.VMEM((TM, D_MODEL), F32),
        ],
        compiler_params=_params(48, "arbitrary"),
        name="hgrn_recurrence",
    )(q, v, lff, kf, q, v, lfb, kb, state_hgrn)


def _route(logits):
    lane = lax.broadcasted_iota(jnp.int32, logits.shape, 1)
    big = jnp.int32(ROUTER_LANES)
    neg = jnp.float32(-jnp.inf)
    is_group = (lane >= N_EXPERTS) & (lane < N_EXPERTS + N_GROUPS)
    gl = jnp.where(is_group, logits, neg)
    gmax = jnp.max(gl, axis=-1, keepdims=True)
    gsum = jnp.sum(jnp.exp(gl - gmax), axis=-1, keepdims=True)
    p_g = 1.0 / gsum
    g_sel = jnp.min(jnp.where(is_group & (gl == gmax), lane - N_EXPERTS, big), axis=-1, keepdims=True)
    in_sel = (lane < N_EXPERTS) & ((lane // EXPERTS_PER_GROUP) == g_sel)
    el = jnp.where(in_sel, logits, neg)
    m1 = jnp.max(el, axis=-1, keepdims=True)
    i1 = jnp.min(jnp.where(in_sel & (el == m1), lane, big), axis=-1, keepdims=True)
    el2 = jnp.where(lane == i1, neg, el)
    m2 = jnp.max(el2, axis=-1, keepdims=True)
    i2 = jnp.min(jnp.where(in_sel & (lane != i1) & (el2 == m2), lane, big), axis=-1, keepdims=True)
    e2 = jnp.exp(m2 - m1)
    return i1, i2, p_g / (1.0 + e2), p_g * e2 / (1.0 + e2)


def _pack_bf16_pairs(x):
    half = x.shape[-1] // 2
    bits = pltpu.bitcast(x.astype(BF16).astype(F32), jnp.uint32)
    return (bits[:, half:] & jnp.uint32(0xFFFF0000)) | (bits[:, :half] >> 16)


def _unpack_bf16_pairs(words):
    lo = pltpu.bitcast(words << 16, F32).astype(BF16)
    hi = pltpu.bitcast(words & jnp.uint32(0xFFFF0000), F32).astype(BF16)
    return lo, hi


def _mixer_tail(mix_bf16, x_ref, mod_ref, nw2_ref, wo_ref, wr_ref, br_ref,
                x1_ref, h2p_ref, ri_ref, rw_ref, cnt_ref, carry_scr):
    i = pl.program_id(0)
    cond = _tile_cond(i)
    out = jnp.dot(mix_bf16, wo_ref[...], preferred_element_type=F32)
    x1 = x_ref[...] + _mod_row(mod_ref, cond, 2) * out
    x1_ref[...] = x1
    h2 = _norm_mod(x1, nw2_ref[...], _mod_row(mod_ref, cond, 3), _mod_row(mod_ref, cond, 4))
    h2p_ref[...] = _pack_bf16_pairs(h2)
    logits = jnp.dot(h2, wr_ref[...], precision=lax.Precision.HIGHEST, preferred_element_type=F32) + br_ref[...]
    i1, i2, w1, w2 = _route(logits)

    @pl.when(i == 0)
    def _():
        carry_scr[...] = jnp.zeros_like(carry_scr)

    lane = lax.broadcasted_iota(jnp.int32, logits.shape, 1)
    chosen = ((lane == i1) | (lane == i2)).astype(BF16)
    r = lax.broadcasted_iota(jnp.int32, (TM, TM), 0)
    c = lax.broadcasted_iota(jnp.int32, (TM, TM), 1)
    before = jnp.dot((c < r).astype(BF16), chosen, preferred_element_type=F32) + carry_scr[...]
    r1 = jnp.sum(jnp.where(lane == i1, before, 0.0), axis=-1, keepdims=True).astype(jnp.int32)
    r2 = jnp.sum(jnp.where(lane == i2, before, 0.0), axis=-1, keepdims=True).astype(jnp.int32)
    total = carry_scr[...] + jnp.sum(chosen.astype(F32), axis=0, keepdims=True)
    carry_scr[...] = total
    cnt_ref[...] = total
    ri_ref[...] = jnp.where(lane == 0, i1, jnp.where(lane == 1, i2, jnp.where(lane == 2, r1, r2)))
    rw_ref[...] = jnp.where(lane == 0, w1, w2)


def _hg_out_kernel(of_ref, ob_ref, g_ref, on_ref, *rest):
    o = of_ref[...] + ob_ref[...]
    parts = []
    for h in range(HG_HEADS):
        oh = o[:, h * HG_DK:(h + 1) * HG_DK]
        parts.append(oh * lax.rsqrt(jnp.mean(oh * oh, axis=-1, keepdims=True) + EPS) * on_ref[...])
    y = jnp.concatenate(parts, axis=1) * _silu(g_ref[...].astype(F32))
    _mixer_tail(y.astype(BF16), *rest)


def _at_out_kernel(a_ref, *rest):
    _mixer_tail(a_ref[...], *rest)


def _mixer_out(kernel_fn, name, mix_inputs, mix_specs, x, mod, nw2, w_out, w_router, b_router):
    tile = lambda i: (i, 0)
    fixed2 = lambda i: (0, 0)
    blk = pl.BlockSpec((TM, D_MODEL), tile)
    lanes_blk = pl.BlockSpec((TM, ROUTER_LANES), tile)
    return pl.pallas_call(
        kernel_fn,
        out_shape=(
            jax.ShapeDtypeStruct((N_TOK, D_MODEL), F32),
            jax.ShapeDtypeStruct((N_TOK, D_MODEL // 2), jnp.uint32),
            jax.ShapeDtypeStruct((N_TOK, ROUTER_LANES), jnp.int32),
            jax.ShapeDtypeStruct((N_TOK, ROUTER_LANES), F32),
            jax.ShapeDtypeStruct((1, ROUTER_LANES), F32),
        ),
        grid=(N_TILES,),
        in_specs=list(mix_specs) + [
            blk,
            pl.BlockSpec(mod.shape, fixed2),
            pl.BlockSpec((1, D_MODEL), fixed2),
            pl.BlockSpec(w_out.shape, fixed2),
            pl.BlockSpec(w_router.shape, fixed2),
            pl.BlockSpec((1, ROUTER_LANES), fixed2),
        ],
        out_specs=(blk, pl.BlockSpec((TM, D_MODEL // 2), tile), lanes_blk, lanes_blk,
                   pl.BlockSpec((1, ROUTER_LANES), fixed2)),
        scratch_shapes=[pltpu.VMEM((1, ROUTER_LANES), F32)],
        compiler_params=_params(40, "arbitrary"),
        name=name,
    )(*mix_inputs, x, mod, nw2, w_out, w_router, b_router)


def _moe_plan(route_i, counts):
    cnt = counts[0, :N_EXPERTS].astype(jnp.int32)
    padded = ((cnt + MOE_TILE - 1) // MOE_TILE) * MOE_TILE
    ends = jnp.cumsum(padded)
    offs = ends - padded
    experts = route_i[:, 0:2]
    pos = jnp.sum(jnp.where(experts[:, :, None] == jnp.arange(N_EXPERTS)[None, None, :], offs[None, None, :], 0),
                  axis=-1) + route_i[:, 2:4]
    tile_start = jnp.arange(MOE_ROWS // MOE_TILE, dtype=jnp.int32) * MOE_TILE
    tile_expert = jnp.minimum(jnp.sum(ends[None, :] <= tile_start[:, None], axis=1), N_EXPERTS - 1).astype(jnp.int32)
    tile_rows = jnp.clip(offs[tile_expert] + cnt[tile_expert] - tile_start, 0, MOE_TILE).astype(jnp.int32)
    tile_first = (tile_start == offs[tile_expert]).astype(jnp.int32)
    n_active = (ends[-1] // MOE_TILE).astype(jnp.int32).reshape(1)
    return pos.astype(jnp.int32), tile_expert, tile_rows, tile_first, n_active


SC_WINDOW = 128
SC_ROW_WORDS = 256


def _sc_mesh():
    return plsc.VectorSubcoreMesh(core_axis_name="c", subcore_axis_name="s")


def _sc_scatter_rows(x, idx_a, idx_b, n_out_rows):
    n = x.shape[0]

    @functools.partial(pl.kernel, out_type=jax.ShapeDtypeStruct((n_out_rows, SC_ROW_WORDS), x.dtype), mesh=_sc_mesh())
    def scatter(x_hbm, ia_hbm, ib_hbm, o_hbm):
        def body(x_vmem, ia_vmem, ib_vmem):
            pltpu.sync_copy(x_vmem, o_hbm.at[ia_vmem.at[0]])
            pltpu.sync_copy(x_vmem, o_hbm.at[ib_vmem.at[0]])

        idx_spec = pl.BlockSpec((1, SC_WINDOW), index_map=lambda i: (0, i))
        pltpu.emit_pipeline(
            body, grid=(n // SC_WINDOW,),
            in_specs=[pl.BlockSpec((SC_WINDOW, SC_ROW_WORDS), index_map=lambda i: (i, 0)), idx_spec, idx_spec],
            out_specs=[],
            core_axis_name=("c", "s"), dimension_semantics=(pltpu.PARALLEL,),
        )(x_hbm, ia_hbm, ib_hbm)

    return scatter(x, idx_a.reshape(1, n), idx_b.reshape(1, n))


def _sc_gather_rows(table, idx):
    n = idx.shape[0]

    @functools.partial(pl.kernel, out_type=jax.ShapeDtypeStruct((n, SC_ROW_WORDS), table.dtype), mesh=_sc_mesh())
    def gather(t_hbm, i_hbm, o_hbm):
        def body(i_vmem, o_vmem):
            pltpu.sync_copy(t_hbm.at[i_vmem.at[0]], o_vmem)

        pltpu.emit_pipeline(
            body, grid=(n // SC_WINDOW,),
            in_specs=[pl.BlockSpec((1, SC_WINDOW), index_map=lambda i: (0, i))],
            out_specs=[pl.BlockSpec((SC_WINDOW, SC_ROW_WORDS), index_map=lambda i: (i, 0))],
            core_axis_name=("c", "s"), dimension_semantics=(pltpu.PARALLEL,),
        )(i_hbm, o_hbm)

    return gather(table, idx.reshape(1, n))


def _half_row_index(pos_col):
    return (2 * pos_col[:, None] + jnp.arange(2, dtype=jnp.int32)[None, :]).reshape(-1)


def _ffn_kernel(te_ref, tr_ref, tf_ref, na_ref, xs_ref, w1_ref, w3_ref, w2_ref, ys_ref, w1_scr, w3_scr, w2_scr):
    t = pl.program_id(0)

    @pl.when(t < na_ref[0])
    def _():
        @pl.when(tf_ref[t] == 1)
        def _():
            w1_scr[...] = w1_ref[0].astype(BF16)
            w3_scr[...] = w3_ref[0].astype(BF16)
            w2_scr[...] = w2_ref[0].astype(BF16)

        words = xs_ref[...]
        row = lax.broadcasted_iota(jnp.int32, words.shape, 0)
        words = jnp.where(row < tr_ref[t], words, jnp.zeros_like(words))
        lo, hi = _unpack_bf16_pairs(words)
        half = D_MODEL // 2

        def up(w_scr):
            return (jnp.dot(lo, w_scr[:half, :], preferred_element_type=F32)
                    + jnp.dot(hi, w_scr[half:, :], preferred_element_type=F32))

        hid = (_silu(up(w1_scr)) * up(w3_scr)).astype(BF16)
        ys_ref[...] = _pack_bf16_pairs(jnp.dot(hid, w2_scr[...], preferred_element_type=F32))


def _ffn(xs, tile_expert, tile_rows, tile_first, n_active, w1, w3, w2):
    row_tile = lambda t, te, tr, tf, na: (jnp.minimum(t, na[0] - 1), 0)
    expert = lambda t, te, tr, tf, na: (te[t], 0, 0)
    return pl.pallas_call(
        _ffn_kernel,
        out_shape=jax.ShapeDtypeStruct((MOE_ROWS, D_MODEL // 2), jnp.uint32),
        grid_spec=pltpu.PrefetchScalarGridSpec(
            num_scalar_prefetch=4,
            grid=(MOE_ROWS // MOE_TILE,),
            in_specs=[
                pl.BlockSpec((MOE_TILE, D_MODEL // 2), row_tile),
                pl.BlockSpec((1, D_MODEL, D_EXPERT), expert),
                pl.BlockSpec((1, D_MODEL, D_EXPERT), expert),
                pl.BlockSpec((1, D_EXPERT, D_MODEL), expert),
            ],
            out_specs=pl.BlockSpec((MOE_TILE, D_MODEL // 2), row_tile),
            scratch_shapes=[
                pltpu.VMEM((D_MODEL, D_EXPERT), BF16),
                pltpu.VMEM((D_MODEL, D_EXPERT), BF16),
                pltpu.VMEM((D_EXPERT, D_MODEL), BF16),
            ],
        ),
        compiler_params=_params(32, "arbitrary"),
        name="moe_experts",
    )(tile_expert, tile_rows, tile_first, n_active, xs, w1, w3, w2)


def _moe_res(x_ref, ya_ref, yb_ref, rw_ref, mod_ref):
    cond = _tile_cond(pl.program_id(0))
    rw = rw_ref[...]
    wa = rw[:, 0:1]
    wb = rw[:, 1:2]
    a_lo, a_hi = _unpack_bf16_pairs(ya_ref[...])
    b_lo, b_hi = _unpack_bf16_pairs(yb_ref[...])
    y = jnp.concatenate([wa * a_lo.astype(F32) + wb * b_lo.astype(F32),
                         wa * a_hi.astype(F32) + wb * b_hi.astype(F32)], axis=1)
    return x_ref[...] + _mod_row(mod_ref, cond, 5) * y


def _moe_res_kernel(x_ref, ya_ref, yb_ref, rw_ref, mod_ref, o_ref):
    o_ref[...] = _moe_res(x_ref, ya_ref, yb_ref, rw_ref, mod_ref)


def _moe_res_final_kernel(x_ref, ya_ref, yb_ref, rw_ref, mod_ref, fn_ref, o_ref):
    x = _moe_res(x_ref, ya_ref, yb_ref, rw_ref, mod_ref)
    o_ref[...] = x * lax.rsqrt(jnp.mean(x * x, axis=-1, keepdims=True) + EPS) * fn_ref[...]


def _moe_combine(x1, y_pairs, route_w, mod, final_norm=None):
    tile = lambda i: (i, 0)
    fixed2 = lambda i: (0, 0)
    blk = pl.BlockSpec((TM, D_MODEL), tile)
    specs = [blk,
             pl.BlockSpec((TM, D_MODEL // 2), tile),
             pl.BlockSpec((TM, D_MODEL // 2), lambda i: (i + N_TILES, 0)),
             pl.BlockSpec((TM, ROUTER_LANES), tile),
             pl.BlockSpec(mod.shape, fixed2)]
    args = [x1, y_pairs, y_pairs, route_w, mod]
    if final_norm is not None:
        specs.append(pl.BlockSpec((1, D_MODEL), fixed2))
        args.append(final_norm)
    return pl.pallas_call(
        _moe_res_kernel if final_norm is None else _moe_res_final_kernel,
        out_shape=jax.ShapeDtypeStruct((N_TOK, D_MODEL), F32),
        grid=(N_TILES,),
        in_specs=specs,
        out_specs=blk,
        compiler_params=_params(32, "arbitrary"),
        name="moe_combine",
    )(*args)


def _moe_layer(x1, h2p, route_i, route_w, counts, mod, w1, w3, w2, final_norm=None):
    pos, tile_expert, tile_rows, tile_first, n_active = _moe_plan(route_i, counts)
    idx_a = _half_row_index(pos[:, 0])
    idx_b = _half_row_index(pos[:, 1])
    xs = _sc_scatter_rows(h2p.reshape(2 * N_TOK, SC_ROW_WORDS), idx_a, idx_b, 2 * MOE_ROWS)
    ys = _ffn(xs.reshape(MOE_ROWS, D_MODEL // 2), tile_expert, tile_rows, tile_first, n_active, w1, w3, w2)
    y_pairs = _sc_gather_rows(ys.reshape(2 * MOE_ROWS, SC_ROW_WORDS), jnp.concatenate([idx_a, idx_b]))
    return _moe_combine(x1, y_pairs.reshape(2 * N_TOK, D_MODEL // 2), route_w, mod, final_norm)


def _swap_rotary_halves(x):
    n = x.shape[-1]
    lane = lax.broadcasted_iota(jnp.int32, x.shape, 1)
    quarter = ROPE_HALF // 2
    return jnp.where((lane % ROPE_HALF) < quarter, pltpu.roll(x, n - quarter, 1), pltpu.roll(x, quarter, 1))


def _at_in_kernel(x_ref, mod_ref, nw_ref, w_ref, cos_ref, sin_ref, q_ref, k_ref, v_ref):
    cond = _tile_cond(pl.program_id(0))
    h = _norm_mod(x_ref[...], nw_ref[...], _mod_row(mod_ref, cond, 0), _mod_row(mod_ref, cond, 1)).astype(BF16)
    nq = ATT_Q_HEADS * ATT_HEAD_DIM
    nk = ATT_KV_HEADS * ATT_HEAD_DIM
    cos = cos_ref[...]
    sin = sin_ref[...]

    def rope(x):
        reps = x.shape[-1] // LANES
        return x * jnp.concatenate([cos] * reps, axis=1) + _swap_rotary_halves(x) * jnp.concatenate([sin] * reps, axis=1)

    q_ref[...] = rope(jnp.dot(h, w_ref[:, :nq], preferred_element_type=F32)).astype(BF16)
    k_ref[...] = rope(jnp.dot(h, w_ref[:, nq:nq + nk], preferred_element_type=F32))
    v_ref[...] = jnp.dot(h, w_ref[:, nq + nk:], preferred_element_type=F32)


def _rope_tables():
    pos = jnp.arange(DEC_SEQ)
    t_row = (pos // GRID_W).astype(F32)
    t_col = (pos % GRID_W).astype(F32)
    inv = ROPE_BASE ** (-jnp.arange(0, ROPE_HALF, 2, dtype=F32) / ROPE_HALF)
    lane = jnp.arange(LANES)
    j = lane % ATT_HEAD_DIM
    freq = inv[(j % ROPE_HALF) % (ROPE_HALF // 2)]
    t = jnp.where((j < ROPE_HALF)[None, :], t_row[:, None], t_col[:, None])
    ang = t * freq[None, :]
    sign = jnp.where((j % ROPE_HALF) < ROPE_HALF // 2, -1.0, 1.0).astype(F32)
    cos = jnp.concatenate([jnp.ones((TM, LANES), F32), jnp.cos(ang)], axis=0)
    sin = jnp.concatenate([jnp.zeros((TM, LANES), F32), jnp.sin(ang) * sign[None, :]], axis=0)
    return cos, sin


def _at_in(x, mod, nw, w_in, cos, sin):
    tile = lambda i: (i, 0)
    fixed2 = lambda i: (0, 0)
    rope_tile = lambda i: (jnp.where(i < PROMPT_TILES, 0, 1 + (i - PROMPT_TILES) % TILES_PER_DEC_SEQ), 0)
    nk = ATT_KV_HEADS * ATT_HEAD_DIM
    return pl.pallas_call(
        _at_in_kernel,
        out_shape=(
            jax.ShapeDtypeStruct((N_TOK, D_MODEL), BF16),
            jax.ShapeDtypeStruct((N_TOK, nk), F32),
            jax.ShapeDtypeStruct((N_TOK, nk), F32),
        ),
        grid=(N_TILES,),
        in_specs=[
            pl.BlockSpec((TM, D_MODEL), tile),
            pl.BlockSpec(mod.shape, fixed2),
            pl.BlockSpec((1, D_MODEL), fixed2),
            pl.BlockSpec(w_in.shape, fixed2),
            pl.BlockSpec((TM, LANES), rope_tile),
            pl.BlockSpec((TM, LANES), rope_tile),
        ],
        out_specs=(pl.BlockSpec((TM, D_MODEL), tile), pl.BlockSpec((TM, nk), tile), pl.BlockSpec((TM, nk), tile)),
        compiler_params=_params(40, "arbitrary"),
        name="attn_in_proj",
    )(x, mod, nw, w_in, cos, sin)


def _attend(q, k_all, v_all, mask, sink_ref, o_ref):
    group_lanes = ATT_GROUP * ATT_HEAD_DIM
    scale = ATT_HEAD_DIM ** -0.5
    nt = (((1,), (1,)), ((), ()))
    lane = lax.broadcasted_iota(jnp.int32, (q.shape[0], group_lanes), 1)
    for hk in range(ATT_KV_HEADS):
        kh = k_all[:, hk * ATT_HEAD_DIM:(hk + 1) * ATT_HEAD_DIM]
        vh = v_all[:, hk * ATT_HEAD_DIM:(hk + 1) * ATT_HEAD_DIM]
        kt = jnp.concatenate([kh] * ATT_GROUP, axis=1)
        vt = jnp.concatenate([vh] * ATT_GROUP, axis=1)
        qg = q[:, hk * group_lanes:(hk + 1) * group_lanes]
        acc = jnp.zeros((q.shape[0], group_lanes), F32)
        for g in range(ATT_GROUP):
            mine = (lane // ATT_HEAD_DIM) == g
            qm = jnp.where(mine, qg, jnp.zeros_like(qg))
            s = lax.dot_general(qm, kt, nt, preferred_element_type=F32) * scale
            if mask is not None:
                s = jnp.where(mask, s, -jnp.inf)
            sink = sink_ref[hk * ATT_GROUP + g]
            m = jnp.maximum(jnp.max(s, axis=-1, keepdims=True), sink)
            p = jnp.exp(s - m)
            denom = jnp.sum(p, axis=-1, keepdims=True) + jnp.exp(sink - m)
            o = jnp.dot(p.astype(BF16), vt, preferred_element_type=F32) / denom
            acc = acc + jnp.where(mine, o, 0.0)
        o_ref[:, hk * group_lanes:(hk + 1) * group_lanes] = acc.astype(BF16)


def _ctx_attn_kernel(sink_ref, q_ref, k_ref, v_ref, o_ref):
    _attend(q_ref[...], k_ref[...].astype(BF16), v_ref[...].astype(BF16), None, sink_ref, o_ref)


def _lat_attn_kernel(sink_ref, q_ref, kp_ref, kc_ref, kn_ref, vp_ref, vc_ref, vn_ref, ck_ref, cv_ref, o_ref):
    jb = pl.program_id(1)
    k_all = jnp.concatenate([kp_ref[...], kc_ref[...], kn_ref[...], ck_ref[0]], axis=0).astype(BF16)
    v_all = jnp.concatenate([vp_ref[...], vc_ref[...], vn_ref[...], cv_ref[0]], axis=0).astype(BF16)
    nkeys = 3 * BLOCK + PAST_LEN
    qi = lax.broadcasted_iota(jnp.int32, (BLOCK, nkeys), 0)
    kj = lax.broadcasted_iota(jnp.int32, (BLOCK, nkeys), 1)
    qpos = jb * BLOCK + qi
    kpos = (jb - 1) * BLOCK + kj
    local_ok = (jnp.abs(qpos - kpos) <= WINDOW) & (kpos >= 0) & (kpos < DEC_SEQ)
    mask = (kj >= 3 * BLOCK) | local_ok
    _attend(q_ref[...], k_all, v_all, mask, sink_ref, o_ref)


def _attention(q, k, v, cache_k, cache_v, sink):
    nk = ATT_KV_HEADS * ATT_HEAD_DIM
    smem = pl.BlockSpec(memory_space=pltpu.SMEM)
    ctx = pl.pallas_call(
        _ctx_attn_kernel,
        out_shape=jax.ShapeDtypeStruct((N_PROMPT, D_MODEL), BF16),
        grid=(BATCH,),
        in_specs=[
            smem,
            pl.BlockSpec((SEQ, D_MODEL), lambda b: (b, 0)),
            pl.BlockSpec((SEQ, nk), lambda b: (b, 0)),
            pl.BlockSpec((SEQ, nk), lambda b: (b, 0)),
        ],
        out_specs=pl.BlockSpec((SEQ, D_MODEL), lambda b: (b, 0)),
        compiler_params=_params(40, "arbitrary"),
        name="context_attention",
    )(sink, q, k, v)

    nb = DEC_SEQ // BLOCK
    base = N_PROMPT // BLOCK
    cur = lambda b, j: (base + b * nb + j, 0)
    prev = lambda b, j: (base + b * nb + jnp.maximum(j - 1, 0), 0)
    nxt = lambda b, j: (base + b * nb + jnp.minimum(j + 1, nb - 1), 0)
    kv_blk = lambda f: pl.BlockSpec((BLOCK, nk), f)
    cache_blk = pl.BlockSpec((1, PAST_LEN, nk), lambda b, j: (b, 0, 0))
    lat = pl.pallas_call(
        _lat_attn_kernel,
        out_shape=jax.ShapeDtypeStruct((DEC_BATCH * DEC_SEQ, D_MODEL), BF16),
        grid=(DEC_BATCH, nb),
        in_specs=[
            smem,
            pl.BlockSpec((BLOCK, D_MODEL), cur),
            kv_blk(prev), kv_blk(cur), kv_blk(nxt),
            kv_blk(prev), kv_blk(cur), kv_blk(nxt),
            cache_blk, cache_blk,
        ],
        out_specs=pl.BlockSpec((BLOCK, D_MODEL), lambda b, j: (b * nb + j, 0)),
        compiler_params=_params(40, "arbitrary", "arbitrary"),
        name="latent_attention",
    )(sink, q, k, k, k, v, v, v, cache_k, cache_v)
    return jnp.concatenate([ctx, lat], axis=0)


def kernel(x_prompt, x_sample, state_hgrn, cache_k, cache_v, c, c_ctx, ada_w, ada_b, norm_w, hg_w_in,
           hg_lb_logits, hg_onorm, hg_w_out, at_w_in, at_sink, at_w_out, moe_w_group, moe_b_group,
           moe_w_expert, moe_b_expert, moe_w1, moe_w3, moe_w2, final_norm):
    x = jnp.concatenate([x_prompt.reshape(N_PROMPT, D_MODEL), x_sample.reshape(-1, D_MODEL)], axis=0)
    cond = jnp.concatenate([c_ctx[None, :], c, jnp.zeros((N_COND - 1 - DEC_BATCH, D_MODEL), F32)], axis=0)
    mod = _ada(cond, ada_w, ada_b)
    nk = ATT_KV_HEADS * ATT_HEAD_DIM

    def router_params(i):
        pad = jnp.zeros((D_MODEL, ROUTER_LANES - N_EXPERTS - N_GROUPS), F32)
        w = jnp.concatenate([moe_w_expert[i], moe_w_group[i], pad], axis=1)
        b = jnp.concatenate([moe_b_expert[i], moe_b_group[i], pad[0]])[None, :]
        return w, b

    tile = lambda i: (i, 0)
    blk = pl.BlockSpec((TM, D_MODEL), tile)

    q, v, g, lff, kf, lfb, kb = _hg_in(x, mod[0], norm_w[0, 0][None, :], hg_w_in[0].astype(BF16), hg_lb_logits)
    o_f, o_b, state_new = _gla(q, v, lff, kf, lfb, kb, state_hgrn)
    wr, br = router_params(0)
    routed = _mixer_out(
        _hg_out_kernel, "hgrn_out_route", (o_f, o_b, g, hg_onorm[0][None, :]),
        (blk, blk, blk, pl.BlockSpec((1, HG_DK), lambda i: (0, 0))),
        x, mod[0], norm_w[0, 1][None, :], hg_w_out[0].astype(BF16), wr, br)
    x = _moe_layer(*routed, mod[0], moe_w1[0], moe_w3[0], moe_w2[0])

    cos, sin = _rope_tables()
    qa, ka, va = _at_in(x, mod[1], norm_w[1, 0][None, :], at_w_in[0].astype(BF16), cos, sin)
    attn = _attention(qa, ka, va, cache_k[:, 0].reshape(DEC_BATCH, PAST_LEN, nk),
                      cache_v[:, 0].reshape(DEC_BATCH, PAST_LEN, nk), at_sink[0])
    wr, br = router_params(1)
    routed = _mixer_out(
        _at_out_kernel, "attn_out_route", (attn,), (blk,),
        x, mod[1], norm_w[1, 1][None, :], at_w_out[0].astype(BF16), wr, br)
    y = _moe_layer(*routed, mod[1], moe_w1[1], moe_w3[1], moe_w2[1], final_norm[None, :])

    y_prompt = y[:N_PROMPT].reshape(BATCH, SEQ, D_MODEL)
    y_sample = y[N_PROMPT:].reshape(DEC_BATCH, DEC_SEQ, D_MODEL)
    cache_k_new = ka[:N_PROMPT].reshape(BATCH, 1, SEQ, ATT_KV_HEADS, ATT_HEAD_DIM)
    cache_v_new = va[:N_PROMPT].reshape(BATCH, 1, SEQ, ATT_KV_HEADS, ATT_HEAD_DIM)
    return (y_prompt, y_sample, state_new, cache_k_new, cache_v_new)
```

```python
import functools

import jax
import jax.numpy as jnp
from jax import lax
from jax.experimental import pallas as pl
from jax.experimental.pallas import tpu as pltpu
from jax.experimental.pallas import tpu_sc as plsc

F32 = jnp.float32
BF16 = jnp.bfloat16

D_MODEL = 1024
BATCH = 16
SEQ = 256
DEC_BATCH = 2
DEC_SEQ = 1024
PAST_LEN = 512
GRID_W = 64
HG_HEADS = 8
HG_DK = 128
CHUNK = 16
ATT_HEAD_DIM = 64
ATT_Q_HEADS = 16
ATT_KV_HEADS = 4
ATT_GROUP = 4
WINDOW = 128
BLOCK = 128
ROPE_HALF = 32
ROPE_BASE = 10000.0
N_GROUPS = 4
EXPERTS_PER_GROUP = 8
N_EXPERTS = 32
D_EXPERT = 256
EPS = 1e-6

N_PROMPT = BATCH * SEQ
N_TOK = N_PROMPT + DEC_BATCH * DEC_SEQ
TM = 256
N_TILES = N_TOK // TM
PROMPT_TILES = N_PROMPT // TM
TILES_PER_DEC_SEQ = DEC_SEQ // TM
LANES = 128
N_COND = 8
ROUTER_LANES = 128
GLA_UNROLL = 2
DECAY_CLAMP = 60.0
PACK_WORDS = D_MODEL // 4
MOE_TILE = 128
MOE_ROWS = 2 * N_TOK + N_EXPERTS * MOE_TILE
MIB = 1024 * 1024


def _params(vmem_mib, *semantics):
    return pltpu.CompilerParams(dimension_semantics=semantics, vmem_limit_bytes=vmem_mib * MIB)


def _tile_cond(i):
    return jnp.where(i < PROMPT_TILES, 0, 1 + (i - PROMPT_TILES) // TILES_PER_DEC_SEQ)


def _prompt_tile(i):
    return (jnp.minimum(i, PROMPT_TILES - 1), 0)


def _sample_tile(i):
    return (jnp.maximum(i - PROMPT_TILES, 0), 0)


def _stream_tile(prompt_ref, sample_ref):
    return jnp.where(pl.program_id(0) < PROMPT_TILES, prompt_ref[...], sample_ref[...])


def _mod_row(mod_ref, cond, which):
    return mod_ref[pl.ds(cond, 1), which * D_MODEL:(which + 1) * D_MODEL]


def _norm_mod(x, nw, shift, scale):
    y = x * lax.rsqrt(jnp.mean(x * x, axis=-1, keepdims=True) + EPS)
    return (y * nw) * (1.0 + scale) + shift


def _silu(x):
    return x * jax.nn.sigmoid(x)


def _ada_kernel(c_ref, w_ref, b_ref, o_ref):
    s = _silu(c_ref[...])
    o_ref[0] = jnp.dot(s, w_ref[0], precision=lax.Precision.HIGHEST, preferred_element_type=F32) + b_ref[0]


def _ada(cond, ada_w, ada_b):
    depth, _, n = ada_w.shape
    tn = 1536
    return pl.pallas_call(
        _ada_kernel,
        out_shape=jax.ShapeDtypeStruct((depth, N_COND, n), F32),
        grid=(depth, n // tn),
        in_specs=[
            pl.BlockSpec((N_COND, D_MODEL), lambda l, j: (0, 0)),
            pl.BlockSpec((1, D_MODEL, tn), lambda l, j: (l, 0, j)),
            pl.BlockSpec((1, 1, tn), lambda l, j: (l, 0, j)),
        ],
        out_specs=pl.BlockSpec((1, N_COND, tn), lambda l, j: (l, 0, j)),
        compiler_params=_params(40, "arbitrary", "arbitrary"),
        name="ada_modulation",
    )(cond, ada_w, ada_b.reshape(depth, 1, n))


def _hg_in_kernel(xp_ref, xs_ref, mod_ref, nw_ref, w_ref, lbl_ref,
                  q_ref, v_ref, g_ref, lff_ref, kf_ref, lfb_ref, kb_ref):
    cond = _tile_cond(pl.program_id(0))
    x = _stream_tile(xp_ref, xs_ref)
    h = _norm_mod(x, nw_ref[...], _mod_row(mod_ref, cond, 0), _mod_row(mod_ref, cond, 1)).astype(BF16)

    def proj(c):
        return jnp.dot(h, w_ref[:, c * D_MODEL:(c + 1) * D_MODEL], preferred_element_type=F32)

    l0, l1, l2 = lbl_ref[0], lbl_ref[1], lbl_ref[2]
    m = jnp.maximum(jnp.maximum(l0, l1), l2)
    e0, e1, e2 = jnp.exp(l0 - m), jnp.exp(l1 - m), jnp.exp(l2 - m)
    lb = e0 / (e0 + e1 + e2)

    q_ref[...] = proj(0).astype(BF16)
    v_ref[...] = proj(1).astype(BF16)
    for d, (lf_ref, k_ref) in enumerate(((lff_ref, kf_ref), (lfb_ref, kb_ref))):
        lbd = lb[d:d + 1, :]
        f = lbd + (1.0 - lbd) * jax.nn.sigmoid(proj(2 + d))
        lf_ref[...] = jnp.log(f)
        k_ref[...] = (1.0 - f).astype(BF16)
    g_ref[...] = proj(4).astype(BF16)


def _hg_in(xp, xs, mod, nw, w_in, lb_logits):
    tile = lambda i: (i, 0)
    fixed2 = lambda i: (0, 0)
    bf = jax.ShapeDtypeStruct((N_TOK, D_MODEL), BF16)
    ff = jax.ShapeDtypeStruct((N_TOK, D_MODEL), F32)
    blk = pl.BlockSpec((TM, D_MODEL), tile)
    return pl.pallas_call(
        _hg_in_kernel,
        out_shape=(bf, bf, bf, ff, bf, ff, bf),
        grid=(N_TILES,),
        in_specs=[
            pl.BlockSpec((TM, D_MODEL), _prompt_tile),
            pl.BlockSpec((TM, D_MODEL), _sample_tile),
            pl.BlockSpec(mod.shape, fixed2),
            pl.BlockSpec((1, D_MODEL), fixed2),
            pl.BlockSpec(w_in.shape, fixed2),
            pl.BlockSpec(lb_logits.shape, lambda i: (0, 0, 0)),
        ],
        out_specs=(blk,) * 7,
        compiler_params=_params(56, "arbitrary"),
        name="hgrn_in_proj",
    )(xp, xs, mod, nw, w_in, lb_logits)


def _gla_direction(reverse, q_ref, v_ref, lf_ref, k_ref, st_ref, sw_ref, o_ref, qd_scr, kd_scr, ki_scr, cd_scr):
    lf = lf_ref[...]
    r = lax.broadcasted_iota(jnp.int32, (TM, TM), 0)
    c = lax.broadcasted_iota(jnp.int32, (TM, TM), 1)
    same = (r // CHUNK) == (c // CHUNK)
    tri = (same & ((c >= r) if reverse else (c <= r))).astype(BF16)
    hi = lf.astype(BF16)
    lo = (lf - hi.astype(F32)).astype(BF16)
    b = jnp.dot(tri, hi, preferred_element_type=F32) + jnp.dot(tri, lo, preferred_element_type=F32)
    b3 = b.reshape(TM // CHUNK, CHUNK, D_MODEL)
    edge = 0 if reverse else CHUNK - 1
    tot = jnp.broadcast_to(b3[:, edge:edge + 1, :], b3.shape).reshape(TM, D_MODEL)
    kf = k_ref[...].astype(F32)
    qd_scr[...] = (q_ref[...].astype(F32) * jnp.exp(b)).astype(BF16)
    kd_scr[...] = (kf * jnp.exp(tot - b)).astype(BF16)
    ki_scr[...] = (kf * jnp.exp(jnp.minimum(-b, DECAY_CLAMP))).astype(BF16)
    cd_scr[...] = jnp.exp(tot)

    tr = lax.broadcasted_iota(jnp.int32, (CHUNK, CHUNK), 0)
    tc = lax.broadcasted_iota(jnp.int32, (CHUNK, CHUNK), 1)
    keep = (tc >= tr) if reverse else (tc <= tr)
    nt = (((1,), (1,)), ((), ()))
    tn = (((0,), (0,)), ((), ()))

    heads = [slice(h * HG_DK, (h + 1) * HG_DK) for h in range(HG_HEADS)]
    for h in range(HG_HEADS):
        sw_ref[0, h] = st_ref[0, h].T.astype(BF16)

    def chunk(ci, src, dst):
        row0 = pl.multiple_of(ci * CHUNK, CHUNK)
        rows = pl.ds(row0, CHUNK)
        for h, cols in enumerate(heads):
            ut = lax.dot_general(v_ref[rows, cols], kd_scr[rows, cols], tn, preferred_element_type=F32)
            new = st_ref[src, h] * cd_scr[pl.ds(row0, 1), cols] + ut
            st_ref[dst, h] = new
            sw_ref[dst, h] = new.T.astype(BF16)
        a = [lax.dot_general(qd_scr[rows, cols], ki_scr[rows, cols], nt, preferred_element_type=F32) for cols in heads]
        inter = [jnp.dot(qd_scr[rows, cols], sw_ref[src, h], preferred_element_type=F32)
                 for h, cols in enumerate(heads)]
        for h, cols in enumerate(heads):
            am = jnp.where(keep, a[h], 0.0).astype(BF16)
            o_ref[rows, cols] = jnp.dot(am, v_ref[rows, cols], preferred_element_type=F32) + inter[h]

    n_chunks = TM // CHUNK

    def chunk_group(it, carry):
        for p in range(GLA_UNROLL):
            step = GLA_UNROLL * it + p
            chunk((n_chunks - 1 - step) if reverse else step, p % 2, 1 - p % 2)
        return carry

    lax.fori_loop(0, n_chunks // GLA_UNROLL, chunk_group, 0)


def _gla_kernel(qf_ref, vf_ref, lff_ref, kf_ref, qb_ref, vb_ref, lfb_ref, kb_ref, s0_ref,
                of_ref, ob_ref, sout_ref, stf_scr, stb_scr, sw_scr, qd_scr, kd_scr, ki_scr, cd_scr):
    i = pl.program_id(0)
    is_prompt = i < PROMPT_TILES
    first = jnp.logical_or(is_prompt, (i - PROMPT_TILES) % TILES_PER_DEC_SEQ == 0)

    @pl.when(first)
    def _():
        keep0 = jnp.where(is_prompt, 0.0, 1.0)
        for h in range(HG_HEADS):
            stf_scr[0, h] = s0_ref[0, 0, 0, h].T * keep0
            stb_scr[0, h] = s0_ref[0, 0, 1, h].T * keep0

    _gla_direction(False, qf_ref, vf_ref, lff_ref, kf_ref, stf_scr, sw_scr, of_ref, qd_scr, kd_scr, ki_scr, cd_scr)
    _gla_direction(True, qb_ref, vb_ref, lfb_ref, kb_ref, stb_scr, sw_scr, ob_ref, qd_scr, kd_scr, ki_scr, cd_scr)

    @pl.when(is_prompt)
    def _():
        for h in range(HG_HEADS):
            sout_ref[0, 0, 0, h] = stf_scr[0, h].T
            sout_ref[0, 0, 1, h] = stb_scr[0, h].T


def _gla(q, v, lff, kf, lfb, kb, state_hgrn):
    def fwd_tile(i):
        return (i, 0)

    def bwd_tile(i):
        j = (i - PROMPT_TILES) % TILES_PER_DEC_SEQ
        return (jnp.where(i < PROMPT_TILES, i, i - j + (TILES_PER_DEC_SEQ - 1 - j)), 0)

    def s0_idx(i):
        return (jnp.maximum(i - PROMPT_TILES, 0) // TILES_PER_DEC_SEQ, 0, 0, 0, 0, 0)

    def sout_idx(i):
        return (jnp.minimum(i, PROMPT_TILES - 1), 0, 0, 0, 0, 0)

    f_blk = pl.BlockSpec((TM, D_MODEL), fwd_tile)
    b_blk = pl.BlockSpec((TM, D_MODEL), bwd_tile)
    st_blk = (1, 1, 2, HG_HEADS, HG_DK, HG_DK)
    return pl.pallas_call(
        _gla_kernel,
        out_shape=(
            jax.ShapeDtypeStruct((N_TOK, D_MODEL), F32),
            jax.ShapeDtypeStruct((N_TOK, D_MODEL), F32),
            jax.ShapeDtypeStruct((BATCH,) + st_blk[1:], F32),
        ),
        grid=(N_TILES,),
        in_specs=[f_blk, f_blk, f_blk, f_blk, b_blk, b_blk, b_blk, b_blk, pl.BlockSpec(st_blk, s0_idx)],
        out_specs=(f_blk, b_blk, pl.BlockSpec(st_blk, sout_idx)),
        scratch_shapes=[
            pltpu.VMEM((2, HG_HEADS, HG_DK, HG_DK), F32),
            pltpu.VMEM((2, HG_HEADS, HG_DK, HG_DK), F32),
            pltpu.VMEM((2, HG_HEADS, HG_DK, HG_DK), BF16),
            pltpu.VMEM((TM, D_MODEL), BF16),
            pltpu.VMEM((TM, D_MODEL), BF16),
            pltpu.VMEM((TM, D_MODEL), BF16),
            pltpu.VMEM((TM, D_MODEL), F32),
        ],
        compiler_params=_params(48, "arbitrary"),
        name="hgrn_recurrence",
    )(q, v, lff, kf, q, v, lfb, kb, state_hgrn)


def _route(logits):
    lane = lax.broadcasted_iota(jnp.int32, logits.shape, 1)
    big = jnp.int32(ROUTER_LANES)
    neg = jnp.float32(-jnp.inf)
    is_group = (lane >= N_EXPERTS) & (lane < N_EXPERTS + N_GROUPS)
    gl = jnp.where(is_group, logits, neg)
    gmax = jnp.max(gl, axis=-1, keepdims=True)
    gsum = jnp.sum(jnp.exp(gl - gmax), axis=-1, keepdims=True)
    p_g = 1.0 / gsum
    g_sel = jnp.min(jnp.where(is_group & (gl == gmax), lane - N_EXPERTS, big), axis=-1, keepdims=True)
    in_sel = (lane < N_EXPERTS) & ((lane // EXPERTS_PER_GROUP) == g_sel)
    el = jnp.where(in_sel, logits, neg)
    m1 = jnp.max(el, axis=-1, keepdims=True)
    i1 = jnp.min(jnp.where(in_sel & (el == m1), lane, big), axis=-1, keepdims=True)
    el2 = jnp.where(lane == i1, neg, el)
    m2 = jnp.max(el2, axis=-1, keepdims=True)
    i2 = jnp.min(jnp.where(in_sel & (lane != i1) & (el2 == m2), lane, big), axis=-1, keepdims=True)
    e2 = jnp.exp(m2 - m1)
    return i1, i2, p_g / (1.0 + e2), p_g * e2 / (1.0 + e2)


def _pack_rows(x):
    q = PACK_WORDS
    bits = pltpu.bitcast(x.astype(BF16).astype(F32), jnp.uint32)
    return [(bits[:, (2 + h) * q:(3 + h) * q] & jnp.uint32(0xFFFF0000)) | (bits[:, h * q:(h + 1) * q] >> 16)
            for h in range(2)]


def _unpack_rows(half0, half1):
    lo = lambda w: pltpu.bitcast(w << 16, F32).astype(BF16)
    hi = lambda w: pltpu.bitcast(w & jnp.uint32(0xFFFF0000), F32).astype(BF16)
    return [lo(half0), lo(half1), hi(half0), hi(half1)]


def _mixer_tail(mix_bf16, x, mod_ref, nw2_ref, wo_ref, wr_ref, br_ref,
                x1_ref, h2p_ref, ri_ref, rw_ref, cnt_ref, carry_scr):
    i = pl.program_id(0)
    cond = _tile_cond(i)
    out = jnp.dot(mix_bf16, wo_ref[...], preferred_element_type=F32)
    x1 = x + _mod_row(mod_ref, cond, 2) * out
    x1_ref[...] = x1
    h2 = _norm_mod(x1, nw2_ref[...], _mod_row(mod_ref, cond, 3), _mod_row(mod_ref, cond, 4))
    for h, words in enumerate(_pack_rows(h2)):
        h2p_ref[h] = words
    logits = jnp.dot(h2, wr_ref[...], precision=lax.Precision.HIGHEST, preferred_element_type=F32) + br_ref[...]
    i1, i2, w1, w2 = _route(logits)

    @pl.when(i == 0)
    def _():
        carry_scr[...] = jnp.zeros_like(carry_scr)

    lane = lax.broadcasted_iota(jnp.int32, logits.shape, 1)
    chosen = ((lane == i1) | (lane == i2)).astype(BF16)
    r = lax.broadcasted_iota(jnp.int32, (TM, TM), 0)
    c = lax.broadcasted_iota(jnp.int32, (TM, TM), 1)
    before = jnp.dot((c < r).astype(BF16), chosen, preferred_element_type=F32) + carry_scr[...]
    r1 = jnp.sum(jnp.where(lane == i1, before, 0.0), axis=-1, keepdims=True).astype(jnp.int32)
    r2 = jnp.sum(jnp.where(lane == i2, before, 0.0), axis=-1, keepdims=True).astype(jnp.int32)
    total = carry_scr[...] + jnp.sum(chosen.astype(F32), axis=0, keepdims=True)
    carry_scr[...] = total
    cnt_ref[...] = total
    ri_ref[...] = jnp.where(lane == 0, i1, jnp.where(lane == 1, i2, jnp.where(lane == 2, r1, r2)))
    rw_ref[...] = jnp.where(lane == 0, w1, w2)


def _hg_out_kernel(of_ref, ob_ref, g_ref, on_ref, xp_ref, xs_ref, *rest):
    o = of_ref[...] + ob_ref[...]
    parts = []
    for h in range(HG_HEADS):
        oh = o[:, h * HG_DK:(h + 1) * HG_DK]
        parts.append(oh * lax.rsqrt(jnp.mean(oh * oh, axis=-1, keepdims=True) + EPS) * on_ref[...])
    y = jnp.concatenate(parts, axis=1) * _silu(g_ref[...].astype(F32))
    _mixer_tail(y.astype(BF16), _stream_tile(xp_ref, xs_ref), *rest)


def _at_out_kernel(ac_ref, al_ref, x_ref, *rest):
    _mixer_tail(_stream_tile(ac_ref, al_ref), x_ref[...], *rest)


def _mixer_out(kernel_fn, name, mix_inputs, mix_specs, mod, nw2, w_out, w_router, b_router):
    tile = lambda i: (i, 0)
    fixed2 = lambda i: (0, 0)
    blk = pl.BlockSpec((TM, D_MODEL), tile)
    lanes_blk = pl.BlockSpec((TM, ROUTER_LANES), tile)
    return pl.pallas_call(
        kernel_fn,
        out_shape=(
            jax.ShapeDtypeStruct((N_TOK, D_MODEL), F32),
            jax.ShapeDtypeStruct((2, N_TOK, PACK_WORDS), jnp.uint32),
            jax.ShapeDtypeStruct((N_TOK, ROUTER_LANES), jnp.int32),
            jax.ShapeDtypeStruct((N_TOK, ROUTER_LANES), F32),
            jax.ShapeDtypeStruct((1, ROUTER_LANES), F32),
        ),
        grid=(N_TILES,),
        in_specs=list(mix_specs) + [
            pl.BlockSpec(mod.shape, fixed2),
            pl.BlockSpec((1, D_MODEL), fixed2),
            pl.BlockSpec(w_out.shape, fixed2),
            pl.BlockSpec(w_router.shape, fixed2),
            pl.BlockSpec((1, ROUTER_LANES), fixed2),
        ],
        out_specs=(blk, pl.BlockSpec((2, TM, PACK_WORDS), lambda i: (0, i, 0)), lanes_blk, lanes_blk,
                   pl.BlockSpec((1, ROUTER_LANES), fixed2)),
        scratch_shapes=[pltpu.VMEM((1, ROUTER_LANES), F32)],
        compiler_params=_params(40, "arbitrary"),
        name=name,
    )(*mix_inputs, mod, nw2, w_out, w_router, b_router)


def _moe_plan(route_i, counts):
    cnt = counts[0, :N_EXPERTS].astype(jnp.int32)
    padded = ((cnt + MOE_TILE - 1) // MOE_TILE) * MOE_TILE
    ends = jnp.cumsum(padded)
    offs = ends - padded
    experts = route_i[:, 0:2]
    pos = jnp.sum(jnp.where(experts[:, :, None] == jnp.arange(N_EXPERTS)[None, None, :], offs[None, None, :], 0),
                  axis=-1) + route_i[:, 2:4]
    tile_start = jnp.arange(MOE_ROWS // MOE_TILE, dtype=jnp.int32) * MOE_TILE
    tile_expert = jnp.minimum(jnp.sum(ends[None, :] <= tile_start[:, None], axis=1), N_EXPERTS - 1).astype(jnp.int32)
    tile_rows = jnp.clip(offs[tile_expert] + cnt[tile_expert] - tile_start, 0, MOE_TILE).astype(jnp.int32)
    tile_first = (tile_start == offs[tile_expert]).astype(jnp.int32)
    n_active = (ends[-1] // MOE_TILE).astype(jnp.int32).reshape(1)
    return pos.astype(jnp.int32), tile_expert, tile_rows, tile_first, n_active


SC_WINDOW = 128


def _sc_mesh():
    return plsc.VectorSubcoreMesh(core_axis_name="c", subcore_axis_name="s")


def _sc_scatter_rows(x, idx_a, idx_b, n_out_rows):
    n = x.shape[0]

    @functools.partial(pl.kernel, out_type=jax.ShapeDtypeStruct((n_out_rows, PACK_WORDS), x.dtype), mesh=_sc_mesh())
    def scatter(x_hbm, ia_hbm, ib_hbm, o_hbm):
        def body(x_vmem, ia_vmem, ib_vmem):
            pltpu.sync_copy(x_vmem, o_hbm.at[ia_vmem.at[0]])
            pltpu.sync_copy(x_vmem, o_hbm.at[ib_vmem.at[0]])

        idx_spec = pl.BlockSpec((1, SC_WINDOW), index_map=lambda i: (0, i))
        pltpu.emit_pipeline(
            body, grid=(n // SC_WINDOW,),
            in_specs=[pl.BlockSpec((SC_WINDOW, PACK_WORDS), index_map=lambda i: (i, 0)), idx_spec, idx_spec],
            out_specs=[],
            core_axis_name=("c", "s"), dimension_semantics=(pltpu.PARALLEL,),
        )(x_hbm, ia_hbm, ib_hbm)

    return scatter(x, idx_a.reshape(1, n), idx_b.reshape(1, n))


def _sc_gather_rows(table, idx):
    n = idx.shape[0]

    @functools.partial(pl.kernel, out_type=jax.ShapeDtypeStruct((n, PACK_WORDS), table.dtype), mesh=_sc_mesh())
    def gather(t_hbm, i_hbm, o_hbm):
        def body(i_vmem, o_vmem):
            pltpu.sync_copy(t_hbm.at[i_vmem.at[0]], o_vmem)

        pltpu.emit_pipeline(
            body, grid=(n // SC_WINDOW,),
            in_specs=[pl.BlockSpec((1, SC_WINDOW), index_map=lambda i: (0, i))],
            out_specs=[pl.BlockSpec((SC_WINDOW, PACK_WORDS), index_map=lambda i: (i, 0))],
            core_axis_name=("c", "s"), dimension_semantics=(pltpu.PARALLEL,),
        )(i_hbm, o_hbm)

    return gather(table, idx.reshape(1, n))


def _ffn_kernel(te_ref, tr_ref, tf_ref, na_ref, xs_ref, w1_ref, w3_ref, w2_ref, ys_ref, w1_scr, w3_scr, w2_scr):
    t = pl.program_id(0)

    @pl.when(t < na_ref[0])
    def _():
        @pl.when(tf_ref[t] == 1)
        def _():
            w1_scr[...] = w1_ref[0].astype(BF16)
            w3_scr[...] = w3_ref[0].astype(BF16)
            w2_scr[...] = w2_ref[0].astype(BF16)

        row = lax.broadcasted_iota(jnp.int32, (MOE_TILE, PACK_WORDS), 0)
        live = row < tr_ref[t]
        halves = [jnp.where(live, xs_ref[h], jnp.zeros((MOE_TILE, PACK_WORDS), jnp.uint32)) for h in range(2)]
        chunks = _unpack_rows(*halves)

        def up(w_scr):
            acc = jnp.dot(chunks[0], w_scr[:PACK_WORDS, :], preferred_element_type=F32)
            for k in range(1, len(chunks)):
                acc = acc + jnp.dot(chunks[k], w_scr[k * PACK_WORDS:(k + 1) * PACK_WORDS, :],
                                    preferred_element_type=F32)
            return acc

        hid = (_silu(up(w1_scr)) * up(w3_scr)).astype(BF16)
        for h, words in enumerate(_pack_rows(jnp.dot(hid, w2_scr[...], preferred_element_type=F32))):
            ys_ref[h] = words


def _ffn(xs, tile_expert, tile_rows, tile_first, n_active, w1, w3, w2):
    row_tile = lambda t, te, tr, tf, na: (0, jnp.minimum(t, na[0] - 1), 0)
    expert = lambda t, te, tr, tf, na: (te[t], 0, 0)
    return pl.pallas_call(
        _ffn_kernel,
        out_shape=jax.ShapeDtypeStruct((2, MOE_ROWS, PACK_WORDS), jnp.uint32),
        grid_spec=pltpu.PrefetchScalarGridSpec(
            num_scalar_prefetch=4,
            grid=(MOE_ROWS // MOE_TILE,),
            in_specs=[
                pl.BlockSpec((2, MOE_TILE, PACK_WORDS), row_tile),
                pl.BlockSpec((1, D_MODEL, D_EXPERT), expert),
                pl.BlockSpec((1, D_MODEL, D_EXPERT), expert),
                pl.BlockSpec((1, D_EXPERT, D_MODEL), expert),
            ],
            out_specs=pl.BlockSpec((2, MOE_TILE, PACK_WORDS), row_tile),
            scratch_shapes=[
                pltpu.VMEM((D_MODEL, D_EXPERT), BF16),
                pltpu.VMEM((D_MODEL, D_EXPERT), BF16),
                pltpu.VMEM((D_EXPERT, D_MODEL), BF16),
            ],
        ),
        compiler_params=_params(32, "arbitrary"),
        name="moe_experts",
    )(tile_expert, tile_rows, tile_first, n_active, xs, w1, w3, w2)


def _moe_res(x_ref, y_ref, rw_ref, mod_ref):
    cond = _tile_cond(pl.program_id(0))
    rw = rw_ref[...]
    wa = rw[:, 0:1]
    wb = rw[:, 1:2]
    ya = _unpack_rows(y_ref[0], y_ref[1])
    yb = _unpack_rows(y_ref[2], y_ref[3])
    y = jnp.concatenate([wa * a.astype(F32) + wb * b.astype(F32) for a, b in zip(ya, yb)], axis=1)
    return x_ref[...] + _mod_row(mod_ref, cond, 5) * y


def _moe_res_kernel(x_ref, y_ref, rw_ref, mod_ref, o_ref):
    o_ref[...] = _moe_res(x_ref, y_ref, rw_ref, mod_ref)


def _moe_res_final_kernel(x_ref, y_ref, rw_ref, mod_ref, fn_ref, op_ref, os_ref):
    x = _moe_res(x_ref, y_ref, rw_ref, mod_ref)
    y = x * lax.rsqrt(jnp.mean(x * x, axis=-1, keepdims=True) + EPS) * fn_ref[...]
    is_prompt = pl.program_id(0) < PROMPT_TILES

    @pl.when(is_prompt)
    def _():
        op_ref[...] = y

    @pl.when(jnp.logical_not(is_prompt))
    def _():
        os_ref[...] = y


def _moe_combine(x1, y_pairs, route_w, mod, final_norm=None):
    tile = lambda i: (i, 0)
    fixed2 = lambda i: (0, 0)
    blk = pl.BlockSpec((TM, D_MODEL), tile)
    specs = [blk,
             pl.BlockSpec((4, TM, PACK_WORDS), lambda i: (0, i, 0)),
             pl.BlockSpec((TM, ROUTER_LANES), tile),
             pl.BlockSpec(mod.shape, fixed2)]
    args = [x1, y_pairs, route_w, mod]
    if final_norm is None:
        kernel_fn = _moe_res_kernel
        out_shape = jax.ShapeDtypeStruct((N_TOK, D_MODEL), F32)
        out_specs = blk
    else:
        kernel_fn = _moe_res_final_kernel
        specs.append(pl.BlockSpec((1, D_MODEL), fixed2))
        args.append(final_norm)
        out_shape = (jax.ShapeDtypeStruct((N_PROMPT, D_MODEL), F32),
                     jax.ShapeDtypeStruct((N_TOK - N_PROMPT, D_MODEL), F32))
        out_specs = (pl.BlockSpec((TM, D_MODEL), _prompt_tile), pl.BlockSpec((TM, D_MODEL), _sample_tile))
    return pl.pallas_call(
        kernel_fn,
        out_shape=out_shape,
        grid=(N_TILES,),
        in_specs=specs,
        out_specs=out_specs,
        compiler_params=_params(32, "arbitrary"),
        name="moe_combine",
    )(*args)


def _moe_layer(x1, h2p, route_i, route_w, counts, mod, w1, w3, w2, final_norm=None):
    pos, tile_expert, tile_rows, tile_first, n_active = _moe_plan(route_i, counts)
    idx_a = jnp.concatenate([pos[:, 0], pos[:, 0] + MOE_ROWS])
    idx_b = jnp.concatenate([pos[:, 1], pos[:, 1] + MOE_ROWS])
    xs = _sc_scatter_rows(h2p.reshape(2 * N_TOK, PACK_WORDS), idx_a, idx_b, 2 * MOE_ROWS)
    ys = _ffn(xs.reshape(2, MOE_ROWS, PACK_WORDS), tile_expert, tile_rows, tile_first, n_active, w1, w3, w2)
    y_pairs = _sc_gather_rows(ys.reshape(2 * MOE_ROWS, PACK_WORDS), jnp.concatenate([idx_a, idx_b]))
    return _moe_combine(x1, y_pairs.reshape(4, N_TOK, PACK_WORDS), route_w, mod, final_norm)


def _swap_rotary_halves(x):
    n = x.shape[-1]
    lane = lax.broadcasted_iota(jnp.int32, x.shape, 1)
    quarter = ROPE_HALF // 2
    return jnp.where((lane % ROPE_HALF) < quarter, pltpu.roll(x, n - quarter, 1), pltpu.roll(x, quarter, 1))


def _at_in_kernel(x_ref, mod_ref, nw_ref, w_ref, cos_ref, sin_ref, q_ref, k_ref, v_ref, kc_ref, vc_ref):
    i = pl.program_id(0)
    cond = _tile_cond(i)
    h = _norm_mod(x_ref[...], nw_ref[...], _mod_row(mod_ref, cond, 0), _mod_row(mod_ref, cond, 1)).astype(BF16)
    nq = ATT_Q_HEADS * ATT_HEAD_DIM
    nk = ATT_KV_HEADS * ATT_HEAD_DIM
    cos = cos_ref[...]
    sin = sin_ref[...]

    def rope(x):
        reps = x.shape[-1] // LANES
        return x * jnp.concatenate([cos] * reps, axis=1) + _swap_rotary_halves(x) * jnp.concatenate([sin] * reps, axis=1)

    q_ref[...] = rope(jnp.dot(h, w_ref[:, :nq], preferred_element_type=F32)).astype(BF16)
    k = rope(jnp.dot(h, w_ref[:, nq:nq + nk], preferred_element_type=F32))
    v = jnp.dot(h, w_ref[:, nq + nk:], preferred_element_type=F32)
    k_ref[...] = k.astype(BF16)
    v_ref[...] = v.astype(BF16)

    @pl.when(i < PROMPT_TILES)
    def _():
        kc_ref[...] = k
        vc_ref[...] = v


def _rope_tables():
    pos = jnp.arange(DEC_SEQ)
    t_row = (pos // GRID_W).astype(F32)
    t_col = (pos % GRID_W).astype(F32)
    inv = ROPE_BASE ** (-jnp.arange(0, ROPE_HALF, 2, dtype=F32) / ROPE_HALF)
    lane = jnp.arange(LANES)
    j = lane % ATT_HEAD_DIM
    freq = inv[(j % ROPE_HALF) % (ROPE_HALF // 2)]
    t = jnp.where((j < ROPE_HALF)[None, :], t_row[:, None], t_col[:, None])
    ang = t * freq[None, :]
    sign = jnp.where((j % ROPE_HALF) < ROPE_HALF // 2, -1.0, 1.0).astype(F32)
    cos = jnp.concatenate([jnp.ones((TM, LANES), F32), jnp.cos(ang)], axis=0)
    sin = jnp.concatenate([jnp.zeros((TM, LANES), F32), jnp.sin(ang) * sign[None, :]], axis=0)
    return cos, sin


def _at_in(x, mod, nw, w_in, cos, sin):
    tile = lambda i: (i, 0)
    fixed2 = lambda i: (0, 0)
    rope_tile = lambda i: (jnp.where(i < PROMPT_TILES, 0, 1 + (i - PROMPT_TILES) % TILES_PER_DEC_SEQ), 0)
    nk = ATT_KV_HEADS * ATT_HEAD_DIM
    return pl.pallas_call(
        _at_in_kernel,
        out_shape=(
            jax.ShapeDtypeStruct((N_TOK, D_MODEL), BF16),
            jax.ShapeDtypeStruct((N_TOK, nk), BF16),
            jax.ShapeDtypeStruct((N_TOK, nk), BF16),
            jax.ShapeDtypeStruct((N_PROMPT, nk), F32),
            jax.ShapeDtypeStruct((N_PROMPT, nk), F32),
        ),
        grid=(N_TILES,),
        in_specs=[
            pl.BlockSpec((TM, D_MODEL), tile),
            pl.BlockSpec(mod.shape, fixed2),
            pl.BlockSpec((1, D_MODEL), fixed2),
            pl.BlockSpec(w_in.shape, fixed2),
            pl.BlockSpec((TM, LANES), rope_tile),
            pl.BlockSpec((TM, LANES), rope_tile),
        ],
        out_specs=(pl.BlockSpec((TM, D_MODEL), tile), pl.BlockSpec((TM, nk), tile), pl.BlockSpec((TM, nk), tile),
                   pl.BlockSpec((TM, nk), _prompt_tile), pl.BlockSpec((TM, nk), _prompt_tile)),
        compiler_params=_params(40, "arbitrary"),
        name="attn_in_proj",
    )(x, mod, nw, w_in, cos, sin)


def _attend(q, k_all, v_all, mask, sink_ref, o_ref):
    group_lanes = ATT_GROUP * ATT_HEAD_DIM
    scale = ATT_HEAD_DIM ** -0.5
    nt = (((1,), (1,)), ((), ()))
    lane = lax.broadcasted_iota(jnp.int32, (q.shape[0], group_lanes), 1)
    for hk in range(ATT_KV_HEADS):
        kh = k_all[:, hk * ATT_HEAD_DIM:(hk + 1) * ATT_HEAD_DIM]
        vh = v_all[:, hk * ATT_HEAD_DIM:(hk + 1) * ATT_HEAD_DIM]
        kt = jnp.concatenate([kh] * ATT_GROUP, axis=1)
        vt = jnp.concatenate([vh] * ATT_GROUP, axis=1)
        qg = q[:, hk * group_lanes:(hk + 1) * group_lanes]
        acc = jnp.zeros((q.shape[0], group_lanes), F32)
        for g in range(ATT_GROUP):
            mine = (lane // ATT_HEAD_DIM) == g
            qm = jnp.where(mine, qg, jnp.zeros_like(qg))
            s = lax.dot_general(qm, kt, nt, preferred_element_type=F32) * scale
            if mask is not None:
                s = jnp.where(mask, s, -jnp.inf)
            sink = sink_ref[hk * ATT_GROUP + g]
            m = jnp.maximum(jnp.max(s, axis=-1, keepdims=True), sink)
            p = jnp.exp(s - m)
            denom = jnp.sum(p, axis=-1, keepdims=True) + jnp.exp(sink - m)
            o = jnp.dot(p.astype(BF16), vt, preferred_element_type=F32) / denom
            acc = acc + jnp.where(mine, o, 0.0)
        o_ref[:, hk * group_lanes:(hk + 1) * group_lanes] = acc.astype(BF16)


def _ctx_attn_kernel(sink_ref, q_ref, k_ref, v_ref, o_ref):
    _attend(q_ref[...], k_ref[...], v_ref[...], None, sink_ref, o_ref)


def _lat_attn_kernel(sink_ref, q_ref, kp_ref, kc_ref, kn_ref, vp_ref, vc_ref, vn_ref, ck_ref, cv_ref, o_ref):
    jb = pl.program_id(1)
    k_all = jnp.concatenate([kp_ref[...], kc_ref[...], kn_ref[...], ck_ref[0].astype(BF16)], axis=0)
    v_all = jnp.concatenate([vp_ref[...], vc_ref[...], vn_ref[...], cv_ref[0].astype(BF16)], axis=0)
    nkeys = 3 * BLOCK + PAST_LEN
    qi = lax.broadcasted_iota(jnp.int32, (BLOCK, nkeys), 0)
    kj = lax.broadcasted_iota(jnp.int32, (BLOCK, nkeys), 1)
    qpos = jb * BLOCK + qi
    kpos = (jb - 1) * BLOCK + kj
    local_ok = (jnp.abs(qpos - kpos) <= WINDOW) & (kpos >= 0) & (kpos < DEC_SEQ)
    mask = (kj >= 3 * BLOCK) | local_ok
    _attend(q_ref[...], k_all, v_all, mask, sink_ref, o_ref)


def _attention(q, k, v, cache_k, cache_v, sink):
    nk = ATT_KV_HEADS * ATT_HEAD_DIM
    smem = pl.BlockSpec(memory_space=pltpu.SMEM)
    ctx = pl.pallas_call(
        _ctx_attn_kernel,
        out_shape=jax.ShapeDtypeStruct((N_PROMPT, D_MODEL), BF16),
        grid=(BATCH,),
        in_specs=[
            smem,
            pl.BlockSpec((SEQ, D_MODEL), lambda b: (b, 0)),
            pl.BlockSpec((SEQ, nk), lambda b: (b, 0)),
            pl.BlockSpec((SEQ, nk), lambda b: (b, 0)),
        ],
        out_specs=pl.BlockSpec((SEQ, D_MODEL), lambda b: (b, 0)),
        compiler_params=_params(40, "arbitrary"),
        name="context_attention",
    )(sink, q, k, v)

    nb = DEC_SEQ // BLOCK
    base = N_PROMPT // BLOCK
    cur = lambda b, j: (base + b * nb + j, 0)
    prev = lambda b, j: (base + b * nb + jnp.maximum(j - 1, 0), 0)
    nxt = lambda b, j: (base + b * nb + jnp.minimum(j + 1, nb - 1), 0)
    kv_blk = lambda f: pl.BlockSpec((BLOCK, nk), f)
    cache_blk = pl.BlockSpec((1, PAST_LEN, nk), lambda b, j: (b, 0, 0))
    lat = pl.pallas_call(
        _lat_attn_kernel,
        out_shape=jax.ShapeDtypeStruct((DEC_BATCH * DEC_SEQ, D_MODEL), BF16),
        grid=(DEC_BATCH, nb),
        in_specs=[
            smem,
            pl.BlockSpec((BLOCK, D_MODEL), cur),
            kv_blk(prev), kv_blk(cur), kv_blk(nxt),
            kv_blk(prev), kv_blk(cur), kv_blk(nxt),
            cache_blk, cache_blk,
        ],
        out_specs=pl.BlockSpec((BLOCK, D_MODEL), lambda b, j: (b * nb + j, 0)),
        compiler_params=_params(40, "arbitrary", "arbitrary"),
        name="latent_attention",
    )(sink, q, k, k, k, v, v, v, cache_k, cache_v)
    return ctx, lat


def kernel(x_prompt, x_sample, state_hgrn, cache_k, cache_v, c, c_ctx, ada_w, ada_b, norm_w, hg_w_in,
           hg_lb_logits, hg_onorm, hg_w_out, at_w_in, at_sink, at_w_out, moe_w_group, moe_b_group,
           moe_w_expert, moe_b_expert, moe_w1, moe_w3, moe_w2, final_norm):
    xp = x_prompt.reshape(N_PROMPT, D_MODEL)
    xs = x_sample.reshape(N_TOK - N_PROMPT, D_MODEL)
    cond = jnp.concatenate([c_ctx[None, :], c, jnp.zeros((N_COND - 1 - DEC_BATCH, D_MODEL), F32)], axis=0)
    mod = _ada(cond, ada_w, ada_b)
    nk = ATT_KV_HEADS * ATT_HEAD_DIM

    def router_params(i):
        pad = jnp.zeros((D_MODEL, ROUTER_LANES - N_EXPERTS - N_GROUPS), F32)
        w = jnp.concatenate([moe_w_expert[i], moe_w_group[i], pad], axis=1)
        b = jnp.concatenate([moe_b_expert[i], moe_b_group[i], pad[0]])[None, :]
        return w, b

    tile = lambda i: (i, 0)
    blk = pl.BlockSpec((TM, D_MODEL), tile)

    prompt_blk = pl.BlockSpec((TM, D_MODEL), _prompt_tile)
    sample_blk = pl.BlockSpec((TM, D_MODEL), _sample_tile)
    q, v, g, lff, kf, lfb, kb = _hg_in(xp, xs, mod[0], norm_w[0, 0][None, :], hg_w_in[0].astype(BF16), hg_lb_logits)
    o_f, o_b, state_new = _gla(q, v, lff, kf, lfb, kb, state_hgrn)
    wr, br = router_params(0)
    routed = _mixer_out(
        _hg_out_kernel, "hgrn_out_route", (o_f, o_b, g, hg_onorm[0][None, :], xp, xs),
        (blk, blk, blk, pl.BlockSpec((1, HG_DK), lambda i: (0, 0)), prompt_blk, sample_blk),
        mod[0], norm_w[0, 1][None, :], hg_w_out[0].astype(BF16), wr, br)
    x = _moe_layer(*routed, mod[0], moe_w1[0], moe_w3[0], moe_w2[0])

    cos, sin = _rope_tables()
    qa, ka, va, k_ctx, v_ctx = _at_in(x, mod[1], norm_w[1, 0][None, :], at_w_in[0].astype(BF16), cos, sin)
    attn_ctx, attn_lat = _attention(qa, ka, va, cache_k[:, 0].reshape(DEC_BATCH, PAST_LEN, nk),
                                    cache_v[:, 0].reshape(DEC_BATCH, PAST_LEN, nk), at_sink[0])
    wr, br = router_params(1)
    routed = _mixer_out(
        _at_out_kernel, "attn_out_route", (attn_ctx, attn_lat, x), (prompt_blk, sample_blk, blk),
        mod[1], norm_w[1, 1][None, :], at_w_out[0].astype(BF16), wr, br)
    y_prompt, y_sample = _moe_layer(*routed, mod[1], moe_w1[1], moe_w3[1], moe_w2[1], final_norm[None, :])

    cache_shape = (BATCH, 1, SEQ, ATT_KV_HEADS, ATT_HEAD_DIM)
    return (y_prompt.reshape(BATCH, SEQ, D_MODEL), y_sample.reshape(DEC_BATCH, DEC_SEQ, D_MODEL), state_new,
            k_ctx.reshape(cache_shape), v_ctx.reshape(cache_shape))
```

```python
import functools

import jax
import jax.numpy as jnp
from jax import lax
from jax.experimental import pallas as pl
from jax.experimental.pallas import tpu as pltpu
from jax.experimental.pallas import tpu_sc as plsc

F32 = jnp.float32
BF16 = jnp.bfloat16

D_MODEL = 1024
BATCH = 16
SEQ = 256
DEC_BATCH = 2
DEC_SEQ = 1024
PAST_LEN = 512
GRID_W = 64
HG_HEADS = 8
HG_DK = 128
CHUNK = 16
ATT_HEAD_DIM = 64
ATT_Q_HEADS = 16
ATT_KV_HEADS = 4
ATT_GROUP = 4
WINDOW = 128
BLOCK = 128
ROPE_HALF = 32
ROPE_BASE = 10000.0
N_GROUPS = 4
EXPERTS_PER_GROUP = 8
N_EXPERTS = 32
D_EXPERT = 256
EPS = 1e-6

N_PROMPT = BATCH * SEQ
N_TOK = N_PROMPT + DEC_BATCH * DEC_SEQ
TM = 256
N_TILES = N_TOK // TM
PROMPT_TILES = N_PROMPT // TM
TILES_PER_DEC_SEQ = DEC_SEQ // TM
LANES = 128
N_COND = 8
ROUTER_LANES = 128
GLA_UNROLL = 2
DECAY_CLAMP = 60.0
PACK_WORDS = D_MODEL // 4
MOE_TILE = 128
MOE_ROWS = 2 * N_TOK + N_EXPERTS * MOE_TILE
MIB = 1024 * 1024


def _params(vmem_mib, *semantics):
    return pltpu.CompilerParams(dimension_semantics=semantics, vmem_limit_bytes=vmem_mib * MIB)


def _tile_cond(i):
    return jnp.where(i < PROMPT_TILES, 0, 1 + (i - PROMPT_TILES) // TILES_PER_DEC_SEQ)


def _prompt_tile(i):
    return (jnp.minimum(i, PROMPT_TILES - 1), 0)


def _sample_tile(i):
    return (jnp.maximum(i - PROMPT_TILES, 0), 0)


def _stream_tile(prompt_ref, sample_ref):
    return jnp.where(pl.program_id(0) < PROMPT_TILES, prompt_ref[...], sample_ref[...])


def _mod_row(mod_ref, cond, which):
    return mod_ref[pl.ds(cond, 1), which * D_MODEL:(which + 1) * D_MODEL]


def _norm_mod(x, nw, shift, scale):
    y = x * lax.rsqrt(jnp.mean(x * x, axis=-1, keepdims=True) + EPS)
    return (y * nw) * (1.0 + scale) + shift


def _silu(x):
    return x * jax.nn.sigmoid(x)


def _ada_kernel(c_ref, w_ref, b_ref, o_ref):
    s = _silu(c_ref[...])
    o_ref[0] = jnp.dot(s, w_ref[0], precision=lax.Precision.HIGHEST, preferred_element_type=F32) + b_ref[0]


def _ada(cond, ada_w, ada_b):
    depth, _, n = ada_w.shape
    tn = 1536
    return pl.pallas_call(
        _ada_kernel,
        out_shape=jax.ShapeDtypeStruct((depth, N_COND, n), F32),
        grid=(depth, n // tn),
        in_specs=[
            pl.BlockSpec((N_COND, D_MODEL), lambda l, j: (0, 0)),
            pl.BlockSpec((1, D_MODEL, tn), lambda l, j: (l, 0, j)),
            pl.BlockSpec((1, 1, tn), lambda l, j: (l, 0, j)),
        ],
        out_specs=pl.BlockSpec((1, N_COND, tn), lambda l, j: (l, 0, j)),
        compiler_params=_params(40, "arbitrary", "arbitrary"),
        name="ada_modulation",
    )(cond, ada_w, ada_b.reshape(depth, 1, n))


def _hg_in_kernel(xp_ref, xs_ref, mod_ref, nw_ref, w_ref, lbl_ref,
                  q_ref, v_ref, g_ref, lff_ref, kf_ref, lfb_ref, kb_ref):
    cond = _tile_cond(pl.program_id(0))
    x = _stream_tile(xp_ref, xs_ref)
    h = _norm_mod(x, nw_ref[...], _mod_row(mod_ref, cond, 0), _mod_row(mod_ref, cond, 1)).astype(BF16)

    def proj(c):
        return jnp.dot(h, w_ref[:, c * D_MODEL:(c + 1) * D_MODEL], preferred_element_type=F32)

    l0, l1, l2 = lbl_ref[0], lbl_ref[1], lbl_ref[2]
    m = jnp.maximum(jnp.maximum(l0, l1), l2)
    e0, e1, e2 = jnp.exp(l0 - m), jnp.exp(l1 - m), jnp.exp(l2 - m)
    lb = e0 / (e0 + e1 + e2)

    q_ref[...] = proj(0).astype(BF16)
    v_ref[...] = proj(1).astype(BF16)
    for d, (lf_ref, k_ref) in enumerate(((lff_ref, kf_ref), (lfb_ref, kb_ref))):
        lbd = lb[d:d + 1, :]
        f = lbd + (1.0 - lbd) * jax.nn.sigmoid(proj(2 + d))
        lf_ref[...] = jnp.log(f)
        k_ref[...] = (1.0 - f).astype(BF16)
    g_ref[...] = proj(4).astype(BF16)


def _hg_in(xp, xs, mod, nw, w_in, lb_logits):
    tile = lambda i: (i, 0)
    fixed2 = lambda i: (0, 0)
    bf = jax.ShapeDtypeStruct((N_TOK, D_MODEL), BF16)
    ff = jax.ShapeDtypeStruct((N_TOK, D_MODEL), F32)
    blk = pl.BlockSpec((TM, D_MODEL), tile)
    return pl.pallas_call(
        _hg_in_kernel,
        out_shape=(bf, bf, bf, ff, bf, ff, bf),
        grid=(N_TILES,),
        in_specs=[
            pl.BlockSpec((TM, D_MODEL), _prompt_tile),
            pl.BlockSpec((TM, D_MODEL), _sample_tile),
            pl.BlockSpec(mod.shape, fixed2),
            pl.BlockSpec((1, D_MODEL), fixed2),
            pl.BlockSpec(w_in.shape, fixed2),
            pl.BlockSpec(lb_logits.shape, lambda i: (0, 0, 0)),
        ],
        out_specs=(blk,) * 7,
        compiler_params=_params(56, "arbitrary"),
        name="hgrn_in_proj",
    )(xp, xs, mod, nw, w_in, lb_logits)


def _gla_direction(reverse, q_ref, v_ref, lf_ref, k_ref, st_ref, sw_ref, o_ref, qd_scr, kd_scr, ki_scr, cd_scr):
    lf = lf_ref[...]
    r = lax.broadcasted_iota(jnp.int32, (TM, TM), 0)
    c = lax.broadcasted_iota(jnp.int32, (TM, TM), 1)
    same = (r // CHUNK) == (c // CHUNK)
    tri = (same & ((c >= r) if reverse else (c <= r))).astype(BF16)
    hi = lf.astype(BF16)
    lo = (lf - hi.astype(F32)).astype(BF16)
    b = jnp.dot(tri, hi, preferred_element_type=F32) + jnp.dot(tri, lo, preferred_element_type=F32)
    b3 = b.reshape(TM // CHUNK, CHUNK, D_MODEL)
    edge = 0 if reverse else CHUNK - 1
    tot = jnp.broadcast_to(b3[:, edge:edge + 1, :], b3.shape).reshape(TM, D_MODEL)
    kf = k_ref[...].astype(F32)
    qd_scr[...] = (q_ref[...].astype(F32) * jnp.exp(b)).astype(BF16)
    kd_scr[...] = (kf * jnp.exp(tot - b)).astype(BF16)
    ki_scr[...] = (kf * jnp.exp(jnp.minimum(-b, DECAY_CLAMP))).astype(BF16)
    cd_scr[...] = jnp.exp(tot)

    tr = lax.broadcasted_iota(jnp.int32, (CHUNK, CHUNK), 0)
    tc = lax.broadcasted_iota(jnp.int32, (CHUNK, CHUNK), 1)
    keep = (tc >= tr) if reverse else (tc <= tr)
    nt = (((1,), (1,)), ((), ()))
    tn = (((0,), (0,)), ((), ()))

    heads = [slice(h * HG_DK, (h + 1) * HG_DK) for h in range(HG_HEADS)]
    for h in range(HG_HEADS):
        sw_ref[0, h] = st_ref[0, h].T.astype(BF16)

    def chunk(ci, src, dst):
        row0 = pl.multiple_of(ci * CHUNK, CHUNK)
        rows = pl.ds(row0, CHUNK)
        for h, cols in enumerate(heads):
            ut = lax.dot_general(v_ref[rows, cols], kd_scr[rows, cols], tn, preferred_element_type=F32)
            new = st_ref[src, h] * cd_scr[pl.ds(row0, 1), cols] + ut
            st_ref[dst, h] = new
            sw_ref[dst, h] = new.T.astype(BF16)
        a = [lax.dot_general(qd_scr[rows, cols], ki_scr[rows, cols], nt, preferred_element_type=F32) for cols in heads]
        inter = [jnp.dot(qd_scr[rows, cols], sw_ref[src, h], preferred_element_type=F32)
                 for h, cols in enumerate(heads)]
        for h, cols in enumerate(heads):
            am = jnp.where(keep, a[h], 0.0).astype(BF16)
            o_ref[rows, cols] = jnp.dot(am, v_ref[rows, cols], preferred_element_type=F32) + inter[h]

    n_chunks = TM // CHUNK

    def chunk_group(it, carry):
        for p in range(GLA_UNROLL):
            step = GLA_UNROLL * it + p
            chunk((n_chunks - 1 - step) if reverse else step, p % 2, 1 - p % 2)
        return carry

    lax.fori_loop(0, n_chunks // GLA_UNROLL, chunk_group, 0)


def _gla_kernel(qf_ref, vf_ref, lff_ref, kf_ref, qb_ref, vb_ref, lfb_ref, kb_ref, s0_ref,
                of_ref, ob_ref, sout_ref, stf_scr, stb_scr, sw_scr, qd_scr, kd_scr, ki_scr, cd_scr):
    i = pl.program_id(0)
    is_prompt = i < PROMPT_TILES
    first = jnp.logical_or(is_prompt, (i - PROMPT_TILES) % TILES_PER_DEC_SEQ == 0)

    @pl.when(first)
    def _():
        keep0 = jnp.where(is_prompt, 0.0, 1.0)
        for h in range(HG_HEADS):
            stf_scr[0, h] = s0_ref[0, 0, 0, h].T * keep0
            stb_scr[0, h] = s0_ref[0, 0, 1, h].T * keep0

    _gla_direction(False, qf_ref, vf_ref, lff_ref, kf_ref, stf_scr, sw_scr, of_ref, qd_scr, kd_scr, ki_scr, cd_scr)
    _gla_direction(True, qb_ref, vb_ref, lfb_ref, kb_ref, stb_scr, sw_scr, ob_ref, qd_scr, kd_scr, ki_scr, cd_scr)

    @pl.when(is_prompt)
    def _():
        for h in range(HG_HEADS):
            sout_ref[0, 0, 0, h] = stf_scr[0, h].T
            sout_ref[0, 0, 1, h] = stb_scr[0, h].T


def _gla(q, v, lff, kf, lfb, kb, state_hgrn):
    def fwd_tile(i):
        return (i, 0)

    def bwd_tile(i):
        j = (i - PROMPT_TILES) % TILES_PER_DEC_SEQ
        return (jnp.where(i < PROMPT_TILES, i, i - j + (TILES_PER_DEC_SEQ - 1 - j)), 0)

    def s0_idx(i):
        return (jnp.maximum(i - PROMPT_TILES, 0) // TILES_PER_DEC_SEQ, 0, 0, 0, 0, 0)

    def sout_idx(i):
        return (jnp.minimum(i, PROMPT_TILES - 1), 0, 0, 0, 0, 0)

    f_blk = pl.BlockSpec((TM, D_MODEL), fwd_tile)
    b_blk = pl.BlockSpec((TM, D_MODEL), bwd_tile)
    st_blk = (1, 1, 2, HG_HEADS, HG_DK, HG_DK)
    return pl.pallas_call(
        _gla_kernel,
        out_shape=(
            jax.ShapeDtypeStruct((N_TOK, D_MODEL), F32),
            jax.ShapeDtypeStruct((N_TOK, D_MODEL), F32),
            jax.ShapeDtypeStruct((BATCH,) + st_blk[1:], F32),
        ),
        grid=(N_TILES,),
        in_specs=[f_blk, f_blk, f_blk, f_blk, b_blk, b_blk, b_blk, b_blk, pl.BlockSpec(st_blk, s0_idx)],
        out_specs=(f_blk, b_blk, pl.BlockSpec(st_blk, sout_idx)),
        scratch_shapes=[
            pltpu.VMEM((2, HG_HEADS, HG_DK, HG_DK), F32),
            pltpu.VMEM((2, HG_HEADS, HG_DK, HG_DK), F32),
            pltpu.VMEM((2, HG_HEADS, HG_DK, HG_DK), BF16),
            pltpu.VMEM((TM, D_MODEL), BF16),
            pltpu.VMEM((TM, D_MODEL), BF16),
            pltpu.VMEM((TM, D_MODEL), BF16),
            pltpu.VMEM((TM, D_MODEL), F32),
        ],
        compiler_params=_params(48, "arbitrary"),
        name="hgrn_recurrence",
    )(q, v, lff, kf, q, v, lfb, kb, state_hgrn)


def _route(logits):
    lane = lax.broadcasted_iota(jnp.int32, logits.shape, 1)
    big = jnp.int32(ROUTER_LANES)
    neg = jnp.float32(-jnp.inf)
    is_group = (lane >= N_EXPERTS) & (lane < N_EXPERTS + N_GROUPS)
    gl = jnp.where(is_group, logits, neg)
    gmax = jnp.max(gl, axis=-1, keepdims=True)
    gsum = jnp.sum(jnp.exp(gl - gmax), axis=-1, keepdims=True)
    p_g = 1.0 / gsum
    g_sel = jnp.min(jnp.where(is_group & (gl == gmax), lane - N_EXPERTS, big), axis=-1, keepdims=True)
    in_sel = (lane < N_EXPERTS) & ((lane // EXPERTS_PER_GROUP) == g_sel)
    el = jnp.where(in_sel, logits, neg)
    m1 = jnp.max(el, axis=-1, keepdims=True)
    i1 = jnp.min(jnp.where(in_sel & (el == m1), lane, big), axis=-1, keepdims=True)
    el2 = jnp.where(lane == i1, neg, el)
    m2 = jnp.max(el2, axis=-1, keepdims=True)
    i2 = jnp.min(jnp.where(in_sel & (lane != i1) & (el2 == m2), lane, big), axis=-1, keepdims=True)
    e2 = jnp.exp(m2 - m1)
    return i1, i2, p_g / (1.0 + e2), p_g * e2 / (1.0 + e2)


def _pack_rows(x):
    q = PACK_WORDS
    bits = pltpu.bitcast(x.astype(BF16).astype(F32), jnp.uint32)
    return [(bits[:, (2 + h) * q:(3 + h) * q] & jnp.uint32(0xFFFF0000)) | (bits[:, h * q:(h + 1) * q] >> 16)
            for h in range(2)]


def _unpack_rows(half0, half1):
    lo = lambda w: pltpu.bitcast(w << 16, F32).astype(BF16)
    hi = lambda w: pltpu.bitcast(w & jnp.uint32(0xFFFF0000), F32).astype(BF16)
    return [lo(half0), lo(half1), hi(half0), hi(half1)]


def _mixer_tail(mix_bf16, x, mod_ref, nw2_ref, wo_ref, wr_ref, br_ref,
                x1_ref, h2p_ref, ri_ref, rw_ref, cnt_ref, carry_scr):
    i = pl.program_id(0)
    cond = _tile_cond(i)
    out = jnp.dot(mix_bf16, wo_ref[...], preferred_element_type=F32)
    x1 = x + _mod_row(mod_ref, cond, 2) * out
    x1_ref[...] = x1
    h2 = _norm_mod(x1, nw2_ref[...], _mod_row(mod_ref, cond, 3), _mod_row(mod_ref, cond, 4))
    for h, words in enumerate(_pack_rows(h2)):
        h2p_ref[h] = words
    logits = jnp.dot(h2, wr_ref[...], precision=lax.Precision.HIGHEST, preferred_element_type=F32) + br_ref[...]
    i1, i2, w1, w2 = _route(logits)

    @pl.when(i == 0)
    def _():
        carry_scr[...] = jnp.zeros_like(carry_scr)

    lane = lax.broadcasted_iota(jnp.int32, logits.shape, 1)
    chosen = ((lane == i1) | (lane == i2)).astype(BF16)
    r = lax.broadcasted_iota(jnp.int32, (TM, TM), 0)
    c = lax.broadcasted_iota(jnp.int32, (TM, TM), 1)
    before = jnp.dot((c < r).astype(BF16), chosen, preferred_element_type=F32) + carry_scr[...]
    r1 = jnp.sum(jnp.where(lane == i1, before, 0.0), axis=-1, keepdims=True).astype(jnp.int32)
    r2 = jnp.sum(jnp.where(lane == i2, before, 0.0), axis=-1, keepdims=True).astype(jnp.int32)
    total = carry_scr[...] + jnp.sum(chosen.astype(F32), axis=0, keepdims=True)
    carry_scr[...] = total
    cnt_ref[...] = total
    ri_ref[...] = jnp.where(lane == 0, i1, jnp.where(lane == 1, i2, jnp.where(lane == 2, r1, r2)))
    rw_ref[...] = jnp.where(lane == 0, w1, w2)


def _hg_out_kernel(of_ref, ob_ref, g_ref, on_ref, xp_ref, xs_ref, *rest):
    o = of_ref[...] + ob_ref[...]
    parts = []
    for h in range(HG_HEADS):
        oh = o[:, h * HG_DK:(h + 1) * HG_DK]
        parts.append(oh * lax.rsqrt(jnp.mean(oh * oh, axis=-1, keepdims=True) + EPS) * on_ref[...])
    y = jnp.concatenate(parts, axis=1) * _silu(g_ref[...].astype(F32))
    _mixer_tail(y.astype(BF16), _stream_tile(xp_ref, xs_ref), *rest)


def _at_out_kernel(ac_ref, al_ref, x_ref, *rest):
    _mixer_tail(_stream_tile(ac_ref, al_ref), x_ref[...], *rest)


def _mixer_out(kernel_fn, name, mix_inputs, mix_specs, mod, nw2, w_out, w_router, b_router):
    tile = lambda i: (i, 0)
    fixed2 = lambda i: (0, 0)
    blk = pl.BlockSpec((TM, D_MODEL), tile)
    lanes_blk = pl.BlockSpec((TM, ROUTER_LANES), tile)
    return pl.pallas_call(
        kernel_fn,
        out_shape=(
            jax.ShapeDtypeStruct((N_TOK, D_MODEL), F32),
            jax.ShapeDtypeStruct((2, N_TOK, PACK_WORDS), jnp.uint32),
            jax.ShapeDtypeStruct((N_TOK, ROUTER_LANES), jnp.int32),
            jax.ShapeDtypeStruct((N_TOK, ROUTER_LANES), F32),
            jax.ShapeDtypeStruct((1, ROUTER_LANES), F32),
        ),
        grid=(N_TILES,),
        in_specs=list(mix_specs) + [
            pl.BlockSpec(mod.shape, fixed2),
            pl.BlockSpec((1, D_MODEL), fixed2),
            pl.BlockSpec(w_out.shape, fixed2),
            pl.BlockSpec(w_router.shape, fixed2),
            pl.BlockSpec((1, ROUTER_LANES), fixed2),
        ],
        out_specs=(blk, pl.BlockSpec((2, TM, PACK_WORDS), lambda i: (0, i, 0)), lanes_blk, lanes_blk,
                   pl.BlockSpec((1, ROUTER_LANES), fixed2)),
        scratch_shapes=[pltpu.VMEM((1, ROUTER_LANES), F32)],
        compiler_params=_params(40, "arbitrary"),
        name=name,
    )(*mix_inputs, mod, nw2, w_out, w_router, b_router)


def _moe_plan(route_i, counts):
    cnt = counts[0, :N_EXPERTS].astype(jnp.int32)
    padded = ((cnt + MOE_TILE - 1) // MOE_TILE) * MOE_TILE
    ends = jnp.cumsum(padded)
    offs = ends - padded
    experts = route_i[:, 0:2]
    pos = jnp.sum(jnp.where(experts[:, :, None] == jnp.arange(N_EXPERTS)[None, None, :], offs[None, None, :], 0),
                  axis=-1) + route_i[:, 2:4]
    tile_start = jnp.arange(MOE_ROWS // MOE_TILE, dtype=jnp.int32) * MOE_TILE
    tile_expert = jnp.minimum(jnp.sum(ends[None, :] <= tile_start[:, None], axis=1), N_EXPERTS - 1).astype(jnp.int32)
    tile_rows = jnp.clip(offs[tile_expert] + cnt[tile_expert] - tile_start, 0, MOE_TILE).astype(jnp.int32)
    tile_first = (tile_start == offs[tile_expert]).astype(jnp.int32)
    n_active = (ends[-1] // MOE_TILE).astype(jnp.int32).reshape(1)
    return pos.astype(jnp.int32), tile_expert, tile_rows, tile_first, n_active


SC_WINDOW = 128


def _sc_mesh():
    return plsc.VectorSubcoreMesh(core_axis_name="c", subcore_axis_name="s")


def _sc_scatter_rows(x, idx_a, idx_b, n_out_rows):
    n = x.shape[0]

    @functools.partial(pl.kernel, out_type=jax.ShapeDtypeStruct((n_out_rows, PACK_WORDS), x.dtype), mesh=_sc_mesh())
    def scatter(x_hbm, ia_hbm, ib_hbm, o_hbm):
        def body(x_vmem, ia_vmem, ib_vmem):
            pltpu.sync_copy(x_vmem, o_hbm.at[ia_vmem.at[0]])
            pltpu.sync_copy(x_vmem, o_hbm.at[ib_vmem.at[0]])

        idx_spec = pl.BlockSpec((1, SC_WINDOW), index_map=lambda i: (0, i))
        pltpu.emit_pipeline(
            body, grid=(n // SC_WINDOW,),
            in_specs=[pl.BlockSpec((SC_WINDOW, PACK_WORDS), index_map=lambda i: (i, 0)), idx_spec, idx_spec],
            out_specs=[],
            core_axis_name=("c", "s"), dimension_semantics=(pltpu.PARALLEL,),
        )(x_hbm, ia_hbm, ib_hbm)

    return scatter(x, idx_a.reshape(1, n), idx_b.reshape(1, n))


def _sc_gather_rows(table, idx):
    n = idx.shape[0]

    @functools.partial(pl.kernel, out_type=jax.ShapeDtypeStruct((n, PACK_WORDS), table.dtype), mesh=_sc_mesh())
    def gather(t_hbm, i_hbm, o_hbm):
        def body(i_vmem, o_vmem):
            pltpu.sync_copy(t_hbm.at[i_vmem.at[0]], o_vmem)

        pltpu.emit_pipeline(
            body, grid=(n // SC_WINDOW,),
            in_specs=[pl.BlockSpec((1, SC_WINDOW), index_map=lambda i: (0, i))],
            out_specs=[pl.BlockSpec((SC_WINDOW, PACK_WORDS), index_map=lambda i: (i, 0))],
            core_axis_name=("c", "s"), dimension_semantics=(pltpu.PARALLEL,),
        )(i_hbm, o_hbm)

    return gather(table, idx.reshape(1, n))


def _ffn_kernel(te_ref, tr_ref, tf_ref, na_ref, xs_ref, w1_ref, w3_ref, w2_ref, ys_ref, w1_scr, w3_scr, w2_scr):
    t = pl.program_id(0)

    @pl.when(t < na_ref[0])
    def _():
        @pl.when(tf_ref[t] == 1)
        def _():
            w1_scr[...] = w1_ref[0, 0].astype(BF16)
            w3_scr[...] = w3_ref[0, 0].astype(BF16)
            w2_scr[...] = w2_ref[0, 0].astype(BF16)

        row = lax.broadcasted_iota(jnp.int32, (MOE_TILE, PACK_WORDS), 0)
        live = row < tr_ref[t]
        halves = [jnp.where(live, xs_ref[h], jnp.zeros((MOE_TILE, PACK_WORDS), jnp.uint32)) for h in range(2)]
        chunks = _unpack_rows(*halves)

        def up(w_scr):
            acc = jnp.dot(chunks[0], w_scr[:PACK_WORDS, :], preferred_element_type=F32)
            for k in range(1, len(chunks)):
                acc = acc + jnp.dot(chunks[k], w_scr[k * PACK_WORDS:(k + 1) * PACK_WORDS, :],
                                    preferred_element_type=F32)
            return acc

        hid = (_silu(up(w1_scr)) * up(w3_scr)).astype(BF16)
        for h, words in enumerate(_pack_rows(jnp.dot(hid, w2_scr[...], preferred_element_type=F32))):
            ys_ref[h] = words


def _ffn(xs, tile_expert, tile_rows, tile_first, n_active, layer, w1, w3, w2):
    row_tile = lambda t, te, tr, tf, na: (0, jnp.minimum(t, na[0] - 1), 0)
    expert = lambda t, te, tr, tf, na: (layer, te[t], 0, 0)
    return pl.pallas_call(
        _ffn_kernel,
        out_shape=jax.ShapeDtypeStruct((2, MOE_ROWS, PACK_WORDS), jnp.uint32),
        grid_spec=pltpu.PrefetchScalarGridSpec(
            num_scalar_prefetch=4,
            grid=(MOE_ROWS // MOE_TILE,),
            in_specs=[
                pl.BlockSpec((2, MOE_TILE, PACK_WORDS), row_tile),
                pl.BlockSpec((1, 1, D_MODEL, D_EXPERT), expert),
                pl.BlockSpec((1, 1, D_MODEL, D_EXPERT), expert),
                pl.BlockSpec((1, 1, D_EXPERT, D_MODEL), expert),
            ],
            out_specs=pl.BlockSpec((2, MOE_TILE, PACK_WORDS), row_tile),
            scratch_shapes=[
                pltpu.VMEM((D_MODEL, D_EXPERT), BF16),
                pltpu.VMEM((D_MODEL, D_EXPERT), BF16),
                pltpu.VMEM((D_EXPERT, D_MODEL), BF16),
            ],
        ),
        compiler_params=_params(32, "arbitrary"),
        name="moe_experts",
    )(tile_expert, tile_rows, tile_first, n_active, xs, w1, w3, w2)


def _moe_res(x_ref, y_ref, rw_ref, mod_ref):
    cond = _tile_cond(pl.program_id(0))
    rw = rw_ref[...]
    wa = rw[:, 0:1]
    wb = rw[:, 1:2]
    ya = _unpack_rows(y_ref[0], y_ref[1])
    yb = _unpack_rows(y_ref[2], y_ref[3])
    y = jnp.concatenate([wa * a.astype(F32) + wb * b.astype(F32) for a, b in zip(ya, yb)], axis=1)
    return x_ref[...] + _mod_row(mod_ref, cond, 5) * y


def _moe_res_kernel(x_ref, y_ref, rw_ref, mod_ref, o_ref):
    o_ref[...] = _moe_res(x_ref, y_ref, rw_ref, mod_ref)


def _moe_res_final_kernel(x_ref, y_ref, rw_ref, mod_ref, fn_ref, op_ref, os_ref):
    x = _moe_res(x_ref, y_ref, rw_ref, mod_ref)
    y = x * lax.rsqrt(jnp.mean(x * x, axis=-1, keepdims=True) + EPS) * fn_ref[...]
    is_prompt = pl.program_id(0) < PROMPT_TILES

    @pl.when(is_prompt)
    def _():
        op_ref[...] = y

    @pl.when(jnp.logical_not(is_prompt))
    def _():
        os_ref[...] = y


def _moe_combine(x1, y_pairs, route_w, mod, final_norm=None):
    tile = lambda i: (i, 0)
    fixed2 = lambda i: (0, 0)
    blk = pl.BlockSpec((TM, D_MODEL), tile)
    specs = [blk,
             pl.BlockSpec((4, TM, PACK_WORDS), lambda i: (0, i, 0)),
             pl.BlockSpec((TM, ROUTER_LANES), tile),
             pl.BlockSpec(mod.shape, fixed2)]
    args = [x1, y_pairs, route_w, mod]
    if final_norm is None:
        kernel_fn = _moe_res_kernel
        out_shape = jax.ShapeDtypeStruct((N_TOK, D_MODEL), F32)
        out_specs = blk
    else:
        kernel_fn = _moe_res_final_kernel
        specs.append(pl.BlockSpec((1, D_MODEL), fixed2))
        args.append(final_norm)
        out_shape = (jax.ShapeDtypeStruct((N_PROMPT, D_MODEL), F32),
                     jax.ShapeDtypeStruct((N_TOK - N_PROMPT, D_MODEL), F32))
        out_specs = (pl.BlockSpec((TM, D_MODEL), _prompt_tile), pl.BlockSpec((TM, D_MODEL), _sample_tile))
    return pl.pallas_call(
        kernel_fn,
        out_shape=out_shape,
        grid=(N_TILES,),
        in_specs=specs,
        out_specs=out_specs,
        compiler_params=_params(32, "arbitrary"),
        name="moe_combine",
    )(*args)


def _moe_layer(x1, h2p, route_i, route_w, counts, mod, layer, w1, w3, w2, final_norm=None):
    pos, tile_expert, tile_rows, tile_first, n_active = _moe_plan(route_i, counts)
    idx_a = jnp.concatenate([pos[:, 0], pos[:, 0] + MOE_ROWS])
    idx_b = jnp.concatenate([pos[:, 1], pos[:, 1] + MOE_ROWS])
    xs = _sc_scatter_rows(h2p.reshape(2 * N_TOK, PACK_WORDS), idx_a, idx_b, 2 * MOE_ROWS)
    ys = _ffn(xs.reshape(2, MOE_ROWS, PACK_WORDS), tile_expert, tile_rows, tile_first, n_active,
              layer, w1, w3, w2)
    y_pairs = _sc_gather_rows(ys.reshape(2 * MOE_ROWS, PACK_WORDS), jnp.concatenate([idx_a, idx_b]))
    return _moe_combine(x1, y_pairs.reshape(4, N_TOK, PACK_WORDS), route_w, mod, final_norm)


def _swap_rotary_halves(x):
    n = x.shape[-1]
    lane = lax.broadcasted_iota(jnp.int32, x.shape, 1)
    quarter = ROPE_HALF // 2
    return jnp.where((lane % ROPE_HALF) < quarter, pltpu.roll(x, n - quarter, 1), pltpu.roll(x, quarter, 1))


def _at_in_kernel(x_ref, mod_ref, nw_ref, w_ref, cos_ref, sin_ref, q_ref, k_ref, v_ref, kc_ref, vc_ref):
    i = pl.program_id(0)
    cond = _tile_cond(i)
    h = _norm_mod(x_ref[...], nw_ref[...], _mod_row(mod_ref, cond, 0), _mod_row(mod_ref, cond, 1)).astype(BF16)
    nq = ATT_Q_HEADS * ATT_HEAD_DIM
    nk = ATT_KV_HEADS * ATT_HEAD_DIM
    cos = cos_ref[...]
    sin = sin_ref[...]

    def rope(x):
        reps = x.shape[-1] // LANES
        return x * jnp.concatenate([cos] * reps, axis=1) + _swap_rotary_halves(x) * jnp.concatenate([sin] * reps, axis=1)

    q_ref[...] = rope(jnp.dot(h, w_ref[:, :nq], preferred_element_type=F32)).astype(BF16)
    k = rope(jnp.dot(h, w_ref[:, nq:nq + nk], preferred_element_type=F32))
    v = jnp.dot(h, w_ref[:, nq + nk:], preferred_element_type=F32)
    k_ref[...] = k.astype(BF16)
    v_ref[...] = v.astype(BF16)

    @pl.when(i < PROMPT_TILES)
    def _():
        kc_ref[...] = k
        vc_ref[...] = v


def _rope_tables():
    pos = jnp.arange(DEC_SEQ)
    t_row = (pos // GRID_W).astype(F32)
    t_col = (pos % GRID_W).astype(F32)
    inv = ROPE_BASE ** (-jnp.arange(0, ROPE_HALF, 2, dtype=F32) / ROPE_HALF)
    lane = jnp.arange(LANES)
    j = lane % ATT_HEAD_DIM
    freq = inv[(j % ROPE_HALF) % (ROPE_HALF // 2)]
    t = jnp.where((j < ROPE_HALF)[None, :], t_row[:, None], t_col[:, None])
    ang = t * freq[None, :]
    sign = jnp.where((j % ROPE_HALF) < ROPE_HALF // 2, -1.0, 1.0).astype(F32)
    cos = jnp.concatenate([jnp.ones((TM, LANES), F32), jnp.cos(ang)], axis=0)
    sin = jnp.concatenate([jnp.zeros((TM, LANES), F32), jnp.sin(ang) * sign[None, :]], axis=0)
    return cos, sin


def _at_in(x, mod, nw, w_in, cos, sin):
    tile = lambda i: (i, 0)
    fixed2 = lambda i: (0, 0)
    rope_tile = lambda i: (jnp.where(i < PROMPT_TILES, 0, 1 + (i - PROMPT_TILES) % TILES_PER_DEC_SEQ), 0)
    nk = ATT_KV_HEADS * ATT_HEAD_DIM
    return pl.pallas_call(
        _at_in_kernel,
        out_shape=(
            jax.ShapeDtypeStruct((N_TOK, D_MODEL), BF16),
            jax.ShapeDtypeStruct((N_TOK, nk), BF16),
            jax.ShapeDtypeStruct((N_TOK, nk), BF16),
            jax.ShapeDtypeStruct((N_PROMPT, nk), F32),
            jax.ShapeDtypeStruct((N_PROMPT, nk), F32),
        ),
        grid=(N_TILES,),
        in_specs=[
            pl.BlockSpec((TM, D_MODEL), tile),
            pl.BlockSpec(mod.shape, fixed2),
            pl.BlockSpec((1, D_MODEL), fixed2),
            pl.BlockSpec(w_in.shape, fixed2),
            pl.BlockSpec((TM, LANES), rope_tile),
            pl.BlockSpec((TM, LANES), rope_tile),
        ],
        out_specs=(pl.BlockSpec((TM, D_MODEL), tile), pl.BlockSpec((TM, nk), tile), pl.BlockSpec((TM, nk), tile),
                   pl.BlockSpec((TM, nk), _prompt_tile), pl.BlockSpec((TM, nk), _prompt_tile)),
        compiler_params=_params(40, "arbitrary"),
        name="attn_in_proj",
    )(x, mod, nw, w_in, cos, sin)


def _attend(q, k_all, v_all, mask, sink_ref, o_ref):
    group_lanes = ATT_GROUP * ATT_HEAD_DIM
    scale = ATT_HEAD_DIM ** -0.5
    nt = (((1,), (1,)), ((), ()))
    lane = lax.broadcasted_iota(jnp.int32, (q.shape[0], group_lanes), 1)
    for hk in range(ATT_KV_HEADS):
        kh = k_all[:, hk * ATT_HEAD_DIM:(hk + 1) * ATT_HEAD_DIM]
        vh = v_all[:, hk * ATT_HEAD_DIM:(hk + 1) * ATT_HEAD_DIM]
        kt = jnp.concatenate([kh] * ATT_GROUP, axis=1)
        vt = jnp.concatenate([vh] * ATT_GROUP, axis=1)
        qg = q[:, hk * group_lanes:(hk + 1) * group_lanes]
        acc = jnp.zeros((q.shape[0], group_lanes), F32)
        for g in range(ATT_GROUP):
            mine = (lane // ATT_HEAD_DIM) == g
            qm = jnp.where(mine, qg, jnp.zeros_like(qg))
            s = lax.dot_general(qm, kt, nt, preferred_element_type=F32) * scale
            if mask is not None:
                s = jnp.where(mask, s, -jnp.inf)
            sink = sink_ref[hk * ATT_GROUP + g]
            m = jnp.maximum(jnp.max(s, axis=-1, keepdims=True), sink)
            p = jnp.exp(s - m)
            denom = jnp.sum(p, axis=-1, keepdims=True) + jnp.exp(sink - m)
            o = jnp.dot(p.astype(BF16), vt, preferred_element_type=F32) / denom
            acc = acc + jnp.where(mine, o, 0.0)
        o_ref[:, hk * group_lanes:(hk + 1) * group_lanes] = acc.astype(BF16)


def _ctx_attn_kernel(sink_ref, q_ref, k_ref, v_ref, o_ref):
    _attend(q_ref[...], k_ref[...], v_ref[...], None, sink_ref, o_ref)


def _lat_attn_kernel(sink_ref, q_ref, kp_ref, kc_ref, kn_ref, vp_ref, vc_ref, vn_ref, ck_ref, cv_ref, o_ref):
    jb = pl.program_id(1)
    k_all = jnp.concatenate([kp_ref[...], kc_ref[...], kn_ref[...], ck_ref[0].astype(BF16)], axis=0)
    v_all = jnp.concatenate([vp_ref[...], vc_ref[...], vn_ref[...], cv_ref[0].astype(BF16)], axis=0)
    nkeys = 3 * BLOCK + PAST_LEN
    qi = lax.broadcasted_iota(jnp.int32, (BLOCK, nkeys), 0)
    kj = lax.broadcasted_iota(jnp.int32, (BLOCK, nkeys), 1)
    qpos = jb * BLOCK + qi
    kpos = (jb - 1) * BLOCK + kj
    local_ok = (jnp.abs(qpos - kpos) <= WINDOW) & (kpos >= 0) & (kpos < DEC_SEQ)
    mask = (kj >= 3 * BLOCK) | local_ok
    _attend(q_ref[...], k_all, v_all, mask, sink_ref, o_ref)


def _attention(q, k, v, cache_k, cache_v, sink):
    nk = ATT_KV_HEADS * ATT_HEAD_DIM
    smem = pl.BlockSpec(memory_space=pltpu.SMEM)
    ctx = pl.pallas_call(
        _ctx_attn_kernel,
        out_shape=jax.ShapeDtypeStruct((N_PROMPT, D_MODEL), BF16),
        grid=(BATCH,),
        in_specs=[
            smem,
            pl.BlockSpec((SEQ, D_MODEL), lambda b: (b, 0)),
            pl.BlockSpec((SEQ, nk), lambda b: (b, 0)),
            pl.BlockSpec((SEQ, nk), lambda b: (b, 0)),
        ],
        out_specs=pl.BlockSpec((SEQ, D_MODEL), lambda b: (b, 0)),
        compiler_params=_params(40, "arbitrary"),
        name="context_attention",
    )(sink, q, k, v)

    nb = DEC_SEQ // BLOCK
    base = N_PROMPT // BLOCK
    cur = lambda b, j: (base + b * nb + j, 0)
    prev = lambda b, j: (base + b * nb + jnp.maximum(j - 1, 0), 0)
    nxt = lambda b, j: (base + b * nb + jnp.minimum(j + 1, nb - 1), 0)
    kv_blk = lambda f: pl.BlockSpec((BLOCK, nk), f)
    cache_blk = pl.BlockSpec((1, PAST_LEN, nk), lambda b, j: (b, 0, 0))
    lat = pl.pallas_call(
        _lat_attn_kernel,
        out_shape=jax.ShapeDtypeStruct((DEC_BATCH * DEC_SEQ, D_MODEL), BF16),
        grid=(DEC_BATCH, nb),
        in_specs=[
            smem,
            pl.BlockSpec((BLOCK, D_MODEL), cur),
            kv_blk(prev), kv_blk(cur), kv_blk(nxt),
            kv_blk(prev), kv_blk(cur), kv_blk(nxt),
            cache_blk, cache_blk,
        ],
        out_specs=pl.BlockSpec((BLOCK, D_MODEL), lambda b, j: (b * nb + j, 0)),
        compiler_params=_params(40, "arbitrary", "arbitrary"),
        name="latent_attention",
    )(sink, q, k, k, k, v, v, v, cache_k, cache_v)
    return ctx, lat


def kernel(x_prompt, x_sample, state_hgrn, cache_k, cache_v, c, c_ctx, ada_w, ada_b, norm_w, hg_w_in,
           hg_lb_logits, hg_onorm, hg_w_out, at_w_in, at_sink, at_w_out, moe_w_group, moe_b_group,
           moe_w_expert, moe_b_expert, moe_w1, moe_w3, moe_w2, final_norm):
    xp = x_prompt.reshape(N_PROMPT, D_MODEL)
    xs = x_sample.reshape(N_TOK - N_PROMPT, D_MODEL)
    cond = jnp.concatenate([c_ctx[None, :], c, jnp.zeros((N_COND - 1 - DEC_BATCH, D_MODEL), F32)], axis=0)
    mod = _ada(cond, ada_w, ada_b)
    nk = ATT_KV_HEADS * ATT_HEAD_DIM

    def router_params(i):
        pad = jnp.zeros((D_MODEL, ROUTER_LANES - N_EXPERTS - N_GROUPS), F32)
        w = jnp.concatenate([moe_w_expert[i], moe_w_group[i], pad], axis=1)
        b = jnp.concatenate([moe_b_expert[i], moe_b_group[i], pad[0]])[None, :]
        return w, b

    tile = lambda i: (i, 0)
    blk = pl.BlockSpec((TM, D_MODEL), tile)

    prompt_blk = pl.BlockSpec((TM, D_MODEL), _prompt_tile)
    sample_blk = pl.BlockSpec((TM, D_MODEL), _sample_tile)
    q, v, g, lff, kf, lfb, kb = _hg_in(xp, xs, mod[0], norm_w[0, 0][None, :], hg_w_in[0].astype(BF16), hg_lb_logits)
    o_f, o_b, state_new = _gla(q, v, lff, kf, lfb, kb, state_hgrn)
    wr, br = router_params(0)
    routed = _mixer_out(
        _hg_out_kernel, "hgrn_out_route", (o_f, o_b, g, hg_onorm[0][None, :], xp, xs),
        (blk, blk, blk, pl.BlockSpec((1, HG_DK), lambda i: (0, 0)), prompt_blk, sample_blk),
        mod[0], norm_w[0, 1][None, :], hg_w_out[0].astype(BF16), wr, br)
    x = _moe_layer(*routed, mod[0], 0, moe_w1, moe_w3, moe_w2)

    cos, sin = _rope_tables()
    qa, ka, va, k_ctx, v_ctx = _at_in(x, mod[1], norm_w[1, 0][None, :], at_w_in[0].astype(BF16), cos, sin)
    attn_ctx, attn_lat = _attention(qa, ka, va, cache_k[:, 0].reshape(DEC_BATCH, PAST_LEN, nk),
                                    cache_v[:, 0].reshape(DEC_BATCH, PAST_LEN, nk), at_sink[0])
    wr, br = router_params(1)
    routed = _mixer_out(
        _at_out_kernel, "attn_out_route", (attn_ctx, attn_lat, x), (prompt_blk, sample_blk, blk),
        mod[1], norm_w[1, 1][None, :], at_w_out[0].astype(BF16), wr, br)
    y_prompt, y_sample = _moe_layer(*routed, mod[1], 1, moe_w1, moe_w3, moe_w2, final_norm[None, :])

    cache_shape = (BATCH, 1, SEQ, ATT_KV_HEADS, ATT_HEAD_DIM)
    return (y_prompt.reshape(BATCH, SEQ, D_MODEL), y_sample.reshape(DEC_BATCH, DEC_SEQ, D_MODEL), state_new,
            k_ctx.reshape(cache_shape), v_ctx.reshape(cache_shape))
```

```python
import functools

import jax
import jax.numpy as jnp
from jax import lax
from jax.experimental import pallas as pl
from jax.experimental.pallas import tpu as pltpu
from jax.experimental.pallas import tpu_sc as plsc

F32 = jnp.float32
BF16 = jnp.bfloat16

D_MODEL = 1024
BATCH = 16
SEQ = 256
DEC_BATCH = 2
DEC_SEQ = 1024
PAST_LEN = 512
GRID_W = 64
HG_HEADS = 8
HG_DK = 128
CHUNK = 16
ATT_HEAD_DIM = 64
ATT_Q_HEADS = 16
ATT_KV_HEADS = 4
ATT_GROUP = 4
WINDOW = 128
BLOCK = 128
ROPE_HALF = 32
ROPE_BASE = 10000.0
N_GROUPS = 4
EXPERTS_PER_GROUP = 8
N_EXPERTS = 32
D_EXPERT = 256
EPS = 1e-6

N_PROMPT = BATCH * SEQ
N_TOK = N_PROMPT + DEC_BATCH * DEC_SEQ
TM = 256
N_TILES = N_TOK // TM
PROMPT_TILES = N_PROMPT // TM
TILES_PER_DEC_SEQ = DEC_SEQ // TM
LANES = 128
N_COND = 8
ROUTER_LANES = 128
GLA_UNROLL = 2
DECAY_CLAMP = 60.0
PACK_WORDS = D_MODEL // 4
MOE_TILE = 256
MOE_ROWS = 2 * N_TOK + N_EXPERTS * MOE_TILE
MIB = 1024 * 1024


def _params(vmem_mib, *semantics):
    return pltpu.CompilerParams(dimension_semantics=semantics, vmem_limit_bytes=vmem_mib * MIB)


def _tile_cond(i):
    return jnp.where(i < PROMPT_TILES, 0, 1 + (i - PROMPT_TILES) // TILES_PER_DEC_SEQ)


def _prompt_tile(i):
    return (jnp.minimum(i, PROMPT_TILES - 1), 0)


def _sample_tile(i):
    return (jnp.maximum(i - PROMPT_TILES, 0), 0)


def _stream_tile(prompt_ref, sample_ref):
    return jnp.where(pl.program_id(0) < PROMPT_TILES, prompt_ref[...], sample_ref[...])


def _mod_row(mod_ref, cond, which):
    return mod_ref[pl.ds(cond, 1), which * D_MODEL:(which + 1) * D_MODEL]


def _norm_mod(x, nw, shift, scale):
    y = x * lax.rsqrt(jnp.mean(x * x, axis=-1, keepdims=True) + EPS)
    return (y * nw) * (1.0 + scale) + shift


def _silu(x):
    return x * jax.nn.sigmoid(x)


def _ada_kernel(c_ref, w_ref, b_ref, o_ref):
    s = _silu(c_ref[...])
    o_ref[0] = jnp.dot(s, w_ref[0], precision=lax.Precision.HIGHEST, preferred_element_type=F32) + b_ref[0]


def _ada(cond, ada_w, ada_b):
    depth, _, n = ada_w.shape
    tn = 1536
    return pl.pallas_call(
        _ada_kernel,
        out_shape=jax.ShapeDtypeStruct((depth, N_COND, n), F32),
        grid=(depth, n // tn),
        in_specs=[
            pl.BlockSpec((N_COND, D_MODEL), lambda l, j: (0, 0)),
            pl.BlockSpec((1, D_MODEL, tn), lambda l, j: (l, 0, j)),
            pl.BlockSpec((1, 1, tn), lambda l, j: (l, 0, j)),
        ],
        out_specs=pl.BlockSpec((1, N_COND, tn), lambda l, j: (l, 0, j)),
        compiler_params=_params(40, "arbitrary", "arbitrary"),
        name="ada_modulation",
    )(cond, ada_w, ada_b.reshape(depth, 1, n))


def _hg_in_kernel(xp_ref, xs_ref, mod_ref, nw_ref, w_ref, lbl_ref,
                  q_ref, v_ref, g_ref, lff_ref, kf_ref, lfb_ref, kb_ref):
    cond = _tile_cond(pl.program_id(0))
    x = _stream_tile(xp_ref, xs_ref)
    h = _norm_mod(x, nw_ref[...], _mod_row(mod_ref, cond, 0), _mod_row(mod_ref, cond, 1)).astype(BF16)

    def proj(c):
        return jnp.dot(h, w_ref[:, c * D_MODEL:(c + 1) * D_MODEL], preferred_element_type=F32)

    l0, l1, l2 = lbl_ref[0], lbl_ref[1], lbl_ref[2]
    m = jnp.maximum(jnp.maximum(l0, l1), l2)
    e0, e1, e2 = jnp.exp(l0 - m), jnp.exp(l1 - m), jnp.exp(l2 - m)
    lb = e0 / (e0 + e1 + e2)

    q_ref[...] = proj(0).astype(BF16)
    v_ref[...] = proj(1).astype(BF16)
    for d, (lf_ref, k_ref) in enumerate(((lff_ref, kf_ref), (lfb_ref, kb_ref))):
        lbd = lb[d:d + 1, :]
        f = lbd + (1.0 - lbd) * jax.nn.sigmoid(proj(2 + d))
        lf_ref[...] = jnp.log(f)
        k_ref[...] = (1.0 - f).astype(BF16)
    g_ref[...] = proj(4).astype(BF16)


def _hg_in(xp, xs, mod, nw, w_in, lb_logits):
    tile = lambda i: (i, 0)
    fixed2 = lambda i: (0, 0)
    bf = jax.ShapeDtypeStruct((N_TOK, D_MODEL), BF16)
    ff = jax.ShapeDtypeStruct((N_TOK, D_MODEL), F32)
    blk = pl.BlockSpec((TM, D_MODEL), tile)
    return pl.pallas_call(
        _hg_in_kernel,
        out_shape=(bf, bf, bf, ff, bf, ff, bf),
        grid=(N_TILES,),
        in_specs=[
            pl.BlockSpec((TM, D_MODEL), _prompt_tile),
            pl.BlockSpec((TM, D_MODEL), _sample_tile),
            pl.BlockSpec(mod.shape, fixed2),
            pl.BlockSpec((1, D_MODEL), fixed2),
            pl.BlockSpec(w_in.shape, fixed2),
            pl.BlockSpec(lb_logits.shape, lambda i: (0, 0, 0)),
        ],
        out_specs=(blk,) * 7,
        compiler_params=_params(56, "arbitrary"),
        name="hgrn_in_proj",
    )(xp, xs, mod, nw, w_in, lb_logits)


def _gla_direction(reverse, q_ref, v_ref, lf_ref, k_ref, st_ref, sw_ref, o_ref, qd_scr, kd_scr, ki_scr, cd_scr):
    lf = lf_ref[...]
    r = lax.broadcasted_iota(jnp.int32, (TM, TM), 0)
    c = lax.broadcasted_iota(jnp.int32, (TM, TM), 1)
    same = (r // CHUNK) == (c // CHUNK)
    tri = (same & ((c >= r) if reverse else (c <= r))).astype(BF16)
    hi = lf.astype(BF16)
    lo = (lf - hi.astype(F32)).astype(BF16)
    b = jnp.dot(tri, hi, preferred_element_type=F32) + jnp.dot(tri, lo, preferred_element_type=F32)
    b3 = b.reshape(TM // CHUNK, CHUNK, D_MODEL)
    edge = 0 if reverse else CHUNK - 1
    tot = jnp.broadcast_to(b3[:, edge:edge + 1, :], b3.shape).reshape(TM, D_MODEL)
    kf = k_ref[...].astype(F32)
    qd_scr[...] = (q_ref[...].astype(F32) * jnp.exp(b)).astype(BF16)
    kd_scr[...] = (kf * jnp.exp(tot - b)).astype(BF16)
    ki_scr[...] = (kf * jnp.exp(jnp.minimum(-b, DECAY_CLAMP))).astype(BF16)
    cd_scr[...] = jnp.exp(tot)

    tr = lax.broadcasted_iota(jnp.int32, (CHUNK, CHUNK), 0)
    tc = lax.broadcasted_iota(jnp.int32, (CHUNK, CHUNK), 1)
    keep = (tc >= tr) if reverse else (tc <= tr)
    nt = (((1,), (1,)), ((), ()))
    tn = (((0,), (0,)), ((), ()))

    heads = [slice(h * HG_DK, (h + 1) * HG_DK) for h in range(HG_HEADS)]
    for h in range(HG_HEADS):
        sw_ref[0, h] = st_ref[0, h].T.astype(BF16)

    def chunk(ci, src, dst):
        row0 = pl.multiple_of(ci * CHUNK, CHUNK)
        rows = pl.ds(row0, CHUNK)
        for h, cols in enumerate(heads):
            ut = lax.dot_general(v_ref[rows, cols], kd_scr[rows, cols], tn, preferred_element_type=F32)
            new = st_ref[src, h] * cd_scr[pl.ds(row0, 1), cols] + ut
            st_ref[dst, h] = new
            sw_ref[dst, h] = new.T.astype(BF16)
        a = [lax.dot_general(qd_scr[rows, cols], ki_scr[rows, cols], nt, preferred_element_type=F32) for cols in heads]
        inter = [jnp.dot(qd_scr[rows, cols], sw_ref[src, h], preferred_element_type=F32)
                 for h, cols in enumerate(heads)]
        for h, cols in enumerate(heads):
            am = jnp.where(keep, a[h], 0.0).astype(BF16)
            o_ref[rows, cols] = jnp.dot(am, v_ref[rows, cols], preferred_element_type=F32) + inter[h]

    n_chunks = TM // CHUNK

    def chunk_group(it, carry):
        for p in range(GLA_UNROLL):
            step = GLA_UNROLL * it + p
            chunk((n_chunks - 1 - step) if reverse else step, p % 2, 1 - p % 2)
        return carry

    lax.fori_loop(0, n_chunks // GLA_UNROLL, chunk_group, 0)


def _gla_kernel(qf_ref, vf_ref, lff_ref, kf_ref, qb_ref, vb_ref, lfb_ref, kb_ref, s0_ref,
                of_ref, ob_ref, sout_ref, stf_scr, stb_scr, sw_scr, qd_scr, kd_scr, ki_scr, cd_scr):
    i = pl.program_id(0)
    is_prompt = i < PROMPT_TILES
    first = jnp.logical_or(is_prompt, (i - PROMPT_TILES) % TILES_PER_DEC_SEQ == 0)

    @pl.when(first)
    def _():
        keep0 = jnp.where(is_prompt, 0.0, 1.0)
        for h in range(HG_HEADS):
            stf_scr[0, h] = s0_ref[0, 0, 0, h].T * keep0
            stb_scr[0, h] = s0_ref[0, 0, 1, h].T * keep0

    _gla_direction(False, qf_ref, vf_ref, lff_ref, kf_ref, stf_scr, sw_scr, of_ref, qd_scr, kd_scr, ki_scr, cd_scr)
    _gla_direction(True, qb_ref, vb_ref, lfb_ref, kb_ref, stb_scr, sw_scr, ob_ref, qd_scr, kd_scr, ki_scr, cd_scr)

    @pl.when(is_prompt)
    def _():
        for h in range(HG_HEADS):
            sout_ref[0, 0, 0, h] = stf_scr[0, h].T
            sout_ref[0, 0, 1, h] = stb_scr[0, h].T


def _gla(q, v, lff, kf, lfb, kb, state_hgrn):
    def fwd_tile(i):
        return (i, 0)

    def bwd_tile(i):
        j = (i - PROMPT_TILES) % TILES_PER_DEC_SEQ
        return (jnp.where(i < PROMPT_TILES, i, i - j + (TILES_PER_DEC_SEQ - 1 - j)), 0)

    def s0_idx(i):
        return (jnp.maximum(i - PROMPT_TILES, 0) // TILES_PER_DEC_SEQ, 0, 0, 0, 0, 0)

    def sout_idx(i):
        return (jnp.minimum(i, PROMPT_TILES - 1), 0, 0, 0, 0, 0)

    f_blk = pl.BlockSpec((TM, D_MODEL), fwd_tile)
    b_blk = pl.BlockSpec((TM, D_MODEL), bwd_tile)
    st_blk = (1, 1, 2, HG_HEADS, HG_DK, HG_DK)
    return pl.pallas_call(
        _gla_kernel,
        out_shape=(
            jax.ShapeDtypeStruct((N_TOK, D_MODEL), F32),
            jax.ShapeDtypeStruct((N_TOK, D_MODEL), F32),
            jax.ShapeDtypeStruct((BATCH,) + st_blk[1:], F32),
        ),
        grid=(N_TILES,),
        in_specs=[f_blk, f_blk, f_blk, f_blk, b_blk, b_blk, b_blk, b_blk, pl.BlockSpec(st_blk, s0_idx)],
        out_specs=(f_blk, b_blk, pl.BlockSpec(st_blk, sout_idx)),
        scratch_shapes=[
            pltpu.VMEM((2, HG_HEADS, HG_DK, HG_DK), F32),
            pltpu.VMEM((2, HG_HEADS, HG_DK, HG_DK), F32),
            pltpu.VMEM((2, HG_HEADS, HG_DK, HG_DK), BF16),
            pltpu.VMEM((TM, D_MODEL), BF16),
            pltpu.VMEM((TM, D_MODEL), BF16),
            pltpu.VMEM((TM, D_MODEL), BF16),
            pltpu.VMEM((TM, D_MODEL), F32),
        ],
        compiler_params=_params(48, "arbitrary"),
        name="hgrn_recurrence",
    )(q, v, lff, kf, q, v, lfb, kb, state_hgrn)


def _route(logits):
    lane = lax.broadcasted_iota(jnp.int32, logits.shape, 1)
    big = jnp.int32(ROUTER_LANES)
    neg = jnp.float32(-jnp.inf)
    is_group = (lane >= N_EXPERTS) & (lane < N_EXPERTS + N_GROUPS)
    gl = jnp.where(is_group, logits, neg)
    gmax = jnp.max(gl, axis=-1, keepdims=True)
    gsum = jnp.sum(jnp.exp(gl - gmax), axis=-1, keepdims=True)
    p_g = 1.0 / gsum
    g_sel = jnp.min(jnp.where(is_group & (gl == gmax), lane - N_EXPERTS, big), axis=-1, keepdims=True)
    in_sel = (lane < N_EXPERTS) & ((lane // EXPERTS_PER_GROUP) == g_sel)
    el = jnp.where(in_sel, logits, neg)
    m1 = jnp.max(el, axis=-1, keepdims=True)
    i1 = jnp.min(jnp.where(in_sel & (el == m1), lane, big), axis=-1, keepdims=True)
    el2 = jnp.where(lane == i1, neg, el)
    m2 = jnp.max(el2, axis=-1, keepdims=True)
    i2 = jnp.min(jnp.where(in_sel & (lane != i1) & (el2 == m2), lane, big), axis=-1, keepdims=True)
    e2 = jnp.exp(m2 - m1)
    return i1, i2, p_g / (1.0 + e2), p_g * e2 / (1.0 + e2)


def _pack_rows(x):
    q = PACK_WORDS
    bits = pltpu.bitcast(x.astype(BF16).astype(F32), jnp.uint32)
    return [(bits[:, (2 + h) * q:(3 + h) * q] & jnp.uint32(0xFFFF0000)) | (bits[:, h * q:(h + 1) * q] >> 16)
            for h in range(2)]


def _unpack_rows(half0, half1):
    lo = lambda w: pltpu.bitcast(w << 16, F32).astype(BF16)
    hi = lambda w: pltpu.bitcast(w & jnp.uint32(0xFFFF0000), F32).astype(BF16)
    return [lo(half0), lo(half1), hi(half0), hi(half1)]


def _mixer_tail(mix_bf16, x, mod_ref, nw2_ref, wo_ref, wr_ref, br_ref,
                x1_ref, h2p_ref, ri_ref, rw_ref, cnt_ref, carry_scr):
    i = pl.program_id(0)
    cond = _tile_cond(i)
    out = jnp.dot(mix_bf16, wo_ref[...], preferred_element_type=F32)
    x1 = x + _mod_row(mod_ref, cond, 2) * out
    x1_ref[...] = x1
    h2 = _norm_mod(x1, nw2_ref[...], _mod_row(mod_ref, cond, 3), _mod_row(mod_ref, cond, 4))
    for h, words in enumerate(_pack_rows(h2)):
        h2p_ref[h] = words
    h_hi = h2.astype(BF16)
    h_lo = (h2 - h_hi.astype(F32)).astype(BF16)
    logits = (jnp.dot(h_hi, wr_ref[0], preferred_element_type=F32)
              + (jnp.dot(h_hi, wr_ref[1], preferred_element_type=F32)
                 + jnp.dot(h_lo, wr_ref[0], preferred_element_type=F32))) + br_ref[...]
    i1, i2, w1, w2 = _route(logits)

    @pl.when(i == 0)
    def _():
        carry_scr[...] = jnp.zeros_like(carry_scr)

    lane = lax.broadcasted_iota(jnp.int32, logits.shape, 1)
    chosen = ((lane == i1) | (lane == i2)).astype(BF16)
    r = lax.broadcasted_iota(jnp.int32, (TM, TM), 0)
    c = lax.broadcasted_iota(jnp.int32, (TM, TM), 1)
    before = jnp.dot((c < r).astype(BF16), chosen, preferred_element_type=F32) + carry_scr[...]
    r1 = jnp.sum(jnp.where(lane == i1, before, 0.0), axis=-1, keepdims=True).astype(jnp.int32)
    r2 = jnp.sum(jnp.where(lane == i2, before, 0.0), axis=-1, keepdims=True).astype(jnp.int32)
    total = carry_scr[...] + jnp.sum(chosen.astype(F32), axis=0, keepdims=True)
    carry_scr[...] = total
    cnt_ref[...] = total
    ri_ref[...] = jnp.where(lane == 0, i1, jnp.where(lane == 1, i2, jnp.where(lane == 2, r1, r2)))
    rw_ref[...] = jnp.where(lane == 0, w1, w2)


def _hg_out_kernel(of_ref, ob_ref, g_ref, on_ref, xp_ref, xs_ref, *rest):
    o = of_ref[...] + ob_ref[...]
    parts = []
    for h in range(HG_HEADS):
        oh = o[:, h * HG_DK:(h + 1) * HG_DK]
        parts.append(oh * lax.rsqrt(jnp.mean(oh * oh, axis=-1, keepdims=True) + EPS) * on_ref[...])
    y = jnp.concatenate(parts, axis=1) * _silu(g_ref[...].astype(F32))
    _mixer_tail(y.astype(BF16), _stream_tile(xp_ref, xs_ref), *rest)


def _at_out_kernel(ac_ref, al_ref, x_ref, *rest):
    _mixer_tail(_stream_tile(ac_ref, al_ref), x_ref[...], *rest)


def _mixer_out(kernel_fn, name, mix_inputs, mix_specs, mod, nw2, w_out, w_router, b_router):
    tile = lambda i: (i, 0)
    fixed2 = lambda i: (0, 0)
    blk = pl.BlockSpec((TM, D_MODEL), tile)
    lanes_blk = pl.BlockSpec((TM, ROUTER_LANES), tile)
    return pl.pallas_call(
        kernel_fn,
        out_shape=(
            jax.ShapeDtypeStruct((N_TOK, D_MODEL), F32),
            jax.ShapeDtypeStruct((2, N_TOK, PACK_WORDS), jnp.uint32),
            jax.ShapeDtypeStruct((N_TOK, ROUTER_LANES), jnp.int32),
            jax.ShapeDtypeStruct((N_TOK, ROUTER_LANES), F32),
            jax.ShapeDtypeStruct((1, ROUTER_LANES), F32),
        ),
        grid=(N_TILES,),
        in_specs=list(mix_specs) + [
            pl.BlockSpec(mod.shape, fixed2),
            pl.BlockSpec((1, D_MODEL), fixed2),
            pl.BlockSpec(w_out.shape, fixed2),
            pl.BlockSpec(w_router.shape, lambda i: (0, 0, 0)),
            pl.BlockSpec((1, ROUTER_LANES), fixed2),
        ],
        out_specs=(blk, pl.BlockSpec((2, TM, PACK_WORDS), lambda i: (0, i, 0)), lanes_blk, lanes_blk,
                   pl.BlockSpec((1, ROUTER_LANES), fixed2)),
        scratch_shapes=[pltpu.VMEM((1, ROUTER_LANES), F32)],
        compiler_params=_params(40, "arbitrary"),
        name=name,
    )(*mix_inputs, mod, nw2, w_out, w_router, b_router)


def _moe_plan(route_i, counts):
    cnt = counts[0, :N_EXPERTS].astype(jnp.int32)
    padded = ((cnt + MOE_TILE - 1) // MOE_TILE) * MOE_TILE
    ends = jnp.cumsum(padded)
    offs = ends - padded
    experts = route_i[:, 0:2]
    pos = jnp.sum(jnp.where(experts[:, :, None] == jnp.arange(N_EXPERTS)[None, None, :], offs[None, None, :], 0),
                  axis=-1) + route_i[:, 2:4]
    tile_start = jnp.arange(MOE_ROWS // MOE_TILE, dtype=jnp.int32) * MOE_TILE
    tile_expert = jnp.minimum(jnp.sum(ends[None, :] <= tile_start[:, None], axis=1), N_EXPERTS - 1).astype(jnp.int32)
    tile_rows = jnp.clip(offs[tile_expert] + cnt[tile_expert] - tile_start, 0, MOE_TILE).astype(jnp.int32)
    tile_first = (tile_start == offs[tile_expert]).astype(jnp.int32)
    n_active = (ends[-1] // MOE_TILE).astype(jnp.int32).reshape(1)
    return pos.astype(jnp.int32), tile_expert, tile_rows, tile_first, n_active


SC_WINDOW = 128


def _sc_mesh():
    return plsc.VectorSubcoreMesh(core_axis_name="c", subcore_axis_name="s")


def _sc_scatter_rows(x, idx_a, idx_b, n_out_rows):
    n = x.shape[0]

    @functools.partial(pl.kernel, out_type=jax.ShapeDtypeStruct((n_out_rows, PACK_WORDS), x.dtype), mesh=_sc_mesh())
    def scatter(x_hbm, ia_hbm, ib_hbm, o_hbm):
        def body(x_vmem, ia_vmem, ib_vmem):
            pltpu.sync_copy(x_vmem, o_hbm.at[ia_vmem.at[0]])
            pltpu.sync_copy(x_vmem, o_hbm.at[ib_vmem.at[0]])

        idx_spec = pl.BlockSpec((1, SC_WINDOW), index_map=lambda i: (0, i))
        pltpu.emit_pipeline(
            body, grid=(n // SC_WINDOW,),
            in_specs=[pl.BlockSpec((SC_WINDOW, PACK_WORDS), index_map=lambda i: (i, 0)), idx_spec, idx_spec],
            out_specs=[],
            core_axis_name=("c", "s"), dimension_semantics=(pltpu.PARALLEL,),
        )(x_hbm, ia_hbm, ib_hbm)

    return scatter(x, idx_a.reshape(1, n), idx_b.reshape(1, n))


def _sc_gather_rows(table, idx):
    n = idx.shape[0]

    @functools.partial(pl.kernel, out_type=jax.ShapeDtypeStruct((n, PACK_WORDS), table.dtype), mesh=_sc_mesh())
    def gather(t_hbm, i_hbm, o_hbm):
        def body(i_vmem, o_vmem):
            pltpu.sync_copy(t_hbm.at[i_vmem.at[0]], o_vmem)

        pltpu.emit_pipeline(
            body, grid=(n // SC_WINDOW,),
            in_specs=[pl.BlockSpec((1, SC_WINDOW), index_map=lambda i: (0, i))],
            out_specs=[pl.BlockSpec((SC_WINDOW, PACK_WORDS), index_map=lambda i: (i, 0))],
            core_axis_name=("c", "s"), dimension_semantics=(pltpu.PARALLEL,),
        )(i_hbm, o_hbm)

    return gather(table, idx.reshape(1, n))


def _ffn_kernel(te_ref, tr_ref, tf_ref, na_ref, xs_ref, w1_ref, w3_ref, w2_ref, ys_ref, w1_scr, w3_scr, w2_scr):
    t = pl.program_id(0)

    @pl.when(t < na_ref[0])
    def _():
        @pl.when(tf_ref[t] == 1)
        def _():
            w1_scr[...] = w1_ref[0, 0].astype(BF16)
            w3_scr[...] = w3_ref[0, 0].astype(BF16)
            w2_scr[...] = w2_ref[0, 0].astype(BF16)

        row = lax.broadcasted_iota(jnp.int32, (MOE_TILE, PACK_WORDS), 0)
        live = row < tr_ref[t]
        halves = [jnp.where(live, xs_ref[h], jnp.zeros((MOE_TILE, PACK_WORDS), jnp.uint32)) for h in range(2)]
        chunks = _unpack_rows(*halves)

        def up(w_scr):
            acc = jnp.dot(chunks[0], w_scr[:PACK_WORDS, :], preferred_element_type=F32)
            for k in range(1, len(chunks)):
                acc = acc + jnp.dot(chunks[k], w_scr[k * PACK_WORDS:(k + 1) * PACK_WORDS, :],
                                    preferred_element_type=F32)
            return acc

        hid = (_silu(up(w1_scr)) * up(w3_scr)).astype(BF16)
        for h, words in enumerate(_pack_rows(jnp.dot(hid, w2_scr[...], preferred_element_type=F32))):
            ys_ref[h] = words


def _ffn(xs, tile_expert, tile_rows, tile_first, n_active, layer, w1, w3, w2):
    row_tile = lambda t, te, tr, tf, na: (0, jnp.minimum(t, na[0] - 1), 0)
    expert = lambda t, te, tr, tf, na: (layer, te[t], 0, 0)
    return pl.pallas_call(
        _ffn_kernel,
        out_shape=jax.ShapeDtypeStruct((2, MOE_ROWS, PACK_WORDS), jnp.uint32),
        grid_spec=pltpu.PrefetchScalarGridSpec(
            num_scalar_prefetch=4,
            grid=(MOE_ROWS // MOE_TILE,),
            in_specs=[
                pl.BlockSpec((2, MOE_TILE, PACK_WORDS), row_tile),
                pl.BlockSpec((1, 1, D_MODEL, D_EXPERT), expert),
                pl.BlockSpec((1, 1, D_MODEL, D_EXPERT), expert),
                pl.BlockSpec((1, 1, D_EXPERT, D_MODEL), expert),
            ],
            out_specs=pl.BlockSpec((2, MOE_TILE, PACK_WORDS), row_tile),
            scratch_shapes=[
                pltpu.VMEM((D_MODEL, D_EXPERT), BF16),
                pltpu.VMEM((D_MODEL, D_EXPERT), BF16),
                pltpu.VMEM((D_EXPERT, D_MODEL), BF16),
            ],
        ),
        compiler_params=_params(32, "arbitrary"),
        name="moe_experts",
    )(tile_expert, tile_rows, tile_first, n_active, xs, w1, w3, w2)


def _moe_res(x_ref, y_ref, rw_ref, mod_ref):
    cond = _tile_cond(pl.program_id(0))
    rw = rw_ref[...]
    wa = rw[:, 0:1]
    wb = rw[:, 1:2]
    ya = _unpack_rows(y_ref[0], y_ref[1])
    yb = _unpack_rows(y_ref[2], y_ref[3])
    y = jnp.concatenate([wa * a.astype(F32) + wb * b.astype(F32) for a, b in zip(ya, yb)], axis=1)
    return x_ref[...] + _mod_row(mod_ref, cond, 5) * y


def _moe_res_kernel(x_ref, y_ref, rw_ref, mod_ref, o_ref):
    o_ref[...] = _moe_res(x_ref, y_ref, rw_ref, mod_ref)


def _moe_res_final_kernel(x_ref, y_ref, rw_ref, mod_ref, fn_ref, op_ref, os_ref):
    x = _moe_res(x_ref, y_ref, rw_ref, mod_ref)
    y = x * lax.rsqrt(jnp.mean(x * x, axis=-1, keepdims=True) + EPS) * fn_ref[...]
    is_prompt = pl.program_id(0) < PROMPT_TILES

    @pl.when(is_prompt)
    def _():
        op_ref[...] = y

    @pl.when(jnp.logical_not(is_prompt))
    def _():
        os_ref[...] = y


def _moe_combine(x1, y_pairs, route_w, mod, final_norm=None):
    tile = lambda i: (i, 0)
    fixed2 = lambda i: (0, 0)
    blk = pl.BlockSpec((TM, D_MODEL), tile)
    specs = [blk,
             pl.BlockSpec((4, TM, PACK_WORDS), lambda i: (0, i, 0)),
             pl.BlockSpec((TM, ROUTER_LANES), tile),
             pl.BlockSpec(mod.shape, fixed2)]
    args = [x1, y_pairs, route_w, mod]
    if final_norm is None:
        kernel_fn = _moe_res_kernel
        out_shape = jax.ShapeDtypeStruct((N_TOK, D_MODEL), F32)
        out_specs = blk
    else:
        kernel_fn = _moe_res_final_kernel
        specs.append(pl.BlockSpec((1, D_MODEL), fixed2))
        args.append(final_norm)
        out_shape = (jax.ShapeDtypeStruct((N_PROMPT, D_MODEL), F32),
                     jax.ShapeDtypeStruct((N_TOK - N_PROMPT, D_MODEL), F32))
        out_specs = (pl.BlockSpec((TM, D_MODEL), _prompt_tile), pl.BlockSpec((TM, D_MODEL), _sample_tile))
    return pl.pallas_call(
        kernel_fn,
        out_shape=out_shape,
        grid=(N_TILES,),
        in_specs=specs,
        out_specs=out_specs,
        compiler_params=_params(32, "arbitrary"),
        name="moe_combine",
    )(*args)


def _moe_layer(x1, h2p, route_i, route_w, counts, mod, layer, w1, w3, w2, final_norm=None):
    pos, tile_expert, tile_rows, tile_first, n_active = _moe_plan(route_i, counts)
    idx_a = jnp.concatenate([pos[:, 0], pos[:, 0] + MOE_ROWS])
    idx_b = jnp.concatenate([pos[:, 1], pos[:, 1] + MOE_ROWS])
    xs = _sc_scatter_rows(h2p.reshape(2 * N_TOK, PACK_WORDS), idx_a, idx_b, 2 * MOE_ROWS)
    ys = _ffn(xs.reshape(2, MOE_ROWS, PACK_WORDS), tile_expert, tile_rows, tile_first, n_active,
              layer, w1, w3, w2)
    y_pairs = _sc_gather_rows(ys.reshape(2 * MOE_ROWS, PACK_WORDS), jnp.concatenate([idx_a, idx_b]))
    return _moe_combine(x1, y_pairs.reshape(4, N_TOK, PACK_WORDS), route_w, mod, final_norm)


def _swap_rotary_halves(x):
    n = x.shape[-1]
    lane = lax.broadcasted_iota(jnp.int32, x.shape, 1)
    quarter = ROPE_HALF // 2
    return jnp.where((lane % ROPE_HALF) < quarter, pltpu.roll(x, n - quarter, 1), pltpu.roll(x, quarter, 1))


def _at_in_kernel(x_ref, mod_ref, nw_ref, w_ref, cos_ref, sin_ref, q_ref, k_ref, v_ref, kc_ref, vc_ref):
    i = pl.program_id(0)
    cond = _tile_cond(i)
    h = _norm_mod(x_ref[...], nw_ref[...], _mod_row(mod_ref, cond, 0), _mod_row(mod_ref, cond, 1)).astype(BF16)
    nq = ATT_Q_HEADS * ATT_HEAD_DIM
    nk = ATT_KV_HEADS * ATT_HEAD_DIM
    cos = cos_ref[...]
    sin = sin_ref[...]

    def rope(x):
        reps = x.shape[-1] // LANES
        return x * jnp.concatenate([cos] * reps, axis=1) + _swap_rotary_halves(x) * jnp.concatenate([sin] * reps, axis=1)

    q_ref[...] = rope(jnp.dot(h, w_ref[:, :nq], preferred_element_type=F32)).astype(BF16)
    k = rope(jnp.dot(h, w_ref[:, nq:nq + nk], preferred_element_type=F32))
    v = jnp.dot(h, w_ref[:, nq + nk:], preferred_element_type=F32)
    k_ref[...] = k.astype(BF16)
    v_ref[...] = v.astype(BF16)

    @pl.when(i < PROMPT_TILES)
    def _():
        kc_ref[...] = k
        vc_ref[...] = v


def _rope_tables():
    pos = jnp.arange(DEC_SEQ)
    t_row = (pos // GRID_W).astype(F32)
    t_col = (pos % GRID_W).astype(F32)
    inv = ROPE_BASE ** (-jnp.arange(0, ROPE_HALF, 2, dtype=F32) / ROPE_HALF)
    lane = jnp.arange(LANES)
    j = lane % ATT_HEAD_DIM
    freq = inv[(j % ROPE_HALF) % (ROPE_HALF // 2)]
    t = jnp.where((j < ROPE_HALF)[None, :], t_row[:, None], t_col[:, None])
    ang = t * freq[None, :]
    sign = jnp.where((j % ROPE_HALF) < ROPE_HALF // 2, -1.0, 1.0).astype(F32)
    cos = jnp.concatenate([jnp.ones((TM, LANES), F32), jnp.cos(ang)], axis=0)
    sin = jnp.concatenate([jnp.zeros((TM, LANES), F32), jnp.sin(ang) * sign[None, :]], axis=0)
    return cos, sin


def _at_in(x, mod, nw, w_in, cos, sin):
    tile = lambda i: (i, 0)
    fixed2 = lambda i: (0, 0)
    rope_tile = lambda i: (jnp.where(i < PROMPT_TILES, 0, 1 + (i - PROMPT_TILES) % TILES_PER_DEC_SEQ), 0)
    nk = ATT_KV_HEADS * ATT_HEAD_DIM
    return pl.pallas_call(
        _at_in_kernel,
        out_shape=(
            jax.ShapeDtypeStruct((N_TOK, D_MODEL), BF16),
            jax.ShapeDtypeStruct((N_TOK, nk), BF16),
            jax.ShapeDtypeStruct((N_TOK, nk), BF16),
            jax.ShapeDtypeStruct((N_PROMPT, nk), F32),
            jax.ShapeDtypeStruct((N_PROMPT, nk), F32),
        ),
        grid=(N_TILES,),
        in_specs=[
            pl.BlockSpec((TM, D_MODEL), tile),
            pl.BlockSpec(mod.shape, fixed2),
            pl.BlockSpec((1, D_MODEL), fixed2),
            pl.BlockSpec(w_in.shape, fixed2),
            pl.BlockSpec((TM, LANES), rope_tile),
            pl.BlockSpec((TM, LANES), rope_tile),
        ],
        out_specs=(pl.BlockSpec((TM, D_MODEL), tile), pl.BlockSpec((TM, nk), tile), pl.BlockSpec((TM, nk), tile),
                   pl.BlockSpec((TM, nk), _prompt_tile), pl.BlockSpec((TM, nk), _prompt_tile)),
        compiler_params=_params(40, "arbitrary"),
        name="attn_in_proj",
    )(x, mod, nw, w_in, cos, sin)


def _attend(q, k_all, v_all, mask, sink_ref, o_ref):
    group_lanes = ATT_GROUP * ATT_HEAD_DIM
    scale = ATT_HEAD_DIM ** -0.5
    nt = (((1,), (1,)), ((), ()))
    lane = lax.broadcasted_iota(jnp.int32, (q.shape[0], group_lanes), 1)
    for hk in range(ATT_KV_HEADS):
        kh = k_all[:, hk * ATT_HEAD_DIM:(hk + 1) * ATT_HEAD_DIM]
        vh = v_all[:, hk * ATT_HEAD_DIM:(hk + 1) * ATT_HEAD_DIM]
        kt = jnp.concatenate([kh] * ATT_GROUP, axis=1)
        vt = jnp.concatenate([vh] * ATT_GROUP, axis=1)
        qg = q[:, hk * group_lanes:(hk + 1) * group_lanes]
        acc = jnp.zeros((q.shape[0], group_lanes), F32)
        for g in range(ATT_GROUP):
            mine = (lane // ATT_HEAD_DIM) == g
            qm = jnp.where(mine, qg, jnp.zeros_like(qg))
            s = lax.dot_general(qm, kt, nt, preferred_element_type=F32) * scale
            if mask is not None:
                s = jnp.where(mask, s, -jnp.inf)
            sink = sink_ref[hk * ATT_GROUP + g]
            m = jnp.maximum(jnp.max(s, axis=-1, keepdims=True), sink)
            p = jnp.exp(s - m)
            denom = jnp.sum(p, axis=-1, keepdims=True) + jnp.exp(sink - m)
            o = jnp.dot(p.astype(BF16), vt, preferred_element_type=F32) / denom
            acc = acc + jnp.where(mine, o, 0.0)
        o_ref[:, hk * group_lanes:(hk + 1) * group_lanes] = acc.astype(BF16)


def _ctx_attn_kernel(sink_ref, q_ref, k_ref, v_ref, o_ref):
    _attend(q_ref[...], k_ref[...], v_ref[...], None, sink_ref, o_ref)


def _lat_attn_kernel(sink_ref, q_ref, kp_ref, kc_ref, kn_ref, vp_ref, vc_ref, vn_ref, ck_ref, cv_ref, o_ref):
    jb = pl.program_id(1)
    k_all = jnp.concatenate([kp_ref[...], kc_ref[...], kn_ref[...], ck_ref[0].astype(BF16)], axis=0)
    v_all = jnp.concatenate([vp_ref[...], vc_ref[...], vn_ref[...], cv_ref[0].astype(BF16)], axis=0)
    nkeys = 3 * BLOCK + PAST_LEN
    qi = lax.broadcasted_iota(jnp.int32, (BLOCK, nkeys), 0)
    kj = lax.broadcasted_iota(jnp.int32, (BLOCK, nkeys), 1)
    qpos = jb * BLOCK + qi
    kpos = (jb - 1) * BLOCK + kj
    local_ok = (jnp.abs(qpos - kpos) <= WINDOW) & (kpos >= 0) & (kpos < DEC_SEQ)
    mask = (kj >= 3 * BLOCK) | local_ok
    _attend(q_ref[...], k_all, v_all, mask, sink_ref, o_ref)


def _attention(q, k, v, cache_k, cache_v, sink):
    nk = ATT_KV_HEADS * ATT_HEAD_DIM
    smem = pl.BlockSpec(memory_space=pltpu.SMEM)
    ctx = pl.pallas_call(
        _ctx_attn_kernel,
        out_shape=jax.ShapeDtypeStruct((N_PROMPT, D_MODEL), BF16),
        grid=(BATCH,),
        in_specs=[
            smem,
            pl.BlockSpec((SEQ, D_MODEL), lambda b: (b, 0)),
            pl.BlockSpec((SEQ, nk), lambda b: (b, 0)),
            pl.BlockSpec((SEQ, nk), lambda b: (b, 0)),
        ],
        out_specs=pl.BlockSpec((SEQ, D_MODEL), lambda b: (b, 0)),
        compiler_params=_params(40, "arbitrary"),
        name="context_attention",
    )(sink, q, k, v)

    nb = DEC_SEQ // BLOCK
    base = N_PROMPT // BLOCK
    cur = lambda b, j: (base + b * nb + j, 0)
    prev = lambda b, j: (base + b * nb + jnp.maximum(j - 1, 0), 0)
    nxt = lambda b, j: (base + b * nb + jnp.minimum(j + 1, nb - 1), 0)
    kv_blk = lambda f: pl.BlockSpec((BLOCK, nk), f)
    cache_blk = pl.BlockSpec((1, PAST_LEN, nk), lambda b, j: (b, 0, 0))
    lat = pl.pallas_call(
        _lat_attn_kernel,
        out_shape=jax.ShapeDtypeStruct((DEC_BATCH * DEC_SEQ, D_MODEL), BF16),
        grid=(DEC_BATCH, nb),
        in_specs=[
            smem,
            pl.BlockSpec((BLOCK, D_MODEL), cur),
            kv_blk(prev), kv_blk(cur), kv_blk(nxt),
            kv_blk(prev), kv_blk(cur), kv_blk(nxt),
            cache_blk, cache_blk,
        ],
        out_specs=pl.BlockSpec((BLOCK, D_MODEL), lambda b, j: (b * nb + j, 0)),
        compiler_params=_params(40, "arbitrary", "arbitrary"),
        name="latent_attention",
    )(sink, q, k, k, k, v, v, v, cache_k, cache_v)
    return ctx, lat


def kernel(x_prompt, x_sample, state_hgrn, cache_k, cache_v, c, c_ctx, ada_w, ada_b, norm_w, hg_w_in,
           hg_lb_logits, hg_onorm, hg_w_out, at_w_in, at_sink, at_w_out, moe_w_group, moe_b_group,
           moe_w_expert, moe_b_expert, moe_w1, moe_w3, moe_w2, final_norm):
    xp = x_prompt.reshape(N_PROMPT, D_MODEL)
    xs = x_sample.reshape(N_TOK - N_PROMPT, D_MODEL)
    cond = jnp.concatenate([c_ctx[None, :], c, jnp.zeros((N_COND - 1 - DEC_BATCH, D_MODEL), F32)], axis=0)
    mod = _ada(cond, ada_w, ada_b)
    nk = ATT_KV_HEADS * ATT_HEAD_DIM

    def router_params(i):
        pad = jnp.zeros((D_MODEL, ROUTER_LANES - N_EXPERTS - N_GROUPS), F32)
        w = jnp.concatenate([moe_w_expert[i], moe_w_group[i], pad], axis=1)
        b = jnp.concatenate([moe_b_expert[i], moe_b_group[i], pad[0]])[None, :]
        hi = w.astype(BF16)
        lo = (w - hi.astype(F32)).astype(BF16)
        return jnp.stack([hi, lo]), b

    tile = lambda i: (i, 0)
    blk = pl.BlockSpec((TM, D_MODEL), tile)

    prompt_blk = pl.BlockSpec((TM, D_MODEL), _prompt_tile)
    sample_blk = pl.BlockSpec((TM, D_MODEL), _sample_tile)
    q, v, g, lff, kf, lfb, kb = _hg_in(xp, xs, mod[0], norm_w[0, 0][None, :], hg_w_in[0].astype(BF16), hg_lb_logits)
    o_f, o_b, state_new = _gla(q, v, lff, kf, lfb, kb, state_hgrn)
    wr, br = router_params(0)
    routed = _mixer_out(
        _hg_out_kernel, "hgrn_out_route", (o_f, o_b, g, hg_onorm[0][None, :], xp, xs),
        (blk, blk, blk, pl.BlockSpec((1, HG_DK), lambda i: (0, 0)), prompt_blk, sample_blk),
        mod[0], norm_w[0, 1][None, :], hg_w_out[0].astype(BF16), wr, br)
    x = _moe_layer(*routed, mod[0], 0, moe_w1, moe_w3, moe_w2)

    cos, sin = _rope_tables()
    qa, ka, va, k_ctx, v_ctx = _at_in(x, mod[1], norm_w[1, 0][None, :], at_w_in[0].astype(BF16), cos, sin)
    attn_ctx, attn_lat = _attention(qa, ka, va, cache_k[:, 0].reshape(DEC_BATCH, PAST_LEN, nk),
                                    cache_v[:, 0].reshape(DEC_BATCH, PAST_LEN, nk), at_sink[0])
    wr, br = router_params(1)
    routed = _mixer_out(
        _at_out_kernel, "attn_out_route", (attn_ctx, attn_lat, x), (prompt_blk, sample_blk, blk),
        mod[1], norm_w[1, 1][None, :], at_w_out[0].astype(BF16), wr, br)
    y_prompt, y_sample = _moe_layer(*routed, mod[1], 1, moe_w1, moe_w3, moe_w2, final_norm[None, :])

    cache_shape = (BATCH, 1, SEQ, ATT_KV_HEADS, ATT_HEAD_DIM)
    return (y_prompt.reshape(BATCH, SEQ, D_MODEL), y_sample.reshape(DEC_BATCH, DEC_SEQ, D_MODEL), state_new,
            k_ctx.reshape(cache_shape), v_ctx.reshape(cache_shape))
```

```python
import functools

import jax
import jax.numpy as jnp
from jax import lax
from jax.experimental import pallas as pl
from jax.experimental.pallas import tpu as pltpu
from jax.experimental.pallas import tpu_sc as plsc

F32 = jnp.float32
BF16 = jnp.bfloat16

D_MODEL = 1024
BATCH = 16
SEQ = 256
DEC_BATCH = 2
DEC_SEQ = 1024
PAST_LEN = 512
GRID_W = 64
HG_HEADS = 8
HG_DK = 128
CHUNK = 16
ATT_HEAD_DIM = 64
ATT_Q_HEADS = 16
ATT_KV_HEADS = 4
ATT_GROUP = 4
WINDOW = 128
BLOCK = 128
ROPE_HALF = 32
ROPE_BASE = 10000.0
N_GROUPS = 4
EXPERTS_PER_GROUP = 8
N_EXPERTS = 32
D_EXPERT = 256
EPS = 1e-6

N_PROMPT = BATCH * SEQ
N_TOK = N_PROMPT + DEC_BATCH * DEC_SEQ
TM = 256
N_TILES = N_TOK // TM
PROMPT_TILES = N_PROMPT // TM
TILES_PER_DEC_SEQ = DEC_SEQ // TM
LANES = 128
N_COND = 8
ROUTER_LANES = 128
GLA_UNROLL = 8
DECAY_CLAMP = 60.0
PACK_WORDS = D_MODEL // 4
MOE_TILE = 256
MOE_ROWS = 2 * N_TOK + N_EXPERTS * MOE_TILE
MIB = 1024 * 1024


def _params(vmem_mib, *semantics):
    return pltpu.CompilerParams(dimension_semantics=semantics, vmem_limit_bytes=vmem_mib * MIB)


def _tile_cond(i):
    return jnp.where(i < PROMPT_TILES, 0, 1 + (i - PROMPT_TILES) // TILES_PER_DEC_SEQ)


def _prompt_tile(i):
    return (jnp.minimum(i, PROMPT_TILES - 1), 0)


def _sample_tile(i):
    return (jnp.maximum(i - PROMPT_TILES, 0), 0)


def _stream_tile(prompt_ref, sample_ref):
    return jnp.where(pl.program_id(0) < PROMPT_TILES, prompt_ref[...], sample_ref[...])


def _mod_row(mod_ref, cond, which):
    return mod_ref[pl.ds(cond, 1), which * D_MODEL:(which + 1) * D_MODEL]


def _norm_mod(x, nw, shift, scale):
    y = x * lax.rsqrt(jnp.mean(x * x, axis=-1, keepdims=True) + EPS)
    return (y * nw) * (1.0 + scale) + shift


def _silu(x):
    return x * jax.nn.sigmoid(x)


def _ada_kernel(c_ref, w_ref, b_ref, o_ref):
    s = _silu(c_ref[...])
    o_ref[0] = jnp.dot(s, w_ref[0], precision=lax.Precision.HIGHEST, preferred_element_type=F32) + b_ref[0]


def _ada(cond, ada_w, ada_b):
    depth, _, n = ada_w.shape
    tn = 1536
    return pl.pallas_call(
        _ada_kernel,
        out_shape=jax.ShapeDtypeStruct((depth, N_COND, n), F32),
        grid=(depth, n // tn),
        in_specs=[
            pl.BlockSpec((N_COND, D_MODEL), lambda l, j: (0, 0)),
            pl.BlockSpec((1, D_MODEL, tn), lambda l, j: (l, 0, j)),
            pl.BlockSpec((1, 1, tn), lambda l, j: (l, 0, j)),
        ],
        out_specs=pl.BlockSpec((1, N_COND, tn), lambda l, j: (l, 0, j)),
        compiler_params=_params(40, "arbitrary", "arbitrary"),
        name="ada_modulation",
    )(cond, ada_w, ada_b.reshape(depth, 1, n))


def _hg_in_kernel(xp_ref, xs_ref, mod_ref, nw_ref, w_ref, lbl_ref,
                  q_ref, v_ref, g_ref, lff_ref, kf_ref, lfb_ref, kb_ref):
    cond = _tile_cond(pl.program_id(0))
    x = _stream_tile(xp_ref, xs_ref)
    h = _norm_mod(x, nw_ref[...], _mod_row(mod_ref, cond, 0), _mod_row(mod_ref, cond, 1)).astype(BF16)

    def proj(c):
        return jnp.dot(h, w_ref[:, c * D_MODEL:(c + 1) * D_MODEL], preferred_element_type=F32)

    l0, l1, l2 = lbl_ref[0], lbl_ref[1], lbl_ref[2]
    m = jnp.maximum(jnp.maximum(l0, l1), l2)
    e0, e1, e2 = jnp.exp(l0 - m), jnp.exp(l1 - m), jnp.exp(l2 - m)
    lb = e0 / (e0 + e1 + e2)

    q_ref[...] = proj(0).astype(BF16)
    v_ref[...] = proj(1).astype(BF16)
    for d, (lf_ref, k_ref) in enumerate(((lff_ref, kf_ref), (lfb_ref, kb_ref))):
        lbd = lb[d:d + 1, :]
        f = lbd + (1.0 - lbd) * jax.nn.sigmoid(proj(2 + d))
        lf_ref[...] = jnp.log(f)
        k_ref[...] = (1.0 - f).astype(BF16)
    g_ref[...] = proj(4).astype(BF16)


def _hg_in(xp, xs, mod, nw, w_in, lb_logits):
    tile = lambda i: (i, 0)
    fixed2 = lambda i: (0, 0)
    bf = jax.ShapeDtypeStruct((N_TOK, D_MODEL), BF16)
    ff = jax.ShapeDtypeStruct((N_TOK, D_MODEL), F32)
    blk = pl.BlockSpec((TM, D_MODEL), tile)
    return pl.pallas_call(
        _hg_in_kernel,
        out_shape=(bf, bf, bf, ff, bf, ff, bf),
        grid=(N_TILES,),
        in_specs=[
            pl.BlockSpec((TM, D_MODEL), _prompt_tile),
            pl.BlockSpec((TM, D_MODEL), _sample_tile),
            pl.BlockSpec(mod.shape, fixed2),
            pl.BlockSpec((1, D_MODEL), fixed2),
            pl.BlockSpec(w_in.shape, fixed2),
            pl.BlockSpec(lb_logits.shape, lambda i: (0, 0, 0)),
        ],
        out_specs=(blk,) * 7,
        compiler_params=_params(56, "arbitrary"),
        name="hgrn_in_proj",
    )(xp, xs, mod, nw, w_in, lb_logits)


def _gla_direction(reverse, q_ref, v_ref, lf_ref, k_ref, st_ref, sw_ref, o_ref,
                   ut_scr, qd_scr, kd_scr, ki_scr, cd_scr):
    lf = lf_ref[...]
    r = lax.broadcasted_iota(jnp.int32, (TM, TM), 0)
    c = lax.broadcasted_iota(jnp.int32, (TM, TM), 1)
    same = (r // CHUNK) == (c // CHUNK)
    tri = (same & ((c >= r) if reverse else (c <= r))).astype(BF16)
    hi = lf.astype(BF16)
    lo = (lf - hi.astype(F32)).astype(BF16)
    b = jnp.dot(tri, hi, preferred_element_type=F32) + jnp.dot(tri, lo, preferred_element_type=F32)
    b3 = b.reshape(TM // CHUNK, CHUNK, D_MODEL)
    edge = 0 if reverse else CHUNK - 1
    tot = jnp.broadcast_to(b3[:, edge:edge + 1, :], b3.shape).reshape(TM, D_MODEL)
    kf = k_ref[...].astype(F32)
    qd_scr[...] = (q_ref[...].astype(F32) * jnp.exp(b)).astype(BF16)
    kd_scr[...] = (kf * jnp.exp(tot - b)).astype(BF16)
    ki_scr[...] = (kf * jnp.exp(jnp.minimum(-b, DECAY_CLAMP))).astype(BF16)
    cd_scr[...] = jnp.exp(tot)

    tr = lax.broadcasted_iota(jnp.int32, (CHUNK, CHUNK), 0)
    tc = lax.broadcasted_iota(jnp.int32, (CHUNK, CHUNK), 1)
    keep = (tc >= tr) if reverse else (tc <= tr)
    nt = (((1,), (1,)), ((), ()))
    tn = (((0,), (0,)), ((), ()))

    heads = [slice(h * HG_DK, (h + 1) * HG_DK) for h in range(HG_HEADS)]
    for h in range(HG_HEADS):
        sw_ref[0, h] = st_ref[0, h].T.astype(BF16)

    n_chunks = TM // CHUNK

    def chunk_rows(step):
        ci = (n_chunks - 1 - step) if reverse else step
        row0 = pl.multiple_of(ci * CHUNK, CHUNK)
        return row0, pl.ds(row0, CHUNK)

    def key_value_product(rows):
        for h, cols in enumerate(heads):
            ut_scr[h] = lax.dot_general(v_ref[rows, cols], kd_scr[rows, cols], tn, preferred_element_type=F32)

    key_value_product(chunk_rows(0)[1])

    def chunk(step, src, dst):
        row0, rows = chunk_rows(step)
        cd_row = cd_scr[pl.ds(row0, 1), :]
        for h, cols in enumerate(heads):
            new = st_ref[src, h] * cd_row[:, cols] + ut_scr[h]
            st_ref[dst, h] = new
            sw_ref[dst, h] = new.T.astype(BF16)
        a = [lax.dot_general(qd_scr[rows, cols], ki_scr[rows, cols], nt, preferred_element_type=F32) for cols in heads]
        inter = [jnp.dot(qd_scr[rows, cols], sw_ref[src, h], preferred_element_type=F32)
                 for h, cols in enumerate(heads)]
        key_value_product(chunk_rows(jnp.minimum(step + 1, n_chunks - 1))[1])
        for h, cols in enumerate(heads):
            am = jnp.where(keep, a[h], 0.0).astype(BF16)
            o_ref[rows, cols] = jnp.dot(am, v_ref[rows, cols], preferred_element_type=F32) + inter[h]

    def chunk_group(it, carry):
        for p in range(GLA_UNROLL):
            chunk(GLA_UNROLL * it + p, p % 2, 1 - p % 2)
        return carry

    lax.fori_loop(0, n_chunks // GLA_UNROLL, chunk_group, 0)


def _gla_kernel(qf_ref, vf_ref, lff_ref, kf_ref, qb_ref, vb_ref, lfb_ref, kb_ref, s0_ref,
                of_ref, ob_ref, sout_ref, stf_scr, stb_scr, sw_scr, ut_scr, qd_scr, kd_scr, ki_scr, cd_scr):
    i = pl.program_id(0)
    is_prompt = i < PROMPT_TILES
    first = jnp.logical_or(is_prompt, (i - PROMPT_TILES) % TILES_PER_DEC_SEQ == 0)

    @pl.when(first)
    def _():
        keep0 = jnp.where(is_prompt, 0.0, 1.0)
        for h in range(HG_HEADS):
            stf_scr[0, h] = s0_ref[0, 0, 0, h].T * keep0
            stb_scr[0, h] = s0_ref[0, 0, 1, h].T * keep0

    _gla_direction(False, qf_ref, vf_ref, lff_ref, kf_ref, stf_scr, sw_scr, of_ref,
                   ut_scr, qd_scr, kd_scr, ki_scr, cd_scr)
    _gla_direction(True, qb_ref, vb_ref, lfb_ref, kb_ref, stb_scr, sw_scr, ob_ref,
                   ut_scr, qd_scr, kd_scr, ki_scr, cd_scr)

    @pl.when(is_prompt)
    def _():
        for h in range(HG_HEADS):
            sout_ref[0, 0, 0, h] = stf_scr[0, h].T
            sout_ref[0, 0, 1, h] = stb_scr[0, h].T


def _gla(q, v, lff, kf, lfb, kb, state_hgrn):
    def fwd_tile(i):
        return (i, 0)

    def bwd_tile(i):
        j = (i - PROMPT_TILES) % TILES_PER_DEC_SEQ
        return (jnp.where(i < PROMPT_TILES, i, i - j + (TILES_PER_DEC_SEQ - 1 - j)), 0)

    def s0_idx(i):
        return (jnp.maximum(i - PROMPT_TILES, 0) // TILES_PER_DEC_SEQ, 0, 0, 0, 0, 0)

    def sout_idx(i):
        return (jnp.minimum(i, PROMPT_TILES - 1), 0, 0, 0, 0, 0)

    f_blk = pl.BlockSpec((TM, D_MODEL), fwd_tile)
    b_blk = pl.BlockSpec((TM, D_MODEL), bwd_tile)
    st_blk = (1, 1, 2, HG_HEADS, HG_DK, HG_DK)
    return pl.pallas_call(
        _gla_kernel,
        out_shape=(
            jax.ShapeDtypeStruct((N_TOK, D_MODEL), F32),
            jax.ShapeDtypeStruct((N_TOK, D_MODEL), F32),
            jax.ShapeDtypeStruct((BATCH,) + st_blk[1:], F32),
        ),
        grid=(N_TILES,),
        in_specs=[f_blk, f_blk, f_blk, f_blk, b_blk, b_blk, b_blk, b_blk, pl.BlockSpec(st_blk, s0_idx)],
        out_specs=(f_blk, b_blk, pl.BlockSpec(st_blk, sout_idx)),
        scratch_shapes=[
            pltpu.VMEM((2, HG_HEADS, HG_DK, HG_DK), F32),
            pltpu.VMEM((2, HG_HEADS, HG_DK, HG_DK), F32),
            pltpu.VMEM((2, HG_HEADS, HG_DK, HG_DK), BF16),
            pltpu.VMEM((HG_HEADS, HG_DK, HG_DK), F32),
            pltpu.VMEM((TM, D_MODEL), BF16),
            pltpu.VMEM((TM, D_MODEL), BF16),
            pltpu.VMEM((TM, D_MODEL), BF16),
            pltpu.VMEM((TM, D_MODEL), F32),
        ],
        compiler_params=_params(48, "arbitrary"),
        name="hgrn_recurrence",
    )(q, v, lff, kf, q, v, lfb, kb, state_hgrn)


def _route(logits):
    lane = lax.broadcasted_iota(jnp.int32, logits.shape, 1)
    big = jnp.int32(ROUTER_LANES)
    neg = jnp.float32(-jnp.inf)
    is_group = (lane >= N_EXPERTS) & (lane < N_EXPERTS + N_GROUPS)
    gl = jnp.where(is_group, logits, neg)
    gmax = jnp.max(gl, axis=-1, keepdims=True)
    gsum = jnp.sum(jnp.exp(gl - gmax), axis=-1, keepdims=True)
    p_g = 1.0 / gsum
    g_sel = jnp.min(jnp.where(is_group & (gl == gmax), lane - N_EXPERTS, big), axis=-1, keepdims=True)
    in_sel = (lane < N_EXPERTS) & ((lane // EXPERTS_PER_GROUP) == g_sel)
    el = jnp.where(in_sel, logits, neg)
    m1 = jnp.max(el, axis=-1, keepdims=True)
    i1 = jnp.min(jnp.where(in_sel & (el == m1), lane, big), axis=-1, keepdims=True)
    el2 = jnp.where(lane == i1, neg, el)
    m2 = jnp.max(el2, axis=-1, keepdims=True)
    i2 = jnp.min(jnp.where(in_sel & (lane != i1) & (el2 == m2), lane, big), axis=-1, keepdims=True)
    e2 = jnp.exp(m2 - m1)
    return i1, i2, p_g / (1.0 + e2), p_g * e2 / (1.0 + e2)


def _pack_rows(x):
    q = PACK_WORDS
    bits = pltpu.bitcast(x.astype(BF16).astype(F32), jnp.uint32)
    return [(bits[:, (2 + h) * q:(3 + h) * q] & jnp.uint32(0xFFFF0000)) | (bits[:, h * q:(h + 1) * q] >> 16)
            for h in range(2)]


def _unpack_rows(half0, half1):
    lo = lambda w: pltpu.bitcast(w << 16, F32).astype(BF16)
    hi = lambda w: pltpu.bitcast(w & jnp.uint32(0xFFFF0000), F32).astype(BF16)
    return [lo(half0), lo(half1), hi(half0), hi(half1)]


def _mixer_tail(mix_bf16, x, mod_ref, nw2_ref, wo_ref, wr_ref, br_ref,
                x1_ref, h2p_ref, ri_ref, rw_ref, cnt_ref, carry_scr):
    i = pl.program_id(0)
    cond = _tile_cond(i)
    out = jnp.dot(mix_bf16, wo_ref[...], preferred_element_type=F32)
    x1 = x + _mod_row(mod_ref, cond, 2) * out
    x1_ref[...] = x1
    h2 = _norm_mod(x1, nw2_ref[...], _mod_row(mod_ref, cond, 3), _mod_row(mod_ref, cond, 4))
    for h, words in enumerate(_pack_rows(h2)):
        h2p_ref[h] = words
    h_hi = h2.astype(BF16)
    h_lo = (h2 - h_hi.astype(F32)).astype(BF16)
    logits = (jnp.dot(h_hi, wr_ref[0], preferred_element_type=F32)
              + (jnp.dot(h_hi, wr_ref[1], preferred_element_type=F32)
                 + jnp.dot(h_lo, wr_ref[0], preferred_element_type=F32))) + br_ref[...]
    i1, i2, w1, w2 = _route(logits)

    @pl.when(i == 0)
    def _():
        carry_scr[...] = jnp.zeros_like(carry_scr)

    lane = lax.broadcasted_iota(jnp.int32, logits.shape, 1)
    chosen = ((lane == i1) | (lane == i2)).astype(BF16)
    r = lax.broadcasted_iota(jnp.int32, (TM, TM), 0)
    c = lax.broadcasted_iota(jnp.int32, (TM, TM), 1)
    before = jnp.dot((c < r).astype(BF16), chosen, preferred_element_type=F32) + carry_scr[...]
    r1 = jnp.sum(jnp.where(lane == i1, before, 0.0), axis=-1, keepdims=True).astype(jnp.int32)
    r2 = jnp.sum(jnp.where(lane == i2, before, 0.0), axis=-1, keepdims=True).astype(jnp.int32)
    total = carry_scr[...] + jnp.sum(chosen.astype(F32), axis=0, keepdims=True)
    carry_scr[...] = total
    cnt_ref[...] = total
    ri_ref[...] = jnp.where(lane == 0, i1, jnp.where(lane == 1, i2, jnp.where(lane == 2, r1, r2)))
    rw_ref[...] = jnp.where(lane == 0, w1, w2)


def _hg_out_kernel(of_ref, ob_ref, g_ref, on_ref, xp_ref, xs_ref, *rest):
    o = of_ref[...] + ob_ref[...]
    parts = []
    for h in range(HG_HEADS):
        oh = o[:, h * HG_DK:(h + 1) * HG_DK]
        parts.append(oh * lax.rsqrt(jnp.mean(oh * oh, axis=-1, keepdims=True) + EPS) * on_ref[...])
    y = jnp.concatenate(parts, axis=1) * _silu(g_ref[...].astype(F32))
    _mixer_tail(y.astype(BF16), _stream_tile(xp_ref, xs_ref), *rest)


def _at_out_kernel(ac_ref, al_ref, x_ref, *rest):
    _mixer_tail(_stream_tile(ac_ref, al_ref), x_ref[...], *rest)


def _mixer_out(kernel_fn, name, mix_inputs, mix_specs, mod, nw2, w_out, w_router, b_router):
    tile = lambda i: (i, 0)
    fixed2 = lambda i: (0, 0)
    blk = pl.BlockSpec((TM, D_MODEL), tile)
    lanes_blk = pl.BlockSpec((TM, ROUTER_LANES), tile)
    return pl.pallas_call(
        kernel_fn,
        out_shape=(
            jax.ShapeDtypeStruct((N_TOK, D_MODEL), F32),
            jax.ShapeDtypeStruct((2, N_TOK, PACK_WORDS), jnp.uint32),
            jax.ShapeDtypeStruct((N_TOK, ROUTER_LANES), jnp.int32),
            jax.ShapeDtypeStruct((N_TOK, ROUTER_LANES), F32),
            jax.ShapeDtypeStruct((1, ROUTER_LANES), F32),
        ),
        grid=(N_TILES,),
        in_specs=list(mix_specs) + [
            pl.BlockSpec(mod.shape, fixed2),
            pl.BlockSpec((1, D_MODEL), fixed2),
            pl.BlockSpec(w_out.shape, fixed2),
            pl.BlockSpec(w_router.shape, lambda i: (0, 0, 0)),
            pl.BlockSpec((1, ROUTER_LANES), fixed2),
        ],
        out_specs=(blk, pl.BlockSpec((2, TM, PACK_WORDS), lambda i: (0, i, 0)), lanes_blk, lanes_blk,
                   pl.BlockSpec((1, ROUTER_LANES), fixed2)),
        scratch_shapes=[pltpu.VMEM((1, ROUTER_LANES), F32)],
        compiler_params=_params(40, "arbitrary"),
        name=name,
    )(*mix_inputs, mod, nw2, w_out, w_router, b_router)


def _moe_plan(route_i, counts):
    cnt = counts[0, :N_EXPERTS].astype(jnp.int32)
    padded = ((cnt + MOE_TILE - 1) // MOE_TILE) * MOE_TILE
    ends = jnp.cumsum(padded)
    offs = ends - padded
    experts = route_i[:, 0:2]
    pos = jnp.sum(jnp.where(experts[:, :, None] == jnp.arange(N_EXPERTS)[None, None, :], offs[None, None, :], 0),
                  axis=-1) + route_i[:, 2:4]
    tile_start = jnp.arange(MOE_ROWS // MOE_TILE, dtype=jnp.int32) * MOE_TILE
    tile_expert = jnp.minimum(jnp.sum(ends[None, :] <= tile_start[:, None], axis=1), N_EXPERTS - 1).astype(jnp.int32)
    tile_rows = jnp.clip(offs[tile_expert] + cnt[tile_expert] - tile_start, 0, MOE_TILE).astype(jnp.int32)
    tile_first = (tile_start == offs[tile_expert]).astype(jnp.int32)
    n_active = (ends[-1] // MOE_TILE).astype(jnp.int32).reshape(1)
    return pos.astype(jnp.int32), tile_expert, tile_rows, tile_first, n_active


SC_WINDOW = 128


def _sc_mesh():
    return plsc.VectorSubcoreMesh(core_axis_name="c", subcore_axis_name="s")


def _sc_scatter_rows(x, idx_a, idx_b, n_out_rows):
    n = x.shape[0]

    @functools.partial(pl.kernel, out_type=jax.ShapeDtypeStruct((n_out_rows, PACK_WORDS), x.dtype), mesh=_sc_mesh())
    def scatter(x_hbm, ia_hbm, ib_hbm, o_hbm):
        def body(x_vmem, ia_vmem, ib_vmem):
            pltpu.sync_copy(x_vmem, o_hbm.at[ia_vmem.at[0]])
            pltpu.sync_copy(x_vmem, o_hbm.at[ib_vmem.at[0]])

        idx_spec = pl.BlockSpec((1, SC_WINDOW), index_map=lambda i: (0, i))
        pltpu.emit_pipeline(
            body, grid=(n // SC_WINDOW,),
            in_specs=[pl.BlockSpec((SC_WINDOW, PACK_WORDS), index_map=lambda i: (i, 0)), idx_spec, idx_spec],
            out_specs=[],
            core_axis_name=("c", "s"), dimension_semantics=(pltpu.PARALLEL,),
        )(x_hbm, ia_hbm, ib_hbm)

    return scatter(x, idx_a.reshape(1, n), idx_b.reshape(1, n))


def _sc_gather_rows(table, idx):
    n = idx.shape[0]

    @functools.partial(pl.kernel, out_type=jax.ShapeDtypeStruct((n, PACK_WORDS), table.dtype), mesh=_sc_mesh())
    def gather(t_hbm, i_hbm, o_hbm):
        def body(i_vmem, o_vmem):
            pltpu.sync_copy(t_hbm.at[i_vmem.at[0]], o_vmem)

        pltpu.emit_pipeline(
            body, grid=(n // SC_WINDOW,),
            in_specs=[pl.BlockSpec((1, SC_WINDOW), index_map=lambda i: (0, i))],
            out_specs=[pl.BlockSpec((SC_WINDOW, PACK_WORDS), index_map=lambda i: (i, 0))],
            core_axis_name=("c", "s"), dimension_semantics=(pltpu.PARALLEL,),
        )(i_hbm, o_hbm)

    return gather(table, idx.reshape(1, n))


def _ffn_kernel(te_ref, tr_ref, tf_ref, na_ref, xs_ref, w1_ref, w3_ref, w2_ref, ys_ref, w1_scr, w3_scr, w2_scr):
    t = pl.program_id(0)

    @pl.when(t < na_ref[0])
    def _():
        @pl.when(tf_ref[t] == 1)
        def _():
            w1_scr[...] = w1_ref[0, 0].astype(BF16)
            w3_scr[...] = w3_ref[0, 0].astype(BF16)
            w2_scr[...] = w2_ref[0, 0].astype(BF16)

        row = lax.broadcasted_iota(jnp.int32, (MOE_TILE, PACK_WORDS), 0)
        live = row < tr_ref[t]
        halves = [jnp.where(live, xs_ref[h], jnp.zeros((MOE_TILE, PACK_WORDS), jnp.uint32)) for h in range(2)]
        chunks = _unpack_rows(*halves)

        def up(w_scr):
            acc = jnp.dot(chunks[0], w_scr[:PACK_WORDS, :], preferred_element_type=F32)
            for k in range(1, len(chunks)):
                acc = acc + jnp.dot(chunks[k], w_scr[k * PACK_WORDS:(k + 1) * PACK_WORDS, :],
                                    preferred_element_type=F32)
            return acc

        hid = (_silu(up(w1_scr)) * up(w3_scr)).astype(BF16)
        for h, words in enumerate(_pack_rows(jnp.dot(hid, w2_scr[...], preferred_element_type=F32))):
            ys_ref[h] = words


def _ffn(xs, tile_expert, tile_rows, tile_first, n_active, layer, w1, w3, w2):
    row_tile = lambda t, te, tr, tf, na: (0, jnp.minimum(t, na[0] - 1), 0)
    expert = lambda t, te, tr, tf, na: (layer, te[t], 0, 0)
    return pl.pallas_call(
        _ffn_kernel,
        out_shape=jax.ShapeDtypeStruct((2, MOE_ROWS, PACK_WORDS), jnp.uint32),
        grid_spec=pltpu.PrefetchScalarGridSpec(
            num_scalar_prefetch=4,
            grid=(MOE_ROWS // MOE_TILE,),
            in_specs=[
                pl.BlockSpec((2, MOE_TILE, PACK_WORDS), row_tile),
                pl.BlockSpec((1, 1, D_MODEL, D_EXPERT), expert),
                pl.BlockSpec((1, 1, D_MODEL, D_EXPERT), expert),
                pl.BlockSpec((1, 1, D_EXPERT, D_MODEL), expert),
            ],
            out_specs=pl.BlockSpec((2, MOE_TILE, PACK_WORDS), row_tile),
            scratch_shapes=[
                pltpu.VMEM((D_MODEL, D_EXPERT), BF16),
                pltpu.VMEM((D_MODEL, D_EXPERT), BF16),
                pltpu.VMEM((D_EXPERT, D_MODEL), BF16),
            ],
        ),
        compiler_params=_params(32, "arbitrary"),
        name="moe_experts",
    )(tile_expert, tile_rows, tile_first, n_active, xs, w1, w3, w2)


def _moe_res(x_ref, y_ref, rw_ref, mod_ref):
    cond = _tile_cond(pl.program_id(0))
    rw = rw_ref[...]
    wa = rw[:, 0:1]
    wb = rw[:, 1:2]
    ya = _unpack_rows(y_ref[0], y_ref[1])
    yb = _unpack_rows(y_ref[2], y_ref[3])
    y = jnp.concatenate([wa * a.astype(F32) + wb * b.astype(F32) for a, b in zip(ya, yb)], axis=1)
    return x_ref[...] + _mod_row(mod_ref, cond, 5) * y


def _moe_res_kernel(x_ref, y_ref, rw_ref, mod_ref, o_ref):
    o_ref[...] = _moe_res(x_ref, y_ref, rw_ref, mod_ref)


def _moe_res_final_kernel(x_ref, y_ref, rw_ref, mod_ref, fn_ref, op_ref, os_ref):
    x = _moe_res(x_ref, y_ref, rw_ref, mod_ref)
    y = x * lax.rsqrt(jnp.mean(x * x, axis=-1, keepdims=True) + EPS) * fn_ref[...]
    is_prompt = pl.program_id(0) < PROMPT_TILES

    @pl.when(is_prompt)
    def _():
        op_ref[...] = y

    @pl.when(jnp.logical_not(is_prompt))
    def _():
        os_ref[...] = y


def _moe_combine(x1, y_pairs, route_w, mod, final_norm=None):
    tile = lambda i: (i, 0)
    fixed2 = lambda i: (0, 0)
    blk = pl.BlockSpec((TM, D_MODEL), tile)
    specs = [blk,
             pl.BlockSpec((4, TM, PACK_WORDS), lambda i: (0, i, 0)),
             pl.BlockSpec((TM, ROUTER_LANES), tile),
             pl.BlockSpec(mod.shape, fixed2)]
    args = [x1, y_pairs, route_w, mod]
    if final_norm is None:
        kernel_fn = _moe_res_kernel
        out_shape = jax.ShapeDtypeStruct((N_TOK, D_MODEL), F32)
        out_specs = blk
    else:
        kernel_fn = _moe_res_final_kernel
        specs.append(pl.BlockSpec((1, D_MODEL), fixed2))
        args.append(final_norm)
        out_shape = (jax.ShapeDtypeStruct((N_PROMPT, D_MODEL), F32),
                     jax.ShapeDtypeStruct((N_TOK - N_PROMPT, D_MODEL), F32))
        out_specs = (pl.BlockSpec((TM, D_MODEL), _prompt_tile), pl.BlockSpec((TM, D_MODEL), _sample_tile))
    return pl.pallas_call(
        kernel_fn,
        out_shape=out_shape,
        grid=(N_TILES,),
        in_specs=specs,
        out_specs=out_specs,
        compiler_params=_params(32, "arbitrary"),
        name="moe_combine",
    )(*args)


def _moe_layer(x1, h2p, route_i, route_w, counts, mod, layer, w1, w3, w2, final_norm=None):
    pos, tile_expert, tile_rows, tile_first, n_active = _moe_plan(route_i, counts)
    idx_a = jnp.concatenate([pos[:, 0], pos[:, 0] + MOE_ROWS])
    idx_b = jnp.concatenate([pos[:, 1], pos[:, 1] + MOE_ROWS])
    xs = _sc_scatter_rows(h2p.reshape(2 * N_TOK, PACK_WORDS), idx_a, idx_b, 2 * MOE_ROWS)
    ys = _ffn(xs.reshape(2, MOE_ROWS, PACK_WORDS), tile_expert, tile_rows, tile_first, n_active,
              layer, w1, w3, w2)
    y_pairs = _sc_gather_rows(ys.reshape(2 * MOE_ROWS, PACK_WORDS), jnp.concatenate([idx_a, idx_b]))
    return _moe_combine(x1, y_pairs.reshape(4, N_TOK, PACK_WORDS), route_w, mod, final_norm)


def _swap_rotary_halves(x):
    n = x.shape[-1]
    lane = lax.broadcasted_iota(jnp.int32, x.shape, 1)
    quarter = ROPE_HALF // 2
    return jnp.where((lane % ROPE_HALF) < quarter, pltpu.roll(x, n - quarter, 1), pltpu.roll(x, quarter, 1))


def _at_in_kernel(x_ref, mod_ref, nw_ref, w_ref, cos_ref, sin_ref, q_ref, k_ref, v_ref, kc_ref, vc_ref):
    i = pl.program_id(0)
    cond = _tile_cond(i)
    h = _norm_mod(x_ref[...], nw_ref[...], _mod_row(mod_ref, cond, 0), _mod_row(mod_ref, cond, 1)).astype(BF16)
    nq = ATT_Q_HEADS * ATT_HEAD_DIM
    nk = ATT_KV_HEADS * ATT_HEAD_DIM
    cos = cos_ref[...]
    sin = sin_ref[...]

    def rope(x):
        reps = x.shape[-1] // LANES
        return x * jnp.concatenate([cos] * reps, axis=1) + _swap_rotary_halves(x) * jnp.concatenate([sin] * reps, axis=1)

    q_ref[...] = rope(jnp.dot(h, w_ref[:, :nq], preferred_element_type=F32)).astype(BF16)
    k = rope(jnp.dot(h, w_ref[:, nq:nq + nk], preferred_element_type=F32))
    v = jnp.dot(h, w_ref[:, nq + nk:], preferred_element_type=F32)
    k_ref[...] = k.astype(BF16)
    v_ref[...] = v.astype(BF16)

    @pl.when(i < PROMPT_TILES)
    def _():
        kc_ref[...] = k
        vc_ref[...] = v


def _rope_tables():
    pos = jnp.arange(DEC_SEQ)
    t_row = (pos // GRID_W).astype(F32)
    t_col = (pos % GRID_W).astype(F32)
    inv = ROPE_BASE ** (-jnp.arange(0, ROPE_HALF, 2, dtype=F32) / ROPE_HALF)
    lane = jnp.arange(LANES)
    j = lane % ATT_HEAD_DIM
    freq = inv[(j % ROPE_HALF) % (ROPE_HALF // 2)]
    t = jnp.where((j < ROPE_HALF)[None, :], t_row[:, None], t_col[:, None])
    ang = t * freq[None, :]
    sign = jnp.where((j % ROPE_HALF) < ROPE_HALF // 2, -1.0, 1.0).astype(F32)
    cos = jnp.concatenate([jnp.ones((TM, LANES), F32), jnp.cos(ang)], axis=0)
    sin = jnp.concatenate([jnp.zeros((TM, LANES), F32), jnp.sin(ang) * sign[None, :]], axis=0)
    return cos, sin


def _at_in(x, mod, nw, w_in, cos, sin):
    tile = lambda i: (i, 0)
    fixed2 = lambda i: (0, 0)
    rope_tile = lambda i: (jnp.where(i < PROMPT_TILES, 0, 1 + (i - PROMPT_TILES) % TILES_PER_DEC_SEQ), 0)
    nk = ATT_KV_HEADS * ATT_HEAD_DIM
    return pl.pallas_call(
        _at_in_kernel,
        out_shape=(
            jax.ShapeDtypeStruct((N_TOK, D_MODEL), BF16),
            jax.ShapeDtypeStruct((N_TOK, nk), BF16),
            jax.ShapeDtypeStruct((N_TOK, nk), BF16),
            jax.ShapeDtypeStruct((N_PROMPT, nk), F32),
            jax.ShapeDtypeStruct((N_PROMPT, nk), F32),
        ),
        grid=(N_TILES,),
        in_specs=[
            pl.BlockSpec((TM, D_MODEL), tile),
            pl.BlockSpec(mod.shape, fixed2),
            pl.BlockSpec((1, D_MODEL), fixed2),
            pl.BlockSpec(w_in.shape, fixed2),
            pl.BlockSpec((TM, LANES), rope_tile),
            pl.BlockSpec((TM, LANES), rope_tile),
        ],
        out_specs=(pl.BlockSpec((TM, D_MODEL), tile), pl.BlockSpec((TM, nk), tile), pl.BlockSpec((TM, nk), tile),
                   pl.BlockSpec((TM, nk), _prompt_tile), pl.BlockSpec((TM, nk), _prompt_tile)),
        compiler_params=_params(40, "arbitrary"),
        name="attn_in_proj",
    )(x, mod, nw, w_in, cos, sin)


def _attend(q, k_all, v_all, mask, sink_ref, o_ref):
    group_lanes = ATT_GROUP * ATT_HEAD_DIM
    scale = ATT_HEAD_DIM ** -0.5
    nt = (((1,), (1,)), ((), ()))
    lane = lax.broadcasted_iota(jnp.int32, (q.shape[0], group_lanes), 1)
    for hk in range(ATT_KV_HEADS):
        kh = k_all[:, hk * ATT_HEAD_DIM:(hk + 1) * ATT_HEAD_DIM]
        vh = v_all[:, hk * ATT_HEAD_DIM:(hk + 1) * ATT_HEAD_DIM]
        kt = jnp.concatenate([kh] * ATT_GROUP, axis=1)
        vt = jnp.concatenate([vh] * ATT_GROUP, axis=1)
        qg = q[:, hk * group_lanes:(hk + 1) * group_lanes]
        acc = jnp.zeros((q.shape[0], group_lanes), F32)
        for g in range(ATT_GROUP):
            mine = (lane // ATT_HEAD_DIM) == g
            qm = jnp.where(mine, qg, jnp.zeros_like(qg))
            s = lax.dot_general(qm, kt, nt, preferred_element_type=F32) * scale
            if mask is not None:
                s = jnp.where(mask, s, -jnp.inf)
            sink = sink_ref[hk * ATT_GROUP + g]
            m = jnp.maximum(jnp.max(s, axis=-1, keepdims=True), sink)
            p = jnp.exp(s - m)
            denom = jnp.sum(p, axis=-1, keepdims=True) + jnp.exp(sink - m)
            o = jnp.dot(p.astype(BF16), vt, preferred_element_type=F32) / denom
            acc = acc + jnp.where(mine, o, 0.0)
        o_ref[:, hk * group_lanes:(hk + 1) * group_lanes] = acc.astype(BF16)


def _ctx_attn_kernel(sink_ref, q_ref, k_ref, v_ref, o_ref):
    _attend(q_ref[...], k_ref[...], v_ref[...], None, sink_ref, o_ref)


def _lat_attn_kernel(sink_ref, q_ref, kp_ref, kc_ref, kn_ref, vp_ref, vc_ref, vn_ref, ck_ref, cv_ref, o_ref):
    jb = pl.program_id(1)
    k_all = jnp.concatenate([kp_ref[...], kc_ref[...], kn_ref[...], ck_ref[0].astype(BF16)], axis=0)
    v_all = jnp.concatenate([vp_ref[...], vc_ref[...], vn_ref[...], cv_ref[0].astype(BF16)], axis=0)
    nkeys = 3 * BLOCK + PAST_LEN
    qi = lax.broadcasted_iota(jnp.int32, (BLOCK, nkeys), 0)
    kj = lax.broadcasted_iota(jnp.int32, (BLOCK, nkeys), 1)
    qpos = jb * BLOCK + qi
    kpos = (jb - 1) * BLOCK + kj
    local_ok = (jnp.abs(qpos - kpos) <= WINDOW) & (kpos >= 0) & (kpos < DEC_SEQ)
    mask = (kj >= 3 * BLOCK) | local_ok
    _attend(q_ref[...], k_all, v_all, mask, sink_ref, o_ref)


def _attention(q, k, v, cache_k, cache_v, sink):
    nk = ATT_KV_HEADS * ATT_HEAD_DIM
    smem = pl.BlockSpec(memory_space=pltpu.SMEM)
    ctx = pl.pallas_call(
        _ctx_attn_kernel,
        out_shape=jax.ShapeDtypeStruct((N_PROMPT, D_MODEL), BF16),
        grid=(BATCH,),
        in_specs=[
            smem,
            pl.BlockSpec((SEQ, D_MODEL), lambda b: (b, 0)),
            pl.BlockSpec((SEQ, nk), lambda b: (b, 0)),
            pl.BlockSpec((SEQ, nk), lambda b: (b, 0)),
        ],
        out_specs=pl.BlockSpec((SEQ, D_MODEL), lambda b: (b, 0)),
        compiler_params=_params(40, "arbitrary"),
        name="context_attention",
    )(sink, q, k, v)

    nb = DEC_SEQ // BLOCK
    base = N_PROMPT // BLOCK
    cur = lambda b, j: (base + b * nb + j, 0)
    prev = lambda b, j: (base + b * nb + jnp.maximum(j - 1, 0), 0)
    nxt = lambda b, j: (base + b * nb + jnp.minimum(j + 1, nb - 1), 0)
    kv_blk = lambda f: pl.BlockSpec((BLOCK, nk), f)
    cache_blk = pl.BlockSpec((1, PAST_LEN, nk), lambda b, j: (b, 0, 0))
    lat = pl.pallas_call(
        _lat_attn_kernel,
        out_shape=jax.ShapeDtypeStruct((DEC_BATCH * DEC_SEQ, D_MODEL), BF16),
        grid=(DEC_BATCH, nb),
        in_specs=[
            smem,
            pl.BlockSpec((BLOCK, D_MODEL), cur),
            kv_blk(prev), kv_blk(cur), kv_blk(nxt),
            kv_blk(prev), kv_blk(cur), kv_blk(nxt),
            cache_blk, cache_blk,
        ],
        out_specs=pl.BlockSpec((BLOCK, D_MODEL), lambda b, j: (b * nb + j, 0)),
        compiler_params=_params(40, "arbitrary", "arbitrary"),
        name="latent_attention",
    )(sink, q, k, k, k, v, v, v, cache_k, cache_v)
    return ctx, lat


def kernel(x_prompt, x_sample, state_hgrn, cache_k, cache_v, c, c_ctx, ada_w, ada_b, norm_w, hg_w_in,
           hg_lb_logits, hg_onorm, hg_w_out, at_w_in, at_sink, at_w_out, moe_w_group, moe_b_group,
           moe_w_expert, moe_b_expert, moe_w1, moe_w3, moe_w2, final_norm):
    xp = x_prompt.reshape(N_PROMPT, D_MODEL)
    xs = x_sample.reshape(N_TOK - N_PROMPT, D_MODEL)
    cond = jnp.concatenate([c_ctx[None, :], c, jnp.zeros((N_COND - 1 - DEC_BATCH, D_MODEL), F32)], axis=0)
    mod = _ada(cond, ada_w, ada_b)
    nk = ATT_KV_HEADS * ATT_HEAD_DIM

    def router_params(i):
        pad = jnp.zeros((D_MODEL, ROUTER_LANES - N_EXPERTS - N_GROUPS), F32)
        w = jnp.concatenate([moe_w_expert[i], moe_w_group[i], pad], axis=1)
        b = jnp.concatenate([moe_b_expert[i], moe_b_group[i], pad[0]])[None, :]
        hi = w.astype(BF16)
        lo = (w - hi.astype(F32)).astype(BF16)
        return jnp.stack([hi, lo]), b

    tile = lambda i: (i, 0)
    blk = pl.BlockSpec((TM, D_MODEL), tile)

    prompt_blk = pl.BlockSpec((TM, D_MODEL), _prompt_tile)
    sample_blk = pl.BlockSpec((TM, D_MODEL), _sample_tile)
    q, v, g, lff, kf, lfb, kb = _hg_in(xp, xs, mod[0], norm_w[0, 0][None, :], hg_w_in[0].astype(BF16), hg_lb_logits)
    o_f, o_b, state_new = _gla(q, v, lff, kf, lfb, kb, state_hgrn)
    wr, br = router_params(0)
    routed = _mixer_out(
        _hg_out_kernel, "hgrn_out_route", (o_f, o_b, g, hg_onorm[0][None, :], xp, xs),
        (blk, blk, blk, pl.BlockSpec((1, HG_DK), lambda i: (0, 0)), prompt_blk, sample_blk),
        mod[0], norm_w[0, 1][None, :], hg_w_out[0].astype(BF16), wr, br)
    x = _moe_layer(*routed, mod[0], 0, moe_w1, moe_w3, moe_w2)

    cos, sin = _rope_tables()
    qa, ka, va, k_ctx, v_ctx = _at_in(x, mod[1], norm_w[1, 0][None, :], at_w_in[0].astype(BF16), cos, sin)
    attn_ctx, attn_lat = _attention(qa, ka, va, cache_k[:, 0].reshape(DEC_BATCH, PAST_LEN, nk),
                                    cache_v[:, 0].reshape(DEC_BATCH, PAST_LEN, nk), at_sink[0])
    wr, br = router_params(1)
    routed = _mixer_out(
        _at_out_kernel, "attn_out_route", (attn_ctx, attn_lat, x), (prompt_blk, sample_blk, blk),
        mod[1], norm_w[1, 1][None, :], at_w_out[0].astype(BF16), wr, br)
    y_prompt, y_sample = _moe_layer(*routed, mod[1], 1, moe_w1, moe_w3, moe_w2, final_norm[None, :])

    cache_shape = (BATCH, 1, SEQ, ATT_KV_HEADS, ATT_HEAD_DIM)
    return (y_prompt.reshape(BATCH, SEQ, D_MODEL), y_sample.reshape(DEC_BATCH, DEC_SEQ, D_MODEL), state_new,
            k_ctx.reshape(cache_shape), v_ctx.reshape(cache_shape))
```

```python
import functools

import jax
import jax.numpy as jnp
from jax import lax
from jax.experimental import pallas as pl
from jax.experimental.pallas import tpu as pltpu
from jax.experimental.pallas import tpu_sc as plsc

F32 = jnp.float32
BF16 = jnp.bfloat16

D_MODEL = 1024
BATCH = 16
SEQ = 256
DEC_BATCH = 2
DEC_SEQ = 1024
PAST_LEN = 512
GRID_W = 64
HG_HEADS = 8
HG_DK = 128
CHUNK = 16
ATT_HEAD_DIM = 64
ATT_Q_HEADS = 16
ATT_KV_HEADS = 4
ATT_GROUP = 4
WINDOW = 128
BLOCK = 128
ROPE_HALF = 32
ROPE_BASE = 10000.0
N_GROUPS = 4
EXPERTS_PER_GROUP = 8
N_EXPERTS = 32
D_EXPERT = 256
EPS = 1e-6

N_PROMPT = BATCH * SEQ
N_TOK = N_PROMPT + DEC_BATCH * DEC_SEQ
TM = 256
N_TILES = N_TOK // TM
PROMPT_TILES = N_PROMPT // TM
TILES_PER_DEC_SEQ = DEC_SEQ // TM
LANES = 128
N_COND = 8
ROUTER_LANES = 128
GLA_UNROLL = 8
DECAY_CLAMP = 60.0
PACK_WORDS = D_MODEL // 4
MOE_TILE = 256
MOE_ROWS = 2 * N_TOK + N_EXPERTS * MOE_TILE
MIB = 1024 * 1024


def _params(vmem_mib, *semantics):
    return pltpu.CompilerParams(dimension_semantics=semantics, vmem_limit_bytes=vmem_mib * MIB)


def _tile_cond(i):
    return jnp.where(i < PROMPT_TILES, 0, 1 + (i - PROMPT_TILES) // TILES_PER_DEC_SEQ)


def _prompt_tile(i):
    return (jnp.minimum(i, PROMPT_TILES - 1), 0)


def _sample_tile(i):
    return (jnp.maximum(i - PROMPT_TILES, 0), 0)


def _stream_tile(prompt_ref, sample_ref):
    return jnp.where(pl.program_id(0) < PROMPT_TILES, prompt_ref[...], sample_ref[...])


def _mod_row(mod_ref, cond, which):
    return mod_ref[pl.ds(cond, 1), which * D_MODEL:(which + 1) * D_MODEL]


def _norm_mod(x, nw, shift, scale):
    y = x * lax.rsqrt(jnp.mean(x * x, axis=-1, keepdims=True) + EPS)
    return (y * nw) * (1.0 + scale) + shift


def _silu(x):
    return x * jax.nn.sigmoid(x)


def _ada_kernel(c_ref, w_ref, b_ref, o_ref):
    s = _silu(c_ref[...])
    o_ref[0] = jnp.dot(s, w_ref[0], precision=lax.Precision.HIGHEST, preferred_element_type=F32) + b_ref[0]


def _ada(cond, ada_w, ada_b):
    depth, _, n = ada_w.shape
    tn = 1536
    return pl.pallas_call(
        _ada_kernel,
        out_shape=jax.ShapeDtypeStruct((depth, N_COND, n), F32),
        grid=(depth, n // tn),
        in_specs=[
            pl.BlockSpec((N_COND, D_MODEL), lambda l, j: (0, 0)),
            pl.BlockSpec((1, D_MODEL, tn), lambda l, j: (l, 0, j)),
            pl.BlockSpec((1, 1, tn), lambda l, j: (l, 0, j)),
        ],
        out_specs=pl.BlockSpec((1, N_COND, tn), lambda l, j: (l, 0, j)),
        compiler_params=_params(40, "arbitrary", "arbitrary"),
        name="ada_modulation",
    )(cond, ada_w, ada_b.reshape(depth, 1, n))


def _hg_in_kernel(xp_ref, xs_ref, mod_ref, nw_ref, w_ref, lbl_ref,
                  q_ref, v_ref, g_ref, lff_ref, kf_ref, lfb_ref, kb_ref):
    cond = _tile_cond(pl.program_id(0))
    x = _stream_tile(xp_ref, xs_ref)
    h = _norm_mod(x, nw_ref[...], _mod_row(mod_ref, cond, 0), _mod_row(mod_ref, cond, 1)).astype(BF16)

    def proj(c):
        return jnp.dot(h, w_ref[:, c * D_MODEL:(c + 1) * D_MODEL], preferred_element_type=F32)

    l0, l1, l2 = lbl_ref[0], lbl_ref[1], lbl_ref[2]
    m = jnp.maximum(jnp.maximum(l0, l1), l2)
    e0, e1, e2 = jnp.exp(l0 - m), jnp.exp(l1 - m), jnp.exp(l2 - m)
    lb = e0 / (e0 + e1 + e2)

    q_ref[...] = proj(0).astype(BF16)
    v_ref[...] = proj(1).astype(BF16)
    for d, (lf_ref, k_ref) in enumerate(((lff_ref, kf_ref), (lfb_ref, kb_ref))):
        lbd = lb[d:d + 1, :]
        f = lbd + (1.0 - lbd) * jax.nn.sigmoid(proj(2 + d))
        lf_ref[...] = jnp.log(f)
        k_ref[...] = (1.0 - f).astype(BF16)
    g_ref[...] = proj(4).astype(BF16)


def _hg_in(xp, xs, mod, nw, w_in, lb_logits):
    tile = lambda i: (i, 0)
    fixed2 = lambda i: (0, 0)
    bf = jax.ShapeDtypeStruct((N_TOK, D_MODEL), BF16)
    ff = jax.ShapeDtypeStruct((N_TOK, D_MODEL), F32)
    blk = pl.BlockSpec((TM, D_MODEL), tile)
    return pl.pallas_call(
        _hg_in_kernel,
        out_shape=(bf, bf, bf, ff, bf, ff, bf),
        grid=(N_TILES,),
        in_specs=[
            pl.BlockSpec((TM, D_MODEL), _prompt_tile),
            pl.BlockSpec((TM, D_MODEL), _sample_tile),
            pl.BlockSpec(mod.shape, fixed2),
            pl.BlockSpec((1, D_MODEL), fixed2),
            pl.BlockSpec(w_in.shape, fixed2),
            pl.BlockSpec(lb_logits.shape, lambda i: (0, 0, 0)),
        ],
        out_specs=(blk,) * 7,
        compiler_params=_params(56, "arbitrary"),
        name="hgrn_in_proj",
    )(xp, xs, mod, nw, w_in, lb_logits)


def _gla_direction(reverse, q_ref, v_ref, lf_ref, k_ref, st_ref, sw_ref, o_ref,
                   ut_scr, qd_scr, kd_scr, ki_scr, cd_scr):
    lf = lf_ref[...]
    r = lax.broadcasted_iota(jnp.int32, (TM, TM), 0)
    c = lax.broadcasted_iota(jnp.int32, (TM, TM), 1)
    same = (r // CHUNK) == (c // CHUNK)
    tri = (same & ((c >= r) if reverse else (c <= r))).astype(BF16)
    hi = lf.astype(BF16)
    lo = (lf - hi.astype(F32)).astype(BF16)
    b = jnp.dot(tri, hi, preferred_element_type=F32) + jnp.dot(tri, lo, preferred_element_type=F32)
    b3 = b.reshape(TM // CHUNK, CHUNK, D_MODEL)
    edge = 0 if reverse else CHUNK - 1
    tot = jnp.broadcast_to(b3[:, edge:edge + 1, :], b3.shape).reshape(TM, D_MODEL)
    kf = k_ref[...].astype(F32)
    qd_scr[...] = (q_ref[...].astype(F32) * jnp.exp(b)).astype(BF16)
    kd_scr[...] = (kf * jnp.exp(tot - b)).astype(BF16)
    ki_scr[...] = (kf * jnp.exp(jnp.minimum(-b, DECAY_CLAMP))).astype(BF16)
    cd_scr[...] = jnp.exp(tot)

    tr = lax.broadcasted_iota(jnp.int32, (CHUNK, CHUNK), 0)
    tc = lax.broadcasted_iota(jnp.int32, (CHUNK, CHUNK), 1)
    keep = (tc >= tr) if reverse else (tc <= tr)
    nt = (((1,), (1,)), ((), ()))
    tn = (((0,), (0,)), ((), ()))

    heads = [slice(h * HG_DK, (h + 1) * HG_DK) for h in range(HG_HEADS)]
    for h in range(HG_HEADS):
        sw_ref[0, h] = st_ref[0, h].T.astype(BF16)

    n_chunks = TM // CHUNK

    def chunk_rows(step):
        ci = (n_chunks - 1 - step) if reverse else step
        row0 = pl.multiple_of(ci * CHUNK, CHUNK)
        return row0, pl.ds(row0, CHUNK)

    def key_value_product(rows):
        for h, cols in enumerate(heads):
            ut_scr[h] = lax.dot_general(v_ref[rows, cols], kd_scr[rows, cols], tn, preferred_element_type=F32)

    key_value_product(chunk_rows(0)[1])

    def chunk(step, src, dst):
        row0, rows = chunk_rows(step)
        cd_row = cd_scr[pl.ds(row0, 1), :]
        for h, cols in enumerate(heads):
            new = st_ref[src, h] * cd_row[:, cols] + ut_scr[h]
            st_ref[dst, h] = new
            sw_ref[dst, h] = new.T.astype(BF16)
        a = [lax.dot_general(qd_scr[rows, cols], ki_scr[rows, cols], nt, preferred_element_type=F32) for cols in heads]
        inter = [jnp.dot(qd_scr[rows, cols], sw_ref[src, h], preferred_element_type=F32)
                 for h, cols in enumerate(heads)]
        key_value_product(chunk_rows(jnp.minimum(step + 1, n_chunks - 1))[1])
        for h, cols in enumerate(heads):
            am = jnp.where(keep, a[h], 0.0).astype(BF16)
            o_ref[rows, cols] = jnp.dot(am, v_ref[rows, cols], preferred_element_type=F32) + inter[h]

    def chunk_group(it, carry):
        for p in range(GLA_UNROLL):
            chunk(GLA_UNROLL * it + p, p % 2, 1 - p % 2)
        return carry

    lax.fori_loop(0, n_chunks // GLA_UNROLL, chunk_group, 0)


def _gla_kernel(qf_ref, vf_ref, lff_ref, kf_ref, qb_ref, vb_ref, lfb_ref, kb_ref, s0_ref,
                of_ref, ob_ref, sout_ref, stf_scr, stb_scr, sw_scr, ut_scr, qd_scr, kd_scr, ki_scr, cd_scr):
    i = pl.program_id(0)
    is_prompt = i < PROMPT_TILES
    first = jnp.logical_or(is_prompt, (i - PROMPT_TILES) % TILES_PER_DEC_SEQ == 0)

    @pl.when(first)
    def _():
        keep0 = jnp.where(is_prompt, 0.0, 1.0)
        for h in range(HG_HEADS):
            stf_scr[0, h] = s0_ref[0, 0, 0, h].T * keep0
            stb_scr[0, h] = s0_ref[0, 0, 1, h].T * keep0

    _gla_direction(False, qf_ref, vf_ref, lff_ref, kf_ref, stf_scr, sw_scr, of_ref,
                   ut_scr, qd_scr, kd_scr, ki_scr, cd_scr)
    _gla_direction(True, qb_ref, vb_ref, lfb_ref, kb_ref, stb_scr, sw_scr, ob_ref,
                   ut_scr, qd_scr, kd_scr, ki_scr, cd_scr)

    @pl.when(is_prompt)
    def _():
        for h in range(HG_HEADS):
            sout_ref[0, 0, 0, h] = stf_scr[0, h].T
            sout_ref[0, 0, 1, h] = stb_scr[0, h].T


def _gla(q, v, lff, kf, lfb, kb, state_hgrn):
    def fwd_tile(i):
        return (i, 0)

    def bwd_tile(i):
        j = (i - PROMPT_TILES) % TILES_PER_DEC_SEQ
        return (jnp.where(i < PROMPT_TILES, i, i - j + (TILES_PER_DEC_SEQ - 1 - j)), 0)

    def s0_idx(i):
        return (jnp.maximum(i - PROMPT_TILES, 0) // TILES_PER_DEC_SEQ, 0, 0, 0, 0, 0)

    def sout_idx(i):
        return (jnp.minimum(i, PROMPT_TILES - 1), 0, 0, 0, 0, 0)

    f_blk = pl.BlockSpec((TM, D_MODEL), fwd_tile)
    b_blk = pl.BlockSpec((TM, D_MODEL), bwd_tile)
    st_blk = (1, 1, 2, HG_HEADS, HG_DK, HG_DK)
    return pl.pallas_call(
        _gla_kernel,
        out_shape=(
            jax.ShapeDtypeStruct((N_TOK, D_MODEL), F32),
            jax.ShapeDtypeStruct((N_TOK, D_MODEL), F32),
            jax.ShapeDtypeStruct((BATCH,) + st_blk[1:], F32),
        ),
        grid=(N_TILES,),
        in_specs=[f_blk, f_blk, f_blk, f_blk, b_blk, b_blk, b_blk, b_blk, pl.BlockSpec(st_blk, s0_idx)],
        out_specs=(f_blk, b_blk, pl.BlockSpec(st_blk, sout_idx)),
        scratch_shapes=[
            pltpu.VMEM((2, HG_HEADS, HG_DK, HG_DK), F32),
            pltpu.VMEM((2, HG_HEADS, HG_DK, HG_DK), F32),
            pltpu.VMEM((2, HG_HEADS, HG_DK, HG_DK), BF16),
            pltpu.VMEM((HG_HEADS, HG_DK, HG_DK), F32),
            pltpu.VMEM((TM, D_MODEL), BF16),
            pltpu.VMEM((TM, D_MODEL), BF16),
            pltpu.VMEM((TM, D_MODEL), BF16),
            pltpu.VMEM((TM, D_MODEL), F32),
        ],
        compiler_params=_params(48, "arbitrary"),
        name="hgrn_recurrence",
    )(q, v, lff, kf, q, v, lfb, kb, state_hgrn)


def _route(logits):
    lane = lax.broadcasted_iota(jnp.int32, logits.shape, 1)
    big = jnp.int32(ROUTER_LANES)
    neg = jnp.float32(-jnp.inf)
    is_group = (lane >= N_EXPERTS) & (lane < N_EXPERTS + N_GROUPS)
    gl = jnp.where(is_group, logits, neg)
    gmax = jnp.max(gl, axis=-1, keepdims=True)
    gsum = jnp.sum(jnp.exp(gl - gmax), axis=-1, keepdims=True)
    p_g = 1.0 / gsum
    g_sel = jnp.min(jnp.where(is_group & (gl == gmax), lane - N_EXPERTS, big), axis=-1, keepdims=True)
    in_sel = (lane < N_EXPERTS) & ((lane // EXPERTS_PER_GROUP) == g_sel)
    el = jnp.where(in_sel, logits, neg)
    m1 = jnp.max(el, axis=-1, keepdims=True)
    i1 = jnp.min(jnp.where(in_sel & (el == m1), lane, big), axis=-1, keepdims=True)
    el2 = jnp.where(lane == i1, neg, el)
    m2 = jnp.max(el2, axis=-1, keepdims=True)
    i2 = jnp.min(jnp.where(in_sel & (lane != i1) & (el2 == m2), lane, big), axis=-1, keepdims=True)
    e2 = jnp.exp(m2 - m1)
    return i1, i2, p_g / (1.0 + e2), p_g * e2 / (1.0 + e2)


def _pack_rows(x):
    q = PACK_WORDS
    bits = pltpu.bitcast(x.astype(BF16).astype(F32), jnp.uint32)
    return [(bits[:, (2 + h) * q:(3 + h) * q] & jnp.uint32(0xFFFF0000)) | (bits[:, h * q:(h + 1) * q] >> 16)
            for h in range(2)]


def _unpack_rows(half0, half1):
    lo = lambda w: pltpu.bitcast(w << 16, F32).astype(BF16)
    hi = lambda w: pltpu.bitcast(w & jnp.uint32(0xFFFF0000), F32).astype(BF16)
    return [lo(half0), lo(half1), hi(half0), hi(half1)]


def _mixer_tail(mix_bf16, x, mod_ref, nw2_ref, wo_ref, wr_ref, br_ref,
                x1_ref, h2p_ref, ri_ref, rw_ref, cnt_ref, carry_scr):
    i = pl.program_id(0)
    cond = _tile_cond(i)
    out = jnp.dot(mix_bf16, wo_ref[...], preferred_element_type=F32)
    x1 = x + _mod_row(mod_ref, cond, 2) * out
    x1_ref[...] = x1
    h2 = _norm_mod(x1, nw2_ref[...], _mod_row(mod_ref, cond, 3), _mod_row(mod_ref, cond, 4))
    for h, words in enumerate(_pack_rows(h2)):
        h2p_ref[h] = words
    h_hi = h2.astype(BF16)
    h_lo = (h2 - h_hi.astype(F32)).astype(BF16)
    logits = (jnp.dot(h_hi, wr_ref[0], preferred_element_type=F32)
              + (jnp.dot(h_hi, wr_ref[1], preferred_element_type=F32)
                 + jnp.dot(h_lo, wr_ref[0], preferred_element_type=F32))) + br_ref[...]
    i1, i2, w1, w2 = _route(logits)

    @pl.when(i == 0)
    def _():
        carry_scr[...] = jnp.zeros_like(carry_scr)

    lane = lax.broadcasted_iota(jnp.int32, logits.shape, 1)
    chosen = ((lane == i1) | (lane == i2)).astype(BF16)
    r = lax.broadcasted_iota(jnp.int32, (TM, TM), 0)
    c = lax.broadcasted_iota(jnp.int32, (TM, TM), 1)
    before = jnp.dot((c < r).astype(BF16), chosen, preferred_element_type=F32) + carry_scr[...]
    r1 = jnp.sum(jnp.where(lane == i1, before, 0.0), axis=-1, keepdims=True).astype(jnp.int32)
    r2 = jnp.sum(jnp.where(lane == i2, before, 0.0), axis=-1, keepdims=True).astype(jnp.int32)
    total = carry_scr[...] + jnp.sum(chosen.astype(F32), axis=0, keepdims=True)
    carry_scr[...] = total
    cnt_ref[...] = total
    ri_ref[...] = jnp.where(lane == 0, i1, jnp.where(lane == 1, i2, jnp.where(lane == 2, r1, r2)))
    rw_ref[...] = jnp.where(lane == 0, w1, w2)


def _hg_out_kernel(of_ref, ob_ref, g_ref, on_ref, xp_ref, xs_ref, *rest):
    o = of_ref[...] + ob_ref[...]
    parts = []
    for h in range(HG_HEADS):
        oh = o[:, h * HG_DK:(h + 1) * HG_DK]
        parts.append(oh * lax.rsqrt(jnp.mean(oh * oh, axis=-1, keepdims=True) + EPS) * on_ref[...])
    y = jnp.concatenate(parts, axis=1) * _silu(g_ref[...].astype(F32))
    _mixer_tail(y.astype(BF16), _stream_tile(xp_ref, xs_ref), *rest)


def _at_out_kernel(ac_ref, al_ref, x_ref, *rest):
    _mixer_tail(_stream_tile(ac_ref, al_ref), x_ref[...], *rest)


def _mixer_out(kernel_fn, name, mix_inputs, mix_specs, mod, nw2, w_out, w_router, b_router):
    tile = lambda i: (i, 0)
    fixed2 = lambda i: (0, 0)
    blk = pl.BlockSpec((TM, D_MODEL), tile)
    lanes_blk = pl.BlockSpec((TM, ROUTER_LANES), tile)
    return pl.pallas_call(
        kernel_fn,
        out_shape=(
            jax.ShapeDtypeStruct((N_TOK, D_MODEL), F32),
            jax.ShapeDtypeStruct((2, N_TOK, PACK_WORDS), jnp.uint32),
            jax.ShapeDtypeStruct((N_TOK, ROUTER_LANES), jnp.int32),
            jax.ShapeDtypeStruct((N_TOK, ROUTER_LANES), F32),
            jax.ShapeDtypeStruct((1, ROUTER_LANES), F32),
        ),
        grid=(N_TILES,),
        in_specs=list(mix_specs) + [
            pl.BlockSpec(mod.shape, fixed2),
            pl.BlockSpec((1, D_MODEL), fixed2),
            pl.BlockSpec(w_out.shape, fixed2),
            pl.BlockSpec(w_router.shape, lambda i: (0, 0, 0)),
            pl.BlockSpec((1, ROUTER_LANES), fixed2),
        ],
        out_specs=(blk, pl.BlockSpec((2, TM, PACK_WORDS), lambda i: (0, i, 0)), lanes_blk, lanes_blk,
                   pl.BlockSpec((1, ROUTER_LANES), fixed2)),
        scratch_shapes=[pltpu.VMEM((1, ROUTER_LANES), F32)],
        compiler_params=_params(40, "arbitrary"),
        name=name,
    )(*mix_inputs, mod, nw2, w_out, w_router, b_router)


def _moe_plan(route_i, counts):
    cnt = counts[0, :N_EXPERTS].astype(jnp.int32)
    padded = ((cnt + MOE_TILE - 1) // MOE_TILE) * MOE_TILE
    ends = jnp.cumsum(padded)
    offs = ends - padded
    experts = route_i[:, 0:2]
    pos = jnp.sum(jnp.where(experts[:, :, None] == jnp.arange(N_EXPERTS)[None, None, :], offs[None, None, :], 0),
                  axis=-1) + route_i[:, 2:4]
    tile_start = jnp.arange(MOE_ROWS // MOE_TILE, dtype=jnp.int32) * MOE_TILE
    tile_expert = jnp.minimum(jnp.sum(ends[None, :] <= tile_start[:, None], axis=1), N_EXPERTS - 1).astype(jnp.int32)
    of_tile = tile_expert[:, None] == jnp.arange(N_EXPERTS)[None, :]
    tile_offs = jnp.sum(jnp.where(of_tile, offs[None, :], 0), axis=1)
    tile_cnt = jnp.sum(jnp.where(of_tile, cnt[None, :], 0), axis=1)
    tile_rows = jnp.clip(tile_offs + tile_cnt - tile_start, 0, MOE_TILE).astype(jnp.int32)
    tile_first = (tile_start == tile_offs).astype(jnp.int32)
    n_active = (ends[-1] // MOE_TILE).astype(jnp.int32).reshape(1)
    return pos.astype(jnp.int32), tile_expert, tile_rows, tile_first, n_active


SC_WINDOW = 128


def _sc_mesh():
    return plsc.VectorSubcoreMesh(core_axis_name="c", subcore_axis_name="s")


def _sc_scatter_rows(x, idx_a, idx_b, n_out_rows):
    n = x.shape[0]

    @functools.partial(pl.kernel, out_type=jax.ShapeDtypeStruct((n_out_rows, PACK_WORDS), x.dtype), mesh=_sc_mesh())
    def scatter(x_hbm, ia_hbm, ib_hbm, o_hbm):
        def body(x_vmem, ia_vmem, ib_vmem):
            pltpu.sync_copy(x_vmem, o_hbm.at[ia_vmem.at[0]])
            pltpu.sync_copy(x_vmem, o_hbm.at[ib_vmem.at[0]])

        idx_spec = pl.BlockSpec((1, SC_WINDOW), index_map=lambda i: (0, i))
        pltpu.emit_pipeline(
            body, grid=(n // SC_WINDOW,),
            in_specs=[pl.BlockSpec((SC_WINDOW, PACK_WORDS), index_map=lambda i: (i, 0)), idx_spec, idx_spec],
            out_specs=[],
            core_axis_name=("c", "s"), dimension_semantics=(pltpu.PARALLEL,),
        )(x_hbm, ia_hbm, ib_hbm)

    return scatter(x, idx_a.reshape(1, n), idx_b.reshape(1, n))


def _sc_gather_rows(table, idx):
    n = idx.shape[0]

    @functools.partial(pl.kernel, out_type=jax.ShapeDtypeStruct((n, PACK_WORDS), table.dtype), mesh=_sc_mesh())
    def gather(t_hbm, i_hbm, o_hbm):
        def body(i_vmem, o_vmem):
            pltpu.sync_copy(t_hbm.at[i_vmem.at[0]], o_vmem)

        pltpu.emit_pipeline(
            body, grid=(n // SC_WINDOW,),
            in_specs=[pl.BlockSpec((1, SC_WINDOW), index_map=lambda i: (0, i))],
            out_specs=[pl.BlockSpec((SC_WINDOW, PACK_WORDS), index_map=lambda i: (i, 0))],
            core_axis_name=("c", "s"), dimension_semantics=(pltpu.PARALLEL,),
        )(i_hbm, o_hbm)

    return gather(table, idx.reshape(1, n))


def _ffn_kernel(te_ref, tr_ref, tf_ref, na_ref, xs_ref, w1_ref, w3_ref, w2_ref, ys_ref, w1_scr, w3_scr, w2_scr):
    t = pl.program_id(0)

    @pl.when(t < na_ref[0])
    def _():
        @pl.when(tf_ref[t] == 1)
        def _():
            w1_scr[...] = w1_ref[0, 0].astype(BF16)
            w3_scr[...] = w3_ref[0, 0].astype(BF16)
            w2_scr[...] = w2_ref[0, 0].astype(BF16)

        row = lax.broadcasted_iota(jnp.int32, (MOE_TILE, PACK_WORDS), 0)
        live = row < tr_ref[t]
        halves = [jnp.where(live, xs_ref[h], jnp.zeros((MOE_TILE, PACK_WORDS), jnp.uint32)) for h in range(2)]
        chunks = _unpack_rows(*halves)

        def up(w_scr):
            acc = jnp.dot(chunks[0], w_scr[:PACK_WORDS, :], preferred_element_type=F32)
            for k in range(1, len(chunks)):
                acc = acc + jnp.dot(chunks[k], w_scr[k * PACK_WORDS:(k + 1) * PACK_WORDS, :],
                                    preferred_element_type=F32)
            return acc

        hid = (_silu(up(w1_scr)) * up(w3_scr)).astype(BF16)
        for h, words in enumerate(_pack_rows(jnp.dot(hid, w2_scr[...], preferred_element_type=F32))):
            ys_ref[h] = words


def _ffn(xs, tile_expert, tile_rows, tile_first, n_active, layer, w1, w3, w2):
    row_tile = lambda t, te, tr, tf, na: (0, jnp.minimum(t, na[0] - 1), 0)
    expert = lambda t, te, tr, tf, na: (layer, te[t], 0, 0)
    return pl.pallas_call(
        _ffn_kernel,
        out_shape=jax.ShapeDtypeStruct((2, MOE_ROWS, PACK_WORDS), jnp.uint32),
        grid_spec=pltpu.PrefetchScalarGridSpec(
            num_scalar_prefetch=4,
            grid=(MOE_ROWS // MOE_TILE,),
            in_specs=[
                pl.BlockSpec((2, MOE_TILE, PACK_WORDS), row_tile),
                pl.BlockSpec((1, 1, D_MODEL, D_EXPERT), expert),
                pl.BlockSpec((1, 1, D_MODEL, D_EXPERT), expert),
                pl.BlockSpec((1, 1, D_EXPERT, D_MODEL), expert),
            ],
            out_specs=pl.BlockSpec((2, MOE_TILE, PACK_WORDS), row_tile),
            scratch_shapes=[
                pltpu.VMEM((D_MODEL, D_EXPERT), BF16),
                pltpu.VMEM((D_MODEL, D_EXPERT), BF16),
                pltpu.VMEM((D_EXPERT, D_MODEL), BF16),
            ],
        ),
        compiler_params=_params(32, "arbitrary"),
        name="moe_experts",
    )(tile_expert, tile_rows, tile_first, n_active, xs, w1, w3, w2)


def _moe_res(x_ref, y_ref, rw_ref, mod_ref):
    cond = _tile_cond(pl.program_id(0))
    rw = rw_ref[...]
    wa = rw[:, 0:1]
    wb = rw[:, 1:2]
    ya = _unpack_rows(y_ref[0], y_ref[1])
    yb = _unpack_rows(y_ref[2], y_ref[3])
    y = jnp.concatenate([wa * a.astype(F32) + wb * b.astype(F32) for a, b in zip(ya, yb)], axis=1)
    return x_ref[...] + _mod_row(mod_ref, cond, 5) * y


def _moe_res_kernel(x_ref, y_ref, rw_ref, mod_ref, o_ref):
    o_ref[...] = _moe_res(x_ref, y_ref, rw_ref, mod_ref)


def _moe_res_final_kernel(x_ref, y_ref, rw_ref, mod_ref, fn_ref, op_ref, os_ref):
    x = _moe_res(x_ref, y_ref, rw_ref, mod_ref)
    y = x * lax.rsqrt(jnp.mean(x * x, axis=-1, keepdims=True) + EPS) * fn_ref[...]
    is_prompt = pl.program_id(0) < PROMPT_TILES

    @pl.when(is_prompt)
    def _():
        op_ref[...] = y

    @pl.when(jnp.logical_not(is_prompt))
    def _():
        os_ref[...] = y


def _moe_combine(x1, y_pairs, route_w, mod, final_norm=None):
    tile = lambda i: (i, 0)
    fixed2 = lambda i: (0, 0)
    blk = pl.BlockSpec((TM, D_MODEL), tile)
    specs = [blk,
             pl.BlockSpec((4, TM, PACK_WORDS), lambda i: (0, i, 0)),
             pl.BlockSpec((TM, ROUTER_LANES), tile),
             pl.BlockSpec(mod.shape, fixed2)]
    args = [x1, y_pairs, route_w, mod]
    if final_norm is None:
        kernel_fn = _moe_res_kernel
        out_shape = jax.ShapeDtypeStruct((N_TOK, D_MODEL), F32)
        out_specs = blk
    else:
        kernel_fn = _moe_res_final_kernel
        specs.append(pl.BlockSpec((1, D_MODEL), fixed2))
        args.append(final_norm)
        out_shape = (jax.ShapeDtypeStruct((N_PROMPT, D_MODEL), F32),
                     jax.ShapeDtypeStruct((N_TOK - N_PROMPT, D_MODEL), F32))
        out_specs = (pl.BlockSpec((TM, D_MODEL), _prompt_tile), pl.BlockSpec((TM, D_MODEL), _sample_tile))
    return pl.pallas_call(
        kernel_fn,
        out_shape=out_shape,
        grid=(N_TILES,),
        in_specs=specs,
        out_specs=out_specs,
        compiler_params=_params(32, "arbitrary"),
        name="moe_combine",
    )(*args)


def _moe_layer(x1, h2p, route_i, route_w, counts, mod, layer, w1, w3, w2, final_norm=None):
    pos, tile_expert, tile_rows, tile_first, n_active = _moe_plan(route_i, counts)
    idx_a = jnp.concatenate([pos[:, 0], pos[:, 0] + MOE_ROWS])
    idx_b = jnp.concatenate([pos[:, 1], pos[:, 1] + MOE_ROWS])
    xs = _sc_scatter_rows(h2p.reshape(2 * N_TOK, PACK_WORDS), idx_a, idx_b, 2 * MOE_ROWS)
    ys = _ffn(xs.reshape(2, MOE_ROWS, PACK_WORDS), tile_expert, tile_rows, tile_first, n_active,
              layer, w1, w3, w2)
    y_pairs = _sc_gather_rows(ys.reshape(2 * MOE_ROWS, PACK_WORDS), jnp.concatenate([idx_a, idx_b]))
    return _moe_combine(x1, y_pairs.reshape(4, N_TOK, PACK_WORDS), route_w, mod, final_norm)


def _swap_rotary_halves(x):
    n = x.shape[-1]
    lane = lax.broadcasted_iota(jnp.int32, x.shape, 1)
    quarter = ROPE_HALF // 2
    return jnp.where((lane % ROPE_HALF) < quarter, pltpu.roll(x, n - quarter, 1), pltpu.roll(x, quarter, 1))


def _at_in_kernel(x_ref, mod_ref, nw_ref, w_ref, cos_ref, sin_ref, q_ref, k_ref, v_ref, kc_ref, vc_ref):
    i = pl.program_id(0)
    cond = _tile_cond(i)
    h = _norm_mod(x_ref[...], nw_ref[...], _mod_row(mod_ref, cond, 0), _mod_row(mod_ref, cond, 1)).astype(BF16)
    nq = ATT_Q_HEADS * ATT_HEAD_DIM
    nk = ATT_KV_HEADS * ATT_HEAD_DIM
    cos = cos_ref[...]
    sin = sin_ref[...]

    def rope(x):
        reps = x.shape[-1] // LANES
        return x * jnp.concatenate([cos] * reps, axis=1) + _swap_rotary_halves(x) * jnp.concatenate([sin] * reps, axis=1)

    q_ref[...] = rope(jnp.dot(h, w_ref[:, :nq], preferred_element_type=F32)).astype(BF16)
    k = rope(jnp.dot(h, w_ref[:, nq:nq + nk], preferred_element_type=F32))
    v = jnp.dot(h, w_ref[:, nq + nk:], preferred_element_type=F32)
    k_ref[...] = k.astype(BF16)
    v_ref[...] = v.astype(BF16)

    @pl.when(i < PROMPT_TILES)
    def _():
        kc_ref[...] = k.T
        vc_ref[...] = v.T


def _rope_tables():
    pos = jnp.arange(DEC_SEQ)
    t_row = (pos // GRID_W).astype(F32)
    t_col = (pos % GRID_W).astype(F32)
    inv = ROPE_BASE ** (-jnp.arange(0, ROPE_HALF, 2, dtype=F32) / ROPE_HALF)
    lane = jnp.arange(LANES)
    j = lane % ATT_HEAD_DIM
    freq = inv[(j % ROPE_HALF) % (ROPE_HALF // 2)]
    t = jnp.where((j < ROPE_HALF)[None, :], t_row[:, None], t_col[:, None])
    ang = t * freq[None, :]
    sign = jnp.where((j % ROPE_HALF) < ROPE_HALF // 2, -1.0, 1.0).astype(F32)
    cos = jnp.concatenate([jnp.ones((TM, LANES), F32), jnp.cos(ang)], axis=0)
    sin = jnp.concatenate([jnp.zeros((TM, LANES), F32), jnp.sin(ang) * sign[None, :]], axis=0)
    return cos, sin


def _at_in(x, mod, nw, w_in, cos, sin):
    tile = lambda i: (i, 0)
    fixed2 = lambda i: (0, 0)
    rope_tile = lambda i: (jnp.where(i < PROMPT_TILES, 0, 1 + (i - PROMPT_TILES) % TILES_PER_DEC_SEQ), 0)
    nk = ATT_KV_HEADS * ATT_HEAD_DIM
    return pl.pallas_call(
        _at_in_kernel,
        out_shape=(
            jax.ShapeDtypeStruct((N_TOK, D_MODEL), BF16),
            jax.ShapeDtypeStruct((N_TOK, nk), BF16),
            jax.ShapeDtypeStruct((N_TOK, nk), BF16),
            jax.ShapeDtypeStruct((N_PROMPT, nk), F32),
            jax.ShapeDtypeStruct((N_PROMPT, nk), F32),
        ),
        grid=(N_TILES,),
        in_specs=[
            pl.BlockSpec((TM, D_MODEL), tile),
            pl.BlockSpec(mod.shape, fixed2),
            pl.BlockSpec((1, D_MODEL), fixed2),
            pl.BlockSpec(w_in.shape, fixed2),
            pl.BlockSpec((TM, LANES), rope_tile),
            pl.BlockSpec((TM, LANES), rope_tile),
        ],
        out_specs=(pl.BlockSpec((TM, D_MODEL), tile), pl.BlockSpec((TM, nk), tile), pl.BlockSpec((TM, nk), tile),
                   pl.BlockSpec((TM, nk), _prompt_tile), pl.BlockSpec((TM, nk), _prompt_tile)),
        compiler_params=_params(40, "arbitrary"),
        name="attn_in_proj",
    )(x, mod, nw, w_in, cos, sin)


def _attend(q, k_all, v_all, mask, sink_ref, o_ref):
    nq = q.shape[0]
    group_lanes = ATT_GROUP * ATT_HEAD_DIM
    scale = ATT_HEAD_DIM ** -0.5
    nt = (((1,), (1,)), ((), ()))
    lane = lax.broadcasted_iota(jnp.int32, (nq, group_lanes), 1)
    mine = [(lane // ATT_HEAD_DIM) == g for g in range(ATT_GROUP)]
    row_head = lax.broadcasted_iota(jnp.int32, (ATT_GROUP * nq, 1), 0) // nq
    if mask is not None:
        mask = jnp.concatenate([mask] * ATT_GROUP, axis=0)
    for hk in range(ATT_KV_HEADS):
        kh = k_all[:, hk * ATT_HEAD_DIM:(hk + 1) * ATT_HEAD_DIM]
        vh = v_all[:, hk * ATT_HEAD_DIM:(hk + 1) * ATT_HEAD_DIM]
        kt = jnp.concatenate([kh] * ATT_GROUP, axis=1)
        vt = jnp.concatenate([vh] * ATT_GROUP, axis=1)
        qg = q[:, hk * group_lanes:(hk + 1) * group_lanes]
        q_stack = jnp.concatenate([jnp.where(mine[g], qg, jnp.zeros_like(qg)) for g in range(ATT_GROUP)], axis=0)
        s = lax.dot_general(q_stack, kt, nt, preferred_element_type=F32) * scale
        if mask is not None:
            s = jnp.where(mask, s, -jnp.inf)
        sink = jnp.zeros((ATT_GROUP * nq, 1), F32)
        for g in range(ATT_GROUP):
            sink = jnp.where(row_head == g, sink_ref[hk * ATT_GROUP + g], sink)
        m = jnp.maximum(jnp.max(s, axis=-1, keepdims=True), sink)
        p = jnp.exp(s - m)
        denom = jnp.sum(p, axis=-1, keepdims=True) + jnp.exp(sink - m)
        o = jnp.dot(p.astype(BF16), vt, preferred_element_type=F32) / denom
        acc = jnp.where(mine[0], o[:nq], 0.0)
        for g in range(1, ATT_GROUP):
            acc = acc + jnp.where(mine[g], o[g * nq:(g + 1) * nq], 0.0)
        o_ref[:, hk * group_lanes:(hk + 1) * group_lanes] = acc.astype(BF16)


def _ctx_attn_kernel(sink_ref, q_ref, k_ref, v_ref, o_ref):
    _attend(q_ref[...], k_ref[...], v_ref[...], None, sink_ref, o_ref)


def _lat_attn_kernel(sink_ref, q_ref, kp_ref, kc_ref, kn_ref, vp_ref, vc_ref, vn_ref, ck_ref, cv_ref, o_ref):
    jb = pl.program_id(1)
    k_all = jnp.concatenate([kp_ref[...], kc_ref[...], kn_ref[...], ck_ref[0].astype(BF16)], axis=0)
    v_all = jnp.concatenate([vp_ref[...], vc_ref[...], vn_ref[...], cv_ref[0].astype(BF16)], axis=0)
    nkeys = 3 * BLOCK + PAST_LEN
    qi = lax.broadcasted_iota(jnp.int32, (BLOCK, nkeys), 0)
    kj = lax.broadcasted_iota(jnp.int32, (BLOCK, nkeys), 1)
    qpos = jb * BLOCK + qi
    kpos = (jb - 1) * BLOCK + kj
    local_ok = (jnp.abs(qpos - kpos) <= WINDOW) & (kpos >= 0) & (kpos < DEC_SEQ)
    mask = (kj >= 3 * BLOCK) | local_ok
    _attend(q_ref[...], k_all, v_all, mask, sink_ref, o_ref)


def _attention(q, k, v, cache_k, cache_v, sink):
    nk = ATT_KV_HEADS * ATT_HEAD_DIM
    smem = pl.BlockSpec(memory_space=pltpu.SMEM)
    ctx = pl.pallas_call(
        _ctx_attn_kernel,
        out_shape=jax.ShapeDtypeStruct((N_PROMPT, D_MODEL), BF16),
        grid=(BATCH,),
        in_specs=[
            smem,
            pl.BlockSpec((SEQ, D_MODEL), lambda b: (b, 0)),
            pl.BlockSpec((SEQ, nk), lambda b: (b, 0)),
            pl.BlockSpec((SEQ, nk), lambda b: (b, 0)),
        ],
        out_specs=pl.BlockSpec((SEQ, D_MODEL), lambda b: (b, 0)),
        compiler_params=_params(40, "arbitrary"),
        name="context_attention",
    )(sink, q, k, v)

    nb = DEC_SEQ // BLOCK
    base = N_PROMPT // BLOCK
    cur = lambda b, j: (base + b * nb + j, 0)
    prev = lambda b, j: (base + b * nb + jnp.maximum(j - 1, 0), 0)
    nxt = lambda b, j: (base + b * nb + jnp.minimum(j + 1, nb - 1), 0)
    kv_blk = lambda f: pl.BlockSpec((BLOCK, nk), f)
    cache_blk = pl.BlockSpec((1, PAST_LEN, nk), lambda b, j: (b, 0, 0))
    lat = pl.pallas_call(
        _lat_attn_kernel,
        out_shape=jax.ShapeDtypeStruct((DEC_BATCH * DEC_SEQ, D_MODEL), BF16),
        grid=(DEC_BATCH, nb),
        in_specs=[
            smem,
            pl.BlockSpec((BLOCK, D_MODEL), cur),
            kv_blk(prev), kv_blk(cur), kv_blk(nxt),
            kv_blk(prev), kv_blk(cur), kv_blk(nxt),
            cache_blk, cache_blk,
        ],
        out_specs=pl.BlockSpec((BLOCK, D_MODEL), lambda b, j: (b * nb + j, 0)),
        compiler_params=_params(40, "arbitrary", "arbitrary"),
        name="latent_attention",
    )(sink, q, k, k, k, v, v, v, cache_k, cache_v)
    return ctx, lat


def kernel(x_prompt, x_sample, state_hgrn, cache_k, cache_v, c, c_ctx, ada_w, ada_b, norm_w, hg_w_in,
           hg_lb_logits, hg_onorm, hg_w_out, at_w_in, at_sink, at_w_out, moe_w_group, moe_b_group,
           moe_w_expert, moe_b_expert, moe_w1, moe_w3, moe_w2, final_norm):
    xp = x_prompt.reshape(N_PROMPT, D_MODEL)
    xs = x_sample.reshape(N_TOK - N_PROMPT, D_MODEL)
    cond = jnp.concatenate([c_ctx[None, :], c, jnp.zeros((N_COND - 1 - DEC_BATCH, D_MODEL), F32)], axis=0)
    mod = _ada(cond, ada_w, ada_b)
    nk = ATT_KV_HEADS * ATT_HEAD_DIM

    def router_params(i):
        pad = jnp.zeros((D_MODEL, ROUTER_LANES - N_EXPERTS - N_GROUPS), F32)
        w = jnp.concatenate([moe_w_expert[i], moe_w_group[i], pad], axis=1)
        b = jnp.concatenate([moe_b_expert[i], moe_b_group[i], pad[0]])[None, :]
        hi = w.astype(BF16)
        lo = (w - hi.astype(F32)).astype(BF16)
        return jnp.stack([hi, lo]), b

    tile = lambda i: (i, 0)
    blk = pl.BlockSpec((TM, D_MODEL), tile)

    prompt_blk = pl.BlockSpec((TM, D_MODEL), _prompt_tile)
    sample_blk = pl.BlockSpec((TM, D_MODEL), _sample_tile)
    q, v, g, lff, kf, lfb, kb = _hg_in(xp, xs, mod[0], norm_w[0, 0][None, :], hg_w_in[0].astype(BF16), hg_lb_logits)
    o_f, o_b, state_new = _gla(q, v, lff, kf, lfb, kb, state_hgrn)
    wr, br = router_params(0)
    routed = _mixer_out(
        _hg_out_kernel, "hgrn_out_route", (o_f, o_b, g, hg_onorm[0][None, :], xp, xs),
        (blk, blk, blk, pl.BlockSpec((1, HG_DK), lambda i: (0, 0)), prompt_blk, sample_blk),
        mod[0], norm_w[0, 1][None, :], hg_w_out[0].astype(BF16), wr, br)
    x = _moe_layer(*routed, mod[0], 0, moe_w1, moe_w3, moe_w2)

    cos, sin = _rope_tables()
    qa, ka, va, k_ctx, v_ctx = _at_in(x, mod[1], norm_w[1, 0][None, :], at_w_in[0].astype(BF16), cos, sin)
    attn_ctx, attn_lat = _attention(qa, ka, va, cache_k[:, 0].reshape(DEC_BATCH, PAST_LEN, nk),
                                    cache_v[:, 0].reshape(DEC_BATCH, PAST_LEN, nk), at_sink[0])
    wr, br = router_params(1)
    routed = _mixer_out(
        _at_out_kernel, "attn_out_route", (attn_ctx, attn_lat, x), (prompt_blk, sample_blk, blk),
        mod[1], norm_w[1, 1][None, :], at_w_out[0].astype(BF16), wr, br)
    y_prompt, y_sample = _moe_layer(*routed, mod[1], 1, moe_w1, moe_w3, moe_w2, final_norm[None, :])

    def cache(feature_major):
        return feature_major.reshape(BATCH, 1, ATT_KV_HEADS, ATT_HEAD_DIM, SEQ).transpose(0, 1, 4, 2, 3)

    return (y_prompt.reshape(BATCH, SEQ, D_MODEL), y_sample.reshape(DEC_BATCH, DEC_SEQ, D_MODEL), state_new,
            cache(k_ctx), cache(v_ctx))
```

```python
import functools

import jax
import jax.numpy as jnp
import numpy as np
from jax import lax
from jax.experimental import pallas as pl
from jax.experimental.pallas import tpu as pltpu
from jax.experimental.pallas import tpu_sc as plsc

F32 = jnp.float32
BF16 = jnp.bfloat16

D_MODEL = 1024
BATCH = 16
SEQ = 256
DEC_BATCH = 2
DEC_SEQ = 1024
PAST_LEN = 512
GRID_W = 64
HG_HEADS = 8
HG_DK = 128
CHUNK = 16
ATT_HEAD_DIM = 64
ATT_Q_HEADS = 16
ATT_KV_HEADS = 4
ATT_GROUP = 4
WINDOW = 128
BLOCK = 128
ROPE_HALF = 32
ROPE_BASE = 10000.0
N_GROUPS = 4
EXPERTS_PER_GROUP = 8
N_EXPERTS = 32
D_EXPERT = 256
EPS = 1e-6

N_PROMPT = BATCH * SEQ
N_TOK = N_PROMPT + DEC_BATCH * DEC_SEQ
TM = 256
N_TILES = N_TOK // TM
PROMPT_TILES = N_PROMPT // TM
TILES_PER_DEC_SEQ = DEC_SEQ // TM
LANES = 128
N_COND = 8
ROUTER_LANES = 128
GLA_UNROLL = 8
DECAY_CLAMP = 60.0
PACK_WORDS = D_MODEL // 4
MOE_TILE = 256
MOE_ROWS = 2 * N_TOK + N_EXPERTS * MOE_TILE
MIB = 1024 * 1024


def _params(vmem_mib, *semantics):
    return pltpu.CompilerParams(dimension_semantics=semantics, vmem_limit_bytes=vmem_mib * MIB)


def _tile_cond(i):
    return jnp.where(i < PROMPT_TILES, 0, 1 + (i - PROMPT_TILES) // TILES_PER_DEC_SEQ)


def _prompt_tile(i):
    return (jnp.minimum(i, PROMPT_TILES - 1), 0)


def _sample_tile(i):
    return (jnp.maximum(i - PROMPT_TILES, 0), 0)


def _stream_tile(prompt_ref, sample_ref):
    return jnp.where(pl.program_id(0) < PROMPT_TILES, prompt_ref[...], sample_ref[...])


def _mod_row(mod_ref, cond, which):
    return mod_ref[pl.ds(cond, 1), which * D_MODEL:(which + 1) * D_MODEL]


def _norm_mod(x, nw, shift, scale):
    y = x * lax.rsqrt(jnp.mean(x * x, axis=-1, keepdims=True) + EPS)
    return (y * nw) * (1.0 + scale) + shift


def _silu(x):
    return x * jax.nn.sigmoid(x)


def _ada_kernel(c_ref, w_ref, b_ref, o_ref):
    s = _silu(c_ref[...])
    o_ref[0] = jnp.dot(s, w_ref[0], precision=lax.Precision.HIGHEST, preferred_element_type=F32) + b_ref[0]


def _ada(cond, ada_w, ada_b):
    depth, _, n = ada_w.shape
    tn = 1536
    return pl.pallas_call(
        _ada_kernel,
        out_shape=jax.ShapeDtypeStruct((depth, N_COND, n), F32),
        grid=(depth, n // tn),
        in_specs=[
            pl.BlockSpec((N_COND, D_MODEL), lambda l, j: (0, 0)),
            pl.BlockSpec((1, D_MODEL, tn), lambda l, j: (l, 0, j)),
            pl.BlockSpec((1, 1, tn), lambda l, j: (l, 0, j)),
        ],
        out_specs=pl.BlockSpec((1, N_COND, tn), lambda l, j: (l, 0, j)),
        compiler_params=_params(40, "arbitrary", "arbitrary"),
        name="ada_modulation",
    )(cond, ada_w, ada_b.reshape(depth, 1, n))


def _hg_in_kernel(xp_ref, xs_ref, mod_ref, nw_ref, w_ref, lbl_ref,
                  q_ref, v_ref, g_ref, lff_ref, kf_ref, lfb_ref, kb_ref):
    cond = _tile_cond(pl.program_id(0))
    x = _stream_tile(xp_ref, xs_ref)
    h = _norm_mod(x, nw_ref[...], _mod_row(mod_ref, cond, 0), _mod_row(mod_ref, cond, 1)).astype(BF16)

    def proj(c):
        return jnp.dot(h, w_ref[:, c * D_MODEL:(c + 1) * D_MODEL], preferred_element_type=F32)

    l0, l1, l2 = lbl_ref[0], lbl_ref[1], lbl_ref[2]
    m = jnp.maximum(jnp.maximum(l0, l1), l2)
    e0, e1, e2 = jnp.exp(l0 - m), jnp.exp(l1 - m), jnp.exp(l2 - m)
    lb = e0 / (e0 + e1 + e2)

    q_ref[...] = proj(0).astype(BF16)
    v_ref[...] = proj(1).astype(BF16)
    for d, (lf_ref, k_ref) in enumerate(((lff_ref, kf_ref), (lfb_ref, kb_ref))):
        lbd = lb[d:d + 1, :]
        f = lbd + (1.0 - lbd) * jax.nn.sigmoid(proj(2 + d))
        lf = jnp.log(f)
        hi = lf.astype(BF16)
        lf_ref[0] = hi
        lf_ref[1] = (lf - hi.astype(F32)).astype(BF16)
        k_ref[...] = (1.0 - f).astype(BF16)
    g_ref[...] = proj(4).astype(BF16)


def _hg_in(xp, xs, mod, nw, w_in, lb_logits):
    tile = lambda i: (i, 0)
    fixed2 = lambda i: (0, 0)
    bf = jax.ShapeDtypeStruct((N_TOK, D_MODEL), BF16)
    ff = jax.ShapeDtypeStruct((2, N_TOK, D_MODEL), BF16)
    blk = pl.BlockSpec((TM, D_MODEL), tile)
    split_blk = pl.BlockSpec((2, TM, D_MODEL), lambda i: (0, i, 0))
    return pl.pallas_call(
        _hg_in_kernel,
        out_shape=(bf, bf, bf, ff, bf, ff, bf),
        grid=(N_TILES,),
        in_specs=[
            pl.BlockSpec((TM, D_MODEL), _prompt_tile),
            pl.BlockSpec((TM, D_MODEL), _sample_tile),
            pl.BlockSpec(mod.shape, fixed2),
            pl.BlockSpec((1, D_MODEL), fixed2),
            pl.BlockSpec(w_in.shape, fixed2),
            pl.BlockSpec(lb_logits.shape, lambda i: (0, 0, 0)),
        ],
        out_specs=(blk, blk, blk, split_blk, blk, split_blk, blk),
        compiler_params=_params(56, "arbitrary"),
        name="hgrn_in_proj",
    )(xp, xs, mod, nw, w_in, lb_logits)


def _gla_direction(reverse, q_ref, v_ref, lf_ref, k_ref, st_ref, sw_ref, o_ref,
                   ut_scr, qd_scr, kd_scr, ki_scr, cd_scr):
    r = lax.broadcasted_iota(jnp.int32, (TM, TM), 0)
    c = lax.broadcasted_iota(jnp.int32, (TM, TM), 1)
    same = (r // CHUNK) == (c // CHUNK)
    tri = (same & ((c >= r) if reverse else (c <= r))).astype(BF16)
    b = jnp.dot(tri, lf_ref[0], preferred_element_type=F32) + jnp.dot(tri, lf_ref[1], preferred_element_type=F32)
    b3 = b.reshape(TM // CHUNK, CHUNK, D_MODEL)
    edge = 0 if reverse else CHUNK - 1
    tot = jnp.broadcast_to(b3[:, edge:edge + 1, :], b3.shape).reshape(TM, D_MODEL)
    k = k_ref[...]
    qd_scr[...] = q_ref[...] * jnp.exp(b).astype(BF16)
    kd_scr[...] = k * jnp.exp(tot - b).astype(BF16)
    ki_scr[...] = k * jnp.exp(jnp.minimum(-b, DECAY_CLAMP)).astype(BF16)
    cd_scr[...] = jnp.exp(tot)

    tr = lax.broadcasted_iota(jnp.int32, (CHUNK, CHUNK), 0)
    tc = lax.broadcasted_iota(jnp.int32, (CHUNK, CHUNK), 1)
    keep = (tc >= tr) if reverse else (tc <= tr)
    nt = (((1,), (1,)), ((), ()))
    tn = (((0,), (0,)), ((), ()))

    heads = [slice(h * HG_DK, (h + 1) * HG_DK) for h in range(HG_HEADS)]
    for h in range(HG_HEADS):
        sw_ref[0, h] = st_ref[0, h].T.astype(BF16)

    n_chunks = TM // CHUNK

    def chunk_rows(step):
        ci = (n_chunks - 1 - step) if reverse else step
        row0 = pl.multiple_of(ci * CHUNK, CHUNK)
        return row0, pl.ds(row0, CHUNK)

    def key_value_product(rows):
        for h, cols in enumerate(heads):
            ut_scr[h] = lax.dot_general(v_ref[rows, cols], kd_scr[rows, cols], tn, preferred_element_type=F32)

    key_value_product(chunk_rows(0)[1])

    def chunk(step, src, dst):
        row0, rows = chunk_rows(step)
        cd_row = cd_scr[pl.ds(row0, 1), :]
        for h, cols in enumerate(heads):
            new = st_ref[src, h] * cd_row[:, cols] + ut_scr[h]
            st_ref[dst, h] = new
            sw_ref[dst, h] = new.T.astype(BF16)
        a = [lax.dot_general(qd_scr[rows, cols], ki_scr[rows, cols], nt, preferred_element_type=F32) for cols in heads]
        inter = [jnp.dot(qd_scr[rows, cols], sw_ref[src, h], preferred_element_type=F32)
                 for h, cols in enumerate(heads)]
        key_value_product(chunk_rows(jnp.minimum(step + 1, n_chunks - 1))[1])
        for h, cols in enumerate(heads):
            am = jnp.where(keep, a[h], 0.0).astype(BF16)
            o_ref[rows, cols] = jnp.dot(am, v_ref[rows, cols], preferred_element_type=F32) + inter[h]

    def chunk_group(it, carry):
        for p in range(GLA_UNROLL):
            chunk(GLA_UNROLL * it + p, p % 2, 1 - p % 2)
        return carry

    lax.fori_loop(0, n_chunks // GLA_UNROLL, chunk_group, 0)


def _gla_kernel(qf_ref, vf_ref, lff_ref, kf_ref, qb_ref, vb_ref, lfb_ref, kb_ref, s0_ref,
                of_ref, ob_ref, sout_ref, stf_scr, stb_scr, sw_scr, ut_scr, qd_scr, kd_scr, ki_scr, cd_scr):
    i = pl.program_id(0)
    is_prompt = i < PROMPT_TILES
    first = jnp.logical_or(is_prompt, (i - PROMPT_TILES) % TILES_PER_DEC_SEQ == 0)

    @pl.when(first)
    def _():
        keep0 = jnp.where(is_prompt, 0.0, 1.0)
        for h in range(HG_HEADS):
            stf_scr[0, h] = s0_ref[0, 0, 0, h].T * keep0
            stb_scr[0, h] = s0_ref[0, 0, 1, h].T * keep0

    _gla_direction(False, qf_ref, vf_ref, lff_ref, kf_ref, stf_scr, sw_scr, of_ref,
                   ut_scr, qd_scr, kd_scr, ki_scr, cd_scr)
    _gla_direction(True, qb_ref, vb_ref, lfb_ref, kb_ref, stb_scr, sw_scr, ob_ref,
                   ut_scr, qd_scr, kd_scr, ki_scr, cd_scr)

    @pl.when(is_prompt)
    def _():
        for h in range(HG_HEADS):
            sout_ref[0, 0, 0, h] = stf_scr[0, h].T
            sout_ref[0, 0, 1, h] = stb_scr[0, h].T


def _gla(q, v, lff, kf, lfb, kb, state_hgrn):
    def fwd_tile(i):
        return (i, 0)

    def bwd_tile(i):
        j = (i - PROMPT_TILES) % TILES_PER_DEC_SEQ
        return (jnp.where(i < PROMPT_TILES, i, i - j + (TILES_PER_DEC_SEQ - 1 - j)), 0)

    def s0_idx(i):
        return (jnp.maximum(i - PROMPT_TILES, 0) // TILES_PER_DEC_SEQ, 0, 0, 0, 0, 0)

    def sout_idx(i):
        return (jnp.minimum(i, PROMPT_TILES - 1), 0, 0, 0, 0, 0)

    f_blk = pl.BlockSpec((TM, D_MODEL), fwd_tile)
    b_blk = pl.BlockSpec((TM, D_MODEL), bwd_tile)
    f_split = pl.BlockSpec((2, TM, D_MODEL), lambda i: (0,) + fwd_tile(i))
    b_split = pl.BlockSpec((2, TM, D_MODEL), lambda i: (0,) + bwd_tile(i))
    st_blk = (1, 1, 2, HG_HEADS, HG_DK, HG_DK)
    return pl.pallas_call(
        _gla_kernel,
        out_shape=(
            jax.ShapeDtypeStruct((N_TOK, D_MODEL), F32),
            jax.ShapeDtypeStruct((N_TOK, D_MODEL), F32),
            jax.ShapeDtypeStruct((BATCH,) + st_blk[1:], F32),
        ),
        grid=(N_TILES,),
        in_specs=[f_blk, f_blk, f_split, f_blk, b_blk, b_blk, b_split, b_blk, pl.BlockSpec(st_blk, s0_idx)],
        out_specs=(f_blk, b_blk, pl.BlockSpec(st_blk, sout_idx)),
        scratch_shapes=[
            pltpu.VMEM((2, HG_HEADS, HG_DK, HG_DK), F32),
            pltpu.VMEM((2, HG_HEADS, HG_DK, HG_DK), F32),
            pltpu.VMEM((2, HG_HEADS, HG_DK, HG_DK), BF16),
            pltpu.VMEM((HG_HEADS, HG_DK, HG_DK), F32),
            pltpu.VMEM((TM, D_MODEL), BF16),
            pltpu.VMEM((TM, D_MODEL), BF16),
            pltpu.VMEM((TM, D_MODEL), BF16),
            pltpu.VMEM((TM, D_MODEL), F32),
        ],
        compiler_params=_params(48, "arbitrary"),
        name="hgrn_recurrence",
    )(q, v, lff, kf, q, v, lfb, kb, state_hgrn)


def _route(logits):
    lane = lax.broadcasted_iota(jnp.int32, logits.shape, 1)
    big = jnp.int32(ROUTER_LANES)
    neg = jnp.float32(-jnp.inf)
    is_group = (lane >= N_EXPERTS) & (lane < N_EXPERTS + N_GROUPS)
    gl = jnp.where(is_group, logits, neg)
    gmax = jnp.max(gl, axis=-1, keepdims=True)
    gsum = jnp.sum(jnp.exp(gl - gmax), axis=-1, keepdims=True)
    p_g = 1.0 / gsum
    g_sel = jnp.min(jnp.where(is_group & (gl == gmax), lane - N_EXPERTS, big), axis=-1, keepdims=True)
    in_sel = (lane < N_EXPERTS) & ((lane // EXPERTS_PER_GROUP) == g_sel)
    el = jnp.where(in_sel, logits, neg)
    m1 = jnp.max(el, axis=-1, keepdims=True)
    i1 = jnp.min(jnp.where(in_sel & (el == m1), lane, big), axis=-1, keepdims=True)
    el2 = jnp.where(lane == i1, neg, el)
    m2 = jnp.max(el2, axis=-1, keepdims=True)
    i2 = jnp.min(jnp.where(in_sel & (lane != i1) & (el2 == m2), lane, big), axis=-1, keepdims=True)
    e2 = jnp.exp(m2 - m1)
    return i1, i2, p_g / (1.0 + e2), p_g * e2 / (1.0 + e2)


def _pack_rows(x):
    q = PACK_WORDS
    bits = pltpu.bitcast(x.astype(BF16).astype(F32), jnp.uint32)
    return [(bits[:, (2 + h) * q:(3 + h) * q] & jnp.uint32(0xFFFF0000)) | (bits[:, h * q:(h + 1) * q] >> 16)
            for h in range(2)]


def _unpack_rows(half0, half1):
    lo = lambda w: pltpu.bitcast(w << 16, F32).astype(BF16)
    hi = lambda w: pltpu.bitcast(w & jnp.uint32(0xFFFF0000), F32).astype(BF16)
    return [lo(half0), lo(half1), hi(half0), hi(half1)]


def _mixer_tail(mix_bf16, x, mod_ref, nw2_ref, wo_ref, wr_ref, br_ref,
                x1_ref, h2p_ref, ri_ref, rw_ref, cnt_ref, carry_scr):
    i = pl.program_id(0)
    cond = _tile_cond(i)
    out = jnp.dot(mix_bf16, wo_ref[...], preferred_element_type=F32)
    x1 = x + _mod_row(mod_ref, cond, 2) * out
    x1_ref[...] = x1
    h2 = _norm_mod(x1, nw2_ref[...], _mod_row(mod_ref, cond, 3), _mod_row(mod_ref, cond, 4))
    for h, words in enumerate(_pack_rows(h2)):
        h2p_ref[h] = words
    h_hi = h2.astype(BF16)
    h_lo = (h2 - h_hi.astype(F32)).astype(BF16)
    logits = (jnp.dot(h_hi, wr_ref[0], preferred_element_type=F32)
              + (jnp.dot(h_hi, wr_ref[1], preferred_element_type=F32)
                 + jnp.dot(h_lo, wr_ref[0], preferred_element_type=F32))) + br_ref[...]
    i1, i2, w1, w2 = _route(logits)

    @pl.when(i == 0)
    def _():
        carry_scr[...] = jnp.zeros_like(carry_scr)

    lane = lax.broadcasted_iota(jnp.int32, logits.shape, 1)
    chosen = ((lane == i1) | (lane == i2)).astype(BF16)
    r = lax.broadcasted_iota(jnp.int32, (TM, TM), 0)
    c = lax.broadcasted_iota(jnp.int32, (TM, TM), 1)
    before = jnp.dot((c < r).astype(BF16), chosen, preferred_element_type=F32) + carry_scr[...]
    r1 = jnp.sum(jnp.where(lane == i1, before, 0.0), axis=-1, keepdims=True).astype(jnp.int32)
    r2 = jnp.sum(jnp.where(lane == i2, before, 0.0), axis=-1, keepdims=True).astype(jnp.int32)
    total = carry_scr[...] + jnp.sum(chosen.astype(F32), axis=0, keepdims=True)
    carry_scr[...] = total
    cnt_ref[...] = total
    ri_ref[...] = jnp.where(lane == 0, i1, jnp.where(lane == 1, i2, jnp.where(lane == 2, r1, r2)))
    rw_ref[...] = jnp.where(lane == 0, w1, w2)


def _hg_out_kernel(of_ref, ob_ref, g_ref, on_ref, xp_ref, xs_ref, *rest):
    o = of_ref[...] + ob_ref[...]
    parts = []
    for h in range(HG_HEADS):
        oh = o[:, h * HG_DK:(h + 1) * HG_DK]
        parts.append(oh * lax.rsqrt(jnp.mean(oh * oh, axis=-1, keepdims=True) + EPS) * on_ref[...])
    y = jnp.concatenate(parts, axis=1) * _silu(g_ref[...].astype(F32))
    _mixer_tail(y.astype(BF16), _stream_tile(xp_ref, xs_ref), *rest)


def _at_out_kernel(ac_ref, al_ref, x_ref, *rest):
    _mixer_tail(_stream_tile(ac_ref, al_ref), x_ref[...], *rest)


def _mixer_out(kernel_fn, name, mix_inputs, mix_specs, mod, nw2, w_out, w_router, b_router):
    tile = lambda i: (i, 0)
    fixed2 = lambda i: (0, 0)
    blk = pl.BlockSpec((TM, D_MODEL), tile)
    lanes_blk = pl.BlockSpec((TM, ROUTER_LANES), tile)
    return pl.pallas_call(
        kernel_fn,
        out_shape=(
            jax.ShapeDtypeStruct((N_TOK, D_MODEL), F32),
            jax.ShapeDtypeStruct((2, N_TOK, PACK_WORDS), jnp.uint32),
            jax.ShapeDtypeStruct((N_TOK, ROUTER_LANES), jnp.int32),
            jax.ShapeDtypeStruct((N_TOK, ROUTER_LANES), F32),
            jax.ShapeDtypeStruct((1, ROUTER_LANES), F32),
        ),
        grid=(N_TILES,),
        in_specs=list(mix_specs) + [
            pl.BlockSpec(mod.shape, fixed2),
            pl.BlockSpec((1, D_MODEL), fixed2),
            pl.BlockSpec(w_out.shape, fixed2),
            pl.BlockSpec(w_router.shape, lambda i: (0, 0, 0)),
            pl.BlockSpec((1, ROUTER_LANES), fixed2),
        ],
        out_specs=(blk, pl.BlockSpec((2, TM, PACK_WORDS), lambda i: (0, i, 0)), lanes_blk, lanes_blk,
                   pl.BlockSpec((1, ROUTER_LANES), fixed2)),
        scratch_shapes=[pltpu.VMEM((1, ROUTER_LANES), F32)],
        compiler_params=_params(40, "arbitrary"),
        name=name,
    )(*mix_inputs, mod, nw2, w_out, w_router, b_router)


def _moe_plan(route_i, counts):
    cnt = counts[0, :N_EXPERTS].astype(jnp.int32)
    padded = ((cnt + MOE_TILE - 1) // MOE_TILE) * MOE_TILE
    ends = jnp.cumsum(padded)
    offs = ends - padded
    experts = route_i[:, 0:2]
    pos = jnp.sum(jnp.where(experts[:, :, None] == jnp.arange(N_EXPERTS)[None, None, :], offs[None, None, :], 0),
                  axis=-1) + route_i[:, 2:4]
    tile_start = jnp.arange(MOE_ROWS // MOE_TILE, dtype=jnp.int32) * MOE_TILE
    tile_expert = jnp.minimum(jnp.sum(ends[None, :] <= tile_start[:, None], axis=1), N_EXPERTS - 1).astype(jnp.int32)
    of_tile = tile_expert[:, None] == jnp.arange(N_EXPERTS)[None, :]
    tile_offs = jnp.sum(jnp.where(of_tile, offs[None, :], 0), axis=1)
    tile_cnt = jnp.sum(jnp.where(of_tile, cnt[None, :], 0), axis=1)
    tile_rows = jnp.clip(tile_offs + tile_cnt - tile_start, 0, MOE_TILE).astype(jnp.int32)
    tile_first = (tile_start == tile_offs).astype(jnp.int32)
    n_active = (ends[-1] // MOE_TILE).astype(jnp.int32).reshape(1)
    return pos.astype(jnp.int32), tile_expert, tile_rows, tile_first, n_active


SC_WINDOW = 128


def _sc_mesh():
    return plsc.VectorSubcoreMesh(core_axis_name="c", subcore_axis_name="s")


def _sc_scatter_rows(x, idx_a, idx_b, n_out_rows):
    n = x.shape[0]

    @functools.partial(pl.kernel, out_type=jax.ShapeDtypeStruct((n_out_rows, PACK_WORDS), x.dtype), mesh=_sc_mesh())
    def scatter(x_hbm, ia_hbm, ib_hbm, o_hbm):
        def body(x_vmem, ia_vmem, ib_vmem):
            pltpu.sync_copy(x_vmem, o_hbm.at[ia_vmem.at[0]])
            pltpu.sync_copy(x_vmem, o_hbm.at[ib_vmem.at[0]])

        idx_spec = pl.BlockSpec((1, SC_WINDOW), index_map=lambda i: (0, i))
        pltpu.emit_pipeline(
            body, grid=(n // SC_WINDOW,),
            in_specs=[pl.BlockSpec((SC_WINDOW, PACK_WORDS), index_map=lambda i: (i, 0)), idx_spec, idx_spec],
            out_specs=[],
            core_axis_name=("c", "s"), dimension_semantics=(pltpu.PARALLEL,),
        )(x_hbm, ia_hbm, ib_hbm)

    return scatter(x, idx_a.reshape(1, n), idx_b.reshape(1, n))


def _sc_gather_rows(table, idx):
    n = idx.shape[0]

    @functools.partial(pl.kernel, out_type=jax.ShapeDtypeStruct((n, PACK_WORDS), table.dtype), mesh=_sc_mesh())
    def gather(t_hbm, i_hbm, o_hbm):
        def body(i_vmem, o_vmem):
            pltpu.sync_copy(t_hbm.at[i_vmem.at[0]], o_vmem)

        pltpu.emit_pipeline(
            body, grid=(n // SC_WINDOW,),
            in_specs=[pl.BlockSpec((1, SC_WINDOW), index_map=lambda i: (0, i))],
            out_specs=[pl.BlockSpec((SC_WINDOW, PACK_WORDS), index_map=lambda i: (i, 0))],
            core_axis_name=("c", "s"), dimension_semantics=(pltpu.PARALLEL,),
        )(i_hbm, o_hbm)

    return gather(table, idx.reshape(1, n))


def _ffn_kernel(te_ref, tr_ref, tf_ref, na_ref, xs_ref, w1_ref, w3_ref, w2_ref, ys_ref, w1_scr, w3_scr, w2_scr):
    t = pl.program_id(0)

    @pl.when(t < na_ref[0])
    def _():
        @pl.when(tf_ref[t] == 1)
        def _():
            w1_scr[...] = w1_ref[0, 0].astype(BF16)
            w3_scr[...] = w3_ref[0, 0].astype(BF16)
            w2_scr[...] = w2_ref[0, 0].astype(BF16)

        row = lax.broadcasted_iota(jnp.int32, (MOE_TILE, PACK_WORDS), 0)
        live = row < tr_ref[t]
        halves = [jnp.where(live, xs_ref[h], jnp.zeros((MOE_TILE, PACK_WORDS), jnp.uint32)) for h in range(2)]
        chunks = _unpack_rows(*halves)

        def up(w_scr):
            acc = jnp.dot(chunks[0], w_scr[:PACK_WORDS, :], preferred_element_type=F32)
            for k in range(1, len(chunks)):
                acc = acc + jnp.dot(chunks[k], w_scr[k * PACK_WORDS:(k + 1) * PACK_WORDS, :],
                                    preferred_element_type=F32)
            return acc

        hid = (_silu(up(w1_scr)) * up(w3_scr)).astype(BF16)
        for h, words in enumerate(_pack_rows(jnp.dot(hid, w2_scr[...], preferred_element_type=F32))):
            ys_ref[h] = words


def _ffn(xs, tile_expert, tile_rows, tile_first, n_active, layer, w1, w3, w2):
    row_tile = lambda t, te, tr, tf, na: (0, jnp.minimum(t, na[0] - 1), 0)
    expert = lambda t, te, tr, tf, na: (layer, te[t], 0, 0)
    return pl.pallas_call(
        _ffn_kernel,
        out_shape=jax.ShapeDtypeStruct((2, MOE_ROWS, PACK_WORDS), jnp.uint32),
        grid_spec=pltpu.PrefetchScalarGridSpec(
            num_scalar_prefetch=4,
            grid=(MOE_ROWS // MOE_TILE,),
            in_specs=[
                pl.BlockSpec((2, MOE_TILE, PACK_WORDS), row_tile),
                pl.BlockSpec((1, 1, D_MODEL, D_EXPERT), expert),
                pl.BlockSpec((1, 1, D_MODEL, D_EXPERT), expert),
                pl.BlockSpec((1, 1, D_EXPERT, D_MODEL), expert),
            ],
            out_specs=pl.BlockSpec((2, MOE_TILE, PACK_WORDS), row_tile),
            scratch_shapes=[
                pltpu.VMEM((D_MODEL, D_EXPERT), BF16),
                pltpu.VMEM((D_MODEL, D_EXPERT), BF16),
                pltpu.VMEM((D_EXPERT, D_MODEL), BF16),
            ],
        ),
        compiler_params=_params(32, "arbitrary"),
        name="moe_experts",
    )(tile_expert, tile_rows, tile_first, n_active, xs, w1, w3, w2)


def _moe_res(x_ref, y_ref, rw_ref, mod_ref):
    cond = _tile_cond(pl.program_id(0))
    rw = rw_ref[...]
    wa = rw[:, 0:1]
    wb = rw[:, 1:2]
    ya = _unpack_rows(y_ref[0], y_ref[1])
    yb = _unpack_rows(y_ref[2], y_ref[3])
    y = jnp.concatenate([wa * a.astype(F32) + wb * b.astype(F32) for a, b in zip(ya, yb)], axis=1)
    return x_ref[...] + _mod_row(mod_ref, cond, 5) * y


def _moe_res_kernel(x_ref, y_ref, rw_ref, mod_ref, o_ref):
    o_ref[...] = _moe_res(x_ref, y_ref, rw_ref, mod_ref)


def _moe_res_final_kernel(x_ref, y_ref, rw_ref, mod_ref, fn_ref, op_ref, os_ref):
    x = _moe_res(x_ref, y_ref, rw_ref, mod_ref)
    y = x * lax.rsqrt(jnp.mean(x * x, axis=-1, keepdims=True) + EPS) * fn_ref[...]
    is_prompt = pl.program_id(0) < PROMPT_TILES

    @pl.when(is_prompt)
    def _():
        op_ref[...] = y

    @pl.when(jnp.logical_not(is_prompt))
    def _():
        os_ref[...] = y


def _moe_combine(x1, y_pairs, route_w, mod, final_norm=None):
    tile = lambda i: (i, 0)
    fixed2 = lambda i: (0, 0)
    blk = pl.BlockSpec((TM, D_MODEL), tile)
    specs = [blk,
             pl.BlockSpec((4, TM, PACK_WORDS), lambda i: (0, i, 0)),
             pl.BlockSpec((TM, ROUTER_LANES), tile),
             pl.BlockSpec(mod.shape, fixed2)]
    args = [x1, y_pairs, route_w, mod]
    if final_norm is None:
        kernel_fn = _moe_res_kernel
        out_shape = jax.ShapeDtypeStruct((N_TOK, D_MODEL), F32)
        out_specs = blk
    else:
        kernel_fn = _moe_res_final_kernel
        specs.append(pl.BlockSpec((1, D_MODEL), fixed2))
        args.append(final_norm)
        out_shape = (jax.ShapeDtypeStruct((N_PROMPT, D_MODEL), F32),
                     jax.ShapeDtypeStruct((N_TOK - N_PROMPT, D_MODEL), F32))
        out_specs = (pl.BlockSpec((TM, D_MODEL), _prompt_tile), pl.BlockSpec((TM, D_MODEL), _sample_tile))
    return pl.pallas_call(
        kernel_fn,
        out_shape=out_shape,
        grid=(N_TILES,),
        in_specs=specs,
        out_specs=out_specs,
        compiler_params=_params(32, "arbitrary"),
        name="moe_combine",
    )(*args)


def _moe_layer(x1, h2p, route_i, route_w, counts, mod, layer, w1, w3, w2, final_norm=None):
    pos, tile_expert, tile_rows, tile_first, n_active = _moe_plan(route_i, counts)
    idx_a = jnp.concatenate([pos[:, 0], pos[:, 0] + MOE_ROWS])
    idx_b = jnp.concatenate([pos[:, 1], pos[:, 1] + MOE_ROWS])
    xs = _sc_scatter_rows(h2p.reshape(2 * N_TOK, PACK_WORDS), idx_a, idx_b, 2 * MOE_ROWS)
    ys = _ffn(xs.reshape(2, MOE_ROWS, PACK_WORDS), tile_expert, tile_rows, tile_first, n_active,
              layer, w1, w3, w2)
    y_pairs = _sc_gather_rows(ys.reshape(2 * MOE_ROWS, PACK_WORDS), jnp.concatenate([idx_a, idx_b]))
    return _moe_combine(x1, y_pairs.reshape(4, N_TOK, PACK_WORDS), route_w, mod, final_norm)


def _swap_rotary_halves(x):
    n = x.shape[-1]
    lane = lax.broadcasted_iota(jnp.int32, x.shape, 1)
    quarter = ROPE_HALF // 2
    return jnp.where((lane % ROPE_HALF) < quarter, pltpu.roll(x, n - quarter, 1), pltpu.roll(x, quarter, 1))


def _at_in_kernel(x_ref, mod_ref, nw_ref, w_ref, cos_ref, sin_ref, q_ref, kt_ref, v_ref, kc_ref, vc_ref):
    i = pl.program_id(0)
    cond = _tile_cond(i)
    h = _norm_mod(x_ref[...], nw_ref[...], _mod_row(mod_ref, cond, 0), _mod_row(mod_ref, cond, 1)).astype(BF16)
    nq = ATT_Q_HEADS * ATT_HEAD_DIM
    nk = ATT_KV_HEADS * ATT_HEAD_DIM
    cos = cos_ref[...]
    sin = sin_ref[...]

    def rope(x):
        reps = x.shape[-1] // LANES
        return x * jnp.concatenate([cos] * reps, axis=1) + _swap_rotary_halves(x) * jnp.concatenate([sin] * reps, axis=1)

    q_ref[...] = (rope(jnp.dot(h, w_ref[:, :nq], preferred_element_type=F32)) * ATT_HEAD_DIM ** -0.5).astype(BF16)
    k = rope(jnp.dot(h, w_ref[:, nq:nq + nk], preferred_element_type=F32))
    v = jnp.dot(h, w_ref[:, nq + nk:], preferred_element_type=F32)
    kt = k.T
    kt_ref[...] = kt.astype(BF16)
    v_ref[...] = v.astype(BF16)

    @pl.when(i < PROMPT_TILES)
    def _():
        kc_ref[...] = kt
        vc_ref[...] = v.T


def _rope_tables():
    f32 = np.float32
    pos = np.arange(DEC_SEQ)
    t_row = (pos // GRID_W).astype(f32)
    t_col = (pos % GRID_W).astype(f32)
    inv = f32(ROPE_BASE) ** (-np.arange(0, ROPE_HALF, 2, dtype=f32) / f32(ROPE_HALF))
    j = np.arange(LANES) % ATT_HEAD_DIM
    freq = inv[(j % ROPE_HALF) % (ROPE_HALF // 2)]
    ang = (np.where((j < ROPE_HALF)[None, :], t_row[:, None], t_col[:, None]) * freq[None, :]).astype(f32)
    sign = np.where((j % ROPE_HALF) < ROPE_HALF // 2, -1.0, 1.0).astype(f32)
    cos = np.concatenate([np.ones((TM, LANES), f32), np.cos(ang)], axis=0)
    sin = np.concatenate([np.zeros((TM, LANES), f32), np.sin(ang) * sign[None, :]], axis=0)
    return jnp.asarray(cos, F32), jnp.asarray(sin, F32)


def _at_in(x, mod, nw, w_in, cos, sin):
    tile = lambda i: (i, 0)
    fixed2 = lambda i: (0, 0)
    rope_tile = lambda i: (jnp.where(i < PROMPT_TILES, 0, 1 + (i - PROMPT_TILES) % TILES_PER_DEC_SEQ), 0)
    nk = ATT_KV_HEADS * ATT_HEAD_DIM
    return pl.pallas_call(
        _at_in_kernel,
        out_shape=(
            jax.ShapeDtypeStruct((N_TOK, D_MODEL), BF16),
            jax.ShapeDtypeStruct((nk, N_TOK), BF16),
            jax.ShapeDtypeStruct((N_TOK, nk), BF16),
            jax.ShapeDtypeStruct((N_PROMPT, nk), F32),
            jax.ShapeDtypeStruct((N_PROMPT, nk), F32),
        ),
        grid=(N_TILES,),
        in_specs=[
            pl.BlockSpec((TM, D_MODEL), tile),
            pl.BlockSpec(mod.shape, fixed2),
            pl.BlockSpec((1, D_MODEL), fixed2),
            pl.BlockSpec(w_in.shape, fixed2),
            pl.BlockSpec((TM, LANES), rope_tile),
            pl.BlockSpec((TM, LANES), rope_tile),
        ],
        out_specs=(pl.BlockSpec((TM, D_MODEL), tile), pl.BlockSpec((nk, TM), lambda i: (0, i)),
                   pl.BlockSpec((TM, nk), tile),
                   pl.BlockSpec((TM, nk), _prompt_tile), pl.BlockSpec((TM, nk), _prompt_tile)),
        compiler_params=_params(40, "arbitrary"),
        name="attn_in_proj",
    )(x, mod, nw, w_in, cos, sin)


def _attend(q, kt_all, v_all, mask, sink_ref, o_ref):
    nq = q.shape[0]
    group_lanes = ATT_GROUP * ATT_HEAD_DIM
    lane = lax.broadcasted_iota(jnp.int32, (nq, group_lanes), 1)
    mine = [(lane // ATT_HEAD_DIM) == g for g in range(ATT_GROUP)]
    row_head = lax.broadcasted_iota(jnp.int32, (ATT_GROUP * nq, 1), 0) // nq
    if mask is not None:
        mask = jnp.concatenate([mask] * ATT_GROUP, axis=0)
    for hk in range(ATT_KV_HEADS):
        vh = v_all[:, hk * ATT_HEAD_DIM:(hk + 1) * ATT_HEAD_DIM]
        kt = jnp.concatenate([kt_all[hk * ATT_HEAD_DIM:(hk + 1) * ATT_HEAD_DIM, :]] * ATT_GROUP, axis=0)
        vt = jnp.concatenate([vh] * ATT_GROUP, axis=1)
        qg = q[:, hk * group_lanes:(hk + 1) * group_lanes]
        q_stack = jnp.concatenate([jnp.where(mine[g], qg, jnp.zeros_like(qg)) for g in range(ATT_GROUP)], axis=0)
        s = jnp.dot(q_stack, kt, preferred_element_type=F32)
        if mask is not None:
            s = jnp.where(mask, s, -jnp.inf)
        sink = jnp.zeros((ATT_GROUP * nq, 1), F32)
        for g in range(ATT_GROUP):
            sink = jnp.where(row_head == g, sink_ref[hk * ATT_GROUP + g], sink)
        m = jnp.maximum(jnp.max(s, axis=-1, keepdims=True), sink)
        p = jnp.exp(s - m)
        denom = jnp.sum(p, axis=-1, keepdims=True) + jnp.exp(sink - m)
        o = jnp.dot(p.astype(BF16), vt, preferred_element_type=F32) / denom
        acc = jnp.where(mine[0], o[:nq], 0.0)
        for g in range(1, ATT_GROUP):
            acc = acc + jnp.where(mine[g], o[g * nq:(g + 1) * nq], 0.0)
        o_ref[:, hk * group_lanes:(hk + 1) * group_lanes] = acc.astype(BF16)


def _ctx_attn_kernel(sink_ref, q_ref, k_ref, v_ref, o_ref):
    _attend(q_ref[...], k_ref[...], v_ref[...], None, sink_ref, o_ref)


def _lat_attn_kernel(sink_ref, q_ref, kp_ref, kc_ref, kn_ref, vp_ref, vc_ref, vn_ref, ck_ref, cv_ref, o_ref):
    jb = pl.program_id(1)
    kt_all = jnp.concatenate([kp_ref[...], kc_ref[...], kn_ref[...], ck_ref[0].astype(BF16)], axis=1)
    v_all = jnp.concatenate([vp_ref[...], vc_ref[...], vn_ref[...], cv_ref[0].astype(BF16)], axis=0)
    nkeys = 3 * BLOCK + PAST_LEN
    qi = lax.broadcasted_iota(jnp.int32, (BLOCK, nkeys), 0)
    kj = lax.broadcasted_iota(jnp.int32, (BLOCK, nkeys), 1)
    qpos = jb * BLOCK + qi
    kpos = (jb - 1) * BLOCK + kj
    local_ok = (jnp.abs(qpos - kpos) <= WINDOW) & (kpos >= 0) & (kpos < DEC_SEQ)
    mask = (kj >= 3 * BLOCK) | local_ok
    _attend(q_ref[...], kt_all, v_all, mask, sink_ref, o_ref)


def _attention(q, kt, v, cache_kt, cache_v, sink):
    nk = ATT_KV_HEADS * ATT_HEAD_DIM
    smem = pl.BlockSpec(memory_space=pltpu.SMEM)
    ctx = pl.pallas_call(
        _ctx_attn_kernel,
        out_shape=jax.ShapeDtypeStruct((N_PROMPT, D_MODEL), BF16),
        grid=(BATCH,),
        in_specs=[
            smem,
            pl.BlockSpec((SEQ, D_MODEL), lambda b: (b, 0)),
            pl.BlockSpec((nk, SEQ), lambda b: (0, b)),
            pl.BlockSpec((SEQ, nk), lambda b: (b, 0)),
        ],
        out_specs=pl.BlockSpec((SEQ, D_MODEL), lambda b: (b, 0)),
        compiler_params=_params(40, "arbitrary"),
        name="context_attention",
    )(sink, q, kt, v)

    nb = DEC_SEQ // BLOCK
    base = N_PROMPT // BLOCK
    cur = lambda b, j: (base + b * nb + j, 0)
    prev = lambda b, j: (base + b * nb + jnp.maximum(j - 1, 0), 0)
    nxt = lambda b, j: (base + b * nb + jnp.minimum(j + 1, nb - 1), 0)
    kv_blk = lambda f: pl.BlockSpec((BLOCK, nk), f)
    kt_blk = lambda f: pl.BlockSpec((nk, BLOCK), lambda b, j: f(b, j)[::-1])
    cache_blk = pl.BlockSpec((1, PAST_LEN, nk), lambda b, j: (b, 0, 0))
    cache_kt_blk = pl.BlockSpec((1, nk, PAST_LEN), lambda b, j: (b, 0, 0))
    lat = pl.pallas_call(
        _lat_attn_kernel,
        out_shape=jax.ShapeDtypeStruct((DEC_BATCH * DEC_SEQ, D_MODEL), BF16),
        grid=(DEC_BATCH, nb),
        in_specs=[
            smem,
            pl.BlockSpec((BLOCK, D_MODEL), cur),
            kt_blk(prev), kt_blk(cur), kt_blk(nxt),
            kv_blk(prev), kv_blk(cur), kv_blk(nxt),
            cache_kt_blk, cache_blk,
        ],
        out_specs=pl.BlockSpec((BLOCK, D_MODEL), lambda b, j: (b * nb + j, 0)),
        compiler_params=_params(40, "arbitrary", "arbitrary"),
        name="latent_attention",
    )(sink, q, kt, kt, kt, v, v, v, cache_kt, cache_v)
    return ctx, lat


def kernel(x_prompt, x_sample, state_hgrn, cache_k, cache_v, c, c_ctx, ada_w, ada_b, norm_w, hg_w_in,
           hg_lb_logits, hg_onorm, hg_w_out, at_w_in, at_sink, at_w_out, moe_w_group, moe_b_group,
           moe_w_expert, moe_b_expert, moe_w1, moe_w3, moe_w2, final_norm):
    xp = x_prompt.reshape(N_PROMPT, D_MODEL)
    xs = x_sample.reshape(N_TOK - N_PROMPT, D_MODEL)
    cond = jnp.concatenate([c_ctx[None, :], c, jnp.zeros((N_COND - 1 - DEC_BATCH, D_MODEL), F32)], axis=0)
    mod = _ada(cond, ada_w, ada_b)
    nk = ATT_KV_HEADS * ATT_HEAD_DIM

    def router_params(i):
        pad = jnp.zeros((D_MODEL, ROUTER_LANES - N_EXPERTS - N_GROUPS), F32)
        w = jnp.concatenate([moe_w_expert[i], moe_w_group[i], pad], axis=1)
        b = jnp.concatenate([moe_b_expert[i], moe_b_group[i], pad[0]])[None, :]
        hi = w.astype(BF16)
        lo = (w - hi.astype(F32)).astype(BF16)
        return jnp.stack([hi, lo]), b

    tile = lambda i: (i, 0)
    blk = pl.BlockSpec((TM, D_MODEL), tile)

    prompt_blk = pl.BlockSpec((TM, D_MODEL), _prompt_tile)
    sample_blk = pl.BlockSpec((TM, D_MODEL), _sample_tile)
    q, v, g, lff, kf, lfb, kb = _hg_in(xp, xs, mod[0], norm_w[0, 0][None, :], hg_w_in[0].astype(BF16), hg_lb_logits)
    o_f, o_b, state_new = _gla(q, v, lff, kf, lfb, kb, state_hgrn)
    wr, br = router_params(0)
    routed = _mixer_out(
        _hg_out_kernel, "hgrn_out_route", (o_f, o_b, g, hg_onorm[0][None, :], xp, xs),
        (blk, blk, blk, pl.BlockSpec((1, HG_DK), lambda i: (0, 0)), prompt_blk, sample_blk),
        mod[0], norm_w[0, 1][None, :], hg_w_out[0].astype(BF16), wr, br)
    x = _moe_layer(*routed, mod[0], 0, moe_w1, moe_w3, moe_w2)

    cos, sin = _rope_tables()
    qa, ka, va, k_ctx, v_ctx = _at_in(x, mod[1], norm_w[1, 0][None, :], at_w_in[0].astype(BF16), cos, sin)
    attn_ctx, attn_lat = _attention(qa, ka, va, cache_k[:, 0].reshape(DEC_BATCH, PAST_LEN, nk).transpose(0, 2, 1),
                                    cache_v[:, 0].reshape(DEC_BATCH, PAST_LEN, nk), at_sink[0])
    wr, br = router_params(1)
    routed = _mixer_out(
        _at_out_kernel, "attn_out_route", (attn_ctx, attn_lat, x), (prompt_blk, sample_blk, blk),
        mod[1], norm_w[1, 1][None, :], at_w_out[0].astype(BF16), wr, br)
    y_prompt, y_sample = _moe_layer(*routed, mod[1], 1, moe_w1, moe_w3, moe_w2, final_norm[None, :])

    def cache(feature_major):
        return feature_major.reshape(BATCH, 1, ATT_KV_HEADS, ATT_HEAD_DIM, SEQ).transpose(0, 1, 4, 2, 3)

    return (y_prompt.reshape(BATCH, SEQ, D_MODEL), y_sample.reshape(DEC_BATCH, DEC_SEQ, D_MODEL), state_new,
            cache(k_ctx), cache(v_ctx))
```

```python
import functools

import jax
import jax.numpy as jnp
import numpy as np
from jax import lax
from jax.experimental import pallas as pl
from jax.experimental.pallas import tpu as pltpu
from jax.experimental.pallas import tpu_sc as plsc

F32 = jnp.float32
BF16 = jnp.bfloat16

D_MODEL = 1024
BATCH = 16
SEQ = 256
DEC_BATCH = 2
DEC_SEQ = 1024
PAST_LEN = 512
GRID_W = 64
HG_HEADS = 8
HG_DK = 128
CHUNK = 16
ATT_HEAD_DIM = 64
ATT_Q_HEADS = 16
ATT_KV_HEADS = 4
ATT_GROUP = 4
WINDOW = 128
BLOCK = 128
ROPE_HALF = 32
ROPE_BASE = 10000.0
N_GROUPS = 4
EXPERTS_PER_GROUP = 8
N_EXPERTS = 32
D_EXPERT = 256
EPS = 1e-6

N_PROMPT = BATCH * SEQ
N_TOK = N_PROMPT + DEC_BATCH * DEC_SEQ
TM = 256
N_TILES = N_TOK // TM
PROMPT_TILES = N_PROMPT // TM
TILES_PER_DEC_SEQ = DEC_SEQ // TM
LANES = 128
N_COND_USED = 1 + DEC_BATCH
N_COND = 8
ROUTER_LANES = 128
GLA_UNROLL = 8
DECAY_CLAMP = 60.0
PACK_WORDS = D_MODEL // 4
MOE_TILE = 256
MOE_ROWS = 2 * N_TOK + N_EXPERTS * MOE_TILE
MIB = 1024 * 1024


def _params(vmem_mib, *semantics):
    return pltpu.CompilerParams(dimension_semantics=semantics, vmem_limit_bytes=vmem_mib * MIB)


def _tile_cond(i):
    return jnp.where(i < PROMPT_TILES, 0, 1 + (i - PROMPT_TILES) // TILES_PER_DEC_SEQ)


def _prompt_tile(i):
    return (jnp.minimum(i, PROMPT_TILES - 1), 0)


def _sample_tile(i):
    return (jnp.maximum(i - PROMPT_TILES, 0), 0)


def _stream_tile(prompt_ref, sample_ref):
    return jnp.where(pl.program_id(0) < PROMPT_TILES, prompt_ref[...], sample_ref[...])


def _mod_row(mod_ref, cond, which):
    return mod_ref[pl.ds(cond, 1), which * D_MODEL:(which + 1) * D_MODEL]


def _norm_mod(x, nw, shift, scale):
    y = x * lax.rsqrt(jnp.mean(x * x, axis=-1, keepdims=True) + EPS)
    return (y * nw) * (1.0 + scale) + shift


def _silu(x):
    return x * jax.nn.sigmoid(x)


def _ada_kernel(c_ref, w_ref, b_ref, o_ref):
    s = [_silu(c_ref[r]) for r in range(N_COND_USED)]
    tn = w_ref.shape[-1]
    rows = []
    for r in range(N_COND_USED):
        cols = [jnp.sum(w_ref[0, :, j * LANES:(j + 1) * LANES] * s[r], axis=0, keepdims=True)
                for j in range(tn // LANES)]
        rows.append(jnp.concatenate(cols, axis=1) + b_ref[0])
    rows.append(jnp.zeros((N_COND - N_COND_USED, tn), F32))
    o_ref[0] = jnp.concatenate(rows, axis=0)


def _ada(cond, ada_w, ada_b):
    depth, _, n = ada_w.shape
    tn = 1536
    cond_cols = jnp.broadcast_to(cond[:, :, None], (N_COND_USED, D_MODEL, LANES))
    return pl.pallas_call(
        _ada_kernel,
        out_shape=jax.ShapeDtypeStruct((depth, N_COND, n), F32),
        grid=(depth, n // tn),
        in_specs=[
            pl.BlockSpec((N_COND_USED, D_MODEL, LANES), lambda l, j: (0, 0, 0)),
            pl.BlockSpec((1, D_MODEL, tn), lambda l, j: (l, 0, j)),
            pl.BlockSpec((1, 1, tn), lambda l, j: (l, 0, j)),
        ],
        out_specs=pl.BlockSpec((1, N_COND, tn), lambda l, j: (l, 0, j)),
        compiler_params=_params(40, "arbitrary", "arbitrary"),
        name="ada_modulation",
    )(cond_cols, ada_w, ada_b.reshape(depth, 1, n))


def _hg_in_kernel(xp_ref, xs_ref, mod_ref, nw_ref, w_ref, lbl_ref,
                  q_ref, v_ref, g_ref, lff_ref, kf_ref, lfb_ref, kb_ref):
    cond = _tile_cond(pl.program_id(0))
    x = _stream_tile(xp_ref, xs_ref)
    h = _norm_mod(x, nw_ref[...], _mod_row(mod_ref, cond, 0), _mod_row(mod_ref, cond, 1)).astype(BF16)

    def proj(c):
        return jnp.dot(h, w_ref[:, c * D_MODEL:(c + 1) * D_MODEL], preferred_element_type=F32)

    l0, l1, l2 = lbl_ref[0], lbl_ref[1], lbl_ref[2]
    m = jnp.maximum(jnp.maximum(l0, l1), l2)
    e0, e1, e2 = jnp.exp(l0 - m), jnp.exp(l1 - m), jnp.exp(l2 - m)
    lb = e0 / (e0 + e1 + e2)

    q_ref[...] = proj(0).astype(BF16)
    v_ref[...] = proj(1).astype(BF16)
    for d, (lf_ref, k_ref) in enumerate(((lff_ref, kf_ref), (lfb_ref, kb_ref))):
        lbd = lb[d:d + 1, :]
        f = lbd + (1.0 - lbd) * jax.nn.sigmoid(proj(2 + d))
        lf = jnp.log(f)
        hi = lf.astype(BF16)
        lf_ref[0] = hi
        lf_ref[1] = (lf - hi.astype(F32)).astype(BF16)
        k_ref[...] = (1.0 - f).astype(BF16)
    g_ref[...] = proj(4).astype(BF16)


def _hg_in(xp, xs, mod, nw, w_in, lb_logits):
    tile = lambda i: (i, 0)
    fixed2 = lambda i: (0, 0)
    bf = jax.ShapeDtypeStruct((N_TOK, D_MODEL), BF16)
    ff = jax.ShapeDtypeStruct((2, N_TOK, D_MODEL), BF16)
    blk = pl.BlockSpec((TM, D_MODEL), tile)
    split_blk = pl.BlockSpec((2, TM, D_MODEL), lambda i: (0, i, 0))
    return pl.pallas_call(
        _hg_in_kernel,
        out_shape=(bf, bf, bf, ff, bf, ff, bf),
        grid=(N_TILES,),
        in_specs=[
            pl.BlockSpec((TM, D_MODEL), _prompt_tile),
            pl.BlockSpec((TM, D_MODEL), _sample_tile),
            pl.BlockSpec(mod.shape, fixed2),
            pl.BlockSpec((1, D_MODEL), fixed2),
            pl.BlockSpec(w_in.shape, fixed2),
            pl.BlockSpec(lb_logits.shape, lambda i: (0, 0, 0)),
        ],
        out_specs=(blk, blk, blk, split_blk, blk, split_blk, blk),
        compiler_params=_params(56, "arbitrary"),
        name="hgrn_in_proj",
    )(xp, xs, mod, nw, w_in, lb_logits)


def _gla_direction(reverse, q_ref, v_ref, lf_ref, k_ref, st_ref, sw_ref, o_ref,
                   ut_scr, qd_scr, kd_scr, ki_scr, cd_scr):
    r = lax.broadcasted_iota(jnp.int32, (TM, TM), 0)
    c = lax.broadcasted_iota(jnp.int32, (TM, TM), 1)
    same = (r // CHUNK) == (c // CHUNK)
    tri = (same & ((c >= r) if reverse else (c <= r))).astype(BF16)
    b = jnp.dot(tri, lf_ref[0], preferred_element_type=F32) + jnp.dot(tri, lf_ref[1], preferred_element_type=F32)
    b3 = b.reshape(TM // CHUNK, CHUNK, D_MODEL)
    edge = 0 if reverse else CHUNK - 1
    tot = jnp.broadcast_to(b3[:, edge:edge + 1, :], b3.shape).reshape(TM, D_MODEL)
    k = k_ref[...]
    qd_scr[...] = q_ref[...] * jnp.exp(b).astype(BF16)
    kd_scr[...] = k * jnp.exp(tot - b).astype(BF16)
    ki_scr[...] = k * jnp.exp(jnp.minimum(-b, DECAY_CLAMP)).astype(BF16)
    cd_scr[...] = jnp.exp(tot)

    tr = lax.broadcasted_iota(jnp.int32, (CHUNK, CHUNK), 0)
    tc = lax.broadcasted_iota(jnp.int32, (CHUNK, CHUNK), 1)
    keep = (tc >= tr) if reverse else (tc <= tr)
    nt = (((1,), (1,)), ((), ()))
    tn = (((0,), (0,)), ((), ()))

    heads = [slice(h * HG_DK, (h + 1) * HG_DK) for h in range(HG_HEADS)]
    for h in range(HG_HEADS):
        sw_ref[0, h] = st_ref[0, h].T.astype(BF16)

    n_chunks = TM // CHUNK

    def chunk_rows(step):
        ci = (n_chunks - 1 - step) if reverse else step
        row0 = pl.multiple_of(ci * CHUNK, CHUNK)
        return row0, pl.ds(row0, CHUNK)

    def key_value_product(rows):
        for h, cols in enumerate(heads):
            ut_scr[h] = lax.dot_general(v_ref[rows, cols], kd_scr[rows, cols], tn, preferred_element_type=F32)

    key_value_product(chunk_rows(0)[1])

    def chunk(step, src, dst):
        row0, rows = chunk_rows(step)
        cd_row = cd_scr[pl.ds(row0, 1), :]
        for h, cols in enumerate(heads):
            new = st_ref[src, h] * cd_row[:, cols] + ut_scr[h]
            st_ref[dst, h] = new
            sw_ref[dst, h] = new.T.astype(BF16)
        a = [lax.dot_general(qd_scr[rows, cols], ki_scr[rows, cols], nt, preferred_element_type=F32) for cols in heads]
        inter = [jnp.dot(qd_scr[rows, cols], sw_ref[src, h], preferred_element_type=F32)
                 for h, cols in enumerate(heads)]
        key_value_product(chunk_rows(jnp.minimum(step + 1, n_chunks - 1))[1])
        for h, cols in enumerate(heads):
            am = jnp.where(keep, a[h], 0.0).astype(BF16)
            o_ref[rows, cols] = jnp.dot(am, v_ref[rows, cols], preferred_element_type=F32) + inter[h]

    def chunk_group(it, carry):
        for p in range(GLA_UNROLL):
            chunk(GLA_UNROLL * it + p, p % 2, 1 - p % 2)
        return carry

    lax.fori_loop(0, n_chunks // GLA_UNROLL, chunk_group, 0)


def _gla_kernel(qf_ref, vf_ref, lff_ref, kf_ref, qb_ref, vb_ref, lfb_ref, kb_ref, s0_ref,
                of_ref, ob_ref, sout_ref, stf_scr, stb_scr, sw_scr, ut_scr, qd_scr, kd_scr, ki_scr, cd_scr):
    i = pl.program_id(0)
    is_prompt = i < PROMPT_TILES
    first = jnp.logical_or(is_prompt, (i - PROMPT_TILES) % TILES_PER_DEC_SEQ == 0)

    @pl.when(first)
    def _():
        keep0 = jnp.where(is_prompt, 0.0, 1.0)
        for h in range(HG_HEADS):
            stf_scr[0, h] = s0_ref[0, 0, 0, h].T * keep0
            stb_scr[0, h] = s0_ref[0, 0, 1, h].T * keep0

    _gla_direction(False, qf_ref, vf_ref, lff_ref, kf_ref, stf_scr, sw_scr, of_ref,
                   ut_scr, qd_scr, kd_scr, ki_scr, cd_scr)
    _gla_direction(True, qb_ref, vb_ref, lfb_ref, kb_ref, stb_scr, sw_scr, ob_ref,
                   ut_scr, qd_scr, kd_scr, ki_scr, cd_scr)

    @pl.when(is_prompt)
    def _():
        for h in range(HG_HEADS):
            sout_ref[0, 0, 0, h] = stf_scr[0, h].T
            sout_ref[0, 0, 1, h] = stb_scr[0, h].T


def _gla(q, v, lff, kf, lfb, kb, state_hgrn):
    def fwd_tile(i):
        return (i, 0)

    def bwd_tile(i):
        j = (i - PROMPT_TILES) % TILES_PER_DEC_SEQ
        return (jnp.where(i < PROMPT_TILES, i, i - j + (TILES_PER_DEC_SEQ - 1 - j)), 0)

    def s0_idx(i):
        return (jnp.maximum(i - PROMPT_TILES, 0) // TILES_PER_DEC_SEQ, 0, 0, 0, 0, 0)

    def sout_idx(i):
        return (jnp.minimum(i, PROMPT_TILES - 1), 0, 0, 0, 0, 0)

    f_blk = pl.BlockSpec((TM, D_MODEL), fwd_tile)
    b_blk = pl.BlockSpec((TM, D_MODEL), bwd_tile)
    f_split = pl.BlockSpec((2, TM, D_MODEL), lambda i: (0,) + fwd_tile(i))
    b_split = pl.BlockSpec((2, TM, D_MODEL), lambda i: (0,) + bwd_tile(i))
    st_blk = (1, 1, 2, HG_HEADS, HG_DK, HG_DK)
    return pl.pallas_call(
        _gla_kernel,
        out_shape=(
            jax.ShapeDtypeStruct((N_TOK, D_MODEL), F32),
            jax.ShapeDtypeStruct((N_TOK, D_MODEL), F32),
            jax.ShapeDtypeStruct((BATCH,) + st_blk[1:], F32),
        ),
        grid=(N_TILES,),
        in_specs=[f_blk, f_blk, f_split, f_blk, b_blk, b_blk, b_split, b_blk, pl.BlockSpec(st_blk, s0_idx)],
        out_specs=(f_blk, b_blk, pl.BlockSpec(st_blk, sout_idx)),
        scratch_shapes=[
            pltpu.VMEM((2, HG_HEADS, HG_DK, HG_DK), F32),
            pltpu.VMEM((2, HG_HEADS, HG_DK, HG_DK), F32),
            pltpu.VMEM((2, HG_HEADS, HG_DK, HG_DK), BF16),
            pltpu.VMEM((HG_HEADS, HG_DK, HG_DK), F32),
            pltpu.VMEM((TM, D_MODEL), BF16),
            pltpu.VMEM((TM, D_MODEL), BF16),
            pltpu.VMEM((TM, D_MODEL), BF16),
            pltpu.VMEM((TM, D_MODEL), F32),
        ],
        compiler_params=_params(48, "arbitrary"),
        name="hgrn_recurrence",
    )(q, v, lff, kf, q, v, lfb, kb, state_hgrn)


def _route(logits):
    lane = lax.broadcasted_iota(jnp.int32, logits.shape, 1)
    neg = jnp.float32(-jnp.inf)
    is_group = (lane >= N_EXPERTS) & (lane < N_EXPERTS + N_GROUPS)
    gl = jnp.where(is_group, logits, neg)
    gmax = jnp.max(gl, axis=-1, keepdims=True)
    g_sel = jnp.argmax(gl, axis=-1, keepdims=True).astype(jnp.int32) - N_EXPERTS
    gsum = jnp.sum(jnp.exp(gl - gmax), axis=-1, keepdims=True)
    p_g = 1.0 / gsum
    in_sel = (lane < N_EXPERTS) & ((lane // EXPERTS_PER_GROUP) == g_sel)
    el = jnp.where(in_sel, logits, neg)
    m1 = jnp.max(el, axis=-1, keepdims=True)
    i1 = jnp.argmax(el, axis=-1, keepdims=True).astype(jnp.int32)
    el2 = jnp.where(lane == i1, neg, el)
    m2 = jnp.max(el2, axis=-1, keepdims=True)
    i2 = jnp.argmax(el2, axis=-1, keepdims=True).astype(jnp.int32)
    e2 = jnp.exp(m2 - m1)
    return i1, i2, p_g / (1.0 + e2), p_g * e2 / (1.0 + e2)


def _pack_rows(x):
    q = PACK_WORDS
    bits = pltpu.bitcast(x.astype(BF16).astype(F32), jnp.uint32)
    return [(bits[:, (2 + h) * q:(3 + h) * q] & jnp.uint32(0xFFFF0000)) | (bits[:, h * q:(h + 1) * q] >> 16)
            for h in range(2)]


def _unpack_rows(half0, half1):
    lo = lambda w: pltpu.bitcast(w << 16, F32).astype(BF16)
    hi = lambda w: pltpu.bitcast(w & jnp.uint32(0xFFFF0000), F32).astype(BF16)
    return [lo(half0), lo(half1), hi(half0), hi(half1)]


def _mixer_tail(mix_bf16, x, mod_ref, nw2_ref, wo_ref, wr_ref, br_ref,
                x1_ref, h2p_ref, ri_ref, rw_ref, cnt_ref, carry_scr):
    i = pl.program_id(0)
    cond = _tile_cond(i)
    out = jnp.dot(mix_bf16, wo_ref[...], preferred_element_type=F32)
    x1 = x + _mod_row(mod_ref, cond, 2) * out
    x1_ref[...] = x1
    h2 = _norm_mod(x1, nw2_ref[...], _mod_row(mod_ref, cond, 3), _mod_row(mod_ref, cond, 4))
    for h, words in enumerate(_pack_rows(h2)):
        h2p_ref[h] = words
    h_hi = h2.astype(BF16)
    h_lo = (h2 - h_hi.astype(F32)).astype(BF16)
    logits = (jnp.dot(h_hi, wr_ref[0], preferred_element_type=F32)
              + (jnp.dot(h_hi, wr_ref[1], preferred_element_type=F32)
                 + jnp.dot(h_lo, wr_ref[0], preferred_element_type=F32))) + br_ref[...]
    i1, i2, w1, w2 = _route(logits)

    @pl.when(i == 0)
    def _():
        carry_scr[...] = jnp.zeros_like(carry_scr)

    lane = lax.broadcasted_iota(jnp.int32, logits.shape, 1)
    chosen = ((lane == i1) | (lane == i2)).astype(BF16)
    r = lax.broadcasted_iota(jnp.int32, (TM, TM), 0)
    c = lax.broadcasted_iota(jnp.int32, (TM, TM), 1)
    before = jnp.dot((c < r).astype(BF16), chosen, preferred_element_type=F32) + carry_scr[...]
    r1 = jnp.sum(jnp.where(lane == i1, before, 0.0), axis=-1, keepdims=True).astype(jnp.int32)
    r2 = jnp.sum(jnp.where(lane == i2, before, 0.0), axis=-1, keepdims=True).astype(jnp.int32)
    total = carry_scr[...] + jnp.sum(chosen.astype(F32), axis=0, keepdims=True)
    carry_scr[...] = total
    cnt_ref[...] = total
    ri_ref[...] = jnp.where(lane == 0, i1, jnp.where(lane == 1, i2, jnp.where(lane == 2, r1, r2)))
    rw_ref[...] = jnp.where(lane == 0, w1, w2)


def _hg_out_kernel(of_ref, ob_ref, g_ref, on_ref, xp_ref, xs_ref, *rest):
    o = of_ref[...] + ob_ref[...]
    parts = []
    for h in range(HG_HEADS):
        oh = o[:, h * HG_DK:(h + 1) * HG_DK]
        parts.append(oh * lax.rsqrt(jnp.mean(oh * oh, axis=-1, keepdims=True) + EPS) * on_ref[...])
    y = jnp.concatenate(parts, axis=1) * _silu(g_ref[...].astype(F32))
    _mixer_tail(y.astype(BF16), _stream_tile(xp_ref, xs_ref), *rest)


def _at_out_kernel(ac_ref, al_ref, x_ref, *rest):
    _mixer_tail(_stream_tile(ac_ref, al_ref), x_ref[...], *rest)


def _mixer_out(kernel_fn, name, mix_inputs, mix_specs, mod, nw2, w_out, w_router, b_router):
    tile = lambda i: (i, 0)
    fixed2 = lambda i: (0, 0)
    blk = pl.BlockSpec((TM, D_MODEL), tile)
    lanes_blk = pl.BlockSpec((TM, ROUTER_LANES), tile)
    return pl.pallas_call(
        kernel_fn,
        out_shape=(
            jax.ShapeDtypeStruct((N_TOK, D_MODEL), F32),
            jax.ShapeDtypeStruct((2, N_TOK, PACK_WORDS), jnp.uint32),
            jax.ShapeDtypeStruct((N_TOK, ROUTER_LANES), jnp.int32),
            jax.ShapeDtypeStruct((N_TOK, ROUTER_LANES), F32),
            jax.ShapeDtypeStruct((1, ROUTER_LANES), F32),
        ),
        grid=(N_TILES,),
        in_specs=list(mix_specs) + [
            pl.BlockSpec(mod.shape, fixed2),
            pl.BlockSpec((1, D_MODEL), fixed2),
            pl.BlockSpec(w_out.shape, fixed2),
            pl.BlockSpec(w_router.shape, lambda i: (0, 0, 0)),
            pl.BlockSpec((1, ROUTER_LANES), fixed2),
        ],
        out_specs=(blk, pl.BlockSpec((2, TM, PACK_WORDS), lambda i: (0, i, 0)), lanes_blk, lanes_blk,
                   pl.BlockSpec((1, ROUTER_LANES), fixed2)),
        scratch_shapes=[pltpu.VMEM((1, ROUTER_LANES), F32)],
        compiler_params=_params(40, "arbitrary"),
        name=name,
    )(*mix_inputs, mod, nw2, w_out, w_router, b_router)


def _moe_plan(route_i, counts):
    cnt = counts[0, :N_EXPERTS].astype(jnp.int32)
    padded = ((cnt + MOE_TILE - 1) // MOE_TILE) * MOE_TILE
    ends = jnp.cumsum(padded)
    offs = ends - padded
    experts = route_i[:, 0:2]
    pos = jnp.sum(jnp.where(experts[:, :, None] == jnp.arange(N_EXPERTS)[None, None, :], offs[None, None, :], 0),
                  axis=-1) + route_i[:, 2:4]
    tile_start = jnp.arange(MOE_ROWS // MOE_TILE, dtype=jnp.int32) * MOE_TILE
    tile_expert = jnp.minimum(jnp.sum(ends[None, :] <= tile_start[:, None], axis=1), N_EXPERTS - 1).astype(jnp.int32)
    of_tile = tile_expert[:, None] == jnp.arange(N_EXPERTS)[None, :]
    tile_offs = jnp.sum(jnp.where(of_tile, offs[None, :], 0), axis=1)
    tile_cnt = jnp.sum(jnp.where(of_tile, cnt[None, :], 0), axis=1)
    tile_rows = jnp.clip(tile_offs + tile_cnt - tile_start, 0, MOE_TILE).astype(jnp.int32)
    tile_first = (tile_start == tile_offs).astype(jnp.int32)
    n_active = (ends[-1] // MOE_TILE).astype(jnp.int32).reshape(1)
    return pos.astype(jnp.int32), tile_expert, tile_rows, tile_first, n_active


SC_WINDOW = 128


def _sc_mesh():
    return plsc.VectorSubcoreMesh(core_axis_name="c", subcore_axis_name="s")


def _sc_scatter_rows(x, idx_a, idx_b, n_out_rows):
    n = x.shape[0]

    @functools.partial(pl.kernel, out_type=jax.ShapeDtypeStruct((n_out_rows, PACK_WORDS), x.dtype), mesh=_sc_mesh(),
                       scratch_types=[pltpu.SemaphoreType.DMA, pltpu.SemaphoreType.DMA])
    def scatter(x_hbm, ia_hbm, ib_hbm, o_hbm, sem_a, sem_b):
        def body(x_vmem, ia_vmem, ib_vmem):
            copy_a = pltpu.async_copy(x_vmem, o_hbm.at[ia_vmem.at[0]], sem_a)
            copy_b = pltpu.async_copy(x_vmem, o_hbm.at[ib_vmem.at[0]], sem_b)
            copy_a.wait()
            copy_b.wait()

        idx_spec = pl.BlockSpec((1, SC_WINDOW), index_map=lambda i: (0, i))
        pltpu.emit_pipeline(
            body, grid=(n // SC_WINDOW,),
            in_specs=[pl.BlockSpec((SC_WINDOW, PACK_WORDS), index_map=lambda i: (i, 0)), idx_spec, idx_spec],
            out_specs=[],
            core_axis_name=("c", "s"), dimension_semantics=(pltpu.PARALLEL,),
        )(x_hbm, ia_hbm, ib_hbm)

    return scatter(x, idx_a.reshape(1, n), idx_b.reshape(1, n))


def _sc_gather_rows(table, idx):
    n = idx.shape[0]

    @functools.partial(pl.kernel, out_type=jax.ShapeDtypeStruct((n, PACK_WORDS), table.dtype), mesh=_sc_mesh())
    def gather(t_hbm, i_hbm, o_hbm):
        def body(i_vmem, o_vmem):
            pltpu.sync_copy(t_hbm.at[i_vmem.at[0]], o_vmem)

        pltpu.emit_pipeline(
            body, grid=(n // SC_WINDOW,),
            in_specs=[pl.BlockSpec((1, SC_WINDOW), index_map=lambda i: (0, i))],
            out_specs=[pl.BlockSpec((SC_WINDOW, PACK_WORDS), index_map=lambda i: (i, 0))],
            core_axis_name=("c", "s"), dimension_semantics=(pltpu.PARALLEL,),
        )(i_hbm, o_hbm)

    return gather(table, idx.reshape(1, n))


def _ffn_kernel(te_ref, tr_ref, tf_ref, na_ref, xs_ref, w1_ref, w3_ref, w2_ref, ys_ref, w1_scr, w3_scr, w2_scr):
    t = pl.program_id(0)

    @pl.when(t < na_ref[0])
    def _():
        @pl.when(tf_ref[t] == 1)
        def _():
            w1_scr[...] = w1_ref[0, 0].astype(BF16)
            w3_scr[...] = w3_ref[0, 0].astype(BF16)
            w2_scr[...] = w2_ref[0, 0].astype(BF16)

        row = lax.broadcasted_iota(jnp.int32, (MOE_TILE, PACK_WORDS), 0)
        live = row < tr_ref[t]
        halves = [jnp.where(live, xs_ref[h], jnp.zeros((MOE_TILE, PACK_WORDS), jnp.uint32)) for h in range(2)]
        chunks = _unpack_rows(*halves)

        def up(w_scr):
            acc = jnp.dot(chunks[0], w_scr[:PACK_WORDS, :], preferred_element_type=F32)
            for k in range(1, len(chunks)):
                acc = acc + jnp.dot(chunks[k], w_scr[k * PACK_WORDS:(k + 1) * PACK_WORDS, :],
                                    preferred_element_type=F32)
            return acc

        hid = (_silu(up(w1_scr)) * up(w3_scr)).astype(BF16)
        for h, words in enumerate(_pack_rows(jnp.dot(hid, w2_scr[...], preferred_element_type=F32))):
            ys_ref[h] = words


def _ffn(xs, tile_expert, tile_rows, tile_first, n_active, layer, w1, w3, w2):
    row_tile = lambda t, te, tr, tf, na: (0, jnp.minimum(t, na[0] - 1), 0)
    expert = lambda t, te, tr, tf, na: (layer, te[t], 0, 0)
    return pl.pallas_call(
        _ffn_kernel,
        out_shape=jax.ShapeDtypeStruct((2, MOE_ROWS, PACK_WORDS), jnp.uint32),
        grid_spec=pltpu.PrefetchScalarGridSpec(
            num_scalar_prefetch=4,
            grid=(MOE_ROWS // MOE_TILE,),
            in_specs=[
                pl.BlockSpec((2, MOE_TILE, PACK_WORDS), row_tile),
                pl.BlockSpec((1, 1, D_MODEL, D_EXPERT), expert),
                pl.BlockSpec((1, 1, D_MODEL, D_EXPERT), expert),
                pl.BlockSpec((1, 1, D_EXPERT, D_MODEL), expert),
            ],
            out_specs=pl.BlockSpec((2, MOE_TILE, PACK_WORDS), row_tile),
            scratch_shapes=[
                pltpu.VMEM((D_MODEL, D_EXPERT), BF16),
                pltpu.VMEM((D_MODEL, D_EXPERT), BF16),
                pltpu.VMEM((D_EXPERT, D_MODEL), BF16),
            ],
        ),
        compiler_params=_params(32, "arbitrary"),
        name="moe_experts",
    )(tile_expert, tile_rows, tile_first, n_active, xs, w1, w3, w2)


def _moe_res(x_ref, y_ref, rw_ref, mod_ref):
    cond = _tile_cond(pl.program_id(0))
    rw = rw_ref[...]
    wa = rw[:, 0:1]
    wb = rw[:, 1:2]
    ya = _unpack_rows(y_ref[0], y_ref[1])
    yb = _unpack_rows(y_ref[2], y_ref[3])
    y = jnp.concatenate([wa * a.astype(F32) + wb * b.astype(F32) for a, b in zip(ya, yb)], axis=1)
    return x_ref[...] + _mod_row(mod_ref, cond, 5) * y


def _moe_res_kernel(x_ref, y_ref, rw_ref, mod_ref, o_ref):
    o_ref[...] = _moe_res(x_ref, y_ref, rw_ref, mod_ref)


def _moe_res_final_kernel(x_ref, y_ref, rw_ref, mod_ref, fn_ref, op_ref, os_ref):
    x = _moe_res(x_ref, y_ref, rw_ref, mod_ref)
    y = x * lax.rsqrt(jnp.mean(x * x, axis=-1, keepdims=True) + EPS) * fn_ref[...]
    is_prompt = pl.program_id(0) < PROMPT_TILES

    @pl.when(is_prompt)
    def _():
        op_ref[...] = y

    @pl.when(jnp.logical_not(is_prompt))
    def _():
        os_ref[...] = y


def _moe_combine(x1, y_pairs, route_w, mod, final_norm=None):
    tile = lambda i: (i, 0)
    fixed2 = lambda i: (0, 0)
    blk = pl.BlockSpec((TM, D_MODEL), tile)
    specs = [blk,
             pl.BlockSpec((4, TM, PACK_WORDS), lambda i: (0, i, 0)),
             pl.BlockSpec((TM, ROUTER_LANES), tile),
             pl.BlockSpec(mod.shape, fixed2)]
    args = [x1, y_pairs, route_w, mod]
    if final_norm is None:
        kernel_fn = _moe_res_kernel
        out_shape = jax.ShapeDtypeStruct((N_TOK, D_MODEL), F32)
        out_specs = blk
    else:
        kernel_fn = _moe_res_final_kernel
        specs.append(pl.BlockSpec((1, D_MODEL), fixed2))
        args.append(final_norm)
        out_shape = (jax.ShapeDtypeStruct((N_PROMPT, D_MODEL), F32),
                     jax.ShapeDtypeStruct((N_TOK - N_PROMPT, D_MODEL), F32))
        out_specs = (pl.BlockSpec((TM, D_MODEL), _prompt_tile), pl.BlockSpec((TM, D_MODEL), _sample_tile))
    return pl.pallas_call(
        kernel_fn,
        out_shape=out_shape,
        grid=(N_TILES,),
        in_specs=specs,
        out_specs=out_specs,
        compiler_params=_params(32, "arbitrary"),
        name="moe_combine",
    )(*args)


def _moe_layer(x1, h2p, route_i, route_w, counts, mod, layer, w1, w3, w2, final_norm=None):
    pos, tile_expert, tile_rows, tile_first, n_active = _moe_plan(route_i, counts)
    idx_a = jnp.concatenate([pos[:, 0], pos[:, 0] + MOE_ROWS])
    idx_b = jnp.concatenate([pos[:, 1], pos[:, 1] + MOE_ROWS])
    xs = _sc_scatter_rows(h2p.reshape(2 * N_TOK, PACK_WORDS), idx_a, idx_b, 2 * MOE_ROWS)
    ys = _ffn(xs.reshape(2, MOE_ROWS, PACK_WORDS), tile_expert, tile_rows, tile_first, n_active,
              layer, w1, w3, w2)
    y_pairs = _sc_gather_rows(ys.reshape(2 * MOE_ROWS, PACK_WORDS), jnp.concatenate([idx_a, idx_b]))
    return _moe_combine(x1, y_pairs.reshape(4, N_TOK, PACK_WORDS), route_w, mod, final_norm)


def _swap_rotary_halves(x):
    n = x.shape[-1]
    lane = lax.broadcasted_iota(jnp.int32, x.shape, 1)
    quarter = ROPE_HALF // 2
    return jnp.where((lane % ROPE_HALF) < quarter, pltpu.roll(x, n - quarter, 1), pltpu.roll(x, quarter, 1))


def _at_in_kernel(x_ref, mod_ref, nw_ref, w_ref, cos_ref, sin_ref, q_ref, kt_ref, v_ref, kc_ref, vc_ref):
    i = pl.program_id(0)
    cond = _tile_cond(i)
    h = _norm_mod(x_ref[...], nw_ref[...], _mod_row(mod_ref, cond, 0), _mod_row(mod_ref, cond, 1)).astype(BF16)
    nq = ATT_Q_HEADS * ATT_HEAD_DIM
    nk = ATT_KV_HEADS * ATT_HEAD_DIM
    cos = cos_ref[...]
    sin = sin_ref[...]

    def rope(x):
        reps = x.shape[-1] // LANES
        return x * jnp.concatenate([cos] * reps, axis=1) + _swap_rotary_halves(x) * jnp.concatenate([sin] * reps, axis=1)

    q_ref[...] = (rope(jnp.dot(h, w_ref[:, :nq], preferred_element_type=F32)) * ATT_HEAD_DIM ** -0.5).astype(BF16)
    k = rope(jnp.dot(h, w_ref[:, nq:nq + nk], preferred_element_type=F32))
    v = jnp.dot(h, w_ref[:, nq + nk:], preferred_element_type=F32)
    kt = k.T
    kt_ref[...] = kt.astype(BF16)
    v_ref[...] = v.astype(BF16)

    @pl.when(i < PROMPT_TILES)
    def _():
        kc_ref[...] = kt
        vc_ref[...] = v.T


def _rope_tables():
    f32 = np.float32
    pos = np.arange(DEC_SEQ)
    t_row = (pos // GRID_W).astype(f32)
    t_col = (pos % GRID_W).astype(f32)
    inv = f32(ROPE_BASE) ** (-np.arange(0, ROPE_HALF, 2, dtype=f32) / f32(ROPE_HALF))
    j = np.arange(LANES) % ATT_HEAD_DIM
    freq = inv[(j % ROPE_HALF) % (ROPE_HALF // 2)]
    ang = (np.where((j < ROPE_HALF)[None, :], t_row[:, None], t_col[:, None]) * freq[None, :]).astype(f32)
    sign = np.where((j % ROPE_HALF) < ROPE_HALF // 2, -1.0, 1.0).astype(f32)
    cos = np.concatenate([np.ones((TM, LANES), f32), np.cos(ang)], axis=0)
    sin = np.concatenate([np.zeros((TM, LANES), f32), np.sin(ang) * sign[None, :]], axis=0)
    return jnp.asarray(cos, F32), jnp.asarray(sin, F32)


def _at_in(x, mod, nw, w_in, cos, sin):
    tile = lambda i: (i, 0)
    fixed2 = lambda i: (0, 0)
    rope_tile = lambda i: (jnp.where(i < PROMPT_TILES, 0, 1 + (i - PROMPT_TILES) % TILES_PER_DEC_SEQ), 0)
    nk = ATT_KV_HEADS * ATT_HEAD_DIM
    return pl.pallas_call(
        _at_in_kernel,
        out_shape=(
            jax.ShapeDtypeStruct((N_TOK, D_MODEL), BF16),
            jax.ShapeDtypeStruct((nk, N_TOK), BF16),
            jax.ShapeDtypeStruct((N_TOK, nk), BF16),
            jax.ShapeDtypeStruct((N_PROMPT, nk), F32),
            jax.ShapeDtypeStruct((N_PROMPT, nk), F32),
        ),
        grid=(N_TILES,),
        in_specs=[
            pl.BlockSpec((TM, D_MODEL), tile),
            pl.BlockSpec(mod.shape, fixed2),
            pl.BlockSpec((1, D_MODEL), fixed2),
            pl.BlockSpec(w_in.shape, fixed2),
            pl.BlockSpec((TM, LANES), rope_tile),
            pl.BlockSpec((TM, LANES), rope_tile),
        ],
        out_specs=(pl.BlockSpec((TM, D_MODEL), tile), pl.BlockSpec((nk, TM), lambda i: (0, i)),
                   pl.BlockSpec((TM, nk), tile),
                   pl.BlockSpec((TM, nk), _prompt_tile), pl.BlockSpec((TM, nk), _prompt_tile)),
        compiler_params=_params(40, "arbitrary"),
        name="attn_in_proj",
    )(x, mod, nw, w_in, cos, sin)


def _attend(q, kt_all, v_all, mask, sink_ref, o_ref):
    nq = q.shape[0]
    group_lanes = ATT_GROUP * ATT_HEAD_DIM
    lane = lax.broadcasted_iota(jnp.int32, (nq, group_lanes), 1)
    mine = [(lane // ATT_HEAD_DIM) == g for g in range(ATT_GROUP)]
    row_head = lax.broadcasted_iota(jnp.int32, (ATT_GROUP * nq, 1), 0) // nq
    if mask is not None:
        mask = jnp.concatenate([mask] * ATT_GROUP, axis=0)
    for hk in range(ATT_KV_HEADS):
        vh = v_all[:, hk * ATT_HEAD_DIM:(hk + 1) * ATT_HEAD_DIM]
        kt = jnp.concatenate([kt_all[hk * ATT_HEAD_DIM:(hk + 1) * ATT_HEAD_DIM, :]] * ATT_GROUP, axis=0)
        vt = jnp.concatenate([vh] * ATT_GROUP, axis=1)
        qg = q[:, hk * group_lanes:(hk + 1) * group_lanes]
        q_stack = jnp.concatenate([jnp.where(mine[g], qg, jnp.zeros_like(qg)) for g in range(ATT_GROUP)], axis=0)
        s = jnp.dot(q_stack, kt, preferred_element_type=F32)
        if mask is not None:
            s = jnp.where(mask, s, -jnp.inf)
        sink = jnp.zeros((ATT_GROUP * nq, 1), F32)
        for g in range(ATT_GROUP):
            sink = jnp.where(row_head == g, sink_ref[hk * ATT_GROUP + g], sink)
        m = jnp.maximum(jnp.max(s, axis=-1, keepdims=True), sink)
        p = jnp.exp(s - m)
        denom = jnp.sum(p, axis=-1, keepdims=True) + jnp.exp(sink - m)
        o = jnp.dot(p.astype(BF16), vt, preferred_element_type=F32) / denom
        acc = jnp.where(mine[0], o[:nq], 0.0)
        for g in range(1, ATT_GROUP):
            acc = acc + jnp.where(mine[g], o[g * nq:(g + 1) * nq], 0.0)
        o_ref[:, hk * group_lanes:(hk + 1) * group_lanes] = acc.astype(BF16)


def _ctx_attn_kernel(sink_ref, q_ref, k_ref, v_ref, o_ref):
    _attend(q_ref[...], k_ref[...], v_ref[...], None, sink_ref, o_ref)


def _lat_attn_kernel(sink_ref, q_ref, kp_ref, kc_ref, kn_ref, vp_ref, vc_ref, vn_ref, ck_ref, cv_ref, o_ref):
    jb = pl.program_id(1)
    kt_all = jnp.concatenate([kp_ref[...], kc_ref[...], kn_ref[...], ck_ref[0].astype(BF16)], axis=1)
    v_all = jnp.concatenate([vp_ref[...], vc_ref[...], vn_ref[...], cv_ref[0].astype(BF16)], axis=0)
    nkeys = 3 * BLOCK + PAST_LEN
    qi = lax.broadcasted_iota(jnp.int32, (BLOCK, nkeys), 0)
    kj = lax.broadcasted_iota(jnp.int32, (BLOCK, nkeys), 1)
    qpos = jb * BLOCK + qi
    kpos = (jb - 1) * BLOCK + kj
    local_ok = (jnp.abs(qpos - kpos) <= WINDOW) & (kpos >= 0) & (kpos < DEC_SEQ)
    mask = (kj >= 3 * BLOCK) | local_ok
    _attend(q_ref[...], kt_all, v_all, mask, sink_ref, o_ref)


def _attention(q, kt, v, cache_kt, cache_v, sink):
    nk = ATT_KV_HEADS * ATT_HEAD_DIM
    smem = pl.BlockSpec(memory_space=pltpu.SMEM)
    ctx = pl.pallas_call(
        _ctx_attn_kernel,
        out_shape=jax.ShapeDtypeStruct((N_PROMPT, D_MODEL), BF16),
        grid=(BATCH,),
        in_specs=[
            smem,
            pl.BlockSpec((SEQ, D_MODEL), lambda b: (b, 0)),
            pl.BlockSpec((nk, SEQ), lambda b: (0, b)),
            pl.BlockSpec((SEQ, nk), lambda b: (b, 0)),
        ],
        out_specs=pl.BlockSpec((SEQ, D_MODEL), lambda b: (b, 0)),
        compiler_params=_params(40, "arbitrary"),
        name="context_attention",
    )(sink, q, kt, v)

    nb = DEC_SEQ // BLOCK
    base = N_PROMPT // BLOCK
    cur = lambda b, j: (base + b * nb + j, 0)
    prev = lambda b, j: (base + b * nb + jnp.maximum(j - 1, 0), 0)
    nxt = lambda b, j: (base + b * nb + jnp.minimum(j + 1, nb - 1), 0)
    kv_blk = lambda f: pl.BlockSpec((BLOCK, nk), f)
    kt_blk = lambda f: pl.BlockSpec((nk, BLOCK), lambda b, j: f(b, j)[::-1])
    cache_blk = pl.BlockSpec((1, PAST_LEN, nk), lambda b, j: (b, 0, 0))
    cache_kt_blk = pl.BlockSpec((1, nk, PAST_LEN), lambda b, j: (b, 0, 0))
    lat = pl.pallas_call(
        _lat_attn_kernel,
        out_shape=jax.ShapeDtypeStruct((DEC_BATCH * DEC_SEQ, D_MODEL), BF16),
        grid=(DEC_BATCH, nb),
        in_specs=[
            smem,
            pl.BlockSpec((BLOCK, D_MODEL), cur),
            kt_blk(prev), kt_blk(cur), kt_blk(nxt),
            kv_blk(prev), kv_blk(cur), kv_blk(nxt),
            cache_kt_blk, cache_blk,
        ],
        out_specs=pl.BlockSpec((BLOCK, D_MODEL), lambda b, j: (b * nb + j, 0)),
        compiler_params=_params(40, "arbitrary", "arbitrary"),
        name="latent_attention",
    )(sink, q, kt, kt, kt, v, v, v, cache_kt, cache_v)
    return ctx, lat


def kernel(x_prompt, x_sample, state_hgrn, cache_k, cache_v, c, c_ctx, ada_w, ada_b, norm_w, hg_w_in,
           hg_lb_logits, hg_onorm, hg_w_out, at_w_in, at_sink, at_w_out, moe_w_group, moe_b_group,
           moe_w_expert, moe_b_expert, moe_w1, moe_w3, moe_w2, final_norm):
    xp = x_prompt.reshape(N_PROMPT, D_MODEL)
    xs = x_sample.reshape(N_TOK - N_PROMPT, D_MODEL)
    cond = jnp.concatenate([c_ctx[None, :], c], axis=0)
    mod = _ada(cond, ada_w, ada_b)
    nk = ATT_KV_HEADS * ATT_HEAD_DIM

    def router_params(i):
        pad = jnp.zeros((D_MODEL, ROUTER_LANES - N_EXPERTS - N_GROUPS), F32)
        w = jnp.concatenate([moe_w_expert[i], moe_w_group[i], pad], axis=1)
        b = jnp.concatenate([moe_b_expert[i], moe_b_group[i], pad[0]])[None, :]
        hi = w.astype(BF16)
        lo = (w - hi.astype(F32)).astype(BF16)
        return jnp.stack([hi, lo]), b

    tile = lambda i: (i, 0)
    blk = pl.BlockSpec((TM, D_MODEL), tile)

    prompt_blk = pl.BlockSpec((TM, D_MODEL), _prompt_tile)
    sample_blk = pl.BlockSpec((TM, D_MODEL), _sample_tile)
    q, v, g, lff, kf, lfb, kb = _hg_in(xp, xs, mod[0], norm_w[0, 0][None, :], hg_w_in[0].astype(BF16), hg_lb_logits)
    o_f, o_b, state_new = _gla(q, v, lff, kf, lfb, kb, state_hgrn)
    wr, br = router_params(0)
    routed = _mixer_out(
        _hg_out_kernel, "hgrn_out_route", (o_f, o_b, g, hg_onorm[0][None, :], xp, xs),
        (blk, blk, blk, pl.BlockSpec((1, HG_DK), lambda i: (0, 0)), prompt_blk, sample_blk),
        mod[0], norm_w[0, 1][None, :], hg_w_out[0].astype(BF16), wr, br)
    x = _moe_layer(*routed, mod[0], 0, moe_w1, moe_w3, moe_w2)

    cos, sin = _rope_tables()
    qa, ka, va, k_ctx, v_ctx = _at_in(x, mod[1], norm_w[1, 0][None, :], at_w_in[0].astype(BF16), cos, sin)
    attn_ctx, attn_lat = _attention(qa, ka, va, cache_k[:, 0].reshape(DEC_BATCH, PAST_LEN, nk).transpose(0, 2, 1),
                                    cache_v[:, 0].reshape(DEC_BATCH, PAST_LEN, nk), at_sink[0])
    wr, br = router_params(1)
    routed = _mixer_out(
        _at_out_kernel, "attn_out_route", (attn_ctx, attn_lat, x), (prompt_blk, sample_blk, blk),
        mod[1], norm_w[1, 1][None, :], at_w_out[0].astype(BF16), wr, br)
    y_prompt, y_sample = _moe_layer(*routed, mod[1], 1, moe_w1, moe_w3, moe_w2, final_norm[None, :])

    def cache(feature_major):
        return feature_major.reshape(BATCH, 1, ATT_KV_HEADS, ATT_HEAD_DIM, SEQ).transpose(0, 1, 4, 2, 3)

    return (y_prompt.reshape(BATCH, SEQ, D_MODEL), y_sample.reshape(DEC_BATCH, DEC_SEQ, D_MODEL), state_new,
            cache(k_ctx), cache(v_ctx))
```

```python
import functools

import jax
import jax.numpy as jnp
import numpy as np
from jax import lax
from jax.experimental import pallas as pl
from jax.experimental.pallas import tpu as pltpu
from jax.experimental.pallas import tpu_sc as plsc

F32 = jnp.float32
BF16 = jnp.bfloat16

D_MODEL = 1024
BATCH = 16
SEQ = 256
DEC_BATCH = 2
DEC_SEQ = 1024
PAST_LEN = 512
GRID_W = 64
HG_HEADS = 8
HG_DK = 128
CHUNK = 16
ATT_HEAD_DIM = 64
ATT_Q_HEADS = 16
ATT_KV_HEADS = 4
ATT_GROUP = 4
WINDOW = 128
BLOCK = 128
ROPE_HALF = 32
ROPE_BASE = 10000.0
N_GROUPS = 4
EXPERTS_PER_GROUP = 8
N_EXPERTS = 32
D_EXPERT = 256
EPS = 1e-6

N_PROMPT = BATCH * SEQ
N_TOK = N_PROMPT + DEC_BATCH * DEC_SEQ
TM = 256
N_TILES = N_TOK // TM
PROMPT_TILES = N_PROMPT // TM
TILES_PER_DEC_SEQ = DEC_SEQ // TM
LANES = 128
N_COND_USED = 1 + DEC_BATCH
N_COND = 8
ROUTER_LANES = 128
GLA_UNROLL = 8
DECAY_CLAMP = 60.0
PACK_WORDS = D_MODEL // 4
MOE_TILE = 256
MOE_ROWS = 2 * N_TOK + N_EXPERTS * MOE_TILE
MIB = 1024 * 1024


def _params(vmem_mib, *semantics):
    return pltpu.CompilerParams(dimension_semantics=semantics, vmem_limit_bytes=vmem_mib * MIB)


def _tile_cond(i):
    return jnp.where(i < PROMPT_TILES, 0, 1 + (i - PROMPT_TILES) // TILES_PER_DEC_SEQ)


def _prompt_tile(i):
    return (jnp.minimum(i, PROMPT_TILES - 1), 0)


def _sample_tile(i):
    return (jnp.maximum(i - PROMPT_TILES, 0), 0)


def _stream_tile(prompt_ref, sample_ref):
    return jnp.where(pl.program_id(0) < PROMPT_TILES, prompt_ref[...], sample_ref[...])


def _mod_row(mod_ref, cond, which):
    return mod_ref[pl.ds(cond, 1), which * D_MODEL:(which + 1) * D_MODEL]


def _norm_mod(x, nw, shift, scale):
    y = x * lax.rsqrt(jnp.mean(x * x, axis=-1, keepdims=True) + EPS)
    return (y * nw) * (1.0 + scale) + shift


def _silu(x):
    return x * jax.nn.sigmoid(x)


def _ada_kernel(c_ref, w_ref, b_ref, o_ref):
    s = [_silu(c_ref[r]) for r in range(N_COND_USED)]
    tn = w_ref.shape[-1]
    rows = []
    for r in range(N_COND_USED):
        cols = [jnp.sum(w_ref[0, :, j * LANES:(j + 1) * LANES] * s[r], axis=0, keepdims=True)
                for j in range(tn // LANES)]
        rows.append(jnp.concatenate(cols, axis=1) + b_ref[0])
    rows.append(jnp.zeros((N_COND - N_COND_USED, tn), F32))
    o_ref[0] = jnp.concatenate(rows, axis=0)


def _ada(cond, ada_w, ada_b):
    depth, _, n = ada_w.shape
    tn = 1536
    cond_cols = jnp.broadcast_to(cond[:, :, None], (N_COND_USED, D_MODEL, LANES))
    return pl.pallas_call(
        _ada_kernel,
        out_shape=jax.ShapeDtypeStruct((depth, N_COND, n), F32),
        grid=(depth, n // tn),
        in_specs=[
            pl.BlockSpec((N_COND_USED, D_MODEL, LANES), lambda l, j: (0, 0, 0)),
            pl.BlockSpec((1, D_MODEL, tn), lambda l, j: (l, 0, j)),
            pl.BlockSpec((1, 1, tn), lambda l, j: (l, 0, j)),
        ],
        out_specs=pl.BlockSpec((1, N_COND, tn), lambda l, j: (l, 0, j)),
        compiler_params=_params(40, "arbitrary", "arbitrary"),
        name="ada_modulation",
    )(cond_cols, ada_w, ada_b.reshape(depth, 1, n))


def _hg_in_kernel(xp_ref, xs_ref, mod_ref, nw_ref, w_ref, lbl_ref,
                  q_ref, v_ref, g_ref, lff_ref, kf_ref, lfb_ref, kb_ref):
    cond = _tile_cond(pl.program_id(0))
    x = _stream_tile(xp_ref, xs_ref)
    h = _norm_mod(x, nw_ref[...], _mod_row(mod_ref, cond, 0), _mod_row(mod_ref, cond, 1)).astype(BF16)

    def proj(c):
        return jnp.dot(h, w_ref[:, c * D_MODEL:(c + 1) * D_MODEL], preferred_element_type=F32)

    l0, l1, l2 = lbl_ref[0], lbl_ref[1], lbl_ref[2]
    m = jnp.maximum(jnp.maximum(l0, l1), l2)
    e0, e1, e2 = jnp.exp(l0 - m), jnp.exp(l1 - m), jnp.exp(l2 - m)
    lb = e0 / (e0 + e1 + e2)

    q_ref[...] = proj(0).astype(BF16)
    v_ref[...] = proj(1).astype(BF16)
    for d, (lf_ref, k_ref) in enumerate(((lff_ref, kf_ref), (lfb_ref, kb_ref))):
        lbd = lb[d:d + 1, :]
        f = lbd + (1.0 - lbd) * jax.nn.sigmoid(proj(2 + d))
        lf = jnp.log(f)
        hi = lf.astype(BF16)
        lf_ref[0] = hi
        lf_ref[1] = (lf - hi.astype(F32)).astype(BF16)
        k_ref[...] = (1.0 - f).astype(BF16)
    g_ref[...] = proj(4).astype(BF16)


def _hg_in(xp, xs, mod, nw, w_in, lb_logits):
    tile = lambda i: (i, 0)
    fixed2 = lambda i: (0, 0)
    bf = jax.ShapeDtypeStruct((N_TOK, D_MODEL), BF16)
    ff = jax.ShapeDtypeStruct((2, N_TOK, D_MODEL), BF16)
    blk = pl.BlockSpec((TM, D_MODEL), tile)
    split_blk = pl.BlockSpec((2, TM, D_MODEL), lambda i: (0, i, 0))
    return pl.pallas_call(
        _hg_in_kernel,
        out_shape=(bf, bf, bf, ff, bf, ff, bf),
        grid=(N_TILES,),
        in_specs=[
            pl.BlockSpec((TM, D_MODEL), _prompt_tile),
            pl.BlockSpec((TM, D_MODEL), _sample_tile),
            pl.BlockSpec(mod.shape, fixed2),
            pl.BlockSpec((1, D_MODEL), fixed2),
            pl.BlockSpec(w_in.shape, fixed2),
            pl.BlockSpec(lb_logits.shape, lambda i: (0, 0, 0)),
        ],
        out_specs=(blk, blk, blk, split_blk, blk, split_blk, blk),
        compiler_params=_params(56, "arbitrary"),
        name="hgrn_in_proj",
    )(xp, xs, mod, nw, w_in, lb_logits)


def _gla_exact(reverse, q_ref, v_ref, lf_ref, k_ref, st_ref, o_ref):
    nt = (((1,), (1,)), ((), ()))
    tn = (((0,), (0,)), ((), ()))
    heads = [slice(h * HG_DK, (h + 1) * HG_DK) for h in range(HG_HEADS)]
    n_chunks = TM // CHUNK
    row_id = lax.broadcasted_iota(jnp.int32, (CHUNK, HG_DK), 0)
    order = range(CHUNK - 1, -1, -1) if reverse else range(CHUNK)

    def chunk(step, carry):
        ci = (n_chunks - 1 - step) if reverse else step
        rows = pl.ds(pl.multiple_of(ci * CHUNK, CHUNK), CHUNK)
        f = jnp.exp(lf_ref[0, rows, :].astype(F32) + lf_ref[1, rows, :].astype(F32))
        for h, cols in enumerate(heads):
            q, k, v, fh = q_ref[rows, cols], k_ref[rows, cols], v_ref[rows, cols], f[:, cols]
            st = st_ref[0, h]
            o = jnp.zeros((CHUNK, HG_DK), F32)
            for t in order:
                one = row_id == t
                v_t = jnp.where(one, v, jnp.zeros_like(v))
                st = st * fh[t:t + 1, :] + lax.dot_general(v_t, k, tn, preferred_element_type=F32)
                o = jnp.where(one, lax.dot_general(q, st.astype(BF16), nt, preferred_element_type=F32), o)
            st_ref[0, h] = st
            o_ref[rows, cols] = o
        return carry

    lax.fori_loop(0, n_chunks, chunk, 0)


def _gla_direction(reverse, q_ref, v_ref, lf_ref, k_ref, st_ref, sw_ref, o_ref,
                   ut_scr, qd_scr, kd_scr, ki_scr, cd_scr):
    r = lax.broadcasted_iota(jnp.int32, (TM, TM), 0)
    c = lax.broadcasted_iota(jnp.int32, (TM, TM), 1)
    same = (r // CHUNK) == (c // CHUNK)
    tri = (same & ((c >= r) if reverse else (c <= r))).astype(BF16)
    b = jnp.dot(tri, lf_ref[0], preferred_element_type=F32) + jnp.dot(tri, lf_ref[1], preferred_element_type=F32)
    b3 = b.reshape(TM // CHUNK, CHUNK, D_MODEL)
    edge = 0 if reverse else CHUNK - 1
    tot = jnp.broadcast_to(b3[:, edge:edge + 1, :], b3.shape).reshape(TM, D_MODEL)
    k = k_ref[...]
    qd_scr[...] = q_ref[...] * jnp.exp(b).astype(BF16)
    kd_scr[...] = k * jnp.exp(tot - b).astype(BF16)
    ki_scr[...] = k * jnp.exp(jnp.minimum(-b, DECAY_CLAMP)).astype(BF16)
    cd_scr[...] = jnp.exp(tot)

    tr = lax.broadcasted_iota(jnp.int32, (CHUNK, CHUNK), 0)
    tc = lax.broadcasted_iota(jnp.int32, (CHUNK, CHUNK), 1)
    keep = (tc >= tr) if reverse else (tc <= tr)
    nt = (((1,), (1,)), ((), ()))
    tn = (((0,), (0,)), ((), ()))

    decay_ok = jnp.max(-b) <= DECAY_CLAMP

    @pl.when(jnp.logical_not(decay_ok))
    def _():
        _gla_exact(reverse, q_ref, v_ref, lf_ref, k_ref, st_ref, o_ref)

    @pl.when(decay_ok)
    def _():
        heads = [slice(h * HG_DK, (h + 1) * HG_DK) for h in range(HG_HEADS)]
        for h in range(HG_HEADS):
            sw_ref[0, h] = st_ref[0, h].T.astype(BF16)

        n_chunks = TM // CHUNK

        def chunk_rows(step):
            ci = (n_chunks - 1 - step) if reverse else step
            row0 = pl.multiple_of(ci * CHUNK, CHUNK)
            return row0, pl.ds(row0, CHUNK)

        def key_value_product(rows):
            for h, cols in enumerate(heads):
                ut_scr[h] = lax.dot_general(v_ref[rows, cols], kd_scr[rows, cols], tn, preferred_element_type=F32)

        key_value_product(chunk_rows(0)[1])

        def chunk(step, src, dst):
            row0, rows = chunk_rows(step)
            cd_row = cd_scr[pl.ds(row0, 1), :]
            for h, cols in enumerate(heads):
                new = st_ref[src, h] * cd_row[:, cols] + ut_scr[h]
                st_ref[dst, h] = new
                sw_ref[dst, h] = new.T.astype(BF16)
            a = [lax.dot_general(qd_scr[rows, cols], ki_scr[rows, cols], nt, preferred_element_type=F32)
                 for cols in heads]
            inter = [jnp.dot(qd_scr[rows, cols], sw_ref[src, h], preferred_element_type=F32)
                     for h, cols in enumerate(heads)]
            key_value_product(chunk_rows(jnp.minimum(step + 1, n_chunks - 1))[1])
            for h, cols in enumerate(heads):
                am = jnp.where(keep, a[h], 0.0).astype(BF16)
                o_ref[rows, cols] = jnp.dot(am, v_ref[rows, cols], preferred_element_type=F32) + inter[h]

        def chunk_group(it, carry):
            for p in range(GLA_UNROLL):
                chunk(GLA_UNROLL * it + p, p % 2, 1 - p % 2)
            return carry

        lax.fori_loop(0, n_chunks // GLA_UNROLL, chunk_group, 0)


def _gla_kernel(qf_ref, vf_ref, lff_ref, kf_ref, qb_ref, vb_ref, lfb_ref, kb_ref, s0_ref,
                of_ref, ob_ref, sout_ref, stf_scr, stb_scr, sw_scr, ut_scr, qd_scr, kd_scr, ki_scr, cd_scr):
    i = pl.program_id(0)
    is_prompt = i < PROMPT_TILES
    first = jnp.logical_or(is_prompt, (i - PROMPT_TILES) % TILES_PER_DEC_SEQ == 0)

    @pl.when(first)
    def _():
        keep0 = jnp.where(is_prompt, 0.0, 1.0)
        for h in range(HG_HEADS):
            stf_scr[0, h] = s0_ref[0, 0, 0, h].T * keep0
            stb_scr[0, h] = s0_ref[0, 0, 1, h].T * keep0

    _gla_direction(False, qf_ref, vf_ref, lff_ref, kf_ref, stf_scr, sw_scr, of_ref,
                   ut_scr, qd_scr, kd_scr, ki_scr, cd_scr)
    _gla_direction(True, qb_ref, vb_ref, lfb_ref, kb_ref, stb_scr, sw_scr, ob_ref,
                   ut_scr, qd_scr, kd_scr, ki_scr, cd_scr)

    @pl.when(is_prompt)
    def _():
        for h in range(HG_HEADS):
            sout_ref[0, 0, 0, h] = stf_scr[0, h].T
            sout_ref[0, 0, 1, h] = stb_scr[0, h].T


def _gla(q, v, lff, kf, lfb, kb, state_hgrn):
    def fwd_tile(i):
        return (i, 0)

    def bwd_tile(i):
        j = (i - PROMPT_TILES) % TILES_PER_DEC_SEQ
        return (jnp.where(i < PROMPT_TILES, i, i - j + (TILES_PER_DEC_SEQ - 1 - j)), 0)

    def s0_idx(i):
        return (jnp.maximum(i - PROMPT_TILES, 0) // TILES_PER_DEC_SEQ, 0, 0, 0, 0, 0)

    def sout_idx(i):
        return (jnp.minimum(i, PROMPT_TILES - 1), 0, 0, 0, 0, 0)

    f_blk = pl.BlockSpec((TM, D_MODEL), fwd_tile)
    b_blk = pl.BlockSpec((TM, D_MODEL), bwd_tile)
    f_split = pl.BlockSpec((2, TM, D_MODEL), lambda i: (0,) + fwd_tile(i))
    b_split = pl.BlockSpec((2, TM, D_MODEL), lambda i: (0,) + bwd_tile(i))
    st_blk = (1, 1, 2, HG_HEADS, HG_DK, HG_DK)
    return pl.pallas_call(
        _gla_kernel,
        out_shape=(
            jax.ShapeDtypeStruct((N_TOK, D_MODEL), F32),
            jax.ShapeDtypeStruct((N_TOK, D_MODEL), F32),
            jax.ShapeDtypeStruct((BATCH,) + st_blk[1:], F32),
        ),
        grid=(N_TILES,),
        in_specs=[f_blk, f_blk, f_split, f_blk, b_blk, b_blk, b_split, b_blk, pl.BlockSpec(st_blk, s0_idx)],
        out_specs=(f_blk, b_blk, pl.BlockSpec(st_blk, sout_idx)),
        scratch_shapes=[
            pltpu.VMEM((2, HG_HEADS, HG_DK, HG_DK), F32),
            pltpu.VMEM((2, HG_HEADS, HG_DK, HG_DK), F32),
            pltpu.VMEM((2, HG_HEADS, HG_DK, HG_DK), BF16),
            pltpu.VMEM((HG_HEADS, HG_DK, HG_DK), F32),
            pltpu.VMEM((TM, D_MODEL), BF16),
            pltpu.VMEM((TM, D_MODEL), BF16),
            pltpu.VMEM((TM, D_MODEL), BF16),
            pltpu.VMEM((TM, D_MODEL), F32),
        ],
        compiler_params=_params(48, "arbitrary"),
        name="hgrn_recurrence",
    )(q, v, lff, kf, q, v, lfb, kb, state_hgrn)


def _route(logits):
    lane = lax.broadcasted_iota(jnp.int32, logits.shape, 1)
    neg = jnp.float32(-jnp.inf)
    is_group = (lane >= N_EXPERTS) & (lane < N_EXPERTS + N_GROUPS)
    gl = jnp.where(is_group, logits, neg)
    gmax = jnp.max(gl, axis=-1, keepdims=True)
    g_sel = jnp.argmax(gl, axis=-1, keepdims=True).astype(jnp.int32) - N_EXPERTS
    gsum = jnp.sum(jnp.exp(gl - gmax), axis=-1, keepdims=True)
    p_g = 1.0 / gsum
    in_sel = (lane < N_EXPERTS) & ((lane // EXPERTS_PER_GROUP) == g_sel)
    el = jnp.where(in_sel, logits, neg)
    m1 = jnp.max(el, axis=-1, keepdims=True)
    i1 = jnp.argmax(el, axis=-1, keepdims=True).astype(jnp.int32)
    el2 = jnp.where(lane == i1, neg, el)
    m2 = jnp.max(el2, axis=-1, keepdims=True)
    i2 = jnp.argmax(el2, axis=-1, keepdims=True).astype(jnp.int32)
    e2 = jnp.exp(m2 - m1)
    return i1, i2, p_g / (1.0 + e2), p_g * e2 / (1.0 + e2)


def _pack_rows(x):
    q = PACK_WORDS
    bits = pltpu.bitcast(x.astype(BF16).astype(F32), jnp.uint32)
    return [(bits[:, (2 + h) * q:(3 + h) * q] & jnp.uint32(0xFFFF0000)) | (bits[:, h * q:(h + 1) * q] >> 16)
            for h in range(2)]


def _unpack_rows(half0, half1):
    lo = lambda w: pltpu.bitcast(w << 16, F32).astype(BF16)
    hi = lambda w: pltpu.bitcast(w & jnp.uint32(0xFFFF0000), F32).astype(BF16)
    return [lo(half0), lo(half1), hi(half0), hi(half1)]


def _mixer_tail(mix_bf16, x, mod_ref, nw2_ref, wo_ref, wr_ref, br_ref,
                x1_ref, h2p_ref, ri_ref, rw_ref, cnt_ref, carry_scr):
    i = pl.program_id(0)
    cond = _tile_cond(i)
    out = jnp.dot(mix_bf16, wo_ref[...], preferred_element_type=F32)
    x1 = x + _mod_row(mod_ref, cond, 2) * out
    x1_ref[...] = x1
    h2 = _norm_mod(x1, nw2_ref[...], _mod_row(mod_ref, cond, 3), _mod_row(mod_ref, cond, 4))
    for h, words in enumerate(_pack_rows(h2)):
        h2p_ref[h] = words
    h_hi = h2.astype(BF16)
    h_lo = (h2 - h_hi.astype(F32)).astype(BF16)
    logits = (jnp.dot(h_hi, wr_ref[0], preferred_element_type=F32)
              + (jnp.dot(h_hi, wr_ref[1], preferred_element_type=F32)
                 + jnp.dot(h_lo, wr_ref[0], preferred_element_type=F32))) + br_ref[...]
    i1, i2, w1, w2 = _route(logits)

    @pl.when(i == 0)
    def _():
        carry_scr[...] = jnp.zeros_like(carry_scr)

    lane = lax.broadcasted_iota(jnp.int32, logits.shape, 1)
    chosen = ((lane == i1) | (lane == i2)).astype(BF16)
    r = lax.broadcasted_iota(jnp.int32, (TM, TM), 0)
    c = lax.broadcasted_iota(jnp.int32, (TM, TM), 1)
    before = jnp.dot((c < r).astype(BF16), chosen, preferred_element_type=F32) + carry_scr[...]
    r1 = jnp.sum(jnp.where(lane == i1, before, 0.0), axis=-1, keepdims=True).astype(jnp.int32)
    r2 = jnp.sum(jnp.where(lane == i2, before, 0.0), axis=-1, keepdims=True).astype(jnp.int32)
    total = carry_scr[...] + jnp.sum(chosen.astype(F32), axis=0, keepdims=True)
    carry_scr[...] = total
    cnt_ref[...] = total
    ri_ref[...] = jnp.where(lane == 0, i1, jnp.where(lane == 1, i2, jnp.where(lane == 2, r1, r2)))
    rw_ref[...] = jnp.where(lane == 0, w1, w2)


def _hg_out_kernel(of_ref, ob_ref, g_ref, on_ref, xp_ref, xs_ref, *rest):
    o = of_ref[...] + ob_ref[...]
    parts = []
    for h in range(HG_HEADS):
        oh = o[:, h * HG_DK:(h + 1) * HG_DK]
        parts.append(oh * lax.rsqrt(jnp.mean(oh * oh, axis=-1, keepdims=True) + EPS) * on_ref[...])
    y = jnp.concatenate(parts, axis=1) * _silu(g_ref[...].astype(F32))
    _mixer_tail(y.astype(BF16), _stream_tile(xp_ref, xs_ref), *rest)


def _at_out_kernel(ac_ref, al_ref, x_ref, *rest):
    _mixer_tail(_stream_tile(ac_ref, al_ref), x_ref[...], *rest)


def _mixer_out(kernel_fn, name, mix_inputs, mix_specs, mod, nw2, w_out, w_router, b_router):
    tile = lambda i: (i, 0)
    fixed2 = lambda i: (0, 0)
    blk = pl.BlockSpec((TM, D_MODEL), tile)
    lanes_blk = pl.BlockSpec((TM, ROUTER_LANES), tile)
    return pl.pallas_call(
        kernel_fn,
        out_shape=(
            jax.ShapeDtypeStruct((N_TOK, D_MODEL), F32),
            jax.ShapeDtypeStruct((2, N_TOK, PACK_WORDS), jnp.uint32),
            jax.ShapeDtypeStruct((N_TOK, ROUTER_LANES), jnp.int32),
            jax.ShapeDtypeStruct((N_TOK, ROUTER_LANES), F32),
            jax.ShapeDtypeStruct((1, ROUTER_LANES), F32),
        ),
        grid=(N_TILES,),
        in_specs=list(mix_specs) + [
            pl.BlockSpec(mod.shape, fixed2),
            pl.BlockSpec((1, D_MODEL), fixed2),
            pl.BlockSpec(w_out.shape, fixed2),
            pl.BlockSpec(w_router.shape, lambda i: (0, 0, 0)),
            pl.BlockSpec((1, ROUTER_LANES), fixed2),
        ],
        out_specs=(blk, pl.BlockSpec((2, TM, PACK_WORDS), lambda i: (0, i, 0)), lanes_blk, lanes_blk,
                   pl.BlockSpec((1, ROUTER_LANES), fixed2)),
        scratch_shapes=[pltpu.VMEM((1, ROUTER_LANES), F32)],
        compiler_params=_params(40, "arbitrary"),
        name=name,
    )(*mix_inputs, mod, nw2, w_out, w_router, b_router)


def _moe_plan(route_i, counts):
    cnt = counts[0, :N_EXPERTS].astype(jnp.int32)
    padded = ((cnt + MOE_TILE - 1) // MOE_TILE) * MOE_TILE
    ends = jnp.cumsum(padded)
    offs = ends - padded
    experts = route_i[:, 0:2]
    pos = jnp.sum(jnp.where(experts[:, :, None] == jnp.arange(N_EXPERTS)[None, None, :], offs[None, None, :], 0),
                  axis=-1) + route_i[:, 2:4]
    tile_start = jnp.arange(MOE_ROWS // MOE_TILE, dtype=jnp.int32) * MOE_TILE
    tile_expert = jnp.minimum(jnp.sum(ends[None, :] <= tile_start[:, None], axis=1), N_EXPERTS - 1).astype(jnp.int32)
    of_tile = tile_expert[:, None] == jnp.arange(N_EXPERTS)[None, :]
    tile_offs = jnp.sum(jnp.where(of_tile, offs[None, :], 0), axis=1)
    tile_cnt = jnp.sum(jnp.where(of_tile, cnt[None, :], 0), axis=1)
    tile_rows = jnp.clip(tile_offs + tile_cnt - tile_start, 0, MOE_TILE).astype(jnp.int32)
    tile_first = (tile_start == tile_offs).astype(jnp.int32)
    n_active = (ends[-1] // MOE_TILE).astype(jnp.int32).reshape(1)
    return pos.astype(jnp.int32), tile_expert, tile_rows, tile_first, n_active


SC_WINDOW = 128


def _sc_mesh():
    return plsc.VectorSubcoreMesh(core_axis_name="c", subcore_axis_name="s")


def _sc_scatter_rows(x, idx_a, idx_b, n_out_rows):
    n = x.shape[0]

    @functools.partial(pl.kernel, out_type=jax.ShapeDtypeStruct((n_out_rows, PACK_WORDS), x.dtype), mesh=_sc_mesh(),
                       scratch_types=[pltpu.SemaphoreType.DMA, pltpu.SemaphoreType.DMA])
    def scatter(x_hbm, ia_hbm, ib_hbm, o_hbm, sem_a, sem_b):
        def body(x_vmem, ia_vmem, ib_vmem):
            copy_a = pltpu.async_copy(x_vmem, o_hbm.at[ia_vmem.at[0]], sem_a)
            copy_b = pltpu.async_copy(x_vmem, o_hbm.at[ib_vmem.at[0]], sem_b)
            copy_a.wait()
            copy_b.wait()

        idx_spec = pl.BlockSpec((1, SC_WINDOW), index_map=lambda i: (0, i))
        pltpu.emit_pipeline(
            body, grid=(n // SC_WINDOW,),
            in_specs=[pl.BlockSpec((SC_WINDOW, PACK_WORDS), index_map=lambda i: (i, 0)), idx_spec, idx_spec],
            out_specs=[],
            core_axis_name=("c", "s"), dimension_semantics=(pltpu.PARALLEL,),
        )(x_hbm, ia_hbm, ib_hbm)

    return scatter(x, idx_a.reshape(1, n), idx_b.reshape(1, n))


def _sc_gather_rows(table, idx):
    n = idx.shape[0]

    @functools.partial(pl.kernel, out_type=jax.ShapeDtypeStruct((n, PACK_WORDS), table.dtype), mesh=_sc_mesh())
    def gather(t_hbm, i_hbm, o_hbm):
        def body(i_vmem, o_vmem):
            pltpu.sync_copy(t_hbm.at[i_vmem.at[0]], o_vmem)

        pltpu.emit_pipeline(
            body, grid=(n // SC_WINDOW,),
            in_specs=[pl.BlockSpec((1, SC_WINDOW), index_map=lambda i: (0, i))],
            out_specs=[pl.BlockSpec((SC_WINDOW, PACK_WORDS), index_map=lambda i: (i, 0))],
            core_axis_name=("c", "s"), dimension_semantics=(pltpu.PARALLEL,),
        )(i_hbm, o_hbm)

    return gather(table, idx.reshape(1, n))


def _ffn_kernel(te_ref, tr_ref, tf_ref, na_ref, xs_ref, w1_ref, w3_ref, w2_ref, ys_ref, w1_scr, w3_scr, w2_scr):
    t = pl.program_id(0)

    @pl.when(t < na_ref[0])
    def _():
        @pl.when(tf_ref[t] == 1)
        def _():
            w1_scr[...] = w1_ref[0, 0].astype(BF16)
            w3_scr[...] = w3_ref[0, 0].astype(BF16)
            w2_scr[...] = w2_ref[0, 0].astype(BF16)

        row = lax.broadcasted_iota(jnp.int32, (MOE_TILE, PACK_WORDS), 0)
        live = row < tr_ref[t]
        halves = [jnp.where(live, xs_ref[h], jnp.zeros((MOE_TILE, PACK_WORDS), jnp.uint32)) for h in range(2)]
        chunks = _unpack_rows(*halves)

        def up(w_scr):
            acc = jnp.dot(chunks[0], w_scr[:PACK_WORDS, :], preferred_element_type=F32)
            for k in range(1, len(chunks)):
                acc = acc + jnp.dot(chunks[k], w_scr[k * PACK_WORDS:(k + 1) * PACK_WORDS, :],
                                    preferred_element_type=F32)
            return acc

        hid = (_silu(up(w1_scr)) * up(w3_scr)).astype(BF16)
        for h, words in enumerate(_pack_rows(jnp.dot(hid, w2_scr[...], preferred_element_type=F32))):
            ys_ref[h] = words


def _ffn(xs, tile_expert, tile_rows, tile_first, n_active, layer, w1, w3, w2):
    row_tile = lambda t, te, tr, tf, na: (0, jnp.minimum(t, na[0] - 1), 0)
    expert = lambda t, te, tr, tf, na: (layer, te[t], 0, 0)
    return pl.pallas_call(
        _ffn_kernel,
        out_shape=jax.ShapeDtypeStruct((2, MOE_ROWS, PACK_WORDS), jnp.uint32),
        grid_spec=pltpu.PrefetchScalarGridSpec(
            num_scalar_prefetch=4,
            grid=(MOE_ROWS // MOE_TILE,),
            in_specs=[
                pl.BlockSpec((2, MOE_TILE, PACK_WORDS), row_tile),
                pl.BlockSpec((1, 1, D_MODEL, D_EXPERT), expert),
                pl.BlockSpec((1, 1, D_MODEL, D_EXPERT), expert),
                pl.BlockSpec((1, 1, D_EXPERT, D_MODEL), expert),
            ],
            out_specs=pl.BlockSpec((2, MOE_TILE, PACK_WORDS), row_tile),
            scratch_shapes=[
                pltpu.VMEM((D_MODEL, D_EXPERT), BF16),
                pltpu.VMEM((D_MODEL, D_EXPERT), BF16),
                pltpu.VMEM((D_EXPERT, D_MODEL), BF16),
            ],
        ),
        compiler_params=_params(32, "arbitrary"),
        name="moe_experts",
    )(tile_expert, tile_rows, tile_first, n_active, xs, w1, w3, w2)


def _moe_res(x_ref, y_ref, rw_ref, mod_ref):
    cond = _tile_cond(pl.program_id(0))
    rw = rw_ref[...]
    wa = rw[:, 0:1]
    wb = rw[:, 1:2]
    ya = _unpack_rows(y_ref[0], y_ref[1])
    yb = _unpack_rows(y_ref[2], y_ref[3])
    y = jnp.concatenate([wa * a.astype(F32) + wb * b.astype(F32) for a, b in zip(ya, yb)], axis=1)
    return x_ref[...] + _mod_row(mod_ref, cond, 5) * y


def _moe_res_kernel(x_ref, y_ref, rw_ref, mod_ref, o_ref):
    o_ref[...] = _moe_res(x_ref, y_ref, rw_ref, mod_ref)


def _moe_res_final_kernel(x_ref, y_ref, rw_ref, mod_ref, fn_ref, op_ref, os_ref):
    x = _moe_res(x_ref, y_ref, rw_ref, mod_ref)
    y = x * lax.rsqrt(jnp.mean(x * x, axis=-1, keepdims=True) + EPS) * fn_ref[...]
    is_prompt = pl.program_id(0) < PROMPT_TILES

    @pl.when(is_prompt)
    def _():
        op_ref[...] = y

    @pl.when(jnp.logical_not(is_prompt))
    def _():
        os_ref[...] = y


def _moe_combine(x1, y_pairs, route_w, mod, final_norm=None):
    tile = lambda i: (i, 0)
    fixed2 = lambda i: (0, 0)
    blk = pl.BlockSpec((TM, D_MODEL), tile)
    specs = [blk,
             pl.BlockSpec((4, TM, PACK_WORDS), lambda i: (0, i, 0)),
             pl.BlockSpec((TM, ROUTER_LANES), tile),
             pl.BlockSpec(mod.shape, fixed2)]
    args = [x1, y_pairs, route_w, mod]
    if final_norm is None:
        kernel_fn = _moe_res_kernel
        out_shape = jax.ShapeDtypeStruct((N_TOK, D_MODEL), F32)
        out_specs = blk
    else:
        kernel_fn = _moe_res_final_kernel
        specs.append(pl.BlockSpec((1, D_MODEL), fixed2))
        args.append(final_norm)
        out_shape = (jax.ShapeDtypeStruct((N_PROMPT, D_MODEL), F32),
                     jax.ShapeDtypeStruct((N_TOK - N_PROMPT, D_MODEL), F32))
        out_specs = (pl.BlockSpec((TM, D_MODEL), _prompt_tile), pl.BlockSpec((TM, D_MODEL), _sample_tile))
    return pl.pallas_call(
        kernel_fn,
        out_shape=out_shape,
        grid=(N_TILES,),
        in_specs=specs,
        out_specs=out_specs,
        compiler_params=_params(32, "arbitrary"),
        name="moe_combine",
    )(*args)


def _moe_layer(x1, h2p, route_i, route_w, counts, mod, layer, w1, w3, w2, final_norm=None):
    pos, tile_expert, tile_rows, tile_first, n_active = _moe_plan(route_i, counts)
    idx_a = jnp.concatenate([pos[:, 0], pos[:, 0] + MOE_ROWS])
    idx_b = jnp.concatenate([pos[:, 1], pos[:, 1] + MOE_ROWS])
    xs = _sc_scatter_rows(h2p.reshape(2 * N_TOK, PACK_WORDS), idx_a, idx_b, 2 * MOE_ROWS)
    ys = _ffn(xs.reshape(2, MOE_ROWS, PACK_WORDS), tile_expert, tile_rows, tile_first, n_active,
              layer, w1, w3, w2)
    y_pairs = _sc_gather_rows(ys.reshape(2 * MOE_ROWS, PACK_WORDS), jnp.concatenate([idx_a, idx_b]))
    return _moe_combine(x1, y_pairs.reshape(4, N_TOK, PACK_WORDS), route_w, mod, final_norm)


def _swap_rotary_halves(x):
    n = x.shape[-1]
    lane = lax.broadcasted_iota(jnp.int32, x.shape, 1)
    quarter = ROPE_HALF // 2
    return jnp.where((lane % ROPE_HALF) < quarter, pltpu.roll(x, n - quarter, 1), pltpu.roll(x, quarter, 1))


def _at_in_kernel(x_ref, mod_ref, nw_ref, w_ref, cos_ref, sin_ref, q_ref, kt_ref, v_ref, kc_ref, vc_ref):
    i = pl.program_id(0)
    cond = _tile_cond(i)
    h = _norm_mod(x_ref[...], nw_ref[...], _mod_row(mod_ref, cond, 0), _mod_row(mod_ref, cond, 1)).astype(BF16)
    nq = ATT_Q_HEADS * ATT_HEAD_DIM
    nk = ATT_KV_HEADS * ATT_HEAD_DIM
    cos = cos_ref[...]
    sin = sin_ref[...]

    def rope(x):
        reps = x.shape[-1] // LANES
        return x * jnp.concatenate([cos] * reps, axis=1) + _swap_rotary_halves(x) * jnp.concatenate([sin] * reps, axis=1)

    q_ref[...] = (rope(jnp.dot(h, w_ref[:, :nq], preferred_element_type=F32)) * ATT_HEAD_DIM ** -0.5).astype(BF16)
    k = rope(jnp.dot(h, w_ref[:, nq:nq + nk], preferred_element_type=F32))
    v = jnp.dot(h, w_ref[:, nq + nk:], preferred_element_type=F32)
    kt = k.T
    kt_ref[...] = kt.astype(BF16)
    v_ref[...] = v.astype(BF16)

    @pl.when(i < PROMPT_TILES)
    def _():
        kc_ref[...] = kt
        vc_ref[...] = v.T


def _rope_tables():
    f32 = np.float32
    pos = np.arange(DEC_SEQ)
    t_row = (pos // GRID_W).astype(f32)
    t_col = (pos % GRID_W).astype(f32)
    inv = f32(ROPE_BASE) ** (-np.arange(0, ROPE_HALF, 2, dtype=f32) / f32(ROPE_HALF))
    j = np.arange(LANES) % ATT_HEAD_DIM
    freq = inv[(j % ROPE_HALF) % (ROPE_HALF // 2)]
    ang = (np.where((j < ROPE_HALF)[None, :], t_row[:, None], t_col[:, None]) * freq[None, :]).astype(f32)
    sign = np.where((j % ROPE_HALF) < ROPE_HALF // 2, -1.0, 1.0).astype(f32)
    cos = np.concatenate([np.ones((TM, LANES), f32), np.cos(ang)], axis=0)
    sin = np.concatenate([np.zeros((TM, LANES), f32), np.sin(ang) * sign[None, :]], axis=0)
    return jnp.asarray(cos, F32), jnp.asarray(sin, F32)


def _at_in(x, mod, nw, w_in, cos, sin):
    tile = lambda i: (i, 0)
    fixed2 = lambda i: (0, 0)
    rope_tile = lambda i: (jnp.where(i < PROMPT_TILES, 0, 1 + (i - PROMPT_TILES) % TILES_PER_DEC_SEQ), 0)
    nk = ATT_KV_HEADS * ATT_HEAD_DIM
    return pl.pallas_call(
        _at_in_kernel,
        out_shape=(
            jax.ShapeDtypeStruct((N_TOK, D_MODEL), BF16),
            jax.ShapeDtypeStruct((nk, N_TOK), BF16),
            jax.ShapeDtypeStruct((N_TOK, nk), BF16),
            jax.ShapeDtypeStruct((N_PROMPT, nk), F32),
            jax.ShapeDtypeStruct((N_PROMPT, nk), F32),
        ),
        grid=(N_TILES,),
        in_specs=[
            pl.BlockSpec((TM, D_MODEL), tile),
            pl.BlockSpec(mod.shape, fixed2),
            pl.BlockSpec((1, D_MODEL), fixed2),
            pl.BlockSpec(w_in.shape, fixed2),
            pl.BlockSpec((TM, LANES), rope_tile),
            pl.BlockSpec((TM, LANES), rope_tile),
        ],
        out_specs=(pl.BlockSpec((TM, D_MODEL), tile), pl.BlockSpec((nk, TM), lambda i: (0, i)),
                   pl.BlockSpec((TM, nk), tile),
                   pl.BlockSpec((TM, nk), _prompt_tile), pl.BlockSpec((TM, nk), _prompt_tile)),
        compiler_params=_params(40, "arbitrary"),
        name="attn_in_proj",
    )(x, mod, nw, w_in, cos, sin)


def _attend(q, kt_all, v_all, mask, sink_ref, o_ref):
    nq = q.shape[0]
    group_lanes = ATT_GROUP * ATT_HEAD_DIM
    lane = lax.broadcasted_iota(jnp.int32, (nq, group_lanes), 1)
    mine = [(lane // ATT_HEAD_DIM) == g for g in range(ATT_GROUP)]
    row_head = lax.broadcasted_iota(jnp.int32, (ATT_GROUP * nq, 1), 0) // nq
    if mask is not None:
        mask = jnp.concatenate([mask] * ATT_GROUP, axis=0)
    for hk in range(ATT_KV_HEADS):
        vh = v_all[:, hk * ATT_HEAD_DIM:(hk + 1) * ATT_HEAD_DIM]
        kt = jnp.concatenate([kt_all[hk * ATT_HEAD_DIM:(hk + 1) * ATT_HEAD_DIM, :]] * ATT_GROUP, axis=0)
        vt = jnp.concatenate([vh] * ATT_GROUP, axis=1)
        qg = q[:, hk * group_lanes:(hk + 1) * group_lanes]
        q_stack = jnp.concatenate([jnp.where(mine[g], qg, jnp.zeros_like(qg)) for g in range(ATT_GROUP)], axis=0)
        s = jnp.dot(q_stack, kt, preferred_element_type=F32)
        if mask is not None:
            s = jnp.where(mask, s, -jnp.inf)
        sink = jnp.zeros((ATT_GROUP * nq, 1), F32)
        for g in range(ATT_GROUP):
            sink = jnp.where(row_head == g, sink_ref[hk * ATT_GROUP + g], sink)
        m = jnp.maximum(jnp.max(s, axis=-1, keepdims=True), sink)
        p = jnp.exp(s - m)
        denom = jnp.sum(p, axis=-1, keepdims=True) + jnp.exp(sink - m)
        o = jnp.dot(p.astype(BF16), vt, preferred_element_type=F32) / denom
        acc = jnp.where(mine[0], o[:nq], 0.0)
        for g in range(1, ATT_GROUP):
            acc = acc + jnp.where(mine[g], o[g * nq:(g + 1) * nq], 0.0)
        o_ref[:, hk * group_lanes:(hk + 1) * group_lanes] = acc.astype(BF16)


def _ctx_attn_kernel(sink_ref, q_ref, k_ref, v_ref, o_ref):
    _attend(q_ref[...], k_ref[...], v_ref[...], None, sink_ref, o_ref)


def _lat_attn_kernel(sink_ref, q_ref, kp_ref, kc_ref, kn_ref, vp_ref, vc_ref, vn_ref, ck_ref, cv_ref, o_ref):
    jb = pl.program_id(1)
    kt_all = jnp.concatenate([kp_ref[...], kc_ref[...], kn_ref[...], ck_ref[0].astype(BF16)], axis=1)
    v_all = jnp.concatenate([vp_ref[...], vc_ref[...], vn_ref[...], cv_ref[0].astype(BF16)], axis=0)
    nkeys = 3 * BLOCK + PAST_LEN
    qi = lax.broadcasted_iota(jnp.int32, (BLOCK, nkeys), 0)
    kj = lax.broadcasted_iota(jnp.int32, (BLOCK, nkeys), 1)
    qpos = jb * BLOCK + qi
    kpos = (jb - 1) * BLOCK + kj
    local_ok = (jnp.abs(qpos - kpos) <= WINDOW) & (kpos >= 0) & (kpos < DEC_SEQ)
    mask = (kj >= 3 * BLOCK) | local_ok
    _attend(q_ref[...], kt_all, v_all, mask, sink_ref, o_ref)


def _attention(q, kt, v, cache_kt, cache_v, sink):
    nk = ATT_KV_HEADS * ATT_HEAD_DIM
    smem = pl.BlockSpec(memory_space=pltpu.SMEM)
    ctx = pl.pallas_call(
        _ctx_attn_kernel,
        out_shape=jax.ShapeDtypeStruct((N_PROMPT, D_MODEL), BF16),
        grid=(BATCH,),
        in_specs=[
            smem,
            pl.BlockSpec((SEQ, D_MODEL), lambda b: (b, 0)),
            pl.BlockSpec((nk, SEQ), lambda b: (0, b)),
            pl.BlockSpec((SEQ, nk), lambda b: (b, 0)),
        ],
        out_specs=pl.BlockSpec((SEQ, D_MODEL), lambda b: (b, 0)),
        compiler_params=_params(40, "arbitrary"),
        name="context_attention",
    )(sink, q, kt, v)

    nb = DEC_SEQ // BLOCK
    base = N_PROMPT // BLOCK
    cur = lambda b, j: (base + b * nb + j, 0)
    prev = lambda b, j: (base + b * nb + jnp.maximum(j - 1, 0), 0)
    nxt = lambda b, j: (base + b * nb + jnp.minimum(j + 1, nb - 1), 0)
    kv_blk = lambda f: pl.BlockSpec((BLOCK, nk), f)
    kt_blk = lambda f: pl.BlockSpec((nk, BLOCK), lambda b, j: f(b, j)[::-1])
    cache_blk = pl.BlockSpec((1, PAST_LEN, nk), lambda b, j: (b, 0, 0))
    cache_kt_blk = pl.BlockSpec((1, nk, PAST_LEN), lambda b, j: (b, 0, 0))
    lat = pl.pallas_call(
        _lat_attn_kernel,
        out_shape=jax.ShapeDtypeStruct((DEC_BATCH * DEC_SEQ, D_MODEL), BF16),
        grid=(DEC_BATCH, nb),
        in_specs=[
            smem,
            pl.BlockSpec((BLOCK, D_MODEL), cur),
            kt_blk(prev), kt_blk(cur), kt_blk(nxt),
            kv_blk(prev), kv_blk(cur), kv_blk(nxt),
            cache_kt_blk, cache_blk,
        ],
        out_specs=pl.BlockSpec((BLOCK, D_MODEL), lambda b, j: (b * nb + j, 0)),
        compiler_params=_params(40, "arbitrary", "arbitrary"),
        name="latent_attention",
    )(sink, q, kt, kt, kt, v, v, v, cache_kt, cache_v)
    return ctx, lat


def kernel(x_prompt, x_sample, state_hgrn, cache_k, cache_v, c, c_ctx, ada_w, ada_b, norm_w, hg_w_in,
           hg_lb_logits, hg_onorm, hg_w_out, at_w_in, at_sink, at_w_out, moe_w_group, moe_b_group,
           moe_w_expert, moe_b_expert, moe_w1, moe_w3, moe_w2, final_norm):
    xp = x_prompt.reshape(N_PROMPT, D_MODEL)
    xs = x_sample.reshape(N_TOK - N_PROMPT, D_MODEL)
    cond = jnp.concatenate([c_ctx[None, :], c], axis=0)
    mod = _ada(cond, ada_w, ada_b)
    nk = ATT_KV_HEADS * ATT_HEAD_DIM

    def router_params(i):
        pad = jnp.zeros((D_MODEL, ROUTER_LANES - N_EXPERTS - N_GROUPS), F32)
        w = jnp.concatenate([moe_w_expert[i], moe_w_group[i], pad], axis=1)
        b = jnp.concatenate([moe_b_expert[i], moe_b_group[i], pad[0]])[None, :]
        hi = w.astype(BF16)
        lo = (w - hi.astype(F32)).astype(BF16)
        return jnp.stack([hi, lo]), b

    tile = lambda i: (i, 0)
    blk = pl.BlockSpec((TM, D_MODEL), tile)

    prompt_blk = pl.BlockSpec((TM, D_MODEL), _prompt_tile)
    sample_blk = pl.BlockSpec((TM, D_MODEL), _sample_tile)
    q, v, g, lff, kf, lfb, kb = _hg_in(xp, xs, mod[0], norm_w[0, 0][None, :], hg_w_in[0].astype(BF16), hg_lb_logits)
    o_f, o_b, state_new = _gla(q, v, lff, kf, lfb, kb, state_hgrn)
    wr, br = router_params(0)
    routed = _mixer_out(
        _hg_out_kernel, "hgrn_out_route", (o_f, o_b, g, hg_onorm[0][None, :], xp, xs),
        (blk, blk, blk, pl.BlockSpec((1, HG_DK), lambda i: (0, 0)), prompt_blk, sample_blk),
        mod[0], norm_w[0, 1][None, :], hg_w_out[0].astype(BF16), wr, br)
    x = _moe_layer(*routed, mod[0], 0, moe_w1, moe_w3, moe_w2)

    cos, sin = _rope_tables()
    qa, ka, va, k_ctx, v_ctx = _at_in(x, mod[1], norm_w[1, 0][None, :], at_w_in[0].astype(BF16), cos, sin)
    attn_ctx, attn_lat = _attention(qa, ka, va, cache_k[:, 0].reshape(DEC_BATCH, PAST_LEN, nk).transpose(0, 2, 1),
                                    cache_v[:, 0].reshape(DEC_BATCH, PAST_LEN, nk), at_sink[0])
    wr, br = router_params(1)
    routed = _mixer_out(
        _at_out_kernel, "attn_out_route", (attn_ctx, attn_lat, x), (prompt_blk, sample_blk, blk),
        mod[1], norm_w[1, 1][None, :], at_w_out[0].astype(BF16), wr, br)
    y_prompt, y_sample = _moe_layer(*routed, mod[1], 1, moe_w1, moe_w3, moe_w2, final_norm[None, :])

    def cache(feature_major):
        return feature_major.reshape(BATCH, 1, ATT_KV_HEADS, ATT_HEAD_DIM, SEQ).transpose(0, 1, 4, 2, 3)

    return (y_prompt.reshape(BATCH, SEQ, D_MODEL), y_sample.reshape(DEC_BATCH, DEC_SEQ, D_MODEL), state_new,
            cache(k_ctx), cache(v_ctx))
```

```python
import functools

import jax
import jax.numpy as jnp
import numpy as np
from jax import lax
from jax.experimental import pallas as pl
from jax.experimental.pallas import tpu as pltpu
from jax.experimental.pallas import tpu_sc as plsc

F32 = jnp.float32
BF16 = jnp.bfloat16

D_MODEL = 1024
BATCH = 16
SEQ = 256
DEC_BATCH = 2
DEC_SEQ = 1024
PAST_LEN = 512
GRID_W = 64
HG_HEADS = 8
HG_DK = 128
CHUNK = 16
ATT_HEAD_DIM = 64
ATT_Q_HEADS = 16
ATT_KV_HEADS = 4
ATT_GROUP = 4
WINDOW = 128
BLOCK = 128
ROPE_HALF = 32
ROPE_BASE = 10000.0
N_GROUPS = 4
EXPERTS_PER_GROUP = 8
N_EXPERTS = 32
D_EXPERT = 256
EPS = 1e-6

N_PROMPT = BATCH * SEQ
N_TOK = N_PROMPT + DEC_BATCH * DEC_SEQ
TM = 256
N_TILES = N_TOK // TM
PROMPT_TILES = N_PROMPT // TM
TILES_PER_DEC_SEQ = DEC_SEQ // TM
LANES = 128
N_COND_USED = 1 + DEC_BATCH
N_COND = 8
ROUTER_LANES = 128
GLA_UNROLL = 8
DECAY_CLAMP = 60.0
PACK_WORDS = D_MODEL // 4
MOE_TILE = 256
MOE_ROWS = 2 * N_TOK + N_EXPERTS * MOE_TILE
MIB = 1024 * 1024


def _params(vmem_mib, *semantics):
    return pltpu.CompilerParams(dimension_semantics=semantics, vmem_limit_bytes=vmem_mib * MIB)


def _tile_cond(i):
    return jnp.where(i < PROMPT_TILES, 0, 1 + (i - PROMPT_TILES) // TILES_PER_DEC_SEQ)


def _prompt_tile(i):
    return (jnp.minimum(i, PROMPT_TILES - 1), 0)


def _sample_tile(i):
    return (jnp.maximum(i - PROMPT_TILES, 0), 0)


def _stream_tile(prompt_ref, sample_ref):
    return jnp.where(pl.program_id(0) < PROMPT_TILES, prompt_ref[...], sample_ref[...])


def _mod_row(mod_ref, cond, which):
    return mod_ref[pl.ds(cond, 1), which * D_MODEL:(which + 1) * D_MODEL]


def _norm_mod(x, nw, shift, scale):
    y = x * lax.rsqrt(jnp.mean(x * x, axis=-1, keepdims=True) + EPS)
    return (y * nw) * (1.0 + scale) + shift


def _silu(x):
    return x * jax.nn.sigmoid(x)


def _ada_kernel(c_ref, w_ref, b_ref, o_ref):
    s = [_silu(c_ref[r]) for r in range(N_COND_USED)]
    tn = w_ref.shape[-1]
    rows = []
    for r in range(N_COND_USED):
        cols = [jnp.sum(w_ref[0, :, j * LANES:(j + 1) * LANES] * s[r], axis=0, keepdims=True)
                for j in range(tn // LANES)]
        rows.append(jnp.concatenate(cols, axis=1) + b_ref[0])
    rows.append(jnp.zeros((N_COND - N_COND_USED, tn), F32))
    o_ref[0] = jnp.concatenate(rows, axis=0)


def _ada(cond, ada_w, ada_b):
    depth, _, n = ada_w.shape
    tn = 1536
    cond_cols = jnp.broadcast_to(cond[:, :, None], (N_COND_USED, D_MODEL, LANES))
    return pl.pallas_call(
        _ada_kernel,
        out_shape=jax.ShapeDtypeStruct((depth, N_COND, n), F32),
        grid=(depth, n // tn),
        in_specs=[
            pl.BlockSpec((N_COND_USED, D_MODEL, LANES), lambda l, j: (0, 0, 0)),
            pl.BlockSpec((1, D_MODEL, tn), lambda l, j: (l, 0, j)),
            pl.BlockSpec((1, 1, tn), lambda l, j: (l, 0, j)),
        ],
        out_specs=pl.BlockSpec((1, N_COND, tn), lambda l, j: (l, 0, j)),
        compiler_params=_params(40, "arbitrary", "arbitrary"),
        name="ada_modulation",
    )(cond_cols, ada_w, ada_b.reshape(depth, 1, n))


def _hg_in_kernel(xp_ref, xs_ref, mod_ref, nw_ref, w_ref, lbl_ref,
                  q_ref, v_ref, g_ref, lff_ref, kf_ref, lfb_ref, kb_ref):
    cond = _tile_cond(pl.program_id(0))
    x = _stream_tile(xp_ref, xs_ref)
    h = _norm_mod(x, nw_ref[...], _mod_row(mod_ref, cond, 0), _mod_row(mod_ref, cond, 1)).astype(BF16)

    def proj(c):
        return jnp.dot(h, w_ref[:, c * D_MODEL:(c + 1) * D_MODEL], preferred_element_type=F32)

    l0, l1, l2 = lbl_ref[0], lbl_ref[1], lbl_ref[2]
    m = jnp.maximum(jnp.maximum(l0, l1), l2)
    e0, e1, e2 = jnp.exp(l0 - m), jnp.exp(l1 - m), jnp.exp(l2 - m)
    lb = e0 / (e0 + e1 + e2)

    q_ref[...] = proj(0).astype(BF16)
    v_ref[...] = proj(1).astype(BF16)
    for d, (lf_ref, k_ref) in enumerate(((lff_ref, kf_ref), (lfb_ref, kb_ref))):
        lbd = lb[d:d + 1, :]
        f = lbd + (1.0 - lbd) * jax.nn.sigmoid(proj(2 + d))
        lf = jnp.log(f)
        hi = lf.astype(BF16)
        lf_ref[0] = hi
        lf_ref[1] = (lf - hi.astype(F32)).astype(BF16)
        k_ref[...] = (1.0 - f).astype(BF16)
    g_ref[...] = proj(4).astype(BF16)


def _hg_in(xp, xs, mod, nw, w_in, lb_logits):
    tile = lambda i: (i, 0)
    fixed2 = lambda i: (0, 0)
    bf = jax.ShapeDtypeStruct((N_TOK, D_MODEL), BF16)
    ff = jax.ShapeDtypeStruct((2, N_TOK, D_MODEL), BF16)
    blk = pl.BlockSpec((TM, D_MODEL), tile)
    split_blk = pl.BlockSpec((2, TM, D_MODEL), lambda i: (0, i, 0))
    return pl.pallas_call(
        _hg_in_kernel,
        out_shape=(bf, bf, bf, ff, bf, ff, bf),
        grid=(N_TILES,),
        in_specs=[
            pl.BlockSpec((TM, D_MODEL), _prompt_tile),
            pl.BlockSpec((TM, D_MODEL), _sample_tile),
            pl.BlockSpec(mod.shape, fixed2),
            pl.BlockSpec((1, D_MODEL), fixed2),
            pl.BlockSpec(w_in.shape, fixed2),
            pl.BlockSpec(lb_logits.shape, lambda i: (0, 0, 0)),
        ],
        out_specs=(blk, blk, blk, split_blk, blk, split_blk, blk),
        compiler_params=_params(56, "arbitrary"),
        name="hgrn_in_proj",
    )(xp, xs, mod, nw, w_in, lb_logits)


def _gla_exact(reverse, q_ref, v_ref, lf_ref, k_ref, st_ref, o_ref):
    nt = (((1,), (1,)), ((), ()))
    tn = (((0,), (0,)), ((), ()))
    n_chunks = TM // CHUNK
    row_id = lax.broadcasted_iota(jnp.int32, (CHUNK, HG_DK), 0)
    order = range(CHUNK - 1, -1, -1) if reverse else range(CHUNK)

    def chunk_head(it, carry):
        step, h = it // HG_HEADS, it % HG_HEADS
        ci = (n_chunks - 1 - step) if reverse else step
        rows = pl.ds(pl.multiple_of(ci * CHUNK, CHUNK), CHUNK)
        cols = pl.ds(pl.multiple_of(h * HG_DK, HG_DK), HG_DK)
        q, k, v = q_ref[rows, cols], k_ref[rows, cols], v_ref[rows, cols]
        f = jnp.exp(lf_ref[0, rows, cols].astype(F32) + lf_ref[1, rows, cols].astype(F32))
        st = st_ref[0, h]
        o = jnp.zeros((CHUNK, HG_DK), F32)
        for t in order:
            one = row_id == t
            v_t = jnp.where(one, v, jnp.zeros_like(v))
            st = st * f[t:t + 1, :] + lax.dot_general(v_t, k, tn, preferred_element_type=F32)
            o = jnp.where(one, lax.dot_general(q, st.astype(BF16), nt, preferred_element_type=F32), o)
        st_ref[0, h] = st
        o_ref[rows, cols] = o
        return carry

    lax.fori_loop(0, n_chunks * HG_HEADS, chunk_head, 0)


def _gla_direction(reverse, q_ref, v_ref, lf_ref, k_ref, st_ref, sw_ref, o_ref,
                   ut_scr, qd_scr, kd_scr, ki_scr, cd_scr):
    r = lax.broadcasted_iota(jnp.int32, (TM, TM), 0)
    c = lax.broadcasted_iota(jnp.int32, (TM, TM), 1)
    same = (r // CHUNK) == (c // CHUNK)
    tri = (same & ((c >= r) if reverse else (c <= r))).astype(BF16)
    b = jnp.dot(tri, lf_ref[0], preferred_element_type=F32) + jnp.dot(tri, lf_ref[1], preferred_element_type=F32)
    b3 = b.reshape(TM // CHUNK, CHUNK, D_MODEL)
    edge = 0 if reverse else CHUNK - 1
    tot = jnp.broadcast_to(b3[:, edge:edge + 1, :], b3.shape).reshape(TM, D_MODEL)
    k = k_ref[...]
    qd_scr[...] = q_ref[...] * jnp.exp(b).astype(BF16)
    kd_scr[...] = k * jnp.exp(tot - b).astype(BF16)
    ki_scr[...] = k * jnp.exp(jnp.minimum(-b, DECAY_CLAMP)).astype(BF16)
    cd_scr[...] = jnp.exp(tot)

    tr = lax.broadcasted_iota(jnp.int32, (CHUNK, CHUNK), 0)
    tc = lax.broadcasted_iota(jnp.int32, (CHUNK, CHUNK), 1)
    keep = (tc >= tr) if reverse else (tc <= tr)
    nt = (((1,), (1,)), ((), ()))
    tn = (((0,), (0,)), ((), ()))

    decay_ok = jnp.max(-b) <= DECAY_CLAMP

    @pl.when(jnp.logical_not(decay_ok))
    def _():
        _gla_exact(reverse, q_ref, v_ref, lf_ref, k_ref, st_ref, o_ref)

    @pl.when(decay_ok)
    def _():
        heads = [slice(h * HG_DK, (h + 1) * HG_DK) for h in range(HG_HEADS)]
        for h in range(HG_HEADS):
            sw_ref[0, h] = st_ref[0, h].T.astype(BF16)

        n_chunks = TM // CHUNK

        def chunk_rows(step):
            ci = (n_chunks - 1 - step) if reverse else step
            row0 = pl.multiple_of(ci * CHUNK, CHUNK)
            return row0, pl.ds(row0, CHUNK)

        def key_value_product(rows):
            for h, cols in enumerate(heads):
                ut_scr[h] = lax.dot_general(v_ref[rows, cols], kd_scr[rows, cols], tn, preferred_element_type=F32)

        key_value_product(chunk_rows(0)[1])

        def chunk(step, src, dst):
            row0, rows = chunk_rows(step)
            cd_row = cd_scr[pl.ds(row0, 1), :]
            for h, cols in enumerate(heads):
                new = st_ref[src, h] * cd_row[:, cols] + ut_scr[h]
                st_ref[dst, h] = new
                sw_ref[dst, h] = new.T.astype(BF16)
            a = [lax.dot_general(qd_scr[rows, cols], ki_scr[rows, cols], nt, preferred_element_type=F32)
                 for cols in heads]
            inter = [jnp.dot(qd_scr[rows, cols], sw_ref[src, h], preferred_element_type=F32)
                     for h, cols in enumerate(heads)]
            key_value_product(chunk_rows(jnp.minimum(step + 1, n_chunks - 1))[1])
            for h, cols in enumerate(heads):
                am = jnp.where(keep, a[h], 0.0).astype(BF16)
                o_ref[rows, cols] = jnp.dot(am, v_ref[rows, cols], preferred_element_type=F32) + inter[h]

        def chunk_group(it, carry):
            for p in range(GLA_UNROLL):
                chunk(GLA_UNROLL * it + p, p % 2, 1 - p % 2)
            return carry

        lax.fori_loop(0, n_chunks // GLA_UNROLL, chunk_group, 0)


def _gla_kernel(qf_ref, vf_ref, lff_ref, kf_ref, qb_ref, vb_ref, lfb_ref, kb_ref, s0_ref,
                of_ref, ob_ref, sout_ref, stf_scr, stb_scr, sw_scr, ut_scr, qd_scr, kd_scr, ki_scr, cd_scr):
    i = pl.program_id(0)
    is_prompt = i < PROMPT_TILES
    first = jnp.logical_or(is_prompt, (i - PROMPT_TILES) % TILES_PER_DEC_SEQ == 0)

    @pl.when(first)
    def _():
        keep0 = jnp.where(is_prompt, 0.0, 1.0)
        for h in range(HG_HEADS):
            stf_scr[0, h] = s0_ref[0, 0, 0, h].T * keep0
            stb_scr[0, h] = s0_ref[0, 0, 1, h].T * keep0

    _gla_direction(False, qf_ref, vf_ref, lff_ref, kf_ref, stf_scr, sw_scr, of_ref,
                   ut_scr, qd_scr, kd_scr, ki_scr, cd_scr)
    _gla_direction(True, qb_ref, vb_ref, lfb_ref, kb_ref, stb_scr, sw_scr, ob_ref,
                   ut_scr, qd_scr, kd_scr, ki_scr, cd_scr)

    @pl.when(is_prompt)
    def _():
        for h in range(HG_HEADS):
            sout_ref[0, 0, 0, h] = stf_scr[0, h].T
            sout_ref[0, 0, 1, h] = stb_scr[0, h].T


def _gla(q, v, lff, kf, lfb, kb, state_hgrn):
    def fwd_tile(i):
        return (i, 0)

    def bwd_tile(i):
        j = (i - PROMPT_TILES) % TILES_PER_DEC_SEQ
        return (jnp.where(i < PROMPT_TILES, i, i - j + (TILES_PER_DEC_SEQ - 1 - j)), 0)

    def s0_idx(i):
        return (jnp.maximum(i - PROMPT_TILES, 0) // TILES_PER_DEC_SEQ, 0, 0, 0, 0, 0)

    def sout_idx(i):
        return (jnp.minimum(i, PROMPT_TILES - 1), 0, 0, 0, 0, 0)

    f_blk = pl.BlockSpec((TM, D_MODEL), fwd_tile)
    b_blk = pl.BlockSpec((TM, D_MODEL), bwd_tile)
    f_split = pl.BlockSpec((2, TM, D_MODEL), lambda i: (0,) + fwd_tile(i))
    b_split = pl.BlockSpec((2, TM, D_MODEL), lambda i: (0,) + bwd_tile(i))
    st_blk = (1, 1, 2, HG_HEADS, HG_DK, HG_DK)
    return pl.pallas_call(
        _gla_kernel,
        out_shape=(
            jax.ShapeDtypeStruct((N_TOK, D_MODEL), F32),
            jax.ShapeDtypeStruct((N_TOK, D_MODEL), F32),
            jax.ShapeDtypeStruct((BATCH,) + st_blk[1:], F32),
        ),
        grid=(N_TILES,),
        in_specs=[f_blk, f_blk, f_split, f_blk, b_blk, b_blk, b_split, b_blk, pl.BlockSpec(st_blk, s0_idx)],
        out_specs=(f_blk, b_blk, pl.BlockSpec(st_blk, sout_idx)),
        scratch_shapes=[
            pltpu.VMEM((2, HG_HEADS, HG_DK, HG_DK), F32),
            pltpu.VMEM((2, HG_HEADS, HG_DK, HG_DK), F32),
            pltpu.VMEM((2, HG_HEADS, HG_DK, HG_DK), BF16),
            pltpu.VMEM((HG_HEADS, HG_DK, HG_DK), F32),
            pltpu.VMEM((TM, D_MODEL), BF16),
            pltpu.VMEM((TM, D_MODEL), BF16),
            pltpu.VMEM((TM, D_MODEL), BF16),
            pltpu.VMEM((TM, D_MODEL), F32),
        ],
        compiler_params=_params(48, "arbitrary"),
        name="hgrn_recurrence",
    )(q, v, lff, kf, q, v, lfb, kb, state_hgrn)


def _route(logits):
    lane = lax.broadcasted_iota(jnp.int32, logits.shape, 1)
    neg = jnp.float32(-jnp.inf)
    is_group = (lane >= N_EXPERTS) & (lane < N_EXPERTS + N_GROUPS)
    gl = jnp.where(is_group, logits, neg)
    gmax = jnp.max(gl, axis=-1, keepdims=True)
    g_sel = jnp.argmax(gl, axis=-1, keepdims=True).astype(jnp.int32) - N_EXPERTS
    gsum = jnp.sum(jnp.exp(gl - gmax), axis=-1, keepdims=True)
    p_g = 1.0 / gsum
    in_sel = (lane < N_EXPERTS) & ((lane // EXPERTS_PER_GROUP) == g_sel)
    el = jnp.where(in_sel, logits, neg)
    m1 = jnp.max(el, axis=-1, keepdims=True)
    i1 = jnp.argmax(el, axis=-1, keepdims=True).astype(jnp.int32)
    el2 = jnp.where(lane == i1, neg, el)
    m2 = jnp.max(el2, axis=-1, keepdims=True)
    i2 = jnp.argmax(el2, axis=-1, keepdims=True).astype(jnp.int32)
    e2 = jnp.exp(m2 - m1)
    return i1, i2, p_g / (1.0 + e2), p_g * e2 / (1.0 + e2)


def _pack_rows(x):
    q = PACK_WORDS
    bits = pltpu.bitcast(x.astype(BF16).astype(F32), jnp.uint32)
    return [(bits[:, (2 + h) * q:(3 + h) * q] & jnp.uint32(0xFFFF0000)) | (bits[:, h * q:(h + 1) * q] >> 16)
            for h in range(2)]


def _unpack_rows(half0, half1):
    lo = lambda w: pltpu.bitcast(w << 16, F32).astype(BF16)
    hi = lambda w: pltpu.bitcast(w & jnp.uint32(0xFFFF0000), F32).astype(BF16)
    return [lo(half0), lo(half1), hi(half0), hi(half1)]


def _mixer_tail(mix_bf16, x, mod_ref, nw2_ref, wo_ref, wr_ref, br_ref,
                x1_ref, h2p_ref, ri_ref, rw_ref, cnt_ref, carry_scr):
    i = pl.program_id(0)
    cond = _tile_cond(i)
    out = jnp.dot(mix_bf16, wo_ref[...], preferred_element_type=F32)
    x1 = x + _mod_row(mod_ref, cond, 2) * out
    x1_ref[...] = x1
    h2 = _norm_mod(x1, nw2_ref[...], _mod_row(mod_ref, cond, 3), _mod_row(mod_ref, cond, 4))
    for h, words in enumerate(_pack_rows(h2)):
        h2p_ref[h] = words
    h_hi = h2.astype(BF16)
    h_lo = (h2 - h_hi.astype(F32)).astype(BF16)
    logits = (jnp.dot(h_hi, wr_ref[0], preferred_element_type=F32)
              + (jnp.dot(h_hi, wr_ref[1], preferred_element_type=F32)
                 + jnp.dot(h_lo, wr_ref[0], preferred_element_type=F32))) + br_ref[...]
    i1, i2, w1, w2 = _route(logits)

    @pl.when(i == 0)
    def _():
        carry_scr[...] = jnp.zeros_like(carry_scr)

    lane = lax.broadcasted_iota(jnp.int32, logits.shape, 1)
    chosen = ((lane == i1) | (lane == i2)).astype(BF16)
    r = lax.broadcasted_iota(jnp.int32, (TM, TM), 0)
    c = lax.broadcasted_iota(jnp.int32, (TM, TM), 1)
    before = jnp.dot((c < r).astype(BF16), chosen, preferred_element_type=F32) + carry_scr[...]
    r1 = jnp.sum(jnp.where(lane == i1, before, 0.0), axis=-1, keepdims=True).astype(jnp.int32)
    r2 = jnp.sum(jnp.where(lane == i2, before, 0.0), axis=-1, keepdims=True).astype(jnp.int32)
    total = carry_scr[...] + jnp.sum(chosen.astype(F32), axis=0, keepdims=True)
    carry_scr[...] = total
    cnt_ref[...] = total
    ri_ref[...] = jnp.where(lane == 0, i1, jnp.where(lane == 1, i2, jnp.where(lane == 2, r1, r2)))
    rw_ref[...] = jnp.where(lane == 0, w1, w2)


def _hg_out_kernel(of_ref, ob_ref, g_ref, on_ref, xp_ref, xs_ref, *rest):
    o = of_ref[...] + ob_ref[...]
    parts = []
    for h in range(HG_HEADS):
        oh = o[:, h * HG_DK:(h + 1) * HG_DK]
        parts.append(oh * lax.rsqrt(jnp.mean(oh * oh, axis=-1, keepdims=True) + EPS) * on_ref[...])
    y = jnp.concatenate(parts, axis=1) * _silu(g_ref[...].astype(F32))
    _mixer_tail(y.astype(BF16), _stream_tile(xp_ref, xs_ref), *rest)


def _at_out_kernel(ac_ref, al_ref, x_ref, *rest):
    _mixer_tail(_stream_tile(ac_ref, al_ref), x_ref[...], *rest)


def _mixer_out(kernel_fn, name, mix_inputs, mix_specs, mod, nw2, w_out, w_router, b_router):
    tile = lambda i: (i, 0)
    fixed2 = lambda i: (0, 0)
    blk = pl.BlockSpec((TM, D_MODEL), tile)
    lanes_blk = pl.BlockSpec((TM, ROUTER_LANES), tile)
    return pl.pallas_call(
        kernel_fn,
        out_shape=(
            jax.ShapeDtypeStruct((N_TOK, D_MODEL), F32),
            jax.ShapeDtypeStruct((2, N_TOK, PACK_WORDS), jnp.uint32),
            jax.ShapeDtypeStruct((N_TOK, ROUTER_LANES), jnp.int32),
            jax.ShapeDtypeStruct((N_TOK, ROUTER_LANES), F32),
            jax.ShapeDtypeStruct((1, ROUTER_LANES), F32),
        ),
        grid=(N_TILES,),
        in_specs=list(mix_specs) + [
            pl.BlockSpec(mod.shape, fixed2),
            pl.BlockSpec((1, D_MODEL), fixed2),
            pl.BlockSpec(w_out.shape, fixed2),
            pl.BlockSpec(w_router.shape, lambda i: (0, 0, 0)),
            pl.BlockSpec((1, ROUTER_LANES), fixed2),
        ],
        out_specs=(blk, pl.BlockSpec((2, TM, PACK_WORDS), lambda i: (0, i, 0)), lanes_blk, lanes_blk,
                   pl.BlockSpec((1, ROUTER_LANES), fixed2)),
        scratch_shapes=[pltpu.VMEM((1, ROUTER_LANES), F32)],
        compiler_params=_params(40, "arbitrary"),
        name=name,
    )(*mix_inputs, mod, nw2, w_out, w_router, b_router)


def _moe_plan(route_i, counts):
    cnt = counts[0, :N_EXPERTS].astype(jnp.int32)
    padded = ((cnt + MOE_TILE - 1) // MOE_TILE) * MOE_TILE
    ends = jnp.cumsum(padded)
    offs = ends - padded
    experts = route_i[:, 0:2]
    pos = jnp.sum(jnp.where(experts[:, :, None] == jnp.arange(N_EXPERTS)[None, None, :], offs[None, None, :], 0),
                  axis=-1) + route_i[:, 2:4]
    tile_start = jnp.arange(MOE_ROWS // MOE_TILE, dtype=jnp.int32) * MOE_TILE
    tile_expert = jnp.minimum(jnp.sum(ends[None, :] <= tile_start[:, None], axis=1), N_EXPERTS - 1).astype(jnp.int32)
    of_tile = tile_expert[:, None] == jnp.arange(N_EXPERTS)[None, :]
    tile_offs = jnp.sum(jnp.where(of_tile, offs[None, :], 0), axis=1)
    tile_cnt = jnp.sum(jnp.where(of_tile, cnt[None, :], 0), axis=1)
    tile_rows = jnp.clip(tile_offs + tile_cnt - tile_start, 0, MOE_TILE).astype(jnp.int32)
    tile_first = (tile_start == tile_offs).astype(jnp.int32)
    n_active = (ends[-1] // MOE_TILE).astype(jnp.int32).reshape(1)
    return pos.astype(jnp.int32), tile_expert, tile_rows, tile_first, n_active


SC_WINDOW = 128


def _sc_mesh():
    return plsc.VectorSubcoreMesh(core_axis_name="c", subcore_axis_name="s")


def _sc_scatter_rows(x, idx_a, idx_b, n_out_rows):
    n = x.shape[0]

    @functools.partial(pl.kernel, out_type=jax.ShapeDtypeStruct((n_out_rows, PACK_WORDS), x.dtype), mesh=_sc_mesh(),
                       scratch_types=[pltpu.SemaphoreType.DMA, pltpu.SemaphoreType.DMA])
    def scatter(x_hbm, ia_hbm, ib_hbm, o_hbm, sem_a, sem_b):
        def body(x_vmem, ia_vmem, ib_vmem):
            copy_a = pltpu.async_copy(x_vmem, o_hbm.at[ia_vmem.at[0]], sem_a)
            copy_b = pltpu.async_copy(x_vmem, o_hbm.at[ib_vmem.at[0]], sem_b)
            copy_a.wait()
            copy_b.wait()

        idx_spec = pl.BlockSpec((1, SC_WINDOW), index_map=lambda i: (0, i))
        pltpu.emit_pipeline(
            body, grid=(n // SC_WINDOW,),
            in_specs=[pl.BlockSpec((SC_WINDOW, PACK_WORDS), index_map=lambda i: (i, 0)), idx_spec, idx_spec],
            out_specs=[],
            core_axis_name=("c", "s"), dimension_semantics=(pltpu.PARALLEL,),
        )(x_hbm, ia_hbm, ib_hbm)

    return scatter(x, idx_a.reshape(1, n), idx_b.reshape(1, n))


def _sc_gather_rows(table, idx):
    n = idx.shape[0]

    @functools.partial(pl.kernel, out_type=jax.ShapeDtypeStruct((n, PACK_WORDS), table.dtype), mesh=_sc_mesh())
    def gather(t_hbm, i_hbm, o_hbm):
        def body(i_vmem, o_vmem):
            pltpu.sync_copy(t_hbm.at[i_vmem.at[0]], o_vmem)

        pltpu.emit_pipeline(
            body, grid=(n // SC_WINDOW,),
            in_specs=[pl.BlockSpec((1, SC_WINDOW), index_map=lambda i: (0, i))],
            out_specs=[pl.BlockSpec((SC_WINDOW, PACK_WORDS), index_map=lambda i: (i, 0))],
            core_axis_name=("c", "s"), dimension_semantics=(pltpu.PARALLEL,),
        )(i_hbm, o_hbm)

    return gather(table, idx.reshape(1, n))


def _ffn_kernel(te_ref, tr_ref, tf_ref, na_ref, xs_ref, w1_ref, w3_ref, w2_ref, ys_ref, w1_scr, w3_scr, w2_scr):
    t = pl.program_id(0)

    @pl.when(t < na_ref[0])
    def _():
        @pl.when(tf_ref[t] == 1)
        def _():
            w1_scr[...] = w1_ref[0, 0].astype(BF16)
            w3_scr[...] = w3_ref[0, 0].astype(BF16)
            w2_scr[...] = w2_ref[0, 0].astype(BF16)

        row = lax.broadcasted_iota(jnp.int32, (MOE_TILE, PACK_WORDS), 0)
        live = row < tr_ref[t]
        halves = [jnp.where(live, xs_ref[h], jnp.zeros((MOE_TILE, PACK_WORDS), jnp.uint32)) for h in range(2)]
        chunks = _unpack_rows(*halves)

        def up(w_scr):
            acc = jnp.dot(chunks[0], w_scr[:PACK_WORDS, :], preferred_element_type=F32)
            for k in range(1, len(chunks)):
                acc = acc + jnp.dot(chunks[k], w_scr[k * PACK_WORDS:(k + 1) * PACK_WORDS, :],
                                    preferred_element_type=F32)
            return acc

        hid = (_silu(up(w1_scr)) * up(w3_scr)).astype(BF16)
        for h, words in enumerate(_pack_rows(jnp.dot(hid, w2_scr[...], preferred_element_type=F32))):
            ys_ref[h] = words


def _ffn(xs, tile_expert, tile_rows, tile_first, n_active, layer, w1, w3, w2):
    row_tile = lambda t, te, tr, tf, na: (0, jnp.minimum(t, na[0] - 1), 0)
    expert = lambda t, te, tr, tf, na: (layer, te[t], 0, 0)
    return pl.pallas_call(
        _ffn_kernel,
        out_shape=jax.ShapeDtypeStruct((2, MOE_ROWS, PACK_WORDS), jnp.uint32),
        grid_spec=pltpu.PrefetchScalarGridSpec(
            num_scalar_prefetch=4,
            grid=(MOE_ROWS // MOE_TILE,),
            in_specs=[
                pl.BlockSpec((2, MOE_TILE, PACK_WORDS), row_tile),
                pl.BlockSpec((1, 1, D_MODEL, D_EXPERT), expert),
                pl.BlockSpec((1, 1, D_MODEL, D_EXPERT), expert),
                pl.BlockSpec((1, 1, D_EXPERT, D_MODEL), expert),
            ],
            out_specs=pl.BlockSpec((2, MOE_TILE, PACK_WORDS), row_tile),
            scratch_shapes=[
                pltpu.VMEM((D_MODEL, D_EXPERT), BF16),
                pltpu.VMEM((D_MODEL, D_EXPERT), BF16),
                pltpu.VMEM((D_EXPERT, D_MODEL), BF16),
            ],
        ),
        compiler_params=_params(32, "arbitrary"),
        name="moe_experts",
    )(tile_expert, tile_rows, tile_first, n_active, xs, w1, w3, w2)


def _moe_res(x_ref, y_ref, rw_ref, mod_ref):
    cond = _tile_cond(pl.program_id(0))
    rw = rw_ref[...]
    wa = rw[:, 0:1]
    wb = rw[:, 1:2]
    ya = _unpack_rows(y_ref[0], y_ref[1])
    yb = _unpack_rows(y_ref[2], y_ref[3])
    y = jnp.concatenate([wa * a.astype(F32) + wb * b.astype(F32) for a, b in zip(ya, yb)], axis=1)
    return x_ref[...] + _mod_row(mod_ref, cond, 5) * y


def _moe_res_kernel(x_ref, y_ref, rw_ref, mod_ref, o_ref):
    o_ref[...] = _moe_res(x_ref, y_ref, rw_ref, mod_ref)


def _moe_res_final_kernel(x_ref, y_ref, rw_ref, mod_ref, fn_ref, op_ref, os_ref):
    x = _moe_res(x_ref, y_ref, rw_ref, mod_ref)
    y = x * lax.rsqrt(jnp.mean(x * x, axis=-1, keepdims=True) + EPS) * fn_ref[...]
    is_prompt = pl.program_id(0) < PROMPT_TILES

    @pl.when(is_prompt)
    def _():
        op_ref[...] = y

    @pl.when(jnp.logical_not(is_prompt))
    def _():
        os_ref[...] = y


def _moe_combine(x1, y_pairs, route_w, mod, final_norm=None):
    tile = lambda i: (i, 0)
    fixed2 = lambda i: (0, 0)
    blk = pl.BlockSpec((TM, D_MODEL), tile)
    specs = [blk,
             pl.BlockSpec((4, TM, PACK_WORDS), lambda i: (0, i, 0)),
             pl.BlockSpec((TM, ROUTER_LANES), tile),
             pl.BlockSpec(mod.shape, fixed2)]
    args = [x1, y_pairs, route_w, mod]
    if final_norm is None:
        kernel_fn = _moe_res_kernel
        out_shape = jax.ShapeDtypeStruct((N_TOK, D_MODEL), F32)
        out_specs = blk
    else:
        kernel_fn = _moe_res_final_kernel
        specs.append(pl.BlockSpec((1, D_MODEL), fixed2))
        args.append(final_norm)
        out_shape = (jax.ShapeDtypeStruct((N_PROMPT, D_MODEL), F32),
                     jax.ShapeDtypeStruct((N_TOK - N_PROMPT, D_MODEL), F32))
        out_specs = (pl.BlockSpec((TM, D_MODEL), _prompt_tile), pl.BlockSpec((TM, D_MODEL), _sample_tile))
    return pl.pallas_call(
        kernel_fn,
        out_shape=out_shape,
        grid=(N_TILES,),
        in_specs=specs,
        out_specs=out_specs,
        compiler_params=_params(32, "arbitrary"),
        name="moe_combine",
    )(*args)


def _moe_layer(x1, h2p, route_i, route_w, counts, mod, layer, w1, w3, w2, final_norm=None):
    pos, tile_expert, tile_rows, tile_first, n_active = _moe_plan(route_i, counts)
    idx_a = jnp.concatenate([pos[:, 0], pos[:, 0] + MOE_ROWS])
    idx_b = jnp.concatenate([pos[:, 1], pos[:, 1] + MOE_ROWS])
    xs = _sc_scatter_rows(h2p.reshape(2 * N_TOK, PACK_WORDS), idx_a, idx_b, 2 * MOE_ROWS)
    ys = _ffn(xs.reshape(2, MOE_ROWS, PACK_WORDS), tile_expert, tile_rows, tile_first, n_active,
              layer, w1, w3, w2)
    y_pairs = _sc_gather_rows(ys.reshape(2 * MOE_ROWS, PACK_WORDS), jnp.concatenate([idx_a, idx_b]))
    return _moe_combine(x1, y_pairs.reshape(4, N_TOK, PACK_WORDS), route_w, mod, final_norm)


def _swap_rotary_halves(x):
    n = x.shape[-1]
    lane = lax.broadcasted_iota(jnp.int32, x.shape, 1)
    quarter = ROPE_HALF // 2
    return jnp.where((lane % ROPE_HALF) < quarter, pltpu.roll(x, n - quarter, 1), pltpu.roll(x, quarter, 1))


def _at_in_kernel(x_ref, mod_ref, nw_ref, w_ref, cos_ref, sin_ref, q_ref, kt_ref, v_ref, kc_ref, vc_ref):
    i = pl.program_id(0)
    cond = _tile_cond(i)
    h = _norm_mod(x_ref[...], nw_ref[...], _mod_row(mod_ref, cond, 0), _mod_row(mod_ref, cond, 1)).astype(BF16)
    nq = ATT_Q_HEADS * ATT_HEAD_DIM
    nk = ATT_KV_HEADS * ATT_HEAD_DIM
    cos = cos_ref[...]
    sin = sin_ref[...]

    def rope(x):
        reps = x.shape[-1] // LANES
        return x * jnp.concatenate([cos] * reps, axis=1) + _swap_rotary_halves(x) * jnp.concatenate([sin] * reps, axis=1)

    q_ref[...] = (rope(jnp.dot(h, w_ref[:, :nq], preferred_element_type=F32)) * ATT_HEAD_DIM ** -0.5).astype(BF16)
    k = rope(jnp.dot(h, w_ref[:, nq:nq + nk], preferred_element_type=F32))
    v = jnp.dot(h, w_ref[:, nq + nk:], preferred_element_type=F32)
    kt = k.T
    kt_ref[...] = kt.astype(BF16)
    v_ref[...] = v.astype(BF16)

    @pl.when(i < PROMPT_TILES)
    def _():
        kc_ref[...] = kt
        vc_ref[...] = v.T


def _rope_tables():
    f32 = np.float32
    pos = np.arange(DEC_SEQ)
    t_row = (pos // GRID_W).astype(f32)
    t_col = (pos % GRID_W).astype(f32)
    inv = f32(ROPE_BASE) ** (-np.arange(0, ROPE_HALF, 2, dtype=f32) / f32(ROPE_HALF))
    j = np.arange(LANES) % ATT_HEAD_DIM
    freq = inv[(j % ROPE_HALF) % (ROPE_HALF // 2)]
    ang = (np.where((j < ROPE_HALF)[None, :], t_row[:, None], t_col[:, None]) * freq[None, :]).astype(f32)
    sign = np.where((j % ROPE_HALF) < ROPE_HALF // 2, -1.0, 1.0).astype(f32)
    cos = np.concatenate([np.ones((TM, LANES), f32), np.cos(ang)], axis=0)
    sin = np.concatenate([np.zeros((TM, LANES), f32), np.sin(ang) * sign[None, :]], axis=0)
    return jnp.asarray(cos, F32), jnp.asarray(sin, F32)


def _at_in(x, mod, nw, w_in, cos, sin):
    tile = lambda i: (i, 0)
    fixed2 = lambda i: (0, 0)
    rope_tile = lambda i: (jnp.where(i < PROMPT_TILES, 0, 1 + (i - PROMPT_TILES) % TILES_PER_DEC_SEQ), 0)
    nk = ATT_KV_HEADS * ATT_HEAD_DIM
    return pl.pallas_call(
        _at_in_kernel,
        out_shape=(
            jax.ShapeDtypeStruct((N_TOK, D_MODEL), BF16),
            jax.ShapeDtypeStruct((nk, N_TOK), BF16),
            jax.ShapeDtypeStruct((N_TOK, nk), BF16),
            jax.ShapeDtypeStruct((N_PROMPT, nk), F32),
            jax.ShapeDtypeStruct((N_PROMPT, nk), F32),
        ),
        grid=(N_TILES,),
        in_specs=[
            pl.BlockSpec((TM, D_MODEL), tile),
            pl.BlockSpec(mod.shape, fixed2),
            pl.BlockSpec((1, D_MODEL), fixed2),
            pl.BlockSpec(w_in.shape, fixed2),
            pl.BlockSpec((TM, LANES), rope_tile),
            pl.BlockSpec((TM, LANES), rope_tile),
        ],
        out_specs=(pl.BlockSpec((TM, D_MODEL), tile), pl.BlockSpec((nk, TM), lambda i: (0, i)),
                   pl.BlockSpec((TM, nk), tile),
                   pl.BlockSpec((TM, nk), _prompt_tile), pl.BlockSpec((TM, nk), _prompt_tile)),
        compiler_params=_params(40, "arbitrary"),
        name="attn_in_proj",
    )(x, mod, nw, w_in, cos, sin)


def _attend(q, kt_all, v_all, mask, sink_ref, o_ref):
    nq = q.shape[0]
    group_lanes = ATT_GROUP * ATT_HEAD_DIM
    lane = lax.broadcasted_iota(jnp.int32, (nq, group_lanes), 1)
    mine = [(lane // ATT_HEAD_DIM) == g for g in range(ATT_GROUP)]
    row_head = lax.broadcasted_iota(jnp.int32, (ATT_GROUP * nq, 1), 0) // nq
    if mask is not None:
        mask = jnp.concatenate([mask] * ATT_GROUP, axis=0)
    for hk in range(ATT_KV_HEADS):
        vh = v_all[:, hk * ATT_HEAD_DIM:(hk + 1) * ATT_HEAD_DIM]
        kt = jnp.concatenate([kt_all[hk * ATT_HEAD_DIM:(hk + 1) * ATT_HEAD_DIM, :]] * ATT_GROUP, axis=0)
        vt = jnp.concatenate([vh] * ATT_GROUP, axis=1)
        qg = q[:, hk * group_lanes:(hk + 1) * group_lanes]
        q_stack = jnp.concatenate([jnp.where(mine[g], qg, jnp.zeros_like(qg)) for g in range(ATT_GROUP)], axis=0)
        s = jnp.dot(q_stack, kt, preferred_element_type=F32)
        if mask is not None:
            s = jnp.where(mask, s, -jnp.inf)
        sink = jnp.zeros((ATT_GROUP * nq, 1), F32)
        for g in range(ATT_GROUP):
            sink = jnp.where(row_head == g, sink_ref[hk * ATT_GROUP + g], sink)
        m = jnp.maximum(jnp.max(s, axis=-1, keepdims=True), sink)
        p = jnp.exp(s - m)
        denom = jnp.sum(p, axis=-1, keepdims=True) + jnp.exp(sink - m)
        o = jnp.dot(p.astype(BF16), vt, preferred_element_type=F32) / denom
        acc = jnp.where(mine[0], o[:nq], 0.0)
        for g in range(1, ATT_GROUP):
            acc = acc + jnp.where(mine[g], o[g * nq:(g + 1) * nq], 0.0)
        o_ref[:, hk * group_lanes:(hk + 1) * group_lanes] = acc.astype(BF16)


def _ctx_attn_kernel(sink_ref, q_ref, k_ref, v_ref, o_ref):
    _attend(q_ref[...], k_ref[...], v_ref[...], None, sink_ref, o_ref)


def _lat_attn_kernel(sink_ref, q_ref, kp_ref, kc_ref, kn_ref, vp_ref, vc_ref, vn_ref, ck_ref, cv_ref, o_ref):
    jb = pl.program_id(1)
    kt_all = jnp.concatenate([kp_ref[...], kc_ref[...], kn_ref[...], ck_ref[0].astype(BF16)], axis=1)
    v_all = jnp.concatenate([vp_ref[...], vc_ref[...], vn_ref[...], cv_ref[0].astype(BF16)], axis=0)
    nkeys = 3 * BLOCK + PAST_LEN
    qi = lax.broadcasted_iota(jnp.int32, (BLOCK, nkeys), 0)
    kj = lax.broadcasted_iota(jnp.int32, (BLOCK, nkeys), 1)
    qpos = jb * BLOCK + qi
    kpos = (jb - 1) * BLOCK + kj
    local_ok = (jnp.abs(qpos - kpos) <= WINDOW) & (kpos >= 0) & (kpos < DEC_SEQ)
    mask = (kj >= 3 * BLOCK) | local_ok
    _attend(q_ref[...], kt_all, v_all, mask, sink_ref, o_ref)


def _attention(q, kt, v, cache_kt, cache_v, sink):
    nk = ATT_KV_HEADS * ATT_HEAD_DIM
    smem = pl.BlockSpec(memory_space=pltpu.SMEM)
    ctx = pl.pallas_call(
        _ctx_attn_kernel,
        out_shape=jax.ShapeDtypeStruct((N_PROMPT, D_MODEL), BF16),
        grid=(BATCH,),
        in_specs=[
            smem,
            pl.BlockSpec((SEQ, D_MODEL), lambda b: (b, 0)),
            pl.BlockSpec((nk, SEQ), lambda b: (0, b)),
            pl.BlockSpec((SEQ, nk), lambda b: (b, 0)),
        ],
        out_specs=pl.BlockSpec((SEQ, D_MODEL), lambda b: (b, 0)),
        compiler_params=_params(40, "arbitrary"),
        name="context_attention",
    )(sink, q, kt, v)

    nb = DEC_SEQ // BLOCK
    base = N_PROMPT // BLOCK
    cur = lambda b, j: (base + b * nb + j, 0)
    prev = lambda b, j: (base + b * nb + jnp.maximum(j - 1, 0), 0)
    nxt = lambda b, j: (base + b * nb + jnp.minimum(j + 1, nb - 1), 0)
    kv_blk = lambda f: pl.BlockSpec((BLOCK, nk), f)
    kt_blk = lambda f: pl.BlockSpec((nk, BLOCK), lambda b, j: f(b, j)[::-1])
    cache_blk = pl.BlockSpec((1, PAST_LEN, nk), lambda b, j: (b, 0, 0))
    cache_kt_blk = pl.BlockSpec((1, nk, PAST_LEN), lambda b, j: (b, 0, 0))
    lat = pl.pallas_call(
        _lat_attn_kernel,
        out_shape=jax.ShapeDtypeStruct((DEC_BATCH * DEC_SEQ, D_MODEL), BF16),
        grid=(DEC_BATCH, nb),
        in_specs=[
            smem,
            pl.BlockSpec((BLOCK, D_MODEL), cur),
            kt_blk(prev), kt_blk(cur), kt_blk(nxt),
            kv_blk(prev), kv_blk(cur), kv_blk(nxt),
            cache_kt_blk, cache_blk,
        ],
        out_specs=pl.BlockSpec((BLOCK, D_MODEL), lambda b, j: (b * nb + j, 0)),
        compiler_params=_params(40, "arbitrary", "arbitrary"),
        name="latent_attention",
    )(sink, q, kt, kt, kt, v, v, v, cache_kt, cache_v)
    return ctx, lat


def kernel(x_prompt, x_sample, state_hgrn, cache_k, cache_v, c, c_ctx, ada_w, ada_b, norm_w, hg_w_in,
           hg_lb_logits, hg_onorm, hg_w_out, at_w_in, at_sink, at_w_out, moe_w_group, moe_b_group,
           moe_w_expert, moe_b_expert, moe_w1, moe_w3, moe_w2, final_norm):
    xp = x_prompt.reshape(N_PROMPT, D_MODEL)
    xs = x_sample.reshape(N_TOK - N_PROMPT, D_MODEL)
    cond = jnp.concatenate([c_ctx[None, :], c], axis=0)
    mod = _ada(cond, ada_w, ada_b)
    nk = ATT_KV_HEADS * ATT_HEAD_DIM

    def router_params(i):
        pad = jnp.zeros((D_MODEL, ROUTER_LANES - N_EXPERTS - N_GROUPS), F32)
        w = jnp.concatenate([moe_w_expert[i], moe_w_group[i], pad], axis=1)
        b = jnp.concatenate([moe_b_expert[i], moe_b_group[i], pad[0]])[None, :]
        hi = w.astype(BF16)
        lo = (w - hi.astype(F32)).astype(BF16)
        return jnp.stack([hi, lo]), b

    tile = lambda i: (i, 0)
    blk = pl.BlockSpec((TM, D_MODEL), tile)

    prompt_blk = pl.BlockSpec((TM, D_MODEL), _prompt_tile)
    sample_blk = pl.BlockSpec((TM, D_MODEL), _sample_tile)
    q, v, g, lff, kf, lfb, kb = _hg_in(xp, xs, mod[0], norm_w[0, 0][None, :], hg_w_in[0].astype(BF16), hg_lb_logits)
    o_f, o_b, state_new = _gla(q, v, lff, kf, lfb, kb, state_hgrn)
    wr, br = router_params(0)
    routed = _mixer_out(
        _hg_out_kernel, "hgrn_out_route", (o_f, o_b, g, hg_onorm[0][None, :], xp, xs),
        (blk, blk, blk, pl.BlockSpec((1, HG_DK), lambda i: (0, 0)), prompt_blk, sample_blk),
        mod[0], norm_w[0, 1][None, :], hg_w_out[0].astype(BF16), wr, br)
    x = _moe_layer(*routed, mod[0], 0, moe_w1, moe_w3, moe_w2)

    cos, sin = _rope_tables()
    qa, ka, va, k_ctx, v_ctx = _at_in(x, mod[1], norm_w[1, 0][None, :], at_w_in[0].astype(BF16), cos, sin)
    attn_ctx, attn_lat = _attention(qa, ka, va, cache_k[:, 0].reshape(DEC_BATCH, PAST_LEN, nk).transpose(0, 2, 1),
                                    cache_v[:, 0].reshape(DEC_BATCH, PAST_LEN, nk), at_sink[0])
    wr, br = router_params(1)
    routed = _mixer_out(
        _at_out_kernel, "attn_out_route", (attn_ctx, attn_lat, x), (prompt_blk, sample_blk, blk),
        mod[1], norm_w[1, 1][None, :], at_w_out[0].astype(BF16), wr, br)
    y_prompt, y_sample = _moe_layer(*routed, mod[1], 1, moe_w1, moe_w3, moe_w2, final_norm[None, :])

    def cache(feature_major):
        return feature_major.reshape(BATCH, 1, ATT_KV_HEADS, ATT_HEAD_DIM, SEQ).transpose(0, 1, 4, 2, 3)

    return (y_prompt.reshape(BATCH, SEQ, D_MODEL), y_sample.reshape(DEC_BATCH, DEC_SEQ, D_MODEL), state_new,
            cache(k_ctx), cache(v_ctx))
```

```python
import functools

import jax
import jax.numpy as jnp
import numpy as np
from jax import lax
from jax.experimental import pallas as pl
from jax.experimental.pallas import tpu as pltpu
from jax.experimental.pallas import tpu_sc as plsc

F32 = jnp.float32
BF16 = jnp.bfloat16

D_MODEL = 1024
BATCH = 16
SEQ = 256
DEC_BATCH = 2
DEC_SEQ = 1024
PAST_LEN = 512
GRID_W = 64
HG_HEADS = 8
HG_DK = 128
CHUNK = 16
ATT_HEAD_DIM = 64
ATT_Q_HEADS = 16
ATT_KV_HEADS = 4
ATT_GROUP = 4
WINDOW = 128
BLOCK = 128
ROPE_HALF = 32
ROPE_BASE = 10000.0
N_GROUPS = 4
EXPERTS_PER_GROUP = 8
N_EXPERTS = 32
D_EXPERT = 256
EPS = 1e-6

N_PROMPT = BATCH * SEQ
N_TOK = N_PROMPT + DEC_BATCH * DEC_SEQ
TM = 256
N_TILES = N_TOK // TM
PROMPT_TILES = N_PROMPT // TM
TILES_PER_DEC_SEQ = DEC_SEQ // TM
LANES = 128
N_COND_USED = 1 + DEC_BATCH
N_COND = 8
ROUTER_LANES = 128
GLA_UNROLL = 8
DECAY_CLAMP = 60.0
PACK_WORDS = D_MODEL // 4
MOE_TILE = 256
MOE_ROWS = 2 * N_TOK + N_EXPERTS * MOE_TILE
MIB = 1024 * 1024


def _params(vmem_mib, *semantics):
    return pltpu.CompilerParams(dimension_semantics=semantics, vmem_limit_bytes=vmem_mib * MIB)


def _tile_cond(i):
    return jnp.where(i < PROMPT_TILES, 0, 1 + (i - PROMPT_TILES) // TILES_PER_DEC_SEQ)


def _prompt_tile(i):
    return (jnp.minimum(i, PROMPT_TILES - 1), 0)


def _sample_tile(i):
    return (jnp.maximum(i - PROMPT_TILES, 0), 0)


def _stream_tile(prompt_ref, sample_ref):
    return jnp.where(pl.program_id(0) < PROMPT_TILES, prompt_ref[...], sample_ref[...])


def _mod_row(mod_ref, cond, which):
    return mod_ref[pl.ds(cond, 1), which * D_MODEL:(which + 1) * D_MODEL]


def _norm_mod(x, nw, shift, scale):
    y = x * lax.rsqrt(jnp.mean(x * x, axis=-1, keepdims=True) + EPS)
    return (y * nw) * (1.0 + scale) + shift


def _silu(x):
    return x * jax.nn.sigmoid(x)


def _ada_kernel(c_ref, w_ref, b_ref, o_ref):
    s = [_silu(c_ref[r]) for r in range(N_COND_USED)]
    tn = w_ref.shape[-1]
    rows = []
    for r in range(N_COND_USED):
        cols = [jnp.sum(w_ref[0, :, j * LANES:(j + 1) * LANES] * s[r], axis=0, keepdims=True)
                for j in range(tn // LANES)]
        rows.append(jnp.concatenate(cols, axis=1) + b_ref[0])
    rows.append(jnp.zeros((N_COND - N_COND_USED, tn), F32))
    o_ref[0] = jnp.concatenate(rows, axis=0)


def _ada(cond, ada_w, ada_b):
    depth, _, n = ada_w.shape
    tn = 1536
    cond_cols = jnp.broadcast_to(cond[:, :, None], (N_COND_USED, D_MODEL, LANES))
    return pl.pallas_call(
        _ada_kernel,
        out_shape=jax.ShapeDtypeStruct((depth, N_COND, n), F32),
        grid=(depth, n // tn),
        in_specs=[
            pl.BlockSpec((N_COND_USED, D_MODEL, LANES), lambda l, j: (0, 0, 0)),
            pl.BlockSpec((1, D_MODEL, tn), lambda l, j: (l, 0, j)),
            pl.BlockSpec((1, 1, tn), lambda l, j: (l, 0, j)),
        ],
        out_specs=pl.BlockSpec((1, N_COND, tn), lambda l, j: (l, 0, j)),
        compiler_params=_params(40, "arbitrary", "arbitrary"),
        name="ada_modulation",
    )(cond_cols, ada_w, ada_b.reshape(depth, 1, n))


def _hg_in_kernel(xp_ref, xs_ref, mod_ref, nw_ref, w_ref, lbl_ref,
                  q_ref, v_ref, g_ref, lff_ref, kf_ref, lfb_ref, kb_ref):
    cond = _tile_cond(pl.program_id(0))
    x = _stream_tile(xp_ref, xs_ref)
    h = _norm_mod(x, nw_ref[...], _mod_row(mod_ref, cond, 0), _mod_row(mod_ref, cond, 1)).astype(BF16)

    def proj(c):
        return jnp.dot(h, w_ref[:, c * D_MODEL:(c + 1) * D_MODEL], preferred_element_type=F32)

    l0, l1, l2 = lbl_ref[0], lbl_ref[1], lbl_ref[2]
    m = jnp.maximum(jnp.maximum(l0, l1), l2)
    e0, e1, e2 = jnp.exp(l0 - m), jnp.exp(l1 - m), jnp.exp(l2 - m)
    lb = e0 / (e0 + e1 + e2)

    q_ref[...] = proj(0).astype(BF16)
    v_ref[...] = proj(1).astype(BF16)
    for d, (lf_ref, k_ref) in enumerate(((lff_ref, kf_ref), (lfb_ref, kb_ref))):
        lbd = lb[d:d + 1, :]
        f = lbd + (1.0 - lbd) * jax.nn.sigmoid(proj(2 + d))
        lf = jnp.log(f)
        hi = lf.astype(BF16)
        lf_ref[0] = hi
        lf_ref[1] = (lf - hi.astype(F32)).astype(BF16)
        k_ref[...] = (1.0 - f).astype(BF16)
    g_ref[...] = proj(4).astype(BF16)


def _hg_in(xp, xs, mod, nw, w_in, lb_logits):
    tile = lambda i: (i, 0)
    fixed2 = lambda i: (0, 0)
    bf = jax.ShapeDtypeStruct((N_TOK, D_MODEL), BF16)
    ff = jax.ShapeDtypeStruct((2, N_TOK, D_MODEL), BF16)
    blk = pl.BlockSpec((TM, D_MODEL), tile)
    split_blk = pl.BlockSpec((2, TM, D_MODEL), lambda i: (0, i, 0))
    return pl.pallas_call(
        _hg_in_kernel,
        out_shape=(bf, bf, bf, ff, bf, ff, bf),
        grid=(N_TILES,),
        in_specs=[
            pl.BlockSpec((TM, D_MODEL), _prompt_tile),
            pl.BlockSpec((TM, D_MODEL), _sample_tile),
            pl.BlockSpec(mod.shape, fixed2),
            pl.BlockSpec((1, D_MODEL), fixed2),
            pl.BlockSpec(w_in.shape, fixed2),
            pl.BlockSpec(lb_logits.shape, lambda i: (0, 0, 0)),
        ],
        out_specs=(blk, blk, blk, split_blk, blk, split_blk, blk),
        compiler_params=_params(56, "arbitrary"),
        name="hgrn_in_proj",
    )(xp, xs, mod, nw, w_in, lb_logits)


def _gla_exact(reverse, q_ref, v_ref, lf_ref, k_ref, st_ref, o_ref):
    nt = (((1,), (1,)), ((), ()))
    tn = (((0,), (0,)), ((), ()))
    n_chunks = TM // CHUNK
    row_id = lax.broadcasted_iota(jnp.int32, (CHUNK, HG_DK), 0)
    order = range(CHUNK - 1, -1, -1) if reverse else range(CHUNK)

    def chunk_head(it, carry):
        step, h = it // HG_HEADS, it % HG_HEADS
        ci = (n_chunks - 1 - step) if reverse else step
        rows = pl.ds(pl.multiple_of(ci * CHUNK, CHUNK), CHUNK)
        cols = pl.ds(pl.multiple_of(h * HG_DK, HG_DK), HG_DK)
        q, k, v = q_ref[rows, cols], k_ref[rows, cols], v_ref[rows, cols]
        f = jnp.exp(lf_ref[0, rows, cols].astype(F32) + lf_ref[1, rows, cols].astype(F32))
        st = st_ref[0, h]
        o = jnp.zeros((CHUNK, HG_DK), F32)
        for t in order:
            one = row_id == t
            v_t = jnp.where(one, v, jnp.zeros_like(v))
            st = st * f[t:t + 1, :] + lax.dot_general(v_t, k, tn, preferred_element_type=F32)
            o = jnp.where(one, lax.dot_general(q, st.astype(BF16), nt, preferred_element_type=F32), o)
        st_ref[0, h] = st
        o_ref[rows, cols] = o
        return carry

    lax.fori_loop(0, n_chunks * HG_HEADS, chunk_head, 0)


def _gla_direction(reverse, q_ref, v_ref, lf_ref, k_ref, st_ref, sw_ref, o_ref,
                   ut_scr, qd_scr, kd_scr, ki_scr, cd_scr):
    r = lax.broadcasted_iota(jnp.int32, (TM, TM), 0)
    c = lax.broadcasted_iota(jnp.int32, (TM, TM), 1)
    same = (r // CHUNK) == (c // CHUNK)
    tri = (same & ((c >= r) if reverse else (c <= r))).astype(BF16)
    b = jnp.dot(tri, lf_ref[0], preferred_element_type=F32) + jnp.dot(tri, lf_ref[1], preferred_element_type=F32)
    b3 = b.reshape(TM // CHUNK, CHUNK, D_MODEL)
    edge = 0 if reverse else CHUNK - 1
    tot = jnp.broadcast_to(b3[:, edge:edge + 1, :], b3.shape).reshape(TM, D_MODEL)
    k = k_ref[...]
    qd_scr[...] = q_ref[...] * jnp.exp(b).astype(BF16)
    kd_scr[...] = k * jnp.exp(tot - b).astype(BF16)
    ki_scr[...] = k * jnp.exp(jnp.minimum(-b, DECAY_CLAMP)).astype(BF16)
    cd_scr[...] = jnp.exp(tot)

    tr = lax.broadcasted_iota(jnp.int32, (CHUNK, CHUNK), 0)
    tc = lax.broadcasted_iota(jnp.int32, (CHUNK, CHUNK), 1)
    keep = (tc >= tr) if reverse else (tc <= tr)
    nt = (((1,), (1,)), ((), ()))
    tn = (((0,), (0,)), ((), ()))

    decay_ok = jnp.max(-b3[:, edge:edge + 1, :]) <= DECAY_CLAMP

    @pl.when(jnp.logical_not(decay_ok))
    def _():
        _gla_exact(reverse, q_ref, v_ref, lf_ref, k_ref, st_ref, o_ref)

    @pl.when(decay_ok)
    def _():
        heads = [slice(h * HG_DK, (h + 1) * HG_DK) for h in range(HG_HEADS)]
        for h in range(HG_HEADS):
            sw_ref[0, h] = st_ref[0, h].T.astype(BF16)

        n_chunks = TM // CHUNK

        def chunk_rows(step):
            ci = (n_chunks - 1 - step) if reverse else step
            row0 = pl.multiple_of(ci * CHUNK, CHUNK)
            return row0, pl.ds(row0, CHUNK)

        def key_value_product(rows):
            for h, cols in enumerate(heads):
                ut_scr[h] = lax.dot_general(v_ref[rows, cols], kd_scr[rows, cols], tn, preferred_element_type=F32)

        key_value_product(chunk_rows(0)[1])

        def chunk(step, src, dst):
            row0, rows = chunk_rows(step)
            cd_row = cd_scr[pl.ds(row0, 1), :]
            for h, cols in enumerate(heads):
                new = st_ref[src, h] * cd_row[:, cols] + ut_scr[h]
                st_ref[dst, h] = new
                sw_ref[dst, h] = new.T.astype(BF16)
            a = [lax.dot_general(qd_scr[rows, cols], ki_scr[rows, cols], nt, preferred_element_type=F32)
                 for cols in heads]
            inter = [jnp.dot(qd_scr[rows, cols], sw_ref[src, h], preferred_element_type=F32)
                     for h, cols in enumerate(heads)]
            key_value_product(chunk_rows(jnp.minimum(step + 1, n_chunks - 1))[1])
            for h, cols in enumerate(heads):
                am = jnp.where(keep, a[h], 0.0).astype(BF16)
                o_ref[rows, cols] = jnp.dot(am, v_ref[rows, cols], preferred_element_type=F32) + inter[h]

        def chunk_group(it, carry):
            for p in range(GLA_UNROLL):
                chunk(GLA_UNROLL * it + p, p % 2, 1 - p % 2)
            return carry

        lax.fori_loop(0, n_chunks // GLA_UNROLL, chunk_group, 0)


def _gla_kernel(qf_ref, vf_ref, lff_ref, kf_ref, qb_ref, vb_ref, lfb_ref, kb_ref, s0_ref,
                of_ref, ob_ref, sout_ref, stf_scr, stb_scr, sw_scr, ut_scr, qd_scr, kd_scr, ki_scr, cd_scr):
    i = pl.program_id(0)
    is_prompt = i < PROMPT_TILES
    first = jnp.logical_or(is_prompt, (i - PROMPT_TILES) % TILES_PER_DEC_SEQ == 0)

    @pl.when(first)
    def _():
        keep0 = jnp.where(is_prompt, 0.0, 1.0)
        for h in range(HG_HEADS):
            stf_scr[0, h] = s0_ref[0, 0, 0, h].T * keep0
            stb_scr[0, h] = s0_ref[0, 0, 1, h].T * keep0

    _gla_direction(False, qf_ref, vf_ref, lff_ref, kf_ref, stf_scr, sw_scr, of_ref,
                   ut_scr, qd_scr, kd_scr, ki_scr, cd_scr)
    _gla_direction(True, qb_ref, vb_ref, lfb_ref, kb_ref, stb_scr, sw_scr, ob_ref,
                   ut_scr, qd_scr, kd_scr, ki_scr, cd_scr)

    @pl.when(is_prompt)
    def _():
        for h in range(HG_HEADS):
            sout_ref[0, 0, 0, h] = stf_scr[0, h].T
            sout_ref[0, 0, 1, h] = stb_scr[0, h].T


def _gla(q, v, lff, kf, lfb, kb, state_hgrn):
    def fwd_tile(i):
        return (i, 0)

    def bwd_tile(i):
        j = (i - PROMPT_TILES) % TILES_PER_DEC_SEQ
        return (jnp.where(i < PROMPT_TILES, i, i - j + (TILES_PER_DEC_SEQ - 1 - j)), 0)

    def s0_idx(i):
        return (jnp.maximum(i - PROMPT_TILES, 0) // TILES_PER_DEC_SEQ, 0, 0, 0, 0, 0)

    def sout_idx(i):
        return (jnp.minimum(i, PROMPT_TILES - 1), 0, 0, 0, 0, 0)

    f_blk = pl.BlockSpec((TM, D_MODEL), fwd_tile)
    b_blk = pl.BlockSpec((TM, D_MODEL), bwd_tile)
    f_split = pl.BlockSpec((2, TM, D_MODEL), lambda i: (0,) + fwd_tile(i))
    b_split = pl.BlockSpec((2, TM, D_MODEL), lambda i: (0,) + bwd_tile(i))
    st_blk = (1, 1, 2, HG_HEADS, HG_DK, HG_DK)
    return pl.pallas_call(
        _gla_kernel,
        out_shape=(
            jax.ShapeDtypeStruct((N_TOK, D_MODEL), F32),
            jax.ShapeDtypeStruct((N_TOK, D_MODEL), F32),
            jax.ShapeDtypeStruct((BATCH,) + st_blk[1:], F32),
        ),
        grid=(N_TILES,),
        in_specs=[f_blk, f_blk, f_split, f_blk, b_blk, b_blk, b_split, b_blk, pl.BlockSpec(st_blk, s0_idx)],
        out_specs=(f_blk, b_blk, pl.BlockSpec(st_blk, sout_idx)),
        scratch_shapes=[
            pltpu.VMEM((2, HG_HEADS, HG_DK, HG_DK), F32),
            pltpu.VMEM((2, HG_HEADS, HG_DK, HG_DK), F32),
            pltpu.VMEM((2, HG_HEADS, HG_DK, HG_DK), BF16),
            pltpu.VMEM((HG_HEADS, HG_DK, HG_DK), F32),
            pltpu.VMEM((TM, D_MODEL), BF16),
            pltpu.VMEM((TM, D_MODEL), BF16),
            pltpu.VMEM((TM, D_MODEL), BF16),
            pltpu.VMEM((TM, D_MODEL), F32),
        ],
        compiler_params=_params(48, "arbitrary"),
        name="hgrn_recurrence",
    )(q, v, lff, kf, q, v, lfb, kb, state_hgrn)


def _route(logits):
    lane = lax.broadcasted_iota(jnp.int32, logits.shape, 1)
    neg = jnp.float32(-jnp.inf)
    big = jnp.int32(ROUTER_LANES)

    def first_max(x):
        m = jnp.max(x, axis=-1, keepdims=True)
        return m, jnp.min(jnp.where(x == m, lane, big), axis=-1, keepdims=True)

    is_group = (lane >= N_EXPERTS) & (lane < N_EXPERTS + N_GROUPS)
    gl = jnp.where(is_group, logits, neg)
    gmax, g_lane = first_max(gl)
    g_sel = g_lane - N_EXPERTS
    gsum = jnp.sum(jnp.exp(gl - gmax), axis=-1, keepdims=True)
    p_g = 1.0 / gsum
    in_sel = (lane < N_EXPERTS) & ((lane // EXPERTS_PER_GROUP) == g_sel)
    m1, i1 = first_max(jnp.where(in_sel, logits, neg))
    m2, i2 = first_max(jnp.where(in_sel & (lane != i1), logits, neg))
    e2 = jnp.exp(m2 - m1)
    return i1, i2, p_g / (1.0 + e2), p_g * e2 / (1.0 + e2)


def _pack_rows(x):
    q = PACK_WORDS
    bits = pltpu.bitcast(x.astype(BF16).astype(F32), jnp.uint32)
    return [(bits[:, (2 + h) * q:(3 + h) * q] & jnp.uint32(0xFFFF0000)) | (bits[:, h * q:(h + 1) * q] >> 16)
            for h in range(2)]


def _unpack_rows(half0, half1):
    lo = lambda w: pltpu.bitcast(w << 16, F32).astype(BF16)
    hi = lambda w: pltpu.bitcast(w & jnp.uint32(0xFFFF0000), F32).astype(BF16)
    return [lo(half0), lo(half1), hi(half0), hi(half1)]


def _mixer_tail(mix_bf16, x, mod_ref, nw2_ref, wo_ref, wr_ref, br_ref,
                x1_ref, h2p_ref, ri_ref, rw_ref, cnt_ref, carry_scr):
    i = pl.program_id(0)
    cond = _tile_cond(i)
    out = jnp.dot(mix_bf16, wo_ref[...], preferred_element_type=F32)
    x1 = x + _mod_row(mod_ref, cond, 2) * out
    x1_ref[...] = x1
    h2 = _norm_mod(x1, nw2_ref[...], _mod_row(mod_ref, cond, 3), _mod_row(mod_ref, cond, 4))
    for h, words in enumerate(_pack_rows(h2)):
        h2p_ref[h] = words
    h_hi = h2.astype(BF16)
    h_lo = (h2 - h_hi.astype(F32)).astype(BF16)
    logits = (jnp.dot(h_hi, wr_ref[0], preferred_element_type=F32)
              + (jnp.dot(h_hi, wr_ref[1], preferred_element_type=F32)
                 + jnp.dot(h_lo, wr_ref[0], preferred_element_type=F32))) + br_ref[...]
    i1, i2, w1, w2 = _route(logits)

    @pl.when(i == 0)
    def _():
        carry_scr[...] = jnp.zeros_like(carry_scr)

    lane = lax.broadcasted_iota(jnp.int32, logits.shape, 1)
    chosen = ((lane == i1) | (lane == i2)).astype(BF16)
    r = lax.broadcasted_iota(jnp.int32, (TM, TM), 0)
    c = lax.broadcasted_iota(jnp.int32, (TM, TM), 1)
    before = jnp.dot((c < r).astype(BF16), chosen, preferred_element_type=F32) + carry_scr[...]
    r1 = jnp.sum(jnp.where(lane == i1, before, 0.0), axis=-1, keepdims=True).astype(jnp.int32)
    r2 = jnp.sum(jnp.where(lane == i2, before, 0.0), axis=-1, keepdims=True).astype(jnp.int32)
    total = carry_scr[...] + jnp.sum(chosen.astype(F32), axis=0, keepdims=True)
    carry_scr[...] = total
    cnt_ref[...] = total
    ri_ref[...] = jnp.where(lane == 0, i1, jnp.where(lane == 1, i2, jnp.where(lane == 2, r1, r2)))
    rw_ref[...] = jnp.where(lane == 0, w1, w2)


def _hg_out_kernel(of_ref, ob_ref, g_ref, on_ref, xp_ref, xs_ref, *rest):
    o = of_ref[...] + ob_ref[...]
    parts = []
    for h in range(HG_HEADS):
        oh = o[:, h * HG_DK:(h + 1) * HG_DK]
        parts.append(oh * lax.rsqrt(jnp.mean(oh * oh, axis=-1, keepdims=True) + EPS) * on_ref[...])
    y = jnp.concatenate(parts, axis=1) * _silu(g_ref[...].astype(F32))
    _mixer_tail(y.astype(BF16), _stream_tile(xp_ref, xs_ref), *rest)


def _at_out_kernel(ac_ref, al_ref, x_ref, *rest):
    _mixer_tail(_stream_tile(ac_ref, al_ref), x_ref[...], *rest)


def _mixer_out(kernel_fn, name, mix_inputs, mix_specs, mod, nw2, w_out, w_router, b_router):
    tile = lambda i: (i, 0)
    fixed2 = lambda i: (0, 0)
    blk = pl.BlockSpec((TM, D_MODEL), tile)
    lanes_blk = pl.BlockSpec((TM, ROUTER_LANES), tile)
    return pl.pallas_call(
        kernel_fn,
        out_shape=(
            jax.ShapeDtypeStruct((N_TOK, D_MODEL), F32),
            jax.ShapeDtypeStruct((2, N_TOK, PACK_WORDS), jnp.uint32),
            jax.ShapeDtypeStruct((N_TOK, ROUTER_LANES), jnp.int32),
            jax.ShapeDtypeStruct((N_TOK, ROUTER_LANES), F32),
            jax.ShapeDtypeStruct((1, ROUTER_LANES), F32),
        ),
        grid=(N_TILES,),
        in_specs=list(mix_specs) + [
            pl.BlockSpec(mod.shape, fixed2),
            pl.BlockSpec((1, D_MODEL), fixed2),
            pl.BlockSpec(w_out.shape, fixed2),
            pl.BlockSpec(w_router.shape, lambda i: (0, 0, 0)),
            pl.BlockSpec((1, ROUTER_LANES), fixed2),
        ],
        out_specs=(blk, pl.BlockSpec((2, TM, PACK_WORDS), lambda i: (0, i, 0)), lanes_blk, lanes_blk,
                   pl.BlockSpec((1, ROUTER_LANES), fixed2)),
        scratch_shapes=[pltpu.VMEM((1, ROUTER_LANES), F32)],
        compiler_params=_params(40, "arbitrary"),
        name=name,
    )(*mix_inputs, mod, nw2, w_out, w_router, b_router)


def _moe_plan(route_i, counts):
    cnt = counts[0, :N_EXPERTS].astype(jnp.int32)
    padded = ((cnt + MOE_TILE - 1) // MOE_TILE) * MOE_TILE
    ends = jnp.cumsum(padded)
    offs = ends - padded
    experts = route_i[:, 0:2]
    pos = jnp.sum(jnp.where(experts[:, :, None] == jnp.arange(N_EXPERTS)[None, None, :], offs[None, None, :], 0),
                  axis=-1) + route_i[:, 2:4]
    tile_start = jnp.arange(MOE_ROWS // MOE_TILE, dtype=jnp.int32) * MOE_TILE
    tile_expert = jnp.minimum(jnp.sum(ends[None, :] <= tile_start[:, None], axis=1), N_EXPERTS - 1).astype(jnp.int32)
    of_tile = tile_expert[:, None] == jnp.arange(N_EXPERTS)[None, :]
    tile_offs = jnp.sum(jnp.where(of_tile, offs[None, :], 0), axis=1)
    tile_cnt = jnp.sum(jnp.where(of_tile, cnt[None, :], 0), axis=1)
    tile_rows = jnp.clip(tile_offs + tile_cnt - tile_start, 0, MOE_TILE).astype(jnp.int32)
    tile_first = (tile_start == tile_offs).astype(jnp.int32)
    n_active = (ends[-1] // MOE_TILE).astype(jnp.int32).reshape(1)
    ids = jnp.arange(N_EXPERTS)
    used = cnt > 0
    slot = (jnp.cumsum(used.astype(jnp.int32)) - 1) % 2
    later = jnp.min(jnp.where(used[None, :] & (ids[None, :] > ids[:, None]), ids[None, :], N_EXPERTS), axis=1)
    later = jnp.where(later == N_EXPERTS, -1, later)
    tile_slot = jnp.sum(jnp.where(of_tile, slot[None, :], 0), axis=1).astype(jnp.int32)
    tile_next = jnp.sum(jnp.where(of_tile, later[None, :], 0), axis=1).astype(jnp.int32)
    return pos.astype(jnp.int32), (tile_expert, tile_rows, tile_first, n_active, tile_slot, tile_next)


SC_WINDOW = 128


def _sc_mesh():
    return plsc.VectorSubcoreMesh(core_axis_name="c", subcore_axis_name="s")


def _sc_scatter_rows(x, idx_a, idx_b, n_out_rows):
    n = x.shape[0]

    @functools.partial(pl.kernel, out_type=jax.ShapeDtypeStruct((n_out_rows, PACK_WORDS), x.dtype), mesh=_sc_mesh(),
                       scratch_types=[pltpu.SemaphoreType.DMA, pltpu.SemaphoreType.DMA])
    def scatter(x_hbm, ia_hbm, ib_hbm, o_hbm, sem_a, sem_b):
        def body(x_vmem, ia_vmem, ib_vmem):
            copy_a = pltpu.async_copy(x_vmem, o_hbm.at[ia_vmem.at[0]], sem_a)
            copy_b = pltpu.async_copy(x_vmem, o_hbm.at[ib_vmem.at[0]], sem_b)
            copy_a.wait()
            copy_b.wait()

        idx_spec = pl.BlockSpec((1, SC_WINDOW), index_map=lambda i: (0, i))
        pltpu.emit_pipeline(
            body, grid=(n // SC_WINDOW,),
            in_specs=[pl.BlockSpec((SC_WINDOW, PACK_WORDS), index_map=lambda i: (i, 0)), idx_spec, idx_spec],
            out_specs=[],
            core_axis_name=("c", "s"), dimension_semantics=(pltpu.PARALLEL,),
        )(x_hbm, ia_hbm, ib_hbm)

    return scatter(x, idx_a.reshape(1, n), idx_b.reshape(1, n))


def _sc_gather_rows(table, idx):
    n = idx.shape[0]

    @functools.partial(pl.kernel, out_type=jax.ShapeDtypeStruct((n, PACK_WORDS), table.dtype), mesh=_sc_mesh())
    def gather(t_hbm, i_hbm, o_hbm):
        def body(i_vmem, o_vmem):
            pltpu.sync_copy(t_hbm.at[i_vmem.at[0]], o_vmem)

        pltpu.emit_pipeline(
            body, grid=(n // SC_WINDOW,),
            in_specs=[pl.BlockSpec((1, SC_WINDOW), index_map=lambda i: (0, i))],
            out_specs=[pl.BlockSpec((SC_WINDOW, PACK_WORDS), index_map=lambda i: (i, 0))],
            core_axis_name=("c", "s"), dimension_semantics=(pltpu.PARALLEL,),
        )(i_hbm, o_hbm)

    return gather(table, idx.reshape(1, n))


def _ffn_kernel(te_ref, tr_ref, tf_ref, na_ref, ts_ref, tn_ref, xs_ref, w1_hbm, w3_hbm, w2_hbm, ys_ref,
                w1_scr, w3_scr, w2_scr, w1_buf, w3_buf, w2_buf, sem, *, layer):
    t = pl.program_id(0)

    def weight_copies(expert, slot):
        pairs = ((w1_hbm, w1_buf), (w3_hbm, w3_buf), (w2_hbm, w2_buf))
        return [pltpu.make_async_copy(hbm.at[layer, expert], buf.at[slot], sem.at[slot, j])
                for j, (hbm, buf) in enumerate(pairs)]

    @pl.when(t < na_ref[0])
    def _():
        @pl.when(tf_ref[t] == 1)
        def _():
            slot = ts_ref[t]

            @pl.when(t == 0)
            def _():
                for copy in weight_copies(te_ref[t], slot):
                    copy.start()

            for copy in weight_copies(te_ref[t], slot):
                copy.wait()

            @pl.when(tn_ref[t] >= 0)
            def _():
                for copy in weight_copies(tn_ref[t], 1 - slot):
                    copy.start()

            w1_scr[...] = w1_buf[slot].astype(BF16)
            w3_scr[...] = w3_buf[slot].astype(BF16)
            w2_scr[...] = w2_buf[slot].astype(BF16)

        row = lax.broadcasted_iota(jnp.int32, (MOE_TILE, PACK_WORDS), 0)
        live = row < tr_ref[t]
        halves = [jnp.where(live, xs_ref[h], jnp.zeros((MOE_TILE, PACK_WORDS), jnp.uint32)) for h in range(2)]
        chunks = _unpack_rows(*halves)

        def up(w_scr):
            acc = jnp.dot(chunks[0], w_scr[:PACK_WORDS, :], preferred_element_type=F32)
            for k in range(1, len(chunks)):
                acc = acc + jnp.dot(chunks[k], w_scr[k * PACK_WORDS:(k + 1) * PACK_WORDS, :],
                                    preferred_element_type=F32)
            return acc

        hid = (_silu(up(w1_scr)) * up(w3_scr)).astype(BF16)
        for h, words in enumerate(_pack_rows(jnp.dot(hid, w2_scr[...], preferred_element_type=F32))):
            ys_ref[h] = words


def _ffn(xs, tables, layer, w1, w3, w2):
    row_tile = lambda t, te, tr, tf, na, ts, tn: (0, jnp.minimum(t, na[0] - 1), 0)
    hbm = pl.BlockSpec(memory_space=pl.ANY)
    return pl.pallas_call(
        functools.partial(_ffn_kernel, layer=layer),
        out_shape=jax.ShapeDtypeStruct((2, MOE_ROWS, PACK_WORDS), jnp.uint32),
        grid_spec=pltpu.PrefetchScalarGridSpec(
            num_scalar_prefetch=len(tables),
            grid=(MOE_ROWS // MOE_TILE,),
            in_specs=[pl.BlockSpec((2, MOE_TILE, PACK_WORDS), row_tile), hbm, hbm, hbm],
            out_specs=pl.BlockSpec((2, MOE_TILE, PACK_WORDS), row_tile),
            scratch_shapes=[
                pltpu.VMEM((D_MODEL, D_EXPERT), BF16),
                pltpu.VMEM((D_MODEL, D_EXPERT), BF16),
                pltpu.VMEM((D_EXPERT, D_MODEL), BF16),
                pltpu.VMEM((2, D_MODEL, D_EXPERT), F32),
                pltpu.VMEM((2, D_MODEL, D_EXPERT), F32),
                pltpu.VMEM((2, D_EXPERT, D_MODEL), F32),
                pltpu.SemaphoreType.DMA((2, 3)),
            ],
        ),
        compiler_params=_params(32, "arbitrary"),
        name="moe_experts",
    )(*tables, xs, w1, w3, w2)


def _moe_res(x_ref, y_ref, rw_ref, mod_ref):
    cond = _tile_cond(pl.program_id(0))
    rw = rw_ref[...]
    wa = rw[:, 0:1]
    wb = rw[:, 1:2]
    ya = _unpack_rows(y_ref[0], y_ref[1])
    yb = _unpack_rows(y_ref[2], y_ref[3])
    y = jnp.concatenate([wa * a.astype(F32) + wb * b.astype(F32) for a, b in zip(ya, yb)], axis=1)
    return x_ref[...] + _mod_row(mod_ref, cond, 5) * y


def _moe_res_kernel(x_ref, y_ref, rw_ref, mod_ref, o_ref):
    o_ref[...] = _moe_res(x_ref, y_ref, rw_ref, mod_ref)


def _moe_res_final_kernel(x_ref, y_ref, rw_ref, mod_ref, fn_ref, op_ref, os_ref):
    x = _moe_res(x_ref, y_ref, rw_ref, mod_ref)
    y = x * lax.rsqrt(jnp.mean(x * x, axis=-1, keepdims=True) + EPS) * fn_ref[...]
    is_prompt = pl.program_id(0) < PROMPT_TILES

    @pl.when(is_prompt)
    def _():
        op_ref[...] = y

    @pl.when(jnp.logical_not(is_prompt))
    def _():
        os_ref[...] = y


def _moe_combine(x1, y_pairs, route_w, mod, final_norm=None):
    tile = lambda i: (i, 0)
    fixed2 = lambda i: (0, 0)
    blk = pl.BlockSpec((TM, D_MODEL), tile)
    specs = [blk,
             pl.BlockSpec((4, TM, PACK_WORDS), lambda i: (0, i, 0)),
             pl.BlockSpec((TM, ROUTER_LANES), tile),
             pl.BlockSpec(mod.shape, fixed2)]
    args = [x1, y_pairs, route_w, mod]
    if final_norm is None:
        kernel_fn = _moe_res_kernel
        out_shape = jax.ShapeDtypeStruct((N_TOK, D_MODEL), F32)
        out_specs = blk
    else:
        kernel_fn = _moe_res_final_kernel
        specs.append(pl.BlockSpec((1, D_MODEL), fixed2))
        args.append(final_norm)
        out_shape = (jax.ShapeDtypeStruct((N_PROMPT, D_MODEL), F32),
                     jax.ShapeDtypeStruct((N_TOK - N_PROMPT, D_MODEL), F32))
        out_specs = (pl.BlockSpec((TM, D_MODEL), _prompt_tile), pl.BlockSpec((TM, D_MODEL), _sample_tile))
    return pl.pallas_call(
        kernel_fn,
        out_shape=out_shape,
        grid=(N_TILES,),
        in_specs=specs,
        out_specs=out_specs,
        compiler_params=_params(32, "arbitrary"),
        name="moe_combine",
    )(*args)


def _moe_layer(x1, h2p, route_i, route_w, counts, mod, layer, w1, w3, w2, final_norm=None):
    pos, tables = _moe_plan(route_i, counts)
    idx_a = jnp.concatenate([pos[:, 0], pos[:, 0] + MOE_ROWS])
    idx_b = jnp.concatenate([pos[:, 1], pos[:, 1] + MOE_ROWS])
    xs = _sc_scatter_rows(h2p.reshape(2 * N_TOK, PACK_WORDS), idx_a, idx_b, 2 * MOE_ROWS)
    ys = _ffn(xs.reshape(2, MOE_ROWS, PACK_WORDS), tables, layer, w1, w3, w2)
    y_pairs = _sc_gather_rows(ys.reshape(2 * MOE_ROWS, PACK_WORDS), jnp.concatenate([idx_a, idx_b]))
    return _moe_combine(x1, y_pairs.reshape(4, N_TOK, PACK_WORDS), route_w, mod, final_norm)


def _swap_rotary_halves(x):
    n = x.shape[-1]
    lane = lax.broadcasted_iota(jnp.int32, x.shape, 1)
    quarter = ROPE_HALF // 2
    return jnp.where((lane % ROPE_HALF) < quarter, pltpu.roll(x, n - quarter, 1), pltpu.roll(x, quarter, 1))


def _at_in_kernel(x_ref, mod_ref, nw_ref, w_ref, cos_ref, sin_ref, q_ref, kt_ref, v_ref, kc_ref, vc_ref):
    i = pl.program_id(0)
    cond = _tile_cond(i)
    h = _norm_mod(x_ref[...], nw_ref[...], _mod_row(mod_ref, cond, 0), _mod_row(mod_ref, cond, 1)).astype(BF16)
    nq = ATT_Q_HEADS * ATT_HEAD_DIM
    nk = ATT_KV_HEADS * ATT_HEAD_DIM
    cos = cos_ref[...]
    sin = sin_ref[...]

    def rope(x):
        reps = x.shape[-1] // LANES
        return x * jnp.concatenate([cos] * reps, axis=1) + _swap_rotary_halves(x) * jnp.concatenate([sin] * reps, axis=1)

    q_ref[...] = (rope(jnp.dot(h, w_ref[:, :nq], preferred_element_type=F32)) * ATT_HEAD_DIM ** -0.5).astype(BF16)
    k = rope(jnp.dot(h, w_ref[:, nq:nq + nk], preferred_element_type=F32))
    v = jnp.dot(h, w_ref[:, nq + nk:], preferred_element_type=F32)
    kt = k.T
    kt_ref[...] = kt.astype(BF16)
    v_ref[...] = v.astype(BF16)

    @pl.when(i < PROMPT_TILES)
    def _():
        kc_ref[...] = kt
        vc_ref[...] = v.T


def _rope_tables():
    f32 = np.float32
    pos = np.arange(DEC_SEQ)
    t_row = (pos // GRID_W).astype(f32)
    t_col = (pos % GRID_W).astype(f32)
    inv = f32(ROPE_BASE) ** (-np.arange(0, ROPE_HALF, 2, dtype=f32) / f32(ROPE_HALF))
    j = np.arange(LANES) % ATT_HEAD_DIM
    freq = inv[(j % ROPE_HALF) % (ROPE_HALF // 2)]
    ang = (np.where((j < ROPE_HALF)[None, :], t_row[:, None], t_col[:, None]) * freq[None, :]).astype(f32)
    sign = np.where((j % ROPE_HALF) < ROPE_HALF // 2, -1.0, 1.0).astype(f32)
    cos = np.concatenate([np.ones((TM, LANES), f32), np.cos(ang)], axis=0)
    sin = np.concatenate([np.zeros((TM, LANES), f32), np.sin(ang) * sign[None, :]], axis=0)
    return jnp.asarray(cos, F32), jnp.asarray(sin, F32)


def _at_in(x, mod, nw, w_in, cos, sin):
    tile = lambda i: (i, 0)
    fixed2 = lambda i: (0, 0)
    rope_tile = lambda i: (jnp.where(i < PROMPT_TILES, 0, 1 + (i - PROMPT_TILES) % TILES_PER_DEC_SEQ), 0)
    nk = ATT_KV_HEADS * ATT_HEAD_DIM
    return pl.pallas_call(
        _at_in_kernel,
        out_shape=(
            jax.ShapeDtypeStruct((N_TOK, D_MODEL), BF16),
            jax.ShapeDtypeStruct((nk, N_TOK), BF16),
            jax.ShapeDtypeStruct((N_TOK, nk), BF16),
            jax.ShapeDtypeStruct((N_PROMPT, nk), F32),
            jax.ShapeDtypeStruct((N_PROMPT, nk), F32),
        ),
        grid=(N_TILES,),
        in_specs=[
            pl.BlockSpec((TM, D_MODEL), tile),
            pl.BlockSpec(mod.shape, fixed2),
            pl.BlockSpec((1, D_MODEL), fixed2),
            pl.BlockSpec(w_in.shape, fixed2),
            pl.BlockSpec((TM, LANES), rope_tile),
            pl.BlockSpec((TM, LANES), rope_tile),
        ],
        out_specs=(pl.BlockSpec((TM, D_MODEL), tile), pl.BlockSpec((nk, TM), lambda i: (0, i)),
                   pl.BlockSpec((TM, nk), tile),
                   pl.BlockSpec((TM, nk), _prompt_tile), pl.BlockSpec((TM, nk), _prompt_tile)),
        compiler_params=_params(40, "arbitrary"),
        name="attn_in_proj",
    )(x, mod, nw, w_in, cos, sin)


def _attend(q, kt_all, v_all, mask, sink_ref, o_ref):
    nq = q.shape[0]
    group_lanes = ATT_GROUP * ATT_HEAD_DIM
    lane = lax.broadcasted_iota(jnp.int32, (nq, group_lanes), 1)
    mine = [(lane // ATT_HEAD_DIM) == g for g in range(ATT_GROUP)]
    row_head = lax.broadcasted_iota(jnp.int32, (ATT_GROUP * nq, 1), 0) // nq
    if mask is not None:
        mask = jnp.concatenate([mask] * ATT_GROUP, axis=0)
    for hk in range(ATT_KV_HEADS):
        vh = v_all[:, hk * ATT_HEAD_DIM:(hk + 1) * ATT_HEAD_DIM]
        kt = jnp.concatenate([kt_all[hk * ATT_HEAD_DIM:(hk + 1) * ATT_HEAD_DIM, :]] * ATT_GROUP, axis=0)
        vt = jnp.concatenate([vh] * ATT_GROUP, axis=1)
        qg = q[:, hk * group_lanes:(hk + 1) * group_lanes]
        q_stack = jnp.concatenate([jnp.where(mine[g], qg, jnp.zeros_like(qg)) for g in range(ATT_GROUP)], axis=0)
        s = jnp.dot(q_stack, kt, preferred_element_type=F32)
        if mask is not None:
            s = jnp.where(mask, s, -jnp.inf)
        sink = jnp.zeros((ATT_GROUP * nq, 1), F32)
        for g in range(ATT_GROUP):
            sink = jnp.where(row_head == g, sink_ref[hk * ATT_GROUP + g], sink)
        m = jnp.maximum(jnp.max(s, axis=-1, keepdims=True), sink)
        p = jnp.exp(s - m)
        denom = jnp.sum(p, axis=-1, keepdims=True) + jnp.exp(sink - m)
        o = jnp.dot(p.astype(BF16), vt, preferred_element_type=F32) / denom
        acc = jnp.where(mine[0], o[:nq], 0.0)
        for g in range(1, ATT_GROUP):
            acc = acc + jnp.where(mine[g], o[g * nq:(g + 1) * nq], 0.0)
        o_ref[:, hk * group_lanes:(hk + 1) * group_lanes] = acc.astype(BF16)


def _ctx_attn_kernel(sink_ref, q_ref, k_ref, v_ref, o_ref):
    _attend(q_ref[...], k_ref[...], v_ref[...], None, sink_ref, o_ref)


def _lat_attn_kernel(sink_ref, q_ref, kp_ref, kc_ref, kn_ref, vp_ref, vc_ref, vn_ref, ck_ref, cv_ref, o_ref):
    jb = pl.program_id(1)
    kt_all = jnp.concatenate([kp_ref[...], kc_ref[...], kn_ref[...], ck_ref[0].astype(BF16)], axis=1)
    v_all = jnp.concatenate([vp_ref[...], vc_ref[...], vn_ref[...], cv_ref[0].astype(BF16)], axis=0)
    nkeys = 3 * BLOCK + PAST_LEN
    qi = lax.broadcasted_iota(jnp.int32, (BLOCK, nkeys), 0)
    kj = lax.broadcasted_iota(jnp.int32, (BLOCK, nkeys), 1)
    qpos = jb * BLOCK + qi
    kpos = (jb - 1) * BLOCK + kj
    local_ok = (jnp.abs(qpos - kpos) <= WINDOW) & (kpos >= 0) & (kpos < DEC_SEQ)
    mask = (kj >= 3 * BLOCK) | local_ok
    _attend(q_ref[...], kt_all, v_all, mask, sink_ref, o_ref)


def _attention(q, kt, v, cache_kt, cache_v, sink):
    nk = ATT_KV_HEADS * ATT_HEAD_DIM
    smem = pl.BlockSpec(memory_space=pltpu.SMEM)
    ctx = pl.pallas_call(
        _ctx_attn_kernel,
        out_shape=jax.ShapeDtypeStruct((N_PROMPT, D_MODEL), BF16),
        grid=(BATCH,),
        in_specs=[
            smem,
            pl.BlockSpec((SEQ, D_MODEL), lambda b: (b, 0)),
            pl.BlockSpec((nk, SEQ), lambda b: (0, b)),
            pl.BlockSpec((SEQ, nk), lambda b: (b, 0)),
        ],
        out_specs=pl.BlockSpec((SEQ, D_MODEL), lambda b: (b, 0)),
        compiler_params=_params(40, "arbitrary"),
        name="context_attention",
    )(sink, q, kt, v)

    nb = DEC_SEQ // BLOCK
    base = N_PROMPT // BLOCK
    cur = lambda b, j: (base + b * nb + j, 0)
    prev = lambda b, j: (base + b * nb + jnp.maximum(j - 1, 0), 0)
    nxt = lambda b, j: (base + b * nb + jnp.minimum(j + 1, nb - 1), 0)
    kv_blk = lambda f: pl.BlockSpec((BLOCK, nk), f)
    kt_blk = lambda f: pl.BlockSpec((nk, BLOCK), lambda b, j: f(b, j)[::-1])
    cache_blk = pl.BlockSpec((1, PAST_LEN, nk), lambda b, j: (b, 0, 0))
    cache_kt_blk = pl.BlockSpec((1, nk, PAST_LEN), lambda b, j: (b, 0, 0))
    lat = pl.pallas_call(
        _lat_attn_kernel,
        out_shape=jax.ShapeDtypeStruct((DEC_BATCH * DEC_SEQ, D_MODEL), BF16),
        grid=(DEC_BATCH, nb),
        in_specs=[
            smem,
            pl.BlockSpec((BLOCK, D_MODEL), cur),
            kt_blk(prev), kt_blk(cur), kt_blk(nxt),
            kv_blk(prev), kv_blk(cur), kv_blk(nxt),
            cache_kt_blk, cache_blk,
        ],
        out_specs=pl.BlockSpec((BLOCK, D_MODEL), lambda b, j: (b * nb + j, 0)),
        compiler_params=_params(40, "arbitrary", "arbitrary"),
        name="latent_attention",
    )(sink, q, kt, kt, kt, v, v, v, cache_kt, cache_v)
    return ctx, lat


def kernel(x_prompt, x_sample, state_hgrn, cache_k, cache_v, c, c_ctx, ada_w, ada_b, norm_w, hg_w_in,
           hg_lb_logits, hg_onorm, hg_w_out, at_w_in, at_sink, at_w_out, moe_w_group, moe_b_group,
           moe_w_expert, moe_b_expert, moe_w1, moe_w3, moe_w2, final_norm):
    xp = x_prompt.reshape(N_PROMPT, D_MODEL)
    xs = x_sample.reshape(N_TOK - N_PROMPT, D_MODEL)
    cond = jnp.concatenate([c_ctx[None, :], c], axis=0)
    mod = _ada(cond, ada_w, ada_b)
    nk = ATT_KV_HEADS * ATT_HEAD_DIM

    def router_params(i):
        pad = jnp.zeros((D_MODEL, ROUTER_LANES - N_EXPERTS - N_GROUPS), F32)
        w = jnp.concatenate([moe_w_expert[i], moe_w_group[i], pad], axis=1)
        b = jnp.concatenate([moe_b_expert[i], moe_b_group[i], pad[0]])[None, :]
        hi = w.astype(BF16)
        lo = (w - hi.astype(F32)).astype(BF16)
        return jnp.stack([hi, lo]), b

    tile = lambda i: (i, 0)
    blk = pl.BlockSpec((TM, D_MODEL), tile)

    prompt_blk = pl.BlockSpec((TM, D_MODEL), _prompt_tile)
    sample_blk = pl.BlockSpec((TM, D_MODEL), _sample_tile)
    q, v, g, lff, kf, lfb, kb = _hg_in(xp, xs, mod[0], norm_w[0, 0][None, :], hg_w_in[0].astype(BF16), hg_lb_logits)
    o_f, o_b, state_new = _gla(q, v, lff, kf, lfb, kb, state_hgrn)
    wr, br = router_params(0)
    routed = _mixer_out(
        _hg_out_kernel, "hgrn_out_route", (o_f, o_b, g, hg_onorm[0][None, :], xp, xs),
        (blk, blk, blk, pl.BlockSpec((1, HG_DK), lambda i: (0, 0)), prompt_blk, sample_blk),
        mod[0], norm_w[0, 1][None, :], hg_w_out[0].astype(BF16), wr, br)
    x = _moe_layer(*routed, mod[0], 0, moe_w1, moe_w3, moe_w2)

    cos, sin = _rope_tables()
    qa, ka, va, k_ctx, v_ctx = _at_in(x, mod[1], norm_w[1, 0][None, :], at_w_in[0].astype(BF16), cos, sin)
    attn_ctx, attn_lat = _attention(qa, ka, va, cache_k[:, 0].reshape(DEC_BATCH, PAST_LEN, nk).transpose(0, 2, 1),
                                    cache_v[:, 0].reshape(DEC_BATCH, PAST_LEN, nk), at_sink[0])
    wr, br = router_params(1)
    routed = _mixer_out(
        _at_out_kernel, "attn_out_route", (attn_ctx, attn_lat, x), (prompt_blk, sample_blk, blk),
        mod[1], norm_w[1, 1][None, :], at_w_out[0].astype(BF16), wr, br)
    y_prompt, y_sample = _moe_layer(*routed, mod[1], 1, moe_w1, moe_w3, moe_w2, final_norm[None, :])

    def cache(feature_major):
        return feature_major.reshape(BATCH, 1, ATT_KV_HEADS, ATT_HEAD_DIM, SEQ).transpose(0, 1, 4, 2, 3)

    return (y_prompt.reshape(BATCH, SEQ, D_MODEL), y_sample.reshape(DEC_BATCH, DEC_SEQ, D_MODEL), state_new,
            cache(k_ctx), cache(v_ctx))
```

```python
import functools
from typing import Any, NamedTuple

import jax
import jax.numpy as jnp
import numpy as np
from jax import lax
from jax.experimental import pallas as pl
from jax.experimental.pallas import tpu as pltpu
from jax.experimental.pallas import tpu_sc as plsc

F32 = jnp.float32
BF16 = jnp.bfloat16

D_MODEL = 1024
BATCH = 16
SEQ = 256
DEC_BATCH = 2
DEC_SEQ = 1024
PAST_LEN = 512
GRID_W = 64
HG_HEADS = 8
HG_DK = 128
CHUNK = 16
ATT_HEAD_DIM = 64
ATT_Q_HEADS = 16
ATT_KV_HEADS = 4
ATT_GROUP = 4
WINDOW = 128
BLOCK = 128
ROPE_HALF = 32
ROPE_BASE = 10000.0
N_GROUPS = 4
EXPERTS_PER_GROUP = 8
N_EXPERTS = 32
D_EXPERT = 256
EPS = 1e-6

N_PROMPT = BATCH * SEQ
N_TOK = N_PROMPT + DEC_BATCH * DEC_SEQ
TM = 256
N_TILES = N_TOK // TM
PROMPT_TILES = N_PROMPT // TM
TILES_PER_DEC_SEQ = DEC_SEQ // TM
LANES = 128
N_COND_USED = 1 + DEC_BATCH
N_COND = 8
ROUTER_LANES = 128
GLA_UNROLL = 2
DECAY_CLAMP = 60.0
PACK_WORDS = D_MODEL // 4
MOE_TILE = 256
MOE_ROWS = 2 * N_TOK + N_EXPERTS * MOE_TILE
MIB = 1024 * 1024


def _params(vmem_mib, *semantics):
    return pltpu.CompilerParams(dimension_semantics=semantics, vmem_limit_bytes=vmem_mib * MIB)


def _tile_cond(i):
    return jnp.where(i < PROMPT_TILES, 0, 1 + (i - PROMPT_TILES) // TILES_PER_DEC_SEQ)


def _prompt_tile(i):
    return (jnp.minimum(i, PROMPT_TILES - 1), 0)


def _sample_tile(i):
    return (jnp.maximum(i - PROMPT_TILES, 0), 0)


def _stream_tile(prompt_ref, sample_ref):
    return jnp.where(pl.program_id(0) < PROMPT_TILES, prompt_ref[...], sample_ref[...])


def _mod_row(mod_ref, cond, which):
    return mod_ref[pl.ds(cond, 1), which * D_MODEL:(which + 1) * D_MODEL]


def _norm_mod(x, nw, shift, scale):
    y = x * lax.rsqrt(jnp.mean(x * x, axis=-1, keepdims=True) + EPS)
    return (y * nw) * (1.0 + scale) + shift


def _silu(x):
    return x * jax.nn.sigmoid(x)


def _ada_kernel(c_ref, w_ref, b_ref, o_ref):
    s = [_silu(c_ref[r]) for r in range(N_COND_USED)]
    tn = w_ref.shape[-1]
    rows = []
    for r in range(N_COND_USED):
        cols = [jnp.sum(w_ref[0, :, j * LANES:(j + 1) * LANES] * s[r], axis=0, keepdims=True)
                for j in range(tn // LANES)]
        rows.append(jnp.concatenate(cols, axis=1) + b_ref[0])
    rows.append(jnp.zeros((N_COND - N_COND_USED, tn), F32))
    o_ref[0] = jnp.concatenate(rows, axis=0)


def _ada(cond, ada_w, ada_b):
    depth, _, n = ada_w.shape
    tn = 1536
    cond_cols = jnp.broadcast_to(cond[:, :, None], (N_COND_USED, D_MODEL, LANES))
    return pl.pallas_call(
        _ada_kernel,
        out_shape=jax.ShapeDtypeStruct((depth, N_COND, n), F32),
        grid=(depth, n // tn),
        in_specs=[
            pl.BlockSpec((N_COND_USED, D_MODEL, LANES), lambda l, j: (0, 0, 0)),
            pl.BlockSpec((1, D_MODEL, tn), lambda l, j: (l, 0, j)),
            pl.BlockSpec((1, 1, tn), lambda l, j: (l, 0, j)),
        ],
        out_specs=pl.BlockSpec((1, N_COND, tn), lambda l, j: (l, 0, j)),
        compiler_params=_params(40, "arbitrary", "arbitrary"),
        name="ada_modulation",
    )(cond_cols, ada_w, ada_b.reshape(depth, 1, n))


def _hg_in_kernel(xp_ref, xs_ref, mod_ref, nw_ref, w_ref, lbl_ref,
                  q_ref, v_ref, g_ref, lff_ref, kf_ref, lfb_ref, kb_ref):
    cond = _tile_cond(pl.program_id(0))
    x = _stream_tile(xp_ref, xs_ref)
    h = _norm_mod(x, nw_ref[...], _mod_row(mod_ref, cond, 0), _mod_row(mod_ref, cond, 1)).astype(BF16)

    def proj(c):
        return jnp.dot(h, w_ref[:, c * D_MODEL:(c + 1) * D_MODEL], preferred_element_type=F32)

    l0, l1, l2 = lbl_ref[0], lbl_ref[1], lbl_ref[2]
    m = jnp.maximum(jnp.maximum(l0, l1), l2)
    e0, e1, e2 = jnp.exp(l0 - m), jnp.exp(l1 - m), jnp.exp(l2 - m)
    lb = e0 / (e0 + e1 + e2)

    q_ref[...] = proj(0).astype(BF16)
    v_ref[...] = proj(1).astype(BF16)
    for d, (lf_ref, k_ref) in enumerate(((lff_ref, kf_ref), (lfb_ref, kb_ref))):
        lbd = lb[d:d + 1, :]
        f = lbd + (1.0 - lbd) * jax.nn.sigmoid(proj(2 + d))
        lf = jnp.log(f)
        hi = lf.astype(BF16)
        lf_ref[0] = hi
        lf_ref[1] = (lf - hi.astype(F32)).astype(BF16)
        k_ref[...] = (1.0 - f).astype(BF16)
    g_ref[...] = proj(4).astype(BF16)


def _hg_in(xp, xs, mod, nw, w_in, lb_logits):
    tile = lambda i: (i, 0)
    fixed2 = lambda i: (0, 0)
    bf = jax.ShapeDtypeStruct((N_TOK, D_MODEL), BF16)
    ff = jax.ShapeDtypeStruct((2, N_TOK, D_MODEL), BF16)
    blk = pl.BlockSpec((TM, D_MODEL), tile)
    split_blk = pl.BlockSpec((2, TM, D_MODEL), lambda i: (0, i, 0))
    return pl.pallas_call(
        _hg_in_kernel,
        out_shape=(bf, bf, bf, ff, bf, ff, bf),
        grid=(N_TILES,),
        in_specs=[
            pl.BlockSpec((TM, D_MODEL), _prompt_tile),
            pl.BlockSpec((TM, D_MODEL), _sample_tile),
            pl.BlockSpec(mod.shape, fixed2),
            pl.BlockSpec((1, D_MODEL), fixed2),
            pl.BlockSpec(w_in.shape, fixed2),
            pl.BlockSpec(lb_logits.shape, lambda i: (0, 0, 0)),
        ],
        out_specs=(blk, blk, blk, split_blk, blk, split_blk, blk),
        compiler_params=_params(56, "arbitrary"),
        name="hgrn_in_proj",
    )(xp, xs, mod, nw, w_in, lb_logits)


def _gla_exact(reverse, q_ref, v_ref, lf_ref, k_ref, st_ref, o_ref):
    nt = (((1,), (1,)), ((), ()))
    tn = (((0,), (0,)), ((), ()))
    n_chunks = TM // CHUNK
    row_id = lax.broadcasted_iota(jnp.int32, (CHUNK, HG_DK), 0)
    order = range(CHUNK - 1, -1, -1) if reverse else range(CHUNK)

    def chunk_head(it, carry):
        step, h = it // HG_HEADS, it % HG_HEADS
        ci = (n_chunks - 1 - step) if reverse else step
        rows = pl.ds(pl.multiple_of(ci * CHUNK, CHUNK), CHUNK)
        cols = pl.ds(pl.multiple_of(h * HG_DK, HG_DK), HG_DK)
        q, k, v = q_ref[rows, cols], k_ref[rows, cols], v_ref[rows, cols]
        f = jnp.exp(lf_ref[0, rows, cols].astype(F32) + lf_ref[1, rows, cols].astype(F32))
        st = st_ref[0, h]
        o = jnp.zeros((CHUNK, HG_DK), F32)
        for t in order:
            one = row_id == t
            v_t = jnp.where(one, v, jnp.zeros_like(v))
            st = st * f[t:t + 1, :] + lax.dot_general(v_t, k, tn, preferred_element_type=F32)
            o = jnp.where(one, lax.dot_general(q, st.astype(BF16), nt, preferred_element_type=F32), o)
        st_ref[0, h] = st
        o_ref[rows, cols] = o
        return carry

    lax.fori_loop(0, n_chunks * HG_HEADS, chunk_head, 0)


class _GlaDirection(NamedTuple):
    reverse: bool
    q: Any
    v: Any
    lf: Any
    k: Any
    o: Any
    st: Any
    sw: Any
    ut: Any
    qd: Any
    kd: Any
    ki: Any
    cd: Any


_NT = (((1,), (1,)), ((), ()))
_TN = (((0,), (0,)), ((), ()))
_HG_COLS = [slice(h * HG_DK, (h + 1) * HG_DK) for h in range(HG_HEADS)]
_N_CHUNKS = TM // CHUNK
_N_PAIRS = _N_CHUNKS // 2


def _gla_prepare(d):
    r = lax.broadcasted_iota(jnp.int32, (TM, TM), 0)
    c = lax.broadcasted_iota(jnp.int32, (TM, TM), 1)
    same = (r // CHUNK) == (c // CHUNK)
    tri = (same & ((c >= r) if d.reverse else (c <= r))).astype(BF16)
    b = jnp.dot(tri, d.lf[0], preferred_element_type=F32) + jnp.dot(tri, d.lf[1], preferred_element_type=F32)
    b3 = b.reshape(_N_CHUNKS, CHUNK, D_MODEL)
    edge = 0 if d.reverse else CHUNK - 1
    tot = jnp.broadcast_to(b3[:, edge:edge + 1, :], b3.shape).reshape(TM, D_MODEL)
    k = d.k[...]
    d.qd[...] = d.q[...] * jnp.exp(b).astype(BF16)
    d.kd[...] = k * jnp.exp(tot - b).astype(BF16)
    d.ki[...] = k * jnp.exp(jnp.minimum(-b, DECAY_CLAMP)).astype(BF16)
    d.cd[...] = jnp.exp(tot)
    return jnp.max(-b3[:, edge:edge + 1, :]) <= DECAY_CLAMP


def _gla_pair_rows(d, step):
    pi = (_N_PAIRS - 1 - step) if d.reverse else step
    row0 = pl.multiple_of(pi * 2 * CHUNK, 2 * CHUNK)
    lo, hi = pl.ds(row0, CHUNK), pl.ds(row0 + CHUNK, CHUNK)
    return pl.ds(row0, 2 * CHUNK), (hi, lo) if d.reverse else (lo, hi)


def _rows_scaled(x, scale_row, second_half):
    scale = jnp.broadcast_to(scale_row.astype(BF16), (CHUNK, x.shape[1]))
    ones = jnp.ones((CHUNK, x.shape[1]), BF16)
    return x * jnp.concatenate([ones, scale] if second_half else [scale, ones], axis=0)


def _gla_key_value_product(d, step):
    both, (first, second) = _gla_pair_rows(d, step)
    decay_second = d.cd[pl.ds(second.start, 1), :]
    for h, cols in enumerate(_HG_COLS):
        keys = _rows_scaled(d.kd[both, cols], decay_second[:, cols], second_half=d.reverse)
        d.ut[h] = lax.dot_general(d.v[both, cols], keys, _TN, preferred_element_type=F32)


def _gla_start(d):
    for h in range(HG_HEADS):
        d.sw[0, h] = d.st[0, h].T.astype(BF16)
    _gla_key_value_product(d, 0)


def _gla_pair(d, step, src, dst):
    sr = lax.broadcasted_iota(jnp.int32, (CHUNK, CHUNK), 0)
    sc = lax.broadcasted_iota(jnp.int32, (CHUNK, CHUNK), 1)
    tr = lax.broadcasted_iota(jnp.int32, (CHUNK, 2 * CHUNK), 0)
    tc = lax.broadcasted_iota(jnp.int32, (CHUNK, 2 * CHUNK), 1)
    if d.reverse:
        keep_first = sc >= sr
        keep_second = (tc >= tr) | (tc >= CHUNK)
    else:
        keep_first = sc <= sr
        keep_second = (tc < CHUNK) | (tc - CHUNK <= tr)
    both, (first, second) = _gla_pair_rows(d, step)
    decay_first = d.cd[pl.ds(first.start, 1), :]
    decay_second = d.cd[pl.ds(second.start, 1), :]
    decay_both = decay_first * decay_second
    for h, cols in enumerate(_HG_COLS):
        new = d.st[src, h] * decay_both[:, cols] + d.ut[h]
        d.st[dst, h] = new
        d.sw[dst, h] = new.T.astype(BF16)
    a_first = [lax.dot_general(d.qd[first, cols], d.ki[first, cols], _NT, preferred_element_type=F32)
               for cols in _HG_COLS]
    a_second = []
    for cols in _HG_COLS:
        lo_keys = (d.ki if d.reverse else d.kd)[pl.ds(both.start, CHUNK), cols]
        hi_keys = (d.kd if d.reverse else d.ki)[pl.ds(both.start + CHUNK, CHUNK), cols]
        keys = jnp.concatenate([lo_keys, hi_keys], axis=0)
        a_second.append(lax.dot_general(d.qd[second, cols], keys, _NT, preferred_element_type=F32))
    inter = [jnp.dot(_rows_scaled(d.qd[both, cols], decay_first[:, cols], second_half=not d.reverse),
                     d.sw[src, h], preferred_element_type=F32) for h, cols in enumerate(_HG_COLS)]
    _gla_key_value_product(d, jnp.minimum(step + 1, _N_PAIRS - 1))
    first_half, second_half = (slice(CHUNK, None), slice(None, CHUNK)) if d.reverse else \
                              (slice(None, CHUNK), slice(CHUNK, None))
    for h, cols in enumerate(_HG_COLS):
        am = jnp.where(keep_first, a_first[h], 0.0).astype(BF16)
        d.o[first, cols] = jnp.dot(am, d.v[first, cols], preferred_element_type=F32) + inter[h][first_half]
        am = jnp.where(keep_second, a_second[h], 0.0).astype(BF16)
        d.o[second, cols] = jnp.dot(am, d.v[both, cols], preferred_element_type=F32) + inter[h][second_half]


def _gla_kernel(qf_ref, vf_ref, lff_ref, kf_ref, qb_ref, vb_ref, lfb_ref, kb_ref, s0_ref,
                of_ref, ob_ref, sout_ref, st_scr, sw_scr, ut_scr, qd_scr, kd_scr, ki_scr, cd_scr):
    i = pl.program_id(0)
    is_prompt = i < PROMPT_TILES
    first = jnp.logical_or(is_prompt, (i - PROMPT_TILES) % TILES_PER_DEC_SEQ == 0)
    work = [tuple(scr.at[n] for scr in (st_scr, sw_scr, ut_scr, qd_scr, kd_scr, ki_scr, cd_scr)) for n in range(2)]
    fwd = _GlaDirection(False, qf_ref, vf_ref, lff_ref, kf_ref, of_ref, *work[0])
    bwd = _GlaDirection(True, qb_ref, vb_ref, lfb_ref, kb_ref, ob_ref, *work[1])

    @pl.when(first)
    def _():
        keep0 = jnp.where(is_prompt, 0.0, 1.0)
        for n, d in enumerate((fwd, bwd)):
            for h in range(HG_HEADS):
                d.st[0, h] = s0_ref[0, 0, n, h].T * keep0

    exact_scores = jnp.logical_and(_gla_prepare(fwd), _gla_prepare(bwd))

    @pl.when(exact_scores)
    def _():
        _gla_start(fwd)
        _gla_start(bwd)

        def pair_group(it, carry):
            for p in range(GLA_UNROLL):
                for d in (fwd, bwd):
                    _gla_pair(d, GLA_UNROLL * it + p, p % 2, 1 - p % 2)
            return carry

        lax.fori_loop(0, _N_PAIRS // GLA_UNROLL, pair_group, 0)

    @pl.when(jnp.logical_not(exact_scores))
    def _():
        for d in (fwd, bwd):
            _gla_exact(d.reverse, d.q, d.v, d.lf, d.k, d.st, d.o)

    @pl.when(is_prompt)
    def _():
        for n, d in enumerate((fwd, bwd)):
            for h in range(HG_HEADS):
                sout_ref[0, 0, n, h] = d.st[0, h].T


def _gla(q, v, lff, kf, lfb, kb, state_hgrn):
    def fwd_tile(i):
        return (i, 0)

    def bwd_tile(i):
        j = (i - PROMPT_TILES) % TILES_PER_DEC_SEQ
        return (jnp.where(i < PROMPT_TILES, i, i - j + (TILES_PER_DEC_SEQ - 1 - j)), 0)

    def s0_idx(i):
        return (jnp.maximum(i - PROMPT_TILES, 0) // TILES_PER_DEC_SEQ, 0, 0, 0, 0, 0)

    def sout_idx(i):
        return (jnp.minimum(i, PROMPT_TILES - 1), 0, 0, 0, 0, 0)

    f_blk = pl.BlockSpec((TM, D_MODEL), fwd_tile)
    b_blk = pl.BlockSpec((TM, D_MODEL), bwd_tile)
    f_split = pl.BlockSpec((2, TM, D_MODEL), lambda i: (0,) + fwd_tile(i))
    b_split = pl.BlockSpec((2, TM, D_MODEL), lambda i: (0,) + bwd_tile(i))
    st_blk = (1, 1, 2, HG_HEADS, HG_DK, HG_DK)
    return pl.pallas_call(
        _gla_kernel,
        out_shape=(
            jax.ShapeDtypeStruct((N_TOK, D_MODEL), F32),
            jax.ShapeDtypeStruct((N_TOK, D_MODEL), F32),
            jax.ShapeDtypeStruct((BATCH,) + st_blk[1:], F32),
        ),
        grid=(N_TILES,),
        in_specs=[f_blk, f_blk, f_split, f_blk, b_blk, b_blk, b_split, b_blk, pl.BlockSpec(st_blk, s0_idx)],
        out_specs=(f_blk, b_blk, pl.BlockSpec(st_blk, sout_idx)),
        scratch_shapes=[
            pltpu.VMEM((2, 2, HG_HEADS, HG_DK, HG_DK), F32),
            pltpu.VMEM((2, 2, HG_HEADS, HG_DK, HG_DK), BF16),
            pltpu.VMEM((2, HG_HEADS, HG_DK, HG_DK), F32),
            pltpu.VMEM((2, TM, D_MODEL), BF16),
            pltpu.VMEM((2, TM, D_MODEL), BF16),
            pltpu.VMEM((2, TM, D_MODEL), BF16),
            pltpu.VMEM((2, TM, D_MODEL), F32),
        ],
        compiler_params=_params(48, "arbitrary"),
        name="hgrn_recurrence",
    )(q, v, lff, kf, q, v, lfb, kb, state_hgrn)


def _route(logits):
    lane = lax.broadcasted_iota(jnp.int32, logits.shape, 1)
    neg = jnp.float32(-jnp.inf)
    big = jnp.int32(ROUTER_LANES)

    def first_max(x):
        m = jnp.max(x, axis=-1, keepdims=True)
        return m, jnp.min(jnp.where(x == m, lane, big), axis=-1, keepdims=True)

    is_group = (lane >= N_EXPERTS) & (lane < N_EXPERTS + N_GROUPS)
    gl = jnp.where(is_group, logits, neg)
    gmax, g_lane = first_max(gl)
    g_sel = g_lane - N_EXPERTS
    gsum = jnp.sum(jnp.exp(gl - gmax), axis=-1, keepdims=True)
    p_g = 1.0 / gsum
    in_sel = (lane < N_EXPERTS) & ((lane // EXPERTS_PER_GROUP) == g_sel)
    m1, i1 = first_max(jnp.where(in_sel, logits, neg))
    m2, i2 = first_max(jnp.where(in_sel & (lane != i1), logits, neg))
    e2 = jnp.exp(m2 - m1)
    return i1, i2, p_g / (1.0 + e2), p_g * e2 / (1.0 + e2)


def _pack_rows(x):
    q = PACK_WORDS
    bits = pltpu.bitcast(x.astype(BF16).astype(F32), jnp.uint32)
    return [(bits[:, (2 + h) * q:(3 + h) * q] & jnp.uint32(0xFFFF0000)) | (bits[:, h * q:(h + 1) * q] >> 16)
            for h in range(2)]


def _unpack_rows(half0, half1):
    lo = lambda w: pltpu.bitcast(w << 16, F32).astype(BF16)
    hi = lambda w: pltpu.bitcast(w & jnp.uint32(0xFFFF0000), F32).astype(BF16)
    return [lo(half0), lo(half1), hi(half0), hi(half1)]


def _mixer_tail(mix_bf16, x, mod_ref, nw2_ref, wo_ref, wr_ref, br_ref,
                x1_ref, h2p_ref, ri_ref, rw_ref, cnt_ref, carry_scr):
    i = pl.program_id(0)
    cond = _tile_cond(i)
    out = jnp.dot(mix_bf16, wo_ref[...], preferred_element_type=F32)
    x1 = x + _mod_row(mod_ref, cond, 2) * out
    x1_ref[...] = x1
    h2 = _norm_mod(x1, nw2_ref[...], _mod_row(mod_ref, cond, 3), _mod_row(mod_ref, cond, 4))
    for h, words in enumerate(_pack_rows(h2)):
        h2p_ref[h] = words
    h_hi = h2.astype(BF16)
    h_lo = (h2 - h_hi.astype(F32)).astype(BF16)
    logits = (jnp.dot(h_hi, wr_ref[0], preferred_element_type=F32)
              + (jnp.dot(h_hi, wr_ref[1], preferred_element_type=F32)
                 + jnp.dot(h_lo, wr_ref[0], preferred_element_type=F32))) + br_ref[...]
    i1, i2, w1, w2 = _route(logits)

    @pl.when(i == 0)
    def _():
        carry_scr[...] = jnp.zeros_like(carry_scr)

    lane = lax.broadcasted_iota(jnp.int32, logits.shape, 1)
    chosen = ((lane == i1) | (lane == i2)).astype(BF16)
    r = lax.broadcasted_iota(jnp.int32, (TM, TM), 0)
    c = lax.broadcasted_iota(jnp.int32, (TM, TM), 1)
    before = jnp.dot((c < r).astype(BF16), chosen, preferred_element_type=F32) + carry_scr[...]
    r1 = jnp.sum(jnp.where(lane == i1, before, 0.0), axis=-1, keepdims=True).astype(jnp.int32)
    r2 = jnp.sum(jnp.where(lane == i2, before, 0.0), axis=-1, keepdims=True).astype(jnp.int32)
    total = carry_scr[...] + jnp.sum(chosen.astype(F32), axis=0, keepdims=True)
    carry_scr[...] = total
    cnt_ref[...] = total
    ri_ref[...] = jnp.where(lane == 0, i1, jnp.where(lane == 1, i2, jnp.where(lane == 2, r1, r2)))
    rw_ref[...] = jnp.where(lane == 0, w1, w2)


def _hg_out_kernel(of_ref, ob_ref, g_ref, on_ref, xp_ref, xs_ref, *rest):
    o = of_ref[...] + ob_ref[...]
    parts = []
    for h in range(HG_HEADS):
        oh = o[:, h * HG_DK:(h + 1) * HG_DK]
        parts.append(oh * lax.rsqrt(jnp.mean(oh * oh, axis=-1, keepdims=True) + EPS) * on_ref[...])
    y = jnp.concatenate(parts, axis=1) * _silu(g_ref[...].astype(F32))
    _mixer_tail(y.astype(BF16), _stream_tile(xp_ref, xs_ref), *rest)


def _at_out_kernel(ac_ref, al_ref, x_ref, *rest):
    _mixer_tail(_stream_tile(ac_ref, al_ref), x_ref[...], *rest)


def _mixer_out(kernel_fn, name, mix_inputs, mix_specs, mod, nw2, w_out, w_router, b_router):
    tile = lambda i: (i, 0)
    fixed2 = lambda i: (0, 0)
    blk = pl.BlockSpec((TM, D_MODEL), tile)
    lanes_blk = pl.BlockSpec((TM, ROUTER_LANES), tile)
    return pl.pallas_call(
        kernel_fn,
        out_shape=(
            jax.ShapeDtypeStruct((N_TOK, D_MODEL), F32),
            jax.ShapeDtypeStruct((2, N_TOK, PACK_WORDS), jnp.uint32),
            jax.ShapeDtypeStruct((N_TOK, ROUTER_LANES), jnp.int32),
            jax.ShapeDtypeStruct((N_TOK, ROUTER_LANES), F32),
            jax.ShapeDtypeStruct((1, ROUTER_LANES), F32),
        ),
        grid=(N_TILES,),
        in_specs=list(mix_specs) + [
            pl.BlockSpec(mod.shape, fixed2),
            pl.BlockSpec((1, D_MODEL), fixed2),
            pl.BlockSpec(w_out.shape, fixed2),
            pl.BlockSpec(w_router.shape, lambda i: (0, 0, 0)),
            pl.BlockSpec((1, ROUTER_LANES), fixed2),
        ],
        out_specs=(blk, pl.BlockSpec((2, TM, PACK_WORDS), lambda i: (0, i, 0)), lanes_blk, lanes_blk,
                   pl.BlockSpec((1, ROUTER_LANES), fixed2)),
        scratch_shapes=[pltpu.VMEM((1, ROUTER_LANES), F32)],
        compiler_params=_params(40, "arbitrary"),
        name=name,
    )(*mix_inputs, mod, nw2, w_out, w_router, b_router)


def _moe_plan(route_i, counts):
    cnt = counts[0, :N_EXPERTS].astype(jnp.int32)
    padded = ((cnt + MOE_TILE - 1) // MOE_TILE) * MOE_TILE
    ends = jnp.cumsum(padded)
    offs = ends - padded
    experts = route_i[:, 0:2]
    pos = jnp.sum(jnp.where(experts[:, :, None] == jnp.arange(N_EXPERTS)[None, None, :], offs[None, None, :], 0),
                  axis=-1) + route_i[:, 2:4]
    tile_start = jnp.arange(MOE_ROWS // MOE_TILE, dtype=jnp.int32) * MOE_TILE
    tile_expert = jnp.minimum(jnp.sum(ends[None, :] <= tile_start[:, None], axis=1), N_EXPERTS - 1).astype(jnp.int32)
    of_tile = tile_expert[:, None] == jnp.arange(N_EXPERTS)[None, :]
    tile_offs = jnp.sum(jnp.where(of_tile, offs[None, :], 0), axis=1)
    tile_cnt = jnp.sum(jnp.where(of_tile, cnt[None, :], 0), axis=1)
    tile_rows = jnp.clip(tile_offs + tile_cnt - tile_start, 0, MOE_TILE).astype(jnp.int32)
    tile_first = (tile_start == tile_offs).astype(jnp.int32)
    n_active = (ends[-1] // MOE_TILE).astype(jnp.int32).reshape(1)
    ids = jnp.arange(N_EXPERTS)
    used = cnt > 0
    slot = (jnp.cumsum(used.astype(jnp.int32)) - 1) % 2
    later = jnp.min(jnp.where(used[None, :] & (ids[None, :] > ids[:, None]), ids[None, :], N_EXPERTS), axis=1)
    later = jnp.where(later == N_EXPERTS, -1, later)
    tile_slot = jnp.sum(jnp.where(of_tile, slot[None, :], 0), axis=1).astype(jnp.int32)
    tile_next = jnp.sum(jnp.where(of_tile, later[None, :], 0), axis=1).astype(jnp.int32)
    return pos.astype(jnp.int32), (tile_expert, tile_rows, tile_first, n_active, tile_slot, tile_next)


SC_WINDOW = 128


def _sc_mesh():
    return plsc.VectorSubcoreMesh(core_axis_name="c", subcore_axis_name="s")


def _sc_scatter_rows(x, idx_a, idx_b, n_out_rows):
    n = x.shape[0]

    @functools.partial(pl.kernel, out_type=jax.ShapeDtypeStruct((n_out_rows, PACK_WORDS), x.dtype), mesh=_sc_mesh(),
                       scratch_types=[pltpu.SemaphoreType.DMA, pltpu.SemaphoreType.DMA])
    def scatter(x_hbm, ia_hbm, ib_hbm, o_hbm, sem_a, sem_b):
        def body(x_vmem, ia_vmem, ib_vmem):
            copy_a = pltpu.async_copy(x_vmem, o_hbm.at[ia_vmem.at[0]], sem_a)
            copy_b = pltpu.async_copy(x_vmem, o_hbm.at[ib_vmem.at[0]], sem_b)
            copy_a.wait()
            copy_b.wait()

        idx_spec = pl.BlockSpec((1, SC_WINDOW), index_map=lambda i: (0, i))
        pltpu.emit_pipeline(
            body, grid=(n // SC_WINDOW,),
            in_specs=[pl.BlockSpec((SC_WINDOW, PACK_WORDS), index_map=lambda i: (i, 0)), idx_spec, idx_spec],
            out_specs=[],
            core_axis_name=("c", "s"), dimension_semantics=(pltpu.PARALLEL,),
        )(x_hbm, ia_hbm, ib_hbm)

    return scatter(x, idx_a.reshape(1, n), idx_b.reshape(1, n))


def _sc_gather_rows(table, idx):
    n = idx.shape[0]

    @functools.partial(pl.kernel, out_type=jax.ShapeDtypeStruct((n, PACK_WORDS), table.dtype), mesh=_sc_mesh())
    def gather(t_hbm, i_hbm, o_hbm):
        def body(i_vmem, o_vmem):
            pltpu.sync_copy(t_hbm.at[i_vmem.at[0]], o_vmem)

        pltpu.emit_pipeline(
            body, grid=(n // SC_WINDOW,),
            in_specs=[pl.BlockSpec((1, SC_WINDOW), index_map=lambda i: (0, i))],
            out_specs=[pl.BlockSpec((SC_WINDOW, PACK_WORDS), index_map=lambda i: (i, 0))],
            core_axis_name=("c", "s"), dimension_semantics=(pltpu.PARALLEL,),
        )(i_hbm, o_hbm)

    return gather(table, idx.reshape(1, n))


def _ffn_kernel(te_ref, tr_ref, tf_ref, na_ref, ts_ref, tn_ref, xs_ref, w1_hbm, w3_hbm, w2_hbm, ys_ref,
                w1_scr, w3_scr, w2_scr, w1_buf, w3_buf, w2_buf, sem, *, layer):
    t = pl.program_id(0)

    def weight_copies(expert, slot):
        pairs = ((w1_hbm, w1_buf), (w3_hbm, w3_buf), (w2_hbm, w2_buf))
        return [pltpu.make_async_copy(hbm.at[layer, expert], buf.at[slot], sem.at[slot, j])
                for j, (hbm, buf) in enumerate(pairs)]

    @pl.when(t < na_ref[0])
    def _():
        @pl.when(tf_ref[t] == 1)
        def _():
            slot = ts_ref[t]

            @pl.when(t == 0)
            def _():
                for copy in weight_copies(te_ref[t], slot):
                    copy.start()

            for copy in weight_copies(te_ref[t], slot):
                copy.wait()

            @pl.when(tn_ref[t] >= 0)
            def _():
                for copy in weight_copies(tn_ref[t], 1 - slot):
                    copy.start()

            w1_scr[...] = w1_buf[slot].astype(BF16)
            w3_scr[...] = w3_buf[slot].astype(BF16)
            w2_scr[...] = w2_buf[slot].astype(BF16)

        row = lax.broadcasted_iota(jnp.int32, (MOE_TILE, PACK_WORDS), 0)
        live = row < tr_ref[t]
        halves = [jnp.where(live, xs_ref[h], jnp.zeros((MOE_TILE, PACK_WORDS), jnp.uint32)) for h in range(2)]
        chunks = _unpack_rows(*halves)

        def up(w_scr):
            acc = jnp.dot(chunks[0], w_scr[:PACK_WORDS, :], preferred_element_type=F32)
            for k in range(1, len(chunks)):
                acc = acc + jnp.dot(chunks[k], w_scr[k * PACK_WORDS:(k + 1) * PACK_WORDS, :],
                                    preferred_element_type=F32)
            return acc

        hid = (_silu(up(w1_scr)) * up(w3_scr)).astype(BF16)
        for h, words in enumerate(_pack_rows(jnp.dot(hid, w2_scr[...], preferred_element_type=F32))):
            ys_ref[h] = words


def _ffn(xs, tables, layer, w1, w3, w2):
    row_tile = lambda t, te, tr, tf, na, ts, tn: (0, jnp.minimum(t, na[0] - 1), 0)
    hbm = pl.BlockSpec(memory_space=pl.ANY)
    return pl.pallas_call(
        functools.partial(_ffn_kernel, layer=layer),
        out_shape=jax.ShapeDtypeStruct((2, MOE_ROWS, PACK_WORDS), jnp.uint32),
        grid_spec=pltpu.PrefetchScalarGridSpec(
            num_scalar_prefetch=len(tables),
            grid=(MOE_ROWS // MOE_TILE,),
            in_specs=[pl.BlockSpec((2, MOE_TILE, PACK_WORDS), row_tile), hbm, hbm, hbm],
            out_specs=pl.BlockSpec((2, MOE_TILE, PACK_WORDS), row_tile),
            scratch_shapes=[
                pltpu.VMEM((D_MODEL, D_EXPERT), BF16),
                pltpu.VMEM((D_MODEL, D_EXPERT), BF16),
                pltpu.VMEM((D_EXPERT, D_MODEL), BF16),
                pltpu.VMEM((2, D_MODEL, D_EXPERT), F32),
                pltpu.VMEM((2, D_MODEL, D_EXPERT), F32),
                pltpu.VMEM((2, D_EXPERT, D_MODEL), F32),
                pltpu.SemaphoreType.DMA((2, 3)),
            ],
        ),
        compiler_params=_params(32, "arbitrary"),
        name="moe_experts",
    )(*tables, xs, w1, w3, w2)


def _moe_res(x_ref, y_ref, rw_ref, mod_ref):
    cond = _tile_cond(pl.program_id(0))
    rw = rw_ref[...]
    wa = rw[:, 0:1]
    wb = rw[:, 1:2]
    ya = _unpack_rows(y_ref[0], y_ref[1])
    yb = _unpack_rows(y_ref[2], y_ref[3])
    y = jnp.concatenate([wa * a.astype(F32) + wb * b.astype(F32) for a, b in zip(ya, yb)], axis=1)
    return x_ref[...] + _mod_row(mod_ref, cond, 5) * y


def _moe_res_kernel(x_ref, y_ref, rw_ref, mod_ref, o_ref):
    o_ref[...] = _moe_res(x_ref, y_ref, rw_ref, mod_ref)


def _moe_res_final_kernel(x_ref, y_ref, rw_ref, mod_ref, fn_ref, op_ref, os_ref):
    x = _moe_res(x_ref, y_ref, rw_ref, mod_ref)
    y = x * lax.rsqrt(jnp.mean(x * x, axis=-1, keepdims=True) + EPS) * fn_ref[...]
    is_prompt = pl.program_id(0) < PROMPT_TILES

    @pl.when(is_prompt)
    def _():
        op_ref[...] = y

    @pl.when(jnp.logical_not(is_prompt))
    def _():
        os_ref[...] = y


def _moe_combine(x1, y_pairs, route_w, mod, final_norm=None):
    tile = lambda i: (i, 0)
    fixed2 = lambda i: (0, 0)
    blk = pl.BlockSpec((TM, D_MODEL), tile)
    specs = [blk,
             pl.BlockSpec((4, TM, PACK_WORDS), lambda i: (0, i, 0)),
             pl.BlockSpec((TM, ROUTER_LANES), tile),
             pl.BlockSpec(mod.shape, fixed2)]
    args = [x1, y_pairs, route_w, mod]
    if final_norm is None:
        kernel_fn = _moe_res_kernel
        out_shape = jax.ShapeDtypeStruct((N_TOK, D_MODEL), F32)
        out_specs = blk
    else:
        kernel_fn = _moe_res_final_kernel
        specs.append(pl.BlockSpec((1, D_MODEL), fixed2))
        args.append(final_norm)
        out_shape = (jax.ShapeDtypeStruct((N_PROMPT, D_MODEL), F32),
                     jax.ShapeDtypeStruct((N_TOK - N_PROMPT, D_MODEL), F32))
        out_specs = (pl.BlockSpec((TM, D_MODEL), _prompt_tile), pl.BlockSpec((TM, D_MODEL), _sample_tile))
    return pl.pallas_call(
        kernel_fn,
        out_shape=out_shape,
        grid=(N_TILES,),
        in_specs=specs,
        out_specs=out_specs,
        compiler_params=_params(32, "arbitrary"),
        name="moe_combine",
    )(*args)


def _moe_layer(x1, h2p, route_i, route_w, counts, mod, layer, w1, w3, w2, final_norm=None):
    pos, tables = _moe_plan(route_i, counts)
    idx_a = jnp.concatenate([pos[:, 0], pos[:, 0] + MOE_ROWS])
    idx_b = jnp.concatenate([pos[:, 1], pos[:, 1] + MOE_ROWS])
    xs = _sc_scatter_rows(h2p.reshape(2 * N_TOK, PACK_WORDS), idx_a, idx_b, 2 * MOE_ROWS)
    ys = _ffn(xs.reshape(2, MOE_ROWS, PACK_WORDS), tables, layer, w1, w3, w2)
    y_pairs = _sc_gather_rows(ys.reshape(2 * MOE_ROWS, PACK_WORDS), jnp.concatenate([idx_a, idx_b]))
    return _moe_combine(x1, y_pairs.reshape(4, N_TOK, PACK_WORDS), route_w, mod, final_norm)


def _swap_rotary_halves(x):
    n = x.shape[-1]
    lane = lax.broadcasted_iota(jnp.int32, x.shape, 1)
    quarter = ROPE_HALF // 2
    return jnp.where((lane % ROPE_HALF) < quarter, pltpu.roll(x, n - quarter, 1), pltpu.roll(x, quarter, 1))


def _at_in_kernel(x_ref, mod_ref, nw_ref, w_ref, cos_ref, sin_ref, q_ref, kt_ref, v_ref, kc_ref, vc_ref):
    i = pl.program_id(0)
    cond = _tile_cond(i)
    h = _norm_mod(x_ref[...], nw_ref[...], _mod_row(mod_ref, cond, 0), _mod_row(mod_ref, cond, 1)).astype(BF16)
    nq = ATT_Q_HEADS * ATT_HEAD_DIM
    nk = ATT_KV_HEADS * ATT_HEAD_DIM
    cos = cos_ref[...]
    sin = sin_ref[...]

    def rope(x):
        reps = x.shape[-1] // LANES
        return x * jnp.concatenate([cos] * reps, axis=1) + _swap_rotary_halves(x) * jnp.concatenate([sin] * reps, axis=1)

    q_ref[...] = (rope(jnp.dot(h, w_ref[:, :nq], preferred_element_type=F32)) * ATT_HEAD_DIM ** -0.5).astype(BF16)
    k = rope(jnp.dot(h, w_ref[:, nq:nq + nk], preferred_element_type=F32))
    v = jnp.dot(h, w_ref[:, nq + nk:], preferred_element_type=F32)
    kt = k.T
    kt_ref[...] = kt.astype(BF16)
    v_ref[...] = v.astype(BF16)

    @pl.when(i < PROMPT_TILES)
    def _():
        kc_ref[...] = kt
        vc_ref[...] = v.T


def _rope_tables():
    f32 = np.float32
    pos = np.arange(DEC_SEQ)
    t_row = (pos // GRID_W).astype(f32)
    t_col = (pos % GRID_W).astype(f32)
    inv = f32(ROPE_BASE) ** (-np.arange(0, ROPE_HALF, 2, dtype=f32) / f32(ROPE_HALF))
    j = np.arange(LANES) % ATT_HEAD_DIM
    freq = inv[(j % ROPE_HALF) % (ROPE_HALF // 2)]
    ang = (np.where((j < ROPE_HALF)[None, :], t_row[:, None], t_col[:, None]) * freq[None, :]).astype(f32)
    sign = np.where((j % ROPE_HALF) < ROPE_HALF // 2, -1.0, 1.0).astype(f32)
    cos = np.concatenate([np.ones((TM, LANES), f32), np.cos(ang)], axis=0)
    sin = np.concatenate([np.zeros((TM, LANES), f32), np.sin(ang) * sign[None, :]], axis=0)
    return jnp.asarray(cos, F32), jnp.asarray(sin, F32)


def _at_in(x, mod, nw, w_in, cos, sin):
    tile = lambda i: (i, 0)
    fixed2 = lambda i: (0, 0)
    rope_tile = lambda i: (jnp.where(i < PROMPT_TILES, 0, 1 + (i - PROMPT_TILES) % TILES_PER_DEC_SEQ), 0)
    nk = ATT_KV_HEADS * ATT_HEAD_DIM
    return pl.pallas_call(
        _at_in_kernel,
        out_shape=(
            jax.ShapeDtypeStruct((N_TOK, D_MODEL), BF16),
            jax.ShapeDtypeStruct((nk, N_TOK), BF16),
            jax.ShapeDtypeStruct((N_TOK, nk), BF16),
            jax.ShapeDtypeStruct((N_PROMPT, nk), F32),
            jax.ShapeDtypeStruct((N_PROMPT, nk), F32),
        ),
        grid=(N_TILES,),
        in_specs=[
            pl.BlockSpec((TM, D_MODEL), tile),
            pl.BlockSpec(mod.shape, fixed2),
            pl.BlockSpec((1, D_MODEL), fixed2),
            pl.BlockSpec(w_in.shape, fixed2),
            pl.BlockSpec((TM, LANES), rope_tile),
            pl.BlockSpec((TM, LANES), rope_tile),
        ],
        out_specs=(pl.BlockSpec((TM, D_MODEL), tile), pl.BlockSpec((nk, TM), lambda i: (0, i)),
                   pl.BlockSpec((TM, nk), tile),
                   pl.BlockSpec((TM, nk), _prompt_tile), pl.BlockSpec((TM, nk), _prompt_tile)),
        compiler_params=_params(40, "arbitrary"),
        name="attn_in_proj",
    )(x, mod, nw, w_in, cos, sin)


def _attend(q, kt_all, v_all, mask, sink_ref, o_ref):
    nq = q.shape[0]
    group_lanes = ATT_GROUP * ATT_HEAD_DIM
    lane = lax.broadcasted_iota(jnp.int32, (nq, group_lanes), 1)
    mine = [(lane // ATT_HEAD_DIM) == g for g in range(ATT_GROUP)]
    row_head = lax.broadcasted_iota(jnp.int32, (ATT_GROUP * nq, 1), 0) // nq
    if mask is not None:
        mask = jnp.concatenate([mask] * ATT_GROUP, axis=0)
    for hk in range(ATT_KV_HEADS):
        vh = v_all[:, hk * ATT_HEAD_DIM:(hk + 1) * ATT_HEAD_DIM]
        kt = jnp.concatenate([kt_all[hk * ATT_HEAD_DIM:(hk + 1) * ATT_HEAD_DIM, :]] * ATT_GROUP, axis=0)
        vt = jnp.concatenate([vh] * ATT_GROUP, axis=1)
        qg = q[:, hk * group_lanes:(hk + 1) * group_lanes]
        q_stack = jnp.concatenate([jnp.where(mine[g], qg, jnp.zeros_like(qg)) for g in range(ATT_GROUP)], axis=0)
        s = jnp.dot(q_stack, kt, preferred_element_type=F32)
        if mask is not None:
            s = jnp.where(mask, s, -jnp.inf)
        sink = jnp.zeros((ATT_GROUP * nq, 1), F32)
        for g in range(ATT_GROUP):
            sink = jnp.where(row_head == g, sink_ref[hk * ATT_GROUP + g], sink)
        m = jnp.maximum(jnp.max(s, axis=-1, keepdims=True), sink)
        p = jnp.exp(s - m)
        denom = jnp.sum(p, axis=-1, keepdims=True) + jnp.exp(sink - m)
        o = jnp.dot(p.astype(BF16), vt, preferred_element_type=F32) / denom
        acc = jnp.where(mine[0], o[:nq], 0.0)
        for g in range(1, ATT_GROUP):
            acc = acc + jnp.where(mine[g], o[g * nq:(g + 1) * nq], 0.0)
        o_ref[:, hk * group_lanes:(hk + 1) * group_lanes] = acc.astype(BF16)


def _ctx_attn_kernel(sink_ref, q_ref, k_ref, v_ref, o_ref):
    _attend(q_ref[...], k_ref[...], v_ref[...], None, sink_ref, o_ref)


def _lat_attn_kernel(sink_ref, q_ref, kp_ref, kc_ref, kn_ref, vp_ref, vc_ref, vn_ref, ck_ref, cv_ref, o_ref):
    jb = pl.program_id(1)
    kt_all = jnp.concatenate([kp_ref[...], kc_ref[...], kn_ref[...], ck_ref[0].astype(BF16)], axis=1)
    v_all = jnp.concatenate([vp_ref[...], vc_ref[...], vn_ref[...], cv_ref[0].astype(BF16)], axis=0)
    nkeys = 3 * BLOCK + PAST_LEN
    qi = lax.broadcasted_iota(jnp.int32, (BLOCK, nkeys), 0)
    kj = lax.broadcasted_iota(jnp.int32, (BLOCK, nkeys), 1)
    qpos = jb * BLOCK + qi
    kpos = (jb - 1) * BLOCK + kj
    local_ok = (jnp.abs(qpos - kpos) <= WINDOW) & (kpos >= 0) & (kpos < DEC_SEQ)
    mask = (kj >= 3 * BLOCK) | local_ok
    _attend(q_ref[...], kt_all, v_all, mask, sink_ref, o_ref)


def _attention(q, kt, v, cache_kt, cache_v, sink):
    nk = ATT_KV_HEADS * ATT_HEAD_DIM
    smem = pl.BlockSpec(memory_space=pltpu.SMEM)
    ctx = pl.pallas_call(
        _ctx_attn_kernel,
        out_shape=jax.ShapeDtypeStruct((N_PROMPT, D_MODEL), BF16),
        grid=(BATCH,),
        in_specs=[
            smem,
            pl.BlockSpec((SEQ, D_MODEL), lambda b: (b, 0)),
            pl.BlockSpec((nk, SEQ), lambda b: (0, b)),
            pl.BlockSpec((SEQ, nk), lambda b: (b, 0)),
        ],
        out_specs=pl.BlockSpec((SEQ, D_MODEL), lambda b: (b, 0)),
        compiler_params=_params(40, "arbitrary"),
        name="context_attention",
    )(sink, q, kt, v)

    nb = DEC_SEQ // BLOCK
    base = N_PROMPT // BLOCK
    cur = lambda b, j: (base + b * nb + j, 0)
    prev = lambda b, j: (base + b * nb + jnp.maximum(j - 1, 0), 0)
    nxt = lambda b, j: (base + b * nb + jnp.minimum(j + 1, nb - 1), 0)
    kv_blk = lambda f: pl.BlockSpec((BLOCK, nk), f)
    kt_blk = lambda f: pl.BlockSpec((nk, BLOCK), lambda b, j: f(b, j)[::-1])
    cache_blk = pl.BlockSpec((1, PAST_LEN, nk), lambda b, j: (b, 0, 0))
    cache_kt_blk = pl.BlockSpec((1, nk, PAST_LEN), lambda b, j: (b, 0, 0))
    lat = pl.pallas_call(
        _lat_attn_kernel,
        out_shape=jax.ShapeDtypeStruct((DEC_BATCH * DEC_SEQ, D_MODEL), BF16),
        grid=(DEC_BATCH, nb),
        in_specs=[
            smem,
            pl.BlockSpec((BLOCK, D_MODEL), cur),
            kt_blk(prev), kt_blk(cur), kt_blk(nxt),
            kv_blk(prev), kv_blk(cur), kv_blk(nxt),
            cache_kt_blk, cache_blk,
        ],
        out_specs=pl.BlockSpec((BLOCK, D_MODEL), lambda b, j: (b * nb + j, 0)),
        compiler_params=_params(40, "arbitrary", "arbitrary"),
        name="latent_attention",
    )(sink, q, kt, kt, kt, v, v, v, cache_kt, cache_v)
    return ctx, lat


def kernel(x_prompt, x_sample, state_hgrn, cache_k, cache_v, c, c_ctx, ada_w, ada_b, norm_w, hg_w_in,
           hg_lb_logits, hg_onorm, hg_w_out, at_w_in, at_sink, at_w_out, moe_w_group, moe_b_group,
           moe_w_expert, moe_b_expert, moe_w1, moe_w3, moe_w2, final_norm):
    xp = x_prompt.reshape(N_PROMPT, D_MODEL)
    xs = x_sample.reshape(N_TOK - N_PROMPT, D_MODEL)
    cond = jnp.concatenate([c_ctx[None, :], c], axis=0)
    mod = _ada(cond, ada_w, ada_b)
    nk = ATT_KV_HEADS * ATT_HEAD_DIM

    def router_params(i):
        pad = jnp.zeros((D_MODEL, ROUTER_LANES - N_EXPERTS - N_GROUPS), F32)
        w = jnp.concatenate([moe_w_expert[i], moe_w_group[i], pad], axis=1)
        b = jnp.concatenate([moe_b_expert[i], moe_b_group[i], pad[0]])[None, :]
        hi = w.astype(BF16)
        lo = (w - hi.astype(F32)).astype(BF16)
        return jnp.stack([hi, lo]), b

    tile = lambda i: (i, 0)
    blk = pl.BlockSpec((TM, D_MODEL), tile)

    prompt_blk = pl.BlockSpec((TM, D_MODEL), _prompt_tile)
    sample_blk = pl.BlockSpec((TM, D_MODEL), _sample_tile)
    q, v, g, lff, kf, lfb, kb = _hg_in(xp, xs, mod[0], norm_w[0, 0][None, :], hg_w_in[0].astype(BF16), hg_lb_logits)
    o_f, o_b, state_new = _gla(q, v, lff, kf, lfb, kb, state_hgrn)
    wr, br = router_params(0)
    routed = _mixer_out(
        _hg_out_kernel, "hgrn_out_route", (o_f, o_b, g, hg_onorm[0][None, :], xp, xs),
        (blk, blk, blk, pl.BlockSpec((1, HG_DK), lambda i: (0, 0)), prompt_blk, sample_blk),
        mod[0], norm_w[0, 1][None, :], hg_w_out[0].astype(BF16), wr, br)
    x = _moe_layer(*routed, mod[0], 0, moe_w1, moe_w3, moe_w2)

    cos, sin = _rope_tables()
    qa, ka, va, k_ctx, v_ctx = _at_in(x, mod[1], norm_w[1, 0][None, :], at_w_in[0].astype(BF16), cos, sin)
    attn_ctx, attn_lat = _attention(qa, ka, va, cache_k[:, 0].reshape(DEC_BATCH, PAST_LEN, nk).transpose(0, 2, 1),
                                    cache_v[:, 0].reshape(DEC_BATCH, PAST_LEN, nk), at_sink[0])
    wr, br = router_params(1)
    routed = _mixer_out(
        _at_out_kernel, "attn_out_route", (attn_ctx, attn_lat, x), (prompt_blk, sample_blk, blk),
        mod[1], norm_w[1, 1][None, :], at_w_out[0].astype(BF16), wr, br)
    y_prompt, y_sample = _moe_layer(*routed, mod[1], 1, moe_w1, moe_w3, moe_w2, final_norm[None, :])

    def cache(feature_major):
        return feature_major.reshape(BATCH, 1, ATT_KV_HEADS, ATT_HEAD_DIM, SEQ).transpose(0, 1, 4, 2, 3)

    return (y_prompt.reshape(BATCH, SEQ, D_MODEL), y_sample.reshape(DEC_BATCH, DEC_SEQ, D_MODEL), state_new,
            cache(k_ctx), cache(v_ctx))
```

```python
import functools
from typing import Any, NamedTuple

import jax
import jax.numpy as jnp
import numpy as np
from jax import lax
from jax.experimental import pallas as pl
from jax.experimental.pallas import tpu as pltpu
from jax.experimental.pallas import tpu_sc as plsc

F32 = jnp.float32
BF16 = jnp.bfloat16

D_MODEL = 1024
BATCH = 16
SEQ = 256
DEC_BATCH = 2
DEC_SEQ = 1024
PAST_LEN = 512
GRID_W = 64
HG_HEADS = 8
HG_DK = 128
CHUNK = 16
ATT_HEAD_DIM = 64
ATT_Q_HEADS = 16
ATT_KV_HEADS = 4
ATT_GROUP = 4
WINDOW = 128
BLOCK = 128
ROPE_HALF = 32
ROPE_BASE = 10000.0
N_GROUPS = 4
EXPERTS_PER_GROUP = 8
N_EXPERTS = 32
D_EXPERT = 256
EPS = 1e-6

N_PROMPT = BATCH * SEQ
N_TOK = N_PROMPT + DEC_BATCH * DEC_SEQ
TM = 256
N_TILES = N_TOK // TM
PROMPT_TILES = N_PROMPT // TM
TILES_PER_DEC_SEQ = DEC_SEQ // TM
LANES = 128
N_COND_USED = 1 + DEC_BATCH
N_COND = 8
ROUTER_LANES = 128
ROUTE_TM = 1024
HG_IN_TM = 256
GLA_UNROLL = 2
DECAY_CLAMP = 60.0
PACK_WORDS = D_MODEL // 4
MOE_TILE = 256
MOE_ROWS = 2 * N_TOK + N_EXPERTS * MOE_TILE
MIB = 1024 * 1024


def _params(vmem_mib, *semantics):
    return pltpu.CompilerParams(dimension_semantics=semantics, vmem_limit_bytes=vmem_mib * MIB)


def _tile_cond(i):
    return jnp.where(i < PROMPT_TILES, 0, 1 + (i - PROMPT_TILES) // TILES_PER_DEC_SEQ)


def _prompt_tile(i):
    return (jnp.minimum(i, PROMPT_TILES - 1), 0)


def _sample_tile(i):
    return (jnp.maximum(i - PROMPT_TILES, 0), 0)


def _stream_tile(prompt_ref, sample_ref):
    return jnp.where(pl.program_id(0) < PROMPT_TILES, prompt_ref[...], sample_ref[...])


def _mod_row(mod_ref, cond, which):
    return mod_ref[pl.ds(cond, 1), which * D_MODEL:(which + 1) * D_MODEL]


def _norm_mod(x, nw, shift, scale):
    y = x * lax.rsqrt(jnp.mean(x * x, axis=-1, keepdims=True) + EPS)
    return (y * nw) * (1.0 + scale) + shift


def _silu(x):
    return x * jax.nn.sigmoid(x)


def _ada_kernel(c_ref, w_ref, b_ref, o_ref):
    s = [_silu(c_ref[r]) for r in range(N_COND_USED)]
    tn = w_ref.shape[-1]
    rows = []
    for r in range(N_COND_USED):
        cols = [jnp.sum(w_ref[0, :, j * LANES:(j + 1) * LANES] * s[r], axis=0, keepdims=True)
                for j in range(tn // LANES)]
        rows.append(jnp.concatenate(cols, axis=1) + b_ref[0])
    rows.append(jnp.zeros((N_COND - N_COND_USED, tn), F32))
    o_ref[0] = jnp.concatenate(rows, axis=0)


def _ada(cond, ada_w, ada_b):
    depth, _, n = ada_w.shape
    tn = 1536
    cond_cols = jnp.broadcast_to(cond[:, :, None], (N_COND_USED, D_MODEL, LANES))
    return pl.pallas_call(
        _ada_kernel,
        out_shape=jax.ShapeDtypeStruct((depth, N_COND, n), F32),
        grid=(depth, n // tn),
        in_specs=[
            pl.BlockSpec((N_COND_USED, D_MODEL, LANES), lambda l, j: (0, 0, 0)),
            pl.BlockSpec((1, D_MODEL, tn), lambda l, j: (l, 0, j)),
            pl.BlockSpec((1, 1, tn), lambda l, j: (l, 0, j)),
        ],
        out_specs=pl.BlockSpec((1, N_COND, tn), lambda l, j: (l, 0, j)),
        compiler_params=_params(40, "arbitrary", "arbitrary"),
        name="ada_modulation",
    )(cond_cols, ada_w, ada_b.reshape(depth, 1, n))


def _hg_in_kernel(xp_ref, xs_ref, mod_ref, nw_ref, w_ref, lbl_ref,
                  q_ref, v_ref, g_ref, lff_ref, kf_ref, lfb_ref, kb_ref):
    i = pl.program_id(0) * (HG_IN_TM // TM)
    cond = _tile_cond(i)
    x = jnp.where(i < PROMPT_TILES, xp_ref[...], xs_ref[...])
    h = _norm_mod(x, nw_ref[...], _mod_row(mod_ref, cond, 0), _mod_row(mod_ref, cond, 1)).astype(BF16)

    def proj(c):
        return jnp.dot(h, w_ref[:, c * D_MODEL:(c + 1) * D_MODEL], preferred_element_type=F32)

    l0, l1, l2 = lbl_ref[0], lbl_ref[1], lbl_ref[2]
    m = jnp.maximum(jnp.maximum(l0, l1), l2)
    e0, e1, e2 = jnp.exp(l0 - m), jnp.exp(l1 - m), jnp.exp(l2 - m)
    lb = e0 / (e0 + e1 + e2)

    q_ref[...] = proj(0).astype(BF16)
    v_ref[...] = proj(1).astype(BF16)
    for d, (lf_ref, k_ref) in enumerate(((lff_ref, kf_ref), (lfb_ref, kb_ref))):
        lbd = lb[d:d + 1, :]
        f = lbd + (1.0 - lbd) * jax.nn.sigmoid(proj(2 + d))
        lf = jnp.log(f)
        hi = lf.astype(BF16)
        lf_ref[0] = hi
        lf_ref[1] = (lf - hi.astype(F32)).astype(BF16)
        k_ref[...] = (1.0 - f).astype(BF16)
    g_ref[...] = proj(4).astype(BF16)


def _hg_in(xp, xs, mod, nw, w_in, lb_logits):
    tile = lambda i: (i, 0)
    fixed2 = lambda i: (0, 0)
    bf = jax.ShapeDtypeStruct((N_TOK, D_MODEL), BF16)
    ff = jax.ShapeDtypeStruct((2, N_TOK, D_MODEL), BF16)
    blk = pl.BlockSpec((HG_IN_TM, D_MODEL), tile)
    split_blk = pl.BlockSpec((2, HG_IN_TM, D_MODEL), lambda i: (0, i, 0))
    prompt_steps = N_PROMPT // HG_IN_TM
    return pl.pallas_call(
        _hg_in_kernel,
        out_shape=(bf, bf, bf, ff, bf, ff, bf),
        grid=(N_TOK // HG_IN_TM,),
        in_specs=[
            pl.BlockSpec((HG_IN_TM, D_MODEL), lambda i: (jnp.minimum(i, prompt_steps - 1), 0)),
            pl.BlockSpec((HG_IN_TM, D_MODEL), lambda i: (jnp.maximum(i - prompt_steps, 0), 0)),
            pl.BlockSpec(mod.shape, fixed2),
            pl.BlockSpec((1, D_MODEL), fixed2),
            pl.BlockSpec(w_in.shape, fixed2, pipeline_mode=pl.Buffered(1)),
            pl.BlockSpec(lb_logits.shape, lambda i: (0, 0, 0)),
        ],
        out_specs=(blk, blk, blk, split_blk, blk, split_blk, blk),
        compiler_params=_params(56, "arbitrary"),
        name="hgrn_in_proj",
    )(xp, xs, mod, nw, w_in, lb_logits)


def _gla_exact(reverse, q_ref, v_ref, lf_ref, k_ref, st_ref, o_ref):
    nt = (((1,), (1,)), ((), ()))
    tn = (((0,), (0,)), ((), ()))
    n_chunks = TM // CHUNK
    row_id = lax.broadcasted_iota(jnp.int32, (CHUNK, HG_DK), 0)
    order = range(CHUNK - 1, -1, -1) if reverse else range(CHUNK)

    def chunk_head(it, carry):
        step, h = it // HG_HEADS, it % HG_HEADS
        ci = (n_chunks - 1 - step) if reverse else step
        rows = pl.ds(pl.multiple_of(ci * CHUNK, CHUNK), CHUNK)
        cols = pl.ds(pl.multiple_of(h * HG_DK, HG_DK), HG_DK)
        q, k, v = q_ref[rows, cols], k_ref[rows, cols], v_ref[rows, cols]
        f = jnp.exp(lf_ref[0, rows, cols].astype(F32) + lf_ref[1, rows, cols].astype(F32))
        st = st_ref[0, h]
        o = jnp.zeros((CHUNK, HG_DK), F32)
        for t in order:
            one = row_id == t
            v_t = jnp.where(one, v, jnp.zeros_like(v))
            st = st * f[t:t + 1, :] + lax.dot_general(v_t, k, tn, preferred_element_type=F32)
            o = jnp.where(one, lax.dot_general(q, st.astype(BF16), nt, preferred_element_type=F32), o)
        st_ref[0, h] = st
        o_ref[rows, cols] = o
        return carry

    lax.fori_loop(0, n_chunks * HG_HEADS, chunk_head, 0)


class _GlaDirection(NamedTuple):
    reverse: bool
    q: Any
    v: Any
    lf: Any
    k: Any
    o: Any
    st: Any
    sw: Any
    ut: Any
    qd: Any
    kd: Any
    ki: Any
    cd: Any


_NT = (((1,), (1,)), ((), ()))
_TN = (((0,), (0,)), ((), ()))
_HG_COLS = [slice(h * HG_DK, (h + 1) * HG_DK) for h in range(HG_HEADS)]
_N_CHUNKS = TM // CHUNK
_N_PAIRS = _N_CHUNKS // 2


def _gla_prepare(d):
    r = lax.broadcasted_iota(jnp.int32, (TM, TM), 0)
    c = lax.broadcasted_iota(jnp.int32, (TM, TM), 1)
    same = (r // CHUNK) == (c // CHUNK)
    tri = (same & ((c >= r) if d.reverse else (c <= r))).astype(BF16)
    b = jnp.dot(tri, d.lf[0], preferred_element_type=F32) + jnp.dot(tri, d.lf[1], preferred_element_type=F32)
    b3 = b.reshape(_N_CHUNKS, CHUNK, D_MODEL)
    edge = 0 if d.reverse else CHUNK - 1
    tot = jnp.broadcast_to(b3[:, edge:edge + 1, :], b3.shape).reshape(TM, D_MODEL)
    k = d.k[...]
    d.qd[...] = d.q[...] * jnp.exp(b).astype(BF16)
    d.kd[...] = k * jnp.exp(tot - b).astype(BF16)
    d.ki[...] = k * jnp.exp(jnp.minimum(-b, DECAY_CLAMP)).astype(BF16)
    d.cd[...] = jnp.exp(tot)
    return jnp.max(-b3[:, edge:edge + 1, :]) <= DECAY_CLAMP


def _gla_pair_rows(d, step):
    pi = (_N_PAIRS - 1 - step) if d.reverse else step
    row0 = pl.multiple_of(pi * 2 * CHUNK, 2 * CHUNK)
    lo, hi = pl.ds(row0, CHUNK), pl.ds(row0 + CHUNK, CHUNK)
    return pl.ds(row0, 2 * CHUNK), (hi, lo) if d.reverse else (lo, hi)


def _rows_scaled(x, scale_row, second_half):
    scale = jnp.broadcast_to(scale_row.astype(BF16), (CHUNK, x.shape[1]))
    ones = jnp.ones((CHUNK, x.shape[1]), BF16)
    return x * jnp.concatenate([ones, scale] if second_half else [scale, ones], axis=0)


def _gla_key_value_product(d, step):
    both, (first, second) = _gla_pair_rows(d, step)
    decay_second = d.cd[pl.ds(second.start, 1), :]
    for h, cols in enumerate(_HG_COLS):
        keys = _rows_scaled(d.kd[both, cols], decay_second[:, cols], second_half=d.reverse)
        d.ut[h] = lax.dot_general(d.v[both, cols], keys, _TN, preferred_element_type=F32)


def _gla_start(d):
    for h in range(HG_HEADS):
        d.sw[0, h] = d.st[0, h].T.astype(BF16)
    _gla_key_value_product(d, 0)


def _gla_pair(d, step, src, dst):
    sr = lax.broadcasted_iota(jnp.int32, (CHUNK, CHUNK), 0)
    sc = lax.broadcasted_iota(jnp.int32, (CHUNK, CHUNK), 1)
    tr = lax.broadcasted_iota(jnp.int32, (CHUNK, 2 * CHUNK), 0)
    tc = lax.broadcasted_iota(jnp.int32, (CHUNK, 2 * CHUNK), 1)
    if d.reverse:
        keep_first = sc >= sr
        keep_second = (tc >= tr) | (tc >= CHUNK)
    else:
        keep_first = sc <= sr
        keep_second = (tc < CHUNK) | (tc - CHUNK <= tr)
    both, (first, second) = _gla_pair_rows(d, step)
    decay_first = d.cd[pl.ds(first.start, 1), :]
    decay_second = d.cd[pl.ds(second.start, 1), :]
    decay_both = decay_first * decay_second
    for h, cols in enumerate(_HG_COLS):
        new = d.st[src, h] * decay_both[:, cols] + d.ut[h]
        d.st[dst, h] = new
        d.sw[dst, h] = new.T.astype(BF16)
    a_first = [lax.dot_general(d.qd[first, cols], d.ki[first, cols], _NT, preferred_element_type=F32)
               for cols in _HG_COLS]
    a_second = []
    for cols in _HG_COLS:
        lo_keys = (d.ki if d.reverse else d.kd)[pl.ds(both.start, CHUNK), cols]
        hi_keys = (d.kd if d.reverse else d.ki)[pl.ds(both.start + CHUNK, CHUNK), cols]
        keys = jnp.concatenate([lo_keys, hi_keys], axis=0)
        a_second.append(lax.dot_general(d.qd[second, cols], keys, _NT, preferred_element_type=F32))
    inter = [jnp.dot(_rows_scaled(d.qd[both, cols], decay_first[:, cols], second_half=not d.reverse),
                     d.sw[src, h], preferred_element_type=F32) for h, cols in enumerate(_HG_COLS)]
    _gla_key_value_product(d, jnp.minimum(step + 1, _N_PAIRS - 1))
    first_half, second_half = (slice(CHUNK, None), slice(None, CHUNK)) if d.reverse else \
                              (slice(None, CHUNK), slice(CHUNK, None))
    for h, cols in enumerate(_HG_COLS):
        am = jnp.where(keep_first, a_first[h], 0.0).astype(BF16)
        d.o[first, cols] = jnp.dot(am, d.v[first, cols], preferred_element_type=F32) + inter[h][first_half]
        am = jnp.where(keep_second, a_second[h], 0.0).astype(BF16)
        d.o[second, cols] = jnp.dot(am, d.v[both, cols], preferred_element_type=F32) + inter[h][second_half]


def _gla_kernel(qf_ref, vf_ref, lff_ref, kf_ref, qb_ref, vb_ref, lfb_ref, kb_ref, s0_ref,
                of_ref, ob_ref, sout_ref, st_scr, sw_scr, ut_scr, qd_scr, kd_scr, ki_scr, cd_scr):
    i = pl.program_id(0)
    is_prompt = i < PROMPT_TILES
    first = jnp.logical_or(is_prompt, (i - PROMPT_TILES) % TILES_PER_DEC_SEQ == 0)
    work = [tuple(scr.at[n] for scr in (st_scr, sw_scr, ut_scr, qd_scr, kd_scr, ki_scr, cd_scr)) for n in range(2)]
    fwd = _GlaDirection(False, qf_ref, vf_ref, lff_ref, kf_ref, of_ref, *work[0])
    bwd = _GlaDirection(True, qb_ref, vb_ref, lfb_ref, kb_ref, ob_ref, *work[1])

    @pl.when(first)
    def _():
        keep0 = jnp.where(is_prompt, 0.0, 1.0)
        for n, d in enumerate((fwd, bwd)):
            for h in range(HG_HEADS):
                d.st[0, h] = s0_ref[0, 0, n, h].T * keep0

    exact_scores = jnp.logical_and(_gla_prepare(fwd), _gla_prepare(bwd))

    @pl.when(exact_scores)
    def _():
        _gla_start(fwd)
        _gla_start(bwd)

        def pair_group(it, carry):
            for p in range(GLA_UNROLL):
                for d in (fwd, bwd):
                    _gla_pair(d, GLA_UNROLL * it + p, p % 2, 1 - p % 2)
            return carry

        lax.fori_loop(0, _N_PAIRS // GLA_UNROLL, pair_group, 0)

    @pl.when(jnp.logical_not(exact_scores))
    def _():
        for d in (fwd, bwd):
            _gla_exact(d.reverse, d.q, d.v, d.lf, d.k, d.st, d.o)

    @pl.when(is_prompt)
    def _():
        for n, d in enumerate((fwd, bwd)):
            for h in range(HG_HEADS):
                sout_ref[0, 0, n, h] = d.st[0, h].T


def _gla(q, v, lff, kf, lfb, kb, state_hgrn):
    def fwd_tile(i):
        return (i, 0)

    def bwd_tile(i):
        j = (i - PROMPT_TILES) % TILES_PER_DEC_SEQ
        return (jnp.where(i < PROMPT_TILES, i, i - j + (TILES_PER_DEC_SEQ - 1 - j)), 0)

    def s0_idx(i):
        return (jnp.maximum(i - PROMPT_TILES, 0) // TILES_PER_DEC_SEQ, 0, 0, 0, 0, 0)

    def sout_idx(i):
        return (jnp.minimum(i, PROMPT_TILES - 1), 0, 0, 0, 0, 0)

    f_blk = pl.BlockSpec((TM, D_MODEL), fwd_tile)
    b_blk = pl.BlockSpec((TM, D_MODEL), bwd_tile)
    f_split = pl.BlockSpec((2, TM, D_MODEL), lambda i: (0,) + fwd_tile(i))
    b_split = pl.BlockSpec((2, TM, D_MODEL), lambda i: (0,) + bwd_tile(i))
    st_blk = (1, 1, 2, HG_HEADS, HG_DK, HG_DK)
    return pl.pallas_call(
        _gla_kernel,
        out_shape=(
            jax.ShapeDtypeStruct((N_TOK, D_MODEL), F32),
            jax.ShapeDtypeStruct((N_TOK, D_MODEL), F32),
            jax.ShapeDtypeStruct((BATCH,) + st_blk[1:], F32),
        ),
        grid=(N_TILES,),
        in_specs=[f_blk, f_blk, f_split, f_blk, b_blk, b_blk, b_split, b_blk, pl.BlockSpec(st_blk, s0_idx)],
        out_specs=(f_blk, b_blk, pl.BlockSpec(st_blk, sout_idx)),
        scratch_shapes=[
            pltpu.VMEM((2, 2, HG_HEADS, HG_DK, HG_DK), F32),
            pltpu.VMEM((2, 2, HG_HEADS, HG_DK, HG_DK), BF16),
            pltpu.VMEM((2, HG_HEADS, HG_DK, HG_DK), F32),
            pltpu.VMEM((2, TM, D_MODEL), BF16),
            pltpu.VMEM((2, TM, D_MODEL), BF16),
            pltpu.VMEM((2, TM, D_MODEL), BF16),
            pltpu.VMEM((2, TM, D_MODEL), F32),
        ],
        compiler_params=_params(48, "arbitrary"),
        name="hgrn_recurrence",
    )(q, v, lff, kf, q, v, lfb, kb, state_hgrn)


def _route(logits):
    lane = lax.broadcasted_iota(jnp.int32, logits.shape, 1)
    neg = jnp.float32(-jnp.inf)
    big = jnp.int32(ROUTER_LANES)

    def first_max(x):
        m = jnp.max(x, axis=-1, keepdims=True)
        return m, jnp.min(jnp.where(x == m, lane, big), axis=-1, keepdims=True)

    is_group = (lane >= N_EXPERTS) & (lane < N_EXPERTS + N_GROUPS)
    gl = jnp.where(is_group, logits, neg)
    gmax, g_lane = first_max(gl)
    g_sel = g_lane - N_EXPERTS
    gsum = jnp.sum(jnp.exp(gl - gmax), axis=-1, keepdims=True)
    p_g = 1.0 / gsum
    in_sel = (lane < N_EXPERTS) & ((lane // EXPERTS_PER_GROUP) == g_sel)
    m1, i1 = first_max(jnp.where(in_sel, logits, neg))
    m2, i2 = first_max(jnp.where(in_sel & (lane != i1), logits, neg))
    e2 = jnp.exp(m2 - m1)
    return i1, i2, p_g / (1.0 + e2), p_g * e2 / (1.0 + e2)


def _pack_rows(x):
    q = PACK_WORDS
    bits = pltpu.bitcast(x.astype(BF16).astype(F32), jnp.uint32)
    return [(bits[:, (2 + h) * q:(3 + h) * q] & jnp.uint32(0xFFFF0000)) | (bits[:, h * q:(h + 1) * q] >> 16)
            for h in range(2)]


def _unpack_rows(half0, half1):
    lo = lambda w: pltpu.bitcast(w << 16, F32).astype(BF16)
    hi = lambda w: pltpu.bitcast(w & jnp.uint32(0xFFFF0000), F32).astype(BF16)
    return [lo(half0), lo(half1), hi(half0), hi(half1)]


def _mixer_tail(mix_bf16, x, mod_ref, nw2_ref, wo_ref, wr_ref, br_ref, x1_ref, h2p_ref, lg_ref):
    cond = _tile_cond(pl.program_id(0))
    out = jnp.dot(mix_bf16, wo_ref[...], preferred_element_type=F32)
    x1 = x + _mod_row(mod_ref, cond, 2) * out
    x1_ref[...] = x1
    h2 = _norm_mod(x1, nw2_ref[...], _mod_row(mod_ref, cond, 3), _mod_row(mod_ref, cond, 4))
    for h, words in enumerate(_pack_rows(h2)):
        h2p_ref[h] = words
    h_hi = h2.astype(BF16)
    h_lo = (h2 - h_hi.astype(F32)).astype(BF16)
    lg_ref[...] = (jnp.dot(h_hi, wr_ref[0], preferred_element_type=F32)
                   + (jnp.dot(h_hi, wr_ref[1], preferred_element_type=F32)
                      + jnp.dot(h_lo, wr_ref[0], preferred_element_type=F32))) + br_ref[...]


def _route_kernel(lg_ref, ri_ref, rw_ref, cnt_ref, carry_scr):
    @pl.when(pl.program_id(0) == 0)
    def _():
        carry_scr[...] = jnp.zeros_like(carry_scr)

    logits = lg_ref[...]
    i1, i2, w1, w2 = _route(logits)
    lane = lax.broadcasted_iota(jnp.int32, logits.shape, 1)
    chosen = ((lane == i1) | (lane == i2)).astype(BF16)
    r = lax.broadcasted_iota(jnp.int32, (ROUTE_TM, ROUTE_TM), 0)
    c = lax.broadcasted_iota(jnp.int32, (ROUTE_TM, ROUTE_TM), 1)
    before = jnp.dot((c < r).astype(BF16), chosen, preferred_element_type=F32) + carry_scr[...]
    r1 = jnp.sum(jnp.where(lane == i1, before, 0.0), axis=-1, keepdims=True).astype(jnp.int32)
    r2 = jnp.sum(jnp.where(lane == i2, before, 0.0), axis=-1, keepdims=True).astype(jnp.int32)
    total = carry_scr[...] + jnp.sum(chosen.astype(F32), axis=0, keepdims=True)
    carry_scr[...] = total
    cnt_ref[...] = total
    ri_ref[...] = jnp.where(lane == 0, i1, jnp.where(lane == 1, i2, jnp.where(lane == 2, r1, r2)))
    rw_ref[...] = jnp.where(lane == 0, w1, w2)


def _route_tokens(logits):
    blk = pl.BlockSpec((ROUTE_TM, ROUTER_LANES), lambda i: (i, 0))
    return pl.pallas_call(
        _route_kernel,
        out_shape=(
            jax.ShapeDtypeStruct((N_TOK, ROUTER_LANES), jnp.int32),
            jax.ShapeDtypeStruct((N_TOK, ROUTER_LANES), F32),
            jax.ShapeDtypeStruct((1, ROUTER_LANES), F32),
        ),
        grid=(N_TOK // ROUTE_TM,),
        in_specs=[blk],
        out_specs=(blk, blk, pl.BlockSpec((1, ROUTER_LANES), lambda i: (0, 0))),
        scratch_shapes=[pltpu.VMEM((1, ROUTER_LANES), F32)],
        compiler_params=_params(32, "arbitrary"),
        name="moe_route",
    )(logits)


def _hg_out_kernel(of_ref, ob_ref, g_ref, on_ref, xp_ref, xs_ref, *rest):
    o = of_ref[...] + ob_ref[...]
    parts = []
    for h in range(HG_HEADS):
        oh = o[:, h * HG_DK:(h + 1) * HG_DK]
        parts.append(oh * lax.rsqrt(jnp.mean(oh * oh, axis=-1, keepdims=True) + EPS) * on_ref[...])
    y = jnp.concatenate(parts, axis=1) * _silu(g_ref[...].astype(F32))
    _mixer_tail(y.astype(BF16), _stream_tile(xp_ref, xs_ref), *rest)


def _at_out_kernel(ac_ref, al_ref, x_ref, *rest):
    _mixer_tail(_stream_tile(ac_ref, al_ref), x_ref[...], *rest)


def _mixer_out(kernel_fn, name, mix_inputs, mix_specs, mod, nw2, w_out, w_router, b_router):
    tile = lambda i: (i, 0)
    fixed2 = lambda i: (0, 0)
    blk = pl.BlockSpec((TM, D_MODEL), tile)
    lanes_blk = pl.BlockSpec((TM, ROUTER_LANES), tile)
    return pl.pallas_call(
        kernel_fn,
        out_shape=(
            jax.ShapeDtypeStruct((N_TOK, D_MODEL), F32),
            jax.ShapeDtypeStruct((2, N_TOK, PACK_WORDS), jnp.uint32),
            jax.ShapeDtypeStruct((N_TOK, ROUTER_LANES), F32),
        ),
        grid=(N_TILES,),
        in_specs=list(mix_specs) + [
            pl.BlockSpec(mod.shape, fixed2),
            pl.BlockSpec((1, D_MODEL), fixed2),
            pl.BlockSpec(w_out.shape, fixed2),
            pl.BlockSpec(w_router.shape, lambda i: (0, 0, 0)),
            pl.BlockSpec((1, ROUTER_LANES), fixed2),
        ],
        out_specs=(blk, pl.BlockSpec((2, TM, PACK_WORDS), lambda i: (0, i, 0)), lanes_blk),
        compiler_params=_params(40, "arbitrary"),
        name=name,
    )(*mix_inputs, mod, nw2, w_out, w_router, b_router)


def _moe_plan(route_i, counts):
    cnt = counts[0, :N_EXPERTS].astype(jnp.int32)
    padded = ((cnt + MOE_TILE - 1) // MOE_TILE) * MOE_TILE
    ends = jnp.cumsum(padded)
    offs = ends - padded
    experts = route_i[:, 0:2]
    pos = jnp.sum(jnp.where(experts[:, :, None] == jnp.arange(N_EXPERTS)[None, None, :], offs[None, None, :], 0),
                  axis=-1) + route_i[:, 2:4]
    tile_start = jnp.arange(MOE_ROWS // MOE_TILE, dtype=jnp.int32) * MOE_TILE
    tile_expert = jnp.minimum(jnp.sum(ends[None, :] <= tile_start[:, None], axis=1), N_EXPERTS - 1).astype(jnp.int32)
    of_tile = tile_expert[:, None] == jnp.arange(N_EXPERTS)[None, :]
    tile_offs = jnp.sum(jnp.where(of_tile, offs[None, :], 0), axis=1)
    tile_cnt = jnp.sum(jnp.where(of_tile, cnt[None, :], 0), axis=1)
    tile_rows = jnp.clip(tile_offs + tile_cnt - tile_start, 0, MOE_TILE).astype(jnp.int32)
    tile_first = (tile_start == tile_offs).astype(jnp.int32)
    n_active = (ends[-1] // MOE_TILE).astype(jnp.int32).reshape(1)
    ids = jnp.arange(N_EXPERTS)
    used = cnt > 0
    slot = (jnp.cumsum(used.astype(jnp.int32)) - 1) % 2
    later = jnp.min(jnp.where(used[None, :] & (ids[None, :] > ids[:, None]), ids[None, :], N_EXPERTS), axis=1)
    later = jnp.where(later == N_EXPERTS, -1, later)
    tile_slot = jnp.sum(jnp.where(of_tile, slot[None, :], 0), axis=1).astype(jnp.int32)
    tile_next = jnp.sum(jnp.where(of_tile, later[None, :], 0), axis=1).astype(jnp.int32)
    return pos.astype(jnp.int32), (tile_expert, tile_rows, tile_first, n_active, tile_slot, tile_next)


SC_WINDOW = 128


def _sc_mesh():
    return plsc.VectorSubcoreMesh(core_axis_name="c", subcore_axis_name="s")


def _sc_scatter_rows(x, idx_a, idx_b, n_out_rows):
    n = x.shape[0]

    @functools.partial(pl.kernel, out_type=jax.ShapeDtypeStruct((n_out_rows, PACK_WORDS), x.dtype), mesh=_sc_mesh(),
                       scratch_types=[pltpu.SemaphoreType.DMA, pltpu.SemaphoreType.DMA])
    def scatter(x_hbm, ia_hbm, ib_hbm, o_hbm, sem_a, sem_b):
        def body(x_vmem, ia_vmem, ib_vmem):
            copy_a = pltpu.async_copy(x_vmem, o_hbm.at[ia_vmem.at[0]], sem_a)
            copy_b = pltpu.async_copy(x_vmem, o_hbm.at[ib_vmem.at[0]], sem_b)
            copy_a.wait()
            copy_b.wait()

        idx_spec = pl.BlockSpec((1, SC_WINDOW), index_map=lambda i: (0, i))
        pltpu.emit_pipeline(
            body, grid=(n // SC_WINDOW,),
            in_specs=[pl.BlockSpec((SC_WINDOW, PACK_WORDS), index_map=lambda i: (i, 0)), idx_spec, idx_spec],
            out_specs=[],
            core_axis_name=("c", "s"), dimension_semantics=(pltpu.PARALLEL,),
        )(x_hbm, ia_hbm, ib_hbm)

    return scatter(x, idx_a.reshape(1, n), idx_b.reshape(1, n))


def _sc_gather_rows(table, idx):
    n = idx.shape[0]

    @functools.partial(pl.kernel, out_type=jax.ShapeDtypeStruct((n, PACK_WORDS), table.dtype), mesh=_sc_mesh())
    def gather(t_hbm, i_hbm, o_hbm):
        def body(i_vmem, o_vmem):
            pltpu.sync_copy(t_hbm.at[i_vmem.at[0]], o_vmem)

        pltpu.emit_pipeline(
            body, grid=(n // SC_WINDOW,),
            in_specs=[pl.BlockSpec((1, SC_WINDOW), index_map=lambda i: (0, i))],
            out_specs=[pl.BlockSpec((SC_WINDOW, PACK_WORDS), index_map=lambda i: (i, 0))],
            core_axis_name=("c", "s"), dimension_semantics=(pltpu.PARALLEL,),
        )(i_hbm, o_hbm)

    return gather(table, idx.reshape(1, n))


def _ffn_kernel(te_ref, tr_ref, tf_ref, na_ref, ts_ref, tn_ref, xs_ref, w1_hbm, w3_hbm, w2_hbm, ys_ref,
                w1_scr, w3_scr, w2_scr, w1_buf, w3_buf, w2_buf, sem, *, layer):
    t = pl.program_id(0)

    def weight_copies(expert, slot):
        pairs = ((w1_hbm, w1_buf), (w3_hbm, w3_buf), (w2_hbm, w2_buf))
        return [pltpu.make_async_copy(hbm.at[layer, expert], buf.at[slot], sem.at[slot, j])
                for j, (hbm, buf) in enumerate(pairs)]

    @pl.when(t < na_ref[0])
    def _():
        @pl.when(tf_ref[t] == 1)
        def _():
            slot = ts_ref[t]

            @pl.when(t == 0)
            def _():
                for copy in weight_copies(te_ref[t], slot):
                    copy.start()

            for copy in weight_copies(te_ref[t], slot):
                copy.wait()

            @pl.when(tn_ref[t] >= 0)
            def _():
                for copy in weight_copies(tn_ref[t], 1 - slot):
                    copy.start()

            w1_scr[...] = w1_buf[slot].astype(BF16)
            w3_scr[...] = w3_buf[slot].astype(BF16)
            w2_scr[...] = w2_buf[slot].astype(BF16)

        row = lax.broadcasted_iota(jnp.int32, (MOE_TILE, PACK_WORDS), 0)
        live = row < tr_ref[t]
        halves = [jnp.where(live, xs_ref[h], jnp.zeros((MOE_TILE, PACK_WORDS), jnp.uint32)) for h in range(2)]
        chunks = _unpack_rows(*halves)

        def up(w_scr):
            acc = jnp.dot(chunks[0], w_scr[:PACK_WORDS, :], preferred_element_type=F32)
            for k in range(1, len(chunks)):
                acc = acc + jnp.dot(chunks[k], w_scr[k * PACK_WORDS:(k + 1) * PACK_WORDS, :],
                                    preferred_element_type=F32)
            return acc

        hid = (_silu(up(w1_scr)) * up(w3_scr)).astype(BF16)
        for h, words in enumerate(_pack_rows(jnp.dot(hid, w2_scr[...], preferred_element_type=F32))):
            ys_ref[h] = words


def _ffn(xs, tables, layer, w1, w3, w2):
    row_tile = lambda t, te, tr, tf, na, ts, tn: (0, jnp.minimum(t, na[0] - 1), 0)
    hbm = pl.BlockSpec(memory_space=pl.ANY)
    return pl.pallas_call(
        functools.partial(_ffn_kernel, layer=layer),
        out_shape=jax.ShapeDtypeStruct((2, MOE_ROWS, PACK_WORDS), jnp.uint32),
        grid_spec=pltpu.PrefetchScalarGridSpec(
            num_scalar_prefetch=len(tables),
            grid=(MOE_ROWS // MOE_TILE,),
            in_specs=[pl.BlockSpec((2, MOE_TILE, PACK_WORDS), row_tile), hbm, hbm, hbm],
            out_specs=pl.BlockSpec((2, MOE_TILE, PACK_WORDS), row_tile),
            scratch_shapes=[
                pltpu.VMEM((D_MODEL, D_EXPERT), BF16),
                pltpu.VMEM((D_MODEL, D_EXPERT), BF16),
                pltpu.VMEM((D_EXPERT, D_MODEL), BF16),
                pltpu.VMEM((2, D_MODEL, D_EXPERT), F32),
                pltpu.VMEM((2, D_MODEL, D_EXPERT), F32),
                pltpu.VMEM((2, D_EXPERT, D_MODEL), F32),
                pltpu.SemaphoreType.DMA((2, 3)),
            ],
        ),
        compiler_params=_params(32, "arbitrary"),
        name="moe_experts",
    )(*tables, xs, w1, w3, w2)


def _moe_res(x_ref, y_ref, rw_ref, mod_ref):
    cond = _tile_cond(pl.program_id(0))
    rw = rw_ref[...]
    wa = rw[:, 0:1]
    wb = rw[:, 1:2]
    ya = _unpack_rows(y_ref[0], y_ref[1])
    yb = _unpack_rows(y_ref[2], y_ref[3])
    y = jnp.concatenate([wa * a.astype(F32) + wb * b.astype(F32) for a, b in zip(ya, yb)], axis=1)
    return x_ref[...] + _mod_row(mod_ref, cond, 5) * y


def _moe_res_kernel(x_ref, y_ref, rw_ref, mod_ref, o_ref):
    o_ref[...] = _moe_res(x_ref, y_ref, rw_ref, mod_ref)


def _moe_res_final_kernel(x_ref, y_ref, rw_ref, mod_ref, fn_ref, op_ref, os_ref):
    x = _moe_res(x_ref, y_ref, rw_ref, mod_ref)
    y = x * lax.rsqrt(jnp.mean(x * x, axis=-1, keepdims=True) + EPS) * fn_ref[...]
    is_prompt = pl.program_id(0) < PROMPT_TILES

    @pl.when(is_prompt)
    def _():
        op_ref[...] = y

    @pl.when(jnp.logical_not(is_prompt))
    def _():
        os_ref[...] = y


def _moe_combine(x1, y_pairs, route_w, mod, final_norm=None):
    tile = lambda i: (i, 0)
    fixed2 = lambda i: (0, 0)
    blk = pl.BlockSpec((TM, D_MODEL), tile)
    specs = [blk,
             pl.BlockSpec((4, TM, PACK_WORDS), lambda i: (0, i, 0)),
             pl.BlockSpec((TM, ROUTER_LANES), tile),
             pl.BlockSpec(mod.shape, fixed2)]
    args = [x1, y_pairs, route_w, mod]
    if final_norm is None:
        kernel_fn = _moe_res_kernel
        out_shape = jax.ShapeDtypeStruct((N_TOK, D_MODEL), F32)
        out_specs = blk
    else:
        kernel_fn = _moe_res_final_kernel
        specs.append(pl.BlockSpec((1, D_MODEL), fixed2))
        args.append(final_norm)
        out_shape = (jax.ShapeDtypeStruct((N_PROMPT, D_MODEL), F32),
                     jax.ShapeDtypeStruct((N_TOK - N_PROMPT, D_MODEL), F32))
        out_specs = (pl.BlockSpec((TM, D_MODEL), _prompt_tile), pl.BlockSpec((TM, D_MODEL), _sample_tile))
    return pl.pallas_call(
        kernel_fn,
        out_shape=out_shape,
        grid=(N_TILES,),
        in_specs=specs,
        out_specs=out_specs,
        compiler_params=_params(32, "arbitrary"),
        name="moe_combine",
    )(*args)


def _moe_layer(x1, h2p, logits, mod, layer, w1, w3, w2, final_norm=None):
    route_i, route_w, counts = _route_tokens(logits)
    pos, tables = _moe_plan(route_i, counts)
    idx_a = jnp.concatenate([pos[:, 0], pos[:, 0] + MOE_ROWS])
    idx_b = jnp.concatenate([pos[:, 1], pos[:, 1] + MOE_ROWS])
    xs = _sc_scatter_rows(h2p.reshape(2 * N_TOK, PACK_WORDS), idx_a, idx_b, 2 * MOE_ROWS)
    ys = _ffn(xs.reshape(2, MOE_ROWS, PACK_WORDS), tables, layer, w1, w3, w2)
    y_pairs = _sc_gather_rows(ys.reshape(2 * MOE_ROWS, PACK_WORDS), jnp.concatenate([idx_a, idx_b]))
    return _moe_combine(x1, y_pairs.reshape(4, N_TOK, PACK_WORDS), route_w, mod, final_norm)


def _swap_rotary_halves(x):
    n = x.shape[-1]
    lane = lax.broadcasted_iota(jnp.int32, x.shape, 1)
    quarter = ROPE_HALF // 2
    return jnp.where((lane % ROPE_HALF) < quarter, pltpu.roll(x, n - quarter, 1), pltpu.roll(x, quarter, 1))


def _at_in_kernel(x_ref, mod_ref, nw_ref, w_ref, cos_ref, sin_ref, q_ref, kt_ref, v_ref, kc_ref, vc_ref):
    i = pl.program_id(0)
    cond = _tile_cond(i)
    h = _norm_mod(x_ref[...], nw_ref[...], _mod_row(mod_ref, cond, 0), _mod_row(mod_ref, cond, 1)).astype(BF16)
    nq = ATT_Q_HEADS * ATT_HEAD_DIM
    nk = ATT_KV_HEADS * ATT_HEAD_DIM
    cos = cos_ref[...]
    sin = sin_ref[...]

    def rope(x):
        reps = x.shape[-1] // LANES
        return x * jnp.concatenate([cos] * reps, axis=1) + _swap_rotary_halves(x) * jnp.concatenate([sin] * reps, axis=1)

    q_ref[...] = (rope(jnp.dot(h, w_ref[:, :nq], preferred_element_type=F32)) * ATT_HEAD_DIM ** -0.5).astype(BF16)
    k = rope(jnp.dot(h, w_ref[:, nq:nq + nk], preferred_element_type=F32))
    v = jnp.dot(h, w_ref[:, nq + nk:], preferred_element_type=F32)
    kt = k.T
    kt_ref[...] = kt.astype(BF16)
    v_ref[...] = v.astype(BF16)

    @pl.when(i < PROMPT_TILES)
    def _():
        kc_ref[...] = kt
        vc_ref[...] = v.T


def _rope_tables():
    f32 = np.float32
    pos = np.arange(DEC_SEQ)
    t_row = (pos // GRID_W).astype(f32)
    t_col = (pos % GRID_W).astype(f32)
    inv = f32(ROPE_BASE) ** (-np.arange(0, ROPE_HALF, 2, dtype=f32) / f32(ROPE_HALF))
    j = np.arange(LANES) % ATT_HEAD_DIM
    freq = inv[(j % ROPE_HALF) % (ROPE_HALF // 2)]
    ang = (np.where((j < ROPE_HALF)[None, :], t_row[:, None], t_col[:, None]) * freq[None, :]).astype(f32)
    sign = np.where((j % ROPE_HALF) < ROPE_HALF // 2, -1.0, 1.0).astype(f32)
    cos = np.concatenate([np.ones((TM, LANES), f32), np.cos(ang)], axis=0)
    sin = np.concatenate([np.zeros((TM, LANES), f32), np.sin(ang) * sign[None, :]], axis=0)
    return jnp.asarray(cos, F32), jnp.asarray(sin, F32)


def _at_in(x, mod, nw, w_in, cos, sin):
    tile = lambda i: (i, 0)
    fixed2 = lambda i: (0, 0)
    rope_tile = lambda i: (jnp.where(i < PROMPT_TILES, 0, 1 + (i - PROMPT_TILES) % TILES_PER_DEC_SEQ), 0)
    nk = ATT_KV_HEADS * ATT_HEAD_DIM
    return pl.pallas_call(
        _at_in_kernel,
        out_shape=(
            jax.ShapeDtypeStruct((N_TOK, D_MODEL), BF16),
            jax.ShapeDtypeStruct((nk, N_TOK), BF16),
            jax.ShapeDtypeStruct((N_TOK, nk), BF16),
            jax.ShapeDtypeStruct((N_PROMPT, nk), F32),
            jax.ShapeDtypeStruct((N_PROMPT, nk), F32),
        ),
        grid=(N_TILES,),
        in_specs=[
            pl.BlockSpec((TM, D_MODEL), tile),
            pl.BlockSpec(mod.shape, fixed2),
            pl.BlockSpec((1, D_MODEL), fixed2),
            pl.BlockSpec(w_in.shape, fixed2),
            pl.BlockSpec((TM, LANES), rope_tile),
            pl.BlockSpec((TM, LANES), rope_tile),
        ],
        out_specs=(pl.BlockSpec((TM, D_MODEL), tile), pl.BlockSpec((nk, TM), lambda i: (0, i)),
                   pl.BlockSpec((TM, nk), tile),
                   pl.BlockSpec((TM, nk), _prompt_tile), pl.BlockSpec((TM, nk), _prompt_tile)),
        compiler_params=_params(40, "arbitrary"),
        name="attn_in_proj",
    )(x, mod, nw, w_in, cos, sin)


def _attend(q, kt_all, v_all, mask, sink_ref, o_ref):
    nq = q.shape[0]
    group_lanes = ATT_GROUP * ATT_HEAD_DIM
    lane = lax.broadcasted_iota(jnp.int32, (nq, group_lanes), 1)
    mine = [(lane // ATT_HEAD_DIM) == g for g in range(ATT_GROUP)]
    row_head = lax.broadcasted_iota(jnp.int32, (ATT_GROUP * nq, 1), 0) // nq
    if mask is not None:
        mask = jnp.concatenate([mask] * ATT_GROUP, axis=0)
    for hk in range(ATT_KV_HEADS):
        vh = v_all[:, hk * ATT_HEAD_DIM:(hk + 1) * ATT_HEAD_DIM]
        kt = jnp.concatenate([kt_all[hk * ATT_HEAD_DIM:(hk + 1) * ATT_HEAD_DIM, :]] * ATT_GROUP, axis=0)
        vt = jnp.concatenate([vh] * ATT_GROUP, axis=1)
        qg = q[:, hk * group_lanes:(hk + 1) * group_lanes]
        q_stack = jnp.concatenate([jnp.where(mine[g], qg, jnp.zeros_like(qg)) for g in range(ATT_GROUP)], axis=0)
        s = jnp.dot(q_stack, kt, preferred_element_type=F32)
        if mask is not None:
            s = jnp.where(mask, s, -jnp.inf)
        sink = jnp.zeros((ATT_GROUP * nq, 1), F32)
        for g in range(ATT_GROUP):
            sink = jnp.where(row_head == g, sink_ref[hk * ATT_GROUP + g], sink)
        m = jnp.maximum(jnp.max(s, axis=-1, keepdims=True), sink)
        p = jnp.exp(s - m)
        denom = jnp.sum(p, axis=-1, keepdims=True) + jnp.exp(sink - m)
        o = jnp.dot(p.astype(BF16), vt, preferred_element_type=F32) / denom
        acc = jnp.where(mine[0], o[:nq], 0.0)
        for g in range(1, ATT_GROUP):
            acc = acc + jnp.where(mine[g], o[g * nq:(g + 1) * nq], 0.0)
        o_ref[:, hk * group_lanes:(hk + 1) * group_lanes] = acc.astype(BF16)


def _ctx_attn_kernel(sink_ref, q_ref, k_ref, v_ref, o_ref):
    _attend(q_ref[...], k_ref[...], v_ref[...], None, sink_ref, o_ref)


def _lat_attn_kernel(sink_ref, q_ref, kp_ref, kc_ref, kn_ref, vp_ref, vc_ref, vn_ref, ck_ref, cv_ref, o_ref):
    jb = pl.program_id(1)
    kt_all = jnp.concatenate([kp_ref[...], kc_ref[...], kn_ref[...], ck_ref[0].astype(BF16)], axis=1)
    v_all = jnp.concatenate([vp_ref[...], vc_ref[...], vn_ref[...], cv_ref[0].astype(BF16)], axis=0)
    nkeys = 3 * BLOCK + PAST_LEN
    qi = lax.broadcasted_iota(jnp.int32, (BLOCK, nkeys), 0)
    kj = lax.broadcasted_iota(jnp.int32, (BLOCK, nkeys), 1)
    qpos = jb * BLOCK + qi
    kpos = (jb - 1) * BLOCK + kj
    local_ok = (jnp.abs(qpos - kpos) <= WINDOW) & (kpos >= 0) & (kpos < DEC_SEQ)
    mask = (kj >= 3 * BLOCK) | local_ok
    _attend(q_ref[...], kt_all, v_all, mask, sink_ref, o_ref)


def _attention(q, kt, v, cache_kt, cache_v, sink):
    nk = ATT_KV_HEADS * ATT_HEAD_DIM
    smem = pl.BlockSpec(memory_space=pltpu.SMEM)
    ctx = pl.pallas_call(
        _ctx_attn_kernel,
        out_shape=jax.ShapeDtypeStruct((N_PROMPT, D_MODEL), BF16),
        grid=(BATCH,),
        in_specs=[
            smem,
            pl.BlockSpec((SEQ, D_MODEL), lambda b: (b, 0)),
            pl.BlockSpec((nk, SEQ), lambda b: (0, b)),
            pl.BlockSpec((SEQ, nk), lambda b: (b, 0)),
        ],
        out_specs=pl.BlockSpec((SEQ, D_MODEL), lambda b: (b, 0)),
        compiler_params=_params(40, "arbitrary"),
        name="context_attention",
    )(sink, q, kt, v)

    nb = DEC_SEQ // BLOCK
    base = N_PROMPT // BLOCK
    cur = lambda b, j: (base + b * nb + j, 0)
    prev = lambda b, j: (base + b * nb + jnp.maximum(j - 1, 0), 0)
    nxt = lambda b, j: (base + b * nb + jnp.minimum(j + 1, nb - 1), 0)
    kv_blk = lambda f: pl.BlockSpec((BLOCK, nk), f)
    kt_blk = lambda f: pl.BlockSpec((nk, BLOCK), lambda b, j: f(b, j)[::-1])
    cache_blk = pl.BlockSpec((1, PAST_LEN, nk), lambda b, j: (b, 0, 0))
    cache_kt_blk = pl.BlockSpec((1, nk, PAST_LEN), lambda b, j: (b, 0, 0))
    lat = pl.pallas_call(
        _lat_attn_kernel,
        out_shape=jax.ShapeDtypeStruct((DEC_BATCH * DEC_SEQ, D_MODEL), BF16),
        grid=(DEC_BATCH, nb),
        in_specs=[
            smem,
            pl.BlockSpec((BLOCK, D_MODEL), cur),
            kt_blk(prev), kt_blk(cur), kt_blk(nxt),
            kv_blk(prev), kv_blk(cur), kv_blk(nxt),
            cache_kt_blk, cache_blk,
        ],
        out_specs=pl.BlockSpec((BLOCK, D_MODEL), lambda b, j: (b * nb + j, 0)),
        compiler_params=_params(40, "arbitrary", "arbitrary"),
        name="latent_attention",
    )(sink, q, kt, kt, kt, v, v, v, cache_kt, cache_v)
    return ctx, lat


def kernel(x_prompt, x_sample, state_hgrn, cache_k, cache_v, c, c_ctx, ada_w, ada_b, norm_w, hg_w_in,
           hg_lb_logits, hg_onorm, hg_w_out, at_w_in, at_sink, at_w_out, moe_w_group, moe_b_group,
           moe_w_expert, moe_b_expert, moe_w1, moe_w3, moe_w2, final_norm):
    xp = x_prompt.reshape(N_PROMPT, D_MODEL)
    xs = x_sample.reshape(N_TOK - N_PROMPT, D_MODEL)
    cond = jnp.concatenate([c_ctx[None, :], c], axis=0)
    mod = _ada(cond, ada_w, ada_b)
    nk = ATT_KV_HEADS * ATT_HEAD_DIM

    def router_params(i):
        pad = jnp.zeros((D_MODEL, ROUTER_LANES - N_EXPERTS - N_GROUPS), F32)
        w = jnp.concatenate([moe_w_expert[i], moe_w_group[i], pad], axis=1)
        b = jnp.concatenate([moe_b_expert[i], moe_b_group[i], pad[0]])[None, :]
        hi = w.astype(BF16)
        lo = (w - hi.astype(F32)).astype(BF16)
        return jnp.stack([hi, lo]), b

    tile = lambda i: (i, 0)
    blk = pl.BlockSpec((TM, D_MODEL), tile)

    prompt_blk = pl.BlockSpec((TM, D_MODEL), _prompt_tile)
    sample_blk = pl.BlockSpec((TM, D_MODEL), _sample_tile)
    q, v, g, lff, kf, lfb, kb = _hg_in(xp, xs, mod[0], norm_w[0, 0][None, :], hg_w_in[0].astype(BF16), hg_lb_logits)
    o_f, o_b, state_new = _gla(q, v, lff, kf, lfb, kb, state_hgrn)
    wr, br = router_params(0)
    routed = _mixer_out(
        _hg_out_kernel, "hgrn_out_route", (o_f, o_b, g, hg_onorm[0][None, :], xp, xs),
        (blk, blk, blk, pl.BlockSpec((1, HG_DK), lambda i: (0, 0)), prompt_blk, sample_blk),
        mod[0], norm_w[0, 1][None, :], hg_w_out[0].astype(BF16), wr, br)
    x = _moe_layer(*routed, mod[0], 0, moe_w1, moe_w3, moe_w2)

    cos, sin = _rope_tables()
    qa, ka, va, k_ctx, v_ctx = _at_in(x, mod[1], norm_w[1, 0][None, :], at_w_in[0].astype(BF16), cos, sin)
    attn_ctx, attn_lat = _attention(qa, ka, va, cache_k[:, 0].reshape(DEC_BATCH, PAST_LEN, nk).transpose(0, 2, 1),
                                    cache_v[:, 0].reshape(DEC_BATCH, PAST_LEN, nk), at_sink[0])
    wr, br = router_params(1)
    routed = _mixer_out(
        _at_out_kernel, "attn_out_route", (attn_ctx, attn_lat, x), (prompt_blk, sample_blk, blk),
        mod[1], norm_w[1, 1][None, :], at_w_out[0].astype(BF16), wr, br)
    y_prompt, y_sample = _moe_layer(*routed, mod[1], 1, moe_w1, moe_w3, moe_w2, final_norm[None, :])

    def cache(feature_major):
        return feature_major.reshape(BATCH, 1, ATT_KV_HEADS, ATT_HEAD_DIM, SEQ).transpose(0, 1, 4, 2, 3)

    return (y_prompt.reshape(BATCH, SEQ, D_MODEL), y_sample.reshape(DEC_BATCH, DEC_SEQ, D_MODEL), state_new,
            cache(k_ctx), cache(v_ctx))
```

```python
import functools
from typing import Any, NamedTuple

import jax
import jax.numpy as jnp
import numpy as np
from jax import lax
from jax.experimental import pallas as pl
from jax.experimental.pallas import tpu as pltpu
from jax.experimental.pallas import tpu_sc as plsc

F32 = jnp.float32
BF16 = jnp.bfloat16

D_MODEL = 1024
BATCH = 16
SEQ = 256
DEC_BATCH = 2
DEC_SEQ = 1024
PAST_LEN = 512
GRID_W = 64
HG_HEADS = 8
HG_DK = 128
CHUNK = 16
ATT_HEAD_DIM = 64
ATT_Q_HEADS = 16
ATT_KV_HEADS = 4
ATT_GROUP = 4
WINDOW = 128
BLOCK = 128
ROPE_HALF = 32
ROPE_BASE = 10000.0
N_GROUPS = 4
EXPERTS_PER_GROUP = 8
N_EXPERTS = 32
D_EXPERT = 256
EPS = 1e-6

N_PROMPT = BATCH * SEQ
N_TOK = N_PROMPT + DEC_BATCH * DEC_SEQ
TM = 256
N_TILES = N_TOK // TM
PROMPT_TILES = N_PROMPT // TM
TILES_PER_DEC_SEQ = DEC_SEQ // TM
LANES = 128
N_COND_USED = 1 + DEC_BATCH
N_COND = 8
ROUTER_LANES = 128
ROUTE_TM = 1024
HG_IN_TM = 256
GLA_UNROLL = 2
DECAY_CLAMP = 60.0
PACK_WORDS = D_MODEL // 4
MOE_TILE = 256
MOE_ROWS = 2 * N_TOK + N_EXPERTS * MOE_TILE
MIB = 1024 * 1024


def _params(vmem_mib, *semantics):
    return pltpu.CompilerParams(dimension_semantics=semantics, vmem_limit_bytes=vmem_mib * MIB)


def _tile_cond(i):
    return jnp.where(i < PROMPT_TILES, 0, 1 + (i - PROMPT_TILES) // TILES_PER_DEC_SEQ)


def _prompt_tile(i):
    return (jnp.minimum(i, PROMPT_TILES - 1), 0)


def _sample_tile(i):
    return (jnp.maximum(i - PROMPT_TILES, 0), 0)


def _stream_tile(prompt_ref, sample_ref):
    return jnp.where(pl.program_id(0) < PROMPT_TILES, prompt_ref[...], sample_ref[...])


def _mod_row(mod_ref, cond, which):
    return mod_ref[pl.ds(cond, 1), which * D_MODEL:(which + 1) * D_MODEL]


def _norm_mod(x, nw, shift, scale):
    y = x * lax.rsqrt(jnp.mean(x * x, axis=-1, keepdims=True) + EPS)
    return (y * nw) * (1.0 + scale) + shift


def _silu(x):
    return x * jax.nn.sigmoid(x)


def _ada_kernel(c_ref, w_ref, b_ref, o_ref):
    s = [_silu(c_ref[r]) for r in range(N_COND_USED)]
    tn = w_ref.shape[-1]
    rows = []
    for r in range(N_COND_USED):
        cols = [jnp.sum(w_ref[0, :, j * LANES:(j + 1) * LANES] * s[r], axis=0, keepdims=True)
                for j in range(tn // LANES)]
        rows.append(jnp.concatenate(cols, axis=1) + b_ref[0])
    rows.append(jnp.zeros((N_COND - N_COND_USED, tn), F32))
    o_ref[0] = jnp.concatenate(rows, axis=0)


def _ada(cond, ada_w, ada_b):
    depth, _, n = ada_w.shape
    tn = 1536
    cond_cols = jnp.broadcast_to(cond[:, :, None], (N_COND_USED, D_MODEL, LANES))
    return pl.pallas_call(
        _ada_kernel,
        out_shape=jax.ShapeDtypeStruct((depth, N_COND, n), F32),
        grid=(depth, n // tn),
        in_specs=[
            pl.BlockSpec((N_COND_USED, D_MODEL, LANES), lambda l, j: (0, 0, 0)),
            pl.BlockSpec((1, D_MODEL, tn), lambda l, j: (l, 0, j)),
            pl.BlockSpec((1, 1, tn), lambda l, j: (l, 0, j)),
        ],
        out_specs=pl.BlockSpec((1, N_COND, tn), lambda l, j: (l, 0, j)),
        compiler_params=_params(40, "arbitrary", "arbitrary"),
        name="ada_modulation",
    )(cond_cols, ada_w, ada_b.reshape(depth, 1, n))


def _hg_in_kernel(xp_ref, xs_ref, mod_ref, nw_ref, w_ref, lbl_ref,
                  q_ref, v_ref, g_ref, lff_ref, kf_ref, lfb_ref, kb_ref):
    i = pl.program_id(0) * (HG_IN_TM // TM)
    cond = _tile_cond(i)
    x = jnp.where(i < PROMPT_TILES, xp_ref[...], xs_ref[...])
    h = _norm_mod(x, nw_ref[...], _mod_row(mod_ref, cond, 0), _mod_row(mod_ref, cond, 1)).astype(BF16)

    def proj(c):
        return jnp.dot(h, w_ref[:, c * D_MODEL:(c + 1) * D_MODEL], preferred_element_type=F32)

    l0, l1, l2 = lbl_ref[0], lbl_ref[1], lbl_ref[2]
    m = jnp.maximum(jnp.maximum(l0, l1), l2)
    e0, e1, e2 = jnp.exp(l0 - m), jnp.exp(l1 - m), jnp.exp(l2 - m)
    lb = e0 / (e0 + e1 + e2)

    q_ref[...] = proj(0).astype(BF16)
    v_ref[...] = proj(1).astype(BF16)
    for d, (lf_ref, k_ref) in enumerate(((lff_ref, kf_ref), (lfb_ref, kb_ref))):
        lbd = lb[d:d + 1, :]
        f = lbd + (1.0 - lbd) * jax.nn.sigmoid(proj(2 + d))
        lf = jnp.log(f)
        hi = lf.astype(BF16)
        lf_ref[0] = hi
        lf_ref[1] = (lf - hi.astype(F32)).astype(BF16)
        k_ref[...] = (1.0 - f).astype(BF16)
    g_ref[...] = proj(4).astype(BF16)


def _hg_in(xp, xs, mod, nw, w_in, lb_logits):
    tile = lambda i: (i, 0)
    fixed2 = lambda i: (0, 0)
    bf = jax.ShapeDtypeStruct((N_TOK, D_MODEL), BF16)
    ff = jax.ShapeDtypeStruct((2, N_TOK, D_MODEL), BF16)
    blk = pl.BlockSpec((HG_IN_TM, D_MODEL), tile)
    split_blk = pl.BlockSpec((2, HG_IN_TM, D_MODEL), lambda i: (0, i, 0))
    prompt_steps = N_PROMPT // HG_IN_TM
    return pl.pallas_call(
        _hg_in_kernel,
        out_shape=(bf, bf, bf, ff, bf, ff, bf),
        grid=(N_TOK // HG_IN_TM,),
        in_specs=[
            pl.BlockSpec((HG_IN_TM, D_MODEL), lambda i: (jnp.minimum(i, prompt_steps - 1), 0)),
            pl.BlockSpec((HG_IN_TM, D_MODEL), lambda i: (jnp.maximum(i - prompt_steps, 0), 0)),
            pl.BlockSpec(mod.shape, fixed2),
            pl.BlockSpec((1, D_MODEL), fixed2),
            pl.BlockSpec(w_in.shape, fixed2, pipeline_mode=pl.Buffered(1)),
            pl.BlockSpec(lb_logits.shape, lambda i: (0, 0, 0)),
        ],
        out_specs=(blk, blk, blk, split_blk, blk, split_blk, blk),
        compiler_params=_params(56, "arbitrary"),
        name="hgrn_in_proj",
    )(xp, xs, mod, nw, w_in, lb_logits)


def _gla_exact(reverse, q_ref, v_ref, lf_ref, k_ref, st_ref, o_ref):
    nt = (((1,), (1,)), ((), ()))
    tn = (((0,), (0,)), ((), ()))
    n_chunks = TM // CHUNK
    row_id = lax.broadcasted_iota(jnp.int32, (CHUNK, HG_DK), 0)
    order = range(CHUNK - 1, -1, -1) if reverse else range(CHUNK)

    def chunk_head(it, carry):
        step, h = it // HG_HEADS, it % HG_HEADS
        ci = (n_chunks - 1 - step) if reverse else step
        rows = pl.ds(pl.multiple_of(ci * CHUNK, CHUNK), CHUNK)
        cols = pl.ds(pl.multiple_of(h * HG_DK, HG_DK), HG_DK)
        q, k, v = q_ref[rows, cols], k_ref[rows, cols], v_ref[rows, cols]
        f = jnp.exp(lf_ref[0, rows, cols].astype(F32) + lf_ref[1, rows, cols].astype(F32))
        st = st_ref[0, h]
        o = jnp.zeros((CHUNK, HG_DK), F32)
        for t in order:
            one = row_id == t
            v_t = jnp.where(one, v, jnp.zeros_like(v))
            st = st * f[t:t + 1, :] + lax.dot_general(v_t, k, tn, preferred_element_type=F32)
            o = jnp.where(one, lax.dot_general(q, st.astype(BF16), nt, preferred_element_type=F32), o)
        st_ref[0, h] = st
        o_ref[rows, cols] = o
        return carry

    lax.fori_loop(0, n_chunks * HG_HEADS, chunk_head, 0)


class _GlaDirection(NamedTuple):
    reverse: bool
    q: Any
    v: Any
    lf: Any
    k: Any
    o: Any
    st: Any
    sw: Any
    ut: Any
    qd: Any
    kd: Any
    ki: Any
    cd: Any


_NT = (((1,), (1,)), ((), ()))
_TN = (((0,), (0,)), ((), ()))
_HG_COLS = [slice(h * HG_DK, (h + 1) * HG_DK) for h in range(HG_HEADS)]
_N_CHUNKS = TM // CHUNK
_N_PAIRS = _N_CHUNKS // 2


def _gla_prepare(d):
    r = lax.broadcasted_iota(jnp.int32, (TM, TM), 0)
    c = lax.broadcasted_iota(jnp.int32, (TM, TM), 1)
    same = (r // CHUNK) == (c // CHUNK)
    tri = (same & ((c >= r) if d.reverse else (c <= r))).astype(BF16)
    b = jnp.dot(tri, d.lf[0], preferred_element_type=F32) + jnp.dot(tri, d.lf[1], preferred_element_type=F32)
    b3 = b.reshape(_N_CHUNKS, CHUNK, D_MODEL)
    edge = 0 if d.reverse else CHUNK - 1
    total = b3[:, edge:edge + 1, :]
    decay = jnp.broadcast_to(jnp.exp(total), b3.shape).reshape(TM, D_MODEL)
    ki = d.k[...] * jnp.exp(jnp.minimum(-b, DECAY_CLAMP)).astype(BF16)
    d.qd[...] = d.q[...] * jnp.exp(b).astype(BF16)
    d.ki[...] = ki
    d.kd[...] = ki * decay.astype(BF16)
    d.cd[...] = decay
    return jnp.max(-total) <= DECAY_CLAMP


def _gla_pair_rows(d, step):
    pi = (_N_PAIRS - 1 - step) if d.reverse else step
    row0 = pl.multiple_of(pi * 2 * CHUNK, 2 * CHUNK)
    lo, hi = pl.ds(row0, CHUNK), pl.ds(row0 + CHUNK, CHUNK)
    return pl.ds(row0, 2 * CHUNK), (hi, lo) if d.reverse else (lo, hi)


def _rows_scaled(x, scale_row, second_half):
    scale = jnp.broadcast_to(scale_row.astype(BF16), (CHUNK, x.shape[1]))
    ones = jnp.ones((CHUNK, x.shape[1]), BF16)
    return x * jnp.concatenate([ones, scale] if second_half else [scale, ones], axis=0)


def _gla_key_value_product(d, step):
    both, (first, second) = _gla_pair_rows(d, step)
    decay_second = d.cd[pl.ds(second.start, 1), :]
    for h, cols in enumerate(_HG_COLS):
        keys = _rows_scaled(d.kd[both, cols], decay_second[:, cols], second_half=d.reverse)
        d.ut[h] = lax.dot_general(d.v[both, cols], keys, _TN, preferred_element_type=F32)


def _gla_start(d):
    for h in range(HG_HEADS):
        d.sw[0, h] = d.st[0, h].T.astype(BF16)
    _gla_key_value_product(d, 0)


def _gla_pair(d, step, src, dst):
    sr = lax.broadcasted_iota(jnp.int32, (CHUNK, CHUNK), 0)
    sc = lax.broadcasted_iota(jnp.int32, (CHUNK, CHUNK), 1)
    tr = lax.broadcasted_iota(jnp.int32, (CHUNK, 2 * CHUNK), 0)
    tc = lax.broadcasted_iota(jnp.int32, (CHUNK, 2 * CHUNK), 1)
    if d.reverse:
        keep_first = sc >= sr
        keep_second = (tc >= tr) | (tc >= CHUNK)
    else:
        keep_first = sc <= sr
        keep_second = (tc < CHUNK) | (tc - CHUNK <= tr)
    both, (first, second) = _gla_pair_rows(d, step)
    decay_first = d.cd[pl.ds(first.start, 1), :]
    decay_second = d.cd[pl.ds(second.start, 1), :]
    decay_both = decay_first * decay_second
    for h, cols in enumerate(_HG_COLS):
        new = d.st[src, h] * decay_both[:, cols] + d.ut[h]
        d.st[dst, h] = new
        d.sw[dst, h] = new.T.astype(BF16)
    a_first = [lax.dot_general(d.qd[first, cols], d.ki[first, cols], _NT, preferred_element_type=F32)
               for cols in _HG_COLS]
    a_second = []
    for cols in _HG_COLS:
        lo_keys = (d.ki if d.reverse else d.kd)[pl.ds(both.start, CHUNK), cols]
        hi_keys = (d.kd if d.reverse else d.ki)[pl.ds(both.start + CHUNK, CHUNK), cols]
        keys = jnp.concatenate([lo_keys, hi_keys], axis=0)
        a_second.append(lax.dot_general(d.qd[second, cols], keys, _NT, preferred_element_type=F32))
    inter = [jnp.dot(_rows_scaled(d.qd[both, cols], decay_first[:, cols], second_half=not d.reverse),
                     d.sw[src, h], preferred_element_type=F32) for h, cols in enumerate(_HG_COLS)]
    _gla_key_value_product(d, jnp.minimum(step + 1, _N_PAIRS - 1))
    first_half, second_half = (slice(CHUNK, None), slice(None, CHUNK)) if d.reverse else \
                              (slice(None, CHUNK), slice(CHUNK, None))
    for h, cols in enumerate(_HG_COLS):
        am = jnp.where(keep_first, a_first[h], 0.0).astype(BF16)
        d.o[first, cols] = jnp.dot(am, d.v[first, cols], preferred_element_type=F32) + inter[h][first_half]
        am = jnp.where(keep_second, a_second[h], 0.0).astype(BF16)
        d.o[second, cols] = jnp.dot(am, d.v[both, cols], preferred_element_type=F32) + inter[h][second_half]


def _gla_kernel(qf_ref, vf_ref, lff_ref, kf_ref, qb_ref, vb_ref, lfb_ref, kb_ref, s0_ref,
                of_ref, ob_ref, sout_ref, st_scr, sw_scr, ut_scr, qd_scr, kd_scr, ki_scr, cd_scr):
    i = pl.program_id(0)
    is_prompt = i < PROMPT_TILES
    first = jnp.logical_or(is_prompt, (i - PROMPT_TILES) % TILES_PER_DEC_SEQ == 0)
    work = [tuple(scr.at[n] for scr in (st_scr, sw_scr, ut_scr, qd_scr, kd_scr, ki_scr, cd_scr)) for n in range(2)]
    fwd = _GlaDirection(False, qf_ref, vf_ref, lff_ref, kf_ref, of_ref, *work[0])
    bwd = _GlaDirection(True, qb_ref, vb_ref, lfb_ref, kb_ref, ob_ref, *work[1])

    @pl.when(is_prompt)
    def _():
        for d in (fwd, bwd):
            d.st[0] = jnp.zeros((HG_HEADS, HG_DK, HG_DK), F32)

    @pl.when(jnp.logical_and(first, jnp.logical_not(is_prompt)))
    def _():
        for n, d in enumerate((fwd, bwd)):
            for h in range(HG_HEADS):
                d.st[0, h] = s0_ref[0, 0, n, h].T

    exact_scores = jnp.logical_and(_gla_prepare(fwd), _gla_prepare(bwd))

    @pl.when(exact_scores)
    def _():
        _gla_start(fwd)
        _gla_start(bwd)

        def pair_group(it, carry):
            for p in range(GLA_UNROLL):
                for d in (fwd, bwd):
                    _gla_pair(d, GLA_UNROLL * it + p, p % 2, 1 - p % 2)
            return carry

        lax.fori_loop(0, _N_PAIRS // GLA_UNROLL, pair_group, 0)

    @pl.when(jnp.logical_not(exact_scores))
    def _():
        for d in (fwd, bwd):
            _gla_exact(d.reverse, d.q, d.v, d.lf, d.k, d.st, d.o)

    @pl.when(is_prompt)
    def _():
        for n, d in enumerate((fwd, bwd)):
            for h in range(HG_HEADS):
                sout_ref[0, 0, n, h] = d.st[0, h].T


def _gla(q, v, lff, kf, lfb, kb, state_hgrn):
    def fwd_tile(i):
        return (i, 0)

    def bwd_tile(i):
        j = (i - PROMPT_TILES) % TILES_PER_DEC_SEQ
        return (jnp.where(i < PROMPT_TILES, i, i - j + (TILES_PER_DEC_SEQ - 1 - j)), 0)

    def s0_idx(i):
        return (jnp.maximum(i - PROMPT_TILES, 0) // TILES_PER_DEC_SEQ, 0, 0, 0, 0, 0)

    def sout_idx(i):
        return (jnp.minimum(i, PROMPT_TILES - 1), 0, 0, 0, 0, 0)

    f_blk = pl.BlockSpec((TM, D_MODEL), fwd_tile)
    b_blk = pl.BlockSpec((TM, D_MODEL), bwd_tile)
    f_split = pl.BlockSpec((2, TM, D_MODEL), lambda i: (0,) + fwd_tile(i))
    b_split = pl.BlockSpec((2, TM, D_MODEL), lambda i: (0,) + bwd_tile(i))
    st_blk = (1, 1, 2, HG_HEADS, HG_DK, HG_DK)
    return pl.pallas_call(
        _gla_kernel,
        out_shape=(
            jax.ShapeDtypeStruct((N_TOK, D_MODEL), F32),
            jax.ShapeDtypeStruct((N_TOK, D_MODEL), F32),
            jax.ShapeDtypeStruct((BATCH,) + st_blk[1:], F32),
        ),
        grid=(N_TILES,),
        in_specs=[f_blk, f_blk, f_split, f_blk, b_blk, b_blk, b_split, b_blk, pl.BlockSpec(st_blk, s0_idx)],
        out_specs=(f_blk, b_blk, pl.BlockSpec(st_blk, sout_idx)),
        scratch_shapes=[
            pltpu.VMEM((2, 2, HG_HEADS, HG_DK, HG_DK), F32),
            pltpu.VMEM((2, 2, HG_HEADS, HG_DK, HG_DK), BF16),
            pltpu.VMEM((2, HG_HEADS, HG_DK, HG_DK), F32),
            pltpu.VMEM((2, TM, D_MODEL), BF16),
            pltpu.VMEM((2, TM, D_MODEL), BF16),
            pltpu.VMEM((2, TM, D_MODEL), BF16),
            pltpu.VMEM((2, TM, D_MODEL), F32),
        ],
        compiler_params=_params(48, "arbitrary"),
        name="hgrn_recurrence",
    )(q, v, lff, kf, q, v, lfb, kb, state_hgrn)


def _route(logits):
    lane = lax.broadcasted_iota(jnp.int32, logits.shape, 1)
    neg = jnp.float32(-jnp.inf)
    big = jnp.int32(ROUTER_LANES)

    def first_max(x):
        m = jnp.max(x, axis=-1, keepdims=True)
        return m, jnp.min(jnp.where(x == m, lane, big), axis=-1, keepdims=True)

    is_group = (lane >= N_EXPERTS) & (lane < N_EXPERTS + N_GROUPS)
    gl = jnp.where(is_group, logits, neg)
    gmax, g_lane = first_max(gl)
    g_sel = g_lane - N_EXPERTS
    gsum = jnp.sum(jnp.exp(gl - gmax), axis=-1, keepdims=True)
    p_g = 1.0 / gsum
    in_sel = (lane < N_EXPERTS) & ((lane // EXPERTS_PER_GROUP) == g_sel)
    m1, i1 = first_max(jnp.where(in_sel, logits, neg))
    m2, i2 = first_max(jnp.where(in_sel & (lane != i1), logits, neg))
    e2 = jnp.exp(m2 - m1)
    return i1, i2, p_g / (1.0 + e2), p_g * e2 / (1.0 + e2)


def _pack_rows(x):
    q = PACK_WORDS
    bits = pltpu.bitcast(x.astype(BF16).astype(F32), jnp.uint32)
    return [(bits[:, (2 + h) * q:(3 + h) * q] & jnp.uint32(0xFFFF0000)) | (bits[:, h * q:(h + 1) * q] >> 16)
            for h in range(2)]


def _unpack_rows(half0, half1):
    lo = lambda w: pltpu.bitcast(w << 16, F32).astype(BF16)
    hi = lambda w: pltpu.bitcast(w & jnp.uint32(0xFFFF0000), F32).astype(BF16)
    return [lo(half0), lo(half1), hi(half0), hi(half1)]


def _mixer_tail(mix_bf16, x, mod_ref, nw2_ref, wo_ref, wr_ref, br_ref, x1_ref, h2p_ref, lg_ref):
    cond = _tile_cond(pl.program_id(0))
    out = jnp.dot(mix_bf16, wo_ref[...], preferred_element_type=F32)
    x1 = x + _mod_row(mod_ref, cond, 2) * out
    x1_ref[...] = x1
    h2 = _norm_mod(x1, nw2_ref[...], _mod_row(mod_ref, cond, 3), _mod_row(mod_ref, cond, 4))
    for h, words in enumerate(_pack_rows(h2)):
        h2p_ref[h] = words
    h_hi = h2.astype(BF16)
    h_lo = (h2 - h_hi.astype(F32)).astype(BF16)
    lg_ref[...] = (jnp.dot(h_hi, wr_ref[0], preferred_element_type=F32)
                   + (jnp.dot(h_hi, wr_ref[1], preferred_element_type=F32)
                      + jnp.dot(h_lo, wr_ref[0], preferred_element_type=F32))) + br_ref[...]


def _route_kernel(lg_ref, ri_ref, rw_ref, cnt_ref, carry_scr):
    @pl.when(pl.program_id(0) == 0)
    def _():
        carry_scr[...] = jnp.zeros_like(carry_scr)

    logits = lg_ref[...]
    i1, i2, w1, w2 = _route(logits)
    lane = lax.broadcasted_iota(jnp.int32, logits.shape, 1)
    chosen = ((lane == i1) | (lane == i2)).astype(BF16)
    r = lax.broadcasted_iota(jnp.int32, (ROUTE_TM, ROUTE_TM), 0)
    c = lax.broadcasted_iota(jnp.int32, (ROUTE_TM, ROUTE_TM), 1)
    before = jnp.dot((c < r).astype(BF16), chosen, preferred_element_type=F32) + carry_scr[...]
    r1 = jnp.sum(jnp.where(lane == i1, before, 0.0), axis=-1, keepdims=True).astype(jnp.int32)
    r2 = jnp.sum(jnp.where(lane == i2, before, 0.0), axis=-1, keepdims=True).astype(jnp.int32)
    total = carry_scr[...] + jnp.sum(chosen.astype(F32), axis=0, keepdims=True)
    carry_scr[...] = total
    cnt_ref[...] = total
    ri_ref[...] = jnp.where(lane == 0, i1, jnp.where(lane == 1, i2, jnp.where(lane == 2, r1, r2)))
    rw_ref[...] = jnp.where(lane == 0, w1, w2)


def _route_tokens(logits):
    blk = pl.BlockSpec((ROUTE_TM, ROUTER_LANES), lambda i: (i, 0))
    return pl.pallas_call(
        _route_kernel,
        out_shape=(
            jax.ShapeDtypeStruct((N_TOK, ROUTER_LANES), jnp.int32),
            jax.ShapeDtypeStruct((N_TOK, ROUTER_LANES), F32),
            jax.ShapeDtypeStruct((1, ROUTER_LANES), F32),
        ),
        grid=(N_TOK // ROUTE_TM,),
        in_specs=[blk],
        out_specs=(blk, blk, pl.BlockSpec((1, ROUTER_LANES), lambda i: (0, 0))),
        scratch_shapes=[pltpu.VMEM((1, ROUTER_LANES), F32)],
        compiler_params=_params(32, "arbitrary"),
        name="moe_route",
    )(logits)


def _hg_out_kernel(of_ref, ob_ref, g_ref, on_ref, xp_ref, xs_ref, *rest):
    o = of_ref[...] + ob_ref[...]
    parts = []
    for h in range(HG_HEADS):
        oh = o[:, h * HG_DK:(h + 1) * HG_DK]
        parts.append(oh * lax.rsqrt(jnp.mean(oh * oh, axis=-1, keepdims=True) + EPS) * on_ref[...])
    y = jnp.concatenate(parts, axis=1) * _silu(g_ref[...].astype(F32))
    _mixer_tail(y.astype(BF16), _stream_tile(xp_ref, xs_ref), *rest)


def _at_out_kernel(ac_ref, al_ref, x_ref, *rest):
    _mixer_tail(_stream_tile(ac_ref, al_ref), x_ref[...], *rest)


def _mixer_out(kernel_fn, name, mix_inputs, mix_specs, mod, nw2, w_out, w_router, b_router):
    tile = lambda i: (i, 0)
    fixed2 = lambda i: (0, 0)
    blk = pl.BlockSpec((TM, D_MODEL), tile)
    lanes_blk = pl.BlockSpec((TM, ROUTER_LANES), tile)
    return pl.pallas_call(
        kernel_fn,
        out_shape=(
            jax.ShapeDtypeStruct((N_TOK, D_MODEL), F32),
            jax.ShapeDtypeStruct((2, N_TOK, PACK_WORDS), jnp.uint32),
            jax.ShapeDtypeStruct((N_TOK, ROUTER_LANES), F32),
        ),
        grid=(N_TILES,),
        in_specs=list(mix_specs) + [
            pl.BlockSpec(mod.shape, fixed2),
            pl.BlockSpec((1, D_MODEL), fixed2),
            pl.BlockSpec(w_out.shape, fixed2),
            pl.BlockSpec(w_router.shape, lambda i: (0, 0, 0)),
            pl.BlockSpec((1, ROUTER_LANES), fixed2),
        ],
        out_specs=(blk, pl.BlockSpec((2, TM, PACK_WORDS), lambda i: (0, i, 0)), lanes_blk),
        compiler_params=_params(40, "arbitrary"),
        name=name,
    )(*mix_inputs, mod, nw2, w_out, w_router, b_router)


def _moe_plan(route_i, counts):
    cnt = counts[0, :N_EXPERTS].astype(jnp.int32)
    padded = ((cnt + MOE_TILE - 1) // MOE_TILE) * MOE_TILE
    ends = jnp.cumsum(padded)
    offs = ends - padded
    experts = route_i[:, 0:2]
    pos = jnp.sum(jnp.where(experts[:, :, None] == jnp.arange(N_EXPERTS)[None, None, :], offs[None, None, :], 0),
                  axis=-1) + route_i[:, 2:4]
    tile_start = jnp.arange(MOE_ROWS // MOE_TILE, dtype=jnp.int32) * MOE_TILE
    tile_expert = jnp.minimum(jnp.sum(ends[None, :] <= tile_start[:, None], axis=1), N_EXPERTS - 1).astype(jnp.int32)
    of_tile = tile_expert[:, None] == jnp.arange(N_EXPERTS)[None, :]
    tile_offs = jnp.sum(jnp.where(of_tile, offs[None, :], 0), axis=1)
    tile_cnt = jnp.sum(jnp.where(of_tile, cnt[None, :], 0), axis=1)
    tile_rows = jnp.clip(tile_offs + tile_cnt - tile_start, 0, MOE_TILE).astype(jnp.int32)
    tile_first = (tile_start == tile_offs).astype(jnp.int32)
    n_active = (ends[-1] // MOE_TILE).astype(jnp.int32).reshape(1)
    ids = jnp.arange(N_EXPERTS)
    used = cnt > 0
    slot = (jnp.cumsum(used.astype(jnp.int32)) - 1) % 2
    later = jnp.min(jnp.where(used[None, :] & (ids[None, :] > ids[:, None]), ids[None, :], N_EXPERTS), axis=1)
    later = jnp.where(later == N_EXPERTS, -1, later)
    tile_slot = jnp.sum(jnp.where(of_tile, slot[None, :], 0), axis=1).astype(jnp.int32)
    tile_next = jnp.sum(jnp.where(of_tile, later[None, :], 0), axis=1).astype(jnp.int32)
    return pos.astype(jnp.int32), (tile_expert, tile_rows, tile_first, n_active, tile_slot, tile_next)


SC_WINDOW = 128


def _sc_mesh():
    return plsc.VectorSubcoreMesh(core_axis_name="c", subcore_axis_name="s")


def _sc_scatter_rows(x, idx_a, idx_b, n_out_rows):
    n = x.shape[0]

    @functools.partial(pl.kernel, out_type=jax.ShapeDtypeStruct((n_out_rows, PACK_WORDS), x.dtype), mesh=_sc_mesh(),
                       scratch_types=[pltpu.SemaphoreType.DMA, pltpu.SemaphoreType.DMA])
    def scatter(x_hbm, ia_hbm, ib_hbm, o_hbm, sem_a, sem_b):
        def body(x_vmem, ia_vmem, ib_vmem):
            copy_a = pltpu.async_copy(x_vmem, o_hbm.at[ia_vmem.at[0]], sem_a)
            copy_b = pltpu.async_copy(x_vmem, o_hbm.at[ib_vmem.at[0]], sem_b)
            copy_a.wait()
            copy_b.wait()

        idx_spec = pl.BlockSpec((1, SC_WINDOW), index_map=lambda i: (0, i))
        pltpu.emit_pipeline(
            body, grid=(n // SC_WINDOW,),
            in_specs=[pl.BlockSpec((SC_WINDOW, PACK_WORDS), index_map=lambda i: (i, 0)), idx_spec, idx_spec],
            out_specs=[],
            core_axis_name=("c", "s"), dimension_semantics=(pltpu.PARALLEL,),
        )(x_hbm, ia_hbm, ib_hbm)

    return scatter(x, idx_a.reshape(1, n), idx_b.reshape(1, n))


def _sc_gather_rows(table, idx):
    n = idx.shape[0]

    @functools.partial(pl.kernel, out_type=jax.ShapeDtypeStruct((n, PACK_WORDS), table.dtype), mesh=_sc_mesh())
    def gather(t_hbm, i_hbm, o_hbm):
        def body(i_vmem, o_vmem):
            pltpu.sync_copy(t_hbm.at[i_vmem.at[0]], o_vmem)

        pltpu.emit_pipeline(
            body, grid=(n // SC_WINDOW,),
            in_specs=[pl.BlockSpec((1, SC_WINDOW), index_map=lambda i: (0, i))],
            out_specs=[pl.BlockSpec((SC_WINDOW, PACK_WORDS), index_map=lambda i: (i, 0))],
            core_axis_name=("c", "s"), dimension_semantics=(pltpu.PARALLEL,),
        )(i_hbm, o_hbm)

    return gather(table, idx.reshape(1, n))


def _ffn_kernel(te_ref, tr_ref, tf_ref, na_ref, ts_ref, tn_ref, xs_ref, w1_hbm, w3_hbm, w2_hbm, ys_ref,
                w1_scr, w3_scr, w2_scr, w1_buf, w3_buf, w2_buf, sem, *, layer):
    t = pl.program_id(0)

    def weight_copies(expert, slot):
        pairs = ((w1_hbm, w1_buf), (w3_hbm, w3_buf), (w2_hbm, w2_buf))
        return [pltpu.make_async_copy(hbm.at[layer, expert], buf.at[slot], sem.at[slot, j])
                for j, (hbm, buf) in enumerate(pairs)]

    @pl.when(t < na_ref[0])
    def _():
        @pl.when(tf_ref[t] == 1)
        def _():
            slot = ts_ref[t]

            @pl.when(t == 0)
            def _():
                for copy in weight_copies(te_ref[t], slot):
                    copy.start()

            for copy in weight_copies(te_ref[t], slot):
                copy.wait()

            @pl.when(tn_ref[t] >= 0)
            def _():
                for copy in weight_copies(tn_ref[t], 1 - slot):
                    copy.start()

            w1_scr[...] = w1_buf[slot].astype(BF16)
            w3_scr[...] = w3_buf[slot].astype(BF16)
            w2_scr[...] = w2_buf[slot].astype(BF16)

        row = lax.broadcasted_iota(jnp.int32, (MOE_TILE, PACK_WORDS), 0)
        live = row < tr_ref[t]
        halves = [jnp.where(live, xs_ref[h], jnp.zeros((MOE_TILE, PACK_WORDS), jnp.uint32)) for h in range(2)]
        chunks = _unpack_rows(*halves)

        def up(w_scr):
            acc = jnp.dot(chunks[0], w_scr[:PACK_WORDS, :], preferred_element_type=F32)
            for k in range(1, len(chunks)):
                acc = acc + jnp.dot(chunks[k], w_scr[k * PACK_WORDS:(k + 1) * PACK_WORDS, :],
                                    preferred_element_type=F32)
            return acc

        hid = (_silu(up(w1_scr)) * up(w3_scr)).astype(BF16)
        for h, words in enumerate(_pack_rows(jnp.dot(hid, w2_scr[...], preferred_element_type=F32))):
            ys_ref[h] = words


def _ffn(xs, tables, layer, w1, w3, w2):
    row_tile = lambda t, te, tr, tf, na, ts, tn: (0, jnp.minimum(t, na[0] - 1), 0)
    hbm = pl.BlockSpec(memory_space=pl.ANY)
    return pl.pallas_call(
        functools.partial(_ffn_kernel, layer=layer),
        out_shape=jax.ShapeDtypeStruct((2, MOE_ROWS, PACK_WORDS), jnp.uint32),
        grid_spec=pltpu.PrefetchScalarGridSpec(
            num_scalar_prefetch=len(tables),
            grid=(MOE_ROWS // MOE_TILE,),
            in_specs=[pl.BlockSpec((2, MOE_TILE, PACK_WORDS), row_tile), hbm, hbm, hbm],
            out_specs=pl.BlockSpec((2, MOE_TILE, PACK_WORDS), row_tile),
            scratch_shapes=[
                pltpu.VMEM((D_MODEL, D_EXPERT), BF16),
                pltpu.VMEM((D_MODEL, D_EXPERT), BF16),
                pltpu.VMEM((D_EXPERT, D_MODEL), BF16),
                pltpu.VMEM((2, D_MODEL, D_EXPERT), F32),
                pltpu.VMEM((2, D_MODEL, D_EXPERT), F32),
                pltpu.VMEM((2, D_EXPERT, D_MODEL), F32),
                pltpu.SemaphoreType.DMA((2, 3)),
            ],
        ),
        compiler_params=_params(32, "arbitrary"),
        name="moe_experts",
    )(*tables, xs, w1, w3, w2)


def _moe_res(x_ref, y_ref, rw_ref, mod_ref):
    cond = _tile_cond(pl.program_id(0))
    rw = rw_ref[...]
    wa = rw[:, 0:1]
    wb = rw[:, 1:2]
    ya = _unpack_rows(y_ref[0], y_ref[1])
    yb = _unpack_rows(y_ref[2], y_ref[3])
    y = jnp.concatenate([wa * a.astype(F32) + wb * b.astype(F32) for a, b in zip(ya, yb)], axis=1)
    return x_ref[...] + _mod_row(mod_ref, cond, 5) * y


def _moe_res_final_kernel(x_ref, y_ref, rw_ref, mod_ref, fn_ref, op_ref, os_ref):
    x = _moe_res(x_ref, y_ref, rw_ref, mod_ref)
    y = x * lax.rsqrt(jnp.mean(x * x, axis=-1, keepdims=True) + EPS) * fn_ref[...]
    is_prompt = pl.program_id(0) < PROMPT_TILES

    @pl.when(is_prompt)
    def _():
        op_ref[...] = y

    @pl.when(jnp.logical_not(is_prompt))
    def _():
        os_ref[...] = y


def _moe_combine(x1, y_pairs, route_w, mod, final_norm):
    tile = lambda i: (i, 0)
    fixed2 = lambda i: (0, 0)
    return pl.pallas_call(
        _moe_res_final_kernel,
        out_shape=(jax.ShapeDtypeStruct((N_PROMPT, D_MODEL), F32),
                   jax.ShapeDtypeStruct((N_TOK - N_PROMPT, D_MODEL), F32)),
        grid=(N_TILES,),
        in_specs=[pl.BlockSpec((TM, D_MODEL), tile),
                  pl.BlockSpec((4, TM, PACK_WORDS), lambda i: (0, i, 0)),
                  pl.BlockSpec((TM, ROUTER_LANES), tile),
                  pl.BlockSpec(mod.shape, fixed2),
                  pl.BlockSpec((1, D_MODEL), fixed2)],
        out_specs=(pl.BlockSpec((TM, D_MODEL), _prompt_tile), pl.BlockSpec((TM, D_MODEL), _sample_tile)),
        compiler_params=_params(32, "arbitrary"),
        name="moe_combine",
    )(x1, y_pairs, route_w, mod, final_norm)


def _moe_layer(x1, h2p, logits, layer, w1, w3, w2):
    route_i, route_w, counts = _route_tokens(logits)
    pos, tables = _moe_plan(route_i, counts)
    idx_a = jnp.concatenate([pos[:, 0], pos[:, 0] + MOE_ROWS])
    idx_b = jnp.concatenate([pos[:, 1], pos[:, 1] + MOE_ROWS])
    xs = _sc_scatter_rows(h2p.reshape(2 * N_TOK, PACK_WORDS), idx_a, idx_b, 2 * MOE_ROWS)
    ys = _ffn(xs.reshape(2, MOE_ROWS, PACK_WORDS), tables, layer, w1, w3, w2)
    y_pairs = _sc_gather_rows(ys.reshape(2 * MOE_ROWS, PACK_WORDS), jnp.concatenate([idx_a, idx_b]))
    return x1, y_pairs.reshape(4, N_TOK, PACK_WORDS), route_w


def _swap_rotary_halves(x):
    n = x.shape[-1]
    lane = lax.broadcasted_iota(jnp.int32, x.shape, 1)
    quarter = ROPE_HALF // 2
    return jnp.where((lane % ROPE_HALF) < quarter, pltpu.roll(x, n - quarter, 1), pltpu.roll(x, quarter, 1))


def _at_in_kernel(x1_ref, y_ref, rw_ref, mod_prev_ref, mod_ref, nw_ref, w_ref, cos_ref, sin_ref,
                  x_ref, q_ref, kt_ref, v_ref, kc_ref, vc_ref):
    i = pl.program_id(0)
    cond = _tile_cond(i)
    x = _moe_res(x1_ref, y_ref, rw_ref, mod_prev_ref)
    x_ref[...] = x
    h = _norm_mod(x, nw_ref[...], _mod_row(mod_ref, cond, 0), _mod_row(mod_ref, cond, 1)).astype(BF16)
    nq = ATT_Q_HEADS * ATT_HEAD_DIM
    nk = ATT_KV_HEADS * ATT_HEAD_DIM
    cos = cos_ref[...]
    sin = sin_ref[...]

    def rope(x):
        reps = x.shape[-1] // LANES
        return x * jnp.concatenate([cos] * reps, axis=1) + _swap_rotary_halves(x) * jnp.concatenate([sin] * reps, axis=1)

    q_ref[...] = (rope(jnp.dot(h, w_ref[:, :nq], preferred_element_type=F32)) * ATT_HEAD_DIM ** -0.5).astype(BF16)
    k = rope(jnp.dot(h, w_ref[:, nq:nq + nk], preferred_element_type=F32))
    v = jnp.dot(h, w_ref[:, nq + nk:], preferred_element_type=F32)
    kt = k.T
    kt_ref[...] = kt.astype(BF16)
    v_ref[...] = v.astype(BF16)

    @pl.when(i < PROMPT_TILES)
    def _():
        kc_ref[...] = kt
        vc_ref[...] = v.T


def _rope_tables():
    f32 = np.float32
    pos = np.arange(DEC_SEQ)
    t_row = (pos // GRID_W).astype(f32)
    t_col = (pos % GRID_W).astype(f32)
    inv = f32(ROPE_BASE) ** (-np.arange(0, ROPE_HALF, 2, dtype=f32) / f32(ROPE_HALF))
    j = np.arange(LANES) % ATT_HEAD_DIM
    freq = inv[(j % ROPE_HALF) % (ROPE_HALF // 2)]
    ang = (np.where((j < ROPE_HALF)[None, :], t_row[:, None], t_col[:, None]) * freq[None, :]).astype(f32)
    sign = np.where((j % ROPE_HALF) < ROPE_HALF // 2, -1.0, 1.0).astype(f32)
    cos = np.concatenate([np.ones((TM, LANES), f32), np.cos(ang)], axis=0)
    sin = np.concatenate([np.zeros((TM, LANES), f32), np.sin(ang) * sign[None, :]], axis=0)
    return jnp.asarray(cos, F32), jnp.asarray(sin, F32)


def _at_in(x1, y_pairs, route_w, mod_prev, mod, nw, w_in, cos, sin):
    tile = lambda i: (i, 0)
    fixed2 = lambda i: (0, 0)
    rope_tile = lambda i: (jnp.where(i < PROMPT_TILES, 0, 1 + (i - PROMPT_TILES) % TILES_PER_DEC_SEQ), 0)
    nk = ATT_KV_HEADS * ATT_HEAD_DIM
    return pl.pallas_call(
        _at_in_kernel,
        out_shape=(
            jax.ShapeDtypeStruct((N_TOK, D_MODEL), F32),
            jax.ShapeDtypeStruct((N_TOK, D_MODEL), BF16),
            jax.ShapeDtypeStruct((nk, N_TOK), BF16),
            jax.ShapeDtypeStruct((N_TOK, nk), BF16),
            jax.ShapeDtypeStruct((N_PROMPT, nk), F32),
            jax.ShapeDtypeStruct((N_PROMPT, nk), F32),
        ),
        grid=(N_TILES,),
        in_specs=[
            pl.BlockSpec((TM, D_MODEL), tile),
            pl.BlockSpec((4, TM, PACK_WORDS), lambda i: (0, i, 0)),
            pl.BlockSpec((TM, ROUTER_LANES), tile),
            pl.BlockSpec(mod_prev.shape, fixed2),
            pl.BlockSpec(mod.shape, fixed2),
            pl.BlockSpec((1, D_MODEL), fixed2),
            pl.BlockSpec(w_in.shape, fixed2),
            pl.BlockSpec((TM, LANES), rope_tile),
            pl.BlockSpec((TM, LANES), rope_tile),
        ],
        out_specs=(pl.BlockSpec((TM, D_MODEL), tile), pl.BlockSpec((TM, D_MODEL), tile),
                   pl.BlockSpec((nk, TM), lambda i: (0, i)),
                   pl.BlockSpec((TM, nk), tile),
                   pl.BlockSpec((TM, nk), _prompt_tile), pl.BlockSpec((TM, nk), _prompt_tile)),
        compiler_params=_params(40, "arbitrary"),
        name="attn_in_proj",
    )(x1, y_pairs, route_w, mod_prev, mod, nw, w_in, cos, sin)


def _attend(q, kt_all, v_all, mask, sink_ref, o_ref):
    nq = q.shape[0]
    group_lanes = ATT_GROUP * ATT_HEAD_DIM
    lane = lax.broadcasted_iota(jnp.int32, (nq, group_lanes), 1)
    mine = [(lane // ATT_HEAD_DIM) == g for g in range(ATT_GROUP)]
    row_head = lax.broadcasted_iota(jnp.int32, (ATT_GROUP * nq, 1), 0) // nq
    if mask is not None:
        mask = jnp.concatenate([mask] * ATT_GROUP, axis=0)
    for hk in range(ATT_KV_HEADS):
        vh = v_all[:, hk * ATT_HEAD_DIM:(hk + 1) * ATT_HEAD_DIM]
        kt = jnp.concatenate([kt_all[hk * ATT_HEAD_DIM:(hk + 1) * ATT_HEAD_DIM, :]] * ATT_GROUP, axis=0)
        vt = jnp.concatenate([vh] * ATT_GROUP, axis=1)
        qg = q[:, hk * group_lanes:(hk + 1) * group_lanes]
        q_stack = jnp.concatenate([jnp.where(mine[g], qg, jnp.zeros_like(qg)) for g in range(ATT_GROUP)], axis=0)
        s = jnp.dot(q_stack, kt, preferred_element_type=F32)
        if mask is not None:
            s = jnp.where(mask, s, -jnp.inf)
        sink = jnp.zeros((ATT_GROUP * nq, 1), F32)
        for g in range(ATT_GROUP):
            sink = jnp.where(row_head == g, sink_ref[hk * ATT_GROUP + g], sink)
        m = jnp.maximum(jnp.max(s, axis=-1, keepdims=True), sink)
        p = jnp.exp(s - m)
        denom = jnp.sum(p, axis=-1, keepdims=True) + jnp.exp(sink - m)
        o = jnp.dot(p.astype(BF16), vt, preferred_element_type=F32) / denom
        acc = jnp.where(mine[0], o[:nq], 0.0)
        for g in range(1, ATT_GROUP):
            acc = acc + jnp.where(mine[g], o[g * nq:(g + 1) * nq], 0.0)
        o_ref[:, hk * group_lanes:(hk + 1) * group_lanes] = acc.astype(BF16)


def _ctx_attn_kernel(sink_ref, q_ref, k_ref, v_ref, o_ref):
    _attend(q_ref[...], k_ref[...], v_ref[...], None, sink_ref, o_ref)


def _lat_attn_kernel(sink_ref, q_ref, kp_ref, kc_ref, kn_ref, vp_ref, vc_ref, vn_ref, ck_ref, cv_ref, o_ref):
    jb = pl.program_id(1)
    kt_all = jnp.concatenate([kp_ref[...], kc_ref[...], kn_ref[...], ck_ref[0].astype(BF16)], axis=1)
    v_all = jnp.concatenate([vp_ref[...], vc_ref[...], vn_ref[...], cv_ref[0].astype(BF16)], axis=0)
    nkeys = 3 * BLOCK + PAST_LEN
    qi = lax.broadcasted_iota(jnp.int32, (BLOCK, nkeys), 0)
    kj = lax.broadcasted_iota(jnp.int32, (BLOCK, nkeys), 1)
    qpos = jb * BLOCK + qi
    kpos = (jb - 1) * BLOCK + kj
    local_ok = (jnp.abs(qpos - kpos) <= WINDOW) & (kpos >= 0) & (kpos < DEC_SEQ)
    mask = (kj >= 3 * BLOCK) | local_ok
    _attend(q_ref[...], kt_all, v_all, mask, sink_ref, o_ref)


def _attention(q, kt, v, cache_kt, cache_v, sink):
    nk = ATT_KV_HEADS * ATT_HEAD_DIM
    smem = pl.BlockSpec(memory_space=pltpu.SMEM)
    ctx = pl.pallas_call(
        _ctx_attn_kernel,
        out_shape=jax.ShapeDtypeStruct((N_PROMPT, D_MODEL), BF16),
        grid=(BATCH,),
        in_specs=[
            smem,
            pl.BlockSpec((SEQ, D_MODEL), lambda b: (b, 0)),
            pl.BlockSpec((nk, SEQ), lambda b: (0, b)),
            pl.BlockSpec((SEQ, nk), lambda b: (b, 0)),
        ],
        out_specs=pl.BlockSpec((SEQ, D_MODEL), lambda b: (b, 0)),
        compiler_params=_params(40, "arbitrary"),
        name="context_attention",
    )(sink, q, kt, v)

    nb = DEC_SEQ // BLOCK
    base = N_PROMPT // BLOCK
    cur = lambda b, j: (base + b * nb + j, 0)
    prev = lambda b, j: (base + b * nb + jnp.maximum(j - 1, 0), 0)
    nxt = lambda b, j: (base + b * nb + jnp.minimum(j + 1, nb - 1), 0)
    kv_blk = lambda f: pl.BlockSpec((BLOCK, nk), f)
    kt_blk = lambda f: pl.BlockSpec((nk, BLOCK), lambda b, j: f(b, j)[::-1])
    cache_blk = pl.BlockSpec((1, PAST_LEN, nk), lambda b, j: (b, 0, 0))
    cache_kt_blk = pl.BlockSpec((1, nk, PAST_LEN), lambda b, j: (b, 0, 0))
    lat = pl.pallas_call(
        _lat_attn_kernel,
        out_shape=jax.ShapeDtypeStruct((DEC_BATCH * DEC_SEQ, D_MODEL), BF16),
        grid=(DEC_BATCH, nb),
        in_specs=[
            smem,
            pl.BlockSpec((BLOCK, D_MODEL), cur),
            kt_blk(prev), kt_blk(cur), kt_blk(nxt),
            kv_blk(prev), kv_blk(cur), kv_blk(nxt),
            cache_kt_blk, cache_blk,
        ],
        out_specs=pl.BlockSpec((BLOCK, D_MODEL), lambda b, j: (b * nb + j, 0)),
        compiler_params=_params(40, "arbitrary", "arbitrary"),
        name="latent_attention",
    )(sink, q, kt, kt, kt, v, v, v, cache_kt, cache_v)
    return ctx, lat


def kernel(x_prompt, x_sample, state_hgrn, cache_k, cache_v, c, c_ctx, ada_w, ada_b, norm_w, hg_w_in,
           hg_lb_logits, hg_onorm, hg_w_out, at_w_in, at_sink, at_w_out, moe_w_group, moe_b_group,
           moe_w_expert, moe_b_expert, moe_w1, moe_w3, moe_w2, final_norm):
    xp = x_prompt.reshape(N_PROMPT, D_MODEL)
    xs = x_sample.reshape(N_TOK - N_PROMPT, D_MODEL)
    cond = jnp.concatenate([c_ctx[None, :], c], axis=0)
    mod = _ada(cond, ada_w, ada_b)
    nk = ATT_KV_HEADS * ATT_HEAD_DIM

    def router_params(i):
        pad = jnp.zeros((D_MODEL, ROUTER_LANES - N_EXPERTS - N_GROUPS), F32)
        w = jnp.concatenate([moe_w_expert[i], moe_w_group[i], pad], axis=1)
        b = jnp.concatenate([moe_b_expert[i], moe_b_group[i], pad[0]])[None, :]
        hi = w.astype(BF16)
        lo = (w - hi.astype(F32)).astype(BF16)
        return jnp.stack([hi, lo]), b

    tile = lambda i: (i, 0)
    blk = pl.BlockSpec((TM, D_MODEL), tile)

    prompt_blk = pl.BlockSpec((TM, D_MODEL), _prompt_tile)
    sample_blk = pl.BlockSpec((TM, D_MODEL), _sample_tile)
    q, v, g, lff, kf, lfb, kb = _hg_in(xp, xs, mod[0], norm_w[0, 0][None, :], hg_w_in[0].astype(BF16), hg_lb_logits)
    o_f, o_b, state_new = _gla(q, v, lff, kf, lfb, kb, state_hgrn)
    wr, br = router_params(0)
    routed = _mixer_out(
        _hg_out_kernel, "hgrn_out_route", (o_f, o_b, g, hg_onorm[0][None, :], xp, xs),
        (blk, blk, blk, pl.BlockSpec((1, HG_DK), lambda i: (0, 0)), prompt_blk, sample_blk),
        mod[0], norm_w[0, 1][None, :], hg_w_out[0].astype(BF16), wr, br)
    moe_out = _moe_layer(*routed, 0, moe_w1, moe_w3, moe_w2)

    cos, sin = _rope_tables()
    x, qa, ka, va, k_ctx, v_ctx = _at_in(*moe_out, mod[0], mod[1], norm_w[1, 0][None, :],
                                         at_w_in[0].astype(BF16), cos, sin)
    attn_ctx, attn_lat = _attention(qa, ka, va, cache_k[:, 0].reshape(DEC_BATCH, PAST_LEN, nk).transpose(0, 2, 1),
                                    cache_v[:, 0].reshape(DEC_BATCH, PAST_LEN, nk), at_sink[0])
    wr, br = router_params(1)
    routed = _mixer_out(
        _at_out_kernel, "attn_out_route", (attn_ctx, attn_lat, x), (prompt_blk, sample_blk, blk),
        mod[1], norm_w[1, 1][None, :], at_w_out[0].astype(BF16), wr, br)
    y_prompt, y_sample = _moe_combine(*_moe_layer(*routed, 1, moe_w1, moe_w3, moe_w2), mod[1], final_norm[None, :])

    def cache(feature_major):
        return feature_major.reshape(BATCH, 1, ATT_KV_HEADS, ATT_HEAD_DIM, SEQ).transpose(0, 1, 4, 2, 3)

    return (y_prompt.reshape(BATCH, SEQ, D_MODEL), y_sample.reshape(DEC_BATCH, DEC_SEQ, D_MODEL), state_new,
            cache(k_ctx), cache(v_ctx))
```

```python
import functools
from typing import Any, NamedTuple

import jax
import jax.numpy as jnp
import numpy as np
from jax import lax
from jax.experimental import pallas as pl
from jax.experimental.pallas import tpu as pltpu
from jax.experimental.pallas import tpu_sc as plsc

F32 = jnp.float32
BF16 = jnp.bfloat16

D_MODEL = 1024
BATCH = 16
SEQ = 256
DEC_BATCH = 2
DEC_SEQ = 1024
PAST_LEN = 512
GRID_W = 64
HG_HEADS = 8
HG_DK = 128
CHUNK = 16
ATT_HEAD_DIM = 64
ATT_Q_HEADS = 16
ATT_KV_HEADS = 4
ATT_GROUP = 4
WINDOW = 128
BLOCK = 128
ROPE_HALF = 32
ROPE_BASE = 10000.0
N_GROUPS = 4
EXPERTS_PER_GROUP = 8
N_EXPERTS = 32
D_EXPERT = 256
EPS = 1e-6

N_PROMPT = BATCH * SEQ
N_TOK = N_PROMPT + DEC_BATCH * DEC_SEQ
TM = 256
N_TILES = N_TOK // TM
PROMPT_TILES = N_PROMPT // TM
TILES_PER_DEC_SEQ = DEC_SEQ // TM
LANES = 128
N_COND_USED = 1 + DEC_BATCH
N_COND = 8
ROUTER_LANES = 128
ROUTE_TM = 1024
HG_IN_TM = 256
GLA_UNROLL = 2
DECAY_CLAMP = 60.0
PACK_WORDS = D_MODEL // 4
MOE_TILE = 256
MOE_ROWS = 2 * N_TOK + N_EXPERTS * MOE_TILE
MIB = 1024 * 1024


def _params(vmem_mib, *semantics):
    return pltpu.CompilerParams(dimension_semantics=semantics, vmem_limit_bytes=vmem_mib * MIB)


def _tile_cond(i):
    return jnp.where(i < PROMPT_TILES, 0, 1 + (i - PROMPT_TILES) // TILES_PER_DEC_SEQ)


def _prompt_tile(i):
    return (jnp.minimum(i, PROMPT_TILES - 1), 0)


def _sample_tile(i):
    return (jnp.maximum(i - PROMPT_TILES, 0), 0)


def _stream_tile(prompt_ref, sample_ref):
    return jnp.where(pl.program_id(0) < PROMPT_TILES, prompt_ref[...], sample_ref[...])


def _mod_row(mod_ref, cond, which):
    return mod_ref[pl.ds(cond, 1), which * D_MODEL:(which + 1) * D_MODEL]


def _norm_mod(x, nw, shift, scale):
    y = x * lax.rsqrt(jnp.mean(x * x, axis=-1, keepdims=True) + EPS)
    return (y * nw) * (1.0 + scale) + shift


def _silu(x):
    return x * jax.nn.sigmoid(x)


def _ada_kernel(c_ref, w_ref, b_ref, o_ref):
    s = [_silu(c_ref[r]) for r in range(N_COND_USED)]
    tn = w_ref.shape[-1]
    rows = []
    for r in range(N_COND_USED):
        cols = [jnp.sum(w_ref[0, :, j * LANES:(j + 1) * LANES] * s[r], axis=0, keepdims=True)
                for j in range(tn // LANES)]
        rows.append(jnp.concatenate(cols, axis=1) + b_ref[0])
    rows.append(jnp.zeros((N_COND - N_COND_USED, tn), F32))
    o_ref[0] = jnp.concatenate(rows, axis=0)


def _ada(cond, ada_w, ada_b):
    depth, _, n = ada_w.shape
    tn = 1536
    cond_cols = jnp.broadcast_to(cond[:, :, None], (N_COND_USED, D_MODEL, LANES))
    return pl.pallas_call(
        _ada_kernel,
        out_shape=jax.ShapeDtypeStruct((depth, N_COND, n), F32),
        grid=(depth, n // tn),
        in_specs=[
            pl.BlockSpec((N_COND_USED, D_MODEL, LANES), lambda l, j: (0, 0, 0)),
            pl.BlockSpec((1, D_MODEL, tn), lambda l, j: (l, 0, j)),
            pl.BlockSpec((1, 1, tn), lambda l, j: (l, 0, j)),
        ],
        out_specs=pl.BlockSpec((1, N_COND, tn), lambda l, j: (l, 0, j)),
        compiler_params=_params(40, "arbitrary", "arbitrary"),
        name="ada_modulation",
    )(cond_cols, ada_w, ada_b.reshape(depth, 1, n))


def _hg_in_kernel(xp_ref, xs_ref, mod_ref, nw_ref, w_ref, lbl_ref,
                  q_ref, v_ref, g_ref, lff_ref, kf_ref, lfb_ref, kb_ref):
    i = pl.program_id(0) * (HG_IN_TM // TM)
    cond = _tile_cond(i)
    x = jnp.where(i < PROMPT_TILES, xp_ref[...], xs_ref[...])
    h = _norm_mod(x, nw_ref[...], _mod_row(mod_ref, cond, 0), _mod_row(mod_ref, cond, 1)).astype(BF16)

    def proj(c):
        return jnp.dot(h, w_ref[:, c * D_MODEL:(c + 1) * D_MODEL], preferred_element_type=F32)

    l0, l1, l2 = lbl_ref[0], lbl_ref[1], lbl_ref[2]
    m = jnp.maximum(jnp.maximum(l0, l1), l2)
    e0, e1, e2 = jnp.exp(l0 - m), jnp.exp(l1 - m), jnp.exp(l2 - m)
    lb = e0 / (e0 + e1 + e2)

    q_ref[...] = proj(0).astype(BF16)
    v_ref[...] = proj(1).astype(BF16)
    for d, (lf_ref, k_ref) in enumerate(((lff_ref, kf_ref), (lfb_ref, kb_ref))):
        lbd = lb[d:d + 1, :]
        f = lbd + (1.0 - lbd) * jax.nn.sigmoid(proj(2 + d))
        lf = jnp.log(f)
        hi = lf.astype(BF16)
        lf_ref[0] = hi
        lf_ref[1] = (lf - hi.astype(F32)).astype(BF16)
        k_ref[...] = (1.0 - f).astype(BF16)
    g_ref[...] = proj(4).astype(BF16)


def _hg_in(xp, xs, mod, nw, w_in, lb_logits):
    tile = lambda i: (i, 0)
    fixed2 = lambda i: (0, 0)
    bf = jax.ShapeDtypeStruct((N_TOK, D_MODEL), BF16)
    ff = jax.ShapeDtypeStruct((2, N_TOK, D_MODEL), BF16)
    blk = pl.BlockSpec((HG_IN_TM, D_MODEL), tile)
    split_blk = pl.BlockSpec((2, HG_IN_TM, D_MODEL), lambda i: (0, i, 0))
    prompt_steps = N_PROMPT // HG_IN_TM
    return pl.pallas_call(
        _hg_in_kernel,
        out_shape=(bf, bf, bf, ff, bf, ff, bf),
        grid=(N_TOK // HG_IN_TM,),
        in_specs=[
            pl.BlockSpec((HG_IN_TM, D_MODEL), lambda i: (jnp.minimum(i, prompt_steps - 1), 0)),
            pl.BlockSpec((HG_IN_TM, D_MODEL), lambda i: (jnp.maximum(i - prompt_steps, 0), 0)),
            pl.BlockSpec(mod.shape, fixed2),
            pl.BlockSpec((1, D_MODEL), fixed2),
            pl.BlockSpec(w_in.shape, fixed2, pipeline_mode=pl.Buffered(1)),
            pl.BlockSpec(lb_logits.shape, lambda i: (0, 0, 0)),
        ],
        out_specs=(blk, blk, blk, split_blk, blk, split_blk, blk),
        compiler_params=_params(56, "arbitrary"),
        name="hgrn_in_proj",
    )(xp, xs, mod, nw, w_in, lb_logits)


def _gla_exact(reverse, q_ref, v_ref, lf_ref, k_ref, st_ref, o_ref):
    nt = (((1,), (1,)), ((), ()))
    tn = (((0,), (0,)), ((), ()))
    n_chunks = TM // CHUNK
    row_id = lax.broadcasted_iota(jnp.int32, (CHUNK, HG_DK), 0)
    order = range(CHUNK - 1, -1, -1) if reverse else range(CHUNK)

    def chunk_head(it, carry):
        step, h = it // HG_HEADS, it % HG_HEADS
        ci = (n_chunks - 1 - step) if reverse else step
        rows = pl.ds(pl.multiple_of(ci * CHUNK, CHUNK), CHUNK)
        cols = pl.ds(pl.multiple_of(h * HG_DK, HG_DK), HG_DK)
        q, k, v = q_ref[rows, cols], k_ref[rows, cols], v_ref[rows, cols]
        f = jnp.exp(lf_ref[0, rows, cols].astype(F32) + lf_ref[1, rows, cols].astype(F32))
        st = st_ref[0, h]
        o = jnp.zeros((CHUNK, HG_DK), F32)
        for t in order:
            one = row_id == t
            v_t = jnp.where(one, v, jnp.zeros_like(v))
            st = st * f[t:t + 1, :] + lax.dot_general(v_t, k, tn, preferred_element_type=F32)
            o = jnp.where(one, lax.dot_general(q, st.astype(BF16), nt, preferred_element_type=F32), o)
        st_ref[0, h] = st
        o_ref[rows, cols] = o.astype(o_ref.dtype)
        return carry

    lax.fori_loop(0, n_chunks * HG_HEADS, chunk_head, 0)


class _GlaDirection(NamedTuple):
    reverse: bool
    q: Any
    v: Any
    lf: Any
    k: Any
    o: Any
    st: Any
    sw: Any
    ut: Any
    qd: Any
    kd: Any
    ki: Any
    cd: Any


_NT = (((1,), (1,)), ((), ()))
_TN = (((0,), (0,)), ((), ()))
_HG_COLS = [slice(h * HG_DK, (h + 1) * HG_DK) for h in range(HG_HEADS)]
_N_CHUNKS = TM // CHUNK
_N_PAIRS = _N_CHUNKS // 2


def _gla_prepare(d):
    r = lax.broadcasted_iota(jnp.int32, (TM, TM), 0)
    c = lax.broadcasted_iota(jnp.int32, (TM, TM), 1)
    same = (r // CHUNK) == (c // CHUNK)
    tri = (same & ((c >= r) if d.reverse else (c <= r))).astype(BF16)
    b = jnp.dot(tri, d.lf[0], preferred_element_type=F32) + jnp.dot(tri, d.lf[1], preferred_element_type=F32)
    b3 = b.reshape(_N_CHUNKS, CHUNK, D_MODEL)
    edge = 0 if d.reverse else CHUNK - 1
    total = b3[:, edge:edge + 1, :]
    decay = jnp.broadcast_to(jnp.exp(total), b3.shape).reshape(TM, D_MODEL)
    ki = d.k[...] * jnp.exp(jnp.minimum(-b, DECAY_CLAMP)).astype(BF16)
    d.qd[...] = d.q[...] * jnp.exp(b).astype(BF16)
    d.ki[...] = ki
    d.kd[...] = ki * decay.astype(BF16)
    d.cd[...] = decay
    return jnp.max(-total) <= DECAY_CLAMP


def _gla_pair_rows(d, step):
    pi = (_N_PAIRS - 1 - step) if d.reverse else step
    row0 = pl.multiple_of(pi * 2 * CHUNK, 2 * CHUNK)
    lo, hi = pl.ds(row0, CHUNK), pl.ds(row0 + CHUNK, CHUNK)
    return pl.ds(row0, 2 * CHUNK), (hi, lo) if d.reverse else (lo, hi)


def _rows_scaled(x, scale_row, second_half):
    scale = jnp.broadcast_to(scale_row.astype(BF16), (CHUNK, x.shape[1]))
    ones = jnp.ones((CHUNK, x.shape[1]), BF16)
    return x * jnp.concatenate([ones, scale] if second_half else [scale, ones], axis=0)


def _gla_key_value_product(d, step):
    both, (first, second) = _gla_pair_rows(d, step)
    decay_second = d.cd[pl.ds(second.start, 1), :]
    for h, cols in enumerate(_HG_COLS):
        keys = _rows_scaled(d.kd[both, cols], decay_second[:, cols], second_half=d.reverse)
        d.ut[h] = lax.dot_general(d.v[both, cols], keys, _TN, preferred_element_type=F32)


def _gla_start(d):
    for h in range(HG_HEADS):
        d.sw[0, h] = d.st[0, h].T.astype(BF16)
    _gla_key_value_product(d, 0)


def _gla_pair(d, step, src, dst):
    sr = lax.broadcasted_iota(jnp.int32, (CHUNK, CHUNK), 0)
    sc = lax.broadcasted_iota(jnp.int32, (CHUNK, CHUNK), 1)
    tr = lax.broadcasted_iota(jnp.int32, (CHUNK, 2 * CHUNK), 0)
    tc = lax.broadcasted_iota(jnp.int32, (CHUNK, 2 * CHUNK), 1)
    if d.reverse:
        keep_first = sc >= sr
        keep_second = (tc >= tr) | (tc >= CHUNK)
    else:
        keep_first = sc <= sr
        keep_second = (tc < CHUNK) | (tc - CHUNK <= tr)
    both, (first, second) = _gla_pair_rows(d, step)
    decay_first = d.cd[pl.ds(first.start, 1), :]
    decay_second = d.cd[pl.ds(second.start, 1), :]
    decay_both = decay_first * decay_second
    for h, cols in enumerate(_HG_COLS):
        new = d.st[src, h] * decay_both[:, cols] + d.ut[h]
        d.st[dst, h] = new
        d.sw[dst, h] = new.T.astype(BF16)
    a_first = [lax.dot_general(d.qd[first, cols], d.ki[first, cols], _NT, preferred_element_type=F32)
               for cols in _HG_COLS]
    a_second = []
    for cols in _HG_COLS:
        lo_keys = (d.ki if d.reverse else d.kd)[pl.ds(both.start, CHUNK), cols]
        hi_keys = (d.kd if d.reverse else d.ki)[pl.ds(both.start + CHUNK, CHUNK), cols]
        keys = jnp.concatenate([lo_keys, hi_keys], axis=0)
        a_second.append(lax.dot_general(d.qd[second, cols], keys, _NT, preferred_element_type=F32))
    inter = [jnp.dot(_rows_scaled(d.qd[both, cols], decay_first[:, cols], second_half=not d.reverse),
                     d.sw[src, h], preferred_element_type=F32) for h, cols in enumerate(_HG_COLS)]
    _gla_key_value_product(d, jnp.minimum(step + 1, _N_PAIRS - 1))
    first_half, second_half = (slice(CHUNK, None), slice(None, CHUNK)) if d.reverse else \
                              (slice(None, CHUNK), slice(CHUNK, None))
    for h, cols in enumerate(_HG_COLS):
        am = jnp.where(keep_first, a_first[h], 0.0).astype(BF16)
        d.o[first, cols] = (jnp.dot(am, d.v[first, cols], preferred_element_type=F32)
                            + inter[h][first_half]).astype(d.o.dtype)
        am = jnp.where(keep_second, a_second[h], 0.0).astype(BF16)
        d.o[second, cols] = (jnp.dot(am, d.v[both, cols], preferred_element_type=F32)
                             + inter[h][second_half]).astype(d.o.dtype)


def _gla_kernel(qf_ref, vf_ref, lff_ref, kf_ref, qb_ref, vb_ref, lfb_ref, kb_ref, s0_ref,
                of_ref, ob_ref, sout_ref, st_scr, sw_scr, ut_scr, qd_scr, kd_scr, ki_scr, cd_scr):
    i = pl.program_id(0)
    is_prompt = i < PROMPT_TILES
    first = jnp.logical_or(is_prompt, (i - PROMPT_TILES) % TILES_PER_DEC_SEQ == 0)
    work = [tuple(scr.at[n] for scr in (st_scr, sw_scr, ut_scr, qd_scr, kd_scr, ki_scr, cd_scr)) for n in range(2)]
    fwd = _GlaDirection(False, qf_ref, vf_ref, lff_ref, kf_ref, of_ref, *work[0])
    bwd = _GlaDirection(True, qb_ref, vb_ref, lfb_ref, kb_ref, ob_ref, *work[1])

    @pl.when(is_prompt)
    def _():
        for d in (fwd, bwd):
            d.st[0] = jnp.zeros((HG_HEADS, HG_DK, HG_DK), F32)

    @pl.when(jnp.logical_and(first, jnp.logical_not(is_prompt)))
    def _():
        for n, d in enumerate((fwd, bwd)):
            for h in range(HG_HEADS):
                d.st[0, h] = s0_ref[0, 0, n, h].T

    exact_scores = jnp.logical_and(_gla_prepare(fwd), _gla_prepare(bwd))

    @pl.when(exact_scores)
    def _():
        _gla_start(fwd)
        _gla_start(bwd)

        def pair_group(it, carry):
            for p in range(GLA_UNROLL):
                for d in (fwd, bwd):
                    _gla_pair(d, GLA_UNROLL * it + p, p % 2, 1 - p % 2)
            return carry

        lax.fori_loop(0, _N_PAIRS // GLA_UNROLL, pair_group, 0)

    @pl.when(jnp.logical_not(exact_scores))
    def _():
        for d in (fwd, bwd):
            _gla_exact(d.reverse, d.q, d.v, d.lf, d.k, d.st, d.o)

    @pl.when(is_prompt)
    def _():
        for n, d in enumerate((fwd, bwd)):
            for h in range(HG_HEADS):
                sout_ref[0, 0, n, h] = d.st[0, h].T


def _gla(q, v, lff, kf, lfb, kb, state_hgrn):
    def fwd_tile(i):
        return (i, 0)

    def bwd_tile(i):
        j = (i - PROMPT_TILES) % TILES_PER_DEC_SEQ
        return (jnp.where(i < PROMPT_TILES, i, i - j + (TILES_PER_DEC_SEQ - 1 - j)), 0)

    def s0_idx(i):
        return (jnp.maximum(i - PROMPT_TILES, 0) // TILES_PER_DEC_SEQ, 0, 0, 0, 0, 0)

    def sout_idx(i):
        return (jnp.minimum(i, PROMPT_TILES - 1), 0, 0, 0, 0, 0)

    f_blk = pl.BlockSpec((TM, D_MODEL), fwd_tile)
    b_blk = pl.BlockSpec((TM, D_MODEL), bwd_tile)
    f_split = pl.BlockSpec((2, TM, D_MODEL), lambda i: (0,) + fwd_tile(i))
    b_split = pl.BlockSpec((2, TM, D_MODEL), lambda i: (0,) + bwd_tile(i))
    st_blk = (1, 1, 2, HG_HEADS, HG_DK, HG_DK)
    return pl.pallas_call(
        _gla_kernel,
        out_shape=(
            jax.ShapeDtypeStruct((N_TOK, D_MODEL), BF16),
            jax.ShapeDtypeStruct((N_TOK, D_MODEL), BF16),
            jax.ShapeDtypeStruct((BATCH,) + st_blk[1:], F32),
        ),
        grid=(N_TILES,),
        in_specs=[f_blk, f_blk, f_split, f_blk, b_blk, b_blk, b_split, b_blk, pl.BlockSpec(st_blk, s0_idx)],
        out_specs=(f_blk, b_blk, pl.BlockSpec(st_blk, sout_idx)),
        scratch_shapes=[
            pltpu.VMEM((2, 2, HG_HEADS, HG_DK, HG_DK), F32),
            pltpu.VMEM((2, 2, HG_HEADS, HG_DK, HG_DK), BF16),
            pltpu.VMEM((2, HG_HEADS, HG_DK, HG_DK), F32),
            pltpu.VMEM((2, TM, D_MODEL), BF16),
            pltpu.VMEM((2, TM, D_MODEL), BF16),
            pltpu.VMEM((2, TM, D_MODEL), BF16),
            pltpu.VMEM((2, TM, D_MODEL), F32),
        ],
        compiler_params=_params(48, "arbitrary"),
        name="hgrn_recurrence",
    )(q, v, lff, kf, q, v, lfb, kb, state_hgrn)


def _route(logits):
    lane = lax.broadcasted_iota(jnp.int32, logits.shape, 1)
    neg = jnp.float32(-jnp.inf)
    big = jnp.int32(ROUTER_LANES)

    def first_max(x):
        m = jnp.max(x, axis=-1, keepdims=True)
        return m, jnp.min(jnp.where(x == m, lane, big), axis=-1, keepdims=True)

    is_group = (lane >= N_EXPERTS) & (lane < N_EXPERTS + N_GROUPS)
    gl = jnp.where(is_group, logits, neg)
    gmax, g_lane = first_max(gl)
    g_sel = g_lane - N_EXPERTS
    gsum = jnp.sum(jnp.exp(gl - gmax), axis=-1, keepdims=True)
    p_g = 1.0 / gsum
    in_sel = (lane < N_EXPERTS) & ((lane // EXPERTS_PER_GROUP) == g_sel)
    m1, i1 = first_max(jnp.where(in_sel, logits, neg))
    m2, i2 = first_max(jnp.where(in_sel & (lane != i1), logits, neg))
    e2 = jnp.exp(m2 - m1)
    return i1, i2, p_g / (1.0 + e2), p_g * e2 / (1.0 + e2)


def _pack_rows(x):
    q = PACK_WORDS
    bits = pltpu.bitcast(x.astype(BF16).astype(F32), jnp.uint32)
    return [(bits[:, (2 + h) * q:(3 + h) * q] & jnp.uint32(0xFFFF0000)) | (bits[:, h * q:(h + 1) * q] >> 16)
            for h in range(2)]


def _unpack_rows(half0, half1):
    lo = lambda w: pltpu.bitcast(w << 16, F32).astype(BF16)
    hi = lambda w: pltpu.bitcast(w & jnp.uint32(0xFFFF0000), F32).astype(BF16)
    return [lo(half0), lo(half1), hi(half0), hi(half1)]


def _mixer_tail(mix_bf16, x, mod_ref, nw2_ref, wo_ref, wr_ref, br_ref, x1_ref, h2p_ref, lg_ref):
    cond = _tile_cond(pl.program_id(0))
    out = jnp.dot(mix_bf16, wo_ref[...], preferred_element_type=F32)
    x1 = x + _mod_row(mod_ref, cond, 2) * out
    x1_ref[...] = x1
    h2 = _norm_mod(x1, nw2_ref[...], _mod_row(mod_ref, cond, 3), _mod_row(mod_ref, cond, 4))
    for h, words in enumerate(_pack_rows(h2)):
        h2p_ref[h] = words
    h_hi = h2.astype(BF16)
    h_lo = (h2 - h_hi.astype(F32)).astype(BF16)
    lg_ref[...] = (jnp.dot(h_hi, wr_ref[0], preferred_element_type=F32)
                   + (jnp.dot(h_hi, wr_ref[1], preferred_element_type=F32)
                      + jnp.dot(h_lo, wr_ref[0], preferred_element_type=F32))) + br_ref[...]


def _route_kernel(lg_ref, ri_ref, rw_ref, cnt_ref, carry_scr):
    @pl.when(pl.program_id(0) == 0)
    def _():
        carry_scr[...] = jnp.zeros_like(carry_scr)

    logits = lg_ref[...]
    i1, i2, w1, w2 = _route(logits)
    lane = lax.broadcasted_iota(jnp.int32, logits.shape, 1)
    chosen = ((lane == i1) | (lane == i2)).astype(BF16)
    r = lax.broadcasted_iota(jnp.int32, (ROUTE_TM, ROUTE_TM), 0)
    c = lax.broadcasted_iota(jnp.int32, (ROUTE_TM, ROUTE_TM), 1)
    before = jnp.dot((c < r).astype(BF16), chosen, preferred_element_type=F32) + carry_scr[...]
    r1 = jnp.sum(jnp.where(lane == i1, before, 0.0), axis=-1, keepdims=True).astype(jnp.int32)
    r2 = jnp.sum(jnp.where(lane == i2, before, 0.0), axis=-1, keepdims=True).astype(jnp.int32)
    total = carry_scr[...] + jnp.sum(chosen.astype(F32), axis=0, keepdims=True)
    carry_scr[...] = total
    cnt_ref[...] = total
    ri_ref[...] = jnp.where(lane == 0, i1, jnp.where(lane == 1, i2, jnp.where(lane == 2, r1, r2)))
    rw_ref[...] = jnp.where(lane == 0, w1, w2)


def _route_tokens(logits):
    blk = pl.BlockSpec((ROUTE_TM, ROUTER_LANES), lambda i: (i, 0))
    return pl.pallas_call(
        _route_kernel,
        out_shape=(
            jax.ShapeDtypeStruct((N_TOK, ROUTER_LANES), jnp.int32),
            jax.ShapeDtypeStruct((N_TOK, ROUTER_LANES), F32),
            jax.ShapeDtypeStruct((1, ROUTER_LANES), F32),
        ),
        grid=(N_TOK // ROUTE_TM,),
        in_specs=[blk],
        out_specs=(blk, blk, pl.BlockSpec((1, ROUTER_LANES), lambda i: (0, 0))),
        scratch_shapes=[pltpu.VMEM((1, ROUTER_LANES), F32)],
        compiler_params=_params(32, "arbitrary"),
        name="moe_route",
    )(logits)


def _hg_out_kernel(of_ref, ob_ref, g_ref, on_ref, xp_ref, xs_ref, *rest):
    o = of_ref[...].astype(F32) + ob_ref[...].astype(F32)
    parts = []
    for h in range(HG_HEADS):
        oh = o[:, h * HG_DK:(h + 1) * HG_DK]
        parts.append(oh * lax.rsqrt(jnp.mean(oh * oh, axis=-1, keepdims=True) + EPS) * on_ref[...])
    y = jnp.concatenate(parts, axis=1) * _silu(g_ref[...].astype(F32))
    _mixer_tail(y.astype(BF16), _stream_tile(xp_ref, xs_ref), *rest)


def _at_out_kernel(ac_ref, al_ref, x_ref, *rest):
    _mixer_tail(_stream_tile(ac_ref, al_ref), x_ref[...], *rest)


def _mixer_out(kernel_fn, name, mix_inputs, mix_specs, mod, nw2, w_out, w_router, b_router):
    tile = lambda i: (i, 0)
    fixed2 = lambda i: (0, 0)
    blk = pl.BlockSpec((TM, D_MODEL), tile)
    lanes_blk = pl.BlockSpec((TM, ROUTER_LANES), tile)
    return pl.pallas_call(
        kernel_fn,
        out_shape=(
            jax.ShapeDtypeStruct((N_TOK, D_MODEL), F32),
            jax.ShapeDtypeStruct((2, N_TOK, PACK_WORDS), jnp.uint32),
            jax.ShapeDtypeStruct((N_TOK, ROUTER_LANES), F32),
        ),
        grid=(N_TILES,),
        in_specs=list(mix_specs) + [
            pl.BlockSpec(mod.shape, fixed2),
            pl.BlockSpec((1, D_MODEL), fixed2),
            pl.BlockSpec(w_out.shape, fixed2),
            pl.BlockSpec(w_router.shape, lambda i: (0, 0, 0)),
            pl.BlockSpec((1, ROUTER_LANES), fixed2),
        ],
        out_specs=(blk, pl.BlockSpec((2, TM, PACK_WORDS), lambda i: (0, i, 0)), lanes_blk),
        compiler_params=_params(40, "arbitrary"),
        name=name,
    )(*mix_inputs, mod, nw2, w_out, w_router, b_router)


def _moe_plan(route_i, counts):
    cnt = counts[0, :N_EXPERTS].astype(jnp.int32)
    padded = ((cnt + MOE_TILE - 1) // MOE_TILE) * MOE_TILE
    ends = jnp.cumsum(padded)
    offs = ends - padded
    experts = route_i[:, 0:2]
    pos = jnp.sum(jnp.where(experts[:, :, None] == jnp.arange(N_EXPERTS)[None, None, :], offs[None, None, :], 0),
                  axis=-1) + route_i[:, 2:4]
    tile_start = jnp.arange(MOE_ROWS // MOE_TILE, dtype=jnp.int32) * MOE_TILE
    tile_expert = jnp.minimum(jnp.sum(ends[None, :] <= tile_start[:, None], axis=1), N_EXPERTS - 1).astype(jnp.int32)
    of_tile = tile_expert[:, None] == jnp.arange(N_EXPERTS)[None, :]
    tile_offs = jnp.sum(jnp.where(of_tile, offs[None, :], 0), axis=1)
    tile_cnt = jnp.sum(jnp.where(of_tile, cnt[None, :], 0), axis=1)
    tile_rows = jnp.clip(tile_offs + tile_cnt - tile_start, 0, MOE_TILE).astype(jnp.int32)
    tile_first = (tile_start == tile_offs).astype(jnp.int32)
    n_active = (ends[-1] // MOE_TILE).astype(jnp.int32).reshape(1)
    ids = jnp.arange(N_EXPERTS)
    used = cnt > 0
    slot = (jnp.cumsum(used.astype(jnp.int32)) - 1) % 2
    later = jnp.min(jnp.where(used[None, :] & (ids[None, :] > ids[:, None]), ids[None, :], N_EXPERTS), axis=1)
    later = jnp.where(later == N_EXPERTS, -1, later)
    tile_slot = jnp.sum(jnp.where(of_tile, slot[None, :], 0), axis=1).astype(jnp.int32)
    tile_next = jnp.sum(jnp.where(of_tile, later[None, :], 0), axis=1).astype(jnp.int32)
    return pos.astype(jnp.int32), (tile_expert, tile_rows, tile_first, n_active, tile_slot, tile_next)


SC_WINDOW = 128


def _sc_mesh():
    return plsc.VectorSubcoreMesh(core_axis_name="c", subcore_axis_name="s")


def _sc_scatter_rows(x, idx_a, idx_b, n_out_rows):
    n = x.shape[0]

    @functools.partial(pl.kernel, out_type=jax.ShapeDtypeStruct((n_out_rows, PACK_WORDS), x.dtype), mesh=_sc_mesh(),
                       scratch_types=[pltpu.SemaphoreType.DMA, pltpu.SemaphoreType.DMA])
    def scatter(x_hbm, ia_hbm, ib_hbm, o_hbm, sem_a, sem_b):
        def body(x_vmem, ia_vmem, ib_vmem):
            copy_a = pltpu.async_copy(x_vmem, o_hbm.at[ia_vmem.at[0]], sem_a)
            copy_b = pltpu.async_copy(x_vmem, o_hbm.at[ib_vmem.at[0]], sem_b)
            copy_a.wait()
            copy_b.wait()

        idx_spec = pl.BlockSpec((1, SC_WINDOW), index_map=lambda i: (0, i))
        pltpu.emit_pipeline(
            body, grid=(n // SC_WINDOW,),
            in_specs=[pl.BlockSpec((SC_WINDOW, PACK_WORDS), index_map=lambda i: (i, 0)), idx_spec, idx_spec],
            out_specs=[],
            core_axis_name=("c", "s"), dimension_semantics=(pltpu.PARALLEL,),
        )(x_hbm, ia_hbm, ib_hbm)

    return scatter(x, idx_a.reshape(1, n), idx_b.reshape(1, n))


def _sc_gather_rows(table, idx):
    n = idx.shape[0]

    @functools.partial(pl.kernel, out_type=jax.ShapeDtypeStruct((n, PACK_WORDS), table.dtype), mesh=_sc_mesh())
    def gather(t_hbm, i_hbm, o_hbm):
        def body(i_vmem, o_vmem):
            pltpu.sync_copy(t_hbm.at[i_vmem.at[0]], o_vmem)

        pltpu.emit_pipeline(
            body, grid=(n // SC_WINDOW,),
            in_specs=[pl.BlockSpec((1, SC_WINDOW), index_map=lambda i: (0, i))],
            out_specs=[pl.BlockSpec((SC_WINDOW, PACK_WORDS), index_map=lambda i: (i, 0))],
            core_axis_name=("c", "s"), dimension_semantics=(pltpu.PARALLEL,),
        )(i_hbm, o_hbm)

    return gather(table, idx.reshape(1, n))


def _ffn_kernel(te_ref, tr_ref, tf_ref, na_ref, ts_ref, tn_ref, xs_ref, w1_hbm, w3_hbm, w2_hbm, ys_ref,
                w1_scr, w3_scr, w2_scr, w1_buf, w3_buf, w2_buf, sem, *, layer):
    t = pl.program_id(0)

    def weight_copies(expert, slot):
        pairs = ((w1_hbm, w1_buf), (w3_hbm, w3_buf), (w2_hbm, w2_buf))
        return [pltpu.make_async_copy(hbm.at[layer, expert], buf.at[slot], sem.at[slot, j])
                for j, (hbm, buf) in enumerate(pairs)]

    @pl.when(t < na_ref[0])
    def _():
        @pl.when(tf_ref[t] == 1)
        def _():
            slot = ts_ref[t]

            @pl.when(t == 0)
            def _():
                for copy in weight_copies(te_ref[t], slot):
                    copy.start()

            for copy in weight_copies(te_ref[t], slot):
                copy.wait()

            @pl.when(tn_ref[t] >= 0)
            def _():
                for copy in weight_copies(tn_ref[t], 1 - slot):
                    copy.start()

            w1_scr[...] = w1_buf[slot].astype(BF16)
            w3_scr[...] = w3_buf[slot].astype(BF16)
            w2_scr[...] = w2_buf[slot].astype(BF16)

        row = lax.broadcasted_iota(jnp.int32, (MOE_TILE, PACK_WORDS), 0)
        live = row < tr_ref[t]
        halves = [jnp.where(live, xs_ref[h], jnp.zeros((MOE_TILE, PACK_WORDS), jnp.uint32)) for h in range(2)]
        chunks = _unpack_rows(*halves)

        def up(w_scr):
            acc = jnp.dot(chunks[0], w_scr[:PACK_WORDS, :], preferred_element_type=F32)
            for k in range(1, len(chunks)):
                acc = acc + jnp.dot(chunks[k], w_scr[k * PACK_WORDS:(k + 1) * PACK_WORDS, :],
                                    preferred_element_type=F32)
            return acc

        hid = (_silu(up(w1_scr)) * up(w3_scr)).astype(BF16)
        for h, words in enumerate(_pack_rows(jnp.dot(hid, w2_scr[...], preferred_element_type=F32))):
            ys_ref[h] = words


def _ffn(xs, tables, layer, w1, w3, w2):
    row_tile = lambda t, te, tr, tf, na, ts, tn: (0, jnp.minimum(t, na[0] - 1), 0)
    hbm = pl.BlockSpec(memory_space=pl.ANY)
    return pl.pallas_call(
        functools.partial(_ffn_kernel, layer=layer),
        out_shape=jax.ShapeDtypeStruct((2, MOE_ROWS, PACK_WORDS), jnp.uint32),
        grid_spec=pltpu.PrefetchScalarGridSpec(
            num_scalar_prefetch=len(tables),
            grid=(MOE_ROWS // MOE_TILE,),
            in_specs=[pl.BlockSpec((2, MOE_TILE, PACK_WORDS), row_tile), hbm, hbm, hbm],
            out_specs=pl.BlockSpec((2, MOE_TILE, PACK_WORDS), row_tile),
            scratch_shapes=[
                pltpu.VMEM((D_MODEL, D_EXPERT), BF16),
                pltpu.VMEM((D_MODEL, D_EXPERT), BF16),
                pltpu.VMEM((D_EXPERT, D_MODEL), BF16),
                pltpu.VMEM((2, D_MODEL, D_EXPERT), F32),
                pltpu.VMEM((2, D_MODEL, D_EXPERT), F32),
                pltpu.VMEM((2, D_EXPERT, D_MODEL), F32),
                pltpu.SemaphoreType.DMA((2, 3)),
            ],
        ),
        compiler_params=_params(32, "arbitrary"),
        name="moe_experts",
    )(*tables, xs, w1, w3, w2)


def _moe_res(x_ref, y_ref, rw_ref, mod_ref):
    cond = _tile_cond(pl.program_id(0))
    rw = rw_ref[...]
    wa = rw[:, 0:1]
    wb = rw[:, 1:2]
    ya = _unpack_rows(y_ref[0], y_ref[1])
    yb = _unpack_rows(y_ref[2], y_ref[3])
    y = jnp.concatenate([wa * a.astype(F32) + wb * b.astype(F32) for a, b in zip(ya, yb)], axis=1)
    return x_ref[...] + _mod_row(mod_ref, cond, 5) * y


def _moe_res_final_kernel(x_ref, y_ref, rw_ref, mod_ref, fn_ref, op_ref, os_ref):
    x = _moe_res(x_ref, y_ref, rw_ref, mod_ref)
    y = x * lax.rsqrt(jnp.mean(x * x, axis=-1, keepdims=True) + EPS) * fn_ref[...]
    is_prompt = pl.program_id(0) < PROMPT_TILES

    @pl.when(is_prompt)
    def _():
        op_ref[...] = y

    @pl.when(jnp.logical_not(is_prompt))
    def _():
        os_ref[...] = y


def _moe_combine(x1, y_pairs, route_w, mod, final_norm):
    tile = lambda i: (i, 0)
    fixed2 = lambda i: (0, 0)
    return pl.pallas_call(
        _moe_res_final_kernel,
        out_shape=(jax.ShapeDtypeStruct((N_PROMPT, D_MODEL), F32),
                   jax.ShapeDtypeStruct((N_TOK - N_PROMPT, D_MODEL), F32)),
        grid=(N_TILES,),
        in_specs=[pl.BlockSpec((TM, D_MODEL), tile),
                  pl.BlockSpec((4, TM, PACK_WORDS), lambda i: (0, i, 0)),
                  pl.BlockSpec((TM, ROUTER_LANES), tile),
                  pl.BlockSpec(mod.shape, fixed2),
                  pl.BlockSpec((1, D_MODEL), fixed2)],
        out_specs=(pl.BlockSpec((TM, D_MODEL), _prompt_tile), pl.BlockSpec((TM, D_MODEL), _sample_tile)),
        compiler_params=_params(32, "arbitrary"),
        name="moe_combine",
    )(x1, y_pairs, route_w, mod, final_norm)


def _moe_layer(x1, h2p, logits, layer, w1, w3, w2):
    route_i, route_w, counts = _route_tokens(logits)
    pos, tables = _moe_plan(route_i, counts)
    idx_a = jnp.concatenate([pos[:, 0], pos[:, 0] + MOE_ROWS])
    idx_b = jnp.concatenate([pos[:, 1], pos[:, 1] + MOE_ROWS])
    xs = _sc_scatter_rows(h2p.reshape(2 * N_TOK, PACK_WORDS), idx_a, idx_b, 2 * MOE_ROWS)
    ys = _ffn(xs.reshape(2, MOE_ROWS, PACK_WORDS), tables, layer, w1, w3, w2)
    y_pairs = _sc_gather_rows(ys.reshape(2 * MOE_ROWS, PACK_WORDS), jnp.concatenate([idx_a, idx_b]))
    return x1, y_pairs.reshape(4, N_TOK, PACK_WORDS), route_w


def _swap_rotary_halves(x):
    n = x.shape[-1]
    lane = lax.broadcasted_iota(jnp.int32, x.shape, 1)
    quarter = ROPE_HALF // 2
    return jnp.where((lane % ROPE_HALF) < quarter, pltpu.roll(x, n - quarter, 1), pltpu.roll(x, quarter, 1))


def _at_in_kernel(x1_ref, y_ref, rw_ref, mod_prev_ref, mod_ref, nw_ref, w_ref, cos_ref, sin_ref,
                  x_ref, q_ref, kt_ref, v_ref, kc_ref, vc_ref):
    i = pl.program_id(0)
    cond = _tile_cond(i)
    x = _moe_res(x1_ref, y_ref, rw_ref, mod_prev_ref)
    x_ref[...] = x
    h = _norm_mod(x, nw_ref[...], _mod_row(mod_ref, cond, 0), _mod_row(mod_ref, cond, 1)).astype(BF16)
    nq = ATT_Q_HEADS * ATT_HEAD_DIM
    nk = ATT_KV_HEADS * ATT_HEAD_DIM
    cos = cos_ref[...]
    sin = sin_ref[...]

    def rope(x):
        reps = x.shape[-1] // LANES
        return x * jnp.concatenate([cos] * reps, axis=1) + _swap_rotary_halves(x) * jnp.concatenate([sin] * reps, axis=1)

    q_ref[...] = (rope(jnp.dot(h, w_ref[:, :nq], preferred_element_type=F32)) * ATT_HEAD_DIM ** -0.5).astype(BF16)
    k = rope(jnp.dot(h, w_ref[:, nq:nq + nk], preferred_element_type=F32))
    v = jnp.dot(h, w_ref[:, nq + nk:], preferred_element_type=F32)
    kt = k.T
    kt_ref[...] = kt.astype(BF16)
    v_ref[...] = v.astype(BF16)

    @pl.when(i < PROMPT_TILES)
    def _():
        kc_ref[...] = kt
        vc_ref[...] = v.T


def _rope_tables():
    f32 = np.float32
    pos = np.arange(DEC_SEQ)
    t_row = (pos // GRID_W).astype(f32)
    t_col = (pos % GRID_W).astype(f32)
    inv = f32(ROPE_BASE) ** (-np.arange(0, ROPE_HALF, 2, dtype=f32) / f32(ROPE_HALF))
    j = np.arange(LANES) % ATT_HEAD_DIM
    freq = inv[(j % ROPE_HALF) % (ROPE_HALF // 2)]
    ang = (np.where((j < ROPE_HALF)[None, :], t_row[:, None], t_col[:, None]) * freq[None, :]).astype(f32)
    sign = np.where((j % ROPE_HALF) < ROPE_HALF // 2, -1.0, 1.0).astype(f32)
    cos = np.concatenate([np.ones((TM, LANES), f32), np.cos(ang)], axis=0)
    sin = np.concatenate([np.zeros((TM, LANES), f32), np.sin(ang) * sign[None, :]], axis=0)
    return jnp.asarray(cos, F32), jnp.asarray(sin, F32)


def _at_in(x1, y_pairs, route_w, mod_prev, mod, nw, w_in, cos, sin):
    tile = lambda i: (i, 0)
    fixed2 = lambda i: (0, 0)
    rope_tile = lambda i: (jnp.where(i < PROMPT_TILES, 0, 1 + (i - PROMPT_TILES) % TILES_PER_DEC_SEQ), 0)
    nk = ATT_KV_HEADS * ATT_HEAD_DIM
    return pl.pallas_call(
        _at_in_kernel,
        out_shape=(
            jax.ShapeDtypeStruct((N_TOK, D_MODEL), F32),
            jax.ShapeDtypeStruct((N_TOK, D_MODEL), BF16),
            jax.ShapeDtypeStruct((nk, N_TOK), BF16),
            jax.ShapeDtypeStruct((N_TOK, nk), BF16),
            jax.ShapeDtypeStruct((N_PROMPT, nk), F32),
            jax.ShapeDtypeStruct((N_PROMPT, nk), F32),
        ),
        grid=(N_TILES,),
        in_specs=[
            pl.BlockSpec((TM, D_MODEL), tile),
            pl.BlockSpec((4, TM, PACK_WORDS), lambda i: (0, i, 0)),
            pl.BlockSpec((TM, ROUTER_LANES), tile),
            pl.BlockSpec(mod_prev.shape, fixed2),
            pl.BlockSpec(mod.shape, fixed2),
            pl.BlockSpec((1, D_MODEL), fixed2),
            pl.BlockSpec(w_in.shape, fixed2),
            pl.BlockSpec((TM, LANES), rope_tile),
            pl.BlockSpec((TM, LANES), rope_tile),
        ],
        out_specs=(pl.BlockSpec((TM, D_MODEL), tile), pl.BlockSpec((TM, D_MODEL), tile),
                   pl.BlockSpec((nk, TM), lambda i: (0, i)),
                   pl.BlockSpec((TM, nk), tile),
                   pl.BlockSpec((TM, nk), _prompt_tile), pl.BlockSpec((TM, nk), _prompt_tile)),
        compiler_params=_params(40, "arbitrary"),
        name="attn_in_proj",
    )(x1, y_pairs, route_w, mod_prev, mod, nw, w_in, cos, sin)


def _attend(q, kt_all, v_all, mask, sink_ref, o_ref):
    nq = q.shape[0]
    group_lanes = ATT_GROUP * ATT_HEAD_DIM
    lane = lax.broadcasted_iota(jnp.int32, (nq, group_lanes), 1)
    mine = [(lane // ATT_HEAD_DIM) == g for g in range(ATT_GROUP)]
    row_head = lax.broadcasted_iota(jnp.int32, (ATT_GROUP * nq, 1), 0) // nq
    if mask is not None:
        mask = jnp.concatenate([mask] * ATT_GROUP, axis=0)
    for hk in range(ATT_KV_HEADS):
        vh = v_all[:, hk * ATT_HEAD_DIM:(hk + 1) * ATT_HEAD_DIM]
        kt = jnp.concatenate([kt_all[hk * ATT_HEAD_DIM:(hk + 1) * ATT_HEAD_DIM, :]] * ATT_GROUP, axis=0)
        vt = jnp.concatenate([vh] * ATT_GROUP, axis=1)
        qg = q[:, hk * group_lanes:(hk + 1) * group_lanes]
        q_stack = jnp.concatenate([jnp.where(mine[g], qg, jnp.zeros_like(qg)) for g in range(ATT_GROUP)], axis=0)
        s = jnp.dot(q_stack, kt, preferred_element_type=F32)
        if mask is not None:
            s = jnp.where(mask, s, -jnp.inf)
        sink = jnp.zeros((ATT_GROUP * nq, 1), F32)
        for g in range(ATT_GROUP):
            sink = jnp.where(row_head == g, sink_ref[hk * ATT_GROUP + g], sink)
        m = jnp.maximum(jnp.max(s, axis=-1, keepdims=True), sink)
        p = jnp.exp(s - m)
        denom = jnp.sum(p, axis=-1, keepdims=True) + jnp.exp(sink - m)
        o = jnp.dot(p.astype(BF16), vt, preferred_element_type=F32) / denom
        acc = jnp.where(mine[0], o[:nq], 0.0)
        for g in range(1, ATT_GROUP):
            acc = acc + jnp.where(mine[g], o[g * nq:(g + 1) * nq], 0.0)
        o_ref[:, hk * group_lanes:(hk + 1) * group_lanes] = acc.astype(BF16)


def _ctx_attn_kernel(sink_ref, q_ref, k_ref, v_ref, o_ref):
    _attend(q_ref[...], k_ref[...], v_ref[...], None, sink_ref, o_ref)


def _lat_attn_kernel(sink_ref, q_ref, kp_ref, kc_ref, kn_ref, vp_ref, vc_ref, vn_ref, ck_ref, cv_ref, o_ref):
    jb = pl.program_id(1)
    kt_all = jnp.concatenate([kp_ref[...], kc_ref[...], kn_ref[...], ck_ref[0].astype(BF16)], axis=1)
    v_all = jnp.concatenate([vp_ref[...], vc_ref[...], vn_ref[...], cv_ref[0].astype(BF16)], axis=0)
    nkeys = 3 * BLOCK + PAST_LEN
    qi = lax.broadcasted_iota(jnp.int32, (BLOCK, nkeys), 0)
    kj = lax.broadcasted_iota(jnp.int32, (BLOCK, nkeys), 1)
    qpos = jb * BLOCK + qi
    kpos = (jb - 1) * BLOCK + kj
    local_ok = (jnp.abs(qpos - kpos) <= WINDOW) & (kpos >= 0) & (kpos < DEC_SEQ)
    mask = (kj >= 3 * BLOCK) | local_ok
    _attend(q_ref[...], kt_all, v_all, mask, sink_ref, o_ref)


def _attention(q, kt, v, cache_kt, cache_v, sink):
    nk = ATT_KV_HEADS * ATT_HEAD_DIM
    smem = pl.BlockSpec(memory_space=pltpu.SMEM)
    ctx = pl.pallas_call(
        _ctx_attn_kernel,
        out_shape=jax.ShapeDtypeStruct((N_PROMPT, D_MODEL), BF16),
        grid=(BATCH,),
        in_specs=[
            smem,
            pl.BlockSpec((SEQ, D_MODEL), lambda b: (b, 0)),
            pl.BlockSpec((nk, SEQ), lambda b: (0, b)),
            pl.BlockSpec((SEQ, nk), lambda b: (b, 0)),
        ],
        out_specs=pl.BlockSpec((SEQ, D_MODEL), lambda b: (b, 0)),
        compiler_params=_params(40, "arbitrary"),
        name="context_attention",
    )(sink, q, kt, v)

    nb = DEC_SEQ // BLOCK
    base = N_PROMPT // BLOCK
    cur = lambda b, j: (base + b * nb + j, 0)
    prev = lambda b, j: (base + b * nb + jnp.maximum(j - 1, 0), 0)
    nxt = lambda b, j: (base + b * nb + jnp.minimum(j + 1, nb - 1), 0)
    kv_blk = lambda f: pl.BlockSpec((BLOCK, nk), f)
    kt_blk = lambda f: pl.BlockSpec((nk, BLOCK), lambda b, j: f(b, j)[::-1])
    cache_blk = pl.BlockSpec((1, PAST_LEN, nk), lambda b, j: (b, 0, 0))
    cache_kt_blk = pl.BlockSpec((1, nk, PAST_LEN), lambda b, j: (b, 0, 0))
    lat = pl.pallas_call(
        _lat_attn_kernel,
        out_shape=jax.ShapeDtypeStruct((DEC_BATCH * DEC_SEQ, D_MODEL), BF16),
        grid=(DEC_BATCH, nb),
        in_specs=[
            smem,
            pl.BlockSpec((BLOCK, D_MODEL), cur),
            kt_blk(prev), kt_blk(cur), kt_blk(nxt),
            kv_blk(prev), kv_blk(cur), kv_blk(nxt),
            cache_kt_blk, cache_blk,
        ],
        out_specs=pl.BlockSpec((BLOCK, D_MODEL), lambda b, j: (b * nb + j, 0)),
        compiler_params=_params(40, "arbitrary", "arbitrary"),
        name="latent_attention",
    )(sink, q, kt, kt, kt, v, v, v, cache_kt, cache_v)
    return ctx, lat


def kernel(x_prompt, x_sample, state_hgrn, cache_k, cache_v, c, c_ctx, ada_w, ada_b, norm_w, hg_w_in,
           hg_lb_logits, hg_onorm, hg_w_out, at_w_in, at_sink, at_w_out, moe_w_group, moe_b_group,
           moe_w_expert, moe_b_expert, moe_w1, moe_w3, moe_w2, final_norm):
    xp = x_prompt.reshape(N_PROMPT, D_MODEL)
    xs = x_sample.reshape(N_TOK - N_PROMPT, D_MODEL)
    cond = jnp.concatenate([c_ctx[None, :], c], axis=0)
    mod = _ada(cond, ada_w, ada_b)
    nk = ATT_KV_HEADS * ATT_HEAD_DIM

    def router_params(i):
        pad = jnp.zeros((D_MODEL, ROUTER_LANES - N_EXPERTS - N_GROUPS), F32)
        w = jnp.concatenate([moe_w_expert[i], moe_w_group[i], pad], axis=1)
        b = jnp.concatenate([moe_b_expert[i], moe_b_group[i], pad[0]])[None, :]
        hi = w.astype(BF16)
        lo = (w - hi.astype(F32)).astype(BF16)
        return jnp.stack([hi, lo]), b

    tile = lambda i: (i, 0)
    blk = pl.BlockSpec((TM, D_MODEL), tile)

    prompt_blk = pl.BlockSpec((TM, D_MODEL), _prompt_tile)
    sample_blk = pl.BlockSpec((TM, D_MODEL), _sample_tile)
    q, v, g, lff, kf, lfb, kb = _hg_in(xp, xs, mod[0], norm_w[0, 0][None, :], hg_w_in[0].astype(BF16), hg_lb_logits)
    o_f, o_b, state_new = _gla(q, v, lff, kf, lfb, kb, state_hgrn)
    wr, br = router_params(0)
    routed = _mixer_out(
        _hg_out_kernel, "hgrn_out_route", (o_f, o_b, g, hg_onorm[0][None, :], xp, xs),
        (blk, blk, blk, pl.BlockSpec((1, HG_DK), lambda i: (0, 0)), prompt_blk, sample_blk),
        mod[0], norm_w[0, 1][None, :], hg_w_out[0].astype(BF16), wr, br)
    moe_out = _moe_layer(*routed, 0, moe_w1, moe_w3, moe_w2)

    cos, sin = _rope_tables()
    x, qa, ka, va, k_ctx, v_ctx = _at_in(*moe_out, mod[0], mod[1], norm_w[1, 0][None, :],
                                         at_w_in[0].astype(BF16), cos, sin)
    attn_ctx, attn_lat = _attention(qa, ka, va, cache_k[:, 0].reshape(DEC_BATCH, PAST_LEN, nk).transpose(0, 2, 1),
                                    cache_v[:, 0].reshape(DEC_BATCH, PAST_LEN, nk), at_sink[0])
    wr, br = router_params(1)
    routed = _mixer_out(
        _at_out_kernel, "attn_out_route", (attn_ctx, attn_lat, x), (prompt_blk, sample_blk, blk),
        mod[1], norm_w[1, 1][None, :], at_w_out[0].astype(BF16), wr, br)
    y_prompt, y_sample = _moe_combine(*_moe_layer(*routed, 1, moe_w1, moe_w3, moe_w2), mod[1], final_norm[None, :])

    def cache(feature_major):
        return feature_major.reshape(BATCH, 1, ATT_KV_HEADS, ATT_HEAD_DIM, SEQ).transpose(0, 1, 4, 2, 3)

    return (y_prompt.reshape(BATCH, SEQ, D_MODEL), y_sample.reshape(DEC_BATCH, DEC_SEQ, D_MODEL), state_new,
            cache(k_ctx), cache(v_ctx))
```

```python
import functools
from typing import Any, NamedTuple

import jax
import jax.numpy as jnp
import numpy as np
from jax import lax
from jax.experimental import pallas as pl
from jax.experimental.pallas import tpu as pltpu
from jax.experimental.pallas import tpu_sc as plsc

F32 = jnp.float32
BF16 = jnp.bfloat16

D_MODEL = 1024
BATCH = 16
SEQ = 256
DEC_BATCH = 2
DEC_SEQ = 1024
PAST_LEN = 512
GRID_W = 64
HG_HEADS = 8
HG_DK = 128
CHUNK = 16
ATT_HEAD_DIM = 64
ATT_Q_HEADS = 16
ATT_KV_HEADS = 4
ATT_GROUP = 4
WINDOW = 128
BLOCK = 128
ROPE_HALF = 32
ROPE_BASE = 10000.0
N_GROUPS = 4
EXPERTS_PER_GROUP = 8
N_EXPERTS = 32
D_EXPERT = 256
EPS = 1e-6

N_PROMPT = BATCH * SEQ
N_TOK = N_PROMPT + DEC_BATCH * DEC_SEQ
TM = 256
N_TILES = N_TOK // TM
PROMPT_TILES = N_PROMPT // TM
TILES_PER_DEC_SEQ = DEC_SEQ // TM
LANES = 128
N_COND_USED = 1 + DEC_BATCH
N_COND = 8
ROUTER_LANES = 128
ROUTE_TM = 1024
HG_IN_TM = 256
GLA_UNROLL = 2
DECAY_CLAMP = 60.0
PACK_WORDS = D_MODEL // 4
MOE_TILE = 256
MOE_STEP_TILES = 4
MOE_ROWS = 2 * N_TOK + N_EXPERTS * MOE_TILE
MIB = 1024 * 1024


def _params(vmem_mib, *semantics):
    return pltpu.CompilerParams(dimension_semantics=semantics, vmem_limit_bytes=vmem_mib * MIB)


def _tile_cond(i):
    return jnp.where(i < PROMPT_TILES, 0, 1 + (i - PROMPT_TILES) // TILES_PER_DEC_SEQ)


def _prompt_tile(i):
    return (jnp.minimum(i, PROMPT_TILES - 1), 0)


def _sample_tile(i):
    return (jnp.maximum(i - PROMPT_TILES, 0), 0)


def _stream_tile(prompt_ref, sample_ref):
    return jnp.where(pl.program_id(0) < PROMPT_TILES, prompt_ref[...], sample_ref[...])


def _mod_row(mod_ref, cond, which):
    return mod_ref[pl.ds(cond, 1), which * D_MODEL:(which + 1) * D_MODEL]


def _norm_mod(x, nw, shift, scale):
    y = x * lax.rsqrt(jnp.mean(x * x, axis=-1, keepdims=True) + EPS)
    return (y * nw) * (1.0 + scale) + shift


def _silu(x):
    return x * jax.nn.sigmoid(x)


def _ada_kernel(c_ref, w_ref, b_ref, o_ref):
    s = [_silu(c_ref[r]) for r in range(N_COND_USED)]
    tn = w_ref.shape[-1]
    rows = []
    for r in range(N_COND_USED):
        cols = [jnp.sum(w_ref[0, :, j * LANES:(j + 1) * LANES] * s[r], axis=0, keepdims=True)
                for j in range(tn // LANES)]
        rows.append(jnp.concatenate(cols, axis=1) + b_ref[0])
    rows.append(jnp.zeros((N_COND - N_COND_USED, tn), F32))
    o_ref[0] = jnp.concatenate(rows, axis=0)


def _ada(cond, ada_w, ada_b):
    depth, _, n = ada_w.shape
    tn = 1536
    cond_cols = jnp.broadcast_to(cond[:, :, None], (N_COND_USED, D_MODEL, LANES))
    return pl.pallas_call(
        _ada_kernel,
        out_shape=jax.ShapeDtypeStruct((depth, N_COND, n), F32),
        grid=(depth, n // tn),
        in_specs=[
            pl.BlockSpec((N_COND_USED, D_MODEL, LANES), lambda l, j: (0, 0, 0)),
            pl.BlockSpec((1, D_MODEL, tn), lambda l, j: (l, 0, j)),
            pl.BlockSpec((1, 1, tn), lambda l, j: (l, 0, j)),
        ],
        out_specs=pl.BlockSpec((1, N_COND, tn), lambda l, j: (l, 0, j)),
        compiler_params=_params(40, "arbitrary", "arbitrary"),
        name="ada_modulation",
    )(cond_cols, ada_w, ada_b.reshape(depth, 1, n))


def _hg_in_kernel(xp_ref, xs_ref, mod_ref, nw_ref, w_ref, lbl_ref,
                  q_ref, v_ref, g_ref, lff_ref, kf_ref, lfb_ref, kb_ref):
    i = pl.program_id(0) * (HG_IN_TM // TM)
    cond = _tile_cond(i)
    x = jnp.where(i < PROMPT_TILES, xp_ref[...], xs_ref[...])
    h = _norm_mod(x, nw_ref[...], _mod_row(mod_ref, cond, 0), _mod_row(mod_ref, cond, 1)).astype(BF16)

    def proj(c):
        return jnp.dot(h, w_ref[:, c * D_MODEL:(c + 1) * D_MODEL], preferred_element_type=F32)

    l0, l1, l2 = lbl_ref[0], lbl_ref[1], lbl_ref[2]
    m = jnp.maximum(jnp.maximum(l0, l1), l2)
    e0, e1, e2 = jnp.exp(l0 - m), jnp.exp(l1 - m), jnp.exp(l2 - m)
    lb = e0 / (e0 + e1 + e2)

    q_ref[...] = proj(0).astype(BF16)
    v_ref[...] = proj(1).astype(BF16)
    for d, (lf_ref, k_ref) in enumerate(((lff_ref, kf_ref), (lfb_ref, kb_ref))):
        lbd = lb[d:d + 1, :]
        f = lbd + (1.0 - lbd) * jax.nn.sigmoid(proj(2 + d))
        lf = jnp.log(f)
        hi = lf.astype(BF16)
        lf_ref[0] = hi
        lf_ref[1] = (lf - hi.astype(F32)).astype(BF16)
        k_ref[...] = (1.0 - f).astype(BF16)
    g_ref[...] = proj(4).astype(BF16)


def _hg_in(xp, xs, mod, nw, w_in, lb_logits):
    tile = lambda i: (i, 0)
    fixed2 = lambda i: (0, 0)
    bf = jax.ShapeDtypeStruct((N_TOK, D_MODEL), BF16)
    ff = jax.ShapeDtypeStruct((2, N_TOK, D_MODEL), BF16)
    blk = pl.BlockSpec((HG_IN_TM, D_MODEL), tile)
    split_blk = pl.BlockSpec((2, HG_IN_TM, D_MODEL), lambda i: (0, i, 0))
    prompt_steps = N_PROMPT // HG_IN_TM
    return pl.pallas_call(
        _hg_in_kernel,
        out_shape=(bf, bf, bf, ff, bf, ff, bf),
        grid=(N_TOK // HG_IN_TM,),
        in_specs=[
            pl.BlockSpec((HG_IN_TM, D_MODEL), lambda i: (jnp.minimum(i, prompt_steps - 1), 0)),
            pl.BlockSpec((HG_IN_TM, D_MODEL), lambda i: (jnp.maximum(i - prompt_steps, 0), 0)),
            pl.BlockSpec(mod.shape, fixed2),
            pl.BlockSpec((1, D_MODEL), fixed2),
            pl.BlockSpec(w_in.shape, fixed2, pipeline_mode=pl.Buffered(1)),
            pl.BlockSpec(lb_logits.shape, lambda i: (0, 0, 0)),
        ],
        out_specs=(blk, blk, blk, split_blk, blk, split_blk, blk),
        compiler_params=_params(56, "arbitrary"),
        name="hgrn_in_proj",
    )(xp, xs, mod, nw, w_in, lb_logits)


def _gla_exact(reverse, q_ref, v_ref, lf_ref, k_ref, st_ref, o_ref):
    nt = (((1,), (1,)), ((), ()))
    tn = (((0,), (0,)), ((), ()))
    n_chunks = TM // CHUNK
    row_id = lax.broadcasted_iota(jnp.int32, (CHUNK, HG_DK), 0)
    order = range(CHUNK - 1, -1, -1) if reverse else range(CHUNK)

    def chunk_head(it, carry):
        step, h = it // HG_HEADS, it % HG_HEADS
        ci = (n_chunks - 1 - step) if reverse else step
        rows = pl.ds(pl.multiple_of(ci * CHUNK, CHUNK), CHUNK)
        cols = pl.ds(pl.multiple_of(h * HG_DK, HG_DK), HG_DK)
        q, k, v = q_ref[rows, cols], k_ref[rows, cols], v_ref[rows, cols]
        f = jnp.exp(lf_ref[0, rows, cols].astype(F32) + lf_ref[1, rows, cols].astype(F32))
        st = st_ref[0, h]
        o = jnp.zeros((CHUNK, HG_DK), F32)
        for t in order:
            one = row_id == t
            v_t = jnp.where(one, v, jnp.zeros_like(v))
            st = st * f[t:t + 1, :] + lax.dot_general(v_t, k, tn, preferred_element_type=F32)
            o = jnp.where(one, lax.dot_general(q, st.astype(BF16), nt, preferred_element_type=F32), o)
        st_ref[0, h] = st
        o_ref[rows, cols] = o.astype(o_ref.dtype)
        return carry

    lax.fori_loop(0, n_chunks * HG_HEADS, chunk_head, 0)


class _GlaDirection(NamedTuple):
    reverse: bool
    q: Any
    v: Any
    lf: Any
    k: Any
    o: Any
    st: Any
    sw: Any
    ut: Any
    qd: Any
    kd: Any
    ki: Any
    cd: Any


_NT = (((1,), (1,)), ((), ()))
_TN = (((0,), (0,)), ((), ()))
_HG_COLS = [slice(h * HG_DK, (h + 1) * HG_DK) for h in range(HG_HEADS)]
_N_CHUNKS = TM // CHUNK
_N_PAIRS = _N_CHUNKS // 2


def _gla_prepare(d):
    r = lax.broadcasted_iota(jnp.int32, (TM, TM), 0)
    c = lax.broadcasted_iota(jnp.int32, (TM, TM), 1)
    same = (r // CHUNK) == (c // CHUNK)
    tri = (same & ((c >= r) if d.reverse else (c <= r))).astype(BF16)
    b = jnp.dot(tri, d.lf[0], preferred_element_type=F32) + jnp.dot(tri, d.lf[1], preferred_element_type=F32)
    b3 = b.reshape(_N_CHUNKS, CHUNK, D_MODEL)
    edge = 0 if d.reverse else CHUNK - 1
    total = b3[:, edge:edge + 1, :]
    decay = jnp.broadcast_to(jnp.exp(total), b3.shape).reshape(TM, D_MODEL)
    ki = d.k[...] * jnp.exp(jnp.minimum(-b, DECAY_CLAMP)).astype(BF16)
    d.qd[...] = d.q[...] * jnp.exp(b).astype(BF16)
    d.ki[...] = ki
    d.kd[...] = ki * decay.astype(BF16)
    d.cd[...] = decay
    return jnp.max(-total) <= DECAY_CLAMP


def _gla_pair_rows(d, step):
    pi = (_N_PAIRS - 1 - step) if d.reverse else step
    row0 = pl.multiple_of(pi * 2 * CHUNK, 2 * CHUNK)
    lo, hi = pl.ds(row0, CHUNK), pl.ds(row0 + CHUNK, CHUNK)
    return pl.ds(row0, 2 * CHUNK), (hi, lo) if d.reverse else (lo, hi)


def _rows_scaled(x, scale_row, second_half):
    scale = jnp.broadcast_to(scale_row.astype(BF16), (CHUNK, x.shape[1]))
    ones = jnp.ones((CHUNK, x.shape[1]), BF16)
    return x * jnp.concatenate([ones, scale] if second_half else [scale, ones], axis=0)


def _gla_key_value_product(d, step):
    both, (first, second) = _gla_pair_rows(d, step)
    decay_second = d.cd[pl.ds(second.start, 1), :]
    for h, cols in enumerate(_HG_COLS):
        keys = _rows_scaled(d.kd[both, cols], decay_second[:, cols], second_half=d.reverse)
        d.ut[h] = lax.dot_general(d.v[both, cols], keys, _TN, preferred_element_type=F32)


def _gla_start(d):
    for h in range(HG_HEADS):
        d.sw[0, h] = d.st[0, h].T.astype(BF16)
    _gla_key_value_product(d, 0)


def _gla_pair(d, step, src, dst):
    sr = lax.broadcasted_iota(jnp.int32, (CHUNK, CHUNK), 0)
    sc = lax.broadcasted_iota(jnp.int32, (CHUNK, CHUNK), 1)
    tr = lax.broadcasted_iota(jnp.int32, (CHUNK, 2 * CHUNK), 0)
    tc = lax.broadcasted_iota(jnp.int32, (CHUNK, 2 * CHUNK), 1)
    if d.reverse:
        keep_first = sc >= sr
        keep_second = (tc >= tr) | (tc >= CHUNK)
    else:
        keep_first = sc <= sr
        keep_second = (tc < CHUNK) | (tc - CHUNK <= tr)
    both, (first, second) = _gla_pair_rows(d, step)
    decay_first = d.cd[pl.ds(first.start, 1), :]
    decay_second = d.cd[pl.ds(second.start, 1), :]
    decay_both = decay_first * decay_second
    for h, cols in enumerate(_HG_COLS):
        new = d.st[src, h] * decay_both[:, cols] + d.ut[h]
        d.st[dst, h] = new
        d.sw[dst, h] = new.T.astype(BF16)
    a_first = [lax.dot_general(d.qd[first, cols], d.ki[first, cols], _NT, preferred_element_type=F32)
               for cols in _HG_COLS]
    a_second = []
    for cols in _HG_COLS:
        lo_keys = (d.ki if d.reverse else d.kd)[pl.ds(both.start, CHUNK), cols]
        hi_keys = (d.kd if d.reverse else d.ki)[pl.ds(both.start + CHUNK, CHUNK), cols]
        keys = jnp.concatenate([lo_keys, hi_keys], axis=0)
        a_second.append(lax.dot_general(d.qd[second, cols], keys, _NT, preferred_element_type=F32))
    inter = [jnp.dot(_rows_scaled(d.qd[both, cols], decay_first[:, cols], second_half=not d.reverse),
                     d.sw[src, h], preferred_element_type=F32) for h, cols in enumerate(_HG_COLS)]
    _gla_key_value_product(d, jnp.minimum(step + 1, _N_PAIRS - 1))
    first_half, second_half = (slice(CHUNK, None), slice(None, CHUNK)) if d.reverse else \
                              (slice(None, CHUNK), slice(CHUNK, None))
    for h, cols in enumerate(_HG_COLS):
        am = jnp.where(keep_first, a_first[h], 0.0).astype(BF16)
        d.o[first, cols] = (jnp.dot(am, d.v[first, cols], preferred_element_type=F32)
                            + inter[h][first_half]).astype(d.o.dtype)
        am = jnp.where(keep_second, a_second[h], 0.0).astype(BF16)
        d.o[second, cols] = (jnp.dot(am, d.v[both, cols], preferred_element_type=F32)
                             + inter[h][second_half]).astype(d.o.dtype)


def _gla_kernel(qf_ref, vf_ref, lff_ref, kf_ref, qb_ref, vb_ref, lfb_ref, kb_ref, s0_ref,
                of_ref, ob_ref, sout_ref, st_scr, sw_scr, ut_scr, qd_scr, kd_scr, ki_scr, cd_scr):
    i = pl.program_id(0)
    is_prompt = i < PROMPT_TILES
    first = jnp.logical_or(is_prompt, (i - PROMPT_TILES) % TILES_PER_DEC_SEQ == 0)
    work = [tuple(scr.at[n] for scr in (st_scr, sw_scr, ut_scr, qd_scr, kd_scr, ki_scr, cd_scr)) for n in range(2)]
    fwd = _GlaDirection(False, qf_ref, vf_ref, lff_ref, kf_ref, of_ref, *work[0])
    bwd = _GlaDirection(True, qb_ref, vb_ref, lfb_ref, kb_ref, ob_ref, *work[1])

    @pl.when(is_prompt)
    def _():
        for d in (fwd, bwd):
            d.st[0] = jnp.zeros((HG_HEADS, HG_DK, HG_DK), F32)

    @pl.when(jnp.logical_and(first, jnp.logical_not(is_prompt)))
    def _():
        for n, d in enumerate((fwd, bwd)):
            for h in range(HG_HEADS):
                d.st[0, h] = s0_ref[0, 0, n, h].T

    exact_scores = jnp.logical_and(_gla_prepare(fwd), _gla_prepare(bwd))

    @pl.when(exact_scores)
    def _():
        _gla_start(fwd)
        _gla_start(bwd)

        def pair_group(it, carry):
            for p in range(GLA_UNROLL):
                for d in (fwd, bwd):
                    _gla_pair(d, GLA_UNROLL * it + p, p % 2, 1 - p % 2)
            return carry

        lax.fori_loop(0, _N_PAIRS // GLA_UNROLL, pair_group, 0)

    @pl.when(jnp.logical_not(exact_scores))
    def _():
        for d in (fwd, bwd):
            _gla_exact(d.reverse, d.q, d.v, d.lf, d.k, d.st, d.o)

    @pl.when(is_prompt)
    def _():
        for n, d in enumerate((fwd, bwd)):
            for h in range(HG_HEADS):
                sout_ref[0, 0, n, h] = d.st[0, h].T


def _gla(q, v, lff, kf, lfb, kb, state_hgrn):
    def fwd_tile(i):
        return (i, 0)

    def bwd_tile(i):
        j = (i - PROMPT_TILES) % TILES_PER_DEC_SEQ
        return (jnp.where(i < PROMPT_TILES, i, i - j + (TILES_PER_DEC_SEQ - 1 - j)), 0)

    def s0_idx(i):
        return (jnp.maximum(i - PROMPT_TILES, 0) // TILES_PER_DEC_SEQ, 0, 0, 0, 0, 0)

    def sout_idx(i):
        return (jnp.minimum(i, PROMPT_TILES - 1), 0, 0, 0, 0, 0)

    f_blk = pl.BlockSpec((TM, D_MODEL), fwd_tile)
    b_blk = pl.BlockSpec((TM, D_MODEL), bwd_tile)
    f_split = pl.BlockSpec((2, TM, D_MODEL), lambda i: (0,) + fwd_tile(i))
    b_split = pl.BlockSpec((2, TM, D_MODEL), lambda i: (0,) + bwd_tile(i))
    st_blk = (1, 1, 2, HG_HEADS, HG_DK, HG_DK)
    return pl.pallas_call(
        _gla_kernel,
        out_shape=(
            jax.ShapeDtypeStruct((N_TOK, D_MODEL), BF16),
            jax.ShapeDtypeStruct((N_TOK, D_MODEL), BF16),
            jax.ShapeDtypeStruct((BATCH,) + st_blk[1:], F32),
        ),
        grid=(N_TILES,),
        in_specs=[f_blk, f_blk, f_split, f_blk, b_blk, b_blk, b_split, b_blk, pl.BlockSpec(st_blk, s0_idx)],
        out_specs=(f_blk, b_blk, pl.BlockSpec(st_blk, sout_idx)),
        scratch_shapes=[
            pltpu.VMEM((2, 2, HG_HEADS, HG_DK, HG_DK), F32),
            pltpu.VMEM((2, 2, HG_HEADS, HG_DK, HG_DK), BF16),
            pltpu.VMEM((2, HG_HEADS, HG_DK, HG_DK), F32),
            pltpu.VMEM((2, TM, D_MODEL), BF16),
            pltpu.VMEM((2, TM, D_MODEL), BF16),
            pltpu.VMEM((2, TM, D_MODEL), BF16),
            pltpu.VMEM((2, TM, D_MODEL), F32),
        ],
        compiler_params=_params(48, "arbitrary"),
        name="hgrn_recurrence",
    )(q, v, lff, kf, q, v, lfb, kb, state_hgrn)


def _route(logits):
    lane = lax.broadcasted_iota(jnp.int32, logits.shape, 1)
    neg = jnp.float32(-jnp.inf)
    big = jnp.int32(ROUTER_LANES)

    def first_max(x):
        m = jnp.max(x, axis=-1, keepdims=True)
        return m, jnp.min(jnp.where(x == m, lane, big), axis=-1, keepdims=True)

    is_group = (lane >= N_EXPERTS) & (lane < N_EXPERTS + N_GROUPS)
    gl = jnp.where(is_group, logits, neg)
    gmax, g_lane = first_max(gl)
    g_sel = g_lane - N_EXPERTS
    gsum = jnp.sum(jnp.exp(gl - gmax), axis=-1, keepdims=True)
    p_g = 1.0 / gsum
    in_sel = (lane < N_EXPERTS) & ((lane // EXPERTS_PER_GROUP) == g_sel)
    m1, i1 = first_max(jnp.where(in_sel, logits, neg))
    m2, i2 = first_max(jnp.where(in_sel & (lane != i1), logits, neg))
    e2 = jnp.exp(m2 - m1)
    return i1, i2, p_g / (1.0 + e2), p_g * e2 / (1.0 + e2)


def _pack_rows(x):
    q = PACK_WORDS
    bits = pltpu.bitcast(x.astype(BF16).astype(F32), jnp.uint32)
    return [(bits[:, (2 + h) * q:(3 + h) * q] & jnp.uint32(0xFFFF0000)) | (bits[:, h * q:(h + 1) * q] >> 16)
            for h in range(2)]


def _unpack_rows(half0, half1):
    lo = lambda w: pltpu.bitcast(w << 16, F32).astype(BF16)
    hi = lambda w: pltpu.bitcast(w & jnp.uint32(0xFFFF0000), F32).astype(BF16)
    return [lo(half0), lo(half1), hi(half0), hi(half1)]


def _mixer_tail(mix_bf16, x, mod_ref, nw2_ref, wo_ref, wr_ref, br_ref, x1_ref, h2p_ref, lg_ref):
    cond = _tile_cond(pl.program_id(0))
    out = jnp.dot(mix_bf16, wo_ref[...], preferred_element_type=F32)
    x1 = x + _mod_row(mod_ref, cond, 2) * out
    x1_ref[...] = x1
    h2 = _norm_mod(x1, nw2_ref[...], _mod_row(mod_ref, cond, 3), _mod_row(mod_ref, cond, 4))
    for h, words in enumerate(_pack_rows(h2)):
        h2p_ref[h] = words
    h_hi = h2.astype(BF16)
    h_lo = (h2 - h_hi.astype(F32)).astype(BF16)
    lg_ref[...] = (jnp.dot(h_hi, wr_ref[0], preferred_element_type=F32)
                   + (jnp.dot(h_hi, wr_ref[1], preferred_element_type=F32)
                      + jnp.dot(h_lo, wr_ref[0], preferred_element_type=F32))) + br_ref[...]


def _route_kernel(lg_ref, ri_ref, rw_ref, cnt_ref, carry_scr):
    @pl.when(pl.program_id(0) == 0)
    def _():
        carry_scr[...] = jnp.zeros_like(carry_scr)

    logits = lg_ref[...]
    i1, i2, w1, w2 = _route(logits)
    lane = lax.broadcasted_iota(jnp.int32, logits.shape, 1)
    chosen = ((lane == i1) | (lane == i2)).astype(BF16)
    r = lax.broadcasted_iota(jnp.int32, (ROUTE_TM, ROUTE_TM), 0)
    c = lax.broadcasted_iota(jnp.int32, (ROUTE_TM, ROUTE_TM), 1)
    before = jnp.dot((c < r).astype(BF16), chosen, preferred_element_type=F32) + carry_scr[...]
    r1 = jnp.sum(jnp.where(lane == i1, before, 0.0), axis=-1, keepdims=True).astype(jnp.int32)
    r2 = jnp.sum(jnp.where(lane == i2, before, 0.0), axis=-1, keepdims=True).astype(jnp.int32)
    total = carry_scr[...] + jnp.sum(chosen.astype(F32), axis=0, keepdims=True)
    carry_scr[...] = total
    cnt_ref[...] = total
    ri_ref[...] = jnp.where(lane == 0, i1, jnp.where(lane == 1, i2, jnp.where(lane == 2, r1, r2)))
    rw_ref[...] = jnp.where(lane == 0, w1, w2)


def _route_tokens(logits):
    blk = pl.BlockSpec((ROUTE_TM, ROUTER_LANES), lambda i: (i, 0))
    return pl.pallas_call(
        _route_kernel,
        out_shape=(
            jax.ShapeDtypeStruct((N_TOK, ROUTER_LANES), jnp.int32),
            jax.ShapeDtypeStruct((N_TOK, ROUTER_LANES), F32),
            jax.ShapeDtypeStruct((1, ROUTER_LANES), F32),
        ),
        grid=(N_TOK // ROUTE_TM,),
        in_specs=[blk],
        out_specs=(blk, blk, pl.BlockSpec((1, ROUTER_LANES), lambda i: (0, 0))),
        scratch_shapes=[pltpu.VMEM((1, ROUTER_LANES), F32)],
        compiler_params=_params(32, "arbitrary"),
        name="moe_route",
    )(logits)


def _hg_out_kernel(of_ref, ob_ref, g_ref, on_ref, xp_ref, xs_ref, *rest):
    o = of_ref[...].astype(F32) + ob_ref[...].astype(F32)
    parts = []
    for h in range(HG_HEADS):
        oh = o[:, h * HG_DK:(h + 1) * HG_DK]
        parts.append(oh * lax.rsqrt(jnp.mean(oh * oh, axis=-1, keepdims=True) + EPS) * on_ref[...])
    y = jnp.concatenate(parts, axis=1) * _silu(g_ref[...].astype(F32))
    _mixer_tail(y.astype(BF16), _stream_tile(xp_ref, xs_ref), *rest)


def _at_out_kernel(ac_ref, al_ref, x_ref, *rest):
    _mixer_tail(_stream_tile(ac_ref, al_ref), x_ref[...], *rest)


def _mixer_out(kernel_fn, name, mix_inputs, mix_specs, mod, nw2, w_out, w_router, b_router):
    tile = lambda i: (i, 0)
    fixed2 = lambda i: (0, 0)
    blk = pl.BlockSpec((TM, D_MODEL), tile)
    lanes_blk = pl.BlockSpec((TM, ROUTER_LANES), tile)
    return pl.pallas_call(
        kernel_fn,
        out_shape=(
            jax.ShapeDtypeStruct((N_TOK, D_MODEL), F32),
            jax.ShapeDtypeStruct((2, N_TOK, PACK_WORDS), jnp.uint32),
            jax.ShapeDtypeStruct((N_TOK, ROUTER_LANES), F32),
        ),
        grid=(N_TILES,),
        in_specs=list(mix_specs) + [
            pl.BlockSpec(mod.shape, fixed2),
            pl.BlockSpec((1, D_MODEL), fixed2),
            pl.BlockSpec(w_out.shape, fixed2),
            pl.BlockSpec(w_router.shape, lambda i: (0, 0, 0)),
            pl.BlockSpec((1, ROUTER_LANES), fixed2),
        ],
        out_specs=(blk, pl.BlockSpec((2, TM, PACK_WORDS), lambda i: (0, i, 0)), lanes_blk),
        compiler_params=_params(40, "arbitrary"),
        name=name,
    )(*mix_inputs, mod, nw2, w_out, w_router, b_router)


def _moe_plan(route_i, counts):
    cnt = counts[0, :N_EXPERTS].astype(jnp.int32)
    padded = ((cnt + MOE_TILE - 1) // MOE_TILE) * MOE_TILE
    ends = jnp.cumsum(padded)
    offs = ends - padded
    experts = route_i[:, 0:2]
    pos = jnp.sum(jnp.where(experts[:, :, None] == jnp.arange(N_EXPERTS)[None, None, :], offs[None, None, :], 0),
                  axis=-1) + route_i[:, 2:4]
    tile_start = jnp.arange(MOE_ROWS // MOE_TILE, dtype=jnp.int32) * MOE_TILE
    tile_expert = jnp.minimum(jnp.sum(ends[None, :] <= tile_start[:, None], axis=1), N_EXPERTS - 1).astype(jnp.int32)
    of_tile = tile_expert[:, None] == jnp.arange(N_EXPERTS)[None, :]
    tile_offs = jnp.sum(jnp.where(of_tile, offs[None, :], 0), axis=1)
    tile_cnt = jnp.sum(jnp.where(of_tile, cnt[None, :], 0), axis=1)
    tile_rows = jnp.clip(tile_offs + tile_cnt - tile_start, 0, MOE_TILE).astype(jnp.int32)
    tile_first = (tile_start == tile_offs).astype(jnp.int32)
    n_active = (ends[-1] // MOE_TILE).astype(jnp.int32).reshape(1)
    ids = jnp.arange(N_EXPERTS)
    used = cnt > 0
    slot = (jnp.cumsum(used.astype(jnp.int32)) - 1) % 2
    later = jnp.min(jnp.where(used[None, :] & (ids[None, :] > ids[:, None]), ids[None, :], N_EXPERTS), axis=1)
    later = jnp.where(later == N_EXPERTS, -1, later)
    tile_slot = jnp.sum(jnp.where(of_tile, slot[None, :], 0), axis=1).astype(jnp.int32)
    tile_next = jnp.sum(jnp.where(of_tile, later[None, :], 0), axis=1).astype(jnp.int32)
    return pos.astype(jnp.int32), (tile_expert, tile_rows, tile_first, n_active, tile_slot, tile_next)


SC_WINDOW = 128


def _sc_mesh():
    return plsc.VectorSubcoreMesh(core_axis_name="c", subcore_axis_name="s")


def _sc_scatter_rows(x, idx_a, idx_b, n_out_rows):
    n = x.shape[0]

    @functools.partial(pl.kernel, out_type=jax.ShapeDtypeStruct((n_out_rows, PACK_WORDS), x.dtype), mesh=_sc_mesh(),
                       scratch_types=[pltpu.SemaphoreType.DMA, pltpu.SemaphoreType.DMA])
    def scatter(x_hbm, ia_hbm, ib_hbm, o_hbm, sem_a, sem_b):
        def body(x_vmem, ia_vmem, ib_vmem):
            copy_a = pltpu.async_copy(x_vmem, o_hbm.at[ia_vmem.at[0]], sem_a)
            copy_b = pltpu.async_copy(x_vmem, o_hbm.at[ib_vmem.at[0]], sem_b)
            copy_a.wait()
            copy_b.wait()

        idx_spec = pl.BlockSpec((1, SC_WINDOW), index_map=lambda i: (0, i))
        pltpu.emit_pipeline(
            body, grid=(n // SC_WINDOW,),
            in_specs=[pl.BlockSpec((SC_WINDOW, PACK_WORDS), index_map=lambda i: (i, 0)), idx_spec, idx_spec],
            out_specs=[],
            core_axis_name=("c", "s"), dimension_semantics=(pltpu.PARALLEL,),
        )(x_hbm, ia_hbm, ib_hbm)

    return scatter(x, idx_a.reshape(1, n), idx_b.reshape(1, n))


def _sc_gather_rows(table, idx):
    n = idx.shape[0]

    @functools.partial(pl.kernel, out_type=jax.ShapeDtypeStruct((n, PACK_WORDS), table.dtype), mesh=_sc_mesh())
    def gather(t_hbm, i_hbm, o_hbm):
        def body(i_vmem, o_vmem):
            pltpu.sync_copy(t_hbm.at[i_vmem.at[0]], o_vmem)

        pltpu.emit_pipeline(
            body, grid=(n // SC_WINDOW,),
            in_specs=[pl.BlockSpec((1, SC_WINDOW), index_map=lambda i: (0, i))],
            out_specs=[pl.BlockSpec((SC_WINDOW, PACK_WORDS), index_map=lambda i: (i, 0))],
            core_axis_name=("c", "s"), dimension_semantics=(pltpu.PARALLEL,),
        )(i_hbm, o_hbm)

    return gather(table, idx.reshape(1, n))


def _ffn_kernel(te_ref, tr_ref, tf_ref, na_ref, ts_ref, tn_ref, xs_ref, w1_hbm, w3_hbm, w2_hbm, ys_ref,
                w1_scr, w3_scr, w2_scr, w1_buf, w3_buf, w2_buf, sem, *, layer):
    def weight_copies(expert, slot):
        pairs = ((w1_hbm, w1_buf), (w3_hbm, w3_buf), (w2_hbm, w2_buf))
        return [pltpu.make_async_copy(hbm.at[layer, expert], buf.at[slot], sem.at[slot, j])
                for j, (hbm, buf) in enumerate(pairs)]

    def tile(t, rows):
        @pl.when(tf_ref[t] == 1)
        def _():
            slot = ts_ref[t]

            @pl.when(t == 0)
            def _():
                for copy in weight_copies(te_ref[t], slot):
                    copy.start()

            for copy in weight_copies(te_ref[t], slot):
                copy.wait()

            @pl.when(tn_ref[t] >= 0)
            def _():
                for copy in weight_copies(tn_ref[t], 1 - slot):
                    copy.start()

            w1_scr[...] = w1_buf[slot].astype(BF16)
            w3_scr[...] = w3_buf[slot].astype(BF16)
            w2_scr[...] = w2_buf[slot].astype(BF16)

        row = lax.broadcasted_iota(jnp.int32, (MOE_TILE, PACK_WORDS), 0)
        live = row < tr_ref[t]
        halves = [jnp.where(live, xs_ref[h, rows, :], jnp.zeros((MOE_TILE, PACK_WORDS), jnp.uint32))
                  for h in range(2)]
        chunks = _unpack_rows(*halves)

        def up(w_scr):
            acc = jnp.dot(chunks[0], w_scr[:PACK_WORDS, :], preferred_element_type=F32)
            for k in range(1, len(chunks)):
                acc = acc + jnp.dot(chunks[k], w_scr[k * PACK_WORDS:(k + 1) * PACK_WORDS, :],
                                    preferred_element_type=F32)
            return acc

        hid = (_silu(up(w1_scr)) * up(w3_scr)).astype(BF16)
        for h, words in enumerate(_pack_rows(jnp.dot(hid, w2_scr[...], preferred_element_type=F32))):
            ys_ref[h, rows, :] = words

    for j in range(MOE_STEP_TILES):
        t = pl.program_id(0) * MOE_STEP_TILES + j
        pl.when(t < na_ref[0])(functools.partial(tile, t, slice(j * MOE_TILE, (j + 1) * MOE_TILE)))


def _ffn(xs, tables, layer, w1, w3, w2):
    step_rows = MOE_STEP_TILES * MOE_TILE
    row_tile = lambda s, te, tr, tf, na, ts, tn: (0, jnp.minimum(s, (na[0] - 1) // MOE_STEP_TILES), 0)
    hbm = pl.BlockSpec(memory_space=pl.ANY)
    return pl.pallas_call(
        functools.partial(_ffn_kernel, layer=layer),
        out_shape=jax.ShapeDtypeStruct((2, MOE_ROWS, PACK_WORDS), jnp.uint32),
        grid_spec=pltpu.PrefetchScalarGridSpec(
            num_scalar_prefetch=len(tables),
            grid=(MOE_ROWS // step_rows,),
            in_specs=[pl.BlockSpec((2, step_rows, PACK_WORDS), row_tile), hbm, hbm, hbm],
            out_specs=pl.BlockSpec((2, step_rows, PACK_WORDS), row_tile),
            scratch_shapes=[
                pltpu.VMEM((D_MODEL, D_EXPERT), BF16),
                pltpu.VMEM((D_MODEL, D_EXPERT), BF16),
                pltpu.VMEM((D_EXPERT, D_MODEL), BF16),
                pltpu.VMEM((2, D_MODEL, D_EXPERT), F32),
                pltpu.VMEM((2, D_MODEL, D_EXPERT), F32),
                pltpu.VMEM((2, D_EXPERT, D_MODEL), F32),
                pltpu.SemaphoreType.DMA((2, 3)),
            ],
        ),
        compiler_params=_params(32, "arbitrary"),
        name="moe_experts",
    )(*tables, xs, w1, w3, w2)


def _moe_res(x_ref, y_ref, rw_ref, mod_ref):
    cond = _tile_cond(pl.program_id(0))
    rw = rw_ref[...]
    wa = rw[:, 0:1]
    wb = rw[:, 1:2]
    ya = _unpack_rows(y_ref[0], y_ref[1])
    yb = _unpack_rows(y_ref[2], y_ref[3])
    y = jnp.concatenate([wa * a.astype(F32) + wb * b.astype(F32) for a, b in zip(ya, yb)], axis=1)
    return x_ref[...] + _mod_row(mod_ref, cond, 5) * y


def _moe_res_final_kernel(x_ref, y_ref, rw_ref, mod_ref, fn_ref, op_ref, os_ref):
    x = _moe_res(x_ref, y_ref, rw_ref, mod_ref)
    y = x * lax.rsqrt(jnp.mean(x * x, axis=-1, keepdims=True) + EPS) * fn_ref[...]
    is_prompt = pl.program_id(0) < PROMPT_TILES

    @pl.when(is_prompt)
    def _():
        op_ref[...] = y

    @pl.when(jnp.logical_not(is_prompt))
    def _():
        os_ref[...] = y


def _moe_combine(x1, y_pairs, route_w, mod, final_norm):
    tile = lambda i: (i, 0)
    fixed2 = lambda i: (0, 0)
    return pl.pallas_call(
        _moe_res_final_kernel,
        out_shape=(jax.ShapeDtypeStruct((N_PROMPT, D_MODEL), F32),
                   jax.ShapeDtypeStruct((N_TOK - N_PROMPT, D_MODEL), F32)),
        grid=(N_TILES,),
        in_specs=[pl.BlockSpec((TM, D_MODEL), tile),
                  pl.BlockSpec((4, TM, PACK_WORDS), lambda i: (0, i, 0)),
                  pl.BlockSpec((TM, ROUTER_LANES), tile),
                  pl.BlockSpec(mod.shape, fixed2),
                  pl.BlockSpec((1, D_MODEL), fixed2)],
        out_specs=(pl.BlockSpec((TM, D_MODEL), _prompt_tile), pl.BlockSpec((TM, D_MODEL), _sample_tile)),
        compiler_params=_params(32, "arbitrary"),
        name="moe_combine",
    )(x1, y_pairs, route_w, mod, final_norm)


def _moe_layer(x1, h2p, logits, layer, w1, w3, w2):
    route_i, route_w, counts = _route_tokens(logits)
    pos, tables = _moe_plan(route_i, counts)
    idx_a = jnp.concatenate([pos[:, 0], pos[:, 0] + MOE_ROWS])
    idx_b = jnp.concatenate([pos[:, 1], pos[:, 1] + MOE_ROWS])
    xs = _sc_scatter_rows(h2p.reshape(2 * N_TOK, PACK_WORDS), idx_a, idx_b, 2 * MOE_ROWS)
    ys = _ffn(xs.reshape(2, MOE_ROWS, PACK_WORDS), tables, layer, w1, w3, w2)
    y_pairs = _sc_gather_rows(ys.reshape(2 * MOE_ROWS, PACK_WORDS), jnp.concatenate([idx_a, idx_b]))
    return x1, y_pairs.reshape(4, N_TOK, PACK_WORDS), route_w


def _swap_rotary_halves(x):
    n = x.shape[-1]
    lane = lax.broadcasted_iota(jnp.int32, x.shape, 1)
    quarter = ROPE_HALF // 2
    return jnp.where((lane % ROPE_HALF) < quarter, pltpu.roll(x, n - quarter, 1), pltpu.roll(x, quarter, 1))


def _at_in_kernel(x1_ref, y_ref, rw_ref, mod_prev_ref, mod_ref, nw_ref, w_ref, cos_ref, sin_ref,
                  x_ref, q_ref, kt_ref, v_ref, kc_ref, vc_ref):
    i = pl.program_id(0)
    cond = _tile_cond(i)
    x = _moe_res(x1_ref, y_ref, rw_ref, mod_prev_ref)
    x_ref[...] = x
    h = _norm_mod(x, nw_ref[...], _mod_row(mod_ref, cond, 0), _mod_row(mod_ref, cond, 1)).astype(BF16)
    nq = ATT_Q_HEADS * ATT_HEAD_DIM
    nk = ATT_KV_HEADS * ATT_HEAD_DIM
    cos = cos_ref[...]
    sin = sin_ref[...]

    def rope(x):
        reps = x.shape[-1] // LANES
        return x * jnp.concatenate([cos] * reps, axis=1) + _swap_rotary_halves(x) * jnp.concatenate([sin] * reps, axis=1)

    q_ref[...] = (rope(jnp.dot(h, w_ref[:, :nq], preferred_element_type=F32)) * ATT_HEAD_DIM ** -0.5).astype(BF16)
    k = rope(jnp.dot(h, w_ref[:, nq:nq + nk], preferred_element_type=F32))
    v = jnp.dot(h, w_ref[:, nq + nk:], preferred_element_type=F32)
    kt = k.T
    kt_ref[...] = kt.astype(BF16)
    v_ref[...] = v.astype(BF16)

    @pl.when(i < PROMPT_TILES)
    def _():
        kc_ref[...] = kt
        vc_ref[...] = v.T


def _rope_tables():
    f32 = np.float32
    pos = np.arange(DEC_SEQ)
    t_row = (pos // GRID_W).astype(f32)
    t_col = (pos % GRID_W).astype(f32)
    inv = f32(ROPE_BASE) ** (-np.arange(0, ROPE_HALF, 2, dtype=f32) / f32(ROPE_HALF))
    j = np.arange(LANES) % ATT_HEAD_DIM
    freq = inv[(j % ROPE_HALF) % (ROPE_HALF // 2)]
    ang = (np.where((j < ROPE_HALF)[None, :], t_row[:, None], t_col[:, None]) * freq[None, :]).astype(f32)
    sign = np.where((j % ROPE_HALF) < ROPE_HALF // 2, -1.0, 1.0).astype(f32)
    cos = np.concatenate([np.ones((TM, LANES), f32), np.cos(ang)], axis=0)
    sin = np.concatenate([np.zeros((TM, LANES), f32), np.sin(ang) * sign[None, :]], axis=0)
    return jnp.asarray(cos, F32), jnp.asarray(sin, F32)


def _at_in(x1, y_pairs, route_w, mod_prev, mod, nw, w_in, cos, sin):
    tile = lambda i: (i, 0)
    fixed2 = lambda i: (0, 0)
    rope_tile = lambda i: (jnp.where(i < PROMPT_TILES, 0, 1 + (i - PROMPT_TILES) % TILES_PER_DEC_SEQ), 0)
    nk = ATT_KV_HEADS * ATT_HEAD_DIM
    return pl.pallas_call(
        _at_in_kernel,
        out_shape=(
            jax.ShapeDtypeStruct((N_TOK, D_MODEL), F32),
            jax.ShapeDtypeStruct((N_TOK, D_MODEL), BF16),
            jax.ShapeDtypeStruct((nk, N_TOK), BF16),
            jax.ShapeDtypeStruct((N_TOK, nk), BF16),
            jax.ShapeDtypeStruct((N_PROMPT, nk), F32),
            jax.ShapeDtypeStruct((N_PROMPT, nk), F32),
        ),
        grid=(N_TILES,),
        in_specs=[
            pl.BlockSpec((TM, D_MODEL), tile),
            pl.BlockSpec((4, TM, PACK_WORDS), lambda i: (0, i, 0)),
            pl.BlockSpec((TM, ROUTER_LANES), tile),
            pl.BlockSpec(mod_prev.shape, fixed2),
            pl.BlockSpec(mod.shape, fixed2),
            pl.BlockSpec((1, D_MODEL), fixed2),
            pl.BlockSpec(w_in.shape, fixed2),
            pl.BlockSpec((TM, LANES), rope_tile),
            pl.BlockSpec((TM, LANES), rope_tile),
        ],
        out_specs=(pl.BlockSpec((TM, D_MODEL), tile), pl.BlockSpec((TM, D_MODEL), tile),
                   pl.BlockSpec((nk, TM), lambda i: (0, i)),
                   pl.BlockSpec((TM, nk), tile),
                   pl.BlockSpec((TM, nk), _prompt_tile), pl.BlockSpec((TM, nk), _prompt_tile)),
        compiler_params=_params(40, "arbitrary"),
        name="attn_in_proj",
    )(x1, y_pairs, route_w, mod_prev, mod, nw, w_in, cos, sin)


def _attend(q, kt_all, v_all, mask, sink_ref, o_ref):
    nq = q.shape[0]
    group_lanes = ATT_GROUP * ATT_HEAD_DIM
    lane = lax.broadcasted_iota(jnp.int32, (nq, group_lanes), 1)
    mine = [(lane // ATT_HEAD_DIM) == g for g in range(ATT_GROUP)]
    row_head = lax.broadcasted_iota(jnp.int32, (ATT_GROUP * nq, 1), 0) // nq
    if mask is not None:
        mask = jnp.concatenate([mask] * ATT_GROUP, axis=0)
    for hk in range(ATT_KV_HEADS):
        vh = v_all[:, hk * ATT_HEAD_DIM:(hk + 1) * ATT_HEAD_DIM]
        kt = jnp.concatenate([kt_all[hk * ATT_HEAD_DIM:(hk + 1) * ATT_HEAD_DIM, :]] * ATT_GROUP, axis=0)
        vt = jnp.concatenate([vh] * ATT_GROUP, axis=1)
        qg = q[:, hk * group_lanes:(hk + 1) * group_lanes]
        q_stack = jnp.concatenate([jnp.where(mine[g], qg, jnp.zeros_like(qg)) for g in range(ATT_GROUP)], axis=0)
        s = jnp.dot(q_stack, kt, preferred_element_type=F32)
        if mask is not None:
            s = jnp.where(mask, s, -jnp.inf)
        sink = jnp.zeros((ATT_GROUP * nq, 1), F32)
        for g in range(ATT_GROUP):
            sink = jnp.where(row_head == g, sink_ref[hk * ATT_GROUP + g], sink)
        m = jnp.maximum(jnp.max(s, axis=-1, keepdims=True), sink)
        p = jnp.exp(s - m)
        denom = jnp.sum(p, axis=-1, keepdims=True) + jnp.exp(sink - m)
        o = jnp.dot(p.astype(BF16), vt, preferred_element_type=F32) / denom
        acc = jnp.where(mine[0], o[:nq], 0.0)
        for g in range(1, ATT_GROUP):
            acc = acc + jnp.where(mine[g], o[g * nq:(g + 1) * nq], 0.0)
        o_ref[:, hk * group_lanes:(hk + 1) * group_lanes] = acc.astype(BF16)


def _ctx_attn_kernel(sink_ref, q_ref, k_ref, v_ref, o_ref):
    _attend(q_ref[...], k_ref[...], v_ref[...], None, sink_ref, o_ref)


def _lat_attn_kernel(sink_ref, q_ref, kp_ref, kc_ref, kn_ref, vp_ref, vc_ref, vn_ref, ck_ref, cv_ref, o_ref):
    jb = pl.program_id(1)
    kt_all = jnp.concatenate([kp_ref[...], kc_ref[...], kn_ref[...], ck_ref[0].astype(BF16)], axis=1)
    v_all = jnp.concatenate([vp_ref[...], vc_ref[...], vn_ref[...], cv_ref[0].astype(BF16)], axis=0)
    nkeys = 3 * BLOCK + PAST_LEN
    qi = lax.broadcasted_iota(jnp.int32, (BLOCK, nkeys), 0)
    kj = lax.broadcasted_iota(jnp.int32, (BLOCK, nkeys), 1)
    qpos = jb * BLOCK + qi
    kpos = (jb - 1) * BLOCK + kj
    local_ok = (jnp.abs(qpos - kpos) <= WINDOW) & (kpos >= 0) & (kpos < DEC_SEQ)
    mask = (kj >= 3 * BLOCK) | local_ok
    _attend(q_ref[...], kt_all, v_all, mask, sink_ref, o_ref)


def _attention(q, kt, v, cache_kt, cache_v, sink):
    nk = ATT_KV_HEADS * ATT_HEAD_DIM
    smem = pl.BlockSpec(memory_space=pltpu.SMEM)
    ctx = pl.pallas_call(
        _ctx_attn_kernel,
        out_shape=jax.ShapeDtypeStruct((N_PROMPT, D_MODEL), BF16),
        grid=(BATCH,),
        in_specs=[
            smem,
            pl.BlockSpec((SEQ, D_MODEL), lambda b: (b, 0)),
            pl.BlockSpec((nk, SEQ), lambda b: (0, b)),
            pl.BlockSpec((SEQ, nk), lambda b: (b, 0)),
        ],
        out_specs=pl.BlockSpec((SEQ, D_MODEL), lambda b: (b, 0)),
        compiler_params=_params(40, "arbitrary"),
        name="context_attention",
    )(sink, q, kt, v)

    nb = DEC_SEQ // BLOCK
    base = N_PROMPT // BLOCK
    cur = lambda b, j: (base + b * nb + j, 0)
    prev = lambda b, j: (base + b * nb + jnp.maximum(j - 1, 0), 0)
    nxt = lambda b, j: (base + b * nb + jnp.minimum(j + 1, nb - 1), 0)
    kv_blk = lambda f: pl.BlockSpec((BLOCK, nk), f)
    kt_blk = lambda f: pl.BlockSpec((nk, BLOCK), lambda b, j: f(b, j)[::-1])
    cache_blk = pl.BlockSpec((1, PAST_LEN, nk), lambda b, j: (b, 0, 0))
    cache_kt_blk = pl.BlockSpec((1, nk, PAST_LEN), lambda b, j: (b, 0, 0))
    lat = pl.pallas_call(
        _lat_attn_kernel,
        out_shape=jax.ShapeDtypeStruct((DEC_BATCH * DEC_SEQ, D_MODEL), BF16),
        grid=(DEC_BATCH, nb),
        in_specs=[
            smem,
            pl.BlockSpec((BLOCK, D_MODEL), cur),
            kt_blk(prev), kt_blk(cur), kt_blk(nxt),
            kv_blk(prev), kv_blk(cur), kv_blk(nxt),
            cache_kt_blk, cache_blk,
        ],
        out_specs=pl.BlockSpec((BLOCK, D_MODEL), lambda b, j: (b * nb + j, 0)),
        compiler_params=_params(40, "arbitrary", "arbitrary"),
        name="latent_attention",
    )(sink, q, kt, kt, kt, v, v, v, cache_kt, cache_v)
    return ctx, lat


def kernel(x_prompt, x_sample, state_hgrn, cache_k, cache_v, c, c_ctx, ada_w, ada_b, norm_w, hg_w_in,
           hg_lb_logits, hg_onorm, hg_w_out, at_w_in, at_sink, at_w_out, moe_w_group, moe_b_group,
           moe_w_expert, moe_b_expert, moe_w1, moe_w3, moe_w2, final_norm):
    xp = x_prompt.reshape(N_PROMPT, D_MODEL)
    xs = x_sample.reshape(N_TOK - N_PROMPT, D_MODEL)
    cond = jnp.concatenate([c_ctx[None, :], c], axis=0)
    mod = _ada(cond, ada_w, ada_b)
    nk = ATT_KV_HEADS * ATT_HEAD_DIM

    def router_params(i):
        pad = jnp.zeros((D_MODEL, ROUTER_LANES - N_EXPERTS - N_GROUPS), F32)
        w = jnp.concatenate([moe_w_expert[i], moe_w_group[i], pad], axis=1)
        b = jnp.concatenate([moe_b_expert[i], moe_b_group[i], pad[0]])[None, :]
        hi = w.astype(BF16)
        lo = (w - hi.astype(F32)).astype(BF16)
        return jnp.stack([hi, lo]), b

    tile = lambda i: (i, 0)
    blk = pl.BlockSpec((TM, D_MODEL), tile)

    prompt_blk = pl.BlockSpec((TM, D_MODEL), _prompt_tile)
    sample_blk = pl.BlockSpec((TM, D_MODEL), _sample_tile)
    q, v, g, lff, kf, lfb, kb = _hg_in(xp, xs, mod[0], norm_w[0, 0][None, :], hg_w_in[0].astype(BF16), hg_lb_logits)
    o_f, o_b, state_new = _gla(q, v, lff, kf, lfb, kb, state_hgrn)
    wr, br = router_params(0)
    routed = _mixer_out(
        _hg_out_kernel, "hgrn_out_route", (o_f, o_b, g, hg_onorm[0][None, :], xp, xs),
        (blk, blk, blk, pl.BlockSpec((1, HG_DK), lambda i: (0, 0)), prompt_blk, sample_blk),
        mod[0], norm_w[0, 1][None, :], hg_w_out[0].astype(BF16), wr, br)
    moe_out = _moe_layer(*routed, 0, moe_w1, moe_w3, moe_w2)

    cos, sin = _rope_tables()
    x, qa, ka, va, k_ctx, v_ctx = _at_in(*moe_out, mod[0], mod[1], norm_w[1, 0][None, :],
                                         at_w_in[0].astype(BF16), cos, sin)
    attn_ctx, attn_lat = _attention(qa, ka, va, cache_k[:, 0].reshape(DEC_BATCH, PAST_LEN, nk).transpose(0, 2, 1),
                                    cache_v[:, 0].reshape(DEC_BATCH, PAST_LEN, nk), at_sink[0])
    wr, br = router_params(1)
    routed = _mixer_out(
        _at_out_kernel, "attn_out_route", (attn_ctx, attn_lat, x), (prompt_blk, sample_blk, blk),
        mod[1], norm_w[1, 1][None, :], at_w_out[0].astype(BF16), wr, br)
    y_prompt, y_sample = _moe_combine(*_moe_layer(*routed, 1, moe_w1, moe_w3, moe_w2), mod[1], final_norm[None, :])

    def cache(feature_major):
        return feature_major.reshape(BATCH, 1, ATT_KV_HEADS, ATT_HEAD_DIM, SEQ).transpose(0, 1, 4, 2, 3)

    return (y_prompt.reshape(BATCH, SEQ, D_MODEL), y_sample.reshape(DEC_BATCH, DEC_SEQ, D_MODEL), state_new,
            cache(k_ctx), cache(v_ctx))
```

```python
import functools
from typing import Any, NamedTuple

import jax
import jax.numpy as jnp
import numpy as np
from jax import lax
from jax.experimental import pallas as pl
from jax.experimental.pallas import tpu as pltpu
from jax.experimental.pallas import tpu_sc as plsc

F32 = jnp.float32
BF16 = jnp.bfloat16

D_MODEL = 1024
BATCH = 16
SEQ = 256
DEC_BATCH = 2
DEC_SEQ = 1024
PAST_LEN = 512
GRID_W = 64
HG_HEADS = 8
HG_DK = 128
CHUNK = 16
ATT_HEAD_DIM = 64
ATT_Q_HEADS = 16
ATT_KV_HEADS = 4
ATT_GROUP = 4
WINDOW = 128
BLOCK = 128
ROPE_HALF = 32
ROPE_BASE = 10000.0
N_GROUPS = 4
EXPERTS_PER_GROUP = 8
N_EXPERTS = 32
D_EXPERT = 256
EPS = 1e-6

N_PROMPT = BATCH * SEQ
N_TOK = N_PROMPT + DEC_BATCH * DEC_SEQ
TM = 256
N_TILES = N_TOK // TM
PROMPT_TILES = N_PROMPT // TM
TILES_PER_DEC_SEQ = DEC_SEQ // TM
LANES = 128
N_COND_USED = 1 + DEC_BATCH
N_COND = 8
ROUTER_LANES = 128
ROUTE_TM = 1024
HG_IN_TM = 256
DECAY_CLAMP = 60.0
PACK_WORDS = D_MODEL // 4
MOE_TILE = 256
MOE_STEP_TILES = 4
MOE_ROWS = 2 * N_TOK + N_EXPERTS * MOE_TILE
MIB = 1024 * 1024


def _params(vmem_mib, *semantics):
    return pltpu.CompilerParams(dimension_semantics=semantics, vmem_limit_bytes=vmem_mib * MIB)


def _tile_cond(i):
    return jnp.where(i < PROMPT_TILES, 0, 1 + (i - PROMPT_TILES) // TILES_PER_DEC_SEQ)


def _prompt_tile(i):
    return (jnp.minimum(i, PROMPT_TILES - 1), 0)


def _sample_tile(i):
    return (jnp.maximum(i - PROMPT_TILES, 0), 0)


def _stream_tile(prompt_ref, sample_ref):
    return jnp.where(pl.program_id(0) < PROMPT_TILES, prompt_ref[...], sample_ref[...])


def _mod_row(mod_ref, cond, which):
    return mod_ref[pl.ds(cond, 1), which * D_MODEL:(which + 1) * D_MODEL]


def _norm_mod(x, nw, shift, scale):
    y = x * lax.rsqrt(jnp.mean(x * x, axis=-1, keepdims=True) + EPS)
    return (y * nw) * (1.0 + scale) + shift


def _silu(x):
    return x * jax.nn.sigmoid(x)


def _ada_kernel(c_ref, w_ref, b_ref, o_ref):
    s = [_silu(c_ref[r]) for r in range(N_COND_USED)]
    tn = w_ref.shape[-1]
    rows = []
    for r in range(N_COND_USED):
        cols = [jnp.sum(w_ref[0, :, j * LANES:(j + 1) * LANES] * s[r], axis=0, keepdims=True)
                for j in range(tn // LANES)]
        rows.append(jnp.concatenate(cols, axis=1) + b_ref[0])
    rows.append(jnp.zeros((N_COND - N_COND_USED, tn), F32))
    o_ref[0] = jnp.concatenate(rows, axis=0)


def _ada(cond, ada_w, ada_b):
    depth, _, n = ada_w.shape
    tn = 1536
    cond_cols = jnp.broadcast_to(cond[:, :, None], (N_COND_USED, D_MODEL, LANES))
    return pl.pallas_call(
        _ada_kernel,
        out_shape=jax.ShapeDtypeStruct((depth, N_COND, n), F32),
        grid=(depth, n // tn),
        in_specs=[
            pl.BlockSpec((N_COND_USED, D_MODEL, LANES), lambda l, j: (0, 0, 0)),
            pl.BlockSpec((1, D_MODEL, tn), lambda l, j: (l, 0, j)),
            pl.BlockSpec((1, 1, tn), lambda l, j: (l, 0, j)),
        ],
        out_specs=pl.BlockSpec((1, N_COND, tn), lambda l, j: (l, 0, j)),
        compiler_params=_params(40, "arbitrary", "arbitrary"),
        name="ada_modulation",
    )(cond_cols, ada_w, ada_b.reshape(depth, 1, n))


def _hg_in_kernel(xp_ref, xs_ref, mod_ref, nw_ref, w_ref, lbl_ref,
                  q_ref, v_ref, g_ref, lff_ref, kf_ref, lfb_ref, kb_ref):
    i = pl.program_id(0) * (HG_IN_TM // TM)
    cond = _tile_cond(i)
    x = jnp.where(i < PROMPT_TILES, xp_ref[...], xs_ref[...])
    h = _norm_mod(x, nw_ref[...], _mod_row(mod_ref, cond, 0), _mod_row(mod_ref, cond, 1)).astype(BF16)

    def proj(c):
        return jnp.dot(h, w_ref[:, c * D_MODEL:(c + 1) * D_MODEL], preferred_element_type=F32)

    l0, l1, l2 = lbl_ref[0], lbl_ref[1], lbl_ref[2]
    m = jnp.maximum(jnp.maximum(l0, l1), l2)
    e0, e1, e2 = jnp.exp(l0 - m), jnp.exp(l1 - m), jnp.exp(l2 - m)
    lb = e0 / (e0 + e1 + e2)

    q_ref[...] = proj(0).astype(BF16)
    v_ref[...] = proj(1).astype(BF16)
    for d, (lf_ref, k_ref) in enumerate(((lff_ref, kf_ref), (lfb_ref, kb_ref))):
        lbd = lb[d:d + 1, :]
        f = lbd + (1.0 - lbd) * jax.nn.sigmoid(proj(2 + d))
        lf = jnp.log(f)
        hi = lf.astype(BF16)
        lf_ref[0] = hi
        lf_ref[1] = (lf - hi.astype(F32)).astype(BF16)
        k_ref[...] = (1.0 - f).astype(BF16)
    g_ref[...] = proj(4).astype(BF16)


def _hg_in(xp, xs, mod, nw, w_in, lb_logits):
    tile = lambda i: (i, 0)
    fixed2 = lambda i: (0, 0)
    bf = jax.ShapeDtypeStruct((N_TOK, D_MODEL), BF16)
    ff = jax.ShapeDtypeStruct((2, N_TOK, D_MODEL), BF16)
    blk = pl.BlockSpec((HG_IN_TM, D_MODEL), tile)
    split_blk = pl.BlockSpec((2, HG_IN_TM, D_MODEL), lambda i: (0, i, 0))
    prompt_steps = N_PROMPT // HG_IN_TM
    return pl.pallas_call(
        _hg_in_kernel,
        out_shape=(bf, bf, bf, ff, bf, ff, bf),
        grid=(N_TOK // HG_IN_TM,),
        in_specs=[
            pl.BlockSpec((HG_IN_TM, D_MODEL), lambda i: (jnp.minimum(i, prompt_steps - 1), 0)),
            pl.BlockSpec((HG_IN_TM, D_MODEL), lambda i: (jnp.maximum(i - prompt_steps, 0), 0)),
            pl.BlockSpec(mod.shape, fixed2),
            pl.BlockSpec((1, D_MODEL), fixed2),
            pl.BlockSpec(w_in.shape, fixed2, pipeline_mode=pl.Buffered(1)),
            pl.BlockSpec(lb_logits.shape, lambda i: (0, 0, 0)),
        ],
        out_specs=(blk, blk, blk, split_blk, blk, split_blk, blk),
        compiler_params=_params(56, "arbitrary"),
        name="hgrn_in_proj",
    )(xp, xs, mod, nw, w_in, lb_logits)


def _gla_exact(reverse, q_ref, v_ref, lf_ref, k_ref, st_ref, o_ref):
    nt = (((1,), (1,)), ((), ()))
    tn = (((0,), (0,)), ((), ()))
    n_chunks = TM // CHUNK
    row_id = lax.broadcasted_iota(jnp.int32, (CHUNK, HG_DK), 0)
    order = range(CHUNK - 1, -1, -1) if reverse else range(CHUNK)

    def chunk_head(it, carry):
        step, h = it // HG_HEADS, it % HG_HEADS
        ci = (n_chunks - 1 - step) if reverse else step
        rows = pl.ds(pl.multiple_of(ci * CHUNK, CHUNK), CHUNK)
        cols = pl.ds(pl.multiple_of(h * HG_DK, HG_DK), HG_DK)
        q, k, v = q_ref[rows, cols], k_ref[rows, cols], v_ref[rows, cols]
        f = jnp.exp(lf_ref[0, rows, cols].astype(F32) + lf_ref[1, rows, cols].astype(F32))
        st = st_ref[0, h]
        o = jnp.zeros((CHUNK, HG_DK), F32)
        for t in order:
            one = row_id == t
            v_t = jnp.where(one, v, jnp.zeros_like(v))
            st = st * f[t:t + 1, :] + lax.dot_general(v_t, k, tn, preferred_element_type=F32)
            o = jnp.where(one, lax.dot_general(q, st.astype(BF16), nt, preferred_element_type=F32), o)
        st_ref[0, h] = st
        o_ref[rows, cols] = o.astype(o_ref.dtype)
        return carry

    lax.fori_loop(0, n_chunks * HG_HEADS, chunk_head, 0)


class _GlaDirection(NamedTuple):
    reverse: bool
    q: Any
    v: Any
    lf: Any
    k: Any
    o: Any
    st: Any
    sw: Any
    ut: Any
    qd: Any
    kd: Any
    ki: Any
    cd: Any
    b: Any


_NT = (((1,), (1,)), ((), ()))
_TN = (((0,), (0,)), ((), ()))
_HG_COLS = [slice(h * HG_DK, (h + 1) * HG_DK) for h in range(HG_HEADS)]
_N_CHUNKS = TM // CHUNK
_N_PAIRS = _N_CHUNKS // 2


def _gla_prepare(d):
    r = lax.broadcasted_iota(jnp.int32, (TM, TM), 0)
    c = lax.broadcasted_iota(jnp.int32, (TM, TM), 1)
    same = (r // CHUNK) == (c // CHUNK)
    tri = (same & ((c >= r) if d.reverse else (c <= r))).astype(BF16)
    b = jnp.dot(tri, d.lf[0], preferred_element_type=F32) + jnp.dot(tri, d.lf[1], preferred_element_type=F32)
    d.b[...] = b
    edge = 0 if d.reverse else CHUNK - 1
    total = b.reshape(_N_CHUNKS, CHUNK, D_MODEL)[:, edge, :]
    d.cd[...] = jnp.exp(total)
    return jnp.max(-total) <= DECAY_CLAMP


def _gla_pair_rows(d, step):
    pi = (_N_PAIRS - 1 - step) if d.reverse else step
    row0 = pi * 2 * CHUNK
    lo, hi = (pl.ds(row0, CHUNK), 2 * pi), (pl.ds(row0 + CHUNK, CHUNK), 2 * pi + 1)
    return pl.ds(row0, 2 * CHUNK), ((hi, lo) if d.reverse else (lo, hi))


def _chunk_decay(d, chunk):
    return d.cd[pl.ds(chunk, 1), :]


def _gla_decayed_operands(d, step):
    both, ((_, c_first), (_, c_second)) = _gla_pair_rows(d, step)
    b = d.b[both, :]
    ki = d.k[both, :] * jnp.exp(jnp.minimum(-b, DECAY_CLAMP)).astype(BF16)
    d.qd[both, :] = d.q[both, :] * jnp.exp(b).astype(BF16)
    d.ki[both, :] = ki
    c_lo, c_hi = (c_second, c_first) if d.reverse else (c_first, c_second)
    ends = jnp.concatenate([jnp.broadcast_to(_chunk_decay(d, c).astype(BF16), (CHUNK, D_MODEL)) for c in (c_lo, c_hi)],
                           axis=0)
    d.kd[both, :] = ki * ends


def _rows_scaled(x, scale_row, second_half):
    scale = jnp.broadcast_to(scale_row.astype(BF16), (CHUNK, x.shape[1]))
    ones = jnp.ones((CHUNK, x.shape[1]), BF16)
    return x * jnp.concatenate([ones, scale] if second_half else [scale, ones], axis=0)


def _gla_key_value_product(d, step):
    _gla_decayed_operands(d, step)
    both, (_, (_, c_second)) = _gla_pair_rows(d, step)
    decay_second = _chunk_decay(d, c_second)
    for h, cols in enumerate(_HG_COLS):
        keys = _rows_scaled(d.kd[both, cols], decay_second[:, cols], second_half=d.reverse)
        d.ut[h] = lax.dot_general(d.v[both, cols], keys, _TN, preferred_element_type=F32)


def _gla_start(d):
    for h in range(HG_HEADS):
        d.sw[0, h] = d.st[0, h].T.astype(BF16)
    _gla_key_value_product(d, 0)


def _gla_pair(d, step, src, dst):
    sr = lax.broadcasted_iota(jnp.int32, (CHUNK, CHUNK), 0)
    sc = lax.broadcasted_iota(jnp.int32, (CHUNK, CHUNK), 1)
    tr = lax.broadcasted_iota(jnp.int32, (CHUNK, 2 * CHUNK), 0)
    tc = lax.broadcasted_iota(jnp.int32, (CHUNK, 2 * CHUNK), 1)
    if d.reverse:
        keep_first = sc >= sr
        keep_second = (tc >= tr) | (tc >= CHUNK)
    else:
        keep_first = sc <= sr
        keep_second = (tc < CHUNK) | (tc - CHUNK <= tr)
    both, ((first, c_first), (second, c_second)) = _gla_pair_rows(d, step)
    decay_first = _chunk_decay(d, c_first)
    decay_both = decay_first * _chunk_decay(d, c_second)
    for h, cols in enumerate(_HG_COLS):
        new = d.st[src, h] * decay_both[:, cols] + d.ut[h]
        d.st[dst, h] = new
        d.sw[dst, h] = new.T.astype(BF16)
    a_first = [lax.dot_general(d.qd[first, cols], d.ki[first, cols], _NT, preferred_element_type=F32)
               for cols in _HG_COLS]
    a_second = []
    for cols in _HG_COLS:
        lo_keys = (d.ki if d.reverse else d.kd)[pl.ds(both.start, CHUNK), cols]
        hi_keys = (d.kd if d.reverse else d.ki)[pl.ds(both.start + CHUNK, CHUNK), cols]
        keys = jnp.concatenate([lo_keys, hi_keys], axis=0)
        a_second.append(lax.dot_general(d.qd[second, cols], keys, _NT, preferred_element_type=F32))
    inter = [jnp.dot(_rows_scaled(d.qd[both, cols], decay_first[:, cols], second_half=not d.reverse),
                     d.sw[src, h], preferred_element_type=F32) for h, cols in enumerate(_HG_COLS)]
    if step + 1 < _N_PAIRS:
        _gla_key_value_product(d, step + 1)
    first_half, second_half = (slice(CHUNK, None), slice(None, CHUNK)) if d.reverse else \
                              (slice(None, CHUNK), slice(CHUNK, None))
    for h, cols in enumerate(_HG_COLS):
        am = jnp.where(keep_first, a_first[h], 0.0).astype(BF16)
        d.o[first, cols] = (jnp.dot(am, d.v[first, cols], preferred_element_type=F32)
                            + inter[h][first_half]).astype(d.o.dtype)
        am = jnp.where(keep_second, a_second[h], 0.0).astype(BF16)
        d.o[second, cols] = (jnp.dot(am, d.v[both, cols], preferred_element_type=F32)
                             + inter[h][second_half]).astype(d.o.dtype)


def _gla_kernel(qf_ref, vf_ref, lff_ref, kf_ref, qb_ref, vb_ref, lfb_ref, kb_ref, s0_ref,
                of_ref, ob_ref, sout_ref, st_scr, sw_scr, ut_scr, qd_scr, kd_scr, ki_scr, cd_scr, b_scr):
    i = pl.program_id(0)
    is_prompt = i < PROMPT_TILES
    first = jnp.logical_or(is_prompt, (i - PROMPT_TILES) % TILES_PER_DEC_SEQ == 0)
    work = [tuple(scr.at[n] for scr in (st_scr, sw_scr, ut_scr, qd_scr, kd_scr, ki_scr, cd_scr, b_scr))
            for n in range(2)]
    fwd = _GlaDirection(False, qf_ref, vf_ref, lff_ref, kf_ref, of_ref, *work[0])
    bwd = _GlaDirection(True, qb_ref, vb_ref, lfb_ref, kb_ref, ob_ref, *work[1])

    @pl.when(is_prompt)
    def _():
        for d in (fwd, bwd):
            d.st[0] = jnp.zeros((HG_HEADS, HG_DK, HG_DK), F32)

    @pl.when(jnp.logical_and(first, jnp.logical_not(is_prompt)))
    def _():
        for n, d in enumerate((fwd, bwd)):
            for h in range(HG_HEADS):
                d.st[0, h] = s0_ref[0, 0, n, h].T

    exact_scores = jnp.logical_and(_gla_prepare(fwd), _gla_prepare(bwd))

    @pl.when(exact_scores)
    def _():
        _gla_start(fwd)
        _gla_start(bwd)

        for step in range(_N_PAIRS):
            for d in (fwd, bwd):
                _gla_pair(d, step, step % 2, 1 - step % 2)

    @pl.when(jnp.logical_not(exact_scores))
    def _():
        for d in (fwd, bwd):
            _gla_exact(d.reverse, d.q, d.v, d.lf, d.k, d.st, d.o)

    @pl.when(is_prompt)
    def _():
        for n, d in enumerate((fwd, bwd)):
            for h in range(HG_HEADS):
                sout_ref[0, 0, n, h] = d.st[0, h].T


def _gla(q, v, lff, kf, lfb, kb, state_hgrn):
    def fwd_tile(i):
        return (i, 0)

    def bwd_tile(i):
        j = (i - PROMPT_TILES) % TILES_PER_DEC_SEQ
        return (jnp.where(i < PROMPT_TILES, i, i - j + (TILES_PER_DEC_SEQ - 1 - j)), 0)

    def s0_idx(i):
        return (jnp.maximum(i - PROMPT_TILES, 0) // TILES_PER_DEC_SEQ, 0, 0, 0, 0, 0)

    def sout_idx(i):
        return (jnp.minimum(i, PROMPT_TILES - 1), 0, 0, 0, 0, 0)

    f_blk = pl.BlockSpec((TM, D_MODEL), fwd_tile)
    b_blk = pl.BlockSpec((TM, D_MODEL), bwd_tile)
    f_split = pl.BlockSpec((2, TM, D_MODEL), lambda i: (0,) + fwd_tile(i))
    b_split = pl.BlockSpec((2, TM, D_MODEL), lambda i: (0,) + bwd_tile(i))
    st_blk = (1, 1, 2, HG_HEADS, HG_DK, HG_DK)
    return pl.pallas_call(
        _gla_kernel,
        out_shape=(
            jax.ShapeDtypeStruct((N_TOK, D_MODEL), BF16),
            jax.ShapeDtypeStruct((N_TOK, D_MODEL), BF16),
            jax.ShapeDtypeStruct((BATCH,) + st_blk[1:], F32),
        ),
        grid=(N_TILES,),
        in_specs=[f_blk, f_blk, f_split, f_blk, b_blk, b_blk, b_split, b_blk, pl.BlockSpec(st_blk, s0_idx)],
        out_specs=(f_blk, b_blk, pl.BlockSpec(st_blk, sout_idx)),
        scratch_shapes=[
            pltpu.VMEM((2, 2, HG_HEADS, HG_DK, HG_DK), F32),
            pltpu.VMEM((2, 2, HG_HEADS, HG_DK, HG_DK), BF16),
            pltpu.VMEM((2, HG_HEADS, HG_DK, HG_DK), F32),
            pltpu.VMEM((2, TM, D_MODEL), BF16),
            pltpu.VMEM((2, TM, D_MODEL), BF16),
            pltpu.VMEM((2, TM, D_MODEL), BF16),
            pltpu.VMEM((2, TM // CHUNK, D_MODEL), F32),
            pltpu.VMEM((2, TM, D_MODEL), F32),
        ],
        compiler_params=_params(48, "arbitrary"),
        name="hgrn_recurrence",
    )(q, v, lff, kf, q, v, lfb, kb, state_hgrn)


def _route(logits):
    lane = lax.broadcasted_iota(jnp.int32, logits.shape, 1)
    neg = jnp.float32(-jnp.inf)
    big = jnp.int32(ROUTER_LANES)

    def first_max(x):
        m = jnp.max(x, axis=-1, keepdims=True)
        return m, jnp.min(jnp.where(x == m, lane, big), axis=-1, keepdims=True)

    is_group = (lane >= N_EXPERTS) & (lane < N_EXPERTS + N_GROUPS)
    gl = jnp.where(is_group, logits, neg)
    gmax, g_lane = first_max(gl)
    g_sel = g_lane - N_EXPERTS
    gsum = jnp.sum(jnp.exp(gl - gmax), axis=-1, keepdims=True)
    p_g = 1.0 / gsum
    in_sel = (lane < N_EXPERTS) & ((lane // EXPERTS_PER_GROUP) == g_sel)
    m1, i1 = first_max(jnp.where(in_sel, logits, neg))
    m2, i2 = first_max(jnp.where(in_sel & (lane != i1), logits, neg))
    e2 = jnp.exp(m2 - m1)
    return i1, i2, p_g / (1.0 + e2), p_g * e2 / (1.0 + e2)


def _pack_rows(x):
    q = PACK_WORDS
    bits = pltpu.bitcast(x.astype(BF16).astype(F32), jnp.uint32)
    return [(bits[:, (2 + h) * q:(3 + h) * q] & jnp.uint32(0xFFFF0000)) | (bits[:, h * q:(h + 1) * q] >> 16)
            for h in range(2)]


def _unpack_rows(half0, half1):
    lo = lambda w: pltpu.bitcast(w << 16, F32).astype(BF16)
    hi = lambda w: pltpu.bitcast(w & jnp.uint32(0xFFFF0000), F32).astype(BF16)
    return [lo(half0), lo(half1), hi(half0), hi(half1)]


def _mixer_tail(mix_bf16, x, mod_ref, nw2_ref, wo_ref, wr_ref, br_ref, x1_ref, h2p_ref, lg_ref):
    cond = _tile_cond(pl.program_id(0))
    out = jnp.dot(mix_bf16, wo_ref[...], preferred_element_type=F32)
    x1 = x + _mod_row(mod_ref, cond, 2) * out
    x1_ref[...] = x1
    h2 = _norm_mod(x1, nw2_ref[...], _mod_row(mod_ref, cond, 3), _mod_row(mod_ref, cond, 4))
    for h, words in enumerate(_pack_rows(h2)):
        h2p_ref[h] = words
    h_hi = h2.astype(BF16)
    h_lo = (h2 - h_hi.astype(F32)).astype(BF16)
    lg_ref[...] = (jnp.dot(h_hi, wr_ref[0], preferred_element_type=F32)
                   + (jnp.dot(h_hi, wr_ref[1], preferred_element_type=F32)
                      + jnp.dot(h_lo, wr_ref[0], preferred_element_type=F32))) + br_ref[...]


def _route_kernel(lg_ref, ri_ref, rw_ref, cnt_ref, carry_scr):
    @pl.when(pl.program_id(0) == 0)
    def _():
        carry_scr[...] = jnp.zeros_like(carry_scr)

    logits = lg_ref[...]
    i1, i2, w1, w2 = _route(logits)
    lane = lax.broadcasted_iota(jnp.int32, logits.shape, 1)
    chosen = ((lane == i1) | (lane == i2)).astype(BF16)
    r = lax.broadcasted_iota(jnp.int32, (ROUTE_TM, ROUTE_TM), 0)
    c = lax.broadcasted_iota(jnp.int32, (ROUTE_TM, ROUTE_TM), 1)
    before = jnp.dot((c < r).astype(BF16), chosen, preferred_element_type=F32) + carry_scr[...]
    r1 = jnp.sum(jnp.where(lane == i1, before, 0.0), axis=-1, keepdims=True).astype(jnp.int32)
    r2 = jnp.sum(jnp.where(lane == i2, before, 0.0), axis=-1, keepdims=True).astype(jnp.int32)
    total = carry_scr[...] + jnp.sum(chosen.astype(F32), axis=0, keepdims=True)
    carry_scr[...] = total
    cnt_ref[...] = total
    ri_ref[...] = jnp.where(lane == 0, i1, jnp.where(lane == 1, i2, jnp.where(lane == 2, r1, r2)))
    rw_ref[...] = jnp.where(lane == 0, w1, w2)


def _route_tokens(logits):
    blk = pl.BlockSpec((ROUTE_TM, ROUTER_LANES), lambda i: (i, 0))
    return pl.pallas_call(
        _route_kernel,
        out_shape=(
            jax.ShapeDtypeStruct((N_TOK, ROUTER_LANES), jnp.int32),
            jax.ShapeDtypeStruct((N_TOK, ROUTER_LANES), F32),
            jax.ShapeDtypeStruct((1, ROUTER_LANES), F32),
        ),
        grid=(N_TOK // ROUTE_TM,),
        in_specs=[blk],
        out_specs=(blk, blk, pl.BlockSpec((1, ROUTER_LANES), lambda i: (0, 0))),
        scratch_shapes=[pltpu.VMEM((1, ROUTER_LANES), F32)],
        compiler_params=_params(32, "arbitrary"),
        name="moe_route",
    )(logits)


def _hg_out_kernel(of_ref, ob_ref, g_ref, on_ref, xp_ref, xs_ref, *rest):
    o = of_ref[...].astype(F32) + ob_ref[...].astype(F32)
    parts = []
    for h in range(HG_HEADS):
        oh = o[:, h * HG_DK:(h + 1) * HG_DK]
        parts.append(oh * lax.rsqrt(jnp.mean(oh * oh, axis=-1, keepdims=True) + EPS) * on_ref[...])
    y = jnp.concatenate(parts, axis=1) * _silu(g_ref[...].astype(F32))
    _mixer_tail(y.astype(BF16), _stream_tile(xp_ref, xs_ref), *rest)


def _at_out_kernel(ac_ref, al_ref, x_ref, *rest):
    _mixer_tail(_stream_tile(ac_ref, al_ref), x_ref[...], *rest)


def _mixer_out(kernel_fn, name, mix_inputs, mix_specs, mod, nw2, w_out, w_router, b_router):
    tile = lambda i: (i, 0)
    fixed2 = lambda i: (0, 0)
    blk = pl.BlockSpec((TM, D_MODEL), tile)
    lanes_blk = pl.BlockSpec((TM, ROUTER_LANES), tile)
    return pl.pallas_call(
        kernel_fn,
        out_shape=(
            jax.ShapeDtypeStruct((N_TOK, D_MODEL), F32),
            jax.ShapeDtypeStruct((2, N_TOK, PACK_WORDS), jnp.uint32),
            jax.ShapeDtypeStruct((N_TOK, ROUTER_LANES), F32),
        ),
        grid=(N_TILES,),
        in_specs=list(mix_specs) + [
            pl.BlockSpec(mod.shape, fixed2),
            pl.BlockSpec((1, D_MODEL), fixed2),
            pl.BlockSpec(w_out.shape, fixed2),
            pl.BlockSpec(w_router.shape, lambda i: (0, 0, 0)),
            pl.BlockSpec((1, ROUTER_LANES), fixed2),
        ],
        out_specs=(blk, pl.BlockSpec((2, TM, PACK_WORDS), lambda i: (0, i, 0)), lanes_blk),
        compiler_params=_params(40, "arbitrary"),
        name=name,
    )(*mix_inputs, mod, nw2, w_out, w_router, b_router)


def _moe_plan(route_i, counts):
    cnt = counts[0, :N_EXPERTS].astype(jnp.int32)
    padded = ((cnt + MOE_TILE - 1) // MOE_TILE) * MOE_TILE
    ends = jnp.cumsum(padded)
    offs = ends - padded
    experts = route_i[:, 0:2]
    pos = jnp.sum(jnp.where(experts[:, :, None] == jnp.arange(N_EXPERTS)[None, None, :], offs[None, None, :], 0),
                  axis=-1) + route_i[:, 2:4]
    tile_start = jnp.arange(MOE_ROWS // MOE_TILE, dtype=jnp.int32) * MOE_TILE
    tile_expert = jnp.minimum(jnp.sum(ends[None, :] <= tile_start[:, None], axis=1), N_EXPERTS - 1).astype(jnp.int32)
    of_tile = tile_expert[:, None] == jnp.arange(N_EXPERTS)[None, :]
    tile_offs = jnp.sum(jnp.where(of_tile, offs[None, :], 0), axis=1)
    tile_cnt = jnp.sum(jnp.where(of_tile, cnt[None, :], 0), axis=1)
    tile_rows = jnp.clip(tile_offs + tile_cnt - tile_start, 0, MOE_TILE).astype(jnp.int32)
    tile_first = (tile_start == tile_offs).astype(jnp.int32)
    n_active = (ends[-1] // MOE_TILE).astype(jnp.int32).reshape(1)
    ids = jnp.arange(N_EXPERTS)
    used = cnt > 0
    slot = (jnp.cumsum(used.astype(jnp.int32)) - 1) % 2
    later = jnp.min(jnp.where(used[None, :] & (ids[None, :] > ids[:, None]), ids[None, :], N_EXPERTS), axis=1)
    later = jnp.where(later == N_EXPERTS, -1, later)
    tile_slot = jnp.sum(jnp.where(of_tile, slot[None, :], 0), axis=1).astype(jnp.int32)
    tile_next = jnp.sum(jnp.where(of_tile, later[None, :], 0), axis=1).astype(jnp.int32)
    return pos.astype(jnp.int32), (tile_expert, tile_rows, tile_first, n_active, tile_slot, tile_next)


SC_WINDOW = 128


def _sc_mesh():
    return plsc.VectorSubcoreMesh(core_axis_name="c", subcore_axis_name="s")


def _sc_scatter_rows(x, idx_a, idx_b, n_out_rows):
    n = x.shape[0]

    @functools.partial(pl.kernel, out_type=jax.ShapeDtypeStruct((n_out_rows, PACK_WORDS), x.dtype), mesh=_sc_mesh(),
                       scratch_types=[pltpu.SemaphoreType.DMA, pltpu.SemaphoreType.DMA])
    def scatter(x_hbm, ia_hbm, ib_hbm, o_hbm, sem_a, sem_b):
        def body(x_vmem, ia_vmem, ib_vmem):
            copy_a = pltpu.async_copy(x_vmem, o_hbm.at[ia_vmem.at[0]], sem_a)
            copy_b = pltpu.async_copy(x_vmem, o_hbm.at[ib_vmem.at[0]], sem_b)
            copy_a.wait()
            copy_b.wait()

        idx_spec = pl.BlockSpec((1, SC_WINDOW), index_map=lambda i: (0, i))
        pltpu.emit_pipeline(
            body, grid=(n // SC_WINDOW,),
            in_specs=[pl.BlockSpec((SC_WINDOW, PACK_WORDS), index_map=lambda i: (i, 0)), idx_spec, idx_spec],
            out_specs=[],
            core_axis_name=("c", "s"), dimension_semantics=(pltpu.PARALLEL,),
        )(x_hbm, ia_hbm, ib_hbm)

    return scatter(x, idx_a.reshape(1, n), idx_b.reshape(1, n))


def _sc_gather_rows(table, idx):
    n = idx.shape[0]

    @functools.partial(pl.kernel, out_type=jax.ShapeDtypeStruct((n, PACK_WORDS), table.dtype), mesh=_sc_mesh())
    def gather(t_hbm, i_hbm, o_hbm):
        def body(i_vmem, o_vmem):
            pltpu.sync_copy(t_hbm.at[i_vmem.at[0]], o_vmem)

        pltpu.emit_pipeline(
            body, grid=(n // SC_WINDOW,),
            in_specs=[pl.BlockSpec((1, SC_WINDOW), index_map=lambda i: (0, i))],
            out_specs=[pl.BlockSpec((SC_WINDOW, PACK_WORDS), index_map=lambda i: (i, 0))],
            core_axis_name=("c", "s"), dimension_semantics=(pltpu.PARALLEL,),
        )(i_hbm, o_hbm)

    return gather(table, idx.reshape(1, n))


def _ffn_kernel(te_ref, tr_ref, tf_ref, na_ref, ts_ref, tn_ref, xs_ref, w1_hbm, w3_hbm, w2_hbm, ys_ref,
                w1_scr, w3_scr, w2_scr, w1_buf, w3_buf, w2_buf, sem, *, layer):
    def weight_copies(expert, slot):
        pairs = ((w1_hbm, w1_buf), (w3_hbm, w3_buf), (w2_hbm, w2_buf))
        return [pltpu.make_async_copy(hbm.at[layer, expert], buf.at[slot], sem.at[slot, j])
                for j, (hbm, buf) in enumerate(pairs)]

    def tile(t, rows):
        @pl.when(tf_ref[t] == 1)
        def _():
            slot = ts_ref[t]

            @pl.when(t == 0)
            def _():
                for copy in weight_copies(te_ref[t], slot):
                    copy.start()

            for copy in weight_copies(te_ref[t], slot):
                copy.wait()

            @pl.when(tn_ref[t] >= 0)
            def _():
                for copy in weight_copies(tn_ref[t], 1 - slot):
                    copy.start()

            w1_scr[...] = w1_buf[slot].astype(BF16)
            w3_scr[...] = w3_buf[slot].astype(BF16)
            w2_scr[...] = w2_buf[slot].astype(BF16)

        row = lax.broadcasted_iota(jnp.int32, (MOE_TILE, PACK_WORDS), 0)
        live = row < tr_ref[t]
        halves = [jnp.where(live, xs_ref[h, rows, :], jnp.zeros((MOE_TILE, PACK_WORDS), jnp.uint32))
                  for h in range(2)]
        chunks = _unpack_rows(*halves)

        def up(w_scr):
            acc = jnp.dot(chunks[0], w_scr[:PACK_WORDS, :], preferred_element_type=F32)
            for k in range(1, len(chunks)):
                acc = acc + jnp.dot(chunks[k], w_scr[k * PACK_WORDS:(k + 1) * PACK_WORDS, :],
                                    preferred_element_type=F32)
            return acc

        hid = (_silu(up(w1_scr)) * up(w3_scr)).astype(BF16)
        for h, words in enumerate(_pack_rows(jnp.dot(hid, w2_scr[...], preferred_element_type=F32))):
            ys_ref[h, rows, :] = words

    for j in range(MOE_STEP_TILES):
        t = pl.program_id(0) * MOE_STEP_TILES + j
        pl.when(t < na_ref[0])(functools.partial(tile, t, slice(j * MOE_TILE, (j + 1) * MOE_TILE)))


def _ffn(xs, tables, layer, w1, w3, w2):
    step_rows = MOE_STEP_TILES * MOE_TILE
    row_tile = lambda s, te, tr, tf, na, ts, tn: (0, jnp.minimum(s, (na[0] - 1) // MOE_STEP_TILES), 0)
    hbm = pl.BlockSpec(memory_space=pl.ANY)
    return pl.pallas_call(
        functools.partial(_ffn_kernel, layer=layer),
        out_shape=jax.ShapeDtypeStruct((2, MOE_ROWS, PACK_WORDS), jnp.uint32),
        grid_spec=pltpu.PrefetchScalarGridSpec(
            num_scalar_prefetch=len(tables),
            grid=(MOE_ROWS // step_rows,),
            in_specs=[pl.BlockSpec((2, step_rows, PACK_WORDS), row_tile), hbm, hbm, hbm],
            out_specs=pl.BlockSpec((2, step_rows, PACK_WORDS), row_tile),
            scratch_shapes=[
                pltpu.VMEM((D_MODEL, D_EXPERT), BF16),
                pltpu.VMEM((D_MODEL, D_EXPERT), BF16),
                pltpu.VMEM((D_EXPERT, D_MODEL), BF16),
                pltpu.VMEM((2, D_MODEL, D_EXPERT), F32),
                pltpu.VMEM((2, D_MODEL, D_EXPERT), F32),
                pltpu.VMEM((2, D_EXPERT, D_MODEL), F32),
                pltpu.SemaphoreType.DMA((2, 3)),
            ],
        ),
        compiler_params=_params(32, "arbitrary"),
        name="moe_experts",
    )(*tables, xs, w1, w3, w2)


def _moe_res(x_ref, y_ref, rw_ref, mod_ref):
    cond = _tile_cond(pl.program_id(0))
    rw = rw_ref[...]
    wa = rw[:, 0:1]
    wb = rw[:, 1:2]
    ya = _unpack_rows(y_ref[0], y_ref[1])
    yb = _unpack_rows(y_ref[2], y_ref[3])
    y = jnp.concatenate([wa * a.astype(F32) + wb * b.astype(F32) for a, b in zip(ya, yb)], axis=1)
    return x_ref[...] + _mod_row(mod_ref, cond, 5) * y


def _moe_res_final_kernel(x_ref, y_ref, rw_ref, mod_ref, fn_ref, op_ref, os_ref):
    x = _moe_res(x_ref, y_ref, rw_ref, mod_ref)
    y = x * lax.rsqrt(jnp.mean(x * x, axis=-1, keepdims=True) + EPS) * fn_ref[...]
    is_prompt = pl.program_id(0) < PROMPT_TILES

    @pl.when(is_prompt)
    def _():
        op_ref[...] = y

    @pl.when(jnp.logical_not(is_prompt))
    def _():
        os_ref[...] = y


def _moe_combine(x1, y_pairs, route_w, mod, final_norm):
    tile = lambda i: (i, 0)
    fixed2 = lambda i: (0, 0)
    return pl.pallas_call(
        _moe_res_final_kernel,
        out_shape=(jax.ShapeDtypeStruct((N_PROMPT, D_MODEL), F32),
                   jax.ShapeDtypeStruct((N_TOK - N_PROMPT, D_MODEL), F32)),
        grid=(N_TILES,),
        in_specs=[pl.BlockSpec((TM, D_MODEL), tile),
                  pl.BlockSpec((4, TM, PACK_WORDS), lambda i: (0, i, 0)),
                  pl.BlockSpec((TM, ROUTER_LANES), tile),
                  pl.BlockSpec(mod.shape, fixed2),
                  pl.BlockSpec((1, D_MODEL), fixed2)],
        out_specs=(pl.BlockSpec((TM, D_MODEL), _prompt_tile), pl.BlockSpec((TM, D_MODEL), _sample_tile)),
        compiler_params=_params(32, "arbitrary"),
        name="moe_combine",
    )(x1, y_pairs, route_w, mod, final_norm)


def _moe_layer(x1, h2p, logits, layer, w1, w3, w2):
    route_i, route_w, counts = _route_tokens(logits)
    pos, tables = _moe_plan(route_i, counts)
    idx_a = jnp.concatenate([pos[:, 0], pos[:, 0] + MOE_ROWS])
    idx_b = jnp.concatenate([pos[:, 1], pos[:, 1] + MOE_ROWS])
    xs = _sc_scatter_rows(h2p.reshape(2 * N_TOK, PACK_WORDS), idx_a, idx_b, 2 * MOE_ROWS)
    ys = _ffn(xs.reshape(2, MOE_ROWS, PACK_WORDS), tables, layer, w1, w3, w2)
    y_pairs = _sc_gather_rows(ys.reshape(2 * MOE_ROWS, PACK_WORDS), jnp.concatenate([idx_a, idx_b]))
    return x1, y_pairs.reshape(4, N_TOK, PACK_WORDS), route_w


def _swap_rotary_halves(x):
    n = x.shape[-1]
    lane = lax.broadcasted_iota(jnp.int32, x.shape, 1)
    quarter = ROPE_HALF // 2
    return jnp.where((lane % ROPE_HALF) < quarter, pltpu.roll(x, n - quarter, 1), pltpu.roll(x, quarter, 1))


def _at_in_kernel(x1_ref, y_ref, rw_ref, mod_prev_ref, mod_ref, nw_ref, w_ref, cos_ref, sin_ref,
                  x_ref, q_ref, kt_ref, v_ref, kc_ref, vc_ref):
    i = pl.program_id(0)
    cond = _tile_cond(i)
    x = _moe_res(x1_ref, y_ref, rw_ref, mod_prev_ref)
    x_ref[...] = x
    h = _norm_mod(x, nw_ref[...], _mod_row(mod_ref, cond, 0), _mod_row(mod_ref, cond, 1)).astype(BF16)
    nq = ATT_Q_HEADS * ATT_HEAD_DIM
    nk = ATT_KV_HEADS * ATT_HEAD_DIM
    cos = cos_ref[...]
    sin = sin_ref[...]

    def rope(x):
        reps = x.shape[-1] // LANES
        return x * jnp.concatenate([cos] * reps, axis=1) + _swap_rotary_halves(x) * jnp.concatenate([sin] * reps, axis=1)

    q_ref[...] = (rope(jnp.dot(h, w_ref[:, :nq], preferred_element_type=F32)) * ATT_HEAD_DIM ** -0.5).astype(BF16)
    k = rope(jnp.dot(h, w_ref[:, nq:nq + nk], preferred_element_type=F32))
    v = jnp.dot(h, w_ref[:, nq + nk:], preferred_element_type=F32)
    kt = k.T
    kt_ref[...] = kt.astype(BF16)
    v_ref[...] = v.astype(BF16)

    @pl.when(i < PROMPT_TILES)
    def _():
        kc_ref[...] = kt
        vc_ref[...] = v.T


def _rope_tables():
    f32 = np.float32
    pos = np.arange(DEC_SEQ)
    t_row = (pos // GRID_W).astype(f32)
    t_col = (pos % GRID_W).astype(f32)
    inv = f32(ROPE_BASE) ** (-np.arange(0, ROPE_HALF, 2, dtype=f32) / f32(ROPE_HALF))
    j = np.arange(LANES) % ATT_HEAD_DIM
    freq = inv[(j % ROPE_HALF) % (ROPE_HALF // 2)]
    ang = (np.where((j < ROPE_HALF)[None, :], t_row[:, None], t_col[:, None]) * freq[None, :]).astype(f32)
    sign = np.where((j % ROPE_HALF) < ROPE_HALF // 2, -1.0, 1.0).astype(f32)
    cos = np.concatenate([np.ones((TM, LANES), f32), np.cos(ang)], axis=0)
    sin = np.concatenate([np.zeros((TM, LANES), f32), np.sin(ang) * sign[None, :]], axis=0)
    return jnp.asarray(cos, F32), jnp.asarray(sin, F32)


def _at_in(x1, y_pairs, route_w, mod_prev, mod, nw, w_in, cos, sin):
    tile = lambda i: (i, 0)
    fixed2 = lambda i: (0, 0)
    rope_tile = lambda i: (jnp.where(i < PROMPT_TILES, 0, 1 + (i - PROMPT_TILES) % TILES_PER_DEC_SEQ), 0)
    nk = ATT_KV_HEADS * ATT_HEAD_DIM
    return pl.pallas_call(
        _at_in_kernel,
        out_shape=(
            jax.ShapeDtypeStruct((N_TOK, D_MODEL), F32),
            jax.ShapeDtypeStruct((N_TOK, D_MODEL), BF16),
            jax.ShapeDtypeStruct((nk, N_TOK), BF16),
            jax.ShapeDtypeStruct((N_TOK, nk), BF16),
            jax.ShapeDtypeStruct((N_PROMPT, nk), F32),
            jax.ShapeDtypeStruct((N_PROMPT, nk), F32),
        ),
        grid=(N_TILES,),
        in_specs=[
            pl.BlockSpec((TM, D_MODEL), tile),
            pl.BlockSpec((4, TM, PACK_WORDS), lambda i: (0, i, 0)),
            pl.BlockSpec((TM, ROUTER_LANES), tile),
            pl.BlockSpec(mod_prev.shape, fixed2),
            pl.BlockSpec(mod.shape, fixed2),
            pl.BlockSpec((1, D_MODEL), fixed2),
            pl.BlockSpec(w_in.shape, fixed2),
            pl.BlockSpec((TM, LANES), rope_tile),
            pl.BlockSpec((TM, LANES), rope_tile),
        ],
        out_specs=(pl.BlockSpec((TM, D_MODEL), tile), pl.BlockSpec((TM, D_MODEL), tile),
                   pl.BlockSpec((nk, TM), lambda i: (0, i)),
                   pl.BlockSpec((TM, nk), tile),
                   pl.BlockSpec((TM, nk), _prompt_tile), pl.BlockSpec((TM, nk), _prompt_tile)),
        compiler_params=_params(40, "arbitrary"),
        name="attn_in_proj",
    )(x1, y_pairs, route_w, mod_prev, mod, nw, w_in, cos, sin)


def _attend(q, kt_all, v_all, mask, sink_ref, o_ref):
    nq = q.shape[0]
    group_lanes = ATT_GROUP * ATT_HEAD_DIM
    lane = lax.broadcasted_iota(jnp.int32, (nq, group_lanes), 1)
    mine = [(lane // ATT_HEAD_DIM) == g for g in range(ATT_GROUP)]
    row_head = lax.broadcasted_iota(jnp.int32, (ATT_GROUP * nq, 1), 0) // nq
    if mask is not None:
        mask = jnp.concatenate([mask] * ATT_GROUP, axis=0)
    for hk in range(ATT_KV_HEADS):
        vh = v_all[:, hk * ATT_HEAD_DIM:(hk + 1) * ATT_HEAD_DIM]
        kt = jnp.concatenate([kt_all[hk * ATT_HEAD_DIM:(hk + 1) * ATT_HEAD_DIM, :]] * ATT_GROUP, axis=0)
        vt = jnp.concatenate([vh] * ATT_GROUP, axis=1)
        qg = q[:, hk * group_lanes:(hk + 1) * group_lanes]
        q_stack = jnp.concatenate([jnp.where(mine[g], qg, jnp.zeros_like(qg)) for g in range(ATT_GROUP)], axis=0)
        s = jnp.dot(q_stack, kt, preferred_element_type=F32)
        if mask is not None:
            s = jnp.where(mask, s, -jnp.inf)
        sink = jnp.zeros((ATT_GROUP * nq, 1), F32)
        for g in range(ATT_GROUP):
            sink = jnp.where(row_head == g, sink_ref[hk * ATT_GROUP + g], sink)
        m = jnp.maximum(jnp.max(s, axis=-1, keepdims=True), sink)
        p = jnp.exp(s - m)
        denom = jnp.sum(p, axis=-1, keepdims=True) + jnp.exp(sink - m)
        o = jnp.dot(p.astype(BF16), vt, preferred_element_type=F32) / denom
        acc = jnp.where(mine[0], o[:nq], 0.0)
        for g in range(1, ATT_GROUP):
            acc = acc + jnp.where(mine[g], o[g * nq:(g + 1) * nq], 0.0)
        o_ref[:, hk * group_lanes:(hk + 1) * group_lanes] = acc.astype(BF16)


def _ctx_attn_kernel(sink_ref, q_ref, k_ref, v_ref, o_ref):
    _attend(q_ref[...], k_ref[...], v_ref[...], None, sink_ref, o_ref)


def _lat_attn_kernel(sink_ref, q_ref, kp_ref, kc_ref, kn_ref, vp_ref, vc_ref, vn_ref, ck_ref, cv_ref, o_ref):
    jb = pl.program_id(1)
    kt_all = jnp.concatenate([kp_ref[...], kc_ref[...], kn_ref[...], ck_ref[0].astype(BF16)], axis=1)
    v_all = jnp.concatenate([vp_ref[...], vc_ref[...], vn_ref[...], cv_ref[0].astype(BF16)], axis=0)
    nkeys = 3 * BLOCK + PAST_LEN
    qi = lax.broadcasted_iota(jnp.int32, (BLOCK, nkeys), 0)
    kj = lax.broadcasted_iota(jnp.int32, (BLOCK, nkeys), 1)
    qpos = jb * BLOCK + qi
    kpos = (jb - 1) * BLOCK + kj
    local_ok = (jnp.abs(qpos - kpos) <= WINDOW) & (kpos >= 0) & (kpos < DEC_SEQ)
    mask = (kj >= 3 * BLOCK) | local_ok
    _attend(q_ref[...], kt_all, v_all, mask, sink_ref, o_ref)


def _attention(q, kt, v, cache_kt, cache_v, sink):
    nk = ATT_KV_HEADS * ATT_HEAD_DIM
    smem = pl.BlockSpec(memory_space=pltpu.SMEM)
    ctx = pl.pallas_call(
        _ctx_attn_kernel,
        out_shape=jax.ShapeDtypeStruct((N_PROMPT, D_MODEL), BF16),
        grid=(BATCH,),
        in_specs=[
            smem,
            pl.BlockSpec((SEQ, D_MODEL), lambda b: (b, 0)),
            pl.BlockSpec((nk, SEQ), lambda b: (0, b)),
            pl.BlockSpec((SEQ, nk), lambda b: (b, 0)),
        ],
        out_specs=pl.BlockSpec((SEQ, D_MODEL), lambda b: (b, 0)),
        compiler_params=_params(40, "arbitrary"),
        name="context_attention",
    )(sink, q, kt, v)

    nb = DEC_SEQ // BLOCK
    base = N_PROMPT // BLOCK
    cur = lambda b, j: (base + b * nb + j, 0)
    prev = lambda b, j: (base + b * nb + jnp.maximum(j - 1, 0), 0)
    nxt = lambda b, j: (base + b * nb + jnp.minimum(j + 1, nb - 1), 0)
    kv_blk = lambda f: pl.BlockSpec((BLOCK, nk), f)
    kt_blk = lambda f: pl.BlockSpec((nk, BLOCK), lambda b, j: f(b, j)[::-1])
    cache_blk = pl.BlockSpec((1, PAST_LEN, nk), lambda b, j: (b, 0, 0))
    cache_kt_blk = pl.BlockSpec((1, nk, PAST_LEN), lambda b, j: (b, 0, 0))
    lat = pl.pallas_call(
        _lat_attn_kernel,
        out_shape=jax.ShapeDtypeStruct((DEC_BATCH * DEC_SEQ, D_MODEL), BF16),
        grid=(DEC_BATCH, nb),
        in_specs=[
            smem,
            pl.BlockSpec((BLOCK, D_MODEL), cur),
            kt_blk(prev), kt_blk(cur), kt_blk(nxt),
            kv_blk(prev), kv_blk(cur), kv_blk(nxt),
            cache_kt_blk, cache_blk,
        ],
        out_specs=pl.BlockSpec((BLOCK, D_MODEL), lambda b, j: (b * nb + j, 0)),
        compiler_params=_params(40, "arbitrary", "arbitrary"),
        name="latent_attention",
    )(sink, q, kt, kt, kt, v, v, v, cache_kt, cache_v)
    return ctx, lat


def kernel(x_prompt, x_sample, state_hgrn, cache_k, cache_v, c, c_ctx, ada_w, ada_b, norm_w, hg_w_in,
           hg_lb_logits, hg_onorm, hg_w_out, at_w_in, at_sink, at_w_out, moe_w_group, moe_b_group,
           moe_w_expert, moe_b_expert, moe_w1, moe_w3, moe_w2, final_norm):
    xp = x_prompt.reshape(N_PROMPT, D_MODEL)
    xs = x_sample.reshape(N_TOK - N_PROMPT, D_MODEL)
    cond = jnp.concatenate([c_ctx[None, :], c], axis=0)
    mod = _ada(cond, ada_w, ada_b)
    nk = ATT_KV_HEADS * ATT_HEAD_DIM

    def router_params(i):
        pad = jnp.zeros((D_MODEL, ROUTER_LANES - N_EXPERTS - N_GROUPS), F32)
        w = jnp.concatenate([moe_w_expert[i], moe_w_group[i], pad], axis=1)
        b = jnp.concatenate([moe_b_expert[i], moe_b_group[i], pad[0]])[None, :]
        hi = w.astype(BF16)
        lo = (w - hi.astype(F32)).astype(BF16)
        return jnp.stack([hi, lo]), b

    tile = lambda i: (i, 0)
    blk = pl.BlockSpec((TM, D_MODEL), tile)

    prompt_blk = pl.BlockSpec((TM, D_MODEL), _prompt_tile)
    sample_blk = pl.BlockSpec((TM, D_MODEL), _sample_tile)
    q, v, g, lff, kf, lfb, kb = _hg_in(xp, xs, mod[0], norm_w[0, 0][None, :], hg_w_in[0].astype(BF16), hg_lb_logits)
    o_f, o_b, state_new = _gla(q, v, lff, kf, lfb, kb, state_hgrn)
    wr, br = router_params(0)
    routed = _mixer_out(
        _hg_out_kernel, "hgrn_out_route", (o_f, o_b, g, hg_onorm[0][None, :], xp, xs),
        (blk, blk, blk, pl.BlockSpec((1, HG_DK), lambda i: (0, 0)), prompt_blk, sample_blk),
        mod[0], norm_w[0, 1][None, :], hg_w_out[0].astype(BF16), wr, br)
    moe_out = _moe_layer(*routed, 0, moe_w1, moe_w3, moe_w2)

    cos, sin = _rope_tables()
    x, qa, ka, va, k_ctx, v_ctx = _at_in(*moe_out, mod[0], mod[1], norm_w[1, 0][None, :],
                                         at_w_in[0].astype(BF16), cos, sin)
    attn_ctx, attn_lat = _attention(qa, ka, va, cache_k[:, 0].reshape(DEC_BATCH, PAST_LEN, nk).transpose(0, 2, 1),
                                    cache_v[:, 0].reshape(DEC_BATCH, PAST_LEN, nk), at_sink[0])
    wr, br = router_params(1)
    routed = _mixer_out(
        _at_out_kernel, "attn_out_route", (attn_ctx, attn_lat, x), (prompt_blk, sample_blk, blk),
        mod[1], norm_w[1, 1][None, :], at_w_out[0].astype(BF16), wr, br)
    y_prompt, y_sample = _moe_combine(*_moe_layer(*routed, 1, moe_w1, moe_w3, moe_w2), mod[1], final_norm[None, :])

    def cache(feature_major):
        return feature_major.reshape(BATCH, 1, ATT_KV_HEADS, ATT_HEAD_DIM, SEQ).transpose(0, 1, 4, 2, 3)

    return (y_prompt.reshape(BATCH, SEQ, D_MODEL), y_sample.reshape(DEC_BATCH, DEC_SEQ, D_MODEL), state_new,
            cache(k_ctx), cache(v_ctx))
```

```python
import functools
from typing import Any, NamedTuple

import jax
import jax.numpy as jnp
import numpy as np
from jax import lax
from jax.experimental import pallas as pl
from jax.experimental.pallas import tpu as pltpu
from jax.experimental.pallas import tpu_sc as plsc

F32 = jnp.float32
BF16 = jnp.bfloat16

D_MODEL = 1024
BATCH = 16
SEQ = 256
DEC_BATCH = 2
DEC_SEQ = 1024
PAST_LEN = 512
GRID_W = 64
HG_HEADS = 8
HG_DK = 128
CHUNK = 16
ATT_HEAD_DIM = 64
ATT_Q_HEADS = 16
ATT_KV_HEADS = 4
ATT_GROUP = 4
WINDOW = 128
BLOCK = 128
ROPE_HALF = 32
ROPE_BASE = 10000.0
N_GROUPS = 4
EXPERTS_PER_GROUP = 8
N_EXPERTS = 32
D_EXPERT = 256
EPS = 1e-6

N_PROMPT = BATCH * SEQ
N_TOK = N_PROMPT + DEC_BATCH * DEC_SEQ
TM = 256
N_TILES = N_TOK // TM
PROMPT_TILES = N_PROMPT // TM
TILES_PER_DEC_SEQ = DEC_SEQ // TM
LANES = 128
N_COND_USED = 1 + DEC_BATCH
N_COND = 8
ROUTER_LANES = 128
ROUTE_TM = 1024
HG_IN_TM = 256
DECAY_CLAMP = 60.0
PACK_WORDS = D_MODEL // 4
ROW_WORDS = 2 * PACK_WORDS
MOE_TILE = 256
MOE_STEP_TILES = 4
MOE_ROWS = 2 * N_TOK + N_EXPERTS * MOE_TILE
MIB = 1024 * 1024


def _params(vmem_mib, *semantics):
    return pltpu.CompilerParams(dimension_semantics=semantics, vmem_limit_bytes=vmem_mib * MIB)


def _tile_cond(i):
    return jnp.where(i < PROMPT_TILES, 0, 1 + (i - PROMPT_TILES) // TILES_PER_DEC_SEQ)


def _prompt_tile(i):
    return (jnp.minimum(i, PROMPT_TILES - 1), 0)


def _sample_tile(i):
    return (jnp.maximum(i - PROMPT_TILES, 0), 0)


def _stream_tile(prompt_ref, sample_ref):
    return jnp.where(pl.program_id(0) < PROMPT_TILES, prompt_ref[...], sample_ref[...])


def _mod_row(mod_ref, cond, which):
    return mod_ref[pl.ds(cond, 1), which * D_MODEL:(which + 1) * D_MODEL]


def _norm_mod(x, nw, shift, scale):
    y = x * lax.rsqrt(jnp.mean(x * x, axis=-1, keepdims=True) + EPS)
    return (y * nw) * (1.0 + scale) + shift


def _silu(x):
    return x * jax.nn.sigmoid(x)


def _ada_kernel(c_ref, w_ref, b_ref, o_ref):
    s = [_silu(c_ref[r]) for r in range(N_COND_USED)]
    tn = w_ref.shape[-1]
    rows = []
    for r in range(N_COND_USED):
        cols = [jnp.sum(w_ref[0, :, j * LANES:(j + 1) * LANES] * s[r], axis=0, keepdims=True)
                for j in range(tn // LANES)]
        rows.append(jnp.concatenate(cols, axis=1) + b_ref[0])
    rows.append(jnp.zeros((N_COND - N_COND_USED, tn), F32))
    o_ref[0] = jnp.concatenate(rows, axis=0)


def _ada(cond, ada_w, ada_b):
    depth, _, n = ada_w.shape
    tn = 1536
    cond_cols = jnp.broadcast_to(cond[:, :, None], (N_COND_USED, D_MODEL, LANES))
    return pl.pallas_call(
        _ada_kernel,
        out_shape=jax.ShapeDtypeStruct((depth, N_COND, n), F32),
        grid=(depth, n // tn),
        in_specs=[
            pl.BlockSpec((N_COND_USED, D_MODEL, LANES), lambda l, j: (0, 0, 0)),
            pl.BlockSpec((1, D_MODEL, tn), lambda l, j: (l, 0, j)),
            pl.BlockSpec((1, 1, tn), lambda l, j: (l, 0, j)),
        ],
        out_specs=pl.BlockSpec((1, N_COND, tn), lambda l, j: (l, 0, j)),
        compiler_params=_params(40, "arbitrary", "arbitrary"),
        name="ada_modulation",
    )(cond_cols, ada_w, ada_b.reshape(depth, 1, n))


def _hg_in_kernel(xp_ref, xs_ref, mod_ref, nw_ref, w_ref, lbl_ref,
                  q_ref, v_ref, g_ref, lff_ref, kf_ref, lfb_ref, kb_ref):
    i = pl.program_id(0) * (HG_IN_TM // TM)
    cond = _tile_cond(i)
    x = jnp.where(i < PROMPT_TILES, xp_ref[...], xs_ref[...])
    h = _norm_mod(x, nw_ref[...], _mod_row(mod_ref, cond, 0), _mod_row(mod_ref, cond, 1)).astype(BF16)

    def proj(c):
        return jnp.dot(h, w_ref[:, c * D_MODEL:(c + 1) * D_MODEL], preferred_element_type=F32)

    l0, l1, l2 = lbl_ref[0], lbl_ref[1], lbl_ref[2]
    m = jnp.maximum(jnp.maximum(l0, l1), l2)
    e0, e1, e2 = jnp.exp(l0 - m), jnp.exp(l1 - m), jnp.exp(l2 - m)
    lb = e0 / (e0 + e1 + e2)

    q_ref[...] = proj(0).astype(BF16)
    v_ref[...] = proj(1).astype(BF16)
    for d, (lf_ref, k_ref) in enumerate(((lff_ref, kf_ref), (lfb_ref, kb_ref))):
        lbd = lb[d:d + 1, :]
        f = lbd + (1.0 - lbd) * jax.nn.sigmoid(proj(2 + d))
        lf = jnp.log(f)
        hi = lf.astype(BF16)
        lf_ref[0] = hi
        lf_ref[1] = (lf - hi.astype(F32)).astype(BF16)
        k_ref[...] = (1.0 - f).astype(BF16)
    g_ref[...] = proj(4).astype(BF16)


def _hg_in(xp, xs, mod, nw, w_in, lb_logits):
    tile = lambda i: (i, 0)
    fixed2 = lambda i: (0, 0)
    bf = jax.ShapeDtypeStruct((N_TOK, D_MODEL), BF16)
    ff = jax.ShapeDtypeStruct((2, N_TOK, D_MODEL), BF16)
    blk = pl.BlockSpec((HG_IN_TM, D_MODEL), tile)
    split_blk = pl.BlockSpec((2, HG_IN_TM, D_MODEL), lambda i: (0, i, 0))
    prompt_steps = N_PROMPT // HG_IN_TM
    return pl.pallas_call(
        _hg_in_kernel,
        out_shape=(bf, bf, bf, ff, bf, ff, bf),
        grid=(N_TOK // HG_IN_TM,),
        in_specs=[
            pl.BlockSpec((HG_IN_TM, D_MODEL), lambda i: (jnp.minimum(i, prompt_steps - 1), 0)),
            pl.BlockSpec((HG_IN_TM, D_MODEL), lambda i: (jnp.maximum(i - prompt_steps, 0), 0)),
            pl.BlockSpec(mod.shape, fixed2),
            pl.BlockSpec((1, D_MODEL), fixed2),
            pl.BlockSpec(w_in.shape, fixed2, pipeline_mode=pl.Buffered(1)),
            pl.BlockSpec(lb_logits.shape, lambda i: (0, 0, 0)),
        ],
        out_specs=(blk, blk, blk, split_blk, blk, split_blk, blk),
        compiler_params=_params(56, "arbitrary"),
        name="hgrn_in_proj",
    )(xp, xs, mod, nw, w_in, lb_logits)


def _gla_exact(reverse, q_ref, v_ref, lf_ref, k_ref, st_ref, o_ref):
    nt = (((1,), (1,)), ((), ()))
    tn = (((0,), (0,)), ((), ()))
    n_chunks = TM // CHUNK
    row_id = lax.broadcasted_iota(jnp.int32, (CHUNK, HG_DK), 0)
    order = range(CHUNK - 1, -1, -1) if reverse else range(CHUNK)

    def chunk_head(it, carry):
        step, h = it // HG_HEADS, it % HG_HEADS
        ci = (n_chunks - 1 - step) if reverse else step
        rows = pl.ds(pl.multiple_of(ci * CHUNK, CHUNK), CHUNK)
        cols = pl.ds(pl.multiple_of(h * HG_DK, HG_DK), HG_DK)
        q, k, v = q_ref[rows, cols], k_ref[rows, cols], v_ref[rows, cols]
        f = jnp.exp(lf_ref[0, rows, cols].astype(F32) + lf_ref[1, rows, cols].astype(F32))
        st = st_ref[0, h]
        o = jnp.zeros((CHUNK, HG_DK), F32)
        for t in order:
            one = row_id == t
            v_t = jnp.where(one, v, jnp.zeros_like(v))
            st = st * f[t:t + 1, :] + lax.dot_general(v_t, k, tn, preferred_element_type=F32)
            o = jnp.where(one, lax.dot_general(q, st.astype(BF16), nt, preferred_element_type=F32), o)
        st_ref[0, h] = st
        o_ref[rows, cols] = o.astype(o_ref.dtype)
        return carry

    lax.fori_loop(0, n_chunks * HG_HEADS, chunk_head, 0)


class _GlaDirection(NamedTuple):
    reverse: bool
    q: Any
    v: Any
    lf: Any
    k: Any
    o: Any
    st: Any
    sw: Any
    ut: Any
    qd: Any
    kd: Any
    ki: Any
    cd: Any
    b: Any


_NT = (((1,), (1,)), ((), ()))
_TN = (((0,), (0,)), ((), ()))
_HG_COLS = [slice(h * HG_DK, (h + 1) * HG_DK) for h in range(HG_HEADS)]
_N_CHUNKS = TM // CHUNK
_N_PAIRS = _N_CHUNKS // 2


def _gla_prepare(d):
    r = lax.broadcasted_iota(jnp.int32, (TM, TM), 0)
    c = lax.broadcasted_iota(jnp.int32, (TM, TM), 1)
    same = (r // CHUNK) == (c // CHUNK)
    tri = (same & ((c >= r) if d.reverse else (c <= r))).astype(BF16)
    b = jnp.dot(tri, d.lf[0], preferred_element_type=F32) + jnp.dot(tri, d.lf[1], preferred_element_type=F32)
    d.b[...] = b
    edge = 0 if d.reverse else CHUNK - 1
    total = b.reshape(_N_CHUNKS, CHUNK, D_MODEL)[:, edge, :]
    d.cd[...] = jnp.exp(total)
    return jnp.max(-total) <= DECAY_CLAMP


def _gla_pair_rows(d, step):
    pi = (_N_PAIRS - 1 - step) if d.reverse else step
    row0 = pi * 2 * CHUNK
    lo, hi = (pl.ds(row0, CHUNK), 2 * pi), (pl.ds(row0 + CHUNK, CHUNK), 2 * pi + 1)
    return pl.ds(row0, 2 * CHUNK), ((hi, lo) if d.reverse else (lo, hi))


def _chunk_decay(d, chunk):
    return d.cd[pl.ds(chunk, 1), :]


def _gla_decayed_operands(d, step):
    both, ((_, c_first), (_, c_second)) = _gla_pair_rows(d, step)
    b = d.b[both, :]
    ki = d.k[both, :] * jnp.exp(jnp.minimum(-b, DECAY_CLAMP)).astype(BF16)
    d.qd[both, :] = d.q[both, :] * jnp.exp(b).astype(BF16)
    d.ki[both, :] = ki
    c_lo, c_hi = (c_second, c_first) if d.reverse else (c_first, c_second)
    ends = jnp.concatenate([jnp.broadcast_to(_chunk_decay(d, c).astype(BF16), (CHUNK, D_MODEL)) for c in (c_lo, c_hi)],
                           axis=0)
    d.kd[both, :] = ki * ends


def _rows_scaled(x, scale_row, second_half):
    scale = jnp.broadcast_to(scale_row.astype(BF16), (CHUNK, x.shape[1]))
    ones = jnp.ones((CHUNK, x.shape[1]), BF16)
    return x * jnp.concatenate([ones, scale] if second_half else [scale, ones], axis=0)


def _gla_key_value_product(d, step):
    _gla_decayed_operands(d, step)
    both, (_, (_, c_second)) = _gla_pair_rows(d, step)
    decay_second = _chunk_decay(d, c_second)
    for h, cols in enumerate(_HG_COLS):
        keys = _rows_scaled(d.kd[both, cols], decay_second[:, cols], second_half=d.reverse)
        d.ut[h] = lax.dot_general(d.v[both, cols], keys, _TN, preferred_element_type=F32)


def _gla_start(d):
    for h in range(HG_HEADS):
        d.sw[0, h] = d.st[0, h].T.astype(BF16)
    _gla_key_value_product(d, 0)


def _gla_pair(d, step, src, dst):
    sr = lax.broadcasted_iota(jnp.int32, (CHUNK, CHUNK), 0)
    sc = lax.broadcasted_iota(jnp.int32, (CHUNK, CHUNK), 1)
    tr = lax.broadcasted_iota(jnp.int32, (CHUNK, 2 * CHUNK), 0)
    tc = lax.broadcasted_iota(jnp.int32, (CHUNK, 2 * CHUNK), 1)
    if d.reverse:
        keep_first = sc >= sr
        keep_second = (tc >= tr) | (tc >= CHUNK)
    else:
        keep_first = sc <= sr
        keep_second = (tc < CHUNK) | (tc - CHUNK <= tr)
    both, ((first, c_first), (second, c_second)) = _gla_pair_rows(d, step)
    decay_first = _chunk_decay(d, c_first)
    decay_both = decay_first * _chunk_decay(d, c_second)
    for h, cols in enumerate(_HG_COLS):
        new = d.st[src, h] * decay_both[:, cols] + d.ut[h]
        d.st[dst, h] = new
        d.sw[dst, h] = new.T.astype(BF16)
    a_first = [lax.dot_general(d.qd[first, cols], d.ki[first, cols], _NT, preferred_element_type=F32)
               for cols in _HG_COLS]
    a_second = []
    for cols in _HG_COLS:
        lo_keys = (d.ki if d.reverse else d.kd)[pl.ds(both.start, CHUNK), cols]
        hi_keys = (d.kd if d.reverse else d.ki)[pl.ds(both.start + CHUNK, CHUNK), cols]
        keys = jnp.concatenate([lo_keys, hi_keys], axis=0)
        a_second.append(lax.dot_general(d.qd[second, cols], keys, _NT, preferred_element_type=F32))
    inter = [jnp.dot(_rows_scaled(d.qd[both, cols], decay_first[:, cols], second_half=not d.reverse),
                     d.sw[src, h], preferred_element_type=F32) for h, cols in enumerate(_HG_COLS)]
    if step + 1 < _N_PAIRS:
        _gla_key_value_product(d, step + 1)
    first_half, second_half = (slice(CHUNK, None), slice(None, CHUNK)) if d.reverse else \
                              (slice(None, CHUNK), slice(CHUNK, None))
    for h, cols in enumerate(_HG_COLS):
        am = jnp.where(keep_first, a_first[h], 0.0).astype(BF16)
        d.o[first, cols] = (jnp.dot(am, d.v[first, cols], preferred_element_type=F32)
                            + inter[h][first_half]).astype(d.o.dtype)
        am = jnp.where(keep_second, a_second[h], 0.0).astype(BF16)
        d.o[second, cols] = (jnp.dot(am, d.v[both, cols], preferred_element_type=F32)
                             + inter[h][second_half]).astype(d.o.dtype)


def _gla_kernel(qf_ref, vf_ref, lff_ref, kf_ref, qb_ref, vb_ref, lfb_ref, kb_ref, s0_ref,
                of_ref, ob_ref, sout_ref, st_scr, sw_scr, ut_scr, qd_scr, kd_scr, ki_scr, cd_scr, b_scr):
    i = pl.program_id(0)
    is_prompt = i < PROMPT_TILES
    first = jnp.logical_or(is_prompt, (i - PROMPT_TILES) % TILES_PER_DEC_SEQ == 0)
    work = [tuple(scr.at[n] for scr in (st_scr, sw_scr, ut_scr, qd_scr, kd_scr, ki_scr, cd_scr, b_scr))
            for n in range(2)]
    fwd = _GlaDirection(False, qf_ref, vf_ref, lff_ref, kf_ref, of_ref, *work[0])
    bwd = _GlaDirection(True, qb_ref, vb_ref, lfb_ref, kb_ref, ob_ref, *work[1])

    @pl.when(is_prompt)
    def _():
        for d in (fwd, bwd):
            d.st[0] = jnp.zeros((HG_HEADS, HG_DK, HG_DK), F32)

    @pl.when(jnp.logical_and(first, jnp.logical_not(is_prompt)))
    def _():
        for n, d in enumerate((fwd, bwd)):
            for h in range(HG_HEADS):
                d.st[0, h] = s0_ref[0, 0, n, h].T

    exact_scores = jnp.logical_and(_gla_prepare(fwd), _gla_prepare(bwd))

    @pl.when(exact_scores)
    def _():
        _gla_start(fwd)
        _gla_start(bwd)

        for step in range(_N_PAIRS):
            for d in (fwd, bwd):
                _gla_pair(d, step, step % 2, 1 - step % 2)

    @pl.when(jnp.logical_not(exact_scores))
    def _():
        for d in (fwd, bwd):
            _gla_exact(d.reverse, d.q, d.v, d.lf, d.k, d.st, d.o)

    @pl.when(is_prompt)
    def _():
        for n, d in enumerate((fwd, bwd)):
            for h in range(HG_HEADS):
                sout_ref[0, 0, n, h] = d.st[0, h].T


def _gla(q, v, lff, kf, lfb, kb, state_hgrn):
    def fwd_tile(i):
        return (i, 0)

    def bwd_tile(i):
        j = (i - PROMPT_TILES) % TILES_PER_DEC_SEQ
        return (jnp.where(i < PROMPT_TILES, i, i - j + (TILES_PER_DEC_SEQ - 1 - j)), 0)

    def s0_idx(i):
        return (jnp.maximum(i - PROMPT_TILES, 0) // TILES_PER_DEC_SEQ, 0, 0, 0, 0, 0)

    def sout_idx(i):
        return (jnp.minimum(i, PROMPT_TILES - 1), 0, 0, 0, 0, 0)

    f_blk = pl.BlockSpec((TM, D_MODEL), fwd_tile)
    b_blk = pl.BlockSpec((TM, D_MODEL), bwd_tile)
    f_split = pl.BlockSpec((2, TM, D_MODEL), lambda i: (0,) + fwd_tile(i))
    b_split = pl.BlockSpec((2, TM, D_MODEL), lambda i: (0,) + bwd_tile(i))
    st_blk = (1, 1, 2, HG_HEADS, HG_DK, HG_DK)
    return pl.pallas_call(
        _gla_kernel,
        out_shape=(
            jax.ShapeDtypeStruct((N_TOK, D_MODEL), BF16),
            jax.ShapeDtypeStruct((N_TOK, D_MODEL), BF16),
            jax.ShapeDtypeStruct((BATCH,) + st_blk[1:], F32),
        ),
        grid=(N_TILES,),
        in_specs=[f_blk, f_blk, f_split, f_blk, b_blk, b_blk, b_split, b_blk, pl.BlockSpec(st_blk, s0_idx)],
        out_specs=(f_blk, b_blk, pl.BlockSpec(st_blk, sout_idx)),
        scratch_shapes=[
            pltpu.VMEM((2, 2, HG_HEADS, HG_DK, HG_DK), F32),
            pltpu.VMEM((2, 2, HG_HEADS, HG_DK, HG_DK), BF16),
            pltpu.VMEM((2, HG_HEADS, HG_DK, HG_DK), F32),
            pltpu.VMEM((2, TM, D_MODEL), BF16),
            pltpu.VMEM((2, TM, D_MODEL), BF16),
            pltpu.VMEM((2, TM, D_MODEL), BF16),
            pltpu.VMEM((2, TM // CHUNK, D_MODEL), F32),
            pltpu.VMEM((2, TM, D_MODEL), F32),
        ],
        compiler_params=_params(48, "arbitrary"),
        name="hgrn_recurrence",
    )(q, v, lff, kf, q, v, lfb, kb, state_hgrn)


def _route(logits):
    lane = lax.broadcasted_iota(jnp.int32, logits.shape, 1)
    neg = jnp.float32(-jnp.inf)
    big = jnp.int32(ROUTER_LANES)

    def first_max(x):
        m = jnp.max(x, axis=-1, keepdims=True)
        return m, jnp.min(jnp.where(x == m, lane, big), axis=-1, keepdims=True)

    is_group = (lane >= N_EXPERTS) & (lane < N_EXPERTS + N_GROUPS)
    gl = jnp.where(is_group, logits, neg)
    gmax, g_lane = first_max(gl)
    g_sel = g_lane - N_EXPERTS
    gsum = jnp.sum(jnp.exp(gl - gmax), axis=-1, keepdims=True)
    p_g = 1.0 / gsum
    in_sel = (lane < N_EXPERTS) & ((lane // EXPERTS_PER_GROUP) == g_sel)
    m1, i1 = first_max(jnp.where(in_sel, logits, neg))
    m2, i2 = first_max(jnp.where(in_sel & (lane != i1), logits, neg))
    e2 = jnp.exp(m2 - m1)
    return i1, i2, p_g / (1.0 + e2), p_g * e2 / (1.0 + e2)


def _pack_rows(x):
    q = PACK_WORDS
    bits = pltpu.bitcast(x.astype(BF16).astype(F32), jnp.uint32)
    return [(bits[:, (2 + h) * q:(3 + h) * q] & jnp.uint32(0xFFFF0000)) | (bits[:, h * q:(h + 1) * q] >> 16)
            for h in range(2)]


def _unpack_rows(half0, half1):
    lo = lambda w: pltpu.bitcast(w << 16, F32).astype(BF16)
    hi = lambda w: pltpu.bitcast(w & jnp.uint32(0xFFFF0000), F32).astype(BF16)
    return [lo(half0), lo(half1), hi(half0), hi(half1)]


def _mixer_tail(mix_bf16, x, mod_ref, nw2_ref, wo_ref, wr_ref, br_ref, x1_ref, h2p_ref, lg_ref):
    cond = _tile_cond(pl.program_id(0))
    out = jnp.dot(mix_bf16, wo_ref[...], preferred_element_type=F32)
    x1 = x + _mod_row(mod_ref, cond, 2) * out
    x1_ref[...] = x1
    h2 = _norm_mod(x1, nw2_ref[...], _mod_row(mod_ref, cond, 3), _mod_row(mod_ref, cond, 4))
    h2p_ref[...] = jnp.concatenate(_pack_rows(h2), axis=1)
    h_hi = h2.astype(BF16)
    h_lo = (h2 - h_hi.astype(F32)).astype(BF16)
    lg_ref[...] = (jnp.dot(h_hi, wr_ref[0], preferred_element_type=F32)
                   + (jnp.dot(h_hi, wr_ref[1], preferred_element_type=F32)
                      + jnp.dot(h_lo, wr_ref[0], preferred_element_type=F32))) + br_ref[...]


def _route_kernel(lg_ref, ri_ref, rw_ref, cnt_ref, carry_scr):
    @pl.when(pl.program_id(0) == 0)
    def _():
        carry_scr[...] = jnp.zeros_like(carry_scr)

    logits = lg_ref[...]
    i1, i2, w1, w2 = _route(logits)
    lane = lax.broadcasted_iota(jnp.int32, logits.shape, 1)
    chosen = ((lane == i1) | (lane == i2)).astype(BF16)
    r = lax.broadcasted_iota(jnp.int32, (ROUTE_TM, ROUTE_TM), 0)
    c = lax.broadcasted_iota(jnp.int32, (ROUTE_TM, ROUTE_TM), 1)
    before = jnp.dot((c < r).astype(BF16), chosen, preferred_element_type=F32) + carry_scr[...]
    r1 = jnp.sum(jnp.where(lane == i1, before, 0.0), axis=-1, keepdims=True).astype(jnp.int32)
    r2 = jnp.sum(jnp.where(lane == i2, before, 0.0), axis=-1, keepdims=True).astype(jnp.int32)
    total = carry_scr[...] + jnp.sum(chosen.astype(F32), axis=0, keepdims=True)
    carry_scr[...] = total
    cnt_ref[...] = total
    ri_ref[...] = jnp.where(lane == 0, i1, jnp.where(lane == 1, i2, jnp.where(lane == 2, r1, r2)))
    rw_ref[...] = jnp.where(lane == 0, w1, w2)


def _route_tokens(logits):
    blk = pl.BlockSpec((ROUTE_TM, ROUTER_LANES), lambda i: (i, 0))
    return pl.pallas_call(
        _route_kernel,
        out_shape=(
            jax.ShapeDtypeStruct((N_TOK, ROUTER_LANES), jnp.int32),
            jax.ShapeDtypeStruct((N_TOK, ROUTER_LANES), F32),
            jax.ShapeDtypeStruct((1, ROUTER_LANES), F32),
        ),
        grid=(N_TOK // ROUTE_TM,),
        in_specs=[blk],
        out_specs=(blk, blk, pl.BlockSpec((1, ROUTER_LANES), lambda i: (0, 0))),
        scratch_shapes=[pltpu.VMEM((1, ROUTER_LANES), F32)],
        compiler_params=_params(32, "arbitrary"),
        name="moe_route",
    )(logits)


def _hg_out_kernel(of_ref, ob_ref, g_ref, on_ref, xp_ref, xs_ref, *rest):
    o = of_ref[...].astype(F32) + ob_ref[...].astype(F32)
    parts = []
    for h in range(HG_HEADS):
        oh = o[:, h * HG_DK:(h + 1) * HG_DK]
        parts.append(oh * lax.rsqrt(jnp.mean(oh * oh, axis=-1, keepdims=True) + EPS) * on_ref[...])
    y = jnp.concatenate(parts, axis=1) * _silu(g_ref[...].astype(F32))
    _mixer_tail(y.astype(BF16), _stream_tile(xp_ref, xs_ref), *rest)


def _at_out_kernel(ac_ref, al_ref, x_ref, *rest):
    _mixer_tail(_stream_tile(ac_ref, al_ref), x_ref[...], *rest)


def _mixer_out(kernel_fn, name, mix_inputs, mix_specs, mod, nw2, w_out, w_router, b_router):
    tile = lambda i: (i, 0)
    fixed2 = lambda i: (0, 0)
    blk = pl.BlockSpec((TM, D_MODEL), tile)
    lanes_blk = pl.BlockSpec((TM, ROUTER_LANES), tile)
    return pl.pallas_call(
        kernel_fn,
        out_shape=(
            jax.ShapeDtypeStruct((N_TOK, D_MODEL), F32),
            jax.ShapeDtypeStruct((N_TOK, ROW_WORDS), jnp.uint32),
            jax.ShapeDtypeStruct((N_TOK, ROUTER_LANES), F32),
        ),
        grid=(N_TILES,),
        in_specs=list(mix_specs) + [
            pl.BlockSpec(mod.shape, fixed2),
            pl.BlockSpec((1, D_MODEL), fixed2),
            pl.BlockSpec(w_out.shape, fixed2),
            pl.BlockSpec(w_router.shape, lambda i: (0, 0, 0)),
            pl.BlockSpec((1, ROUTER_LANES), fixed2),
        ],
        out_specs=(blk, pl.BlockSpec((TM, ROW_WORDS), tile), lanes_blk),
        compiler_params=_params(40, "arbitrary"),
        name=name,
    )(*mix_inputs, mod, nw2, w_out, w_router, b_router)


def _moe_plan(route_i, counts):
    cnt = counts[0, :N_EXPERTS].astype(jnp.int32)
    padded = ((cnt + MOE_TILE - 1) // MOE_TILE) * MOE_TILE
    ends = jnp.cumsum(padded)
    offs = ends - padded
    experts = route_i[:, 0:2]
    pos = jnp.sum(jnp.where(experts[:, :, None] == jnp.arange(N_EXPERTS)[None, None, :], offs[None, None, :], 0),
                  axis=-1) + route_i[:, 2:4]
    tile_start = jnp.arange(MOE_ROWS // MOE_TILE, dtype=jnp.int32) * MOE_TILE
    tile_expert = jnp.minimum(jnp.sum(ends[None, :] <= tile_start[:, None], axis=1), N_EXPERTS - 1).astype(jnp.int32)
    of_tile = tile_expert[:, None] == jnp.arange(N_EXPERTS)[None, :]
    tile_offs = jnp.sum(jnp.where(of_tile, offs[None, :], 0), axis=1)
    tile_cnt = jnp.sum(jnp.where(of_tile, cnt[None, :], 0), axis=1)
    tile_rows = jnp.clip(tile_offs + tile_cnt - tile_start, 0, MOE_TILE).astype(jnp.int32)
    tile_first = (tile_start == tile_offs).astype(jnp.int32)
    n_active = (ends[-1] // MOE_TILE).astype(jnp.int32).reshape(1)
    ids = jnp.arange(N_EXPERTS)
    used = cnt > 0
    slot = (jnp.cumsum(used.astype(jnp.int32)) - 1) % 2
    later = jnp.min(jnp.where(used[None, :] & (ids[None, :] > ids[:, None]), ids[None, :], N_EXPERTS), axis=1)
    later = jnp.where(later == N_EXPERTS, -1, later)
    tile_slot = jnp.sum(jnp.where(of_tile, slot[None, :], 0), axis=1).astype(jnp.int32)
    tile_next = jnp.sum(jnp.where(of_tile, later[None, :], 0), axis=1).astype(jnp.int32)
    return pos.astype(jnp.int32), (tile_expert, tile_rows, tile_first, n_active, tile_slot, tile_next)


SC_WINDOW = 64


def _sc_mesh():
    return plsc.VectorSubcoreMesh(core_axis_name="c", subcore_axis_name="s")


def _sc_scatter_rows(x, idx_a, idx_b, n_out_rows):
    n = x.shape[0]

    @functools.partial(pl.kernel, out_type=jax.ShapeDtypeStruct((n_out_rows, ROW_WORDS), x.dtype), mesh=_sc_mesh(),
                       scratch_types=[pltpu.SemaphoreType.DMA, pltpu.SemaphoreType.DMA])
    def scatter(x_hbm, ia_hbm, ib_hbm, o_hbm, sem_a, sem_b):
        def body(x_vmem, ia_vmem, ib_vmem):
            copy_a = pltpu.async_copy(x_vmem, o_hbm.at[ia_vmem.at[0]], sem_a)
            copy_b = pltpu.async_copy(x_vmem, o_hbm.at[ib_vmem.at[0]], sem_b)
            copy_a.wait()
            copy_b.wait()

        idx_spec = pl.BlockSpec((1, SC_WINDOW), index_map=lambda i: (i, 0))
        pltpu.emit_pipeline(
            body, grid=(n // SC_WINDOW,),
            in_specs=[pl.BlockSpec((SC_WINDOW, ROW_WORDS), index_map=lambda i: (i, 0)), idx_spec, idx_spec],
            out_specs=[],
            core_axis_name=("c", "s"), dimension_semantics=(pltpu.PARALLEL,),
        )(x_hbm, ia_hbm, ib_hbm)

    return scatter(x, idx_a.reshape(n // SC_WINDOW, SC_WINDOW), idx_b.reshape(n // SC_WINDOW, SC_WINDOW))


def _sc_gather_rows(table, idx):
    n = idx.shape[0]

    @functools.partial(pl.kernel, out_type=jax.ShapeDtypeStruct((n, ROW_WORDS), table.dtype), mesh=_sc_mesh())
    def gather(t_hbm, i_hbm, o_hbm):
        def body(i_vmem, o_vmem):
            pltpu.sync_copy(t_hbm.at[i_vmem.at[0]], o_vmem)

        pltpu.emit_pipeline(
            body, grid=(n // SC_WINDOW,),
            in_specs=[pl.BlockSpec((1, SC_WINDOW), index_map=lambda i: (i, 0))],
            out_specs=[pl.BlockSpec((SC_WINDOW, ROW_WORDS), index_map=lambda i: (i, 0))],
            core_axis_name=("c", "s"), dimension_semantics=(pltpu.PARALLEL,),
        )(i_hbm, o_hbm)

    return gather(table, idx.reshape(n // SC_WINDOW, SC_WINDOW))


def _ffn_kernel(te_ref, tr_ref, tf_ref, na_ref, ts_ref, tn_ref, xs_ref, w1_hbm, w3_hbm, w2_hbm, ys_ref,
                w1_scr, w3_scr, w2_scr, w1_buf, w3_buf, w2_buf, sem, *, layer):
    def weight_copies(expert, slot):
        pairs = ((w1_hbm, w1_buf), (w3_hbm, w3_buf), (w2_hbm, w2_buf))
        return [pltpu.make_async_copy(hbm.at[layer, expert], buf.at[slot], sem.at[slot, j])
                for j, (hbm, buf) in enumerate(pairs)]

    def tile(t, rows):
        @pl.when(tf_ref[t] == 1)
        def _():
            slot = ts_ref[t]

            @pl.when(t == 0)
            def _():
                for copy in weight_copies(te_ref[t], slot):
                    copy.start()

            for copy in weight_copies(te_ref[t], slot):
                copy.wait()

            @pl.when(tn_ref[t] >= 0)
            def _():
                for copy in weight_copies(tn_ref[t], 1 - slot):
                    copy.start()

            w1_scr[...] = w1_buf[slot].astype(BF16)
            w3_scr[...] = w3_buf[slot].astype(BF16)
            w2_scr[...] = w2_buf[slot].astype(BF16)

        row = lax.broadcasted_iota(jnp.int32, (MOE_TILE, PACK_WORDS), 0)
        live = row < tr_ref[t]
        halves = [jnp.where(live, xs_ref[rows, h * PACK_WORDS:(h + 1) * PACK_WORDS],
                            jnp.zeros((MOE_TILE, PACK_WORDS), jnp.uint32)) for h in range(2)]
        chunks = _unpack_rows(*halves)

        def up(w_scr):
            acc = jnp.dot(chunks[0], w_scr[:PACK_WORDS, :], preferred_element_type=F32)
            for k in range(1, len(chunks)):
                acc = acc + jnp.dot(chunks[k], w_scr[k * PACK_WORDS:(k + 1) * PACK_WORDS, :],
                                    preferred_element_type=F32)
            return acc

        hid = (_silu(up(w1_scr)) * up(w3_scr)).astype(BF16)
        ys_ref[rows, :] = jnp.concatenate(_pack_rows(jnp.dot(hid, w2_scr[...], preferred_element_type=F32)), axis=1)

    for j in range(MOE_STEP_TILES):
        t = pl.program_id(0) * MOE_STEP_TILES + j
        pl.when(t < na_ref[0])(functools.partial(tile, t, slice(j * MOE_TILE, (j + 1) * MOE_TILE)))


def _ffn(xs, tables, layer, w1, w3, w2):
    step_rows = MOE_STEP_TILES * MOE_TILE
    row_tile = lambda s, te, tr, tf, na, ts, tn: (jnp.minimum(s, (na[0] - 1) // MOE_STEP_TILES), 0)
    hbm = pl.BlockSpec(memory_space=pl.ANY)
    return pl.pallas_call(
        functools.partial(_ffn_kernel, layer=layer),
        out_shape=jax.ShapeDtypeStruct((MOE_ROWS, ROW_WORDS), jnp.uint32),
        grid_spec=pltpu.PrefetchScalarGridSpec(
            num_scalar_prefetch=len(tables),
            grid=(MOE_ROWS // step_rows,),
            in_specs=[pl.BlockSpec((step_rows, ROW_WORDS), row_tile), hbm, hbm, hbm],
            out_specs=pl.BlockSpec((step_rows, ROW_WORDS), row_tile),
            scratch_shapes=[
                pltpu.VMEM((D_MODEL, D_EXPERT), BF16),
                pltpu.VMEM((D_MODEL, D_EXPERT), BF16),
                pltpu.VMEM((D_EXPERT, D_MODEL), BF16),
                pltpu.VMEM((2, D_MODEL, D_EXPERT), F32),
                pltpu.VMEM((2, D_MODEL, D_EXPERT), F32),
                pltpu.VMEM((2, D_EXPERT, D_MODEL), F32),
                pltpu.SemaphoreType.DMA((2, 3)),
            ],
        ),
        compiler_params=_params(32, "arbitrary"),
        name="moe_experts",
    )(*tables, xs, w1, w3, w2)


def _moe_res(x_ref, y_ref, rw_ref, mod_ref):
    cond = _tile_cond(pl.program_id(0))
    rw = rw_ref[...]
    wa = rw[:, 0:1]
    wb = rw[:, 1:2]
    ya = _unpack_rows(y_ref[0, :, :PACK_WORDS], y_ref[0, :, PACK_WORDS:])
    yb = _unpack_rows(y_ref[1, :, :PACK_WORDS], y_ref[1, :, PACK_WORDS:])
    y = jnp.concatenate([wa * a.astype(F32) + wb * b.astype(F32) for a, b in zip(ya, yb)], axis=1)
    return x_ref[...] + _mod_row(mod_ref, cond, 5) * y


def _moe_res_final_kernel(x_ref, y_ref, rw_ref, mod_ref, fn_ref, op_ref, os_ref):
    x = _moe_res(x_ref, y_ref, rw_ref, mod_ref)
    y = x * lax.rsqrt(jnp.mean(x * x, axis=-1, keepdims=True) + EPS) * fn_ref[...]
    is_prompt = pl.program_id(0) < PROMPT_TILES

    @pl.when(is_prompt)
    def _():
        op_ref[...] = y

    @pl.when(jnp.logical_not(is_prompt))
    def _():
        os_ref[...] = y


def _moe_combine(x1, y_pairs, route_w, mod, final_norm):
    tile = lambda i: (i, 0)
    fixed2 = lambda i: (0, 0)
    return pl.pallas_call(
        _moe_res_final_kernel,
        out_shape=(jax.ShapeDtypeStruct((N_PROMPT, D_MODEL), F32),
                   jax.ShapeDtypeStruct((N_TOK - N_PROMPT, D_MODEL), F32)),
        grid=(N_TILES,),
        in_specs=[pl.BlockSpec((TM, D_MODEL), tile),
                  pl.BlockSpec((2, TM, ROW_WORDS), lambda i: (0, i, 0)),
                  pl.BlockSpec((TM, ROUTER_LANES), tile),
                  pl.BlockSpec(mod.shape, fixed2),
                  pl.BlockSpec((1, D_MODEL), fixed2)],
        out_specs=(pl.BlockSpec((TM, D_MODEL), _prompt_tile), pl.BlockSpec((TM, D_MODEL), _sample_tile)),
        compiler_params=_params(32, "arbitrary"),
        name="moe_combine",
    )(x1, y_pairs, route_w, mod, final_norm)


def _moe_layer(x1, h2p, logits, layer, w1, w3, w2):
    route_i, route_w, counts = _route_tokens(logits)
    pos, tables = _moe_plan(route_i, counts)
    xs = _sc_scatter_rows(h2p, pos[:, 0], pos[:, 1], MOE_ROWS)
    ys = _ffn(xs, tables, layer, w1, w3, w2)
    y_pairs = _sc_gather_rows(ys, jnp.concatenate([pos[:, 0], pos[:, 1]]))
    return x1, y_pairs.reshape(2, N_TOK, ROW_WORDS), route_w


def _swap_rotary_halves(x):
    n = x.shape[-1]
    lane = lax.broadcasted_iota(jnp.int32, x.shape, 1)
    quarter = ROPE_HALF // 2
    return jnp.where((lane % ROPE_HALF) < quarter, pltpu.roll(x, n - quarter, 1), pltpu.roll(x, quarter, 1))


def _at_in_kernel(x1_ref, y_ref, rw_ref, mod_prev_ref, mod_ref, nw_ref, w_ref, cos_ref, sin_ref,
                  x_ref, q_ref, kt_ref, v_ref, kc_ref, vc_ref):
    i = pl.program_id(0)
    cond = _tile_cond(i)
    x = _moe_res(x1_ref, y_ref, rw_ref, mod_prev_ref)
    x_ref[...] = x
    h = _norm_mod(x, nw_ref[...], _mod_row(mod_ref, cond, 0), _mod_row(mod_ref, cond, 1)).astype(BF16)
    nq = ATT_Q_HEADS * ATT_HEAD_DIM
    nk = ATT_KV_HEADS * ATT_HEAD_DIM
    cos = cos_ref[...]
    sin = sin_ref[...]

    def rope(x):
        reps = x.shape[-1] // LANES
        return x * jnp.concatenate([cos] * reps, axis=1) + _swap_rotary_halves(x) * jnp.concatenate([sin] * reps, axis=1)

    q_ref[...] = (rope(jnp.dot(h, w_ref[:, :nq], preferred_element_type=F32)) * ATT_HEAD_DIM ** -0.5).astype(BF16)
    k = rope(jnp.dot(h, w_ref[:, nq:nq + nk], preferred_element_type=F32))
    v = jnp.dot(h, w_ref[:, nq + nk:], preferred_element_type=F32)
    kt = k.T
    kt_ref[...] = kt.astype(BF16)
    v_ref[...] = v.astype(BF16)

    @pl.when(i < PROMPT_TILES)
    def _():
        kc_ref[...] = kt
        vc_ref[...] = v.T


def _rope_tables():
    f32 = np.float32
    pos = np.arange(DEC_SEQ)
    t_row = (pos // GRID_W).astype(f32)
    t_col = (pos % GRID_W).astype(f32)
    inv = f32(ROPE_BASE) ** (-np.arange(0, ROPE_HALF, 2, dtype=f32) / f32(ROPE_HALF))
    j = np.arange(LANES) % ATT_HEAD_DIM
    freq = inv[(j % ROPE_HALF) % (ROPE_HALF // 2)]
    ang = (np.where((j < ROPE_HALF)[None, :], t_row[:, None], t_col[:, None]) * freq[None, :]).astype(f32)
    sign = np.where((j % ROPE_HALF) < ROPE_HALF // 2, -1.0, 1.0).astype(f32)
    cos = np.concatenate([np.ones((TM, LANES), f32), np.cos(ang)], axis=0)
    sin = np.concatenate([np.zeros((TM, LANES), f32), np.sin(ang) * sign[None, :]], axis=0)
    return jnp.asarray(cos, F32), jnp.asarray(sin, F32)


def _at_in(x1, y_pairs, route_w, mod_prev, mod, nw, w_in, cos, sin):
    tile = lambda i: (i, 0)
    fixed2 = lambda i: (0, 0)
    rope_tile = lambda i: (jnp.where(i < PROMPT_TILES, 0, 1 + (i - PROMPT_TILES) % TILES_PER_DEC_SEQ), 0)
    nk = ATT_KV_HEADS * ATT_HEAD_DIM
    return pl.pallas_call(
        _at_in_kernel,
        out_shape=(
            jax.ShapeDtypeStruct((N_TOK, D_MODEL), F32),
            jax.ShapeDtypeStruct((N_TOK, D_MODEL), BF16),
            jax.ShapeDtypeStruct((nk, N_TOK), BF16),
            jax.ShapeDtypeStruct((N_TOK, nk), BF16),
            jax.ShapeDtypeStruct((N_PROMPT, nk), F32),
            jax.ShapeDtypeStruct((N_PROMPT, nk), F32),
        ),
        grid=(N_TILES,),
        in_specs=[
            pl.BlockSpec((TM, D_MODEL), tile),
            pl.BlockSpec((2, TM, ROW_WORDS), lambda i: (0, i, 0)),
            pl.BlockSpec((TM, ROUTER_LANES), tile),
            pl.BlockSpec(mod_prev.shape, fixed2),
            pl.BlockSpec(mod.shape, fixed2),
            pl.BlockSpec((1, D_MODEL), fixed2),
            pl.BlockSpec(w_in.shape, fixed2),
            pl.BlockSpec((TM, LANES), rope_tile),
            pl.BlockSpec((TM, LANES), rope_tile),
        ],
        out_specs=(pl.BlockSpec((TM, D_MODEL), tile), pl.BlockSpec((TM, D_MODEL), tile),
                   pl.BlockSpec((nk, TM), lambda i: (0, i)),
                   pl.BlockSpec((TM, nk), tile),
                   pl.BlockSpec((TM, nk), _prompt_tile), pl.BlockSpec((TM, nk), _prompt_tile)),
        compiler_params=_params(40, "arbitrary"),
        name="attn_in_proj",
    )(x1, y_pairs, route_w, mod_prev, mod, nw, w_in, cos, sin)


def _attend(q, kt_all, v_all, mask, sink_ref, o_ref):
    nq = q.shape[0]
    group_lanes = ATT_GROUP * ATT_HEAD_DIM
    lane = lax.broadcasted_iota(jnp.int32, (nq, group_lanes), 1)
    mine = [(lane // ATT_HEAD_DIM) == g for g in range(ATT_GROUP)]
    row_head = lax.broadcasted_iota(jnp.int32, (ATT_GROUP * nq, 1), 0) // nq
    if mask is not None:
        mask = jnp.concatenate([mask] * ATT_GROUP, axis=0)
    for hk in range(ATT_KV_HEADS):
        vh = v_all[:, hk * ATT_HEAD_DIM:(hk + 1) * ATT_HEAD_DIM]
        kt = jnp.concatenate([kt_all[hk * ATT_HEAD_DIM:(hk + 1) * ATT_HEAD_DIM, :]] * ATT_GROUP, axis=0)
        vt = jnp.concatenate([vh] * ATT_GROUP, axis=1)
        qg = q[:, hk * group_lanes:(hk + 1) * group_lanes]
        q_stack = jnp.concatenate([jnp.where(mine[g], qg, jnp.zeros_like(qg)) for g in range(ATT_GROUP)], axis=0)
        s = jnp.dot(q_stack, kt, preferred_element_type=F32)
        if mask is not None:
            s = jnp.where(mask, s, -jnp.inf)
        sink = jnp.zeros((ATT_GROUP * nq, 1), F32)
        for g in range(ATT_GROUP):
            sink = jnp.where(row_head == g, sink_ref[hk * ATT_GROUP + g], sink)
        m = jnp.maximum(jnp.max(s, axis=-1, keepdims=True), sink)
        p = jnp.exp(s - m)
        denom = jnp.sum(p, axis=-1, keepdims=True) + jnp.exp(sink - m)
        o = jnp.dot(p.astype(BF16), vt, preferred_element_type=F32) / denom
        acc = jnp.where(mine[0], o[:nq], 0.0)
        for g in range(1, ATT_GROUP):
            acc = acc + jnp.where(mine[g], o[g * nq:(g + 1) * nq], 0.0)
        o_ref[:, hk * group_lanes:(hk + 1) * group_lanes] = acc.astype(BF16)


def _ctx_attn_kernel(sink_ref, q_ref, k_ref, v_ref, o_ref):
    _attend(q_ref[...], k_ref[...], v_ref[...], None, sink_ref, o_ref)


def _lat_attn_kernel(sink_ref, q_ref, kp_ref, kc_ref, kn_ref, vp_ref, vc_ref, vn_ref, ck_ref, cv_ref, o_ref):
    jb = pl.program_id(1)
    kt_all = jnp.concatenate([kp_ref[...], kc_ref[...], kn_ref[...], ck_ref[0].astype(BF16)], axis=1)
    v_all = jnp.concatenate([vp_ref[...], vc_ref[...], vn_ref[...], cv_ref[0].astype(BF16)], axis=0)
    nkeys = 3 * BLOCK + PAST_LEN
    qi = lax.broadcasted_iota(jnp.int32, (BLOCK, nkeys), 0)
    kj = lax.broadcasted_iota(jnp.int32, (BLOCK, nkeys), 1)
    qpos = jb * BLOCK + qi
    kpos = (jb - 1) * BLOCK + kj
    local_ok = (jnp.abs(qpos - kpos) <= WINDOW) & (kpos >= 0) & (kpos < DEC_SEQ)
    mask = (kj >= 3 * BLOCK) | local_ok
    _attend(q_ref[...], kt_all, v_all, mask, sink_ref, o_ref)


def _attention(q, kt, v, cache_kt, cache_v, sink):
    nk = ATT_KV_HEADS * ATT_HEAD_DIM
    smem = pl.BlockSpec(memory_space=pltpu.SMEM)
    ctx = pl.pallas_call(
        _ctx_attn_kernel,
        out_shape=jax.ShapeDtypeStruct((N_PROMPT, D_MODEL), BF16),
        grid=(BATCH,),
        in_specs=[
            smem,
            pl.BlockSpec((SEQ, D_MODEL), lambda b: (b, 0)),
            pl.BlockSpec((nk, SEQ), lambda b: (0, b)),
            pl.BlockSpec((SEQ, nk), lambda b: (b, 0)),
        ],
        out_specs=pl.BlockSpec((SEQ, D_MODEL), lambda b: (b, 0)),
        compiler_params=_params(40, "arbitrary"),
        name="context_attention",
    )(sink, q, kt, v)

    nb = DEC_SEQ // BLOCK
    base = N_PROMPT // BLOCK
    cur = lambda b, j: (base + b * nb + j, 0)
    prev = lambda b, j: (base + b * nb + jnp.maximum(j - 1, 0), 0)
    nxt = lambda b, j: (base + b * nb + jnp.minimum(j + 1, nb - 1), 0)
    kv_blk = lambda f: pl.BlockSpec((BLOCK, nk), f)
    kt_blk = lambda f: pl.BlockSpec((nk, BLOCK), lambda b, j: f(b, j)[::-1])
    cache_blk = pl.BlockSpec((1, PAST_LEN, nk), lambda b, j: (b, 0, 0))
    cache_kt_blk = pl.BlockSpec((1, nk, PAST_LEN), lambda b, j: (b, 0, 0))
    lat = pl.pallas_call(
        _lat_attn_kernel,
        out_shape=jax.ShapeDtypeStruct((DEC_BATCH * DEC_SEQ, D_MODEL), BF16),
        grid=(DEC_BATCH, nb),
        in_specs=[
            smem,
            pl.BlockSpec((BLOCK, D_MODEL), cur),
            kt_blk(prev), kt_blk(cur), kt_blk(nxt),
            kv_blk(prev), kv_blk(cur), kv_blk(nxt),
            cache_kt_blk, cache_blk,
        ],
        out_specs=pl.BlockSpec((BLOCK, D_MODEL), lambda b, j: (b * nb + j, 0)),
        compiler_params=_params(40, "arbitrary", "arbitrary"),
        name="latent_attention",
    )(sink, q, kt, kt, kt, v, v, v, cache_kt, cache_v)
    return ctx, lat


def kernel(x_prompt, x_sample, state_hgrn, cache_k, cache_v, c, c_ctx, ada_w, ada_b, norm_w, hg_w_in,
           hg_lb_logits, hg_onorm, hg_w_out, at_w_in, at_sink, at_w_out, moe_w_group, moe_b_group,
           moe_w_expert, moe_b_expert, moe_w1, moe_w3, moe_w2, final_norm):
    xp = x_prompt.reshape(N_PROMPT, D_MODEL)
    xs = x_sample.reshape(N_TOK - N_PROMPT, D_MODEL)
    cond = jnp.concatenate([c_ctx[None, :], c], axis=0)
    mod = _ada(cond, ada_w, ada_b)
    nk = ATT_KV_HEADS * ATT_HEAD_DIM

    def router_params(i):
        pad = jnp.zeros((D_MODEL, ROUTER_LANES - N_EXPERTS - N_GROUPS), F32)
        w = jnp.concatenate([moe_w_expert[i], moe_w_group[i], pad], axis=1)
        b = jnp.concatenate([moe_b_expert[i], moe_b_group[i], pad[0]])[None, :]
        hi = w.astype(BF16)
        lo = (w - hi.astype(F32)).astype(BF16)
        return jnp.stack([hi, lo]), b

    tile = lambda i: (i, 0)
    blk = pl.BlockSpec((TM, D_MODEL), tile)

    prompt_blk = pl.BlockSpec((TM, D_MODEL), _prompt_tile)
    sample_blk = pl.BlockSpec((TM, D_MODEL), _sample_tile)
    q, v, g, lff, kf, lfb, kb = _hg_in(xp, xs, mod[0], norm_w[0, 0][None, :], hg_w_in[0].astype(BF16), hg_lb_logits)
    o_f, o_b, state_new = _gla(q, v, lff, kf, lfb, kb, state_hgrn)
    wr, br = router_params(0)
    routed = _mixer_out(
        _hg_out_kernel, "hgrn_out_route", (o_f, o_b, g, hg_onorm[0][None, :], xp, xs),
        (blk, blk, blk, pl.BlockSpec((1, HG_DK), lambda i: (0, 0)), prompt_blk, sample_blk),
        mod[0], norm_w[0, 1][None, :], hg_w_out[0].astype(BF16), wr, br)
    moe_out = _moe_layer(*routed, 0, moe_w1, moe_w3, moe_w2)

    cos, sin = _rope_tables()
    x, qa, ka, va, k_ctx, v_ctx = _at_in(*moe_out, mod[0], mod[1], norm_w[1, 0][None, :],
                                         at_w_in[0].astype(BF16), cos, sin)
    attn_ctx, attn_lat = _attention(qa, ka, va, cache_k[:, 0].reshape(DEC_BATCH, PAST_LEN, nk).transpose(0, 2, 1),
                                    cache_v[:, 0].reshape(DEC_BATCH, PAST_LEN, nk), at_sink[0])
    wr, br = router_params(1)
    routed = _mixer_out(
        _at_out_kernel, "attn_out_route", (attn_ctx, attn_lat, x), (prompt_blk, sample_blk, blk),
        mod[1], norm_w[1, 1][None, :], at_w_out[0].astype(BF16), wr, br)
    y_prompt, y_sample = _moe_combine(*_moe_layer(*routed, 1, moe_w1, moe_w3, moe_w2), mod[1], final_norm[None, :])

    def cache(feature_major):
        return feature_major.reshape(BATCH, 1, ATT_KV_HEADS, ATT_HEAD_DIM, SEQ).transpose(0, 1, 4, 2, 3)

    return (y_prompt.reshape(BATCH, SEQ, D_MODEL), y_sample.reshape(DEC_BATCH, DEC_SEQ, D_MODEL), state_new,
            cache(k_ctx), cache(v_ctx))
```

```python
import functools
from typing import Any, NamedTuple

import jax
import jax.numpy as jnp
import numpy as np
from jax import lax
from jax.experimental import pallas as pl
from jax.experimental.pallas import tpu as pltpu
from jax.experimental.pallas import tpu_sc as plsc

F32 = jnp.float32
BF16 = jnp.bfloat16

D_MODEL = 1024
BATCH = 16
SEQ = 256
DEC_BATCH = 2
DEC_SEQ = 1024
PAST_LEN = 512
GRID_W = 64
HG_HEADS = 8
HG_DK = 128
CHUNK = 16
ATT_HEAD_DIM = 64
ATT_Q_HEADS = 16
ATT_KV_HEADS = 4
ATT_GROUP = 4
WINDOW = 128
BLOCK = 128
ROPE_HALF = 32
ROPE_BASE = 10000.0
N_GROUPS = 4
EXPERTS_PER_GROUP = 8
N_EXPERTS = 32
D_EXPERT = 256
EPS = 1e-6

N_PROMPT = BATCH * SEQ
N_TOK = N_PROMPT + DEC_BATCH * DEC_SEQ
TM = 256
N_TILES = N_TOK // TM
PROMPT_TILES = N_PROMPT // TM
TILES_PER_DEC_SEQ = DEC_SEQ // TM
LANES = 128
N_COND_USED = 1 + DEC_BATCH
N_COND = 8
ROUTER_LANES = 128
ROUTE_TM = 1024
HG_IN_TM = 256
DECAY_CLAMP = 60.0
PACK_WORDS = D_MODEL // 4
ROW_WORDS = 2 * PACK_WORDS
MOE_TILE = 256
MOE_STEP_TILES = 4
MOE_WEIGHT_SLOTS = 2 * MOE_STEP_TILES
MOE_ROWS = 2 * N_TOK + N_EXPERTS * MOE_TILE
MIB = 1024 * 1024


def _params(vmem_mib, *semantics):
    return pltpu.CompilerParams(dimension_semantics=semantics, vmem_limit_bytes=vmem_mib * MIB)


def _tile_cond(i):
    return jnp.where(i < PROMPT_TILES, 0, 1 + (i - PROMPT_TILES) // TILES_PER_DEC_SEQ)


def _prompt_tile(i):
    return (jnp.minimum(i, PROMPT_TILES - 1), 0)


def _sample_tile(i):
    return (jnp.maximum(i - PROMPT_TILES, 0), 0)


def _stream_tile(prompt_ref, sample_ref):
    return jnp.where(pl.program_id(0) < PROMPT_TILES, prompt_ref[...], sample_ref[...])


def _mod_row(mod_ref, cond, which):
    return mod_ref[pl.ds(cond, 1), which * D_MODEL:(which + 1) * D_MODEL]


def _norm_mod(x, nw, shift, scale):
    y = x * lax.rsqrt(jnp.mean(x * x, axis=-1, keepdims=True) + EPS)
    return (y * nw) * (1.0 + scale) + shift


def _silu(x):
    return x * jax.nn.sigmoid(x)


def _ada_kernel(c_ref, w_ref, b_ref, o_ref):
    s = [_silu(c_ref[r]) for r in range(N_COND_USED)]
    tn = w_ref.shape[-1]
    rows = []
    for r in range(N_COND_USED):
        cols = [jnp.sum(w_ref[0, :, j * LANES:(j + 1) * LANES] * s[r], axis=0, keepdims=True)
                for j in range(tn // LANES)]
        rows.append(jnp.concatenate(cols, axis=1) + b_ref[0])
    rows.append(jnp.zeros((N_COND - N_COND_USED, tn), F32))
    o_ref[0] = jnp.concatenate(rows, axis=0)


def _ada(cond, ada_w, ada_b):
    depth, _, n = ada_w.shape
    tn = 1536
    cond_cols = jnp.broadcast_to(cond[:, :, None], (N_COND_USED, D_MODEL, LANES))
    return pl.pallas_call(
        _ada_kernel,
        out_shape=jax.ShapeDtypeStruct((depth, N_COND, n), F32),
        grid=(depth, n // tn),
        in_specs=[
            pl.BlockSpec((N_COND_USED, D_MODEL, LANES), lambda l, j: (0, 0, 0)),
            pl.BlockSpec((1, D_MODEL, tn), lambda l, j: (l, 0, j)),
            pl.BlockSpec((1, 1, tn), lambda l, j: (l, 0, j)),
        ],
        out_specs=pl.BlockSpec((1, N_COND, tn), lambda l, j: (l, 0, j)),
        compiler_params=_params(40, "arbitrary", "arbitrary"),
        name="ada_modulation",
    )(cond_cols, ada_w, ada_b.reshape(depth, 1, n))


def _hg_in_kernel(xp_ref, xs_ref, mod_ref, nw_ref, w_ref, lbl_ref,
                  q_ref, v_ref, g_ref, lff_ref, kf_ref, lfb_ref, kb_ref):
    i = pl.program_id(0) * (HG_IN_TM // TM)
    cond = _tile_cond(i)
    x = jnp.where(i < PROMPT_TILES, xp_ref[...], xs_ref[...])
    h = _norm_mod(x, nw_ref[...], _mod_row(mod_ref, cond, 0), _mod_row(mod_ref, cond, 1)).astype(BF16)

    def proj(c):
        return jnp.dot(h, w_ref[:, c * D_MODEL:(c + 1) * D_MODEL], preferred_element_type=F32)

    l0, l1, l2 = lbl_ref[0], lbl_ref[1], lbl_ref[2]
    m = jnp.maximum(jnp.maximum(l0, l1), l2)
    e0, e1, e2 = jnp.exp(l0 - m), jnp.exp(l1 - m), jnp.exp(l2 - m)
    lb = e0 / (e0 + e1 + e2)

    q_ref[...] = proj(0).astype(BF16)
    v_ref[...] = proj(1).astype(BF16)
    for d, (lf_ref, k_ref) in enumerate(((lff_ref, kf_ref), (lfb_ref, kb_ref))):
        lbd = lb[d:d + 1, :]
        f = lbd + (1.0 - lbd) * jax.nn.sigmoid(proj(2 + d))
        lf = jnp.log(f)
        hi = lf.astype(BF16)
        lf_ref[0] = hi
        lf_ref[1] = (lf - hi.astype(F32)).astype(BF16)
        k_ref[...] = (1.0 - f).astype(BF16)
    g_ref[...] = proj(4).astype(BF16)


def _hg_in(xp, xs, mod, nw, w_in, lb_logits):
    tile = lambda i: (i, 0)
    fixed2 = lambda i: (0, 0)
    bf = jax.ShapeDtypeStruct((N_TOK, D_MODEL), BF16)
    ff = jax.ShapeDtypeStruct((2, N_TOK, D_MODEL), BF16)
    blk = pl.BlockSpec((HG_IN_TM, D_MODEL), tile)
    split_blk = pl.BlockSpec((2, HG_IN_TM, D_MODEL), lambda i: (0, i, 0))
    prompt_steps = N_PROMPT // HG_IN_TM
    return pl.pallas_call(
        _hg_in_kernel,
        out_shape=(bf, bf, bf, ff, bf, ff, bf),
        grid=(N_TOK // HG_IN_TM,),
        in_specs=[
            pl.BlockSpec((HG_IN_TM, D_MODEL), lambda i: (jnp.minimum(i, prompt_steps - 1), 0)),
            pl.BlockSpec((HG_IN_TM, D_MODEL), lambda i: (jnp.maximum(i - prompt_steps, 0), 0)),
            pl.BlockSpec(mod.shape, fixed2),
            pl.BlockSpec((1, D_MODEL), fixed2),
            pl.BlockSpec(w_in.shape, fixed2, pipeline_mode=pl.Buffered(1)),
            pl.BlockSpec(lb_logits.shape, lambda i: (0, 0, 0)),
        ],
        out_specs=(blk, blk, blk, split_blk, blk, split_blk, blk),
        compiler_params=_params(56, "arbitrary"),
        name="hgrn_in_proj",
    )(xp, xs, mod, nw, w_in, lb_logits)


def _gla_exact(reverse, q_ref, v_ref, lf_ref, k_ref, st_ref, o_ref):
    nt = (((1,), (1,)), ((), ()))
    tn = (((0,), (0,)), ((), ()))
    n_chunks = TM // CHUNK
    row_id = lax.broadcasted_iota(jnp.int32, (CHUNK, HG_DK), 0)
    order = range(CHUNK - 1, -1, -1) if reverse else range(CHUNK)

    def chunk_head(it, carry):
        step, h = it // HG_HEADS, it % HG_HEADS
        ci = (n_chunks - 1 - step) if reverse else step
        rows = pl.ds(pl.multiple_of(ci * CHUNK, CHUNK), CHUNK)
        cols = pl.ds(pl.multiple_of(h * HG_DK, HG_DK), HG_DK)
        q, k, v = q_ref[rows, cols], k_ref[rows, cols], v_ref[rows, cols]
        f = jnp.exp(lf_ref[0, rows, cols].astype(F32) + lf_ref[1, rows, cols].astype(F32))
        st = st_ref[0, h]
        o = jnp.zeros((CHUNK, HG_DK), F32)
        for t in order:
            one = row_id == t
            v_t = jnp.where(one, v, jnp.zeros_like(v))
            st = st * f[t:t + 1, :] + lax.dot_general(v_t, k, tn, preferred_element_type=F32)
            o = jnp.where(one, lax.dot_general(q, st.astype(BF16), nt, preferred_element_type=F32), o)
        st_ref[0, h] = st
        o_ref[rows, cols] = o.astype(o_ref.dtype)
        return carry

    lax.fori_loop(0, n_chunks * HG_HEADS, chunk_head, 0)


class _GlaDirection(NamedTuple):
    reverse: bool
    q: Any
    v: Any
    lf: Any
    k: Any
    o: Any
    st: Any
    sw: Any
    ut: Any
    qd: Any
    kd: Any
    ki: Any
    cd: Any
    b: Any


_NT = (((1,), (1,)), ((), ()))
_TN = (((0,), (0,)), ((), ()))
_HG_COLS = [slice(h * HG_DK, (h + 1) * HG_DK) for h in range(HG_HEADS)]
_N_CHUNKS = TM // CHUNK
_N_PAIRS = _N_CHUNKS // 2


def _gla_prepare(d):
    r = lax.broadcasted_iota(jnp.int32, (TM, TM), 0)
    c = lax.broadcasted_iota(jnp.int32, (TM, TM), 1)
    same = (r // CHUNK) == (c // CHUNK)
    tri = (same & ((c >= r) if d.reverse else (c <= r))).astype(BF16)
    b = jnp.dot(tri, d.lf[0], preferred_element_type=F32) + jnp.dot(tri, d.lf[1], preferred_element_type=F32)
    d.b[...] = b
    edge = 0 if d.reverse else CHUNK - 1
    total = b.reshape(_N_CHUNKS, CHUNK, D_MODEL)[:, edge, :]
    d.cd[...] = jnp.exp(total)
    return jnp.max(-total) <= DECAY_CLAMP


def _gla_pair_rows(d, step):
    pi = (_N_PAIRS - 1 - step) if d.reverse else step
    row0 = pi * 2 * CHUNK
    lo, hi = (pl.ds(row0, CHUNK), 2 * pi), (pl.ds(row0 + CHUNK, CHUNK), 2 * pi + 1)
    return pl.ds(row0, 2 * CHUNK), ((hi, lo) if d.reverse else (lo, hi))


def _chunk_decay(d, chunk):
    return d.cd[pl.ds(chunk, 1), :]


def _gla_decayed_operands(d, step):
    both, ((_, c_first), (_, c_second)) = _gla_pair_rows(d, step)
    b = d.b[both, :]
    ki = d.k[both, :] * jnp.exp(jnp.minimum(-b, DECAY_CLAMP)).astype(BF16)
    d.qd[both, :] = d.q[both, :] * jnp.exp(b).astype(BF16)
    d.ki[both, :] = ki
    c_lo, c_hi = (c_second, c_first) if d.reverse else (c_first, c_second)
    ends = jnp.concatenate([jnp.broadcast_to(_chunk_decay(d, c).astype(BF16), (CHUNK, D_MODEL)) for c in (c_lo, c_hi)],
                           axis=0)
    d.kd[both, :] = ki * ends


def _rows_scaled(x, scale_row, second_half):
    scale = jnp.broadcast_to(scale_row.astype(BF16), (CHUNK, x.shape[1]))
    ones = jnp.ones((CHUNK, x.shape[1]), BF16)
    return x * jnp.concatenate([ones, scale] if second_half else [scale, ones], axis=0)


def _gla_key_value_product(d, step):
    _gla_decayed_operands(d, step)
    both, (_, (_, c_second)) = _gla_pair_rows(d, step)
    decay_second = _chunk_decay(d, c_second)
    for h, cols in enumerate(_HG_COLS):
        keys = _rows_scaled(d.kd[both, cols], decay_second[:, cols], second_half=d.reverse)
        d.ut[h] = lax.dot_general(d.v[both, cols], keys, _TN, preferred_element_type=F32)


def _gla_start(d):
    for h in range(HG_HEADS):
        d.sw[0, h] = d.st[0, h].T.astype(BF16)
    _gla_key_value_product(d, 0)


def _gla_pair(d, step, src, dst):
    sr = lax.broadcasted_iota(jnp.int32, (CHUNK, CHUNK), 0)
    sc = lax.broadcasted_iota(jnp.int32, (CHUNK, CHUNK), 1)
    tr = lax.broadcasted_iota(jnp.int32, (CHUNK, 2 * CHUNK), 0)
    tc = lax.broadcasted_iota(jnp.int32, (CHUNK, 2 * CHUNK), 1)
    if d.reverse:
        keep_first = sc >= sr
        keep_second = (tc >= tr) | (tc >= CHUNK)
    else:
        keep_first = sc <= sr
        keep_second = (tc < CHUNK) | (tc - CHUNK <= tr)
    both, ((first, c_first), (second, c_second)) = _gla_pair_rows(d, step)
    decay_first = _chunk_decay(d, c_first)
    decay_both = decay_first * _chunk_decay(d, c_second)
    for h, cols in enumerate(_HG_COLS):
        new = d.st[src, h] * decay_both[:, cols] + d.ut[h]
        d.st[dst, h] = new
        d.sw[dst, h] = new.T.astype(BF16)
    a_first = [lax.dot_general(d.qd[first, cols], d.ki[first, cols], _NT, preferred_element_type=F32)
               for cols in _HG_COLS]
    a_second = []
    for cols in _HG_COLS:
        lo_keys = (d.ki if d.reverse else d.kd)[pl.ds(both.start, CHUNK), cols]
        hi_keys = (d.kd if d.reverse else d.ki)[pl.ds(both.start + CHUNK, CHUNK), cols]
        keys = jnp.concatenate([lo_keys, hi_keys], axis=0)
        a_second.append(lax.dot_general(d.qd[second, cols], keys, _NT, preferred_element_type=F32))
    inter = [jnp.dot(_rows_scaled(d.qd[both, cols], decay_first[:, cols], second_half=not d.reverse),
                     d.sw[src, h], preferred_element_type=F32) for h, cols in enumerate(_HG_COLS)]
    if step + 1 < _N_PAIRS:
        _gla_key_value_product(d, step + 1)
    first_half, second_half = (slice(CHUNK, None), slice(None, CHUNK)) if d.reverse else \
                              (slice(None, CHUNK), slice(CHUNK, None))
    for h, cols in enumerate(_HG_COLS):
        am = jnp.where(keep_first, a_first[h], 0.0).astype(BF16)
        d.o[first, cols] = (jnp.dot(am, d.v[first, cols], preferred_element_type=F32)
                            + inter[h][first_half]).astype(d.o.dtype)
        am = jnp.where(keep_second, a_second[h], 0.0).astype(BF16)
        d.o[second, cols] = (jnp.dot(am, d.v[both, cols], preferred_element_type=F32)
                             + inter[h][second_half]).astype(d.o.dtype)


def _gla_kernel(qf_ref, vf_ref, lff_ref, kf_ref, qb_ref, vb_ref, lfb_ref, kb_ref, s0_ref,
                of_ref, ob_ref, sout_ref, st_scr, sw_scr, ut_scr, qd_scr, kd_scr, ki_scr, cd_scr, b_scr):
    i = pl.program_id(0)
    is_prompt = i < PROMPT_TILES
    first = jnp.logical_or(is_prompt, (i - PROMPT_TILES) % TILES_PER_DEC_SEQ == 0)
    work = [tuple(scr.at[n] for scr in (st_scr, sw_scr, ut_scr, qd_scr, kd_scr, ki_scr, cd_scr, b_scr))
            for n in range(2)]
    fwd = _GlaDirection(False, qf_ref, vf_ref, lff_ref, kf_ref, of_ref, *work[0])
    bwd = _GlaDirection(True, qb_ref, vb_ref, lfb_ref, kb_ref, ob_ref, *work[1])

    @pl.when(is_prompt)
    def _():
        for d in (fwd, bwd):
            d.st[0] = jnp.zeros((HG_HEADS, HG_DK, HG_DK), F32)

    @pl.when(jnp.logical_and(first, jnp.logical_not(is_prompt)))
    def _():
        for n, d in enumerate((fwd, bwd)):
            for h in range(HG_HEADS):
                d.st[0, h] = s0_ref[0, 0, n, h].T

    exact_scores = jnp.logical_and(_gla_prepare(fwd), _gla_prepare(bwd))

    @pl.when(exact_scores)
    def _():
        _gla_start(fwd)
        _gla_start(bwd)

        for step in range(_N_PAIRS):
            for d in (fwd, bwd):
                _gla_pair(d, step, step % 2, 1 - step % 2)

    @pl.when(jnp.logical_not(exact_scores))
    def _():
        for d in (fwd, bwd):
            _gla_exact(d.reverse, d.q, d.v, d.lf, d.k, d.st, d.o)

    @pl.when(is_prompt)
    def _():
        for n, d in enumerate((fwd, bwd)):
            for h in range(HG_HEADS):
                sout_ref[0, 0, n, h] = d.st[0, h].T


def _gla(q, v, lff, kf, lfb, kb, state_hgrn):
    def fwd_tile(i):
        return (i, 0)

    def bwd_tile(i):
        j = (i - PROMPT_TILES) % TILES_PER_DEC_SEQ
        return (jnp.where(i < PROMPT_TILES, i, i - j + (TILES_PER_DEC_SEQ - 1 - j)), 0)

    def s0_idx(i):
        return (jnp.maximum(i - PROMPT_TILES, 0) // TILES_PER_DEC_SEQ, 0, 0, 0, 0, 0)

    def sout_idx(i):
        return (jnp.minimum(i, PROMPT_TILES - 1), 0, 0, 0, 0, 0)

    f_blk = pl.BlockSpec((TM, D_MODEL), fwd_tile)
    b_blk = pl.BlockSpec((TM, D_MODEL), bwd_tile)
    f_split = pl.BlockSpec((2, TM, D_MODEL), lambda i: (0,) + fwd_tile(i))
    b_split = pl.BlockSpec((2, TM, D_MODEL), lambda i: (0,) + bwd_tile(i))
    st_blk = (1, 1, 2, HG_HEADS, HG_DK, HG_DK)
    return pl.pallas_call(
        _gla_kernel,
        out_shape=(
            jax.ShapeDtypeStruct((N_TOK, D_MODEL), BF16),
            jax.ShapeDtypeStruct((N_TOK, D_MODEL), BF16),
            jax.ShapeDtypeStruct((BATCH,) + st_blk[1:], F32),
        ),
        grid=(N_TILES,),
        in_specs=[f_blk, f_blk, f_split, f_blk, b_blk, b_blk, b_split, b_blk, pl.BlockSpec(st_blk, s0_idx)],
        out_specs=(f_blk, b_blk, pl.BlockSpec(st_blk, sout_idx)),
        scratch_shapes=[
            pltpu.VMEM((2, 2, HG_HEADS, HG_DK, HG_DK), F32),
            pltpu.VMEM((2, 2, HG_HEADS, HG_DK, HG_DK), BF16),
            pltpu.VMEM((2, HG_HEADS, HG_DK, HG_DK), F32),
            pltpu.VMEM((2, TM, D_MODEL), BF16),
            pltpu.VMEM((2, TM, D_MODEL), BF16),
            pltpu.VMEM((2, TM, D_MODEL), BF16),
            pltpu.VMEM((2, TM // CHUNK, D_MODEL), F32),
            pltpu.VMEM((2, TM, D_MODEL), F32),
        ],
        compiler_params=_params(48, "arbitrary"),
        name="hgrn_recurrence",
    )(q, v, lff, kf, q, v, lfb, kb, state_hgrn)


def _route(logits):
    lane = lax.broadcasted_iota(jnp.int32, logits.shape, 1)
    neg = jnp.float32(-jnp.inf)
    big = jnp.int32(ROUTER_LANES)

    def first_max(x):
        m = jnp.max(x, axis=-1, keepdims=True)
        return m, jnp.min(jnp.where(x == m, lane, big), axis=-1, keepdims=True)

    is_group = (lane >= N_EXPERTS) & (lane < N_EXPERTS + N_GROUPS)
    gl = jnp.where(is_group, logits, neg)
    gmax, g_lane = first_max(gl)
    g_sel = g_lane - N_EXPERTS
    gsum = jnp.sum(jnp.exp(gl - gmax), axis=-1, keepdims=True)
    p_g = 1.0 / gsum
    in_sel = (lane < N_EXPERTS) & ((lane // EXPERTS_PER_GROUP) == g_sel)
    m1, i1 = first_max(jnp.where(in_sel, logits, neg))
    m2, i2 = first_max(jnp.where(in_sel & (lane != i1), logits, neg))
    e2 = jnp.exp(m2 - m1)
    return i1, i2, p_g / (1.0 + e2), p_g * e2 / (1.0 + e2)


def _pack_rows(x):
    q = PACK_WORDS
    bits = pltpu.bitcast(x.astype(BF16).astype(F32), jnp.uint32)
    return [(bits[:, (2 + h) * q:(3 + h) * q] & jnp.uint32(0xFFFF0000)) | (bits[:, h * q:(h + 1) * q] >> 16)
            for h in range(2)]


def _unpack_rows(half0, half1):
    lo = lambda w: pltpu.bitcast(w << 16, F32).astype(BF16)
    hi = lambda w: pltpu.bitcast(w & jnp.uint32(0xFFFF0000), F32).astype(BF16)
    return [lo(half0), lo(half1), hi(half0), hi(half1)]


def _mixer_tail(mix_bf16, x, mod_ref, nw2_ref, wo_ref, wr_ref, br_ref, x1_ref, h2p_ref, lg_ref):
    cond = _tile_cond(pl.program_id(0))
    out = jnp.dot(mix_bf16, wo_ref[...], preferred_element_type=F32)
    x1 = x + _mod_row(mod_ref, cond, 2) * out
    x1_ref[...] = x1
    h2 = _norm_mod(x1, nw2_ref[...], _mod_row(mod_ref, cond, 3), _mod_row(mod_ref, cond, 4))
    h2p_ref[...] = jnp.concatenate(_pack_rows(h2), axis=1)
    h_hi = h2.astype(BF16)
    h_lo = (h2 - h_hi.astype(F32)).astype(BF16)
    lg_ref[...] = (jnp.dot(h_hi, wr_ref[0], preferred_element_type=F32)
                   + (jnp.dot(h_hi, wr_ref[1], preferred_element_type=F32)
                      + jnp.dot(h_lo, wr_ref[0], preferred_element_type=F32))) + br_ref[...]


def _route_kernel(lg_ref, ri_ref, rw_ref, cnt_ref, carry_scr):
    @pl.when(pl.program_id(0) == 0)
    def _():
        carry_scr[...] = jnp.zeros_like(carry_scr)

    logits = lg_ref[...]
    i1, i2, w1, w2 = _route(logits)
    lane = lax.broadcasted_iota(jnp.int32, logits.shape, 1)
    chosen = ((lane == i1) | (lane == i2)).astype(BF16)
    r = lax.broadcasted_iota(jnp.int32, (ROUTE_TM, ROUTE_TM), 0)
    c = lax.broadcasted_iota(jnp.int32, (ROUTE_TM, ROUTE_TM), 1)
    before = jnp.dot((c < r).astype(BF16), chosen, preferred_element_type=F32) + carry_scr[...]
    r1 = jnp.sum(jnp.where(lane == i1, before, 0.0), axis=-1, keepdims=True).astype(jnp.int32)
    r2 = jnp.sum(jnp.where(lane == i2, before, 0.0), axis=-1, keepdims=True).astype(jnp.int32)
    total = carry_scr[...] + jnp.sum(chosen.astype(F32), axis=0, keepdims=True)
    carry_scr[...] = total
    cnt_ref[...] = total
    ri_ref[...] = jnp.where(lane == 0, i1, jnp.where(lane == 1, i2, jnp.where(lane == 2, r1, r2)))
    rw_ref[...] = jnp.where(lane == 0, w1, w2)


def _route_tokens(logits):
    blk = pl.BlockSpec((ROUTE_TM, ROUTER_LANES), lambda i: (i, 0))
    return pl.pallas_call(
        _route_kernel,
        out_shape=(
            jax.ShapeDtypeStruct((N_TOK, ROUTER_LANES), jnp.int32),
            jax.ShapeDtypeStruct((N_TOK, ROUTER_LANES), F32),
            jax.ShapeDtypeStruct((1, ROUTER_LANES), F32),
        ),
        grid=(N_TOK // ROUTE_TM,),
        in_specs=[blk],
        out_specs=(blk, blk, pl.BlockSpec((1, ROUTER_LANES), lambda i: (0, 0))),
        scratch_shapes=[pltpu.VMEM((1, ROUTER_LANES), F32)],
        compiler_params=_params(32, "arbitrary"),
        name="moe_route",
    )(logits)


def _hg_out_kernel(of_ref, ob_ref, g_ref, on_ref, xp_ref, xs_ref, *rest):
    o = of_ref[...].astype(F32) + ob_ref[...].astype(F32)
    parts = []
    for h in range(HG_HEADS):
        oh = o[:, h * HG_DK:(h + 1) * HG_DK]
        parts.append(oh * lax.rsqrt(jnp.mean(oh * oh, axis=-1, keepdims=True) + EPS) * on_ref[...])
    y = jnp.concatenate(parts, axis=1) * _silu(g_ref[...].astype(F32))
    _mixer_tail(y.astype(BF16), _stream_tile(xp_ref, xs_ref), *rest)


def _at_out_kernel(ac_ref, al_ref, x_ref, *rest):
    _mixer_tail(_stream_tile(ac_ref, al_ref), x_ref[...], *rest)


def _mixer_out(kernel_fn, name, mix_inputs, mix_specs, mod, nw2, w_out, w_router, b_router):
    tile = lambda i: (i, 0)
    fixed2 = lambda i: (0, 0)
    blk = pl.BlockSpec((TM, D_MODEL), tile)
    lanes_blk = pl.BlockSpec((TM, ROUTER_LANES), tile)
    return pl.pallas_call(
        kernel_fn,
        out_shape=(
            jax.ShapeDtypeStruct((N_TOK, D_MODEL), F32),
            jax.ShapeDtypeStruct((N_TOK, ROW_WORDS), jnp.uint32),
            jax.ShapeDtypeStruct((N_TOK, ROUTER_LANES), F32),
        ),
        grid=(N_TILES,),
        in_specs=list(mix_specs) + [
            pl.BlockSpec(mod.shape, fixed2),
            pl.BlockSpec((1, D_MODEL), fixed2),
            pl.BlockSpec(w_out.shape, fixed2),
            pl.BlockSpec(w_router.shape, lambda i: (0, 0, 0)),
            pl.BlockSpec((1, ROUTER_LANES), fixed2),
        ],
        out_specs=(blk, pl.BlockSpec((TM, ROW_WORDS), tile), lanes_blk),
        compiler_params=_params(40, "arbitrary"),
        name=name,
    )(*mix_inputs, mod, nw2, w_out, w_router, b_router)


def _moe_plan(route_i, counts):
    cnt = counts[0, :N_EXPERTS].astype(jnp.int32)
    padded = ((cnt + MOE_TILE - 1) // MOE_TILE) * MOE_TILE
    ends = jnp.cumsum(padded)
    offs = ends - padded
    experts = route_i[:, 0:2]
    pos = jnp.sum(jnp.where(experts[:, :, None] == jnp.arange(N_EXPERTS)[None, None, :], offs[None, None, :], 0),
                  axis=-1) + route_i[:, 2:4]
    tile_start = jnp.arange(MOE_ROWS // MOE_TILE, dtype=jnp.int32) * MOE_TILE
    tile_expert = jnp.minimum(jnp.sum(ends[None, :] <= tile_start[:, None], axis=1), N_EXPERTS - 1).astype(jnp.int32)
    of_tile = tile_expert[:, None] == jnp.arange(N_EXPERTS)[None, :]
    tile_offs = jnp.sum(jnp.where(of_tile, offs[None, :], 0), axis=1)
    tile_cnt = jnp.sum(jnp.where(of_tile, cnt[None, :], 0), axis=1)
    tile_rows = jnp.clip(tile_offs + tile_cnt - tile_start, 0, MOE_TILE).astype(jnp.int32)
    active = tile_start < ends[-1]
    tile_opens = jnp.where(active & (tile_start == tile_offs), tile_expert, -1).astype(jnp.int32)
    tile_opens = jnp.concatenate([tile_opens, jnp.full((MOE_STEP_TILES,), -1, jnp.int32)])
    n_active = (ends[-1] // MOE_TILE).astype(jnp.int32).reshape(1)
    slot = (jnp.cumsum((cnt > 0).astype(jnp.int32)) - 1) % MOE_WEIGHT_SLOTS
    tile_slot = jnp.sum(jnp.where(of_tile, slot[None, :], 0), axis=1)
    last_slot = jnp.sum(jnp.where(tile_start == ends[-1] - MOE_TILE, tile_slot, 0))
    tile_slot = jnp.where(active, tile_slot, last_slot).astype(jnp.int32)
    return pos.astype(jnp.int32), (tile_rows, tile_slot, tile_opens, n_active)


SC_WINDOW = 64


def _sc_mesh():
    return plsc.VectorSubcoreMesh(core_axis_name="c", subcore_axis_name="s")


def _sc_scatter_rows(x, idx_a, idx_b, n_out_rows):
    n = x.shape[0]

    @functools.partial(pl.kernel, out_type=jax.ShapeDtypeStruct((n_out_rows, ROW_WORDS), x.dtype), mesh=_sc_mesh(),
                       scratch_types=[pltpu.SemaphoreType.DMA, pltpu.SemaphoreType.DMA])
    def scatter(x_hbm, ia_hbm, ib_hbm, o_hbm, sem_a, sem_b):
        def body(x_vmem, ia_vmem, ib_vmem):
            copy_a = pltpu.async_copy(x_vmem, o_hbm.at[ia_vmem.at[0]], sem_a)
            copy_b = pltpu.async_copy(x_vmem, o_hbm.at[ib_vmem.at[0]], sem_b)
            copy_a.wait()
            copy_b.wait()

        idx_spec = pl.BlockSpec((1, SC_WINDOW), index_map=lambda i: (i, 0))
        pltpu.emit_pipeline(
            body, grid=(n // SC_WINDOW,),
            in_specs=[pl.BlockSpec((SC_WINDOW, ROW_WORDS), index_map=lambda i: (i, 0)), idx_spec, idx_spec],
            out_specs=[],
            core_axis_name=("c", "s"), dimension_semantics=(pltpu.PARALLEL,),
        )(x_hbm, ia_hbm, ib_hbm)

    return scatter(x, idx_a.reshape(n // SC_WINDOW, SC_WINDOW), idx_b.reshape(n // SC_WINDOW, SC_WINDOW))


def _sc_gather_rows(table, idx):
    n = idx.shape[0]

    @functools.partial(pl.kernel, out_type=jax.ShapeDtypeStruct((n, ROW_WORDS), table.dtype), mesh=_sc_mesh())
    def gather(t_hbm, i_hbm, o_hbm):
        def body(i_vmem, o_vmem):
            pltpu.sync_copy(t_hbm.at[i_vmem.at[0]], o_vmem)

        pltpu.emit_pipeline(
            body, grid=(n // SC_WINDOW,),
            in_specs=[pl.BlockSpec((1, SC_WINDOW), index_map=lambda i: (i, 0))],
            out_specs=[pl.BlockSpec((SC_WINDOW, ROW_WORDS), index_map=lambda i: (i, 0))],
            core_axis_name=("c", "s"), dimension_semantics=(pltpu.PARALLEL,),
        )(i_hbm, o_hbm)

    return gather(table, idx.reshape(n // SC_WINDOW, SC_WINDOW))


def _ffn_kernel(tr_ref, ts_ref, to_ref, na_ref, xs_ref, w1_hbm, w3_hbm, w2_hbm, ys_ref,
                w1_buf, w3_buf, w2_buf, sem, *, layer):
    def weight_copies(t):
        pairs = ((w1_hbm, w1_buf), (w3_hbm, w3_buf), (w2_hbm, w2_buf))
        return [pltpu.make_async_copy(hbm.at[layer, to_ref[t]], buf.at[ts_ref[t]], sem.at[ts_ref[t], j])
                for j, (hbm, buf) in enumerate(pairs)]

    def for_experts_opened_from(first_tile, action):
        for j in range(MOE_STEP_TILES):
            t = first_tile + j

            @pl.when(to_ref[t] >= 0)
            def _():
                for copy in weight_copies(t):
                    action(copy)

    def tile(t, rows):
        slot = ts_ref[t]
        row = lax.broadcasted_iota(jnp.int32, (MOE_TILE, PACK_WORDS), 0)
        live = row < tr_ref[t]
        halves = [jnp.where(live, xs_ref[rows, h * PACK_WORDS:(h + 1) * PACK_WORDS],
                            jnp.zeros((MOE_TILE, PACK_WORDS), jnp.uint32)) for h in range(2)]
        chunks = _unpack_rows(*halves)

        def up(w_buf):
            acc = None
            for k, chunk in enumerate(chunks):
                w = w_buf[slot, k * PACK_WORDS:(k + 1) * PACK_WORDS, :].astype(BF16)
                part = jnp.dot(chunk, w, preferred_element_type=F32)
                acc = part if acc is None else acc + part
            return acc

        hid = (_silu(up(w1_buf)) * up(w3_buf)).astype(BF16)
        y = jnp.dot(hid, w2_buf[slot].astype(BF16), preferred_element_type=F32)
        ys_ref[rows, :] = jnp.concatenate(_pack_rows(y), axis=1)

    step = pl.program_id(0)
    first_tile = step * MOE_STEP_TILES

    @pl.when(first_tile < na_ref[0])
    def _():
        @pl.when(step == 0)
        def _():
            for_experts_opened_from(first_tile, lambda copy: copy.start())

        for_experts_opened_from(first_tile + MOE_STEP_TILES, lambda copy: copy.start())
        for_experts_opened_from(first_tile, lambda copy: copy.wait())
        for j in range(MOE_STEP_TILES):
            tile(first_tile + j, slice(j * MOE_TILE, (j + 1) * MOE_TILE))


def _ffn(xs, tables, layer, w1, w3, w2):
    step_rows = MOE_STEP_TILES * MOE_TILE
    row_tile = lambda s, tr, ts, to, na: (jnp.minimum(s, (na[0] - 1) // MOE_STEP_TILES), 0)
    hbm = pl.BlockSpec(memory_space=pl.ANY)
    return pl.pallas_call(
        functools.partial(_ffn_kernel, layer=layer),
        out_shape=jax.ShapeDtypeStruct((MOE_ROWS, ROW_WORDS), jnp.uint32),
        grid_spec=pltpu.PrefetchScalarGridSpec(
            num_scalar_prefetch=len(tables),
            grid=(MOE_ROWS // step_rows,),
            in_specs=[pl.BlockSpec((step_rows, ROW_WORDS), row_tile), hbm, hbm, hbm],
            out_specs=pl.BlockSpec((step_rows, ROW_WORDS), row_tile),
            scratch_shapes=[
                pltpu.VMEM((MOE_WEIGHT_SLOTS, D_MODEL, D_EXPERT), F32),
                pltpu.VMEM((MOE_WEIGHT_SLOTS, D_MODEL, D_EXPERT), F32),
                pltpu.VMEM((MOE_WEIGHT_SLOTS, D_EXPERT, D_MODEL), F32),
                pltpu.SemaphoreType.DMA((MOE_WEIGHT_SLOTS, 3)),
            ],
        ),
        compiler_params=_params(48, "arbitrary"),
        name="moe_experts",
    )(*tables, xs, w1, w3, w2)


def _moe_res(x_ref, y_ref, rw_ref, mod_ref):
    cond = _tile_cond(pl.program_id(0))
    rw = rw_ref[...]
    wa = rw[:, 0:1]
    wb = rw[:, 1:2]
    ya = _unpack_rows(y_ref[0, :, :PACK_WORDS], y_ref[0, :, PACK_WORDS:])
    yb = _unpack_rows(y_ref[1, :, :PACK_WORDS], y_ref[1, :, PACK_WORDS:])
    y = jnp.concatenate([wa * a.astype(F32) + wb * b.astype(F32) for a, b in zip(ya, yb)], axis=1)
    return x_ref[...] + _mod_row(mod_ref, cond, 5) * y


def _moe_res_final_kernel(x_ref, y_ref, rw_ref, mod_ref, fn_ref, op_ref, os_ref):
    x = _moe_res(x_ref, y_ref, rw_ref, mod_ref)
    y = x * lax.rsqrt(jnp.mean(x * x, axis=-1, keepdims=True) + EPS) * fn_ref[...]
    is_prompt = pl.program_id(0) < PROMPT_TILES

    @pl.when(is_prompt)
    def _():
        op_ref[...] = y

    @pl.when(jnp.logical_not(is_prompt))
    def _():
        os_ref[...] = y


def _moe_combine(x1, y_pairs, route_w, mod, final_norm):
    tile = lambda i: (i, 0)
    fixed2 = lambda i: (0, 0)
    return pl.pallas_call(
        _moe_res_final_kernel,
        out_shape=(jax.ShapeDtypeStruct((N_PROMPT, D_MODEL), F32),
                   jax.ShapeDtypeStruct((N_TOK - N_PROMPT, D_MODEL), F32)),
        grid=(N_TILES,),
        in_specs=[pl.BlockSpec((TM, D_MODEL), tile),
                  pl.BlockSpec((2, TM, ROW_WORDS), lambda i: (0, i, 0)),
                  pl.BlockSpec((TM, ROUTER_LANES), tile),
                  pl.BlockSpec(mod.shape, fixed2),
                  pl.BlockSpec((1, D_MODEL), fixed2)],
        out_specs=(pl.BlockSpec((TM, D_MODEL), _prompt_tile), pl.BlockSpec((TM, D_MODEL), _sample_tile)),
        compiler_params=_params(32, "arbitrary"),
        name="moe_combine",
    )(x1, y_pairs, route_w, mod, final_norm)


def _moe_layer(x1, h2p, logits, layer, w1, w3, w2):
    route_i, route_w, counts = _route_tokens(logits)
    pos, tables = _moe_plan(route_i, counts)
    xs = _sc_scatter_rows(h2p, pos[:, 0], pos[:, 1], MOE_ROWS)
    ys = _ffn(xs, tables, layer, w1, w3, w2)
    y_pairs = _sc_gather_rows(ys, jnp.concatenate([pos[:, 0], pos[:, 1]]))
    return x1, y_pairs.reshape(2, N_TOK, ROW_WORDS), route_w


def _swap_rotary_halves(x):
    n = x.shape[-1]
    lane = lax.broadcasted_iota(jnp.int32, x.shape, 1)
    quarter = ROPE_HALF // 2
    return jnp.where((lane % ROPE_HALF) < quarter, pltpu.roll(x, n - quarter, 1), pltpu.roll(x, quarter, 1))


def _at_in_kernel(x1_ref, y_ref, rw_ref, mod_prev_ref, mod_ref, nw_ref, w_ref, cos_ref, sin_ref,
                  x_ref, q_ref, kt_ref, v_ref, kc_ref, vc_ref):
    i = pl.program_id(0)
    cond = _tile_cond(i)
    x = _moe_res(x1_ref, y_ref, rw_ref, mod_prev_ref)
    x_ref[...] = x
    h = _norm_mod(x, nw_ref[...], _mod_row(mod_ref, cond, 0), _mod_row(mod_ref, cond, 1)).astype(BF16)
    nq = ATT_Q_HEADS * ATT_HEAD_DIM
    nk = ATT_KV_HEADS * ATT_HEAD_DIM
    cos = cos_ref[...]
    sin = sin_ref[...]

    def rope(x):
        reps = x.shape[-1] // LANES
        return x * jnp.concatenate([cos] * reps, axis=1) + _swap_rotary_halves(x) * jnp.concatenate([sin] * reps, axis=1)

    q_ref[...] = (rope(jnp.dot(h, w_ref[:, :nq], preferred_element_type=F32)) * ATT_HEAD_DIM ** -0.5).astype(BF16)
    k = rope(jnp.dot(h, w_ref[:, nq:nq + nk], preferred_element_type=F32))
    v = jnp.dot(h, w_ref[:, nq + nk:], preferred_element_type=F32)
    kt = k.T
    kt_ref[...] = kt.astype(BF16)
    v_ref[...] = v.astype(BF16)

    @pl.when(i < PROMPT_TILES)
    def _():
        kc_ref[...] = kt
        vc_ref[...] = v.T


def _rope_tables():
    f32 = np.float32
    pos = np.arange(DEC_SEQ)
    t_row = (pos // GRID_W).astype(f32)
    t_col = (pos % GRID_W).astype(f32)
    inv = f32(ROPE_BASE) ** (-np.arange(0, ROPE_HALF, 2, dtype=f32) / f32(ROPE_HALF))
    j = np.arange(LANES) % ATT_HEAD_DIM
    freq = inv[(j % ROPE_HALF) % (ROPE_HALF // 2)]
    ang = (np.where((j < ROPE_HALF)[None, :], t_row[:, None], t_col[:, None]) * freq[None, :]).astype(f32)
    sign = np.where((j % ROPE_HALF) < ROPE_HALF // 2, -1.0, 1.0).astype(f32)
    cos = np.concatenate([np.ones((TM, LANES), f32), np.cos(ang)], axis=0)
    sin = np.concatenate([np.zeros((TM, LANES), f32), np.sin(ang) * sign[None, :]], axis=0)
    return jnp.asarray(cos, F32), jnp.asarray(sin, F32)


def _at_in(x1, y_pairs, route_w, mod_prev, mod, nw, w_in, cos, sin):
    tile = lambda i: (i, 0)
    fixed2 = lambda i: (0, 0)
    rope_tile = lambda i: (jnp.where(i < PROMPT_TILES, 0, 1 + (i - PROMPT_TILES) % TILES_PER_DEC_SEQ), 0)
    nk = ATT_KV_HEADS * ATT_HEAD_DIM
    return pl.pallas_call(
        _at_in_kernel,
        out_shape=(
            jax.ShapeDtypeStruct((N_TOK, D_MODEL), F32),
            jax.ShapeDtypeStruct((N_TOK, D_MODEL), BF16),
            jax.ShapeDtypeStruct((nk, N_TOK), BF16),
            jax.ShapeDtypeStruct((N_TOK, nk), BF16),
            jax.ShapeDtypeStruct((N_PROMPT, nk), F32),
            jax.ShapeDtypeStruct((N_PROMPT, nk), F32),
        ),
        grid=(N_TILES,),
        in_specs=[
            pl.BlockSpec((TM, D_MODEL), tile),
            pl.BlockSpec((2, TM, ROW_WORDS), lambda i: (0, i, 0)),
            pl.BlockSpec((TM, ROUTER_LANES), tile),
            pl.BlockSpec(mod_prev.shape, fixed2),
            pl.BlockSpec(mod.shape, fixed2),
            pl.BlockSpec((1, D_MODEL), fixed2),
            pl.BlockSpec(w_in.shape, fixed2),
            pl.BlockSpec((TM, LANES), rope_tile),
            pl.BlockSpec((TM, LANES), rope_tile),
        ],
        out_specs=(pl.BlockSpec((TM, D_MODEL), tile), pl.BlockSpec((TM, D_MODEL), tile),
                   pl.BlockSpec((nk, TM), lambda i: (0, i)),
                   pl.BlockSpec((TM, nk), tile),
                   pl.BlockSpec((TM, nk), _prompt_tile), pl.BlockSpec((TM, nk), _prompt_tile)),
        compiler_params=_params(40, "arbitrary"),
        name="attn_in_proj",
    )(x1, y_pairs, route_w, mod_prev, mod, nw, w_in, cos, sin)


def _attend(q, kt_all, v_all, mask, sink_ref, o_ref):
    nq = q.shape[0]
    group_lanes = ATT_GROUP * ATT_HEAD_DIM
    lane = lax.broadcasted_iota(jnp.int32, (nq, group_lanes), 1)
    mine = [(lane // ATT_HEAD_DIM) == g for g in range(ATT_GROUP)]
    row_head = lax.broadcasted_iota(jnp.int32, (ATT_GROUP * nq, 1), 0) // nq
    if mask is not None:
        mask = jnp.concatenate([mask] * ATT_GROUP, axis=0)
    for hk in range(ATT_KV_HEADS):
        vh = v_all[:, hk * ATT_HEAD_DIM:(hk + 1) * ATT_HEAD_DIM]
        kt = jnp.concatenate([kt_all[hk * ATT_HEAD_DIM:(hk + 1) * ATT_HEAD_DIM, :]] * ATT_GROUP, axis=0)
        vt = jnp.concatenate([vh] * ATT_GROUP, axis=1)
        qg = q[:, hk * group_lanes:(hk + 1) * group_lanes]
        q_stack = jnp.concatenate([jnp.where(mine[g], qg, jnp.zeros_like(qg)) for g in range(ATT_GROUP)], axis=0)
        s = jnp.dot(q_stack, kt, preferred_element_type=F32)
        if mask is not None:
            s = jnp.where(mask, s, -jnp.inf)
        sink = jnp.zeros((ATT_GROUP * nq, 1), F32)
        for g in range(ATT_GROUP):
            sink = jnp.where(row_head == g, sink_ref[hk * ATT_GROUP + g], sink)
        m = jnp.maximum(jnp.max(s, axis=-1, keepdims=True), sink)
        p = jnp.exp(s - m)
        denom = jnp.sum(p, axis=-1, keepdims=True) + jnp.exp(sink - m)
        o = jnp.dot(p.astype(BF16), vt, preferred_element_type=F32) / denom
        acc = jnp.where(mine[0], o[:nq], 0.0)
        for g in range(1, ATT_GROUP):
            acc = acc + jnp.where(mine[g], o[g * nq:(g + 1) * nq], 0.0)
        o_ref[:, hk * group_lanes:(hk + 1) * group_lanes] = acc.astype(BF16)


def _ctx_attn_kernel(sink_ref, q_ref, k_ref, v_ref, o_ref):
    _attend(q_ref[...], k_ref[...], v_ref[...], None, sink_ref, o_ref)


def _lat_attn_kernel(sink_ref, q_ref, kp_ref, kc_ref, kn_ref, vp_ref, vc_ref, vn_ref, ck_ref, cv_ref, o_ref):
    jb = pl.program_id(1)
    kt_all = jnp.concatenate([kp_ref[...], kc_ref[...], kn_ref[...], ck_ref[0].astype(BF16)], axis=1)
    v_all = jnp.concatenate([vp_ref[...], vc_ref[...], vn_ref[...], cv_ref[0].astype(BF16)], axis=0)
    nkeys = 3 * BLOCK + PAST_LEN
    qi = lax.broadcasted_iota(jnp.int32, (BLOCK, nkeys), 0)
    kj = lax.broadcasted_iota(jnp.int32, (BLOCK, nkeys), 1)
    qpos = jb * BLOCK + qi
    kpos = (jb - 1) * BLOCK + kj
    local_ok = (jnp.abs(qpos - kpos) <= WINDOW) & (kpos >= 0) & (kpos < DEC_SEQ)
    mask = (kj >= 3 * BLOCK) | local_ok
    _attend(q_ref[...], kt_all, v_all, mask, sink_ref, o_ref)


def _attention(q, kt, v, cache_kt, cache_v, sink):
    nk = ATT_KV_HEADS * ATT_HEAD_DIM
    smem = pl.BlockSpec(memory_space=pltpu.SMEM)
    ctx = pl.pallas_call(
        _ctx_attn_kernel,
        out_shape=jax.ShapeDtypeStruct((N_PROMPT, D_MODEL), BF16),
        grid=(BATCH,),
        in_specs=[
            smem,
            pl.BlockSpec((SEQ, D_MODEL), lambda b: (b, 0)),
            pl.BlockSpec((nk, SEQ), lambda b: (0, b)),
            pl.BlockSpec((SEQ, nk), lambda b: (b, 0)),
        ],
        out_specs=pl.BlockSpec((SEQ, D_MODEL), lambda b: (b, 0)),
        compiler_params=_params(40, "arbitrary"),
        name="context_attention",
    )(sink, q, kt, v)

    nb = DEC_SEQ // BLOCK
    base = N_PROMPT // BLOCK
    cur = lambda b, j: (base + b * nb + j, 0)
    prev = lambda b, j: (base + b * nb + jnp.maximum(j - 1, 0), 0)
    nxt = lambda b, j: (base + b * nb + jnp.minimum(j + 1, nb - 1), 0)
    kv_blk = lambda f: pl.BlockSpec((BLOCK, nk), f)
    kt_blk = lambda f: pl.BlockSpec((nk, BLOCK), lambda b, j: f(b, j)[::-1])
    cache_blk = pl.BlockSpec((1, PAST_LEN, nk), lambda b, j: (b, 0, 0))
    cache_kt_blk = pl.BlockSpec((1, nk, PAST_LEN), lambda b, j: (b, 0, 0))
    lat = pl.pallas_call(
        _lat_attn_kernel,
        out_shape=jax.ShapeDtypeStruct((DEC_BATCH * DEC_SEQ, D_MODEL), BF16),
        grid=(DEC_BATCH, nb),
        in_specs=[
            smem,
            pl.BlockSpec((BLOCK, D_MODEL), cur),
            kt_blk(prev), kt_blk(cur), kt_blk(nxt),
            kv_blk(prev), kv_blk(cur), kv_blk(nxt),
            cache_kt_blk, cache_blk,
        ],
        out_specs=pl.BlockSpec((BLOCK, D_MODEL), lambda b, j: (b * nb + j, 0)),
        compiler_params=_params(40, "arbitrary", "arbitrary"),
        name="latent_attention",
    )(sink, q, kt, kt, kt, v, v, v, cache_kt, cache_v)
    return ctx, lat


def kernel(x_prompt, x_sample, state_hgrn, cache_k, cache_v, c, c_ctx, ada_w, ada_b, norm_w, hg_w_in,
           hg_lb_logits, hg_onorm, hg_w_out, at_w_in, at_sink, at_w_out, moe_w_group, moe_b_group,
           moe_w_expert, moe_b_expert, moe_w1, moe_w3, moe_w2, final_norm):
    xp = x_prompt.reshape(N_PROMPT, D_MODEL)
    xs = x_sample.reshape(N_TOK - N_PROMPT, D_MODEL)
    cond = jnp.concatenate([c_ctx[None, :], c], axis=0)
    mod = _ada(cond, ada_w, ada_b)
    nk = ATT_KV_HEADS * ATT_HEAD_DIM

    def router_params(i):
        pad = jnp.zeros((D_MODEL, ROUTER_LANES - N_EXPERTS - N_GROUPS), F32)
        w = jnp.concatenate([moe_w_expert[i], moe_w_group[i], pad], axis=1)
        b = jnp.concatenate([moe_b_expert[i], moe_b_group[i], pad[0]])[None, :]
        hi = w.astype(BF16)
        lo = (w - hi.astype(F32)).astype(BF16)
        return jnp.stack([hi, lo]), b

    tile = lambda i: (i, 0)
    blk = pl.BlockSpec((TM, D_MODEL), tile)

    prompt_blk = pl.BlockSpec((TM, D_MODEL), _prompt_tile)
    sample_blk = pl.BlockSpec((TM, D_MODEL), _sample_tile)
    q, v, g, lff, kf, lfb, kb = _hg_in(xp, xs, mod[0], norm_w[0, 0][None, :], hg_w_in[0].astype(BF16), hg_lb_logits)
    o_f, o_b, state_new = _gla(q, v, lff, kf, lfb, kb, state_hgrn)
    wr, br = router_params(0)
    routed = _mixer_out(
        _hg_out_kernel, "hgrn_out_route", (o_f, o_b, g, hg_onorm[0][None, :], xp, xs),
        (blk, blk, blk, pl.BlockSpec((1, HG_DK), lambda i: (0, 0)), prompt_blk, sample_blk),
        mod[0], norm_w[0, 1][None, :], hg_w_out[0].astype(BF16), wr, br)
    moe_out = _moe_layer(*routed, 0, moe_w1, moe_w3, moe_w2)

    cos, sin = _rope_tables()
    x, qa, ka, va, k_ctx, v_ctx = _at_in(*moe_out, mod[0], mod[1], norm_w[1, 0][None, :],
                                         at_w_in[0].astype(BF16), cos, sin)
    attn_ctx, attn_lat = _attention(qa, ka, va, cache_k[:, 0].reshape(DEC_BATCH, PAST_LEN, nk).transpose(0, 2, 1),
                                    cache_v[:, 0].reshape(DEC_BATCH, PAST_LEN, nk), at_sink[0])
    wr, br = router_params(1)
    routed = _mixer_out(
        _at_out_kernel, "attn_out_route", (attn_ctx, attn_lat, x), (prompt_blk, sample_blk, blk),
        mod[1], norm_w[1, 1][None, :], at_w_out[0].astype(BF16), wr, br)
    y_prompt, y_sample = _moe_combine(*_moe_layer(*routed, 1, moe_w1, moe_w3, moe_w2), mod[1], final_norm[None, :])

    def cache(feature_major):
        return feature_major.reshape(BATCH, 1, ATT_KV_HEADS, ATT_HEAD_DIM, SEQ).transpose(0, 1, 4, 2, 3)

    return (y_prompt.reshape(BATCH, SEQ, D_MODEL), y_sample.reshape(DEC_BATCH, DEC_SEQ, D_MODEL), state_new,
            cache(k_ctx), cache(v_ctx))
```

```python
import functools
from typing import Any, NamedTuple

import jax
import jax.numpy as jnp
import numpy as np
from jax import lax
from jax.experimental import pallas as pl
from jax.experimental.pallas import tpu as pltpu
from jax.experimental.pallas import tpu_sc as plsc

F32 = jnp.float32
BF16 = jnp.bfloat16

D_MODEL = 1024
BATCH = 16
SEQ = 256
DEC_BATCH = 2
DEC_SEQ = 1024
PAST_LEN = 512
GRID_W = 64
HG_HEADS = 8
HG_DK = 128
CHUNK = 16
ATT_HEAD_DIM = 64
ATT_Q_HEADS = 16
ATT_KV_HEADS = 4
ATT_GROUP = 4
WINDOW = 128
BLOCK = 128
ROPE_HALF = 32
ROPE_BASE = 10000.0
N_GROUPS = 4
EXPERTS_PER_GROUP = 8
N_EXPERTS = 32
D_EXPERT = 256
EPS = 1e-6

N_PROMPT = BATCH * SEQ
N_TOK = N_PROMPT + DEC_BATCH * DEC_SEQ
TM = 256
N_TILES = N_TOK // TM
PROMPT_TILES = N_PROMPT // TM
TILES_PER_DEC_SEQ = DEC_SEQ // TM
LANES = 128
N_COND_USED = 1 + DEC_BATCH
N_COND = 8
ROUTER_LANES = 128
ROUTE_TM = 1024
ROUTE_ROWS = 8
HG_IN_TM = 256
DECAY_CLAMP = 60.0
PACK_WORDS = D_MODEL // 4
ROW_WORDS = 2 * PACK_WORDS
MOE_TILE = 256
MOE_STEP_TILES = 4
MOE_WEIGHT_SLOTS = 2 * MOE_STEP_TILES
MOE_ROWS = 2 * N_TOK + N_EXPERTS * MOE_TILE
MIB = 1024 * 1024


def _params(vmem_mib, *semantics):
    return pltpu.CompilerParams(dimension_semantics=semantics, vmem_limit_bytes=vmem_mib * MIB)


def _tile_cond(i):
    return jnp.where(i < PROMPT_TILES, 0, 1 + (i - PROMPT_TILES) // TILES_PER_DEC_SEQ)


def _prompt_tile(i):
    return (jnp.minimum(i, PROMPT_TILES - 1), 0)


def _sample_tile(i):
    return (jnp.maximum(i - PROMPT_TILES, 0), 0)


def _stream_tile(prompt_ref, sample_ref):
    return jnp.where(pl.program_id(0) < PROMPT_TILES, prompt_ref[...], sample_ref[...])


def _mod_row(mod_ref, cond, which):
    return mod_ref[pl.ds(cond, 1), which * D_MODEL:(which + 1) * D_MODEL]


def _norm_mod(x, nw, shift, scale):
    y = x * lax.rsqrt(jnp.mean(x * x, axis=-1, keepdims=True) + EPS)
    return (y * nw) * (1.0 + scale) + shift


def _silu(x):
    return x * jax.nn.sigmoid(x)


def _ada_kernel(c_ref, w_ref, b_ref, o_ref):
    s = [_silu(c_ref[r]) for r in range(N_COND_USED)]
    tn = w_ref.shape[-1]
    rows = []
    for r in range(N_COND_USED):
        cols = [jnp.sum(w_ref[0, :, j * LANES:(j + 1) * LANES] * s[r], axis=0, keepdims=True)
                for j in range(tn // LANES)]
        rows.append(jnp.concatenate(cols, axis=1) + b_ref[0])
    rows.append(jnp.zeros((N_COND - N_COND_USED, tn), F32))
    o_ref[0] = jnp.concatenate(rows, axis=0)


def _ada(cond, ada_w, ada_b):
    depth, _, n = ada_w.shape
    tn = 1536
    cond_cols = jnp.broadcast_to(cond[:, :, None], (N_COND_USED, D_MODEL, LANES))
    return pl.pallas_call(
        _ada_kernel,
        out_shape=jax.ShapeDtypeStruct((depth, N_COND, n), F32),
        grid=(depth, n // tn),
        in_specs=[
            pl.BlockSpec((N_COND_USED, D_MODEL, LANES), lambda l, j: (0, 0, 0)),
            pl.BlockSpec((1, D_MODEL, tn), lambda l, j: (l, 0, j)),
            pl.BlockSpec((1, 1, tn), lambda l, j: (l, 0, j)),
        ],
        out_specs=pl.BlockSpec((1, N_COND, tn), lambda l, j: (l, 0, j)),
        compiler_params=_params(40, "arbitrary", "arbitrary"),
        name="ada_modulation",
    )(cond_cols, ada_w, ada_b.reshape(depth, 1, n))


def _hg_in_kernel(xp_ref, xs_ref, mod_ref, nw_ref, w_ref, lbl_ref,
                  q_ref, v_ref, g_ref, lff_ref, kf_ref, lfb_ref, kb_ref):
    i = pl.program_id(0) * (HG_IN_TM // TM)
    cond = _tile_cond(i)
    x = jnp.where(i < PROMPT_TILES, xp_ref[...], xs_ref[...])
    h = _norm_mod(x, nw_ref[...], _mod_row(mod_ref, cond, 0), _mod_row(mod_ref, cond, 1)).astype(BF16)

    def proj(c):
        return jnp.dot(h, w_ref[:, c * D_MODEL:(c + 1) * D_MODEL], preferred_element_type=F32)

    l0, l1, l2 = lbl_ref[0], lbl_ref[1], lbl_ref[2]
    m = jnp.maximum(jnp.maximum(l0, l1), l2)
    e0, e1, e2 = jnp.exp(l0 - m), jnp.exp(l1 - m), jnp.exp(l2 - m)
    lb = e0 / (e0 + e1 + e2)

    q_ref[...] = proj(0).astype(BF16)
    v_ref[...] = proj(1).astype(BF16)
    for d, (lf_ref, k_ref) in enumerate(((lff_ref, kf_ref), (lfb_ref, kb_ref))):
        lbd = lb[d:d + 1, :]
        f = lbd + (1.0 - lbd) * jax.nn.sigmoid(proj(2 + d))
        lf = jnp.log(f)
        hi = lf.astype(BF16)
        lf_ref[0] = hi
        lf_ref[1] = (lf - hi.astype(F32)).astype(BF16)
        k_ref[...] = (1.0 - f).astype(BF16)
    g_ref[...] = proj(4).astype(BF16)


def _hg_in(xp, xs, mod, nw, w_in, lb_logits):
    tile = lambda i: (i, 0)
    fixed2 = lambda i: (0, 0)
    bf = jax.ShapeDtypeStruct((N_TOK, D_MODEL), BF16)
    ff = jax.ShapeDtypeStruct((2, N_TOK, D_MODEL), BF16)
    blk = pl.BlockSpec((HG_IN_TM, D_MODEL), tile)
    split_blk = pl.BlockSpec((2, HG_IN_TM, D_MODEL), lambda i: (0, i, 0))
    prompt_steps = N_PROMPT // HG_IN_TM
    return pl.pallas_call(
        _hg_in_kernel,
        out_shape=(bf, bf, bf, ff, bf, ff, bf),
        grid=(N_TOK // HG_IN_TM,),
        in_specs=[
            pl.BlockSpec((HG_IN_TM, D_MODEL), lambda i: (jnp.minimum(i, prompt_steps - 1), 0)),
            pl.BlockSpec((HG_IN_TM, D_MODEL), lambda i: (jnp.maximum(i - prompt_steps, 0), 0)),
            pl.BlockSpec(mod.shape, fixed2),
            pl.BlockSpec((1, D_MODEL), fixed2),
            pl.BlockSpec(w_in.shape, fixed2, pipeline_mode=pl.Buffered(1)),
            pl.BlockSpec(lb_logits.shape, lambda i: (0, 0, 0)),
        ],
        out_specs=(blk, blk, blk, split_blk, blk, split_blk, blk),
        compiler_params=_params(56, "arbitrary"),
        name="hgrn_in_proj",
    )(xp, xs, mod, nw, w_in, lb_logits)


def _gla_exact(reverse, q_ref, v_ref, lf_ref, k_ref, st_ref, o_ref):
    nt = (((1,), (1,)), ((), ()))
    tn = (((0,), (0,)), ((), ()))
    n_chunks = TM // CHUNK
    row_id = lax.broadcasted_iota(jnp.int32, (CHUNK, HG_DK), 0)
    order = range(CHUNK - 1, -1, -1) if reverse else range(CHUNK)

    def chunk_head(it, carry):
        step, h = it // HG_HEADS, it % HG_HEADS
        ci = (n_chunks - 1 - step) if reverse else step
        rows = pl.ds(pl.multiple_of(ci * CHUNK, CHUNK), CHUNK)
        cols = pl.ds(pl.multiple_of(h * HG_DK, HG_DK), HG_DK)
        q, k, v = q_ref[rows, cols], k_ref[rows, cols], v_ref[rows, cols]
        f = jnp.exp(lf_ref[0, rows, cols].astype(F32) + lf_ref[1, rows, cols].astype(F32))
        st = st_ref[0, h]
        o = jnp.zeros((CHUNK, HG_DK), F32)
        for t in order:
            one = row_id == t
            v_t = jnp.where(one, v, jnp.zeros_like(v))
            st = st * f[t:t + 1, :] + lax.dot_general(v_t, k, tn, preferred_element_type=F32)
            o = jnp.where(one, lax.dot_general(q, st.astype(BF16), nt, preferred_element_type=F32), o)
        st_ref[0, h] = st
        o_ref[rows, cols] = o.astype(o_ref.dtype)
        return carry

    lax.fori_loop(0, n_chunks * HG_HEADS, chunk_head, 0)


class _GlaDirection(NamedTuple):
    reverse: bool
    q: Any
    v: Any
    lf: Any
    k: Any
    o: Any
    st: Any
    sw: Any
    ut: Any
    qd: Any
    kd: Any
    ki: Any
    cd: Any
    b: Any


_NT = (((1,), (1,)), ((), ()))
_TN = (((0,), (0,)), ((), ()))
_HG_COLS = [slice(h * HG_DK, (h + 1) * HG_DK) for h in range(HG_HEADS)]
_N_CHUNKS = TM // CHUNK
_N_PAIRS = _N_CHUNKS // 2


def _gla_prepare(d):
    r = lax.broadcasted_iota(jnp.int32, (TM, TM), 0)
    c = lax.broadcasted_iota(jnp.int32, (TM, TM), 1)
    same = (r // CHUNK) == (c // CHUNK)
    tri = (same & ((c >= r) if d.reverse else (c <= r))).astype(BF16)
    b = jnp.dot(tri, d.lf[0], preferred_element_type=F32) + jnp.dot(tri, d.lf[1], preferred_element_type=F32)
    d.b[...] = b
    edge = 0 if d.reverse else CHUNK - 1
    total = b.reshape(_N_CHUNKS, CHUNK, D_MODEL)[:, edge, :]
    d.cd[...] = jnp.exp(total)
    return jnp.max(-total) <= DECAY_CLAMP


def _gla_pair_rows(d, step):
    pi = (_N_PAIRS - 1 - step) if d.reverse else step
    row0 = pi * 2 * CHUNK
    lo, hi = (pl.ds(row0, CHUNK), 2 * pi), (pl.ds(row0 + CHUNK, CHUNK), 2 * pi + 1)
    return pl.ds(row0, 2 * CHUNK), ((hi, lo) if d.reverse else (lo, hi))


def _chunk_decay(d, chunk):
    return d.cd[pl.ds(chunk, 1), :]


def _gla_decayed_operands(d, step):
    both, ((_, c_first), (_, c_second)) = _gla_pair_rows(d, step)
    b = d.b[both, :]
    ki = d.k[both, :] * jnp.exp(jnp.minimum(-b, DECAY_CLAMP)).astype(BF16)
    d.qd[both, :] = d.q[both, :] * jnp.exp(b).astype(BF16)
    d.ki[both, :] = ki
    c_lo, c_hi = (c_second, c_first) if d.reverse else (c_first, c_second)
    ends = jnp.concatenate([jnp.broadcast_to(_chunk_decay(d, c).astype(BF16), (CHUNK, D_MODEL)) for c in (c_lo, c_hi)],
                           axis=0)
    d.kd[both, :] = ki * ends


def _rows_scaled(x, scale_row, second_half):
    scale = jnp.broadcast_to(scale_row.astype(BF16), (CHUNK, x.shape[1]))
    ones = jnp.ones((CHUNK, x.shape[1]), BF16)
    return x * jnp.concatenate([ones, scale] if second_half else [scale, ones], axis=0)


def _gla_key_value_product(d, step):
    _gla_decayed_operands(d, step)
    both, (_, (_, c_second)) = _gla_pair_rows(d, step)
    decay_second = _chunk_decay(d, c_second)
    for h, cols in enumerate(_HG_COLS):
        keys = _rows_scaled(d.kd[both, cols], decay_second[:, cols], second_half=d.reverse)
        d.ut[h] = lax.dot_general(d.v[both, cols], keys, _TN, preferred_element_type=F32)


def _gla_start(d):
    for h in range(HG_HEADS):
        d.sw[0, h] = d.st[0, h].T.astype(BF16)
    _gla_key_value_product(d, 0)


def _gla_pair(d, step, src, dst):
    sr = lax.broadcasted_iota(jnp.int32, (CHUNK, CHUNK), 0)
    sc = lax.broadcasted_iota(jnp.int32, (CHUNK, CHUNK), 1)
    tr = lax.broadcasted_iota(jnp.int32, (CHUNK, 2 * CHUNK), 0)
    tc = lax.broadcasted_iota(jnp.int32, (CHUNK, 2 * CHUNK), 1)
    if d.reverse:
        keep_first = sc >= sr
        keep_second = (tc >= tr) | (tc >= CHUNK)
    else:
        keep_first = sc <= sr
        keep_second = (tc < CHUNK) | (tc - CHUNK <= tr)
    both, ((first, c_first), (second, c_second)) = _gla_pair_rows(d, step)
    decay_first = _chunk_decay(d, c_first)
    decay_both = decay_first * _chunk_decay(d, c_second)
    for h, cols in enumerate(_HG_COLS):
        new = d.st[src, h] * decay_both[:, cols] + d.ut[h]
        d.st[dst, h] = new
        d.sw[dst, h] = new.T.astype(BF16)
    a_first = [lax.dot_general(d.qd[first, cols], d.ki[first, cols], _NT, preferred_element_type=F32)
               for cols in _HG_COLS]
    a_second = []
    for cols in _HG_COLS:
        lo_keys = (d.ki if d.reverse else d.kd)[pl.ds(both.start, CHUNK), cols]
        hi_keys = (d.kd if d.reverse else d.ki)[pl.ds(both.start + CHUNK, CHUNK), cols]
        keys = jnp.concatenate([lo_keys, hi_keys], axis=0)
        a_second.append(lax.dot_general(d.qd[second, cols], keys, _NT, preferred_element_type=F32))
    inter = [jnp.dot(_rows_scaled(d.qd[both, cols], decay_first[:, cols], second_half=not d.reverse),
                     d.sw[src, h], preferred_element_type=F32) for h, cols in enumerate(_HG_COLS)]
    if step + 1 < _N_PAIRS:
        _gla_key_value_product(d, step + 1)
    first_half, second_half = (slice(CHUNK, None), slice(None, CHUNK)) if d.reverse else \
                              (slice(None, CHUNK), slice(CHUNK, None))
    for h, cols in enumerate(_HG_COLS):
        am = jnp.where(keep_first, a_first[h], 0.0).astype(BF16)
        d.o[first, cols] = (jnp.dot(am, d.v[first, cols], preferred_element_type=F32)
                            + inter[h][first_half]).astype(d.o.dtype)
        am = jnp.where(keep_second, a_second[h], 0.0).astype(BF16)
        d.o[second, cols] = (jnp.dot(am, d.v[both, cols], preferred_element_type=F32)
                             + inter[h][second_half]).astype(d.o.dtype)


def _gla_kernel(qf_ref, vf_ref, lff_ref, kf_ref, qb_ref, vb_ref, lfb_ref, kb_ref, s0_ref,
                of_ref, ob_ref, sout_ref, st_scr, sw_scr, ut_scr, qd_scr, kd_scr, ki_scr, cd_scr, b_scr):
    i = pl.program_id(0)
    is_prompt = i < PROMPT_TILES
    first = jnp.logical_or(is_prompt, (i - PROMPT_TILES) % TILES_PER_DEC_SEQ == 0)
    work = [tuple(scr.at[n] for scr in (st_scr, sw_scr, ut_scr, qd_scr, kd_scr, ki_scr, cd_scr, b_scr))
            for n in range(2)]
    fwd = _GlaDirection(False, qf_ref, vf_ref, lff_ref, kf_ref, of_ref, *work[0])
    bwd = _GlaDirection(True, qb_ref, vb_ref, lfb_ref, kb_ref, ob_ref, *work[1])

    @pl.when(is_prompt)
    def _():
        for d in (fwd, bwd):
            d.st[0] = jnp.zeros((HG_HEADS, HG_DK, HG_DK), F32)

    @pl.when(jnp.logical_and(first, jnp.logical_not(is_prompt)))
    def _():
        for n, d in enumerate((fwd, bwd)):
            for h in range(HG_HEADS):
                d.st[0, h] = s0_ref[0, 0, n, h].T

    exact_scores = jnp.logical_and(_gla_prepare(fwd), _gla_prepare(bwd))

    @pl.when(exact_scores)
    def _():
        _gla_start(fwd)
        _gla_start(bwd)

        for step in range(_N_PAIRS):
            for d in (fwd, bwd):
                _gla_pair(d, step, step % 2, 1 - step % 2)

    @pl.when(jnp.logical_not(exact_scores))
    def _():
        for d in (fwd, bwd):
            _gla_exact(d.reverse, d.q, d.v, d.lf, d.k, d.st, d.o)

    @pl.when(is_prompt)
    def _():
        for n, d in enumerate((fwd, bwd)):
            for h in range(HG_HEADS):
                sout_ref[0, 0, n, h] = d.st[0, h].T


def _gla(q, v, lff, kf, lfb, kb, state_hgrn):
    def fwd_tile(i):
        return (i, 0)

    def bwd_tile(i):
        j = (i - PROMPT_TILES) % TILES_PER_DEC_SEQ
        return (jnp.where(i < PROMPT_TILES, i, i - j + (TILES_PER_DEC_SEQ - 1 - j)), 0)

    def s0_idx(i):
        return (jnp.maximum(i - PROMPT_TILES, 0) // TILES_PER_DEC_SEQ, 0, 0, 0, 0, 0)

    def sout_idx(i):
        return (jnp.minimum(i, PROMPT_TILES - 1), 0, 0, 0, 0, 0)

    f_blk = pl.BlockSpec((TM, D_MODEL), fwd_tile)
    b_blk = pl.BlockSpec((TM, D_MODEL), bwd_tile)
    f_split = pl.BlockSpec((2, TM, D_MODEL), lambda i: (0,) + fwd_tile(i))
    b_split = pl.BlockSpec((2, TM, D_MODEL), lambda i: (0,) + bwd_tile(i))
    st_blk = (1, 1, 2, HG_HEADS, HG_DK, HG_DK)
    return pl.pallas_call(
        _gla_kernel,
        out_shape=(
            jax.ShapeDtypeStruct((N_TOK, D_MODEL), BF16),
            jax.ShapeDtypeStruct((N_TOK, D_MODEL), BF16),
            jax.ShapeDtypeStruct((BATCH,) + st_blk[1:], F32),
        ),
        grid=(N_TILES,),
        in_specs=[f_blk, f_blk, f_split, f_blk, b_blk, b_blk, b_split, b_blk, pl.BlockSpec(st_blk, s0_idx)],
        out_specs=(f_blk, b_blk, pl.BlockSpec(st_blk, sout_idx)),
        scratch_shapes=[
            pltpu.VMEM((2, 2, HG_HEADS, HG_DK, HG_DK), F32),
            pltpu.VMEM((2, 2, HG_HEADS, HG_DK, HG_DK), BF16),
            pltpu.VMEM((2, HG_HEADS, HG_DK, HG_DK), F32),
            pltpu.VMEM((2, TM, D_MODEL), BF16),
            pltpu.VMEM((2, TM, D_MODEL), BF16),
            pltpu.VMEM((2, TM, D_MODEL), BF16),
            pltpu.VMEM((2, TM // CHUNK, D_MODEL), F32),
            pltpu.VMEM((2, TM, D_MODEL), F32),
        ],
        compiler_params=_params(48, "arbitrary"),
        name="hgrn_recurrence",
    )(q, v, lff, kf, q, v, lfb, kb, state_hgrn)


def _route(logits):
    lane = lax.broadcasted_iota(jnp.int32, logits.shape, 1)
    neg = jnp.float32(-jnp.inf)
    big = jnp.int32(ROUTER_LANES)

    def first_max(x):
        m = jnp.max(x, axis=-1, keepdims=True)
        return m, jnp.min(jnp.where(x == m, lane, big), axis=-1, keepdims=True)

    is_group = (lane >= N_EXPERTS) & (lane < N_EXPERTS + N_GROUPS)
    gl = jnp.where(is_group, logits, neg)
    gmax, g_lane = first_max(gl)
    g_sel = g_lane - N_EXPERTS
    gsum = jnp.sum(jnp.exp(gl - gmax), axis=-1, keepdims=True)
    p_g = 1.0 / gsum
    in_sel = (lane < N_EXPERTS) & ((lane // EXPERTS_PER_GROUP) == g_sel)
    m1, i1 = first_max(jnp.where(in_sel, logits, neg))
    m2, i2 = first_max(jnp.where(in_sel & (lane != i1), logits, neg))
    e2 = jnp.exp(m2 - m1)
    return i1, i2, p_g / (1.0 + e2), p_g * e2 / (1.0 + e2)


def _pack_rows(x):
    q = PACK_WORDS
    bits = pltpu.bitcast(x.astype(BF16).astype(F32), jnp.uint32)
    return [(bits[:, (2 + h) * q:(3 + h) * q] & jnp.uint32(0xFFFF0000)) | (bits[:, h * q:(h + 1) * q] >> 16)
            for h in range(2)]


def _unpack_rows(half0, half1):
    lo = lambda w: pltpu.bitcast(w << 16, F32).astype(BF16)
    hi = lambda w: pltpu.bitcast(w & jnp.uint32(0xFFFF0000), F32).astype(BF16)
    return [lo(half0), lo(half1), hi(half0), hi(half1)]


def _mixer_tail(mix_bf16, x, mod_ref, nw2_ref, wo_ref, wr_ref, br_ref, x1_ref, h2p_ref, lg_ref):
    cond = _tile_cond(pl.program_id(0))
    out = jnp.dot(mix_bf16, wo_ref[...], preferred_element_type=F32)
    x1 = x + _mod_row(mod_ref, cond, 2) * out
    x1_ref[...] = x1
    h2 = _norm_mod(x1, nw2_ref[...], _mod_row(mod_ref, cond, 3), _mod_row(mod_ref, cond, 4))
    h2p_ref[...] = jnp.concatenate(_pack_rows(h2), axis=1)
    h_hi = h2.astype(BF16)
    h_lo = (h2 - h_hi.astype(F32)).astype(BF16)
    lg_ref[...] = (jnp.dot(h_hi, wr_ref[0], preferred_element_type=F32)
                   + (jnp.dot(h_hi, wr_ref[1], preferred_element_type=F32)
                      + jnp.dot(h_lo, wr_ref[0], preferred_element_type=F32))) + br_ref[...]


def _route_kernel(lg_ref, rt_ref, rw_ref, cnt_ref, carry_scr):
    @pl.when(pl.program_id(0) == 0)
    def _():
        carry_scr[...] = jnp.zeros_like(carry_scr)

    logits = lg_ref[...]
    i1, i2, w1, w2 = _route(logits)
    lane = lax.broadcasted_iota(jnp.int32, logits.shape, 1)
    chosen = ((lane == i1) | (lane == i2)).astype(BF16)
    r = lax.broadcasted_iota(jnp.int32, (ROUTE_TM, ROUTE_TM), 0)
    c = lax.broadcasted_iota(jnp.int32, (ROUTE_TM, ROUTE_TM), 1)
    before = jnp.dot((c < r).astype(BF16), chosen, preferred_element_type=F32) + carry_scr[...]
    r1 = jnp.sum(jnp.where(lane == i1, before, 0.0), axis=-1, keepdims=True).astype(jnp.int32)
    r2 = jnp.sum(jnp.where(lane == i2, before, 0.0), axis=-1, keepdims=True).astype(jnp.int32)
    total = carry_scr[...] + jnp.sum(chosen.astype(F32), axis=0, keepdims=True)
    carry_scr[...] = total
    cnt_ref[...] = total
    by_token = jnp.where(lane == 0, i1, jnp.where(lane == 1, i2, jnp.where(lane == 2, r1, r2)))
    rt_ref[...] = by_token.T[:ROUTE_ROWS, :]
    rw_ref[...] = jnp.where(lane == 0, w1, w2)


def _route_tokens(logits):
    blk = pl.BlockSpec((ROUTE_TM, ROUTER_LANES), lambda i: (i, 0))
    return pl.pallas_call(
        _route_kernel,
        out_shape=(
            jax.ShapeDtypeStruct((ROUTE_ROWS, N_TOK), jnp.int32),
            jax.ShapeDtypeStruct((N_TOK, ROUTER_LANES), F32),
            jax.ShapeDtypeStruct((1, ROUTER_LANES), F32),
        ),
        grid=(N_TOK // ROUTE_TM,),
        in_specs=[blk],
        out_specs=(pl.BlockSpec((ROUTE_ROWS, ROUTE_TM), lambda i: (0, i)), blk,
                   pl.BlockSpec((1, ROUTER_LANES), lambda i: (0, 0))),
        scratch_shapes=[pltpu.VMEM((1, ROUTER_LANES), F32)],
        compiler_params=_params(32, "arbitrary"),
        name="moe_route",
    )(logits)


def _hg_out_kernel(of_ref, ob_ref, g_ref, on_ref, xp_ref, xs_ref, *rest):
    o = of_ref[...].astype(F32) + ob_ref[...].astype(F32)
    parts = []
    for h in range(HG_HEADS):
        oh = o[:, h * HG_DK:(h + 1) * HG_DK]
        parts.append(oh * lax.rsqrt(jnp.mean(oh * oh, axis=-1, keepdims=True) + EPS) * on_ref[...])
    y = jnp.concatenate(parts, axis=1) * _silu(g_ref[...].astype(F32))
    _mixer_tail(y.astype(BF16), _stream_tile(xp_ref, xs_ref), *rest)


def _at_out_kernel(ac_ref, al_ref, x_ref, *rest):
    _mixer_tail(_stream_tile(ac_ref, al_ref), x_ref[...], *rest)


def _mixer_out(kernel_fn, name, mix_inputs, mix_specs, mod, nw2, w_out, w_router, b_router):
    tile = lambda i: (i, 0)
    fixed2 = lambda i: (0, 0)
    blk = pl.BlockSpec((TM, D_MODEL), tile)
    lanes_blk = pl.BlockSpec((TM, ROUTER_LANES), tile)
    return pl.pallas_call(
        kernel_fn,
        out_shape=(
            jax.ShapeDtypeStruct((N_TOK, D_MODEL), F32),
            jax.ShapeDtypeStruct((N_TOK, ROW_WORDS), jnp.uint32),
            jax.ShapeDtypeStruct((N_TOK, ROUTER_LANES), F32),
        ),
        grid=(N_TILES,),
        in_specs=list(mix_specs) + [
            pl.BlockSpec(mod.shape, fixed2),
            pl.BlockSpec((1, D_MODEL), fixed2),
            pl.BlockSpec(w_out.shape, fixed2),
            pl.BlockSpec(w_router.shape, lambda i: (0, 0, 0)),
            pl.BlockSpec((1, ROUTER_LANES), fixed2),
        ],
        out_specs=(blk, pl.BlockSpec((TM, ROW_WORDS), tile), lanes_blk),
        compiler_params=_params(40, "arbitrary"),
        name=name,
    )(*mix_inputs, mod, nw2, w_out, w_router, b_router)


def _moe_plan(route_t, counts):
    cnt = counts[0, :N_EXPERTS].astype(jnp.int32)
    padded = ((cnt + MOE_TILE - 1) // MOE_TILE) * MOE_TILE
    ends = jnp.cumsum(padded)
    offs = ends - padded
    experts = route_t[0:2]
    pos = jnp.sum(jnp.where(experts[None, :, :] == jnp.arange(N_EXPERTS)[:, None, None], offs[:, None, None], 0),
                  axis=0) + route_t[2:4]
    tile_start = jnp.arange(MOE_ROWS // MOE_TILE, dtype=jnp.int32) * MOE_TILE
    tile_expert = jnp.minimum(jnp.sum(ends[None, :] <= tile_start[:, None], axis=1), N_EXPERTS - 1).astype(jnp.int32)
    of_tile = tile_expert[:, None] == jnp.arange(N_EXPERTS)[None, :]
    tile_offs = jnp.sum(jnp.where(of_tile, offs[None, :], 0), axis=1)
    tile_cnt = jnp.sum(jnp.where(of_tile, cnt[None, :], 0), axis=1)
    tile_rows = jnp.clip(tile_offs + tile_cnt - tile_start, 0, MOE_TILE).astype(jnp.int32)
    active = tile_start < ends[-1]
    tile_opens = jnp.where(active & (tile_start == tile_offs), tile_expert, -1).astype(jnp.int32)
    tile_opens = jnp.concatenate([tile_opens, jnp.full((MOE_STEP_TILES,), -1, jnp.int32)])
    n_active = (ends[-1] // MOE_TILE).astype(jnp.int32).reshape(1)
    slot = (jnp.cumsum((cnt > 0).astype(jnp.int32)) - 1) % MOE_WEIGHT_SLOTS
    tile_slot = jnp.sum(jnp.where(of_tile, slot[None, :], 0), axis=1)
    last_slot = jnp.sum(jnp.where(tile_start == ends[-1] - MOE_TILE, tile_slot, 0))
    tile_slot = jnp.where(active, tile_slot, last_slot).astype(jnp.int32)
    return pos.astype(jnp.int32), (tile_rows, tile_slot, tile_opens, n_active)


SC_WINDOW = 64


def _sc_mesh():
    return plsc.VectorSubcoreMesh(core_axis_name="c", subcore_axis_name="s")


def _sc_scatter_rows(x, idx_a, idx_b, n_out_rows):
    n = x.shape[0]

    @functools.partial(pl.kernel, out_type=jax.ShapeDtypeStruct((n_out_rows, ROW_WORDS), x.dtype), mesh=_sc_mesh(),
                       scratch_types=[pltpu.SemaphoreType.DMA, pltpu.SemaphoreType.DMA])
    def scatter(x_hbm, ia_hbm, ib_hbm, o_hbm, sem_a, sem_b):
        def body(x_vmem, ia_vmem, ib_vmem):
            copy_a = pltpu.async_copy(x_vmem, o_hbm.at[ia_vmem.at[0]], sem_a)
            copy_b = pltpu.async_copy(x_vmem, o_hbm.at[ib_vmem.at[0]], sem_b)
            copy_a.wait()
            copy_b.wait()

        idx_spec = pl.BlockSpec((1, SC_WINDOW), index_map=lambda i: (i, 0))
        pltpu.emit_pipeline(
            body, grid=(n // SC_WINDOW,),
            in_specs=[pl.BlockSpec((SC_WINDOW, ROW_WORDS), index_map=lambda i: (i, 0)), idx_spec, idx_spec],
            out_specs=[],
            core_axis_name=("c", "s"), dimension_semantics=(pltpu.PARALLEL,),
        )(x_hbm, ia_hbm, ib_hbm)

    return scatter(x, idx_a, idx_b)


def _sc_gather_rows(table, idx):
    n = idx.size

    @functools.partial(pl.kernel, out_type=jax.ShapeDtypeStruct((n, ROW_WORDS), table.dtype), mesh=_sc_mesh())
    def gather(t_hbm, i_hbm, o_hbm):
        def body(i_vmem, o_vmem):
            pltpu.sync_copy(t_hbm.at[i_vmem.at[0]], o_vmem)

        pltpu.emit_pipeline(
            body, grid=(n // SC_WINDOW,),
            in_specs=[pl.BlockSpec((1, SC_WINDOW), index_map=lambda i: (i, 0))],
            out_specs=[pl.BlockSpec((SC_WINDOW, ROW_WORDS), index_map=lambda i: (i, 0))],
            core_axis_name=("c", "s"), dimension_semantics=(pltpu.PARALLEL,),
        )(i_hbm, o_hbm)

    return gather(table, idx)


def _ffn_kernel(tr_ref, ts_ref, to_ref, na_ref, xs_ref, w1_hbm, w3_hbm, w2_hbm, ys_ref,
                w1_buf, w3_buf, w2_buf, sem, *, layer):
    def weight_copies(t):
        pairs = ((w1_hbm, w1_buf), (w3_hbm, w3_buf), (w2_hbm, w2_buf))
        return [pltpu.make_async_copy(hbm.at[layer, to_ref[t]], buf.at[ts_ref[t]], sem.at[ts_ref[t], j])
                for j, (hbm, buf) in enumerate(pairs)]

    def for_experts_opened_from(first_tile, action):
        for j in range(MOE_STEP_TILES):
            t = first_tile + j

            @pl.when(to_ref[t] >= 0)
            def _():
                for copy in weight_copies(t):
                    action(copy)

    def tile(t, rows):
        slot = ts_ref[t]
        row = lax.broadcasted_iota(jnp.int32, (MOE_TILE, PACK_WORDS), 0)
        live = row < tr_ref[t]
        halves = [jnp.where(live, xs_ref[rows, h * PACK_WORDS:(h + 1) * PACK_WORDS],
                            jnp.zeros((MOE_TILE, PACK_WORDS), jnp.uint32)) for h in range(2)]
        chunks = _unpack_rows(*halves)

        def up(w_buf):
            acc = None
            for k, chunk in enumerate(chunks):
                w = w_buf[slot, k * PACK_WORDS:(k + 1) * PACK_WORDS, :].astype(BF16)
                part = jnp.dot(chunk, w, preferred_element_type=F32)
                acc = part if acc is None else acc + part
            return acc

        hid = (_silu(up(w1_buf)) * up(w3_buf)).astype(BF16)
        y = jnp.dot(hid, w2_buf[slot].astype(BF16), preferred_element_type=F32)
        ys_ref[rows, :] = jnp.concatenate(_pack_rows(y), axis=1)

    step = pl.program_id(0)
    first_tile = step * MOE_STEP_TILES

    @pl.when(first_tile < na_ref[0])
    def _():
        @pl.when(step == 0)
        def _():
            for_experts_opened_from(first_tile, lambda copy: copy.start())

        for_experts_opened_from(first_tile + MOE_STEP_TILES, lambda copy: copy.start())
        for_experts_opened_from(first_tile, lambda copy: copy.wait())
        for j in range(MOE_STEP_TILES):
            tile(first_tile + j, slice(j * MOE_TILE, (j + 1) * MOE_TILE))


def _ffn(xs, tables, layer, w1, w3, w2):
    step_rows = MOE_STEP_TILES * MOE_TILE
    row_tile = lambda s, tr, ts, to, na: (jnp.minimum(s, (na[0] - 1) // MOE_STEP_TILES), 0)
    hbm = pl.BlockSpec(memory_space=pl.ANY)
    return pl.pallas_call(
        functools.partial(_ffn_kernel, layer=layer),
        out_shape=jax.ShapeDtypeStruct((MOE_ROWS, ROW_WORDS), jnp.uint32),
        grid_spec=pltpu.PrefetchScalarGridSpec(
            num_scalar_prefetch=len(tables),
            grid=(MOE_ROWS // step_rows,),
            in_specs=[pl.BlockSpec((step_rows, ROW_WORDS), row_tile), hbm, hbm, hbm],
            out_specs=pl.BlockSpec((step_rows, ROW_WORDS), row_tile),
            scratch_shapes=[
                pltpu.VMEM((MOE_WEIGHT_SLOTS, D_MODEL, D_EXPERT), F32),
                pltpu.VMEM((MOE_WEIGHT_SLOTS, D_MODEL, D_EXPERT), F32),
                pltpu.VMEM((MOE_WEIGHT_SLOTS, D_EXPERT, D_MODEL), F32),
                pltpu.SemaphoreType.DMA((MOE_WEIGHT_SLOTS, 3)),
            ],
        ),
        compiler_params=_params(48, "arbitrary"),
        name="moe_experts",
    )(*tables, xs, w1, w3, w2)


def _moe_res(x_ref, y_ref, rw_ref, mod_ref):
    cond = _tile_cond(pl.program_id(0))
    rw = rw_ref[...]
    wa = rw[:, 0:1]
    wb = rw[:, 1:2]
    ya = _unpack_rows(y_ref[0, :, :PACK_WORDS], y_ref[0, :, PACK_WORDS:])
    yb = _unpack_rows(y_ref[1, :, :PACK_WORDS], y_ref[1, :, PACK_WORDS:])
    y = jnp.concatenate([wa * a.astype(F32) + wb * b.astype(F32) for a, b in zip(ya, yb)], axis=1)
    return x_ref[...] + _mod_row(mod_ref, cond, 5) * y


def _moe_res_final_kernel(x_ref, y_ref, rw_ref, mod_ref, fn_ref, op_ref, os_ref):
    x = _moe_res(x_ref, y_ref, rw_ref, mod_ref)
    y = x * lax.rsqrt(jnp.mean(x * x, axis=-1, keepdims=True) + EPS) * fn_ref[...]
    is_prompt = pl.program_id(0) < PROMPT_TILES

    @pl.when(is_prompt)
    def _():
        op_ref[...] = y

    @pl.when(jnp.logical_not(is_prompt))
    def _():
        os_ref[...] = y


def _moe_combine(x1, y_pairs, route_w, mod, final_norm):
    tile = lambda i: (i, 0)
    fixed2 = lambda i: (0, 0)
    return pl.pallas_call(
        _moe_res_final_kernel,
        out_shape=(jax.ShapeDtypeStruct((N_PROMPT, D_MODEL), F32),
                   jax.ShapeDtypeStruct((N_TOK - N_PROMPT, D_MODEL), F32)),
        grid=(N_TILES,),
        in_specs=[pl.BlockSpec((TM, D_MODEL), tile),
                  pl.BlockSpec((2, TM, ROW_WORDS), lambda i: (0, i, 0)),
                  pl.BlockSpec((TM, ROUTER_LANES), tile),
                  pl.BlockSpec(mod.shape, fixed2),
                  pl.BlockSpec((1, D_MODEL), fixed2)],
        out_specs=(pl.BlockSpec((TM, D_MODEL), _prompt_tile), pl.BlockSpec((TM, D_MODEL), _sample_tile)),
        compiler_params=_params(32, "arbitrary"),
        name="moe_combine",
    )(x1, y_pairs, route_w, mod, final_norm)


def _moe_layer(x1, h2p, logits, layer, w1, w3, w2):
    route_t, route_w, counts = _route_tokens(logits)
    pos, tables = _moe_plan(route_t, counts)
    windows = pos.reshape(2, N_TOK // SC_WINDOW, SC_WINDOW)
    xs = _sc_scatter_rows(h2p, windows[0], windows[1], MOE_ROWS)
    ys = _ffn(xs, tables, layer, w1, w3, w2)
    y_pairs = _sc_gather_rows(ys, windows.reshape(2 * N_TOK // SC_WINDOW, SC_WINDOW))
    return x1, y_pairs.reshape(2, N_TOK, ROW_WORDS), route_w


def _swap_rotary_halves(x):
    n = x.shape[-1]
    lane = lax.broadcasted_iota(jnp.int32, x.shape, 1)
    quarter = ROPE_HALF // 2
    return jnp.where((lane % ROPE_HALF) < quarter, pltpu.roll(x, n - quarter, 1), pltpu.roll(x, quarter, 1))


def _at_in_kernel(x1_ref, y_ref, rw_ref, mod_prev_ref, mod_ref, nw_ref, w_ref, cos_ref, sin_ref,
                  x_ref, q_ref, kt_ref, v_ref, kc_ref, vc_ref):
    i = pl.program_id(0)
    cond = _tile_cond(i)
    x = _moe_res(x1_ref, y_ref, rw_ref, mod_prev_ref)
    x_ref[...] = x
    h = _norm_mod(x, nw_ref[...], _mod_row(mod_ref, cond, 0), _mod_row(mod_ref, cond, 1)).astype(BF16)
    nq = ATT_Q_HEADS * ATT_HEAD_DIM
    nk = ATT_KV_HEADS * ATT_HEAD_DIM
    cos = cos_ref[...]
    sin = sin_ref[...]

    def rope(x):
        reps = x.shape[-1] // LANES
        return x * jnp.concatenate([cos] * reps, axis=1) + _swap_rotary_halves(x) * jnp.concatenate([sin] * reps, axis=1)

    q_ref[...] = (rope(jnp.dot(h, w_ref[:, :nq], preferred_element_type=F32)) * ATT_HEAD_DIM ** -0.5).astype(BF16)
    k = rope(jnp.dot(h, w_ref[:, nq:nq + nk], preferred_element_type=F32))
    v = jnp.dot(h, w_ref[:, nq + nk:], preferred_element_type=F32)
    kt = k.T
    kt_ref[...] = kt.astype(BF16)
    v_ref[...] = v.astype(BF16)

    @pl.when(i < PROMPT_TILES)
    def _():
        kc_ref[...] = kt
        vc_ref[...] = v.T


def _rope_tables():
    f32 = np.float32
    pos = np.arange(DEC_SEQ)
    t_row = (pos // GRID_W).astype(f32)
    t_col = (pos % GRID_W).astype(f32)
    inv = f32(ROPE_BASE) ** (-np.arange(0, ROPE_HALF, 2, dtype=f32) / f32(ROPE_HALF))
    j = np.arange(LANES) % ATT_HEAD_DIM
    freq = inv[(j % ROPE_HALF) % (ROPE_HALF // 2)]
    ang = (np.where((j < ROPE_HALF)[None, :], t_row[:, None], t_col[:, None]) * freq[None, :]).astype(f32)
    sign = np.where((j % ROPE_HALF) < ROPE_HALF // 2, -1.0, 1.0).astype(f32)
    cos = np.concatenate([np.ones((TM, LANES), f32), np.cos(ang)], axis=0)
    sin = np.concatenate([np.zeros((TM, LANES), f32), np.sin(ang) * sign[None, :]], axis=0)
    return jnp.asarray(cos, F32), jnp.asarray(sin, F32)


def _at_in(x1, y_pairs, route_w, mod_prev, mod, nw, w_in, cos, sin):
    tile = lambda i: (i, 0)
    fixed2 = lambda i: (0, 0)
    rope_tile = lambda i: (jnp.where(i < PROMPT_TILES, 0, 1 + (i - PROMPT_TILES) % TILES_PER_DEC_SEQ), 0)
    nk = ATT_KV_HEADS * ATT_HEAD_DIM
    return pl.pallas_call(
        _at_in_kernel,
        out_shape=(
            jax.ShapeDtypeStruct((N_TOK, D_MODEL), F32),
            jax.ShapeDtypeStruct((N_TOK, D_MODEL), BF16),
            jax.ShapeDtypeStruct((nk, N_TOK), BF16),
            jax.ShapeDtypeStruct((N_TOK, nk), BF16),
            jax.ShapeDtypeStruct((N_PROMPT, nk), F32),
            jax.ShapeDtypeStruct((N_PROMPT, nk), F32),
        ),
        grid=(N_TILES,),
        in_specs=[
            pl.BlockSpec((TM, D_MODEL), tile),
            pl.BlockSpec((2, TM, ROW_WORDS), lambda i: (0, i, 0)),
            pl.BlockSpec((TM, ROUTER_LANES), tile),
            pl.BlockSpec(mod_prev.shape, fixed2),
            pl.BlockSpec(mod.shape, fixed2),
            pl.BlockSpec((1, D_MODEL), fixed2),
            pl.BlockSpec(w_in.shape, fixed2),
            pl.BlockSpec((TM, LANES), rope_tile),
            pl.BlockSpec((TM, LANES), rope_tile),
        ],
        out_specs=(pl.BlockSpec((TM, D_MODEL), tile), pl.BlockSpec((TM, D_MODEL), tile),
                   pl.BlockSpec((nk, TM), lambda i: (0, i)),
                   pl.BlockSpec((TM, nk), tile),
                   pl.BlockSpec((TM, nk), _prompt_tile), pl.BlockSpec((TM, nk), _prompt_tile)),
        compiler_params=_params(40, "arbitrary"),
        name="attn_in_proj",
    )(x1, y_pairs, route_w, mod_prev, mod, nw, w_in, cos, sin)


def _attend(q, kt_all, v_all, mask, sink_ref, o_ref):
    nq = q.shape[0]
    group_lanes = ATT_GROUP * ATT_HEAD_DIM
    lane = lax.broadcasted_iota(jnp.int32, (nq, group_lanes), 1)
    mine = [(lane // ATT_HEAD_DIM) == g for g in range(ATT_GROUP)]
    row_head = lax.broadcasted_iota(jnp.int32, (ATT_GROUP * nq, 1), 0) // nq
    if mask is not None:
        mask = jnp.concatenate([mask] * ATT_GROUP, axis=0)
    for hk in range(ATT_KV_HEADS):
        vh = v_all[:, hk * ATT_HEAD_DIM:(hk + 1) * ATT_HEAD_DIM]
        kt = jnp.concatenate([kt_all[hk * ATT_HEAD_DIM:(hk + 1) * ATT_HEAD_DIM, :]] * ATT_GROUP, axis=0)
        vt = jnp.concatenate([vh] * ATT_GROUP, axis=1)
        qg = q[:, hk * group_lanes:(hk + 1) * group_lanes]
        q_stack = jnp.concatenate([jnp.where(mine[g], qg, jnp.zeros_like(qg)) for g in range(ATT_GROUP)], axis=0)
        s = jnp.dot(q_stack, kt, preferred_element_type=F32)
        if mask is not None:
            s = jnp.where(mask, s, -jnp.inf)
        sink = jnp.zeros((ATT_GROUP * nq, 1), F32)
        for g in range(ATT_GROUP):
            sink = jnp.where(row_head == g, sink_ref[hk * ATT_GROUP + g], sink)
        m = jnp.maximum(jnp.max(s, axis=-1, keepdims=True), sink)
        p = jnp.exp(s - m)
        denom = jnp.sum(p, axis=-1, keepdims=True) + jnp.exp(sink - m)
        o = jnp.dot(p.astype(BF16), vt, preferred_element_type=F32) / denom
        acc = jnp.where(mine[0], o[:nq], 0.0)
        for g in range(1, ATT_GROUP):
            acc = acc + jnp.where(mine[g], o[g * nq:(g + 1) * nq], 0.0)
        o_ref[:, hk * group_lanes:(hk + 1) * group_lanes] = acc.astype(BF16)


def _ctx_attn_kernel(sink_ref, q_ref, k_ref, v_ref, o_ref):
    _attend(q_ref[...], k_ref[...], v_ref[...], None, sink_ref, o_ref)


def _lat_attn_kernel(sink_ref, q_ref, kp_ref, kc_ref, kn_ref, vp_ref, vc_ref, vn_ref, ck_ref, cv_ref, o_ref):
    jb = pl.program_id(1)
    kt_all = jnp.concatenate([kp_ref[...], kc_ref[...], kn_ref[...], ck_ref[0].astype(BF16)], axis=1)
    v_all = jnp.concatenate([vp_ref[...], vc_ref[...], vn_ref[...], cv_ref[0].astype(BF16)], axis=0)
    nkeys = 3 * BLOCK + PAST_LEN
    qi = lax.broadcasted_iota(jnp.int32, (BLOCK, nkeys), 0)
    kj = lax.broadcasted_iota(jnp.int32, (BLOCK, nkeys), 1)
    qpos = jb * BLOCK + qi
    kpos = (jb - 1) * BLOCK + kj
    local_ok = (jnp.abs(qpos - kpos) <= WINDOW) & (kpos >= 0) & (kpos < DEC_SEQ)
    mask = (kj >= 3 * BLOCK) | local_ok
    _attend(q_ref[...], kt_all, v_all, mask, sink_ref, o_ref)


def _attention(q, kt, v, cache_kt, cache_v, sink):
    nk = ATT_KV_HEADS * ATT_HEAD_DIM
    smem = pl.BlockSpec(memory_space=pltpu.SMEM)
    ctx = pl.pallas_call(
        _ctx_attn_kernel,
        out_shape=jax.ShapeDtypeStruct((N_PROMPT, D_MODEL), BF16),
        grid=(BATCH,),
        in_specs=[
            smem,
            pl.BlockSpec((SEQ, D_MODEL), lambda b: (b, 0)),
            pl.BlockSpec((nk, SEQ), lambda b: (0, b)),
            pl.BlockSpec((SEQ, nk), lambda b: (b, 0)),
        ],
        out_specs=pl.BlockSpec((SEQ, D_MODEL), lambda b: (b, 0)),
        compiler_params=_params(40, "arbitrary"),
        name="context_attention",
    )(sink, q, kt, v)

    nb = DEC_SEQ // BLOCK
    base = N_PROMPT // BLOCK
    cur = lambda b, j: (base + b * nb + j, 0)
    prev = lambda b, j: (base + b * nb + jnp.maximum(j - 1, 0), 0)
    nxt = lambda b, j: (base + b * nb + jnp.minimum(j + 1, nb - 1), 0)
    kv_blk = lambda f: pl.BlockSpec((BLOCK, nk), f)
    kt_blk = lambda f: pl.BlockSpec((nk, BLOCK), lambda b, j: f(b, j)[::-1])
    cache_blk = pl.BlockSpec((1, PAST_LEN, nk), lambda b, j: (b, 0, 0))
    cache_kt_blk = pl.BlockSpec((1, nk, PAST_LEN), lambda b, j: (b, 0, 0))
    lat = pl.pallas_call(
        _lat_attn_kernel,
        out_shape=jax.ShapeDtypeStruct((DEC_BATCH * DEC_SEQ, D_MODEL), BF16),
        grid=(DEC_BATCH, nb),
        in_specs=[
            smem,
            pl.BlockSpec((BLOCK, D_MODEL), cur),
            kt_blk(prev), kt_blk(cur), kt_blk(nxt),
            kv_blk(prev), kv_blk(cur), kv_blk(nxt),
            cache_kt_blk, cache_blk,
        ],
        out_specs=pl.BlockSpec((BLOCK, D_MODEL), lambda b, j: (b * nb + j, 0)),
        compiler_params=_params(40, "arbitrary", "arbitrary"),
        name="latent_attention",
    )(sink, q, kt, kt, kt, v, v, v, cache_kt, cache_v)
    return ctx, lat


def kernel(x_prompt, x_sample, state_hgrn, cache_k, cache_v, c, c_ctx, ada_w, ada_b, norm_w, hg_w_in,
           hg_lb_logits, hg_onorm, hg_w_out, at_w_in, at_sink, at_w_out, moe_w_group, moe_b_group,
           moe_w_expert, moe_b_expert, moe_w1, moe_w3, moe_w2, final_norm):
    xp = x_prompt.reshape(N_PROMPT, D_MODEL)
    xs = x_sample.reshape(N_TOK - N_PROMPT, D_MODEL)
    cond = jnp.concatenate([c_ctx[None, :], c], axis=0)
    mod = _ada(cond, ada_w, ada_b)
    nk = ATT_KV_HEADS * ATT_HEAD_DIM

    def router_params(i):
        pad = jnp.zeros((D_MODEL, ROUTER_LANES - N_EXPERTS - N_GROUPS), F32)
        w = jnp.concatenate([moe_w_expert[i], moe_w_group[i], pad], axis=1)
        b = jnp.concatenate([moe_b_expert[i], moe_b_group[i], pad[0]])[None, :]
        hi = w.astype(BF16)
        lo = (w - hi.astype(F32)).astype(BF16)
        return jnp.stack([hi, lo]), b

    tile = lambda i: (i, 0)
    blk = pl.BlockSpec((TM, D_MODEL), tile)

    prompt_blk = pl.BlockSpec((TM, D_MODEL), _prompt_tile)
    sample_blk = pl.BlockSpec((TM, D_MODEL), _sample_tile)
    q, v, g, lff, kf, lfb, kb = _hg_in(xp, xs, mod[0], norm_w[0, 0][None, :], hg_w_in[0].astype(BF16), hg_lb_logits)
    o_f, o_b, state_new = _gla(q, v, lff, kf, lfb, kb, state_hgrn)
    wr, br = router_params(0)
    routed = _mixer_out(
        _hg_out_kernel, "hgrn_out_route", (o_f, o_b, g, hg_onorm[0][None, :], xp, xs),
        (blk, blk, blk, pl.BlockSpec((1, HG_DK), lambda i: (0, 0)), prompt_blk, sample_blk),
        mod[0], norm_w[0, 1][None, :], hg_w_out[0].astype(BF16), wr, br)
    moe_out = _moe_layer(*routed, 0, moe_w1, moe_w3, moe_w2)

    cos, sin = _rope_tables()
    x, qa, ka, va, k_ctx, v_ctx = _at_in(*moe_out, mod[0], mod[1], norm_w[1, 0][None, :],
                                         at_w_in[0].astype(BF16), cos, sin)
    attn_ctx, attn_lat = _attention(qa, ka, va, cache_k[:, 0].reshape(DEC_BATCH, PAST_LEN, nk).transpose(0, 2, 1),
                                    cache_v[:, 0].reshape(DEC_BATCH, PAST_LEN, nk), at_sink[0])
    wr, br = router_params(1)
    routed = _mixer_out(
        _at_out_kernel, "attn_out_route", (attn_ctx, attn_lat, x), (prompt_blk, sample_blk, blk),
        mod[1], norm_w[1, 1][None, :], at_w_out[0].astype(BF16), wr, br)
    y_prompt, y_sample = _moe_combine(*_moe_layer(*routed, 1, moe_w1, moe_w3, moe_w2), mod[1], final_norm[None, :])

    def cache(feature_major):
        return feature_major.reshape(BATCH, 1, ATT_KV_HEADS, ATT_HEAD_DIM, SEQ).transpose(0, 1, 4, 2, 3)

    return (y_prompt.reshape(BATCH, SEQ, D_MODEL), y_sample.reshape(DEC_BATCH, DEC_SEQ, D_MODEL), state_new,
            cache(k_ctx), cache(v_ctx))
```

```python
import functools
from typing import Any, NamedTuple

import jax
import jax.numpy as jnp
import numpy as np
from jax import lax
from jax.experimental import pallas as pl
from jax.experimental.pallas import tpu as pltpu
from jax.experimental.pallas import tpu_sc as plsc

F32 = jnp.float32
BF16 = jnp.bfloat16

D_MODEL = 1024
BATCH = 16
SEQ = 256
DEC_BATCH = 2
DEC_SEQ = 1024
PAST_LEN = 512
GRID_W = 64
HG_HEADS = 8
HG_DK = 128
CHUNK = 16
ATT_HEAD_DIM = 64
ATT_Q_HEADS = 16
ATT_KV_HEADS = 4
ATT_GROUP = 4
WINDOW = 128
BLOCK = 128
ROPE_HALF = 32
ROPE_BASE = 10000.0
N_GROUPS = 4
EXPERTS_PER_GROUP = 8
N_EXPERTS = 32
D_EXPERT = 256
EPS = 1e-6

N_PROMPT = BATCH * SEQ
N_TOK = N_PROMPT + DEC_BATCH * DEC_SEQ
TM = 256
N_TILES = N_TOK // TM
PROMPT_TILES = N_PROMPT // TM
TILES_PER_DEC_SEQ = DEC_SEQ // TM
LANES = 128
SUBLANES = 8
ADA_LANE_BLOCKS = 4
N_COND_USED = 1 + DEC_BATCH
N_COND = 8
ROUTER_LANES = 128
ROUTE_TM = 1024
ROUTE_ROWS = 8
HG_IN_TM = 256
DECAY_CLAMP = 60.0
PACK_WORDS = D_MODEL // 4
ROW_WORDS = 2 * PACK_WORDS
MOE_TILE = 256
MOE_STEP_TILES = 4
MOE_WEIGHT_SLOTS = 2 * MOE_STEP_TILES
MOE_ROWS = 2 * N_TOK + N_EXPERTS * MOE_TILE
MIB = 1024 * 1024


def _params(vmem_mib, *semantics):
    return pltpu.CompilerParams(dimension_semantics=semantics, vmem_limit_bytes=vmem_mib * MIB)


def _tile_cond(i):
    return jnp.where(i < PROMPT_TILES, 0, 1 + (i - PROMPT_TILES) // TILES_PER_DEC_SEQ)


def _prompt_tile(i):
    return (jnp.minimum(i, PROMPT_TILES - 1), 0)


def _sample_tile(i):
    return (jnp.maximum(i - PROMPT_TILES, 0), 0)


def _stream_tile(prompt_ref, sample_ref):
    return jnp.where(pl.program_id(0) < PROMPT_TILES, prompt_ref[...], sample_ref[...])


def _mod_row(mod_ref, cond, which):
    return mod_ref[pl.ds(cond, 1), which * D_MODEL:(which + 1) * D_MODEL]


def _norm_mod(x, nw, shift, scale):
    y = x * lax.rsqrt(jnp.mean(x * x, axis=-1, keepdims=True) + EPS)
    return (y * nw) * (1.0 + scale) + shift


def _silu(x):
    return x * jax.nn.sigmoid(x)


def _ada_kernel(c_ref, w_ref, b_ref, o_ref, s_scr):
    @pl.when((pl.program_id(0) == 0) & (pl.program_id(1) == 0))
    def _():
        s_scr[...] = _silu(c_ref[...])

    tn = w_ref.shape[-1]
    cols = [[] for _ in range(N_COND_USED)]
    for j0 in range(0, tn // LANES, ADA_LANE_BLOCKS):
        acc = [[jnp.zeros((SUBLANES, LANES), F32) for _ in range(ADA_LANE_BLOCKS)] for _ in range(N_COND_USED)]
        for g in range(D_MODEL // SUBLANES):
            rows = slice(g * SUBLANES, (g + 1) * SUBLANES)
            w = [w_ref[0, rows, (j0 + j) * LANES:(j0 + j + 1) * LANES] for j in range(ADA_LANE_BLOCKS)]
            for r in range(N_COND_USED):
                s = s_scr[r, rows, :]
                for j in range(ADA_LANE_BLOCKS):
                    acc[r][j] = acc[r][j] + w[j] * s
        for r in range(N_COND_USED):
            cols[r] += [jnp.sum(a, axis=0, keepdims=True) for a in acc[r]]
    out = [jnp.concatenate(cols[r], axis=1) + b_ref[0] for r in range(N_COND_USED)]
    out.append(jnp.zeros((N_COND - N_COND_USED, tn), F32))
    o_ref[0] = jnp.concatenate(out, axis=0)


def _ada(cond, ada_w, ada_b):
    depth, _, n = ada_w.shape
    tn = 1536
    cond_cols = jnp.broadcast_to(cond[:, :, None], (N_COND_USED, D_MODEL, LANES))
    return pl.pallas_call(
        _ada_kernel,
        out_shape=jax.ShapeDtypeStruct((depth, N_COND, n), F32),
        grid=(depth, n // tn),
        in_specs=[
            pl.BlockSpec((N_COND_USED, D_MODEL, LANES), lambda l, j: (0, 0, 0)),
            pl.BlockSpec((1, D_MODEL, tn), lambda l, j: (l, 0, j)),
            pl.BlockSpec((1, 1, tn), lambda l, j: (l, 0, j)),
        ],
        out_specs=pl.BlockSpec((1, N_COND, tn), lambda l, j: (l, 0, j)),
        scratch_shapes=[pltpu.VMEM((N_COND_USED, D_MODEL, LANES), F32)],
        compiler_params=_params(40, "arbitrary", "arbitrary"),
        name="ada_modulation",
    )(cond_cols, ada_w, ada_b.reshape(depth, 1, n))


def _hg_in_kernel(xp_ref, xs_ref, mod_ref, nw_ref, w_ref, lbl_ref,
                  q_ref, v_ref, g_ref, lff_ref, kf_ref, lfb_ref, kb_ref):
    i = pl.program_id(0) * (HG_IN_TM // TM)
    cond = _tile_cond(i)
    x = jnp.where(i < PROMPT_TILES, xp_ref[...], xs_ref[...])
    h = _norm_mod(x, nw_ref[...], _mod_row(mod_ref, cond, 0), _mod_row(mod_ref, cond, 1)).astype(BF16)

    def proj(c):
        return jnp.dot(h, w_ref[:, c * D_MODEL:(c + 1) * D_MODEL], preferred_element_type=F32)

    l0, l1, l2 = lbl_ref[0], lbl_ref[1], lbl_ref[2]
    m = jnp.maximum(jnp.maximum(l0, l1), l2)
    e0, e1, e2 = jnp.exp(l0 - m), jnp.exp(l1 - m), jnp.exp(l2 - m)
    lb = e0 / (e0 + e1 + e2)

    q_ref[...] = proj(0).astype(BF16)
    v_ref[...] = proj(1).astype(BF16)
    for d, (lf_ref, k_ref) in enumerate(((lff_ref, kf_ref), (lfb_ref, kb_ref))):
        lbd = lb[d:d + 1, :]
        f = lbd + (1.0 - lbd) * jax.nn.sigmoid(proj(2 + d))
        lf = jnp.log(f)
        hi = lf.astype(BF16)
        lf_ref[0] = hi
        lf_ref[1] = (lf - hi.astype(F32)).astype(BF16)
        k_ref[...] = (1.0 - f).astype(BF16)
    g_ref[...] = proj(4).astype(BF16)


def _hg_in(xp, xs, mod, nw, w_in, lb_logits):
    tile = lambda i: (i, 0)
    fixed2 = lambda i: (0, 0)
    bf = jax.ShapeDtypeStruct((N_TOK, D_MODEL), BF16)
    ff = jax.ShapeDtypeStruct((2, N_TOK, D_MODEL), BF16)
    blk = pl.BlockSpec((HG_IN_TM, D_MODEL), tile)
    split_blk = pl.BlockSpec((2, HG_IN_TM, D_MODEL), lambda i: (0, i, 0))
    prompt_steps = N_PROMPT // HG_IN_TM
    return pl.pallas_call(
        _hg_in_kernel,
        out_shape=(bf, bf, bf, ff, bf, ff, bf),
        grid=(N_TOK // HG_IN_TM,),
        in_specs=[
            pl.BlockSpec((HG_IN_TM, D_MODEL), lambda i: (jnp.minimum(i, prompt_steps - 1), 0)),
            pl.BlockSpec((HG_IN_TM, D_MODEL), lambda i: (jnp.maximum(i - prompt_steps, 0), 0)),
            pl.BlockSpec(mod.shape, fixed2),
            pl.BlockSpec((1, D_MODEL), fixed2),
            pl.BlockSpec(w_in.shape, fixed2, pipeline_mode=pl.Buffered(1)),
            pl.BlockSpec(lb_logits.shape, lambda i: (0, 0, 0)),
        ],
        out_specs=(blk, blk, blk, split_blk, blk, split_blk, blk),
        compiler_params=_params(56, "arbitrary"),
        name="hgrn_in_proj",
    )(xp, xs, mod, nw, w_in, lb_logits)


def _gla_exact(reverse, q_ref, v_ref, lf_ref, k_ref, st_ref, o_ref):
    nt = (((1,), (1,)), ((), ()))
    tn = (((0,), (0,)), ((), ()))
    n_chunks = TM // CHUNK
    row_id = lax.broadcasted_iota(jnp.int32, (CHUNK, HG_DK), 0)
    order = range(CHUNK - 1, -1, -1) if reverse else range(CHUNK)

    def chunk_head(it, carry):
        step, h = it // HG_HEADS, it % HG_HEADS
        ci = (n_chunks - 1 - step) if reverse else step
        rows = pl.ds(pl.multiple_of(ci * CHUNK, CHUNK), CHUNK)
        cols = pl.ds(pl.multiple_of(h * HG_DK, HG_DK), HG_DK)
        q, k, v = q_ref[rows, cols], k_ref[rows, cols], v_ref[rows, cols]
        f = jnp.exp(lf_ref[0, rows, cols].astype(F32) + lf_ref[1, rows, cols].astype(F32))
        st = st_ref[0, h]
        o = jnp.zeros((CHUNK, HG_DK), F32)
        for t in order:
            one = row_id == t
            v_t = jnp.where(one, v, jnp.zeros_like(v))
            st = st * f[t:t + 1, :] + lax.dot_general(v_t, k, tn, preferred_element_type=F32)
            o = jnp.where(one, lax.dot_general(q, st.astype(BF16), nt, preferred_element_type=F32), o)
        st_ref[0, h] = st
        o_ref[rows, cols] = o.astype(o_ref.dtype)
        return carry

    lax.fori_loop(0, n_chunks * HG_HEADS, chunk_head, 0)


class _GlaDirection(NamedTuple):
    reverse: bool
    q: Any
    v: Any
    lf: Any
    k: Any
    o: Any
    st: Any
    sw: Any
    ut: Any
    qd: Any
    kd: Any
    ki: Any
    cd: Any
    b: Any


_NT = (((1,), (1,)), ((), ()))
_TN = (((0,), (0,)), ((), ()))
_HG_COLS = [slice(h * HG_DK, (h + 1) * HG_DK) for h in range(HG_HEADS)]
_N_CHUNKS = TM // CHUNK
_N_PAIRS = _N_CHUNKS // 2


def _gla_prepare(d):
    r = lax.broadcasted_iota(jnp.int32, (TM, TM), 0)
    c = lax.broadcasted_iota(jnp.int32, (TM, TM), 1)
    same = (r // CHUNK) == (c // CHUNK)
    tri = (same & ((c >= r) if d.reverse else (c <= r))).astype(BF16)
    b = jnp.dot(tri, d.lf[0], preferred_element_type=F32) + jnp.dot(tri, d.lf[1], preferred_element_type=F32)
    d.b[...] = b
    edge = 0 if d.reverse else CHUNK - 1
    total = b.reshape(_N_CHUNKS, CHUNK, D_MODEL)[:, edge, :]
    d.cd[...] = jnp.exp(total)
    return jnp.max(-total) <= DECAY_CLAMP


def _gla_pair_rows(d, step):
    pi = (_N_PAIRS - 1 - step) if d.reverse else step
    row0 = pi * 2 * CHUNK
    lo, hi = (pl.ds(row0, CHUNK), 2 * pi), (pl.ds(row0 + CHUNK, CHUNK), 2 * pi + 1)
    return pl.ds(row0, 2 * CHUNK), ((hi, lo) if d.reverse else (lo, hi))


def _chunk_decay(d, chunk):
    return d.cd[pl.ds(chunk, 1), :]


def _gla_decayed_operands(d, step):
    both, ((_, c_first), (_, c_second)) = _gla_pair_rows(d, step)
    b = d.b[both, :]
    ki = d.k[both, :] * jnp.exp(jnp.minimum(-b, DECAY_CLAMP)).astype(BF16)
    d.qd[both, :] = d.q[both, :] * jnp.exp(b).astype(BF16)
    d.ki[both, :] = ki
    c_lo, c_hi = (c_second, c_first) if d.reverse else (c_first, c_second)
    ends = jnp.concatenate([jnp.broadcast_to(_chunk_decay(d, c).astype(BF16), (CHUNK, D_MODEL)) for c in (c_lo, c_hi)],
                           axis=0)
    d.kd[both, :] = ki * ends


def _rows_scaled(x, scale_row, second_half):
    scale = jnp.broadcast_to(scale_row.astype(BF16), (CHUNK, x.shape[1]))
    ones = jnp.ones((CHUNK, x.shape[1]), BF16)
    return x * jnp.concatenate([ones, scale] if second_half else [scale, ones], axis=0)


def _gla_key_value_product(d, step):
    _gla_decayed_operands(d, step)
    both, (_, (_, c_second)) = _gla_pair_rows(d, step)
    decay_second = _chunk_decay(d, c_second)
    for h, cols in enumerate(_HG_COLS):
        keys = _rows_scaled(d.kd[both, cols], decay_second[:, cols], second_half=d.reverse)
        d.ut[h] = lax.dot_general(d.v[both, cols], keys, _TN, preferred_element_type=F32)


def _gla_start(d):
    for h in range(HG_HEADS):
        d.sw[0, h] = d.st[0, h].T.astype(BF16)
    _gla_key_value_product(d, 0)


def _gla_pair(d, step, src, dst):
    sr = lax.broadcasted_iota(jnp.int32, (CHUNK, CHUNK), 0)
    sc = lax.broadcasted_iota(jnp.int32, (CHUNK, CHUNK), 1)
    tr = lax.broadcasted_iota(jnp.int32, (CHUNK, 2 * CHUNK), 0)
    tc = lax.broadcasted_iota(jnp.int32, (CHUNK, 2 * CHUNK), 1)
    if d.reverse:
        keep_first = sc >= sr
        keep_second = (tc >= tr) | (tc >= CHUNK)
    else:
        keep_first = sc <= sr
        keep_second = (tc < CHUNK) | (tc - CHUNK <= tr)
    both, ((first, c_first), (second, c_second)) = _gla_pair_rows(d, step)
    decay_first = _chunk_decay(d, c_first)
    decay_both = decay_first * _chunk_decay(d, c_second)
    for h, cols in enumerate(_HG_COLS):
        new = d.st[src, h] * decay_both[:, cols] + d.ut[h]
        d.st[dst, h] = new
        d.sw[dst, h] = new.T.astype(BF16)
    a_first = [lax.dot_general(d.qd[first, cols], d.ki[first, cols], _NT, preferred_element_type=F32)
               for cols in _HG_COLS]
    a_second = []
    for cols in _HG_COLS:
        lo_keys = (d.ki if d.reverse else d.kd)[pl.ds(both.start, CHUNK), cols]
        hi_keys = (d.kd if d.reverse else d.ki)[pl.ds(both.start + CHUNK, CHUNK), cols]
        keys = jnp.concatenate([lo_keys, hi_keys], axis=0)
        a_second.append(lax.dot_general(d.qd[second, cols], keys, _NT, preferred_element_type=F32))
    inter = [jnp.dot(_rows_scaled(d.qd[both, cols], decay_first[:, cols], second_half=not d.reverse),
                     d.sw[src, h], preferred_element_type=F32) for h, cols in enumerate(_HG_COLS)]
    if step + 1 < _N_PAIRS:
        _gla_key_value_product(d, step + 1)
    first_half, second_half = (slice(CHUNK, None), slice(None, CHUNK)) if d.reverse else \
                              (slice(None, CHUNK), slice(CHUNK, None))
    for h, cols in enumerate(_HG_COLS):
        am = jnp.where(keep_first, a_first[h], 0.0).astype(BF16)
        d.o[first, cols] = (jnp.dot(am, d.v[first, cols], preferred_element_type=F32)
                            + inter[h][first_half]).astype(d.o.dtype)
        am = jnp.where(keep_second, a_second[h], 0.0).astype(BF16)
        d.o[second, cols] = (jnp.dot(am, d.v[both, cols], preferred_element_type=F32)
                             + inter[h][second_half]).astype(d.o.dtype)


def _gla_kernel(qf_ref, vf_ref, lff_ref, kf_ref, qb_ref, vb_ref, lfb_ref, kb_ref, s0_ref,
                of_ref, ob_ref, sout_ref, st_scr, sw_scr, ut_scr, qd_scr, kd_scr, ki_scr, cd_scr, b_scr):
    i = pl.program_id(0)
    is_prompt = i < PROMPT_TILES
    first = jnp.logical_or(is_prompt, (i - PROMPT_TILES) % TILES_PER_DEC_SEQ == 0)
    work = [tuple(scr.at[n] for scr in (st_scr, sw_scr, ut_scr, qd_scr, kd_scr, ki_scr, cd_scr, b_scr))
            for n in range(2)]
    fwd = _GlaDirection(False, qf_ref, vf_ref, lff_ref, kf_ref, of_ref, *work[0])
    bwd = _GlaDirection(True, qb_ref, vb_ref, lfb_ref, kb_ref, ob_ref, *work[1])

    @pl.when(is_prompt)
    def _():
        for d in (fwd, bwd):
            d.st[0] = jnp.zeros((HG_HEADS, HG_DK, HG_DK), F32)

    @pl.when(jnp.logical_and(first, jnp.logical_not(is_prompt)))
    def _():
        for n, d in enumerate((fwd, bwd)):
            for h in range(HG_HEADS):
                d.st[0, h] = s0_ref[0, 0, n, h].T

    exact_scores = jnp.logical_and(_gla_prepare(fwd), _gla_prepare(bwd))

    @pl.when(exact_scores)
    def _():
        _gla_start(fwd)
        _gla_start(bwd)

        for step in range(_N_PAIRS):
            for d in (fwd, bwd):
                _gla_pair(d, step, step % 2, 1 - step % 2)

    @pl.when(jnp.logical_not(exact_scores))
    def _():
        for d in (fwd, bwd):
            _gla_exact(d.reverse, d.q, d.v, d.lf, d.k, d.st, d.o)

    @pl.when(is_prompt)
    def _():
        for n, d in enumerate((fwd, bwd)):
            for h in range(HG_HEADS):
                sout_ref[0, 0, n, h] = d.st[0, h].T


def _gla(q, v, lff, kf, lfb, kb, state_hgrn):
    def fwd_tile(i):
        return (i, 0)

    def bwd_tile(i):
        j = (i - PROMPT_TILES) % TILES_PER_DEC_SEQ
        return (jnp.where(i < PROMPT_TILES, i, i - j + (TILES_PER_DEC_SEQ - 1 - j)), 0)

    def s0_idx(i):
        return (jnp.maximum(i - PROMPT_TILES, 0) // TILES_PER_DEC_SEQ, 0, 0, 0, 0, 0)

    def sout_idx(i):
        return (jnp.minimum(i, PROMPT_TILES - 1), 0, 0, 0, 0, 0)

    f_blk = pl.BlockSpec((TM, D_MODEL), fwd_tile)
    b_blk = pl.BlockSpec((TM, D_MODEL), bwd_tile)
    f_split = pl.BlockSpec((2, TM, D_MODEL), lambda i: (0,) + fwd_tile(i))
    b_split = pl.BlockSpec((2, TM, D_MODEL), lambda i: (0,) + bwd_tile(i))
    st_blk = (1, 1, 2, HG_HEADS, HG_DK, HG_DK)
    return pl.pallas_call(
        _gla_kernel,
        out_shape=(
            jax.ShapeDtypeStruct((N_TOK, D_MODEL), BF16),
            jax.ShapeDtypeStruct((N_TOK, D_MODEL), BF16),
            jax.ShapeDtypeStruct((BATCH,) + st_blk[1:], F32),
        ),
        grid=(N_TILES,),
        in_specs=[f_blk, f_blk, f_split, f_blk, b_blk, b_blk, b_split, b_blk, pl.BlockSpec(st_blk, s0_idx)],
        out_specs=(f_blk, b_blk, pl.BlockSpec(st_blk, sout_idx)),
        scratch_shapes=[
            pltpu.VMEM((2, 2, HG_HEADS, HG_DK, HG_DK), F32),
            pltpu.VMEM((2, 2, HG_HEADS, HG_DK, HG_DK), BF16),
            pltpu.VMEM((2, HG_HEADS, HG_DK, HG_DK), F32),
            pltpu.VMEM((2, TM, D_MODEL), BF16),
            pltpu.VMEM((2, TM, D_MODEL), BF16),
            pltpu.VMEM((2, TM, D_MODEL), BF16),
            pltpu.VMEM((2, TM // CHUNK, D_MODEL), F32),
            pltpu.VMEM((2, TM, D_MODEL), F32),
        ],
        compiler_params=_params(48, "arbitrary"),
        name="hgrn_recurrence",
    )(q, v, lff, kf, q, v, lfb, kb, state_hgrn)


def _route(logits):
    lane = lax.broadcasted_iota(jnp.int32, logits.shape, 1)
    neg = jnp.float32(-jnp.inf)
    big = jnp.int32(ROUTER_LANES)

    def first_max(x):
        m = jnp.max(x, axis=-1, keepdims=True)
        return m, jnp.min(jnp.where(x == m, lane, big), axis=-1, keepdims=True)

    is_group = (lane >= N_EXPERTS) & (lane < N_EXPERTS + N_GROUPS)
    gl = jnp.where(is_group, logits, neg)
    gmax, g_lane = first_max(gl)
    g_sel = g_lane - N_EXPERTS
    gsum = jnp.sum(jnp.exp(gl - gmax), axis=-1, keepdims=True)
    p_g = 1.0 / gsum
    in_sel = (lane < N_EXPERTS) & ((lane // EXPERTS_PER_GROUP) == g_sel)
    m1, i1 = first_max(jnp.where(in_sel, logits, neg))
    m2, i2 = first_max(jnp.where(in_sel & (lane != i1), logits, neg))
    e2 = jnp.exp(m2 - m1)
    return i1, i2, p_g / (1.0 + e2), p_g * e2 / (1.0 + e2)


def _pack_rows(x):
    q = PACK_WORDS
    bits = pltpu.bitcast(x.astype(BF16).astype(F32), jnp.uint32)
    return [(bits[:, (2 + h) * q:(3 + h) * q] & jnp.uint32(0xFFFF0000)) | (bits[:, h * q:(h + 1) * q] >> 16)
            for h in range(2)]


def _unpack_rows(half0, half1):
    lo = lambda w: pltpu.bitcast(w << 16, F32).astype(BF16)
    hi = lambda w: pltpu.bitcast(w & jnp.uint32(0xFFFF0000), F32).astype(BF16)
    return [lo(half0), lo(half1), hi(half0), hi(half1)]


def _mixer_tail(mix_bf16, x, mod_ref, nw2_ref, wo_ref, wr_ref, br_ref, x1_ref, h2p_ref, lg_ref):
    cond = _tile_cond(pl.program_id(0))
    out = jnp.dot(mix_bf16, wo_ref[...], preferred_element_type=F32)
    x1 = x + _mod_row(mod_ref, cond, 2) * out
    x1_ref[...] = x1
    h2 = _norm_mod(x1, nw2_ref[...], _mod_row(mod_ref, cond, 3), _mod_row(mod_ref, cond, 4))
    h2p_ref[...] = jnp.concatenate(_pack_rows(h2), axis=1)
    h_hi = h2.astype(BF16)
    h_lo = (h2 - h_hi.astype(F32)).astype(BF16)
    lg_ref[...] = (jnp.dot(h_hi, wr_ref[0], preferred_element_type=F32)
                   + (jnp.dot(h_hi, wr_ref[1], preferred_element_type=F32)
                      + jnp.dot(h_lo, wr_ref[0], preferred_element_type=F32))) + br_ref[...]


def _route_kernel(lg_ref, rt_ref, rw_ref, cnt_ref, carry_scr):
    @pl.when(pl.program_id(0) == 0)
    def _():
        carry_scr[...] = jnp.zeros_like(carry_scr)

    logits = lg_ref[...]
    i1, i2, w1, w2 = _route(logits)
    lane = lax.broadcasted_iota(jnp.int32, logits.shape, 1)
    chosen = ((lane == i1) | (lane == i2)).astype(BF16)
    r = lax.broadcasted_iota(jnp.int32, (ROUTE_TM, ROUTE_TM), 0)
    c = lax.broadcasted_iota(jnp.int32, (ROUTE_TM, ROUTE_TM), 1)
    before = jnp.dot((c < r).astype(BF16), chosen, preferred_element_type=F32) + carry_scr[...]
    r1 = jnp.sum(jnp.where(lane == i1, before, 0.0), axis=-1, keepdims=True).astype(jnp.int32)
    r2 = jnp.sum(jnp.where(lane == i2, before, 0.0), axis=-1, keepdims=True).astype(jnp.int32)
    total = carry_scr[...] + jnp.sum(chosen.astype(F32), axis=0, keepdims=True)
    carry_scr[...] = total
    tiles = jnp.ceil(total * (1.0 / MOE_TILE))
    e_from = lax.broadcasted_iota(jnp.int32, (ROUTER_LANES, ROUTER_LANES), 0)
    e_to = lax.broadcasted_iota(jnp.int32, (ROUTER_LANES, ROUTER_LANES), 1)
    tiles_before = jnp.dot(jnp.broadcast_to(tiles, (SUBLANES, ROUTER_LANES)).astype(BF16),
                           (e_from < e_to).astype(BF16), preferred_element_type=F32)
    cnt_ref[...] = jnp.concatenate([total, tiles_before[:1] * MOE_TILE,
                                    jnp.zeros((ROUTE_ROWS - 2, ROUTER_LANES), F32)], axis=0)
    by_token = jnp.where(lane == 0, i1, jnp.where(lane == 1, i2, jnp.where(lane == 2, r1, r2)))
    rt_ref[...] = by_token.T[:ROUTE_ROWS, :]
    rw_ref[...] = jnp.where(lane == 0, w1, w2)


def _route_tokens(logits):
    blk = pl.BlockSpec((ROUTE_TM, ROUTER_LANES), lambda i: (i, 0))
    return pl.pallas_call(
        _route_kernel,
        out_shape=(
            jax.ShapeDtypeStruct((ROUTE_ROWS, N_TOK), jnp.int32),
            jax.ShapeDtypeStruct((N_TOK, ROUTER_LANES), F32),
            jax.ShapeDtypeStruct((ROUTE_ROWS, ROUTER_LANES), F32),
        ),
        grid=(N_TOK // ROUTE_TM,),
        in_specs=[blk],
        out_specs=(pl.BlockSpec((ROUTE_ROWS, ROUTE_TM), lambda i: (0, i)), blk,
                   pl.BlockSpec((ROUTE_ROWS, ROUTER_LANES), lambda i: (0, 0))),
        scratch_shapes=[pltpu.VMEM((1, ROUTER_LANES), F32)],
        compiler_params=_params(32, "arbitrary"),
        name="moe_route",
    )(logits)


def _hg_out_kernel(of_ref, ob_ref, g_ref, on_ref, xp_ref, xs_ref, *rest):
    o = of_ref[...].astype(F32) + ob_ref[...].astype(F32)
    parts = []
    for h in range(HG_HEADS):
        oh = o[:, h * HG_DK:(h + 1) * HG_DK]
        parts.append(oh * lax.rsqrt(jnp.mean(oh * oh, axis=-1, keepdims=True) + EPS) * on_ref[...])
    y = jnp.concatenate(parts, axis=1) * _silu(g_ref[...].astype(F32))
    _mixer_tail(y.astype(BF16), _stream_tile(xp_ref, xs_ref), *rest)


def _at_out_kernel(ac_ref, al_ref, x_ref, *rest):
    _mixer_tail(_stream_tile(ac_ref, al_ref), x_ref[...], *rest)


def _mixer_out(kernel_fn, name, mix_inputs, mix_specs, mod, nw2, w_out, w_router, b_router):
    tile = lambda i: (i, 0)
    fixed2 = lambda i: (0, 0)
    blk = pl.BlockSpec((TM, D_MODEL), tile)
    lanes_blk = pl.BlockSpec((TM, ROUTER_LANES), tile)
    return pl.pallas_call(
        kernel_fn,
        out_shape=(
            jax.ShapeDtypeStruct((N_TOK, D_MODEL), F32),
            jax.ShapeDtypeStruct((N_TOK, ROW_WORDS), jnp.uint32),
            jax.ShapeDtypeStruct((N_TOK, ROUTER_LANES), F32),
        ),
        grid=(N_TILES,),
        in_specs=list(mix_specs) + [
            pl.BlockSpec(mod.shape, fixed2),
            pl.BlockSpec((1, D_MODEL), fixed2),
            pl.BlockSpec(w_out.shape, fixed2),
            pl.BlockSpec(w_router.shape, lambda i: (0, 0, 0)),
            pl.BlockSpec((1, ROUTER_LANES), fixed2),
        ],
        out_specs=(blk, pl.BlockSpec((TM, ROW_WORDS), tile), lanes_blk),
        compiler_params=_params(40, "arbitrary"),
        name=name,
    )(*mix_inputs, mod, nw2, w_out, w_router, b_router)


def _moe_plan(route_t, counts):
    cnt = counts[0, :N_EXPERTS].astype(jnp.int32)
    offs = counts[1, :N_EXPERTS].astype(jnp.int32)
    ends = offs + ((cnt + MOE_TILE - 1) // MOE_TILE) * MOE_TILE
    experts = route_t[0:2]
    pos = jnp.sum(jnp.where(experts[None, :, :] == jnp.arange(N_EXPERTS)[:, None, None], offs[:, None, None], 0),
                  axis=0) + route_t[2:4]
    tile_start = jnp.arange(MOE_ROWS // MOE_TILE, dtype=jnp.int32) * MOE_TILE
    tile_expert = jnp.minimum(jnp.sum(ends[None, :] <= tile_start[:, None], axis=1), N_EXPERTS - 1).astype(jnp.int32)
    of_tile = tile_expert[:, None] == jnp.arange(N_EXPERTS)[None, :]
    tile_offs = jnp.sum(jnp.where(of_tile, offs[None, :], 0), axis=1)
    tile_cnt = jnp.sum(jnp.where(of_tile, cnt[None, :], 0), axis=1)
    tile_rows = jnp.clip(tile_offs + tile_cnt - tile_start, 0, MOE_TILE).astype(jnp.int32)
    active = tile_start < ends[-1]
    tile_opens = jnp.where(active & (tile_start == tile_offs), tile_expert, -1).astype(jnp.int32)
    tile_opens = jnp.concatenate([tile_opens, jnp.full((MOE_STEP_TILES,), -1, jnp.int32)])
    n_active = (ends[-1] // MOE_TILE).astype(jnp.int32).reshape(1)
    slot = (jnp.cumsum((cnt > 0).astype(jnp.int32)) - 1) % MOE_WEIGHT_SLOTS
    tile_slot = jnp.sum(jnp.where(of_tile, slot[None, :], 0), axis=1)
    last_slot = jnp.sum(jnp.where(tile_start == ends[-1] - MOE_TILE, tile_slot, 0))
    tile_slot = jnp.where(active, tile_slot, last_slot).astype(jnp.int32)
    return pos.astype(jnp.int32), (tile_rows, tile_slot, tile_opens, n_active)


SC_WINDOW = 64


def _sc_mesh():
    return plsc.VectorSubcoreMesh(core_axis_name="c", subcore_axis_name="s")


def _sc_scatter_rows(x, idx_a, idx_b, n_out_rows):
    n = x.shape[0]

    @functools.partial(pl.kernel, out_type=jax.ShapeDtypeStruct((n_out_rows, ROW_WORDS), x.dtype), mesh=_sc_mesh(),
                       scratch_types=[pltpu.SemaphoreType.DMA, pltpu.SemaphoreType.DMA])
    def scatter(x_hbm, ia_hbm, ib_hbm, o_hbm, sem_a, sem_b):
        def body(x_vmem, ia_vmem, ib_vmem):
            copy_a = pltpu.async_copy(x_vmem, o_hbm.at[ia_vmem.at[0]], sem_a)
            copy_b = pltpu.async_copy(x_vmem, o_hbm.at[ib_vmem.at[0]], sem_b)
            copy_a.wait()
            copy_b.wait()

        idx_spec = pl.BlockSpec((1, SC_WINDOW), index_map=lambda i: (i, 0))
        pltpu.emit_pipeline(
            body, grid=(n // SC_WINDOW,),
            in_specs=[pl.BlockSpec((SC_WINDOW, ROW_WORDS), index_map=lambda i: (i, 0)), idx_spec, idx_spec],
            out_specs=[],
            core_axis_name=("c", "s"), dimension_semantics=(pltpu.PARALLEL,),
        )(x_hbm, ia_hbm, ib_hbm)

    return scatter(x, idx_a, idx_b)


def _sc_gather_rows(table, idx):
    n = idx.size

    @functools.partial(pl.kernel, out_type=jax.ShapeDtypeStruct((n, ROW_WORDS), table.dtype), mesh=_sc_mesh())
    def gather(t_hbm, i_hbm, o_hbm):
        def body(i_vmem, o_vmem):
            pltpu.sync_copy(t_hbm.at[i_vmem.at[0]], o_vmem)

        pltpu.emit_pipeline(
            body, grid=(n // SC_WINDOW,),
            in_specs=[pl.BlockSpec((1, SC_WINDOW), index_map=lambda i: (i, 0))],
            out_specs=[pl.BlockSpec((SC_WINDOW, ROW_WORDS), index_map=lambda i: (i, 0))],
            core_axis_name=("c", "s"), dimension_semantics=(pltpu.PARALLEL,),
        )(i_hbm, o_hbm)

    return gather(table, idx)


def _ffn_kernel(tr_ref, ts_ref, to_ref, na_ref, xs_ref, w1_hbm, w3_hbm, w2_hbm, ys_ref,
                w1_buf, w3_buf, w2_buf, sem, *, layer):
    def weight_copies(t):
        pairs = ((w1_hbm, w1_buf), (w3_hbm, w3_buf), (w2_hbm, w2_buf))
        return [pltpu.make_async_copy(hbm.at[layer, to_ref[t]], buf.at[ts_ref[t]], sem.at[ts_ref[t], j])
                for j, (hbm, buf) in enumerate(pairs)]

    def for_experts_opened_from(first_tile, action):
        for j in range(MOE_STEP_TILES):
            t = first_tile + j

            @pl.when(to_ref[t] >= 0)
            def _():
                for copy in weight_copies(t):
                    action(copy)

    def tile(t, rows):
        slot = ts_ref[t]
        row = lax.broadcasted_iota(jnp.int32, (MOE_TILE, PACK_WORDS), 0)
        live = row < tr_ref[t]
        halves = [jnp.where(live, xs_ref[rows, h * PACK_WORDS:(h + 1) * PACK_WORDS],
                            jnp.zeros((MOE_TILE, PACK_WORDS), jnp.uint32)) for h in range(2)]
        chunks = _unpack_rows(*halves)

        def up(w_buf):
            acc = None
            for k, chunk in enumerate(chunks):
                w = w_buf[slot, k * PACK_WORDS:(k + 1) * PACK_WORDS, :].astype(BF16)
                part = jnp.dot(chunk, w, preferred_element_type=F32)
                acc = part if acc is None else acc + part
            return acc

        hid = (_silu(up(w1_buf)) * up(w3_buf)).astype(BF16)
        y = jnp.dot(hid, w2_buf[slot].astype(BF16), preferred_element_type=F32)
        ys_ref[rows, :] = jnp.concatenate(_pack_rows(y), axis=1)

    step = pl.program_id(0)
    first_tile = step * MOE_STEP_TILES

    @pl.when(first_tile < na_ref[0])
    def _():
        @pl.when(step == 0)
        def _():
            for_experts_opened_from(first_tile, lambda copy: copy.start())

        for_experts_opened_from(first_tile + MOE_STEP_TILES, lambda copy: copy.start())
        for_experts_opened_from(first_tile, lambda copy: copy.wait())
        for j in range(MOE_STEP_TILES):
            tile(first_tile + j, slice(j * MOE_TILE, (j + 1) * MOE_TILE))


def _ffn(xs, tables, layer, w1, w3, w2):
    step_rows = MOE_STEP_TILES * MOE_TILE
    row_tile = lambda s, tr, ts, to, na: (jnp.minimum(s, (na[0] - 1) // MOE_STEP_TILES), 0)
    hbm = pl.BlockSpec(memory_space=pl.ANY)
    return pl.pallas_call(
        functools.partial(_ffn_kernel, layer=layer),
        out_shape=jax.ShapeDtypeStruct((MOE_ROWS, ROW_WORDS), jnp.uint32),
        grid_spec=pltpu.PrefetchScalarGridSpec(
            num_scalar_prefetch=len(tables),
            grid=(MOE_ROWS // step_rows,),
            in_specs=[pl.BlockSpec((step_rows, ROW_WORDS), row_tile), hbm, hbm, hbm],
            out_specs=pl.BlockSpec((step_rows, ROW_WORDS), row_tile),
            scratch_shapes=[
                pltpu.VMEM((MOE_WEIGHT_SLOTS, D_MODEL, D_EXPERT), F32),
                pltpu.VMEM((MOE_WEIGHT_SLOTS, D_MODEL, D_EXPERT), F32),
                pltpu.VMEM((MOE_WEIGHT_SLOTS, D_EXPERT, D_MODEL), F32),
                pltpu.SemaphoreType.DMA((MOE_WEIGHT_SLOTS, 3)),
            ],
        ),
        compiler_params=_params(48, "arbitrary"),
        name="moe_experts",
    )(*tables, xs, w1, w3, w2)


def _moe_res(x_ref, y_ref, rw_ref, mod_ref):
    cond = _tile_cond(pl.program_id(0))
    rw = rw_ref[...]
    wa = rw[:, 0:1]
    wb = rw[:, 1:2]
    ya = _unpack_rows(y_ref[0, :, :PACK_WORDS], y_ref[0, :, PACK_WORDS:])
    yb = _unpack_rows(y_ref[1, :, :PACK_WORDS], y_ref[1, :, PACK_WORDS:])
    y = jnp.concatenate([wa * a.astype(F32) + wb * b.astype(F32) for a, b in zip(ya, yb)], axis=1)
    return x_ref[...] + _mod_row(mod_ref, cond, 5) * y


def _moe_res_final_kernel(x_ref, y_ref, rw_ref, mod_ref, fn_ref, op_ref, os_ref):
    x = _moe_res(x_ref, y_ref, rw_ref, mod_ref)
    y = x * lax.rsqrt(jnp.mean(x * x, axis=-1, keepdims=True) + EPS) * fn_ref[...]
    is_prompt = pl.program_id(0) < PROMPT_TILES

    @pl.when(is_prompt)
    def _():
        op_ref[...] = y

    @pl.when(jnp.logical_not(is_prompt))
    def _():
        os_ref[...] = y


def _moe_combine(x1, y_pairs, route_w, mod, final_norm):
    tile = lambda i: (i, 0)
    fixed2 = lambda i: (0, 0)
    return pl.pallas_call(
        _moe_res_final_kernel,
        out_shape=(jax.ShapeDtypeStruct((N_PROMPT, D_MODEL), F32),
                   jax.ShapeDtypeStruct((N_TOK - N_PROMPT, D_MODEL), F32)),
        grid=(N_TILES,),
        in_specs=[pl.BlockSpec((TM, D_MODEL), tile),
                  pl.BlockSpec((2, TM, ROW_WORDS), lambda i: (0, i, 0)),
                  pl.BlockSpec((TM, ROUTER_LANES), tile),
                  pl.BlockSpec(mod.shape, fixed2),
                  pl.BlockSpec((1, D_MODEL), fixed2)],
        out_specs=(pl.BlockSpec((TM, D_MODEL), _prompt_tile), pl.BlockSpec((TM, D_MODEL), _sample_tile)),
        compiler_params=_params(32, "arbitrary"),
        name="moe_combine",
    )(x1, y_pairs, route_w, mod, final_norm)


def _moe_layer(x1, h2p, logits, layer, w1, w3, w2):
    route_t, route_w, counts = _route_tokens(logits)
    pos, tables = _moe_plan(route_t, counts)
    windows = pos.reshape(2, N_TOK // SC_WINDOW, SC_WINDOW)
    xs = _sc_scatter_rows(h2p, windows[0], windows[1], MOE_ROWS)
    ys = _ffn(xs, tables, layer, w1, w3, w2)
    y_pairs = _sc_gather_rows(ys, windows.reshape(2 * N_TOK // SC_WINDOW, SC_WINDOW))
    return x1, y_pairs.reshape(2, N_TOK, ROW_WORDS), route_w


def _swap_rotary_halves(x):
    n = x.shape[-1]
    lane = lax.broadcasted_iota(jnp.int32, x.shape, 1)
    quarter = ROPE_HALF // 2
    return jnp.where((lane % ROPE_HALF) < quarter, pltpu.roll(x, n - quarter, 1), pltpu.roll(x, quarter, 1))


def _at_in_kernel(x1_ref, y_ref, rw_ref, mod_prev_ref, mod_ref, nw_ref, w_ref, cos_ref, sin_ref,
                  x_ref, q_ref, kt_ref, v_ref, kc_ref, vc_ref):
    i = pl.program_id(0)
    cond = _tile_cond(i)
    x = _moe_res(x1_ref, y_ref, rw_ref, mod_prev_ref)
    x_ref[...] = x
    h = _norm_mod(x, nw_ref[...], _mod_row(mod_ref, cond, 0), _mod_row(mod_ref, cond, 1)).astype(BF16)
    nq = ATT_Q_HEADS * ATT_HEAD_DIM
    nk = ATT_KV_HEADS * ATT_HEAD_DIM
    cos = cos_ref[...]
    sin = sin_ref[...]

    def rope(x):
        reps = x.shape[-1] // LANES
        return x * jnp.concatenate([cos] * reps, axis=1) + _swap_rotary_halves(x) * jnp.concatenate([sin] * reps, axis=1)

    q_ref[...] = (rope(jnp.dot(h, w_ref[:, :nq], preferred_element_type=F32)) * ATT_HEAD_DIM ** -0.5).astype(BF16)
    k = rope(jnp.dot(h, w_ref[:, nq:nq + nk], preferred_element_type=F32))
    v = jnp.dot(h, w_ref[:, nq + nk:], preferred_element_type=F32)
    kt = k.T
    kt_ref[...] = kt.astype(BF16)
    v_ref[...] = v.astype(BF16)

    @pl.when(i < PROMPT_TILES)
    def _():
        kc_ref[...] = kt
        vc_ref[...] = v.T


def _rope_tables():
    f32 = np.float32
    pos = np.arange(DEC_SEQ)
    t_row = (pos // GRID_W).astype(f32)
    t_col = (pos % GRID_W).astype(f32)
    inv = f32(ROPE_BASE) ** (-np.arange(0, ROPE_HALF, 2, dtype=f32) / f32(ROPE_HALF))
    j = np.arange(LANES) % ATT_HEAD_DIM
    freq = inv[(j % ROPE_HALF) % (ROPE_HALF // 2)]
    ang = (np.where((j < ROPE_HALF)[None, :], t_row[:, None], t_col[:, None]) * freq[None, :]).astype(f32)
    sign = np.where((j % ROPE_HALF) < ROPE_HALF // 2, -1.0, 1.0).astype(f32)
    cos = np.concatenate([np.ones((TM, LANES), f32), np.cos(ang)], axis=0)
    sin = np.concatenate([np.zeros((TM, LANES), f32), np.sin(ang) * sign[None, :]], axis=0)
    return jnp.asarray(cos, F32), jnp.asarray(sin, F32)


def _at_in(x1, y_pairs, route_w, mod_prev, mod, nw, w_in, cos, sin):
    tile = lambda i: (i, 0)
    fixed2 = lambda i: (0, 0)
    rope_tile = lambda i: (jnp.where(i < PROMPT_TILES, 0, 1 + (i - PROMPT_TILES) % TILES_PER_DEC_SEQ), 0)
    nk = ATT_KV_HEADS * ATT_HEAD_DIM
    return pl.pallas_call(
        _at_in_kernel,
        out_shape=(
            jax.ShapeDtypeStruct((N_TOK, D_MODEL), F32),
            jax.ShapeDtypeStruct((N_TOK, D_MODEL), BF16),
            jax.ShapeDtypeStruct((nk, N_TOK), BF16),
            jax.ShapeDtypeStruct((N_TOK, nk), BF16),
            jax.ShapeDtypeStruct((N_PROMPT, nk), F32),
            jax.ShapeDtypeStruct((N_PROMPT, nk), F32),
        ),
        grid=(N_TILES,),
        in_specs=[
            pl.BlockSpec((TM, D_MODEL), tile),
            pl.BlockSpec((2, TM, ROW_WORDS), lambda i: (0, i, 0)),
            pl.BlockSpec((TM, ROUTER_LANES), tile),
            pl.BlockSpec(mod_prev.shape, fixed2),
            pl.BlockSpec(mod.shape, fixed2),
            pl.BlockSpec((1, D_MODEL), fixed2),
            pl.BlockSpec(w_in.shape, fixed2),
            pl.BlockSpec((TM, LANES), rope_tile),
            pl.BlockSpec((TM, LANES), rope_tile),
        ],
        out_specs=(pl.BlockSpec((TM, D_MODEL), tile), pl.BlockSpec((TM, D_MODEL), tile),
                   pl.BlockSpec((nk, TM), lambda i: (0, i)),
                   pl.BlockSpec((TM, nk), tile),
                   pl.BlockSpec((TM, nk), _prompt_tile), pl.BlockSpec((TM, nk), _prompt_tile)),
        compiler_params=_params(40, "arbitrary"),
        name="attn_in_proj",
    )(x1, y_pairs, route_w, mod_prev, mod, nw, w_in, cos, sin)


def _attend(q, kt_all, v_all, mask, sink_ref, o_ref):
    nq = q.shape[0]
    group_lanes = ATT_GROUP * ATT_HEAD_DIM
    lane = lax.broadcasted_iota(jnp.int32, (nq, group_lanes), 1)
    mine = [(lane // ATT_HEAD_DIM) == g for g in range(ATT_GROUP)]
    row_head = lax.broadcasted_iota(jnp.int32, (ATT_GROUP * nq, 1), 0) // nq
    if mask is not None:
        mask = jnp.concatenate([mask] * ATT_GROUP, axis=0)
    for hk in range(ATT_KV_HEADS):
        vh = v_all[:, hk * ATT_HEAD_DIM:(hk + 1) * ATT_HEAD_DIM]
        kt = jnp.concatenate([kt_all[hk * ATT_HEAD_DIM:(hk + 1) * ATT_HEAD_DIM, :]] * ATT_GROUP, axis=0)
        vt = jnp.concatenate([vh] * ATT_GROUP, axis=1)
        qg = q[:, hk * group_lanes:(hk + 1) * group_lanes]
        q_stack = jnp.concatenate([jnp.where(mine[g], qg, jnp.zeros_like(qg)) for g in range(ATT_GROUP)], axis=0)
        s = jnp.dot(q_stack, kt, preferred_element_type=F32)
        if mask is not None:
            s = jnp.where(mask, s, -jnp.inf)
        sink = jnp.zeros((ATT_GROUP * nq, 1), F32)
        for g in range(ATT_GROUP):
            sink = jnp.where(row_head == g, sink_ref[hk * ATT_GROUP + g], sink)
        m = jnp.maximum(jnp.max(s, axis=-1, keepdims=True), sink)
        p = jnp.exp(s - m)
        denom = jnp.sum(p, axis=-1, keepdims=True) + jnp.exp(sink - m)
        o = jnp.dot(p.astype(BF16), vt, preferred_element_type=F32) / denom
        acc = jnp.where(mine[0], o[:nq], 0.0)
        for g in range(1, ATT_GROUP):
            acc = acc + jnp.where(mine[g], o[g * nq:(g + 1) * nq], 0.0)
        o_ref[:, hk * group_lanes:(hk + 1) * group_lanes] = acc.astype(BF16)


def _ctx_attn_kernel(sink_ref, q_ref, k_ref, v_ref, o_ref):
    _attend(q_ref[...], k_ref[...], v_ref[...], None, sink_ref, o_ref)


def _lat_attn_kernel(sink_ref, q_ref, kp_ref, kc_ref, kn_ref, vp_ref, vc_ref, vn_ref, ck_ref, cv_ref, o_ref):
    jb = pl.program_id(1)
    kt_all = jnp.concatenate([kp_ref[...], kc_ref[...], kn_ref[...], ck_ref[0].astype(BF16)], axis=1)
    v_all = jnp.concatenate([vp_ref[...], vc_ref[...], vn_ref[...], cv_ref[0].astype(BF16)], axis=0)
    nkeys = 3 * BLOCK + PAST_LEN
    qi = lax.broadcasted_iota(jnp.int32, (BLOCK, nkeys), 0)
    kj = lax.broadcasted_iota(jnp.int32, (BLOCK, nkeys), 1)
    qpos = jb * BLOCK + qi
    kpos = (jb - 1) * BLOCK + kj
    local_ok = (jnp.abs(qpos - kpos) <= WINDOW) & (kpos >= 0) & (kpos < DEC_SEQ)
    mask = (kj >= 3 * BLOCK) | local_ok
    _attend(q_ref[...], kt_all, v_all, mask, sink_ref, o_ref)


def _attention(q, kt, v, cache_kt, cache_v, sink):
    nk = ATT_KV_HEADS * ATT_HEAD_DIM
    smem = pl.BlockSpec(memory_space=pltpu.SMEM)
    ctx = pl.pallas_call(
        _ctx_attn_kernel,
        out_shape=jax.ShapeDtypeStruct((N_PROMPT, D_MODEL), BF16),
        grid=(BATCH,),
        in_specs=[
            smem,
            pl.BlockSpec((SEQ, D_MODEL), lambda b: (b, 0)),
            pl.BlockSpec((nk, SEQ), lambda b: (0, b)),
            pl.BlockSpec((SEQ, nk), lambda b: (b, 0)),
        ],
        out_specs=pl.BlockSpec((SEQ, D_MODEL), lambda b: (b, 0)),
        compiler_params=_params(40, "arbitrary"),
        name="context_attention",
    )(sink, q, kt, v)

    nb = DEC_SEQ // BLOCK
    base = N_PROMPT // BLOCK
    cur = lambda b, j: (base + b * nb + j, 0)
    prev = lambda b, j: (base + b * nb + jnp.maximum(j - 1, 0), 0)
    nxt = lambda b, j: (base + b * nb + jnp.minimum(j + 1, nb - 1), 0)
    kv_blk = lambda f: pl.BlockSpec((BLOCK, nk), f)
    kt_blk = lambda f: pl.BlockSpec((nk, BLOCK), lambda b, j: f(b, j)[::-1])
    cache_blk = pl.BlockSpec((1, PAST_LEN, nk), lambda b, j: (b, 0, 0))
    cache_kt_blk = pl.BlockSpec((1, nk, PAST_LEN), lambda b, j: (b, 0, 0))
    lat = pl.pallas_call(
        _lat_attn_kernel,
        out_shape=jax.ShapeDtypeStruct((DEC_BATCH * DEC_SEQ, D_MODEL), BF16),
        grid=(DEC_BATCH, nb),
        in_specs=[
            smem,
            pl.BlockSpec((BLOCK, D_MODEL), cur),
            kt_blk(prev), kt_blk(cur), kt_blk(nxt),
            kv_blk(prev), kv_blk(cur), kv_blk(nxt),
            cache_kt_blk, cache_blk,
        ],
        out_specs=pl.BlockSpec((BLOCK, D_MODEL), lambda b, j: (b * nb + j, 0)),
        compiler_params=_params(40, "arbitrary", "arbitrary"),
        name="latent_attention",
    )(sink, q, kt, kt, kt, v, v, v, cache_kt, cache_v)
    return ctx, lat


def kernel(x_prompt, x_sample, state_hgrn, cache_k, cache_v, c, c_ctx, ada_w, ada_b, norm_w, hg_w_in,
           hg_lb_logits, hg_onorm, hg_w_out, at_w_in, at_sink, at_w_out, moe_w_group, moe_b_group,
           moe_w_expert, moe_b_expert, moe_w1, moe_w3, moe_w2, final_norm):
    xp = x_prompt.reshape(N_PROMPT, D_MODEL)
    xs = x_sample.reshape(N_TOK - N_PROMPT, D_MODEL)
    cond = jnp.concatenate([c_ctx[None, :], c], axis=0)
    mod = _ada(cond, ada_w, ada_b)
    nk = ATT_KV_HEADS * ATT_HEAD_DIM

    def router_params(i):
        pad = jnp.zeros((D_MODEL, ROUTER_LANES - N_EXPERTS - N_GROUPS), F32)
        w = jnp.concatenate([moe_w_expert[i], moe_w_group[i], pad], axis=1)
        b = jnp.concatenate([moe_b_expert[i], moe_b_group[i], pad[0]])[None, :]
        hi = w.astype(BF16)
        lo = (w - hi.astype(F32)).astype(BF16)
        return jnp.stack([hi, lo]), b

    tile = lambda i: (i, 0)
    blk = pl.BlockSpec((TM, D_MODEL), tile)

    prompt_blk = pl.BlockSpec((TM, D_MODEL), _prompt_tile)
    sample_blk = pl.BlockSpec((TM, D_MODEL), _sample_tile)
    q, v, g, lff, kf, lfb, kb = _hg_in(xp, xs, mod[0], norm_w[0, 0][None, :], hg_w_in[0].astype(BF16), hg_lb_logits)
    o_f, o_b, state_new = _gla(q, v, lff, kf, lfb, kb, state_hgrn)
    wr, br = router_params(0)
    routed = _mixer_out(
        _hg_out_kernel, "hgrn_out_route", (o_f, o_b, g, hg_onorm[0][None, :], xp, xs),
        (blk, blk, blk, pl.BlockSpec((1, HG_DK), lambda i: (0, 0)), prompt_blk, sample_blk),
        mod[0], norm_w[0, 1][None, :], hg_w_out[0].astype(BF16), wr, br)
    moe_out = _moe_layer(*routed, 0, moe_w1, moe_w3, moe_w2)

    cos, sin = _rope_tables()
    x, qa, ka, va, k_ctx, v_ctx = _at_in(*moe_out, mod[0], mod[1], norm_w[1, 0][None, :],
                                         at_w_in[0].astype(BF16), cos, sin)
    attn_ctx, attn_lat = _attention(qa, ka, va, cache_k[:, 0].reshape(DEC_BATCH, PAST_LEN, nk).transpose(0, 2, 1),
                                    cache_v[:, 0].reshape(DEC_BATCH, PAST_LEN, nk), at_sink[0])
    wr, br = router_params(1)
    routed = _mixer_out(
        _at_out_kernel, "attn_out_route", (attn_ctx, attn_lat, x), (prompt_blk, sample_blk, blk),
        mod[1], norm_w[1, 1][None, :], at_w_out[0].astype(BF16), wr, br)
    y_prompt, y_sample = _moe_combine(*_moe_layer(*routed, 1, moe_w1, moe_w3, moe_w2), mod[1], final_norm[None, :])

    def cache(feature_major):
        return feature_major.reshape(BATCH, 1, ATT_KV_HEADS, ATT_HEAD_DIM, SEQ).transpose(0, 1, 4, 2, 3)

    return (y_prompt.reshape(BATCH, SEQ, D_MODEL), y_sample.reshape(DEC_BATCH, DEC_SEQ, D_MODEL), state_new,
            cache(k_ctx), cache(v_ctx))
```

```python
import functools
from typing import Any, NamedTuple

import jax
import jax.numpy as jnp
import numpy as np
from jax import lax
from jax.experimental import pallas as pl
from jax.experimental.pallas import tpu as pltpu
from jax.experimental.pallas import tpu_sc as plsc

F32 = jnp.float32
BF16 = jnp.bfloat16

D_MODEL = 1024
BATCH = 16
SEQ = 256
DEC_BATCH = 2
DEC_SEQ = 1024
PAST_LEN = 512
GRID_W = 64
HG_HEADS = 8
HG_DK = 128
CHUNK = 16
ATT_HEAD_DIM = 64
ATT_Q_HEADS = 16
ATT_KV_HEADS = 4
ATT_GROUP = 4
WINDOW = 128
BLOCK = 128
ROPE_HALF = 32
ROPE_BASE = 10000.0
N_GROUPS = 4
EXPERTS_PER_GROUP = 8
N_EXPERTS = 32
D_EXPERT = 256
EPS = 1e-6

N_PROMPT = BATCH * SEQ
N_TOK = N_PROMPT + DEC_BATCH * DEC_SEQ
TM = 256
N_TILES = N_TOK // TM
PROMPT_TILES = N_PROMPT // TM
TILES_PER_DEC_SEQ = DEC_SEQ // TM
LANES = 128
SUBLANES = 8
ADA_LANE_BLOCKS = 4
N_COND_USED = 1 + DEC_BATCH
N_COND = 8
ROUTER_LANES = 128
ROUTE_TM = 1024
ROUTE_ROWS = 8
HG_IN_TM = 512
HG_IN_SUB = 256
HG_IN_TN = 256
DECAY_CLAMP = 60.0
PACK_WORDS = D_MODEL // 4
ROW_WORDS = 2 * PACK_WORDS
MOE_TILE = 256
MOE_STEP_TILES = 4
MOE_WEIGHT_SLOTS = 2 * MOE_STEP_TILES
MOE_ROWS = 2 * N_TOK + N_EXPERTS * MOE_TILE
MIB = 1024 * 1024


def _params(vmem_mib, *semantics):
    return pltpu.CompilerParams(dimension_semantics=semantics, vmem_limit_bytes=vmem_mib * MIB)


def _tile_cond(i):
    return jnp.where(i < PROMPT_TILES, 0, 1 + (i - PROMPT_TILES) // TILES_PER_DEC_SEQ)


def _prompt_tile(i):
    return (jnp.minimum(i, PROMPT_TILES - 1), 0)


def _sample_tile(i):
    return (jnp.maximum(i - PROMPT_TILES, 0), 0)


def _stream_tile(prompt_ref, sample_ref):
    return jnp.where(pl.program_id(0) < PROMPT_TILES, prompt_ref[...], sample_ref[...])


def _mod_row(mod_ref, cond, which):
    return mod_ref[pl.ds(cond, 1), which * D_MODEL:(which + 1) * D_MODEL]


def _norm_mod(x, nw, shift, scale):
    y = x * lax.rsqrt(jnp.mean(x * x, axis=-1, keepdims=True) + EPS)
    return (y * nw) * (1.0 + scale) + shift


def _silu(x):
    return x * jax.nn.sigmoid(x)


def _ada_kernel(c_ref, w_ref, b_ref, o_ref, s_scr):
    @pl.when((pl.program_id(0) == 0) & (pl.program_id(1) == 0))
    def _():
        s_scr[...] = _silu(c_ref[...])

    tn = w_ref.shape[-1]
    cols = [[] for _ in range(N_COND_USED)]
    for j0 in range(0, tn // LANES, ADA_LANE_BLOCKS):
        acc = [[jnp.zeros((SUBLANES, LANES), F32) for _ in range(ADA_LANE_BLOCKS)] for _ in range(N_COND_USED)]
        for g in range(D_MODEL // SUBLANES):
            rows = slice(g * SUBLANES, (g + 1) * SUBLANES)
            w = [w_ref[0, rows, (j0 + j) * LANES:(j0 + j + 1) * LANES] for j in range(ADA_LANE_BLOCKS)]
            for r in range(N_COND_USED):
                s = s_scr[r, rows, :]
                for j in range(ADA_LANE_BLOCKS):
                    acc[r][j] = acc[r][j] + w[j] * s
        for r in range(N_COND_USED):
            cols[r] += [jnp.sum(a, axis=0, keepdims=True) for a in acc[r]]
    out = [jnp.concatenate(cols[r], axis=1) + b_ref[0] for r in range(N_COND_USED)]
    out.append(jnp.zeros((N_COND - N_COND_USED, tn), F32))
    o_ref[0] = jnp.concatenate(out, axis=0)


def _ada(cond, ada_w, ada_b):
    depth, _, n = ada_w.shape
    tn = 1536
    cond_cols = jnp.broadcast_to(cond[:, :, None], (N_COND_USED, D_MODEL, LANES))
    return pl.pallas_call(
        _ada_kernel,
        out_shape=jax.ShapeDtypeStruct((depth, N_COND, n), F32),
        grid=(depth, n // tn),
        in_specs=[
            pl.BlockSpec((N_COND_USED, D_MODEL, LANES), lambda l, j: (0, 0, 0)),
            pl.BlockSpec((1, D_MODEL, tn), lambda l, j: (l, 0, j)),
            pl.BlockSpec((1, 1, tn), lambda l, j: (l, 0, j)),
        ],
        out_specs=pl.BlockSpec((1, N_COND, tn), lambda l, j: (l, 0, j)),
        scratch_shapes=[pltpu.VMEM((N_COND_USED, D_MODEL, LANES), F32)],
        compiler_params=_params(40, "arbitrary", "arbitrary"),
        name="ada_modulation",
    )(cond_cols, ada_w, ada_b.reshape(depth, 1, n))


def _hg_in_kernel(xp_ref, xs_ref, mod_ref, nw_ref, w_ref, lbl_ref,
                  q_ref, v_ref, g_ref, lff_ref, kf_ref, lfb_ref, kb_ref):
    i = pl.program_id(0) * (HG_IN_TM // TM)
    cond = _tile_cond(i)
    shift, scale = _mod_row(mod_ref, cond, 0), _mod_row(mod_ref, cond, 1)

    def normed(rows):
        x = jnp.where(i < PROMPT_TILES, xp_ref[rows, :], xs_ref[rows, :])
        return _norm_mod(x, nw_ref[...], shift, scale).astype(BF16)

    def proj(h, c, cols):
        return jnp.dot(h, w_ref[:, c * D_MODEL + cols.start:c * D_MODEL + cols.stop], preferred_element_type=F32)

    l0, l1, l2 = lbl_ref[0], lbl_ref[1], lbl_ref[2]
    m = jnp.maximum(jnp.maximum(l0, l1), l2)
    e0, e1, e2 = jnp.exp(l0 - m), jnp.exp(l1 - m), jnp.exp(l2 - m)
    lb = e0 / (e0 + e1 + e2)
    col_tiles = [slice(n * HG_IN_TN, (n + 1) * HG_IN_TN) for n in range(D_MODEL // HG_IN_TN)]

    def gate_and_plain(h, rows, d):
        lf_ref, k_ref, plain_ref = ((lff_ref, kf_ref, q_ref), (lfb_ref, kb_ref, v_ref))[d]
        for cols in col_tiles:
            lbd = lb[d:d + 1, cols]
            z_gate = proj(h, 2 + d, cols)
            z_plain = proj(h, d, cols)
            f = lbd + (1.0 - lbd) * jax.nn.sigmoid(z_gate)
            lf = jnp.log(f)
            hi = lf.astype(BF16)
            lf_ref[0, rows, cols] = hi
            lf_ref[1, rows, cols] = (lf - hi.astype(F32)).astype(BF16)
            k_ref[rows, cols] = (1.0 - f).astype(BF16)
            plain_ref[rows, cols] = z_plain.astype(BF16)

    sub_tiles = [slice(s * HG_IN_SUB, (s + 1) * HG_IN_SUB) for s in range(HG_IN_TM // HG_IN_SUB)]
    h = normed(sub_tiles[0])
    for s, rows in enumerate(sub_tiles):
        gate_and_plain(h, rows, 0)
        h_next = normed(sub_tiles[s + 1]) if s + 1 < len(sub_tiles) else None
        gate_and_plain(h, rows, 1)
        for cols in col_tiles:
            g_ref[rows, cols] = proj(h, 4, cols).astype(BF16)
        h = h_next


def _hg_in(xp, xs, mod, nw, w_in, lb_logits):
    tile = lambda i: (i, 0)
    fixed2 = lambda i: (0, 0)
    bf = jax.ShapeDtypeStruct((N_TOK, D_MODEL), BF16)
    ff = jax.ShapeDtypeStruct((2, N_TOK, D_MODEL), BF16)
    blk = pl.BlockSpec((HG_IN_TM, D_MODEL), tile)
    split_blk = pl.BlockSpec((2, HG_IN_TM, D_MODEL), lambda i: (0, i, 0))
    prompt_steps = N_PROMPT // HG_IN_TM
    return pl.pallas_call(
        _hg_in_kernel,
        out_shape=(bf, bf, bf, ff, bf, ff, bf),
        grid=(N_TOK // HG_IN_TM,),
        in_specs=[
            pl.BlockSpec((HG_IN_TM, D_MODEL), lambda i: (jnp.minimum(i, prompt_steps - 1), 0)),
            pl.BlockSpec((HG_IN_TM, D_MODEL), lambda i: (jnp.maximum(i - prompt_steps, 0), 0)),
            pl.BlockSpec(mod.shape, fixed2),
            pl.BlockSpec((1, D_MODEL), fixed2),
            pl.BlockSpec(w_in.shape, fixed2, pipeline_mode=pl.Buffered(1)),
            pl.BlockSpec(lb_logits.shape, lambda i: (0, 0, 0)),
        ],
        out_specs=(blk, blk, blk, split_blk, blk, split_blk, blk),
        compiler_params=_params(56, "arbitrary"),
        name="hgrn_in_proj",
    )(xp, xs, mod, nw, w_in, lb_logits)


def _gla_exact(reverse, q_ref, v_ref, lf_ref, k_ref, st_ref, o_ref):
    nt = (((1,), (1,)), ((), ()))
    tn = (((0,), (0,)), ((), ()))
    n_chunks = TM // CHUNK
    row_id = lax.broadcasted_iota(jnp.int32, (CHUNK, HG_DK), 0)
    order = range(CHUNK - 1, -1, -1) if reverse else range(CHUNK)

    def chunk_head(it, carry):
        step, h = it // HG_HEADS, it % HG_HEADS
        ci = (n_chunks - 1 - step) if reverse else step
        rows = pl.ds(pl.multiple_of(ci * CHUNK, CHUNK), CHUNK)
        cols = pl.ds(pl.multiple_of(h * HG_DK, HG_DK), HG_DK)
        q, k, v = q_ref[rows, cols], k_ref[rows, cols], v_ref[rows, cols]
        f = jnp.exp(lf_ref[0, rows, cols].astype(F32) + lf_ref[1, rows, cols].astype(F32))
        st = st_ref[0, h]
        o = jnp.zeros((CHUNK, HG_DK), F32)
        for t in order:
            one = row_id == t
            v_t = jnp.where(one, v, jnp.zeros_like(v))
            st = st * f[t:t + 1, :] + lax.dot_general(v_t, k, tn, preferred_element_type=F32)
            o = jnp.where(one, lax.dot_general(q, st.astype(BF16), nt, preferred_element_type=F32), o)
        st_ref[0, h] = st
        o_ref[rows, cols] = o.astype(o_ref.dtype)
        return carry

    lax.fori_loop(0, n_chunks * HG_HEADS, chunk_head, 0)


class _GlaDirection(NamedTuple):
    reverse: bool
    q: Any
    v: Any
    lf: Any
    k: Any
    o: Any
    st: Any
    sw: Any
    ut: Any
    qd: Any
    kd: Any
    ki: Any
    cd: Any
    b: Any


_NT = (((1,), (1,)), ((), ()))
_TN = (((0,), (0,)), ((), ()))
_HG_COLS = [slice(h * HG_DK, (h + 1) * HG_DK) for h in range(HG_HEADS)]
_N_CHUNKS = TM // CHUNK
_N_PAIRS = _N_CHUNKS // 2


def _gla_prepare(d):
    r = lax.broadcasted_iota(jnp.int32, (TM, TM), 0)
    c = lax.broadcasted_iota(jnp.int32, (TM, TM), 1)
    same = (r // CHUNK) == (c // CHUNK)
    tri = (same & ((c >= r) if d.reverse else (c <= r))).astype(BF16)
    b = jnp.dot(tri, d.lf[0], preferred_element_type=F32) + jnp.dot(tri, d.lf[1], preferred_element_type=F32)
    d.b[...] = b
    edge = 0 if d.reverse else CHUNK - 1
    total = b.reshape(_N_CHUNKS, CHUNK, D_MODEL)[:, edge, :]
    d.cd[...] = jnp.exp(total)
    return jnp.max(-total) <= DECAY_CLAMP


def _gla_pair_rows(d, step):
    pi = (_N_PAIRS - 1 - step) if d.reverse else step
    row0 = pi * 2 * CHUNK
    lo, hi = (pl.ds(row0, CHUNK), 2 * pi), (pl.ds(row0 + CHUNK, CHUNK), 2 * pi + 1)
    return pl.ds(row0, 2 * CHUNK), ((hi, lo) if d.reverse else (lo, hi))


def _chunk_decay(d, chunk):
    return d.cd[pl.ds(chunk, 1), :]


def _gla_decayed_operands(d, step):
    both, ((_, c_first), (_, c_second)) = _gla_pair_rows(d, step)
    b = d.b[both, :]
    ki = d.k[both, :] * jnp.exp(jnp.minimum(-b, DECAY_CLAMP)).astype(BF16)
    d.qd[both, :] = d.q[both, :] * jnp.exp(b).astype(BF16)
    d.ki[both, :] = ki
    c_lo, c_hi = (c_second, c_first) if d.reverse else (c_first, c_second)
    ends = jnp.concatenate([jnp.broadcast_to(_chunk_decay(d, c).astype(BF16), (CHUNK, D_MODEL)) for c in (c_lo, c_hi)],
                           axis=0)
    d.kd[both, :] = ki * ends


def _rows_scaled(x, scale_row, second_half):
    scale = jnp.broadcast_to(scale_row.astype(BF16), (CHUNK, x.shape[1]))
    ones = jnp.ones((CHUNK, x.shape[1]), BF16)
    return x * jnp.concatenate([ones, scale] if second_half else [scale, ones], axis=0)


def _gla_key_value_product(d, step):
    _gla_decayed_operands(d, step)
    both, (_, (_, c_second)) = _gla_pair_rows(d, step)
    decay_second = _chunk_decay(d, c_second)
    for h, cols in enumerate(_HG_COLS):
        keys = _rows_scaled(d.kd[both, cols], decay_second[:, cols], second_half=d.reverse)
        d.ut[h] = lax.dot_general(d.v[both, cols], keys, _TN, preferred_element_type=F32)


def _gla_start(d):
    for h in range(HG_HEADS):
        d.sw[0, h] = d.st[0, h].T.astype(BF16)
    _gla_key_value_product(d, 0)


def _gla_pair(d, step, src, dst):
    sr = lax.broadcasted_iota(jnp.int32, (CHUNK, CHUNK), 0)
    sc = lax.broadcasted_iota(jnp.int32, (CHUNK, CHUNK), 1)
    tr = lax.broadcasted_iota(jnp.int32, (CHUNK, 2 * CHUNK), 0)
    tc = lax.broadcasted_iota(jnp.int32, (CHUNK, 2 * CHUNK), 1)
    if d.reverse:
        keep_first = sc >= sr
        keep_second = (tc >= tr) | (tc >= CHUNK)
    else:
        keep_first = sc <= sr
        keep_second = (tc < CHUNK) | (tc - CHUNK <= tr)
    both, ((first, c_first), (second, c_second)) = _gla_pair_rows(d, step)
    decay_first = _chunk_decay(d, c_first)
    decay_both = decay_first * _chunk_decay(d, c_second)
    for h, cols in enumerate(_HG_COLS):
        new = d.st[src, h] * decay_both[:, cols] + d.ut[h]
        d.st[dst, h] = new
        d.sw[dst, h] = new.T.astype(BF16)
    a_first = [lax.dot_general(d.qd[first, cols], d.ki[first, cols], _NT, preferred_element_type=F32)
               for cols in _HG_COLS]
    a_second = []
    for cols in _HG_COLS:
        lo_keys = (d.ki if d.reverse else d.kd)[pl.ds(both.start, CHUNK), cols]
        hi_keys = (d.kd if d.reverse else d.ki)[pl.ds(both.start + CHUNK, CHUNK), cols]
        keys = jnp.concatenate([lo_keys, hi_keys], axis=0)
        a_second.append(lax.dot_general(d.qd[second, cols], keys, _NT, preferred_element_type=F32))
    inter = [jnp.dot(_rows_scaled(d.qd[both, cols], decay_first[:, cols], second_half=not d.reverse),
                     d.sw[src, h], preferred_element_type=F32) for h, cols in enumerate(_HG_COLS)]
    if step + 1 < _N_PAIRS:
        _gla_key_value_product(d, step + 1)
    first_half, second_half = (slice(CHUNK, None), slice(None, CHUNK)) if d.reverse else \
                              (slice(None, CHUNK), slice(CHUNK, None))
    for h, cols in enumerate(_HG_COLS):
        am = jnp.where(keep_first, a_first[h], 0.0).astype(BF16)
        d.o[first, cols] = (jnp.dot(am, d.v[first, cols], preferred_element_type=F32)
                            + inter[h][first_half]).astype(d.o.dtype)
        am = jnp.where(keep_second, a_second[h], 0.0).astype(BF16)
        d.o[second, cols] = (jnp.dot(am, d.v[both, cols], preferred_element_type=F32)
                             + inter[h][second_half]).astype(d.o.dtype)


def _gla_kernel(qf_ref, vf_ref, lff_ref, kf_ref, qb_ref, vb_ref, lfb_ref, kb_ref, s0_ref,
                of_ref, ob_ref, sout_ref, st_scr, sw_scr, ut_scr, qd_scr, kd_scr, ki_scr, cd_scr, b_scr):
    i = pl.program_id(0)
    is_prompt = i < PROMPT_TILES
    first = jnp.logical_or(is_prompt, (i - PROMPT_TILES) % TILES_PER_DEC_SEQ == 0)
    work = [tuple(scr.at[n] for scr in (st_scr, sw_scr, ut_scr, qd_scr, kd_scr, ki_scr, cd_scr, b_scr))
            for n in range(2)]
    fwd = _GlaDirection(False, qf_ref, vf_ref, lff_ref, kf_ref, of_ref, *work[0])
    bwd = _GlaDirection(True, qb_ref, vb_ref, lfb_ref, kb_ref, ob_ref, *work[1])

    @pl.when(is_prompt)
    def _():
        for d in (fwd, bwd):
            d.st[0] = jnp.zeros((HG_HEADS, HG_DK, HG_DK), F32)

    @pl.when(jnp.logical_and(first, jnp.logical_not(is_prompt)))
    def _():
        for n, d in enumerate((fwd, bwd)):
            for h in range(HG_HEADS):
                d.st[0, h] = s0_ref[0, 0, n, h].T

    exact_scores = jnp.logical_and(_gla_prepare(fwd), _gla_prepare(bwd))

    @pl.when(exact_scores)
    def _():
        _gla_start(fwd)
        _gla_start(bwd)

        for step in range(_N_PAIRS):
            for d in (fwd, bwd):
                _gla_pair(d, step, step % 2, 1 - step % 2)

    @pl.when(jnp.logical_not(exact_scores))
    def _():
        for d in (fwd, bwd):
            _gla_exact(d.reverse, d.q, d.v, d.lf, d.k, d.st, d.o)

    @pl.when(is_prompt)
    def _():
        for n, d in enumerate((fwd, bwd)):
            for h in range(HG_HEADS):
                sout_ref[0, 0, n, h] = d.st[0, h].T


def _gla(q, v, lff, kf, lfb, kb, state_hgrn):
    def fwd_tile(i):
        return (i, 0)

    def bwd_tile(i):
        j = (i - PROMPT_TILES) % TILES_PER_DEC_SEQ
        return (jnp.where(i < PROMPT_TILES, i, i - j + (TILES_PER_DEC_SEQ - 1 - j)), 0)

    def s0_idx(i):
        return (jnp.maximum(i - PROMPT_TILES, 0) // TILES_PER_DEC_SEQ, 0, 0, 0, 0, 0)

    def sout_idx(i):
        return (jnp.minimum(i, PROMPT_TILES - 1), 0, 0, 0, 0, 0)

    f_blk = pl.BlockSpec((TM, D_MODEL), fwd_tile)
    b_blk = pl.BlockSpec((TM, D_MODEL), bwd_tile)
    f_split = pl.BlockSpec((2, TM, D_MODEL), lambda i: (0,) + fwd_tile(i))
    b_split = pl.BlockSpec((2, TM, D_MODEL), lambda i: (0,) + bwd_tile(i))
    st_blk = (1, 1, 2, HG_HEADS, HG_DK, HG_DK)
    return pl.pallas_call(
        _gla_kernel,
        out_shape=(
            jax.ShapeDtypeStruct((N_TOK, D_MODEL), BF16),
            jax.ShapeDtypeStruct((N_TOK, D_MODEL), BF16),
            jax.ShapeDtypeStruct((BATCH,) + st_blk[1:], F32),
        ),
        grid=(N_TILES,),
        in_specs=[f_blk, f_blk, f_split, f_blk, b_blk, b_blk, b_split, b_blk, pl.BlockSpec(st_blk, s0_idx)],
        out_specs=(f_blk, b_blk, pl.BlockSpec(st_blk, sout_idx)),
        scratch_shapes=[
            pltpu.VMEM((2, 2, HG_HEADS, HG_DK, HG_DK), F32),
            pltpu.VMEM((2, 2, HG_HEADS, HG_DK, HG_DK), BF16),
            pltpu.VMEM((2, HG_HEADS, HG_DK, HG_DK), F32),
            pltpu.VMEM((2, TM, D_MODEL), BF16),
            pltpu.VMEM((2, TM, D_MODEL), BF16),
            pltpu.VMEM((2, TM, D_MODEL), BF16),
            pltpu.VMEM((2, TM // CHUNK, D_MODEL), F32),
            pltpu.VMEM((2, TM, D_MODEL), F32),
        ],
        compiler_params=_params(48, "arbitrary"),
        name="hgrn_recurrence",
    )(q, v, lff, kf, q, v, lfb, kb, state_hgrn)


def _route(logits):
    lane = lax.broadcasted_iota(jnp.int32, logits.shape, 1)
    neg = jnp.float32(-jnp.inf)
    big = jnp.int32(ROUTER_LANES)

    def first_max(x):
        m = jnp.max(x, axis=-1, keepdims=True)
        return m, jnp.min(jnp.where(x == m, lane, big), axis=-1, keepdims=True)

    is_group = (lane >= N_EXPERTS) & (lane < N_EXPERTS + N_GROUPS)
    gl = jnp.where(is_group, logits, neg)
    gmax, g_lane = first_max(gl)
    g_sel = g_lane - N_EXPERTS
    gsum = jnp.sum(jnp.exp(gl - gmax), axis=-1, keepdims=True)
    p_g = 1.0 / gsum
    in_sel = (lane < N_EXPERTS) & ((lane // EXPERTS_PER_GROUP) == g_sel)
    m1, i1 = first_max(jnp.where(in_sel, logits, neg))
    m2, i2 = first_max(jnp.where(in_sel & (lane != i1), logits, neg))
    e2 = jnp.exp(m2 - m1)
    return i1, i2, p_g / (1.0 + e2), p_g * e2 / (1.0 + e2)


def _pack_rows(x):
    q = PACK_WORDS
    bits = pltpu.bitcast(x.astype(BF16).astype(F32), jnp.uint32)
    return [(bits[:, (2 + h) * q:(3 + h) * q] & jnp.uint32(0xFFFF0000)) | (bits[:, h * q:(h + 1) * q] >> 16)
            for h in range(2)]


def _unpack_rows(half0, half1):
    lo = lambda w: pltpu.bitcast(w << 16, F32).astype(BF16)
    hi = lambda w: pltpu.bitcast(w & jnp.uint32(0xFFFF0000), F32).astype(BF16)
    return [lo(half0), lo(half1), hi(half0), hi(half1)]


def _mixer_tail(mix_bf16, x, mod_ref, nw2_ref, wo_ref, wr_ref, br_ref, x1_ref, h2p_ref, lg_ref):
    cond = _tile_cond(pl.program_id(0))
    out = jnp.dot(mix_bf16, wo_ref[...], preferred_element_type=F32)
    x1 = x + _mod_row(mod_ref, cond, 2) * out
    x1_ref[...] = x1
    h2 = _norm_mod(x1, nw2_ref[...], _mod_row(mod_ref, cond, 3), _mod_row(mod_ref, cond, 4))
    h2p_ref[...] = jnp.concatenate(_pack_rows(h2), axis=1)
    h_hi = h2.astype(BF16)
    h_lo = (h2 - h_hi.astype(F32)).astype(BF16)
    lg_ref[...] = (jnp.dot(h_hi, wr_ref[0], preferred_element_type=F32)
                   + (jnp.dot(h_hi, wr_ref[1], preferred_element_type=F32)
                      + jnp.dot(h_lo, wr_ref[0], preferred_element_type=F32))) + br_ref[...]


def _route_kernel(lg_ref, rt_ref, rw_ref, cnt_ref, carry_scr):
    @pl.when(pl.program_id(0) == 0)
    def _():
        carry_scr[...] = jnp.zeros_like(carry_scr)

    logits = lg_ref[...]
    i1, i2, w1, w2 = _route(logits)
    lane = lax.broadcasted_iota(jnp.int32, logits.shape, 1)
    chosen = ((lane == i1) | (lane == i2)).astype(BF16)
    r = lax.broadcasted_iota(jnp.int32, (ROUTE_TM, ROUTE_TM), 0)
    c = lax.broadcasted_iota(jnp.int32, (ROUTE_TM, ROUTE_TM), 1)
    before = jnp.dot((c < r).astype(BF16), chosen, preferred_element_type=F32) + carry_scr[...]
    r1 = jnp.sum(jnp.where(lane == i1, before, 0.0), axis=-1, keepdims=True).astype(jnp.int32)
    r2 = jnp.sum(jnp.where(lane == i2, before, 0.0), axis=-1, keepdims=True).astype(jnp.int32)
    total = carry_scr[...] + jnp.sum(chosen.astype(F32), axis=0, keepdims=True)
    carry_scr[...] = total
    tiles = jnp.ceil(total * (1.0 / MOE_TILE))
    e_from = lax.broadcasted_iota(jnp.int32, (ROUTER_LANES, ROUTER_LANES), 0)
    e_to = lax.broadcasted_iota(jnp.int32, (ROUTER_LANES, ROUTER_LANES), 1)
    tiles_before = jnp.dot(jnp.broadcast_to(tiles, (SUBLANES, ROUTER_LANES)).astype(BF16),
                           (e_from < e_to).astype(BF16), preferred_element_type=F32)
    cnt_ref[...] = jnp.concatenate([total, tiles_before[:1] * MOE_TILE,
                                    jnp.zeros((ROUTE_ROWS - 2, ROUTER_LANES), F32)], axis=0)
    by_token = jnp.where(lane == 0, i1, jnp.where(lane == 1, i2, jnp.where(lane == 2, r1, r2)))
    rt_ref[...] = by_token.T[:ROUTE_ROWS, :]
    rw_ref[...] = jnp.where(lane == 0, w1, w2)


def _route_tokens(logits):
    blk = pl.BlockSpec((ROUTE_TM, ROUTER_LANES), lambda i: (i, 0))
    return pl.pallas_call(
        _route_kernel,
        out_shape=(
            jax.ShapeDtypeStruct((ROUTE_ROWS, N_TOK), jnp.int32),
            jax.ShapeDtypeStruct((N_TOK, ROUTER_LANES), F32),
            jax.ShapeDtypeStruct((ROUTE_ROWS, ROUTER_LANES), F32),
        ),
        grid=(N_TOK // ROUTE_TM,),
        in_specs=[blk],
        out_specs=(pl.BlockSpec((ROUTE_ROWS, ROUTE_TM), lambda i: (0, i)), blk,
                   pl.BlockSpec((ROUTE_ROWS, ROUTER_LANES), lambda i: (0, 0))),
        scratch_shapes=[pltpu.VMEM((1, ROUTER_LANES), F32)],
        compiler_params=_params(32, "arbitrary"),
        name="moe_route",
    )(logits)


def _hg_out_kernel(of_ref, ob_ref, g_ref, on_ref, xp_ref, xs_ref, *rest):
    o = of_ref[...].astype(F32) + ob_ref[...].astype(F32)
    parts = []
    for h in range(HG_HEADS):
        oh = o[:, h * HG_DK:(h + 1) * HG_DK]
        parts.append(oh * lax.rsqrt(jnp.mean(oh * oh, axis=-1, keepdims=True) + EPS) * on_ref[...])
    y = jnp.concatenate(parts, axis=1) * _silu(g_ref[...].astype(F32))
    _mixer_tail(y.astype(BF16), _stream_tile(xp_ref, xs_ref), *rest)


def _at_out_kernel(ac_ref, al_ref, x_ref, *rest):
    _mixer_tail(_stream_tile(ac_ref, al_ref), x_ref[...], *rest)


def _mixer_out(kernel_fn, name, mix_inputs, mix_specs, mod, nw2, w_out, w_router, b_router):
    tile = lambda i: (i, 0)
    fixed2 = lambda i: (0, 0)
    blk = pl.BlockSpec((TM, D_MODEL), tile)
    lanes_blk = pl.BlockSpec((TM, ROUTER_LANES), tile)
    return pl.pallas_call(
        kernel_fn,
        out_shape=(
            jax.ShapeDtypeStruct((N_TOK, D_MODEL), F32),
            jax.ShapeDtypeStruct((N_TOK, ROW_WORDS), jnp.uint32),
            jax.ShapeDtypeStruct((N_TOK, ROUTER_LANES), F32),
        ),
        grid=(N_TILES,),
        in_specs=list(mix_specs) + [
            pl.BlockSpec(mod.shape, fixed2),
            pl.BlockSpec((1, D_MODEL), fixed2),
            pl.BlockSpec(w_out.shape, fixed2),
            pl.BlockSpec(w_router.shape, lambda i: (0, 0, 0)),
            pl.BlockSpec((1, ROUTER_LANES), fixed2),
        ],
        out_specs=(blk, pl.BlockSpec((TM, ROW_WORDS), tile), lanes_blk),
        compiler_params=_params(40, "arbitrary"),
        name=name,
    )(*mix_inputs, mod, nw2, w_out, w_router, b_router)


def _moe_plan(route_t, counts):
    cnt = counts[0, :N_EXPERTS].astype(jnp.int32)
    offs = counts[1, :N_EXPERTS].astype(jnp.int32)
    ends = offs + ((cnt + MOE_TILE - 1) // MOE_TILE) * MOE_TILE
    experts = route_t[0:2]
    pos = jnp.sum(jnp.where(experts[None, :, :] == jnp.arange(N_EXPERTS)[:, None, None], offs[:, None, None], 0),
                  axis=0) + route_t[2:4]
    tile_start = jnp.arange(MOE_ROWS // MOE_TILE, dtype=jnp.int32) * MOE_TILE
    tile_expert = jnp.minimum(jnp.sum(ends[None, :] <= tile_start[:, None], axis=1), N_EXPERTS - 1).astype(jnp.int32)
    of_tile = tile_expert[:, None] == jnp.arange(N_EXPERTS)[None, :]
    tile_offs = jnp.sum(jnp.where(of_tile, offs[None, :], 0), axis=1)
    tile_cnt = jnp.sum(jnp.where(of_tile, cnt[None, :], 0), axis=1)
    tile_rows = jnp.clip(tile_offs + tile_cnt - tile_start, 0, MOE_TILE).astype(jnp.int32)
    active = tile_start < ends[-1]
    tile_opens = jnp.where(active & (tile_start == tile_offs), tile_expert, -1).astype(jnp.int32)
    tile_opens = jnp.concatenate([tile_opens, jnp.full((MOE_STEP_TILES,), -1, jnp.int32)])
    n_active = (ends[-1] // MOE_TILE).astype(jnp.int32).reshape(1)
    slot = (jnp.cumsum((cnt > 0).astype(jnp.int32)) - 1) % MOE_WEIGHT_SLOTS
    tile_slot = jnp.sum(jnp.where(of_tile, slot[None, :], 0), axis=1)
    last_slot = jnp.sum(jnp.where(tile_start == ends[-1] - MOE_TILE, tile_slot, 0))
    tile_slot = jnp.where(active, tile_slot, last_slot).astype(jnp.int32)
    return pos.astype(jnp.int32), (tile_rows, tile_slot, tile_opens, n_active)


SC_WINDOW = 64


def _sc_mesh():
    return plsc.VectorSubcoreMesh(core_axis_name="c", subcore_axis_name="s")


def _sc_scatter_rows(x, idx_a, idx_b, n_out_rows):
    n = x.shape[0]

    @functools.partial(pl.kernel, out_type=jax.ShapeDtypeStruct((n_out_rows, ROW_WORDS), x.dtype), mesh=_sc_mesh(),
                       scratch_types=[pltpu.SemaphoreType.DMA, pltpu.SemaphoreType.DMA])
    def scatter(x_hbm, ia_hbm, ib_hbm, o_hbm, sem_a, sem_b):
        def body(x_vmem, ia_vmem, ib_vmem):
            copy_a = pltpu.async_copy(x_vmem, o_hbm.at[ia_vmem.at[0]], sem_a)
            copy_b = pltpu.async_copy(x_vmem, o_hbm.at[ib_vmem.at[0]], sem_b)
            copy_a.wait()
            copy_b.wait()

        idx_spec = pl.BlockSpec((1, SC_WINDOW), index_map=lambda i: (i, 0))
        pltpu.emit_pipeline(
            body, grid=(n // SC_WINDOW,),
            in_specs=[pl.BlockSpec((SC_WINDOW, ROW_WORDS), index_map=lambda i: (i, 0)), idx_spec, idx_spec],
            out_specs=[],
            core_axis_name=("c", "s"), dimension_semantics=(pltpu.PARALLEL,),
        )(x_hbm, ia_hbm, ib_hbm)

    return scatter(x, idx_a, idx_b)


def _sc_gather_rows(table, idx):
    n = idx.size

    @functools.partial(pl.kernel, out_type=jax.ShapeDtypeStruct((n, ROW_WORDS), table.dtype), mesh=_sc_mesh())
    def gather(t_hbm, i_hbm, o_hbm):
        def body(i_vmem, o_vmem):
            pltpu.sync_copy(t_hbm.at[i_vmem.at[0]], o_vmem)

        pltpu.emit_pipeline(
            body, grid=(n // SC_WINDOW,),
            in_specs=[pl.BlockSpec((1, SC_WINDOW), index_map=lambda i: (i, 0))],
            out_specs=[pl.BlockSpec((SC_WINDOW, ROW_WORDS), index_map=lambda i: (i, 0))],
            core_axis_name=("c", "s"), dimension_semantics=(pltpu.PARALLEL,),
        )(i_hbm, o_hbm)

    return gather(table, idx)


def _ffn_kernel(tr_ref, ts_ref, to_ref, na_ref, xs_ref, w1_hbm, w3_hbm, w2_hbm, ys_ref,
                w1_buf, w3_buf, w2_buf, sem, *, layer):
    def weight_copies(t):
        pairs = ((w1_hbm, w1_buf), (w3_hbm, w3_buf), (w2_hbm, w2_buf))
        return [pltpu.make_async_copy(hbm.at[layer, to_ref[t]], buf.at[ts_ref[t]], sem.at[ts_ref[t], j])
                for j, (hbm, buf) in enumerate(pairs)]

    def for_experts_opened_from(first_tile, action):
        for j in range(MOE_STEP_TILES):
            t = first_tile + j

            @pl.when(to_ref[t] >= 0)
            def _():
                for copy in weight_copies(t):
                    action(copy)

    def tile(t, rows):
        slot = ts_ref[t]
        row = lax.broadcasted_iota(jnp.int32, (MOE_TILE, PACK_WORDS), 0)
        live = row < tr_ref[t]
        halves = [jnp.where(live, xs_ref[rows, h * PACK_WORDS:(h + 1) * PACK_WORDS],
                            jnp.zeros((MOE_TILE, PACK_WORDS), jnp.uint32)) for h in range(2)]
        chunks = _unpack_rows(*halves)

        def up(w_buf):
            acc = None
            for k, chunk in enumerate(chunks):
                w = w_buf[slot, k * PACK_WORDS:(k + 1) * PACK_WORDS, :].astype(BF16)
                part = jnp.dot(chunk, w, preferred_element_type=F32)
                acc = part if acc is None else acc + part
            return acc

        hid = (_silu(up(w1_buf)) * up(w3_buf)).astype(BF16)
        y = jnp.dot(hid, w2_buf[slot].astype(BF16), preferred_element_type=F32)
        ys_ref[rows, :] = jnp.concatenate(_pack_rows(y), axis=1)

    step = pl.program_id(0)
    first_tile = step * MOE_STEP_TILES

    @pl.when(first_tile < na_ref[0])
    def _():
        @pl.when(step == 0)
        def _():
            for_experts_opened_from(first_tile, lambda copy: copy.start())

        for_experts_opened_from(first_tile + MOE_STEP_TILES, lambda copy: copy.start())
        for_experts_opened_from(first_tile, lambda copy: copy.wait())
        for j in range(MOE_STEP_TILES):
            tile(first_tile + j, slice(j * MOE_TILE, (j + 1) * MOE_TILE))


def _ffn(xs, tables, layer, w1, w3, w2):
    step_rows = MOE_STEP_TILES * MOE_TILE
    row_tile = lambda s, tr, ts, to, na: (jnp.minimum(s, (na[0] - 1) // MOE_STEP_TILES), 0)
    hbm = pl.BlockSpec(memory_space=pl.ANY)
    return pl.pallas_call(
        functools.partial(_ffn_kernel, layer=layer),
        out_shape=jax.ShapeDtypeStruct((MOE_ROWS, ROW_WORDS), jnp.uint32),
        grid_spec=pltpu.PrefetchScalarGridSpec(
            num_scalar_prefetch=len(tables),
            grid=(MOE_ROWS // step_rows,),
            in_specs=[pl.BlockSpec((step_rows, ROW_WORDS), row_tile), hbm, hbm, hbm],
            out_specs=pl.BlockSpec((step_rows, ROW_WORDS), row_tile),
            scratch_shapes=[
                pltpu.VMEM((MOE_WEIGHT_SLOTS, D_MODEL, D_EXPERT), F32),
                pltpu.VMEM((MOE_WEIGHT_SLOTS, D_MODEL, D_EXPERT), F32),
                pltpu.VMEM((MOE_WEIGHT_SLOTS, D_EXPERT, D_MODEL), F32),
                pltpu.SemaphoreType.DMA((MOE_WEIGHT_SLOTS, 3)),
            ],
        ),
        compiler_params=_params(48, "arbitrary"),
        name="moe_experts",
    )(*tables, xs, w1, w3, w2)


def _moe_res(x_ref, y_ref, rw_ref, mod_ref):
    cond = _tile_cond(pl.program_id(0))
    rw = rw_ref[...]
    wa = rw[:, 0:1]
    wb = rw[:, 1:2]
    ya = _unpack_rows(y_ref[0, :, :PACK_WORDS], y_ref[0, :, PACK_WORDS:])
    yb = _unpack_rows(y_ref[1, :, :PACK_WORDS], y_ref[1, :, PACK_WORDS:])
    y = jnp.concatenate([wa * a.astype(F32) + wb * b.astype(F32) for a, b in zip(ya, yb)], axis=1)
    return x_ref[...] + _mod_row(mod_ref, cond, 5) * y


def _moe_res_final_kernel(x_ref, y_ref, rw_ref, mod_ref, fn_ref, op_ref, os_ref):
    x = _moe_res(x_ref, y_ref, rw_ref, mod_ref)
    y = x * lax.rsqrt(jnp.mean(x * x, axis=-1, keepdims=True) + EPS) * fn_ref[...]
    is_prompt = pl.program_id(0) < PROMPT_TILES

    @pl.when(is_prompt)
    def _():
        op_ref[...] = y

    @pl.when(jnp.logical_not(is_prompt))
    def _():
        os_ref[...] = y


def _moe_combine(x1, y_pairs, route_w, mod, final_norm):
    tile = lambda i: (i, 0)
    fixed2 = lambda i: (0, 0)
    return pl.pallas_call(
        _moe_res_final_kernel,
        out_shape=(jax.ShapeDtypeStruct((N_PROMPT, D_MODEL), F32),
                   jax.ShapeDtypeStruct((N_TOK - N_PROMPT, D_MODEL), F32)),
        grid=(N_TILES,),
        in_specs=[pl.BlockSpec((TM, D_MODEL), tile),
                  pl.BlockSpec((2, TM, ROW_WORDS), lambda i: (0, i, 0)),
                  pl.BlockSpec((TM, ROUTER_LANES), tile),
                  pl.BlockSpec(mod.shape, fixed2),
                  pl.BlockSpec((1, D_MODEL), fixed2)],
        out_specs=(pl.BlockSpec((TM, D_MODEL), _prompt_tile), pl.BlockSpec((TM, D_MODEL), _sample_tile)),
        compiler_params=_params(32, "arbitrary"),
        name="moe_combine",
    )(x1, y_pairs, route_w, mod, final_norm)


def _moe_layer(x1, h2p, logits, layer, w1, w3, w2):
    route_t, route_w, counts = _route_tokens(logits)
    pos, tables = _moe_plan(route_t, counts)
    windows = pos.reshape(2, N_TOK // SC_WINDOW, SC_WINDOW)
    xs = _sc_scatter_rows(h2p, windows[0], windows[1], MOE_ROWS)
    ys = _ffn(xs, tables, layer, w1, w3, w2)
    y_pairs = _sc_gather_rows(ys, windows.reshape(2 * N_TOK // SC_WINDOW, SC_WINDOW))
    return x1, y_pairs.reshape(2, N_TOK, ROW_WORDS), route_w


def _swap_rotary_halves(x):
    n = x.shape[-1]
    lane = lax.broadcasted_iota(jnp.int32, x.shape, 1)
    quarter = ROPE_HALF // 2
    return jnp.where((lane % ROPE_HALF) < quarter, pltpu.roll(x, n - quarter, 1), pltpu.roll(x, quarter, 1))


def _at_in_kernel(x1_ref, y_ref, rw_ref, mod_prev_ref, mod_ref, nw_ref, w_ref, cos_ref, sin_ref,
                  x_ref, q_ref, kt_ref, v_ref, kc_ref, vc_ref):
    i = pl.program_id(0)
    cond = _tile_cond(i)
    x = _moe_res(x1_ref, y_ref, rw_ref, mod_prev_ref)
    x_ref[...] = x
    h = _norm_mod(x, nw_ref[...], _mod_row(mod_ref, cond, 0), _mod_row(mod_ref, cond, 1)).astype(BF16)
    nq = ATT_Q_HEADS * ATT_HEAD_DIM
    nk = ATT_KV_HEADS * ATT_HEAD_DIM
    cos = cos_ref[...]
    sin = sin_ref[...]

    def rope(x):
        reps = x.shape[-1] // LANES
        return x * jnp.concatenate([cos] * reps, axis=1) + _swap_rotary_halves(x) * jnp.concatenate([sin] * reps, axis=1)

    q_ref[...] = (rope(jnp.dot(h, w_ref[:, :nq], preferred_element_type=F32)) * ATT_HEAD_DIM ** -0.5).astype(BF16)
    k = rope(jnp.dot(h, w_ref[:, nq:nq + nk], preferred_element_type=F32))
    v = jnp.dot(h, w_ref[:, nq + nk:], preferred_element_type=F32)
    kt = k.T
    kt_ref[...] = kt.astype(BF16)
    v_ref[...] = v.astype(BF16)

    @pl.when(i < PROMPT_TILES)
    def _():
        kc_ref[...] = kt
        vc_ref[...] = v.T


def _rope_tables():
    f32 = np.float32
    pos = np.arange(DEC_SEQ)
    t_row = (pos // GRID_W).astype(f32)
    t_col = (pos % GRID_W).astype(f32)
    inv = f32(ROPE_BASE) ** (-np.arange(0, ROPE_HALF, 2, dtype=f32) / f32(ROPE_HALF))
    j = np.arange(LANES) % ATT_HEAD_DIM
    freq = inv[(j % ROPE_HALF) % (ROPE_HALF // 2)]
    ang = (np.where((j < ROPE_HALF)[None, :], t_row[:, None], t_col[:, None]) * freq[None, :]).astype(f32)
    sign = np.where((j % ROPE_HALF) < ROPE_HALF // 2, -1.0, 1.0).astype(f32)
    cos = np.concatenate([np.ones((TM, LANES), f32), np.cos(ang)], axis=0)
    sin = np.concatenate([np.zeros((TM, LANES), f32), np.sin(ang) * sign[None, :]], axis=0)
    return jnp.asarray(cos, F32), jnp.asarray(sin, F32)


def _at_in(x1, y_pairs, route_w, mod_prev, mod, nw, w_in, cos, sin):
    tile = lambda i: (i, 0)
    fixed2 = lambda i: (0, 0)
    rope_tile = lambda i: (jnp.where(i < PROMPT_TILES, 0, 1 + (i - PROMPT_TILES) % TILES_PER_DEC_SEQ), 0)
    nk = ATT_KV_HEADS * ATT_HEAD_DIM
    return pl.pallas_call(
        _at_in_kernel,
        out_shape=(
            jax.ShapeDtypeStruct((N_TOK, D_MODEL), F32),
            jax.ShapeDtypeStruct((N_TOK, D_MODEL), BF16),
            jax.ShapeDtypeStruct((nk, N_TOK), BF16),
            jax.ShapeDtypeStruct((N_TOK, nk), BF16),
            jax.ShapeDtypeStruct((N_PROMPT, nk), F32),
            jax.ShapeDtypeStruct((N_PROMPT, nk), F32),
        ),
        grid=(N_TILES,),
        in_specs=[
            pl.BlockSpec((TM, D_MODEL), tile),
            pl.BlockSpec((2, TM, ROW_WORDS), lambda i: (0, i, 0)),
            pl.BlockSpec((TM, ROUTER_LANES), tile),
            pl.BlockSpec(mod_prev.shape, fixed2),
            pl.BlockSpec(mod.shape, fixed2),
            pl.BlockSpec((1, D_MODEL), fixed2),
            pl.BlockSpec(w_in.shape, fixed2),
            pl.BlockSpec((TM, LANES), rope_tile),
            pl.BlockSpec((TM, LANES), rope_tile),
        ],
        out_specs=(pl.BlockSpec((TM, D_MODEL), tile), pl.BlockSpec((TM, D_MODEL), tile),
                   pl.BlockSpec((nk, TM), lambda i: (0, i)),
                   pl.BlockSpec((TM, nk), tile),
                   pl.BlockSpec((TM, nk), _prompt_tile), pl.BlockSpec((TM, nk), _prompt_tile)),
        compiler_params=_params(40, "arbitrary"),
        name="attn_in_proj",
    )(x1, y_pairs, route_w, mod_prev, mod, nw, w_in, cos, sin)


def _attend(q, kt_all, v_all, mask, sink_ref, o_ref):
    nq = q.shape[0]
    group_lanes = ATT_GROUP * ATT_HEAD_DIM
    lane = lax.broadcasted_iota(jnp.int32, (nq, group_lanes), 1)
    mine = [(lane // ATT_HEAD_DIM) == g for g in range(ATT_GROUP)]
    row_head = lax.broadcasted_iota(jnp.int32, (ATT_GROUP * nq, 1), 0) // nq
    if mask is not None:
        mask = jnp.concatenate([mask] * ATT_GROUP, axis=0)
    for hk in range(ATT_KV_HEADS):
        vh = v_all[:, hk * ATT_HEAD_DIM:(hk + 1) * ATT_HEAD_DIM]
        kt = jnp.concatenate([kt_all[hk * ATT_HEAD_DIM:(hk + 1) * ATT_HEAD_DIM, :]] * ATT_GROUP, axis=0)
        vt = jnp.concatenate([vh] * ATT_GROUP, axis=1)
        qg = q[:, hk * group_lanes:(hk + 1) * group_lanes]
        q_stack = jnp.concatenate([jnp.where(mine[g], qg, jnp.zeros_like(qg)) for g in range(ATT_GROUP)], axis=0)
        s = jnp.dot(q_stack, kt, preferred_element_type=F32)
        if mask is not None:
            s = jnp.where(mask, s, -jnp.inf)
        sink = jnp.zeros((ATT_GROUP * nq, 1), F32)
        for g in range(ATT_GROUP):
            sink = jnp.where(row_head == g, sink_ref[hk * ATT_GROUP + g], sink)
        m = jnp.maximum(jnp.max(s, axis=-1, keepdims=True), sink)
        p = jnp.exp(s - m)
        denom = jnp.sum(p, axis=-1, keepdims=True) + jnp.exp(sink - m)
        o = jnp.dot(p.astype(BF16), vt, preferred_element_type=F32) / denom
        acc = jnp.where(mine[0], o[:nq], 0.0)
        for g in range(1, ATT_GROUP):
            acc = acc + jnp.where(mine[g], o[g * nq:(g + 1) * nq], 0.0)
        o_ref[:, hk * group_lanes:(hk + 1) * group_lanes] = acc.astype(BF16)


def _ctx_attn_kernel(sink_ref, q_ref, k_ref, v_ref, o_ref):
    _attend(q_ref[...], k_ref[...], v_ref[...], None, sink_ref, o_ref)


def _lat_attn_kernel(sink_ref, q_ref, kp_ref, kc_ref, kn_ref, vp_ref, vc_ref, vn_ref, ck_ref, cv_ref, o_ref):
    jb = pl.program_id(1)
    kt_all = jnp.concatenate([kp_ref[...], kc_ref[...], kn_ref[...], ck_ref[0].astype(BF16)], axis=1)
    v_all = jnp.concatenate([vp_ref[...], vc_ref[...], vn_ref[...], cv_ref[0].astype(BF16)], axis=0)
    nkeys = 3 * BLOCK + PAST_LEN
    qi = lax.broadcasted_iota(jnp.int32, (BLOCK, nkeys), 0)
    kj = lax.broadcasted_iota(jnp.int32, (BLOCK, nkeys), 1)
    qpos = jb * BLOCK + qi
    kpos = (jb - 1) * BLOCK + kj
    local_ok = (jnp.abs(qpos - kpos) <= WINDOW) & (kpos >= 0) & (kpos < DEC_SEQ)
    mask = (kj >= 3 * BLOCK) | local_ok
    _attend(q_ref[...], kt_all, v_all, mask, sink_ref, o_ref)


def _attention(q, kt, v, cache_kt, cache_v, sink):
    nk = ATT_KV_HEADS * ATT_HEAD_DIM
    smem = pl.BlockSpec(memory_space=pltpu.SMEM)
    ctx = pl.pallas_call(
        _ctx_attn_kernel,
        out_shape=jax.ShapeDtypeStruct((N_PROMPT, D_MODEL), BF16),
        grid=(BATCH,),
        in_specs=[
            smem,
            pl.BlockSpec((SEQ, D_MODEL), lambda b: (b, 0)),
            pl.BlockSpec((nk, SEQ), lambda b: (0, b)),
            pl.BlockSpec((SEQ, nk), lambda b: (b, 0)),
        ],
        out_specs=pl.BlockSpec((SEQ, D_MODEL), lambda b: (b, 0)),
        compiler_params=_params(40, "arbitrary"),
        name="context_attention",
    )(sink, q, kt, v)

    nb = DEC_SEQ // BLOCK
    base = N_PROMPT // BLOCK
    cur = lambda b, j: (base + b * nb + j, 0)
    prev = lambda b, j: (base + b * nb + jnp.maximum(j - 1, 0), 0)
    nxt = lambda b, j: (base + b * nb + jnp.minimum(j + 1, nb - 1), 0)
    kv_blk = lambda f: pl.BlockSpec((BLOCK, nk), f)
    kt_blk = lambda f: pl.BlockSpec((nk, BLOCK), lambda b, j: f(b, j)[::-1])
    cache_blk = pl.BlockSpec((1, PAST_LEN, nk), lambda b, j: (b, 0, 0))
    cache_kt_blk = pl.BlockSpec((1, nk, PAST_LEN), lambda b, j: (b, 0, 0))
    lat = pl.pallas_call(
        _lat_attn_kernel,
        out_shape=jax.ShapeDtypeStruct((DEC_BATCH * DEC_SEQ, D_MODEL), BF16),
        grid=(DEC_BATCH, nb),
        in_specs=[
            smem,
            pl.BlockSpec((BLOCK, D_MODEL), cur),
            kt_blk(prev), kt_blk(cur), kt_blk(nxt),
            kv_blk(prev), kv_blk(cur), kv_blk(nxt),
            cache_kt_blk, cache_blk,
        ],
        out_specs=pl.BlockSpec((BLOCK, D_MODEL), lambda b, j: (b * nb + j, 0)),
        compiler_params=_params(40, "arbitrary", "arbitrary"),
        name="latent_attention",
    )(sink, q, kt, kt, kt, v, v, v, cache_kt, cache_v)
    return ctx, lat


def kernel(x_prompt, x_sample, state_hgrn, cache_k, cache_v, c, c_ctx, ada_w, ada_b, norm_w, hg_w_in,
           hg_lb_logits, hg_onorm, hg_w_out, at_w_in, at_sink, at_w_out, moe_w_group, moe_b_group,
           moe_w_expert, moe_b_expert, moe_w1, moe_w3, moe_w2, final_norm):
    xp = x_prompt.reshape(N_PROMPT, D_MODEL)
    xs = x_sample.reshape(N_TOK - N_PROMPT, D_MODEL)
    cond = jnp.concatenate([c_ctx[None, :], c], axis=0)
    mod = _ada(cond, ada_w, ada_b)
    nk = ATT_KV_HEADS * ATT_HEAD_DIM

    def router_params(i):
        pad = jnp.zeros((D_MODEL, ROUTER_LANES - N_EXPERTS - N_GROUPS), F32)
        w = jnp.concatenate([moe_w_expert[i], moe_w_group[i], pad], axis=1)
        b = jnp.concatenate([moe_b_expert[i], moe_b_group[i], pad[0]])[None, :]
        hi = w.astype(BF16)
        lo = (w - hi.astype(F32)).astype(BF16)
        return jnp.stack([hi, lo]), b

    tile = lambda i: (i, 0)
    blk = pl.BlockSpec((TM, D_MODEL), tile)

    prompt_blk = pl.BlockSpec((TM, D_MODEL), _prompt_tile)
    sample_blk = pl.BlockSpec((TM, D_MODEL), _sample_tile)
    q, v, g, lff, kf, lfb, kb = _hg_in(xp, xs, mod[0], norm_w[0, 0][None, :], hg_w_in[0].astype(BF16), hg_lb_logits)
    o_f, o_b, state_new = _gla(q, v, lff, kf, lfb, kb, state_hgrn)
    wr, br = router_params(0)
    routed = _mixer_out(
        _hg_out_kernel, "hgrn_out_route", (o_f, o_b, g, hg_onorm[0][None, :], xp, xs),
        (blk, blk, blk, pl.BlockSpec((1, HG_DK), lambda i: (0, 0)), prompt_blk, sample_blk),
        mod[0], norm_w[0, 1][None, :], hg_w_out[0].astype(BF16), wr, br)
    moe_out = _moe_layer(*routed, 0, moe_w1, moe_w3, moe_w2)

    cos, sin = _rope_tables()
    x, qa, ka, va, k_ctx, v_ctx = _at_in(*moe_out, mod[0], mod[1], norm_w[1, 0][None, :],
                                         at_w_in[0].astype(BF16), cos, sin)
    attn_ctx, attn_lat = _attention(qa, ka, va, cache_k[:, 0].reshape(DEC_BATCH, PAST_LEN, nk).transpose(0, 2, 1),
                                    cache_v[:, 0].reshape(DEC_BATCH, PAST_LEN, nk), at_sink[0])
    wr, br = router_params(1)
    routed = _mixer_out(
        _at_out_kernel, "attn_out_route", (attn_ctx, attn_lat, x), (prompt_blk, sample_blk, blk),
        mod[1], norm_w[1, 1][None, :], at_w_out[0].astype(BF16), wr, br)
    y_prompt, y_sample = _moe_combine(*_moe_layer(*routed, 1, moe_w1, moe_w3, moe_w2), mod[1], final_norm[None, :])

    def cache(feature_major):
        return feature_major.reshape(BATCH, 1, ATT_KV_HEADS, ATT_HEAD_DIM, SEQ).transpose(0, 1, 4, 2, 3)

    return (y_prompt.reshape(BATCH, SEQ, D_MODEL), y_sample.reshape(DEC_BATCH, DEC_SEQ, D_MODEL), state_new,
            cache(k_ctx), cache(v_ctx))
```

```python
import functools
from typing import Any, NamedTuple

import jax
import jax.numpy as jnp
import numpy as np
from jax import lax
from jax.experimental import pallas as pl
from jax.experimental.pallas import tpu as pltpu
from jax.experimental.pallas import tpu_sc as plsc

F32 = jnp.float32
BF16 = jnp.bfloat16

D_MODEL = 1024
BATCH = 16
SEQ = 256
DEC_BATCH = 2
DEC_SEQ = 1024
PAST_LEN = 512
GRID_W = 64
HG_HEADS = 8
HG_DK = 128
CHUNK = 16
ATT_HEAD_DIM = 64
ATT_Q_HEADS = 16
ATT_KV_HEADS = 4
ATT_GROUP = 4
WINDOW = 128
BLOCK = 128
ROPE_HALF = 32
ROPE_BASE = 10000.0
N_GROUPS = 4
EXPERTS_PER_GROUP = 8
N_EXPERTS = 32
D_EXPERT = 256
EPS = 1e-6

N_PROMPT = BATCH * SEQ
N_TOK = N_PROMPT + DEC_BATCH * DEC_SEQ
TM = 256
N_TILES = N_TOK // TM
PROMPT_TILES = N_PROMPT // TM
TILES_PER_DEC_SEQ = DEC_SEQ // TM
LANES = 128
SUBLANES = 8
ADA_LANE_BLOCKS = 4
N_COND_USED = 1 + DEC_BATCH
N_COND = 8
ROUTER_LANES = 128
ROUTE_TM = 1024
ROUTE_ROWS = 8
MIX_TM = 512
HG_IN_TM = 512
HG_IN_SUB = 256
HG_IN_TN = 256
DECAY_CLAMP = 60.0
PACK_WORDS = D_MODEL // 4
ROW_WORDS = 2 * PACK_WORDS
MOE_TILE = 256
MOE_STEP_TILES = 4
MOE_WEIGHT_SLOTS = 2 * MOE_STEP_TILES
MOE_ROWS = 2 * N_TOK + N_EXPERTS * MOE_TILE
MIB = 1024 * 1024


def _params(vmem_mib, *semantics):
    return pltpu.CompilerParams(dimension_semantics=semantics, vmem_limit_bytes=vmem_mib * MIB)


def _tile_cond(i):
    return jnp.where(i < PROMPT_TILES, 0, 1 + (i - PROMPT_TILES) // TILES_PER_DEC_SEQ)


def _prompt_tile(i):
    return (jnp.minimum(i, PROMPT_TILES - 1), 0)


def _sample_tile(i):
    return (jnp.maximum(i - PROMPT_TILES, 0), 0)


def _mod_row(mod_ref, cond, which):
    return mod_ref[pl.ds(cond, 1), which * D_MODEL:(which + 1) * D_MODEL]


def _norm_mod(x, nw, shift, scale):
    y = x * lax.rsqrt(jnp.mean(x * x, axis=-1, keepdims=True) + EPS)
    return (y * nw) * (1.0 + scale) + shift


def _silu(x):
    return x * jax.nn.sigmoid(x)


def _ada_kernel(c_ref, w_ref, b_ref, o_ref, s_scr):
    @pl.when((pl.program_id(0) == 0) & (pl.program_id(1) == 0))
    def _():
        s_scr[...] = _silu(c_ref[...])

    tn = w_ref.shape[-1]
    cols = [[] for _ in range(N_COND_USED)]
    for j0 in range(0, tn // LANES, ADA_LANE_BLOCKS):
        acc = [[jnp.zeros((SUBLANES, LANES), F32) for _ in range(ADA_LANE_BLOCKS)] for _ in range(N_COND_USED)]
        for g in range(D_MODEL // SUBLANES):
            rows = slice(g * SUBLANES, (g + 1) * SUBLANES)
            w = [w_ref[0, rows, (j0 + j) * LANES:(j0 + j + 1) * LANES] for j in range(ADA_LANE_BLOCKS)]
            for r in range(N_COND_USED):
                s = s_scr[r, rows, :]
                for j in range(ADA_LANE_BLOCKS):
                    acc[r][j] = acc[r][j] + w[j] * s
        for r in range(N_COND_USED):
            cols[r] += [jnp.sum(a, axis=0, keepdims=True) for a in acc[r]]
    out = [jnp.concatenate(cols[r], axis=1) + b_ref[0] for r in range(N_COND_USED)]
    out.append(jnp.zeros((N_COND - N_COND_USED, tn), F32))
    o_ref[0] = jnp.concatenate(out, axis=0)


def _ada(cond, ada_w, ada_b):
    depth, _, n = ada_w.shape
    tn = 1536
    cond_cols = jnp.broadcast_to(cond[:, :, None], (N_COND_USED, D_MODEL, LANES))
    return pl.pallas_call(
        _ada_kernel,
        out_shape=jax.ShapeDtypeStruct((depth, N_COND, n), F32),
        grid=(depth, n // tn),
        in_specs=[
            pl.BlockSpec((N_COND_USED, D_MODEL, LANES), lambda l, j: (0, 0, 0)),
            pl.BlockSpec((1, D_MODEL, tn), lambda l, j: (l, 0, j)),
            pl.BlockSpec((1, 1, tn), lambda l, j: (l, 0, j)),
        ],
        out_specs=pl.BlockSpec((1, N_COND, tn), lambda l, j: (l, 0, j)),
        scratch_shapes=[pltpu.VMEM((N_COND_USED, D_MODEL, LANES), F32)],
        compiler_params=_params(40, "arbitrary", "arbitrary"),
        name="ada_modulation",
    )(cond_cols, ada_w, ada_b.reshape(depth, 1, n))


def _hg_in_kernel(xp_ref, xs_ref, mod_ref, nw_ref, w_ref, lbl_ref,
                  q_ref, v_ref, g_ref, lff_ref, kf_ref, lfb_ref, kb_ref):
    i = pl.program_id(0) * (HG_IN_TM // TM)
    cond = _tile_cond(i)
    shift, scale = _mod_row(mod_ref, cond, 0), _mod_row(mod_ref, cond, 1)

    def normed(rows):
        x = jnp.where(i < PROMPT_TILES, xp_ref[rows, :], xs_ref[rows, :])
        return _norm_mod(x, nw_ref[...], shift, scale).astype(BF16)

    def proj(h, c, cols):
        return jnp.dot(h, w_ref[:, c * D_MODEL + cols.start:c * D_MODEL + cols.stop], preferred_element_type=F32)

    l0, l1, l2 = lbl_ref[0], lbl_ref[1], lbl_ref[2]
    m = jnp.maximum(jnp.maximum(l0, l1), l2)
    e0, e1, e2 = jnp.exp(l0 - m), jnp.exp(l1 - m), jnp.exp(l2 - m)
    lb = e0 / (e0 + e1 + e2)
    col_tiles = [slice(n * HG_IN_TN, (n + 1) * HG_IN_TN) for n in range(D_MODEL // HG_IN_TN)]

    def gate_and_plain(h, rows, d):
        lf_ref, k_ref, plain_ref = ((lff_ref, kf_ref, q_ref), (lfb_ref, kb_ref, v_ref))[d]
        for cols in col_tiles:
            lbd = lb[d:d + 1, cols]
            z_gate = proj(h, 2 + d, cols)
            z_plain = proj(h, d, cols)
            f = lbd + (1.0 - lbd) * jax.nn.sigmoid(z_gate)
            lf = jnp.log(f)
            hi = lf.astype(BF16)
            lf_ref[0, rows, cols] = hi
            lf_ref[1, rows, cols] = (lf - hi.astype(F32)).astype(BF16)
            k_ref[rows, cols] = (1.0 - f).astype(BF16)
            plain_ref[rows, cols] = z_plain.astype(BF16)

    sub_tiles = [slice(s * HG_IN_SUB, (s + 1) * HG_IN_SUB) for s in range(HG_IN_TM // HG_IN_SUB)]
    h = normed(sub_tiles[0])
    for s, rows in enumerate(sub_tiles):
        gate_and_plain(h, rows, 0)
        h_next = normed(sub_tiles[s + 1]) if s + 1 < len(sub_tiles) else None
        gate_and_plain(h, rows, 1)
        for cols in col_tiles:
            g_ref[rows, cols] = proj(h, 4, cols).astype(BF16)
        h = h_next


def _hg_in(xp, xs, mod, nw, w_in, lb_logits):
    tile = lambda i: (i, 0)
    fixed2 = lambda i: (0, 0)
    bf = jax.ShapeDtypeStruct((N_TOK, D_MODEL), BF16)
    ff = jax.ShapeDtypeStruct((2, N_TOK, D_MODEL), BF16)
    blk = pl.BlockSpec((HG_IN_TM, D_MODEL), tile)
    split_blk = pl.BlockSpec((2, HG_IN_TM, D_MODEL), lambda i: (0, i, 0))
    prompt_steps = N_PROMPT // HG_IN_TM
    return pl.pallas_call(
        _hg_in_kernel,
        out_shape=(bf, bf, bf, ff, bf, ff, bf),
        grid=(N_TOK // HG_IN_TM,),
        in_specs=[
            pl.BlockSpec((HG_IN_TM, D_MODEL), lambda i: (jnp.minimum(i, prompt_steps - 1), 0)),
            pl.BlockSpec((HG_IN_TM, D_MODEL), lambda i: (jnp.maximum(i - prompt_steps, 0), 0)),
            pl.BlockSpec(mod.shape, fixed2),
            pl.BlockSpec((1, D_MODEL), fixed2),
            pl.BlockSpec(w_in.shape, fixed2, pipeline_mode=pl.Buffered(1)),
            pl.BlockSpec(lb_logits.shape, lambda i: (0, 0, 0)),
        ],
        out_specs=(blk, blk, blk, split_blk, blk, split_blk, blk),
        compiler_params=_params(56, "arbitrary"),
        name="hgrn_in_proj",
    )(xp, xs, mod, nw, w_in, lb_logits)


def _gla_exact(reverse, q_ref, v_ref, lf_ref, k_ref, st_ref, o_ref):
    nt = (((1,), (1,)), ((), ()))
    tn = (((0,), (0,)), ((), ()))
    n_chunks = TM // CHUNK
    row_id = lax.broadcasted_iota(jnp.int32, (CHUNK, HG_DK), 0)
    order = range(CHUNK - 1, -1, -1) if reverse else range(CHUNK)

    def chunk_head(it, carry):
        step, h = it // HG_HEADS, it % HG_HEADS
        ci = (n_chunks - 1 - step) if reverse else step
        rows = pl.ds(pl.multiple_of(ci * CHUNK, CHUNK), CHUNK)
        cols = pl.ds(pl.multiple_of(h * HG_DK, HG_DK), HG_DK)
        q, k, v = q_ref[rows, cols], k_ref[rows, cols], v_ref[rows, cols]
        f = jnp.exp(lf_ref[0, rows, cols].astype(F32) + lf_ref[1, rows, cols].astype(F32))
        st = st_ref[0, h]
        o = jnp.zeros((CHUNK, HG_DK), F32)
        for t in order:
            one = row_id == t
            v_t = jnp.where(one, v, jnp.zeros_like(v))
            st = st * f[t:t + 1, :] + lax.dot_general(v_t, k, tn, preferred_element_type=F32)
            o = jnp.where(one, lax.dot_general(q, st.astype(BF16), nt, preferred_element_type=F32), o)
        st_ref[0, h] = st
        o_ref[rows, cols] = o.astype(o_ref.dtype)
        return carry

    lax.fori_loop(0, n_chunks * HG_HEADS, chunk_head, 0)


class _GlaDirection(NamedTuple):
    reverse: bool
    q: Any
    v: Any
    lf: Any
    k: Any
    o: Any
    st: Any
    sw: Any
    ut: Any
    qd: Any
    kd: Any
    ki: Any
    cd: Any
    b: Any


_NT = (((1,), (1,)), ((), ()))
_TN = (((0,), (0,)), ((), ()))
_HG_COLS = [slice(h * HG_DK, (h + 1) * HG_DK) for h in range(HG_HEADS)]
_N_CHUNKS = TM // CHUNK
_N_PAIRS = _N_CHUNKS // 2


def _gla_prepare(d):
    r = lax.broadcasted_iota(jnp.int32, (TM, TM), 0)
    c = lax.broadcasted_iota(jnp.int32, (TM, TM), 1)
    same = (r // CHUNK) == (c // CHUNK)
    tri = (same & ((c >= r) if d.reverse else (c <= r))).astype(BF16)
    b = jnp.dot(tri, d.lf[0], preferred_element_type=F32) + jnp.dot(tri, d.lf[1], preferred_element_type=F32)
    d.b[...] = b
    edge = 0 if d.reverse else CHUNK - 1
    total = b.reshape(_N_CHUNKS, CHUNK, D_MODEL)[:, edge, :]
    d.cd[...] = jnp.exp(total)
    return jnp.max(-total) <= DECAY_CLAMP


def _gla_pair_rows(d, step):
    pi = (_N_PAIRS - 1 - step) if d.reverse else step
    row0 = pi * 2 * CHUNK
    lo, hi = (pl.ds(row0, CHUNK), 2 * pi), (pl.ds(row0 + CHUNK, CHUNK), 2 * pi + 1)
    return pl.ds(row0, 2 * CHUNK), ((hi, lo) if d.reverse else (lo, hi))


def _chunk_decay(d, chunk):
    return d.cd[pl.ds(chunk, 1), :]


def _gla_decayed_operands(d, step):
    both, ((_, c_first), (_, c_second)) = _gla_pair_rows(d, step)
    b = d.b[both, :]
    ki = d.k[both, :] * jnp.exp(jnp.minimum(-b, DECAY_CLAMP)).astype(BF16)
    d.qd[both, :] = d.q[both, :] * jnp.exp(b).astype(BF16)
    d.ki[both, :] = ki
    c_lo, c_hi = (c_second, c_first) if d.reverse else (c_first, c_second)
    ends = jnp.concatenate([jnp.broadcast_to(_chunk_decay(d, c).astype(BF16), (CHUNK, D_MODEL)) for c in (c_lo, c_hi)],
                           axis=0)
    d.kd[both, :] = ki * ends


def _rows_scaled(x, scale_row, second_half):
    scale = jnp.broadcast_to(scale_row.astype(BF16), (CHUNK, x.shape[1]))
    ones = jnp.ones((CHUNK, x.shape[1]), BF16)
    return x * jnp.concatenate([ones, scale] if second_half else [scale, ones], axis=0)


def _gla_key_value_product(d, step):
    _gla_decayed_operands(d, step)
    both, (_, (_, c_second)) = _gla_pair_rows(d, step)
    decay_second = _chunk_decay(d, c_second)
    for h, cols in enumerate(_HG_COLS):
        keys = _rows_scaled(d.kd[both, cols], decay_second[:, cols], second_half=d.reverse)
        d.ut[h] = lax.dot_general(d.v[both, cols], keys, _TN, preferred_element_type=F32)


def _gla_start(d):
    for h in range(HG_HEADS):
        d.sw[0, h] = d.st[0, h].T.astype(BF16)
    _gla_key_value_product(d, 0)


def _gla_pair(d, step, src, dst):
    sr = lax.broadcasted_iota(jnp.int32, (CHUNK, CHUNK), 0)
    sc = lax.broadcasted_iota(jnp.int32, (CHUNK, CHUNK), 1)
    tr = lax.broadcasted_iota(jnp.int32, (CHUNK, 2 * CHUNK), 0)
    tc = lax.broadcasted_iota(jnp.int32, (CHUNK, 2 * CHUNK), 1)
    if d.reverse:
        keep_first = sc >= sr
        keep_second = (tc >= tr) | (tc >= CHUNK)
    else:
        keep_first = sc <= sr
        keep_second = (tc < CHUNK) | (tc - CHUNK <= tr)
    both, ((first, c_first), (second, c_second)) = _gla_pair_rows(d, step)
    decay_first = _chunk_decay(d, c_first)
    decay_both = decay_first * _chunk_decay(d, c_second)
    for h, cols in enumerate(_HG_COLS):
        new = d.st[src, h] * decay_both[:, cols] + d.ut[h]
        d.st[dst, h] = new
        d.sw[dst, h] = new.T.astype(BF16)
    a_first = [lax.dot_general(d.qd[first, cols], d.ki[first, cols], _NT, preferred_element_type=F32)
               for cols in _HG_COLS]
    a_second = []
    for cols in _HG_COLS:
        lo_keys = (d.ki if d.reverse else d.kd)[pl.ds(both.start, CHUNK), cols]
        hi_keys = (d.kd if d.reverse else d.ki)[pl.ds(both.start + CHUNK, CHUNK), cols]
        keys = jnp.concatenate([lo_keys, hi_keys], axis=0)
        a_second.append(lax.dot_general(d.qd[second, cols], keys, _NT, preferred_element_type=F32))
    inter = [jnp.dot(_rows_scaled(d.qd[both, cols], decay_first[:, cols], second_half=not d.reverse),
                     d.sw[src, h], preferred_element_type=F32) for h, cols in enumerate(_HG_COLS)]
    if step + 1 < _N_PAIRS:
        _gla_key_value_product(d, step + 1)
    first_half, second_half = (slice(CHUNK, None), slice(None, CHUNK)) if d.reverse else \
                              (slice(None, CHUNK), slice(CHUNK, None))
    for h, cols in enumerate(_HG_COLS):
        am = jnp.where(keep_first, a_first[h], 0.0).astype(BF16)
        d.o[first, cols] = (jnp.dot(am, d.v[first, cols], preferred_element_type=F32)
                            + inter[h][first_half]).astype(d.o.dtype)
        am = jnp.where(keep_second, a_second[h], 0.0).astype(BF16)
        d.o[second, cols] = (jnp.dot(am, d.v[both, cols], preferred_element_type=F32)
                             + inter[h][second_half]).astype(d.o.dtype)


def _gla_kernel(qf_ref, vf_ref, lff_ref, kf_ref, qb_ref, vb_ref, lfb_ref, kb_ref, s0_ref,
                of_ref, ob_ref, sout_ref, st_scr, sw_scr, ut_scr, qd_scr, kd_scr, ki_scr, cd_scr, b_scr):
    i = pl.program_id(0)
    is_prompt = i < PROMPT_TILES
    first = jnp.logical_or(is_prompt, (i - PROMPT_TILES) % TILES_PER_DEC_SEQ == 0)
    work = [tuple(scr.at[n] for scr in (st_scr, sw_scr, ut_scr, qd_scr, kd_scr, ki_scr, cd_scr, b_scr))
            for n in range(2)]
    fwd = _GlaDirection(False, qf_ref, vf_ref, lff_ref, kf_ref, of_ref, *work[0])
    bwd = _GlaDirection(True, qb_ref, vb_ref, lfb_ref, kb_ref, ob_ref, *work[1])

    @pl.when(is_prompt)
    def _():
        for d in (fwd, bwd):
            d.st[0] = jnp.zeros((HG_HEADS, HG_DK, HG_DK), F32)

    @pl.when(jnp.logical_and(first, jnp.logical_not(is_prompt)))
    def _():
        for n, d in enumerate((fwd, bwd)):
            for h in range(HG_HEADS):
                d.st[0, h] = s0_ref[0, 0, n, h].T

    exact_scores = jnp.logical_and(_gla_prepare(fwd), _gla_prepare(bwd))

    @pl.when(exact_scores)
    def _():
        _gla_start(fwd)
        _gla_start(bwd)

        for step in range(_N_PAIRS):
            for d in (fwd, bwd):
                _gla_pair(d, step, step % 2, 1 - step % 2)

    @pl.when(jnp.logical_not(exact_scores))
    def _():
        for d in (fwd, bwd):
            _gla_exact(d.reverse, d.q, d.v, d.lf, d.k, d.st, d.o)

    @pl.when(is_prompt)
    def _():
        for n, d in enumerate((fwd, bwd)):
            for h in range(HG_HEADS):
                sout_ref[0, 0, n, h] = d.st[0, h].T


def _gla(q, v, lff, kf, lfb, kb, state_hgrn):
    def fwd_tile(i):
        return (i, 0)

    def bwd_tile(i):
        j = (i - PROMPT_TILES) % TILES_PER_DEC_SEQ
        return (jnp.where(i < PROMPT_TILES, i, i - j + (TILES_PER_DEC_SEQ - 1 - j)), 0)

    def s0_idx(i):
        return (jnp.maximum(i - PROMPT_TILES, 0) // TILES_PER_DEC_SEQ, 0, 0, 0, 0, 0)

    def sout_idx(i):
        return (jnp.minimum(i, PROMPT_TILES - 1), 0, 0, 0, 0, 0)

    f_blk = pl.BlockSpec((TM, D_MODEL), fwd_tile)
    b_blk = pl.BlockSpec((TM, D_MODEL), bwd_tile)
    f_split = pl.BlockSpec((2, TM, D_MODEL), lambda i: (0,) + fwd_tile(i))
    b_split = pl.BlockSpec((2, TM, D_MODEL), lambda i: (0,) + bwd_tile(i))
    st_blk = (1, 1, 2, HG_HEADS, HG_DK, HG_DK)
    return pl.pallas_call(
        _gla_kernel,
        out_shape=(
            jax.ShapeDtypeStruct((N_TOK, D_MODEL), BF16),
            jax.ShapeDtypeStruct((N_TOK, D_MODEL), BF16),
            jax.ShapeDtypeStruct((BATCH,) + st_blk[1:], F32),
        ),
        grid=(N_TILES,),
        in_specs=[f_blk, f_blk, f_split, f_blk, b_blk, b_blk, b_split, b_blk, pl.BlockSpec(st_blk, s0_idx)],
        out_specs=(f_blk, b_blk, pl.BlockSpec(st_blk, sout_idx)),
        scratch_shapes=[
            pltpu.VMEM((2, 2, HG_HEADS, HG_DK, HG_DK), F32),
            pltpu.VMEM((2, 2, HG_HEADS, HG_DK, HG_DK), BF16),
            pltpu.VMEM((2, HG_HEADS, HG_DK, HG_DK), F32),
            pltpu.VMEM((2, TM, D_MODEL), BF16),
            pltpu.VMEM((2, TM, D_MODEL), BF16),
            pltpu.VMEM((2, TM, D_MODEL), BF16),
            pltpu.VMEM((2, TM // CHUNK, D_MODEL), F32),
            pltpu.VMEM((2, TM, D_MODEL), F32),
        ],
        compiler_params=_params(48, "arbitrary"),
        name="hgrn_recurrence",
    )(q, v, lff, kf, q, v, lfb, kb, state_hgrn)


def _route(logits):
    lane = lax.broadcasted_iota(jnp.int32, logits.shape, 1)
    neg = jnp.float32(-jnp.inf)
    big = jnp.int32(ROUTER_LANES)

    def first_max(x):
        m = jnp.max(x, axis=-1, keepdims=True)
        return m, jnp.min(jnp.where(x == m, lane, big), axis=-1, keepdims=True)

    is_group = (lane >= N_EXPERTS) & (lane < N_EXPERTS + N_GROUPS)
    gl = jnp.where(is_group, logits, neg)
    gmax, g_lane = first_max(gl)
    g_sel = g_lane - N_EXPERTS
    gsum = jnp.sum(jnp.exp(gl - gmax), axis=-1, keepdims=True)
    p_g = 1.0 / gsum
    in_sel = (lane < N_EXPERTS) & ((lane // EXPERTS_PER_GROUP) == g_sel)
    m1, i1 = first_max(jnp.where(in_sel, logits, neg))
    m2, i2 = first_max(jnp.where(in_sel & (lane != i1), logits, neg))
    e2 = jnp.exp(m2 - m1)
    return i1, i2, p_g / (1.0 + e2), p_g * e2 / (1.0 + e2)


def _pack_rows(x):
    q = PACK_WORDS
    bits = pltpu.bitcast(x.astype(BF16).astype(F32), jnp.uint32)
    return [(bits[:, (2 + h) * q:(3 + h) * q] & jnp.uint32(0xFFFF0000)) | (bits[:, h * q:(h + 1) * q] >> 16)
            for h in range(2)]


def _unpack_rows(half0, half1):
    lo = lambda w: pltpu.bitcast(w << 16, F32).astype(BF16)
    hi = lambda w: pltpu.bitcast(w & jnp.uint32(0xFFFF0000), F32).astype(BF16)
    return [lo(half0), lo(half1), hi(half0), hi(half1)]


def _mix_stream_rows(prompt_ref, sample_ref, rows):
    return jnp.where(pl.program_id(0) < N_PROMPT // MIX_TM, prompt_ref[rows, :], sample_ref[rows, :])


def _mixer_tail(mix_of, x_of, mod_ref, nw2_ref, wo_ref, wr_ref, br_ref, x1_ref, h2p_ref, lg_ref):
    cond = _tile_cond(pl.program_id(0) * (MIX_TM // TM))
    gate, shift, scale = (_mod_row(mod_ref, cond, which) for which in (2, 3, 4))
    project = lambda rows: jnp.dot(mix_of(rows), wo_ref[...], preferred_element_type=F32)

    def tail(rows, out):
        x1 = x_of(rows) + gate * out
        x1_ref[rows, :] = x1
        h2 = _norm_mod(x1, nw2_ref[...], shift, scale)
        h2p_ref[rows, :] = jnp.concatenate(_pack_rows(h2), axis=1)
        h_hi = h2.astype(BF16)
        h_lo = (h2 - h_hi.astype(F32)).astype(BF16)
        lg_ref[rows, :] = (jnp.dot(h_hi, wr_ref[0], preferred_element_type=F32)
                           + (jnp.dot(h_hi, wr_ref[1], preferred_element_type=F32)
                              + jnp.dot(h_lo, wr_ref[0], preferred_element_type=F32))) + br_ref[...]

    sub_tiles = [slice(s * TM, (s + 1) * TM) for s in range(MIX_TM // TM)]
    out = project(sub_tiles[0])
    for s, rows in enumerate(sub_tiles):
        out_next = project(sub_tiles[s + 1]) if s + 1 < len(sub_tiles) else None
        tail(rows, out)
        out = out_next


def _route_kernel(lg_ref, rt_ref, rw_ref, cnt_ref, carry_scr):
    @pl.when(pl.program_id(0) == 0)
    def _():
        carry_scr[...] = jnp.zeros_like(carry_scr)

    logits = lg_ref[...]
    i1, i2, w1, w2 = _route(logits)
    lane = lax.broadcasted_iota(jnp.int32, logits.shape, 1)
    chosen = ((lane == i1) | (lane == i2)).astype(BF16)
    r = lax.broadcasted_iota(jnp.int32, (ROUTE_TM, ROUTE_TM), 0)
    c = lax.broadcasted_iota(jnp.int32, (ROUTE_TM, ROUTE_TM), 1)
    before = jnp.dot((c < r).astype(BF16), chosen, preferred_element_type=F32) + carry_scr[...]
    r1 = jnp.sum(jnp.where(lane == i1, before, 0.0), axis=-1, keepdims=True).astype(jnp.int32)
    r2 = jnp.sum(jnp.where(lane == i2, before, 0.0), axis=-1, keepdims=True).astype(jnp.int32)
    total = carry_scr[...] + jnp.sum(chosen.astype(F32), axis=0, keepdims=True)
    carry_scr[...] = total
    tiles = jnp.ceil(total * (1.0 / MOE_TILE))
    e_from = lax.broadcasted_iota(jnp.int32, (ROUTER_LANES, ROUTER_LANES), 0)
    e_to = lax.broadcasted_iota(jnp.int32, (ROUTER_LANES, ROUTER_LANES), 1)
    tiles_before = jnp.dot(jnp.broadcast_to(tiles, (SUBLANES, ROUTER_LANES)).astype(BF16),
                           (e_from < e_to).astype(BF16), preferred_element_type=F32)
    cnt_ref[...] = jnp.concatenate([total, tiles_before[:1] * MOE_TILE,
                                    jnp.zeros((ROUTE_ROWS - 2, ROUTER_LANES), F32)], axis=0)
    by_token = jnp.where(lane == 0, i1, jnp.where(lane == 1, i2, jnp.where(lane == 2, r1, r2)))
    rt_ref[...] = by_token.T[:ROUTE_ROWS, :]
    rw_ref[...] = jnp.where(lane == 0, w1, w2)


def _route_tokens(logits):
    blk = pl.BlockSpec((ROUTE_TM, ROUTER_LANES), lambda i: (i, 0))
    return pl.pallas_call(
        _route_kernel,
        out_shape=(
            jax.ShapeDtypeStruct((ROUTE_ROWS, N_TOK), jnp.int32),
            jax.ShapeDtypeStruct((N_TOK, ROUTER_LANES), F32),
            jax.ShapeDtypeStruct((ROUTE_ROWS, ROUTER_LANES), F32),
        ),
        grid=(N_TOK // ROUTE_TM,),
        in_specs=[blk],
        out_specs=(pl.BlockSpec((ROUTE_ROWS, ROUTE_TM), lambda i: (0, i)), blk,
                   pl.BlockSpec((ROUTE_ROWS, ROUTER_LANES), lambda i: (0, 0))),
        scratch_shapes=[pltpu.VMEM((1, ROUTER_LANES), F32)],
        compiler_params=_params(32, "arbitrary"),
        name="moe_route",
    )(logits)


def _hg_out_kernel(of_ref, ob_ref, g_ref, on_ref, xp_ref, xs_ref, *rest):
    def mix_of(rows):
        o = of_ref[rows, :].astype(F32) + ob_ref[rows, :].astype(F32)
        parts = []
        for h in range(HG_HEADS):
            oh = o[:, h * HG_DK:(h + 1) * HG_DK]
            parts.append(oh * lax.rsqrt(jnp.mean(oh * oh, axis=-1, keepdims=True) + EPS) * on_ref[...])
        y = jnp.concatenate(parts, axis=1) * _silu(g_ref[rows, :].astype(F32))
        return y.astype(BF16)

    _mixer_tail(mix_of, functools.partial(_mix_stream_rows, xp_ref, xs_ref), *rest)


def _at_out_kernel(ac_ref, al_ref, x_ref, *rest):
    _mixer_tail(functools.partial(_mix_stream_rows, ac_ref, al_ref), lambda rows: x_ref[rows, :], *rest)


def _mixer_out(kernel_fn, name, mix_inputs, mix_specs, mod, nw2, w_out, w_router, b_router):
    tile = lambda i: (i, 0)
    fixed2 = lambda i: (0, 0)
    blk = pl.BlockSpec((MIX_TM, D_MODEL), tile)
    lanes_blk = pl.BlockSpec((MIX_TM, ROUTER_LANES), tile)
    return pl.pallas_call(
        kernel_fn,
        out_shape=(
            jax.ShapeDtypeStruct((N_TOK, D_MODEL), F32),
            jax.ShapeDtypeStruct((N_TOK, ROW_WORDS), jnp.uint32),
            jax.ShapeDtypeStruct((N_TOK, ROUTER_LANES), F32),
        ),
        grid=(N_TOK // MIX_TM,),
        in_specs=list(mix_specs) + [
            pl.BlockSpec(mod.shape, fixed2),
            pl.BlockSpec((1, D_MODEL), fixed2),
            pl.BlockSpec(w_out.shape, fixed2),
            pl.BlockSpec(w_router.shape, lambda i: (0, 0, 0)),
            pl.BlockSpec((1, ROUTER_LANES), fixed2),
        ],
        out_specs=(blk, pl.BlockSpec((MIX_TM, ROW_WORDS), tile), lanes_blk),
        compiler_params=_params(40, "arbitrary"),
        name=name,
    )(*mix_inputs, mod, nw2, w_out, w_router, b_router)


def _moe_plan(route_t, counts):
    cnt = counts[0, :N_EXPERTS].astype(jnp.int32)
    offs = counts[1, :N_EXPERTS].astype(jnp.int32)
    ends = offs + ((cnt + MOE_TILE - 1) // MOE_TILE) * MOE_TILE
    experts = route_t[0:2]
    pos = jnp.sum(jnp.where(experts[None, :, :] == jnp.arange(N_EXPERTS)[:, None, None], offs[:, None, None], 0),
                  axis=0) + route_t[2:4]
    tile_start = jnp.arange(MOE_ROWS // MOE_TILE, dtype=jnp.int32) * MOE_TILE
    tile_expert = jnp.minimum(jnp.sum(ends[None, :] <= tile_start[:, None], axis=1), N_EXPERTS - 1).astype(jnp.int32)
    of_tile = tile_expert[:, None] == jnp.arange(N_EXPERTS)[None, :]
    tile_offs = jnp.sum(jnp.where(of_tile, offs[None, :], 0), axis=1)
    tile_cnt = jnp.sum(jnp.where(of_tile, cnt[None, :], 0), axis=1)
    tile_rows = jnp.clip(tile_offs + tile_cnt - tile_start, 0, MOE_TILE).astype(jnp.int32)
    active = tile_start < ends[-1]
    tile_opens = jnp.where(active & (tile_start == tile_offs), tile_expert, -1).astype(jnp.int32)
    tile_opens = jnp.concatenate([tile_opens, jnp.full((MOE_STEP_TILES,), -1, jnp.int32)])
    n_active = (ends[-1] // MOE_TILE).astype(jnp.int32).reshape(1)
    slot = (jnp.cumsum((cnt > 0).astype(jnp.int32)) - 1) % MOE_WEIGHT_SLOTS
    tile_slot = jnp.sum(jnp.where(of_tile, slot[None, :], 0), axis=1)
    last_slot = jnp.sum(jnp.where(tile_start == ends[-1] - MOE_TILE, tile_slot, 0))
    tile_slot = jnp.where(active, tile_slot, last_slot).astype(jnp.int32)
    return pos.astype(jnp.int32), (tile_rows, tile_slot, tile_opens, n_active)


SC_WINDOW = 64


def _sc_mesh():
    return plsc.VectorSubcoreMesh(core_axis_name="c", subcore_axis_name="s")


def _sc_scatter_rows(x, idx_a, idx_b, n_out_rows):
    n = x.shape[0]

    @functools.partial(pl.kernel, out_type=jax.ShapeDtypeStruct((n_out_rows, ROW_WORDS), x.dtype), mesh=_sc_mesh(),
                       scratch_types=[pltpu.SemaphoreType.DMA, pltpu.SemaphoreType.DMA])
    def scatter(x_hbm, ia_hbm, ib_hbm, o_hbm, sem_a, sem_b):
        def body(x_vmem, ia_vmem, ib_vmem):
            copy_a = pltpu.async_copy(x_vmem, o_hbm.at[ia_vmem.at[0]], sem_a)
            copy_b = pltpu.async_copy(x_vmem, o_hbm.at[ib_vmem.at[0]], sem_b)
            copy_a.wait()
            copy_b.wait()

        idx_spec = pl.BlockSpec((1, SC_WINDOW), index_map=lambda i: (i, 0))
        pltpu.emit_pipeline(
            body, grid=(n // SC_WINDOW,),
            in_specs=[pl.BlockSpec((SC_WINDOW, ROW_WORDS), index_map=lambda i: (i, 0)), idx_spec, idx_spec],
            out_specs=[],
            core_axis_name=("c", "s"), dimension_semantics=(pltpu.PARALLEL,),
        )(x_hbm, ia_hbm, ib_hbm)

    return scatter(x, idx_a, idx_b)


def _sc_gather_rows(table, idx):
    n = idx.size

    @functools.partial(pl.kernel, out_type=jax.ShapeDtypeStruct((n, ROW_WORDS), table.dtype), mesh=_sc_mesh())
    def gather(t_hbm, i_hbm, o_hbm):
        def body(i_vmem, o_vmem):
            pltpu.sync_copy(t_hbm.at[i_vmem.at[0]], o_vmem)

        pltpu.emit_pipeline(
            body, grid=(n // SC_WINDOW,),
            in_specs=[pl.BlockSpec((1, SC_WINDOW), index_map=lambda i: (i, 0))],
            out_specs=[pl.BlockSpec((SC_WINDOW, ROW_WORDS), index_map=lambda i: (i, 0))],
            core_axis_name=("c", "s"), dimension_semantics=(pltpu.PARALLEL,),
        )(i_hbm, o_hbm)

    return gather(table, idx)


def _ffn_kernel(tr_ref, ts_ref, to_ref, na_ref, xs_ref, w1_hbm, w3_hbm, w2_hbm, ys_ref,
                w1_buf, w3_buf, w2_buf, sem, *, layer):
    def weight_copies(t):
        pairs = ((w1_hbm, w1_buf), (w3_hbm, w3_buf), (w2_hbm, w2_buf))
        return [pltpu.make_async_copy(hbm.at[layer, to_ref[t]], buf.at[ts_ref[t]], sem.at[ts_ref[t], j])
                for j, (hbm, buf) in enumerate(pairs)]

    def for_experts_opened_from(first_tile, action):
        for j in range(MOE_STEP_TILES):
            t = first_tile + j

            @pl.when(to_ref[t] >= 0)
            def _():
                for copy in weight_copies(t):
                    action(copy)

    def tile(t, rows):
        slot = ts_ref[t]
        row = lax.broadcasted_iota(jnp.int32, (MOE_TILE, PACK_WORDS), 0)
        live = row < tr_ref[t]
        halves = [jnp.where(live, xs_ref[rows, h * PACK_WORDS:(h + 1) * PACK_WORDS],
                            jnp.zeros((MOE_TILE, PACK_WORDS), jnp.uint32)) for h in range(2)]
        chunks = _unpack_rows(*halves)

        def up(w_buf):
            acc = None
            for k, chunk in enumerate(chunks):
                w = w_buf[slot, k * PACK_WORDS:(k + 1) * PACK_WORDS, :].astype(BF16)
                part = jnp.dot(chunk, w, preferred_element_type=F32)
                acc = part if acc is None else acc + part
            return acc

        hid = (_silu(up(w1_buf)) * up(w3_buf)).astype(BF16)
        y = jnp.dot(hid, w2_buf[slot].astype(BF16), preferred_element_type=F32)
        ys_ref[rows, :] = jnp.concatenate(_pack_rows(y), axis=1)

    step = pl.program_id(0)
    first_tile = step * MOE_STEP_TILES

    @pl.when(first_tile < na_ref[0])
    def _():
        @pl.when(step == 0)
        def _():
            for_experts_opened_from(first_tile, lambda copy: copy.start())

        for_experts_opened_from(first_tile + MOE_STEP_TILES, lambda copy: copy.start())
        for_experts_opened_from(first_tile, lambda copy: copy.wait())
        for j in range(MOE_STEP_TILES):
            tile(first_tile + j, slice(j * MOE_TILE, (j + 1) * MOE_TILE))


def _ffn(xs, tables, layer, w1, w3, w2):
    step_rows = MOE_STEP_TILES * MOE_TILE
    row_tile = lambda s, tr, ts, to, na: (jnp.minimum(s, (na[0] - 1) // MOE_STEP_TILES), 0)
    hbm = pl.BlockSpec(memory_space=pl.ANY)
    return pl.pallas_call(
        functools.partial(_ffn_kernel, layer=layer),
        out_shape=jax.ShapeDtypeStruct((MOE_ROWS, ROW_WORDS), jnp.uint32),
        grid_spec=pltpu.PrefetchScalarGridSpec(
            num_scalar_prefetch=len(tables),
            grid=(MOE_ROWS // step_rows,),
            in_specs=[pl.BlockSpec((step_rows, ROW_WORDS), row_tile), hbm, hbm, hbm],
            out_specs=pl.BlockSpec((step_rows, ROW_WORDS), row_tile),
            scratch_shapes=[
                pltpu.VMEM((MOE_WEIGHT_SLOTS, D_MODEL, D_EXPERT), F32),
                pltpu.VMEM((MOE_WEIGHT_SLOTS, D_MODEL, D_EXPERT), F32),
                pltpu.VMEM((MOE_WEIGHT_SLOTS, D_EXPERT, D_MODEL), F32),
                pltpu.SemaphoreType.DMA((MOE_WEIGHT_SLOTS, 3)),
            ],
        ),
        compiler_params=_params(48, "arbitrary"),
        name="moe_experts",
    )(*tables, xs, w1, w3, w2)


def _moe_res(x_ref, y_ref, rw_ref, mod_ref):
    cond = _tile_cond(pl.program_id(0))
    rw = rw_ref[...]
    wa = rw[:, 0:1]
    wb = rw[:, 1:2]
    ya = _unpack_rows(y_ref[0, :, :PACK_WORDS], y_ref[0, :, PACK_WORDS:])
    yb = _unpack_rows(y_ref[1, :, :PACK_WORDS], y_ref[1, :, PACK_WORDS:])
    y = jnp.concatenate([wa * a.astype(F32) + wb * b.astype(F32) for a, b in zip(ya, yb)], axis=1)
    return x_ref[...] + _mod_row(mod_ref, cond, 5) * y


def _moe_res_final_kernel(x_ref, y_ref, rw_ref, mod_ref, fn_ref, op_ref, os_ref):
    x = _moe_res(x_ref, y_ref, rw_ref, mod_ref)
    y = x * lax.rsqrt(jnp.mean(x * x, axis=-1, keepdims=True) + EPS) * fn_ref[...]
    is_prompt = pl.program_id(0) < PROMPT_TILES

    @pl.when(is_prompt)
    def _():
        op_ref[...] = y

    @pl.when(jnp.logical_not(is_prompt))
    def _():
        os_ref[...] = y


def _moe_combine(x1, y_pairs, route_w, mod, final_norm):
    tile = lambda i: (i, 0)
    fixed2 = lambda i: (0, 0)
    return pl.pallas_call(
        _moe_res_final_kernel,
        out_shape=(jax.ShapeDtypeStruct((N_PROMPT, D_MODEL), F32),
                   jax.ShapeDtypeStruct((N_TOK - N_PROMPT, D_MODEL), F32)),
        grid=(N_TILES,),
        in_specs=[pl.BlockSpec((TM, D_MODEL), tile),
                  pl.BlockSpec((2, TM, ROW_WORDS), lambda i: (0, i, 0)),
                  pl.BlockSpec((TM, ROUTER_LANES), tile),
                  pl.BlockSpec(mod.shape, fixed2),
                  pl.BlockSpec((1, D_MODEL), fixed2)],
        out_specs=(pl.BlockSpec((TM, D_MODEL), _prompt_tile), pl.BlockSpec((TM, D_MODEL), _sample_tile)),
        compiler_params=_params(32, "arbitrary"),
        name="moe_combine",
    )(x1, y_pairs, route_w, mod, final_norm)


def _moe_layer(x1, h2p, logits, layer, w1, w3, w2):
    route_t, route_w, counts = _route_tokens(logits)
    pos, tables = _moe_plan(route_t, counts)
    windows = pos.reshape(2, N_TOK // SC_WINDOW, SC_WINDOW)
    xs = _sc_scatter_rows(h2p, windows[0], windows[1], MOE_ROWS)
    ys = _ffn(xs, tables, layer, w1, w3, w2)
    y_pairs = _sc_gather_rows(ys, windows.reshape(2 * N_TOK // SC_WINDOW, SC_WINDOW))
    return x1, y_pairs.reshape(2, N_TOK, ROW_WORDS), route_w


def _swap_rotary_halves(x):
    n = x.shape[-1]
    lane = lax.broadcasted_iota(jnp.int32, x.shape, 1)
    quarter = ROPE_HALF // 2
    return jnp.where((lane % ROPE_HALF) < quarter, pltpu.roll(x, n - quarter, 1), pltpu.roll(x, quarter, 1))


def _at_in_kernel(x1_ref, y_ref, rw_ref, mod_prev_ref, mod_ref, nw_ref, w_ref, cos_ref, sin_ref,
                  x_ref, q_ref, kt_ref, v_ref, kc_ref, vc_ref):
    i = pl.program_id(0)
    cond = _tile_cond(i)
    x = _moe_res(x1_ref, y_ref, rw_ref, mod_prev_ref)
    x_ref[...] = x
    h = _norm_mod(x, nw_ref[...], _mod_row(mod_ref, cond, 0), _mod_row(mod_ref, cond, 1)).astype(BF16)
    nq = ATT_Q_HEADS * ATT_HEAD_DIM
    nk = ATT_KV_HEADS * ATT_HEAD_DIM
    cos = cos_ref[...]
    sin = sin_ref[...]

    def rope(x):
        reps = x.shape[-1] // LANES
        return x * jnp.concatenate([cos] * reps, axis=1) + _swap_rotary_halves(x) * jnp.concatenate([sin] * reps, axis=1)

    q_ref[...] = (rope(jnp.dot(h, w_ref[:, :nq], preferred_element_type=F32)) * ATT_HEAD_DIM ** -0.5).astype(BF16)
    k = rope(jnp.dot(h, w_ref[:, nq:nq + nk], preferred_element_type=F32))
    v = jnp.dot(h, w_ref[:, nq + nk:], preferred_element_type=F32)
    kt = k.T
    kt_ref[...] = kt.astype(BF16)
    v_ref[...] = v.astype(BF16)

    @pl.when(i < PROMPT_TILES)
    def _():
        kc_ref[...] = kt
        vc_ref[...] = v.T


def _rope_tables():
    f32 = np.float32
    pos = np.arange(DEC_SEQ)
    t_row = (pos // GRID_W).astype(f32)
    t_col = (pos % GRID_W).astype(f32)
    inv = f32(ROPE_BASE) ** (-np.arange(0, ROPE_HALF, 2, dtype=f32) / f32(ROPE_HALF))
    j = np.arange(LANES) % ATT_HEAD_DIM
    freq = inv[(j % ROPE_HALF) % (ROPE_HALF // 2)]
    ang = (np.where((j < ROPE_HALF)[None, :], t_row[:, None], t_col[:, None]) * freq[None, :]).astype(f32)
    sign = np.where((j % ROPE_HALF) < ROPE_HALF // 2, -1.0, 1.0).astype(f32)
    cos = np.concatenate([np.ones((TM, LANES), f32), np.cos(ang)], axis=0)
    sin = np.concatenate([np.zeros((TM, LANES), f32), np.sin(ang) * sign[None, :]], axis=0)
    return jnp.asarray(cos, F32), jnp.asarray(sin, F32)


def _at_in(x1, y_pairs, route_w, mod_prev, mod, nw, w_in, cos, sin):
    tile = lambda i: (i, 0)
    fixed2 = lambda i: (0, 0)
    rope_tile = lambda i: (jnp.where(i < PROMPT_TILES, 0, 1 + (i - PROMPT_TILES) % TILES_PER_DEC_SEQ), 0)
    nk = ATT_KV_HEADS * ATT_HEAD_DIM
    return pl.pallas_call(
        _at_in_kernel,
        out_shape=(
            jax.ShapeDtypeStruct((N_TOK, D_MODEL), F32),
            jax.ShapeDtypeStruct((N_TOK, D_MODEL), BF16),
            jax.ShapeDtypeStruct((nk, N_TOK), BF16),
            jax.ShapeDtypeStruct((N_TOK, nk), BF16),
            jax.ShapeDtypeStruct((N_PROMPT, nk), F32),
            jax.ShapeDtypeStruct((N_PROMPT, nk), F32),
        ),
        grid=(N_TILES,),
        in_specs=[
            pl.BlockSpec((TM, D_MODEL), tile),
            pl.BlockSpec((2, TM, ROW_WORDS), lambda i: (0, i, 0)),
            pl.BlockSpec((TM, ROUTER_LANES), tile),
            pl.BlockSpec(mod_prev.shape, fixed2),
            pl.BlockSpec(mod.shape, fixed2),
            pl.BlockSpec((1, D_MODEL), fixed2),
            pl.BlockSpec(w_in.shape, fixed2),
            pl.BlockSpec((TM, LANES), rope_tile),
            pl.BlockSpec((TM, LANES), rope_tile),
        ],
        out_specs=(pl.BlockSpec((TM, D_MODEL), tile), pl.BlockSpec((TM, D_MODEL), tile),
                   pl.BlockSpec((nk, TM), lambda i: (0, i)),
                   pl.BlockSpec((TM, nk), tile),
                   pl.BlockSpec((TM, nk), _prompt_tile), pl.BlockSpec((TM, nk), _prompt_tile)),
        compiler_params=_params(40, "arbitrary"),
        name="attn_in_proj",
    )(x1, y_pairs, route_w, mod_prev, mod, nw, w_in, cos, sin)


def _attend(q, kt_all, v_all, mask, sink_ref, o_ref):
    nq = q.shape[0]
    group_lanes = ATT_GROUP * ATT_HEAD_DIM
    lane = lax.broadcasted_iota(jnp.int32, (nq, group_lanes), 1)
    mine = [(lane // ATT_HEAD_DIM) == g for g in range(ATT_GROUP)]
    row_head = lax.broadcasted_iota(jnp.int32, (ATT_GROUP * nq, 1), 0) // nq
    if mask is not None:
        mask = jnp.concatenate([mask] * ATT_GROUP, axis=0)
    for hk in range(ATT_KV_HEADS):
        vh = v_all[:, hk * ATT_HEAD_DIM:(hk + 1) * ATT_HEAD_DIM]
        kt = jnp.concatenate([kt_all[hk * ATT_HEAD_DIM:(hk + 1) * ATT_HEAD_DIM, :]] * ATT_GROUP, axis=0)
        vt = jnp.concatenate([vh] * ATT_GROUP, axis=1)
        qg = q[:, hk * group_lanes:(hk + 1) * group_lanes]
        q_stack = jnp.concatenate([jnp.where(mine[g], qg, jnp.zeros_like(qg)) for g in range(ATT_GROUP)], axis=0)
        s = jnp.dot(q_stack, kt, preferred_element_type=F32)
        if mask is not None:
            s = jnp.where(mask, s, -jnp.inf)
        sink = jnp.zeros((ATT_GROUP * nq, 1), F32)
        for g in range(ATT_GROUP):
            sink = jnp.where(row_head == g, sink_ref[hk * ATT_GROUP + g], sink)
        m = jnp.maximum(jnp.max(s, axis=-1, keepdims=True), sink)
        p = jnp.exp(s - m)
        denom = jnp.sum(p, axis=-1, keepdims=True) + jnp.exp(sink - m)
        o = jnp.dot(p.astype(BF16), vt, preferred_element_type=F32) / denom
        acc = jnp.where(mine[0], o[:nq], 0.0)
        for g in range(1, ATT_GROUP):
            acc = acc + jnp.where(mine[g], o[g * nq:(g + 1) * nq], 0.0)
        o_ref[:, hk * group_lanes:(hk + 1) * group_lanes] = acc.astype(BF16)


def _ctx_attn_kernel(sink_ref, q_ref, k_ref, v_ref, o_ref):
    _attend(q_ref[...], k_ref[...], v_ref[...], None, sink_ref, o_ref)


def _lat_attn_kernel(sink_ref, q_ref, kp_ref, kc_ref, kn_ref, vp_ref, vc_ref, vn_ref, ck_ref, cv_ref, o_ref):
    jb = pl.program_id(1)
    kt_all = jnp.concatenate([kp_ref[...], kc_ref[...], kn_ref[...], ck_ref[0].astype(BF16)], axis=1)
    v_all = jnp.concatenate([vp_ref[...], vc_ref[...], vn_ref[...], cv_ref[0].astype(BF16)], axis=0)
    nkeys = 3 * BLOCK + PAST_LEN
    qi = lax.broadcasted_iota(jnp.int32, (BLOCK, nkeys), 0)
    kj = lax.broadcasted_iota(jnp.int32, (BLOCK, nkeys), 1)
    qpos = jb * BLOCK + qi
    kpos = (jb - 1) * BLOCK + kj
    local_ok = (jnp.abs(qpos - kpos) <= WINDOW) & (kpos >= 0) & (kpos < DEC_SEQ)
    mask = (kj >= 3 * BLOCK) | local_ok
    _attend(q_ref[...], kt_all, v_all, mask, sink_ref, o_ref)


def _attention(q, kt, v, cache_kt, cache_v, sink):
    nk = ATT_KV_HEADS * ATT_HEAD_DIM
    smem = pl.BlockSpec(memory_space=pltpu.SMEM)
    ctx = pl.pallas_call(
        _ctx_attn_kernel,
        out_shape=jax.ShapeDtypeStruct((N_PROMPT, D_MODEL), BF16),
        grid=(BATCH,),
        in_specs=[
            smem,
            pl.BlockSpec((SEQ, D_MODEL), lambda b: (b, 0)),
            pl.BlockSpec((nk, SEQ), lambda b: (0, b)),
            pl.BlockSpec((SEQ, nk), lambda b: (b, 0)),
        ],
        out_specs=pl.BlockSpec((SEQ, D_MODEL), lambda b: (b, 0)),
        compiler_params=_params(40, "arbitrary"),
        name="context_attention",
    )(sink, q, kt, v)

    nb = DEC_SEQ // BLOCK
    base = N_PROMPT // BLOCK
    cur = lambda b, j: (base + b * nb + j, 0)
    prev = lambda b, j: (base + b * nb + jnp.maximum(j - 1, 0), 0)
    nxt = lambda b, j: (base + b * nb + jnp.minimum(j + 1, nb - 1), 0)
    kv_blk = lambda f: pl.BlockSpec((BLOCK, nk), f)
    kt_blk = lambda f: pl.BlockSpec((nk, BLOCK), lambda b, j: f(b, j)[::-1])
    cache_blk = pl.BlockSpec((1, PAST_LEN, nk), lambda b, j: (b, 0, 0))
    cache_kt_blk = pl.BlockSpec((1, nk, PAST_LEN), lambda b, j: (b, 0, 0))
    lat = pl.pallas_call(
        _lat_attn_kernel,
        out_shape=jax.ShapeDtypeStruct((DEC_BATCH * DEC_SEQ, D_MODEL), BF16),
        grid=(DEC_BATCH, nb),
        in_specs=[
            smem,
            pl.BlockSpec((BLOCK, D_MODEL), cur),
            kt_blk(prev), kt_blk(cur), kt_blk(nxt),
            kv_blk(prev), kv_blk(cur), kv_blk(nxt),
            cache_kt_blk, cache_blk,
        ],
        out_specs=pl.BlockSpec((BLOCK, D_MODEL), lambda b, j: (b * nb + j, 0)),
        compiler_params=_params(40, "arbitrary", "arbitrary"),
        name="latent_attention",
    )(sink, q, kt, kt, kt, v, v, v, cache_kt, cache_v)
    return ctx, lat


def kernel(x_prompt, x_sample, state_hgrn, cache_k, cache_v, c, c_ctx, ada_w, ada_b, norm_w, hg_w_in,
           hg_lb_logits, hg_onorm, hg_w_out, at_w_in, at_sink, at_w_out, moe_w_group, moe_b_group,
           moe_w_expert, moe_b_expert, moe_w1, moe_w3, moe_w2, final_norm):
    xp = x_prompt.reshape(N_PROMPT, D_MODEL)
    xs = x_sample.reshape(N_TOK - N_PROMPT, D_MODEL)
    cond = jnp.concatenate([c_ctx[None, :], c], axis=0)
    mod = _ada(cond, ada_w, ada_b)
    nk = ATT_KV_HEADS * ATT_HEAD_DIM

    def router_params(i):
        pad = jnp.zeros((D_MODEL, ROUTER_LANES - N_EXPERTS - N_GROUPS), F32)
        w = jnp.concatenate([moe_w_expert[i], moe_w_group[i], pad], axis=1)
        b = jnp.concatenate([moe_b_expert[i], moe_b_group[i], pad[0]])[None, :]
        hi = w.astype(BF16)
        lo = (w - hi.astype(F32)).astype(BF16)
        return jnp.stack([hi, lo]), b

    mix_blk = pl.BlockSpec((MIX_TM, D_MODEL), lambda i: (i, 0))
    mix_prompt_blk = pl.BlockSpec((MIX_TM, D_MODEL), lambda i: (jnp.minimum(i, N_PROMPT // MIX_TM - 1), 0))
    mix_sample_blk = pl.BlockSpec((MIX_TM, D_MODEL), lambda i: (jnp.maximum(i - N_PROMPT // MIX_TM, 0), 0))
    q, v, g, lff, kf, lfb, kb = _hg_in(xp, xs, mod[0], norm_w[0, 0][None, :], hg_w_in[0].astype(BF16), hg_lb_logits)
    o_f, o_b, state_new = _gla(q, v, lff, kf, lfb, kb, state_hgrn)
    wr, br = router_params(0)
    routed = _mixer_out(
        _hg_out_kernel, "hgrn_out_route", (o_f, o_b, g, hg_onorm[0][None, :], xp, xs),
        (mix_blk, mix_blk, mix_blk, pl.BlockSpec((1, HG_DK), lambda i: (0, 0)), mix_prompt_blk, mix_sample_blk),
        mod[0], norm_w[0, 1][None, :], hg_w_out[0].astype(BF16), wr, br)
    moe_out = _moe_layer(*routed, 0, moe_w1, moe_w3, moe_w2)

    cos, sin = _rope_tables()
    x, qa, ka, va, k_ctx, v_ctx = _at_in(*moe_out, mod[0], mod[1], norm_w[1, 0][None, :],
                                         at_w_in[0].astype(BF16), cos, sin)
    attn_ctx, attn_lat = _attention(qa, ka, va, cache_k[:, 0].reshape(DEC_BATCH, PAST_LEN, nk).transpose(0, 2, 1),
                                    cache_v[:, 0].reshape(DEC_BATCH, PAST_LEN, nk), at_sink[0])
    wr, br = router_params(1)
    routed = _mixer_out(
        _at_out_kernel, "attn_out_route", (attn_ctx, attn_lat, x), (mix_prompt_blk, mix_sample_blk, mix_blk),
        mod[1], norm_w[1, 1][None, :], at_w_out[0].astype(BF16), wr, br)
    y_prompt, y_sample = _moe_combine(*_moe_layer(*routed, 1, moe_w1, moe_w3, moe_w2), mod[1], final_norm[None, :])

    def cache(feature_major):
        return feature_major.reshape(BATCH, 1, ATT_KV_HEADS, ATT_HEAD_DIM, SEQ).transpose(0, 1, 4, 2, 3)

    return (y_prompt.reshape(BATCH, SEQ, D_MODEL), y_sample.reshape(DEC_BATCH, DEC_SEQ, D_MODEL), state_new,
            cache(k_ctx), cache(v_ctx))
```

```python
import functools
from typing import Any, NamedTuple

import jax
import jax.numpy as jnp
import numpy as np
from jax import lax
from jax.experimental import pallas as pl
from jax.experimental.pallas import tpu as pltpu
from jax.experimental.pallas import tpu_sc as plsc

F32 = jnp.float32
BF16 = jnp.bfloat16

D_MODEL = 1024
BATCH = 16
SEQ = 256
DEC_BATCH = 2
DEC_SEQ = 1024
PAST_LEN = 512
GRID_W = 64
HG_HEADS = 8
HG_DK = 128
CHUNK = 16
ATT_HEAD_DIM = 64
ATT_Q_HEADS = 16
ATT_KV_HEADS = 4
ATT_GROUP = 4
WINDOW = 128
BLOCK = 128
ROPE_HALF = 32
ROPE_BASE = 10000.0
N_GROUPS = 4
EXPERTS_PER_GROUP = 8
N_EXPERTS = 32
D_EXPERT = 256
EPS = 1e-6

N_PROMPT = BATCH * SEQ
N_TOK = N_PROMPT + DEC_BATCH * DEC_SEQ
TM = 256
N_TILES = N_TOK // TM
PROMPT_TILES = N_PROMPT // TM
TILES_PER_DEC_SEQ = DEC_SEQ // TM
LANES = 128
SUBLANES = 8
ADA_LANE_BLOCKS = 4
N_COND_USED = 1 + DEC_BATCH
N_COND = 8
ROUTER_LANES = 128
ROUTE_TM = 1024
ROUTE_ROWS = 8
MIX_TM = 512
HG_IN_TM = 512
HG_IN_SUB = 256
HG_IN_TN = 256
DECAY_CLAMP = 60.0
PACK_WORDS = D_MODEL // 4
ROW_WORDS = 2 * PACK_WORDS
MOE_TILE = 256
MOE_STEP_TILES = 4
MOE_WEIGHT_SLOTS = 2 * MOE_STEP_TILES
MOE_ROWS = 2 * N_TOK + N_EXPERTS * MOE_TILE
MIB = 1024 * 1024


def _params(vmem_mib, *semantics):
    return pltpu.CompilerParams(dimension_semantics=semantics, vmem_limit_bytes=vmem_mib * MIB)


def _tile_cond(i):
    return jnp.where(i < PROMPT_TILES, 0, 1 + (i - PROMPT_TILES) // TILES_PER_DEC_SEQ)


def _prompt_tile(i):
    return (jnp.minimum(i, PROMPT_TILES - 1), 0)


def _sample_tile(i):
    return (jnp.maximum(i - PROMPT_TILES, 0), 0)


def _mod_row(mod_ref, cond, which):
    return mod_ref[pl.ds(cond, 1), which * D_MODEL:(which + 1) * D_MODEL]


def _norm_mod(x, nw, shift, scale):
    y = x * lax.rsqrt(jnp.mean(x * x, axis=-1, keepdims=True) + EPS)
    return (y * nw) * (1.0 + scale) + shift


def _silu(x):
    return x * jax.nn.sigmoid(x)


def _ada_kernel(c_ref, w_ref, b_ref, o_ref, s_scr):
    @pl.when((pl.program_id(0) == 0) & (pl.program_id(1) == 0))
    def _():
        s_scr[...] = _silu(c_ref[...])

    tn = w_ref.shape[-1]
    cols = [[] for _ in range(N_COND_USED)]
    for j0 in range(0, tn // LANES, ADA_LANE_BLOCKS):
        acc = [[jnp.zeros((SUBLANES, LANES), F32) for _ in range(ADA_LANE_BLOCKS)] for _ in range(N_COND_USED)]
        for g in range(D_MODEL // SUBLANES):
            rows = slice(g * SUBLANES, (g + 1) * SUBLANES)
            w = [w_ref[0, rows, (j0 + j) * LANES:(j0 + j + 1) * LANES] for j in range(ADA_LANE_BLOCKS)]
            for r in range(N_COND_USED):
                s = s_scr[r, rows, :]
                for j in range(ADA_LANE_BLOCKS):
                    acc[r][j] = acc[r][j] + w[j] * s
        for r in range(N_COND_USED):
            cols[r] += [jnp.sum(a, axis=0, keepdims=True) for a in acc[r]]
    out = [jnp.concatenate(cols[r], axis=1) + b_ref[0] for r in range(N_COND_USED)]
    out.append(jnp.zeros((N_COND - N_COND_USED, tn), F32))
    o_ref[0] = jnp.concatenate(out, axis=0)


def _ada(cond, ada_w, ada_b):
    depth, _, n = ada_w.shape
    tn = 1536
    cond_cols = jnp.broadcast_to(cond[:, :, None], (N_COND_USED, D_MODEL, LANES))
    return pl.pallas_call(
        _ada_kernel,
        out_shape=jax.ShapeDtypeStruct((depth, N_COND, n), F32),
        grid=(depth, n // tn),
        in_specs=[
            pl.BlockSpec((N_COND_USED, D_MODEL, LANES), lambda l, j: (0, 0, 0)),
            pl.BlockSpec((1, D_MODEL, tn), lambda l, j: (l, 0, j)),
            pl.BlockSpec((1, 1, tn), lambda l, j: (l, 0, j)),
        ],
        out_specs=pl.BlockSpec((1, N_COND, tn), lambda l, j: (l, 0, j)),
        scratch_shapes=[pltpu.VMEM((N_COND_USED, D_MODEL, LANES), F32)],
        compiler_params=_params(40, "arbitrary", "arbitrary"),
        name="ada_modulation",
    )(cond_cols, ada_w, ada_b.reshape(depth, 1, n))


def _hg_in_kernel(xp_ref, xs_ref, mod_ref, nw_ref, w_ref, lbl_ref,
                  q_ref, v_ref, g_ref, lff_ref, kf_ref, lfb_ref, kb_ref):
    i = pl.program_id(0) * (HG_IN_TM // TM)
    cond = _tile_cond(i)
    shift, scale = _mod_row(mod_ref, cond, 0), _mod_row(mod_ref, cond, 1)

    def normed(rows):
        x = jnp.where(i < PROMPT_TILES, xp_ref[rows, :], xs_ref[rows, :])
        return _norm_mod(x, nw_ref[...], shift, scale).astype(BF16)

    def proj(h, c, cols):
        return jnp.dot(h, w_ref[:, c * D_MODEL + cols.start:c * D_MODEL + cols.stop], preferred_element_type=F32)

    l0, l1, l2 = lbl_ref[0], lbl_ref[1], lbl_ref[2]
    m = jnp.maximum(jnp.maximum(l0, l1), l2)
    e0, e1, e2 = jnp.exp(l0 - m), jnp.exp(l1 - m), jnp.exp(l2 - m)
    lb = e0 / (e0 + e1 + e2)
    col_tiles = [slice(n * HG_IN_TN, (n + 1) * HG_IN_TN) for n in range(D_MODEL // HG_IN_TN)]

    def gate_and_plain(h, rows, d):
        lf_ref, k_ref, plain_ref = ((lff_ref, kf_ref, q_ref), (lfb_ref, kb_ref, v_ref))[d]
        for cols in col_tiles:
            lbd = lb[d:d + 1, cols]
            z_gate = proj(h, 2 + d, cols)
            z_plain = proj(h, d, cols)
            f = lbd + (1.0 - lbd) * jax.nn.sigmoid(z_gate)
            lf = jnp.log(f)
            hi = lf.astype(BF16)
            lf_ref[0, rows, cols] = hi
            lf_ref[1, rows, cols] = (lf - hi.astype(F32)).astype(BF16)
            k_ref[rows, cols] = (1.0 - f).astype(BF16)
            plain_ref[rows, cols] = z_plain.astype(BF16)

    sub_tiles = [slice(s * HG_IN_SUB, (s + 1) * HG_IN_SUB) for s in range(HG_IN_TM // HG_IN_SUB)]
    h = normed(sub_tiles[0])
    for s, rows in enumerate(sub_tiles):
        gate_and_plain(h, rows, 0)
        h_next = normed(sub_tiles[s + 1]) if s + 1 < len(sub_tiles) else None
        gate_and_plain(h, rows, 1)
        for cols in col_tiles:
            g_ref[rows, cols] = proj(h, 4, cols).astype(BF16)
        h = h_next


def _hg_in(xp, xs, mod, nw, w_in, lb_logits):
    tile = lambda i: (i, 0)
    fixed2 = lambda i: (0, 0)
    bf = jax.ShapeDtypeStruct((N_TOK, D_MODEL), BF16)
    ff = jax.ShapeDtypeStruct((2, N_TOK, D_MODEL), BF16)
    blk = pl.BlockSpec((HG_IN_TM, D_MODEL), tile)
    split_blk = pl.BlockSpec((2, HG_IN_TM, D_MODEL), lambda i: (0, i, 0))
    prompt_steps = N_PROMPT // HG_IN_TM
    return pl.pallas_call(
        _hg_in_kernel,
        out_shape=(bf, bf, bf, ff, bf, ff, bf),
        grid=(N_TOK // HG_IN_TM,),
        in_specs=[
            pl.BlockSpec((HG_IN_TM, D_MODEL), lambda i: (jnp.minimum(i, prompt_steps - 1), 0)),
            pl.BlockSpec((HG_IN_TM, D_MODEL), lambda i: (jnp.maximum(i - prompt_steps, 0), 0)),
            pl.BlockSpec(mod.shape, fixed2),
            pl.BlockSpec((1, D_MODEL), fixed2),
            pl.BlockSpec(w_in.shape, fixed2, pipeline_mode=pl.Buffered(1)),
            pl.BlockSpec(lb_logits.shape, lambda i: (0, 0, 0)),
        ],
        out_specs=(blk, blk, blk, split_blk, blk, split_blk, blk),
        compiler_params=_params(56, "arbitrary"),
        name="hgrn_in_proj",
    )(xp, xs, mod, nw, w_in, lb_logits)


def _gla_exact(reverse, q_ref, v_ref, lf_ref, k_ref, st_ref, o_ref):
    nt = (((1,), (1,)), ((), ()))
    tn = (((0,), (0,)), ((), ()))
    n_chunks = TM // CHUNK
    row_id = lax.broadcasted_iota(jnp.int32, (CHUNK, HG_DK), 0)
    order = range(CHUNK - 1, -1, -1) if reverse else range(CHUNK)

    def chunk_head(it, carry):
        step, h = it // HG_HEADS, it % HG_HEADS
        ci = (n_chunks - 1 - step) if reverse else step
        rows = pl.ds(pl.multiple_of(ci * CHUNK, CHUNK), CHUNK)
        cols = pl.ds(pl.multiple_of(h * HG_DK, HG_DK), HG_DK)
        q, k, v = q_ref[rows, cols], k_ref[rows, cols], v_ref[rows, cols]
        f = jnp.exp(lf_ref[0, rows, cols].astype(F32) + lf_ref[1, rows, cols].astype(F32))
        st = st_ref[0, h]
        o = jnp.zeros((CHUNK, HG_DK), F32)
        for t in order:
            one = row_id == t
            v_t = jnp.where(one, v, jnp.zeros_like(v))
            st = st * f[t:t + 1, :] + lax.dot_general(v_t, k, tn, preferred_element_type=F32)
            o = jnp.where(one, lax.dot_general(q, st.astype(BF16), nt, preferred_element_type=F32), o)
        st_ref[0, h] = st
        o_ref[rows, cols] = o.astype(o_ref.dtype)
        return carry

    lax.fori_loop(0, n_chunks * HG_HEADS, chunk_head, 0)


class _GlaDirection(NamedTuple):
    reverse: bool
    q: Any
    v: Any
    lf: Any
    k: Any
    o: Any
    st: Any
    sw: Any
    ut: Any
    qd: Any
    kd: Any
    ki: Any
    cd: Any
    b: Any


_NT = (((1,), (1,)), ((), ()))
_TN = (((0,), (0,)), ((), ()))
_HG_COLS = [slice(h * HG_DK, (h + 1) * HG_DK) for h in range(HG_HEADS)]
_N_CHUNKS = TM // CHUNK
_N_PAIRS = _N_CHUNKS // 2


def _gla_prepare(d):
    r = lax.broadcasted_iota(jnp.int32, (TM, TM), 0)
    c = lax.broadcasted_iota(jnp.int32, (TM, TM), 1)
    same = (r // CHUNK) == (c // CHUNK)
    tri = (same & ((c >= r) if d.reverse else (c <= r))).astype(BF16)
    b = jnp.dot(tri, d.lf[0], preferred_element_type=F32) + jnp.dot(tri, d.lf[1], preferred_element_type=F32)
    d.b[...] = b
    edge = 0 if d.reverse else CHUNK - 1
    total = b.reshape(_N_CHUNKS, CHUNK, D_MODEL)[:, edge, :]
    d.cd[...] = jnp.exp(total)
    return jnp.max(-total) <= DECAY_CLAMP


def _gla_pair_rows(d, step):
    pi = (_N_PAIRS - 1 - step) if d.reverse else step
    row0 = pi * 2 * CHUNK
    lo, hi = (pl.ds(row0, CHUNK), 2 * pi), (pl.ds(row0 + CHUNK, CHUNK), 2 * pi + 1)
    return pl.ds(row0, 2 * CHUNK), ((hi, lo) if d.reverse else (lo, hi))


def _chunk_decay(d, chunk):
    return d.cd[pl.ds(chunk, 1), :]


def _gla_decayed_operands(d, step):
    both, ((_, c_first), (_, c_second)) = _gla_pair_rows(d, step)
    b = d.b[both, :]
    ki = d.k[both, :] * jnp.exp(jnp.minimum(-b, DECAY_CLAMP)).astype(BF16)
    d.qd[both, :] = d.q[both, :] * jnp.exp(b).astype(BF16)
    d.ki[both, :] = ki
    c_lo, c_hi = (c_second, c_first) if d.reverse else (c_first, c_second)
    ends = jnp.concatenate([jnp.broadcast_to(_chunk_decay(d, c).astype(BF16), (CHUNK, D_MODEL)) for c in (c_lo, c_hi)],
                           axis=0)
    d.kd[both, :] = ki * ends


def _rows_scaled(x, scale_row, second_half):
    scale = jnp.broadcast_to(scale_row.astype(BF16), (CHUNK, x.shape[1]))
    ones = jnp.ones((CHUNK, x.shape[1]), BF16)
    return x * jnp.concatenate([ones, scale] if second_half else [scale, ones], axis=0)


def _gla_key_value_product(d, step):
    _gla_decayed_operands(d, step)
    both, (_, (_, c_second)) = _gla_pair_rows(d, step)
    decay_second = _chunk_decay(d, c_second)
    for h, cols in enumerate(_HG_COLS):
        keys = _rows_scaled(d.kd[both, cols], decay_second[:, cols], second_half=d.reverse)
        d.ut[h] = lax.dot_general(d.v[both, cols], keys, _TN, preferred_element_type=F32)


def _gla_start(d):
    for h in range(HG_HEADS):
        d.sw[0, h] = d.st[0, h].T.astype(BF16)
    _gla_key_value_product(d, 0)


def _gla_pair(d, step, src, dst):
    sr = lax.broadcasted_iota(jnp.int32, (CHUNK, CHUNK), 0)
    sc = lax.broadcasted_iota(jnp.int32, (CHUNK, CHUNK), 1)
    tr = lax.broadcasted_iota(jnp.int32, (CHUNK, 2 * CHUNK), 0)
    tc = lax.broadcasted_iota(jnp.int32, (CHUNK, 2 * CHUNK), 1)
    if d.reverse:
        keep_first = sc >= sr
        keep_second = (tc >= tr) | (tc >= CHUNK)
    else:
        keep_first = sc <= sr
        keep_second = (tc < CHUNK) | (tc - CHUNK <= tr)
    both, ((first, c_first), (second, c_second)) = _gla_pair_rows(d, step)
    decay_first = _chunk_decay(d, c_first)
    decay_both = decay_first * _chunk_decay(d, c_second)
    for h, cols in enumerate(_HG_COLS):
        new = d.st[src, h] * decay_both[:, cols] + d.ut[h]
        d.st[dst, h] = new
        d.sw[dst, h] = new.T.astype(BF16)
    a_first = [lax.dot_general(d.qd[first, cols], d.ki[first, cols], _NT, preferred_element_type=F32)
               for cols in _HG_COLS]
    a_second = []
    for cols in _HG_COLS:
        lo_keys = (d.ki if d.reverse else d.kd)[pl.ds(both.start, CHUNK), cols]
        hi_keys = (d.kd if d.reverse else d.ki)[pl.ds(both.start + CHUNK, CHUNK), cols]
        keys = jnp.concatenate([lo_keys, hi_keys], axis=0)
        a_second.append(lax.dot_general(d.qd[second, cols], keys, _NT, preferred_element_type=F32))
    inter = [jnp.dot(_rows_scaled(d.qd[both, cols], decay_first[:, cols], second_half=not d.reverse),
                     d.sw[src, h], preferred_element_type=F32) for h, cols in enumerate(_HG_COLS)]
    if step + 1 < _N_PAIRS:
        _gla_key_value_product(d, step + 1)
    first_half, second_half = (slice(CHUNK, None), slice(None, CHUNK)) if d.reverse else \
                              (slice(None, CHUNK), slice(CHUNK, None))
    for h, cols in enumerate(_HG_COLS):
        am = jnp.where(keep_first, a_first[h], 0.0).astype(BF16)
        d.o[first, cols] = (jnp.dot(am, d.v[first, cols], preferred_element_type=F32)
                            + inter[h][first_half]).astype(d.o.dtype)
        am = jnp.where(keep_second, a_second[h], 0.0).astype(BF16)
        d.o[second, cols] = (jnp.dot(am, d.v[both, cols], preferred_element_type=F32)
                             + inter[h][second_half]).astype(d.o.dtype)


def _gla_kernel(qf_ref, vf_ref, lff_ref, kf_ref, qb_ref, vb_ref, lfb_ref, kb_ref, s0_ref,
                of_ref, ob_ref, sout_ref, st_scr, sw_scr, ut_scr, qd_scr, kd_scr, ki_scr, cd_scr, b_scr):
    i = pl.program_id(0)
    is_prompt = i < PROMPT_TILES
    first = jnp.logical_or(is_prompt, (i - PROMPT_TILES) % TILES_PER_DEC_SEQ == 0)
    work = [tuple(scr.at[n] for scr in (st_scr, sw_scr, ut_scr, qd_scr, kd_scr, ki_scr, cd_scr, b_scr))
            for n in range(2)]
    fwd = _GlaDirection(False, qf_ref, vf_ref, lff_ref, kf_ref, of_ref, *work[0])
    bwd = _GlaDirection(True, qb_ref, vb_ref, lfb_ref, kb_ref, ob_ref, *work[1])

    @pl.when(is_prompt)
    def _():
        for d in (fwd, bwd):
            d.st[0] = jnp.zeros((HG_HEADS, HG_DK, HG_DK), F32)

    @pl.when(jnp.logical_and(first, jnp.logical_not(is_prompt)))
    def _():
        for n, d in enumerate((fwd, bwd)):
            for h in range(HG_HEADS):
                d.st[0, h] = s0_ref[0, 0, n, h].T

    exact_scores = jnp.logical_and(_gla_prepare(fwd), _gla_prepare(bwd))

    @pl.when(exact_scores)
    def _():
        _gla_start(fwd)
        _gla_start(bwd)

        for step in range(_N_PAIRS):
            for d in (fwd, bwd):
                _gla_pair(d, step, step % 2, 1 - step % 2)

    @pl.when(jnp.logical_not(exact_scores))
    def _():
        for d in (fwd, bwd):
            _gla_exact(d.reverse, d.q, d.v, d.lf, d.k, d.st, d.o)

    @pl.when(is_prompt)
    def _():
        for n, d in enumerate((fwd, bwd)):
            for h in range(HG_HEADS):
                sout_ref[0, 0, n, h] = d.st[0, h].T


def _gla(q, v, lff, kf, lfb, kb, state_hgrn):
    def fwd_tile(i):
        return (i, 0)

    def bwd_tile(i):
        j = (i - PROMPT_TILES) % TILES_PER_DEC_SEQ
        return (jnp.where(i < PROMPT_TILES, i, i - j + (TILES_PER_DEC_SEQ - 1 - j)), 0)

    def s0_idx(i):
        return (jnp.maximum(i - PROMPT_TILES, 0) // TILES_PER_DEC_SEQ, 0, 0, 0, 0, 0)

    def sout_idx(i):
        return (jnp.minimum(i, PROMPT_TILES - 1), 0, 0, 0, 0, 0)

    f_blk = pl.BlockSpec((TM, D_MODEL), fwd_tile)
    b_blk = pl.BlockSpec((TM, D_MODEL), bwd_tile)
    f_split = pl.BlockSpec((2, TM, D_MODEL), lambda i: (0,) + fwd_tile(i))
    b_split = pl.BlockSpec((2, TM, D_MODEL), lambda i: (0,) + bwd_tile(i))
    st_blk = (1, 1, 2, HG_HEADS, HG_DK, HG_DK)
    return pl.pallas_call(
        _gla_kernel,
        out_shape=(
            jax.ShapeDtypeStruct((N_TOK, D_MODEL), BF16),
            jax.ShapeDtypeStruct((N_TOK, D_MODEL), BF16),
            jax.ShapeDtypeStruct((BATCH,) + st_blk[1:], F32),
        ),
        grid=(N_TILES,),
        in_specs=[f_blk, f_blk, f_split, f_blk, b_blk, b_blk, b_split, b_blk, pl.BlockSpec(st_blk, s0_idx)],
        out_specs=(f_blk, b_blk, pl.BlockSpec(st_blk, sout_idx)),
        scratch_shapes=[
            pltpu.VMEM((2, 2, HG_HEADS, HG_DK, HG_DK), F32),
            pltpu.VMEM((2, 2, HG_HEADS, HG_DK, HG_DK), BF16),
            pltpu.VMEM((2, HG_HEADS, HG_DK, HG_DK), F32),
            pltpu.VMEM((2, TM, D_MODEL), BF16),
            pltpu.VMEM((2, TM, D_MODEL), BF16),
            pltpu.VMEM((2, TM, D_MODEL), BF16),
            pltpu.VMEM((2, TM // CHUNK, D_MODEL), F32),
            pltpu.VMEM((2, TM, D_MODEL), F32),
        ],
        compiler_params=_params(48, "arbitrary"),
        name="hgrn_recurrence",
    )(q, v, lff, kf, q, v, lfb, kb, state_hgrn)


def _route(logits):
    lane = lax.broadcasted_iota(jnp.int32, logits.shape, 1)
    neg = jnp.float32(-jnp.inf)
    big = jnp.int32(ROUTER_LANES)

    def first_max(x):
        m = jnp.max(x, axis=-1, keepdims=True)
        return m, jnp.min(jnp.where(x == m, lane, big), axis=-1, keepdims=True)

    is_group = (lane >= N_EXPERTS) & (lane < N_EXPERTS + N_GROUPS)
    gl = jnp.where(is_group, logits, neg)
    gmax, g_lane = first_max(gl)
    g_sel = g_lane - N_EXPERTS
    gsum = jnp.sum(jnp.exp(gl - gmax), axis=-1, keepdims=True)
    p_g = 1.0 / gsum
    in_sel = (lane < N_EXPERTS) & ((lane // EXPERTS_PER_GROUP) == g_sel)
    m1, i1 = first_max(jnp.where(in_sel, logits, neg))
    m2, i2 = first_max(jnp.where(in_sel & (lane != i1), logits, neg))
    e2 = jnp.exp(m2 - m1)
    return i1, i2, p_g / (1.0 + e2), p_g * e2 / (1.0 + e2)


def _pack_rows(x):
    q = PACK_WORDS
    bits = pltpu.bitcast(x.astype(BF16).astype(F32), jnp.uint32)
    return [(bits[:, (2 + h) * q:(3 + h) * q] & jnp.uint32(0xFFFF0000)) | (bits[:, h * q:(h + 1) * q] >> 16)
            for h in range(2)]


def _unpack_rows(half0, half1):
    lo = lambda w: pltpu.bitcast(w << 16, F32).astype(BF16)
    hi = lambda w: pltpu.bitcast(w & jnp.uint32(0xFFFF0000), F32).astype(BF16)
    return [lo(half0), lo(half1), hi(half0), hi(half1)]


def _mix_stream_rows(prompt_ref, sample_ref, rows):
    return jnp.where(pl.program_id(0) < N_PROMPT // MIX_TM, prompt_ref[rows, :], sample_ref[rows, :])


def _mixer_tail(mix_of, x_of, mod_ref, nw2_ref, wo_ref, wr_ref, br_ref, x1_ref, h2p_ref, lg_ref):
    cond = _tile_cond(pl.program_id(0) * (MIX_TM // TM))
    gate, shift, scale = (_mod_row(mod_ref, cond, which) for which in (2, 3, 4))
    project = lambda rows: jnp.dot(mix_of(rows), wo_ref[...], preferred_element_type=F32)

    def tail(rows, out):
        x1 = x_of(rows) + gate * out
        x1_ref[rows, :] = x1
        h2 = _norm_mod(x1, nw2_ref[...], shift, scale)
        h2p_ref[rows, :] = jnp.concatenate(_pack_rows(h2), axis=1)
        h_hi = h2.astype(BF16)
        h_lo = (h2 - h_hi.astype(F32)).astype(BF16)
        lg_ref[rows, :] = (jnp.dot(h_hi, wr_ref[0], preferred_element_type=F32)
                           + (jnp.dot(h_hi, wr_ref[1], preferred_element_type=F32)
                              + jnp.dot(h_lo, wr_ref[0], preferred_element_type=F32))) + br_ref[...]

    sub_tiles = [slice(s * TM, (s + 1) * TM) for s in range(MIX_TM // TM)]
    out = project(sub_tiles[0])
    for s, rows in enumerate(sub_tiles):
        out_next = project(sub_tiles[s + 1]) if s + 1 < len(sub_tiles) else None
        tail(rows, out)
        out = out_next


def _route_kernel(lg_ref, rt_ref, rw_ref, cnt_ref, carry_scr):
    @pl.when(pl.program_id(0) == 0)
    def _():
        carry_scr[...] = jnp.zeros_like(carry_scr)

    logits = lg_ref[...]
    i1, i2, w1, w2 = _route(logits)
    lane = lax.broadcasted_iota(jnp.int32, logits.shape, 1)
    chosen = ((lane == i1) | (lane == i2)).astype(BF16)
    r = lax.broadcasted_iota(jnp.int32, (ROUTE_TM, ROUTE_TM), 0)
    c = lax.broadcasted_iota(jnp.int32, (ROUTE_TM, ROUTE_TM), 1)
    before = jnp.dot((c < r).astype(BF16), chosen, preferred_element_type=F32) + carry_scr[...]
    r1 = jnp.sum(jnp.where(lane == i1, before, 0.0), axis=-1, keepdims=True).astype(jnp.int32)
    r2 = jnp.sum(jnp.where(lane == i2, before, 0.0), axis=-1, keepdims=True).astype(jnp.int32)
    total = carry_scr[...] + jnp.sum(chosen.astype(F32), axis=0, keepdims=True)
    carry_scr[...] = total
    tiles = jnp.ceil(total * (1.0 / MOE_TILE))
    e_from = lax.broadcasted_iota(jnp.int32, (ROUTER_LANES, ROUTER_LANES), 0)
    e_to = lax.broadcasted_iota(jnp.int32, (ROUTER_LANES, ROUTER_LANES), 1)
    tiles_before = jnp.dot(jnp.broadcast_to(tiles, (SUBLANES, ROUTER_LANES)).astype(BF16),
                           (e_from < e_to).astype(BF16), preferred_element_type=F32)
    cnt_ref[...] = jnp.concatenate([total, tiles_before[:1] * MOE_TILE,
                                    jnp.zeros((ROUTE_ROWS - 2, ROUTER_LANES), F32)], axis=0)
    by_token = jnp.where(lane == 0, i1, jnp.where(lane == 1, i2, jnp.where(lane == 2, r1, r2)))
    rt_ref[...] = by_token.T[:ROUTE_ROWS, :]
    rw_ref[...] = jnp.where(lane == 0, w1, w2)


def _route_tokens(logits):
    blk = pl.BlockSpec((ROUTE_TM, ROUTER_LANES), lambda i: (i, 0))
    return pl.pallas_call(
        _route_kernel,
        out_shape=(
            jax.ShapeDtypeStruct((ROUTE_ROWS, N_TOK), jnp.int32),
            jax.ShapeDtypeStruct((N_TOK, ROUTER_LANES), F32),
            jax.ShapeDtypeStruct((ROUTE_ROWS, ROUTER_LANES), F32),
        ),
        grid=(N_TOK // ROUTE_TM,),
        in_specs=[blk],
        out_specs=(pl.BlockSpec((ROUTE_ROWS, ROUTE_TM), lambda i: (0, i)), blk,
                   pl.BlockSpec((ROUTE_ROWS, ROUTER_LANES), lambda i: (0, 0))),
        scratch_shapes=[pltpu.VMEM((1, ROUTER_LANES), F32)],
        compiler_params=_params(32, "arbitrary"),
        name="moe_route",
    )(logits)


def _hg_out_kernel(of_ref, ob_ref, g_ref, on_ref, xp_ref, xs_ref, *rest):
    def mix_of(rows):
        o = of_ref[rows, :].astype(F32) + ob_ref[rows, :].astype(F32)
        parts = []
        for h in range(HG_HEADS):
            oh = o[:, h * HG_DK:(h + 1) * HG_DK]
            parts.append(oh * lax.rsqrt(jnp.mean(oh * oh, axis=-1, keepdims=True) + EPS) * on_ref[...])
        y = jnp.concatenate(parts, axis=1) * _silu(g_ref[rows, :].astype(F32))
        return y.astype(BF16)

    _mixer_tail(mix_of, functools.partial(_mix_stream_rows, xp_ref, xs_ref), *rest)


def _at_out_kernel(ac_ref, al_ref, x_ref, *rest):
    _mixer_tail(functools.partial(_mix_stream_rows, ac_ref, al_ref), lambda rows: x_ref[rows, :], *rest)


def _mixer_out(kernel_fn, name, mix_inputs, mix_specs, mod, nw2, w_out, w_router, b_router):
    tile = lambda i: (i, 0)
    fixed2 = lambda i: (0, 0)
    blk = pl.BlockSpec((MIX_TM, D_MODEL), tile)
    lanes_blk = pl.BlockSpec((MIX_TM, ROUTER_LANES), tile)
    return pl.pallas_call(
        kernel_fn,
        out_shape=(
            jax.ShapeDtypeStruct((N_TOK, D_MODEL), F32),
            jax.ShapeDtypeStruct((N_TOK, ROW_WORDS), jnp.uint32),
            jax.ShapeDtypeStruct((N_TOK, ROUTER_LANES), F32),
        ),
        grid=(N_TOK // MIX_TM,),
        in_specs=list(mix_specs) + [
            pl.BlockSpec(mod.shape, fixed2),
            pl.BlockSpec((1, D_MODEL), fixed2),
            pl.BlockSpec(w_out.shape, fixed2),
            pl.BlockSpec(w_router.shape, lambda i: (0, 0, 0)),
            pl.BlockSpec((1, ROUTER_LANES), fixed2),
        ],
        out_specs=(blk, pl.BlockSpec((MIX_TM, ROW_WORDS), tile), lanes_blk),
        compiler_params=_params(40, "arbitrary"),
        name=name,
    )(*mix_inputs, mod, nw2, w_out, w_router, b_router)


def _moe_plan(route_t, counts):
    cnt = counts[0, :N_EXPERTS].astype(jnp.int32)
    offs = counts[1, :N_EXPERTS].astype(jnp.int32)
    ends = offs + ((cnt + MOE_TILE - 1) // MOE_TILE) * MOE_TILE
    experts = route_t[0:2]
    pos = jnp.sum(jnp.where(experts[None, :, :] == jnp.arange(N_EXPERTS)[:, None, None], offs[:, None, None], 0),
                  axis=0) + route_t[2:4]
    tile_start = jnp.arange(MOE_ROWS // MOE_TILE, dtype=jnp.int32) * MOE_TILE
    tile_expert = jnp.minimum(jnp.sum(ends[None, :] <= tile_start[:, None], axis=1), N_EXPERTS - 1).astype(jnp.int32)
    of_tile = tile_expert[:, None] == jnp.arange(N_EXPERTS)[None, :]
    tile_offs = jnp.sum(jnp.where(of_tile, offs[None, :], 0), axis=1)
    tile_cnt = jnp.sum(jnp.where(of_tile, cnt[None, :], 0), axis=1)
    tile_rows = jnp.clip(tile_offs + tile_cnt - tile_start, 0, MOE_TILE).astype(jnp.int32)
    active = tile_start < ends[-1]
    tile_opens = jnp.where(active & (tile_start == tile_offs), tile_expert, -1).astype(jnp.int32)
    tile_opens = jnp.concatenate([tile_opens, jnp.full((MOE_STEP_TILES,), -1, jnp.int32)])
    n_active = (ends[-1] // MOE_TILE).astype(jnp.int32).reshape(1)
    slot = (jnp.cumsum((cnt > 0).astype(jnp.int32)) - 1) % MOE_WEIGHT_SLOTS
    tile_slot = jnp.sum(jnp.where(of_tile, slot[None, :], 0), axis=1)
    last_slot = jnp.sum(jnp.where(tile_start == ends[-1] - MOE_TILE, tile_slot, 0))
    tile_slot = jnp.where(active, tile_slot, last_slot).astype(jnp.int32)
    return pos.astype(jnp.int32), (tile_rows, tile_slot, tile_opens, n_active)


SC_WINDOW = 64


def _sc_mesh():
    return plsc.VectorSubcoreMesh(core_axis_name="c", subcore_axis_name="s")


def _sc_scatter_rows(x, idx_a, idx_b, n_out_rows):
    n = x.shape[0]

    @functools.partial(pl.kernel, out_type=jax.ShapeDtypeStruct((n_out_rows, ROW_WORDS), x.dtype), mesh=_sc_mesh(),
                       scratch_types=[pltpu.SemaphoreType.DMA, pltpu.SemaphoreType.DMA])
    def scatter(x_hbm, ia_hbm, ib_hbm, o_hbm, sem_a, sem_b):
        def body(x_vmem, ia_vmem, ib_vmem):
            copy_a = pltpu.async_copy(x_vmem, o_hbm.at[ia_vmem.at[0]], sem_a)
            copy_b = pltpu.async_copy(x_vmem, o_hbm.at[ib_vmem.at[0]], sem_b)
            copy_a.wait()
            copy_b.wait()

        idx_spec = pl.BlockSpec((1, SC_WINDOW), index_map=lambda i: (i, 0))
        pltpu.emit_pipeline(
            body, grid=(n // SC_WINDOW,),
            in_specs=[pl.BlockSpec((SC_WINDOW, ROW_WORDS), index_map=lambda i: (i, 0)), idx_spec, idx_spec],
            out_specs=[],
            core_axis_name=("c", "s"), dimension_semantics=(pltpu.PARALLEL,),
        )(x_hbm, ia_hbm, ib_hbm)

    return scatter(x, idx_a, idx_b)


def _sc_gather_rows(table, idx):
    n = idx.size

    @functools.partial(pl.kernel, out_type=jax.ShapeDtypeStruct((n, ROW_WORDS), table.dtype), mesh=_sc_mesh())
    def gather(t_hbm, i_hbm, o_hbm):
        def body(i_vmem, o_vmem):
            pltpu.sync_copy(t_hbm.at[i_vmem.at[0]], o_vmem)

        pltpu.emit_pipeline(
            body, grid=(n // SC_WINDOW,),
            in_specs=[pl.BlockSpec((1, SC_WINDOW), index_map=lambda i: (i, 0))],
            out_specs=[pl.BlockSpec((SC_WINDOW, ROW_WORDS), index_map=lambda i: (i, 0))],
            core_axis_name=("c", "s"), dimension_semantics=(pltpu.PARALLEL,),
        )(i_hbm, o_hbm)

    return gather(table, idx)


def _ffn_kernel(tr_ref, ts_ref, to_ref, na_ref, xs_ref, w1_hbm, w3_hbm, w2_hbm, ys_ref,
                w1_buf, w3_buf, w2_buf, sem, *, layer):
    def weight_copies(t):
        pairs = ((w1_hbm, w1_buf), (w3_hbm, w3_buf), (w2_hbm, w2_buf))
        return [pltpu.make_async_copy(hbm.at[layer, to_ref[t]], buf.at[ts_ref[t]], sem.at[ts_ref[t], j])
                for j, (hbm, buf) in enumerate(pairs)]

    def for_experts_opened_from(first_tile, action):
        for j in range(MOE_STEP_TILES):
            t = first_tile + j

            @pl.when(to_ref[t] >= 0)
            def _():
                for copy in weight_copies(t):
                    action(copy)

    def tile(t, rows):
        slot = ts_ref[t]
        row = lax.broadcasted_iota(jnp.int32, (MOE_TILE, PACK_WORDS), 0)
        live = row < tr_ref[t]
        halves = [jnp.where(live, xs_ref[rows, h * PACK_WORDS:(h + 1) * PACK_WORDS],
                            jnp.zeros((MOE_TILE, PACK_WORDS), jnp.uint32)) for h in range(2)]
        chunks = _unpack_rows(*halves)

        def up(w_buf):
            acc = None
            for k, chunk in enumerate(chunks):
                w = w_buf[slot, k * PACK_WORDS:(k + 1) * PACK_WORDS, :].astype(BF16)
                part = jnp.dot(chunk, w, preferred_element_type=F32)
                acc = part if acc is None else acc + part
            return acc

        hid = (_silu(up(w1_buf)) * up(w3_buf)).astype(BF16)
        y = jnp.dot(hid, w2_buf[slot].astype(BF16), preferred_element_type=F32)
        ys_ref[rows, :] = jnp.concatenate(_pack_rows(y), axis=1)

    step = pl.program_id(0)
    first_tile = step * MOE_STEP_TILES

    @pl.when(first_tile < na_ref[0])
    def _():
        @pl.when(step == 0)
        def _():
            for_experts_opened_from(first_tile, lambda copy: copy.start())

        for_experts_opened_from(first_tile + MOE_STEP_TILES, lambda copy: copy.start())
        for_experts_opened_from(first_tile, lambda copy: copy.wait())
        for j in range(MOE_STEP_TILES):
            tile(first_tile + j, slice(j * MOE_TILE, (j + 1) * MOE_TILE))


def _ffn(xs, tables, layer, w1, w3, w2):
    step_rows = MOE_STEP_TILES * MOE_TILE
    row_tile = lambda s, tr, ts, to, na: (jnp.minimum(s, (na[0] - 1) // MOE_STEP_TILES), 0)
    hbm = pl.BlockSpec(memory_space=pl.ANY)
    return pl.pallas_call(
        functools.partial(_ffn_kernel, layer=layer),
        out_shape=jax.ShapeDtypeStruct((MOE_ROWS, ROW_WORDS), jnp.uint32),
        grid_spec=pltpu.PrefetchScalarGridSpec(
            num_scalar_prefetch=len(tables),
            grid=(MOE_ROWS // step_rows,),
            in_specs=[pl.BlockSpec((step_rows, ROW_WORDS), row_tile), hbm, hbm, hbm],
            out_specs=pl.BlockSpec((step_rows, ROW_WORDS), row_tile),
            scratch_shapes=[
                pltpu.VMEM((MOE_WEIGHT_SLOTS, D_MODEL, D_EXPERT), F32),
                pltpu.VMEM((MOE_WEIGHT_SLOTS, D_MODEL, D_EXPERT), F32),
                pltpu.VMEM((MOE_WEIGHT_SLOTS, D_EXPERT, D_MODEL), F32),
                pltpu.SemaphoreType.DMA((MOE_WEIGHT_SLOTS, 3)),
            ],
        ),
        compiler_params=_params(48, "arbitrary"),
        name="moe_experts",
    )(*tables, xs, w1, w3, w2)


def _moe_res(x_ref, y_ref, rw_ref, mod_ref, first_tile=0):
    cond = _tile_cond(pl.program_id(0) + first_tile)
    rw = rw_ref[...]
    wa = rw[:, 0:1]
    wb = rw[:, 1:2]
    ya = _unpack_rows(y_ref[0, :, :PACK_WORDS], y_ref[0, :, PACK_WORDS:])
    yb = _unpack_rows(y_ref[1, :, :PACK_WORDS], y_ref[1, :, PACK_WORDS:])
    y = jnp.concatenate([wa * a.astype(F32) + wb * b.astype(F32) for a, b in zip(ya, yb)], axis=1)
    return x_ref[...] + _mod_row(mod_ref, cond, 5) * y


def _moe_res_final_kernel(x_ref, y_ref, rw_ref, mod_ref, fn_ref, o_ref, *, first_tile):
    x = _moe_res(x_ref, y_ref, rw_ref, mod_ref, first_tile)
    o_ref[...] = x * lax.rsqrt(jnp.mean(x * x, axis=-1, keepdims=True) + EPS) * fn_ref[...]


def _moe_combine(x1, y_pairs, route_w, mod, final_norm, first_tile):
    n = y_pairs.shape[1]
    stream_tile = lambda i: (i + first_tile, 0)
    fixed2 = lambda i: (0, 0)
    return pl.pallas_call(
        functools.partial(_moe_res_final_kernel, first_tile=first_tile),
        out_shape=jax.ShapeDtypeStruct((n, D_MODEL), F32),
        grid=(n // TM,),
        in_specs=[pl.BlockSpec((TM, D_MODEL), stream_tile),
                  pl.BlockSpec((2, TM, ROW_WORDS), lambda i: (0, i, 0)),
                  pl.BlockSpec((TM, ROUTER_LANES), stream_tile),
                  pl.BlockSpec(mod.shape, fixed2),
                  pl.BlockSpec((1, D_MODEL), fixed2)],
        out_specs=pl.BlockSpec((TM, D_MODEL), lambda i: (i, 0)),
        compiler_params=_params(32, "arbitrary"),
        name="moe_combine",
    )(x1, y_pairs, route_w, mod, final_norm)


def _moe_layer(x1, h2p, logits, layer, w1, w3, w2, token_parts=((0, N_TOK),)):
    route_t, route_w, counts = _route_tokens(logits)
    pos, tables = _moe_plan(route_t, counts)
    windows = pos.reshape(2, N_TOK // SC_WINDOW, SC_WINDOW)
    xs = _sc_scatter_rows(h2p, windows[0], windows[1], MOE_ROWS)
    ys = _ffn(xs, tables, layer, w1, w3, w2)
    y_parts = []
    for first, last in token_parts:
        part = windows[:, first // SC_WINDOW:last // SC_WINDOW].reshape(2 * (last - first) // SC_WINDOW, SC_WINDOW)
        y_parts.append(_sc_gather_rows(ys, part).reshape(2, last - first, ROW_WORDS))
    return x1, y_parts, route_w


def _swap_rotary_halves(x):
    n = x.shape[-1]
    lane = lax.broadcasted_iota(jnp.int32, x.shape, 1)
    quarter = ROPE_HALF // 2
    return jnp.where((lane % ROPE_HALF) < quarter, pltpu.roll(x, n - quarter, 1), pltpu.roll(x, quarter, 1))


def _at_in_kernel(x1_ref, y_ref, rw_ref, mod_prev_ref, mod_ref, nw_ref, w_ref, cos_ref, sin_ref,
                  x_ref, q_ref, kt_ref, v_ref, kc_ref, vc_ref):
    i = pl.program_id(0)
    cond = _tile_cond(i)
    x = _moe_res(x1_ref, y_ref, rw_ref, mod_prev_ref)
    x_ref[...] = x
    h = _norm_mod(x, nw_ref[...], _mod_row(mod_ref, cond, 0), _mod_row(mod_ref, cond, 1)).astype(BF16)
    nq = ATT_Q_HEADS * ATT_HEAD_DIM
    nk = ATT_KV_HEADS * ATT_HEAD_DIM
    cos = cos_ref[...]
    sin = sin_ref[...]

    def rope(x):
        reps = x.shape[-1] // LANES
        return x * jnp.concatenate([cos] * reps, axis=1) + _swap_rotary_halves(x) * jnp.concatenate([sin] * reps, axis=1)

    q_ref[...] = (rope(jnp.dot(h, w_ref[:, :nq], preferred_element_type=F32)) * ATT_HEAD_DIM ** -0.5).astype(BF16)
    k = rope(jnp.dot(h, w_ref[:, nq:nq + nk], preferred_element_type=F32))
    v = jnp.dot(h, w_ref[:, nq + nk:], preferred_element_type=F32)
    kt = k.T
    kt_ref[...] = kt.astype(BF16)
    v_ref[...] = v.astype(BF16)

    @pl.when(i < PROMPT_TILES)
    def _():
        kc_ref[...] = kt
        vc_ref[...] = v.T


def _rope_tables():
    f32 = np.float32
    pos = np.arange(DEC_SEQ)
    t_row = (pos // GRID_W).astype(f32)
    t_col = (pos % GRID_W).astype(f32)
    inv = f32(ROPE_BASE) ** (-np.arange(0, ROPE_HALF, 2, dtype=f32) / f32(ROPE_HALF))
    j = np.arange(LANES) % ATT_HEAD_DIM
    freq = inv[(j % ROPE_HALF) % (ROPE_HALF // 2)]
    ang = (np.where((j < ROPE_HALF)[None, :], t_row[:, None], t_col[:, None]) * freq[None, :]).astype(f32)
    sign = np.where((j % ROPE_HALF) < ROPE_HALF // 2, -1.0, 1.0).astype(f32)
    cos = np.concatenate([np.ones((TM, LANES), f32), np.cos(ang)], axis=0)
    sin = np.concatenate([np.zeros((TM, LANES), f32), np.sin(ang) * sign[None, :]], axis=0)
    return jnp.asarray(cos, F32), jnp.asarray(sin, F32)


def _at_in(x1, y_pairs, route_w, mod_prev, mod, nw, w_in, cos, sin):
    tile = lambda i: (i, 0)
    fixed2 = lambda i: (0, 0)
    rope_tile = lambda i: (jnp.where(i < PROMPT_TILES, 0, 1 + (i - PROMPT_TILES) % TILES_PER_DEC_SEQ), 0)
    nk = ATT_KV_HEADS * ATT_HEAD_DIM
    return pl.pallas_call(
        _at_in_kernel,
        out_shape=(
            jax.ShapeDtypeStruct((N_TOK, D_MODEL), F32),
            jax.ShapeDtypeStruct((N_TOK, D_MODEL), BF16),
            jax.ShapeDtypeStruct((nk, N_TOK), BF16),
            jax.ShapeDtypeStruct((N_TOK, nk), BF16),
            jax.ShapeDtypeStruct((N_PROMPT, nk), F32),
            jax.ShapeDtypeStruct((N_PROMPT, nk), F32),
        ),
        grid=(N_TILES,),
        in_specs=[
            pl.BlockSpec((TM, D_MODEL), tile),
            pl.BlockSpec((2, TM, ROW_WORDS), lambda i: (0, i, 0)),
            pl.BlockSpec((TM, ROUTER_LANES), tile),
            pl.BlockSpec(mod_prev.shape, fixed2),
            pl.BlockSpec(mod.shape, fixed2),
            pl.BlockSpec((1, D_MODEL), fixed2),
            pl.BlockSpec(w_in.shape, fixed2),
            pl.BlockSpec((TM, LANES), rope_tile),
            pl.BlockSpec((TM, LANES), rope_tile),
        ],
        out_specs=(pl.BlockSpec((TM, D_MODEL), tile), pl.BlockSpec((TM, D_MODEL), tile),
                   pl.BlockSpec((nk, TM), lambda i: (0, i)),
                   pl.BlockSpec((TM, nk), tile),
                   pl.BlockSpec((TM, nk), _prompt_tile), pl.BlockSpec((TM, nk), _prompt_tile)),
        compiler_params=_params(40, "arbitrary"),
        name="attn_in_proj",
    )(x1, y_pairs, route_w, mod_prev, mod, nw, w_in, cos, sin)


def _attend(q, kt_all, v_all, mask, sink_ref, o_ref):
    nq = q.shape[0]
    group_lanes = ATT_GROUP * ATT_HEAD_DIM
    lane = lax.broadcasted_iota(jnp.int32, (nq, group_lanes), 1)
    mine = [(lane // ATT_HEAD_DIM) == g for g in range(ATT_GROUP)]
    row_head = lax.broadcasted_iota(jnp.int32, (ATT_GROUP * nq, 1), 0) // nq
    if mask is not None:
        mask = jnp.concatenate([mask] * ATT_GROUP, axis=0)
    for hk in range(ATT_KV_HEADS):
        vh = v_all[:, hk * ATT_HEAD_DIM:(hk + 1) * ATT_HEAD_DIM]
        kt = jnp.concatenate([kt_all[hk * ATT_HEAD_DIM:(hk + 1) * ATT_HEAD_DIM, :]] * ATT_GROUP, axis=0)
        vt = jnp.concatenate([vh] * ATT_GROUP, axis=1)
        qg = q[:, hk * group_lanes:(hk + 1) * group_lanes]
        q_stack = jnp.concatenate([jnp.where(mine[g], qg, jnp.zeros_like(qg)) for g in range(ATT_GROUP)], axis=0)
        s = jnp.dot(q_stack, kt, preferred_element_type=F32)
        if mask is not None:
            s = jnp.where(mask, s, -jnp.inf)
        sink = jnp.zeros((ATT_GROUP * nq, 1), F32)
        for g in range(ATT_GROUP):
            sink = jnp.where(row_head == g, sink_ref[hk * ATT_GROUP + g], sink)
        m = jnp.maximum(jnp.max(s, axis=-1, keepdims=True), sink)
        p = jnp.exp(s - m)
        denom = jnp.sum(p, axis=-1, keepdims=True) + jnp.exp(sink - m)
        o = jnp.dot(p.astype(BF16), vt, preferred_element_type=F32) / denom
        acc = jnp.where(mine[0], o[:nq], 0.0)
        for g in range(1, ATT_GROUP):
            acc = acc + jnp.where(mine[g], o[g * nq:(g + 1) * nq], 0.0)
        o_ref[:, hk * group_lanes:(hk + 1) * group_lanes] = acc.astype(BF16)


def _ctx_attn_kernel(sink_ref, q_ref, k_ref, v_ref, o_ref):
    _attend(q_ref[...], k_ref[...], v_ref[...], None, sink_ref, o_ref)


def _lat_attn_kernel(sink_ref, q_ref, kp_ref, kc_ref, kn_ref, vp_ref, vc_ref, vn_ref, ck_ref, cv_ref, o_ref):
    jb = pl.program_id(1)
    kt_all = jnp.concatenate([kp_ref[...], kc_ref[...], kn_ref[...], ck_ref[0].astype(BF16)], axis=1)
    v_all = jnp.concatenate([vp_ref[...], vc_ref[...], vn_ref[...], cv_ref[0].astype(BF16)], axis=0)
    nkeys = 3 * BLOCK + PAST_LEN
    qi = lax.broadcasted_iota(jnp.int32, (BLOCK, nkeys), 0)
    kj = lax.broadcasted_iota(jnp.int32, (BLOCK, nkeys), 1)
    qpos = jb * BLOCK + qi
    kpos = (jb - 1) * BLOCK + kj
    local_ok = (jnp.abs(qpos - kpos) <= WINDOW) & (kpos >= 0) & (kpos < DEC_SEQ)
    mask = (kj >= 3 * BLOCK) | local_ok
    _attend(q_ref[...], kt_all, v_all, mask, sink_ref, o_ref)


def _attention(q, kt, v, cache_kt, cache_v, sink):
    nk = ATT_KV_HEADS * ATT_HEAD_DIM
    smem = pl.BlockSpec(memory_space=pltpu.SMEM)
    ctx = pl.pallas_call(
        _ctx_attn_kernel,
        out_shape=jax.ShapeDtypeStruct((N_PROMPT, D_MODEL), BF16),
        grid=(BATCH,),
        in_specs=[
            smem,
            pl.BlockSpec((SEQ, D_MODEL), lambda b: (b, 0)),
            pl.BlockSpec((nk, SEQ), lambda b: (0, b)),
            pl.BlockSpec((SEQ, nk), lambda b: (b, 0)),
        ],
        out_specs=pl.BlockSpec((SEQ, D_MODEL), lambda b: (b, 0)),
        compiler_params=_params(40, "arbitrary"),
        name="context_attention",
    )(sink, q, kt, v)

    nb = DEC_SEQ // BLOCK
    base = N_PROMPT // BLOCK
    cur = lambda b, j: (base + b * nb + j, 0)
    prev = lambda b, j: (base + b * nb + jnp.maximum(j - 1, 0), 0)
    nxt = lambda b, j: (base + b * nb + jnp.minimum(j + 1, nb - 1), 0)
    kv_blk = lambda f: pl.BlockSpec((BLOCK, nk), f)
    kt_blk = lambda f: pl.BlockSpec((nk, BLOCK), lambda b, j: f(b, j)[::-1])
    cache_blk = pl.BlockSpec((1, PAST_LEN, nk), lambda b, j: (b, 0, 0))
    cache_kt_blk = pl.BlockSpec((1, nk, PAST_LEN), lambda b, j: (b, 0, 0))
    lat = pl.pallas_call(
        _lat_attn_kernel,
        out_shape=jax.ShapeDtypeStruct((DEC_BATCH * DEC_SEQ, D_MODEL), BF16),
        grid=(DEC_BATCH, nb),
        in_specs=[
            smem,
            pl.BlockSpec((BLOCK, D_MODEL), cur),
            kt_blk(prev), kt_blk(cur), kt_blk(nxt),
            kv_blk(prev), kv_blk(cur), kv_blk(nxt),
            cache_kt_blk, cache_blk,
        ],
        out_specs=pl.BlockSpec((BLOCK, D_MODEL), lambda b, j: (b * nb + j, 0)),
        compiler_params=_params(40, "arbitrary", "arbitrary"),
        name="latent_attention",
    )(sink, q, kt, kt, kt, v, v, v, cache_kt, cache_v)
    return ctx, lat


def kernel(x_prompt, x_sample, state_hgrn, cache_k, cache_v, c, c_ctx, ada_w, ada_b, norm_w, hg_w_in,
           hg_lb_logits, hg_onorm, hg_w_out, at_w_in, at_sink, at_w_out, moe_w_group, moe_b_group,
           moe_w_expert, moe_b_expert, moe_w1, moe_w3, moe_w2, final_norm):
    xp = x_prompt.reshape(N_PROMPT, D_MODEL)
    xs = x_sample.reshape(N_TOK - N_PROMPT, D_MODEL)
    cond = jnp.concatenate([c_ctx[None, :], c], axis=0)
    mod = _ada(cond, ada_w, ada_b)
    nk = ATT_KV_HEADS * ATT_HEAD_DIM

    def router_params(i):
        pad = jnp.zeros((D_MODEL, ROUTER_LANES - N_EXPERTS - N_GROUPS), F32)
        w = jnp.concatenate([moe_w_expert[i], moe_w_group[i], pad], axis=1)
        b = jnp.concatenate([moe_b_expert[i], moe_b_group[i], pad[0]])[None, :]
        hi = w.astype(BF16)
        lo = (w - hi.astype(F32)).astype(BF16)
        return jnp.stack([hi, lo]), b

    mix_blk = pl.BlockSpec((MIX_TM, D_MODEL), lambda i: (i, 0))
    mix_prompt_blk = pl.BlockSpec((MIX_TM, D_MODEL), lambda i: (jnp.minimum(i, N_PROMPT // MIX_TM - 1), 0))
    mix_sample_blk = pl.BlockSpec((MIX_TM, D_MODEL), lambda i: (jnp.maximum(i - N_PROMPT // MIX_TM, 0), 0))
    q, v, g, lff, kf, lfb, kb = _hg_in(xp, xs, mod[0], norm_w[0, 0][None, :], hg_w_in[0].astype(BF16), hg_lb_logits)
    o_f, o_b, state_new = _gla(q, v, lff, kf, lfb, kb, state_hgrn)
    wr, br = router_params(0)
    routed = _mixer_out(
        _hg_out_kernel, "hgrn_out_route", (o_f, o_b, g, hg_onorm[0][None, :], xp, xs),
        (mix_blk, mix_blk, mix_blk, pl.BlockSpec((1, HG_DK), lambda i: (0, 0)), mix_prompt_blk, mix_sample_blk),
        mod[0], norm_w[0, 1][None, :], hg_w_out[0].astype(BF16), wr, br)
    x1, (y_pairs,), route_w = _moe_layer(*routed, 0, moe_w1, moe_w3, moe_w2)

    cos, sin = _rope_tables()
    x, qa, ka, va, k_ctx, v_ctx = _at_in(x1, y_pairs, route_w, mod[0], mod[1], norm_w[1, 0][None, :],
                                         at_w_in[0].astype(BF16), cos, sin)
    attn_ctx, attn_lat = _attention(qa, ka, va, cache_k[:, 0].reshape(DEC_BATCH, PAST_LEN, nk).transpose(0, 2, 1),
                                    cache_v[:, 0].reshape(DEC_BATCH, PAST_LEN, nk), at_sink[0])
    wr, br = router_params(1)
    routed = _mixer_out(
        _at_out_kernel, "attn_out_route", (attn_ctx, attn_lat, x), (mix_prompt_blk, mix_sample_blk, mix_blk),
        mod[1], norm_w[1, 1][None, :], at_w_out[0].astype(BF16), wr, br)
    x1, (y_lat, y_ctx), route_w = _moe_layer(*routed, 1, moe_w1, moe_w3, moe_w2,
                                             token_parts=((N_PROMPT, N_TOK), (0, N_PROMPT)))
    y_sample = _moe_combine(x1, y_lat, route_w, mod[1], final_norm[None, :], PROMPT_TILES)
    y_prompt = _moe_combine(x1, y_ctx, route_w, mod[1], final_norm[None, :], 0)

    def cache(feature_major):
        return feature_major.reshape(BATCH, 1, ATT_KV_HEADS, ATT_HEAD_DIM, SEQ).transpose(0, 1, 4, 2, 3)

    return (y_prompt.reshape(BATCH, SEQ, D_MODEL), y_sample.reshape(DEC_BATCH, DEC_SEQ, D_MODEL), state_new,
            cache(k_ctx), cache(v_ctx))
```

```python
import functools
from typing import Any, NamedTuple

import jax
import jax.numpy as jnp
import numpy as np
from jax import lax
from jax.experimental import pallas as pl
from jax.experimental.pallas import tpu as pltpu
from jax.experimental.pallas import tpu_sc as plsc

F32 = jnp.float32
BF16 = jnp.bfloat16

D_MODEL = 1024
BATCH = 16
SEQ = 256
DEC_BATCH = 2
DEC_SEQ = 1024
PAST_LEN = 512
GRID_W = 64
HG_HEADS = 8
HG_DK = 128
CHUNK = 16
ATT_HEAD_DIM = 64
ATT_Q_HEADS = 16
ATT_KV_HEADS = 4
ATT_GROUP = 4
WINDOW = 128
BLOCK = 128
ROPE_HALF = 32
ROPE_BASE = 10000.0
N_GROUPS = 4
EXPERTS_PER_GROUP = 8
N_EXPERTS = 32
D_EXPERT = 256
EPS = 1e-6

N_PROMPT = BATCH * SEQ
N_TOK = N_PROMPT + DEC_BATCH * DEC_SEQ
TM = 256
N_TILES = N_TOK // TM
PROMPT_TILES = N_PROMPT // TM
TILES_PER_DEC_SEQ = DEC_SEQ // TM
LANES = 128
SUBLANES = 8
ADA_LANE_BLOCKS = 4
N_COND_USED = 1 + DEC_BATCH
N_COND = 8
ROUTER_LANES = 128
ROUTE_TM = 1024
ROUTE_ROWS = 8
MIX_TM = 512
AT_IN_TM = 512
HG_IN_TM = 512
HG_IN_SUB = 256
HG_IN_TN = 256
DECAY_CLAMP = 60.0
PACK_WORDS = D_MODEL // 4
ROW_WORDS = 2 * PACK_WORDS
MOE_TILE = 256
MOE_STEP_TILES = 4
MOE_WEIGHT_SLOTS = 2 * MOE_STEP_TILES
MOE_ROWS = 2 * N_TOK + N_EXPERTS * MOE_TILE
MIB = 1024 * 1024


def _params(vmem_mib, *semantics):
    return pltpu.CompilerParams(dimension_semantics=semantics, vmem_limit_bytes=vmem_mib * MIB)


def _tile_cond(i):
    return jnp.where(i < PROMPT_TILES, 0, 1 + (i - PROMPT_TILES) // TILES_PER_DEC_SEQ)


def _prompt_tile(i):
    return (jnp.minimum(i, PROMPT_TILES - 1), 0)


def _sample_tile(i):
    return (jnp.maximum(i - PROMPT_TILES, 0), 0)


def _mod_row(mod_ref, cond, which):
    return mod_ref[pl.ds(cond, 1), which * D_MODEL:(which + 1) * D_MODEL]


def _norm_mod(x, nw, shift, scale):
    y = x * lax.rsqrt(jnp.mean(x * x, axis=-1, keepdims=True) + EPS)
    return (y * nw) * (1.0 + scale) + shift


def _silu(x):
    return x * jax.nn.sigmoid(x)


def _ada_kernel(c_ref, w_ref, b_ref, o_ref, s_scr):
    @pl.when((pl.program_id(0) == 0) & (pl.program_id(1) == 0))
    def _():
        s_scr[...] = _silu(c_ref[...])

    tn = w_ref.shape[-1]
    cols = [[] for _ in range(N_COND_USED)]
    for j0 in range(0, tn // LANES, ADA_LANE_BLOCKS):
        acc = [[jnp.zeros((SUBLANES, LANES), F32) for _ in range(ADA_LANE_BLOCKS)] for _ in range(N_COND_USED)]
        for g in range(D_MODEL // SUBLANES):
            rows = slice(g * SUBLANES, (g + 1) * SUBLANES)
            w = [w_ref[0, rows, (j0 + j) * LANES:(j0 + j + 1) * LANES] for j in range(ADA_LANE_BLOCKS)]
            for r in range(N_COND_USED):
                s = s_scr[r, rows, :]
                for j in range(ADA_LANE_BLOCKS):
                    acc[r][j] = acc[r][j] + w[j] * s
        for r in range(N_COND_USED):
            cols[r] += [jnp.sum(a, axis=0, keepdims=True) for a in acc[r]]
    out = [jnp.concatenate(cols[r], axis=1) + b_ref[0] for r in range(N_COND_USED)]
    out.append(jnp.zeros((N_COND - N_COND_USED, tn), F32))
    o_ref[0] = jnp.concatenate(out, axis=0)


def _ada(cond, ada_w, ada_b):
    depth, _, n = ada_w.shape
    tn = 1536
    cond_cols = jnp.broadcast_to(cond[:, :, None], (N_COND_USED, D_MODEL, LANES))
    return pl.pallas_call(
        _ada_kernel,
        out_shape=jax.ShapeDtypeStruct((depth, N_COND, n), F32),
        grid=(depth, n // tn),
        in_specs=[
            pl.BlockSpec((N_COND_USED, D_MODEL, LANES), lambda l, j: (0, 0, 0)),
            pl.BlockSpec((1, D_MODEL, tn), lambda l, j: (l, 0, j)),
            pl.BlockSpec((1, 1, tn), lambda l, j: (l, 0, j)),
        ],
        out_specs=pl.BlockSpec((1, N_COND, tn), lambda l, j: (l, 0, j)),
        scratch_shapes=[pltpu.VMEM((N_COND_USED, D_MODEL, LANES), F32)],
        compiler_params=_params(40, "arbitrary", "arbitrary"),
        name="ada_modulation",
    )(cond_cols, ada_w, ada_b.reshape(depth, 1, n))


def _hg_in_kernel(xp_ref, xs_ref, mod_ref, nw_ref, w_ref, lbl_ref,
                  q_ref, v_ref, g_ref, lff_ref, kf_ref, lfb_ref, kb_ref):
    i = pl.program_id(0) * (HG_IN_TM // TM)
    cond = _tile_cond(i)
    shift, scale = _mod_row(mod_ref, cond, 0), _mod_row(mod_ref, cond, 1)

    def normed(rows):
        x = jnp.where(i < PROMPT_TILES, xp_ref[rows, :], xs_ref[rows, :])
        return _norm_mod(x, nw_ref[...], shift, scale).astype(BF16)

    def proj(h, c, cols):
        return jnp.dot(h, w_ref[:, c * D_MODEL + cols.start:c * D_MODEL + cols.stop], preferred_element_type=F32)

    l0, l1, l2 = lbl_ref[0], lbl_ref[1], lbl_ref[2]
    m = jnp.maximum(jnp.maximum(l0, l1), l2)
    e0, e1, e2 = jnp.exp(l0 - m), jnp.exp(l1 - m), jnp.exp(l2 - m)
    lb = e0 / (e0 + e1 + e2)
    col_tiles = [slice(n * HG_IN_TN, (n + 1) * HG_IN_TN) for n in range(D_MODEL // HG_IN_TN)]

    def gate_and_plain(h, rows, d):
        lf_ref, k_ref, plain_ref = ((lff_ref, kf_ref, q_ref), (lfb_ref, kb_ref, v_ref))[d]
        for cols in col_tiles:
            lbd = lb[d:d + 1, cols]
            z_gate = proj(h, 2 + d, cols)
            z_plain = proj(h, d, cols)
            f = lbd + (1.0 - lbd) * jax.nn.sigmoid(z_gate)
            lf = jnp.log(f)
            hi = lf.astype(BF16)
            lf_ref[0, rows, cols] = hi
            lf_ref[1, rows, cols] = (lf - hi.astype(F32)).astype(BF16)
            k_ref[rows, cols] = (1.0 - f).astype(BF16)
            plain_ref[rows, cols] = z_plain.astype(BF16)

    sub_tiles = [slice(s * HG_IN_SUB, (s + 1) * HG_IN_SUB) for s in range(HG_IN_TM // HG_IN_SUB)]
    h = normed(sub_tiles[0])
    for s, rows in enumerate(sub_tiles):
        gate_and_plain(h, rows, 0)
        h_next = normed(sub_tiles[s + 1]) if s + 1 < len(sub_tiles) else None
        gate_and_plain(h, rows, 1)
        for cols in col_tiles:
            g_ref[rows, cols] = proj(h, 4, cols).astype(BF16)
        h = h_next


def _hg_in(xp, xs, mod, nw, w_in, lb_logits):
    tile = lambda i: (i, 0)
    fixed2 = lambda i: (0, 0)
    bf = jax.ShapeDtypeStruct((N_TOK, D_MODEL), BF16)
    ff = jax.ShapeDtypeStruct((2, N_TOK, D_MODEL), BF16)
    blk = pl.BlockSpec((HG_IN_TM, D_MODEL), tile)
    split_blk = pl.BlockSpec((2, HG_IN_TM, D_MODEL), lambda i: (0, i, 0))
    prompt_steps = N_PROMPT // HG_IN_TM
    return pl.pallas_call(
        _hg_in_kernel,
        out_shape=(bf, bf, bf, ff, bf, ff, bf),
        grid=(N_TOK // HG_IN_TM,),
        in_specs=[
            pl.BlockSpec((HG_IN_TM, D_MODEL), lambda i: (jnp.minimum(i, prompt_steps - 1), 0)),
            pl.BlockSpec((HG_IN_TM, D_MODEL), lambda i: (jnp.maximum(i - prompt_steps, 0), 0)),
            pl.BlockSpec(mod.shape, fixed2),
            pl.BlockSpec((1, D_MODEL), fixed2),
            pl.BlockSpec(w_in.shape, fixed2, pipeline_mode=pl.Buffered(1)),
            pl.BlockSpec(lb_logits.shape, lambda i: (0, 0, 0)),
        ],
        out_specs=(blk, blk, blk, split_blk, blk, split_blk, blk),
        compiler_params=_params(56, "arbitrary"),
        name="hgrn_in_proj",
    )(xp, xs, mod, nw, w_in, lb_logits)


def _gla_exact(reverse, q_ref, v_ref, lf_ref, k_ref, st_ref, o_ref):
    nt = (((1,), (1,)), ((), ()))
    tn = (((0,), (0,)), ((), ()))
    n_chunks = TM // CHUNK
    row_id = lax.broadcasted_iota(jnp.int32, (CHUNK, HG_DK), 0)
    order = range(CHUNK - 1, -1, -1) if reverse else range(CHUNK)

    def chunk_head(it, carry):
        step, h = it // HG_HEADS, it % HG_HEADS
        ci = (n_chunks - 1 - step) if reverse else step
        rows = pl.ds(pl.multiple_of(ci * CHUNK, CHUNK), CHUNK)
        cols = pl.ds(pl.multiple_of(h * HG_DK, HG_DK), HG_DK)
        q, k, v = q_ref[rows, cols], k_ref[rows, cols], v_ref[rows, cols]
        f = jnp.exp(lf_ref[0, rows, cols].astype(F32) + lf_ref[1, rows, cols].astype(F32))
        st = st_ref[0, h]
        o = jnp.zeros((CHUNK, HG_DK), F32)
        for t in order:
            one = row_id == t
            v_t = jnp.where(one, v, jnp.zeros_like(v))
            st = st * f[t:t + 1, :] + lax.dot_general(v_t, k, tn, preferred_element_type=F32)
            o = jnp.where(one, lax.dot_general(q, st.astype(BF16), nt, preferred_element_type=F32), o)
        st_ref[0, h] = st
        o_ref[rows, cols] = o.astype(o_ref.dtype)
        return carry

    lax.fori_loop(0, n_chunks * HG_HEADS, chunk_head, 0)


class _GlaDirection(NamedTuple):
    reverse: bool
    q: Any
    v: Any
    lf: Any
    k: Any
    o: Any
    st: Any
    sw: Any
    ut: Any
    qd: Any
    kd: Any
    ki: Any
    cd: Any
    b: Any


_NT = (((1,), (1,)), ((), ()))
_TN = (((0,), (0,)), ((), ()))
_HG_COLS = [slice(h * HG_DK, (h + 1) * HG_DK) for h in range(HG_HEADS)]
_N_CHUNKS = TM // CHUNK
_N_PAIRS = _N_CHUNKS // 2


def _gla_prepare(d):
    r = lax.broadcasted_iota(jnp.int32, (TM, TM), 0)
    c = lax.broadcasted_iota(jnp.int32, (TM, TM), 1)
    same = (r // CHUNK) == (c // CHUNK)
    tri = (same & ((c >= r) if d.reverse else (c <= r))).astype(BF16)
    b = jnp.dot(tri, d.lf[0], preferred_element_type=F32) + jnp.dot(tri, d.lf[1], preferred_element_type=F32)
    d.b[...] = b
    edge = 0 if d.reverse else CHUNK - 1
    total = b.reshape(_N_CHUNKS, CHUNK, D_MODEL)[:, edge, :]
    d.cd[...] = jnp.exp(total)
    return jnp.max(-total) <= DECAY_CLAMP


def _gla_pair_rows(d, step):
    pi = (_N_PAIRS - 1 - step) if d.reverse else step
    row0 = pi * 2 * CHUNK
    lo, hi = (pl.ds(row0, CHUNK), 2 * pi), (pl.ds(row0 + CHUNK, CHUNK), 2 * pi + 1)
    return pl.ds(row0, 2 * CHUNK), ((hi, lo) if d.reverse else (lo, hi))


def _chunk_decay(d, chunk):
    return d.cd[pl.ds(chunk, 1), :]


def _gla_decayed_operands(d, step):
    both, ((_, c_first), (_, c_second)) = _gla_pair_rows(d, step)
    b = d.b[both, :]
    ki = d.k[both, :] * jnp.exp(jnp.minimum(-b, DECAY_CLAMP)).astype(BF16)
    d.qd[both, :] = d.q[both, :] * jnp.exp(b).astype(BF16)
    d.ki[both, :] = ki
    c_lo, c_hi = (c_second, c_first) if d.reverse else (c_first, c_second)
    ends = jnp.concatenate([jnp.broadcast_to(_chunk_decay(d, c).astype(BF16), (CHUNK, D_MODEL)) for c in (c_lo, c_hi)],
                           axis=0)
    d.kd[both, :] = ki * ends


def _rows_scaled(x, scale_row, second_half):
    scale = jnp.broadcast_to(scale_row.astype(BF16), (CHUNK, x.shape[1]))
    ones = jnp.ones((CHUNK, x.shape[1]), BF16)
    return x * jnp.concatenate([ones, scale] if second_half else [scale, ones], axis=0)


def _gla_key_value_product(d, step):
    _gla_decayed_operands(d, step)
    both, (_, (_, c_second)) = _gla_pair_rows(d, step)
    decay_second = _chunk_decay(d, c_second)
    for h, cols in enumerate(_HG_COLS):
        keys = _rows_scaled(d.kd[both, cols], decay_second[:, cols], second_half=d.reverse)
        d.ut[h] = lax.dot_general(d.v[both, cols], keys, _TN, preferred_element_type=F32)


def _gla_start(d):
    for h in range(HG_HEADS):
        d.sw[0, h] = d.st[0, h].T.astype(BF16)
    _gla_key_value_product(d, 0)


def _gla_pair(d, step, src, dst):
    sr = lax.broadcasted_iota(jnp.int32, (CHUNK, CHUNK), 0)
    sc = lax.broadcasted_iota(jnp.int32, (CHUNK, CHUNK), 1)
    tr = lax.broadcasted_iota(jnp.int32, (CHUNK, 2 * CHUNK), 0)
    tc = lax.broadcasted_iota(jnp.int32, (CHUNK, 2 * CHUNK), 1)
    if d.reverse:
        keep_first = sc >= sr
        keep_second = (tc >= tr) | (tc >= CHUNK)
    else:
        keep_first = sc <= sr
        keep_second = (tc < CHUNK) | (tc - CHUNK <= tr)
    both, ((first, c_first), (second, c_second)) = _gla_pair_rows(d, step)
    decay_first = _chunk_decay(d, c_first)
    decay_both = decay_first * _chunk_decay(d, c_second)
    for h, cols in enumerate(_HG_COLS):
        new = d.st[src, h] * decay_both[:, cols] + d.ut[h]
        d.st[dst, h] = new
        d.sw[dst, h] = new.T.astype(BF16)
    a_first = [lax.dot_general(d.qd[first, cols], d.ki[first, cols], _NT, preferred_element_type=F32)
               for cols in _HG_COLS]
    a_second = []
    for cols in _HG_COLS:
        lo_keys = (d.ki if d.reverse else d.kd)[pl.ds(both.start, CHUNK), cols]
        hi_keys = (d.kd if d.reverse else d.ki)[pl.ds(both.start + CHUNK, CHUNK), cols]
        keys = jnp.concatenate([lo_keys, hi_keys], axis=0)
        a_second.append(lax.dot_general(d.qd[second, cols], keys, _NT, preferred_element_type=F32))
    inter = [jnp.dot(_rows_scaled(d.qd[both, cols], decay_first[:, cols], second_half=not d.reverse),
                     d.sw[src, h], preferred_element_type=F32) for h, cols in enumerate(_HG_COLS)]
    if step + 1 < _N_PAIRS:
        _gla_key_value_product(d, step + 1)
    first_half, second_half = (slice(CHUNK, None), slice(None, CHUNK)) if d.reverse else \
                              (slice(None, CHUNK), slice(CHUNK, None))
    for h, cols in enumerate(_HG_COLS):
        am = jnp.where(keep_first, a_first[h], 0.0).astype(BF16)
        d.o[first, cols] = (jnp.dot(am, d.v[first, cols], preferred_element_type=F32)
                            + inter[h][first_half]).astype(d.o.dtype)
        am = jnp.where(keep_second, a_second[h], 0.0).astype(BF16)
        d.o[second, cols] = (jnp.dot(am, d.v[both, cols], preferred_element_type=F32)
                             + inter[h][second_half]).astype(d.o.dtype)


def _gla_kernel(qf_ref, vf_ref, lff_ref, kf_ref, qb_ref, vb_ref, lfb_ref, kb_ref, s0_ref,
                of_ref, ob_ref, sout_ref, st_scr, sw_scr, ut_scr, qd_scr, kd_scr, ki_scr, cd_scr, b_scr):
    i = pl.program_id(0)
    is_prompt = i < PROMPT_TILES
    first = jnp.logical_or(is_prompt, (i - PROMPT_TILES) % TILES_PER_DEC_SEQ == 0)
    work = [tuple(scr.at[n] for scr in (st_scr, sw_scr, ut_scr, qd_scr, kd_scr, ki_scr, cd_scr, b_scr))
            for n in range(2)]
    fwd = _GlaDirection(False, qf_ref, vf_ref, lff_ref, kf_ref, of_ref, *work[0])
    bwd = _GlaDirection(True, qb_ref, vb_ref, lfb_ref, kb_ref, ob_ref, *work[1])

    @pl.when(is_prompt)
    def _():
        for d in (fwd, bwd):
            d.st[0] = jnp.zeros((HG_HEADS, HG_DK, HG_DK), F32)

    @pl.when(jnp.logical_and(first, jnp.logical_not(is_prompt)))
    def _():
        for n, d in enumerate((fwd, bwd)):
            for h in range(HG_HEADS):
                d.st[0, h] = s0_ref[0, 0, n, h].T

    exact_scores = jnp.logical_and(_gla_prepare(fwd), _gla_prepare(bwd))

    @pl.when(exact_scores)
    def _():
        _gla_start(fwd)
        _gla_start(bwd)

        for step in range(_N_PAIRS):
            for d in (fwd, bwd):
                _gla_pair(d, step, step % 2, 1 - step % 2)

    @pl.when(jnp.logical_not(exact_scores))
    def _():
        for d in (fwd, bwd):
            _gla_exact(d.reverse, d.q, d.v, d.lf, d.k, d.st, d.o)

    @pl.when(is_prompt)
    def _():
        for n, d in enumerate((fwd, bwd)):
            for h in range(HG_HEADS):
                sout_ref[0, 0, n, h] = d.st[0, h].T


def _gla(q, v, lff, kf, lfb, kb, state_hgrn):
    def fwd_tile(i):
        return (i, 0)

    def bwd_tile(i):
        j = (i - PROMPT_TILES) % TILES_PER_DEC_SEQ
        return (jnp.where(i < PROMPT_TILES, i, i - j + (TILES_PER_DEC_SEQ - 1 - j)), 0)

    def s0_idx(i):
        return (jnp.maximum(i - PROMPT_TILES, 0) // TILES_PER_DEC_SEQ, 0, 0, 0, 0, 0)

    def sout_idx(i):
        return (jnp.minimum(i, PROMPT_TILES - 1), 0, 0, 0, 0, 0)

    f_blk = pl.BlockSpec((TM, D_MODEL), fwd_tile)
    b_blk = pl.BlockSpec((TM, D_MODEL), bwd_tile)
    f_split = pl.BlockSpec((2, TM, D_MODEL), lambda i: (0,) + fwd_tile(i))
    b_split = pl.BlockSpec((2, TM, D_MODEL), lambda i: (0,) + bwd_tile(i))
    st_blk = (1, 1, 2, HG_HEADS, HG_DK, HG_DK)
    return pl.pallas_call(
        _gla_kernel,
        out_shape=(
            jax.ShapeDtypeStruct((N_TOK, D_MODEL), BF16),
            jax.ShapeDtypeStruct((N_TOK, D_MODEL), BF16),
            jax.ShapeDtypeStruct((BATCH,) + st_blk[1:], F32),
        ),
        grid=(N_TILES,),
        in_specs=[f_blk, f_blk, f_split, f_blk, b_blk, b_blk, b_split, b_blk, pl.BlockSpec(st_blk, s0_idx)],
        out_specs=(f_blk, b_blk, pl.BlockSpec(st_blk, sout_idx)),
        scratch_shapes=[
            pltpu.VMEM((2, 2, HG_HEADS, HG_DK, HG_DK), F32),
            pltpu.VMEM((2, 2, HG_HEADS, HG_DK, HG_DK), BF16),
            pltpu.VMEM((2, HG_HEADS, HG_DK, HG_DK), F32),
            pltpu.VMEM((2, TM, D_MODEL), BF16),
            pltpu.VMEM((2, TM, D_MODEL), BF16),
            pltpu.VMEM((2, TM, D_MODEL), BF16),
            pltpu.VMEM((2, TM // CHUNK, D_MODEL), F32),
            pltpu.VMEM((2, TM, D_MODEL), F32),
        ],
        compiler_params=_params(48, "arbitrary"),
        name="hgrn_recurrence",
    )(q, v, lff, kf, q, v, lfb, kb, state_hgrn)


def _route(logits):
    lane = lax.broadcasted_iota(jnp.int32, logits.shape, 1)
    neg = jnp.float32(-jnp.inf)
    big = jnp.int32(ROUTER_LANES)

    def first_max(x):
        m = jnp.max(x, axis=-1, keepdims=True)
        return m, jnp.min(jnp.where(x == m, lane, big), axis=-1, keepdims=True)

    is_group = (lane >= N_EXPERTS) & (lane < N_EXPERTS + N_GROUPS)
    gl = jnp.where(is_group, logits, neg)
    gmax, g_lane = first_max(gl)
    g_sel = g_lane - N_EXPERTS
    gsum = jnp.sum(jnp.exp(gl - gmax), axis=-1, keepdims=True)
    p_g = 1.0 / gsum
    in_sel = (lane < N_EXPERTS) & ((lane // EXPERTS_PER_GROUP) == g_sel)
    m1, i1 = first_max(jnp.where(in_sel, logits, neg))
    m2, i2 = first_max(jnp.where(in_sel & (lane != i1), logits, neg))
    e2 = jnp.exp(m2 - m1)
    return i1, i2, p_g / (1.0 + e2), p_g * e2 / (1.0 + e2)


def _pack_rows(x):
    q = PACK_WORDS
    bits = pltpu.bitcast(x.astype(BF16).astype(F32), jnp.uint32)
    return [(bits[:, (2 + h) * q:(3 + h) * q] & jnp.uint32(0xFFFF0000)) | (bits[:, h * q:(h + 1) * q] >> 16)
            for h in range(2)]


def _unpack_rows(half0, half1):
    lo = lambda w: pltpu.bitcast(w << 16, F32).astype(BF16)
    hi = lambda w: pltpu.bitcast(w & jnp.uint32(0xFFFF0000), F32).astype(BF16)
    return [lo(half0), lo(half1), hi(half0), hi(half1)]


def _mix_stream_rows(prompt_ref, sample_ref, rows):
    return jnp.where(pl.program_id(0) < N_PROMPT // MIX_TM, prompt_ref[rows, :], sample_ref[rows, :])


def _mixer_tail(mix_of, x_of, mod_ref, nw2_ref, wo_ref, wr_ref, br_ref, x1_ref, h2p_ref, lg_ref):
    cond = _tile_cond(pl.program_id(0) * (MIX_TM // TM))
    gate, shift, scale = (_mod_row(mod_ref, cond, which) for which in (2, 3, 4))
    project = lambda rows: jnp.dot(mix_of(rows), wo_ref[...], preferred_element_type=F32)

    def tail(rows, out):
        x1 = x_of(rows) + gate * out
        x1_ref[rows, :] = x1
        h2 = _norm_mod(x1, nw2_ref[...], shift, scale)
        h2p_ref[rows, :] = jnp.concatenate(_pack_rows(h2), axis=1)
        h_hi = h2.astype(BF16)
        h_lo = (h2 - h_hi.astype(F32)).astype(BF16)
        lg_ref[rows, :] = (jnp.dot(h_hi, wr_ref[0], preferred_element_type=F32)
                           + (jnp.dot(h_hi, wr_ref[1], preferred_element_type=F32)
                              + jnp.dot(h_lo, wr_ref[0], preferred_element_type=F32))) + br_ref[...]

    sub_tiles = [slice(s * TM, (s + 1) * TM) for s in range(MIX_TM // TM)]
    out = project(sub_tiles[0])
    for s, rows in enumerate(sub_tiles):
        out_next = project(sub_tiles[s + 1]) if s + 1 < len(sub_tiles) else None
        tail(rows, out)
        out = out_next


def _route_kernel(lg_ref, rt_ref, rw_ref, cnt_ref, carry_scr):
    @pl.when(pl.program_id(0) == 0)
    def _():
        carry_scr[...] = jnp.zeros_like(carry_scr)

    logits = lg_ref[...]
    i1, i2, w1, w2 = _route(logits)
    lane = lax.broadcasted_iota(jnp.int32, logits.shape, 1)
    chosen = ((lane == i1) | (lane == i2)).astype(BF16)
    r = lax.broadcasted_iota(jnp.int32, (ROUTE_TM, ROUTE_TM), 0)
    c = lax.broadcasted_iota(jnp.int32, (ROUTE_TM, ROUTE_TM), 1)
    before = jnp.dot((c < r).astype(BF16), chosen, preferred_element_type=F32) + carry_scr[...]
    r1 = jnp.sum(jnp.where(lane == i1, before, 0.0), axis=-1, keepdims=True).astype(jnp.int32)
    r2 = jnp.sum(jnp.where(lane == i2, before, 0.0), axis=-1, keepdims=True).astype(jnp.int32)
    total = carry_scr[...] + jnp.sum(chosen.astype(F32), axis=0, keepdims=True)
    carry_scr[...] = total
    tiles = jnp.ceil(total * (1.0 / MOE_TILE))
    e_from = lax.broadcasted_iota(jnp.int32, (ROUTER_LANES, ROUTER_LANES), 0)
    e_to = lax.broadcasted_iota(jnp.int32, (ROUTER_LANES, ROUTER_LANES), 1)
    tiles_before = jnp.dot(jnp.broadcast_to(tiles, (SUBLANES, ROUTER_LANES)).astype(BF16),
                           (e_from < e_to).astype(BF16), preferred_element_type=F32)
    cnt_ref[...] = jnp.concatenate([total, tiles_before[:1] * MOE_TILE,
                                    jnp.zeros((ROUTE_ROWS - 2, ROUTER_LANES), F32)], axis=0)
    by_token = jnp.where(lane == 0, i1, jnp.where(lane == 1, i2, jnp.where(lane == 2, r1, r2)))
    rt_ref[...] = by_token.T[:ROUTE_ROWS, :]
    rw_ref[...] = jnp.where(lane == 0, w1, w2)


def _route_tokens(logits):
    blk = pl.BlockSpec((ROUTE_TM, ROUTER_LANES), lambda i: (i, 0))
    return pl.pallas_call(
        _route_kernel,
        out_shape=(
            jax.ShapeDtypeStruct((ROUTE_ROWS, N_TOK), jnp.int32),
            jax.ShapeDtypeStruct((N_TOK, ROUTER_LANES), F32),
            jax.ShapeDtypeStruct((ROUTE_ROWS, ROUTER_LANES), F32),
        ),
        grid=(N_TOK // ROUTE_TM,),
        in_specs=[blk],
        out_specs=(pl.BlockSpec((ROUTE_ROWS, ROUTE_TM), lambda i: (0, i)), blk,
                   pl.BlockSpec((ROUTE_ROWS, ROUTER_LANES), lambda i: (0, 0))),
        scratch_shapes=[pltpu.VMEM((1, ROUTER_LANES), F32)],
        compiler_params=_params(32, "arbitrary"),
        name="moe_route",
    )(logits)


def _hg_out_kernel(of_ref, ob_ref, g_ref, on_ref, xp_ref, xs_ref, *rest):
    def mix_of(rows):
        o = of_ref[rows, :].astype(F32) + ob_ref[rows, :].astype(F32)
        parts = []
        for h in range(HG_HEADS):
            oh = o[:, h * HG_DK:(h + 1) * HG_DK]
            parts.append(oh * lax.rsqrt(jnp.mean(oh * oh, axis=-1, keepdims=True) + EPS) * on_ref[...])
        y = jnp.concatenate(parts, axis=1) * _silu(g_ref[rows, :].astype(F32))
        return y.astype(BF16)

    _mixer_tail(mix_of, functools.partial(_mix_stream_rows, xp_ref, xs_ref), *rest)


def _at_out_kernel(ac_ref, al_ref, x_ref, *rest):
    _mixer_tail(functools.partial(_mix_stream_rows, ac_ref, al_ref), lambda rows: x_ref[rows, :], *rest)


def _mixer_out(kernel_fn, name, mix_inputs, mix_specs, mod, nw2, w_out, w_router, b_router):
    tile = lambda i: (i, 0)
    fixed2 = lambda i: (0, 0)
    blk = pl.BlockSpec((MIX_TM, D_MODEL), tile)
    lanes_blk = pl.BlockSpec((MIX_TM, ROUTER_LANES), tile)
    return pl.pallas_call(
        kernel_fn,
        out_shape=(
            jax.ShapeDtypeStruct((N_TOK, D_MODEL), F32),
            jax.ShapeDtypeStruct((N_TOK, ROW_WORDS), jnp.uint32),
            jax.ShapeDtypeStruct((N_TOK, ROUTER_LANES), F32),
        ),
        grid=(N_TOK // MIX_TM,),
        in_specs=list(mix_specs) + [
            pl.BlockSpec(mod.shape, fixed2),
            pl.BlockSpec((1, D_MODEL), fixed2),
            pl.BlockSpec(w_out.shape, fixed2),
            pl.BlockSpec(w_router.shape, lambda i: (0, 0, 0)),
            pl.BlockSpec((1, ROUTER_LANES), fixed2),
        ],
        out_specs=(blk, pl.BlockSpec((MIX_TM, ROW_WORDS), tile), lanes_blk),
        compiler_params=_params(40, "arbitrary"),
        name=name,
    )(*mix_inputs, mod, nw2, w_out, w_router, b_router)


def _moe_plan(route_t, counts):
    cnt = counts[0, :N_EXPERTS].astype(jnp.int32)
    offs = counts[1, :N_EXPERTS].astype(jnp.int32)
    ends = offs + ((cnt + MOE_TILE - 1) // MOE_TILE) * MOE_TILE
    experts = route_t[0:2]
    pos = jnp.sum(jnp.where(experts[None, :, :] == jnp.arange(N_EXPERTS)[:, None, None], offs[:, None, None], 0),
                  axis=0) + route_t[2:4]
    tile_start = jnp.arange(MOE_ROWS // MOE_TILE, dtype=jnp.int32) * MOE_TILE
    tile_expert = jnp.minimum(jnp.sum(ends[None, :] <= tile_start[:, None], axis=1), N_EXPERTS - 1).astype(jnp.int32)
    of_tile = tile_expert[:, None] == jnp.arange(N_EXPERTS)[None, :]
    tile_offs = jnp.sum(jnp.where(of_tile, offs[None, :], 0), axis=1)
    tile_cnt = jnp.sum(jnp.where(of_tile, cnt[None, :], 0), axis=1)
    tile_rows = jnp.clip(tile_offs + tile_cnt - tile_start, 0, MOE_TILE).astype(jnp.int32)
    active = tile_start < ends[-1]
    tile_opens = jnp.where(active & (tile_start == tile_offs), tile_expert, -1).astype(jnp.int32)
    tile_opens = jnp.concatenate([tile_opens, jnp.full((MOE_STEP_TILES,), -1, jnp.int32)])
    n_active = (ends[-1] // MOE_TILE).astype(jnp.int32).reshape(1)
    slot = (jnp.cumsum((cnt > 0).astype(jnp.int32)) - 1) % MOE_WEIGHT_SLOTS
    tile_slot = jnp.sum(jnp.where(of_tile, slot[None, :], 0), axis=1)
    last_slot = jnp.sum(jnp.where(tile_start == ends[-1] - MOE_TILE, tile_slot, 0))
    tile_slot = jnp.where(active, tile_slot, last_slot).astype(jnp.int32)
    return pos.astype(jnp.int32), (tile_rows, tile_slot, tile_opens, n_active)


SC_WINDOW = 64


def _sc_mesh():
    return plsc.VectorSubcoreMesh(core_axis_name="c", subcore_axis_name="s")


def _sc_scatter_rows(x, idx_a, idx_b, n_out_rows):
    n = x.shape[0]

    @functools.partial(pl.kernel, out_type=jax.ShapeDtypeStruct((n_out_rows, ROW_WORDS), x.dtype), mesh=_sc_mesh(),
                       scratch_types=[pltpu.SemaphoreType.DMA, pltpu.SemaphoreType.DMA])
    def scatter(x_hbm, ia_hbm, ib_hbm, o_hbm, sem_a, sem_b):
        def body(x_vmem, ia_vmem, ib_vmem):
            copy_a = pltpu.async_copy(x_vmem, o_hbm.at[ia_vmem.at[0]], sem_a)
            copy_b = pltpu.async_copy(x_vmem, o_hbm.at[ib_vmem.at[0]], sem_b)
            copy_a.wait()
            copy_b.wait()

        idx_spec = pl.BlockSpec((1, SC_WINDOW), index_map=lambda i: (i, 0))
        pltpu.emit_pipeline(
            body, grid=(n // SC_WINDOW,),
            in_specs=[pl.BlockSpec((SC_WINDOW, ROW_WORDS), index_map=lambda i: (i, 0)), idx_spec, idx_spec],
            out_specs=[],
            core_axis_name=("c", "s"), dimension_semantics=(pltpu.PARALLEL,),
        )(x_hbm, ia_hbm, ib_hbm)

    return scatter(x, idx_a, idx_b)


def _sc_gather_rows(table, idx):
    n = idx.size

    @functools.partial(pl.kernel, out_type=jax.ShapeDtypeStruct((n, ROW_WORDS), table.dtype), mesh=_sc_mesh())
    def gather(t_hbm, i_hbm, o_hbm):
        def body(i_vmem, o_vmem):
            pltpu.sync_copy(t_hbm.at[i_vmem.at[0]], o_vmem)

        pltpu.emit_pipeline(
            body, grid=(n // SC_WINDOW,),
            in_specs=[pl.BlockSpec((1, SC_WINDOW), index_map=lambda i: (i, 0))],
            out_specs=[pl.BlockSpec((SC_WINDOW, ROW_WORDS), index_map=lambda i: (i, 0))],
            core_axis_name=("c", "s"), dimension_semantics=(pltpu.PARALLEL,),
        )(i_hbm, o_hbm)

    return gather(table, idx)


def _ffn_kernel(tr_ref, ts_ref, to_ref, na_ref, xs_ref, w1_hbm, w3_hbm, w2_hbm, ys_ref,
                w1_buf, w3_buf, w2_buf, sem, *, layer):
    def weight_copies(t):
        pairs = ((w1_hbm, w1_buf), (w3_hbm, w3_buf), (w2_hbm, w2_buf))
        return [pltpu.make_async_copy(hbm.at[layer, to_ref[t]], buf.at[ts_ref[t]], sem.at[ts_ref[t], j])
                for j, (hbm, buf) in enumerate(pairs)]

    def for_experts_opened_from(first_tile, action):
        for j in range(MOE_STEP_TILES):
            t = first_tile + j

            @pl.when(to_ref[t] >= 0)
            def _():
                for copy in weight_copies(t):
                    action(copy)

    def tile(t, rows):
        slot = ts_ref[t]
        row = lax.broadcasted_iota(jnp.int32, (MOE_TILE, PACK_WORDS), 0)
        live = row < tr_ref[t]
        halves = [jnp.where(live, xs_ref[rows, h * PACK_WORDS:(h + 1) * PACK_WORDS],
                            jnp.zeros((MOE_TILE, PACK_WORDS), jnp.uint32)) for h in range(2)]
        chunks = _unpack_rows(*halves)

        def up(w_buf):
            acc = None
            for k, chunk in enumerate(chunks):
                w = w_buf[slot, k * PACK_WORDS:(k + 1) * PACK_WORDS, :].astype(BF16)
                part = jnp.dot(chunk, w, preferred_element_type=F32)
                acc = part if acc is None else acc + part
            return acc

        hid = (_silu(up(w1_buf)) * up(w3_buf)).astype(BF16)
        y = jnp.dot(hid, w2_buf[slot].astype(BF16), preferred_element_type=F32)
        ys_ref[rows, :] = jnp.concatenate(_pack_rows(y), axis=1)

    step = pl.program_id(0)
    first_tile = step * MOE_STEP_TILES

    @pl.when(first_tile < na_ref[0])
    def _():
        @pl.when(step == 0)
        def _():
            for_experts_opened_from(first_tile, lambda copy: copy.start())

        for_experts_opened_from(first_tile + MOE_STEP_TILES, lambda copy: copy.start())
        for_experts_opened_from(first_tile, lambda copy: copy.wait())
        for j in range(MOE_STEP_TILES):
            tile(first_tile + j, slice(j * MOE_TILE, (j + 1) * MOE_TILE))


def _ffn(xs, tables, layer, w1, w3, w2):
    step_rows = MOE_STEP_TILES * MOE_TILE
    row_tile = lambda s, tr, ts, to, na: (jnp.minimum(s, (na[0] - 1) // MOE_STEP_TILES), 0)
    hbm = pl.BlockSpec(memory_space=pl.ANY)
    return pl.pallas_call(
        functools.partial(_ffn_kernel, layer=layer),
        out_shape=jax.ShapeDtypeStruct((MOE_ROWS, ROW_WORDS), jnp.uint32),
        grid_spec=pltpu.PrefetchScalarGridSpec(
            num_scalar_prefetch=len(tables),
            grid=(MOE_ROWS // step_rows,),
            in_specs=[pl.BlockSpec((step_rows, ROW_WORDS), row_tile), hbm, hbm, hbm],
            out_specs=pl.BlockSpec((step_rows, ROW_WORDS), row_tile),
            scratch_shapes=[
                pltpu.VMEM((MOE_WEIGHT_SLOTS, D_MODEL, D_EXPERT), F32),
                pltpu.VMEM((MOE_WEIGHT_SLOTS, D_MODEL, D_EXPERT), F32),
                pltpu.VMEM((MOE_WEIGHT_SLOTS, D_EXPERT, D_MODEL), F32),
                pltpu.SemaphoreType.DMA((MOE_WEIGHT_SLOTS, 3)),
            ],
        ),
        compiler_params=_params(48, "arbitrary"),
        name="moe_experts",
    )(*tables, xs, w1, w3, w2)


def _moe_res(x_ref, y_ref, rw_ref, mod_ref, cond, rows=slice(None)):
    rw = rw_ref[rows, :]
    wa = rw[:, 0:1]
    wb = rw[:, 1:2]
    ya = _unpack_rows(y_ref[0, rows, :PACK_WORDS], y_ref[0, rows, PACK_WORDS:])
    yb = _unpack_rows(y_ref[1, rows, :PACK_WORDS], y_ref[1, rows, PACK_WORDS:])
    y = jnp.concatenate([wa * a.astype(F32) + wb * b.astype(F32) for a, b in zip(ya, yb)], axis=1)
    return x_ref[rows, :] + _mod_row(mod_ref, cond, 5) * y


def _moe_res_final_kernel(x_ref, y_ref, rw_ref, mod_ref, fn_ref, op_ref, os_ref):
    x = _moe_res(x_ref, y_ref, rw_ref, mod_ref, _tile_cond(pl.program_id(0)))
    y = x * lax.rsqrt(jnp.mean(x * x, axis=-1, keepdims=True) + EPS) * fn_ref[...]
    is_prompt = pl.program_id(0) < PROMPT_TILES

    @pl.when(is_prompt)
    def _():
        op_ref[...] = y

    @pl.when(jnp.logical_not(is_prompt))
    def _():
        os_ref[...] = y


def _moe_combine(x1, y_pairs, route_w, mod, final_norm):
    tile = lambda i: (i, 0)
    fixed2 = lambda i: (0, 0)
    return pl.pallas_call(
        _moe_res_final_kernel,
        out_shape=(jax.ShapeDtypeStruct((N_PROMPT, D_MODEL), F32),
                   jax.ShapeDtypeStruct((N_TOK - N_PROMPT, D_MODEL), F32)),
        grid=(N_TILES,),
        in_specs=[pl.BlockSpec((TM, D_MODEL), tile),
                  pl.BlockSpec((2, TM, ROW_WORDS), lambda i: (0, i, 0)),
                  pl.BlockSpec((TM, ROUTER_LANES), tile),
                  pl.BlockSpec(mod.shape, fixed2),
                  pl.BlockSpec((1, D_MODEL), fixed2)],
        out_specs=(pl.BlockSpec((TM, D_MODEL), _prompt_tile), pl.BlockSpec((TM, D_MODEL), _sample_tile)),
        compiler_params=_params(32, "arbitrary"),
        name="moe_combine",
    )(x1, y_pairs, route_w, mod, final_norm)


def _moe_layer(x1, h2p, logits, layer, w1, w3, w2):
    route_t, route_w, counts = _route_tokens(logits)
    pos, tables = _moe_plan(route_t, counts)
    windows = pos.reshape(2, N_TOK // SC_WINDOW, SC_WINDOW)
    xs = _sc_scatter_rows(h2p, windows[0], windows[1], MOE_ROWS)
    ys = _ffn(xs, tables, layer, w1, w3, w2)
    y_pairs = _sc_gather_rows(ys, windows.reshape(2 * N_TOK // SC_WINDOW, SC_WINDOW))
    return x1, y_pairs.reshape(2, N_TOK, ROW_WORDS), route_w


def _swap_rotary_halves(x):
    n = x.shape[-1]
    lane = lax.broadcasted_iota(jnp.int32, x.shape, 1)
    quarter = ROPE_HALF // 2
    return jnp.where((lane % ROPE_HALF) < quarter, pltpu.roll(x, n - quarter, 1), pltpu.roll(x, quarter, 1))


def _at_in_kernel(x1_ref, y_ref, rw_ref, mod_prev_ref, mod_ref, nw_ref, w_ref, cos_ref, sin_ref,
                  x_ref, q_ref, kt_ref, v_ref, kc_ref, vc_ref):
    step = pl.program_id(0)
    cond = _tile_cond(step * (AT_IN_TM // TM))
    shift, scale = _mod_row(mod_ref, cond, 0), _mod_row(mod_ref, cond, 1)
    nq = ATT_Q_HEADS * ATT_HEAD_DIM
    nk = ATT_KV_HEADS * ATT_HEAD_DIM

    def normed(rows):
        x = _moe_res(x1_ref, y_ref, rw_ref, mod_prev_ref, cond, rows)
        x_ref[rows, :] = x
        return _norm_mod(x, nw_ref[...], shift, scale).astype(BF16)

    def rope(x, rows):
        reps = x.shape[-1] // LANES
        cos = jnp.concatenate([cos_ref[rows, :]] * reps, axis=1)
        sin = jnp.concatenate([sin_ref[rows, :]] * reps, axis=1)
        return x * cos + _swap_rotary_halves(x) * sin

    def project_q(h):
        return jnp.dot(h, w_ref[:, :nq], preferred_element_type=F32)

    def project_kv(h):
        return (jnp.dot(h, w_ref[:, nq:nq + nk], preferred_element_type=F32),
                jnp.dot(h, w_ref[:, nq + nk:], preferred_element_type=F32))

    def tail(rows, zq, zk, v):
        q_ref[rows, :] = (rope(zq, rows) * ATT_HEAD_DIM ** -0.5).astype(BF16)
        kt = rope(zk, rows).T
        kt_ref[:, rows] = kt.astype(BF16)
        v_ref[rows, :] = v.astype(BF16)
        return kt, v.T

    sub_tiles = [slice(s * TM, (s + 1) * TM) for s in range(AT_IN_TM // TM)]
    cache_rows = []
    h = normed(sub_tiles[0])
    zq = project_q(h)
    for s, rows in enumerate(sub_tiles):
        last = s + 1 == len(sub_tiles)
        h_next = None if last else normed(sub_tiles[s + 1])
        zk, v = project_kv(h)
        zq_next = None if last else project_q(h_next)
        cache_rows.append(tail(rows, zq, zk, v))
        h, zq = h_next, zq_next

    @pl.when(step < N_PROMPT // AT_IN_TM)
    def _():
        for rows, (kt, vt) in zip(sub_tiles, cache_rows):
            kc_ref[rows, :] = kt
            vc_ref[rows, :] = vt


def _rope_tables():
    f32 = np.float32
    pos = np.arange(DEC_SEQ)
    t_row = (pos // GRID_W).astype(f32)
    t_col = (pos % GRID_W).astype(f32)
    inv = f32(ROPE_BASE) ** (-np.arange(0, ROPE_HALF, 2, dtype=f32) / f32(ROPE_HALF))
    j = np.arange(LANES) % ATT_HEAD_DIM
    freq = inv[(j % ROPE_HALF) % (ROPE_HALF // 2)]
    ang = (np.where((j < ROPE_HALF)[None, :], t_row[:, None], t_col[:, None]) * freq[None, :]).astype(f32)
    sign = np.where((j % ROPE_HALF) < ROPE_HALF // 2, -1.0, 1.0).astype(f32)
    cos = np.concatenate([np.ones((AT_IN_TM, LANES), f32), np.cos(ang)], axis=0)
    sin = np.concatenate([np.zeros((AT_IN_TM, LANES), f32), np.sin(ang) * sign[None, :]], axis=0)
    return jnp.asarray(cos, F32), jnp.asarray(sin, F32)


def _at_in(x1, y_pairs, route_w, mod_prev, mod, nw, w_in, cos, sin):
    tile = lambda i: (i, 0)
    fixed2 = lambda i: (0, 0)
    prompt_steps = N_PROMPT // AT_IN_TM
    rope_tile = lambda i: (jnp.where(i < prompt_steps, 0, 1 + (i - prompt_steps) % (DEC_SEQ // AT_IN_TM)), 0)
    prompt_tile = lambda i: (jnp.minimum(i, prompt_steps - 1), 0)
    nk = ATT_KV_HEADS * ATT_HEAD_DIM
    return pl.pallas_call(
        _at_in_kernel,
        out_shape=(
            jax.ShapeDtypeStruct((N_TOK, D_MODEL), F32),
            jax.ShapeDtypeStruct((N_TOK, D_MODEL), BF16),
            jax.ShapeDtypeStruct((nk, N_TOK), BF16),
            jax.ShapeDtypeStruct((N_TOK, nk), BF16),
            jax.ShapeDtypeStruct((N_PROMPT, nk), F32),
            jax.ShapeDtypeStruct((N_PROMPT, nk), F32),
        ),
        grid=(N_TOK // AT_IN_TM,),
        in_specs=[
            pl.BlockSpec((AT_IN_TM, D_MODEL), tile),
            pl.BlockSpec((2, AT_IN_TM, ROW_WORDS), lambda i: (0, i, 0)),
            pl.BlockSpec((AT_IN_TM, ROUTER_LANES), tile),
            pl.BlockSpec(mod_prev.shape, fixed2),
            pl.BlockSpec(mod.shape, fixed2),
            pl.BlockSpec((1, D_MODEL), fixed2),
            pl.BlockSpec(w_in.shape, fixed2),
            pl.BlockSpec((AT_IN_TM, LANES), rope_tile),
            pl.BlockSpec((AT_IN_TM, LANES), rope_tile),
        ],
        out_specs=(pl.BlockSpec((AT_IN_TM, D_MODEL), tile), pl.BlockSpec((AT_IN_TM, D_MODEL), tile),
                   pl.BlockSpec((nk, AT_IN_TM), lambda i: (0, i)),
                   pl.BlockSpec((AT_IN_TM, nk), tile),
                   pl.BlockSpec((AT_IN_TM, nk), prompt_tile), pl.BlockSpec((AT_IN_TM, nk), prompt_tile)),
        compiler_params=_params(40, "arbitrary"),
        name="attn_in_proj",
    )(x1, y_pairs, route_w, mod_prev, mod, nw, w_in, cos, sin)


def _attend(q, kt_all, v_all, mask, sink_ref, o_ref):
    nq = q.shape[0]
    group_lanes = ATT_GROUP * ATT_HEAD_DIM
    lane = lax.broadcasted_iota(jnp.int32, (nq, group_lanes), 1)
    mine = [(lane // ATT_HEAD_DIM) == g for g in range(ATT_GROUP)]
    row_head = lax.broadcasted_iota(jnp.int32, (ATT_GROUP * nq, 1), 0) // nq
    if mask is not None:
        mask = jnp.concatenate([mask] * ATT_GROUP, axis=0)
    for hk in range(ATT_KV_HEADS):
        vh = v_all[:, hk * ATT_HEAD_DIM:(hk + 1) * ATT_HEAD_DIM]
        kt = jnp.concatenate([kt_all[hk * ATT_HEAD_DIM:(hk + 1) * ATT_HEAD_DIM, :]] * ATT_GROUP, axis=0)
        vt = jnp.concatenate([vh] * ATT_GROUP, axis=1)
        qg = q[:, hk * group_lanes:(hk + 1) * group_lanes]
        q_stack = jnp.concatenate([jnp.where(mine[g], qg, jnp.zeros_like(qg)) for g in range(ATT_GROUP)], axis=0)
        s = jnp.dot(q_stack, kt, preferred_element_type=F32)
        if mask is not None:
            s = jnp.where(mask, s, -jnp.inf)
        sink = jnp.zeros((ATT_GROUP * nq, 1), F32)
        for g in range(ATT_GROUP):
            sink = jnp.where(row_head == g, sink_ref[hk * ATT_GROUP + g], sink)
        m = jnp.maximum(jnp.max(s, axis=-1, keepdims=True), sink)
        p = jnp.exp(s - m)
        denom = jnp.sum(p, axis=-1, keepdims=True) + jnp.exp(sink - m)
        o = jnp.dot(p.astype(BF16), vt, preferred_element_type=F32) / denom
        acc = jnp.where(mine[0], o[:nq], 0.0)
        for g in range(1, ATT_GROUP):
            acc = acc + jnp.where(mine[g], o[g * nq:(g + 1) * nq], 0.0)
        o_ref[:, hk * group_lanes:(hk + 1) * group_lanes] = acc.astype(BF16)


def _ctx_attn_kernel(sink_ref, q_ref, k_ref, v_ref, o_ref):
    _attend(q_ref[...], k_ref[...], v_ref[...], None, sink_ref, o_ref)


def _lat_attn_kernel(sink_ref, q_ref, kp_ref, kc_ref, kn_ref, vp_ref, vc_ref, vn_ref, ck_ref, cv_ref, o_ref):
    jb = pl.program_id(1)
    kt_all = jnp.concatenate([kp_ref[...], kc_ref[...], kn_ref[...], ck_ref[0].astype(BF16)], axis=1)
    v_all = jnp.concatenate([vp_ref[...], vc_ref[...], vn_ref[...], cv_ref[0].astype(BF16)], axis=0)
    nkeys = 3 * BLOCK + PAST_LEN
    qi = lax.broadcasted_iota(jnp.int32, (BLOCK, nkeys), 0)
    kj = lax.broadcasted_iota(jnp.int32, (BLOCK, nkeys), 1)
    qpos = jb * BLOCK + qi
    kpos = (jb - 1) * BLOCK + kj
    local_ok = (jnp.abs(qpos - kpos) <= WINDOW) & (kpos >= 0) & (kpos < DEC_SEQ)
    mask = (kj >= 3 * BLOCK) | local_ok
    _attend(q_ref[...], kt_all, v_all, mask, sink_ref, o_ref)


def _attention(q, kt, v, cache_kt, cache_v, sink):
    nk = ATT_KV_HEADS * ATT_HEAD_DIM
    smem = pl.BlockSpec(memory_space=pltpu.SMEM)
    ctx = pl.pallas_call(
        _ctx_attn_kernel,
        out_shape=jax.ShapeDtypeStruct((N_PROMPT, D_MODEL), BF16),
        grid=(BATCH,),
        in_specs=[
            smem,
            pl.BlockSpec((SEQ, D_MODEL), lambda b: (b, 0)),
            pl.BlockSpec((nk, SEQ), lambda b: (0, b)),
            pl.BlockSpec((SEQ, nk), lambda b: (b, 0)),
        ],
        out_specs=pl.BlockSpec((SEQ, D_MODEL), lambda b: (b, 0)),
        compiler_params=_params(40, "arbitrary"),
        name="context_attention",
    )(sink, q, kt, v)

    nb = DEC_SEQ // BLOCK
    base = N_PROMPT // BLOCK
    cur = lambda b, j: (base + b * nb + j, 0)
    prev = lambda b, j: (base + b * nb + jnp.maximum(j - 1, 0), 0)
    nxt = lambda b, j: (base + b * nb + jnp.minimum(j + 1, nb - 1), 0)
    kv_blk = lambda f: pl.BlockSpec((BLOCK, nk), f)
    kt_blk = lambda f: pl.BlockSpec((nk, BLOCK), lambda b, j: f(b, j)[::-1])
    cache_blk = pl.BlockSpec((1, PAST_LEN, nk), lambda b, j: (b, 0, 0))
    cache_kt_blk = pl.BlockSpec((1, nk, PAST_LEN), lambda b, j: (b, 0, 0))
    lat = pl.pallas_call(
        _lat_attn_kernel,
        out_shape=jax.ShapeDtypeStruct((DEC_BATCH * DEC_SEQ, D_MODEL), BF16),
        grid=(DEC_BATCH, nb),
        in_specs=[
            smem,
            pl.BlockSpec((BLOCK, D_MODEL), cur),
            kt_blk(prev), kt_blk(cur), kt_blk(nxt),
            kv_blk(prev), kv_blk(cur), kv_blk(nxt),
            cache_kt_blk, cache_blk,
        ],
        out_specs=pl.BlockSpec((BLOCK, D_MODEL), lambda b, j: (b * nb + j, 0)),
        compiler_params=_params(40, "arbitrary", "arbitrary"),
        name="latent_attention",
    )(sink, q, kt, kt, kt, v, v, v, cache_kt, cache_v)
    return ctx, lat


def kernel(x_prompt, x_sample, state_hgrn, cache_k, cache_v, c, c_ctx, ada_w, ada_b, norm_w, hg_w_in,
           hg_lb_logits, hg_onorm, hg_w_out, at_w_in, at_sink, at_w_out, moe_w_group, moe_b_group,
           moe_w_expert, moe_b_expert, moe_w1, moe_w3, moe_w2, final_norm):
    xp = x_prompt.reshape(N_PROMPT, D_MODEL)
    xs = x_sample.reshape(N_TOK - N_PROMPT, D_MODEL)
    cond = jnp.concatenate([c_ctx[None, :], c], axis=0)
    mod = _ada(cond, ada_w, ada_b)
    nk = ATT_KV_HEADS * ATT_HEAD_DIM

    def router_params(i):
        pad = jnp.zeros((D_MODEL, ROUTER_LANES - N_EXPERTS - N_GROUPS), F32)
        w = jnp.concatenate([moe_w_expert[i], moe_w_group[i], pad], axis=1)
        b = jnp.concatenate([moe_b_expert[i], moe_b_group[i], pad[0]])[None, :]
        hi = w.astype(BF16)
        lo = (w - hi.astype(F32)).astype(BF16)
        return jnp.stack([hi, lo]), b

    mix_blk = pl.BlockSpec((MIX_TM, D_MODEL), lambda i: (i, 0))
    mix_prompt_blk = pl.BlockSpec((MIX_TM, D_MODEL), lambda i: (jnp.minimum(i, N_PROMPT // MIX_TM - 1), 0))
    mix_sample_blk = pl.BlockSpec((MIX_TM, D_MODEL), lambda i: (jnp.maximum(i - N_PROMPT // MIX_TM, 0), 0))
    q, v, g, lff, kf, lfb, kb = _hg_in(xp, xs, mod[0], norm_w[0, 0][None, :], hg_w_in[0].astype(BF16), hg_lb_logits)
    o_f, o_b, state_new = _gla(q, v, lff, kf, lfb, kb, state_hgrn)
    wr, br = router_params(0)
    routed = _mixer_out(
        _hg_out_kernel, "hgrn_out_route", (o_f, o_b, g, hg_onorm[0][None, :], xp, xs),
        (mix_blk, mix_blk, mix_blk, pl.BlockSpec((1, HG_DK), lambda i: (0, 0)), mix_prompt_blk, mix_sample_blk),
        mod[0], norm_w[0, 1][None, :], hg_w_out[0].astype(BF16), wr, br)
    moe_out = _moe_layer(*routed, 0, moe_w1, moe_w3, moe_w2)

    cos, sin = _rope_tables()
    x, qa, ka, va, k_ctx, v_ctx = _at_in(*moe_out, mod[0], mod[1], norm_w[1, 0][None, :],
                                         at_w_in[0].astype(BF16), cos, sin)
    attn_ctx, attn_lat = _attention(qa, ka, va, cache_k[:, 0].reshape(DEC_BATCH, PAST_LEN, nk).transpose(0, 2, 1),
                                    cache_v[:, 0].reshape(DEC_BATCH, PAST_LEN, nk), at_sink[0])
    wr, br = router_params(1)
    routed = _mixer_out(
        _at_out_kernel, "attn_out_route", (attn_ctx, attn_lat, x), (mix_prompt_blk, mix_sample_blk, mix_blk),
        mod[1], norm_w[1, 1][None, :], at_w_out[0].astype(BF16), wr, br)
    y_prompt, y_sample = _moe_combine(*_moe_layer(*routed, 1, moe_w1, moe_w3, moe_w2), mod[1], final_norm[None, :])

    def cache(feature_major):
        return feature_major.reshape(BATCH, 1, ATT_KV_HEADS, ATT_HEAD_DIM, SEQ).transpose(0, 1, 4, 2, 3)

    return (y_prompt.reshape(BATCH, SEQ, D_MODEL), y_sample.reshape(DEC_BATCH, DEC_SEQ, D_MODEL), state_new,
            cache(k_ctx), cache(v_ctx))
```

```python
import functools
from typing import Any, NamedTuple

import jax
import jax.numpy as jnp
import numpy as np
from jax import lax
from jax.experimental import pallas as pl
from jax.experimental.pallas import tpu as pltpu
from jax.experimental.pallas import tpu_sc as plsc

F32 = jnp.float32
BF16 = jnp.bfloat16

D_MODEL = 1024
BATCH = 16
SEQ = 256
DEC_BATCH = 2
DEC_SEQ = 1024
PAST_LEN = 512
GRID_W = 64
HG_HEADS = 8
HG_DK = 128
CHUNK = 16
ATT_HEAD_DIM = 64
ATT_Q_HEADS = 16
ATT_KV_HEADS = 4
ATT_GROUP = 4
WINDOW = 128
BLOCK = 128
ROPE_HALF = 32
ROPE_BASE = 10000.0
N_GROUPS = 4
EXPERTS_PER_GROUP = 8
N_EXPERTS = 32
D_EXPERT = 256
EPS = 1e-6

N_PROMPT = BATCH * SEQ
N_TOK = N_PROMPT + DEC_BATCH * DEC_SEQ
TM = 256
N_TILES = N_TOK // TM
PROMPT_TILES = N_PROMPT // TM
TILES_PER_DEC_SEQ = DEC_SEQ // TM
LANES = 128
SUBLANES = 8
ADA_LANE_BLOCKS = 4
N_COND_USED = 1 + DEC_BATCH
N_COND = 8
ROUTER_LANES = 128
ROUTE_TM = 1024
ROUTE_ROWS = 8
MIX_TM = 512
AT_IN_TM = 512
HG_IN_TM = 512
HG_IN_SUB = 256
HG_IN_TN = 256
DECAY_CLAMP = 60.0
PACK_WORDS = D_MODEL // 4
ROW_WORDS = 2 * PACK_WORDS
MOE_TILE = 256
MOE_STEP_TILES = 4
MOE_WEIGHT_SLOTS = 2 * MOE_STEP_TILES
MOE_ROWS = 2 * N_TOK + N_EXPERTS * MOE_TILE
MIB = 1024 * 1024


def _params(vmem_mib, *semantics):
    return pltpu.CompilerParams(dimension_semantics=semantics, vmem_limit_bytes=vmem_mib * MIB)


def _tile_cond(i):
    return jnp.where(i < PROMPT_TILES, 0, 1 + (i - PROMPT_TILES) // TILES_PER_DEC_SEQ)


def _prompt_tile(i):
    return (jnp.minimum(i, PROMPT_TILES - 1), 0)


def _sample_tile(i):
    return (jnp.maximum(i - PROMPT_TILES, 0), 0)


def _mod_row(mod_ref, cond, which):
    return mod_ref[pl.ds(cond, 1), which * D_MODEL:(which + 1) * D_MODEL]


def _norm_mod(x, nw, shift, scale):
    y = x * lax.rsqrt(jnp.mean(x * x, axis=-1, keepdims=True) + EPS)
    return (y * nw) * (1.0 + scale) + shift


def _silu(x):
    return x * jax.nn.sigmoid(x)


def _ada_kernel(c_ref, w_ref, b_ref, o_ref, s_scr):
    @pl.when((pl.program_id(0) == 0) & (pl.program_id(1) == 0))
    def _():
        s_scr[...] = _silu(c_ref[...])

    tn = w_ref.shape[-1]
    cols = [[] for _ in range(N_COND_USED)]
    for j0 in range(0, tn // LANES, ADA_LANE_BLOCKS):
        acc = [[jnp.zeros((SUBLANES, LANES), F32) for _ in range(ADA_LANE_BLOCKS)] for _ in range(N_COND_USED)]
        for g in range(D_MODEL // SUBLANES):
            rows = slice(g * SUBLANES, (g + 1) * SUBLANES)
            w = [w_ref[0, rows, (j0 + j) * LANES:(j0 + j + 1) * LANES] for j in range(ADA_LANE_BLOCKS)]
            for r in range(N_COND_USED):
                s = s_scr[r, rows, :]
                for j in range(ADA_LANE_BLOCKS):
                    acc[r][j] = acc[r][j] + w[j] * s
        for r in range(N_COND_USED):
            cols[r] += [jnp.sum(a, axis=0, keepdims=True) for a in acc[r]]
    out = [jnp.concatenate(cols[r], axis=1) + b_ref[0] for r in range(N_COND_USED)]
    out.append(jnp.zeros((N_COND - N_COND_USED, tn), F32))
    o_ref[0] = jnp.concatenate(out, axis=0)


def _ada(cond, ada_w, ada_b):
    depth, _, n = ada_w.shape
    tn = 1536
    cond_cols = jnp.broadcast_to(cond[:, :, None], (N_COND_USED, D_MODEL, LANES))
    return pl.pallas_call(
        _ada_kernel,
        out_shape=jax.ShapeDtypeStruct((depth, N_COND, n), F32),
        grid=(depth, n // tn),
        in_specs=[
            pl.BlockSpec((N_COND_USED, D_MODEL, LANES), lambda l, j: (0, 0, 0)),
            pl.BlockSpec((1, D_MODEL, tn), lambda l, j: (l, 0, j)),
            pl.BlockSpec((1, 1, tn), lambda l, j: (l, 0, j)),
        ],
        out_specs=pl.BlockSpec((1, N_COND, tn), lambda l, j: (l, 0, j)),
        scratch_shapes=[pltpu.VMEM((N_COND_USED, D_MODEL, LANES), F32)],
        compiler_params=_params(40, "arbitrary", "arbitrary"),
        name="ada_modulation",
    )(cond_cols, ada_w, ada_b.reshape(depth, 1, n))


def _hg_in_kernel(xp_ref, xs_ref, mod_ref, nw_ref, w_ref, lbl_ref,
                  q_ref, v_ref, g_ref, lff_ref, kf_ref, lfb_ref, kb_ref):
    i = pl.program_id(0) * (HG_IN_TM // TM)
    cond = _tile_cond(i)
    shift, scale = _mod_row(mod_ref, cond, 0), _mod_row(mod_ref, cond, 1)

    def normed(rows):
        x = jnp.where(i < PROMPT_TILES, xp_ref[rows, :], xs_ref[rows, :])
        return _norm_mod(x, nw_ref[...], shift, scale).astype(BF16)

    def proj(h, c, cols):
        return jnp.dot(h, w_ref[:, c * D_MODEL + cols.start:c * D_MODEL + cols.stop], preferred_element_type=F32)

    l0, l1, l2 = lbl_ref[0], lbl_ref[1], lbl_ref[2]
    m = jnp.maximum(jnp.maximum(l0, l1), l2)
    e0, e1, e2 = jnp.exp(l0 - m), jnp.exp(l1 - m), jnp.exp(l2 - m)
    lb = e0 / (e0 + e1 + e2)
    col_tiles = [slice(n * HG_IN_TN, (n + 1) * HG_IN_TN) for n in range(D_MODEL // HG_IN_TN)]

    def gate_and_plain(h, rows, d):
        lf_ref, k_ref, plain_ref = ((lff_ref, kf_ref, q_ref), (lfb_ref, kb_ref, v_ref))[d]
        for cols in col_tiles:
            lbd = lb[d:d + 1, cols]
            z_gate = proj(h, 2 + d, cols)
            z_plain = proj(h, d, cols)
            f = lbd + (1.0 - lbd) * jax.nn.sigmoid(z_gate)
            lf = jnp.log(f)
            hi = lf.astype(BF16)
            lf_ref[0, rows, cols] = hi
            lf_ref[1, rows, cols] = (lf - hi.astype(F32)).astype(BF16)
            k_ref[rows, cols] = (1.0 - f).astype(BF16)
            plain_ref[rows, cols] = z_plain.astype(BF16)

    sub_tiles = [slice(s * HG_IN_SUB, (s + 1) * HG_IN_SUB) for s in range(HG_IN_TM // HG_IN_SUB)]
    h = normed(sub_tiles[0])
    for s, rows in enumerate(sub_tiles):
        gate_and_plain(h, rows, 0)
        h_next = normed(sub_tiles[s + 1]) if s + 1 < len(sub_tiles) else None
        gate_and_plain(h, rows, 1)
        for cols in col_tiles:
            g_ref[rows, cols] = proj(h, 4, cols).astype(BF16)
        h = h_next


def _hg_in(xp, xs, mod, nw, w_in, lb_logits):
    tile = lambda i: (i, 0)
    fixed2 = lambda i: (0, 0)
    bf = jax.ShapeDtypeStruct((N_TOK, D_MODEL), BF16)
    ff = jax.ShapeDtypeStruct((2, N_TOK, D_MODEL), BF16)
    blk = pl.BlockSpec((HG_IN_TM, D_MODEL), tile)
    split_blk = pl.BlockSpec((2, HG_IN_TM, D_MODEL), lambda i: (0, i, 0))
    prompt_steps = N_PROMPT // HG_IN_TM
    return pl.pallas_call(
        _hg_in_kernel,
        out_shape=(bf, bf, bf, ff, bf, ff, bf),
        grid=(N_TOK // HG_IN_TM,),
        in_specs=[
            pl.BlockSpec((HG_IN_TM, D_MODEL), lambda i: (jnp.minimum(i, prompt_steps - 1), 0)),
            pl.BlockSpec((HG_IN_TM, D_MODEL), lambda i: (jnp.maximum(i - prompt_steps, 0), 0)),
            pl.BlockSpec(mod.shape, fixed2),
            pl.BlockSpec((1, D_MODEL), fixed2),
            pl.BlockSpec(w_in.shape, fixed2, pipeline_mode=pl.Buffered(1)),
            pl.BlockSpec(lb_logits.shape, lambda i: (0, 0, 0)),
        ],
        out_specs=(blk, blk, blk, split_blk, blk, split_blk, blk),
        compiler_params=_params(56, "arbitrary"),
        name="hgrn_in_proj",
    )(xp, xs, mod, nw, w_in, lb_logits)


def _gla_exact(reverse, q_ref, v_ref, lf_ref, k_ref, st_ref, o_ref):
    nt = (((1,), (1,)), ((), ()))
    tn = (((0,), (0,)), ((), ()))
    n_chunks = TM // CHUNK
    row_id = lax.broadcasted_iota(jnp.int32, (CHUNK, HG_DK), 0)
    order = range(CHUNK - 1, -1, -1) if reverse else range(CHUNK)

    def chunk_head(it, carry):
        step, h = it // HG_HEADS, it % HG_HEADS
        ci = (n_chunks - 1 - step) if reverse else step
        rows = pl.ds(pl.multiple_of(ci * CHUNK, CHUNK), CHUNK)
        cols = pl.ds(pl.multiple_of(h * HG_DK, HG_DK), HG_DK)
        q, k, v = q_ref[rows, cols], k_ref[rows, cols], v_ref[rows, cols]
        f = jnp.exp(lf_ref[0, rows, cols].astype(F32) + lf_ref[1, rows, cols].astype(F32))
        st = st_ref[0, h]
        o = jnp.zeros((CHUNK, HG_DK), F32)
        for t in order:
            one = row_id == t
            v_t = jnp.where(one, v, jnp.zeros_like(v))
            st = st * f[t:t + 1, :] + lax.dot_general(v_t, k, tn, preferred_element_type=F32)
            o = jnp.where(one, lax.dot_general(q, st.astype(BF16), nt, preferred_element_type=F32), o)
        st_ref[0, h] = st
        o_ref[rows, cols] = o.astype(o_ref.dtype)
        return carry

    lax.fori_loop(0, n_chunks * HG_HEADS, chunk_head, 0)


class _GlaDirection(NamedTuple):
    reverse: bool
    q: Any
    v: Any
    lf: Any
    k: Any
    o: Any
    st: Any
    sw: Any
    ut: Any
    qd: Any
    kd: Any
    ki: Any
    cd: Any
    b: Any


_NT = (((1,), (1,)), ((), ()))
_TN = (((0,), (0,)), ((), ()))
_HG_COLS = [slice(h * HG_DK, (h + 1) * HG_DK) for h in range(HG_HEADS)]
_N_CHUNKS = TM // CHUNK
_N_PAIRS = _N_CHUNKS // 2


def _gla_prepare(d):
    r = lax.broadcasted_iota(jnp.int32, (TM, TM), 0)
    c = lax.broadcasted_iota(jnp.int32, (TM, TM), 1)
    same = (r // CHUNK) == (c // CHUNK)
    tri = (same & ((c >= r) if d.reverse else (c <= r))).astype(BF16)
    b = jnp.dot(tri, d.lf[0], preferred_element_type=F32) + jnp.dot(tri, d.lf[1], preferred_element_type=F32)
    d.b[...] = b
    edge = 0 if d.reverse else CHUNK - 1
    total = b.reshape(_N_CHUNKS, CHUNK, D_MODEL)[:, edge, :]
    d.cd[...] = jnp.exp(total)
    return jnp.max(-total) <= DECAY_CLAMP


def _gla_pair_rows(d, step):
    pi = (_N_PAIRS - 1 - step) if d.reverse else step
    row0 = pi * 2 * CHUNK
    lo, hi = (pl.ds(row0, CHUNK), 2 * pi), (pl.ds(row0 + CHUNK, CHUNK), 2 * pi + 1)
    return pl.ds(row0, 2 * CHUNK), ((hi, lo) if d.reverse else (lo, hi))


def _chunk_decay(d, chunk):
    return d.cd[pl.ds(chunk, 1), :]


def _gla_decayed_operands(d, step):
    both, ((_, c_first), (_, c_second)) = _gla_pair_rows(d, step)
    b = d.b[both, :]
    ki = d.k[both, :] * jnp.exp(jnp.minimum(-b, DECAY_CLAMP)).astype(BF16)
    d.qd[both, :] = d.q[both, :] * jnp.exp(b).astype(BF16)
    d.ki[both, :] = ki
    c_lo, c_hi = (c_second, c_first) if d.reverse else (c_first, c_second)
    ends = jnp.concatenate([jnp.broadcast_to(_chunk_decay(d, c).astype(BF16), (CHUNK, D_MODEL)) for c in (c_lo, c_hi)],
                           axis=0)
    d.kd[both, :] = ki * ends


def _rows_scaled(x, scale_row, second_half):
    scale = jnp.broadcast_to(scale_row.astype(BF16), (CHUNK, x.shape[1]))
    ones = jnp.ones((CHUNK, x.shape[1]), BF16)
    return x * jnp.concatenate([ones, scale] if second_half else [scale, ones], axis=0)


def _gla_key_value_product(d, step):
    _gla_decayed_operands(d, step)
    both, (_, (_, c_second)) = _gla_pair_rows(d, step)
    decay_second = _chunk_decay(d, c_second)
    for h, cols in enumerate(_HG_COLS):
        keys = _rows_scaled(d.kd[both, cols], decay_second[:, cols], second_half=d.reverse)
        d.ut[h] = lax.dot_general(d.v[both, cols], keys, _TN, preferred_element_type=F32)


def _gla_start(d):
    for h in range(HG_HEADS):
        d.sw[0, h] = d.st[0, h].T.astype(BF16)
    _gla_key_value_product(d, 0)


def _gla_pair(d, step, src, dst):
    sr = lax.broadcasted_iota(jnp.int32, (CHUNK, CHUNK), 0)
    sc = lax.broadcasted_iota(jnp.int32, (CHUNK, CHUNK), 1)
    tr = lax.broadcasted_iota(jnp.int32, (CHUNK, 2 * CHUNK), 0)
    tc = lax.broadcasted_iota(jnp.int32, (CHUNK, 2 * CHUNK), 1)
    if d.reverse:
        keep_first = sc >= sr
        keep_second = (tc >= tr) | (tc >= CHUNK)
    else:
        keep_first = sc <= sr
        keep_second = (tc < CHUNK) | (tc - CHUNK <= tr)
    both, ((first, c_first), (second, c_second)) = _gla_pair_rows(d, step)
    decay_first = _chunk_decay(d, c_first)
    decay_both = decay_first * _chunk_decay(d, c_second)
    for h, cols in enumerate(_HG_COLS):
        new = d.st[src, h] * decay_both[:, cols] + d.ut[h]
        d.st[dst, h] = new
        d.sw[dst, h] = new.T.astype(BF16)
    a_first = [lax.dot_general(d.qd[first, cols], d.ki[first, cols], _NT, preferred_element_type=F32)
               for cols in _HG_COLS]
    a_second = []
    for cols in _HG_COLS:
        lo_keys = (d.ki if d.reverse else d.kd)[pl.ds(both.start, CHUNK), cols]
        hi_keys = (d.kd if d.reverse else d.ki)[pl.ds(both.start + CHUNK, CHUNK), cols]
        keys = jnp.concatenate([lo_keys, hi_keys], axis=0)
        a_second.append(lax.dot_general(d.qd[second, cols], keys, _NT, preferred_element_type=F32))
    inter = [jnp.dot(_rows_scaled(d.qd[both, cols], decay_first[:, cols], second_half=not d.reverse),
                     d.sw[src, h], preferred_element_type=F32) for h, cols in enumerate(_HG_COLS)]
    if step + 1 < _N_PAIRS:
        _gla_key_value_product(d, step + 1)
    first_half, second_half = (slice(CHUNK, None), slice(None, CHUNK)) if d.reverse else \
                              (slice(None, CHUNK), slice(CHUNK, None))
    for h, cols in enumerate(_HG_COLS):
        am = jnp.where(keep_first, a_first[h], 0.0).astype(BF16)
        d.o[first, cols] = (jnp.dot(am, d.v[first, cols], preferred_element_type=F32)
                            + inter[h][first_half]).astype(d.o.dtype)
        am = jnp.where(keep_second, a_second[h], 0.0).astype(BF16)
        d.o[second, cols] = (jnp.dot(am, d.v[both, cols], preferred_element_type=F32)
                             + inter[h][second_half]).astype(d.o.dtype)


def _gla_kernel(qf_ref, vf_ref, lff_ref, kf_ref, qb_ref, vb_ref, lfb_ref, kb_ref, s0_ref,
                of_ref, ob_ref, sout_ref, st_scr, sw_scr, ut_scr, qd_scr, kd_scr, ki_scr, cd_scr, b_scr):
    i = pl.program_id(0)
    is_prompt = i < PROMPT_TILES
    first = jnp.logical_or(is_prompt, (i - PROMPT_TILES) % TILES_PER_DEC_SEQ == 0)
    work = [tuple(scr.at[n] for scr in (st_scr, sw_scr, ut_scr, qd_scr, kd_scr, ki_scr, cd_scr, b_scr))
            for n in range(2)]
    fwd = _GlaDirection(False, qf_ref, vf_ref, lff_ref, kf_ref, of_ref, *work[0])
    bwd = _GlaDirection(True, qb_ref, vb_ref, lfb_ref, kb_ref, ob_ref, *work[1])

    @pl.when(is_prompt)
    def _():
        for d in (fwd, bwd):
            d.st[0] = jnp.zeros((HG_HEADS, HG_DK, HG_DK), F32)

    @pl.when(jnp.logical_and(first, jnp.logical_not(is_prompt)))
    def _():
        for n, d in enumerate((fwd, bwd)):
            for h in range(HG_HEADS):
                d.st[0, h] = s0_ref[0, 0, n, h].T

    exact_scores = jnp.logical_and(_gla_prepare(fwd), _gla_prepare(bwd))

    @pl.when(exact_scores)
    def _():
        _gla_start(fwd)
        _gla_start(bwd)

        for step in range(_N_PAIRS):
            for d in (fwd, bwd):
                _gla_pair(d, step, step % 2, 1 - step % 2)

    @pl.when(jnp.logical_not(exact_scores))
    def _():
        for d in (fwd, bwd):
            _gla_exact(d.reverse, d.q, d.v, d.lf, d.k, d.st, d.o)

    @pl.when(is_prompt)
    def _():
        for n, d in enumerate((fwd, bwd)):
            for h in range(HG_HEADS):
                sout_ref[0, 0, n, h] = d.st[0, h].T


def _gla(q, v, lff, kf, lfb, kb, state_hgrn):
    def fwd_tile(i):
        return (i, 0)

    def bwd_tile(i):
        j = (i - PROMPT_TILES) % TILES_PER_DEC_SEQ
        return (jnp.where(i < PROMPT_TILES, i, i - j + (TILES_PER_DEC_SEQ - 1 - j)), 0)

    def s0_idx(i):
        return (jnp.maximum(i - PROMPT_TILES, 0) // TILES_PER_DEC_SEQ, 0, 0, 0, 0, 0)

    def sout_idx(i):
        return (jnp.minimum(i, PROMPT_TILES - 1), 0, 0, 0, 0, 0)

    f_blk = pl.BlockSpec((TM, D_MODEL), fwd_tile)
    b_blk = pl.BlockSpec((TM, D_MODEL), bwd_tile)
    f_split = pl.BlockSpec((2, TM, D_MODEL), lambda i: (0,) + fwd_tile(i))
    b_split = pl.BlockSpec((2, TM, D_MODEL), lambda i: (0,) + bwd_tile(i))
    st_blk = (1, 1, 2, HG_HEADS, HG_DK, HG_DK)
    return pl.pallas_call(
        _gla_kernel,
        out_shape=(
            jax.ShapeDtypeStruct((N_TOK, D_MODEL), BF16),
            jax.ShapeDtypeStruct((N_TOK, D_MODEL), BF16),
            jax.ShapeDtypeStruct((BATCH,) + st_blk[1:], F32),
        ),
        grid=(N_TILES,),
        in_specs=[f_blk, f_blk, f_split, f_blk, b_blk, b_blk, b_split, b_blk, pl.BlockSpec(st_blk, s0_idx)],
        out_specs=(f_blk, b_blk, pl.BlockSpec(st_blk, sout_idx)),
        scratch_shapes=[
            pltpu.VMEM((2, 2, HG_HEADS, HG_DK, HG_DK), F32),
            pltpu.VMEM((2, 2, HG_HEADS, HG_DK, HG_DK), BF16),
            pltpu.VMEM((2, HG_HEADS, HG_DK, HG_DK), F32),
            pltpu.VMEM((2, TM, D_MODEL), BF16),
            pltpu.VMEM((2, TM, D_MODEL), BF16),
            pltpu.VMEM((2, TM, D_MODEL), BF16),
            pltpu.VMEM((2, TM // CHUNK, D_MODEL), F32),
            pltpu.VMEM((2, TM, D_MODEL), F32),
        ],
        compiler_params=_params(48, "arbitrary"),
        name="hgrn_recurrence",
    )(q, v, lff, kf, q, v, lfb, kb, state_hgrn)


def _route(logits):
    lane = lax.broadcasted_iota(jnp.int32, logits.shape, 1)
    neg = jnp.float32(-jnp.inf)
    lane_f = lane.astype(F32)

    def first_max(x):
        m = jnp.max(x, axis=-1, keepdims=True)
        first = jnp.min(jnp.where(x == m, lane_f, float(ROUTER_LANES)), axis=-1, keepdims=True)
        return m, first.astype(jnp.int32)

    is_group = (lane >= N_EXPERTS) & (lane < N_EXPERTS + N_GROUPS)
    gl = jnp.where(is_group, logits, neg)
    gmax, g_lane = first_max(gl)
    g_sel = g_lane - N_EXPERTS
    gsum = jnp.sum(jnp.exp(gl - gmax), axis=-1, keepdims=True)
    p_g = 1.0 / gsum
    in_sel = (lane < N_EXPERTS) & ((lane // EXPERTS_PER_GROUP) == g_sel)
    m1, i1 = first_max(jnp.where(in_sel, logits, neg))
    m2, i2 = first_max(jnp.where(in_sel & (lane != i1), logits, neg))
    e2 = jnp.exp(m2 - m1)
    return i1, i2, p_g / (1.0 + e2), p_g * e2 / (1.0 + e2)


def _pack_rows(x):
    q = PACK_WORDS
    bits = pltpu.bitcast(x.astype(BF16).astype(F32), jnp.uint32)
    return [(bits[:, (2 + h) * q:(3 + h) * q] & jnp.uint32(0xFFFF0000)) | (bits[:, h * q:(h + 1) * q] >> 16)
            for h in range(2)]


def _unpack_rows(half0, half1):
    lo = lambda w: pltpu.bitcast(w << 16, F32).astype(BF16)
    hi = lambda w: pltpu.bitcast(w & jnp.uint32(0xFFFF0000), F32).astype(BF16)
    return [lo(half0), lo(half1), hi(half0), hi(half1)]


def _mix_stream_rows(prompt_ref, sample_ref, rows):
    return jnp.where(pl.program_id(0) < N_PROMPT // MIX_TM, prompt_ref[rows, :], sample_ref[rows, :])


def _mixer_tail(mix_of, x_of, mod_ref, nw2_ref, wo_ref, wr_ref, br_ref, x1_ref, h2p_ref, lg_ref):
    cond = _tile_cond(pl.program_id(0) * (MIX_TM // TM))
    gate, shift, scale = (_mod_row(mod_ref, cond, which) for which in (2, 3, 4))
    project = lambda rows: jnp.dot(mix_of(rows), wo_ref[...], preferred_element_type=F32)

    def tail(rows, out):
        x1 = x_of(rows) + gate * out
        x1_ref[rows, :] = x1
        h2 = _norm_mod(x1, nw2_ref[...], shift, scale)
        h2p_ref[rows, :] = jnp.concatenate(_pack_rows(h2), axis=1)
        h_hi = h2.astype(BF16)
        h_lo = (h2 - h_hi.astype(F32)).astype(BF16)
        lg_ref[rows, :] = (jnp.dot(h_hi, wr_ref[0], preferred_element_type=F32)
                           + (jnp.dot(h_hi, wr_ref[1], preferred_element_type=F32)
                              + jnp.dot(h_lo, wr_ref[0], preferred_element_type=F32))) + br_ref[...]

    sub_tiles = [slice(s * TM, (s + 1) * TM) for s in range(MIX_TM // TM)]
    out = project(sub_tiles[0])
    for s, rows in enumerate(sub_tiles):
        out_next = project(sub_tiles[s + 1]) if s + 1 < len(sub_tiles) else None
        tail(rows, out)
        out = out_next


def _route_kernel(lg_ref, rt_ref, rw_ref, cnt_ref, carry_scr):
    @pl.when(pl.program_id(0) == 0)
    def _():
        carry_scr[...] = jnp.zeros_like(carry_scr)

    logits = lg_ref[...]
    i1, i2, w1, w2 = _route(logits)
    lane = lax.broadcasted_iota(jnp.int32, logits.shape, 1)
    chosen = ((lane == i1) | (lane == i2)).astype(BF16)
    r = lax.broadcasted_iota(jnp.int32, (ROUTE_TM, ROUTE_TM), 0)
    c = lax.broadcasted_iota(jnp.int32, (ROUTE_TM, ROUTE_TM), 1)
    before = jnp.dot((c < r).astype(BF16), chosen, preferred_element_type=F32) + carry_scr[...]
    r1 = jnp.sum(jnp.where(lane == i1, before, 0.0), axis=-1, keepdims=True).astype(jnp.int32)
    r2 = jnp.sum(jnp.where(lane == i2, before, 0.0), axis=-1, keepdims=True).astype(jnp.int32)
    total = carry_scr[...] + jnp.sum(chosen.astype(F32), axis=0, keepdims=True)
    carry_scr[...] = total
    tiles = jnp.ceil(total * (1.0 / MOE_TILE))
    e_from = lax.broadcasted_iota(jnp.int32, (ROUTER_LANES, ROUTER_LANES), 0)
    e_to = lax.broadcasted_iota(jnp.int32, (ROUTER_LANES, ROUTER_LANES), 1)
    tiles_before = jnp.dot(jnp.broadcast_to(tiles, (SUBLANES, ROUTER_LANES)).astype(BF16),
                           (e_from < e_to).astype(BF16), preferred_element_type=F32)
    cnt_ref[...] = jnp.concatenate([total, tiles_before[:1] * MOE_TILE,
                                    jnp.zeros((ROUTE_ROWS - 2, ROUTER_LANES), F32)], axis=0)
    by_token = jnp.where(lane == 0, i1, jnp.where(lane == 1, i2, jnp.where(lane == 2, r1, r2)))
    rt_ref[...] = by_token.T[:ROUTE_ROWS, :]
    rw_ref[...] = jnp.where(lane == 0, w1, w2)


def _route_tokens(logits):
    blk = pl.BlockSpec((ROUTE_TM, ROUTER_LANES), lambda i: (i, 0))
    return pl.pallas_call(
        _route_kernel,
        out_shape=(
            jax.ShapeDtypeStruct((ROUTE_ROWS, N_TOK), jnp.int32),
            jax.ShapeDtypeStruct((N_TOK, ROUTER_LANES), F32),
            jax.ShapeDtypeStruct((ROUTE_ROWS, ROUTER_LANES), F32),
        ),
        grid=(N_TOK // ROUTE_TM,),
        in_specs=[blk],
        out_specs=(pl.BlockSpec((ROUTE_ROWS, ROUTE_TM), lambda i: (0, i)), blk,
                   pl.BlockSpec((ROUTE_ROWS, ROUTER_LANES), lambda i: (0, 0))),
        scratch_shapes=[pltpu.VMEM((1, ROUTER_LANES), F32)],
        compiler_params=_params(32, "arbitrary"),
        name="moe_route",
    )(logits)


def _hg_out_kernel(of_ref, ob_ref, g_ref, on_ref, xp_ref, xs_ref, *rest):
    def mix_of(rows):
        o = of_ref[rows, :].astype(F32) + ob_ref[rows, :].astype(F32)
        parts = []
        for h in range(HG_HEADS):
            oh = o[:, h * HG_DK:(h + 1) * HG_DK]
            parts.append(oh * lax.rsqrt(jnp.mean(oh * oh, axis=-1, keepdims=True) + EPS) * on_ref[...])
        y = jnp.concatenate(parts, axis=1) * _silu(g_ref[rows, :].astype(F32))
        return y.astype(BF16)

    _mixer_tail(mix_of, functools.partial(_mix_stream_rows, xp_ref, xs_ref), *rest)


def _at_out_kernel(ac_ref, al_ref, x_ref, *rest):
    _mixer_tail(functools.partial(_mix_stream_rows, ac_ref, al_ref), lambda rows: x_ref[rows, :], *rest)


def _mixer_out(kernel_fn, name, mix_inputs, mix_specs, mod, nw2, w_out, w_router, b_router):
    tile = lambda i: (i, 0)
    fixed2 = lambda i: (0, 0)
    blk = pl.BlockSpec((MIX_TM, D_MODEL), tile)
    lanes_blk = pl.BlockSpec((MIX_TM, ROUTER_LANES), tile)
    return pl.pallas_call(
        kernel_fn,
        out_shape=(
            jax.ShapeDtypeStruct((N_TOK, D_MODEL), F32),
            jax.ShapeDtypeStruct((N_TOK, ROW_WORDS), jnp.uint32),
            jax.ShapeDtypeStruct((N_TOK, ROUTER_LANES), F32),
        ),
        grid=(N_TOK // MIX_TM,),
        in_specs=list(mix_specs) + [
            pl.BlockSpec(mod.shape, fixed2),
            pl.BlockSpec((1, D_MODEL), fixed2),
            pl.BlockSpec(w_out.shape, fixed2),
            pl.BlockSpec(w_router.shape, lambda i: (0, 0, 0)),
            pl.BlockSpec((1, ROUTER_LANES), fixed2),
        ],
        out_specs=(blk, pl.BlockSpec((MIX_TM, ROW_WORDS), tile), lanes_blk),
        compiler_params=_params(40, "arbitrary"),
        name=name,
    )(*mix_inputs, mod, nw2, w_out, w_router, b_router)


def _moe_plan(route_t, counts):
    cnt = counts[0, :N_EXPERTS].astype(jnp.int32)
    offs = counts[1, :N_EXPERTS].astype(jnp.int32)
    ends = offs + ((cnt + MOE_TILE - 1) // MOE_TILE) * MOE_TILE
    experts = route_t[0:2]
    pos = jnp.sum(jnp.where(experts[None, :, :] == jnp.arange(N_EXPERTS)[:, None, None], offs[:, None, None], 0),
                  axis=0) + route_t[2:4]
    tile_start = jnp.arange(MOE_ROWS // MOE_TILE, dtype=jnp.int32) * MOE_TILE
    tile_expert = jnp.minimum(jnp.sum(ends[None, :] <= tile_start[:, None], axis=1), N_EXPERTS - 1).astype(jnp.int32)
    of_tile = tile_expert[:, None] == jnp.arange(N_EXPERTS)[None, :]
    tile_offs = jnp.sum(jnp.where(of_tile, offs[None, :], 0), axis=1)
    tile_cnt = jnp.sum(jnp.where(of_tile, cnt[None, :], 0), axis=1)
    tile_rows = jnp.clip(tile_offs + tile_cnt - tile_start, 0, MOE_TILE).astype(jnp.int32)
    active = tile_start < ends[-1]
    tile_opens = jnp.where(active & (tile_start == tile_offs), tile_expert, -1).astype(jnp.int32)
    tile_opens = jnp.concatenate([tile_opens, jnp.full((MOE_STEP_TILES,), -1, jnp.int32)])
    n_active = (ends[-1] // MOE_TILE).astype(jnp.int32).reshape(1)
    slot = (jnp.cumsum((cnt > 0).astype(jnp.int32)) - 1) % MOE_WEIGHT_SLOTS
    tile_slot = jnp.sum(jnp.where(of_tile, slot[None, :], 0), axis=1)
    last_slot = jnp.sum(jnp.where(tile_start == ends[-1] - MOE_TILE, tile_slot, 0))
    tile_slot = jnp.where(active, tile_slot, last_slot).astype(jnp.int32)
    return pos.astype(jnp.int32), (tile_rows, tile_slot, tile_opens, n_active)


SC_WINDOW = 64


def _sc_mesh():
    return plsc.VectorSubcoreMesh(core_axis_name="c", subcore_axis_name="s")


def _sc_scatter_rows(x, idx_a, idx_b, n_out_rows):
    n = x.shape[0]

    @functools.partial(pl.kernel, out_type=jax.ShapeDtypeStruct((n_out_rows, ROW_WORDS), x.dtype), mesh=_sc_mesh(),
                       scratch_types=[pltpu.SemaphoreType.DMA, pltpu.SemaphoreType.DMA])
    def scatter(x_hbm, ia_hbm, ib_hbm, o_hbm, sem_a, sem_b):
        def body(x_vmem, ia_vmem, ib_vmem):
            copy_a = pltpu.async_copy(x_vmem, o_hbm.at[ia_vmem.at[0]], sem_a)
            copy_b = pltpu.async_copy(x_vmem, o_hbm.at[ib_vmem.at[0]], sem_b)
            copy_a.wait()
            copy_b.wait()

        idx_spec = pl.BlockSpec((1, SC_WINDOW), index_map=lambda i: (i, 0))
        pltpu.emit_pipeline(
            body, grid=(n // SC_WINDOW,),
            in_specs=[pl.BlockSpec((SC_WINDOW, ROW_WORDS), index_map=lambda i: (i, 0)), idx_spec, idx_spec],
            out_specs=[],
            core_axis_name=("c", "s"), dimension_semantics=(pltpu.PARALLEL,),
        )(x_hbm, ia_hbm, ib_hbm)

    return scatter(x, idx_a, idx_b)


def _sc_gather_rows(table, idx):
    n = idx.size

    @functools.partial(pl.kernel, out_type=jax.ShapeDtypeStruct((n, ROW_WORDS), table.dtype), mesh=_sc_mesh())
    def gather(t_hbm, i_hbm, o_hbm):
        def body(i_vmem, o_vmem):
            pltpu.sync_copy(t_hbm.at[i_vmem.at[0]], o_vmem)

        pltpu.emit_pipeline(
            body, grid=(n // SC_WINDOW,),
            in_specs=[pl.BlockSpec((1, SC_WINDOW), index_map=lambda i: (i, 0))],
            out_specs=[pl.BlockSpec((SC_WINDOW, ROW_WORDS), index_map=lambda i: (i, 0))],
            core_axis_name=("c", "s"), dimension_semantics=(pltpu.PARALLEL,),
        )(i_hbm, o_hbm)

    return gather(table, idx)


def _ffn_kernel(tr_ref, ts_ref, to_ref, na_ref, xs_ref, w1_hbm, w3_hbm, w2_hbm, ys_ref,
                w1_buf, w3_buf, w2_buf, sem, *, layer):
    def weight_copies(t):
        pairs = ((w1_hbm, w1_buf), (w3_hbm, w3_buf), (w2_hbm, w2_buf))
        return [pltpu.make_async_copy(hbm.at[layer, to_ref[t]], buf.at[ts_ref[t]], sem.at[ts_ref[t], j])
                for j, (hbm, buf) in enumerate(pairs)]

    def for_experts_opened_from(first_tile, action):
        for j in range(MOE_STEP_TILES):
            t = first_tile + j

            @pl.when(to_ref[t] >= 0)
            def _():
                for copy in weight_copies(t):
                    action(copy)

    def tile(t, rows):
        slot = ts_ref[t]
        row = lax.broadcasted_iota(jnp.int32, (MOE_TILE, PACK_WORDS), 0)
        live = row < tr_ref[t]
        halves = [jnp.where(live, xs_ref[rows, h * PACK_WORDS:(h + 1) * PACK_WORDS],
                            jnp.zeros((MOE_TILE, PACK_WORDS), jnp.uint32)) for h in range(2)]
        chunks = _unpack_rows(*halves)

        def up(w_buf):
            acc = None
            for k, chunk in enumerate(chunks):
                w = w_buf[slot, k * PACK_WORDS:(k + 1) * PACK_WORDS, :].astype(BF16)
                part = jnp.dot(chunk, w, preferred_element_type=F32)
                acc = part if acc is None else acc + part
            return acc

        hid = (_silu(up(w1_buf)) * up(w3_buf)).astype(BF16)
        y = jnp.dot(hid, w2_buf[slot].astype(BF16), preferred_element_type=F32)
        ys_ref[rows, :] = jnp.concatenate(_pack_rows(y), axis=1)

    step = pl.program_id(0)
    first_tile = step * MOE_STEP_TILES

    @pl.when(first_tile < na_ref[0])
    def _():
        @pl.when(step == 0)
        def _():
            for_experts_opened_from(first_tile, lambda copy: copy.start())

        for_experts_opened_from(first_tile + MOE_STEP_TILES, lambda copy: copy.start())
        for_experts_opened_from(first_tile, lambda copy: copy.wait())
        for j in range(MOE_STEP_TILES):
            tile(first_tile + j, slice(j * MOE_TILE, (j + 1) * MOE_TILE))


def _ffn(xs, tables, layer, w1, w3, w2):
    step_rows = MOE_STEP_TILES * MOE_TILE
    row_tile = lambda s, tr, ts, to, na: (jnp.minimum(s, (na[0] - 1) // MOE_STEP_TILES), 0)
    hbm = pl.BlockSpec(memory_space=pl.ANY)
    return pl.pallas_call(
        functools.partial(_ffn_kernel, layer=layer),
        out_shape=jax.ShapeDtypeStruct((MOE_ROWS, ROW_WORDS), jnp.uint32),
        grid_spec=pltpu.PrefetchScalarGridSpec(
            num_scalar_prefetch=len(tables),
            grid=(MOE_ROWS // step_rows,),
            in_specs=[pl.BlockSpec((step_rows, ROW_WORDS), row_tile), hbm, hbm, hbm],
            out_specs=pl.BlockSpec((step_rows, ROW_WORDS), row_tile),
            scratch_shapes=[
                pltpu.VMEM((MOE_WEIGHT_SLOTS, D_MODEL, D_EXPERT), F32),
                pltpu.VMEM((MOE_WEIGHT_SLOTS, D_MODEL, D_EXPERT), F32),
                pltpu.VMEM((MOE_WEIGHT_SLOTS, D_EXPERT, D_MODEL), F32),
                pltpu.SemaphoreType.DMA((MOE_WEIGHT_SLOTS, 3)),
            ],
        ),
        compiler_params=_params(48, "arbitrary"),
        name="moe_experts",
    )(*tables, xs, w1, w3, w2)


def _moe_res(x_ref, y_ref, rw_ref, mod_ref, cond, rows=slice(None)):
    rw = rw_ref[rows, :]
    wa = rw[:, 0:1]
    wb = rw[:, 1:2]
    ya = _unpack_rows(y_ref[0, rows, :PACK_WORDS], y_ref[0, rows, PACK_WORDS:])
    yb = _unpack_rows(y_ref[1, rows, :PACK_WORDS], y_ref[1, rows, PACK_WORDS:])
    y = jnp.concatenate([wa * a.astype(F32) + wb * b.astype(F32) for a, b in zip(ya, yb)], axis=1)
    return x_ref[rows, :] + _mod_row(mod_ref, cond, 5) * y


def _moe_res_final_kernel(x_ref, y_ref, rw_ref, mod_ref, fn_ref, op_ref, os_ref):
    x = _moe_res(x_ref, y_ref, rw_ref, mod_ref, _tile_cond(pl.program_id(0)))
    y = x * lax.rsqrt(jnp.mean(x * x, axis=-1, keepdims=True) + EPS) * fn_ref[...]
    is_prompt = pl.program_id(0) < PROMPT_TILES

    @pl.when(is_prompt)
    def _():
        op_ref[...] = y

    @pl.when(jnp.logical_not(is_prompt))
    def _():
        os_ref[...] = y


def _moe_combine(x1, y_pairs, route_w, mod, final_norm):
    tile = lambda i: (i, 0)
    fixed2 = lambda i: (0, 0)
    return pl.pallas_call(
        _moe_res_final_kernel,
        out_shape=(jax.ShapeDtypeStruct((N_PROMPT, D_MODEL), F32),
                   jax.ShapeDtypeStruct((N_TOK - N_PROMPT, D_MODEL), F32)),
        grid=(N_TILES,),
        in_specs=[pl.BlockSpec((TM, D_MODEL), tile),
                  pl.BlockSpec((2, TM, ROW_WORDS), lambda i: (0, i, 0)),
                  pl.BlockSpec((TM, ROUTER_LANES), tile),
                  pl.BlockSpec(mod.shape, fixed2),
                  pl.BlockSpec((1, D_MODEL), fixed2)],
        out_specs=(pl.BlockSpec((TM, D_MODEL), _prompt_tile), pl.BlockSpec((TM, D_MODEL), _sample_tile)),
        compiler_params=_params(32, "arbitrary"),
        name="moe_combine",
    )(x1, y_pairs, route_w, mod, final_norm)


def _moe_layer(x1, h2p, logits, layer, w1, w3, w2):
    route_t, route_w, counts = _route_tokens(logits)
    pos, tables = _moe_plan(route_t, counts)
    windows = pos.reshape(2, N_TOK // SC_WINDOW, SC_WINDOW)
    xs = _sc_scatter_rows(h2p, windows[0], windows[1], MOE_ROWS)
    ys = _ffn(xs, tables, layer, w1, w3, w2)
    y_pairs = _sc_gather_rows(ys, windows.reshape(2 * N_TOK // SC_WINDOW, SC_WINDOW))
    return x1, y_pairs.reshape(2, N_TOK, ROW_WORDS), route_w


def _swap_rotary_halves(x):
    n = x.shape[-1]
    lane = lax.broadcasted_iota(jnp.int32, x.shape, 1)
    quarter = ROPE_HALF // 2
    return jnp.where((lane % ROPE_HALF) < quarter, pltpu.roll(x, n - quarter, 1), pltpu.roll(x, quarter, 1))


def _at_in_kernel(x1_ref, y_ref, rw_ref, mod_prev_ref, mod_ref, nw_ref, w_ref, cos_ref, sin_ref,
                  x_ref, q_ref, kt_ref, v_ref, kc_ref, vc_ref):
    step = pl.program_id(0)
    cond = _tile_cond(step * (AT_IN_TM // TM))
    shift, scale = _mod_row(mod_ref, cond, 0), _mod_row(mod_ref, cond, 1)
    nq = ATT_Q_HEADS * ATT_HEAD_DIM
    nk = ATT_KV_HEADS * ATT_HEAD_DIM

    def normed(rows):
        x = _moe_res(x1_ref, y_ref, rw_ref, mod_prev_ref, cond, rows)
        x_ref[rows, :] = x
        return _norm_mod(x, nw_ref[...], shift, scale).astype(BF16)

    def rope(x, rows):
        reps = x.shape[-1] // LANES
        cos = jnp.concatenate([cos_ref[rows, :]] * reps, axis=1)
        sin = jnp.concatenate([sin_ref[rows, :]] * reps, axis=1)
        return x * cos + _swap_rotary_halves(x) * sin

    def project_q(h):
        return jnp.dot(h, w_ref[:, :nq], preferred_element_type=F32)

    def project_kv(h):
        return (jnp.dot(h, w_ref[:, nq:nq + nk], preferred_element_type=F32),
                jnp.dot(h, w_ref[:, nq + nk:], preferred_element_type=F32))

    def tail(rows, zq, zk, v):
        q_ref[rows, :] = (rope(zq, rows) * ATT_HEAD_DIM ** -0.5).astype(BF16)
        kt = rope(zk, rows).T
        kt_ref[:, rows] = kt.astype(BF16)
        v_ref[rows, :] = v.astype(BF16)
        return kt, v.T

    sub_tiles = [slice(s * TM, (s + 1) * TM) for s in range(AT_IN_TM // TM)]
    cache_rows = []
    h = normed(sub_tiles[0])
    zq = project_q(h)
    for s, rows in enumerate(sub_tiles):
        last = s + 1 == len(sub_tiles)
        h_next = None if last else normed(sub_tiles[s + 1])
        zk, v = project_kv(h)
        zq_next = None if last else project_q(h_next)
        cache_rows.append(tail(rows, zq, zk, v))
        h, zq = h_next, zq_next

    @pl.when(step < N_PROMPT // AT_IN_TM)
    def _():
        for rows, (kt, vt) in zip(sub_tiles, cache_rows):
            kc_ref[rows, :] = kt
            vc_ref[rows, :] = vt


def _rope_tables():
    f32 = np.float32
    pos = np.arange(DEC_SEQ)
    t_row = (pos // GRID_W).astype(f32)
    t_col = (pos % GRID_W).astype(f32)
    inv = f32(ROPE_BASE) ** (-np.arange(0, ROPE_HALF, 2, dtype=f32) / f32(ROPE_HALF))
    j = np.arange(LANES) % ATT_HEAD_DIM
    freq = inv[(j % ROPE_HALF) % (ROPE_HALF // 2)]
    ang = (np.where((j < ROPE_HALF)[None, :], t_row[:, None], t_col[:, None]) * freq[None, :]).astype(f32)
    sign = np.where((j % ROPE_HALF) < ROPE_HALF // 2, -1.0, 1.0).astype(f32)
    cos = np.concatenate([np.ones((AT_IN_TM, LANES), f32), np.cos(ang)], axis=0)
    sin = np.concatenate([np.zeros((AT_IN_TM, LANES), f32), np.sin(ang) * sign[None, :]], axis=0)
    return jnp.asarray(cos, F32), jnp.asarray(sin, F32)


def _at_in(x1, y_pairs, route_w, mod_prev, mod, nw, w_in, cos, sin):
    tile = lambda i: (i, 0)
    fixed2 = lambda i: (0, 0)
    prompt_steps = N_PROMPT // AT_IN_TM
    rope_tile = lambda i: (jnp.where(i < prompt_steps, 0, 1 + (i - prompt_steps) % (DEC_SEQ // AT_IN_TM)), 0)
    prompt_tile = lambda i: (jnp.minimum(i, prompt_steps - 1), 0)
    nk = ATT_KV_HEADS * ATT_HEAD_DIM
    return pl.pallas_call(
        _at_in_kernel,
        out_shape=(
            jax.ShapeDtypeStruct((N_TOK, D_MODEL), F32),
            jax.ShapeDtypeStruct((N_TOK, D_MODEL), BF16),
            jax.ShapeDtypeStruct((nk, N_TOK), BF16),
            jax.ShapeDtypeStruct((N_TOK, nk), BF16),
            jax.ShapeDtypeStruct((N_PROMPT, nk), F32),
            jax.ShapeDtypeStruct((N_PROMPT, nk), F32),
        ),
        grid=(N_TOK // AT_IN_TM,),
        in_specs=[
            pl.BlockSpec((AT_IN_TM, D_MODEL), tile),
            pl.BlockSpec((2, AT_IN_TM, ROW_WORDS), lambda i: (0, i, 0)),
            pl.BlockSpec((AT_IN_TM, ROUTER_LANES), tile),
            pl.BlockSpec(mod_prev.shape, fixed2),
            pl.BlockSpec(mod.shape, fixed2),
            pl.BlockSpec((1, D_MODEL), fixed2),
            pl.BlockSpec(w_in.shape, fixed2),
            pl.BlockSpec((AT_IN_TM, LANES), rope_tile),
            pl.BlockSpec((AT_IN_TM, LANES), rope_tile),
        ],
        out_specs=(pl.BlockSpec((AT_IN_TM, D_MODEL), tile), pl.BlockSpec((AT_IN_TM, D_MODEL), tile),
                   pl.BlockSpec((nk, AT_IN_TM), lambda i: (0, i)),
                   pl.BlockSpec((AT_IN_TM, nk), tile),
                   pl.BlockSpec((AT_IN_TM, nk), prompt_tile), pl.BlockSpec((AT_IN_TM, nk), prompt_tile)),
        compiler_params=_params(40, "arbitrary"),
        name="attn_in_proj",
    )(x1, y_pairs, route_w, mod_prev, mod, nw, w_in, cos, sin)


def _attend(q, kt_all, v_all, mask, sink_ref, o_ref):
    nq = q.shape[0]
    group_lanes = ATT_GROUP * ATT_HEAD_DIM
    lane = lax.broadcasted_iota(jnp.int32, (nq, group_lanes), 1)
    mine = [(lane // ATT_HEAD_DIM) == g for g in range(ATT_GROUP)]
    row_head = lax.broadcasted_iota(jnp.int32, (ATT_GROUP * nq, 1), 0) // nq
    if mask is not None:
        mask = jnp.concatenate([mask] * ATT_GROUP, axis=0)
    for hk in range(ATT_KV_HEADS):
        vh = v_all[:, hk * ATT_HEAD_DIM:(hk + 1) * ATT_HEAD_DIM]
        kt = jnp.concatenate([kt_all[hk * ATT_HEAD_DIM:(hk + 1) * ATT_HEAD_DIM, :]] * ATT_GROUP, axis=0)
        vt = jnp.concatenate([vh] * ATT_GROUP, axis=1)
        qg = q[:, hk * group_lanes:(hk + 1) * group_lanes]
        q_stack = jnp.concatenate([jnp.where(mine[g], qg, jnp.zeros_like(qg)) for g in range(ATT_GROUP)], axis=0)
        s = jnp.dot(q_stack, kt, preferred_element_type=F32)
        if mask is not None:
            s = jnp.where(mask, s, -jnp.inf)
        sink = jnp.zeros((ATT_GROUP * nq, 1), F32)
        for g in range(ATT_GROUP):
            sink = jnp.where(row_head == g, sink_ref[hk * ATT_GROUP + g], sink)
        m = jnp.maximum(jnp.max(s, axis=-1, keepdims=True), sink)
        p = jnp.exp(s - m)
        denom = jnp.sum(p, axis=-1, keepdims=True) + jnp.exp(sink - m)
        o = jnp.dot(p.astype(BF16), vt, preferred_element_type=F32) / denom
        acc = jnp.where(mine[0], o[:nq], 0.0)
        for g in range(1, ATT_GROUP):
            acc = acc + jnp.where(mine[g], o[g * nq:(g + 1) * nq], 0.0)
        o_ref[:, hk * group_lanes:(hk + 1) * group_lanes] = acc.astype(BF16)


def _ctx_attn_kernel(sink_ref, q_ref, k_ref, v_ref, o_ref):
    _attend(q_ref[...], k_ref[...], v_ref[...], None, sink_ref, o_ref)


def _lat_attn_kernel(sink_ref, q_ref, kp_ref, kc_ref, kn_ref, vp_ref, vc_ref, vn_ref, ck_ref, cv_ref, o_ref):
    jb = pl.program_id(1)
    kt_all = jnp.concatenate([kp_ref[...], kc_ref[...], kn_ref[...], ck_ref[0].astype(BF16)], axis=1)
    v_all = jnp.concatenate([vp_ref[...], vc_ref[...], vn_ref[...], cv_ref[0].astype(BF16)], axis=0)
    nkeys = 3 * BLOCK + PAST_LEN
    qi = lax.broadcasted_iota(jnp.int32, (BLOCK, nkeys), 0)
    kj = lax.broadcasted_iota(jnp.int32, (BLOCK, nkeys), 1)
    qpos = jb * BLOCK + qi
    kpos = (jb - 1) * BLOCK + kj
    local_ok = (jnp.abs(qpos - kpos) <= WINDOW) & (kpos >= 0) & (kpos < DEC_SEQ)
    mask = (kj >= 3 * BLOCK) | local_ok
    _attend(q_ref[...], kt_all, v_all, mask, sink_ref, o_ref)


def _attention(q, kt, v, cache_kt, cache_v, sink):
    nk = ATT_KV_HEADS * ATT_HEAD_DIM
    smem = pl.BlockSpec(memory_space=pltpu.SMEM)
    ctx = pl.pallas_call(
        _ctx_attn_kernel,
        out_shape=jax.ShapeDtypeStruct((N_PROMPT, D_MODEL), BF16),
        grid=(BATCH,),
        in_specs=[
            smem,
            pl.BlockSpec((SEQ, D_MODEL), lambda b: (b, 0)),
            pl.BlockSpec((nk, SEQ), lambda b: (0, b)),
            pl.BlockSpec((SEQ, nk), lambda b: (b, 0)),
        ],
        out_specs=pl.BlockSpec((SEQ, D_MODEL), lambda b: (b, 0)),
        compiler_params=_params(40, "arbitrary"),
        name="context_attention",
    )(sink, q, kt, v)

    nb = DEC_SEQ // BLOCK
    base = N_PROMPT // BLOCK
    cur = lambda b, j: (base + b * nb + j, 0)
    prev = lambda b, j: (base + b * nb + jnp.maximum(j - 1, 0), 0)
    nxt = lambda b, j: (base + b * nb + jnp.minimum(j + 1, nb - 1), 0)
    kv_blk = lambda f: pl.BlockSpec((BLOCK, nk), f)
    kt_blk = lambda f: pl.BlockSpec((nk, BLOCK), lambda b, j: f(b, j)[::-1])
    cache_blk = pl.BlockSpec((1, PAST_LEN, nk), lambda b, j: (b, 0, 0))
    cache_kt_blk = pl.BlockSpec((1, nk, PAST_LEN), lambda b, j: (b, 0, 0))
    lat = pl.pallas_call(
        _lat_attn_kernel,
        out_shape=jax.ShapeDtypeStruct((DEC_BATCH * DEC_SEQ, D_MODEL), BF16),
        grid=(DEC_BATCH, nb),
        in_specs=[
            smem,
            pl.BlockSpec((BLOCK, D_MODEL), cur),
            kt_blk(prev), kt_blk(cur), kt_blk(nxt),
            kv_blk(prev), kv_blk(cur), kv_blk(nxt),
            cache_kt_blk, cache_blk,
        ],
        out_specs=pl.BlockSpec((BLOCK, D_MODEL), lambda b, j: (b * nb + j, 0)),
        compiler_params=_params(40, "arbitrary", "arbitrary"),
        name="latent_attention",
    )(sink, q, kt, kt, kt, v, v, v, cache_kt, cache_v)
    return ctx, lat


def kernel(x_prompt, x_sample, state_hgrn, cache_k, cache_v, c, c_ctx, ada_w, ada_b, norm_w, hg_w_in,
           hg_lb_logits, hg_onorm, hg_w_out, at_w_in, at_sink, at_w_out, moe_w_group, moe_b_group,
           moe_w_expert, moe_b_expert, moe_w1, moe_w3, moe_w2, final_norm):
    xp = x_prompt.reshape(N_PROMPT, D_MODEL)
    xs = x_sample.reshape(N_TOK - N_PROMPT, D_MODEL)
    cond = jnp.concatenate([c_ctx[None, :], c], axis=0)
    mod = _ada(cond, ada_w, ada_b)
    nk = ATT_KV_HEADS * ATT_HEAD_DIM

    def router_params(i):
        pad = jnp.zeros((D_MODEL, ROUTER_LANES - N_EXPERTS - N_GROUPS), F32)
        w = jnp.concatenate([moe_w_expert[i], moe_w_group[i], pad], axis=1)
        b = jnp.concatenate([moe_b_expert[i], moe_b_group[i], pad[0]])[None, :]
        hi = w.astype(BF16)
        lo = (w - hi.astype(F32)).astype(BF16)
        return jnp.stack([hi, lo]), b

    mix_blk = pl.BlockSpec((MIX_TM, D_MODEL), lambda i: (i, 0))
    mix_prompt_blk = pl.BlockSpec((MIX_TM, D_MODEL), lambda i: (jnp.minimum(i, N_PROMPT // MIX_TM - 1), 0))
    mix_sample_blk = pl.BlockSpec((MIX_TM, D_MODEL), lambda i: (jnp.maximum(i - N_PROMPT // MIX_TM, 0), 0))
    q, v, g, lff, kf, lfb, kb = _hg_in(xp, xs, mod[0], norm_w[0, 0][None, :], hg_w_in[0].astype(BF16), hg_lb_logits)
    o_f, o_b, state_new = _gla(q, v, lff, kf, lfb, kb, state_hgrn)
    wr, br = router_params(0)
    routed = _mixer_out(
        _hg_out_kernel, "hgrn_out_route", (o_f, o_b, g, hg_onorm[0][None, :], xp, xs),
        (mix_blk, mix_blk, mix_blk, pl.BlockSpec((1, HG_DK), lambda i: (0, 0)), mix_prompt_blk, mix_sample_blk),
        mod[0], norm_w[0, 1][None, :], hg_w_out[0].astype(BF16), wr, br)
    moe_out = _moe_layer(*routed, 0, moe_w1, moe_w3, moe_w2)

    cos, sin = _rope_tables()
    x, qa, ka, va, k_ctx, v_ctx = _at_in(*moe_out, mod[0], mod[1], norm_w[1, 0][None, :],
                                         at_w_in[0].astype(BF16), cos, sin)
    attn_ctx, attn_lat = _attention(qa, ka, va, cache_k[:, 0].reshape(DEC_BATCH, PAST_LEN, nk).transpose(0, 2, 1),
                                    cache_v[:, 0].reshape(DEC_BATCH, PAST_LEN, nk), at_sink[0])
    wr, br = router_params(1)
    routed = _mixer_out(
        _at_out_kernel, "attn_out_route", (attn_ctx, attn_lat, x), (mix_prompt_blk, mix_sample_blk, mix_blk),
        mod[1], norm_w[1, 1][None, :], at_w_out[0].astype(BF16), wr, br)
    y_prompt, y_sample = _moe_combine(*_moe_layer(*routed, 1, moe_w1, moe_w3, moe_w2), mod[1], final_norm[None, :])

    def cache(feature_major):
        return feature_major.reshape(BATCH, 1, ATT_KV_HEADS, ATT_HEAD_DIM, SEQ).transpose(0, 1, 4, 2, 3)

    return (y_prompt.reshape(BATCH, SEQ, D_MODEL), y_sample.reshape(DEC_BATCH, DEC_SEQ, D_MODEL), state_new,
            cache(k_ctx), cache(v_ctx))
```

```python
import functools
from typing import Any, NamedTuple

import jax
import jax.numpy as jnp
import numpy as np
from jax import lax
from jax.experimental import pallas as pl
from jax.experimental.pallas import tpu as pltpu
from jax.experimental.pallas import tpu_sc as plsc

F32 = jnp.float32
BF16 = jnp.bfloat16

D_MODEL = 1024
BATCH = 16
SEQ = 256
DEC_BATCH = 2
DEC_SEQ = 1024
PAST_LEN = 512
GRID_W = 64
HG_HEADS = 8
HG_DK = 128
CHUNK = 16
ATT_HEAD_DIM = 64
ATT_Q_HEADS = 16
ATT_KV_HEADS = 4
ATT_GROUP = 4
WINDOW = 128
BLOCK = 128
ROPE_HALF = 32
ROPE_BASE = 10000.0
N_GROUPS = 4
EXPERTS_PER_GROUP = 8
N_EXPERTS = 32
D_EXPERT = 256
EPS = 1e-6

N_PROMPT = BATCH * SEQ
N_TOK = N_PROMPT + DEC_BATCH * DEC_SEQ
TM = 256
N_TILES = N_TOK // TM
PROMPT_TILES = N_PROMPT // TM
TILES_PER_DEC_SEQ = DEC_SEQ // TM
LANES = 128
SUBLANES = 8
ADA_LANE_BLOCKS = 4
N_COND_USED = 1 + DEC_BATCH
N_COND = 8
ROUTER_LANES = 128
ROUTE_TM = 1024
ROUTE_ROWS = 8
MIX_TM = 512
AT_IN_TM = 512
HG_IN_TM = 512
HG_IN_SUB = 256
HG_IN_TN = 256
DECAY_CLAMP = 60.0
PACK_WORDS = D_MODEL // 4
ROW_WORDS = 2 * PACK_WORDS
MOE_TILE = 256
MOE_STEP_TILES = 4
MOE_WEIGHT_SLOTS = 2 * MOE_STEP_TILES
MOE_ROWS = 2 * N_TOK + N_EXPERTS * MOE_TILE
MIB = 1024 * 1024


def _params(vmem_mib, *semantics):
    return pltpu.CompilerParams(dimension_semantics=semantics, vmem_limit_bytes=vmem_mib * MIB)


def _tile_cond(i):
    return jnp.where(i < PROMPT_TILES, 0, 1 + (i - PROMPT_TILES) // TILES_PER_DEC_SEQ)


def _prompt_tile(i):
    return (jnp.minimum(i, PROMPT_TILES - 1), 0)


def _sample_tile(i):
    return (jnp.maximum(i - PROMPT_TILES, 0), 0)


def _mod_row(mod_ref, cond, which):
    return mod_ref[pl.ds(cond, 1), which * D_MODEL:(which + 1) * D_MODEL]


def _norm_mod(x, nw, shift, scale):
    y = x * lax.rsqrt(jnp.mean(x * x, axis=-1, keepdims=True) + EPS)
    return (y * nw) * (1.0 + scale) + shift


def _silu(x):
    return x * jax.nn.sigmoid(x)


def _ada_kernel(c_ref, w_ref, b_ref, o_ref, s_scr):
    @pl.when((pl.program_id(0) == 0) & (pl.program_id(1) == 0))
    def _():
        s_scr[...] = _silu(c_ref[...])

    tn = w_ref.shape[-1]
    cols = [[] for _ in range(N_COND_USED)]
    for j0 in range(0, tn // LANES, ADA_LANE_BLOCKS):
        acc = [[jnp.zeros((SUBLANES, LANES), F32) for _ in range(ADA_LANE_BLOCKS)] for _ in range(N_COND_USED)]
        for g in range(D_MODEL // SUBLANES):
            rows = slice(g * SUBLANES, (g + 1) * SUBLANES)
            w = [w_ref[0, rows, (j0 + j) * LANES:(j0 + j + 1) * LANES] for j in range(ADA_LANE_BLOCKS)]
            for r in range(N_COND_USED):
                s = s_scr[r, rows, :]
                for j in range(ADA_LANE_BLOCKS):
                    acc[r][j] = acc[r][j] + w[j] * s
        for r in range(N_COND_USED):
            cols[r] += [jnp.sum(a, axis=0, keepdims=True) for a in acc[r]]
    out = [jnp.concatenate(cols[r], axis=1) + b_ref[0] for r in range(N_COND_USED)]
    out.append(jnp.zeros((N_COND - N_COND_USED, tn), F32))
    o_ref[0] = jnp.concatenate(out, axis=0)


def _ada(cond, ada_w, ada_b):
    depth, _, n = ada_w.shape
    tn = 1536
    cond_cols = jnp.broadcast_to(cond[:, :, None], (N_COND_USED, D_MODEL, LANES))
    return pl.pallas_call(
        _ada_kernel,
        out_shape=jax.ShapeDtypeStruct((depth, N_COND, n), F32),
        grid=(depth, n // tn),
        in_specs=[
            pl.BlockSpec((N_COND_USED, D_MODEL, LANES), lambda l, j: (0, 0, 0)),
            pl.BlockSpec((1, D_MODEL, tn), lambda l, j: (l, 0, j)),
            pl.BlockSpec((1, 1, tn), lambda l, j: (l, 0, j)),
        ],
        out_specs=pl.BlockSpec((1, N_COND, tn), lambda l, j: (l, 0, j)),
        scratch_shapes=[pltpu.VMEM((N_COND_USED, D_MODEL, LANES), F32)],
        compiler_params=_params(40, "arbitrary", "arbitrary"),
        name="ada_modulation",
    )(cond_cols, ada_w, ada_b.reshape(depth, 1, n))


def _hg_in_kernel(xp_ref, xs_ref, mod_ref, nw_ref, w_ref, lbl_ref,
                  q_ref, v_ref, g_ref, lff_ref, kf_ref, lfb_ref, kb_ref):
    i = pl.program_id(0) * (HG_IN_TM // TM)
    cond = _tile_cond(i)
    shift, scale = _mod_row(mod_ref, cond, 0), _mod_row(mod_ref, cond, 1)

    def normed(rows):
        x = jnp.where(i < PROMPT_TILES, xp_ref[rows, :], xs_ref[rows, :])
        return _norm_mod(x, nw_ref[...], shift, scale).astype(BF16)

    def proj(h, c, cols):
        return jnp.dot(h, w_ref[:, c * D_MODEL + cols.start:c * D_MODEL + cols.stop], preferred_element_type=F32)

    l0, l1, l2 = lbl_ref[0], lbl_ref[1], lbl_ref[2]
    m = jnp.maximum(jnp.maximum(l0, l1), l2)
    e0, e1, e2 = jnp.exp(l0 - m), jnp.exp(l1 - m), jnp.exp(l2 - m)
    lb = e0 / (e0 + e1 + e2)
    col_tiles = [slice(n * HG_IN_TN, (n + 1) * HG_IN_TN) for n in range(D_MODEL // HG_IN_TN)]

    def gate_and_plain(h, rows, d):
        lf_ref, k_ref, plain_ref = ((lff_ref, kf_ref, q_ref), (lfb_ref, kb_ref, v_ref))[d]
        for cols in col_tiles:
            lbd = lb[d:d + 1, cols]
            z_gate = proj(h, 2 + d, cols)
            z_plain = proj(h, d, cols)
            f = lbd + (1.0 - lbd) * jax.nn.sigmoid(z_gate)
            lf = jnp.log(f)
            hi = lf.astype(BF16)
            lf_ref[0, rows, cols] = hi
            lf_ref[1, rows, cols] = (lf - hi.astype(F32)).astype(BF16)
            k_ref[rows, cols] = (1.0 - f).astype(BF16)
            plain_ref[rows, cols] = z_plain.astype(BF16)

    sub_tiles = [slice(s * HG_IN_SUB, (s + 1) * HG_IN_SUB) for s in range(HG_IN_TM // HG_IN_SUB)]
    h = normed(sub_tiles[0])
    for s, rows in enumerate(sub_tiles):
        gate_and_plain(h, rows, 0)
        h_next = normed(sub_tiles[s + 1]) if s + 1 < len(sub_tiles) else None
        gate_and_plain(h, rows, 1)
        for cols in col_tiles:
            g_ref[rows, cols] = proj(h, 4, cols).astype(BF16)
        h = h_next


def _hg_in(xp, xs, mod, nw, w_in, lb_logits):
    tile = lambda i: (i, 0)
    fixed2 = lambda i: (0, 0)
    bf = jax.ShapeDtypeStruct((N_TOK, D_MODEL), BF16)
    ff = jax.ShapeDtypeStruct((2, N_TOK, D_MODEL), BF16)
    blk = pl.BlockSpec((HG_IN_TM, D_MODEL), tile)
    split_blk = pl.BlockSpec((2, HG_IN_TM, D_MODEL), lambda i: (0, i, 0))
    prompt_steps = N_PROMPT // HG_IN_TM
    return pl.pallas_call(
        _hg_in_kernel,
        out_shape=(bf, bf, bf, ff, bf, ff, bf),
        grid=(N_TOK // HG_IN_TM,),
        in_specs=[
            pl.BlockSpec((HG_IN_TM, D_MODEL), lambda i: (jnp.minimum(i, prompt_steps - 1), 0)),
            pl.BlockSpec((HG_IN_TM, D_MODEL), lambda i: (jnp.maximum(i - prompt_steps, 0), 0)),
            pl.BlockSpec(mod.shape, fixed2),
            pl.BlockSpec((1, D_MODEL), fixed2),
            pl.BlockSpec(w_in.shape, fixed2, pipeline_mode=pl.Buffered(1)),
            pl.BlockSpec(lb_logits.shape, lambda i: (0, 0, 0)),
        ],
        out_specs=(blk, blk, blk, split_blk, blk, split_blk, blk),
        compiler_params=_params(56, "arbitrary"),
        name="hgrn_in_proj",
    )(xp, xs, mod, nw, w_in, lb_logits)


def _gla_exact(reverse, q_ref, v_ref, lf_ref, k_ref, st_ref, o_ref):
    nt = (((1,), (1,)), ((), ()))
    tn = (((0,), (0,)), ((), ()))
    n_chunks = TM // CHUNK
    row_id = lax.broadcasted_iota(jnp.int32, (CHUNK, HG_DK), 0)
    order = range(CHUNK - 1, -1, -1) if reverse else range(CHUNK)

    def chunk_head(it, carry):
        step, h = it // HG_HEADS, it % HG_HEADS
        ci = (n_chunks - 1 - step) if reverse else step
        rows = pl.ds(pl.multiple_of(ci * CHUNK, CHUNK), CHUNK)
        cols = pl.ds(pl.multiple_of(h * HG_DK, HG_DK), HG_DK)
        q, k, v = q_ref[rows, cols], k_ref[rows, cols], v_ref[rows, cols]
        f = jnp.exp(lf_ref[0, rows, cols].astype(F32) + lf_ref[1, rows, cols].astype(F32))
        st = st_ref[0, h]
        o = jnp.zeros((CHUNK, HG_DK), F32)
        for t in order:
            one = row_id == t
            v_t = jnp.where(one, v, jnp.zeros_like(v))
            st = st * f[t:t + 1, :] + lax.dot_general(v_t, k, tn, preferred_element_type=F32)
            o = jnp.where(one, lax.dot_general(q, st.astype(BF16), nt, preferred_element_type=F32), o)
        st_ref[0, h] = st
        o_ref[rows, cols] = o.astype(o_ref.dtype)
        return carry

    lax.fori_loop(0, n_chunks * HG_HEADS, chunk_head, 0)


class _GlaDirection(NamedTuple):
    reverse: bool
    q: Any
    v: Any
    lf: Any
    k: Any
    o: Any
    st: Any
    sw: Any
    ut: Any
    qd: Any
    kd: Any
    ki: Any
    cd: Any
    b: Any


_NT = (((1,), (1,)), ((), ()))
_TN = (((0,), (0,)), ((), ()))
_HG_COLS = [slice(h * HG_DK, (h + 1) * HG_DK) for h in range(HG_HEADS)]
_N_CHUNKS = TM // CHUNK
_N_PAIRS = _N_CHUNKS // 2


def _gla_prepare(d):
    r = lax.broadcasted_iota(jnp.int32, (TM, TM), 0)
    c = lax.broadcasted_iota(jnp.int32, (TM, TM), 1)
    same = (r // CHUNK) == (c // CHUNK)
    tri = (same & ((c >= r) if d.reverse else (c <= r))).astype(BF16)
    b = jnp.dot(tri, d.lf[0], preferred_element_type=F32) + jnp.dot(tri, d.lf[1], preferred_element_type=F32)
    d.b[...] = b
    edge = 0 if d.reverse else CHUNK - 1
    total = b.reshape(_N_CHUNKS, CHUNK, D_MODEL)[:, edge, :]
    d.cd[...] = jnp.exp(total)
    return jnp.max(-total) <= DECAY_CLAMP


def _gla_pair_rows(d, step):
    pi = (_N_PAIRS - 1 - step) if d.reverse else step
    row0 = pi * 2 * CHUNK
    lo, hi = (pl.ds(row0, CHUNK), 2 * pi), (pl.ds(row0 + CHUNK, CHUNK), 2 * pi + 1)
    return pl.ds(row0, 2 * CHUNK), ((hi, lo) if d.reverse else (lo, hi))


def _chunk_decay(d, chunk):
    return d.cd[pl.ds(chunk, 1), :]


def _gla_decayed_operands(d, step):
    both, ((_, c_first), (_, c_second)) = _gla_pair_rows(d, step)
    b = d.b[both, :]
    ki = d.k[both, :] * jnp.exp(jnp.minimum(-b, DECAY_CLAMP)).astype(BF16)
    d.qd[both, :] = d.q[both, :] * jnp.exp(b).astype(BF16)
    d.ki[both, :] = ki
    c_lo, c_hi = (c_second, c_first) if d.reverse else (c_first, c_second)
    ends = jnp.concatenate([jnp.broadcast_to(_chunk_decay(d, c).astype(BF16), (CHUNK, D_MODEL)) for c in (c_lo, c_hi)],
                           axis=0)
    d.kd[both, :] = ki * ends


def _rows_scaled(x, scale_row, second_half):
    scale = jnp.broadcast_to(scale_row.astype(BF16), (CHUNK, x.shape[1]))
    ones = jnp.ones((CHUNK, x.shape[1]), BF16)
    return x * jnp.concatenate([ones, scale] if second_half else [scale, ones], axis=0)


def _gla_key_value_product(d, step):
    _gla_decayed_operands(d, step)
    both, (_, (_, c_second)) = _gla_pair_rows(d, step)
    decay_second = _chunk_decay(d, c_second)
    for h, cols in enumerate(_HG_COLS):
        keys = _rows_scaled(d.kd[both, cols], decay_second[:, cols], second_half=d.reverse)
        d.ut[h] = lax.dot_general(d.v[both, cols], keys, _TN, preferred_element_type=F32)


def _gla_start(d):
    for h in range(HG_HEADS):
        d.sw[0, h] = d.st[0, h].T.astype(BF16)
    _gla_key_value_product(d, 0)


def _gla_pair(d, step, src, dst):
    sr = lax.broadcasted_iota(jnp.int32, (CHUNK, CHUNK), 0)
    sc = lax.broadcasted_iota(jnp.int32, (CHUNK, CHUNK), 1)
    tr = lax.broadcasted_iota(jnp.int32, (CHUNK, 2 * CHUNK), 0)
    tc = lax.broadcasted_iota(jnp.int32, (CHUNK, 2 * CHUNK), 1)
    if d.reverse:
        keep_first = sc >= sr
        keep_second = (tc >= tr) | (tc >= CHUNK)
    else:
        keep_first = sc <= sr
        keep_second = (tc < CHUNK) | (tc - CHUNK <= tr)
    both, ((first, c_first), (second, c_second)) = _gla_pair_rows(d, step)
    decay_first = _chunk_decay(d, c_first)
    decay_both = decay_first * _chunk_decay(d, c_second)
    for h, cols in enumerate(_HG_COLS):
        new = d.st[src, h] * decay_both[:, cols] + d.ut[h]
        d.st[dst, h] = new
        d.sw[dst, h] = new.T.astype(BF16)
    a_first = [lax.dot_general(d.qd[first, cols], d.ki[first, cols], _NT, preferred_element_type=F32)
               for cols in _HG_COLS]
    a_second = []
    for cols in _HG_COLS:
        lo_keys = (d.ki if d.reverse else d.kd)[pl.ds(both.start, CHUNK), cols]
        hi_keys = (d.kd if d.reverse else d.ki)[pl.ds(both.start + CHUNK, CHUNK), cols]
        keys = jnp.concatenate([lo_keys, hi_keys], axis=0)
        a_second.append(lax.dot_general(d.qd[second, cols], keys, _NT, preferred_element_type=F32))
    inter = [jnp.dot(_rows_scaled(d.qd[both, cols], decay_first[:, cols], second_half=not d.reverse),
                     d.sw[src, h], preferred_element_type=F32) for h, cols in enumerate(_HG_COLS)]
    if step + 1 < _N_PAIRS:
        _gla_key_value_product(d, step + 1)
    first_half, second_half = (slice(CHUNK, None), slice(None, CHUNK)) if d.reverse else \
                              (slice(None, CHUNK), slice(CHUNK, None))
    for h, cols in enumerate(_HG_COLS):
        am = jnp.where(keep_first, a_first[h], 0.0).astype(BF16)
        d.o[first, cols] = (jnp.dot(am, d.v[first, cols], preferred_element_type=F32)
                            + inter[h][first_half]).astype(d.o.dtype)
        am = jnp.where(keep_second, a_second[h], 0.0).astype(BF16)
        d.o[second, cols] = (jnp.dot(am, d.v[both, cols], preferred_element_type=F32)
                             + inter[h][second_half]).astype(d.o.dtype)


def _gla_kernel(qf_ref, vf_ref, lff_ref, kf_ref, qb_ref, vb_ref, lfb_ref, kb_ref, s0_ref,
                of_ref, ob_ref, sout_ref, st_scr, sw_scr, ut_scr, qd_scr, kd_scr, ki_scr, cd_scr, b_scr):
    i = pl.program_id(0)
    is_prompt = i < PROMPT_TILES
    first = jnp.logical_or(is_prompt, (i - PROMPT_TILES) % TILES_PER_DEC_SEQ == 0)
    work = [tuple(scr.at[n] for scr in (st_scr, sw_scr, ut_scr, qd_scr, kd_scr, ki_scr, cd_scr, b_scr))
            for n in range(2)]
    fwd = _GlaDirection(False, qf_ref, vf_ref, lff_ref, kf_ref, of_ref, *work[0])
    bwd = _GlaDirection(True, qb_ref, vb_ref, lfb_ref, kb_ref, ob_ref, *work[1])

    @pl.when(is_prompt)
    def _():
        for d in (fwd, bwd):
            d.st[0] = jnp.zeros((HG_HEADS, HG_DK, HG_DK), F32)

    @pl.when(jnp.logical_and(first, jnp.logical_not(is_prompt)))
    def _():
        for n, d in enumerate((fwd, bwd)):
            for h in range(HG_HEADS):
                d.st[0, h] = s0_ref[0, 0, n, h].T

    exact_scores = jnp.logical_and(_gla_prepare(fwd), _gla_prepare(bwd))

    @pl.when(exact_scores)
    def _():
        _gla_start(fwd)
        _gla_start(bwd)

        for step in range(_N_PAIRS):
            for d in (fwd, bwd):
                _gla_pair(d, step, step % 2, 1 - step % 2)

    @pl.when(jnp.logical_not(exact_scores))
    def _():
        for d in (fwd, bwd):
            _gla_exact(d.reverse, d.q, d.v, d.lf, d.k, d.st, d.o)

    @pl.when(is_prompt)
    def _():
        for n, d in enumerate((fwd, bwd)):
            for h in range(HG_HEADS):
                sout_ref[0, 0, n, h] = d.st[0, h].T


def _gla(q, v, lff, kf, lfb, kb, state_hgrn):
    def fwd_tile(i):
        return (i, 0)

    def bwd_tile(i):
        j = (i - PROMPT_TILES) % TILES_PER_DEC_SEQ
        return (jnp.where(i < PROMPT_TILES, i, i - j + (TILES_PER_DEC_SEQ - 1 - j)), 0)

    def s0_idx(i):
        return (jnp.maximum(i - PROMPT_TILES, 0) // TILES_PER_DEC_SEQ, 0, 0, 0, 0, 0)

    def sout_idx(i):
        return (jnp.minimum(i, PROMPT_TILES - 1), 0, 0, 0, 0, 0)

    f_blk = pl.BlockSpec((TM, D_MODEL), fwd_tile)
    b_blk = pl.BlockSpec((TM, D_MODEL), bwd_tile)
    f_split = pl.BlockSpec((2, TM, D_MODEL), lambda i: (0,) + fwd_tile(i))
    b_split = pl.BlockSpec((2, TM, D_MODEL), lambda i: (0,) + bwd_tile(i))
    st_blk = (1, 1, 2, HG_HEADS, HG_DK, HG_DK)
    return pl.pallas_call(
        _gla_kernel,
        out_shape=(
            jax.ShapeDtypeStruct((N_TOK, D_MODEL), BF16),
            jax.ShapeDtypeStruct((N_TOK, D_MODEL), BF16),
            jax.ShapeDtypeStruct((BATCH,) + st_blk[1:], F32),
        ),
        grid=(N_TILES,),
        in_specs=[f_blk, f_blk, f_split, f_blk, b_blk, b_blk, b_split, b_blk, pl.BlockSpec(st_blk, s0_idx)],
        out_specs=(f_blk, b_blk, pl.BlockSpec(st_blk, sout_idx)),
        scratch_shapes=[
            pltpu.VMEM((2, 2, HG_HEADS, HG_DK, HG_DK), F32),
            pltpu.VMEM((2, 2, HG_HEADS, HG_DK, HG_DK), BF16),
            pltpu.VMEM((2, HG_HEADS, HG_DK, HG_DK), F32),
            pltpu.VMEM((2, TM, D_MODEL), BF16),
            pltpu.VMEM((2, TM, D_MODEL), BF16),
            pltpu.VMEM((2, TM, D_MODEL), BF16),
            pltpu.VMEM((2, TM // CHUNK, D_MODEL), F32),
            pltpu.VMEM((2, TM, D_MODEL), F32),
        ],
        compiler_params=_params(48, "arbitrary"),
        name="hgrn_recurrence",
    )(q, v, lff, kf, q, v, lfb, kb, state_hgrn)


def _route(logits):
    lane = lax.broadcasted_iota(jnp.int32, logits.shape, 1)
    neg = jnp.float32(-jnp.inf)
    lane_f = lane.astype(F32)

    def first_max(x):
        m = jnp.max(x, axis=-1, keepdims=True)
        first = jnp.min(jnp.where(x == m, lane_f, float(ROUTER_LANES)), axis=-1, keepdims=True)
        return m, first.astype(jnp.int32)

    is_group = (lane >= N_EXPERTS) & (lane < N_EXPERTS + N_GROUPS)
    gl = jnp.where(is_group, logits, neg)
    gmax, g_lane = first_max(gl)
    g_sel = g_lane - N_EXPERTS
    gsum = jnp.sum(jnp.exp(gl - gmax), axis=-1, keepdims=True)
    p_g = 1.0 / gsum
    in_sel = (lane < N_EXPERTS) & ((lane // EXPERTS_PER_GROUP) == g_sel)
    m1, i1 = first_max(jnp.where(in_sel, logits, neg))
    m2, i2 = first_max(jnp.where(in_sel & (lane != i1), logits, neg))
    e2 = jnp.exp(m2 - m1)
    return i1, i2, p_g / (1.0 + e2), p_g * e2 / (1.0 + e2)


def _pack_rows(x):
    q = PACK_WORDS
    bits = pltpu.bitcast(x.astype(BF16).astype(F32), jnp.uint32)
    return [(bits[:, (2 + h) * q:(3 + h) * q] & jnp.uint32(0xFFFF0000)) | (bits[:, h * q:(h + 1) * q] >> 16)
            for h in range(2)]


def _unpack_rows(half0, half1):
    lo = lambda w: pltpu.bitcast(w << 16, F32).astype(BF16)
    hi = lambda w: pltpu.bitcast(w & jnp.uint32(0xFFFF0000), F32).astype(BF16)
    return [lo(half0), lo(half1), hi(half0), hi(half1)]


def _mix_stream_rows(prompt_ref, sample_ref, rows):
    return jnp.where(pl.program_id(0) < N_PROMPT // MIX_TM, prompt_ref[rows, :], sample_ref[rows, :])


def _mixer_tail(mix_of, x_of, mod_ref, nw2_ref, wo_ref, wr_ref, br_ref, x1_ref, h2p_ref, lg_ref):
    cond = _tile_cond(pl.program_id(0) * (MIX_TM // TM))
    gate, shift, scale = (_mod_row(mod_ref, cond, which) for which in (2, 3, 4))
    project = lambda rows: jnp.dot(mix_of(rows), wo_ref[...], preferred_element_type=F32)

    def tail(rows, out):
        x1 = x_of(rows) + gate * out
        x1_ref[rows, :] = x1
        h2 = _norm_mod(x1, nw2_ref[...], shift, scale)
        h2p_ref[rows, :] = jnp.concatenate(_pack_rows(h2), axis=1)
        h_hi = h2.astype(BF16)
        h_lo = (h2 - h_hi.astype(F32)).astype(BF16)
        lg_ref[rows, :] = (jnp.dot(h_hi, wr_ref[0], preferred_element_type=F32)
                           + (jnp.dot(h_hi, wr_ref[1], preferred_element_type=F32)
                              + jnp.dot(h_lo, wr_ref[0], preferred_element_type=F32))) + br_ref[...]

    sub_tiles = [slice(s * TM, (s + 1) * TM) for s in range(MIX_TM // TM)]
    out = project(sub_tiles[0])
    for s, rows in enumerate(sub_tiles):
        out_next = project(sub_tiles[s + 1]) if s + 1 < len(sub_tiles) else None
        tail(rows, out)
        out = out_next


def _route_kernel(lg_ref, rt_ref, rw_ref, cnt_ref, carry_scr):
    @pl.when(pl.program_id(0) == 0)
    def _():
        carry_scr[...] = jnp.zeros_like(carry_scr)

    logits = lg_ref[...]
    i1, i2, w1, w2 = _route(logits)
    lane = lax.broadcasted_iota(jnp.int32, logits.shape, 1)
    chosen = ((lane == i1) | (lane == i2)).astype(BF16)
    r = lax.broadcasted_iota(jnp.int32, (ROUTE_TM, ROUTE_TM), 0)
    c = lax.broadcasted_iota(jnp.int32, (ROUTE_TM, ROUTE_TM), 1)
    before = jnp.dot((c < r).astype(BF16), chosen, preferred_element_type=F32) + carry_scr[...]
    r1 = jnp.sum(jnp.where(lane == i1, before, 0.0), axis=-1, keepdims=True).astype(jnp.int32)
    r2 = jnp.sum(jnp.where(lane == i2, before, 0.0), axis=-1, keepdims=True).astype(jnp.int32)
    total = carry_scr[...] + jnp.sum(chosen.astype(F32), axis=0, keepdims=True)
    carry_scr[...] = total
    tiles = jnp.ceil(total * (1.0 / MOE_TILE))
    e_from = lax.broadcasted_iota(jnp.int32, (ROUTER_LANES, ROUTER_LANES), 0)
    e_to = lax.broadcasted_iota(jnp.int32, (ROUTER_LANES, ROUTER_LANES), 1)
    tiles_before = jnp.dot(jnp.broadcast_to(tiles, (SUBLANES, ROUTER_LANES)).astype(BF16),
                           (e_from < e_to).astype(BF16), preferred_element_type=F32)
    cnt_ref[...] = jnp.concatenate([total, tiles_before[:1] * MOE_TILE,
                                    jnp.zeros((ROUTE_ROWS - 2, ROUTER_LANES), F32)], axis=0)
    by_token = jnp.where(lane == 0, i1, jnp.where(lane == 1, i2, jnp.where(lane == 2, r1, r2)))
    rt_ref[...] = by_token.T[:ROUTE_ROWS, :]
    rw_ref[...] = jnp.where(lane == 0, w1, w2)


def _route_tokens(logits):
    blk = pl.BlockSpec((ROUTE_TM, ROUTER_LANES), lambda i: (i, 0))
    return pl.pallas_call(
        _route_kernel,
        out_shape=(
            jax.ShapeDtypeStruct((ROUTE_ROWS, N_TOK), jnp.int32),
            jax.ShapeDtypeStruct((N_TOK, ROUTER_LANES), F32),
            jax.ShapeDtypeStruct((ROUTE_ROWS, ROUTER_LANES), F32),
        ),
        grid=(N_TOK // ROUTE_TM,),
        in_specs=[blk],
        out_specs=(pl.BlockSpec((ROUTE_ROWS, ROUTE_TM), lambda i: (0, i)), blk,
                   pl.BlockSpec((ROUTE_ROWS, ROUTER_LANES), lambda i: (0, 0))),
        scratch_shapes=[pltpu.VMEM((1, ROUTER_LANES), F32)],
        compiler_params=_params(32, "arbitrary"),
        name="moe_route",
    )(logits)


def _hg_out_kernel(of_ref, ob_ref, g_ref, on_ref, xp_ref, xs_ref, *rest):
    def mix_of(rows):
        o = of_ref[rows, :].astype(F32) + ob_ref[rows, :].astype(F32)
        parts = []
        for h in range(HG_HEADS):
            oh = o[:, h * HG_DK:(h + 1) * HG_DK]
            parts.append(oh * lax.rsqrt(jnp.mean(oh * oh, axis=-1, keepdims=True) + EPS) * on_ref[...])
        y = jnp.concatenate(parts, axis=1) * _silu(g_ref[rows, :].astype(F32))
        return y.astype(BF16)

    _mixer_tail(mix_of, functools.partial(_mix_stream_rows, xp_ref, xs_ref), *rest)


def _at_out_kernel(ac_ref, al_ref, x_ref, *rest):
    _mixer_tail(functools.partial(_mix_stream_rows, ac_ref, al_ref), lambda rows: x_ref[rows, :], *rest)


def _mixer_out(kernel_fn, name, mix_inputs, mix_specs, mod, nw2, w_out, w_router, b_router):
    tile = lambda i: (i, 0)
    fixed2 = lambda i: (0, 0)
    blk = pl.BlockSpec((MIX_TM, D_MODEL), tile)
    lanes_blk = pl.BlockSpec((MIX_TM, ROUTER_LANES), tile)
    return pl.pallas_call(
        kernel_fn,
        out_shape=(
            jax.ShapeDtypeStruct((N_TOK, D_MODEL), F32),
            jax.ShapeDtypeStruct((N_TOK, ROW_WORDS), jnp.uint32),
            jax.ShapeDtypeStruct((N_TOK, ROUTER_LANES), F32),
        ),
        grid=(N_TOK // MIX_TM,),
        in_specs=list(mix_specs) + [
            pl.BlockSpec(mod.shape, fixed2),
            pl.BlockSpec((1, D_MODEL), fixed2),
            pl.BlockSpec(w_out.shape, fixed2),
            pl.BlockSpec(w_router.shape, lambda i: (0, 0, 0)),
            pl.BlockSpec((1, ROUTER_LANES), fixed2),
        ],
        out_specs=(blk, pl.BlockSpec((MIX_TM, ROW_WORDS), tile), lanes_blk),
        compiler_params=_params(40, "arbitrary"),
        name=name,
    )(*mix_inputs, mod, nw2, w_out, w_router, b_router)


def _moe_plan(route_t, counts):
    cnt = counts[0, :N_EXPERTS].astype(jnp.int32)
    offs = counts[1, :N_EXPERTS].astype(jnp.int32)
    ends = offs + ((cnt + MOE_TILE - 1) // MOE_TILE) * MOE_TILE
    experts = route_t[0:2]
    pos = jnp.sum(jnp.where(experts[None, :, :] == jnp.arange(N_EXPERTS)[:, None, None], offs[:, None, None], 0),
                  axis=0) + route_t[2:4]
    tile_start = jnp.arange(MOE_ROWS // MOE_TILE, dtype=jnp.int32) * MOE_TILE
    tile_expert = jnp.minimum(jnp.sum(ends[None, :] <= tile_start[:, None], axis=1), N_EXPERTS - 1).astype(jnp.int32)
    of_tile = tile_expert[:, None] == jnp.arange(N_EXPERTS)[None, :]
    tile_offs = jnp.sum(jnp.where(of_tile, offs[None, :], 0), axis=1)
    tile_cnt = jnp.sum(jnp.where(of_tile, cnt[None, :], 0), axis=1)
    tile_rows = jnp.clip(tile_offs + tile_cnt - tile_start, 0, MOE_TILE).astype(jnp.int32)
    active = tile_start < ends[-1]
    tile_opens = jnp.where(active & (tile_start == tile_offs), tile_expert, -1).astype(jnp.int32)
    tile_opens = jnp.concatenate([tile_opens, jnp.full((MOE_STEP_TILES,), -1, jnp.int32)])
    n_active = (ends[-1] // MOE_TILE).astype(jnp.int32).reshape(1)
    slot = (jnp.cumsum((cnt > 0).astype(jnp.int32)) - 1) % MOE_WEIGHT_SLOTS
    tile_slot = jnp.sum(jnp.where(of_tile, slot[None, :], 0), axis=1)
    last_slot = jnp.sum(jnp.where(tile_start == ends[-1] - MOE_TILE, tile_slot, 0))
    tile_slot = jnp.where(active, tile_slot, last_slot).astype(jnp.int32)
    return pos.astype(jnp.int32), (tile_rows, tile_slot, tile_opens, n_active)


SC_WINDOW = 64


def _sc_mesh():
    return plsc.VectorSubcoreMesh(core_axis_name="c", subcore_axis_name="s")


def _sc_scatter_rows(x, idx_a, idx_b, n_out_rows):
    n = x.shape[0]

    @functools.partial(pl.kernel, out_type=jax.ShapeDtypeStruct((n_out_rows, ROW_WORDS), x.dtype), mesh=_sc_mesh(),
                       scratch_types=[pltpu.SemaphoreType.DMA, pltpu.SemaphoreType.DMA])
    def scatter(x_hbm, ia_hbm, ib_hbm, o_hbm, sem_a, sem_b):
        def body(x_vmem, ia_vmem, ib_vmem):
            copy_a = pltpu.async_copy(x_vmem, o_hbm.at[ia_vmem.at[0]], sem_a)
            copy_b = pltpu.async_copy(x_vmem, o_hbm.at[ib_vmem.at[0]], sem_b)
            copy_a.wait()
            copy_b.wait()

        idx_spec = pl.BlockSpec((1, SC_WINDOW), index_map=lambda i: (i, 0))
        pltpu.emit_pipeline(
            body, grid=(n // SC_WINDOW,),
            in_specs=[pl.BlockSpec((SC_WINDOW, ROW_WORDS), index_map=lambda i: (i, 0)), idx_spec, idx_spec],
            out_specs=[],
            core_axis_name=("c", "s"), dimension_semantics=(pltpu.PARALLEL,),
        )(x_hbm, ia_hbm, ib_hbm)

    return scatter(x, idx_a, idx_b)


def _sc_gather_rows(table, idx):
    n = idx.size

    @functools.partial(pl.kernel, out_type=jax.ShapeDtypeStruct((n, ROW_WORDS), table.dtype), mesh=_sc_mesh())
    def gather(t_hbm, i_hbm, o_hbm):
        def body(i_vmem, o_vmem):
            pltpu.sync_copy(t_hbm.at[i_vmem.at[0]], o_vmem)

        pltpu.emit_pipeline(
            body, grid=(n // SC_WINDOW,),
            in_specs=[pl.BlockSpec((1, SC_WINDOW), index_map=lambda i: (i, 0))],
            out_specs=[pl.BlockSpec((SC_WINDOW, ROW_WORDS), index_map=lambda i: (i, 0))],
            core_axis_name=("c", "s"), dimension_semantics=(pltpu.PARALLEL,),
        )(i_hbm, o_hbm)

    return gather(table, idx)


def _ffn_kernel(tr_ref, ts_ref, to_ref, na_ref, xs_ref, w1_hbm, w3_hbm, w2_hbm, ys_ref,
                w1_buf, w3_buf, w2_buf, sem, *, layer):
    def weight_copies(t):
        pairs = ((w1_hbm, w1_buf), (w3_hbm, w3_buf), (w2_hbm, w2_buf))
        return [pltpu.make_async_copy(hbm.at[layer, to_ref[t]], buf.at[ts_ref[t]], sem.at[ts_ref[t], j])
                for j, (hbm, buf) in enumerate(pairs)]

    def for_experts_opened_from(first_tile, action):
        for j in range(MOE_STEP_TILES):
            t = first_tile + j

            @pl.when(to_ref[t] >= 0)
            def _():
                for copy in weight_copies(t):
                    action(copy)

    def tile(t, rows):
        slot = ts_ref[t]
        row = lax.broadcasted_iota(jnp.int32, (MOE_TILE, PACK_WORDS), 0)
        live = row < tr_ref[t]
        halves = [jnp.where(live, xs_ref[rows, h * PACK_WORDS:(h + 1) * PACK_WORDS],
                            jnp.zeros((MOE_TILE, PACK_WORDS), jnp.uint32)) for h in range(2)]
        chunks = _unpack_rows(*halves)

        def up(w_buf):
            acc = None
            for k, chunk in enumerate(chunks):
                w = w_buf[slot, k * PACK_WORDS:(k + 1) * PACK_WORDS, :].astype(BF16)
                part = jnp.dot(chunk, w, preferred_element_type=F32)
                acc = part if acc is None else acc + part
            return acc

        hid = (_silu(up(w1_buf)) * up(w3_buf)).astype(BF16)
        y = jnp.dot(hid, w2_buf[slot].astype(BF16), preferred_element_type=F32)
        ys_ref[rows, :] = jnp.concatenate(_pack_rows(y), axis=1)

    step = pl.program_id(0)
    first_tile = step * MOE_STEP_TILES

    @pl.when(first_tile < na_ref[0])
    def _():
        @pl.when(step == 0)
        def _():
            for_experts_opened_from(first_tile, lambda copy: copy.start())

        for_experts_opened_from(first_tile + MOE_STEP_TILES, lambda copy: copy.start())
        for_experts_opened_from(first_tile, lambda copy: copy.wait())
        for j in range(MOE_STEP_TILES):
            tile(first_tile + j, slice(j * MOE_TILE, (j + 1) * MOE_TILE))


def _ffn(xs, tables, layer, w1, w3, w2):
    step_rows = MOE_STEP_TILES * MOE_TILE
    row_tile = lambda s, tr, ts, to, na: (jnp.minimum(s, (na[0] - 1) // MOE_STEP_TILES), 0)
    hbm = pl.BlockSpec(memory_space=pl.ANY)
    return pl.pallas_call(
        functools.partial(_ffn_kernel, layer=layer),
        out_shape=jax.ShapeDtypeStruct((MOE_ROWS, ROW_WORDS), jnp.uint32),
        grid_spec=pltpu.PrefetchScalarGridSpec(
            num_scalar_prefetch=len(tables),
            grid=(MOE_ROWS // step_rows,),
            in_specs=[pl.BlockSpec((step_rows, ROW_WORDS), row_tile), hbm, hbm, hbm],
            out_specs=pl.BlockSpec((step_rows, ROW_WORDS), row_tile),
            scratch_shapes=[
                pltpu.VMEM((MOE_WEIGHT_SLOTS, D_MODEL, D_EXPERT), F32),
                pltpu.VMEM((MOE_WEIGHT_SLOTS, D_MODEL, D_EXPERT), F32),
                pltpu.VMEM((MOE_WEIGHT_SLOTS, D_EXPERT, D_MODEL), F32),
                pltpu.SemaphoreType.DMA((MOE_WEIGHT_SLOTS, 3)),
            ],
        ),
        compiler_params=_params(48, "arbitrary"),
        name="moe_experts",
    )(*tables, xs, w1, w3, w2)


def _moe_res(x_ref, y_ref, rw_ref, mod_ref, cond, rows=slice(None)):
    rw = rw_ref[rows, :]
    wa = rw[:, 0:1]
    wb = rw[:, 1:2]
    ya = _unpack_rows(y_ref[0, rows, :PACK_WORDS], y_ref[0, rows, PACK_WORDS:])
    yb = _unpack_rows(y_ref[1, rows, :PACK_WORDS], y_ref[1, rows, PACK_WORDS:])
    y = jnp.concatenate([wa * a.astype(F32) + wb * b.astype(F32) for a, b in zip(ya, yb)], axis=1)
    return x_ref[rows, :] + _mod_row(mod_ref, cond, 5) * y


def _moe_res_final_kernel(x_ref, y_ref, rw_ref, mod_ref, fn_ref, op_ref, os_ref):
    x = _moe_res(x_ref, y_ref, rw_ref, mod_ref, _tile_cond(pl.program_id(0)))
    y = x * lax.rsqrt(jnp.mean(x * x, axis=-1, keepdims=True) + EPS) * fn_ref[...]
    is_prompt = pl.program_id(0) < PROMPT_TILES

    @pl.when(is_prompt)
    def _():
        op_ref[...] = y

    @pl.when(jnp.logical_not(is_prompt))
    def _():
        os_ref[...] = y


def _moe_combine(x1, y_pairs, route_w, mod, final_norm):
    tile = lambda i: (i, 0)
    fixed2 = lambda i: (0, 0)
    return pl.pallas_call(
        _moe_res_final_kernel,
        out_shape=(jax.ShapeDtypeStruct((N_PROMPT, D_MODEL), F32),
                   jax.ShapeDtypeStruct((N_TOK - N_PROMPT, D_MODEL), F32)),
        grid=(N_TILES,),
        in_specs=[pl.BlockSpec((TM, D_MODEL), tile),
                  pl.BlockSpec((2, TM, ROW_WORDS), lambda i: (0, i, 0)),
                  pl.BlockSpec((TM, ROUTER_LANES), tile),
                  pl.BlockSpec(mod.shape, fixed2),
                  pl.BlockSpec((1, D_MODEL), fixed2)],
        out_specs=(pl.BlockSpec((TM, D_MODEL), _prompt_tile), pl.BlockSpec((TM, D_MODEL), _sample_tile)),
        compiler_params=_params(32, "arbitrary"),
        name="moe_combine",
    )(x1, y_pairs, route_w, mod, final_norm)


def _moe_layer(x1, h2p, logits, layer, w1, w3, w2):
    route_t, route_w, counts = _route_tokens(logits)
    pos, tables = _moe_plan(route_t, counts)
    windows = pos.reshape(2, N_TOK // SC_WINDOW, SC_WINDOW)
    xs = _sc_scatter_rows(h2p, windows[0], windows[1], MOE_ROWS)
    ys = _ffn(xs, tables, layer, w1, w3, w2)
    y_pairs = _sc_gather_rows(ys, windows.reshape(2 * N_TOK // SC_WINDOW, SC_WINDOW))
    return x1, y_pairs.reshape(2, N_TOK, ROW_WORDS), route_w


def _swap_rotary_halves(x):
    n = x.shape[-1]
    lane = lax.broadcasted_iota(jnp.int32, x.shape, 1)
    quarter = ROPE_HALF // 2
    return jnp.where((lane % ROPE_HALF) < quarter, pltpu.roll(x, n - quarter, 1), pltpu.roll(x, quarter, 1))


def _at_in_kernel(x1_ref, y_ref, rw_ref, mod_prev_ref, mod_ref, nw_ref, w_ref, cos_ref, sin_ref,
                  x_ref, q_ref, kt_ref, v_ref, kc_ref, vc_ref):
    step = pl.program_id(0)
    cond = _tile_cond(step * (AT_IN_TM // TM))
    shift, scale = _mod_row(mod_ref, cond, 0), _mod_row(mod_ref, cond, 1)
    nq = ATT_Q_HEADS * ATT_HEAD_DIM
    nk = ATT_KV_HEADS * ATT_HEAD_DIM

    def normed(rows):
        x = _moe_res(x1_ref, y_ref, rw_ref, mod_prev_ref, cond, rows)
        x_ref[rows, :] = x
        return _norm_mod(x, nw_ref[...], shift, scale).astype(BF16)

    def rope(x, rows):
        reps = x.shape[-1] // LANES
        cos = jnp.concatenate([cos_ref[rows, :]] * reps, axis=1)
        sin = jnp.concatenate([sin_ref[rows, :]] * reps, axis=1)
        return x * cos + _swap_rotary_halves(x) * sin

    def project_q(h):
        return jnp.dot(h, w_ref[:, :nq], preferred_element_type=F32)

    def project_kv(h):
        return (jnp.dot(h, w_ref[:, nq:nq + nk], preferred_element_type=F32),
                jnp.dot(h, w_ref[:, nq + nk:], preferred_element_type=F32))

    def tail(rows, zq, zk, v):
        q_ref[rows, :] = (rope(zq, rows) * ATT_HEAD_DIM ** -0.5).astype(BF16)
        kt = rope(zk, rows).T
        kt_ref[:, rows] = kt.astype(BF16)
        v_ref[rows, :] = v.astype(BF16)
        return kt, v.T

    sub_tiles = [slice(s * TM, (s + 1) * TM) for s in range(AT_IN_TM // TM)]
    cache_rows = []
    h = normed(sub_tiles[0])
    zq = project_q(h)
    for s, rows in enumerate(sub_tiles):
        last = s + 1 == len(sub_tiles)
        h_next = None if last else normed(sub_tiles[s + 1])
        zk, v = project_kv(h)
        zq_next = None if last else project_q(h_next)
        cache_rows.append(tail(rows, zq, zk, v))
        h, zq = h_next, zq_next

    @pl.when(step < N_PROMPT // AT_IN_TM)
    def _():
        for rows, (kt, vt) in zip(sub_tiles, cache_rows):
            kc_ref[rows, :] = kt
            vc_ref[rows, :] = vt


def _rope_tables():
    f32 = np.float32
    pos = np.arange(DEC_SEQ)
    t_row = (pos // GRID_W).astype(f32)
    t_col = (pos % GRID_W).astype(f32)
    inv = f32(ROPE_BASE) ** (-np.arange(0, ROPE_HALF, 2, dtype=f32) / f32(ROPE_HALF))
    j = np.arange(LANES) % ATT_HEAD_DIM
    freq = inv[(j % ROPE_HALF) % (ROPE_HALF // 2)]
    ang = (np.where((j < ROPE_HALF)[None, :], t_row[:, None], t_col[:, None]) * freq[None, :]).astype(f32)
    sign = np.where((j % ROPE_HALF) < ROPE_HALF // 2, -1.0, 1.0).astype(f32)
    cos = np.concatenate([np.ones((AT_IN_TM, LANES), f32), np.cos(ang)], axis=0)
    sin = np.concatenate([np.zeros((AT_IN_TM, LANES), f32), np.sin(ang) * sign[None, :]], axis=0)
    return jnp.asarray(cos, F32), jnp.asarray(sin, F32)


def _at_in(x1, y_pairs, route_w, mod_prev, mod, nw, w_in, cos, sin):
    tile = lambda i: (i, 0)
    fixed2 = lambda i: (0, 0)
    prompt_steps = N_PROMPT // AT_IN_TM
    rope_tile = lambda i: (jnp.where(i < prompt_steps, 0, 1 + (i - prompt_steps) % (DEC_SEQ // AT_IN_TM)), 0)
    prompt_tile = lambda i: (jnp.minimum(i, prompt_steps - 1), 0)
    nk = ATT_KV_HEADS * ATT_HEAD_DIM
    return pl.pallas_call(
        _at_in_kernel,
        out_shape=(
            jax.ShapeDtypeStruct((N_TOK, D_MODEL), F32),
            jax.ShapeDtypeStruct((N_TOK, D_MODEL), BF16),
            jax.ShapeDtypeStruct((nk, N_TOK), BF16),
            jax.ShapeDtypeStruct((N_TOK, nk), BF16),
            jax.ShapeDtypeStruct((N_PROMPT, nk), F32),
            jax.ShapeDtypeStruct((N_PROMPT, nk), F32),
        ),
        grid=(N_TOK // AT_IN_TM,),
        in_specs=[
            pl.BlockSpec((AT_IN_TM, D_MODEL), tile),
            pl.BlockSpec((2, AT_IN_TM, ROW_WORDS), lambda i: (0, i, 0)),
            pl.BlockSpec((AT_IN_TM, ROUTER_LANES), tile),
            pl.BlockSpec(mod_prev.shape, fixed2),
            pl.BlockSpec(mod.shape, fixed2),
            pl.BlockSpec((1, D_MODEL), fixed2),
            pl.BlockSpec(w_in.shape, fixed2),
            pl.BlockSpec((AT_IN_TM, LANES), rope_tile),
            pl.BlockSpec((AT_IN_TM, LANES), rope_tile),
        ],
        out_specs=(pl.BlockSpec((AT_IN_TM, D_MODEL), tile), pl.BlockSpec((AT_IN_TM, D_MODEL), tile),
                   pl.BlockSpec((nk, AT_IN_TM), lambda i: (0, i)),
                   pl.BlockSpec((AT_IN_TM, nk), tile),
                   pl.BlockSpec((AT_IN_TM, nk), prompt_tile), pl.BlockSpec((AT_IN_TM, nk), prompt_tile)),
        compiler_params=_params(40, "arbitrary"),
        name="attn_in_proj",
    )(x1, y_pairs, route_w, mod_prev, mod, nw, w_in, cos, sin)


def _attend(q, kt_all, v_all, mask, sink_ref, o_ref):
    nq = q.shape[0]
    group_lanes = ATT_GROUP * ATT_HEAD_DIM
    lane = lax.broadcasted_iota(jnp.int32, (nq, group_lanes), 1)
    mine = [(lane // ATT_HEAD_DIM) == g for g in range(ATT_GROUP)]
    row_head = lax.broadcasted_iota(jnp.int32, (ATT_GROUP * nq, 1), 0) // nq
    if mask is not None:
        mask = jnp.concatenate([mask] * ATT_GROUP, axis=0)

    def scores(hk):
        kt = jnp.concatenate([kt_all[hk * ATT_HEAD_DIM:(hk + 1) * ATT_HEAD_DIM, :]] * ATT_GROUP, axis=0)
        qg = q[:, hk * group_lanes:(hk + 1) * group_lanes]
        q_stack = jnp.concatenate([jnp.where(mine[g], qg, jnp.zeros_like(qg)) for g in range(ATT_GROUP)], axis=0)
        s = jnp.dot(q_stack, kt, preferred_element_type=F32)
        return s if mask is None else jnp.where(mask, s, -jnp.inf)

    s_next = scores(0)
    for hk in range(ATT_KV_HEADS):
        s = s_next
        if hk + 1 < ATT_KV_HEADS:
            s_next = scores(hk + 1)
        vh = v_all[:, hk * ATT_HEAD_DIM:(hk + 1) * ATT_HEAD_DIM]
        vt = jnp.concatenate([vh] * ATT_GROUP, axis=1)
        sink = jnp.zeros((ATT_GROUP * nq, 1), F32)
        for g in range(ATT_GROUP):
            sink = jnp.where(row_head == g, sink_ref[hk * ATT_GROUP + g], sink)
        m = jnp.maximum(jnp.max(s, axis=-1, keepdims=True), sink)
        p = jnp.exp(s - m)
        denom = jnp.sum(p, axis=-1, keepdims=True) + jnp.exp(sink - m)
        o = jnp.dot(p.astype(BF16), vt, preferred_element_type=F32) / denom
        acc = jnp.where(mine[0], o[:nq], 0.0)
        for g in range(1, ATT_GROUP):
            acc = acc + jnp.where(mine[g], o[g * nq:(g + 1) * nq], 0.0)
        o_ref[:, hk * group_lanes:(hk + 1) * group_lanes] = acc.astype(BF16)


def _ctx_attn_kernel(sink_ref, q_ref, k_ref, v_ref, o_ref):
    _attend(q_ref[...], k_ref[...], v_ref[...], None, sink_ref, o_ref)


def _lat_attn_kernel(sink_ref, q_ref, kp_ref, kc_ref, kn_ref, vp_ref, vc_ref, vn_ref, ck_ref, cv_ref, o_ref):
    jb = pl.program_id(1)
    kt_all = jnp.concatenate([kp_ref[...], kc_ref[...], kn_ref[...], ck_ref[0].astype(BF16)], axis=1)
    v_all = jnp.concatenate([vp_ref[...], vc_ref[...], vn_ref[...], cv_ref[0].astype(BF16)], axis=0)
    nkeys = 3 * BLOCK + PAST_LEN
    qi = lax.broadcasted_iota(jnp.int32, (BLOCK, nkeys), 0)
    kj = lax.broadcasted_iota(jnp.int32, (BLOCK, nkeys), 1)
    qpos = jb * BLOCK + qi
    kpos = (jb - 1) * BLOCK + kj
    local_ok = (jnp.abs(qpos - kpos) <= WINDOW) & (kpos >= 0) & (kpos < DEC_SEQ)
    mask = (kj >= 3 * BLOCK) | local_ok
    _attend(q_ref[...], kt_all, v_all, mask, sink_ref, o_ref)


def _attention(q, kt, v, cache_kt, cache_v, sink):
    nk = ATT_KV_HEADS * ATT_HEAD_DIM
    smem = pl.BlockSpec(memory_space=pltpu.SMEM)
    ctx = pl.pallas_call(
        _ctx_attn_kernel,
        out_shape=jax.ShapeDtypeStruct((N_PROMPT, D_MODEL), BF16),
        grid=(BATCH,),
        in_specs=[
            smem,
            pl.BlockSpec((SEQ, D_MODEL), lambda b: (b, 0)),
            pl.BlockSpec((nk, SEQ), lambda b: (0, b)),
            pl.BlockSpec((SEQ, nk), lambda b: (b, 0)),
        ],
        out_specs=pl.BlockSpec((SEQ, D_MODEL), lambda b: (b, 0)),
        compiler_params=_params(40, "arbitrary"),
        name="context_attention",
    )(sink, q, kt, v)

    nb = DEC_SEQ // BLOCK
    base = N_PROMPT // BLOCK
    cur = lambda b, j: (base + b * nb + j, 0)
    prev = lambda b, j: (base + b * nb + jnp.maximum(j - 1, 0), 0)
    nxt = lambda b, j: (base + b * nb + jnp.minimum(j + 1, nb - 1), 0)
    kv_blk = lambda f: pl.BlockSpec((BLOCK, nk), f)
    kt_blk = lambda f: pl.BlockSpec((nk, BLOCK), lambda b, j: f(b, j)[::-1])
    cache_blk = pl.BlockSpec((1, PAST_LEN, nk), lambda b, j: (b, 0, 0))
    cache_kt_blk = pl.BlockSpec((1, nk, PAST_LEN), lambda b, j: (b, 0, 0))
    lat = pl.pallas_call(
        _lat_attn_kernel,
        out_shape=jax.ShapeDtypeStruct((DEC_BATCH * DEC_SEQ, D_MODEL), BF16),
        grid=(DEC_BATCH, nb),
        in_specs=[
            smem,
            pl.BlockSpec((BLOCK, D_MODEL), cur),
            kt_blk(prev), kt_blk(cur), kt_blk(nxt),
            kv_blk(prev), kv_blk(cur), kv_blk(nxt),
            cache_kt_blk, cache_blk,
        ],
        out_specs=pl.BlockSpec((BLOCK, D_MODEL), lambda b, j: (b * nb + j, 0)),
        compiler_params=_params(40, "arbitrary", "arbitrary"),
        name="latent_attention",
    )(sink, q, kt, kt, kt, v, v, v, cache_kt, cache_v)
    return ctx, lat


def kernel(x_prompt, x_sample, state_hgrn, cache_k, cache_v, c, c_ctx, ada_w, ada_b, norm_w, hg_w_in,
           hg_lb_logits, hg_onorm, hg_w_out, at_w_in, at_sink, at_w_out, moe_w_group, moe_b_group,
           moe_w_expert, moe_b_expert, moe_w1, moe_w3, moe_w2, final_norm):
    xp = x_prompt.reshape(N_PROMPT, D_MODEL)
    xs = x_sample.reshape(N_TOK - N_PROMPT, D_MODEL)
    cond = jnp.concatenate([c_ctx[None, :], c], axis=0)
    mod = _ada(cond, ada_w, ada_b)
    nk = ATT_KV_HEADS * ATT_HEAD_DIM

    def router_params(i):
        pad = jnp.zeros((D_MODEL, ROUTER_LANES - N_EXPERTS - N_GROUPS), F32)
        w = jnp.concatenate([moe_w_expert[i], moe_w_group[i], pad], axis=1)
        b = jnp.concatenate([moe_b_expert[i], moe_b_group[i], pad[0]])[None, :]
        hi = w.astype(BF16)
        lo = (w - hi.astype(F32)).astype(BF16)
        return jnp.stack([hi, lo]), b

    mix_blk = pl.BlockSpec((MIX_TM, D_MODEL), lambda i: (i, 0))
    mix_prompt_blk = pl.BlockSpec((MIX_TM, D_MODEL), lambda i: (jnp.minimum(i, N_PROMPT // MIX_TM - 1), 0))
    mix_sample_blk = pl.BlockSpec((MIX_TM, D_MODEL), lambda i: (jnp.maximum(i - N_PROMPT // MIX_TM, 0), 0))
    q, v, g, lff, kf, lfb, kb = _hg_in(xp, xs, mod[0], norm_w[0, 0][None, :], hg_w_in[0].astype(BF16), hg_lb_logits)
    o_f, o_b, state_new = _gla(q, v, lff, kf, lfb, kb, state_hgrn)
    wr, br = router_params(0)
    routed = _mixer_out(
        _hg_out_kernel, "hgrn_out_route", (o_f, o_b, g, hg_onorm[0][None, :], xp, xs),
        (mix_blk, mix_blk, mix_blk, pl.BlockSpec((1, HG_DK), lambda i: (0, 0)), mix_prompt_blk, mix_sample_blk),
        mod[0], norm_w[0, 1][None, :], hg_w_out[0].astype(BF16), wr, br)
    moe_out = _moe_layer(*routed, 0, moe_w1, moe_w3, moe_w2)

    cos, sin = _rope_tables()
    x, qa, ka, va, k_ctx, v_ctx = _at_in(*moe_out, mod[0], mod[1], norm_w[1, 0][None, :],
                                         at_w_in[0].astype(BF16), cos, sin)
    attn_ctx, attn_lat = _attention(qa, ka, va, cache_k[:, 0].reshape(DEC_BATCH, PAST_LEN, nk).transpose(0, 2, 1),
                                    cache_v[:, 0].reshape(DEC_BATCH, PAST_LEN, nk), at_sink[0])
    wr, br = router_params(1)
    routed = _mixer_out(
        _at_out_kernel, "attn_out_route", (attn_ctx, attn_lat, x), (mix_prompt_blk, mix_sample_blk, mix_blk),
        mod[1], norm_w[1, 1][None, :], at_w_out[0].astype(BF16), wr, br)
    y_prompt, y_sample = _moe_combine(*_moe_layer(*routed, 1, moe_w1, moe_w3, moe_w2), mod[1], final_norm[None, :])

    def cache(feature_major):
        return feature_major.reshape(BATCH, 1, ATT_KV_HEADS, ATT_HEAD_DIM, SEQ).transpose(0, 1, 4, 2, 3)

    return (y_prompt.reshape(BATCH, SEQ, D_MODEL), y_sample.reshape(DEC_BATCH, DEC_SEQ, D_MODEL), state_new,
            cache(k_ctx), cache(v_ctx))
```

```python
import functools
from typing import Any, NamedTuple

import jax
import jax.numpy as jnp
import numpy as np
from jax import lax
from jax.experimental import pallas as pl
from jax.experimental.pallas import tpu as pltpu
from jax.experimental.pallas import tpu_sc as plsc

F32 = jnp.float32
BF16 = jnp.bfloat16

D_MODEL = 1024
BATCH = 16
SEQ = 256
DEC_BATCH = 2
DEC_SEQ = 1024
PAST_LEN = 512
GRID_W = 64
HG_HEADS = 8
HG_DK = 128
CHUNK = 16
ATT_HEAD_DIM = 64
ATT_Q_HEADS = 16
ATT_KV_HEADS = 4
ATT_GROUP = 4
WINDOW = 128
BLOCK = 128
ROPE_HALF = 32
ROPE_BASE = 10000.0
N_GROUPS = 4
EXPERTS_PER_GROUP = 8
N_EXPERTS = 32
D_EXPERT = 256
EPS = 1e-6

N_PROMPT = BATCH * SEQ
N_TOK = N_PROMPT + DEC_BATCH * DEC_SEQ
TM = 256
N_TILES = N_TOK // TM
PROMPT_TILES = N_PROMPT // TM
TILES_PER_DEC_SEQ = DEC_SEQ // TM
LANES = 128
SUBLANES = 8
ADA_LANE_BLOCKS = 4
N_COND_USED = 1 + DEC_BATCH
N_COND = 8
ROUTER_LANES = 128
ROUTE_TM = 1024
ROUTE_ROWS = 8
MIX_TM = 512
AT_IN_TM = 512
COMBINE_TM = 512
HG_IN_TM = 512
HG_IN_SUB = 256
HG_IN_TN = 256
DECAY_CLAMP = 60.0
PACK_WORDS = D_MODEL // 4
ROW_WORDS = 2 * PACK_WORDS
MOE_TILE = 256
MOE_STEP_TILES = 4
MOE_WEIGHT_SLOTS = 2 * MOE_STEP_TILES
MOE_ROWS = 2 * N_TOK + N_EXPERTS * MOE_TILE
MIB = 1024 * 1024


def _params(vmem_mib, *semantics):
    return pltpu.CompilerParams(dimension_semantics=semantics, vmem_limit_bytes=vmem_mib * MIB)


def _tile_cond(i):
    return jnp.where(i < PROMPT_TILES, 0, 1 + (i - PROMPT_TILES) // TILES_PER_DEC_SEQ)


def _mod_row(mod_ref, cond, which):
    return mod_ref[pl.ds(cond, 1), which * D_MODEL:(which + 1) * D_MODEL]


def _norm_mod(x, nw, shift, scale):
    y = x * lax.rsqrt(jnp.mean(x * x, axis=-1, keepdims=True) + EPS)
    return (y * nw) * (1.0 + scale) + shift


def _silu(x):
    return x * jax.nn.sigmoid(x)


def _ada_kernel(c_ref, w_ref, b_ref, o_ref, s_scr):
    @pl.when((pl.program_id(0) == 0) & (pl.program_id(1) == 0))
    def _():
        s_scr[...] = _silu(c_ref[...])

    tn = w_ref.shape[-1]
    cols = [[] for _ in range(N_COND_USED)]
    for j0 in range(0, tn // LANES, ADA_LANE_BLOCKS):
        acc = [[jnp.zeros((SUBLANES, LANES), F32) for _ in range(ADA_LANE_BLOCKS)] for _ in range(N_COND_USED)]
        for g in range(D_MODEL // SUBLANES):
            rows = slice(g * SUBLANES, (g + 1) * SUBLANES)
            w = [w_ref[0, rows, (j0 + j) * LANES:(j0 + j + 1) * LANES] for j in range(ADA_LANE_BLOCKS)]
            for r in range(N_COND_USED):
                s = s_scr[r, rows, :]
                for j in range(ADA_LANE_BLOCKS):
                    acc[r][j] = acc[r][j] + w[j] * s
        for r in range(N_COND_USED):
            cols[r] += [jnp.sum(a, axis=0, keepdims=True) for a in acc[r]]
    out = [jnp.concatenate(cols[r], axis=1) + b_ref[0] for r in range(N_COND_USED)]
    out.append(jnp.zeros((N_COND - N_COND_USED, tn), F32))
    o_ref[0] = jnp.concatenate(out, axis=0)


def _ada(cond, ada_w, ada_b):
    depth, _, n = ada_w.shape
    tn = 1536
    cond_cols = jnp.broadcast_to(cond[:, :, None], (N_COND_USED, D_MODEL, LANES))
    return pl.pallas_call(
        _ada_kernel,
        out_shape=jax.ShapeDtypeStruct((depth, N_COND, n), F32),
        grid=(depth, n // tn),
        in_specs=[
            pl.BlockSpec((N_COND_USED, D_MODEL, LANES), lambda l, j: (0, 0, 0)),
            pl.BlockSpec((1, D_MODEL, tn), lambda l, j: (l, 0, j)),
            pl.BlockSpec((1, 1, tn), lambda l, j: (l, 0, j)),
        ],
        out_specs=pl.BlockSpec((1, N_COND, tn), lambda l, j: (l, 0, j)),
        scratch_shapes=[pltpu.VMEM((N_COND_USED, D_MODEL, LANES), F32)],
        compiler_params=_params(40, "arbitrary", "arbitrary"),
        name="ada_modulation",
    )(cond_cols, ada_w, ada_b.reshape(depth, 1, n))


def _hg_in_kernel(xp_ref, xs_ref, mod_ref, nw_ref, w_ref, lbl_ref,
                  q_ref, v_ref, g_ref, lff_ref, kf_ref, lfb_ref, kb_ref):
    i = pl.program_id(0) * (HG_IN_TM // TM)
    cond = _tile_cond(i)
    shift, scale = _mod_row(mod_ref, cond, 0), _mod_row(mod_ref, cond, 1)

    def normed(rows):
        x = jnp.where(i < PROMPT_TILES, xp_ref[rows, :], xs_ref[rows, :])
        return _norm_mod(x, nw_ref[...], shift, scale).astype(BF16)

    def proj(h, c, cols):
        return jnp.dot(h, w_ref[:, c * D_MODEL + cols.start:c * D_MODEL + cols.stop], preferred_element_type=F32)

    l0, l1, l2 = lbl_ref[0], lbl_ref[1], lbl_ref[2]
    m = jnp.maximum(jnp.maximum(l0, l1), l2)
    e0, e1, e2 = jnp.exp(l0 - m), jnp.exp(l1 - m), jnp.exp(l2 - m)
    lb = e0 / (e0 + e1 + e2)
    col_tiles = [slice(n * HG_IN_TN, (n + 1) * HG_IN_TN) for n in range(D_MODEL // HG_IN_TN)]

    def gate_and_plain(h, rows, d):
        lf_ref, k_ref, plain_ref = ((lff_ref, kf_ref, q_ref), (lfb_ref, kb_ref, v_ref))[d]
        for cols in col_tiles:
            lbd = lb[d:d + 1, cols]
            z_gate = proj(h, 2 + d, cols)
            z_plain = proj(h, d, cols)
            f = lbd + (1.0 - lbd) * jax.nn.sigmoid(z_gate)
            lf = jnp.log(f)
            hi = lf.astype(BF16)
            lf_ref[0, rows, cols] = hi
            lf_ref[1, rows, cols] = (lf - hi.astype(F32)).astype(BF16)
            k_ref[rows, cols] = (1.0 - f).astype(BF16)
            plain_ref[rows, cols] = z_plain.astype(BF16)

    sub_tiles = [slice(s * HG_IN_SUB, (s + 1) * HG_IN_SUB) for s in range(HG_IN_TM // HG_IN_SUB)]
    h = normed(sub_tiles[0])
    for s, rows in enumerate(sub_tiles):
        gate_and_plain(h, rows, 0)
        h_next = normed(sub_tiles[s + 1]) if s + 1 < len(sub_tiles) else None
        gate_and_plain(h, rows, 1)
        for cols in col_tiles:
            g_ref[rows, cols] = proj(h, 4, cols).astype(BF16)
        h = h_next


def _hg_in(xp, xs, mod, nw, w_in, lb_logits):
    tile = lambda i: (i, 0)
    fixed2 = lambda i: (0, 0)
    bf = jax.ShapeDtypeStruct((N_TOK, D_MODEL), BF16)
    ff = jax.ShapeDtypeStruct((2, N_TOK, D_MODEL), BF16)
    blk = pl.BlockSpec((HG_IN_TM, D_MODEL), tile)
    split_blk = pl.BlockSpec((2, HG_IN_TM, D_MODEL), lambda i: (0, i, 0))
    prompt_steps = N_PROMPT // HG_IN_TM
    return pl.pallas_call(
        _hg_in_kernel,
        out_shape=(bf, bf, bf, ff, bf, ff, bf),
        grid=(N_TOK // HG_IN_TM,),
        in_specs=[
            pl.BlockSpec((HG_IN_TM, D_MODEL), lambda i: (jnp.minimum(i, prompt_steps - 1), 0)),
            pl.BlockSpec((HG_IN_TM, D_MODEL), lambda i: (jnp.maximum(i - prompt_steps, 0), 0)),
            pl.BlockSpec(mod.shape, fixed2),
            pl.BlockSpec((1, D_MODEL), fixed2),
            pl.BlockSpec(w_in.shape, fixed2, pipeline_mode=pl.Buffered(1)),
            pl.BlockSpec(lb_logits.shape, lambda i: (0, 0, 0)),
        ],
        out_specs=(blk, blk, blk, split_blk, blk, split_blk, blk),
        compiler_params=_params(56, "arbitrary"),
        name="hgrn_in_proj",
    )(xp, xs, mod, nw, w_in, lb_logits)


def _gla_exact(reverse, q_ref, v_ref, lf_ref, k_ref, st_ref, o_ref):
    nt = (((1,), (1,)), ((), ()))
    tn = (((0,), (0,)), ((), ()))
    n_chunks = TM // CHUNK
    row_id = lax.broadcasted_iota(jnp.int32, (CHUNK, HG_DK), 0)
    order = range(CHUNK - 1, -1, -1) if reverse else range(CHUNK)

    def chunk_head(it, carry):
        step, h = it // HG_HEADS, it % HG_HEADS
        ci = (n_chunks - 1 - step) if reverse else step
        rows = pl.ds(pl.multiple_of(ci * CHUNK, CHUNK), CHUNK)
        cols = pl.ds(pl.multiple_of(h * HG_DK, HG_DK), HG_DK)
        q, k, v = q_ref[rows, cols], k_ref[rows, cols], v_ref[rows, cols]
        f = jnp.exp(lf_ref[0, rows, cols].astype(F32) + lf_ref[1, rows, cols].astype(F32))
        st = st_ref[0, h]
        o = jnp.zeros((CHUNK, HG_DK), F32)
        for t in order:
            one = row_id == t
            v_t = jnp.where(one, v, jnp.zeros_like(v))
            st = st * f[t:t + 1, :] + lax.dot_general(v_t, k, tn, preferred_element_type=F32)
            o = jnp.where(one, lax.dot_general(q, st.astype(BF16), nt, preferred_element_type=F32), o)
        st_ref[0, h] = st
        o_ref[rows, cols] = o.astype(o_ref.dtype)
        return carry

    lax.fori_loop(0, n_chunks * HG_HEADS, chunk_head, 0)


class _GlaDirection(NamedTuple):
    reverse: bool
    q: Any
    v: Any
    lf: Any
    k: Any
    o: Any
    st: Any
    sw: Any
    ut: Any
    qd: Any
    kd: Any
    ki: Any
    cd: Any
    b: Any


_NT = (((1,), (1,)), ((), ()))
_TN = (((0,), (0,)), ((), ()))
_HG_COLS = [slice(h * HG_DK, (h + 1) * HG_DK) for h in range(HG_HEADS)]
_N_CHUNKS = TM // CHUNK
_N_PAIRS = _N_CHUNKS // 2


def _gla_prepare(d):
    r = lax.broadcasted_iota(jnp.int32, (TM, TM), 0)
    c = lax.broadcasted_iota(jnp.int32, (TM, TM), 1)
    same = (r // CHUNK) == (c // CHUNK)
    tri = (same & ((c >= r) if d.reverse else (c <= r))).astype(BF16)
    b = jnp.dot(tri, d.lf[0], preferred_element_type=F32) + jnp.dot(tri, d.lf[1], preferred_element_type=F32)
    d.b[...] = b
    edge = 0 if d.reverse else CHUNK - 1
    total = b.reshape(_N_CHUNKS, CHUNK, D_MODEL)[:, edge, :]
    d.cd[...] = jnp.exp(total)
    return jnp.max(-total) <= DECAY_CLAMP


def _gla_pair_rows(d, step):
    pi = (_N_PAIRS - 1 - step) if d.reverse else step
    row0 = pi * 2 * CHUNK
    lo, hi = (pl.ds(row0, CHUNK), 2 * pi), (pl.ds(row0 + CHUNK, CHUNK), 2 * pi + 1)
    return pl.ds(row0, 2 * CHUNK), ((hi, lo) if d.reverse else (lo, hi))


def _chunk_decay(d, chunk):
    return d.cd[pl.ds(chunk, 1), :]


def _gla_decayed_operands(d, step):
    both, ((_, c_first), (_, c_second)) = _gla_pair_rows(d, step)
    b = d.b[both, :]
    ki = d.k[both, :] * jnp.exp(jnp.minimum(-b, DECAY_CLAMP)).astype(BF16)
    d.qd[both, :] = d.q[both, :] * jnp.exp(b).astype(BF16)
    d.ki[both, :] = ki
    c_lo, c_hi = (c_second, c_first) if d.reverse else (c_first, c_second)
    ends = jnp.concatenate([jnp.broadcast_to(_chunk_decay(d, c).astype(BF16), (CHUNK, D_MODEL)) for c in (c_lo, c_hi)],
                           axis=0)
    d.kd[both, :] = ki * ends


def _rows_scaled(x, scale_row, second_half):
    scale = jnp.broadcast_to(scale_row.astype(BF16), (CHUNK, x.shape[1]))
    ones = jnp.ones((CHUNK, x.shape[1]), BF16)
    return x * jnp.concatenate([ones, scale] if second_half else [scale, ones], axis=0)


def _gla_key_value_product(d, step):
    _gla_decayed_operands(d, step)
    both, (_, (_, c_second)) = _gla_pair_rows(d, step)
    decay_second = _chunk_decay(d, c_second)
    for h, cols in enumerate(_HG_COLS):
        keys = _rows_scaled(d.kd[both, cols], decay_second[:, cols], second_half=d.reverse)
        d.ut[h] = lax.dot_general(d.v[both, cols], keys, _TN, preferred_element_type=F32)


def _gla_start(d):
    for h in range(HG_HEADS):
        d.sw[0, h] = d.st[0, h].T.astype(BF16)
    _gla_key_value_product(d, 0)


def _gla_pair(d, step, src, dst):
    sr = lax.broadcasted_iota(jnp.int32, (CHUNK, CHUNK), 0)
    sc = lax.broadcasted_iota(jnp.int32, (CHUNK, CHUNK), 1)
    tr = lax.broadcasted_iota(jnp.int32, (CHUNK, 2 * CHUNK), 0)
    tc = lax.broadcasted_iota(jnp.int32, (CHUNK, 2 * CHUNK), 1)
    if d.reverse:
        keep_first = sc >= sr
        keep_second = (tc >= tr) | (tc >= CHUNK)
    else:
        keep_first = sc <= sr
        keep_second = (tc < CHUNK) | (tc - CHUNK <= tr)
    both, ((first, c_first), (second, c_second)) = _gla_pair_rows(d, step)
    decay_first = _chunk_decay(d, c_first)
    decay_both = decay_first * _chunk_decay(d, c_second)
    for h, cols in enumerate(_HG_COLS):
        new = d.st[src, h] * decay_both[:, cols] + d.ut[h]
        d.st[dst, h] = new
        d.sw[dst, h] = new.T.astype(BF16)
    a_first = [lax.dot_general(d.qd[first, cols], d.ki[first, cols], _NT, preferred_element_type=F32)
               for cols in _HG_COLS]
    a_second = []
    for cols in _HG_COLS:
        lo_keys = (d.ki if d.reverse else d.kd)[pl.ds(both.start, CHUNK), cols]
        hi_keys = (d.kd if d.reverse else d.ki)[pl.ds(both.start + CHUNK, CHUNK), cols]
        keys = jnp.concatenate([lo_keys, hi_keys], axis=0)
        a_second.append(lax.dot_general(d.qd[second, cols], keys, _NT, preferred_element_type=F32))
    inter = [jnp.dot(_rows_scaled(d.qd[both, cols], decay_first[:, cols], second_half=not d.reverse),
                     d.sw[src, h], preferred_element_type=F32) for h, cols in enumerate(_HG_COLS)]
    if step + 1 < _N_PAIRS:
        _gla_key_value_product(d, step + 1)
    first_half, second_half = (slice(CHUNK, None), slice(None, CHUNK)) if d.reverse else \
                              (slice(None, CHUNK), slice(CHUNK, None))
    for h, cols in enumerate(_HG_COLS):
        am = jnp.where(keep_first, a_first[h], 0.0).astype(BF16)
        d.o[first, cols] = (jnp.dot(am, d.v[first, cols], preferred_element_type=F32)
                            + inter[h][first_half]).astype(d.o.dtype)
        am = jnp.where(keep_second, a_second[h], 0.0).astype(BF16)
        d.o[second, cols] = (jnp.dot(am, d.v[both, cols], preferred_element_type=F32)
                             + inter[h][second_half]).astype(d.o.dtype)


def _gla_kernel(qf_ref, vf_ref, lff_ref, kf_ref, qb_ref, vb_ref, lfb_ref, kb_ref, s0_ref,
                of_ref, ob_ref, sout_ref, st_scr, sw_scr, ut_scr, qd_scr, kd_scr, ki_scr, cd_scr, b_scr):
    i = pl.program_id(0)
    is_prompt = i < PROMPT_TILES
    first = jnp.logical_or(is_prompt, (i - PROMPT_TILES) % TILES_PER_DEC_SEQ == 0)
    work = [tuple(scr.at[n] for scr in (st_scr, sw_scr, ut_scr, qd_scr, kd_scr, ki_scr, cd_scr, b_scr))
            for n in range(2)]
    fwd = _GlaDirection(False, qf_ref, vf_ref, lff_ref, kf_ref, of_ref, *work[0])
    bwd = _GlaDirection(True, qb_ref, vb_ref, lfb_ref, kb_ref, ob_ref, *work[1])

    @pl.when(is_prompt)
    def _():
        for d in (fwd, bwd):
            d.st[0] = jnp.zeros((HG_HEADS, HG_DK, HG_DK), F32)

    @pl.when(jnp.logical_and(first, jnp.logical_not(is_prompt)))
    def _():
        for n, d in enumerate((fwd, bwd)):
            for h in range(HG_HEADS):
                d.st[0, h] = s0_ref[0, 0, n, h].T

    exact_scores = jnp.logical_and(_gla_prepare(fwd), _gla_prepare(bwd))

    @pl.when(exact_scores)
    def _():
        _gla_start(fwd)
        _gla_start(bwd)

        for step in range(_N_PAIRS):
            for d in (fwd, bwd):
                _gla_pair(d, step, step % 2, 1 - step % 2)

    @pl.when(jnp.logical_not(exact_scores))
    def _():
        for d in (fwd, bwd):
            _gla_exact(d.reverse, d.q, d.v, d.lf, d.k, d.st, d.o)

    @pl.when(is_prompt)
    def _():
        for n, d in enumerate((fwd, bwd)):
            for h in range(HG_HEADS):
                sout_ref[0, 0, n, h] = d.st[0, h].T


def _gla(q, v, lff, kf, lfb, kb, state_hgrn):
    def fwd_tile(i):
        return (i, 0)

    def bwd_tile(i):
        j = (i - PROMPT_TILES) % TILES_PER_DEC_SEQ
        return (jnp.where(i < PROMPT_TILES, i, i - j + (TILES_PER_DEC_SEQ - 1 - j)), 0)

    def s0_idx(i):
        return (jnp.maximum(i - PROMPT_TILES, 0) // TILES_PER_DEC_SEQ, 0, 0, 0, 0, 0)

    def sout_idx(i):
        return (jnp.minimum(i, PROMPT_TILES - 1), 0, 0, 0, 0, 0)

    f_blk = pl.BlockSpec((TM, D_MODEL), fwd_tile)
    b_blk = pl.BlockSpec((TM, D_MODEL), bwd_tile)
    f_split = pl.BlockSpec((2, TM, D_MODEL), lambda i: (0,) + fwd_tile(i))
    b_split = pl.BlockSpec((2, TM, D_MODEL), lambda i: (0,) + bwd_tile(i))
    st_blk = (1, 1, 2, HG_HEADS, HG_DK, HG_DK)
    return pl.pallas_call(
        _gla_kernel,
        out_shape=(
            jax.ShapeDtypeStruct((N_TOK, D_MODEL), BF16),
            jax.ShapeDtypeStruct((N_TOK, D_MODEL), BF16),
            jax.ShapeDtypeStruct((BATCH,) + st_blk[1:], F32),
        ),
        grid=(N_TILES,),
        in_specs=[f_blk, f_blk, f_split, f_blk, b_blk, b_blk, b_split, b_blk, pl.BlockSpec(st_blk, s0_idx)],
        out_specs=(f_blk, b_blk, pl.BlockSpec(st_blk, sout_idx)),
        scratch_shapes=[
            pltpu.VMEM((2, 2, HG_HEADS, HG_DK, HG_DK), F32),
            pltpu.VMEM((2, 2, HG_HEADS, HG_DK, HG_DK), BF16),
            pltpu.VMEM((2, HG_HEADS, HG_DK, HG_DK), F32),
            pltpu.VMEM((2, TM, D_MODEL), BF16),
            pltpu.VMEM((2, TM, D_MODEL), BF16),
            pltpu.VMEM((2, TM, D_MODEL), BF16),
            pltpu.VMEM((2, TM // CHUNK, D_MODEL), F32),
            pltpu.VMEM((2, TM, D_MODEL), F32),
        ],
        compiler_params=_params(48, "arbitrary"),
        name="hgrn_recurrence",
    )(q, v, lff, kf, q, v, lfb, kb, state_hgrn)


def _route(logits):
    lane = lax.broadcasted_iota(jnp.int32, logits.shape, 1)
    neg = jnp.float32(-jnp.inf)
    lane_f = lane.astype(F32)

    def first_max(x):
        m = jnp.max(x, axis=-1, keepdims=True)
        first = jnp.min(jnp.where(x == m, lane_f, float(ROUTER_LANES)), axis=-1, keepdims=True)
        return m, first.astype(jnp.int32)

    is_group = (lane >= N_EXPERTS) & (lane < N_EXPERTS + N_GROUPS)
    gl = jnp.where(is_group, logits, neg)
    gmax, g_lane = first_max(gl)
    g_sel = g_lane - N_EXPERTS
    gsum = jnp.sum(jnp.exp(gl - gmax), axis=-1, keepdims=True)
    p_g = 1.0 / gsum
    in_sel = (lane < N_EXPERTS) & ((lane // EXPERTS_PER_GROUP) == g_sel)
    m1, i1 = first_max(jnp.where(in_sel, logits, neg))
    m2, i2 = first_max(jnp.where(in_sel & (lane != i1), logits, neg))
    e2 = jnp.exp(m2 - m1)
    return i1, i2, p_g / (1.0 + e2), p_g * e2 / (1.0 + e2)


def _pack_rows(x):
    q = PACK_WORDS
    bits = pltpu.bitcast(x.astype(BF16).astype(F32), jnp.uint32)
    return [(bits[:, (2 + h) * q:(3 + h) * q] & jnp.uint32(0xFFFF0000)) | (bits[:, h * q:(h + 1) * q] >> 16)
            for h in range(2)]


def _unpack_rows(half0, half1):
    lo = lambda w: pltpu.bitcast(w << 16, F32).astype(BF16)
    hi = lambda w: pltpu.bitcast(w & jnp.uint32(0xFFFF0000), F32).astype(BF16)
    return [lo(half0), lo(half1), hi(half0), hi(half1)]


def _mix_stream_rows(prompt_ref, sample_ref, rows):
    return jnp.where(pl.program_id(0) < N_PROMPT // MIX_TM, prompt_ref[rows, :], sample_ref[rows, :])


def _mixer_tail(mix_of, x_of, mod_ref, nw2_ref, wo_ref, wr_ref, br_ref, x1_ref, h2p_ref, lg_ref):
    cond = _tile_cond(pl.program_id(0) * (MIX_TM // TM))
    gate, shift, scale = (_mod_row(mod_ref, cond, which) for which in (2, 3, 4))
    project = lambda rows: jnp.dot(mix_of(rows), wo_ref[...], preferred_element_type=F32)

    def tail(rows, out):
        x1 = x_of(rows) + gate * out
        x1_ref[rows, :] = x1
        h2 = _norm_mod(x1, nw2_ref[...], shift, scale)
        h2p_ref[rows, :] = jnp.concatenate(_pack_rows(h2), axis=1)
        h_hi = h2.astype(BF16)
        h_lo = (h2 - h_hi.astype(F32)).astype(BF16)
        lg_ref[rows, :] = (jnp.dot(h_hi, wr_ref[0], preferred_element_type=F32)
                           + (jnp.dot(h_hi, wr_ref[1], preferred_element_type=F32)
                              + jnp.dot(h_lo, wr_ref[0], preferred_element_type=F32))) + br_ref[...]

    sub_tiles = [slice(s * TM, (s + 1) * TM) for s in range(MIX_TM // TM)]
    out = project(sub_tiles[0])
    for s, rows in enumerate(sub_tiles):
        out_next = project(sub_tiles[s + 1]) if s + 1 < len(sub_tiles) else None
        tail(rows, out)
        out = out_next


def _route_kernel(lg_ref, rt_ref, rw_ref, cnt_ref, carry_scr):
    @pl.when(pl.program_id(0) == 0)
    def _():
        carry_scr[...] = jnp.zeros_like(carry_scr)

    logits = lg_ref[...]
    i1, i2, w1, w2 = _route(logits)
    lane = lax.broadcasted_iota(jnp.int32, logits.shape, 1)
    chosen = ((lane == i1) | (lane == i2)).astype(BF16)
    r = lax.broadcasted_iota(jnp.int32, (ROUTE_TM, ROUTE_TM), 0)
    c = lax.broadcasted_iota(jnp.int32, (ROUTE_TM, ROUTE_TM), 1)
    before = jnp.dot((c < r).astype(BF16), chosen, preferred_element_type=F32) + carry_scr[...]
    r1 = jnp.sum(jnp.where(lane == i1, before, 0.0), axis=-1, keepdims=True).astype(jnp.int32)
    r2 = jnp.sum(jnp.where(lane == i2, before, 0.0), axis=-1, keepdims=True).astype(jnp.int32)
    total = carry_scr[...] + jnp.sum(chosen.astype(F32), axis=0, keepdims=True)
    carry_scr[...] = total
    tiles = jnp.ceil(total * (1.0 / MOE_TILE))
    e_from = lax.broadcasted_iota(jnp.int32, (ROUTER_LANES, ROUTER_LANES), 0)
    e_to = lax.broadcasted_iota(jnp.int32, (ROUTER_LANES, ROUTER_LANES), 1)
    tiles_before = jnp.dot(jnp.broadcast_to(tiles, (SUBLANES, ROUTER_LANES)).astype(BF16),
                           (e_from < e_to).astype(BF16), preferred_element_type=F32)
    cnt_ref[...] = jnp.concatenate([total, tiles_before[:1] * MOE_TILE,
                                    jnp.zeros((ROUTE_ROWS - 2, ROUTER_LANES), F32)], axis=0)
    by_token = jnp.where(lane == 0, i1, jnp.where(lane == 1, i2, jnp.where(lane == 2, r1, r2)))
    rt_ref[...] = by_token.T[:ROUTE_ROWS, :]
    rw_ref[...] = jnp.where(lane == 0, w1, w2)


def _route_tokens(logits):
    blk = pl.BlockSpec((ROUTE_TM, ROUTER_LANES), lambda i: (i, 0))
    return pl.pallas_call(
        _route_kernel,
        out_shape=(
            jax.ShapeDtypeStruct((ROUTE_ROWS, N_TOK), jnp.int32),
            jax.ShapeDtypeStruct((N_TOK, ROUTER_LANES), F32),
            jax.ShapeDtypeStruct((ROUTE_ROWS, ROUTER_LANES), F32),
        ),
        grid=(N_TOK // ROUTE_TM,),
        in_specs=[blk],
        out_specs=(pl.BlockSpec((ROUTE_ROWS, ROUTE_TM), lambda i: (0, i)), blk,
                   pl.BlockSpec((ROUTE_ROWS, ROUTER_LANES), lambda i: (0, 0))),
        scratch_shapes=[pltpu.VMEM((1, ROUTER_LANES), F32)],
        compiler_params=_params(32, "arbitrary"),
        name="moe_route",
    )(logits)


def _hg_out_kernel(of_ref, ob_ref, g_ref, on_ref, xp_ref, xs_ref, *rest):
    def mix_of(rows):
        o = of_ref[rows, :].astype(F32) + ob_ref[rows, :].astype(F32)
        parts = []
        for h in range(HG_HEADS):
            oh = o[:, h * HG_DK:(h + 1) * HG_DK]
            parts.append(oh * lax.rsqrt(jnp.mean(oh * oh, axis=-1, keepdims=True) + EPS) * on_ref[...])
        y = jnp.concatenate(parts, axis=1) * _silu(g_ref[rows, :].astype(F32))
        return y.astype(BF16)

    _mixer_tail(mix_of, functools.partial(_mix_stream_rows, xp_ref, xs_ref), *rest)


def _at_out_kernel(ac_ref, al_ref, x_ref, *rest):
    _mixer_tail(functools.partial(_mix_stream_rows, ac_ref, al_ref), lambda rows: x_ref[rows, :], *rest)


def _mixer_out(kernel_fn, name, mix_inputs, mix_specs, mod, nw2, w_out, w_router, b_router):
    tile = lambda i: (i, 0)
    fixed2 = lambda i: (0, 0)
    blk = pl.BlockSpec((MIX_TM, D_MODEL), tile)
    lanes_blk = pl.BlockSpec((MIX_TM, ROUTER_LANES), tile)
    return pl.pallas_call(
        kernel_fn,
        out_shape=(
            jax.ShapeDtypeStruct((N_TOK, D_MODEL), F32),
            jax.ShapeDtypeStruct((N_TOK, ROW_WORDS), jnp.uint32),
            jax.ShapeDtypeStruct((N_TOK, ROUTER_LANES), F32),
        ),
        grid=(N_TOK // MIX_TM,),
        in_specs=list(mix_specs) + [
            pl.BlockSpec(mod.shape, fixed2),
            pl.BlockSpec((1, D_MODEL), fixed2),
            pl.BlockSpec(w_out.shape, fixed2),
            pl.BlockSpec(w_router.shape, lambda i: (0, 0, 0)),
            pl.BlockSpec((1, ROUTER_LANES), fixed2),
        ],
        out_specs=(blk, pl.BlockSpec((MIX_TM, ROW_WORDS), tile), lanes_blk),
        compiler_params=_params(40, "arbitrary"),
        name=name,
    )(*mix_inputs, mod, nw2, w_out, w_router, b_router)


def _moe_plan(route_t, counts):
    cnt = counts[0, :N_EXPERTS].astype(jnp.int32)
    offs = counts[1, :N_EXPERTS].astype(jnp.int32)
    ends = offs + ((cnt + MOE_TILE - 1) // MOE_TILE) * MOE_TILE
    experts = route_t[0:2]
    pos = jnp.sum(jnp.where(experts[None, :, :] == jnp.arange(N_EXPERTS)[:, None, None], offs[:, None, None], 0),
                  axis=0) + route_t[2:4]
    tile_start = jnp.arange(MOE_ROWS // MOE_TILE, dtype=jnp.int32) * MOE_TILE
    tile_expert = jnp.minimum(jnp.sum(ends[None, :] <= tile_start[:, None], axis=1), N_EXPERTS - 1).astype(jnp.int32)
    of_tile = tile_expert[:, None] == jnp.arange(N_EXPERTS)[None, :]
    tile_offs = jnp.sum(jnp.where(of_tile, offs[None, :], 0), axis=1)
    tile_cnt = jnp.sum(jnp.where(of_tile, cnt[None, :], 0), axis=1)
    tile_rows = jnp.clip(tile_offs + tile_cnt - tile_start, 0, MOE_TILE).astype(jnp.int32)
    active = tile_start < ends[-1]
    tile_opens = jnp.where(active & (tile_start == tile_offs), tile_expert, -1).astype(jnp.int32)
    tile_opens = jnp.concatenate([tile_opens, jnp.full((MOE_STEP_TILES,), -1, jnp.int32)])
    n_active = (ends[-1] // MOE_TILE).astype(jnp.int32).reshape(1)
    slot = (jnp.cumsum((cnt > 0).astype(jnp.int32)) - 1) % MOE_WEIGHT_SLOTS
    tile_slot = jnp.sum(jnp.where(of_tile, slot[None, :], 0), axis=1)
    last_slot = jnp.sum(jnp.where(tile_start == ends[-1] - MOE_TILE, tile_slot, 0))
    tile_slot = jnp.where(active, tile_slot, last_slot).astype(jnp.int32)
    return pos.astype(jnp.int32), (tile_rows, tile_slot, tile_opens, n_active)


SC_WINDOW = 64


def _sc_mesh():
    return plsc.VectorSubcoreMesh(core_axis_name="c", subcore_axis_name="s")


def _sc_scatter_rows(x, idx_a, idx_b, n_out_rows):
    n = x.shape[0]

    @functools.partial(pl.kernel, out_type=jax.ShapeDtypeStruct((n_out_rows, ROW_WORDS), x.dtype), mesh=_sc_mesh(),
                       scratch_types=[pltpu.SemaphoreType.DMA, pltpu.SemaphoreType.DMA])
    def scatter(x_hbm, ia_hbm, ib_hbm, o_hbm, sem_a, sem_b):
        def body(x_vmem, ia_vmem, ib_vmem):
            copy_a = pltpu.async_copy(x_vmem, o_hbm.at[ia_vmem.at[0]], sem_a)
            copy_b = pltpu.async_copy(x_vmem, o_hbm.at[ib_vmem.at[0]], sem_b)
            copy_a.wait()
            copy_b.wait()

        idx_spec = pl.BlockSpec((1, SC_WINDOW), index_map=lambda i: (i, 0))
        pltpu.emit_pipeline(
            body, grid=(n // SC_WINDOW,),
            in_specs=[pl.BlockSpec((SC_WINDOW, ROW_WORDS), index_map=lambda i: (i, 0)), idx_spec, idx_spec],
            out_specs=[],
            core_axis_name=("c", "s"), dimension_semantics=(pltpu.PARALLEL,),
        )(x_hbm, ia_hbm, ib_hbm)

    return scatter(x, idx_a, idx_b)


def _sc_gather_rows(table, idx):
    n = idx.size

    @functools.partial(pl.kernel, out_type=jax.ShapeDtypeStruct((n, ROW_WORDS), table.dtype), mesh=_sc_mesh())
    def gather(t_hbm, i_hbm, o_hbm):
        def body(i_vmem, o_vmem):
            pltpu.sync_copy(t_hbm.at[i_vmem.at[0]], o_vmem)

        pltpu.emit_pipeline(
            body, grid=(n // SC_WINDOW,),
            in_specs=[pl.BlockSpec((1, SC_WINDOW), index_map=lambda i: (i, 0))],
            out_specs=[pl.BlockSpec((SC_WINDOW, ROW_WORDS), index_map=lambda i: (i, 0))],
            core_axis_name=("c", "s"), dimension_semantics=(pltpu.PARALLEL,),
        )(i_hbm, o_hbm)

    return gather(table, idx)


def _ffn_kernel(tr_ref, ts_ref, to_ref, na_ref, xs_ref, w1_hbm, w3_hbm, w2_hbm, ys_ref,
                w1_buf, w3_buf, w2_buf, sem, *, layer):
    def weight_copies(t):
        pairs = ((w1_hbm, w1_buf), (w3_hbm, w3_buf), (w2_hbm, w2_buf))
        return [pltpu.make_async_copy(hbm.at[layer, to_ref[t]], buf.at[ts_ref[t]], sem.at[ts_ref[t], j])
                for j, (hbm, buf) in enumerate(pairs)]

    def for_experts_opened_from(first_tile, action):
        for j in range(MOE_STEP_TILES):
            t = first_tile + j

            @pl.when(to_ref[t] >= 0)
            def _():
                for copy in weight_copies(t):
                    action(copy)

    def tile(t, rows):
        slot = ts_ref[t]
        row = lax.broadcasted_iota(jnp.int32, (MOE_TILE, PACK_WORDS), 0)
        live = row < tr_ref[t]
        halves = [jnp.where(live, xs_ref[rows, h * PACK_WORDS:(h + 1) * PACK_WORDS],
                            jnp.zeros((MOE_TILE, PACK_WORDS), jnp.uint32)) for h in range(2)]
        chunks = _unpack_rows(*halves)

        def up(w_buf):
            acc = None
            for k, chunk in enumerate(chunks):
                w = w_buf[slot, k * PACK_WORDS:(k + 1) * PACK_WORDS, :].astype(BF16)
                part = jnp.dot(chunk, w, preferred_element_type=F32)
                acc = part if acc is None else acc + part
            return acc

        hid = (_silu(up(w1_buf)) * up(w3_buf)).astype(BF16)
        y = jnp.dot(hid, w2_buf[slot].astype(BF16), preferred_element_type=F32)
        ys_ref[rows, :] = jnp.concatenate(_pack_rows(y), axis=1)

    step = pl.program_id(0)
    first_tile = step * MOE_STEP_TILES

    @pl.when(first_tile < na_ref[0])
    def _():
        @pl.when(step == 0)
        def _():
            for_experts_opened_from(first_tile, lambda copy: copy.start())

        for_experts_opened_from(first_tile + MOE_STEP_TILES, lambda copy: copy.start())
        for_experts_opened_from(first_tile, lambda copy: copy.wait())
        for j in range(MOE_STEP_TILES):
            tile(first_tile + j, slice(j * MOE_TILE, (j + 1) * MOE_TILE))


def _ffn(xs, tables, layer, w1, w3, w2):
    step_rows = MOE_STEP_TILES * MOE_TILE
    row_tile = lambda s, tr, ts, to, na: (jnp.minimum(s, (na[0] - 1) // MOE_STEP_TILES), 0)
    hbm = pl.BlockSpec(memory_space=pl.ANY)
    return pl.pallas_call(
        functools.partial(_ffn_kernel, layer=layer),
        out_shape=jax.ShapeDtypeStruct((MOE_ROWS, ROW_WORDS), jnp.uint32),
        grid_spec=pltpu.PrefetchScalarGridSpec(
            num_scalar_prefetch=len(tables),
            grid=(MOE_ROWS // step_rows,),
            in_specs=[pl.BlockSpec((step_rows, ROW_WORDS), row_tile), hbm, hbm, hbm],
            out_specs=pl.BlockSpec((step_rows, ROW_WORDS), row_tile),
            scratch_shapes=[
                pltpu.VMEM((MOE_WEIGHT_SLOTS, D_MODEL, D_EXPERT), F32),
                pltpu.VMEM((MOE_WEIGHT_SLOTS, D_MODEL, D_EXPERT), F32),
                pltpu.VMEM((MOE_WEIGHT_SLOTS, D_EXPERT, D_MODEL), F32),
                pltpu.SemaphoreType.DMA((MOE_WEIGHT_SLOTS, 3)),
            ],
        ),
        compiler_params=_params(48, "arbitrary"),
        name="moe_experts",
    )(*tables, xs, w1, w3, w2)


def _moe_res(x_ref, y_ref, rw_ref, mod_ref, cond, rows=slice(None)):
    rw = rw_ref[rows, :]
    wa = rw[:, 0:1]
    wb = rw[:, 1:2]
    ya = _unpack_rows(y_ref[0, rows, :PACK_WORDS], y_ref[0, rows, PACK_WORDS:])
    yb = _unpack_rows(y_ref[1, rows, :PACK_WORDS], y_ref[1, rows, PACK_WORDS:])
    y = jnp.concatenate([wa * a.astype(F32) + wb * b.astype(F32) for a, b in zip(ya, yb)], axis=1)
    return x_ref[rows, :] + _mod_row(mod_ref, cond, 5) * y


def _moe_res_final_kernel(x_ref, y_ref, rw_ref, mod_ref, fn_ref, op_ref, os_ref):
    x = _moe_res(x_ref, y_ref, rw_ref, mod_ref, _tile_cond(pl.program_id(0) * (COMBINE_TM // TM)))
    y = x * lax.rsqrt(jnp.mean(x * x, axis=-1, keepdims=True) + EPS) * fn_ref[...]
    is_prompt = pl.program_id(0) < N_PROMPT // COMBINE_TM

    @pl.when(is_prompt)
    def _():
        op_ref[...] = y

    @pl.when(jnp.logical_not(is_prompt))
    def _():
        os_ref[...] = y


def _moe_combine(x1, y_pairs, route_w, mod, final_norm):
    tile = lambda i: (i, 0)
    fixed2 = lambda i: (0, 0)
    prompt_steps = N_PROMPT // COMBINE_TM
    return pl.pallas_call(
        _moe_res_final_kernel,
        out_shape=(jax.ShapeDtypeStruct((N_PROMPT, D_MODEL), F32),
                   jax.ShapeDtypeStruct((N_TOK - N_PROMPT, D_MODEL), F32)),
        grid=(N_TOK // COMBINE_TM,),
        in_specs=[pl.BlockSpec((COMBINE_TM, D_MODEL), tile),
                  pl.BlockSpec((2, COMBINE_TM, ROW_WORDS), lambda i: (0, i, 0)),
                  pl.BlockSpec((COMBINE_TM, ROUTER_LANES), tile),
                  pl.BlockSpec(mod.shape, fixed2),
                  pl.BlockSpec((1, D_MODEL), fixed2)],
        out_specs=(pl.BlockSpec((COMBINE_TM, D_MODEL), lambda i: (jnp.minimum(i, prompt_steps - 1), 0)),
                   pl.BlockSpec((COMBINE_TM, D_MODEL), lambda i: (jnp.maximum(i - prompt_steps, 0), 0))),
        compiler_params=_params(32, "arbitrary"),
        name="moe_combine",
    )(x1, y_pairs, route_w, mod, final_norm)


def _moe_layer(x1, h2p, logits, layer, w1, w3, w2):
    route_t, route_w, counts = _route_tokens(logits)
    pos, tables = _moe_plan(route_t, counts)
    windows = pos.reshape(2, N_TOK // SC_WINDOW, SC_WINDOW)
    xs = _sc_scatter_rows(h2p, windows[0], windows[1], MOE_ROWS)
    ys = _ffn(xs, tables, layer, w1, w3, w2)
    y_pairs = _sc_gather_rows(ys, windows.reshape(2 * N_TOK // SC_WINDOW, SC_WINDOW))
    return x1, y_pairs.reshape(2, N_TOK, ROW_WORDS), route_w


def _swap_rotary_halves(x):
    n = x.shape[-1]
    lane = lax.broadcasted_iota(jnp.int32, x.shape, 1)
    quarter = ROPE_HALF // 2
    return jnp.where((lane % ROPE_HALF) < quarter, pltpu.roll(x, n - quarter, 1), pltpu.roll(x, quarter, 1))


def _at_in_kernel(x1_ref, y_ref, rw_ref, mod_prev_ref, mod_ref, nw_ref, w_ref, cos_ref, sin_ref,
                  x_ref, q_ref, kt_ref, v_ref, kc_ref, vc_ref):
    step = pl.program_id(0)
    cond = _tile_cond(step * (AT_IN_TM // TM))
    shift, scale = _mod_row(mod_ref, cond, 0), _mod_row(mod_ref, cond, 1)
    nq = ATT_Q_HEADS * ATT_HEAD_DIM
    nk = ATT_KV_HEADS * ATT_HEAD_DIM

    def normed(rows):
        x = _moe_res(x1_ref, y_ref, rw_ref, mod_prev_ref, cond, rows)
        x_ref[rows, :] = x
        return _norm_mod(x, nw_ref[...], shift, scale).astype(BF16)

    def rope(x, rows):
        reps = x.shape[-1] // LANES
        cos = jnp.concatenate([cos_ref[rows, :]] * reps, axis=1)
        sin = jnp.concatenate([sin_ref[rows, :]] * reps, axis=1)
        return x * cos + _swap_rotary_halves(x) * sin

    def project_q(h):
        return jnp.dot(h, w_ref[:, :nq], preferred_element_type=F32)

    def project_kv(h):
        return (jnp.dot(h, w_ref[:, nq:nq + nk], preferred_element_type=F32),
                jnp.dot(h, w_ref[:, nq + nk:], preferred_element_type=F32))

    def tail(rows, zq, zk, v):
        q_ref[rows, :] = (rope(zq, rows) * ATT_HEAD_DIM ** -0.5).astype(BF16)
        kt = rope(zk, rows).T
        kt_ref[:, rows] = kt.astype(BF16)
        v_ref[rows, :] = v.astype(BF16)
        return kt, v.T

    sub_tiles = [slice(s * TM, (s + 1) * TM) for s in range(AT_IN_TM // TM)]
    cache_rows = []
    h = normed(sub_tiles[0])
    zq = project_q(h)
    for s, rows in enumerate(sub_tiles):
        last = s + 1 == len(sub_tiles)
        h_next = None if last else normed(sub_tiles[s + 1])
        zk, v = project_kv(h)
        zq_next = None if last else project_q(h_next)
        cache_rows.append(tail(rows, zq, zk, v))
        h, zq = h_next, zq_next

    @pl.when(step < N_PROMPT // AT_IN_TM)
    def _():
        for rows, (kt, vt) in zip(sub_tiles, cache_rows):
            kc_ref[rows, :] = kt
            vc_ref[rows, :] = vt


def _rope_tables():
    f32 = np.float32
    pos = np.arange(DEC_SEQ)
    t_row = (pos // GRID_W).astype(f32)
    t_col = (pos % GRID_W).astype(f32)
    inv = f32(ROPE_BASE) ** (-np.arange(0, ROPE_HALF, 2, dtype=f32) / f32(ROPE_HALF))
    j = np.arange(LANES) % ATT_HEAD_DIM
    freq = inv[(j % ROPE_HALF) % (ROPE_HALF // 2)]
    ang = (np.where((j < ROPE_HALF)[None, :], t_row[:, None], t_col[:, None]) * freq[None, :]).astype(f32)
    sign = np.where((j % ROPE_HALF) < ROPE_HALF // 2, -1.0, 1.0).astype(f32)
    cos = np.concatenate([np.ones((AT_IN_TM, LANES), f32), np.cos(ang)], axis=0)
    sin = np.concatenate([np.zeros((AT_IN_TM, LANES), f32), np.sin(ang) * sign[None, :]], axis=0)
    return jnp.asarray(cos, F32), jnp.asarray(sin, F32)


def _at_in(x1, y_pairs, route_w, mod_prev, mod, nw, w_in, cos, sin):
    tile = lambda i: (i, 0)
    fixed2 = lambda i: (0, 0)
    prompt_steps = N_PROMPT // AT_IN_TM
    rope_tile = lambda i: (jnp.where(i < prompt_steps, 0, 1 + (i - prompt_steps) % (DEC_SEQ // AT_IN_TM)), 0)
    prompt_tile = lambda i: (jnp.minimum(i, prompt_steps - 1), 0)
    nk = ATT_KV_HEADS * ATT_HEAD_DIM
    return pl.pallas_call(
        _at_in_kernel,
        out_shape=(
            jax.ShapeDtypeStruct((N_TOK, D_MODEL), F32),
            jax.ShapeDtypeStruct((N_TOK, D_MODEL), BF16),
            jax.ShapeDtypeStruct((nk, N_TOK), BF16),
            jax.ShapeDtypeStruct((N_TOK, nk), BF16),
            jax.ShapeDtypeStruct((N_PROMPT, nk), F32),
            jax.ShapeDtypeStruct((N_PROMPT, nk), F32),
        ),
        grid=(N_TOK // AT_IN_TM,),
        in_specs=[
            pl.BlockSpec((AT_IN_TM, D_MODEL), tile),
            pl.BlockSpec((2, AT_IN_TM, ROW_WORDS), lambda i: (0, i, 0)),
            pl.BlockSpec((AT_IN_TM, ROUTER_LANES), tile),
            pl.BlockSpec(mod_prev.shape, fixed2),
            pl.BlockSpec(mod.shape, fixed2),
            pl.BlockSpec((1, D_MODEL), fixed2),
            pl.BlockSpec(w_in.shape, fixed2),
            pl.BlockSpec((AT_IN_TM, LANES), rope_tile),
            pl.BlockSpec((AT_IN_TM, LANES), rope_tile),
        ],
        out_specs=(pl.BlockSpec((AT_IN_TM, D_MODEL), tile), pl.BlockSpec((AT_IN_TM, D_MODEL), tile),
                   pl.BlockSpec((nk, AT_IN_TM), lambda i: (0, i)),
                   pl.BlockSpec((AT_IN_TM, nk), tile),
                   pl.BlockSpec((AT_IN_TM, nk), prompt_tile), pl.BlockSpec((AT_IN_TM, nk), prompt_tile)),
        compiler_params=_params(40, "arbitrary"),
        name="attn_in_proj",
    )(x1, y_pairs, route_w, mod_prev, mod, nw, w_in, cos, sin)


def _attend(q, kt_all, v_all, mask, sink_ref, o_ref):
    nq = q.shape[0]
    group_lanes = ATT_GROUP * ATT_HEAD_DIM
    lane = lax.broadcasted_iota(jnp.int32, (nq, group_lanes), 1)
    mine = [(lane // ATT_HEAD_DIM) == g for g in range(ATT_GROUP)]
    row_head = lax.broadcasted_iota(jnp.int32, (ATT_GROUP * nq, 1), 0) // nq
    if mask is not None:
        mask = jnp.concatenate([mask] * ATT_GROUP, axis=0)

    def scores(hk):
        kt = jnp.concatenate([kt_all[hk * ATT_HEAD_DIM:(hk + 1) * ATT_HEAD_DIM, :]] * ATT_GROUP, axis=0)
        qg = q[:, hk * group_lanes:(hk + 1) * group_lanes]
        q_stack = jnp.concatenate([jnp.where(mine[g], qg, jnp.zeros_like(qg)) for g in range(ATT_GROUP)], axis=0)
        s = jnp.dot(q_stack, kt, preferred_element_type=F32)
        return s if mask is None else jnp.where(mask, s, -jnp.inf)

    s_next = scores(0)
    for hk in range(ATT_KV_HEADS):
        s = s_next
        if hk + 1 < ATT_KV_HEADS:
            s_next = scores(hk + 1)
        vh = v_all[:, hk * ATT_HEAD_DIM:(hk + 1) * ATT_HEAD_DIM]
        vt = jnp.concatenate([vh] * ATT_GROUP, axis=1)
        sink = jnp.zeros((ATT_GROUP * nq, 1), F32)
        for g in range(ATT_GROUP):
            sink = jnp.where(row_head == g, sink_ref[hk * ATT_GROUP + g], sink)
        m = jnp.maximum(jnp.max(s, axis=-1, keepdims=True), sink)
        p = jnp.exp(s - m)
        denom = jnp.sum(p, axis=-1, keepdims=True) + jnp.exp(sink - m)
        o = jnp.dot(p.astype(BF16), vt, preferred_element_type=F32) / denom
        acc = jnp.where(mine[0], o[:nq], 0.0)
        for g in range(1, ATT_GROUP):
            acc = acc + jnp.where(mine[g], o[g * nq:(g + 1) * nq], 0.0)
        o_ref[:, hk * group_lanes:(hk + 1) * group_lanes] = acc.astype(BF16)


def _ctx_attn_kernel(sink_ref, q_ref, k_ref, v_ref, o_ref):
    _attend(q_ref[...], k_ref[...], v_ref[...], None, sink_ref, o_ref)


def _lat_attn_kernel(sink_ref, q_ref, kp_ref, kc_ref, kn_ref, vp_ref, vc_ref, vn_ref, ck_ref, cv_ref, o_ref):
    jb = pl.program_id(1)
    kt_all = jnp.concatenate([kp_ref[...], kc_ref[...], kn_ref[...], ck_ref[0].astype(BF16)], axis=1)
    v_all = jnp.concatenate([vp_ref[...], vc_ref[...], vn_ref[...], cv_ref[0].astype(BF16)], axis=0)
    nkeys = 3 * BLOCK + PAST_LEN
    qi = lax.broadcasted_iota(jnp.int32, (BLOCK, nkeys), 0)
    kj = lax.broadcasted_iota(jnp.int32, (BLOCK, nkeys), 1)
    qpos = jb * BLOCK + qi
    kpos = (jb - 1) * BLOCK + kj
    local_ok = (jnp.abs(qpos - kpos) <= WINDOW) & (kpos >= 0) & (kpos < DEC_SEQ)
    mask = (kj >= 3 * BLOCK) | local_ok
    _attend(q_ref[...], kt_all, v_all, mask, sink_ref, o_ref)


def _attention(q, kt, v, cache_kt, cache_v, sink):
    nk = ATT_KV_HEADS * ATT_HEAD_DIM
    smem = pl.BlockSpec(memory_space=pltpu.SMEM)
    ctx = pl.pallas_call(
        _ctx_attn_kernel,
        out_shape=jax.ShapeDtypeStruct((N_PROMPT, D_MODEL), BF16),
        grid=(BATCH,),
        in_specs=[
            smem,
            pl.BlockSpec((SEQ, D_MODEL), lambda b: (b, 0)),
            pl.BlockSpec((nk, SEQ), lambda b: (0, b)),
            pl.BlockSpec((SEQ, nk), lambda b: (b, 0)),
        ],
        out_specs=pl.BlockSpec((SEQ, D_MODEL), lambda b: (b, 0)),
        compiler_params=_params(40, "arbitrary"),
        name="context_attention",
    )(sink, q, kt, v)

    nb = DEC_SEQ // BLOCK
    base = N_PROMPT // BLOCK
    cur = lambda b, j: (base + b * nb + j, 0)
    prev = lambda b, j: (base + b * nb + jnp.maximum(j - 1, 0), 0)
    nxt = lambda b, j: (base + b * nb + jnp.minimum(j + 1, nb - 1), 0)
    kv_blk = lambda f: pl.BlockSpec((BLOCK, nk), f)
    kt_blk = lambda f: pl.BlockSpec((nk, BLOCK), lambda b, j: f(b, j)[::-1])
    cache_blk = pl.BlockSpec((1, PAST_LEN, nk), lambda b, j: (b, 0, 0))
    cache_kt_blk = pl.BlockSpec((1, nk, PAST_LEN), lambda b, j: (b, 0, 0))
    lat = pl.pallas_call(
        _lat_attn_kernel,
        out_shape=jax.ShapeDtypeStruct((DEC_BATCH * DEC_SEQ, D_MODEL), BF16),
        grid=(DEC_BATCH, nb),
        in_specs=[
            smem,
            pl.BlockSpec((BLOCK, D_MODEL), cur),
            kt_blk(prev), kt_blk(cur), kt_blk(nxt),
            kv_blk(prev), kv_blk(cur), kv_blk(nxt),
            cache_kt_blk, cache_blk,
        ],
        out_specs=pl.BlockSpec((BLOCK, D_MODEL), lambda b, j: (b * nb + j, 0)),
        compiler_params=_params(40, "arbitrary", "arbitrary"),
        name="latent_attention",
    )(sink, q, kt, kt, kt, v, v, v, cache_kt, cache_v)
    return ctx, lat


def kernel(x_prompt, x_sample, state_hgrn, cache_k, cache_v, c, c_ctx, ada_w, ada_b, norm_w, hg_w_in,
           hg_lb_logits, hg_onorm, hg_w_out, at_w_in, at_sink, at_w_out, moe_w_group, moe_b_group,
           moe_w_expert, moe_b_expert, moe_w1, moe_w3, moe_w2, final_norm):
    xp = x_prompt.reshape(N_PROMPT, D_MODEL)
    xs = x_sample.reshape(N_TOK - N_PROMPT, D_MODEL)
    cond = jnp.concatenate([c_ctx[None, :], c], axis=0)
    mod = _ada(cond, ada_w, ada_b)
    nk = ATT_KV_HEADS * ATT_HEAD_DIM

    def router_params(i):
        pad = jnp.zeros((D_MODEL, ROUTER_LANES - N_EXPERTS - N_GROUPS), F32)
        w = jnp.concatenate([moe_w_expert[i], moe_w_group[i], pad], axis=1)
        b = jnp.concatenate([moe_b_expert[i], moe_b_group[i], pad[0]])[None, :]
        hi = w.astype(BF16)
        lo = (w - hi.astype(F32)).astype(BF16)
        return jnp.stack([hi, lo]), b

    mix_blk = pl.BlockSpec((MIX_TM, D_MODEL), lambda i: (i, 0))
    mix_prompt_blk = pl.BlockSpec((MIX_TM, D_MODEL), lambda i: (jnp.minimum(i, N_PROMPT // MIX_TM - 1), 0))
    mix_sample_blk = pl.BlockSpec((MIX_TM, D_MODEL), lambda i: (jnp.maximum(i - N_PROMPT // MIX_TM, 0), 0))
    q, v, g, lff, kf, lfb, kb = _hg_in(xp, xs, mod[0], norm_w[0, 0][None, :], hg_w_in[0].astype(BF16), hg_lb_logits)
    o_f, o_b, state_new = _gla(q, v, lff, kf, lfb, kb, state_hgrn)
    wr, br = router_params(0)
    routed = _mixer_out(
        _hg_out_kernel, "hgrn_out_route", (o_f, o_b, g, hg_onorm[0][None, :], xp, xs),
        (mix_blk, mix_blk, mix_blk, pl.BlockSpec((1, HG_DK), lambda i: (0, 0)), mix_prompt_blk, mix_sample_blk),
        mod[0], norm_w[0, 1][None, :], hg_w_out[0].astype(BF16), wr, br)
    moe_out = _moe_layer(*routed, 0, moe_w1, moe_w3, moe_w2)

    cos, sin = _rope_tables()
    x, qa, ka, va, k_ctx, v_ctx = _at_in(*moe_out, mod[0], mod[1], norm_w[1, 0][None, :],
                                         at_w_in[0].astype(BF16), cos, sin)
    attn_ctx, attn_lat = _attention(qa, ka, va, cache_k[:, 0].reshape(DEC_BATCH, PAST_LEN, nk).transpose(0, 2, 1),
                                    cache_v[:, 0].reshape(DEC_BATCH, PAST_LEN, nk), at_sink[0])
    wr, br = router_params(1)
    routed = _mixer_out(
        _at_out_kernel, "attn_out_route", (attn_ctx, attn_lat, x), (mix_prompt_blk, mix_sample_blk, mix_blk),
        mod[1], norm_w[1, 1][None, :], at_w_out[0].astype(BF16), wr, br)
    y_prompt, y_sample = _moe_combine(*_moe_layer(*routed, 1, moe_w1, moe_w3, moe_w2), mod[1], final_norm[None, :])

    def cache(feature_major):
        return feature_major.reshape(BATCH, 1, ATT_KV_HEADS, ATT_HEAD_DIM, SEQ).transpose(0, 1, 4, 2, 3)

    return (y_prompt.reshape(BATCH, SEQ, D_MODEL), y_sample.reshape(DEC_BATCH, DEC_SEQ, D_MODEL), state_new,
            cache(k_ctx), cache(v_ctx))
```

```python
import functools
from typing import Any, NamedTuple

import jax
import jax.numpy as jnp
import numpy as np
from jax import lax
from jax.experimental import pallas as pl
from jax.experimental.pallas import tpu as pltpu
from jax.experimental.pallas import tpu_sc as plsc

F32 = jnp.float32
BF16 = jnp.bfloat16

D_MODEL = 1024
BATCH = 16
SEQ = 256
DEC_BATCH = 2
DEC_SEQ = 1024
PAST_LEN = 512
GRID_W = 64
HG_HEADS = 8
HG_DK = 128
CHUNK = 16
ATT_HEAD_DIM = 64
ATT_Q_HEADS = 16
ATT_KV_HEADS = 4
ATT_GROUP = 4
WINDOW = 128
BLOCK = 128
ROPE_HALF = 32
ROPE_BASE = 10000.0
N_GROUPS = 4
EXPERTS_PER_GROUP = 8
N_EXPERTS = 32
D_EXPERT = 256
EPS = 1e-6

N_PROMPT = BATCH * SEQ
N_TOK = N_PROMPT + DEC_BATCH * DEC_SEQ
TM = 256
N_TILES = N_TOK // TM
PROMPT_TILES = N_PROMPT // TM
TILES_PER_DEC_SEQ = DEC_SEQ // TM
LANES = 128
SUBLANES = 8
ADA_LANE_BLOCKS = 4
N_COND_USED = 1 + DEC_BATCH
N_COND = 8
ROUTER_LANES = 128
ROUTE_TM = 1024
ROUTE_ROWS = 8
MIX_TM = 512
AT_IN_TM = 512
COMBINE_TM = 1024
HG_IN_TM = 512
HG_IN_SUB = 256
HG_IN_TN = 256
DECAY_CLAMP = 60.0
PACK_WORDS = D_MODEL // 4
ROW_WORDS = 2 * PACK_WORDS
MOE_TILE = 256
MOE_STEP_TILES = 4
MOE_WEIGHT_SLOTS = 2 * MOE_STEP_TILES
MOE_ROWS = 2 * N_TOK + N_EXPERTS * MOE_TILE
MIB = 1024 * 1024


def _params(vmem_mib, *semantics):
    return pltpu.CompilerParams(dimension_semantics=semantics, vmem_limit_bytes=vmem_mib * MIB)


def _tile_cond(i):
    return jnp.where(i < PROMPT_TILES, 0, 1 + (i - PROMPT_TILES) // TILES_PER_DEC_SEQ)


def _mod_row(mod_ref, cond, which):
    return mod_ref[pl.ds(cond, 1), which * D_MODEL:(which + 1) * D_MODEL]


def _norm_mod(x, nw, shift, scale):
    y = x * lax.rsqrt(jnp.mean(x * x, axis=-1, keepdims=True) + EPS)
    return (y * nw) * (1.0 + scale) + shift


def _silu(x):
    return x * jax.nn.sigmoid(x)


def _ada_kernel(c_ref, w_ref, b_ref, o_ref, s_scr):
    @pl.when((pl.program_id(0) == 0) & (pl.program_id(1) == 0))
    def _():
        s_scr[...] = _silu(c_ref[...])

    tn = w_ref.shape[-1]
    cols = [[] for _ in range(N_COND_USED)]
    for j0 in range(0, tn // LANES, ADA_LANE_BLOCKS):
        acc = [[jnp.zeros((SUBLANES, LANES), F32) for _ in range(ADA_LANE_BLOCKS)] for _ in range(N_COND_USED)]
        for g in range(D_MODEL // SUBLANES):
            rows = slice(g * SUBLANES, (g + 1) * SUBLANES)
            w = [w_ref[0, rows, (j0 + j) * LANES:(j0 + j + 1) * LANES] for j in range(ADA_LANE_BLOCKS)]
            for r in range(N_COND_USED):
                s = s_scr[r, rows, :]
                for j in range(ADA_LANE_BLOCKS):
                    acc[r][j] = acc[r][j] + w[j] * s
        for r in range(N_COND_USED):
            cols[r] += [jnp.sum(a, axis=0, keepdims=True) for a in acc[r]]
    out = [jnp.concatenate(cols[r], axis=1) + b_ref[0] for r in range(N_COND_USED)]
    out.append(jnp.zeros((N_COND - N_COND_USED, tn), F32))
    o_ref[0] = jnp.concatenate(out, axis=0)


def _ada(cond, ada_w, ada_b):
    depth, _, n = ada_w.shape
    tn = 1536
    cond_cols = jnp.broadcast_to(cond[:, :, None], (N_COND_USED, D_MODEL, LANES))
    return pl.pallas_call(
        _ada_kernel,
        out_shape=jax.ShapeDtypeStruct((depth, N_COND, n), F32),
        grid=(depth, n // tn),
        in_specs=[
            pl.BlockSpec((N_COND_USED, D_MODEL, LANES), lambda l, j: (0, 0, 0)),
            pl.BlockSpec((1, D_MODEL, tn), lambda l, j: (l, 0, j)),
            pl.BlockSpec((1, 1, tn), lambda l, j: (l, 0, j)),
        ],
        out_specs=pl.BlockSpec((1, N_COND, tn), lambda l, j: (l, 0, j)),
        scratch_shapes=[pltpu.VMEM((N_COND_USED, D_MODEL, LANES), F32)],
        compiler_params=_params(40, "arbitrary", "arbitrary"),
        name="ada_modulation",
    )(cond_cols, ada_w, ada_b.reshape(depth, 1, n))


def _hg_in_kernel(xp_ref, xs_ref, mod_ref, nw_ref, w_ref, lbl_ref,
                  q_ref, v_ref, g_ref, lff_ref, kf_ref, lfb_ref, kb_ref):
    i = pl.program_id(0) * (HG_IN_TM // TM)
    cond = _tile_cond(i)
    shift, scale = _mod_row(mod_ref, cond, 0), _mod_row(mod_ref, cond, 1)

    def normed(rows):
        x = jnp.where(i < PROMPT_TILES, xp_ref[rows, :], xs_ref[rows, :])
        return _norm_mod(x, nw_ref[...], shift, scale).astype(BF16)

    def proj(h, c, cols):
        return jnp.dot(h, w_ref[:, c * D_MODEL + cols.start:c * D_MODEL + cols.stop], preferred_element_type=F32)

    l0, l1, l2 = lbl_ref[0], lbl_ref[1], lbl_ref[2]
    m = jnp.maximum(jnp.maximum(l0, l1), l2)
    e0, e1, e2 = jnp.exp(l0 - m), jnp.exp(l1 - m), jnp.exp(l2 - m)
    lb = e0 / (e0 + e1 + e2)
    col_tiles = [slice(n * HG_IN_TN, (n + 1) * HG_IN_TN) for n in range(D_MODEL // HG_IN_TN)]

    def gate_and_plain(h, rows, d):
        lf_ref, k_ref, plain_ref = ((lff_ref, kf_ref, q_ref), (lfb_ref, kb_ref, v_ref))[d]
        for cols in col_tiles:
            lbd = lb[d:d + 1, cols]
            z_gate = proj(h, 2 + d, cols)
            z_plain = proj(h, d, cols)
            f = lbd + (1.0 - lbd) * jax.nn.sigmoid(z_gate)
            lf = jnp.log(f)
            hi = lf.astype(BF16)
            lf_ref[0, rows, cols] = hi
            lf_ref[1, rows, cols] = (lf - hi.astype(F32)).astype(BF16)
            k_ref[rows, cols] = (1.0 - f).astype(BF16)
            plain_ref[rows, cols] = z_plain.astype(BF16)

    sub_tiles = [slice(s * HG_IN_SUB, (s + 1) * HG_IN_SUB) for s in range(HG_IN_TM // HG_IN_SUB)]
    h = normed(sub_tiles[0])
    for s, rows in enumerate(sub_tiles):
        gate_and_plain(h, rows, 0)
        h_next = normed(sub_tiles[s + 1]) if s + 1 < len(sub_tiles) else None
        gate_and_plain(h, rows, 1)
        for cols in col_tiles:
            g_ref[rows, cols] = proj(h, 4, cols).astype(BF16)
        h = h_next


def _hg_in(xp, xs, mod, nw, w_in, lb_logits):
    tile = lambda i: (i, 0)
    fixed2 = lambda i: (0, 0)
    bf = jax.ShapeDtypeStruct((N_TOK, D_MODEL), BF16)
    ff = jax.ShapeDtypeStruct((2, N_TOK, D_MODEL), BF16)
    blk = pl.BlockSpec((HG_IN_TM, D_MODEL), tile)
    split_blk = pl.BlockSpec((2, HG_IN_TM, D_MODEL), lambda i: (0, i, 0))
    prompt_steps = N_PROMPT // HG_IN_TM
    return pl.pallas_call(
        _hg_in_kernel,
        out_shape=(bf, bf, bf, ff, bf, ff, bf),
        grid=(N_TOK // HG_IN_TM,),
        in_specs=[
            pl.BlockSpec((HG_IN_TM, D_MODEL), lambda i: (jnp.minimum(i, prompt_steps - 1), 0)),
            pl.BlockSpec((HG_IN_TM, D_MODEL), lambda i: (jnp.maximum(i - prompt_steps, 0), 0)),
            pl.BlockSpec(mod.shape, fixed2),
            pl.BlockSpec((1, D_MODEL), fixed2),
            pl.BlockSpec(w_in.shape, fixed2, pipeline_mode=pl.Buffered(1)),
            pl.BlockSpec(lb_logits.shape, lambda i: (0, 0, 0)),
        ],
        out_specs=(blk, blk, blk, split_blk, blk, split_blk, blk),
        compiler_params=_params(56, "arbitrary"),
        name="hgrn_in_proj",
    )(xp, xs, mod, nw, w_in, lb_logits)


def _gla_exact(reverse, q_ref, v_ref, lf_ref, k_ref, st_ref, o_ref):
    nt = (((1,), (1,)), ((), ()))
    tn = (((0,), (0,)), ((), ()))
    n_chunks = TM // CHUNK
    row_id = lax.broadcasted_iota(jnp.int32, (CHUNK, HG_DK), 0)
    order = range(CHUNK - 1, -1, -1) if reverse else range(CHUNK)

    def chunk_head(it, carry):
        step, h = it // HG_HEADS, it % HG_HEADS
        ci = (n_chunks - 1 - step) if reverse else step
        rows = pl.ds(pl.multiple_of(ci * CHUNK, CHUNK), CHUNK)
        cols = pl.ds(pl.multiple_of(h * HG_DK, HG_DK), HG_DK)
        q, k, v = q_ref[rows, cols], k_ref[rows, cols], v_ref[rows, cols]
        f = jnp.exp(lf_ref[0, rows, cols].astype(F32) + lf_ref[1, rows, cols].astype(F32))
        st = st_ref[0, h]
        o = jnp.zeros((CHUNK, HG_DK), F32)
        for t in order:
            one = row_id == t
            v_t = jnp.where(one, v, jnp.zeros_like(v))
            st = st * f[t:t + 1, :] + lax.dot_general(v_t, k, tn, preferred_element_type=F32)
            o = jnp.where(one, lax.dot_general(q, st.astype(BF16), nt, preferred_element_type=F32), o)
        st_ref[0, h] = st
        o_ref[rows, cols] = o.astype(o_ref.dtype)
        return carry

    lax.fori_loop(0, n_chunks * HG_HEADS, chunk_head, 0)


class _GlaDirection(NamedTuple):
    reverse: bool
    q: Any
    v: Any
    lf: Any
    k: Any
    o: Any
    st: Any
    sw: Any
    ut: Any
    qd: Any
    kd: Any
    ki: Any
    cd: Any
    b: Any


_NT = (((1,), (1,)), ((), ()))
_TN = (((0,), (0,)), ((), ()))
_HG_COLS = [slice(h * HG_DK, (h + 1) * HG_DK) for h in range(HG_HEADS)]
_N_CHUNKS = TM // CHUNK
_N_PAIRS = _N_CHUNKS // 2


def _gla_prepare(d):
    r = lax.broadcasted_iota(jnp.int32, (TM, TM), 0)
    c = lax.broadcasted_iota(jnp.int32, (TM, TM), 1)
    same = (r // CHUNK) == (c // CHUNK)
    tri = (same & ((c >= r) if d.reverse else (c <= r))).astype(BF16)
    b = jnp.dot(tri, d.lf[0], preferred_element_type=F32) + jnp.dot(tri, d.lf[1], preferred_element_type=F32)
    d.b[...] = b
    edge = 0 if d.reverse else CHUNK - 1
    total = b.reshape(_N_CHUNKS, CHUNK, D_MODEL)[:, edge, :]
    d.cd[...] = jnp.exp(total)
    return jnp.max(-total) <= DECAY_CLAMP


def _gla_pair_rows(d, step):
    pi = (_N_PAIRS - 1 - step) if d.reverse else step
    row0 = pi * 2 * CHUNK
    lo, hi = (pl.ds(row0, CHUNK), 2 * pi), (pl.ds(row0 + CHUNK, CHUNK), 2 * pi + 1)
    return pl.ds(row0, 2 * CHUNK), ((hi, lo) if d.reverse else (lo, hi))


def _chunk_decay(d, chunk):
    return d.cd[pl.ds(chunk, 1), :]


def _gla_decayed_operands(d, step):
    both, ((_, c_first), (_, c_second)) = _gla_pair_rows(d, step)
    b = d.b[both, :]
    ki = d.k[both, :] * jnp.exp(jnp.minimum(-b, DECAY_CLAMP)).astype(BF16)
    d.qd[both, :] = d.q[both, :] * jnp.exp(b).astype(BF16)
    d.ki[both, :] = ki
    c_lo, c_hi = (c_second, c_first) if d.reverse else (c_first, c_second)
    ends = jnp.concatenate([jnp.broadcast_to(_chunk_decay(d, c).astype(BF16), (CHUNK, D_MODEL)) for c in (c_lo, c_hi)],
                           axis=0)
    d.kd[both, :] = ki * ends


def _rows_scaled(x, scale_row, second_half):
    scale = jnp.broadcast_to(scale_row.astype(BF16), (CHUNK, x.shape[1]))
    ones = jnp.ones((CHUNK, x.shape[1]), BF16)
    return x * jnp.concatenate([ones, scale] if second_half else [scale, ones], axis=0)


def _gla_key_value_product(d, step):
    _gla_decayed_operands(d, step)
    both, (_, (_, c_second)) = _gla_pair_rows(d, step)
    decay_second = _chunk_decay(d, c_second)
    for h, cols in enumerate(_HG_COLS):
        keys = _rows_scaled(d.kd[both, cols], decay_second[:, cols], second_half=d.reverse)
        d.ut[h] = lax.dot_general(d.v[both, cols], keys, _TN, preferred_element_type=F32)


def _gla_start(d):
    for h in range(HG_HEADS):
        d.sw[0, h] = d.st[0, h].T.astype(BF16)
    _gla_key_value_product(d, 0)


def _gla_pair(d, step, src, dst):
    sr = lax.broadcasted_iota(jnp.int32, (CHUNK, CHUNK), 0)
    sc = lax.broadcasted_iota(jnp.int32, (CHUNK, CHUNK), 1)
    tr = lax.broadcasted_iota(jnp.int32, (CHUNK, 2 * CHUNK), 0)
    tc = lax.broadcasted_iota(jnp.int32, (CHUNK, 2 * CHUNK), 1)
    if d.reverse:
        keep_first = sc >= sr
        keep_second = (tc >= tr) | (tc >= CHUNK)
    else:
        keep_first = sc <= sr
        keep_second = (tc < CHUNK) | (tc - CHUNK <= tr)
    both, ((first, c_first), (second, c_second)) = _gla_pair_rows(d, step)
    decay_first = _chunk_decay(d, c_first)
    decay_both = decay_first * _chunk_decay(d, c_second)
    for h, cols in enumerate(_HG_COLS):
        new = d.st[src, h] * decay_both[:, cols] + d.ut[h]
        d.st[dst, h] = new
        d.sw[dst, h] = new.T.astype(BF16)
    a_first = [lax.dot_general(d.qd[first, cols], d.ki[first, cols], _NT, preferred_element_type=F32)
               for cols in _HG_COLS]
    a_second = []
    for cols in _HG_COLS:
        lo_keys = (d.ki if d.reverse else d.kd)[pl.ds(both.start, CHUNK), cols]
        hi_keys = (d.kd if d.reverse else d.ki)[pl.ds(both.start + CHUNK, CHUNK), cols]
        keys = jnp.concatenate([lo_keys, hi_keys], axis=0)
        a_second.append(lax.dot_general(d.qd[second, cols], keys, _NT, preferred_element_type=F32))
    inter = [jnp.dot(_rows_scaled(d.qd[both, cols], decay_first[:, cols], second_half=not d.reverse),
                     d.sw[src, h], preferred_element_type=F32) for h, cols in enumerate(_HG_COLS)]
    if step + 1 < _N_PAIRS:
        _gla_key_value_product(d, step + 1)
    first_half, second_half = (slice(CHUNK, None), slice(None, CHUNK)) if d.reverse else \
                              (slice(None, CHUNK), slice(CHUNK, None))
    for h, cols in enumerate(_HG_COLS):
        am = jnp.where(keep_first, a_first[h], 0.0).astype(BF16)
        d.o[first, cols] = (jnp.dot(am, d.v[first, cols], preferred_element_type=F32)
                            + inter[h][first_half]).astype(d.o.dtype)
        am = jnp.where(keep_second, a_second[h], 0.0).astype(BF16)
        d.o[second, cols] = (jnp.dot(am, d.v[both, cols], preferred_element_type=F32)
                             + inter[h][second_half]).astype(d.o.dtype)


def _gla_kernel(qf_ref, vf_ref, lff_ref, kf_ref, qb_ref, vb_ref, lfb_ref, kb_ref, s0_ref,
                of_ref, ob_ref, sout_ref, st_scr, sw_scr, ut_scr, qd_scr, kd_scr, ki_scr, cd_scr, b_scr):
    i = pl.program_id(0)
    is_prompt = i < PROMPT_TILES
    first = jnp.logical_or(is_prompt, (i - PROMPT_TILES) % TILES_PER_DEC_SEQ == 0)
    work = [tuple(scr.at[n] for scr in (st_scr, sw_scr, ut_scr, qd_scr, kd_scr, ki_scr, cd_scr, b_scr))
            for n in range(2)]
    fwd = _GlaDirection(False, qf_ref, vf_ref, lff_ref, kf_ref, of_ref, *work[0])
    bwd = _GlaDirection(True, qb_ref, vb_ref, lfb_ref, kb_ref, ob_ref, *work[1])

    @pl.when(is_prompt)
    def _():
        for d in (fwd, bwd):
            d.st[0] = jnp.zeros((HG_HEADS, HG_DK, HG_DK), F32)

    @pl.when(jnp.logical_and(first, jnp.logical_not(is_prompt)))
    def _():
        for n, d in enumerate((fwd, bwd)):
            for h in range(HG_HEADS):
                d.st[0, h] = s0_ref[0, 0, n, h].T

    exact_scores = jnp.logical_and(_gla_prepare(fwd), _gla_prepare(bwd))

    @pl.when(exact_scores)
    def _():
        _gla_start(fwd)
        _gla_start(bwd)

        for step in range(_N_PAIRS):
            for d in (fwd, bwd):
                _gla_pair(d, step, step % 2, 1 - step % 2)

    @pl.when(jnp.logical_not(exact_scores))
    def _():
        for d in (fwd, bwd):
            _gla_exact(d.reverse, d.q, d.v, d.lf, d.k, d.st, d.o)

    @pl.when(is_prompt)
    def _():
        for n, d in enumerate((fwd, bwd)):
            for h in range(HG_HEADS):
                sout_ref[0, 0, n, h] = d.st[0, h].T


def _gla(q, v, lff, kf, lfb, kb, state_hgrn):
    def fwd_tile(i):
        return (i, 0)

    def bwd_tile(i):
        j = (i - PROMPT_TILES) % TILES_PER_DEC_SEQ
        return (jnp.where(i < PROMPT_TILES, i, i - j + (TILES_PER_DEC_SEQ - 1 - j)), 0)

    def s0_idx(i):
        return (jnp.maximum(i - PROMPT_TILES, 0) // TILES_PER_DEC_SEQ, 0, 0, 0, 0, 0)

    def sout_idx(i):
        return (jnp.minimum(i, PROMPT_TILES - 1), 0, 0, 0, 0, 0)

    f_blk = pl.BlockSpec((TM, D_MODEL), fwd_tile)
    b_blk = pl.BlockSpec((TM, D_MODEL), bwd_tile)
    f_split = pl.BlockSpec((2, TM, D_MODEL), lambda i: (0,) + fwd_tile(i))
    b_split = pl.BlockSpec((2, TM, D_MODEL), lambda i: (0,) + bwd_tile(i))
    st_blk = (1, 1, 2, HG_HEADS, HG_DK, HG_DK)
    return pl.pallas_call(
        _gla_kernel,
        out_shape=(
            jax.ShapeDtypeStruct((N_TOK, D_MODEL), BF16),
            jax.ShapeDtypeStruct((N_TOK, D_MODEL), BF16),
            jax.ShapeDtypeStruct((BATCH,) + st_blk[1:], F32),
        ),
        grid=(N_TILES,),
        in_specs=[f_blk, f_blk, f_split, f_blk, b_blk, b_blk, b_split, b_blk, pl.BlockSpec(st_blk, s0_idx)],
        out_specs=(f_blk, b_blk, pl.BlockSpec(st_blk, sout_idx)),
        scratch_shapes=[
            pltpu.VMEM((2, 2, HG_HEADS, HG_DK, HG_DK), F32),
            pltpu.VMEM((2, 2, HG_HEADS, HG_DK, HG_DK), BF16),
            pltpu.VMEM((2, HG_HEADS, HG_DK, HG_DK), F32),
            pltpu.VMEM((2, TM, D_MODEL), BF16),
            pltpu.VMEM((2, TM, D_MODEL), BF16),
            pltpu.VMEM((2, TM, D_MODEL), BF16),
            pltpu.VMEM((2, TM // CHUNK, D_MODEL), F32),
            pltpu.VMEM((2, TM, D_MODEL), F32),
        ],
        compiler_params=_params(48, "arbitrary"),
        name="hgrn_recurrence",
    )(q, v, lff, kf, q, v, lfb, kb, state_hgrn)


def _route(logits):
    lane = lax.broadcasted_iota(jnp.int32, logits.shape, 1)
    neg = jnp.float32(-jnp.inf)
    lane_f = lane.astype(F32)

    def first_max(x):
        m = jnp.max(x, axis=-1, keepdims=True)
        first = jnp.min(jnp.where(x == m, lane_f, float(ROUTER_LANES)), axis=-1, keepdims=True)
        return m, first.astype(jnp.int32)

    is_group = (lane >= N_EXPERTS) & (lane < N_EXPERTS + N_GROUPS)
    gl = jnp.where(is_group, logits, neg)
    gmax, g_lane = first_max(gl)
    g_sel = g_lane - N_EXPERTS
    gsum = jnp.sum(jnp.exp(gl - gmax), axis=-1, keepdims=True)
    p_g = 1.0 / gsum
    in_sel = (lane < N_EXPERTS) & ((lane // EXPERTS_PER_GROUP) == g_sel)
    m1, i1 = first_max(jnp.where(in_sel, logits, neg))
    m2, i2 = first_max(jnp.where(in_sel & (lane != i1), logits, neg))
    e2 = jnp.exp(m2 - m1)
    return i1, i2, p_g / (1.0 + e2), p_g * e2 / (1.0 + e2)


def _pack_rows(x):
    q = PACK_WORDS
    bits = pltpu.bitcast(x.astype(BF16).astype(F32), jnp.uint32)
    return [(bits[:, (2 + h) * q:(3 + h) * q] & jnp.uint32(0xFFFF0000)) | (bits[:, h * q:(h + 1) * q] >> 16)
            for h in range(2)]


def _unpack_rows(half0, half1):
    lo = lambda w: pltpu.bitcast(w << 16, F32).astype(BF16)
    hi = lambda w: pltpu.bitcast(w & jnp.uint32(0xFFFF0000), F32).astype(BF16)
    return [lo(half0), lo(half1), hi(half0), hi(half1)]


def _mix_stream_rows(prompt_ref, sample_ref, rows):
    return jnp.where(pl.program_id(0) < N_PROMPT // MIX_TM, prompt_ref[rows, :], sample_ref[rows, :])


def _mixer_tail(mix_of, x_of, mod_ref, nw2_ref, wo_ref, wr_ref, br_ref, x1_ref, h2p_ref, lg_ref):
    cond = _tile_cond(pl.program_id(0) * (MIX_TM // TM))
    gate, shift, scale = (_mod_row(mod_ref, cond, which) for which in (2, 3, 4))
    project = lambda rows: jnp.dot(mix_of(rows), wo_ref[...], preferred_element_type=F32)

    def tail(rows, out):
        x1 = x_of(rows) + gate * out
        x1_ref[rows, :] = x1
        h2 = _norm_mod(x1, nw2_ref[...], shift, scale)
        h2p_ref[rows, :] = jnp.concatenate(_pack_rows(h2), axis=1)
        h_hi = h2.astype(BF16)
        h_lo = (h2 - h_hi.astype(F32)).astype(BF16)
        lg_ref[rows, :] = (jnp.dot(h_hi, wr_ref[0], preferred_element_type=F32)
                           + (jnp.dot(h_hi, wr_ref[1], preferred_element_type=F32)
                              + jnp.dot(h_lo, wr_ref[0], preferred_element_type=F32))) + br_ref[...]

    sub_tiles = [slice(s * TM, (s + 1) * TM) for s in range(MIX_TM // TM)]
    out = project(sub_tiles[0])
    for s, rows in enumerate(sub_tiles):
        out_next = project(sub_tiles[s + 1]) if s + 1 < len(sub_tiles) else None
        tail(rows, out)
        out = out_next


def _route_kernel(lg_ref, rt_ref, rw_ref, cnt_ref, carry_scr):
    @pl.when(pl.program_id(0) == 0)
    def _():
        carry_scr[...] = jnp.zeros_like(carry_scr)

    logits = lg_ref[...]
    i1, i2, w1, w2 = _route(logits)
    lane = lax.broadcasted_iota(jnp.int32, logits.shape, 1)
    chosen = ((lane == i1) | (lane == i2)).astype(BF16)
    r = lax.broadcasted_iota(jnp.int32, (ROUTE_TM, ROUTE_TM), 0)
    c = lax.broadcasted_iota(jnp.int32, (ROUTE_TM, ROUTE_TM), 1)
    before = jnp.dot((c < r).astype(BF16), chosen, preferred_element_type=F32) + carry_scr[...]
    r1 = jnp.sum(jnp.where(lane == i1, before, 0.0), axis=-1, keepdims=True).astype(jnp.int32)
    r2 = jnp.sum(jnp.where(lane == i2, before, 0.0), axis=-1, keepdims=True).astype(jnp.int32)
    total = carry_scr[...] + jnp.sum(chosen.astype(F32), axis=0, keepdims=True)
    carry_scr[...] = total
    tiles = jnp.ceil(total * (1.0 / MOE_TILE))
    e_from = lax.broadcasted_iota(jnp.int32, (ROUTER_LANES, ROUTER_LANES), 0)
    e_to = lax.broadcasted_iota(jnp.int32, (ROUTER_LANES, ROUTER_LANES), 1)
    tiles_before = jnp.dot(jnp.broadcast_to(tiles, (SUBLANES, ROUTER_LANES)).astype(BF16),
                           (e_from < e_to).astype(BF16), preferred_element_type=F32)
    cnt_ref[...] = jnp.concatenate([total, tiles_before[:1] * MOE_TILE,
                                    jnp.zeros((ROUTE_ROWS - 2, ROUTER_LANES), F32)], axis=0)
    by_token = jnp.where(lane == 0, i1, jnp.where(lane == 1, i2, jnp.where(lane == 2, r1, r2)))
    rt_ref[...] = by_token.T[:ROUTE_ROWS, :]
    rw_ref[...] = jnp.where(lane == 0, w1, w2)


def _route_tokens(logits):
    blk = pl.BlockSpec((ROUTE_TM, ROUTER_LANES), lambda i: (i, 0))
    return pl.pallas_call(
        _route_kernel,
        out_shape=(
            jax.ShapeDtypeStruct((ROUTE_ROWS, N_TOK), jnp.int32),
            jax.ShapeDtypeStruct((N_TOK, ROUTER_LANES), F32),
            jax.ShapeDtypeStruct((ROUTE_ROWS, ROUTER_LANES), F32),
        ),
        grid=(N_TOK // ROUTE_TM,),
        in_specs=[blk],
        out_specs=(pl.BlockSpec((ROUTE_ROWS, ROUTE_TM), lambda i: (0, i)), blk,
                   pl.BlockSpec((ROUTE_ROWS, ROUTER_LANES), lambda i: (0, 0))),
        scratch_shapes=[pltpu.VMEM((1, ROUTER_LANES), F32)],
        compiler_params=_params(32, "arbitrary"),
        name="moe_route",
    )(logits)


def _hg_out_kernel(of_ref, ob_ref, g_ref, on_ref, xp_ref, xs_ref, *rest):
    def mix_of(rows):
        o = of_ref[rows, :].astype(F32) + ob_ref[rows, :].astype(F32)
        parts = []
        for h in range(HG_HEADS):
            oh = o[:, h * HG_DK:(h + 1) * HG_DK]
            parts.append(oh * lax.rsqrt(jnp.mean(oh * oh, axis=-1, keepdims=True) + EPS) * on_ref[...])
        y = jnp.concatenate(parts, axis=1) * _silu(g_ref[rows, :].astype(F32))
        return y.astype(BF16)

    _mixer_tail(mix_of, functools.partial(_mix_stream_rows, xp_ref, xs_ref), *rest)


def _at_out_kernel(ac_ref, al_ref, x_ref, *rest):
    _mixer_tail(functools.partial(_mix_stream_rows, ac_ref, al_ref), lambda rows: x_ref[rows, :], *rest)


def _mixer_out(kernel_fn, name, mix_inputs, mix_specs, mod, nw2, w_out, w_router, b_router):
    tile = lambda i: (i, 0)
    fixed2 = lambda i: (0, 0)
    blk = pl.BlockSpec((MIX_TM, D_MODEL), tile)
    lanes_blk = pl.BlockSpec((MIX_TM, ROUTER_LANES), tile)
    return pl.pallas_call(
        kernel_fn,
        out_shape=(
            jax.ShapeDtypeStruct((N_TOK, D_MODEL), F32),
            jax.ShapeDtypeStruct((N_TOK, ROW_WORDS), jnp.uint32),
            jax.ShapeDtypeStruct((N_TOK, ROUTER_LANES), F32),
        ),
        grid=(N_TOK // MIX_TM,),
        in_specs=list(mix_specs) + [
            pl.BlockSpec(mod.shape, fixed2),
            pl.BlockSpec((1, D_MODEL), fixed2),
            pl.BlockSpec(w_out.shape, fixed2),
            pl.BlockSpec(w_router.shape, lambda i: (0, 0, 0)),
            pl.BlockSpec((1, ROUTER_LANES), fixed2),
        ],
        out_specs=(blk, pl.BlockSpec((MIX_TM, ROW_WORDS), tile), lanes_blk),
        compiler_params=_params(40, "arbitrary"),
        name=name,
    )(*mix_inputs, mod, nw2, w_out, w_router, b_router)


def _moe_plan(route_t, counts):
    cnt = counts[0, :N_EXPERTS].astype(jnp.int32)
    offs = counts[1, :N_EXPERTS].astype(jnp.int32)
    ends = offs + ((cnt + MOE_TILE - 1) // MOE_TILE) * MOE_TILE
    experts = route_t[0:2]
    pos = jnp.sum(jnp.where(experts[None, :, :] == jnp.arange(N_EXPERTS)[:, None, None], offs[:, None, None], 0),
                  axis=0) + route_t[2:4]
    tile_start = jnp.arange(MOE_ROWS // MOE_TILE, dtype=jnp.int32) * MOE_TILE
    tile_expert = jnp.minimum(jnp.sum(ends[None, :] <= tile_start[:, None], axis=1), N_EXPERTS - 1).astype(jnp.int32)
    of_tile = tile_expert[:, None] == jnp.arange(N_EXPERTS)[None, :]
    tile_offs = jnp.sum(jnp.where(of_tile, offs[None, :], 0), axis=1)
    tile_cnt = jnp.sum(jnp.where(of_tile, cnt[None, :], 0), axis=1)
    tile_rows = jnp.clip(tile_offs + tile_cnt - tile_start, 0, MOE_TILE).astype(jnp.int32)
    active = tile_start < ends[-1]
    tile_opens = jnp.where(active & (tile_start == tile_offs), tile_expert, -1).astype(jnp.int32)
    tile_opens = jnp.concatenate([tile_opens, jnp.full((MOE_STEP_TILES,), -1, jnp.int32)])
    n_active = (ends[-1] // MOE_TILE).astype(jnp.int32).reshape(1)
    slot = (jnp.cumsum((cnt > 0).astype(jnp.int32)) - 1) % MOE_WEIGHT_SLOTS
    tile_slot = jnp.sum(jnp.where(of_tile, slot[None, :], 0), axis=1)
    last_slot = jnp.sum(jnp.where(tile_start == ends[-1] - MOE_TILE, tile_slot, 0))
    tile_slot = jnp.where(active, tile_slot, last_slot).astype(jnp.int32)
    return pos.astype(jnp.int32), (tile_rows, tile_slot, tile_opens, n_active)


SC_WINDOW = 64


def _sc_mesh():
    return plsc.VectorSubcoreMesh(core_axis_name="c", subcore_axis_name="s")


def _sc_scatter_rows(x, idx_a, idx_b, n_out_rows):
    n = x.shape[0]

    @functools.partial(pl.kernel, out_type=jax.ShapeDtypeStruct((n_out_rows, ROW_WORDS), x.dtype), mesh=_sc_mesh(),
                       scratch_types=[pltpu.SemaphoreType.DMA, pltpu.SemaphoreType.DMA])
    def scatter(x_hbm, ia_hbm, ib_hbm, o_hbm, sem_a, sem_b):
        def body(x_vmem, ia_vmem, ib_vmem):
            copy_a = pltpu.async_copy(x_vmem, o_hbm.at[ia_vmem.at[0]], sem_a)
            copy_b = pltpu.async_copy(x_vmem, o_hbm.at[ib_vmem.at[0]], sem_b)
            copy_a.wait()
            copy_b.wait()

        idx_spec = pl.BlockSpec((1, SC_WINDOW), index_map=lambda i: (i, 0))
        pltpu.emit_pipeline(
            body, grid=(n // SC_WINDOW,),
            in_specs=[pl.BlockSpec((SC_WINDOW, ROW_WORDS), index_map=lambda i: (i, 0)), idx_spec, idx_spec],
            out_specs=[],
            core_axis_name=("c", "s"), dimension_semantics=(pltpu.PARALLEL,),
        )(x_hbm, ia_hbm, ib_hbm)

    return scatter(x, idx_a, idx_b)


def _sc_gather_rows(table, idx):
    n = idx.size

    @functools.partial(pl.kernel, out_type=jax.ShapeDtypeStruct((n, ROW_WORDS), table.dtype), mesh=_sc_mesh())
    def gather(t_hbm, i_hbm, o_hbm):
        def body(i_vmem, o_vmem):
            pltpu.sync_copy(t_hbm.at[i_vmem.at[0]], o_vmem)

        pltpu.emit_pipeline(
            body, grid=(n // SC_WINDOW,),
            in_specs=[pl.BlockSpec((1, SC_WINDOW), index_map=lambda i: (i, 0))],
            out_specs=[pl.BlockSpec((SC_WINDOW, ROW_WORDS), index_map=lambda i: (i, 0))],
            core_axis_name=("c", "s"), dimension_semantics=(pltpu.PARALLEL,),
        )(i_hbm, o_hbm)

    return gather(table, idx)


def _ffn_kernel(tr_ref, ts_ref, to_ref, na_ref, xs_ref, w1_hbm, w3_hbm, w2_hbm, ys_ref,
                w1_buf, w3_buf, w2_buf, sem, *, layer):
    def weight_copies(t):
        pairs = ((w1_hbm, w1_buf), (w3_hbm, w3_buf), (w2_hbm, w2_buf))
        return [pltpu.make_async_copy(hbm.at[layer, to_ref[t]], buf.at[ts_ref[t]], sem.at[ts_ref[t], j])
                for j, (hbm, buf) in enumerate(pairs)]

    def for_experts_opened_from(first_tile, action):
        for j in range(MOE_STEP_TILES):
            t = first_tile + j

            @pl.when(to_ref[t] >= 0)
            def _():
                for copy in weight_copies(t):
                    action(copy)

    def tile(t, rows):
        slot = ts_ref[t]
        row = lax.broadcasted_iota(jnp.int32, (MOE_TILE, PACK_WORDS), 0)
        live = row < tr_ref[t]
        halves = [jnp.where(live, xs_ref[rows, h * PACK_WORDS:(h + 1) * PACK_WORDS],
                            jnp.zeros((MOE_TILE, PACK_WORDS), jnp.uint32)) for h in range(2)]
        chunks = _unpack_rows(*halves)

        def up(w_buf):
            acc = None
            for k, chunk in enumerate(chunks):
                w = w_buf[slot, k * PACK_WORDS:(k + 1) * PACK_WORDS, :].astype(BF16)
                part = jnp.dot(chunk, w, preferred_element_type=F32)
                acc = part if acc is None else acc + part
            return acc

        hid = (_silu(up(w1_buf)) * up(w3_buf)).astype(BF16)
        y = jnp.dot(hid, w2_buf[slot].astype(BF16), preferred_element_type=F32)
        ys_ref[rows, :] = jnp.concatenate(_pack_rows(y), axis=1)

    step = pl.program_id(0)
    first_tile = step * MOE_STEP_TILES

    @pl.when(first_tile < na_ref[0])
    def _():
        @pl.when(step == 0)
        def _():
            for_experts_opened_from(first_tile, lambda copy: copy.start())

        for_experts_opened_from(first_tile + MOE_STEP_TILES, lambda copy: copy.start())
        for_experts_opened_from(first_tile, lambda copy: copy.wait())
        for j in range(MOE_STEP_TILES):
            tile(first_tile + j, slice(j * MOE_TILE, (j + 1) * MOE_TILE))


def _ffn(xs, tables, layer, w1, w3, w2):
    step_rows = MOE_STEP_TILES * MOE_TILE
    row_tile = lambda s, tr, ts, to, na: (jnp.minimum(s, (na[0] - 1) // MOE_STEP_TILES), 0)
    hbm = pl.BlockSpec(memory_space=pl.ANY)
    return pl.pallas_call(
        functools.partial(_ffn_kernel, layer=layer),
        out_shape=jax.ShapeDtypeStruct((MOE_ROWS, ROW_WORDS), jnp.uint32),
        grid_spec=pltpu.PrefetchScalarGridSpec(
            num_scalar_prefetch=len(tables),
            grid=(MOE_ROWS // step_rows,),
            in_specs=[pl.BlockSpec((step_rows, ROW_WORDS), row_tile), hbm, hbm, hbm],
            out_specs=pl.BlockSpec((step_rows, ROW_WORDS), row_tile),
            scratch_shapes=[
                pltpu.VMEM((MOE_WEIGHT_SLOTS, D_MODEL, D_EXPERT), F32),
                pltpu.VMEM((MOE_WEIGHT_SLOTS, D_MODEL, D_EXPERT), F32),
                pltpu.VMEM((MOE_WEIGHT_SLOTS, D_EXPERT, D_MODEL), F32),
                pltpu.SemaphoreType.DMA((MOE_WEIGHT_SLOTS, 3)),
            ],
        ),
        compiler_params=_params(48, "arbitrary"),
        name="moe_experts",
    )(*tables, xs, w1, w3, w2)


def _moe_res(x_ref, y_ref, rw_ref, mod_ref, cond, rows=slice(None)):
    rw = rw_ref[rows, :]
    wa = rw[:, 0:1]
    wb = rw[:, 1:2]
    ya = _unpack_rows(y_ref[0, rows, :PACK_WORDS], y_ref[0, rows, PACK_WORDS:])
    yb = _unpack_rows(y_ref[1, rows, :PACK_WORDS], y_ref[1, rows, PACK_WORDS:])
    y = jnp.concatenate([wa * a.astype(F32) + wb * b.astype(F32) for a, b in zip(ya, yb)], axis=1)
    return x_ref[rows, :] + _mod_row(mod_ref, cond, 5) * y


def _moe_res_final_kernel(x_ref, y_ref, rw_ref, mod_ref, fn_ref, op_ref, os_ref):
    x = _moe_res(x_ref, y_ref, rw_ref, mod_ref, _tile_cond(pl.program_id(0) * (COMBINE_TM // TM)))
    y = x * lax.rsqrt(jnp.mean(x * x, axis=-1, keepdims=True) + EPS) * fn_ref[...]
    is_prompt = pl.program_id(0) < N_PROMPT // COMBINE_TM

    @pl.when(is_prompt)
    def _():
        op_ref[...] = y

    @pl.when(jnp.logical_not(is_prompt))
    def _():
        os_ref[...] = y


def _moe_combine(x1, y_pairs, route_w, mod, final_norm):
    tile = lambda i: (i, 0)
    fixed2 = lambda i: (0, 0)
    prompt_steps = N_PROMPT // COMBINE_TM
    return pl.pallas_call(
        _moe_res_final_kernel,
        out_shape=(jax.ShapeDtypeStruct((N_PROMPT, D_MODEL), F32),
                   jax.ShapeDtypeStruct((N_TOK - N_PROMPT, D_MODEL), F32)),
        grid=(N_TOK // COMBINE_TM,),
        in_specs=[pl.BlockSpec((COMBINE_TM, D_MODEL), tile),
                  pl.BlockSpec((2, COMBINE_TM, ROW_WORDS), lambda i: (0, i, 0)),
                  pl.BlockSpec((COMBINE_TM, ROUTER_LANES), tile),
                  pl.BlockSpec(mod.shape, fixed2),
                  pl.BlockSpec((1, D_MODEL), fixed2)],
        out_specs=(pl.BlockSpec((COMBINE_TM, D_MODEL), lambda i: (jnp.minimum(i, prompt_steps - 1), 0)),
                   pl.BlockSpec((COMBINE_TM, D_MODEL), lambda i: (jnp.maximum(i - prompt_steps, 0), 0))),
        compiler_params=_params(48, "arbitrary"),
        name="moe_combine",
    )(x1, y_pairs, route_w, mod, final_norm)


def _moe_layer(x1, h2p, logits, layer, w1, w3, w2):
    route_t, route_w, counts = _route_tokens(logits)
    pos, tables = _moe_plan(route_t, counts)
    windows = pos.reshape(2, N_TOK // SC_WINDOW, SC_WINDOW)
    xs = _sc_scatter_rows(h2p, windows[0], windows[1], MOE_ROWS)
    ys = _ffn(xs, tables, layer, w1, w3, w2)
    y_pairs = _sc_gather_rows(ys, windows.reshape(2 * N_TOK // SC_WINDOW, SC_WINDOW))
    return x1, y_pairs.reshape(2, N_TOK, ROW_WORDS), route_w


def _swap_rotary_halves(x):
    n = x.shape[-1]
    lane = lax.broadcasted_iota(jnp.int32, x.shape, 1)
    quarter = ROPE_HALF // 2
    return jnp.where((lane % ROPE_HALF) < quarter, pltpu.roll(x, n - quarter, 1), pltpu.roll(x, quarter, 1))


def _at_in_kernel(x1_ref, y_ref, rw_ref, mod_prev_ref, mod_ref, nw_ref, w_ref, cos_ref, sin_ref,
                  x_ref, q_ref, kt_ref, v_ref, kc_ref, vc_ref):
    step = pl.program_id(0)
    cond = _tile_cond(step * (AT_IN_TM // TM))
    shift, scale = _mod_row(mod_ref, cond, 0), _mod_row(mod_ref, cond, 1)
    nq = ATT_Q_HEADS * ATT_HEAD_DIM
    nk = ATT_KV_HEADS * ATT_HEAD_DIM

    def normed(rows):
        x = _moe_res(x1_ref, y_ref, rw_ref, mod_prev_ref, cond, rows)
        x_ref[rows, :] = x
        return _norm_mod(x, nw_ref[...], shift, scale).astype(BF16)

    def rope(x, rows):
        reps = x.shape[-1] // LANES
        cos = jnp.concatenate([cos_ref[rows, :]] * reps, axis=1)
        sin = jnp.concatenate([sin_ref[rows, :]] * reps, axis=1)
        return x * cos + _swap_rotary_halves(x) * sin

    def project_q(h):
        return jnp.dot(h, w_ref[:, :nq], preferred_element_type=F32)

    def project_kv(h):
        return (jnp.dot(h, w_ref[:, nq:nq + nk], preferred_element_type=F32),
                jnp.dot(h, w_ref[:, nq + nk:], preferred_element_type=F32))

    def tail(rows, zq, zk, v):
        q_ref[rows, :] = (rope(zq, rows) * ATT_HEAD_DIM ** -0.5).astype(BF16)
        kt = rope(zk, rows).T
        kt_ref[:, rows] = kt.astype(BF16)
        v_ref[rows, :] = v.astype(BF16)
        return kt, v.T

    sub_tiles = [slice(s * TM, (s + 1) * TM) for s in range(AT_IN_TM // TM)]
    cache_rows = []
    h = normed(sub_tiles[0])
    zq = project_q(h)
    for s, rows in enumerate(sub_tiles):
        last = s + 1 == len(sub_tiles)
        h_next = None if last else normed(sub_tiles[s + 1])
        zk, v = project_kv(h)
        zq_next = None if last else project_q(h_next)
        cache_rows.append(tail(rows, zq, zk, v))
        h, zq = h_next, zq_next

    @pl.when(step < N_PROMPT // AT_IN_TM)
    def _():
        for rows, (kt, vt) in zip(sub_tiles, cache_rows):
            kc_ref[rows, :] = kt
            vc_ref[rows, :] = vt


def _rope_tables():
    f32 = np.float32
    pos = np.arange(DEC_SEQ)
    t_row = (pos // GRID_W).astype(f32)
    t_col = (pos % GRID_W).astype(f32)
    inv = f32(ROPE_BASE) ** (-np.arange(0, ROPE_HALF, 2, dtype=f32) / f32(ROPE_HALF))
    j = np.arange(LANES) % ATT_HEAD_DIM
    freq = inv[(j % ROPE_HALF) % (ROPE_HALF // 2)]
    ang = (np.where((j < ROPE_HALF)[None, :], t_row[:, None], t_col[:, None]) * freq[None, :]).astype(f32)
    sign = np.where((j % ROPE_HALF) < ROPE_HALF // 2, -1.0, 1.0).astype(f32)
    cos = np.concatenate([np.ones((AT_IN_TM, LANES), f32), np.cos(ang)], axis=0)
    sin = np.concatenate([np.zeros((AT_IN_TM, LANES), f32), np.sin(ang) * sign[None, :]], axis=0)
    return jnp.asarray(cos, F32), jnp.asarray(sin, F32)


def _at_in(x1, y_pairs, route_w, mod_prev, mod, nw, w_in, cos, sin):
    tile = lambda i: (i, 0)
    fixed2 = lambda i: (0, 0)
    prompt_steps = N_PROMPT // AT_IN_TM
    rope_tile = lambda i: (jnp.where(i < prompt_steps, 0, 1 + (i - prompt_steps) % (DEC_SEQ // AT_IN_TM)), 0)
    prompt_tile = lambda i: (jnp.minimum(i, prompt_steps - 1), 0)
    nk = ATT_KV_HEADS * ATT_HEAD_DIM
    return pl.pallas_call(
        _at_in_kernel,
        out_shape=(
            jax.ShapeDtypeStruct((N_TOK, D_MODEL), F32),
            jax.ShapeDtypeStruct((N_TOK, D_MODEL), BF16),
            jax.ShapeDtypeStruct((nk, N_TOK), BF16),
            jax.ShapeDtypeStruct((N_TOK, nk), BF16),
            jax.ShapeDtypeStruct((N_PROMPT, nk), F32),
            jax.ShapeDtypeStruct((N_PROMPT, nk), F32),
        ),
        grid=(N_TOK // AT_IN_TM,),
        in_specs=[
            pl.BlockSpec((AT_IN_TM, D_MODEL), tile),
            pl.BlockSpec((2, AT_IN_TM, ROW_WORDS), lambda i: (0, i, 0)),
            pl.BlockSpec((AT_IN_TM, ROUTER_LANES), tile),
            pl.BlockSpec(mod_prev.shape, fixed2),
            pl.BlockSpec(mod.shape, fixed2),
            pl.BlockSpec((1, D_MODEL), fixed2),
            pl.BlockSpec(w_in.shape, fixed2),
            pl.BlockSpec((AT_IN_TM, LANES), rope_tile),
            pl.BlockSpec((AT_IN_TM, LANES), rope_tile),
        ],
        out_specs=(pl.BlockSpec((AT_IN_TM, D_MODEL), tile), pl.BlockSpec((AT_IN_TM, D_MODEL), tile),
                   pl.BlockSpec((nk, AT_IN_TM), lambda i: (0, i)),
                   pl.BlockSpec((AT_IN_TM, nk), tile),
                   pl.BlockSpec((AT_IN_TM, nk), prompt_tile), pl.BlockSpec((AT_IN_TM, nk), prompt_tile)),
        compiler_params=_params(40, "arbitrary"),
        name="attn_in_proj",
    )(x1, y_pairs, route_w, mod_prev, mod, nw, w_in, cos, sin)


def _attend(q, kt_all, v_all, mask, sink_ref, o_ref):
    nq = q.shape[0]
    group_lanes = ATT_GROUP * ATT_HEAD_DIM
    lane = lax.broadcasted_iota(jnp.int32, (nq, group_lanes), 1)
    mine = [(lane // ATT_HEAD_DIM) == g for g in range(ATT_GROUP)]
    row_head = lax.broadcasted_iota(jnp.int32, (ATT_GROUP * nq, 1), 0) // nq
    if mask is not None:
        mask = jnp.concatenate([mask] * ATT_GROUP, axis=0)

    def scores(hk):
        kt = jnp.concatenate([kt_all[hk * ATT_HEAD_DIM:(hk + 1) * ATT_HEAD_DIM, :]] * ATT_GROUP, axis=0)
        qg = q[:, hk * group_lanes:(hk + 1) * group_lanes]
        q_stack = jnp.concatenate([jnp.where(mine[g], qg, jnp.zeros_like(qg)) for g in range(ATT_GROUP)], axis=0)
        s = jnp.dot(q_stack, kt, preferred_element_type=F32)
        return s if mask is None else jnp.where(mask, s, -jnp.inf)

    s_next = scores(0)
    for hk in range(ATT_KV_HEADS):
        s = s_next
        if hk + 1 < ATT_KV_HEADS:
            s_next = scores(hk + 1)
        vh = v_all[:, hk * ATT_HEAD_DIM:(hk + 1) * ATT_HEAD_DIM]
        vt = jnp.concatenate([vh] * ATT_GROUP, axis=1)
        sink = jnp.zeros((ATT_GROUP * nq, 1), F32)
        for g in range(ATT_GROUP):
            sink = jnp.where(row_head == g, sink_ref[hk * ATT_GROUP + g], sink)
        m = jnp.maximum(jnp.max(s, axis=-1, keepdims=True), sink)
        p = jnp.exp(s - m)
        denom = jnp.sum(p, axis=-1, keepdims=True) + jnp.exp(sink - m)
        o = jnp.dot(p.astype(BF16), vt, preferred_element_type=F32) / denom
        acc = jnp.where(mine[0], o[:nq], 0.0)
        for g in range(1, ATT_GROUP):
            acc = acc + jnp.where(mine[g], o[g * nq:(g + 1) * nq], 0.0)
        o_ref[:, hk * group_lanes:(hk + 1) * group_lanes] = acc.astype(BF16)


def _ctx_attn_kernel(sink_ref, q_ref, k_ref, v_ref, o_ref):
    _attend(q_ref[...], k_ref[...], v_ref[...], None, sink_ref, o_ref)


def _lat_attn_kernel(sink_ref, q_ref, kp_ref, kc_ref, kn_ref, vp_ref, vc_ref, vn_ref, ck_ref, cv_ref, o_ref):
    jb = pl.program_id(1)
    kt_all = jnp.concatenate([kp_ref[...], kc_ref[...], kn_ref[...], ck_ref[0].astype(BF16)], axis=1)
    v_all = jnp.concatenate([vp_ref[...], vc_ref[...], vn_ref[...], cv_ref[0].astype(BF16)], axis=0)
    nkeys = 3 * BLOCK + PAST_LEN
    qi = lax.broadcasted_iota(jnp.int32, (BLOCK, nkeys), 0)
    kj = lax.broadcasted_iota(jnp.int32, (BLOCK, nkeys), 1)
    qpos = jb * BLOCK + qi
    kpos = (jb - 1) * BLOCK + kj
    local_ok = (jnp.abs(qpos - kpos) <= WINDOW) & (kpos >= 0) & (kpos < DEC_SEQ)
    mask = (kj >= 3 * BLOCK) | local_ok
    _attend(q_ref[...], kt_all, v_all, mask, sink_ref, o_ref)


def _attention(q, kt, v, cache_kt, cache_v, sink):
    nk = ATT_KV_HEADS * ATT_HEAD_DIM
    smem = pl.BlockSpec(memory_space=pltpu.SMEM)
    ctx = pl.pallas_call(
        _ctx_attn_kernel,
        out_shape=jax.ShapeDtypeStruct((N_PROMPT, D_MODEL), BF16),
        grid=(BATCH,),
        in_specs=[
            smem,
            pl.BlockSpec((SEQ, D_MODEL), lambda b: (b, 0)),
            pl.BlockSpec((nk, SEQ), lambda b: (0, b)),
            pl.BlockSpec((SEQ, nk), lambda b: (b, 0)),
        ],
        out_specs=pl.BlockSpec((SEQ, D_MODEL), lambda b: (b, 0)),
        compiler_params=_params(40, "arbitrary"),
        name="context_attention",
    )(sink, q, kt, v)

    nb = DEC_SEQ // BLOCK
    base = N_PROMPT // BLOCK
    cur = lambda b, j: (base + b * nb + j, 0)
    prev = lambda b, j: (base + b * nb + jnp.maximum(j - 1, 0), 0)
    nxt = lambda b, j: (base + b * nb + jnp.minimum(j + 1, nb - 1), 0)
    kv_blk = lambda f: pl.BlockSpec((BLOCK, nk), f)
    kt_blk = lambda f: pl.BlockSpec((nk, BLOCK), lambda b, j: f(b, j)[::-1])
    cache_blk = pl.BlockSpec((1, PAST_LEN, nk), lambda b, j: (b, 0, 0))
    cache_kt_blk = pl.BlockSpec((1, nk, PAST_LEN), lambda b, j: (b, 0, 0))
    lat = pl.pallas_call(
        _lat_attn_kernel,
        out_shape=jax.ShapeDtypeStruct((DEC_BATCH * DEC_SEQ, D_MODEL), BF16),
        grid=(DEC_BATCH, nb),
        in_specs=[
            smem,
            pl.BlockSpec((BLOCK, D_MODEL), cur),
            kt_blk(prev), kt_blk(cur), kt_blk(nxt),
            kv_blk(prev), kv_blk(cur), kv_blk(nxt),
            cache_kt_blk, cache_blk,
        ],
        out_specs=pl.BlockSpec((BLOCK, D_MODEL), lambda b, j: (b * nb + j, 0)),
        compiler_params=_params(40, "arbitrary", "arbitrary"),
        name="latent_attention",
    )(sink, q, kt, kt, kt, v, v, v, cache_kt, cache_v)
    return ctx, lat


def kernel(x_prompt, x_sample, state_hgrn, cache_k, cache_v, c, c_ctx, ada_w, ada_b, norm_w, hg_w_in,
           hg_lb_logits, hg_onorm, hg_w_out, at_w_in, at_sink, at_w_out, moe_w_group, moe_b_group,
           moe_w_expert, moe_b_expert, moe_w1, moe_w3, moe_w2, final_norm):
    xp = x_prompt.reshape(N_PROMPT, D_MODEL)
    xs = x_sample.reshape(N_TOK - N_PROMPT, D_MODEL)
    cond = jnp.concatenate([c_ctx[None, :], c], axis=0)
    mod = _ada(cond, ada_w, ada_b)
    nk = ATT_KV_HEADS * ATT_HEAD_DIM

    def router_params(i):
        pad = jnp.zeros((D_MODEL, ROUTER_LANES - N_EXPERTS - N_GROUPS), F32)
        w = jnp.concatenate([moe_w_expert[i], moe_w_group[i], pad], axis=1)
        b = jnp.concatenate([moe_b_expert[i], moe_b_group[i], pad[0]])[None, :]
        hi = w.astype(BF16)
        lo = (w - hi.astype(F32)).astype(BF16)
        return jnp.stack([hi, lo]), b

    mix_blk = pl.BlockSpec((MIX_TM, D_MODEL), lambda i: (i, 0))
    mix_prompt_blk = pl.BlockSpec((MIX_TM, D_MODEL), lambda i: (jnp.minimum(i, N_PROMPT // MIX_TM - 1), 0))
    mix_sample_blk = pl.BlockSpec((MIX_TM, D_MODEL), lambda i: (jnp.maximum(i - N_PROMPT // MIX_TM, 0), 0))
    q, v, g, lff, kf, lfb, kb = _hg_in(xp, xs, mod[0], norm_w[0, 0][None, :], hg_w_in[0].astype(BF16), hg_lb_logits)
    o_f, o_b, state_new = _gla(q, v, lff, kf, lfb, kb, state_hgrn)
    wr, br = router_params(0)
    routed = _mixer_out(
        _hg_out_kernel, "hgrn_out_route", (o_f, o_b, g, hg_onorm[0][None, :], xp, xs),
        (mix_blk, mix_blk, mix_blk, pl.BlockSpec((1, HG_DK), lambda i: (0, 0)), mix_prompt_blk, mix_sample_blk),
        mod[0], norm_w[0, 1][None, :], hg_w_out[0].astype(BF16), wr, br)
    moe_out = _moe_layer(*routed, 0, moe_w1, moe_w3, moe_w2)

    cos, sin = _rope_tables()
    x, qa, ka, va, k_ctx, v_ctx = _at_in(*moe_out, mod[0], mod[1], norm_w[1, 0][None, :],
                                         at_w_in[0].astype(BF16), cos, sin)
    attn_ctx, attn_lat = _attention(qa, ka, va, cache_k[:, 0].reshape(DEC_BATCH, PAST_LEN, nk).transpose(0, 2, 1),
                                    cache_v[:, 0].reshape(DEC_BATCH, PAST_LEN, nk), at_sink[0])
    wr, br = router_params(1)
    routed = _mixer_out(
        _at_out_kernel, "attn_out_route", (attn_ctx, attn_lat, x), (mix_prompt_blk, mix_sample_blk, mix_blk),
        mod[1], norm_w[1, 1][None, :], at_w_out[0].astype(BF16), wr, br)
    y_prompt, y_sample = _moe_combine(*_moe_layer(*routed, 1, moe_w1, moe_w3, moe_w2), mod[1], final_norm[None, :])

    def cache(feature_major):
        return feature_major.reshape(BATCH, 1, ATT_KV_HEADS, ATT_HEAD_DIM, SEQ).transpose(0, 1, 4, 2, 3)

    return (y_prompt.reshape(BATCH, SEQ, D_MODEL), y_sample.reshape(DEC_BATCH, DEC_SEQ, D_MODEL), state_new,
            cache(k_ctx), cache(v_ctx))
```

```python
import functools
from typing import Any, NamedTuple

import jax
import jax.numpy as jnp
import numpy as np
from jax import lax
from jax.experimental import pallas as pl
from jax.experimental.pallas import tpu as pltpu
from jax.experimental.pallas import tpu_sc as plsc

F32 = jnp.float32
BF16 = jnp.bfloat16

D_MODEL = 1024
BATCH = 16
SEQ = 256
DEC_BATCH = 2
DEC_SEQ = 1024
PAST_LEN = 512
GRID_W = 64
HG_HEADS = 8
HG_DK = 128
CHUNK = 16
ATT_HEAD_DIM = 64
ATT_Q_HEADS = 16
ATT_KV_HEADS = 4
ATT_GROUP = 4
WINDOW = 128
BLOCK = 128
ROPE_HALF = 32
ROPE_BASE = 10000.0
N_GROUPS = 4
EXPERTS_PER_GROUP = 8
N_EXPERTS = 32
D_EXPERT = 256
EPS = 1e-6

N_PROMPT = BATCH * SEQ
N_TOK = N_PROMPT + DEC_BATCH * DEC_SEQ
TM = 256
N_TILES = N_TOK // TM
PROMPT_TILES = N_PROMPT // TM
TILES_PER_DEC_SEQ = DEC_SEQ // TM
LANES = 128
SUBLANES = 8
ADA_LANE_BLOCKS = 4
N_COND_USED = 1 + DEC_BATCH
N_COND = 8
ROUTER_LANES = 128
ROUTE_TM = 1024
ROUTE_ROWS = 8
MIX_TM = 512
AT_IN_TM = 1024
COMBINE_TM = 512
HG_IN_TM = 512
HG_IN_SUB = 256
HG_IN_TN = 256
DECAY_CLAMP = 60.0
PACK_WORDS = D_MODEL // 4
ROW_WORDS = 2 * PACK_WORDS
MOE_TILE = 256
MOE_STEP_TILES = 4
MOE_WEIGHT_SLOTS = 2 * MOE_STEP_TILES
MOE_ROWS = 2 * N_TOK + N_EXPERTS * MOE_TILE
MIB = 1024 * 1024


def _params(vmem_mib, *semantics):
    return pltpu.CompilerParams(dimension_semantics=semantics, vmem_limit_bytes=vmem_mib * MIB)


def _tile_cond(i):
    return jnp.where(i < PROMPT_TILES, 0, 1 + (i - PROMPT_TILES) // TILES_PER_DEC_SEQ)


def _mod_row(mod_ref, cond, which):
    return mod_ref[pl.ds(cond, 1), which * D_MODEL:(which + 1) * D_MODEL]


def _norm_mod(x, nw, shift, scale):
    y = x * lax.rsqrt(jnp.mean(x * x, axis=-1, keepdims=True) + EPS)
    return (y * nw) * (1.0 + scale) + shift


def _silu(x):
    return x * jax.nn.sigmoid(x)


def _ada_kernel(c_ref, w_ref, b_ref, o_ref, s_scr):
    @pl.when((pl.program_id(0) == 0) & (pl.program_id(1) == 0))
    def _():
        s_scr[...] = _silu(c_ref[...])

    tn = w_ref.shape[-1]
    cols = [[] for _ in range(N_COND_USED)]
    for j0 in range(0, tn // LANES, ADA_LANE_BLOCKS):
        acc = [[jnp.zeros((SUBLANES, LANES), F32) for _ in range(ADA_LANE_BLOCKS)] for _ in range(N_COND_USED)]
        for g in range(D_MODEL // SUBLANES):
            rows = slice(g * SUBLANES, (g + 1) * SUBLANES)
            w = [w_ref[0, rows, (j0 + j) * LANES:(j0 + j + 1) * LANES] for j in range(ADA_LANE_BLOCKS)]
            for r in range(N_COND_USED):
                s = s_scr[r, rows, :]
                for j in range(ADA_LANE_BLOCKS):
                    acc[r][j] = acc[r][j] + w[j] * s
        for r in range(N_COND_USED):
            cols[r] += [jnp.sum(a, axis=0, keepdims=True) for a in acc[r]]
    out = [jnp.concatenate(cols[r], axis=1) + b_ref[0] for r in range(N_COND_USED)]
    out.append(jnp.zeros((N_COND - N_COND_USED, tn), F32))
    o_ref[0] = jnp.concatenate(out, axis=0)


def _ada(cond, ada_w, ada_b):
    depth, _, n = ada_w.shape
    tn = 1536
    cond_cols = jnp.broadcast_to(cond[:, :, None], (N_COND_USED, D_MODEL, LANES))
    return pl.pallas_call(
        _ada_kernel,
        out_shape=jax.ShapeDtypeStruct((depth, N_COND, n), F32),
        grid=(depth, n // tn),
        in_specs=[
            pl.BlockSpec((N_COND_USED, D_MODEL, LANES), lambda l, j: (0, 0, 0)),
            pl.BlockSpec((1, D_MODEL, tn), lambda l, j: (l, 0, j)),
            pl.BlockSpec((1, 1, tn), lambda l, j: (l, 0, j)),
        ],
        out_specs=pl.BlockSpec((1, N_COND, tn), lambda l, j: (l, 0, j)),
        scratch_shapes=[pltpu.VMEM((N_COND_USED, D_MODEL, LANES), F32)],
        compiler_params=_params(40, "arbitrary", "arbitrary"),
        name="ada_modulation",
    )(cond_cols, ada_w, ada_b.reshape(depth, 1, n))


def _hg_in_kernel(xp_ref, xs_ref, mod_ref, nw_ref, w_ref, lbl_ref,
                  q_ref, v_ref, g_ref, lff_ref, kf_ref, lfb_ref, kb_ref):
    i = pl.program_id(0) * (HG_IN_TM // TM)
    cond = _tile_cond(i)
    shift, scale = _mod_row(mod_ref, cond, 0), _mod_row(mod_ref, cond, 1)

    def normed(rows):
        x = jnp.where(i < PROMPT_TILES, xp_ref[rows, :], xs_ref[rows, :])
        return _norm_mod(x, nw_ref[...], shift, scale).astype(BF16)

    def proj(h, c, cols):
        return jnp.dot(h, w_ref[:, c * D_MODEL + cols.start:c * D_MODEL + cols.stop], preferred_element_type=F32)

    l0, l1, l2 = lbl_ref[0], lbl_ref[1], lbl_ref[2]
    m = jnp.maximum(jnp.maximum(l0, l1), l2)
    e0, e1, e2 = jnp.exp(l0 - m), jnp.exp(l1 - m), jnp.exp(l2 - m)
    lb = e0 / (e0 + e1 + e2)
    col_tiles = [slice(n * HG_IN_TN, (n + 1) * HG_IN_TN) for n in range(D_MODEL // HG_IN_TN)]

    def gate_and_plain(h, rows, d):
        lf_ref, k_ref, plain_ref = ((lff_ref, kf_ref, q_ref), (lfb_ref, kb_ref, v_ref))[d]
        for cols in col_tiles:
            lbd = lb[d:d + 1, cols]
            z_gate = proj(h, 2 + d, cols)
            z_plain = proj(h, d, cols)
            f = lbd + (1.0 - lbd) * jax.nn.sigmoid(z_gate)
            lf = jnp.log(f)
            hi = lf.astype(BF16)
            lf_ref[0, rows, cols] = hi
            lf_ref[1, rows, cols] = (lf - hi.astype(F32)).astype(BF16)
            k_ref[rows, cols] = (1.0 - f).astype(BF16)
            plain_ref[rows, cols] = z_plain.astype(BF16)

    sub_tiles = [slice(s * HG_IN_SUB, (s + 1) * HG_IN_SUB) for s in range(HG_IN_TM // HG_IN_SUB)]
    h = normed(sub_tiles[0])
    for s, rows in enumerate(sub_tiles):
        gate_and_plain(h, rows, 0)
        h_next = normed(sub_tiles[s + 1]) if s + 1 < len(sub_tiles) else None
        gate_and_plain(h, rows, 1)
        for cols in col_tiles:
            g_ref[rows, cols] = proj(h, 4, cols).astype(BF16)
        h = h_next


def _hg_in(xp, xs, mod, nw, w_in, lb_logits):
    tile = lambda i: (i, 0)
    fixed2 = lambda i: (0, 0)
    bf = jax.ShapeDtypeStruct((N_TOK, D_MODEL), BF16)
    ff = jax.ShapeDtypeStruct((2, N_TOK, D_MODEL), BF16)
    blk = pl.BlockSpec((HG_IN_TM, D_MODEL), tile)
    split_blk = pl.BlockSpec((2, HG_IN_TM, D_MODEL), lambda i: (0, i, 0))
    prompt_steps = N_PROMPT // HG_IN_TM
    return pl.pallas_call(
        _hg_in_kernel,
        out_shape=(bf, bf, bf, ff, bf, ff, bf),
        grid=(N_TOK // HG_IN_TM,),
        in_specs=[
            pl.BlockSpec((HG_IN_TM, D_MODEL), lambda i: (jnp.minimum(i, prompt_steps - 1), 0)),
            pl.BlockSpec((HG_IN_TM, D_MODEL), lambda i: (jnp.maximum(i - prompt_steps, 0), 0)),
            pl.BlockSpec(mod.shape, fixed2),
            pl.BlockSpec((1, D_MODEL), fixed2),
            pl.BlockSpec(w_in.shape, fixed2, pipeline_mode=pl.Buffered(1)),
            pl.BlockSpec(lb_logits.shape, lambda i: (0, 0, 0)),
        ],
        out_specs=(blk, blk, blk, split_blk, blk, split_blk, blk),
        compiler_params=_params(56, "arbitrary"),
        name="hgrn_in_proj",
    )(xp, xs, mod, nw, w_in, lb_logits)


def _gla_exact(reverse, q_ref, v_ref, lf_ref, k_ref, st_ref, o_ref):
    nt = (((1,), (1,)), ((), ()))
    tn = (((0,), (0,)), ((), ()))
    n_chunks = TM // CHUNK
    row_id = lax.broadcasted_iota(jnp.int32, (CHUNK, HG_DK), 0)
    order = range(CHUNK - 1, -1, -1) if reverse else range(CHUNK)

    def chunk_head(it, carry):
        step, h = it // HG_HEADS, it % HG_HEADS
        ci = (n_chunks - 1 - step) if reverse else step
        rows = pl.ds(pl.multiple_of(ci * CHUNK, CHUNK), CHUNK)
        cols = pl.ds(pl.multiple_of(h * HG_DK, HG_DK), HG_DK)
        q, k, v = q_ref[rows, cols], k_ref[rows, cols], v_ref[rows, cols]
        f = jnp.exp(lf_ref[0, rows, cols].astype(F32) + lf_ref[1, rows, cols].astype(F32))
        st = st_ref[0, h]
        o = jnp.zeros((CHUNK, HG_DK), F32)
        for t in order:
            one = row_id == t
            v_t = jnp.where(one, v, jnp.zeros_like(v))
            st = st * f[t:t + 1, :] + lax.dot_general(v_t, k, tn, preferred_element_type=F32)
            o = jnp.where(one, lax.dot_general(q, st.astype(BF16), nt, preferred_element_type=F32), o)
        st_ref[0, h] = st
        o_ref[rows, cols] = o.astype(o_ref.dtype)
        return carry

    lax.fori_loop(0, n_chunks * HG_HEADS, chunk_head, 0)


class _GlaDirection(NamedTuple):
    reverse: bool
    q: Any
    v: Any
    lf: Any
    k: Any
    o: Any
    st: Any
    sw: Any
    ut: Any
    qd: Any
    kd: Any
    ki: Any
    cd: Any
    b: Any


_NT = (((1,), (1,)), ((), ()))
_TN = (((0,), (0,)), ((), ()))
_HG_COLS = [slice(h * HG_DK, (h + 1) * HG_DK) for h in range(HG_HEADS)]
_N_CHUNKS = TM // CHUNK
_N_PAIRS = _N_CHUNKS // 2


def _gla_prepare(d):
    r = lax.broadcasted_iota(jnp.int32, (TM, TM), 0)
    c = lax.broadcasted_iota(jnp.int32, (TM, TM), 1)
    same = (r // CHUNK) == (c // CHUNK)
    tri = (same & ((c >= r) if d.reverse else (c <= r))).astype(BF16)
    b = jnp.dot(tri, d.lf[0], preferred_element_type=F32) + jnp.dot(tri, d.lf[1], preferred_element_type=F32)
    d.b[...] = b
    edge = 0 if d.reverse else CHUNK - 1
    total = b.reshape(_N_CHUNKS, CHUNK, D_MODEL)[:, edge, :]
    d.cd[...] = jnp.exp(total)
    return jnp.max(-total) <= DECAY_CLAMP


def _gla_pair_rows(d, step):
    pi = (_N_PAIRS - 1 - step) if d.reverse else step
    row0 = pi * 2 * CHUNK
    lo, hi = (pl.ds(row0, CHUNK), 2 * pi), (pl.ds(row0 + CHUNK, CHUNK), 2 * pi + 1)
    return pl.ds(row0, 2 * CHUNK), ((hi, lo) if d.reverse else (lo, hi))


def _chunk_decay(d, chunk):
    return d.cd[pl.ds(chunk, 1), :]


def _gla_decayed_operands(d, step):
    both, ((_, c_first), (_, c_second)) = _gla_pair_rows(d, step)
    b = d.b[both, :]
    ki = d.k[both, :] * jnp.exp(jnp.minimum(-b, DECAY_CLAMP)).astype(BF16)
    d.qd[both, :] = d.q[both, :] * jnp.exp(b).astype(BF16)
    d.ki[both, :] = ki
    c_lo, c_hi = (c_second, c_first) if d.reverse else (c_first, c_second)
    ends = jnp.concatenate([jnp.broadcast_to(_chunk_decay(d, c).astype(BF16), (CHUNK, D_MODEL)) for c in (c_lo, c_hi)],
                           axis=0)
    d.kd[both, :] = ki * ends


def _rows_scaled(x, scale_row, second_half):
    scale = jnp.broadcast_to(scale_row.astype(BF16), (CHUNK, x.shape[1]))
    ones = jnp.ones((CHUNK, x.shape[1]), BF16)
    return x * jnp.concatenate([ones, scale] if second_half else [scale, ones], axis=0)


def _gla_key_value_product(d, step):
    _gla_decayed_operands(d, step)
    both, (_, (_, c_second)) = _gla_pair_rows(d, step)
    decay_second = _chunk_decay(d, c_second)
    for h, cols in enumerate(_HG_COLS):
        keys = _rows_scaled(d.kd[both, cols], decay_second[:, cols], second_half=d.reverse)
        d.ut[h] = lax.dot_general(d.v[both, cols], keys, _TN, preferred_element_type=F32)


def _gla_start(d):
    for h in range(HG_HEADS):
        d.sw[0, h] = d.st[0, h].T.astype(BF16)
    _gla_key_value_product(d, 0)


def _gla_pair(d, step, src, dst):
    sr = lax.broadcasted_iota(jnp.int32, (CHUNK, CHUNK), 0)
    sc = lax.broadcasted_iota(jnp.int32, (CHUNK, CHUNK), 1)
    tr = lax.broadcasted_iota(jnp.int32, (CHUNK, 2 * CHUNK), 0)
    tc = lax.broadcasted_iota(jnp.int32, (CHUNK, 2 * CHUNK), 1)
    if d.reverse:
        keep_first = sc >= sr
        keep_second = (tc >= tr) | (tc >= CHUNK)
    else:
        keep_first = sc <= sr
        keep_second = (tc < CHUNK) | (tc - CHUNK <= tr)
    both, ((first, c_first), (second, c_second)) = _gla_pair_rows(d, step)
    decay_first = _chunk_decay(d, c_first)
    decay_both = decay_first * _chunk_decay(d, c_second)
    for h, cols in enumerate(_HG_COLS):
        new = d.st[src, h] * decay_both[:, cols] + d.ut[h]
        d.st[dst, h] = new
        d.sw[dst, h] = new.T.astype(BF16)
    a_first = [lax.dot_general(d.qd[first, cols], d.ki[first, cols], _NT, preferred_element_type=F32)
               for cols in _HG_COLS]
    a_second = []
    for cols in _HG_COLS:
        lo_keys = (d.ki if d.reverse else d.kd)[pl.ds(both.start, CHUNK), cols]
        hi_keys = (d.kd if d.reverse else d.ki)[pl.ds(both.start + CHUNK, CHUNK), cols]
        keys = jnp.concatenate([lo_keys, hi_keys], axis=0)
        a_second.append(lax.dot_general(d.qd[second, cols], keys, _NT, preferred_element_type=F32))
    inter = [jnp.dot(_rows_scaled(d.qd[both, cols], decay_first[:, cols], second_half=not d.reverse),
                     d.sw[src, h], preferred_element_type=F32) for h, cols in enumerate(_HG_COLS)]
    if step + 1 < _N_PAIRS:
        _gla_key_value_product(d, step + 1)
    first_half, second_half = (slice(CHUNK, None), slice(None, CHUNK)) if d.reverse else \
                              (slice(None, CHUNK), slice(CHUNK, None))
    for h, cols in enumerate(_HG_COLS):
        am = jnp.where(keep_first, a_first[h], 0.0).astype(BF16)
        d.o[first, cols] = (jnp.dot(am, d.v[first, cols], preferred_element_type=F32)
                            + inter[h][first_half]).astype(d.o.dtype)
        am = jnp.where(keep_second, a_second[h], 0.0).astype(BF16)
        d.o[second, cols] = (jnp.dot(am, d.v[both, cols], preferred_element_type=F32)
                             + inter[h][second_half]).astype(d.o.dtype)


def _gla_kernel(qf_ref, vf_ref, lff_ref, kf_ref, qb_ref, vb_ref, lfb_ref, kb_ref, s0_ref,
                of_ref, ob_ref, sout_ref, st_scr, sw_scr, ut_scr, qd_scr, kd_scr, ki_scr, cd_scr, b_scr):
    i = pl.program_id(0)
    is_prompt = i < PROMPT_TILES
    first = jnp.logical_or(is_prompt, (i - PROMPT_TILES) % TILES_PER_DEC_SEQ == 0)
    work = [tuple(scr.at[n] for scr in (st_scr, sw_scr, ut_scr, qd_scr, kd_scr, ki_scr, cd_scr, b_scr))
            for n in range(2)]
    fwd = _GlaDirection(False, qf_ref, vf_ref, lff_ref, kf_ref, of_ref, *work[0])
    bwd = _GlaDirection(True, qb_ref, vb_ref, lfb_ref, kb_ref, ob_ref, *work[1])

    @pl.when(is_prompt)
    def _():
        for d in (fwd, bwd):
            d.st[0] = jnp.zeros((HG_HEADS, HG_DK, HG_DK), F32)

    @pl.when(jnp.logical_and(first, jnp.logical_not(is_prompt)))
    def _():
        for n, d in enumerate((fwd, bwd)):
            for h in range(HG_HEADS):
                d.st[0, h] = s0_ref[0, 0, n, h].T

    exact_scores = jnp.logical_and(_gla_prepare(fwd), _gla_prepare(bwd))

    @pl.when(exact_scores)
    def _():
        _gla_start(fwd)
        _gla_start(bwd)

        for step in range(_N_PAIRS):
            for d in (fwd, bwd):
                _gla_pair(d, step, step % 2, 1 - step % 2)

    @pl.when(jnp.logical_not(exact_scores))
    def _():
        for d in (fwd, bwd):
            _gla_exact(d.reverse, d.q, d.v, d.lf, d.k, d.st, d.o)

    @pl.when(is_prompt)
    def _():
        for n, d in enumerate((fwd, bwd)):
            for h in range(HG_HEADS):
                sout_ref[0, 0, n, h] = d.st[0, h].T


def _gla(q, v, lff, kf, lfb, kb, state_hgrn):
    def fwd_tile(i):
        return (i, 0)

    def bwd_tile(i):
        j = (i - PROMPT_TILES) % TILES_PER_DEC_SEQ
        return (jnp.where(i < PROMPT_TILES, i, i - j + (TILES_PER_DEC_SEQ - 1 - j)), 0)

    def s0_idx(i):
        return (jnp.maximum(i - PROMPT_TILES, 0) // TILES_PER_DEC_SEQ, 0, 0, 0, 0, 0)

    def sout_idx(i):
        return (jnp.minimum(i, PROMPT_TILES - 1), 0, 0, 0, 0, 0)

    f_blk = pl.BlockSpec((TM, D_MODEL), fwd_tile)
    b_blk = pl.BlockSpec((TM, D_MODEL), bwd_tile)
    f_split = pl.BlockSpec((2, TM, D_MODEL), lambda i: (0,) + fwd_tile(i))
    b_split = pl.BlockSpec((2, TM, D_MODEL), lambda i: (0,) + bwd_tile(i))
    st_blk = (1, 1, 2, HG_HEADS, HG_DK, HG_DK)
    return pl.pallas_call(
        _gla_kernel,
        out_shape=(
            jax.ShapeDtypeStruct((N_TOK, D_MODEL), BF16),
            jax.ShapeDtypeStruct((N_TOK, D_MODEL), BF16),
            jax.ShapeDtypeStruct((BATCH,) + st_blk[1:], F32),
        ),
        grid=(N_TILES,),
        in_specs=[f_blk, f_blk, f_split, f_blk, b_blk, b_blk, b_split, b_blk, pl.BlockSpec(st_blk, s0_idx)],
        out_specs=(f_blk, b_blk, pl.BlockSpec(st_blk, sout_idx)),
        scratch_shapes=[
            pltpu.VMEM((2, 2, HG_HEADS, HG_DK, HG_DK), F32),
            pltpu.VMEM((2, 2, HG_HEADS, HG_DK, HG_DK), BF16),
            pltpu.VMEM((2, HG_HEADS, HG_DK, HG_DK), F32),
            pltpu.VMEM((2, TM, D_MODEL), BF16),
            pltpu.VMEM((2, TM, D_MODEL), BF16),
            pltpu.VMEM((2, TM, D_MODEL), BF16),
            pltpu.VMEM((2, TM // CHUNK, D_MODEL), F32),
            pltpu.VMEM((2, TM, D_MODEL), F32),
        ],
        compiler_params=_params(48, "arbitrary"),
        name="hgrn_recurrence",
    )(q, v, lff, kf, q, v, lfb, kb, state_hgrn)


def _route(logits):
    lane = lax.broadcasted_iota(jnp.int32, logits.shape, 1)
    neg = jnp.float32(-jnp.inf)
    lane_f = lane.astype(F32)

    def first_max(x):
        m = jnp.max(x, axis=-1, keepdims=True)
        first = jnp.min(jnp.where(x == m, lane_f, float(ROUTER_LANES)), axis=-1, keepdims=True)
        return m, first.astype(jnp.int32)

    is_group = (lane >= N_EXPERTS) & (lane < N_EXPERTS + N_GROUPS)
    gl = jnp.where(is_group, logits, neg)
    gmax, g_lane = first_max(gl)
    g_sel = g_lane - N_EXPERTS
    gsum = jnp.sum(jnp.exp(gl - gmax), axis=-1, keepdims=True)
    p_g = 1.0 / gsum
    in_sel = (lane < N_EXPERTS) & ((lane // EXPERTS_PER_GROUP) == g_sel)
    m1, i1 = first_max(jnp.where(in_sel, logits, neg))
    m2, i2 = first_max(jnp.where(in_sel & (lane != i1), logits, neg))
    e2 = jnp.exp(m2 - m1)
    return i1, i2, p_g / (1.0 + e2), p_g * e2 / (1.0 + e2)


def _pack_rows(x):
    q = PACK_WORDS
    bits = pltpu.bitcast(x.astype(BF16).astype(F32), jnp.uint32)
    return [(bits[:, (2 + h) * q:(3 + h) * q] & jnp.uint32(0xFFFF0000)) | (bits[:, h * q:(h + 1) * q] >> 16)
            for h in range(2)]


def _unpack_rows(half0, half1):
    lo = lambda w: pltpu.bitcast(w << 16, F32).astype(BF16)
    hi = lambda w: pltpu.bitcast(w & jnp.uint32(0xFFFF0000), F32).astype(BF16)
    return [lo(half0), lo(half1), hi(half0), hi(half1)]


def _mix_stream_rows(prompt_ref, sample_ref, rows):
    return jnp.where(pl.program_id(0) < N_PROMPT // MIX_TM, prompt_ref[rows, :], sample_ref[rows, :])


def _mixer_tail(mix_of, x_of, mod_ref, nw2_ref, wo_ref, wr_ref, br_ref, x1_ref, h2p_ref, lg_ref):
    cond = _tile_cond(pl.program_id(0) * (MIX_TM // TM))
    gate, shift, scale = (_mod_row(mod_ref, cond, which) for which in (2, 3, 4))
    project = lambda rows: jnp.dot(mix_of(rows), wo_ref[...], preferred_element_type=F32)

    def tail(rows, out):
        x1 = x_of(rows) + gate * out
        x1_ref[rows, :] = x1
        h2 = _norm_mod(x1, nw2_ref[...], shift, scale)
        h2p_ref[rows, :] = jnp.concatenate(_pack_rows(h2), axis=1)
        h_hi = h2.astype(BF16)
        h_lo = (h2 - h_hi.astype(F32)).astype(BF16)
        lg_ref[rows, :] = (jnp.dot(h_hi, wr_ref[0], preferred_element_type=F32)
                           + (jnp.dot(h_hi, wr_ref[1], preferred_element_type=F32)
                              + jnp.dot(h_lo, wr_ref[0], preferred_element_type=F32))) + br_ref[...]

    sub_tiles = [slice(s * TM, (s + 1) * TM) for s in range(MIX_TM // TM)]
    out = project(sub_tiles[0])
    for s, rows in enumerate(sub_tiles):
        out_next = project(sub_tiles[s + 1]) if s + 1 < len(sub_tiles) else None
        tail(rows, out)
        out = out_next


def _route_kernel(lg_ref, rt_ref, rw_ref, cnt_ref, carry_scr):
    @pl.when(pl.program_id(0) == 0)
    def _():
        carry_scr[...] = jnp.zeros_like(carry_scr)

    logits = lg_ref[...]
    i1, i2, w1, w2 = _route(logits)
    lane = lax.broadcasted_iota(jnp.int32, logits.shape, 1)
    chosen = ((lane == i1) | (lane == i2)).astype(BF16)
    r = lax.broadcasted_iota(jnp.int32, (ROUTE_TM, ROUTE_TM), 0)
    c = lax.broadcasted_iota(jnp.int32, (ROUTE_TM, ROUTE_TM), 1)
    before = jnp.dot((c < r).astype(BF16), chosen, preferred_element_type=F32) + carry_scr[...]
    r1 = jnp.sum(jnp.where(lane == i1, before, 0.0), axis=-1, keepdims=True).astype(jnp.int32)
    r2 = jnp.sum(jnp.where(lane == i2, before, 0.0), axis=-1, keepdims=True).astype(jnp.int32)
    total = carry_scr[...] + jnp.sum(chosen.astype(F32), axis=0, keepdims=True)
    carry_scr[...] = total
    tiles = jnp.ceil(total * (1.0 / MOE_TILE))
    e_from = lax.broadcasted_iota(jnp.int32, (ROUTER_LANES, ROUTER_LANES), 0)
    e_to = lax.broadcasted_iota(jnp.int32, (ROUTER_LANES, ROUTER_LANES), 1)
    tiles_before = jnp.dot(jnp.broadcast_to(tiles, (SUBLANES, ROUTER_LANES)).astype(BF16),
                           (e_from < e_to).astype(BF16), preferred_element_type=F32)
    cnt_ref[...] = jnp.concatenate([total, tiles_before[:1] * MOE_TILE,
                                    jnp.zeros((ROUTE_ROWS - 2, ROUTER_LANES), F32)], axis=0)
    by_token = jnp.where(lane == 0, i1, jnp.where(lane == 1, i2, jnp.where(lane == 2, r1, r2)))
    rt_ref[...] = by_token.T[:ROUTE_ROWS, :]
    rw_ref[...] = jnp.where(lane == 0, w1, w2)


def _route_tokens(logits):
    blk = pl.BlockSpec((ROUTE_TM, ROUTER_LANES), lambda i: (i, 0))
    return pl.pallas_call(
        _route_kernel,
        out_shape=(
            jax.ShapeDtypeStruct((ROUTE_ROWS, N_TOK), jnp.int32),
            jax.ShapeDtypeStruct((N_TOK, ROUTER_LANES), F32),
            jax.ShapeDtypeStruct((ROUTE_ROWS, ROUTER_LANES), F32),
        ),
        grid=(N_TOK // ROUTE_TM,),
        in_specs=[blk],
        out_specs=(pl.BlockSpec((ROUTE_ROWS, ROUTE_TM), lambda i: (0, i)), blk,
                   pl.BlockSpec((ROUTE_ROWS, ROUTER_LANES), lambda i: (0, 0))),
        scratch_shapes=[pltpu.VMEM((1, ROUTER_LANES), F32)],
        compiler_params=_params(32, "arbitrary"),
        name="moe_route",
    )(logits)


def _hg_out_kernel(of_ref, ob_ref, g_ref, on_ref, xp_ref, xs_ref, *rest):
    def mix_of(rows):
        o = of_ref[rows, :].astype(F32) + ob_ref[rows, :].astype(F32)
        parts = []
        for h in range(HG_HEADS):
            oh = o[:, h * HG_DK:(h + 1) * HG_DK]
            parts.append(oh * lax.rsqrt(jnp.mean(oh * oh, axis=-1, keepdims=True) + EPS) * on_ref[...])
        y = jnp.concatenate(parts, axis=1) * _silu(g_ref[rows, :].astype(F32))
        return y.astype(BF16)

    _mixer_tail(mix_of, functools.partial(_mix_stream_rows, xp_ref, xs_ref), *rest)


def _at_out_kernel(ac_ref, al_ref, x_ref, *rest):
    _mixer_tail(functools.partial(_mix_stream_rows, ac_ref, al_ref), lambda rows: x_ref[rows, :], *rest)


def _mixer_out(kernel_fn, name, mix_inputs, mix_specs, mod, nw2, w_out, w_router, b_router):
    tile = lambda i: (i, 0)
    fixed2 = lambda i: (0, 0)
    blk = pl.BlockSpec((MIX_TM, D_MODEL), tile)
    lanes_blk = pl.BlockSpec((MIX_TM, ROUTER_LANES), tile)
    return pl.pallas_call(
        kernel_fn,
        out_shape=(
            jax.ShapeDtypeStruct((N_TOK, D_MODEL), F32),
            jax.ShapeDtypeStruct((N_TOK, ROW_WORDS), jnp.uint32),
            jax.ShapeDtypeStruct((N_TOK, ROUTER_LANES), F32),
        ),
        grid=(N_TOK // MIX_TM,),
        in_specs=list(mix_specs) + [
            pl.BlockSpec(mod.shape, fixed2),
            pl.BlockSpec((1, D_MODEL), fixed2),
            pl.BlockSpec(w_out.shape, fixed2),
            pl.BlockSpec(w_router.shape, lambda i: (0, 0, 0)),
            pl.BlockSpec((1, ROUTER_LANES), fixed2),
        ],
        out_specs=(blk, pl.BlockSpec((MIX_TM, ROW_WORDS), tile), lanes_blk),
        compiler_params=_params(40, "arbitrary"),
        name=name,
    )(*mix_inputs, mod, nw2, w_out, w_router, b_router)


def _moe_plan(route_t, counts):
    cnt = counts[0, :N_EXPERTS].astype(jnp.int32)
    offs = counts[1, :N_EXPERTS].astype(jnp.int32)
    ends = offs + ((cnt + MOE_TILE - 1) // MOE_TILE) * MOE_TILE
    experts = route_t[0:2]
    pos = jnp.sum(jnp.where(experts[None, :, :] == jnp.arange(N_EXPERTS)[:, None, None], offs[:, None, None], 0),
                  axis=0) + route_t[2:4]
    tile_start = jnp.arange(MOE_ROWS // MOE_TILE, dtype=jnp.int32) * MOE_TILE
    tile_expert = jnp.minimum(jnp.sum(ends[None, :] <= tile_start[:, None], axis=1), N_EXPERTS - 1).astype(jnp.int32)
    of_tile = tile_expert[:, None] == jnp.arange(N_EXPERTS)[None, :]
    tile_offs = jnp.sum(jnp.where(of_tile, offs[None, :], 0), axis=1)
    tile_cnt = jnp.sum(jnp.where(of_tile, cnt[None, :], 0), axis=1)
    tile_rows = jnp.clip(tile_offs + tile_cnt - tile_start, 0, MOE_TILE).astype(jnp.int32)
    active = tile_start < ends[-1]
    tile_opens = jnp.where(active & (tile_start == tile_offs), tile_expert, -1).astype(jnp.int32)
    tile_opens = jnp.concatenate([tile_opens, jnp.full((MOE_STEP_TILES,), -1, jnp.int32)])
    n_active = (ends[-1] // MOE_TILE).astype(jnp.int32).reshape(1)
    slot = (jnp.cumsum((cnt > 0).astype(jnp.int32)) - 1) % MOE_WEIGHT_SLOTS
    tile_slot = jnp.sum(jnp.where(of_tile, slot[None, :], 0), axis=1)
    last_slot = jnp.sum(jnp.where(tile_start == ends[-1] - MOE_TILE, tile_slot, 0))
    tile_slot = jnp.where(active, tile_slot, last_slot).astype(jnp.int32)
    return pos.astype(jnp.int32), (tile_rows, tile_slot, tile_opens, n_active)


SC_WINDOW = 64


def _sc_mesh():
    return plsc.VectorSubcoreMesh(core_axis_name="c", subcore_axis_name="s")


def _sc_scatter_rows(x, idx_a, idx_b, n_out_rows):
    n = x.shape[0]

    @functools.partial(pl.kernel, out_type=jax.ShapeDtypeStruct((n_out_rows, ROW_WORDS), x.dtype), mesh=_sc_mesh(),
                       scratch_types=[pltpu.SemaphoreType.DMA, pltpu.SemaphoreType.DMA])
    def scatter(x_hbm, ia_hbm, ib_hbm, o_hbm, sem_a, sem_b):
        def body(x_vmem, ia_vmem, ib_vmem):
            copy_a = pltpu.async_copy(x_vmem, o_hbm.at[ia_vmem.at[0]], sem_a)
            copy_b = pltpu.async_copy(x_vmem, o_hbm.at[ib_vmem.at[0]], sem_b)
            copy_a.wait()
            copy_b.wait()

        idx_spec = pl.BlockSpec((1, SC_WINDOW), index_map=lambda i: (i, 0))
        pltpu.emit_pipeline(
            body, grid=(n // SC_WINDOW,),
            in_specs=[pl.BlockSpec((SC_WINDOW, ROW_WORDS), index_map=lambda i: (i, 0)), idx_spec, idx_spec],
            out_specs=[],
            core_axis_name=("c", "s"), dimension_semantics=(pltpu.PARALLEL,),
        )(x_hbm, ia_hbm, ib_hbm)

    return scatter(x, idx_a, idx_b)


def _sc_gather_rows(table, idx):
    n = idx.size

    @functools.partial(pl.kernel, out_type=jax.ShapeDtypeStruct((n, ROW_WORDS), table.dtype), mesh=_sc_mesh())
    def gather(t_hbm, i_hbm, o_hbm):
        def body(i_vmem, o_vmem):
            pltpu.sync_copy(t_hbm.at[i_vmem.at[0]], o_vmem)

        pltpu.emit_pipeline(
            body, grid=(n // SC_WINDOW,),
            in_specs=[pl.BlockSpec((1, SC_WINDOW), index_map=lambda i: (i, 0))],
            out_specs=[pl.BlockSpec((SC_WINDOW, ROW_WORDS), index_map=lambda i: (i, 0))],
            core_axis_name=("c", "s"), dimension_semantics=(pltpu.PARALLEL,),
        )(i_hbm, o_hbm)

    return gather(table, idx)


def _ffn_kernel(tr_ref, ts_ref, to_ref, na_ref, xs_ref, w1_hbm, w3_hbm, w2_hbm, ys_ref,
                w1_buf, w3_buf, w2_buf, sem, *, layer):
    def weight_copies(t):
        pairs = ((w1_hbm, w1_buf), (w3_hbm, w3_buf), (w2_hbm, w2_buf))
        return [pltpu.make_async_copy(hbm.at[layer, to_ref[t]], buf.at[ts_ref[t]], sem.at[ts_ref[t], j])
                for j, (hbm, buf) in enumerate(pairs)]

    def for_experts_opened_from(first_tile, action):
        for j in range(MOE_STEP_TILES):
            t = first_tile + j

            @pl.when(to_ref[t] >= 0)
            def _():
                for copy in weight_copies(t):
                    action(copy)

    def tile(t, rows):
        slot = ts_ref[t]
        row = lax.broadcasted_iota(jnp.int32, (MOE_TILE, PACK_WORDS), 0)
        live = row < tr_ref[t]
        halves = [jnp.where(live, xs_ref[rows, h * PACK_WORDS:(h + 1) * PACK_WORDS],
                            jnp.zeros((MOE_TILE, PACK_WORDS), jnp.uint32)) for h in range(2)]
        chunks = _unpack_rows(*halves)

        def up(w_buf):
            acc = None
            for k, chunk in enumerate(chunks):
                w = w_buf[slot, k * PACK_WORDS:(k + 1) * PACK_WORDS, :].astype(BF16)
                part = jnp.dot(chunk, w, preferred_element_type=F32)
                acc = part if acc is None else acc + part
            return acc

        hid = (_silu(up(w1_buf)) * up(w3_buf)).astype(BF16)
        y = jnp.dot(hid, w2_buf[slot].astype(BF16), preferred_element_type=F32)
        ys_ref[rows, :] = jnp.concatenate(_pack_rows(y), axis=1)

    step = pl.program_id(0)
    first_tile = step * MOE_STEP_TILES

    @pl.when(first_tile < na_ref[0])
    def _():
        @pl.when(step == 0)
        def _():
            for_experts_opened_from(first_tile, lambda copy: copy.start())

        for_experts_opened_from(first_tile + MOE_STEP_TILES, lambda copy: copy.start())
        for_experts_opened_from(first_tile, lambda copy: copy.wait())
        for j in range(MOE_STEP_TILES):
            tile(first_tile + j, slice(j * MOE_TILE, (j + 1) * MOE_TILE))


def _ffn(xs, tables, layer, w1, w3, w2):
    step_rows = MOE_STEP_TILES * MOE_TILE
    row_tile = lambda s, tr, ts, to, na: (jnp.minimum(s, (na[0] - 1) // MOE_STEP_TILES), 0)
    hbm = pl.BlockSpec(memory_space=pl.ANY)
    return pl.pallas_call(
        functools.partial(_ffn_kernel, layer=layer),
        out_shape=jax.ShapeDtypeStruct((MOE_ROWS, ROW_WORDS), jnp.uint32),
        grid_spec=pltpu.PrefetchScalarGridSpec(
            num_scalar_prefetch=len(tables),
            grid=(MOE_ROWS // step_rows,),
            in_specs=[pl.BlockSpec((step_rows, ROW_WORDS), row_tile), hbm, hbm, hbm],
            out_specs=pl.BlockSpec((step_rows, ROW_WORDS), row_tile),
            scratch_shapes=[
                pltpu.VMEM((MOE_WEIGHT_SLOTS, D_MODEL, D_EXPERT), F32),
                pltpu.VMEM((MOE_WEIGHT_SLOTS, D_MODEL, D_EXPERT), F32),
                pltpu.VMEM((MOE_WEIGHT_SLOTS, D_EXPERT, D_MODEL), F32),
                pltpu.SemaphoreType.DMA((MOE_WEIGHT_SLOTS, 3)),
            ],
        ),
        compiler_params=_params(48, "arbitrary"),
        name="moe_experts",
    )(*tables, xs, w1, w3, w2)


def _moe_res(x_ref, y_ref, rw_ref, mod_ref, cond, rows=slice(None)):
    rw = rw_ref[rows, :]
    wa = rw[:, 0:1]
    wb = rw[:, 1:2]
    ya = _unpack_rows(y_ref[0, rows, :PACK_WORDS], y_ref[0, rows, PACK_WORDS:])
    yb = _unpack_rows(y_ref[1, rows, :PACK_WORDS], y_ref[1, rows, PACK_WORDS:])
    y = jnp.concatenate([wa * a.astype(F32) + wb * b.astype(F32) for a, b in zip(ya, yb)], axis=1)
    return x_ref[rows, :] + _mod_row(mod_ref, cond, 5) * y


def _moe_res_final_kernel(x_ref, y_ref, rw_ref, mod_ref, fn_ref, op_ref, os_ref):
    x = _moe_res(x_ref, y_ref, rw_ref, mod_ref, _tile_cond(pl.program_id(0) * (COMBINE_TM // TM)))
    y = x * lax.rsqrt(jnp.mean(x * x, axis=-1, keepdims=True) + EPS) * fn_ref[...]
    is_prompt = pl.program_id(0) < N_PROMPT // COMBINE_TM

    @pl.when(is_prompt)
    def _():
        op_ref[...] = y

    @pl.when(jnp.logical_not(is_prompt))
    def _():
        os_ref[...] = y


def _moe_combine(x1, y_pairs, route_w, mod, final_norm):
    tile = lambda i: (i, 0)
    fixed2 = lambda i: (0, 0)
    prompt_steps = N_PROMPT // COMBINE_TM
    return pl.pallas_call(
        _moe_res_final_kernel,
        out_shape=(jax.ShapeDtypeStruct((N_PROMPT, D_MODEL), F32),
                   jax.ShapeDtypeStruct((N_TOK - N_PROMPT, D_MODEL), F32)),
        grid=(N_TOK // COMBINE_TM,),
        in_specs=[pl.BlockSpec((COMBINE_TM, D_MODEL), tile),
                  pl.BlockSpec((2, COMBINE_TM, ROW_WORDS), lambda i: (0, i, 0)),
                  pl.BlockSpec((COMBINE_TM, ROUTER_LANES), tile),
                  pl.BlockSpec(mod.shape, fixed2),
                  pl.BlockSpec((1, D_MODEL), fixed2)],
        out_specs=(pl.BlockSpec((COMBINE_TM, D_MODEL), lambda i: (jnp.minimum(i, prompt_steps - 1), 0)),
                   pl.BlockSpec((COMBINE_TM, D_MODEL), lambda i: (jnp.maximum(i - prompt_steps, 0), 0))),
        compiler_params=_params(32, "arbitrary"),
        name="moe_combine",
    )(x1, y_pairs, route_w, mod, final_norm)


def _moe_layer(x1, h2p, logits, layer, w1, w3, w2):
    route_t, route_w, counts = _route_tokens(logits)
    pos, tables = _moe_plan(route_t, counts)
    windows = pos.reshape(2, N_TOK // SC_WINDOW, SC_WINDOW)
    xs = _sc_scatter_rows(h2p, windows[0], windows[1], MOE_ROWS)
    ys = _ffn(xs, tables, layer, w1, w3, w2)
    y_pairs = _sc_gather_rows(ys, windows.reshape(2 * N_TOK // SC_WINDOW, SC_WINDOW))
    return x1, y_pairs.reshape(2, N_TOK, ROW_WORDS), route_w


def _swap_rotary_halves(x):
    n = x.shape[-1]
    lane = lax.broadcasted_iota(jnp.int32, x.shape, 1)
    quarter = ROPE_HALF // 2
    return jnp.where((lane % ROPE_HALF) < quarter, pltpu.roll(x, n - quarter, 1), pltpu.roll(x, quarter, 1))


def _at_in_kernel(x1_ref, y_ref, rw_ref, mod_prev_ref, mod_ref, nw_ref, w_ref, cos_ref, sin_ref,
                  x_ref, q_ref, kt_ref, v_ref, kc_ref, vc_ref):
    step = pl.program_id(0)
    cond = _tile_cond(step * (AT_IN_TM // TM))
    shift, scale = _mod_row(mod_ref, cond, 0), _mod_row(mod_ref, cond, 1)
    nq = ATT_Q_HEADS * ATT_HEAD_DIM
    nk = ATT_KV_HEADS * ATT_HEAD_DIM

    def normed(rows):
        x = _moe_res(x1_ref, y_ref, rw_ref, mod_prev_ref, cond, rows)
        x_ref[rows, :] = x
        return _norm_mod(x, nw_ref[...], shift, scale).astype(BF16)

    def rope(x, rows):
        reps = x.shape[-1] // LANES
        cos = jnp.concatenate([cos_ref[rows, :]] * reps, axis=1)
        sin = jnp.concatenate([sin_ref[rows, :]] * reps, axis=1)
        return x * cos + _swap_rotary_halves(x) * sin

    def project_q(h):
        return jnp.dot(h, w_ref[:, :nq], preferred_element_type=F32)

    def project_kv(h):
        return (jnp.dot(h, w_ref[:, nq:nq + nk], preferred_element_type=F32),
                jnp.dot(h, w_ref[:, nq + nk:], preferred_element_type=F32))

    def tail(rows, zq, zk, v):
        q_ref[rows, :] = (rope(zq, rows) * ATT_HEAD_DIM ** -0.5).astype(BF16)
        kt = rope(zk, rows).T
        kt_ref[:, rows] = kt.astype(BF16)
        v_ref[rows, :] = v.astype(BF16)
        return kt, v.T

    sub_tiles = [slice(s * TM, (s + 1) * TM) for s in range(AT_IN_TM // TM)]
    cache_rows = []
    h = normed(sub_tiles[0])
    zq = project_q(h)
    for s, rows in enumerate(sub_tiles):
        last = s + 1 == len(sub_tiles)
        h_next = None if last else normed(sub_tiles[s + 1])
        zk, v = project_kv(h)
        zq_next = None if last else project_q(h_next)
        cache_rows.append(tail(rows, zq, zk, v))
        h, zq = h_next, zq_next

    @pl.when(step < N_PROMPT // AT_IN_TM)
    def _():
        for rows, (kt, vt) in zip(sub_tiles, cache_rows):
            kc_ref[rows, :] = kt
            vc_ref[rows, :] = vt


def _rope_tables():
    f32 = np.float32
    pos = np.arange(DEC_SEQ)
    t_row = (pos // GRID_W).astype(f32)
    t_col = (pos % GRID_W).astype(f32)
    inv = f32(ROPE_BASE) ** (-np.arange(0, ROPE_HALF, 2, dtype=f32) / f32(ROPE_HALF))
    j = np.arange(LANES) % ATT_HEAD_DIM
    freq = inv[(j % ROPE_HALF) % (ROPE_HALF // 2)]
    ang = (np.where((j < ROPE_HALF)[None, :], t_row[:, None], t_col[:, None]) * freq[None, :]).astype(f32)
    sign = np.where((j % ROPE_HALF) < ROPE_HALF // 2, -1.0, 1.0).astype(f32)
    cos = np.concatenate([np.ones((AT_IN_TM, LANES), f32), np.cos(ang)], axis=0)
    sin = np.concatenate([np.zeros((AT_IN_TM, LANES), f32), np.sin(ang) * sign[None, :]], axis=0)
    return jnp.asarray(cos, F32), jnp.asarray(sin, F32)


def _at_in(x1, y_pairs, route_w, mod_prev, mod, nw, w_in, cos, sin):
    tile = lambda i: (i, 0)
    fixed2 = lambda i: (0, 0)
    prompt_steps = N_PROMPT // AT_IN_TM
    rope_tile = lambda i: (jnp.where(i < prompt_steps, 0, 1 + (i - prompt_steps) % (DEC_SEQ // AT_IN_TM)), 0)
    prompt_tile = lambda i: (jnp.minimum(i, prompt_steps - 1), 0)
    nk = ATT_KV_HEADS * ATT_HEAD_DIM
    return pl.pallas_call(
        _at_in_kernel,
        out_shape=(
            jax.ShapeDtypeStruct((N_TOK, D_MODEL), F32),
            jax.ShapeDtypeStruct((N_TOK, D_MODEL), BF16),
            jax.ShapeDtypeStruct((nk, N_TOK), BF16),
            jax.ShapeDtypeStruct((N_TOK, nk), BF16),
            jax.ShapeDtypeStruct((N_PROMPT, nk), F32),
            jax.ShapeDtypeStruct((N_PROMPT, nk), F32),
        ),
        grid=(N_TOK // AT_IN_TM,),
        in_specs=[
            pl.BlockSpec((AT_IN_TM, D_MODEL), tile),
            pl.BlockSpec((2, AT_IN_TM, ROW_WORDS), lambda i: (0, i, 0)),
            pl.BlockSpec((AT_IN_TM, ROUTER_LANES), tile),
            pl.BlockSpec(mod_prev.shape, fixed2),
            pl.BlockSpec(mod.shape, fixed2),
            pl.BlockSpec((1, D_MODEL), fixed2),
            pl.BlockSpec(w_in.shape, fixed2),
            pl.BlockSpec((AT_IN_TM, LANES), rope_tile),
            pl.BlockSpec((AT_IN_TM, LANES), rope_tile),
        ],
        out_specs=(pl.BlockSpec((AT_IN_TM, D_MODEL), tile), pl.BlockSpec((AT_IN_TM, D_MODEL), tile),
                   pl.BlockSpec((nk, AT_IN_TM), lambda i: (0, i)),
                   pl.BlockSpec((AT_IN_TM, nk), tile),
                   pl.BlockSpec((AT_IN_TM, nk), prompt_tile), pl.BlockSpec((AT_IN_TM, nk), prompt_tile)),
        compiler_params=_params(56, "arbitrary"),
        name="attn_in_proj",
    )(x1, y_pairs, route_w, mod_prev, mod, nw, w_in, cos, sin)


def _attend(q, kt_all, v_all, mask, sink_ref, o_ref):
    nq = q.shape[0]
    group_lanes = ATT_GROUP * ATT_HEAD_DIM
    lane = lax.broadcasted_iota(jnp.int32, (nq, group_lanes), 1)
    mine = [(lane // ATT_HEAD_DIM) == g for g in range(ATT_GROUP)]
    row_head = lax.broadcasted_iota(jnp.int32, (ATT_GROUP * nq, 1), 0) // nq
    if mask is not None:
        mask = jnp.concatenate([mask] * ATT_GROUP, axis=0)

    def scores(hk):
        kt = jnp.concatenate([kt_all[hk * ATT_HEAD_DIM:(hk + 1) * ATT_HEAD_DIM, :]] * ATT_GROUP, axis=0)
        qg = q[:, hk * group_lanes:(hk + 1) * group_lanes]
        q_stack = jnp.concatenate([jnp.where(mine[g], qg, jnp.zeros_like(qg)) for g in range(ATT_GROUP)], axis=0)
        s = jnp.dot(q_stack, kt, preferred_element_type=F32)
        return s if mask is None else jnp.where(mask, s, -jnp.inf)

    s_next = scores(0)
    for hk in range(ATT_KV_HEADS):
        s = s_next
        if hk + 1 < ATT_KV_HEADS:
            s_next = scores(hk + 1)
        vh = v_all[:, hk * ATT_HEAD_DIM:(hk + 1) * ATT_HEAD_DIM]
        vt = jnp.concatenate([vh] * ATT_GROUP, axis=1)
        sink = jnp.zeros((ATT_GROUP * nq, 1), F32)
        for g in range(ATT_GROUP):
            sink = jnp.where(row_head == g, sink_ref[hk * ATT_GROUP + g], sink)
        m = jnp.maximum(jnp.max(s, axis=-1, keepdims=True), sink)
        p = jnp.exp(s - m)
        denom = jnp.sum(p, axis=-1, keepdims=True) + jnp.exp(sink - m)
        o = jnp.dot(p.astype(BF16), vt, preferred_element_type=F32) / denom
        acc = jnp.where(mine[0], o[:nq], 0.0)
        for g in range(1, ATT_GROUP):
            acc = acc + jnp.where(mine[g], o[g * nq:(g + 1) * nq], 0.0)
        o_ref[:, hk * group_lanes:(hk + 1) * group_lanes] = acc.astype(BF16)


def _ctx_attn_kernel(sink_ref, q_ref, k_ref, v_ref, o_ref):
    _attend(q_ref[...], k_ref[...], v_ref[...], None, sink_ref, o_ref)


def _lat_attn_kernel(sink_ref, q_ref, kp_ref, kc_ref, kn_ref, vp_ref, vc_ref, vn_ref, ck_ref, cv_ref, o_ref):
    jb = pl.program_id(1)
    kt_all = jnp.concatenate([kp_ref[...], kc_ref[...], kn_ref[...], ck_ref[0].astype(BF16)], axis=1)
    v_all = jnp.concatenate([vp_ref[...], vc_ref[...], vn_ref[...], cv_ref[0].astype(BF16)], axis=0)
    nkeys = 3 * BLOCK + PAST_LEN
    qi = lax.broadcasted_iota(jnp.int32, (BLOCK, nkeys), 0)
    kj = lax.broadcasted_iota(jnp.int32, (BLOCK, nkeys), 1)
    qpos = jb * BLOCK + qi
    kpos = (jb - 1) * BLOCK + kj
    local_ok = (jnp.abs(qpos - kpos) <= WINDOW) & (kpos >= 0) & (kpos < DEC_SEQ)
    mask = (kj >= 3 * BLOCK) | local_ok
    _attend(q_ref[...], kt_all, v_all, mask, sink_ref, o_ref)


def _attention(q, kt, v, cache_kt, cache_v, sink):
    nk = ATT_KV_HEADS * ATT_HEAD_DIM
    smem = pl.BlockSpec(memory_space=pltpu.SMEM)
    ctx = pl.pallas_call(
        _ctx_attn_kernel,
        out_shape=jax.ShapeDtypeStruct((N_PROMPT, D_MODEL), BF16),
        grid=(BATCH,),
        in_specs=[
            smem,
            pl.BlockSpec((SEQ, D_MODEL), lambda b: (b, 0)),
            pl.BlockSpec((nk, SEQ), lambda b: (0, b)),
            pl.BlockSpec((SEQ, nk), lambda b: (b, 0)),
        ],
        out_specs=pl.BlockSpec((SEQ, D_MODEL), lambda b: (b, 0)),
        compiler_params=_params(40, "arbitrary"),
        name="context_attention",
    )(sink, q, kt, v)

    nb = DEC_SEQ // BLOCK
    base = N_PROMPT // BLOCK
    cur = lambda b, j: (base + b * nb + j, 0)
    prev = lambda b, j: (base + b * nb + jnp.maximum(j - 1, 0), 0)
    nxt = lambda b, j: (base + b * nb + jnp.minimum(j + 1, nb - 1), 0)
    kv_blk = lambda f: pl.BlockSpec((BLOCK, nk), f)
    kt_blk = lambda f: pl.BlockSpec((nk, BLOCK), lambda b, j: f(b, j)[::-1])
    cache_blk = pl.BlockSpec((1, PAST_LEN, nk), lambda b, j: (b, 0, 0))
    cache_kt_blk = pl.BlockSpec((1, nk, PAST_LEN), lambda b, j: (b, 0, 0))
    lat = pl.pallas_call(
        _lat_attn_kernel,
        out_shape=jax.ShapeDtypeStruct((DEC_BATCH * DEC_SEQ, D_MODEL), BF16),
        grid=(DEC_BATCH, nb),
        in_specs=[
            smem,
            pl.BlockSpec((BLOCK, D_MODEL), cur),
            kt_blk(prev), kt_blk(cur), kt_blk(nxt),
            kv_blk(prev), kv_blk(cur), kv_blk(nxt),
            cache_kt_blk, cache_blk,
        ],
        out_specs=pl.BlockSpec((BLOCK, D_MODEL), lambda b, j: (b * nb + j, 0)),
        compiler_params=_params(40, "arbitrary", "arbitrary"),
        name="latent_attention",
    )(sink, q, kt, kt, kt, v, v, v, cache_kt, cache_v)
    return ctx, lat


def kernel(x_prompt, x_sample, state_hgrn, cache_k, cache_v, c, c_ctx, ada_w, ada_b, norm_w, hg_w_in,
           hg_lb_logits, hg_onorm, hg_w_out, at_w_in, at_sink, at_w_out, moe_w_group, moe_b_group,
           moe_w_expert, moe_b_expert, moe_w1, moe_w3, moe_w2, final_norm):
    xp = x_prompt.reshape(N_PROMPT, D_MODEL)
    xs = x_sample.reshape(N_TOK - N_PROMPT, D_MODEL)
    cond = jnp.concatenate([c_ctx[None, :], c], axis=0)
    mod = _ada(cond, ada_w, ada_b)
    nk = ATT_KV_HEADS * ATT_HEAD_DIM

    def router_params(i):
        pad = jnp.zeros((D_MODEL, ROUTER_LANES - N_EXPERTS - N_GROUPS), F32)
        w = jnp.concatenate([moe_w_expert[i], moe_w_group[i], pad], axis=1)
        b = jnp.concatenate([moe_b_expert[i], moe_b_group[i], pad[0]])[None, :]
        hi = w.astype(BF16)
        lo = (w - hi.astype(F32)).astype(BF16)
        return jnp.stack([hi, lo]), b

    mix_blk = pl.BlockSpec((MIX_TM, D_MODEL), lambda i: (i, 0))
    mix_prompt_blk = pl.BlockSpec((MIX_TM, D_MODEL), lambda i: (jnp.minimum(i, N_PROMPT // MIX_TM - 1), 0))
    mix_sample_blk = pl.BlockSpec((MIX_TM, D_MODEL), lambda i: (jnp.maximum(i - N_PROMPT // MIX_TM, 0), 0))
    q, v, g, lff, kf, lfb, kb = _hg_in(xp, xs, mod[0], norm_w[0, 0][None, :], hg_w_in[0].astype(BF16), hg_lb_logits)
    o_f, o_b, state_new = _gla(q, v, lff, kf, lfb, kb, state_hgrn)
    wr, br = router_params(0)
    routed = _mixer_out(
        _hg_out_kernel, "hgrn_out_route", (o_f, o_b, g, hg_onorm[0][None, :], xp, xs),
        (mix_blk, mix_blk, mix_blk, pl.BlockSpec((1, HG_DK), lambda i: (0, 0)), mix_prompt_blk, mix_sample_blk),
        mod[0], norm_w[0, 1][None, :], hg_w_out[0].astype(BF16), wr, br)
    moe_out = _moe_layer(*routed, 0, moe_w1, moe_w3, moe_w2)

    cos, sin = _rope_tables()
    x, qa, ka, va, k_ctx, v_ctx = _at_in(*moe_out, mod[0], mod[1], norm_w[1, 0][None, :],
                                         at_w_in[0].astype(BF16), cos, sin)
    attn_ctx, attn_lat = _attention(qa, ka, va, cache_k[:, 0].reshape(DEC_BATCH, PAST_LEN, nk).transpose(0, 2, 1),
                                    cache_v[:, 0].reshape(DEC_BATCH, PAST_LEN, nk), at_sink[0])
    wr, br = router_params(1)
    routed = _mixer_out(
        _at_out_kernel, "attn_out_route", (attn_ctx, attn_lat, x), (mix_prompt_blk, mix_sample_blk, mix_blk),
        mod[1], norm_w[1, 1][None, :], at_w_out[0].astype(BF16), wr, br)
    y_prompt, y_sample = _moe_combine(*_moe_layer(*routed, 1, moe_w1, moe_w3, moe_w2), mod[1], final_norm[None, :])

    def cache(feature_major):
        return feature_major.reshape(BATCH, 1, ATT_KV_HEADS, ATT_HEAD_DIM, SEQ).transpose(0, 1, 4, 2, 3)

    return (y_prompt.reshape(BATCH, SEQ, D_MODEL), y_sample.reshape(DEC_BATCH, DEC_SEQ, D_MODEL), state_new,
            cache(k_ctx), cache(v_ctx))
```
